```python
import jax, jax.numpy as jnp
from jax import lax
import numpy as np

D_MODEL = 1024
BATCH = 8
SEQ = 4096
DEPTH = 1

CHUNK = 64
FOX_HEAD_DIM = 64
FOX_WIDTH = D_MODEL // 2
FOX_HEADS = FOX_WIDTH // FOX_HEAD_DIM
Q_BLOCK = 128
SGU_WIDTH = D_MODEL // 2
SGU_GROUP_DIM = 64
SGU_GROUPS = SGU_WIDTH // SGU_GROUP_DIM
SGU_WINDOW = 128
N_BRANCHES = 2
D_FF = -(-8 * D_MODEL // (3 * 256)) * 256
EPS = 1e-6
FORGET_BIAS = 2.0

Q_OFF = 0
K_OFF = Q_OFF + FOX_WIDTH
V_OFF = K_OFF + FOX_WIDTH
F_OFF = V_OFF + FOX_WIDTH
U_OFF = F_OFF + FOX_HEADS
G_OFF = U_OFF + 2 * SGU_WIDTH
IN_COLS = G_OFF + N_BRANCHES * D_MODEL

kernel_name = "fox_gmlp_gated_hybrid_block"


def rmsnorm(x, g):
    xf = x.astype(jnp.float32)
    y = xf * lax.rsqrt(jnp.mean(xf * xf, axis=-1, keepdims=True) + EPS)
    return (y * g.astype(jnp.float32)).astype(x.dtype)


def layernorm(x, g, b):
    xf = x.astype(jnp.float32)
    mu = jnp.mean(xf, axis=-1, keepdims=True)
    xc = xf - mu
    y = xc * lax.rsqrt(jnp.mean(xc * xc, axis=-1, keepdims=True) + EPS)
    return (y * g.astype(jnp.float32) + b.astype(jnp.float32)).astype(x.dtype)


def forgetting_attention(q, k, v, log_f):
    s_len = q.shape[2]
    d_cum = jnp.cumsum(log_f, axis=-1)
    scale = FOX_HEAD_DIM ** -0.5
    outs = []
    for i in range(s_len // Q_BLOCK):
        q0, q1 = i * Q_BLOCK, (i + 1) * Q_BLOCK
        qb = q[:, :, q0:q1]
        kb = k[:, :, :q1]
        vb = v[:, :, :q1]
        logits = jnp.einsum('bhqd,bhkd->bhqk', qb, kb).astype(jnp.float32) * scale
        logits = logits + d_cum[:, :, q0:q1, None] - d_cum[:, :, None, :q1]
        q_pos = jnp.arange(q0, q1)[:, None]
        k_pos = jnp.arange(q1)[None, :]
        logits = jnp.where(k_pos <= q_pos, logits, -jnp.inf)
        p = jax.nn.softmax(logits, axis=-1)
        outs.append(jnp.einsum('bhqk,bhkd->bhqd', p.astype(vb.dtype), vb))
    return jnp.concatenate(outs, axis=2)


def spatial_gating(uv, g_norm, b_norm, w_spatial, b_spatial):
    bsz, s_len, _ = uv.shape
    u, v = uv[..., :SGU_WIDTH], uv[..., SGU_WIDTH:]
    v = layernorm(v, g_norm, b_norm)
    v = v.reshape(bsz, s_len // SGU_WINDOW, SGU_WINDOW, SGU_GROUPS, SGU_GROUP_DIM)
    t_idx = jnp.arange(SGU_WINDOW)[:, None]
    s_idx = jnp.arange(SGU_WINDOW)[None, :]
    mask = (s_idx // CHUNK) <= (t_idx // CHUNK)
    ws = jnp.where(mask[None], w_spatial, jnp.zeros((), w_spatial.dtype))
    mixed = jnp.einsum('gts,bnsgc->bntgc', ws, v)
    mixed = mixed + jnp.transpose(b_spatial)[None, None, :, :, None]
    return u * mixed.reshape(bsz, s_len, SGU_WIDTH)


def _fwd_setup_inputs(seed: int = 0) -> dict:
    key = jax.random.key(seed)
    ks = jax.random.split(key, 20)
    f32 = jnp.float32

    def nrm(k, shape, scale):
        return jax.random.normal(k, shape, f32) * scale

    def gain(k, shape):
        return 1.0 + 0.05 * jax.random.normal(k, shape, f32)

    L = DEPTH
    return {
        "x": jax.random.normal(ks[0], (BATCH, SEQ, D_MODEL), f32),
        "g_pre_mix": gain(ks[1], (L, D_MODEL)),
        "w_in": nrm(ks[2], (L, D_MODEL, IN_COLS), D_MODEL ** -0.5),
        "b_forget": FORGET_BIAS + 0.1 * jax.random.normal(ks[3], (L, FOX_HEADS), f32),
        "g_q": gain(ks[4], (L, FOX_HEAD_DIM)),
        "g_k": gain(ks[5], (L, FOX_HEAD_DIM)),
        "g_sgu": gain(ks[6], (L, SGU_WIDTH)),
        "b_sgu": nrm(ks[7], (L, SGU_WIDTH), 0.02),
        "w_spatial": nrm(ks[8], (L, SGU_GROUPS, SGU_WINDOW, SGU_WINDOW), SGU_WINDOW ** -0.5),
        "b_spatial": 1.0 + 0.05 * jax.random.normal(ks[9], (L, SGU_GROUPS, SGU_WINDOW), f32),
        "w_branch_a": nrm(ks[10], (L, FOX_WIDTH, D_MODEL), FOX_WIDTH ** -0.5),
        "w_branch_b": nrm(ks[11], (L, SGU_WIDTH, D_MODEL), SGU_WIDTH ** -0.5),
        "w_out": nrm(ks[12], (L, D_MODEL, D_MODEL), D_MODEL ** -0.5),
        "g_post_mix": gain(ks[13], (L, D_MODEL)),
        "g_pre_ffn": gain(ks[14], (L, D_MODEL)),
        "w_ffn_in": nrm(ks[15], (L, D_MODEL, 2 * D_FF), D_MODEL ** -0.5),
        "w_ffn_down": nrm(ks[16], (L, D_FF, D_MODEL), D_FF ** -0.5),
        "g_post_ffn": gain(ks[17], (L, D_MODEL)),
    }


def _fwd_reference(x, g_pre_mix, w_in, b_forget, g_q, g_k, g_sgu, b_sgu, w_spatial, b_spatial,
              w_branch_a, w_branch_b, w_out, g_post_mix, g_pre_ffn, w_ffn_in, w_ffn_down,
              g_post_ffn):
    bsz, s_len, _ = x.shape
    for layer in range(DEPTH):
        h = rmsnorm(x, g_pre_mix[layer])
        proj = h @ w_in[layer]

        def heads(t):
            return t.reshape(bsz, s_len, FOX_HEADS, FOX_HEAD_DIM).transpose(0, 2, 1, 3)

        q = rmsnorm(heads(proj[..., Q_OFF:K_OFF]), g_q[layer])
        k = rmsnorm(heads(proj[..., K_OFF:V_OFF]), g_k[layer])
        v = heads(proj[..., V_OFF:F_OFF])
        f_logit = proj[..., F_OFF:U_OFF].astype(jnp.float32) + b_forget[layer].astype(jnp.float32)
        log_f = jnp.transpose(jax.nn.log_sigmoid(f_logit), (0, 2, 1))
        attn = forgetting_attention(q, k, v, log_f)
        attn = attn.transpose(0, 2, 1, 3).reshape(bsz, s_len, FOX_WIDTH)
        y_a = attn @ w_branch_a[layer]

        uv = jax.nn.gelu(proj[..., U_OFF:G_OFF])
        sgu = spatial_gating(uv, g_sgu[layer], b_sgu[layer], w_spatial[layer], b_spatial[layer])
        y_b = sgu @ w_branch_b[layer]

        gates = jax.nn.sigmoid(proj[..., G_OFF:])
        merged = gates[..., :D_MODEL] * y_a + gates[..., D_MODEL:] * y_b
        x = x + rmsnorm(merged @ w_out[layer], g_post_mix[layer])

        h2 = rmsnorm(x, g_pre_ffn[layer])
        gu = h2 @ w_ffn_in[layer]
        ff = (jax.nn.silu(gu[..., :D_FF]) * gu[..., D_FF:]) @ w_ffn_down[layer]
        x = x + rmsnorm(ff, g_post_ffn[layer])
    return x


import jax as _jax
import jax.numpy as _jnp

TWIN_FORMAT = 'train_step'
FWD_PARAMS = ['x', 'g_pre_mix', 'w_in', 'b_forget', 'g_q', 'g_k', 'g_sgu', 'b_sgu', 'w_spatial', 'b_spatial', 'w_branch_a', 'w_branch_b', 'w_out', 'g_post_mix', 'g_pre_ffn', 'w_ffn_in', 'w_ffn_down', 'g_post_ffn']
TWIN_WEIGHTS = ['g_pre_mix', 'w_in', 'b_forget', 'g_q', 'g_k', 'g_sgu', 'b_sgu', 'w_spatial', 'b_spatial', 'w_branch_a', 'w_branch_b', 'w_out', 'g_post_mix', 'g_pre_ffn', 'w_ffn_in', 'w_ffn_down', 'g_post_ffn']
TWIN_DIFF_INPUT = 'x'
TWIN_INPUTS = ['x', 'g_pre_mix', 'w_in', 'b_forget', 'g_q', 'g_k', 'g_sgu', 'b_sgu', 'w_spatial', 'b_spatial', 'w_branch_a', 'w_branch_b', 'w_out', 'g_post_mix', 'g_pre_ffn', 'w_ffn_in', 'w_ffn_down', 'g_post_ffn', 'loss_target', 'm_g_pre_mix', 'm_w_in', 'm_b_forget', 'm_g_q', 'm_g_k', 'm_g_sgu', 'm_b_sgu', 'm_w_spatial', 'm_b_spatial', 'm_w_branch_a', 'm_w_branch_b', 'm_w_out', 'm_g_post_mix', 'm_g_pre_ffn', 'm_w_ffn_in', 'm_w_ffn_down', 'm_g_post_ffn', 'v_g_pre_mix', 'v_w_in', 'v_b_forget', 'v_g_q', 'v_g_k', 'v_g_sgu', 'v_b_sgu', 'v_w_spatial', 'v_b_spatial', 'v_w_branch_a', 'v_w_branch_b', 'v_w_out', 'v_g_post_mix', 'v_g_pre_ffn', 'v_w_ffn_in', 'v_w_ffn_down', 'v_g_post_ffn']
TWIN_OUTPUTS = ['loss', 'grad_x', 'grad_g_pre_mix', 'grad_w_in', 'grad_b_forget', 'grad_g_q', 'grad_g_k', 'grad_g_sgu', 'grad_b_sgu', 'grad_w_spatial', 'grad_b_spatial', 'grad_w_branch_a', 'grad_w_branch_b', 'grad_w_out', 'grad_g_post_mix', 'grad_g_pre_ffn', 'grad_w_ffn_in', 'grad_w_ffn_down', 'grad_g_post_ffn', 'delta_g_pre_mix', 'delta_w_in', 'delta_b_forget', 'delta_g_q', 'delta_g_k', 'delta_g_sgu', 'delta_b_sgu', 'delta_w_spatial', 'delta_b_spatial', 'delta_w_branch_a', 'delta_w_branch_b', 'delta_w_out', 'delta_g_post_mix', 'delta_g_pre_ffn', 'delta_w_ffn_in', 'delta_w_ffn_down', 'delta_g_post_ffn', 'new_m_g_pre_mix', 'new_m_w_in', 'new_m_b_forget', 'new_m_g_q', 'new_m_g_k', 'new_m_g_sgu', 'new_m_b_sgu', 'new_m_w_spatial', 'new_m_b_spatial', 'new_m_w_branch_a', 'new_m_w_branch_b', 'new_m_w_out', 'new_m_g_post_mix', 'new_m_g_pre_ffn', 'new_m_w_ffn_in', 'new_m_w_ffn_down', 'new_m_g_post_ffn', 'new_v_g_pre_mix', 'new_v_w_in', 'new_v_b_forget', 'new_v_g_q', 'new_v_g_k', 'new_v_g_sgu', 'new_v_b_sgu', 'new_v_w_spatial', 'new_v_b_spatial', 'new_v_w_branch_a', 'new_v_w_branch_b', 'new_v_w_out', 'new_v_g_post_mix', 'new_v_g_pre_ffn', 'new_v_w_ffn_in', 'new_v_w_ffn_down', 'new_v_g_post_ffn']
TWIN_LEAF_KINDS = {'loss': 'loss', 'grad_x': 'grad_x', 'grad_g_pre_mix': 'grad_w', 'grad_w_in': 'grad_w', 'grad_b_forget': 'grad_w', 'grad_g_q': 'grad_w', 'grad_g_k': 'grad_w', 'grad_g_sgu': 'grad_w', 'grad_b_sgu': 'grad_w', 'grad_w_spatial': 'grad_w', 'grad_b_spatial': 'grad_w', 'grad_w_branch_a': 'grad_w', 'grad_w_branch_b': 'grad_w', 'grad_w_out': 'grad_w', 'grad_g_post_mix': 'grad_w', 'grad_g_pre_ffn': 'grad_w', 'grad_w_ffn_in': 'grad_w', 'grad_w_ffn_down': 'grad_w', 'grad_g_post_ffn': 'grad_w', 'delta_g_pre_mix': 'delta_w', 'delta_w_in': 'delta_w', 'delta_b_forget': 'delta_w', 'delta_g_q': 'delta_w', 'delta_g_k': 'delta_w', 'delta_g_sgu': 'delta_w', 'delta_b_sgu': 'delta_w', 'delta_w_spatial': 'delta_w', 'delta_b_spatial': 'delta_w', 'delta_w_branch_a': 'delta_w', 'delta_w_branch_b': 'delta_w', 'delta_w_out': 'delta_w', 'delta_g_post_mix': 'delta_w', 'delta_g_pre_ffn': 'delta_w', 'delta_w_ffn_in': 'delta_w', 'delta_w_ffn_down': 'delta_w', 'delta_g_post_ffn': 'delta_w', 'new_m_g_pre_mix': 'new_m', 'new_m_w_in': 'new_m', 'new_m_b_forget': 'new_m', 'new_m_g_q': 'new_m', 'new_m_g_k': 'new_m', 'new_m_g_sgu': 'new_m', 'new_m_b_sgu': 'new_m', 'new_m_w_spatial': 'new_m', 'new_m_b_spatial': 'new_m', 'new_m_w_branch_a': 'new_m', 'new_m_w_branch_b': 'new_m', 'new_m_w_out': 'new_m', 'new_m_g_post_mix': 'new_m', 'new_m_g_pre_ffn': 'new_m', 'new_m_w_ffn_in': 'new_m', 'new_m_w_ffn_down': 'new_m', 'new_m_g_post_ffn': 'new_m', 'new_v_g_pre_mix': 'new_v', 'new_v_w_in': 'new_v', 'new_v_b_forget': 'new_v', 'new_v_g_q': 'new_v', 'new_v_g_k': 'new_v', 'new_v_g_sgu': 'new_v', 'new_v_b_sgu': 'new_v', 'new_v_w_spatial': 'new_v', 'new_v_b_spatial': 'new_v', 'new_v_w_branch_a': 'new_v', 'new_v_w_branch_b': 'new_v', 'new_v_w_out': 'new_v', 'new_v_g_post_mix': 'new_v', 'new_v_g_pre_ffn': 'new_v', 'new_v_w_ffn_in': 'new_v', 'new_v_w_ffn_down': 'new_v', 'new_v_g_post_ffn': 'new_v'}


def _forward(args):
    return _fwd_reference(*[args[k] for k in FWD_PARAMS])


def _output_shape():
    def fwd():
        inp = _fwd_setup_inputs(0)
        return _fwd_reference(*[inp[k] for k in FWD_PARAMS])
    out = _jax.eval_shape(fwd)
    return out.shape, out.dtype

N_MICROBATCH = 1
ADAM_LR = 0.001
ADAM_B1 = 0.9
ADAM_B2 = 0.999
ADAM_EPS = 1e-08
ADAM_WD = 0.01
ADAM_STEP = 10
PER_EXAMPLE_BATCH_AXIS = {'x': 0, 'loss_target': 0}
SHARED_INPUTS = []
_WEIGHT_DTYPES = {'g_pre_mix': _jnp.float32, 'w_in': _jnp.float32, 'b_forget': _jnp.float32, 'g_q': _jnp.float32, 'g_k': _jnp.float32, 'g_sgu': _jnp.float32, 'b_sgu': _jnp.float32, 'w_spatial': _jnp.float32, 'b_spatial': _jnp.float32, 'w_branch_a': _jnp.float32, 'w_branch_b': _jnp.float32, 'w_out': _jnp.float32, 'g_post_mix': _jnp.float32, 'g_pre_ffn': _jnp.float32, 'w_ffn_in': _jnp.float32, 'w_ffn_down': _jnp.float32, 'g_post_ffn': _jnp.float32}
MOMENT_SCALE = {'g_pre_mix': 7.451833e-01, 'w_in': 3.445449e-01, 'b_forget': 1.709273e+00, 'g_q': 6.333811e-01, 'g_k': 6.224734e-01, 'g_sgu': 4.453005e-01, 'b_sgu': 3.790352e-01, 'w_spatial': 2.602385e-01, 'b_spatial': 3.077034e-01, 'w_branch_a': 2.360602e-01, 'w_branch_b': 1.137923e+00, 'w_out': 1.197918e+00, 'g_post_mix': 3.230606e+01, 'g_pre_ffn': 1.106287e+00, 'w_ffn_in': 4.562538e-01, 'w_ffn_down': 9.083824e-01, 'g_post_ffn': 3.203675e+01}


def _to_microbatches(a, axis):
    t = _jnp.moveaxis(a, axis, 0)
    t = t.reshape((N_MICROBATCH, t.shape[0] // N_MICROBATCH) + t.shape[1:])
    return _jnp.moveaxis(t, 1, axis + 1)


def setup_inputs(seed: int = 0) -> dict:
    inp = _fwd_setup_inputs(seed)
    key = _jax.random.fold_in(_jax.random.key(seed), 7919)
    shape, _ = _output_shape()
    out = dict(inp)
    out["loss_target"] = _jax.random.normal(_jax.random.fold_in(key, 0), shape, _jnp.float32)
    for i, name in enumerate(TWIN_WEIGHTS):
        w = inp[name].astype(_jnp.float32)
        if MOMENT_SCALE is None:
            s = _jnp.sqrt(_jnp.mean(_jnp.square(w)) + 1e-30)
        else:
            s = MOMENT_SCALE[name]
        km, kv = _jax.random.split(_jax.random.fold_in(key, i + 1))
        out[name] = w
        out["m_" + name] = s * _jax.random.normal(km, w.shape, _jnp.float32)
        out["v_" + name] = (s * s) * _jax.random.uniform(kv, w.shape, _jnp.float32, 0.5, 1.5)
    if N_MICROBATCH > 1:
        for name, axis in PER_EXAMPLE_BATCH_AXIS.items():
            out[name] = _to_microbatches(out[name], axis)
    return {'x': out['x'], 'g_pre_mix': out['g_pre_mix'], 'w_in': out['w_in'], 'b_forget': out['b_forget'], 'g_q': out['g_q'], 'g_k': out['g_k'], 'g_sgu': out['g_sgu'], 'b_sgu': out['b_sgu'], 'w_spatial': out['w_spatial'], 'b_spatial': out['b_spatial'], 'w_branch_a': out['w_branch_a'], 'w_branch_b': out['w_branch_b'], 'w_out': out['w_out'], 'g_post_mix': out['g_post_mix'], 'g_pre_ffn': out['g_pre_ffn'], 'w_ffn_in': out['w_ffn_in'], 'w_ffn_down': out['w_ffn_down'], 'g_post_ffn': out['g_post_ffn'], 'loss_target': out['loss_target'], 'm_g_pre_mix': out['m_g_pre_mix'], 'm_w_in': out['m_w_in'], 'm_b_forget': out['m_b_forget'], 'm_g_q': out['m_g_q'], 'm_g_k': out['m_g_k'], 'm_g_sgu': out['m_g_sgu'], 'm_b_sgu': out['m_b_sgu'], 'm_w_spatial': out['m_w_spatial'], 'm_b_spatial': out['m_b_spatial'], 'm_w_branch_a': out['m_w_branch_a'], 'm_w_branch_b': out['m_w_branch_b'], 'm_w_out': out['m_w_out'], 'm_g_post_mix': out['m_g_post_mix'], 'm_g_pre_ffn': out['m_g_pre_ffn'], 'm_w_ffn_in': out['m_w_ffn_in'], 'm_w_ffn_down': out['m_w_ffn_down'], 'm_g_post_ffn': out['m_g_post_ffn'], 'v_g_pre_mix': out['v_g_pre_mix'], 'v_w_in': out['v_w_in'], 'v_b_forget': out['v_b_forget'], 'v_g_q': out['v_g_q'], 'v_g_k': out['v_g_k'], 'v_g_sgu': out['v_g_sgu'], 'v_b_sgu': out['v_b_sgu'], 'v_w_spatial': out['v_w_spatial'], 'v_b_spatial': out['v_b_spatial'], 'v_w_branch_a': out['v_w_branch_a'], 'v_w_branch_b': out['v_w_branch_b'], 'v_w_out': out['v_w_out'], 'v_g_post_mix': out['v_g_post_mix'], 'v_g_pre_ffn': out['v_g_pre_ffn'], 'v_w_ffn_in': out['v_w_ffn_in'], 'v_w_ffn_down': out['v_w_ffn_down'], 'v_g_post_ffn': out['v_g_post_ffn']}


def _loss(weights, diff, rest, loss_target):
    with _jax.named_scope("forward"):
        args = {**rest, TWIN_DIFF_INPUT: diff, **{k: w.astype(_WEIGHT_DTYPES[k]) for k, w in weights.items()}}
        y = _forward(args)
    with _jax.named_scope("loss_head"):
        err = _jnp.square(y.astype(_jnp.float32) - loss_target)
        return 0.5 * _jnp.sum(_jnp.mean(err, axis=-1)) if err.ndim else 0.5 * err


def _adamw(w, g, m, v):
    m = ADAM_B1 * m + (1.0 - ADAM_B1) * g
    v = ADAM_B2 * v + (1.0 - ADAM_B2) * _jnp.square(g)
    m_hat = m / (1.0 - ADAM_B1 ** ADAM_STEP)
    v_hat = v / (1.0 - ADAM_B2 ** ADAM_STEP)
    delta = -ADAM_LR * (m_hat / (_jnp.sqrt(v_hat) + ADAM_EPS) + ADAM_WD * w)
    return delta, m, v


def reference(x, g_pre_mix, w_in, b_forget, g_q, g_k, g_sgu, b_sgu, w_spatial, b_spatial, w_branch_a, w_branch_b, w_out, g_post_mix, g_pre_ffn, w_ffn_in, w_ffn_down, g_post_ffn, loss_target, m_g_pre_mix, m_w_in, m_b_forget, m_g_q, m_g_k, m_g_sgu, m_b_sgu, m_w_spatial, m_b_spatial, m_w_branch_a, m_w_branch_b, m_w_out, m_g_post_mix, m_g_pre_ffn, m_w_ffn_in, m_w_ffn_down, m_g_post_ffn, v_g_pre_mix, v_w_in, v_b_forget, v_g_q, v_g_k, v_g_sgu, v_b_sgu, v_w_spatial, v_b_spatial, v_w_branch_a, v_w_branch_b, v_w_out, v_g_post_mix, v_g_pre_ffn, v_w_ffn_in, v_w_ffn_down, v_g_post_ffn):
    given = dict(x=x, g_pre_mix=g_pre_mix, w_in=w_in, b_forget=b_forget, g_q=g_q, g_k=g_k, g_sgu=g_sgu, b_sgu=b_sgu, w_spatial=w_spatial, b_spatial=b_spatial, w_branch_a=w_branch_a, w_branch_b=w_branch_b, w_out=w_out, g_post_mix=g_post_mix, g_pre_ffn=g_pre_ffn, w_ffn_in=w_ffn_in, w_ffn_down=w_ffn_down, g_post_ffn=g_post_ffn, loss_target=loss_target, m_g_pre_mix=m_g_pre_mix, m_w_in=m_w_in, m_b_forget=m_b_forget, m_g_q=m_g_q, m_g_k=m_g_k, m_g_sgu=m_g_sgu, m_b_sgu=m_b_sgu, m_w_spatial=m_w_spatial, m_b_spatial=m_b_spatial, m_w_branch_a=m_w_branch_a, m_w_branch_b=m_w_branch_b, m_w_out=m_w_out, m_g_post_mix=m_g_post_mix, m_g_pre_ffn=m_g_pre_ffn, m_w_ffn_in=m_w_ffn_in, m_w_ffn_down=m_w_ffn_down, m_g_post_ffn=m_g_post_ffn, v_g_pre_mix=v_g_pre_mix, v_w_in=v_w_in, v_b_forget=v_b_forget, v_g_q=v_g_q, v_g_k=v_g_k, v_g_sgu=v_g_sgu, v_b_sgu=v_b_sgu, v_w_spatial=v_w_spatial, v_b_spatial=v_b_spatial, v_w_branch_a=v_w_branch_a, v_w_branch_b=v_w_branch_b, v_w_out=v_w_out, v_g_post_mix=v_g_post_mix, v_g_pre_ffn=v_g_pre_ffn, v_w_ffn_in=v_w_ffn_in, v_w_ffn_down=v_w_ffn_down, v_g_post_ffn=v_g_post_ffn)
    weights = {n: given[n] for n in TWIN_WEIGHTS}
    shared = {n: given[n] for n in SHARED_INPUTS}
    per_example = {n: given[n] for n in ['x']}
    grad_fn = _jax.value_and_grad(_loss, argnums=(0, 1))

    def one_microbatch(ex, loss_target):
        ex = dict(ex)
        diff = ex.pop(TWIN_DIFF_INPUT)
        return grad_fn(weights, diff, {**shared, **ex}, loss_target)

    if N_MICROBATCH == 1:
        loss, (grad_w, grad_x) = one_microbatch(per_example, given["loss_target"])
    else:
        def body(carry, xs):
            loss_sum, grad_sum = carry
            l_k, (gw_k, gx_k) = one_microbatch(xs[0], xs[1])
            with _jax.named_scope("update"):
                return (loss_sum + l_k, _jax.tree.map(_jnp.add, grad_sum, gw_k)), gx_k

        init = (_jnp.zeros((), _jnp.float32), _jax.tree.map(_jnp.zeros_like, weights))
        (loss, grad_w), grad_x = _jax.lax.scan(body, init, (per_example, given["loss_target"]))
    with _jax.named_scope("update"):
        delta_w, new_m, new_v = {}, {}, {}
        for n in TWIN_WEIGHTS:
            delta_w[n], new_m[n], new_v[n] = _adamw(weights[n], grad_w[n], given["m_" + n], given["v_" + n])
    return (loss, grad_x, *[grad_w[n] for n in TWIN_WEIGHTS], *[delta_w[n] for n in TWIN_WEIGHTS],
            *[new_m[n] for n in TWIN_WEIGHTS], *[new_v[n] for n in TWIN_WEIGHTS])
```

```python
import functools
import math

import jax
import jax.numpy as jnp
from jax import lax
from jax.experimental import pallas as pl
from jax.experimental.pallas import tpu as pltpu

F32 = jnp.float32
BF16 = jnp.bfloat16

D_MODEL = 1024
FOX_W = 512
HEADS = 8
HEAD_DIM = 64
SGU_W = 512
GROUPS = 8
WINDOW = 128
CHUNK = 64
D_FF = 2816
IN_COLS = 4616
EPS = 1e-6
N_DEV = 8

C_Q, C_K, C_V, C_F, C_UV, C_G, C_END = 0, 512, 1024, 1536, 1664, 2688, 4736

ADAM_LR, ADAM_B1, ADAM_B2, ADAM_EPS, ADAM_WD, ADAM_STEP = 0.001, 0.9, 0.999, 1e-08, 0.01, 10

MIB = 1024 * 1024
TOKEN_TILE = 256
ATTN_TILE = 256

SMALL_ROWS = 24


def _params(vmem_mib, n_axes):
    return pltpu.CompilerParams(
        dimension_semantics=("arbitrary",) * n_axes, vmem_limit_bytes=vmem_mib * MIB)


def _const_spec(shape):
    nd = len(shape)
    return pl.BlockSpec(shape, lambda *_: (0,) * nd)


def _row_spec(tm, cols):
    return pl.BlockSpec((tm, cols), lambda i: (i, 0))


def _split3_dot(x, e):
    x1 = x.astype(BF16)
    r1 = x - x1.astype(F32)
    x2 = r1.astype(BF16)
    x3 = (r1 - x2.astype(F32)).astype(BF16)
    dot = functools.partial(jnp.dot, preferred_element_type=F32)
    return dot(x1, e) + dot(x2, e) + dot(x3, e)


def _tri_dot(tri, x):
    x1 = x.astype(BF16)
    r1 = x - x1.astype(F32)
    x2 = r1.astype(BF16)
    x3 = (r1 - x2.astype(F32)).astype(BF16)
    dot = functools.partial(jnp.dot, preferred_element_type=F32)
    return dot(tri, x1) + dot(tri, x2) + dot(tri, x3)


def _seg_mean(sq, bd_ref):
    hi = sq.astype(BF16)
    lo = (sq - hi.astype(F32)).astype(BF16)
    bd = bd_ref[...]
    s = jnp.dot(hi, bd, preferred_element_type=F32) + jnp.dot(lo, bd, preferred_element_type=F32)
    return s * (1.0 / HEAD_DIM)


def _dot_nt(a, b):
    return lax.dot_general(a, b, (((1,), (1,)), ((), ())), preferred_element_type=F32)


def _dot_tn(a, b):
    return lax.dot_general(a, b, (((0,), (0,)), ((), ())), preferred_element_type=F32)


def _sigmoid(x):
    return 1.0 / (1.0 + jnp.exp(-x))


_GELU_C = math.sqrt(2.0 / math.pi)


def _gelu_and_grad(x):
    inner = _GELU_C * (x + 0.044715 * x * x * x)
    t = jnp.tanh(inner)
    y = 0.5 * x * (1.0 + t)
    dy = 0.5 * (1.0 + t) + 0.5 * x * (1.0 - t * t) * _GELU_C * (1.0 + 3.0 * 0.044715 * x * x)
    return y, dy


def _rms_bwd(xin, r, g, dy):
    dyg = dy * g
    return r * dyg - xin * (r * r * r) * jnp.mean(dyg * xin, axis=-1, keepdims=True)


def _mesh_pos():
    x, y, c = lax.axis_index("x"), lax.axis_index("y"), lax.axis_index("c")
    return x, y, c


def _peer(k):
    x, y, c = _mesh_pos()
    px = (1 - x) if (k >> 2) & 1 else x
    py = (1 - y) if (k >> 1) & 1 else y
    pc = (1 - c) if k & 1 else c
    return (px, py, pc), 4 * px + 2 * py + pc


def _exchange(arrs, name, gather):
    n = len(arrs)
    if gather:
        out_shape = [jax.ShapeDtypeStruct((N_DEV,) + a.shape, a.dtype) for a in arrs]
    else:
        out_shape = [jax.ShapeDtypeStruct(a.shape, a.dtype) for a in arrs]

    def body(*refs):
        ins, outs = refs[:n], refs[n:2 * n]
        send_sems, recv_sems, local_sems = refs[2 * n:]
        x, y, c = _mesh_pos()
        me = 4 * x + 2 * y + c

        def src(a, idx):
            return ins[a] if gather else ins[a].at[idx]

        local = []
        for a in range(n):
            cp = pltpu.make_async_copy(src(a, me), outs[a].at[me], local_sems.at[a])
            cp.start()
            local.append(cp)
        sends = []
        for k in range(1, N_DEV):
            peer, pidx = _peer(k)
            for a in range(n):
                cp = pltpu.make_async_remote_copy(
                    src_ref=src(a, pidx), dst_ref=outs[a].at[me],
                    send_sem=send_sems.at[a, k - 1], recv_sem=recv_sems.at[a, k - 1],
                    device_id=peer, device_id_type=pl.DeviceIdType.MESH)
                cp.start()
                sends.append(cp)
        for k in range(1, N_DEV):
            peer, pidx = _peer(k)
            for a in range(n):
                pltpu.make_async_remote_copy(
                    src_ref=src(a, pidx), dst_ref=outs[a].at[pidx],
                    send_sem=send_sems.at[a, k - 1], recv_sem=recv_sems.at[a, k - 1],
                    device_id=peer, device_id_type=pl.DeviceIdType.MESH).wait_recv()
        for cp in sends:
            cp.wait_send()
        for cp in local:
            cp.wait()

    any_spec = pl.BlockSpec(memory_space=pl.ANY)
    return pl.pallas_call(
        body, name=name, out_shape=out_shape,
        in_specs=[any_spec] * n, out_specs=[any_spec] * n,
        scratch_shapes=[pltpu.SemaphoreType.DMA((n, N_DEV - 1)),
                        pltpu.SemaphoreType.DMA((n, N_DEV - 1)),
                        pltpu.SemaphoreType.DMA((n,))],
    )(*arrs)


def _proj_fwd(x, g1, wcat, bdiag, gq, gk, bfor, tri):
    s_len = x.shape[0]
    tm = TOKEN_TILE

    def body(x_ref, g1_ref, w_ref, bd_ref, gq_ref, gk_ref, bf_ref, tri_ref,
             h_ref, qn_ref, kn_ref, v_ref, qr_ref, kr_ref, flog_ref, dcol_ref, uv_ref, gp_ref,
             carry):
        @pl.when(pl.program_id(0) == 0)
        def _():
            carry[...] = jnp.zeros_like(carry)

        xf = x_ref[...]
        r = lax.rsqrt(jnp.mean(xf * xf, axis=-1, keepdims=True) + EPS)
        h = (xf * r * g1_ref[...]).astype(BF16)
        h_ref[...] = h

        def proj(lo, hi):
            return jnp.dot(h, w_ref[:, lo:hi], preferred_element_type=F32)

        q = proj(C_Q, C_K)
        qr_ref[...] = q.astype(BF16)
        rq = lax.rsqrt(_seg_mean(q * q, bd_ref) + EPS)
        qn_ref[...] = (q * rq * (gq_ref[...] * HEAD_DIM ** -0.5)).astype(BF16)
        k = proj(C_K, C_V)
        kr_ref[...] = k.astype(BF16)
        rk = lax.rsqrt(_seg_mean(k * k, bd_ref) + EPS)
        kn_ref[...] = (k * rk * gk_ref[...]).astype(BF16)
        v_ref[...] = proj(C_V, C_F).astype(BF16)
        uv_ref[...] = proj(C_UV, C_G).astype(BF16)
        gp_ref[...] = proj(C_G, C_END).astype(BF16)

        flog = proj(C_F, C_UV) + bf_ref[...]
        flog_ref[...] = flog
        lane = lax.broadcasted_iota(jnp.int32, flog.shape, 1)
        logf = jnp.minimum(flog, 0.0) - jnp.log(1.0 + jnp.exp(-jnp.abs(flog)))
        logf = jnp.where(lane < HEADS, logf, 0.0)
        dcum = _tri_dot(tri_ref[...], logf) + carry[...]
        dcol_ref[...] = dcum
        carry[...] = dcum[tm - 1:tm, :]

    bf_shapes = [(D_MODEL, BF16), (FOX_W, BF16), (FOX_W, BF16), (FOX_W, BF16), (FOX_W, BF16),
                 (FOX_W, BF16), (128, F32), (128, F32), (2 * SGU_W, BF16), (2 * D_MODEL, BF16)]
    return pl.pallas_call(
        body, name="proj_fwd", grid=(s_len // tm,),
        in_specs=[_row_spec(tm, D_MODEL), _const_spec((1, D_MODEL)), _const_spec(wcat.shape),
                  _const_spec(bdiag.shape), _const_spec((1, FOX_W)), _const_spec((1, FOX_W)),
                  _const_spec((1, 128)), _const_spec((tm, tm))],
        out_specs=[_row_spec(tm, c) for c, _ in bf_shapes],
        out_shape=[jax.ShapeDtypeStruct((s_len, c), dt) for c, dt in bf_shapes],
        scratch_shapes=[pltpu.VMEM((1, 128), F32)],
        compiler_params=_params(56, 1),
    )(x, g1, wcat, bdiag, gq, gk, bfor, tri)


def _attn_fwd(qn, kn, v, dcol, drow):
    s_len = qn.shape[0]
    t = ATTN_TILE
    nb = s_len // t

    def body(q_ref, k_ref, v_ref, dc_ref, dr_ref, o_ref, lse_ref, m_sc, l_sc, acc_sc):
        i, j = pl.program_id(0), pl.program_id(1)

        @pl.when(j == 0)
        def _():
            m_sc[...] = jnp.full_like(m_sc, -jnp.inf)
            l_sc[...] = jnp.zeros_like(l_sc)
            acc_sc[...] = jnp.zeros_like(acc_sc)
            lse_ref[...] = jnp.zeros_like(lse_ref)

        @pl.when(j <= i)
        def _():
            rows = lax.broadcasted_iota(jnp.int32, (t, t), 0) + i * t
            cols = lax.broadcasted_iota(jnp.int32, (t, t), 1) + j * t
            keep = cols <= rows
            for hd in range(HEADS):
                sl = slice(hd * HEAD_DIM, (hd + 1) * HEAD_DIM)
                s = _dot_nt(q_ref[:, sl], k_ref[:, sl])
                s = s + (dc_ref[:, hd:hd + 1] - dr_ref[hd:hd + 1, :])
                s = jnp.where(keep, s, -jnp.inf)
                m_prev = m_sc[hd]
                m_new = jnp.maximum(m_prev, jnp.max(s, axis=-1, keepdims=True))
                alpha = jnp.exp(m_prev - m_new)
                p = jnp.exp(s - m_new)
                l_sc[hd] = alpha * l_sc[hd] + jnp.sum(p, axis=-1, keepdims=True)
                acc_sc[:, sl] = alpha * acc_sc[:, sl] + jnp.dot(
                    p.astype(BF16), v_ref[:, sl], preferred_element_type=F32)
                m_sc[hd] = m_new

        @pl.when(j == i)
        def _():
            for hd in range(HEADS):
                sl = slice(hd * HEAD_DIM, (hd + 1) * HEAD_DIM)
                l = l_sc[hd]
                o_ref[:, sl] = (acc_sc[:, sl] / l).astype(BF16)
                lse_ref[:, hd:hd + 1] = m_sc[hd] + jnp.log(l)

    qspec = pl.BlockSpec((t, FOX_W), lambda i, j: (i, 0))
    kspec = pl.BlockSpec((t, FOX_W), lambda i, j: (jnp.minimum(i, j), 0))
    return pl.pallas_call(
        body, name="attn_fwd", grid=(nb, nb),
        in_specs=[qspec, kspec, kspec, pl.BlockSpec((t, 128), lambda i, j: (i, 0)),
                  pl.BlockSpec((HEADS, t), lambda i, j: (0, jnp.minimum(i, j)))],
        out_specs=[qspec, pl.BlockSpec((t, 128), lambda i, j: (i, 0))],
        out_shape=[jax.ShapeDtypeStruct((s_len, FOX_W), BF16),
                   jax.ShapeDtypeStruct((s_len, 128), F32)],
        scratch_shapes=[pltpu.VMEM((HEADS, t, 1), F32), pltpu.VMEM((HEADS, t, 1), F32),
                        pltpu.VMEM((t, FOX_W), F32)],
        compiler_params=_params(32, 2),
    )(qn, kn, v, dcol, drow)


def _sgu_mix(vn, ws_ref):
    tm = vn.shape[0]
    lane = lax.broadcasted_iota(jnp.int32, (WINDOW, 128), 1)
    low = lane < HEAD_DIM
    wins = []
    for w in range(tm // WINDOW):
        slabs = []
        for p in range(GROUPS // 2):
            v2 = vn[w * WINDOW:(w + 1) * WINDOW, p * 128:(p + 1) * 128]
            lo = jnp.where(low, v2, 0.0).astype(BF16)
            hi = jnp.where(low, 0.0, v2).astype(BF16)
            slabs.append(jnp.dot(ws_ref[2 * p], lo, preferred_element_type=F32)
                         + jnp.dot(ws_ref[2 * p + 1], hi, preferred_element_type=F32))
        wins.append(jnp.concatenate(slabs, axis=1))
    return jnp.concatenate(wins, axis=0) if len(wins) > 1 else wins[0]


def _layernorm_fwd(vv, g, b):
    mu = jnp.mean(vv, axis=-1, keepdims=True)
    xc = vv - mu
    r = lax.rsqrt(jnp.mean(xc * xc, axis=-1, keepdims=True) + EPS)
    xh = xc * r
    return xh * g + b, xh, r


def _mix_fwd(attn, uvpre, gpre, x, wa, wb, wout, wsm, bsf, gsgu, bsgu, gpost):
    s_len = x.shape[0]
    tm = TOKEN_TILE

    def body(o_ref, uv_ref, gp_ref, x_ref, wa_ref, wb_ref, wo_ref, ws_ref, bs_ref, gs_ref, bsg_ref,
             gpost_ref, sgu_ref, ya_ref, yb_ref, mg_ref, om_ref, x1_ref):
        uvp = uv_ref[...].astype(F32)
        uv, _ = _gelu_and_grad(uvp)
        u, vv = uv[:, :SGU_W], uv[:, SGU_W:]
        vn, _, _ = _layernorm_fwd(vv, gs_ref[...], bsg_ref[...])
        bias = bs_ref[...]
        if tm > WINDOW:
            bias = jnp.concatenate([bias] * (tm // WINDOW), axis=0)
        mixed = _sgu_mix(vn, ws_ref) + bias
        sgu = (u * mixed).astype(BF16)
        sgu_ref[...] = sgu
        ya = jnp.dot(o_ref[...], wa_ref[...], preferred_element_type=F32)
        yb = jnp.dot(sgu, wb_ref[...], preferred_element_type=F32)
        ya_ref[...] = ya.astype(BF16)
        yb_ref[...] = yb.astype(BF16)
        gates = _sigmoid(gp_ref[...].astype(F32))
        merged = (gates[:, :D_MODEL] * ya + gates[:, D_MODEL:] * yb).astype(BF16)
        mg_ref[...] = merged
        om = jnp.dot(merged, wo_ref[...], preferred_element_type=F32)
        om_ref[...] = om
        r = lax.rsqrt(jnp.mean(om * om, axis=-1, keepdims=True) + EPS)
        x1_ref[...] = x_ref[...] + om * r * gpost_ref[...]

    outs = [(SGU_W, BF16), (D_MODEL, BF16), (D_MODEL, BF16), (D_MODEL, BF16), (D_MODEL, F32),
            (D_MODEL, F32)]
    return pl.pallas_call(
        body, name="mix_fwd", grid=(s_len // tm,),
        in_specs=[_row_spec(tm, FOX_W), _row_spec(tm, 2 * SGU_W), _row_spec(tm, 2 * D_MODEL),
                  _row_spec(tm, D_MODEL), _const_spec(wa.shape), _const_spec(wb.shape),
                  _const_spec(wout.shape), _const_spec(wsm.shape), _const_spec(bsf.shape),
                  _const_spec((1, SGU_W)), _const_spec((1, SGU_W)), _const_spec((1, D_MODEL))],
        out_specs=[_row_spec(tm, c) for c, _ in outs],
        out_shape=[jax.ShapeDtypeStruct((s_len, c), dt) for c, dt in outs],
        compiler_params=_params(48, 1),
    )(attn, uvpre, gpre, x, wa, wb, wout, wsm, bsf, gsgu, bsgu, gpost)


def _ffn_fwd_bwd(x1, tgt, wffn, wdown, gpre, gpost):
    s_len = x1.shape[0]
    tm = TOKEN_TILE

    def body(x1_ref, t_ref, wi_ref, wd_ref, gpre_ref, gpost_ref,
             dx1_ref, h2_ref, act_ref, dff_ref, dgu_ref, loss_ref, dgpost_ref, dgpre_ref):
        @pl.when(pl.program_id(0) == 0)
        def _():
            loss_ref[...] = jnp.zeros_like(loss_ref)
            dgpost_ref[...] = jnp.zeros_like(dgpost_ref)
            dgpre_ref[...] = jnp.zeros_like(dgpre_ref)

        x1v = x1_ref[...]
        r2 = lax.rsqrt(jnp.mean(x1v * x1v, axis=-1, keepdims=True) + EPS)
        gpre_v = gpre_ref[...]
        h2 = (x1v * r2 * gpre_v).astype(BF16)
        h2_ref[...] = h2
        gg = jnp.dot(h2, wi_ref[:, :D_FF], preferred_element_type=F32)
        uu = jnp.dot(h2, wi_ref[:, D_FF:], preferred_element_type=F32)
        sg = _sigmoid(gg)
        silu = gg * sg
        act = (silu * uu).astype(BF16)
        act_ref[...] = act
        ff = jnp.dot(act, wd_ref[...], preferred_element_type=F32)
        r3 = lax.rsqrt(jnp.mean(ff * ff, axis=-1, keepdims=True) + EPS)
        gpost_v = gpost_ref[...]
        y = x1v + ff * r3 * gpost_v
        err = y - t_ref[...]
        loss_ref[...] += jnp.sum(err * err) * (0.5 / D_MODEL)
        dy = err * (1.0 / D_MODEL)
        dgpost_ref[...] += jnp.sum(dy * ff * r3, axis=0, keepdims=True)
        dff = _rms_bwd(ff, r3, gpost_v, dy).astype(BF16)
        dff_ref[...] = dff
        dact = _dot_nt(dff, wd_ref[...])
        dgg = (dact * uu * (sg * (1.0 + gg * (1.0 - sg)))).astype(BF16)
        duu = (dact * silu).astype(BF16)
        dgu_ref[:, :D_FF] = dgg
        dgu_ref[:, D_FF:] = duu
        dh2 = _dot_nt(dgg, wi_ref[:, :D_FF]) + _dot_nt(duu, wi_ref[:, D_FF:])
        dgpre_ref[...] += jnp.sum(dh2 * x1v * r2, axis=0, keepdims=True)
        dx1_ref[...] = dy + _rms_bwd(x1v, r2, gpre_v, dh2)

    outs = [((s_len, D_MODEL), F32, _row_spec(tm, D_MODEL)),
            ((s_len, D_MODEL), BF16, _row_spec(tm, D_MODEL)),
            ((s_len, D_FF), BF16, _row_spec(tm, D_FF)),
            ((s_len, D_MODEL), BF16, _row_spec(tm, D_MODEL)),
            ((s_len, 2 * D_FF), BF16, _row_spec(tm, 2 * D_FF)),
            ((1, 128), F32, _const_spec((1, 128))),
            ((1, D_MODEL), F32, _const_spec((1, D_MODEL))),
            ((1, D_MODEL), F32, _const_spec((1, D_MODEL)))]
    return pl.pallas_call(
        body, name="ffn_fwd_bwd", grid=(s_len // tm,),
        in_specs=[_row_spec(tm, D_MODEL), _row_spec(tm, D_MODEL), _const_spec(wffn.shape),
                  _const_spec(wdown.shape), _const_spec((1, D_MODEL)), _const_spec((1, D_MODEL))],
        out_specs=[o[2] for o in outs],
        out_shape=[jax.ShapeDtypeStruct(o[0], o[1]) for o in outs],
        compiler_params=_params(60, 1),
    )(x1, tgt, wffn, wdown, gpre, gpost)


def _mix_bwd(dx1, om, ya, yb, gpre, uvpre, wout, wa, wb, wsm, wsmt, bsf, gsgu, bsgu, gpost,
             wmask, egrp):
    s_len = dx1.shape[0]
    tm = TOKEN_TILE
    nw = tm // WINDOW

    def body(dx1_ref, om_ref, ya_ref, yb_ref, gp_ref, uv_ref, wo_ref, wa_ref, wb_ref, ws_ref,
             wst_ref, bs_ref, gs_ref, bsg_ref, gpost_ref, mask_ref, eg_ref,
             dom_ref, dya_ref, dyb_ref, dgp_ref, dattn_ref, duv_ref,
             dws_ref, dbs_ref, dgs_ref, dbsg_ref, dgpost_ref, dbs_acc):
        step = pl.program_id(0)

        @pl.when(step == 0)
        def _():
            dws_ref[...] = jnp.zeros_like(dws_ref)
            dbs_acc[...] = jnp.zeros_like(dbs_acc)
            dgs_ref[...] = jnp.zeros_like(dgs_ref)
            dbsg_ref[...] = jnp.zeros_like(dbsg_ref)
            dgpost_ref[...] = jnp.zeros_like(dgpost_ref)

        om = om_ref[...]
        dx1v = dx1_ref[...]
        r = lax.rsqrt(jnp.mean(om * om, axis=-1, keepdims=True) + EPS)
        gpost_v = gpost_ref[...]
        dgpost_ref[...] += jnp.sum(dx1v * om * r, axis=0, keepdims=True)
        dom = _rms_bwd(om, r, gpost_v, dx1v).astype(BF16)
        dom_ref[...] = dom
        dmg = _dot_nt(dom, wo_ref[...])

        gates = _sigmoid(gp_ref[...].astype(F32))
        ga, gb = gates[:, :D_MODEL], gates[:, D_MODEL:]
        yav, ybv = ya_ref[...].astype(F32), yb_ref[...].astype(F32)
        dya = (dmg * ga).astype(BF16)
        dyb = (dmg * gb).astype(BF16)
        dya_ref[...] = dya
        dyb_ref[...] = dyb
        dgp_ref[:, :D_MODEL] = (dmg * yav * ga * (1.0 - ga)).astype(BF16)
        dgp_ref[:, D_MODEL:] = (dmg * ybv * gb * (1.0 - gb)).astype(BF16)
        dattn_ref[...] = _dot_nt(dya, wa_ref[...]).astype(BF16)
        dsgu = _dot_nt(dyb, wb_ref[...])

        uvp = uv_ref[...].astype(F32)
        uv, guv = _gelu_and_grad(uvp)
        u, vv = uv[:, :SGU_W], uv[:, SGU_W:]
        gs_v = gs_ref[...]
        vn, xh, rln = _layernorm_fwd(vv, gs_v, bsg_ref[...])
        bias = bs_ref[...]
        if nw > 1:
            bias = jnp.concatenate([bias] * nw, axis=0)
        mixed = _sgu_mix(vn, ws_ref) + bias
        du = dsgu * mixed
        dmixed = dsgu * u

        lane = lax.broadcasted_iota(jnp.int32, (WINDOW, 128), 1)
        low = lane < HEAD_DIM
        dvn_wins = []
        for w in range(nw):
            rows = slice(w * WINDOW, (w + 1) * WINDOW)
            dbs_acc[...] += dmixed[rows, :]
            slabs = []
            for p in range(GROUPS // 2):
                cols = slice(p * 128, (p + 1) * 128)
                dm2 = dmixed[rows, cols]
                dlo = jnp.where(low, dm2, 0.0).astype(BF16)
                dhi = jnp.where(low, 0.0, dm2).astype(BF16)
                vn2 = vn[rows, cols].astype(BF16)
                dws_ref[2 * p] += _dot_nt(dlo, vn2)
                dws_ref[2 * p + 1] += _dot_nt(dhi, vn2)
                slabs.append(jnp.dot(wst_ref[2 * p], dlo, preferred_element_type=F32)
                             + jnp.dot(wst_ref[2 * p + 1], dhi, preferred_element_type=F32))
            dvn_wins.append(jnp.concatenate(slabs, axis=1))
        dvn = jnp.concatenate(dvn_wins, axis=0) if nw > 1 else dvn_wins[0]

        dgs_ref[...] += jnp.sum(dvn * xh, axis=0, keepdims=True)
        dbsg_ref[...] += jnp.sum(dvn, axis=0, keepdims=True)
        dxh = dvn * gs_v
        dvv = rln * (dxh - jnp.mean(dxh, axis=-1, keepdims=True)
                     - xh * jnp.mean(dxh * xh, axis=-1, keepdims=True))
        duv_ref[:, :SGU_W] = (du * guv[:, :SGU_W]).astype(BF16)
        duv_ref[:, SGU_W:] = (dvv * guv[:, SGU_W:]).astype(BF16)

        @pl.when(step == pl.num_programs(0) - 1)
        def _():
            for g in range(GROUPS):
                dws_ref[g] = dws_ref[g] * mask_ref[...]
            dbs_ref[...] = _split3_dot(dbs_acc[...], eg_ref[...])

    rows_out = [(D_MODEL, BF16), (D_MODEL, BF16), (D_MODEL, BF16), (2 * D_MODEL, BF16),
                (FOX_W, BF16), (2 * SGU_W, BF16)]
    acc_out = [((GROUPS, WINDOW, WINDOW), F32), ((WINDOW, 128), F32), ((1, SGU_W), F32),
               ((1, SGU_W), F32), ((1, D_MODEL), F32)]
    return pl.pallas_call(
        body, name="mix_bwd", grid=(s_len // tm,),
        in_specs=[_row_spec(tm, D_MODEL), _row_spec(tm, D_MODEL), _row_spec(tm, D_MODEL),
                  _row_spec(tm, D_MODEL), _row_spec(tm, 2 * D_MODEL), _row_spec(tm, 2 * SGU_W),
                  _const_spec(wout.shape), _const_spec(wa.shape), _const_spec(wb.shape),
                  _const_spec(wsm.shape), _const_spec(wsmt.shape), _const_spec(bsf.shape),
                  _const_spec((1, SGU_W)), _const_spec((1, SGU_W)), _const_spec((1, D_MODEL)),
                  _const_spec(wmask.shape), _const_spec(egrp.shape)],
        out_specs=[_row_spec(tm, c) for c, _ in rows_out] + [_const_spec(s) for s, _ in acc_out],
        out_shape=[jax.ShapeDtypeStruct((s_len, c), dt) for c, dt in rows_out]
        + [jax.ShapeDtypeStruct(s, dt) for s, dt in acc_out],
        scratch_shapes=[pltpu.VMEM((WINDOW, SGU_W), F32)],
        compiler_params=_params(48, 1),
    )(dx1, om, ya, yb, gpre, uvpre, wout, wa, wb, wsm, wsmt, bsf, gsgu, bsgu, gpost, wmask, egrp)


def _attn_bwd(qn, kn, v, o, do, lse, dcol, drow):
    s_len = qn.shape[0]
    t = ATTN_TILE
    nb = s_len // t

    def body(q_ref, k_ref, v_ref, o_ref, do_ref, lse_ref, dc_ref, dr_ref,
             dq_ref, dk_ref, dv_ref, dd_ref, ddq_ref, dk_acc, dv_acc):
        j, i = pl.program_id(0), pl.program_id(1)

        @pl.when((j == 0) & (i == 0))
        def _():
            dq_ref[...] = jnp.zeros_like(dq_ref)
            ddq_ref[...] = jnp.zeros_like(ddq_ref)

        @pl.when(i == 0)
        def _():
            dk_acc[...] = jnp.zeros_like(dk_acc)
            dv_acc[...] = jnp.zeros_like(dv_acc)
            dd_ref[...] = jnp.zeros_like(dd_ref)

        @pl.when(i >= j)
        def _():
            rows = lax.broadcasted_iota(jnp.int32, (t, t), 0) + i * t
            cols = lax.broadcasted_iota(jnp.int32, (t, t), 1) + j * t
            keep = cols <= rows
            qrows = pl.ds(pl.multiple_of(i * t, t), t)
            for hd in range(HEADS):
                sl = slice(hd * HEAD_DIM, (hd + 1) * HEAD_DIM)
                qh, kh, vh = q_ref[:, sl], k_ref[:, sl], v_ref[:, sl]
                doh = do_ref[:, sl]
                s = _dot_nt(qh, kh) + (dc_ref[:, hd:hd + 1] - dr_ref[hd:hd + 1, :])
                s = jnp.where(keep, s, -jnp.inf)
                p = jnp.exp(s - lse_ref[:, hd:hd + 1])
                delta = jnp.sum(doh.astype(F32) * o_ref[:, sl].astype(F32), axis=-1, keepdims=True)
                dp = _dot_nt(doh, vh)
                ds = p * (dp - delta)
                dsb = ds.astype(BF16)
                dv_acc[:, sl] += _dot_tn(p.astype(BF16), doh)
                dk_acc[:, sl] += _dot_tn(dsb, qh)
                dq_ref[qrows, sl] += jnp.dot(dsb, kh, preferred_element_type=F32)
                dd_ref[hd:hd + 1, :] -= jnp.sum(ds, axis=0, keepdims=True)
                ddq_ref[qrows, hd:hd + 1] += jnp.sum(ds, axis=-1, keepdims=True)

        @pl.when(i == nb - 1)
        def _():
            dk_ref[...] = dk_acc[...]
            dv_ref[...] = dv_acc[...]

    qspec = pl.BlockSpec((t, FOX_W), lambda j, i: (jnp.maximum(i, j), 0))
    qcol = pl.BlockSpec((t, 128), lambda j, i: (jnp.maximum(i, j), 0))
    kspec = pl.BlockSpec((t, FOX_W), lambda j, i: (j, 0))
    return pl.pallas_call(
        body, name="attn_bwd", grid=(nb, nb),
        in_specs=[qspec, kspec, kspec, qspec, qspec, qcol, qcol,
                  pl.BlockSpec((HEADS, t), lambda j, i: (0, j))],
        out_specs=[_const_spec((s_len, FOX_W)), kspec, kspec,
                   pl.BlockSpec((HEADS, t), lambda j, i: (0, j)), _const_spec((s_len, 128))],
        out_shape=[jax.ShapeDtypeStruct((s_len, FOX_W), F32),
                   jax.ShapeDtypeStruct((s_len, FOX_W), F32),
                   jax.ShapeDtypeStruct((s_len, FOX_W), F32),
                   jax.ShapeDtypeStruct((HEADS, s_len), F32),
                   jax.ShapeDtypeStruct((s_len, 128), F32)],
        scratch_shapes=[pltpu.VMEM((t, FOX_W), F32), pltpu.VMEM((t, FOX_W), F32)],
        compiler_params=_params(48, 2),
    )(qn, kn, v, o, do, lse, dcol, drow)


def _rev_cumsum(ddcol, ddq, triu):
    s_len = ddcol.shape[0]
    tm = TOKEN_TILE
    n = s_len // tm

    def body(x_ref, y_ref, tri_ref, o_ref, carry):
        @pl.when(pl.program_id(0) == 0)
        def _():
            carry[...] = jnp.zeros_like(carry)
        out = _tri_dot(tri_ref[...], x_ref[...] + y_ref[...]) + carry[...]
        o_ref[...] = out
        carry[...] = out[0:1, :]

    rev = pl.BlockSpec((tm, 128), lambda i: (n - 1 - i, 0))
    return pl.pallas_call(
        body, name="rev_cumsum", grid=(n,),
        in_specs=[rev, rev, _const_spec((tm, tm))], out_specs=rev,
        out_shape=jax.ShapeDtypeStruct((s_len, 128), F32),
        scratch_shapes=[pltpu.VMEM((1, 128), F32)],
        compiler_params=_params(16, 1),
    )(ddcol, ddq, triu)


def _proj_bwd(dqn, dkn, dv, dlogf, flog, qraw, kraw, duv, dgp, x, dx1, wcat, bdiag, gq, gk, g1,
              efold):
    s_len = x.shape[0]
    tm = TOKEN_TILE

    def body(dq_ref, dk_ref, dv_ref, dlf_ref, flog_ref, qr_ref, kr_ref, duv_ref, dgp_ref, x_ref,
             dx1_ref, w_ref, bd_ref, gq_ref, gk_ref, g1_ref, ef_ref,
             dx_ref, dproj_ref, dgq_ref, dgk_ref, dbf_ref, dg1_ref, gq_acc, gk_acc):
        step = pl.program_id(0)

        @pl.when(step == 0)
        def _():
            gq_acc[...] = jnp.zeros_like(gq_acc)
            gk_acc[...] = jnp.zeros_like(gk_acc)
            dbf_ref[...] = jnp.zeros_like(dbf_ref)
            dg1_ref[...] = jnp.zeros_like(dg1_ref)

        def head_bwd(raw_ref, dn_ref, g_ref, acc, scale):
            raw = raw_ref[...].astype(F32)
            r = lax.rsqrt(_seg_mean(raw * raw, bd_ref) + EPS)
            xhat = raw * r
            dn = dn_ref[...] * scale
            acc[0:1, :] += jnp.sum(dn * xhat, axis=0, keepdims=True)
            dyg = dn * g_ref[...]
            return r * (dyg - xhat * _seg_mean(dyg * xhat, bd_ref))

        dproj_ref[:, C_Q:C_K] = head_bwd(qr_ref, dq_ref, gq_ref, gq_acc, HEAD_DIM ** -0.5).astype(BF16)
        dproj_ref[:, C_K:C_V] = head_bwd(kr_ref, dk_ref, gk_ref, gk_acc, 1.0).astype(BF16)
        dproj_ref[:, C_V:C_F] = dv_ref[...].astype(BF16)
        dfl = dlf_ref[...] * _sigmoid(-flog_ref[...])
        dbf_ref[...] += jnp.sum(dfl, axis=0, keepdims=True)
        dproj_ref[:, C_F:C_UV] = dfl.astype(BF16)
        dproj_ref[:, C_UV:C_G] = duv_ref[...]
        dproj_ref[:, C_G:C_END] = dgp_ref[...]

        dh = _dot_nt(dproj_ref[...], w_ref[...])
        xf = x_ref[...]
        r = lax.rsqrt(jnp.mean(xf * xf, axis=-1, keepdims=True) + EPS)
        dg1_ref[...] += jnp.sum(dh * xf * r, axis=0, keepdims=True)
        dx_ref[...] = dx1_ref[...] + _rms_bwd(xf, r, g1_ref[...], dh)

        @pl.when(step == pl.num_programs(0) - 1)
        def _():
            dgq_ref[...] = _split3_dot(gq_acc[...], ef_ref[...])
            dgk_ref[...] = _split3_dot(gk_acc[...], ef_ref[...])

    outs = [((s_len, D_MODEL), F32, _row_spec(tm, D_MODEL)),
            ((s_len, C_END), BF16, _row_spec(tm, C_END)),
            ((8, 128), F32, _const_spec((8, 128))),
            ((8, 128), F32, _const_spec((8, 128))),
            ((1, 128), F32, _const_spec((1, 128))),
            ((1, D_MODEL), F32, _const_spec((1, D_MODEL)))]
    return pl.pallas_call(
        body, name="proj_bwd", grid=(s_len // tm,),
        in_specs=[_row_spec(tm, FOX_W), _row_spec(tm, FOX_W), _row_spec(tm, FOX_W),
                  _row_spec(tm, 128), _row_spec(tm, 128), _row_spec(tm, FOX_W),
                  _row_spec(tm, FOX_W), _row_spec(tm, 2 * SGU_W), _row_spec(tm, 2 * D_MODEL),
                  _row_spec(tm, D_MODEL), _row_spec(tm, D_MODEL), _const_spec(wcat.shape),
                  _const_spec(bdiag.shape), _const_spec((1, FOX_W)), _const_spec((1, FOX_W)),
                  _const_spec((1, D_MODEL)), _const_spec(efold.shape)],
        out_specs=[o[2] for o in outs],
        out_shape=[jax.ShapeDtypeStruct(o[0], o[1]) for o in outs],
        scratch_shapes=[pltpu.VMEM((8, FOX_W), F32), pltpu.VMEM((8, FOX_W), F32)],
        compiler_params=_params(56, 1),
    )(dqn, dkn, dv, dlogf, flog, qraw, kraw, duv, dgp, x, dx1, wcat, bdiag, gq, gk, g1, efold)


def _dw_matmul(a, b, tm, name):
    s_len, m = a.shape
    n = b.shape[1]
    tk = min(512, s_len)
    nk = s_len // tk

    def body(a_ref, b_ref, o_ref, acc):
        kk = pl.program_id(1)

        @pl.when(kk == 0)
        def _():
            acc[...] = jnp.zeros_like(acc)
        acc[...] += _dot_tn(a_ref[...], b_ref[...])

        @pl.when(kk == nk - 1)
        def _():
            o_ref[...] = acc[...].astype(BF16)

    return pl.pallas_call(
        body, name=name, grid=(m // tm, nk),
        in_specs=[pl.BlockSpec((tk, tm), lambda i, k: (k, i)),
                  pl.BlockSpec((tk, n), lambda i, k: (k, 0))],
        out_specs=pl.BlockSpec((tm, n), lambda i, k: (i, 0)),
        out_shape=jax.ShapeDtypeStruct((m, n), BF16),
        scratch_shapes=[pltpu.VMEM((tm, n), F32)],
        compiler_params=_params(56, 2),
    )(a, b)


def _adamw(parts, w, m, v, tr, name):
    n, rows, cols = parts.shape
    bc1 = 1.0 - ADAM_B1 ** ADAM_STEP
    bc2 = 1.0 - ADAM_B2 ** ADAM_STEP

    def body(p_ref, w_ref, m_ref, v_ref, g_ref, d_ref, mo_ref, vo_ref):
        g = p_ref[0].astype(F32)
        for idx in range(1, n):
            g = g + p_ref[idx].astype(F32)
        g_ref[...] = g
        mn = ADAM_B1 * m_ref[...] + (1.0 - ADAM_B1) * g
        vn = ADAM_B2 * v_ref[...] + (1.0 - ADAM_B2) * (g * g)
        mo_ref[...] = mn
        vo_ref[...] = vn
        m_hat = mn / bc1
        v_hat = vn / bc2
        d_ref[...] = -ADAM_LR * (m_hat / (jnp.sqrt(v_hat) + ADAM_EPS) + ADAM_WD * w_ref[...])

    spec = pl.BlockSpec((tr, cols), lambda i: (i, 0))
    return pl.pallas_call(
        body, name=name, grid=(rows // tr,),
        in_specs=[pl.BlockSpec((n, tr, cols), lambda i: (0, i, 0)), spec, spec, spec],
        out_specs=[spec] * 4,
        out_shape=[jax.ShapeDtypeStruct((rows, cols), F32)] * 4,
        compiler_params=_params(48, 1),
    )(parts, w, m, v)


def _sum_parts(parts, name):
    n, rows, cols = parts.shape

    def body(p_ref, o_ref):
        g = p_ref[0]
        for idx in range(1, n):
            g = g + p_ref[idx]
        o_ref[...] = g

    return pl.pallas_call(
        body, name=name, out_shape=jax.ShapeDtypeStruct((rows, cols), F32),
        in_specs=[_const_spec(parts.shape)], out_specs=_const_spec((rows, cols)), grid=(1,),
        compiler_params=_params(16, 1),
    )(parts)


SMALL_NAMES = ("g_pre_mix", "b_forget", "g_q", "g_k", "g_sgu", "b_sgu", "w_spatial", "b_spatial",
               "g_post_mix", "g_pre_ffn", "g_post_ffn")
SMALL_TOTAL = N_DEV * SMALL_ROWS * 1024


def _pack_small(d):
    flat = jnp.concatenate([d[k].reshape(-1).astype(F32) for k in SMALL_NAMES])
    flat = jnp.pad(flat, (0, SMALL_TOTAL - flat.shape[0]))
    return flat.reshape(N_DEV * SMALL_ROWS, 1024)


def _unpack_small(packed, shapes):
    flat = packed.reshape(-1)
    out, off = {}, 0
    for k in SMALL_NAMES:
        size = math.prod(shapes[k])
        out[k] = flat[off:off + size].reshape(shapes[k])
        off += size
    return out


def _cols_to_blocks(full, width):
    r = full.shape[0]
    return jnp.transpose(full.reshape(r, N_DEV, width), (1, 0, 2))


def _blocks_to_cols(blocks):
    n, r, width = blocks.shape
    return jnp.transpose(blocks, (1, 0, 2)).reshape(r, n * width)


def kernel(x, g_pre_mix, w_in, b_forget, g_q, g_k, g_sgu, b_sgu, w_spatial, b_spatial, w_branch_a, w_branch_b, w_out, g_post_mix, g_pre_ffn, w_ffn_in, w_ffn_down, g_post_ffn, loss_target, m_g_pre_mix, m_w_in, m_b_forget, m_g_q, m_g_k, m_g_sgu, m_b_sgu, m_w_spatial, m_b_spatial, m_w_branch_a, m_w_branch_b, m_w_out, m_g_post_mix, m_g_pre_ffn, m_w_ffn_in, m_w_ffn_down, m_g_post_ffn, v_g_pre_mix, v_w_in, v_b_forget, v_g_q, v_g_k, v_g_sgu, v_b_sgu, v_w_spatial, v_b_spatial, v_w_branch_a, v_w_branch_b, v_w_out, v_g_post_mix, v_g_pre_ffn, v_w_ffn_in, v_w_ffn_down, v_g_post_ffn):
    big_names = ("w_in", "w_branch_a", "w_branch_b", "w_out", "w_ffn_in", "w_ffn_down")
    weights = dict(g_pre_mix=g_pre_mix, w_in=w_in, b_forget=b_forget, g_q=g_q, g_k=g_k, g_sgu=g_sgu,
                   b_sgu=b_sgu, w_spatial=w_spatial, b_spatial=b_spatial, w_branch_a=w_branch_a,
                   w_branch_b=w_branch_b, w_out=w_out, g_post_mix=g_post_mix, g_pre_ffn=g_pre_ffn,
                   w_ffn_in=w_ffn_in, w_ffn_down=w_ffn_down, g_post_ffn=g_post_ffn)
    mom1 = dict(g_pre_mix=m_g_pre_mix, w_in=m_w_in, b_forget=m_b_forget, g_q=m_g_q, g_k=m_g_k,
                g_sgu=m_g_sgu, b_sgu=m_b_sgu, w_spatial=m_w_spatial, b_spatial=m_b_spatial,
                w_branch_a=m_w_branch_a, w_branch_b=m_w_branch_b, w_out=m_w_out,
                g_post_mix=m_g_post_mix, g_pre_ffn=m_g_pre_ffn, w_ffn_in=m_w_ffn_in,
                w_ffn_down=m_w_ffn_down, g_post_ffn=m_g_post_ffn)
    mom2 = dict(g_pre_mix=v_g_pre_mix, w_in=v_w_in, b_forget=v_b_forget, g_q=v_g_q, g_k=v_g_k,
                g_sgu=v_g_sgu, b_sgu=v_b_sgu, w_spatial=v_w_spatial, b_spatial=v_b_spatial,
                w_branch_a=v_w_branch_a, w_branch_b=v_w_branch_b, w_out=v_w_out,
                g_post_mix=v_g_post_mix, g_pre_ffn=v_g_pre_ffn, w_ffn_in=v_w_ffn_in,
                w_ffn_down=v_w_ffn_down, g_post_ffn=v_g_post_ffn)
    names = list(weights)
    shapes = {k: weights[k].shape for k in names}

    s_len = x.shape[1]
    xs = x.reshape(s_len, D_MODEL)
    tgt = loss_target.reshape(s_len, D_MODEL)

    shards = [weights[k][0].astype(BF16) for k in big_names]
    gathered = _exchange(shards, "gather_weights", gather=True)
    win_full = _blocks_to_cols(gathered[0])
    wa = _blocks_to_cols(gathered[1])
    wb = _blocks_to_cols(gathered[2])
    wout = gathered[3].reshape(D_MODEL, D_MODEL)
    wffn = _blocks_to_cols(gathered[4])
    wdown = gathered[5].reshape(D_FF, D_MODEL)
    f_off = 3 * FOX_W
    u_off = f_off + HEADS
    g_off = u_off + 2 * SGU_W
    wcat = jnp.concatenate([
        win_full[:, :f_off],
        jnp.pad(win_full[:, f_off:u_off], ((0, 0), (0, 128 - HEADS))),
        win_full[:, u_off:g_off], win_full[:, g_off:]], axis=1)

    seg = jnp.arange(FOX_W) // HEAD_DIM
    bdiag = (seg[:, None] == seg[None, :]).astype(BF16)
    tm = TOKEN_TILE
    tril = (jnp.arange(tm)[None, :] <= jnp.arange(tm)[:, None]).astype(BF16)
    triu = tril.T
    egrp = (seg[:, None] == jnp.arange(128)[None, :]).astype(BF16)
    efold = ((jnp.arange(FOX_W) % HEAD_DIM)[:, None] == jnp.arange(128)[None, :]).astype(BF16)
    gq512 = jnp.tile(g_q.reshape(1, HEAD_DIM), (1, HEADS))
    gk512 = jnp.tile(g_k.reshape(1, HEAD_DIM), (1, HEADS))
    bfor = jnp.pad(b_forget.reshape(1, HEADS), ((0, 0), (0, 128 - HEADS)))
    pos = jnp.arange(WINDOW)
    wmask = ((pos[None, :] // CHUNK) <= (pos[:, None] // CHUNK))
    wsm_f = jnp.where(wmask[None], w_spatial[0], 0.0)
    wsm = wsm_f.astype(BF16)
    wsmt = jnp.transpose(wsm_f, (0, 2, 1)).astype(BF16)
    bsf = jnp.repeat(jnp.transpose(b_spatial[0]), HEAD_DIM, axis=1)
    wmask_f = wmask.astype(F32)

    (h, qn, kn, vv, qraw, kraw, flog, dcol, uvpre, gpre) = _proj_fwd(
        xs, g_pre_mix, wcat, bdiag, gq512, gk512, bfor, tril)
    drow = jnp.transpose(dcol[:, :HEADS])
    attn, lse = _attn_fwd(qn, kn, vv, dcol, drow)
    sgu, ya, yb, merged, om, x1 = _mix_fwd(attn, uvpre, gpre, xs, wa, wb, wout, wsm, bsf,
                                           g_sgu, b_sgu, g_post_mix)
    (dx1, h2, act, dff, dgu, loss_acc, dg_post_ffn, dg_pre_ffn) = _ffn_fwd_bwd(
        x1, tgt, wffn, wdown, g_pre_ffn, g_post_ffn)

    (dom, dya, dyb, dgp, dattn, duv, dws, dbs, dg_sgu, db_sgu, dg_post_mix) = _mix_bwd(
        dx1, om, ya, yb, gpre, uvpre, wout, wa, wb, wsm, wsmt, bsf, g_sgu, b_sgu, g_post_mix,
        wmask_f, egrp)
    dqn, dkn, dvv, ddrow, ddq = _attn_bwd(qn, kn, vv, attn, dattn, lse, dcol, drow)
    ddcol = jnp.pad(jnp.transpose(ddrow), ((0, 0), (0, 128 - HEADS)))
    dlogf = _rev_cumsum(ddcol, ddq, triu)
    dx, dproj, dgq, dgk, dbf, dg_pre_mix = _proj_bwd(
        dqn, dkn, dvv, dlogf, flog, qraw, kraw, duv, dgp, xs, dx1, wcat, bdiag, gq512, gk512,
        g_pre_mix, efold)

    dw_down = _dw_matmul(act, dff, D_FF // 2, "dw_down")
    dw_ffn = _dw_matmul(h2, dgu, 512, "dw_ffn_in")
    dw_out = _dw_matmul(merged, dom, 512, "dw_out")
    dw_a = _dw_matmul(attn, dya, 512, "dw_a")
    dw_b = _dw_matmul(sgu, dyb, 512, "dw_b")
    dw_cat = _dw_matmul(h, dproj, 512, "dw_in")
    dw_in = jnp.concatenate([dw_cat[:, :C_F], dw_cat[:, C_F:C_F + HEADS], dw_cat[:, C_UV:]], axis=1)

    small_local = dict(
        g_pre_mix=dg_pre_mix, b_forget=dbf[:, :HEADS], g_q=dgq[0:1, :HEAD_DIM],
        g_k=dgk[0:1, :HEAD_DIM], g_sgu=dg_sgu, b_sgu=db_sgu, w_spatial=dws,
        b_spatial=jnp.transpose(dbs[:, :GROUPS]), g_post_mix=dg_post_mix, g_pre_ffn=dg_pre_ffn,
        g_post_ffn=dg_post_ffn)
    small_parts = _pack_small(small_local).reshape(N_DEV, SMALL_ROWS, 1024)

    big_parts = [
        _cols_to_blocks(dw_in, IN_COLS // N_DEV),
        _cols_to_blocks(dw_a, D_MODEL // N_DEV),
        _cols_to_blocks(dw_b, D_MODEL // N_DEV),
        dw_out.reshape(N_DEV, D_MODEL // N_DEV, D_MODEL),
        _cols_to_blocks(dw_ffn, 2 * D_FF // N_DEV),
        dw_down.reshape(N_DEV, D_FF // N_DEV, D_MODEL),
    ]
    received = _exchange(big_parts + [small_parts], "scatter_grads", gather=False)

    grads, deltas, new_m, new_v = {}, {}, {}, {}
    row_tiles = {"w_in": 128, "w_branch_a": 512, "w_branch_b": 512, "w_out": 128, "w_ffn_in": 128,
                 "w_ffn_down": 352}
    for idx, k in enumerate(big_names):
        g, d, mn, vn = _adamw(received[idx], weights[k][0], mom1[k][0], mom2[k][0], row_tiles[k],
                              "adamw_" + k)
        grads[k], deltas[k], new_m[k], new_v[k] = g[None], d[None], mn[None], vn[None]

    small_sum = _sum_parts(received[-1], "sum_small")
    (small_all,) = _exchange([small_sum], "gather_small", gather=True)
    small_all = small_all.reshape(1, N_DEV * SMALL_ROWS, 1024)
    sg, sd, sm, sv = _adamw(small_all, _pack_small(weights), _pack_small(mom1), _pack_small(mom2),
                            N_DEV * SMALL_ROWS, "adamw_small")
    for dst, packed in ((grads, sg), (deltas, sd), (new_m, sm), (new_v, sv)):
        dst.update(_unpack_small(packed, shapes))

    loss = lax.psum(loss_acc[0, 0], ("x", "y", "c"))
    return (loss, dx.reshape(x.shape), *[grads[k] for k in names], *[deltas[k] for k in names],
            *[new_m[k] for k in names], *[new_v[k] for k in names])
```

```python
import functools
import math

import jax
import jax.numpy as jnp
from jax import lax
from jax.experimental import pallas as pl
from jax.experimental.pallas import tpu as pltpu

F32 = jnp.float32
BF16 = jnp.bfloat16

D_MODEL = 1024
FOX_W = 512
HEADS = 8
HEAD_DIM = 64
SGU_W = 512
GROUPS = 8
WINDOW = 128
CHUNK = 64
D_FF = 2816
IN_COLS = 4616
EPS = 1e-6
N_DEV = 8
LOG2E = 1.4426950408889634
LN2 = 0.6931471805599453

C_Q, C_K, C_V, C_F, C_UV, C_G, C_END = 0, 512, 1024, 1536, 1664, 2688, 4736

ADAM_LR, ADAM_B1, ADAM_B2, ADAM_EPS, ADAM_WD, ADAM_STEP = 0.001, 0.9, 0.999, 1e-08, 0.01, 10

MIB = 1024 * 1024
TOKEN_TILE = 256
ATTN_TILE = 256
SLAB_W = HEADS * 128
QT_ROWS = 72

SMALL_ROWS = 24


def _params(vmem_mib, n_axes):
    return pltpu.CompilerParams(
        dimension_semantics=("arbitrary",) * n_axes, vmem_limit_bytes=vmem_mib * MIB)


def _const_spec(shape):
    nd = len(shape)
    return pl.BlockSpec(shape, lambda *_: (0,) * nd)


def _row_spec(tm, cols):
    return pl.BlockSpec((tm, cols), lambda i: (i, 0))


def _tile_spec(rows, tm):
    return pl.BlockSpec((1, rows, tm), lambda i: (i, 0, 0))


def _split3_dot(x, e):
    x1 = x.astype(BF16)
    r1 = x - x1.astype(F32)
    x2 = r1.astype(BF16)
    x3 = (r1 - x2.astype(F32)).astype(BF16)
    dot = functools.partial(jnp.dot, preferred_element_type=F32)
    return dot(x1, e) + dot(x2, e) + dot(x3, e)


def _tri_dot(tri, x):
    x1 = x.astype(BF16)
    r1 = x - x1.astype(F32)
    x2 = r1.astype(BF16)
    x3 = (r1 - x2.astype(F32)).astype(BF16)
    dot = functools.partial(jnp.dot, preferred_element_type=F32)
    return dot(tri, x1) + dot(tri, x2) + dot(tri, x3)


def _seg_mean(sq, bd_ref):
    hi = sq.astype(BF16)
    lo = (sq - hi.astype(F32)).astype(BF16)
    bd = bd_ref[...]
    s = jnp.dot(hi, bd, preferred_element_type=F32) + jnp.dot(lo, bd, preferred_element_type=F32)
    return s * (1.0 / HEAD_DIM)


def _dot_nt(a, b):
    return lax.dot_general(a, b, (((1,), (1,)), ((), ())), preferred_element_type=F32)


def _dot_tn(a, b):
    return lax.dot_general(a, b, (((0,), (0,)), ((), ())), preferred_element_type=F32)


def _sigmoid(x):
    return 1.0 / (1.0 + jnp.exp(-x))


_GELU_C = math.sqrt(2.0 / math.pi)


def _gelu_and_grad(x):
    inner = _GELU_C * (x + 0.044715 * x * x * x)
    t = jnp.tanh(inner)
    y = 0.5 * x * (1.0 + t)
    dy = 0.5 * (1.0 + t) + 0.5 * x * (1.0 - t * t) * _GELU_C * (1.0 + 3.0 * 0.044715 * x * x)
    return y, dy


def _rms_bwd(xin, r, g, dy):
    dyg = dy * g
    return r * dyg - xin * (r * r * r) * jnp.mean(dyg * xin, axis=-1, keepdims=True)


def _mesh_pos():
    x, y, c = lax.axis_index("x"), lax.axis_index("y"), lax.axis_index("c")
    return x, y, c


def _peer(k):
    x, y, c = _mesh_pos()
    px = (1 - x) if (k >> 2) & 1 else x
    py = (1 - y) if (k >> 1) & 1 else y
    pc = (1 - c) if k & 1 else c
    return (px, py, pc), 4 * px + 2 * py + pc


def _exchange(arrs, name, gather):
    n = len(arrs)
    if gather:
        out_shape = [jax.ShapeDtypeStruct((N_DEV,) + a.shape, a.dtype) for a in arrs]
    else:
        out_shape = [jax.ShapeDtypeStruct(a.shape, a.dtype) for a in arrs]

    def body(*refs):
        ins, outs = refs[:n], refs[n:2 * n]
        send_sems, recv_sems, local_sems = refs[2 * n:]
        x, y, c = _mesh_pos()
        me = 4 * x + 2 * y + c

        def src(a, idx):
            return ins[a] if gather else ins[a].at[idx]

        local = []
        for a in range(n):
            cp = pltpu.make_async_copy(src(a, me), outs[a].at[me], local_sems.at[a])
            cp.start()
            local.append(cp)
        sends = []
        for k in range(1, N_DEV):
            peer, pidx = _peer(k)
            for a in range(n):
                cp = pltpu.make_async_remote_copy(
                    src_ref=src(a, pidx), dst_ref=outs[a].at[me],
                    send_sem=send_sems.at[a, k - 1], recv_sem=recv_sems.at[a, k - 1],
                    device_id=peer, device_id_type=pl.DeviceIdType.MESH)
                cp.start()
                sends.append(cp)
        for k in range(1, N_DEV):
            peer, pidx = _peer(k)
            for a in range(n):
                pltpu.make_async_remote_copy(
                    src_ref=src(a, pidx), dst_ref=outs[a].at[pidx],
                    send_sem=send_sems.at[a, k - 1], recv_sem=recv_sems.at[a, k - 1],
                    device_id=peer, device_id_type=pl.DeviceIdType.MESH).wait_recv()
        for cp in sends:
            cp.wait_send()
        for cp in local:
            cp.wait()

    any_spec = pl.BlockSpec(memory_space=pl.ANY)
    return pl.pallas_call(
        body, name=name, out_shape=out_shape,
        in_specs=[any_spec] * n, out_specs=[any_spec] * n,
        scratch_shapes=[pltpu.SemaphoreType.DMA((n, N_DEV - 1)),
                        pltpu.SemaphoreType.DMA((n, N_DEV - 1)),
                        pltpu.SemaphoreType.DMA((n,))],
    )(*arrs)


def _proj_fwd(x, g1, wcat, bdiag, gq, gk, bfor, tri, place, pdq, pdk, ones_q, ones_k):
    s_len = x.shape[0]
    tm = TOKEN_TILE
    nt = s_len // tm

    def body(x_ref, g1_ref, w_ref, bd_ref, gq_ref, gk_ref, bf_ref, tri_ref, pl_ref, pdq_ref,
             pdk_ref, oq_ref, ok_ref,
             h_ref, qa_ref, ka_ref, kat_ref, vs_ref, vt_ref, qr_ref, kr_ref, flog_ref, uv_ref,
             gp_ref, carry):
        @pl.when(pl.program_id(0) == 0)
        def _():
            carry[...] = jnp.zeros_like(carry)

        xf = x_ref[...]
        r = lax.rsqrt(jnp.mean(xf * xf, axis=-1, keepdims=True) + EPS)
        h = (xf * r * g1_ref[...]).astype(BF16)
        h_ref[...] = h
        dot = functools.partial(jnp.dot, preferred_element_type=F32)

        def proj(lo, hi):
            return dot(h, w_ref[:, lo:hi])

        flog = proj(C_F, C_UV) + bf_ref[...]
        flog_ref[...] = flog
        lane = lax.broadcasted_iota(jnp.int32, flog.shape, 1)
        logf = jnp.minimum(flog, 0.0) - jnp.log(1.0 + jnp.exp(-jnp.abs(flog)))
        logf = jnp.where(lane < HEADS, logf, 0.0)
        dcum = _tri_dot(tri_ref[...], logf) + carry[...]
        carry[...] = dcum[tm - 1:tm, :]
        d2 = dcum * LOG2E
        d2a = d2.astype(BF16)
        rem = d2 - d2a.astype(F32)
        d2b = rem.astype(BF16)
        d2c = (rem - d2b.astype(F32)).astype(BF16)

        place_m = pl_ref[...]
        q = proj(C_Q, C_K)
        qr_ref[...] = q.astype(BF16)
        rq = lax.rsqrt(_seg_mean(q * q, bd_ref) + EPS)
        qn = (q * rq * (gq_ref[...] * (HEAD_DIM ** -0.5 * LOG2E))).astype(BF16)
        qa = (dot(qn, place_m) + dot(d2a, pdq_ref[0]) + dot(d2b, pdq_ref[1])
              + dot(d2c, pdq_ref[2]) + oq_ref[...])
        qa_ref[...] = qa.astype(BF16)

        k = proj(C_K, C_V)
        kr_ref[...] = k.astype(BF16)
        rk = lax.rsqrt(_seg_mean(k * k, bd_ref) + EPS)
        kn = (k * rk * gk_ref[...]).astype(BF16)
        ka = (dot(kn, place_m) - dot(d2a, pdk_ref[0]) - dot(d2b, pdk_ref[1])
              - dot(d2c, pdk_ref[2]) + ok_ref[...])
        ka_ref[...] = ka.astype(BF16)
        kat_ref[0] = ka.T.astype(BF16)

        v = proj(C_V, C_F)
        vs_ref[...] = dot(v.astype(BF16), place_m).astype(BF16)
        vt_ref[0] = v.T.astype(BF16)
        uv_ref[...] = proj(C_UV, C_G).astype(BF16)
        gp_ref[...] = proj(C_G, C_END).astype(BF16)

    outs = [((s_len, D_MODEL), BF16, _row_spec(tm, D_MODEL)),
            ((s_len, SLAB_W), BF16, _row_spec(tm, SLAB_W)),
            ((s_len, SLAB_W), BF16, _row_spec(tm, SLAB_W)),
            ((nt, SLAB_W, tm), BF16, _tile_spec(SLAB_W, tm)),
            ((s_len, SLAB_W), BF16, _row_spec(tm, SLAB_W)),
            ((nt, FOX_W, tm), BF16, _tile_spec(FOX_W, tm)),
            ((s_len, FOX_W), BF16, _row_spec(tm, FOX_W)),
            ((s_len, FOX_W), BF16, _row_spec(tm, FOX_W)),
            ((s_len, 128), F32, _row_spec(tm, 128)),
            ((s_len, 2 * SGU_W), BF16, _row_spec(tm, 2 * SGU_W)),
            ((s_len, 2 * D_MODEL), BF16, _row_spec(tm, 2 * D_MODEL))]
    return pl.pallas_call(
        body, name="proj_fwd", grid=(nt,),
        in_specs=[_row_spec(tm, D_MODEL), _const_spec((1, D_MODEL)), _const_spec(wcat.shape),
                  _const_spec(bdiag.shape), _const_spec((1, FOX_W)), _const_spec((1, FOX_W)),
                  _const_spec((1, 128)), _const_spec((tm, tm)), _const_spec(place.shape),
                  _const_spec(pdq.shape), _const_spec(pdk.shape), _const_spec(ones_q.shape),
                  _const_spec(ones_k.shape)],
        out_specs=[o[2] for o in outs],
        out_shape=[jax.ShapeDtypeStruct(o[0], o[1]) for o in outs],
        scratch_shapes=[pltpu.VMEM((1, 128), F32)],
        compiler_params=_params(56, 1),
    )(x, g1, wcat, bdiag, gq, gk, bfor, tri, place, pdq, pdk, ones_q, ones_k)


def _attn_fwd(qa, ka, vt):
    s_len = qa.shape[0]
    t = ATTN_TILE
    nb = s_len // t

    def body(q_ref, k_ref, vt_ref, o_ref, lse_ref, m_sc, l_sc, acc_sc):
        i = pl.program_id(0)
        m_sc[...] = jnp.full_like(m_sc, -jnp.inf)
        l_sc[...] = jnp.zeros_like(l_sc)
        acc_sc[...] = jnp.zeros_like(acc_sc)

        def tile(j, masked):
            krows = pl.ds(pl.multiple_of(j * t, t), t)
            if masked:
                keep = (lax.broadcasted_iota(jnp.int32, (t, t), 0)
                        <= lax.broadcasted_iota(jnp.int32, (t, t), 1))
            for hd in range(HEADS):
                sl = slice(hd * 128, (hd + 1) * 128)
                hr = slice(hd * HEAD_DIM, (hd + 1) * HEAD_DIM)
                st = _dot_nt(k_ref[krows, sl], q_ref[:, sl])
                if masked:
                    st = jnp.where(keep, st, -jnp.inf)
                m_prev = m_sc[hd:hd + 1, :]
                m_new = jnp.maximum(m_prev, jnp.max(st, axis=0, keepdims=True))
                alpha = jnp.exp2(m_prev - m_new)
                pt = jnp.exp2(st - m_new)
                l_sc[hd:hd + 1, :] = alpha * l_sc[hd:hd + 1, :] + jnp.sum(pt, axis=0, keepdims=True)
                acc_sc[hr, :] = alpha * acc_sc[hr, :] + jnp.dot(
                    vt_ref[j, hr, :], pt.astype(BF16), preferred_element_type=F32)
                m_sc[hd:hd + 1, :] = m_new

        def off_diagonal(j, carry):
            tile(j, False)
            return carry

        lax.fori_loop(0, i, off_diagonal, 0)
        tile(i, True)

        for hd in range(HEADS):
            hr = slice(hd * HEAD_DIM, (hd + 1) * HEAD_DIM)
            l = l_sc[hd:hd + 1, :]
            acc_sc[hr, :] = acc_sc[hr, :] / l
            lse_ref[0, hd:hd + 1, :] = m_sc[hd:hd + 1, :] + jnp.log2(l)
        o_ref[...] = acc_sc[...].T.astype(BF16)

    return pl.pallas_call(
        body, name="attn_fwd", grid=(nb,),
        in_specs=[_row_spec(t, SLAB_W), _const_spec(ka.shape), _const_spec(vt.shape)],
        out_specs=[_row_spec(t, FOX_W), _tile_spec(HEADS, t)],
        out_shape=[jax.ShapeDtypeStruct((s_len, FOX_W), BF16),
                   jax.ShapeDtypeStruct((nb, HEADS, t), F32)],
        scratch_shapes=[pltpu.VMEM((HEADS, t), F32), pltpu.VMEM((HEADS, t), F32),
                        pltpu.VMEM((FOX_W, t), F32)],
        compiler_params=_params(48, 1),
    )(qa, ka, vt)


def _sgu_mix(vn, ws_ref):
    tm = vn.shape[0]
    lane = lax.broadcasted_iota(jnp.int32, (WINDOW, 128), 1)
    low = lane < HEAD_DIM
    wins = []
    for w in range(tm // WINDOW):
        slabs = []
        for p in range(GROUPS // 2):
            v2 = vn[w * WINDOW:(w + 1) * WINDOW, p * 128:(p + 1) * 128]
            lo = jnp.where(low, v2, 0.0).astype(BF16)
            hi = jnp.where(low, 0.0, v2).astype(BF16)
            slabs.append(jnp.dot(ws_ref[2 * p], lo, preferred_element_type=F32)
                         + jnp.dot(ws_ref[2 * p + 1], hi, preferred_element_type=F32))
        wins.append(jnp.concatenate(slabs, axis=1))
    return jnp.concatenate(wins, axis=0) if len(wins) > 1 else wins[0]


def _layernorm_fwd(vv, g, b):
    mu = jnp.mean(vv, axis=-1, keepdims=True)
    xc = vv - mu
    r = lax.rsqrt(jnp.mean(xc * xc, axis=-1, keepdims=True) + EPS)
    xh = xc * r
    return xh * g + b, xh, r


def _mix_fwd(attn, uvpre, gpre, x, wa, wb, wout, wsm, bsf, gsgu, bsgu, gpost):
    s_len = x.shape[0]
    tm = TOKEN_TILE

    def body(o_ref, uv_ref, gp_ref, x_ref, wa_ref, wb_ref, wo_ref, ws_ref, bs_ref, gs_ref, bsg_ref,
             gpost_ref, sgu_ref, ya_ref, yb_ref, mg_ref, om_ref, x1_ref):
        uvp = uv_ref[...].astype(F32)
        uv, _ = _gelu_and_grad(uvp)
        u, vv = uv[:, :SGU_W], uv[:, SGU_W:]
        vn, _, _ = _layernorm_fwd(vv, gs_ref[...], bsg_ref[...])
        bias = bs_ref[...]
        if tm > WINDOW:
            bias = jnp.concatenate([bias] * (tm // WINDOW), axis=0)
        mixed = _sgu_mix(vn, ws_ref) + bias
        sgu = (u * mixed).astype(BF16)
        sgu_ref[...] = sgu
        ya = jnp.dot(o_ref[...], wa_ref[...], preferred_element_type=F32)
        yb = jnp.dot(sgu, wb_ref[...], preferred_element_type=F32)
        ya_ref[...] = ya.astype(BF16)
        yb_ref[...] = yb.astype(BF16)
        gates = _sigmoid(gp_ref[...].astype(F32))
        merged = (gates[:, :D_MODEL] * ya + gates[:, D_MODEL:] * yb).astype(BF16)
        mg_ref[...] = merged
        om = jnp.dot(merged, wo_ref[...], preferred_element_type=F32)
        om_ref[...] = om
        r = lax.rsqrt(jnp.mean(om * om, axis=-1, keepdims=True) + EPS)
        x1_ref[...] = x_ref[...] + om * r * gpost_ref[...]

    outs = [(SGU_W, BF16), (D_MODEL, BF16), (D_MODEL, BF16), (D_MODEL, BF16), (D_MODEL, F32),
            (D_MODEL, F32)]
    return pl.pallas_call(
        body, name="mix_fwd", grid=(s_len // tm,),
        in_specs=[_row_spec(tm, FOX_W), _row_spec(tm, 2 * SGU_W), _row_spec(tm, 2 * D_MODEL),
                  _row_spec(tm, D_MODEL), _const_spec(wa.shape), _const_spec(wb.shape),
                  _const_spec(wout.shape), _const_spec(wsm.shape), _const_spec(bsf.shape),
                  _const_spec((1, SGU_W)), _const_spec((1, SGU_W)), _const_spec((1, D_MODEL))],
        out_specs=[_row_spec(tm, c) for c, _ in outs],
        out_shape=[jax.ShapeDtypeStruct((s_len, c), dt) for c, dt in outs],
        compiler_params=_params(48, 1),
    )(attn, uvpre, gpre, x, wa, wb, wout, wsm, bsf, gsgu, bsgu, gpost)


def _ffn_fwd_bwd(x1, tgt, wffn, wdown, gpre, gpost):
    s_len = x1.shape[0]
    tm = TOKEN_TILE

    def body(x1_ref, t_ref, wi_ref, wd_ref, gpre_ref, gpost_ref,
             dx1_ref, h2_ref, act_ref, dff_ref, dgu_ref, loss_ref, dgpost_ref, dgpre_ref):
        @pl.when(pl.program_id(0) == 0)
        def _():
            loss_ref[...] = jnp.zeros_like(loss_ref)
            dgpost_ref[...] = jnp.zeros_like(dgpost_ref)
            dgpre_ref[...] = jnp.zeros_like(dgpre_ref)

        x1v = x1_ref[...]
        r2 = lax.rsqrt(jnp.mean(x1v * x1v, axis=-1, keepdims=True) + EPS)
        gpre_v = gpre_ref[...]
        h2 = (x1v * r2 * gpre_v).astype(BF16)
        h2_ref[...] = h2
        gg = jnp.dot(h2, wi_ref[:, :D_FF], preferred_element_type=F32)
        uu = jnp.dot(h2, wi_ref[:, D_FF:], preferred_element_type=F32)
        sg = _sigmoid(gg)
        silu = gg * sg
        act = (silu * uu).astype(BF16)
        act_ref[...] = act
        ff = jnp.dot(act, wd_ref[...], preferred_element_type=F32)
        r3 = lax.rsqrt(jnp.mean(ff * ff, axis=-1, keepdims=True) + EPS)
        gpost_v = gpost_ref[...]
        y = x1v + ff * r3 * gpost_v
        err = y - t_ref[...]
        loss_ref[...] += jnp.sum(err * err) * (0.5 / D_MODEL)
        dy = err * (1.0 / D_MODEL)
        dgpost_ref[...] += jnp.sum(dy * ff * r3, axis=0, keepdims=True)
        dff = _rms_bwd(ff, r3, gpost_v, dy).astype(BF16)
        dff_ref[...] = dff
        dact = _dot_nt(dff, wd_ref[...])
        dgg = (dact * uu * (sg * (1.0 + gg * (1.0 - sg)))).astype(BF16)
        duu = (dact * silu).astype(BF16)
        dgu_ref[:, :D_FF] = dgg
        dgu_ref[:, D_FF:] = duu
        dh2 = _dot_nt(dgg, wi_ref[:, :D_FF]) + _dot_nt(duu, wi_ref[:, D_FF:])
        dgpre_ref[...] += jnp.sum(dh2 * x1v * r2, axis=0, keepdims=True)
        dx1_ref[...] = dy + _rms_bwd(x1v, r2, gpre_v, dh2)

    outs = [((s_len, D_MODEL), F32, _row_spec(tm, D_MODEL)),
            ((s_len, D_MODEL), BF16, _row_spec(tm, D_MODEL)),
            ((s_len, D_FF), BF16, _row_spec(tm, D_FF)),
            ((s_len, D_MODEL), BF16, _row_spec(tm, D_MODEL)),
            ((s_len, 2 * D_FF), BF16, _row_spec(tm, 2 * D_FF)),
            ((1, 128), F32, _const_spec((1, 128))),
            ((1, D_MODEL), F32, _const_spec((1, D_MODEL))),
            ((1, D_MODEL), F32, _const_spec((1, D_MODEL)))]
    return pl.pallas_call(
        body, name="ffn_fwd_bwd", grid=(s_len // tm,),
        in_specs=[_row_spec(tm, D_MODEL), _row_spec(tm, D_MODEL), _const_spec(wffn.shape),
                  _const_spec(wdown.shape), _const_spec((1, D_MODEL)), _const_spec((1, D_MODEL))],
        out_specs=[o[2] for o in outs],
        out_shape=[jax.ShapeDtypeStruct(o[0], o[1]) for o in outs],
        compiler_params=_params(60, 1),
    )(x1, tgt, wffn, wdown, gpre, gpost)


def _mix_bwd(dx1, om, ya, yb, gpre, uvpre, attn, wout, wa, wb, wsm, wsmt, bsf, gsgu, bsgu, gpost,
             wmask, egrp):
    s_len = dx1.shape[0]
    tm = TOKEN_TILE
    nw = tm // WINDOW
    nt = s_len // tm

    def body(dx1_ref, om_ref, ya_ref, yb_ref, gp_ref, uv_ref, o_ref, wo_ref, wa_ref, wb_ref, ws_ref,
             wst_ref, bs_ref, gs_ref, bsg_ref, gpost_ref, mask_ref, eg_ref,
             dom_ref, dya_ref, dyb_ref, dgp_ref, dot_ref, delta_ref, duv_ref,
             dws_ref, dbs_ref, dgs_ref, dbsg_ref, dgpost_ref, dbs_acc):
        step = pl.program_id(0)

        @pl.when(step == 0)
        def _():
            dws_ref[...] = jnp.zeros_like(dws_ref)
            dbs_acc[...] = jnp.zeros_like(dbs_acc)
            dgs_ref[...] = jnp.zeros_like(dgs_ref)
            dbsg_ref[...] = jnp.zeros_like(dbsg_ref)
            dgpost_ref[...] = jnp.zeros_like(dgpost_ref)

        om = om_ref[...]
        dx1v = dx1_ref[...]
        r = lax.rsqrt(jnp.mean(om * om, axis=-1, keepdims=True) + EPS)
        gpost_v = gpost_ref[...]
        dgpost_ref[...] += jnp.sum(dx1v * om * r, axis=0, keepdims=True)
        dom = _rms_bwd(om, r, gpost_v, dx1v).astype(BF16)
        dom_ref[...] = dom
        dmg = _dot_nt(dom, wo_ref[...])

        gates = _sigmoid(gp_ref[...].astype(F32))
        ga, gb = gates[:, :D_MODEL], gates[:, D_MODEL:]
        yav, ybv = ya_ref[...].astype(F32), yb_ref[...].astype(F32)
        dya = (dmg * ga).astype(BF16)
        dyb = (dmg * gb).astype(BF16)
        dya_ref[...] = dya
        dyb_ref[...] = dyb
        dgp_ref[:, :D_MODEL] = (dmg * yav * ga * (1.0 - ga)).astype(BF16)
        dgp_ref[:, D_MODEL:] = (dmg * ybv * gb * (1.0 - gb)).astype(BF16)

        dat_t = _dot_nt(dya, wa_ref[...]).T.astype(BF16)
        dot_ref[0] = dat_t
        o_t = o_ref[...].astype(F32).T
        delta_ref[0] = jnp.sum((dat_t.astype(F32) * o_t).reshape(HEADS, HEAD_DIM, tm), axis=1)
        dsgu = _dot_nt(dyb, wb_ref[...])

        uvp = uv_ref[...].astype(F32)
        uv, guv = _gelu_and_grad(uvp)
        u, vv = uv[:, :SGU_W], uv[:, SGU_W:]
        gs_v = gs_ref[...]
        vn, xh, rln = _layernorm_fwd(vv, gs_v, bsg_ref[...])
        bias = bs_ref[...]
        if nw > 1:
            bias = jnp.concatenate([bias] * nw, axis=0)
        mixed = _sgu_mix(vn, ws_ref) + bias
        du = dsgu * mixed
        dmixed = dsgu * u

        lane = lax.broadcasted_iota(jnp.int32, (WINDOW, 128), 1)
        low = lane < HEAD_DIM
        dvn_wins = []
        for w in range(nw):
            rows = slice(w * WINDOW, (w + 1) * WINDOW)
            dbs_acc[...] += dmixed[rows, :]
            slabs = []
            for p in range(GROUPS // 2):
                cols = slice(p * 128, (p + 1) * 128)
                dm2 = dmixed[rows, cols]
                dlo = jnp.where(low, dm2, 0.0).astype(BF16)
                dhi = jnp.where(low, 0.0, dm2).astype(BF16)
                vn2 = vn[rows, cols].astype(BF16)
                dws_ref[2 * p] += _dot_nt(dlo, vn2)
                dws_ref[2 * p + 1] += _dot_nt(dhi, vn2)
                slabs.append(jnp.dot(wst_ref[2 * p], dlo, preferred_element_type=F32)
                             + jnp.dot(wst_ref[2 * p + 1], dhi, preferred_element_type=F32))
            dvn_wins.append(jnp.concatenate(slabs, axis=1))
        dvn = jnp.concatenate(dvn_wins, axis=0) if nw > 1 else dvn_wins[0]

        dgs_ref[...] += jnp.sum(dvn * xh, axis=0, keepdims=True)
        dbsg_ref[...] += jnp.sum(dvn, axis=0, keepdims=True)
        dxh = dvn * gs_v
        dvv = rln * (dxh - jnp.mean(dxh, axis=-1, keepdims=True)
                     - xh * jnp.mean(dxh * xh, axis=-1, keepdims=True))
        duv_ref[:, :SGU_W] = (du * guv[:, :SGU_W]).astype(BF16)
        duv_ref[:, SGU_W:] = (dvv * guv[:, SGU_W:]).astype(BF16)

        @pl.when(step == pl.num_programs(0) - 1)
        def _():
            for g in range(GROUPS):
                dws_ref[g] = dws_ref[g] * mask_ref[...]
            dbs_ref[...] = _split3_dot(dbs_acc[...], eg_ref[...])

    rows_out = [((s_len, D_MODEL), BF16, _row_spec(tm, D_MODEL)),
                ((s_len, D_MODEL), BF16, _row_spec(tm, D_MODEL)),
                ((s_len, D_MODEL), BF16, _row_spec(tm, D_MODEL)),
                ((s_len, 2 * D_MODEL), BF16, _row_spec(tm, 2 * D_MODEL)),
                ((nt, FOX_W, tm), BF16, _tile_spec(FOX_W, tm)),
                ((nt, HEADS, tm), F32, _tile_spec(HEADS, tm)),
                ((s_len, 2 * SGU_W), BF16, _row_spec(tm, 2 * SGU_W))]
    acc_out = [((GROUPS, WINDOW, WINDOW), F32), ((WINDOW, 128), F32), ((1, SGU_W), F32),
               ((1, SGU_W), F32), ((1, D_MODEL), F32)]
    return pl.pallas_call(
        body, name="mix_bwd", grid=(nt,),
        in_specs=[_row_spec(tm, D_MODEL), _row_spec(tm, D_MODEL), _row_spec(tm, D_MODEL),
                  _row_spec(tm, D_MODEL), _row_spec(tm, 2 * D_MODEL), _row_spec(tm, 2 * SGU_W),
                  _row_spec(tm, FOX_W), _const_spec(wout.shape), _const_spec(wa.shape),
                  _const_spec(wb.shape), _const_spec(wsm.shape), _const_spec(wsmt.shape),
                  _const_spec(bsf.shape), _const_spec((1, SGU_W)), _const_spec((1, SGU_W)),
                  _const_spec((1, D_MODEL)), _const_spec(wmask.shape), _const_spec(egrp.shape)],
        out_specs=[o[2] for o in rows_out] + [_const_spec(s) for s, _ in acc_out],
        out_shape=[jax.ShapeDtypeStruct(o[0], o[1]) for o in rows_out]
        + [jax.ShapeDtypeStruct(s, dt) for s, dt in acc_out],
        scratch_shapes=[pltpu.VMEM((WINDOW, SGU_W), F32)],
        compiler_params=_params(48, 1),
    )(dx1, om, ya, yb, gpre, uvpre, attn, wout, wa, wb, wsm, wsmt, bsf, gsgu, bsgu, gpost, wmask,
      egrp)


def _attn_bwd(qa, ka, kat, vs, dot_, lse, delta):
    s_len = qa.shape[0]
    t = ATTN_TILE
    nb = s_len // t

    def body(k_ref, kt_ref, vs_ref, q_ref, do_ref, lse_ref, dl_ref, gk_ref, dvt_ref, gqt_ref):
        j = pl.program_id(0)

        @pl.when(j == 0)
        def _():
            gqt_ref[...] = jnp.zeros_like(gqt_ref)

        gk_ref[...] = jnp.zeros_like(gk_ref)
        dvt_ref[...] = jnp.zeros_like(dvt_ref)

        def tile(i, masked):
            qrows = pl.ds(pl.multiple_of(i * t, t), t)
            if masked:
                keep = (lax.broadcasted_iota(jnp.int32, (t, t), 0)
                        <= lax.broadcasted_iota(jnp.int32, (t, t), 1))
            for hd in range(HEADS):
                sl = slice(hd * 128, (hd + 1) * 128)
                hr = slice(hd * HEAD_DIM, (hd + 1) * HEAD_DIM)
                q_slab = q_ref[qrows, sl]
                st = _dot_nt(k_ref[:, sl], q_slab)
                if masked:
                    st = jnp.where(keep, st, -jnp.inf)
                pt = jnp.exp2(st - lse_ref[i, hd:hd + 1, :])
                do_h = do_ref[i, hr, :]
                dpt = jnp.dot(vs_ref[:, hd * 128:hd * 128 + HEAD_DIM], do_h,
                              preferred_element_type=F32)
                dst = (pt * (dpt - dl_ref[i, hd:hd + 1, :])).astype(BF16)
                dvt_ref[0, hr, :] += _dot_nt(do_h, pt.astype(BF16))
                gk_ref[:, sl] += jnp.dot(dst, q_slab, preferred_element_type=F32)
                gqt_ref[i, hd * QT_ROWS:(hd + 1) * QT_ROWS, :] += jnp.dot(
                    kt_ref[0, hd * 128:hd * 128 + QT_ROWS, :], dst, preferred_element_type=F32)

        tile(j, True)

        def below_diagonal(i, carry):
            tile(i, False)
            return carry

        lax.fori_loop(j + 1, nb, below_diagonal, 0)

    return pl.pallas_call(
        body, name="attn_bwd", grid=(nb,),
        in_specs=[_row_spec(t, SLAB_W), _tile_spec(SLAB_W, t), _row_spec(t, SLAB_W),
                  _const_spec(qa.shape), _const_spec(dot_.shape), _const_spec(lse.shape),
                  _const_spec(delta.shape)],
        out_specs=[_row_spec(t, SLAB_W), _tile_spec(FOX_W, t),
                   _const_spec((nb, HEADS * QT_ROWS, t))],
        out_shape=[jax.ShapeDtypeStruct((s_len, SLAB_W), F32),
                   jax.ShapeDtypeStruct((nb, FOX_W, t), F32),
                   jax.ShapeDtypeStruct((nb, HEADS * QT_ROWS, t), F32)],
        compiler_params=_params(60, 1),
    )(ka, kat, vs, qa, dot_, lse, delta)


def _rev_cumsum(gk, gqt, triu, ecol):
    s_len = gk.shape[0]
    tm = TOKEN_TILE
    n = s_len // tm

    def body(gk_ref, gqt_ref, tri_ref, ec_ref, o_ref, carry):
        @pl.when(pl.program_id(0) == 0)
        def _():
            carry[...] = jnp.zeros_like(carry)
        col_sums = _split3_dot(gk_ref[...], ec_ref[...])
        rows = [gqt_ref[0, hd * QT_ROWS + HEAD_DIM:hd * QT_ROWS + HEAD_DIM + 1, :]
                for hd in range(HEADS)]
        row_sums = jnp.concatenate(rows + [jnp.zeros((128 - HEADS, tm), F32)], axis=0).T
        out = _tri_dot(tri_ref[...], row_sums - col_sums) + carry[...]
        o_ref[...] = out
        carry[...] = out[0:1, :]

    return pl.pallas_call(
        body, name="rev_cumsum", grid=(n,),
        in_specs=[pl.BlockSpec((tm, SLAB_W), lambda i: (n - 1 - i, 0)),
                  pl.BlockSpec((1, HEADS * QT_ROWS, tm), lambda i: (n - 1 - i, 0, 0)),
                  _const_spec((tm, tm)), _const_spec(ecol.shape)],
        out_specs=pl.BlockSpec((tm, 128), lambda i: (n - 1 - i, 0)),
        out_shape=jax.ShapeDtypeStruct((s_len, 128), F32),
        scratch_shapes=[pltpu.VMEM((1, 128), F32)],
        compiler_params=_params(32, 1),
    )(gk, gqt, triu, ecol)


def _heads_from_slabs(slabs):
    lane = lax.broadcasted_iota(jnp.int32, slabs[0].shape, 1)
    low = lane < HEAD_DIM
    pairs = [jnp.where(low, slabs[2 * p], pltpu.roll(slabs[2 * p + 1], HEAD_DIM, 1))
             for p in range(HEADS // 2)]
    return jnp.concatenate(pairs, axis=1)


def _proj_bwd(gqt, gk, dvt, dlogf, flog, qraw, kraw, duv, dgp, x, dx1, wcat, bdiag, gq, gk_gain, g1,
              efold):
    s_len = x.shape[0]
    tm = TOKEN_TILE

    def body(gqt_ref, gkk_ref, dvt_ref, dlf_ref, flog_ref, qr_ref, kr_ref, duv_ref, dgp_ref, x_ref,
             dx1_ref, w_ref, bd_ref, gq_ref, gk_ref, g1_ref, ef_ref,
             dx_ref, dproj_ref, dgq_ref, dgk_ref, dbf_ref, dg1_ref, gq_acc, gk_acc):
        step = pl.program_id(0)

        @pl.when(step == 0)
        def _():
            gq_acc[...] = jnp.zeros_like(gq_acc)
            gk_acc[...] = jnp.zeros_like(gk_acc)
            dbf_ref[...] = jnp.zeros_like(dbf_ref)
            dg1_ref[...] = jnp.zeros_like(dg1_ref)

        pad = jnp.zeros((128 - QT_ROWS, tm), F32)
        q_slabs = [jnp.concatenate([gqt_ref[0, hd * QT_ROWS:(hd + 1) * QT_ROWS, :], pad], axis=0).T
                   for hd in range(HEADS)]
        dqn = _heads_from_slabs(q_slabs)
        dkn = _heads_from_slabs([gkk_ref[:, hd * 128:(hd + 1) * 128] for hd in range(HEADS)])

        def head_bwd(raw_ref, dn, g_ref, acc):
            raw = raw_ref[...].astype(F32)
            r = lax.rsqrt(_seg_mean(raw * raw, bd_ref) + EPS)
            xhat = raw * r
            acc[0:1, :] += jnp.sum(dn * xhat, axis=0, keepdims=True)
            dyg = dn * g_ref[...]
            return r * (dyg - xhat * _seg_mean(dyg * xhat, bd_ref))

        dproj_ref[:, C_Q:C_K] = head_bwd(qr_ref, dqn * HEAD_DIM ** -0.5, gq_ref, gq_acc).astype(BF16)
        dproj_ref[:, C_K:C_V] = head_bwd(kr_ref, dkn * LN2, gk_ref, gk_acc).astype(BF16)
        dproj_ref[:, C_V:C_F] = dvt_ref[0].T.astype(BF16)
        dfl = dlf_ref[...] * _sigmoid(-flog_ref[...])
        dbf_ref[...] += jnp.sum(dfl, axis=0, keepdims=True)
        dproj_ref[:, C_F:C_UV] = dfl.astype(BF16)
        dproj_ref[:, C_UV:C_G] = duv_ref[...]
        dproj_ref[:, C_G:C_END] = dgp_ref[...]

        dh = _dot_nt(dproj_ref[...], w_ref[...])
        xf = x_ref[...]
        r = lax.rsqrt(jnp.mean(xf * xf, axis=-1, keepdims=True) + EPS)
        dg1_ref[...] += jnp.sum(dh * xf * r, axis=0, keepdims=True)
        dx_ref[...] = dx1_ref[...] + _rms_bwd(xf, r, g1_ref[...], dh)

        @pl.when(step == pl.num_programs(0) - 1)
        def _():
            dgq_ref[...] = _split3_dot(gq_acc[...], ef_ref[...])
            dgk_ref[...] = _split3_dot(gk_acc[...], ef_ref[...])

    outs = [((s_len, D_MODEL), F32, _row_spec(tm, D_MODEL)),
            ((s_len, C_END), BF16, _row_spec(tm, C_END)),
            ((8, 128), F32, _const_spec((8, 128))),
            ((8, 128), F32, _const_spec((8, 128))),
            ((1, 128), F32, _const_spec((1, 128))),
            ((1, D_MODEL), F32, _const_spec((1, D_MODEL)))]
    return pl.pallas_call(
        body, name="proj_bwd", grid=(s_len // tm,),
        in_specs=[_tile_spec(HEADS * QT_ROWS, tm), _row_spec(tm, SLAB_W), _tile_spec(FOX_W, tm),
                  _row_spec(tm, 128), _row_spec(tm, 128), _row_spec(tm, FOX_W),
                  _row_spec(tm, FOX_W), _row_spec(tm, 2 * SGU_W), _row_spec(tm, 2 * D_MODEL),
                  _row_spec(tm, D_MODEL), _row_spec(tm, D_MODEL), _const_spec(wcat.shape),
                  _const_spec(bdiag.shape), _const_spec((1, FOX_W)), _const_spec((1, FOX_W)),
                  _const_spec((1, D_MODEL)), _const_spec(efold.shape)],
        out_specs=[o[2] for o in outs],
        out_shape=[jax.ShapeDtypeStruct(o[0], o[1]) for o in outs],
        scratch_shapes=[pltpu.VMEM((8, FOX_W), F32), pltpu.VMEM((8, FOX_W), F32)],
        compiler_params=_params(56, 1),
    )(gqt, gk, dvt, dlogf, flog, qraw, kraw, duv, dgp, x, dx1, wcat, bdiag, gq, gk_gain, g1, efold)


def _dw_matmul(a, b, tm, name):
    s_len, m = a.shape
    n = b.shape[1]
    tk = min(512, s_len)
    nk = s_len // tk

    def body(a_ref, b_ref, o_ref, acc):
        kk = pl.program_id(1)

        @pl.when(kk == 0)
        def _():
            acc[...] = jnp.zeros_like(acc)
        acc[...] += _dot_tn(a_ref[...], b_ref[...])

        @pl.when(kk == nk - 1)
        def _():
            o_ref[...] = acc[...].astype(BF16)

    return pl.pallas_call(
        body, name=name, grid=(m // tm, nk),
        in_specs=[pl.BlockSpec((tk, tm), lambda i, k: (k, i)),
                  pl.BlockSpec((tk, n), lambda i, k: (k, 0))],
        out_specs=pl.BlockSpec((tm, n), lambda i, k: (i, 0)),
        out_shape=jax.ShapeDtypeStruct((m, n), BF16),
        scratch_shapes=[pltpu.VMEM((tm, n), F32)],
        compiler_params=_params(56, 2),
    )(a, b)


def _adamw(parts, w, m, v, tr, name):
    n, rows, cols = parts.shape
    bc1 = 1.0 - ADAM_B1 ** ADAM_STEP
    bc2 = 1.0 - ADAM_B2 ** ADAM_STEP

    def body(p_ref, w_ref, m_ref, v_ref, g_ref, d_ref, mo_ref, vo_ref):
        g = p_ref[0].astype(F32)
        for idx in range(1, n):
            g = g + p_ref[idx].astype(F32)
        g_ref[...] = g
        mn = ADAM_B1 * m_ref[...] + (1.0 - ADAM_B1) * g
        vn = ADAM_B2 * v_ref[...] + (1.0 - ADAM_B2) * (g * g)
        mo_ref[...] = mn
        vo_ref[...] = vn
        m_hat = mn / bc1
        v_hat = vn / bc2
        d_ref[...] = -ADAM_LR * (m_hat / (jnp.sqrt(v_hat) + ADAM_EPS) + ADAM_WD * w_ref[...])

    spec = pl.BlockSpec((tr, cols), lambda i: (i, 0))
    return pl.pallas_call(
        body, name=name, grid=(rows // tr,),
        in_specs=[pl.BlockSpec((n, tr, cols), lambda i: (0, i, 0)), spec, spec, spec],
        out_specs=[spec] * 4,
        out_shape=[jax.ShapeDtypeStruct((rows, cols), F32)] * 4,
        compiler_params=_params(48, 1),
    )(parts, w, m, v)


def _sum_parts(parts, name):
    n, rows, cols = parts.shape

    def body(p_ref, o_ref):
        g = p_ref[0]
        for idx in range(1, n):
            g = g + p_ref[idx]
        o_ref[...] = g

    return pl.pallas_call(
        body, name=name, out_shape=jax.ShapeDtypeStruct((rows, cols), F32),
        in_specs=[_const_spec(parts.shape)], out_specs=_const_spec((rows, cols)), grid=(1,),
        compiler_params=_params(16, 1),
    )(parts)


SMALL_NAMES = ("g_pre_mix", "b_forget", "g_q", "g_k", "g_sgu", "b_sgu", "w_spatial", "b_spatial",
               "g_post_mix", "g_pre_ffn", "g_post_ffn")
SMALL_TOTAL = N_DEV * SMALL_ROWS * 1024


def _pack_small(d):
    flat = jnp.concatenate([d[k].reshape(-1).astype(F32) for k in SMALL_NAMES])
    flat = jnp.pad(flat, (0, SMALL_TOTAL - flat.shape[0]))
    return flat.reshape(N_DEV * SMALL_ROWS, 1024)


def _unpack_small(packed, shapes):
    flat = packed.reshape(-1)
    out, off = {}, 0
    for k in SMALL_NAMES:
        size = math.prod(shapes[k])
        out[k] = flat[off:off + size].reshape(shapes[k])
        off += size
    return out


def _cols_to_blocks(full, width):
    r = full.shape[0]
    return jnp.transpose(full.reshape(r, N_DEV, width), (1, 0, 2))


def _blocks_to_cols(blocks):
    n, r, width = blocks.shape
    return jnp.transpose(blocks, (1, 0, 2)).reshape(r, n * width)


def kernel(x, g_pre_mix, w_in, b_forget, g_q, g_k, g_sgu, b_sgu, w_spatial, b_spatial, w_branch_a, w_branch_b, w_out, g_post_mix, g_pre_ffn, w_ffn_in, w_ffn_down, g_post_ffn, loss_target, m_g_pre_mix, m_w_in, m_b_forget, m_g_q, m_g_k, m_g_sgu, m_b_sgu, m_w_spatial, m_b_spatial, m_w_branch_a, m_w_branch_b, m_w_out, m_g_post_mix, m_g_pre_ffn, m_w_ffn_in, m_w_ffn_down, m_g_post_ffn, v_g_pre_mix, v_w_in, v_b_forget, v_g_q, v_g_k, v_g_sgu, v_b_sgu, v_w_spatial, v_b_spatial, v_w_branch_a, v_w_branch_b, v_w_out, v_g_post_mix, v_g_pre_ffn, v_w_ffn_in, v_w_ffn_down, v_g_post_ffn):
    big_names = ("w_in", "w_branch_a", "w_branch_b", "w_out", "w_ffn_in", "w_ffn_down")
    weights = dict(g_pre_mix=g_pre_mix, w_in=w_in, b_forget=b_forget, g_q=g_q, g_k=g_k, g_sgu=g_sgu,
                   b_sgu=b_sgu, w_spatial=w_spatial, b_spatial=b_spatial, w_branch_a=w_branch_a,
                   w_branch_b=w_branch_b, w_out=w_out, g_post_mix=g_post_mix, g_pre_ffn=g_pre_ffn,
                   w_ffn_in=w_ffn_in, w_ffn_down=w_ffn_down, g_post_ffn=g_post_ffn)
    mom1 = dict(g_pre_mix=m_g_pre_mix, w_in=m_w_in, b_forget=m_b_forget, g_q=m_g_q, g_k=m_g_k,
                g_sgu=m_g_sgu, b_sgu=m_b_sgu, w_spatial=m_w_spatial, b_spatial=m_b_spatial,
                w_branch_a=m_w_branch_a, w_branch_b=m_w_branch_b, w_out=m_w_out,
                g_post_mix=m_g_post_mix, g_pre_ffn=m_g_pre_ffn, w_ffn_in=m_w_ffn_in,
                w_ffn_down=m_w_ffn_down, g_post_ffn=m_g_post_ffn)
    mom2 = dict(g_pre_mix=v_g_pre_mix, w_in=v_w_in, b_forget=v_b_forget, g_q=v_g_q, g_k=v_g_k,
                g_sgu=v_g_sgu, b_sgu=v_b_sgu, w_spatial=v_w_spatial, b_spatial=v_b_spatial,
                w_branch_a=v_w_branch_a, w_branch_b=v_w_branch_b, w_out=v_w_out,
                g_post_mix=v_g_post_mix, g_pre_ffn=v_g_pre_ffn, w_ffn_in=v_w_ffn_in,
                w_ffn_down=v_w_ffn_down, g_post_ffn=v_g_post_ffn)
    names = list(weights)
    shapes = {k: weights[k].shape for k in names}

    s_len = x.shape[1]
    xs = x.reshape(s_len, D_MODEL)
    tgt = loss_target.reshape(s_len, D_MODEL)

    shards = [weights[k][0].astype(BF16) for k in big_names]
    gathered = _exchange(shards, "gather_weights", gather=True)
    win_full = _blocks_to_cols(gathered[0])
    wa = _blocks_to_cols(gathered[1])
    wb = _blocks_to_cols(gathered[2])
    wout = gathered[3].reshape(D_MODEL, D_MODEL)
    wffn = _blocks_to_cols(gathered[4])
    wdown = gathered[5].reshape(D_FF, D_MODEL)
    f_off = 3 * FOX_W
    u_off = f_off + HEADS
    g_off = u_off + 2 * SGU_W
    wcat = jnp.concatenate([
        win_full[:, :f_off],
        jnp.pad(win_full[:, f_off:u_off], ((0, 0), (0, 128 - HEADS))),
        win_full[:, u_off:g_off], win_full[:, g_off:]], axis=1)

    seg = jnp.arange(FOX_W) // HEAD_DIM
    bdiag = (seg[:, None] == seg[None, :]).astype(BF16)
    tm = TOKEN_TILE
    tril = (jnp.arange(tm)[None, :] <= jnp.arange(tm)[:, None]).astype(BF16)
    triu = tril.T
    egrp = (seg[:, None] == jnp.arange(128)[None, :]).astype(BF16)
    efold = ((jnp.arange(FOX_W) % HEAD_DIM)[:, None] == jnp.arange(128)[None, :]).astype(BF16)
    gq512 = jnp.tile(g_q.reshape(1, HEAD_DIM), (1, HEADS))
    gk512 = jnp.tile(g_k.reshape(1, HEAD_DIM), (1, HEADS))
    bfor = jnp.pad(b_forget.reshape(1, HEADS), ((0, 0), (0, 128 - HEADS)))
    pos = jnp.arange(WINDOW)
    wmask = ((pos[None, :] // CHUNK) <= (pos[:, None] // CHUNK))
    wsm_f = jnp.where(wmask[None], w_spatial[0], 0.0)
    wsm = wsm_f.astype(BF16)
    wsmt = jnp.transpose(wsm_f, (0, 2, 1)).astype(BF16)
    bsf = jnp.repeat(jnp.transpose(b_spatial[0]), HEAD_DIM, axis=1)
    wmask_f = wmask.astype(F32)

    col = jnp.arange(SLAB_W)
    place = ((col[None, :] // 128 == seg[:, None])
             & (col[None, :] % 128 == (jnp.arange(FOX_W) % HEAD_DIM)[:, None])).astype(BF16)
    row128 = jnp.arange(128)

    def d_place(first):
        return jnp.stack([((col[None, :] // 128 == row128[:, None])
                           & (col[None, :] % 128 == first + a)).astype(BF16) for a in range(3)])

    pdq, pdk = d_place(HEAD_DIM), d_place(HEAD_DIM + 3)
    ones_q = ((col % 128 >= HEAD_DIM + 3) & (col % 128 < HEAD_DIM + 6)).astype(F32)[None]
    ones_k = ((col % 128 >= HEAD_DIM) & (col % 128 < HEAD_DIM + 3)).astype(F32)[None]
    ecol = ((col[:, None] // 128 == row128[None, :])
            & (col[:, None] % 128 == HEAD_DIM + 3)).astype(BF16)

    (h, qa, ka, kat, vs, vt, qraw, kraw, flog, uvpre, gpre) = _proj_fwd(
        xs, g_pre_mix, wcat, bdiag, gq512, gk512, bfor, tril, place, pdq, pdk, ones_q, ones_k)
    attn, lse = _attn_fwd(qa, ka, vt)
    sgu, ya, yb, merged, om, x1 = _mix_fwd(attn, uvpre, gpre, xs, wa, wb, wout, wsm, bsf,
                                           g_sgu, b_sgu, g_post_mix)
    (dx1, h2, act, dff, dgu, loss_acc, dg_post_ffn, dg_pre_ffn) = _ffn_fwd_bwd(
        x1, tgt, wffn, wdown, g_pre_ffn, g_post_ffn)

    (dom, dya, dyb, dgp, dot_, delta, duv, dws, dbs, dg_sgu, db_sgu, dg_post_mix) = _mix_bwd(
        dx1, om, ya, yb, gpre, uvpre, attn, wout, wa, wb, wsm, wsmt, bsf, g_sgu, b_sgu,
        g_post_mix, wmask_f, egrp)
    gk_all, dvt, gqt = _attn_bwd(qa, ka, kat, vs, dot_, lse, delta)
    dlogf = _rev_cumsum(gk_all, gqt, triu, ecol)
    dx, dproj, dgq, dgk, dbf, dg_pre_mix = _proj_bwd(
        gqt, gk_all, dvt, dlogf, flog, qraw, kraw, duv, dgp, xs, dx1, wcat, bdiag, gq512, gk512,
        g_pre_mix, efold)

    dw_down = _dw_matmul(act, dff, D_FF // 2, "dw_down")
    dw_ffn = _dw_matmul(h2, dgu, 512, "dw_ffn_in")
    dw_out = _dw_matmul(merged, dom, 512, "dw_out")
    dw_a = _dw_matmul(attn, dya, 512, "dw_a")
    dw_b = _dw_matmul(sgu, dyb, 512, "dw_b")
    dw_cat = _dw_matmul(h, dproj, 512, "dw_in")
    dw_in = jnp.concatenate([dw_cat[:, :C_F], dw_cat[:, C_F:C_F + HEADS], dw_cat[:, C_UV:]], axis=1)

    small_local = dict(
        g_pre_mix=dg_pre_mix, b_forget=dbf[:, :HEADS], g_q=dgq[0:1, :HEAD_DIM],
        g_k=dgk[0:1, :HEAD_DIM], g_sgu=dg_sgu, b_sgu=db_sgu, w_spatial=dws,
        b_spatial=jnp.transpose(dbs[:, :GROUPS]), g_post_mix=dg_post_mix, g_pre_ffn=dg_pre_ffn,
        g_post_ffn=dg_post_ffn)
    small_parts = _pack_small(small_local).reshape(N_DEV, SMALL_ROWS, 1024)

    big_parts = [
        _cols_to_blocks(dw_in, IN_COLS // N_DEV),
        _cols_to_blocks(dw_a, D_MODEL // N_DEV),
        _cols_to_blocks(dw_b, D_MODEL // N_DEV),
        dw_out.reshape(N_DEV, D_MODEL // N_DEV, D_MODEL),
        _cols_to_blocks(dw_ffn, 2 * D_FF // N_DEV),
        dw_down.reshape(N_DEV, D_FF // N_DEV, D_MODEL),
    ]
    received = _exchange(big_parts + [small_parts], "scatter_grads", gather=False)

    grads, deltas, new_m, new_v = {}, {}, {}, {}
    row_tiles = {"w_in": 128, "w_branch_a": 512, "w_branch_b": 512, "w_out": 128, "w_ffn_in": 128,
                 "w_ffn_down": 352}
    for idx, k in enumerate(big_names):
        g, d, mn, vn = _adamw(received[idx], weights[k][0], mom1[k][0], mom2[k][0], row_tiles[k],
                              "adamw_" + k)
        grads[k], deltas[k], new_m[k], new_v[k] = g[None], d[None], mn[None], vn[None]

    small_sum = _sum_parts(received[-1], "sum_small")
    (small_all,) = _exchange([small_sum], "gather_small", gather=True)
    small_all = small_all.reshape(1, N_DEV * SMALL_ROWS, 1024)
    sg, sd, sm, sv = _adamw(small_all, _pack_small(weights), _pack_small(mom1), _pack_small(mom2),
                            N_DEV * SMALL_ROWS, "adamw_small")
    for dst, packed in ((grads, sg), (deltas, sd), (new_m, sm), (new_v, sv)):
        dst.update(_unpack_small(packed, shapes))

    loss = lax.psum(loss_acc[0, 0], ("x", "y", "c"))
    return (loss, dx.reshape(x.shape), *[grads[k] for k in names], *[deltas[k] for k in names],
            *[new_m[k] for k in names], *[new_v[k] for k in names])
```

```python
import functools
import math

import jax
import jax.numpy as jnp
from jax import lax
from jax.experimental import pallas as pl
from jax.experimental.pallas import tpu as pltpu

F32 = jnp.float32
BF16 = jnp.bfloat16

D_MODEL = 1024
FOX_W = 512
HEADS = 8
HEAD_DIM = 64
SGU_W = 512
GROUPS = 8
WINDOW = 128
CHUNK = 64
D_FF = 2816
IN_COLS = 4616
EPS = 1e-6
N_DEV = 8
LOG2E = 1.4426950408889634
LN2 = 0.6931471805599453

C_Q, C_K, C_V, C_F, C_UV, C_G, C_END = 0, 512, 1024, 1536, 1664, 2688, 4736

ADAM_LR, ADAM_B1, ADAM_B2, ADAM_EPS, ADAM_WD, ADAM_STEP = 0.001, 0.9, 0.999, 1e-08, 0.01, 10

MIB = 1024 * 1024
TOKEN_TILE = 256
ATTN_TILE = 256
SLAB_W = HEADS * 128
QT_ROWS = 72

SMALL_ROWS = 24


def _params(vmem_mib, n_axes):
    return pltpu.CompilerParams(
        dimension_semantics=("arbitrary",) * n_axes, vmem_limit_bytes=vmem_mib * MIB)


def _const_spec(shape):
    nd = len(shape)
    return pl.BlockSpec(shape, lambda *_: (0,) * nd)


def _row_spec(tm, cols):
    return pl.BlockSpec((tm, cols), lambda i: (i, 0))


def _tile_spec(rows, tm):
    return pl.BlockSpec((1, rows, tm), lambda i: (i, 0, 0))


def _split3_dot(x, e):
    x1 = x.astype(BF16)
    r1 = x - x1.astype(F32)
    x2 = r1.astype(BF16)
    x3 = (r1 - x2.astype(F32)).astype(BF16)
    dot = functools.partial(jnp.dot, preferred_element_type=F32)
    return dot(x1, e) + dot(x2, e) + dot(x3, e)


def _tri_dot(tri, x):
    x1 = x.astype(BF16)
    r1 = x - x1.astype(F32)
    x2 = r1.astype(BF16)
    x3 = (r1 - x2.astype(F32)).astype(BF16)
    dot = functools.partial(jnp.dot, preferred_element_type=F32)
    return dot(tri, x1) + dot(tri, x2) + dot(tri, x3)


def _seg_mean(sq, bd_ref):
    hi = sq.astype(BF16)
    lo = (sq - hi.astype(F32)).astype(BF16)
    bd = bd_ref[...]
    s = jnp.dot(hi, bd, preferred_element_type=F32) + jnp.dot(lo, bd, preferred_element_type=F32)
    return s * (1.0 / HEAD_DIM)


def _dot_nt(a, b):
    return lax.dot_general(a, b, (((1,), (1,)), ((), ())), preferred_element_type=F32)


def _dot_tn(a, b):
    return lax.dot_general(a, b, (((0,), (0,)), ((), ())), preferred_element_type=F32)


def _sigmoid(x):
    return 1.0 / (1.0 + jnp.exp(-x))


_GELU_C = math.sqrt(2.0 / math.pi)


def _gelu_and_grad(x):
    inner = _GELU_C * (x + 0.044715 * x * x * x)
    t = jnp.tanh(inner)
    y = 0.5 * x * (1.0 + t)
    dy = 0.5 * (1.0 + t) + 0.5 * x * (1.0 - t * t) * _GELU_C * (1.0 + 3.0 * 0.044715 * x * x)
    return y, dy


def _rms_bwd(xin, r, g, dy):
    dyg = dy * g
    return r * dyg - xin * (r * r * r) * jnp.mean(dyg * xin, axis=-1, keepdims=True)


def _mesh_pos():
    x, y, c = lax.axis_index("x"), lax.axis_index("y"), lax.axis_index("c")
    return x, y, c


def _peer(k):
    x, y, c = _mesh_pos()
    px = (1 - x) if (k >> 2) & 1 else x
    py = (1 - y) if (k >> 1) & 1 else y
    pc = (1 - c) if k & 1 else c
    return (px, py, pc), 4 * px + 2 * py + pc


def _exchange(arrs, name, gather):
    n = len(arrs)
    if gather:
        out_shape = [jax.ShapeDtypeStruct((N_DEV,) + a.shape, a.dtype) for a in arrs]
    else:
        out_shape = [jax.ShapeDtypeStruct(a.shape, a.dtype) for a in arrs]

    def body(*refs):
        ins, outs = refs[:n], refs[n:2 * n]
        send_sems, recv_sems, local_sems = refs[2 * n:]
        x, y, c = _mesh_pos()
        me = 4 * x + 2 * y + c

        def src(a, idx):
            return ins[a] if gather else ins[a].at[idx]

        local = []
        for a in range(n):
            cp = pltpu.make_async_copy(src(a, me), outs[a].at[me], local_sems.at[a])
            cp.start()
            local.append(cp)
        sends = []
        for k in range(1, N_DEV):
            peer, pidx = _peer(k)
            for a in range(n):
                cp = pltpu.make_async_remote_copy(
                    src_ref=src(a, pidx), dst_ref=outs[a].at[me],
                    send_sem=send_sems.at[a, k - 1], recv_sem=recv_sems.at[a, k - 1],
                    device_id=peer, device_id_type=pl.DeviceIdType.MESH)
                cp.start()
                sends.append(cp)
        for k in range(1, N_DEV):
            peer, pidx = _peer(k)
            for a in range(n):
                pltpu.make_async_remote_copy(
                    src_ref=src(a, pidx), dst_ref=outs[a].at[pidx],
                    send_sem=send_sems.at[a, k - 1], recv_sem=recv_sems.at[a, k - 1],
                    device_id=peer, device_id_type=pl.DeviceIdType.MESH).wait_recv()
        for cp in sends:
            cp.wait_send()
        for cp in local:
            cp.wait()

    any_spec = pl.BlockSpec(memory_space=pl.ANY)
    return pl.pallas_call(
        body, name=name, out_shape=out_shape,
        in_specs=[any_spec] * n, out_specs=[any_spec] * n,
        scratch_shapes=[pltpu.SemaphoreType.DMA((n, N_DEV - 1)),
                        pltpu.SemaphoreType.DMA((n, N_DEV - 1)),
                        pltpu.SemaphoreType.DMA((n,))],
    )(*arrs)


def _remote_copy(gather, src_ref, land_ref, send_sem, recv_sem, k, receive_side):
    x, y, c = _mesh_pos()
    me = 4 * x + 2 * y + c
    peer, pidx = _peer(k)
    return pltpu.make_async_remote_copy(
        src_ref=src_ref if gather else src_ref.at[pidx],
        dst_ref=land_ref.at[pidx if receive_side else me],
        send_sem=send_sem, recv_sem=recv_sem,
        device_id=peer, device_id_type=pl.DeviceIdType.MESH)


def _exchange_start(groups, name, gather):
    arrs = [a for g in groups for a in g]
    n, n_groups = len(arrs), len(groups)
    lands = [jax.ShapeDtypeStruct(((N_DEV,) + a.shape) if gather else a.shape, a.dtype)
             for a in arrs]

    def body(*refs):
        srcs, zones = refs[:n], refs[n:2 * n]
        sems = refs[2 * n:2 * n + 2 * n_groups]
        token = refs[-1]
        a = 0
        for gi, g in enumerate(groups):
            send_sems, recv_sems = sems[2 * gi], sems[2 * gi + 1]
            for k in range(1, N_DEV):
                for ai in range(len(g)):
                    slot = ai * (N_DEV - 1) + k - 1
                    _remote_copy(gather, srcs[a + ai], zones[a + ai], send_sems.at[slot],
                                 recv_sems.at[slot], k, False).start()
            a += len(g)
        token[...] = jnp.zeros_like(token)

    hbm = pl.BlockSpec(memory_space=pltpu.HBM)
    sem = pl.BlockSpec(memory_space=pltpu.SEMAPHORE)
    sem_shapes = []
    for g in groups:
        sem_shapes += [pltpu.SemaphoreType.DMA((len(g) * (N_DEV - 1),))] * 2
    outs = pl.pallas_call(
        body, name=name,
        in_specs=[hbm] * (2 * n),
        out_shape=sem_shapes + [pltpu.HBM(a.shape, a.dtype) for a in arrs]
        + [pltpu.HBM(z.shape, z.dtype) for z in lands] + [jax.ShapeDtypeStruct((8, 128), F32)],
        out_specs=[sem] * (2 * n_groups) + [hbm] * (2 * n)
        + [pl.BlockSpec(memory_space=pltpu.VMEM)],
        input_output_aliases={i: 2 * n_groups + i for i in range(2 * n)},
        compiler_params=pltpu.CompilerParams(
            has_side_effects=pltpu.SideEffectType.DATAFLOW_SIDE_EFFECTING),
    )(*[pltpu.with_memory_space_constraint(a, pltpu.HBM) for a in arrs],
      *[pltpu.with_memory_space_constraint(lax.empty(z.shape, z.dtype), pltpu.HBM) for z in lands])
    sems = outs[:2 * n_groups]
    thru = outs[2 * n_groups:2 * n_groups + n]
    zones = outs[2 * n_groups + n:2 * n_groups + 2 * n]
    handles, a = [], 0
    for gi, g in enumerate(groups):
        handles.append((sems[2 * gi], sems[2 * gi + 1], thru[a:a + len(g)], zones[a:a + len(g)]))
        a += len(g)
    return handles, outs[-1]


def _exchange_wait(handle, after, name, gather):
    send_sems, recv_sems, thru, zones = handle
    n = len(thru)

    def body(*refs):
        srcs, lands = refs[:n], refs[n:2 * n]
        ssem, rsem = refs[2 * n], refs[2 * n + 1]
        for k in range(1, N_DEV):
            for ai in range(n):
                slot = ai * (N_DEV - 1) + k - 1
                cp = _remote_copy(gather, srcs[ai], lands[ai], ssem.at[slot], rsem.at[slot], k, True)
                cp.wait_send()
                cp.wait_recv()

    hbm = pl.BlockSpec(memory_space=pltpu.HBM)
    sem = pl.BlockSpec(memory_space=pltpu.SEMAPHORE)
    outs = pl.pallas_call(
        body, name=name,
        in_specs=[hbm] * (2 * n) + [sem, sem, pl.BlockSpec(memory_space=pl.ANY)],
        out_shape=[pltpu.HBM(a.shape, a.dtype) for a in thru]
        + [pltpu.HBM(z.shape, z.dtype) for z in zones],
        out_specs=[hbm] * (2 * n),
        input_output_aliases={i: i for i in range(2 * n)},
        compiler_params=pltpu.CompilerParams(
            has_side_effects=pltpu.SideEffectType.DATAFLOW_SIDE_EFFECTING),
    )(*thru, *zones, send_sems, recv_sems, after)
    return outs[n:]


def _own_block(zone, block):
    x, y, c = _mesh_pos()
    me = 4 * x + 2 * y + c
    return lax.dynamic_update_slice_in_dim(zone, block[None], me, axis=0)


def _proj_fwd(x, g1, wcat, bdiag, gq, gk, bfor, tri, place, pdq, pdk, ones_q, ones_k):
    s_len = x.shape[0]
    tm = TOKEN_TILE
    nt = s_len // tm

    def body(x_ref, g1_ref, w_ref, bd_ref, gq_ref, gk_ref, bf_ref, tri_ref, pl_ref, pdq_ref,
             pdk_ref, oq_ref, ok_ref,
             h_ref, qa_ref, ka_ref, kat_ref, vs_ref, vt_ref, qr_ref, kr_ref, flog_ref, uv_ref,
             gp_ref, carry):
        @pl.when(pl.program_id(0) == 0)
        def _():
            carry[...] = jnp.zeros_like(carry)

        xf = x_ref[...]
        r = lax.rsqrt(jnp.mean(xf * xf, axis=-1, keepdims=True) + EPS)
        h = (xf * r * g1_ref[...]).astype(BF16)
        h_ref[...] = h
        dot = functools.partial(jnp.dot, preferred_element_type=F32)

        def proj(lo, hi):
            return dot(h, w_ref[:, lo:hi])

        flog = proj(C_F, C_UV) + bf_ref[...]
        flog_ref[...] = flog
        lane = lax.broadcasted_iota(jnp.int32, flog.shape, 1)
        logf = jnp.minimum(flog, 0.0) - jnp.log(1.0 + jnp.exp(-jnp.abs(flog)))
        logf = jnp.where(lane < HEADS, logf, 0.0)
        dcum = _tri_dot(tri_ref[...], logf) + carry[...]
        carry[...] = dcum[tm - 1:tm, :]
        d2 = dcum * LOG2E
        d2a = d2.astype(BF16)
        rem = d2 - d2a.astype(F32)
        d2b = rem.astype(BF16)
        d2c = (rem - d2b.astype(F32)).astype(BF16)

        place_m = pl_ref[...]
        q = proj(C_Q, C_K)
        qr_ref[...] = q.astype(BF16)
        rq = lax.rsqrt(_seg_mean(q * q, bd_ref) + EPS)
        qn = (q * rq * (gq_ref[...] * (HEAD_DIM ** -0.5 * LOG2E))).astype(BF16)
        qa = (dot(qn, place_m) + dot(d2a, pdq_ref[0]) + dot(d2b, pdq_ref[1])
              + dot(d2c, pdq_ref[2]) + oq_ref[...])
        qa_ref[...] = qa.astype(BF16)

        k = proj(C_K, C_V)
        kr_ref[...] = k.astype(BF16)
        rk = lax.rsqrt(_seg_mean(k * k, bd_ref) + EPS)
        kn = (k * rk * gk_ref[...]).astype(BF16)
        ka = (dot(kn, place_m) - dot(d2a, pdk_ref[0]) - dot(d2b, pdk_ref[1])
              - dot(d2c, pdk_ref[2]) + ok_ref[...])
        ka_ref[...] = ka.astype(BF16)
        kat_ref[0] = ka.T.astype(BF16)

        v = proj(C_V, C_F)
        vs_ref[...] = dot(v.astype(BF16), place_m).astype(BF16)
        vt_ref[0] = v.T.astype(BF16)
        uv_ref[...] = proj(C_UV, C_G).astype(BF16)
        gp_ref[...] = proj(C_G, C_END).astype(BF16)

    outs = [((s_len, D_MODEL), BF16, _row_spec(tm, D_MODEL)),
            ((s_len, SLAB_W), BF16, _row_spec(tm, SLAB_W)),
            ((s_len, SLAB_W), BF16, _row_spec(tm, SLAB_W)),
            ((nt, SLAB_W, tm), BF16, _tile_spec(SLAB_W, tm)),
            ((s_len, SLAB_W), BF16, _row_spec(tm, SLAB_W)),
            ((nt, FOX_W, tm), BF16, _tile_spec(FOX_W, tm)),
            ((s_len, FOX_W), BF16, _row_spec(tm, FOX_W)),
            ((s_len, FOX_W), BF16, _row_spec(tm, FOX_W)),
            ((s_len, 128), F32, _row_spec(tm, 128)),
            ((s_len, 2 * SGU_W), BF16, _row_spec(tm, 2 * SGU_W)),
            ((s_len, 2 * D_MODEL), BF16, _row_spec(tm, 2 * D_MODEL))]
    return pl.pallas_call(
        body, name="proj_fwd", grid=(nt,),
        in_specs=[_row_spec(tm, D_MODEL), _const_spec((1, D_MODEL)), _const_spec(wcat.shape),
                  _const_spec(bdiag.shape), _const_spec((1, FOX_W)), _const_spec((1, FOX_W)),
                  _const_spec((1, 128)), _const_spec((tm, tm)), _const_spec(place.shape),
                  _const_spec(pdq.shape), _const_spec(pdk.shape), _const_spec(ones_q.shape),
                  _const_spec(ones_k.shape)],
        out_specs=[o[2] for o in outs],
        out_shape=[jax.ShapeDtypeStruct(o[0], o[1]) for o in outs],
        scratch_shapes=[pltpu.VMEM((1, 128), F32)],
        compiler_params=_params(56, 1),
    )(x, g1, wcat, bdiag, gq, gk, bfor, tri, place, pdq, pdk, ones_q, ones_k)


def _attn_fwd(qa, ka, vt):
    s_len = qa.shape[0]
    t = ATTN_TILE
    nb = s_len // t

    def body(q_ref, k_ref, vt_ref, o_ref, lse_ref, m_sc, l_sc, acc_sc):
        i = pl.program_id(0)
        m_sc[...] = jnp.full_like(m_sc, -jnp.inf)
        l_sc[...] = jnp.zeros_like(l_sc)
        acc_sc[...] = jnp.zeros_like(acc_sc)

        def tile(j, masked):
            krows = pl.ds(pl.multiple_of(j * t, t), t)
            if masked:
                keep = (lax.broadcasted_iota(jnp.int32, (t, t), 0)
                        <= lax.broadcasted_iota(jnp.int32, (t, t), 1))
            for hd in range(HEADS):
                sl = slice(hd * 128, (hd + 1) * 128)
                hr = slice(hd * HEAD_DIM, (hd + 1) * HEAD_DIM)
                st = _dot_nt(k_ref[krows, sl], q_ref[:, sl])
                if masked:
                    st = jnp.where(keep, st, -jnp.inf)
                m_prev = m_sc[hd:hd + 1, :]
                m_new = jnp.maximum(m_prev, jnp.max(st, axis=0, keepdims=True))
                alpha = jnp.exp2(m_prev - m_new)
                pt = jnp.exp2(st - m_new)
                l_sc[hd:hd + 1, :] = alpha * l_sc[hd:hd + 1, :] + jnp.sum(pt, axis=0, keepdims=True)
                acc_sc[hr, :] = alpha * acc_sc[hr, :] + jnp.dot(
                    vt_ref[j, hr, :], pt.astype(BF16), preferred_element_type=F32)
                m_sc[hd:hd + 1, :] = m_new

        def off_diagonal(j, carry):
            tile(j, False)
            return carry

        lax.fori_loop(0, i, off_diagonal, 0)
        tile(i, True)

        for hd in range(HEADS):
            hr = slice(hd * HEAD_DIM, (hd + 1) * HEAD_DIM)
            l = l_sc[hd:hd + 1, :]
            acc_sc[hr, :] = acc_sc[hr, :] / l
            lse_ref[0, hd:hd + 1, :] = m_sc[hd:hd + 1, :] + jnp.log2(l)
        o_ref[...] = acc_sc[...].T.astype(BF16)

    return pl.pallas_call(
        body, name="attn_fwd", grid=(nb,),
        in_specs=[_row_spec(t, SLAB_W), _const_spec(ka.shape), _const_spec(vt.shape)],
        out_specs=[_row_spec(t, FOX_W), _tile_spec(HEADS, t)],
        out_shape=[jax.ShapeDtypeStruct((s_len, FOX_W), BF16),
                   jax.ShapeDtypeStruct((nb, HEADS, t), F32)],
        scratch_shapes=[pltpu.VMEM((HEADS, t), F32), pltpu.VMEM((HEADS, t), F32),
                        pltpu.VMEM((FOX_W, t), F32)],
        compiler_params=_params(48, 1),
    )(qa, ka, vt)


def _sgu_mix(vn, ws_ref):
    tm = vn.shape[0]
    lane = lax.broadcasted_iota(jnp.int32, (WINDOW, 128), 1)
    low = lane < HEAD_DIM
    wins = []
    for w in range(tm // WINDOW):
        slabs = []
        for p in range(GROUPS // 2):
            v2 = vn[w * WINDOW:(w + 1) * WINDOW, p * 128:(p + 1) * 128]
            lo = jnp.where(low, v2, 0.0).astype(BF16)
            hi = jnp.where(low, 0.0, v2).astype(BF16)
            slabs.append(jnp.dot(ws_ref[2 * p], lo, preferred_element_type=F32)
                         + jnp.dot(ws_ref[2 * p + 1], hi, preferred_element_type=F32))
        wins.append(jnp.concatenate(slabs, axis=1))
    return jnp.concatenate(wins, axis=0) if len(wins) > 1 else wins[0]


def _layernorm_fwd(vv, g, b):
    mu = jnp.mean(vv, axis=-1, keepdims=True)
    xc = vv - mu
    r = lax.rsqrt(jnp.mean(xc * xc, axis=-1, keepdims=True) + EPS)
    xh = xc * r
    return xh * g + b, xh, r


def _mix_fwd(attn, uvpre, gpre, x, wa, wb, wout, wsm, bsf, gsgu, bsgu, gpost):
    s_len = x.shape[0]
    tm = TOKEN_TILE

    def body(o_ref, uv_ref, gp_ref, x_ref, wa_ref, wb_ref, wo_ref, ws_ref, bs_ref, gs_ref, bsg_ref,
             gpost_ref, sgu_ref, ya_ref, yb_ref, mg_ref, om_ref, x1_ref):
        uvp = uv_ref[...].astype(F32)
        uv, _ = _gelu_and_grad(uvp)
        u, vv = uv[:, :SGU_W], uv[:, SGU_W:]
        vn, _, _ = _layernorm_fwd(vv, gs_ref[...], bsg_ref[...])
        bias = bs_ref[...]
        if tm > WINDOW:
            bias = jnp.concatenate([bias] * (tm // WINDOW), axis=0)
        mixed = _sgu_mix(vn, ws_ref) + bias
        sgu = (u * mixed).astype(BF16)
        sgu_ref[...] = sgu
        ya = jnp.dot(o_ref[...], wa_ref[...], preferred_element_type=F32)
        yb = jnp.dot(sgu, wb_ref[...], preferred_element_type=F32)
        ya_ref[...] = ya.astype(BF16)
        yb_ref[...] = yb.astype(BF16)
        gates = _sigmoid(gp_ref[...].astype(F32))
        merged = (gates[:, :D_MODEL] * ya + gates[:, D_MODEL:] * yb).astype(BF16)
        mg_ref[...] = merged
        om = jnp.dot(merged, wo_ref[...], preferred_element_type=F32)
        om_ref[...] = om
        r = lax.rsqrt(jnp.mean(om * om, axis=-1, keepdims=True) + EPS)
        x1_ref[...] = x_ref[...] + om * r * gpost_ref[...]

    outs = [(SGU_W, BF16), (D_MODEL, BF16), (D_MODEL, BF16), (D_MODEL, BF16), (D_MODEL, F32),
            (D_MODEL, F32)]
    return pl.pallas_call(
        body, name="mix_fwd", grid=(s_len // tm,),
        in_specs=[_row_spec(tm, FOX_W), _row_spec(tm, 2 * SGU_W), _row_spec(tm, 2 * D_MODEL),
                  _row_spec(tm, D_MODEL), _const_spec(wa.shape), _const_spec(wb.shape),
                  _const_spec(wout.shape), _const_spec(wsm.shape), _const_spec(bsf.shape),
                  _const_spec((1, SGU_W)), _const_spec((1, SGU_W)), _const_spec((1, D_MODEL))],
        out_specs=[_row_spec(tm, c) for c, _ in outs],
        out_shape=[jax.ShapeDtypeStruct((s_len, c), dt) for c, dt in outs],
        compiler_params=_params(48, 1),
    )(attn, uvpre, gpre, x, wa, wb, wout, wsm, bsf, gsgu, bsgu, gpost)


def _ffn_fwd_bwd(x1, tgt, wffn, wdown, gpre, gpost):
    s_len = x1.shape[0]
    tm = TOKEN_TILE

    def body(x1_ref, t_ref, wi_ref, wd_ref, gpre_ref, gpost_ref,
             dx1_ref, h2_ref, act_ref, dff_ref, dgu_ref, loss_ref, dgpost_ref, dgpre_ref):
        @pl.when(pl.program_id(0) == 0)
        def _():
            loss_ref[...] = jnp.zeros_like(loss_ref)
            dgpost_ref[...] = jnp.zeros_like(dgpost_ref)
            dgpre_ref[...] = jnp.zeros_like(dgpre_ref)

        x1v = x1_ref[...]
        r2 = lax.rsqrt(jnp.mean(x1v * x1v, axis=-1, keepdims=True) + EPS)
        gpre_v = gpre_ref[...]
        h2 = (x1v * r2 * gpre_v).astype(BF16)
        h2_ref[...] = h2
        gg = jnp.dot(h2, wi_ref[:, :D_FF], preferred_element_type=F32)
        uu = jnp.dot(h2, wi_ref[:, D_FF:], preferred_element_type=F32)
        sg = _sigmoid(gg)
        silu = gg * sg
        act = (silu * uu).astype(BF16)
        act_ref[...] = act
        ff = jnp.dot(act, wd_ref[...], preferred_element_type=F32)
        r3 = lax.rsqrt(jnp.mean(ff * ff, axis=-1, keepdims=True) + EPS)
        gpost_v = gpost_ref[...]
        y = x1v + ff * r3 * gpost_v
        err = y - t_ref[...]
        loss_ref[...] += jnp.sum(err * err) * (0.5 / D_MODEL)
        dy = err * (1.0 / D_MODEL)
        dgpost_ref[...] += jnp.sum(dy * ff * r3, axis=0, keepdims=True)
        dff = _rms_bwd(ff, r3, gpost_v, dy).astype(BF16)
        dff_ref[...] = dff
        dact = _dot_nt(dff, wd_ref[...])
        dgg = (dact * uu * (sg * (1.0 + gg * (1.0 - sg)))).astype(BF16)
        duu = (dact * silu).astype(BF16)
        dgu_ref[:, :D_FF] = dgg
        dgu_ref[:, D_FF:] = duu
        dh2 = _dot_nt(dgg, wi_ref[:, :D_FF]) + _dot_nt(duu, wi_ref[:, D_FF:])
        dgpre_ref[...] += jnp.sum(dh2 * x1v * r2, axis=0, keepdims=True)
        dx1_ref[...] = dy + _rms_bwd(x1v, r2, gpre_v, dh2)

    outs = [((s_len, D_MODEL), F32, _row_spec(tm, D_MODEL)),
            ((s_len, D_MODEL), BF16, _row_spec(tm, D_MODEL)),
            ((s_len, D_FF), BF16, _row_spec(tm, D_FF)),
            ((s_len, D_MODEL), BF16, _row_spec(tm, D_MODEL)),
            ((s_len, 2 * D_FF), BF16, _row_spec(tm, 2 * D_FF)),
            ((1, 128), F32, _const_spec((1, 128))),
            ((1, D_MODEL), F32, _const_spec((1, D_MODEL))),
            ((1, D_MODEL), F32, _const_spec((1, D_MODEL)))]
    return pl.pallas_call(
        body, name="ffn_fwd_bwd", grid=(s_len // tm,),
        in_specs=[_row_spec(tm, D_MODEL), _row_spec(tm, D_MODEL), _const_spec(wffn.shape),
                  _const_spec(wdown.shape), _const_spec((1, D_MODEL)), _const_spec((1, D_MODEL))],
        out_specs=[o[2] for o in outs],
        out_shape=[jax.ShapeDtypeStruct(o[0], o[1]) for o in outs],
        compiler_params=_params(60, 1),
    )(x1, tgt, wffn, wdown, gpre, gpost)


def _mix_bwd(dx1, om, ya, yb, gpre, uvpre, attn, wout, wa, wb, wsm, wsmt, bsf, gsgu, bsgu, gpost,
             wmask, egrp):
    s_len = dx1.shape[0]
    tm = TOKEN_TILE
    nw = tm // WINDOW
    nt = s_len // tm

    def body(dx1_ref, om_ref, ya_ref, yb_ref, gp_ref, uv_ref, o_ref, wo_ref, wa_ref, wb_ref, ws_ref,
             wst_ref, bs_ref, gs_ref, bsg_ref, gpost_ref, mask_ref, eg_ref,
             dom_ref, dya_ref, dyb_ref, dgp_ref, dot_ref, delta_ref, duv_ref,
             dws_ref, dbs_ref, dgs_ref, dbsg_ref, dgpost_ref, dbs_acc):
        step = pl.program_id(0)

        @pl.when(step == 0)
        def _():
            dws_ref[...] = jnp.zeros_like(dws_ref)
            dbs_acc[...] = jnp.zeros_like(dbs_acc)
            dgs_ref[...] = jnp.zeros_like(dgs_ref)
            dbsg_ref[...] = jnp.zeros_like(dbsg_ref)
            dgpost_ref[...] = jnp.zeros_like(dgpost_ref)

        om = om_ref[...]
        dx1v = dx1_ref[...]
        r = lax.rsqrt(jnp.mean(om * om, axis=-1, keepdims=True) + EPS)
        gpost_v = gpost_ref[...]
        dgpost_ref[...] += jnp.sum(dx1v * om * r, axis=0, keepdims=True)
        dom = _rms_bwd(om, r, gpost_v, dx1v).astype(BF16)
        dom_ref[...] = dom
        dmg = _dot_nt(dom, wo_ref[...])

        gates = _sigmoid(gp_ref[...].astype(F32))
        ga, gb = gates[:, :D_MODEL], gates[:, D_MODEL:]
        yav, ybv = ya_ref[...].astype(F32), yb_ref[...].astype(F32)
        dya = (dmg * ga).astype(BF16)
        dyb = (dmg * gb).astype(BF16)
        dya_ref[...] = dya
        dyb_ref[...] = dyb
        dgp_ref[:, :D_MODEL] = (dmg * yav * ga * (1.0 - ga)).astype(BF16)
        dgp_ref[:, D_MODEL:] = (dmg * ybv * gb * (1.0 - gb)).astype(BF16)

        dat_t = _dot_nt(dya, wa_ref[...]).T.astype(BF16)
        dot_ref[0] = dat_t
        o_t = o_ref[...].astype(F32).T
        delta_ref[0] = jnp.sum((dat_t.astype(F32) * o_t).reshape(HEADS, HEAD_DIM, tm), axis=1)
        dsgu = _dot_nt(dyb, wb_ref[...])

        uvp = uv_ref[...].astype(F32)
        uv, guv = _gelu_and_grad(uvp)
        u, vv = uv[:, :SGU_W], uv[:, SGU_W:]
        gs_v = gs_ref[...]
        vn, xh, rln = _layernorm_fwd(vv, gs_v, bsg_ref[...])
        bias = bs_ref[...]
        if nw > 1:
            bias = jnp.concatenate([bias] * nw, axis=0)
        mixed = _sgu_mix(vn, ws_ref) + bias
        du = dsgu * mixed
        dmixed = dsgu * u

        lane = lax.broadcasted_iota(jnp.int32, (WINDOW, 128), 1)
        low = lane < HEAD_DIM
        dvn_wins = []
        for w in range(nw):
            rows = slice(w * WINDOW, (w + 1) * WINDOW)
            dbs_acc[...] += dmixed[rows, :]
            slabs = []
            for p in range(GROUPS // 2):
                cols = slice(p * 128, (p + 1) * 128)
                dm2 = dmixed[rows, cols]
                dlo = jnp.where(low, dm2, 0.0).astype(BF16)
                dhi = jnp.where(low, 0.0, dm2).astype(BF16)
                vn2 = vn[rows, cols].astype(BF16)
                dws_ref[2 * p] += _dot_nt(dlo, vn2)
                dws_ref[2 * p + 1] += _dot_nt(dhi, vn2)
                slabs.append(jnp.dot(wst_ref[2 * p], dlo, preferred_element_type=F32)
                             + jnp.dot(wst_ref[2 * p + 1], dhi, preferred_element_type=F32))
            dvn_wins.append(jnp.concatenate(slabs, axis=1))
        dvn = jnp.concatenate(dvn_wins, axis=0) if nw > 1 else dvn_wins[0]

        dgs_ref[...] += jnp.sum(dvn * xh, axis=0, keepdims=True)
        dbsg_ref[...] += jnp.sum(dvn, axis=0, keepdims=True)
        dxh = dvn * gs_v
        dvv = rln * (dxh - jnp.mean(dxh, axis=-1, keepdims=True)
                     - xh * jnp.mean(dxh * xh, axis=-1, keepdims=True))
        duv_ref[:, :SGU_W] = (du * guv[:, :SGU_W]).astype(BF16)
        duv_ref[:, SGU_W:] = (dvv * guv[:, SGU_W:]).astype(BF16)

        @pl.when(step == pl.num_programs(0) - 1)
        def _():
            for g in range(GROUPS):
                dws_ref[g] = dws_ref[g] * mask_ref[...]
            dbs_ref[...] = _split3_dot(dbs_acc[...], eg_ref[...])

    rows_out = [((s_len, D_MODEL), BF16, _row_spec(tm, D_MODEL)),
                ((s_len, D_MODEL), BF16, _row_spec(tm, D_MODEL)),
                ((s_len, D_MODEL), BF16, _row_spec(tm, D_MODEL)),
                ((s_len, 2 * D_MODEL), BF16, _row_spec(tm, 2 * D_MODEL)),
                ((nt, FOX_W, tm), BF16, _tile_spec(FOX_W, tm)),
                ((nt, HEADS, tm), F32, _tile_spec(HEADS, tm)),
                ((s_len, 2 * SGU_W), BF16, _row_spec(tm, 2 * SGU_W))]
    acc_out = [((GROUPS, WINDOW, WINDOW), F32), ((WINDOW, 128), F32), ((1, SGU_W), F32),
               ((1, SGU_W), F32), ((1, D_MODEL), F32)]
    return pl.pallas_call(
        body, name="mix_bwd", grid=(nt,),
        in_specs=[_row_spec(tm, D_MODEL), _row_spec(tm, D_MODEL), _row_spec(tm, D_MODEL),
                  _row_spec(tm, D_MODEL), _row_spec(tm, 2 * D_MODEL), _row_spec(tm, 2 * SGU_W),
                  _row_spec(tm, FOX_W), _const_spec(wout.shape), _const_spec(wa.shape),
                  _const_spec(wb.shape), _const_spec(wsm.shape), _const_spec(wsmt.shape),
                  _const_spec(bsf.shape), _const_spec((1, SGU_W)), _const_spec((1, SGU_W)),
                  _const_spec((1, D_MODEL)), _const_spec(wmask.shape), _const_spec(egrp.shape)],
        out_specs=[o[2] for o in rows_out] + [_const_spec(s) for s, _ in acc_out],
        out_shape=[jax.ShapeDtypeStruct(o[0], o[1]) for o in rows_out]
        + [jax.ShapeDtypeStruct(s, dt) for s, dt in acc_out],
        scratch_shapes=[pltpu.VMEM((WINDOW, SGU_W), F32)],
        compiler_params=_params(48, 1),
    )(dx1, om, ya, yb, gpre, uvpre, attn, wout, wa, wb, wsm, wsmt, bsf, gsgu, bsgu, gpost, wmask,
      egrp)


def _attn_bwd(qa, ka, kat, vs, dot_, lse, delta):
    s_len = qa.shape[0]
    t = ATTN_TILE
    nb = s_len // t

    def body(k_ref, kt_ref, vs_ref, q_ref, do_ref, lse_ref, dl_ref, gk_ref, dvt_ref, gqt_ref):
        j = pl.program_id(0)

        @pl.when(j == 0)
        def _():
            gqt_ref[...] = jnp.zeros_like(gqt_ref)

        gk_ref[...] = jnp.zeros_like(gk_ref)
        dvt_ref[...] = jnp.zeros_like(dvt_ref)

        def tile(i, masked):
            qrows = pl.ds(pl.multiple_of(i * t, t), t)
            if masked:
                keep = (lax.broadcasted_iota(jnp.int32, (t, t), 0)
                        <= lax.broadcasted_iota(jnp.int32, (t, t), 1))
            for hd in range(HEADS):
                sl = slice(hd * 128, (hd + 1) * 128)
                hr = slice(hd * HEAD_DIM, (hd + 1) * HEAD_DIM)
                q_slab = q_ref[qrows, sl]
                st = _dot_nt(k_ref[:, sl], q_slab)
                if masked:
                    st = jnp.where(keep, st, -jnp.inf)
                pt = jnp.exp2(st - lse_ref[i, hd:hd + 1, :])
                do_h = do_ref[i, hr, :]
                dpt = jnp.dot(vs_ref[:, hd * 128:hd * 128 + HEAD_DIM], do_h,
                              preferred_element_type=F32)
                dst = (pt * (dpt - dl_ref[i, hd:hd + 1, :])).astype(BF16)
                dvt_ref[0, hr, :] += _dot_nt(do_h, pt.astype(BF16))
                gk_ref[:, sl] += jnp.dot(dst, q_slab, preferred_element_type=F32)
                gqt_ref[i, hd * QT_ROWS:(hd + 1) * QT_ROWS, :] += jnp.dot(
                    kt_ref[0, hd * 128:hd * 128 + QT_ROWS, :], dst, preferred_element_type=F32)

        tile(j, True)

        def below_diagonal(i, carry):
            tile(i, False)
            return carry

        lax.fori_loop(j + 1, nb, below_diagonal, 0)

    return pl.pallas_call(
        body, name="attn_bwd", grid=(nb,),
        in_specs=[_row_spec(t, SLAB_W), _tile_spec(SLAB_W, t), _row_spec(t, SLAB_W),
                  _const_spec(qa.shape), _const_spec(dot_.shape), _const_spec(lse.shape),
                  _const_spec(delta.shape)],
        out_specs=[_row_spec(t, SLAB_W), _tile_spec(FOX_W, t),
                   _const_spec((nb, HEADS * QT_ROWS, t))],
        out_shape=[jax.ShapeDtypeStruct((s_len, SLAB_W), F32),
                   jax.ShapeDtypeStruct((nb, FOX_W, t), F32),
                   jax.ShapeDtypeStruct((nb, HEADS * QT_ROWS, t), F32)],
        compiler_params=_params(60, 1),
    )(ka, kat, vs, qa, dot_, lse, delta)


def _rev_cumsum(gk, gqt, triu, ecol):
    s_len = gk.shape[0]
    tm = TOKEN_TILE
    n = s_len // tm

    def body(gk_ref, gqt_ref, tri_ref, ec_ref, o_ref, carry):
        @pl.when(pl.program_id(0) == 0)
        def _():
            carry[...] = jnp.zeros_like(carry)
        col_sums = _split3_dot(gk_ref[...], ec_ref[...])
        rows = [gqt_ref[0, hd * QT_ROWS + HEAD_DIM:hd * QT_ROWS + HEAD_DIM + 1, :]
                for hd in range(HEADS)]
        row_sums = jnp.concatenate(rows + [jnp.zeros((128 - HEADS, tm), F32)], axis=0).T
        out = _tri_dot(tri_ref[...], row_sums - col_sums) + carry[...]
        o_ref[...] = out
        carry[...] = out[0:1, :]

    return pl.pallas_call(
        body, name="rev_cumsum", grid=(n,),
        in_specs=[pl.BlockSpec((tm, SLAB_W), lambda i: (n - 1 - i, 0)),
                  pl.BlockSpec((1, HEADS * QT_ROWS, tm), lambda i: (n - 1 - i, 0, 0)),
                  _const_spec((tm, tm)), _const_spec(ecol.shape)],
        out_specs=pl.BlockSpec((tm, 128), lambda i: (n - 1 - i, 0)),
        out_shape=jax.ShapeDtypeStruct((s_len, 128), F32),
        scratch_shapes=[pltpu.VMEM((1, 128), F32)],
        compiler_params=_params(32, 1),
    )(gk, gqt, triu, ecol)


def _heads_from_slabs(slabs):
    lane = lax.broadcasted_iota(jnp.int32, slabs[0].shape, 1)
    low = lane < HEAD_DIM
    pairs = [jnp.where(low, slabs[2 * p], pltpu.roll(slabs[2 * p + 1], HEAD_DIM, 1))
             for p in range(HEADS // 2)]
    return jnp.concatenate(pairs, axis=1)


def _proj_bwd(gqt, gk, dvt, dlogf, flog, qraw, kraw, duv, dgp, x, dx1, wcat, bdiag, gq, gk_gain, g1,
              efold):
    s_len = x.shape[0]
    tm = TOKEN_TILE

    def body(gqt_ref, gkk_ref, dvt_ref, dlf_ref, flog_ref, qr_ref, kr_ref, duv_ref, dgp_ref, x_ref,
             dx1_ref, w_ref, bd_ref, gq_ref, gk_ref, g1_ref, ef_ref,
             dx_ref, dproj_ref, dgq_ref, dgk_ref, dbf_ref, dg1_ref, gq_acc, gk_acc):
        step = pl.program_id(0)

        @pl.when(step == 0)
        def _():
            gq_acc[...] = jnp.zeros_like(gq_acc)
            gk_acc[...] = jnp.zeros_like(gk_acc)
            dbf_ref[...] = jnp.zeros_like(dbf_ref)
            dg1_ref[...] = jnp.zeros_like(dg1_ref)

        pad = jnp.zeros((128 - QT_ROWS, tm), F32)
        q_slabs = [jnp.concatenate([gqt_ref[0, hd * QT_ROWS:(hd + 1) * QT_ROWS, :], pad], axis=0).T
                   for hd in range(HEADS)]
        dqn = _heads_from_slabs(q_slabs)
        dkn = _heads_from_slabs([gkk_ref[:, hd * 128:(hd + 1) * 128] for hd in range(HEADS)])

        def head_bwd(raw_ref, dn, g_ref, acc):
            raw = raw_ref[...].astype(F32)
            r = lax.rsqrt(_seg_mean(raw * raw, bd_ref) + EPS)
            xhat = raw * r
            acc[0:1, :] += jnp.sum(dn * xhat, axis=0, keepdims=True)
            dyg = dn * g_ref[...]
            return r * (dyg - xhat * _seg_mean(dyg * xhat, bd_ref))

        dproj_ref[:, C_Q:C_K] = head_bwd(qr_ref, dqn * HEAD_DIM ** -0.5, gq_ref, gq_acc).astype(BF16)
        dproj_ref[:, C_K:C_V] = head_bwd(kr_ref, dkn * LN2, gk_ref, gk_acc).astype(BF16)
        dproj_ref[:, C_V:C_F] = dvt_ref[0].T.astype(BF16)
        dfl = dlf_ref[...] * _sigmoid(-flog_ref[...])
        dbf_ref[...] += jnp.sum(dfl, axis=0, keepdims=True)
        dproj_ref[:, C_F:C_UV] = dfl.astype(BF16)
        dproj_ref[:, C_UV:C_G] = duv_ref[...]
        dproj_ref[:, C_G:C_END] = dgp_ref[...]

        dh = _dot_nt(dproj_ref[...], w_ref[...])
        xf = x_ref[...]
        r = lax.rsqrt(jnp.mean(xf * xf, axis=-1, keepdims=True) + EPS)
        dg1_ref[...] += jnp.sum(dh * xf * r, axis=0, keepdims=True)
        dx_ref[...] = dx1_ref[...] + _rms_bwd(xf, r, g1_ref[...], dh)

        @pl.when(step == pl.num_programs(0) - 1)
        def _():
            dgq_ref[...] = _split3_dot(gq_acc[...], ef_ref[...])
            dgk_ref[...] = _split3_dot(gk_acc[...], ef_ref[...])

    outs = [((s_len, D_MODEL), F32, _row_spec(tm, D_MODEL)),
            ((s_len, C_END), BF16, _row_spec(tm, C_END)),
            ((8, 128), F32, _const_spec((8, 128))),
            ((8, 128), F32, _const_spec((8, 128))),
            ((1, 128), F32, _const_spec((1, 128))),
            ((1, D_MODEL), F32, _const_spec((1, D_MODEL)))]
    return pl.pallas_call(
        body, name="proj_bwd", grid=(s_len // tm,),
        in_specs=[_tile_spec(HEADS * QT_ROWS, tm), _row_spec(tm, SLAB_W), _tile_spec(FOX_W, tm),
                  _row_spec(tm, 128), _row_spec(tm, 128), _row_spec(tm, FOX_W),
                  _row_spec(tm, FOX_W), _row_spec(tm, 2 * SGU_W), _row_spec(tm, 2 * D_MODEL),
                  _row_spec(tm, D_MODEL), _row_spec(tm, D_MODEL), _const_spec(wcat.shape),
                  _const_spec(bdiag.shape), _const_spec((1, FOX_W)), _const_spec((1, FOX_W)),
                  _const_spec((1, D_MODEL)), _const_spec(efold.shape)],
        out_specs=[o[2] for o in outs],
        out_shape=[jax.ShapeDtypeStruct(o[0], o[1]) for o in outs],
        scratch_shapes=[pltpu.VMEM((8, FOX_W), F32), pltpu.VMEM((8, FOX_W), F32)],
        compiler_params=_params(56, 1),
    )(gqt, gk, dvt, dlogf, flog, qraw, kraw, duv, dgp, x, dx1, wcat, bdiag, gq, gk_gain, g1, efold)


def _dw_matmul(a, b, tm, name):
    s_len, m = a.shape
    n = b.shape[1]
    tk = min(512, s_len)
    nk = s_len // tk

    def body(a_ref, b_ref, o_ref, acc):
        kk = pl.program_id(1)

        @pl.when(kk == 0)
        def _():
            acc[...] = jnp.zeros_like(acc)
        acc[...] += _dot_tn(a_ref[...], b_ref[...])

        @pl.when(kk == nk - 1)
        def _():
            o_ref[...] = acc[...].astype(BF16)

    return pl.pallas_call(
        body, name=name, grid=(m // tm, nk),
        in_specs=[pl.BlockSpec((tk, tm), lambda i, k: (k, i)),
                  pl.BlockSpec((tk, n), lambda i, k: (k, 0))],
        out_specs=pl.BlockSpec((tm, n), lambda i, k: (i, 0)),
        out_shape=jax.ShapeDtypeStruct((m, n), BF16),
        scratch_shapes=[pltpu.VMEM((tm, n), F32)],
        compiler_params=_params(56, 2),
    )(a, b)


def _adamw(parts, w, m, v, tr, name):
    n, rows, cols = parts.shape
    bc1 = 1.0 - ADAM_B1 ** ADAM_STEP
    bc2 = 1.0 - ADAM_B2 ** ADAM_STEP

    def body(p_ref, w_ref, m_ref, v_ref, g_ref, d_ref, mo_ref, vo_ref):
        g = p_ref[0].astype(F32)
        for idx in range(1, n):
            g = g + p_ref[idx].astype(F32)
        g_ref[...] = g
        mn = ADAM_B1 * m_ref[...] + (1.0 - ADAM_B1) * g
        vn = ADAM_B2 * v_ref[...] + (1.0 - ADAM_B2) * (g * g)
        mo_ref[...] = mn
        vo_ref[...] = vn
        m_hat = mn / bc1
        v_hat = vn / bc2
        d_ref[...] = -ADAM_LR * (m_hat / (jnp.sqrt(v_hat) + ADAM_EPS) + ADAM_WD * w_ref[...])

    spec = pl.BlockSpec((tr, cols), lambda i: (i, 0))
    return pl.pallas_call(
        body, name=name, grid=(rows // tr,),
        in_specs=[pl.BlockSpec((n, tr, cols), lambda i: (0, i, 0)), spec, spec, spec],
        out_specs=[spec] * 4,
        out_shape=[jax.ShapeDtypeStruct((rows, cols), F32)] * 4,
        compiler_params=_params(48, 1),
    )(parts, w, m, v)


def _sum_parts(parts, name):
    n, rows, cols = parts.shape

    def body(p_ref, o_ref):
        g = p_ref[0]
        for idx in range(1, n):
            g = g + p_ref[idx]
        o_ref[...] = g

    return pl.pallas_call(
        body, name=name, out_shape=jax.ShapeDtypeStruct((rows, cols), F32),
        in_specs=[_const_spec(parts.shape)], out_specs=_const_spec((rows, cols)), grid=(1,),
        compiler_params=_params(16, 1),
    )(parts)


SMALL_NAMES = ("g_pre_mix", "b_forget", "g_q", "g_k", "g_sgu", "b_sgu", "w_spatial", "b_spatial",
               "g_post_mix", "g_pre_ffn", "g_post_ffn")
SMALL_TOTAL = N_DEV * SMALL_ROWS * 1024


def _pack_small(d):
    flat = jnp.concatenate([d[k].reshape(-1).astype(F32) for k in SMALL_NAMES])
    flat = jnp.pad(flat, (0, SMALL_TOTAL - flat.shape[0]))
    return flat.reshape(N_DEV * SMALL_ROWS, 1024)


def _unpack_small(packed, shapes):
    flat = packed.reshape(-1)
    out, off = {}, 0
    for k in SMALL_NAMES:
        size = math.prod(shapes[k])
        out[k] = flat[off:off + size].reshape(shapes[k])
        off += size
    return out


def _cols_to_blocks(full, width):
    r = full.shape[0]
    return jnp.transpose(full.reshape(r, N_DEV, width), (1, 0, 2))


def _blocks_to_cols(blocks):
    n, r, width = blocks.shape
    return jnp.transpose(blocks, (1, 0, 2)).reshape(r, n * width)


def kernel(x, g_pre_mix, w_in, b_forget, g_q, g_k, g_sgu, b_sgu, w_spatial, b_spatial, w_branch_a, w_branch_b, w_out, g_post_mix, g_pre_ffn, w_ffn_in, w_ffn_down, g_post_ffn, loss_target, m_g_pre_mix, m_w_in, m_b_forget, m_g_q, m_g_k, m_g_sgu, m_b_sgu, m_w_spatial, m_b_spatial, m_w_branch_a, m_w_branch_b, m_w_out, m_g_post_mix, m_g_pre_ffn, m_w_ffn_in, m_w_ffn_down, m_g_post_ffn, v_g_pre_mix, v_w_in, v_b_forget, v_g_q, v_g_k, v_g_sgu, v_b_sgu, v_w_spatial, v_b_spatial, v_w_branch_a, v_w_branch_b, v_w_out, v_g_post_mix, v_g_pre_ffn, v_w_ffn_in, v_w_ffn_down, v_g_post_ffn):
    big_names = ("w_in", "w_branch_a", "w_branch_b", "w_out", "w_ffn_in", "w_ffn_down")
    weights = dict(g_pre_mix=g_pre_mix, w_in=w_in, b_forget=b_forget, g_q=g_q, g_k=g_k, g_sgu=g_sgu,
                   b_sgu=b_sgu, w_spatial=w_spatial, b_spatial=b_spatial, w_branch_a=w_branch_a,
                   w_branch_b=w_branch_b, w_out=w_out, g_post_mix=g_post_mix, g_pre_ffn=g_pre_ffn,
                   w_ffn_in=w_ffn_in, w_ffn_down=w_ffn_down, g_post_ffn=g_post_ffn)
    mom1 = dict(g_pre_mix=m_g_pre_mix, w_in=m_w_in, b_forget=m_b_forget, g_q=m_g_q, g_k=m_g_k,
                g_sgu=m_g_sgu, b_sgu=m_b_sgu, w_spatial=m_w_spatial, b_spatial=m_b_spatial,
                w_branch_a=m_w_branch_a, w_branch_b=m_w_branch_b, w_out=m_w_out,
                g_post_mix=m_g_post_mix, g_pre_ffn=m_g_pre_ffn, w_ffn_in=m_w_ffn_in,
                w_ffn_down=m_w_ffn_down, g_post_ffn=m_g_post_ffn)
    mom2 = dict(g_pre_mix=v_g_pre_mix, w_in=v_w_in, b_forget=v_b_forget, g_q=v_g_q, g_k=v_g_k,
                g_sgu=v_g_sgu, b_sgu=v_b_sgu, w_spatial=v_w_spatial, b_spatial=v_b_spatial,
                w_branch_a=v_w_branch_a, w_branch_b=v_w_branch_b, w_out=v_w_out,
                g_post_mix=v_g_post_mix, g_pre_ffn=v_g_pre_ffn, w_ffn_in=v_w_ffn_in,
                w_ffn_down=v_w_ffn_down, g_post_ffn=v_g_post_ffn)
    names = list(weights)
    shapes = {k: weights[k].shape for k in names}

    s_len = x.shape[1]
    xs = x.reshape(s_len, D_MODEL)
    tgt = loss_target.reshape(s_len, D_MODEL)

    shards = {k: weights[k][0].astype(BF16) for k in big_names}
    (gat_in, gat_mix, gat_ffn), gat_token = _exchange_start(
        [[shards["w_in"]],
         [shards["w_branch_a"], shards["w_branch_b"], shards["w_out"]],
         [shards["w_ffn_in"], shards["w_ffn_down"]]], "gather_start", gather=True)
    (zone_in,) = _exchange_wait(gat_in, gat_token, "gather_wait_in", gather=True)
    win_full = _blocks_to_cols(_own_block(zone_in, shards["w_in"]))
    f_off = 3 * FOX_W
    u_off = f_off + HEADS
    g_off = u_off + 2 * SGU_W
    wcat = jnp.concatenate([
        win_full[:, :f_off],
        jnp.pad(win_full[:, f_off:u_off], ((0, 0), (0, 128 - HEADS))),
        win_full[:, u_off:g_off], win_full[:, g_off:]], axis=1)

    seg = jnp.arange(FOX_W) // HEAD_DIM
    bdiag = (seg[:, None] == seg[None, :]).astype(BF16)
    tm = TOKEN_TILE
    tril = (jnp.arange(tm)[None, :] <= jnp.arange(tm)[:, None]).astype(BF16)
    triu = tril.T
    egrp = (seg[:, None] == jnp.arange(128)[None, :]).astype(BF16)
    efold = ((jnp.arange(FOX_W) % HEAD_DIM)[:, None] == jnp.arange(128)[None, :]).astype(BF16)
    gq512 = jnp.tile(g_q.reshape(1, HEAD_DIM), (1, HEADS))
    gk512 = jnp.tile(g_k.reshape(1, HEAD_DIM), (1, HEADS))
    bfor = jnp.pad(b_forget.reshape(1, HEADS), ((0, 0), (0, 128 - HEADS)))
    pos = jnp.arange(WINDOW)
    wmask = ((pos[None, :] // CHUNK) <= (pos[:, None] // CHUNK))
    wsm_f = jnp.where(wmask[None], w_spatial[0], 0.0)
    wsm = wsm_f.astype(BF16)
    wsmt = jnp.transpose(wsm_f, (0, 2, 1)).astype(BF16)
    bsf = jnp.repeat(jnp.transpose(b_spatial[0]), HEAD_DIM, axis=1)
    wmask_f = wmask.astype(F32)

    col = jnp.arange(SLAB_W)
    place = ((col[None, :] // 128 == seg[:, None])
             & (col[None, :] % 128 == (jnp.arange(FOX_W) % HEAD_DIM)[:, None])).astype(BF16)
    row128 = jnp.arange(128)

    def d_place(first):
        return jnp.stack([((col[None, :] // 128 == row128[:, None])
                           & (col[None, :] % 128 == first + a)).astype(BF16) for a in range(3)])

    pdq, pdk = d_place(HEAD_DIM), d_place(HEAD_DIM + 3)
    ones_q = ((col % 128 >= HEAD_DIM + 3) & (col % 128 < HEAD_DIM + 6)).astype(F32)[None]
    ones_k = ((col % 128 >= HEAD_DIM) & (col % 128 < HEAD_DIM + 3)).astype(F32)[None]
    ecol = ((col[:, None] // 128 == row128[None, :])
            & (col[:, None] % 128 == HEAD_DIM + 3)).astype(BF16)

    (h, qa, ka, kat, vs, vt, qraw, kraw, flog, uvpre, gpre) = _proj_fwd(
        xs, g_pre_mix, wcat, bdiag, gq512, gk512, bfor, tril, place, pdq, pdk, ones_q, ones_k)
    attn, lse = _attn_fwd(qa, ka, vt)
    zone_a, zone_b, zone_out = _exchange_wait(gat_mix, attn, "gather_wait_mix", gather=True)
    wa = _blocks_to_cols(_own_block(zone_a, shards["w_branch_a"]))
    wb = _blocks_to_cols(_own_block(zone_b, shards["w_branch_b"]))
    wout = _own_block(zone_out, shards["w_out"]).reshape(D_MODEL, D_MODEL)
    sgu, ya, yb, merged, om, x1 = _mix_fwd(attn, uvpre, gpre, xs, wa, wb, wout, wsm, bsf,
                                           g_sgu, b_sgu, g_post_mix)
    zone_ffn, zone_down = _exchange_wait(gat_ffn, x1, "gather_wait_ffn", gather=True)
    wffn = _blocks_to_cols(_own_block(zone_ffn, shards["w_ffn_in"]))
    wdown = _own_block(zone_down, shards["w_ffn_down"]).reshape(D_FF, D_MODEL)
    (dx1, h2, act, dff, dgu, loss_acc, dg_post_ffn, dg_pre_ffn) = _ffn_fwd_bwd(
        x1, tgt, wffn, wdown, g_pre_ffn, g_post_ffn)

    dw_down = _dw_matmul(act, dff, D_FF // 2, "dw_down")
    dw_ffn = _dw_matmul(h2, dgu, 512, "dw_ffn_in")
    parts_ffn = [_cols_to_blocks(dw_ffn, 2 * D_FF // N_DEV),
                 dw_down.reshape(N_DEV, D_FF // N_DEV, D_MODEL)]
    (sct_ffn,), sct_ffn_token = _exchange_start([parts_ffn], "scatter_start_ffn", gather=False)

    (dom, dya, dyb, dgp, dot_, delta, duv, dws, dbs, dg_sgu, db_sgu, dg_post_mix) = _mix_bwd(
        dx1, om, ya, yb, gpre, uvpre, attn, wout, wa, wb, wsm, wsmt, bsf, g_sgu, b_sgu,
        g_post_mix + sct_ffn_token[0:1, 0:1], wmask_f, egrp)
    dw_out = _dw_matmul(merged, dom, 512, "dw_out")
    dw_a = _dw_matmul(attn, dya, 512, "dw_a")
    dw_b = _dw_matmul(sgu, dyb, 512, "dw_b")
    parts_mix = [_cols_to_blocks(dw_a, D_MODEL // N_DEV), _cols_to_blocks(dw_b, D_MODEL // N_DEV),
                 dw_out.reshape(N_DEV, D_MODEL // N_DEV, D_MODEL)]
    (sct_mix,), sct_mix_token = _exchange_start([parts_mix], "scatter_start_mix", gather=False)

    gk_all, dvt, gqt = _attn_bwd(qa, ka, kat, vs, dot_, lse, delta + sct_mix_token[0, 0])
    dlogf = _rev_cumsum(gk_all, gqt, triu, ecol)
    dx, dproj, dgq, dgk, dbf, dg_pre_mix = _proj_bwd(
        gqt, gk_all, dvt, dlogf, flog, qraw, kraw, duv, dgp, xs, dx1, wcat, bdiag, gq512, gk512,
        g_pre_mix, efold)
    dw_cat = _dw_matmul(h, dproj, 512, "dw_in")
    dw_in = jnp.concatenate([dw_cat[:, :C_F], dw_cat[:, C_F:C_F + HEADS], dw_cat[:, C_UV:]], axis=1)

    small_local = dict(
        g_pre_mix=dg_pre_mix, b_forget=dbf[:, :HEADS], g_q=dgq[0:1, :HEAD_DIM],
        g_k=dgk[0:1, :HEAD_DIM], g_sgu=dg_sgu, b_sgu=db_sgu, w_spatial=dws,
        b_spatial=jnp.transpose(dbs[:, :GROUPS]), g_post_mix=dg_post_mix, g_pre_ffn=dg_pre_ffn,
        g_post_ffn=dg_post_ffn)
    small_parts = _pack_small(small_local).reshape(N_DEV, SMALL_ROWS, 1024)

    recv_in, recv_small = _exchange([_cols_to_blocks(dw_in, IN_COLS // N_DEV), small_parts],
                                    "scatter_in_small", gather=False)
    x_pos, y_pos, c_pos = _mesh_pos()
    me = 4 * x_pos + 2 * y_pos + c_pos

    def with_own(zones, parts):
        return [_own_block(z, lax.dynamic_index_in_dim(p, me, 0, keepdims=False))
                for z, p in zip(zones, parts)]

    recv_ffn, recv_down = with_own(
        _exchange_wait(sct_ffn, recv_in, "scatter_wait_ffn", gather=False), parts_ffn)
    recv_a, recv_b, recv_out = with_own(
        _exchange_wait(sct_mix, recv_ffn, "scatter_wait_mix", gather=False), parts_mix)
    received = [recv_in, recv_a, recv_b, recv_out, recv_ffn, recv_down]

    grads, deltas, new_m, new_v = {}, {}, {}, {}
    row_tiles = {"w_in": 128, "w_branch_a": 512, "w_branch_b": 512, "w_out": 128, "w_ffn_in": 128,
                 "w_ffn_down": 352}
    for idx, k in enumerate(big_names):
        g, d, mn, vn = _adamw(received[idx], weights[k][0], mom1[k][0], mom2[k][0], row_tiles[k],
                              "adamw_" + k)
        grads[k], deltas[k], new_m[k], new_v[k] = g[None], d[None], mn[None], vn[None]

    small_sum = _sum_parts(recv_small, "sum_small")
    (small_all,) = _exchange([small_sum], "gather_small", gather=True)
    small_all = small_all.reshape(1, N_DEV * SMALL_ROWS, 1024)
    sg, sd, sm, sv = _adamw(small_all, _pack_small(weights), _pack_small(mom1), _pack_small(mom2),
                            N_DEV * SMALL_ROWS, "adamw_small")
    for dst, packed in ((grads, sg), (deltas, sd), (new_m, sm), (new_v, sv)):
        dst.update(_unpack_small(packed, shapes))

    loss = lax.psum(loss_acc[0, 0], ("x", "y", "c"))
    return (loss, dx.reshape(x.shape), *[grads[k] for k in names], *[deltas[k] for k in names],
            *[new_m[k] for k in names], *[new_v[k] for k in names])
```

```python
import functools
import math

import jax
import jax.numpy as jnp
from jax import lax
from jax.experimental import pallas as pl
from jax.experimental.pallas import tpu as pltpu

F32 = jnp.float32
BF16 = jnp.bfloat16

D_MODEL = 1024
FOX_W = 512
HEADS = 8
HEAD_DIM = 64
SGU_W = 512
GROUPS = 8
WINDOW = 128
CHUNK = 64
D_FF = 2816
IN_COLS = 4616
EPS = 1e-6
N_DEV = 8
LOG2E = 1.4426950408889634
LN2 = 0.6931471805599453

C_Q, C_K, C_V, C_F, C_UV, C_G, C_END = 0, 512, 1024, 1536, 1664, 2688, 4736

ADAM_LR, ADAM_B1, ADAM_B2, ADAM_EPS, ADAM_WD, ADAM_STEP = 0.001, 0.9, 0.999, 1e-08, 0.01, 10

MIB = 1024 * 1024
TOKEN_TILE = 256
ATTN_TILE = 256
SLAB_W = HEADS * 128
QT_ROWS = 72

SMALL_ROWS = 24


def _params(vmem_mib, n_axes):
    return pltpu.CompilerParams(
        dimension_semantics=("arbitrary",) * n_axes, vmem_limit_bytes=vmem_mib * MIB)


def _const_spec(shape):
    nd = len(shape)
    return pl.BlockSpec(shape, lambda *_: (0,) * nd)


def _row_spec(tm, cols):
    return pl.BlockSpec((tm, cols), lambda i: (i, 0))


def _tile_spec(rows, tm):
    return pl.BlockSpec((1, rows, tm), lambda i: (i, 0, 0))


def _split3_dot(x, e):
    x1 = x.astype(BF16)
    r1 = x - x1.astype(F32)
    x2 = r1.astype(BF16)
    x3 = (r1 - x2.astype(F32)).astype(BF16)
    dot = functools.partial(jnp.dot, preferred_element_type=F32)
    return dot(x1, e) + dot(x2, e) + dot(x3, e)


def _tri_dot(tri, x):
    x1 = x.astype(BF16)
    r1 = x - x1.astype(F32)
    x2 = r1.astype(BF16)
    x3 = (r1 - x2.astype(F32)).astype(BF16)
    dot = functools.partial(jnp.dot, preferred_element_type=F32)
    return dot(tri, x1) + dot(tri, x2) + dot(tri, x3)


def _seg_mean(sq, bd_ref):
    hi = sq.astype(BF16)
    lo = (sq - hi.astype(F32)).astype(BF16)
    bd = bd_ref[...]
    s = jnp.dot(hi, bd, preferred_element_type=F32) + jnp.dot(lo, bd, preferred_element_type=F32)
    return s * (1.0 / HEAD_DIM)


def _dot_nt(a, b):
    return lax.dot_general(a, b, (((1,), (1,)), ((), ())), preferred_element_type=F32)


def _dot_tn(a, b):
    return lax.dot_general(a, b, (((0,), (0,)), ((), ())), preferred_element_type=F32)


def _sigmoid(x):
    return 1.0 / (1.0 + jnp.exp(-x))


_GELU_C = math.sqrt(2.0 / math.pi)


def _gelu_and_grad(x):
    inner = _GELU_C * (x + 0.044715 * x * x * x)
    t = jnp.tanh(inner)
    y = 0.5 * x * (1.0 + t)
    dy = 0.5 * (1.0 + t) + 0.5 * x * (1.0 - t * t) * _GELU_C * (1.0 + 3.0 * 0.044715 * x * x)
    return y, dy


def _rms_bwd(xin, r, g, dy):
    dyg = dy * g
    return r * dyg - xin * (r * r * r) * jnp.mean(dyg * xin, axis=-1, keepdims=True)


def _mesh_pos():
    x, y, c = lax.axis_index("x"), lax.axis_index("y"), lax.axis_index("c")
    return x, y, c


def _peer(k):
    x, y, c = _mesh_pos()
    px = (1 - x) if (k >> 2) & 1 else x
    py = (1 - y) if (k >> 1) & 1 else y
    pc = (1 - c) if k & 1 else c
    return (px, py, pc), 4 * px + 2 * py + pc


def _exchange(arrs, name, gather):
    n = len(arrs)
    if gather:
        out_shape = [jax.ShapeDtypeStruct((N_DEV,) + a.shape, a.dtype) for a in arrs]
    else:
        out_shape = [jax.ShapeDtypeStruct(a.shape, a.dtype) for a in arrs]

    def body(*refs):
        ins, outs = refs[:n], refs[n:2 * n]
        send_sems, recv_sems, local_sems = refs[2 * n:]
        x, y, c = _mesh_pos()
        me = 4 * x + 2 * y + c

        def src(a, idx):
            return ins[a] if gather else ins[a].at[idx]

        local = []
        for a in range(n):
            cp = pltpu.make_async_copy(src(a, me), outs[a].at[me], local_sems.at[a])
            cp.start()
            local.append(cp)
        sends = []
        for k in range(1, N_DEV):
            peer, pidx = _peer(k)
            for a in range(n):
                cp = pltpu.make_async_remote_copy(
                    src_ref=src(a, pidx), dst_ref=outs[a].at[me],
                    send_sem=send_sems.at[a, k - 1], recv_sem=recv_sems.at[a, k - 1],
                    device_id=peer, device_id_type=pl.DeviceIdType.MESH)
                cp.start()
                sends.append(cp)
        for k in range(1, N_DEV):
            peer, pidx = _peer(k)
            for a in range(n):
                pltpu.make_async_remote_copy(
                    src_ref=src(a, pidx), dst_ref=outs[a].at[pidx],
                    send_sem=send_sems.at[a, k - 1], recv_sem=recv_sems.at[a, k - 1],
                    device_id=peer, device_id_type=pl.DeviceIdType.MESH).wait_recv()
        for cp in sends:
            cp.wait_send()
        for cp in local:
            cp.wait()

    any_spec = pl.BlockSpec(memory_space=pl.ANY)
    return pl.pallas_call(
        body, name=name, out_shape=out_shape,
        in_specs=[any_spec] * n, out_specs=[any_spec] * n,
        scratch_shapes=[pltpu.SemaphoreType.DMA((n, N_DEV - 1)),
                        pltpu.SemaphoreType.DMA((n, N_DEV - 1)),
                        pltpu.SemaphoreType.DMA((n,))],
    )(*arrs)


def _remote_copy(gather, src_ref, land_ref, send_sem, recv_sem, k, receive_side):
    x, y, c = _mesh_pos()
    me = 4 * x + 2 * y + c
    peer, pidx = _peer(k)
    return pltpu.make_async_remote_copy(
        src_ref=src_ref if gather else src_ref.at[pidx],
        dst_ref=land_ref.at[pidx if receive_side else me],
        send_sem=send_sem, recv_sem=recv_sem,
        device_id=peer, device_id_type=pl.DeviceIdType.MESH)


def _exchange_start(groups, name, gather):
    arrs = [a for g in groups for a in g]
    n, n_groups = len(arrs), len(groups)
    lands = [jax.ShapeDtypeStruct(((N_DEV,) + a.shape) if gather else a.shape, a.dtype)
             for a in arrs]

    def body(*refs):
        srcs, zones = refs[:n], refs[n:2 * n]
        sems = refs[2 * n:2 * n + 2 * n_groups]
        token = refs[-1]
        a = 0
        for gi, g in enumerate(groups):
            send_sems, recv_sems = sems[2 * gi], sems[2 * gi + 1]
            for k in range(1, N_DEV):
                for ai in range(len(g)):
                    slot = ai * (N_DEV - 1) + k - 1
                    _remote_copy(gather, srcs[a + ai], zones[a + ai], send_sems.at[slot],
                                 recv_sems.at[slot], k, False).start()
            a += len(g)
        token[...] = jnp.zeros_like(token)

    hbm = pl.BlockSpec(memory_space=pltpu.HBM)
    sem = pl.BlockSpec(memory_space=pltpu.SEMAPHORE)
    sem_shapes = []
    for g in groups:
        sem_shapes += [pltpu.SemaphoreType.DMA((len(g) * (N_DEV - 1),))] * 2
    outs = pl.pallas_call(
        body, name=name,
        in_specs=[hbm] * (2 * n),
        out_shape=sem_shapes + [pltpu.HBM(a.shape, a.dtype) for a in arrs]
        + [pltpu.HBM(z.shape, z.dtype) for z in lands] + [jax.ShapeDtypeStruct((8, 128), F32)],
        out_specs=[sem] * (2 * n_groups) + [hbm] * (2 * n)
        + [pl.BlockSpec(memory_space=pltpu.VMEM)],
        input_output_aliases={i: 2 * n_groups + i for i in range(2 * n)},
        compiler_params=pltpu.CompilerParams(
            has_side_effects=pltpu.SideEffectType.DATAFLOW_SIDE_EFFECTING),
    )(*[pltpu.with_memory_space_constraint(a, pltpu.HBM) for a in arrs],
      *[pltpu.with_memory_space_constraint(lax.empty(z.shape, z.dtype), pltpu.HBM) for z in lands])
    sems = outs[:2 * n_groups]
    thru = outs[2 * n_groups:2 * n_groups + n]
    zones = outs[2 * n_groups + n:2 * n_groups + 2 * n]
    handles, a = [], 0
    for gi, g in enumerate(groups):
        handles.append((sems[2 * gi], sems[2 * gi + 1], thru[a:a + len(g)], zones[a:a + len(g)]))
        a += len(g)
    return handles, outs[-1]


def _exchange_wait(handle, after, name, gather):
    send_sems, recv_sems, thru, zones = handle
    n = len(thru)

    def body(*refs):
        srcs, lands = refs[:n], refs[n:2 * n]
        ssem, rsem = refs[2 * n], refs[2 * n + 1]
        for k in range(1, N_DEV):
            for ai in range(n):
                slot = ai * (N_DEV - 1) + k - 1
                cp = _remote_copy(gather, srcs[ai], lands[ai], ssem.at[slot], rsem.at[slot], k, True)
                cp.wait_send()
                cp.wait_recv()

    hbm = pl.BlockSpec(memory_space=pltpu.HBM)
    sem = pl.BlockSpec(memory_space=pltpu.SEMAPHORE)
    outs = pl.pallas_call(
        body, name=name,
        in_specs=[hbm] * (2 * n) + [sem, sem, pl.BlockSpec(memory_space=pl.ANY)],
        out_shape=[pltpu.HBM(a.shape, a.dtype) for a in thru]
        + [pltpu.HBM(z.shape, z.dtype) for z in zones],
        out_specs=[hbm] * (2 * n),
        input_output_aliases={i: i for i in range(2 * n)},
        compiler_params=pltpu.CompilerParams(
            has_side_effects=pltpu.SideEffectType.DATAFLOW_SIDE_EFFECTING),
    )(*thru, *zones, send_sems, recv_sems, after)
    return outs[n:]


def _own_block(zone, block):
    x, y, c = _mesh_pos()
    me = 4 * x + 2 * y + c
    return lax.dynamic_update_slice_in_dim(zone, block[None], me, axis=0)


def _proj_fwd(x, g1, wcat, bdiag, gq, gk, bfor, tri, place, pdq, pdk, ones_q, ones_k):
    s_len = x.shape[0]
    tm = TOKEN_TILE
    nt = s_len // tm

    def body(x_ref, g1_ref, w_ref, bd_ref, gq_ref, gk_ref, bf_ref, tri_ref, pl_ref, pdq_ref,
             pdk_ref, oq_ref, ok_ref,
             h_ref, qa_ref, ka_ref, kat_ref, vs_ref, vt_ref, qr_ref, kr_ref, flog_ref, uv_ref,
             gp_ref, carry):
        @pl.when(pl.program_id(0) == 0)
        def _():
            carry[...] = jnp.zeros_like(carry)

        xf = x_ref[...]
        r = lax.rsqrt(jnp.mean(xf * xf, axis=-1, keepdims=True) + EPS)
        h = (xf * r * g1_ref[...]).astype(BF16)
        h_ref[...] = h
        dot = functools.partial(jnp.dot, preferred_element_type=F32)

        def proj(lo, hi):
            return dot(h, w_ref[:, lo:hi])

        flog = proj(C_F, C_UV) + bf_ref[...]
        flog_ref[...] = flog
        lane = lax.broadcasted_iota(jnp.int32, flog.shape, 1)
        logf = jnp.minimum(flog, 0.0) - jnp.log(1.0 + jnp.exp(-jnp.abs(flog)))
        logf = jnp.where(lane < HEADS, logf, 0.0)
        dcum = _tri_dot(tri_ref[...], logf) + carry[...]
        carry[...] = dcum[tm - 1:tm, :]
        d2 = dcum * LOG2E
        d2a = d2.astype(BF16)
        rem = d2 - d2a.astype(F32)
        d2b = rem.astype(BF16)
        d2c = (rem - d2b.astype(F32)).astype(BF16)

        place_m = pl_ref[...]
        q = proj(C_Q, C_K)
        qr_ref[...] = q.astype(BF16)
        rq = lax.rsqrt(_seg_mean(q * q, bd_ref) + EPS)
        qn = (q * rq * (gq_ref[...] * (HEAD_DIM ** -0.5 * LOG2E))).astype(BF16)
        qa = (dot(qn, place_m) + dot(d2a, pdq_ref[0]) + dot(d2b, pdq_ref[1])
              + dot(d2c, pdq_ref[2]) + oq_ref[...])
        qa_ref[...] = qa.astype(BF16)

        k = proj(C_K, C_V)
        kr_ref[...] = k.astype(BF16)
        rk = lax.rsqrt(_seg_mean(k * k, bd_ref) + EPS)
        kn = (k * rk * gk_ref[...]).astype(BF16)
        ka = (dot(kn, place_m) - dot(d2a, pdk_ref[0]) - dot(d2b, pdk_ref[1])
              - dot(d2c, pdk_ref[2]) + ok_ref[...])
        ka_ref[...] = ka.astype(BF16)
        kat_ref[0] = ka.T.astype(BF16)

        v = proj(C_V, C_F)
        vs_ref[...] = dot(v.astype(BF16), place_m).astype(BF16)
        vt_ref[0] = v.T.astype(BF16)
        uv_ref[...] = proj(C_UV, C_G).astype(BF16)
        gp_ref[...] = proj(C_G, C_END).astype(BF16)

    outs = [((s_len, D_MODEL), BF16, _row_spec(tm, D_MODEL)),
            ((s_len, SLAB_W), BF16, _row_spec(tm, SLAB_W)),
            ((s_len, SLAB_W), BF16, _row_spec(tm, SLAB_W)),
            ((nt, SLAB_W, tm), BF16, _tile_spec(SLAB_W, tm)),
            ((s_len, SLAB_W), BF16, _row_spec(tm, SLAB_W)),
            ((nt, FOX_W, tm), BF16, _tile_spec(FOX_W, tm)),
            ((s_len, FOX_W), BF16, _row_spec(tm, FOX_W)),
            ((s_len, FOX_W), BF16, _row_spec(tm, FOX_W)),
            ((s_len, 128), F32, _row_spec(tm, 128)),
            ((s_len, 2 * SGU_W), BF16, _row_spec(tm, 2 * SGU_W)),
            ((s_len, 2 * D_MODEL), BF16, _row_spec(tm, 2 * D_MODEL))]
    return pl.pallas_call(
        body, name="proj_fwd", grid=(nt,),
        in_specs=[_row_spec(tm, D_MODEL), _const_spec((1, D_MODEL)), _const_spec(wcat.shape),
                  _const_spec(bdiag.shape), _const_spec((1, FOX_W)), _const_spec((1, FOX_W)),
                  _const_spec((1, 128)), _const_spec((tm, tm)), _const_spec(place.shape),
                  _const_spec(pdq.shape), _const_spec(pdk.shape), _const_spec(ones_q.shape),
                  _const_spec(ones_k.shape)],
        out_specs=[o[2] for o in outs],
        out_shape=[jax.ShapeDtypeStruct(o[0], o[1]) for o in outs],
        scratch_shapes=[pltpu.VMEM((1, 128), F32)],
        compiler_params=_params(56, 1),
    )(x, g1, wcat, bdiag, gq, gk, bfor, tri, place, pdq, pdk, ones_q, ones_k)


def _attn_fwd(qa, ka, vt):
    s_len = qa.shape[0]
    t = ATTN_TILE
    nb = s_len // t

    def body(q_ref, k_ref, vt_ref, o_ref, lse_ref, m_sc, l_sc, acc_sc, s_sc, alpha_sc):
        i = pl.program_id(0)
        m_sc[...] = jnp.full_like(m_sc, -jnp.inf)
        l_sc[...] = jnp.zeros_like(l_sc)
        acc_sc[...] = jnp.zeros_like(acc_sc)

        def tile(j, masked):
            krows = pl.ds(pl.multiple_of(j * t, t), t)
            if masked:
                keep = (lax.broadcasted_iota(jnp.int32, (t, t), 0)
                        <= lax.broadcasted_iota(jnp.int32, (t, t), 1))
            for hd in range(HEADS):
                sl = slice(hd * 128, (hd + 1) * 128)
                st = _dot_nt(k_ref[krows, sl], q_ref[:, sl])
                if masked:
                    st = jnp.where(keep, st, -jnp.inf)
                s_sc[hd] = st
                m_prev = m_sc[hd:hd + 1, :]
                m_new = jnp.maximum(m_prev, jnp.max(st, axis=0, keepdims=True))
                alpha_sc[hd:hd + 1, :] = jnp.exp2(m_prev - m_new)
                m_sc[hd:hd + 1, :] = m_new
            for hd in range(HEADS):
                hr = slice(hd * HEAD_DIM, (hd + 1) * HEAD_DIM)
                alpha = alpha_sc[hd:hd + 1, :]
                pt = jnp.exp2(s_sc[hd] - m_sc[hd:hd + 1, :])
                l_sc[hd:hd + 1, :] = alpha * l_sc[hd:hd + 1, :] + jnp.sum(pt, axis=0, keepdims=True)
                acc_sc[hr, :] = alpha * acc_sc[hr, :] + jnp.dot(
                    vt_ref[j, hr, :], pt.astype(BF16), preferred_element_type=F32)

        def off_diagonal(j, carry):
            tile(j, False)
            return carry

        lax.fori_loop(0, i, off_diagonal, 0)
        tile(i, True)

        for hd in range(HEADS):
            hr = slice(hd * HEAD_DIM, (hd + 1) * HEAD_DIM)
            l = l_sc[hd:hd + 1, :]
            acc_sc[hr, :] = acc_sc[hr, :] / l
            lse_ref[0, hd:hd + 1, :] = m_sc[hd:hd + 1, :] + jnp.log2(l)
        o_ref[...] = acc_sc[...].T.astype(BF16)

    return pl.pallas_call(
        body, name="attn_fwd", grid=(nb,),
        in_specs=[_row_spec(t, SLAB_W), _const_spec(ka.shape), _const_spec(vt.shape)],
        out_specs=[_row_spec(t, FOX_W), _tile_spec(HEADS, t)],
        out_shape=[jax.ShapeDtypeStruct((s_len, FOX_W), BF16),
                   jax.ShapeDtypeStruct((nb, HEADS, t), F32)],
        scratch_shapes=[pltpu.VMEM((HEADS, t), F32), pltpu.VMEM((HEADS, t), F32),
                        pltpu.VMEM((FOX_W, t), F32), pltpu.VMEM((HEADS, t, t), F32),
                        pltpu.VMEM((HEADS, t), F32)],
        compiler_params=_params(48, 1),
    )(qa, ka, vt)


def _sgu_mix(vn, ws_ref):
    tm = vn.shape[0]
    lane = lax.broadcasted_iota(jnp.int32, (WINDOW, 128), 1)
    low = lane < HEAD_DIM
    wins = []
    for w in range(tm // WINDOW):
        slabs = []
        for p in range(GROUPS // 2):
            v2 = vn[w * WINDOW:(w + 1) * WINDOW, p * 128:(p + 1) * 128]
            lo = jnp.where(low, v2, 0.0).astype(BF16)
            hi = jnp.where(low, 0.0, v2).astype(BF16)
            slabs.append(jnp.dot(ws_ref[2 * p], lo, preferred_element_type=F32)
                         + jnp.dot(ws_ref[2 * p + 1], hi, preferred_element_type=F32))
        wins.append(jnp.concatenate(slabs, axis=1))
    return jnp.concatenate(wins, axis=0) if len(wins) > 1 else wins[0]


def _layernorm_fwd(vv, g, b):
    mu = jnp.mean(vv, axis=-1, keepdims=True)
    xc = vv - mu
    r = lax.rsqrt(jnp.mean(xc * xc, axis=-1, keepdims=True) + EPS)
    xh = xc * r
    return xh * g + b, xh, r


def _mix_fwd(attn, uvpre, gpre, x, wa, wb, wout, wsm, bsf, gsgu, bsgu, gpost):
    s_len = x.shape[0]
    tm = TOKEN_TILE

    def body(o_ref, uv_ref, gp_ref, x_ref, wa_ref, wb_ref, wo_ref, ws_ref, bs_ref, gs_ref, bsg_ref,
             gpost_ref, sgu_ref, ya_ref, yb_ref, mg_ref, om_ref, x1_ref):
        uvp = uv_ref[...].astype(F32)
        uv, _ = _gelu_and_grad(uvp)
        u, vv = uv[:, :SGU_W], uv[:, SGU_W:]
        vn, _, _ = _layernorm_fwd(vv, gs_ref[...], bsg_ref[...])
        bias = bs_ref[...]
        if tm > WINDOW:
            bias = jnp.concatenate([bias] * (tm // WINDOW), axis=0)
        mixed = _sgu_mix(vn, ws_ref) + bias
        sgu = (u * mixed).astype(BF16)
        sgu_ref[...] = sgu
        ya = jnp.dot(o_ref[...], wa_ref[...], preferred_element_type=F32)
        yb = jnp.dot(sgu, wb_ref[...], preferred_element_type=F32)
        ya_ref[...] = ya.astype(BF16)
        yb_ref[...] = yb.astype(BF16)
        gates = _sigmoid(gp_ref[...].astype(F32))
        merged = (gates[:, :D_MODEL] * ya + gates[:, D_MODEL:] * yb).astype(BF16)
        mg_ref[...] = merged
        om = jnp.dot(merged, wo_ref[...], preferred_element_type=F32)
        om_ref[...] = om
        r = lax.rsqrt(jnp.mean(om * om, axis=-1, keepdims=True) + EPS)
        x1_ref[...] = x_ref[...] + om * r * gpost_ref[...]

    outs = [(SGU_W, BF16), (D_MODEL, BF16), (D_MODEL, BF16), (D_MODEL, BF16), (D_MODEL, F32),
            (D_MODEL, F32)]
    return pl.pallas_call(
        body, name="mix_fwd", grid=(s_len // tm,),
        in_specs=[_row_spec(tm, FOX_W), _row_spec(tm, 2 * SGU_W), _row_spec(tm, 2 * D_MODEL),
                  _row_spec(tm, D_MODEL), _const_spec(wa.shape), _const_spec(wb.shape),
                  _const_spec(wout.shape), _const_spec(wsm.shape), _const_spec(bsf.shape),
                  _const_spec((1, SGU_W)), _const_spec((1, SGU_W)), _const_spec((1, D_MODEL))],
        out_specs=[_row_spec(tm, c) for c, _ in outs],
        out_shape=[jax.ShapeDtypeStruct((s_len, c), dt) for c, dt in outs],
        compiler_params=_params(48, 1),
    )(attn, uvpre, gpre, x, wa, wb, wout, wsm, bsf, gsgu, bsgu, gpost)


def _ffn_fwd_bwd(x1, tgt, wffn, wdown, gpre, gpost):
    s_len = x1.shape[0]
    tm = TOKEN_TILE

    def body(x1_ref, t_ref, wi_ref, wd_ref, gpre_ref, gpost_ref,
             dx1_ref, h2_ref, act_ref, dff_ref, dgu_ref, loss_ref, dgpost_ref, dgpre_ref):
        @pl.when(pl.program_id(0) == 0)
        def _():
            loss_ref[...] = jnp.zeros_like(loss_ref)
            dgpost_ref[...] = jnp.zeros_like(dgpost_ref)
            dgpre_ref[...] = jnp.zeros_like(dgpre_ref)

        x1v = x1_ref[...]
        r2 = lax.rsqrt(jnp.mean(x1v * x1v, axis=-1, keepdims=True) + EPS)
        gpre_v = gpre_ref[...]
        h2 = (x1v * r2 * gpre_v).astype(BF16)
        h2_ref[...] = h2
        gg = jnp.dot(h2, wi_ref[:, :D_FF], preferred_element_type=F32)
        uu = jnp.dot(h2, wi_ref[:, D_FF:], preferred_element_type=F32)
        sg = _sigmoid(gg)
        silu = gg * sg
        act = (silu * uu).astype(BF16)
        act_ref[...] = act
        ff = jnp.dot(act, wd_ref[...], preferred_element_type=F32)
        r3 = lax.rsqrt(jnp.mean(ff * ff, axis=-1, keepdims=True) + EPS)
        gpost_v = gpost_ref[...]
        y = x1v + ff * r3 * gpost_v
        err = y - t_ref[...]
        loss_ref[...] += jnp.sum(err * err) * (0.5 / D_MODEL)
        dy = err * (1.0 / D_MODEL)
        dgpost_ref[...] += jnp.sum(dy * ff * r3, axis=0, keepdims=True)
        dff = _rms_bwd(ff, r3, gpost_v, dy).astype(BF16)
        dff_ref[...] = dff
        dact = _dot_nt(dff, wd_ref[...])
        dgg = (dact * uu * (sg * (1.0 + gg * (1.0 - sg)))).astype(BF16)
        duu = (dact * silu).astype(BF16)
        dgu_ref[:, :D_FF] = dgg
        dgu_ref[:, D_FF:] = duu
        dh2 = _dot_nt(dgg, wi_ref[:, :D_FF]) + _dot_nt(duu, wi_ref[:, D_FF:])
        dgpre_ref[...] += jnp.sum(dh2 * x1v * r2, axis=0, keepdims=True)
        dx1_ref[...] = dy + _rms_bwd(x1v, r2, gpre_v, dh2)

    outs = [((s_len, D_MODEL), F32, _row_spec(tm, D_MODEL)),
            ((s_len, D_MODEL), BF16, _row_spec(tm, D_MODEL)),
            ((s_len, D_FF), BF16, _row_spec(tm, D_FF)),
            ((s_len, D_MODEL), BF16, _row_spec(tm, D_MODEL)),
            ((s_len, 2 * D_FF), BF16, _row_spec(tm, 2 * D_FF)),
            ((1, 128), F32, _const_spec((1, 128))),
            ((1, D_MODEL), F32, _const_spec((1, D_MODEL))),
            ((1, D_MODEL), F32, _const_spec((1, D_MODEL)))]
    return pl.pallas_call(
        body, name="ffn_fwd_bwd", grid=(s_len // tm,),
        in_specs=[_row_spec(tm, D_MODEL), _row_spec(tm, D_MODEL), _const_spec(wffn.shape),
                  _const_spec(wdown.shape), _const_spec((1, D_MODEL)), _const_spec((1, D_MODEL))],
        out_specs=[o[2] for o in outs],
        out_shape=[jax.ShapeDtypeStruct(o[0], o[1]) for o in outs],
        compiler_params=_params(60, 1),
    )(x1, tgt, wffn, wdown, gpre, gpost)


def _mix_bwd(dx1, om, ya, yb, gpre, uvpre, attn, wout, wa, wb, wsm, wsmt, bsf, gsgu, bsgu, gpost,
             wmask, egrp):
    s_len = dx1.shape[0]
    tm = TOKEN_TILE
    nw = tm // WINDOW
    nt = s_len // tm

    def body(dx1_ref, om_ref, ya_ref, yb_ref, gp_ref, uv_ref, o_ref, wo_ref, wa_ref, wb_ref, ws_ref,
             wst_ref, bs_ref, gs_ref, bsg_ref, gpost_ref, mask_ref, eg_ref,
             dom_ref, dya_ref, dyb_ref, dgp_ref, dot_ref, delta_ref, duv_ref,
             dws_ref, dbs_ref, dgs_ref, dbsg_ref, dgpost_ref, dbs_acc):
        step = pl.program_id(0)

        @pl.when(step == 0)
        def _():
            dws_ref[...] = jnp.zeros_like(dws_ref)
            dbs_acc[...] = jnp.zeros_like(dbs_acc)
            dgs_ref[...] = jnp.zeros_like(dgs_ref)
            dbsg_ref[...] = jnp.zeros_like(dbsg_ref)
            dgpost_ref[...] = jnp.zeros_like(dgpost_ref)

        om = om_ref[...]
        dx1v = dx1_ref[...]
        r = lax.rsqrt(jnp.mean(om * om, axis=-1, keepdims=True) + EPS)
        gpost_v = gpost_ref[...]
        dgpost_ref[...] += jnp.sum(dx1v * om * r, axis=0, keepdims=True)
        dom = _rms_bwd(om, r, gpost_v, dx1v).astype(BF16)
        dom_ref[...] = dom
        dmg = _dot_nt(dom, wo_ref[...])

        gates = _sigmoid(gp_ref[...].astype(F32))
        ga, gb = gates[:, :D_MODEL], gates[:, D_MODEL:]
        yav, ybv = ya_ref[...].astype(F32), yb_ref[...].astype(F32)
        dya = (dmg * ga).astype(BF16)
        dyb = (dmg * gb).astype(BF16)
        dya_ref[...] = dya
        dyb_ref[...] = dyb
        dgp_ref[:, :D_MODEL] = (dmg * yav * ga * (1.0 - ga)).astype(BF16)
        dgp_ref[:, D_MODEL:] = (dmg * ybv * gb * (1.0 - gb)).astype(BF16)

        dat_t = _dot_nt(dya, wa_ref[...]).T.astype(BF16)
        dot_ref[0] = dat_t
        o_t = o_ref[...].astype(F32).T
        delta_ref[0] = jnp.sum((dat_t.astype(F32) * o_t).reshape(HEADS, HEAD_DIM, tm), axis=1)
        dsgu = _dot_nt(dyb, wb_ref[...])

        uvp = uv_ref[...].astype(F32)
        uv, guv = _gelu_and_grad(uvp)
        u, vv = uv[:, :SGU_W], uv[:, SGU_W:]
        gs_v = gs_ref[...]
        vn, xh, rln = _layernorm_fwd(vv, gs_v, bsg_ref[...])
        bias = bs_ref[...]
        if nw > 1:
            bias = jnp.concatenate([bias] * nw, axis=0)
        mixed = _sgu_mix(vn, ws_ref) + bias
        du = dsgu * mixed
        dmixed = dsgu * u

        lane = lax.broadcasted_iota(jnp.int32, (WINDOW, 128), 1)
        low = lane < HEAD_DIM
        dvn_wins = []
        for w in range(nw):
            rows = slice(w * WINDOW, (w + 1) * WINDOW)
            dbs_acc[...] += dmixed[rows, :]
            slabs = []
            for p in range(GROUPS // 2):
                cols = slice(p * 128, (p + 1) * 128)
                dm2 = dmixed[rows, cols]
                dlo = jnp.where(low, dm2, 0.0).astype(BF16)
                dhi = jnp.where(low, 0.0, dm2).astype(BF16)
                vn2 = vn[rows, cols].astype(BF16)
                dws_ref[2 * p] += _dot_nt(dlo, vn2)
                dws_ref[2 * p + 1] += _dot_nt(dhi, vn2)
                slabs.append(jnp.dot(wst_ref[2 * p], dlo, preferred_element_type=F32)
                             + jnp.dot(wst_ref[2 * p + 1], dhi, preferred_element_type=F32))
            dvn_wins.append(jnp.concatenate(slabs, axis=1))
        dvn = jnp.concatenate(dvn_wins, axis=0) if nw > 1 else dvn_wins[0]

        dgs_ref[...] += jnp.sum(dvn * xh, axis=0, keepdims=True)
        dbsg_ref[...] += jnp.sum(dvn, axis=0, keepdims=True)
        dxh = dvn * gs_v
        dvv = rln * (dxh - jnp.mean(dxh, axis=-1, keepdims=True)
                     - xh * jnp.mean(dxh * xh, axis=-1, keepdims=True))
        duv_ref[:, :SGU_W] = (du * guv[:, :SGU_W]).astype(BF16)
        duv_ref[:, SGU_W:] = (dvv * guv[:, SGU_W:]).astype(BF16)

        @pl.when(step == pl.num_programs(0) - 1)
        def _():
            for g in range(GROUPS):
                dws_ref[g] = dws_ref[g] * mask_ref[...]
            dbs_ref[...] = _split3_dot(dbs_acc[...], eg_ref[...])

    rows_out = [((s_len, D_MODEL), BF16, _row_spec(tm, D_MODEL)),
                ((s_len, D_MODEL), BF16, _row_spec(tm, D_MODEL)),
                ((s_len, D_MODEL), BF16, _row_spec(tm, D_MODEL)),
                ((s_len, 2 * D_MODEL), BF16, _row_spec(tm, 2 * D_MODEL)),
                ((nt, FOX_W, tm), BF16, _tile_spec(FOX_W, tm)),
                ((nt, HEADS, tm), F32, _tile_spec(HEADS, tm)),
                ((s_len, 2 * SGU_W), BF16, _row_spec(tm, 2 * SGU_W))]
    acc_out = [((GROUPS, WINDOW, WINDOW), F32), ((WINDOW, 128), F32), ((1, SGU_W), F32),
               ((1, SGU_W), F32), ((1, D_MODEL), F32)]
    return pl.pallas_call(
        body, name="mix_bwd", grid=(nt,),
        in_specs=[_row_spec(tm, D_MODEL), _row_spec(tm, D_MODEL), _row_spec(tm, D_MODEL),
                  _row_spec(tm, D_MODEL), _row_spec(tm, 2 * D_MODEL), _row_spec(tm, 2 * SGU_W),
                  _row_spec(tm, FOX_W), _const_spec(wout.shape), _const_spec(wa.shape),
                  _const_spec(wb.shape), _const_spec(wsm.shape), _const_spec(wsmt.shape),
                  _const_spec(bsf.shape), _const_spec((1, SGU_W)), _const_spec((1, SGU_W)),
                  _const_spec((1, D_MODEL)), _const_spec(wmask.shape), _const_spec(egrp.shape)],
        out_specs=[o[2] for o in rows_out] + [_const_spec(s) for s, _ in acc_out],
        out_shape=[jax.ShapeDtypeStruct(o[0], o[1]) for o in rows_out]
        + [jax.ShapeDtypeStruct(s, dt) for s, dt in acc_out],
        scratch_shapes=[pltpu.VMEM((WINDOW, SGU_W), F32)],
        compiler_params=_params(48, 1),
    )(dx1, om, ya, yb, gpre, uvpre, attn, wout, wa, wb, wsm, wsmt, bsf, gsgu, bsgu, gpost, wmask,
      egrp)


def _attn_bwd(qa, ka, kat, vs, dot_, lse, delta):
    s_len = qa.shape[0]
    t = ATTN_TILE
    nb = s_len // t

    def body(k_ref, kt_ref, vs_ref, q_ref, do_ref, lse_ref, dl_ref, gk_ref, dvt_ref, gqt_ref,
             p_sc, ds_sc):
        j = pl.program_id(0)

        @pl.when(j == 0)
        def _():
            gqt_ref[...] = jnp.zeros_like(gqt_ref)

        gk_ref[...] = jnp.zeros_like(gk_ref)
        dvt_ref[...] = jnp.zeros_like(dvt_ref)

        def tile(i, masked):
            qrows = pl.ds(pl.multiple_of(i * t, t), t)
            if masked:
                keep = (lax.broadcasted_iota(jnp.int32, (t, t), 0)
                        <= lax.broadcasted_iota(jnp.int32, (t, t), 1))
            for hd in range(HEADS):
                sl = slice(hd * 128, (hd + 1) * 128)
                hr = slice(hd * HEAD_DIM, (hd + 1) * HEAD_DIM)
                st = _dot_nt(k_ref[:, sl], q_ref[qrows, sl])
                if masked:
                    st = jnp.where(keep, st, -jnp.inf)
                pt = jnp.exp2(st - lse_ref[i, hd:hd + 1, :])
                dpt = jnp.dot(vs_ref[:, hd * 128:hd * 128 + HEAD_DIM], do_ref[i, hr, :],
                              preferred_element_type=F32)
                p_sc[hd] = pt.astype(BF16)
                ds_sc[hd] = (pt * (dpt - dl_ref[i, hd:hd + 1, :])).astype(BF16)
            for hd in range(HEADS):
                sl = slice(hd * 128, (hd + 1) * 128)
                hr = slice(hd * HEAD_DIM, (hd + 1) * HEAD_DIM)
                dst = ds_sc[hd]
                dvt_ref[0, hr, :] += _dot_nt(do_ref[i, hr, :], p_sc[hd])
                gk_ref[:, sl] += jnp.dot(dst, q_ref[qrows, sl], preferred_element_type=F32)
                gqt_ref[i, hd * QT_ROWS:(hd + 1) * QT_ROWS, :] += jnp.dot(
                    kt_ref[0, hd * 128:hd * 128 + QT_ROWS, :], dst, preferred_element_type=F32)

        tile(j, True)

        def below_diagonal(i, carry):
            tile(i, False)
            return carry

        lax.fori_loop(j + 1, nb, below_diagonal, 0)

    return pl.pallas_call(
        body, name="attn_bwd", grid=(nb,),
        in_specs=[_row_spec(t, SLAB_W), _tile_spec(SLAB_W, t), _row_spec(t, SLAB_W),
                  _const_spec(qa.shape), _const_spec(dot_.shape), _const_spec(lse.shape),
                  _const_spec(delta.shape)],
        out_specs=[_row_spec(t, SLAB_W), _tile_spec(FOX_W, t),
                   _const_spec((nb, HEADS * QT_ROWS, t))],
        out_shape=[jax.ShapeDtypeStruct((s_len, SLAB_W), F32),
                   jax.ShapeDtypeStruct((nb, FOX_W, t), F32),
                   jax.ShapeDtypeStruct((nb, HEADS * QT_ROWS, t), F32)],
        scratch_shapes=[pltpu.VMEM((HEADS, t, t), BF16), pltpu.VMEM((HEADS, t, t), BF16)],
        compiler_params=_params(60, 1),
    )(ka, kat, vs, qa, dot_, lse, delta)


def _rev_cumsum(gk, gqt, triu, ecol):
    s_len = gk.shape[0]
    tm = TOKEN_TILE
    n = s_len // tm

    def body(gk_ref, gqt_ref, tri_ref, ec_ref, o_ref, carry):
        @pl.when(pl.program_id(0) == 0)
        def _():
            carry[...] = jnp.zeros_like(carry)
        col_sums = _split3_dot(gk_ref[...], ec_ref[...])
        rows = [gqt_ref[0, hd * QT_ROWS + HEAD_DIM:hd * QT_ROWS + HEAD_DIM + 1, :]
                for hd in range(HEADS)]
        row_sums = jnp.concatenate(rows + [jnp.zeros((128 - HEADS, tm), F32)], axis=0).T
        out = _tri_dot(tri_ref[...], row_sums - col_sums) + carry[...]
        o_ref[...] = out
        carry[...] = out[0:1, :]

    return pl.pallas_call(
        body, name="rev_cumsum", grid=(n,),
        in_specs=[pl.BlockSpec((tm, SLAB_W), lambda i: (n - 1 - i, 0)),
                  pl.BlockSpec((1, HEADS * QT_ROWS, tm), lambda i: (n - 1 - i, 0, 0)),
                  _const_spec((tm, tm)), _const_spec(ecol.shape)],
        out_specs=pl.BlockSpec((tm, 128), lambda i: (n - 1 - i, 0)),
        out_shape=jax.ShapeDtypeStruct((s_len, 128), F32),
        scratch_shapes=[pltpu.VMEM((1, 128), F32)],
        compiler_params=_params(32, 1),
    )(gk, gqt, triu, ecol)


def _heads_from_slabs(slabs):
    lane = lax.broadcasted_iota(jnp.int32, slabs[0].shape, 1)
    low = lane < HEAD_DIM
    pairs = [jnp.where(low, slabs[2 * p], pltpu.roll(slabs[2 * p + 1], HEAD_DIM, 1))
             for p in range(HEADS // 2)]
    return jnp.concatenate(pairs, axis=1)


def _proj_bwd(gqt, gk, dvt, dlogf, flog, qraw, kraw, duv, dgp, x, dx1, wcat, bdiag, gq, gk_gain, g1,
              efold):
    s_len = x.shape[0]
    tm = TOKEN_TILE

    def body(gqt_ref, gkk_ref, dvt_ref, dlf_ref, flog_ref, qr_ref, kr_ref, duv_ref, dgp_ref, x_ref,
             dx1_ref, w_ref, bd_ref, gq_ref, gk_ref, g1_ref, ef_ref,
             dx_ref, dproj_ref, dgq_ref, dgk_ref, dbf_ref, dg1_ref, gq_acc, gk_acc):
        step = pl.program_id(0)

        @pl.when(step == 0)
        def _():
            gq_acc[...] = jnp.zeros_like(gq_acc)
            gk_acc[...] = jnp.zeros_like(gk_acc)
            dbf_ref[...] = jnp.zeros_like(dbf_ref)
            dg1_ref[...] = jnp.zeros_like(dg1_ref)

        pad = jnp.zeros((128 - QT_ROWS, tm), F32)
        q_slabs = [jnp.concatenate([gqt_ref[0, hd * QT_ROWS:(hd + 1) * QT_ROWS, :], pad], axis=0).T
                   for hd in range(HEADS)]
        dqn = _heads_from_slabs(q_slabs)
        dkn = _heads_from_slabs([gkk_ref[:, hd * 128:(hd + 1) * 128] for hd in range(HEADS)])

        def head_bwd(raw_ref, dn, g_ref, acc):
            raw = raw_ref[...].astype(F32)
            r = lax.rsqrt(_seg_mean(raw * raw, bd_ref) + EPS)
            xhat = raw * r
            acc[0:1, :] += jnp.sum(dn * xhat, axis=0, keepdims=True)
            dyg = dn * g_ref[...]
            return r * (dyg - xhat * _seg_mean(dyg * xhat, bd_ref))

        dproj_ref[:, C_Q:C_K] = head_bwd(qr_ref, dqn * HEAD_DIM ** -0.5, gq_ref, gq_acc).astype(BF16)
        dproj_ref[:, C_K:C_V] = head_bwd(kr_ref, dkn * LN2, gk_ref, gk_acc).astype(BF16)
        dproj_ref[:, C_V:C_F] = dvt_ref[0].T.astype(BF16)
        dfl = dlf_ref[...] * _sigmoid(-flog_ref[...])
        dbf_ref[...] += jnp.sum(dfl, axis=0, keepdims=True)
        dproj_ref[:, C_F:C_UV] = dfl.astype(BF16)
        dproj_ref[:, C_UV:C_G] = duv_ref[...]
        dproj_ref[:, C_G:C_END] = dgp_ref[...]

        dh = _dot_nt(dproj_ref[...], w_ref[...])
        xf = x_ref[...]
        r = lax.rsqrt(jnp.mean(xf * xf, axis=-1, keepdims=True) + EPS)
        dg1_ref[...] += jnp.sum(dh * xf * r, axis=0, keepdims=True)
        dx_ref[...] = dx1_ref[...] + _rms_bwd(xf, r, g1_ref[...], dh)

        @pl.when(step == pl.num_programs(0) - 1)
        def _():
            dgq_ref[...] = _split3_dot(gq_acc[...], ef_ref[...])
            dgk_ref[...] = _split3_dot(gk_acc[...], ef_ref[...])

    outs = [((s_len, D_MODEL), F32, _row_spec(tm, D_MODEL)),
            ((s_len, C_END), BF16, _row_spec(tm, C_END)),
            ((8, 128), F32, _const_spec((8, 128))),
            ((8, 128), F32, _const_spec((8, 128))),
            ((1, 128), F32, _const_spec((1, 128))),
            ((1, D_MODEL), F32, _const_spec((1, D_MODEL)))]
    return pl.pallas_call(
        body, name="proj_bwd", grid=(s_len // tm,),
        in_specs=[_tile_spec(HEADS * QT_ROWS, tm), _row_spec(tm, SLAB_W), _tile_spec(FOX_W, tm),
                  _row_spec(tm, 128), _row_spec(tm, 128), _row_spec(tm, FOX_W),
                  _row_spec(tm, FOX_W), _row_spec(tm, 2 * SGU_W), _row_spec(tm, 2 * D_MODEL),
                  _row_spec(tm, D_MODEL), _row_spec(tm, D_MODEL), _const_spec(wcat.shape),
                  _const_spec(bdiag.shape), _const_spec((1, FOX_W)), _const_spec((1, FOX_W)),
                  _const_spec((1, D_MODEL)), _const_spec(efold.shape)],
        out_specs=[o[2] for o in outs],
        out_shape=[jax.ShapeDtypeStruct(o[0], o[1]) for o in outs],
        scratch_shapes=[pltpu.VMEM((8, FOX_W), F32), pltpu.VMEM((8, FOX_W), F32)],
        compiler_params=_params(56, 1),
    )(gqt, gk, dvt, dlogf, flog, qraw, kraw, duv, dgp, x, dx1, wcat, bdiag, gq, gk_gain, g1, efold)


def _dw_matmul(a, b, tm, name):
    s_len, m = a.shape
    n = b.shape[1]
    tk = min(512, s_len)
    nk = s_len // tk

    def body(a_ref, b_ref, o_ref, acc):
        kk = pl.program_id(1)

        @pl.when(kk == 0)
        def _():
            acc[...] = jnp.zeros_like(acc)
        acc[...] += _dot_tn(a_ref[...], b_ref[...])

        @pl.when(kk == nk - 1)
        def _():
            o_ref[...] = acc[...].astype(BF16)

    return pl.pallas_call(
        body, name=name, grid=(m // tm, nk),
        in_specs=[pl.BlockSpec((tk, tm), lambda i, k: (k, i)),
                  pl.BlockSpec((tk, n), lambda i, k: (k, 0))],
        out_specs=pl.BlockSpec((tm, n), lambda i, k: (i, 0)),
        out_shape=jax.ShapeDtypeStruct((m, n), BF16),
        scratch_shapes=[pltpu.VMEM((tm, n), F32)],
        compiler_params=_params(56, 2),
    )(a, b)


def _adamw(parts, w, m, v, tr, name):
    n, rows, cols = parts.shape
    bc1 = 1.0 - ADAM_B1 ** ADAM_STEP
    bc2 = 1.0 - ADAM_B2 ** ADAM_STEP

    def body(p_ref, w_ref, m_ref, v_ref, g_ref, d_ref, mo_ref, vo_ref):
        g = p_ref[0].astype(F32)
        for idx in range(1, n):
            g = g + p_ref[idx].astype(F32)
        g_ref[...] = g
        mn = ADAM_B1 * m_ref[...] + (1.0 - ADAM_B1) * g
        vn = ADAM_B2 * v_ref[...] + (1.0 - ADAM_B2) * (g * g)
        mo_ref[...] = mn
        vo_ref[...] = vn
        m_hat = mn / bc1
        v_hat = vn / bc2
        d_ref[...] = -ADAM_LR * (m_hat / (jnp.sqrt(v_hat) + ADAM_EPS) + ADAM_WD * w_ref[...])

    spec = pl.BlockSpec((tr, cols), lambda i: (i, 0))
    return pl.pallas_call(
        body, name=name, grid=(rows // tr,),
        in_specs=[pl.BlockSpec((n, tr, cols), lambda i: (0, i, 0)), spec, spec, spec],
        out_specs=[spec] * 4,
        out_shape=[jax.ShapeDtypeStruct((rows, cols), F32)] * 4,
        compiler_params=_params(48, 1),
    )(parts, w, m, v)


def _sum_parts(parts, name):
    n, rows, cols = parts.shape

    def body(p_ref, o_ref):
        g = p_ref[0]
        for idx in range(1, n):
            g = g + p_ref[idx]
        o_ref[...] = g

    return pl.pallas_call(
        body, name=name, out_shape=jax.ShapeDtypeStruct((rows, cols), F32),
        in_specs=[_const_spec(parts.shape)], out_specs=_const_spec((rows, cols)), grid=(1,),
        compiler_params=_params(16, 1),
    )(parts)


SMALL_NAMES = ("g_pre_mix", "b_forget", "g_q", "g_k", "g_sgu", "b_sgu", "w_spatial", "b_spatial",
               "g_post_mix", "g_pre_ffn", "g_post_ffn")
SMALL_TOTAL = N_DEV * SMALL_ROWS * 1024


def _pack_small(d):
    flat = jnp.concatenate([d[k].reshape(-1).astype(F32) for k in SMALL_NAMES])
    flat = jnp.pad(flat, (0, SMALL_TOTAL - flat.shape[0]))
    return flat.reshape(N_DEV * SMALL_ROWS, 1024)


def _unpack_small(packed, shapes):
    flat = packed.reshape(-1)
    out, off = {}, 0
    for k in SMALL_NAMES:
        size = math.prod(shapes[k])
        out[k] = flat[off:off + size].reshape(shapes[k])
        off += size
    return out


def _cols_to_blocks(full, width):
    r = full.shape[0]
    return jnp.transpose(full.reshape(r, N_DEV, width), (1, 0, 2))


def _blocks_to_cols(blocks):
    n, r, width = blocks.shape
    return jnp.transpose(blocks, (1, 0, 2)).reshape(r, n * width)


def kernel(x, g_pre_mix, w_in, b_forget, g_q, g_k, g_sgu, b_sgu, w_spatial, b_spatial, w_branch_a, w_branch_b, w_out, g_post_mix, g_pre_ffn, w_ffn_in, w_ffn_down, g_post_ffn, loss_target, m_g_pre_mix, m_w_in, m_b_forget, m_g_q, m_g_k, m_g_sgu, m_b_sgu, m_w_spatial, m_b_spatial, m_w_branch_a, m_w_branch_b, m_w_out, m_g_post_mix, m_g_pre_ffn, m_w_ffn_in, m_w_ffn_down, m_g_post_ffn, v_g_pre_mix, v_w_in, v_b_forget, v_g_q, v_g_k, v_g_sgu, v_b_sgu, v_w_spatial, v_b_spatial, v_w_branch_a, v_w_branch_b, v_w_out, v_g_post_mix, v_g_pre_ffn, v_w_ffn_in, v_w_ffn_down, v_g_post_ffn):
    big_names = ("w_in", "w_branch_a", "w_branch_b", "w_out", "w_ffn_in", "w_ffn_down")
    weights = dict(g_pre_mix=g_pre_mix, w_in=w_in, b_forget=b_forget, g_q=g_q, g_k=g_k, g_sgu=g_sgu,
                   b_sgu=b_sgu, w_spatial=w_spatial, b_spatial=b_spatial, w_branch_a=w_branch_a,
                   w_branch_b=w_branch_b, w_out=w_out, g_post_mix=g_post_mix, g_pre_ffn=g_pre_ffn,
                   w_ffn_in=w_ffn_in, w_ffn_down=w_ffn_down, g_post_ffn=g_post_ffn)
    mom1 = dict(g_pre_mix=m_g_pre_mix, w_in=m_w_in, b_forget=m_b_forget, g_q=m_g_q, g_k=m_g_k,
                g_sgu=m_g_sgu, b_sgu=m_b_sgu, w_spatial=m_w_spatial, b_spatial=m_b_spatial,
                w_branch_a=m_w_branch_a, w_branch_b=m_w_branch_b, w_out=m_w_out,
                g_post_mix=m_g_post_mix, g_pre_ffn=m_g_pre_ffn, w_ffn_in=m_w_ffn_in,
                w_ffn_down=m_w_ffn_down, g_post_ffn=m_g_post_ffn)
    mom2 = dict(g_pre_mix=v_g_pre_mix, w_in=v_w_in, b_forget=v_b_forget, g_q=v_g_q, g_k=v_g_k,
                g_sgu=v_g_sgu, b_sgu=v_b_sgu, w_spatial=v_w_spatial, b_spatial=v_b_spatial,
                w_branch_a=v_w_branch_a, w_branch_b=v_w_branch_b, w_out=v_w_out,
                g_post_mix=v_g_post_mix, g_pre_ffn=v_g_pre_ffn, w_ffn_in=v_w_ffn_in,
                w_ffn_down=v_w_ffn_down, g_post_ffn=v_g_post_ffn)
    names = list(weights)
    shapes = {k: weights[k].shape for k in names}

    s_len = x.shape[1]
    xs = x.reshape(s_len, D_MODEL)
    tgt = loss_target.reshape(s_len, D_MODEL)

    shards = {k: weights[k][0].astype(BF16) for k in big_names}
    (gat_in, gat_mix, gat_ffn), gat_token = _exchange_start(
        [[shards["w_in"]],
         [shards["w_branch_a"], shards["w_branch_b"], shards["w_out"]],
         [shards["w_ffn_in"], shards["w_ffn_down"]]], "gather_start", gather=True)
    (zone_in,) = _exchange_wait(gat_in, gat_token, "gather_wait_in", gather=True)
    win_full = _blocks_to_cols(_own_block(zone_in, shards["w_in"]))
    f_off = 3 * FOX_W
    u_off = f_off + HEADS
    g_off = u_off + 2 * SGU_W
    wcat = jnp.concatenate([
        win_full[:, :f_off],
        jnp.pad(win_full[:, f_off:u_off], ((0, 0), (0, 128 - HEADS))),
        win_full[:, u_off:g_off], win_full[:, g_off:]], axis=1)

    seg = jnp.arange(FOX_W) // HEAD_DIM
    bdiag = (seg[:, None] == seg[None, :]).astype(BF16)
    tm = TOKEN_TILE
    tril = (jnp.arange(tm)[None, :] <= jnp.arange(tm)[:, None]).astype(BF16)
    triu = tril.T
    egrp = (seg[:, None] == jnp.arange(128)[None, :]).astype(BF16)
    efold = ((jnp.arange(FOX_W) % HEAD_DIM)[:, None] == jnp.arange(128)[None, :]).astype(BF16)
    gq512 = jnp.tile(g_q.reshape(1, HEAD_DIM), (1, HEADS))
    gk512 = jnp.tile(g_k.reshape(1, HEAD_DIM), (1, HEADS))
    bfor = jnp.pad(b_forget.reshape(1, HEADS), ((0, 0), (0, 128 - HEADS)))
    pos = jnp.arange(WINDOW)
    wmask = ((pos[None, :] // CHUNK) <= (pos[:, None] // CHUNK))
    wsm_f = jnp.where(wmask[None], w_spatial[0], 0.0)
    wsm = wsm_f.astype(BF16)
    wsmt = jnp.transpose(wsm_f, (0, 2, 1)).astype(BF16)
    bsf = jnp.repeat(jnp.transpose(b_spatial[0]), HEAD_DIM, axis=1)
    wmask_f = wmask.astype(F32)

    col = jnp.arange(SLAB_W)
    place = ((col[None, :] // 128 == seg[:, None])
             & (col[None, :] % 128 == (jnp.arange(FOX_W) % HEAD_DIM)[:, None])).astype(BF16)
    row128 = jnp.arange(128)

    def d_place(first):
        return jnp.stack([((col[None, :] // 128 == row128[:, None])
                           & (col[None, :] % 128 == first + a)).astype(BF16) for a in range(3)])

    pdq, pdk = d_place(HEAD_DIM), d_place(HEAD_DIM + 3)
    ones_q = ((col % 128 >= HEAD_DIM + 3) & (col % 128 < HEAD_DIM + 6)).astype(F32)[None]
    ones_k = ((col % 128 >= HEAD_DIM) & (col % 128 < HEAD_DIM + 3)).astype(F32)[None]
    ecol = ((col[:, None] // 128 == row128[None, :])
            & (col[:, None] % 128 == HEAD_DIM + 3)).astype(BF16)

    (h, qa, ka, kat, vs, vt, qraw, kraw, flog, uvpre, gpre) = _proj_fwd(
        xs, g_pre_mix, wcat, bdiag, gq512, gk512, bfor, tril, place, pdq, pdk, ones_q, ones_k)
    attn, lse = _attn_fwd(qa, ka, vt)
    zone_a, zone_b, zone_out = _exchange_wait(gat_mix, attn, "gather_wait_mix", gather=True)
    wa = _blocks_to_cols(_own_block(zone_a, shards["w_branch_a"]))
    wb = _blocks_to_cols(_own_block(zone_b, shards["w_branch_b"]))
    wout = _own_block(zone_out, shards["w_out"]).reshape(D_MODEL, D_MODEL)
    sgu, ya, yb, merged, om, x1 = _mix_fwd(attn, uvpre, gpre, xs, wa, wb, wout, wsm, bsf,
                                           g_sgu, b_sgu, g_post_mix)
    zone_ffn, zone_down = _exchange_wait(gat_ffn, x1, "gather_wait_ffn", gather=True)
    wffn = _blocks_to_cols(_own_block(zone_ffn, shards["w_ffn_in"]))
    wdown = _own_block(zone_down, shards["w_ffn_down"]).reshape(D_FF, D_MODEL)
    (dx1, h2, act, dff, dgu, loss_acc, dg_post_ffn, dg_pre_ffn) = _ffn_fwd_bwd(
        x1, tgt, wffn, wdown, g_pre_ffn, g_post_ffn)

    dw_down = _dw_matmul(act, dff, D_FF // 2, "dw_down")
    dw_ffn = _dw_matmul(h2, dgu, 512, "dw_ffn_in")
    parts_ffn = [_cols_to_blocks(dw_ffn, 2 * D_FF // N_DEV),
                 dw_down.reshape(N_DEV, D_FF // N_DEV, D_MODEL)]
    (sct_ffn,), sct_ffn_token = _exchange_start([parts_ffn], "scatter_start_ffn", gather=False)

    (dom, dya, dyb, dgp, dot_, delta, duv, dws, dbs, dg_sgu, db_sgu, dg_post_mix) = _mix_bwd(
        dx1, om, ya, yb, gpre, uvpre, attn, wout, wa, wb, wsm, wsmt, bsf, g_sgu, b_sgu,
        g_post_mix + sct_ffn_token[0:1, 0:1], wmask_f, egrp)
    dw_out = _dw_matmul(merged, dom, 512, "dw_out")
    dw_a = _dw_matmul(attn, dya, 512, "dw_a")
    dw_b = _dw_matmul(sgu, dyb, 512, "dw_b")
    parts_mix = [_cols_to_blocks(dw_a, D_MODEL // N_DEV), _cols_to_blocks(dw_b, D_MODEL // N_DEV),
                 dw_out.reshape(N_DEV, D_MODEL // N_DEV, D_MODEL)]
    (sct_mix,), sct_mix_token = _exchange_start([parts_mix], "scatter_start_mix", gather=False)

    gk_all, dvt, gqt = _attn_bwd(qa, ka, kat, vs, dot_, lse, delta + sct_mix_token[0, 0])
    dlogf = _rev_cumsum(gk_all, gqt, triu, ecol)
    dx, dproj, dgq, dgk, dbf, dg_pre_mix = _proj_bwd(
        gqt, gk_all, dvt, dlogf, flog, qraw, kraw, duv, dgp, xs, dx1, wcat, bdiag, gq512, gk512,
        g_pre_mix, efold)
    dw_cat = _dw_matmul(h, dproj, 512, "dw_in")
    dw_in = jnp.concatenate([dw_cat[:, :C_F], dw_cat[:, C_F:C_F + HEADS], dw_cat[:, C_UV:]], axis=1)

    small_local = dict(
        g_pre_mix=dg_pre_mix, b_forget=dbf[:, :HEADS], g_q=dgq[0:1, :HEAD_DIM],
        g_k=dgk[0:1, :HEAD_DIM], g_sgu=dg_sgu, b_sgu=db_sgu, w_spatial=dws,
        b_spatial=jnp.transpose(dbs[:, :GROUPS]), g_post_mix=dg_post_mix, g_pre_ffn=dg_pre_ffn,
        g_post_ffn=dg_post_ffn)
    small_parts = _pack_small(small_local).reshape(N_DEV, SMALL_ROWS, 1024)

    recv_in, recv_small = _exchange([_cols_to_blocks(dw_in, IN_COLS // N_DEV), small_parts],
                                    "scatter_in_small", gather=False)
    x_pos, y_pos, c_pos = _mesh_pos()
    me = 4 * x_pos + 2 * y_pos + c_pos

    def with_own(zones, parts):
        return [_own_block(z, lax.dynamic_index_in_dim(p, me, 0, keepdims=False))
                for z, p in zip(zones, parts)]

    recv_ffn, recv_down = with_own(
        _exchange_wait(sct_ffn, recv_in, "scatter_wait_ffn", gather=False), parts_ffn)
    recv_a, recv_b, recv_out = with_own(
        _exchange_wait(sct_mix, recv_ffn, "scatter_wait_mix", gather=False), parts_mix)
    received = [recv_in, recv_a, recv_b, recv_out, recv_ffn, recv_down]

    grads, deltas, new_m, new_v = {}, {}, {}, {}
    row_tiles = {"w_in": 128, "w_branch_a": 512, "w_branch_b": 512, "w_out": 128, "w_ffn_in": 128,
                 "w_ffn_down": 352}
    for idx, k in enumerate(big_names):
        g, d, mn, vn = _adamw(received[idx], weights[k][0], mom1[k][0], mom2[k][0], row_tiles[k],
                              "adamw_" + k)
        grads[k], deltas[k], new_m[k], new_v[k] = g[None], d[None], mn[None], vn[None]

    small_sum = _sum_parts(recv_small, "sum_small")
    (small_all,) = _exchange([small_sum], "gather_small", gather=True)
    small_all = small_all.reshape(1, N_DEV * SMALL_ROWS, 1024)
    sg, sd, sm, sv = _adamw(small_all, _pack_small(weights), _pack_small(mom1), _pack_small(mom2),
                            N_DEV * SMALL_ROWS, "adamw_small")
    for dst, packed in ((grads, sg), (deltas, sd), (new_m, sm), (new_v, sv)):
        dst.update(_unpack_small(packed, shapes))

    loss = lax.psum(loss_acc[0, 0], ("x", "y", "c"))
    return (loss, dx.reshape(x.shape), *[grads[k] for k in names], *[deltas[k] for k in names],
            *[new_m[k] for k in names], *[new_v[k] for k in names])
```

```python
import functools
import math

import jax
import jax.numpy as jnp
from jax import lax
from jax.experimental import pallas as pl
from jax.experimental.pallas import tpu as pltpu

F32 = jnp.float32
BF16 = jnp.bfloat16

D_MODEL = 1024
FOX_W = 512
HEADS = 8
HEAD_DIM = 64
SGU_W = 512
GROUPS = 8
WINDOW = 128
CHUNK = 64
D_FF = 2816
IN_COLS = 4616
EPS = 1e-6
N_DEV = 8
LOG2E = 1.4426950408889634
LN2 = 0.6931471805599453

C_Q, C_K, C_V, C_F, C_UV, C_G, C_END = 0, 512, 1024, 1536, 1664, 2688, 4736

ADAM_LR, ADAM_B1, ADAM_B2, ADAM_EPS, ADAM_WD, ADAM_STEP = 0.001, 0.9, 0.999, 1e-08, 0.01, 10

MIB = 1024 * 1024
TOKEN_TILE = 256
ATTN_TILE = 256
SLAB_W = HEADS * 128
QT_ROWS = 72

SMALL_ROWS = 18


def _params(vmem_mib, n_axes):
    return pltpu.CompilerParams(
        dimension_semantics=("arbitrary",) * n_axes, vmem_limit_bytes=vmem_mib * MIB)


def _const_spec(shape):
    nd = len(shape)
    return pl.BlockSpec(shape, lambda *_: (0,) * nd)


def _row_spec(tm, cols):
    return pl.BlockSpec((tm, cols), lambda i: (i, 0))


def _tile_spec(rows, tm):
    return pl.BlockSpec((1, rows, tm), lambda i: (i, 0, 0))


def _split3_dot(x, e):
    x1 = x.astype(BF16)
    r1 = x - x1.astype(F32)
    x2 = r1.astype(BF16)
    x3 = (r1 - x2.astype(F32)).astype(BF16)
    dot = functools.partial(jnp.dot, preferred_element_type=F32)
    return dot(x1, e) + dot(x2, e) + dot(x3, e)


def _tri_dot(tri, x):
    x1 = x.astype(BF16)
    r1 = x - x1.astype(F32)
    x2 = r1.astype(BF16)
    x3 = (r1 - x2.astype(F32)).astype(BF16)
    dot = functools.partial(jnp.dot, preferred_element_type=F32)
    return dot(tri, x1) + dot(tri, x2) + dot(tri, x3)


def _seg_mean(sq, bd_ref):
    hi = sq.astype(BF16)
    lo = (sq - hi.astype(F32)).astype(BF16)
    bd = bd_ref[...]
    s = jnp.dot(hi, bd, preferred_element_type=F32) + jnp.dot(lo, bd, preferred_element_type=F32)
    return s * (1.0 / HEAD_DIM)


def _dot_nt(a, b):
    return lax.dot_general(a, b, (((1,), (1,)), ((), ())), preferred_element_type=F32)


def _dot_tn(a, b):
    return lax.dot_general(a, b, (((0,), (0,)), ((), ())), preferred_element_type=F32)


def _sigmoid(x):
    return 1.0 / (1.0 + jnp.exp(-x))


_GELU_C = math.sqrt(2.0 / math.pi)


def _gelu_and_grad(x):
    inner = _GELU_C * (x + 0.044715 * x * x * x)
    t = jnp.tanh(inner)
    y = 0.5 * x * (1.0 + t)
    dy = 0.5 * (1.0 + t) + 0.5 * x * (1.0 - t * t) * _GELU_C * (1.0 + 3.0 * 0.044715 * x * x)
    return y, dy


def _rms_bwd(xin, r, g, dy):
    dyg = dy * g
    return r * dyg - xin * (r * r * r) * jnp.mean(dyg * xin, axis=-1, keepdims=True)


def _mesh_pos():
    x, y, c = lax.axis_index("x"), lax.axis_index("y"), lax.axis_index("c")
    return x, y, c


def _peer(k):
    x, y, c = _mesh_pos()
    px = (1 - x) if (k >> 2) & 1 else x
    py = (1 - y) if (k >> 1) & 1 else y
    pc = (1 - c) if k & 1 else c
    return (px, py, pc), 4 * px + 2 * py + pc


def _exchange(arrs, name, gather):
    n = len(arrs)
    if gather:
        out_shape = [jax.ShapeDtypeStruct((N_DEV,) + a.shape, a.dtype) for a in arrs]
    else:
        out_shape = [jax.ShapeDtypeStruct(a.shape, a.dtype) for a in arrs]

    def body(*refs):
        ins, outs = refs[:n], refs[n:2 * n]
        send_sems, recv_sems, local_sems = refs[2 * n:]
        x, y, c = _mesh_pos()
        me = 4 * x + 2 * y + c

        def src(a, idx):
            return ins[a] if gather else ins[a].at[idx]

        local = []
        for a in range(n):
            cp = pltpu.make_async_copy(src(a, me), outs[a].at[me], local_sems.at[a])
            cp.start()
            local.append(cp)
        sends = []
        for k in range(1, N_DEV):
            peer, pidx = _peer(k)
            for a in range(n):
                cp = pltpu.make_async_remote_copy(
                    src_ref=src(a, pidx), dst_ref=outs[a].at[me],
                    send_sem=send_sems.at[a, k - 1], recv_sem=recv_sems.at[a, k - 1],
                    device_id=peer, device_id_type=pl.DeviceIdType.MESH)
                cp.start()
                sends.append(cp)
        for k in range(1, N_DEV):
            peer, pidx = _peer(k)
            for a in range(n):
                pltpu.make_async_remote_copy(
                    src_ref=src(a, pidx), dst_ref=outs[a].at[pidx],
                    send_sem=send_sems.at[a, k - 1], recv_sem=recv_sems.at[a, k - 1],
                    device_id=peer, device_id_type=pl.DeviceIdType.MESH).wait_recv()
        for cp in sends:
            cp.wait_send()
        for cp in local:
            cp.wait()

    any_spec = pl.BlockSpec(memory_space=pl.ANY)
    return pl.pallas_call(
        body, name=name, out_shape=out_shape,
        in_specs=[any_spec] * n, out_specs=[any_spec] * n,
        scratch_shapes=[pltpu.SemaphoreType.DMA((n, N_DEV - 1)),
                        pltpu.SemaphoreType.DMA((n, N_DEV - 1)),
                        pltpu.SemaphoreType.DMA((n,))],
    )(*arrs)


def _remote_copy(gather, src_ref, land_ref, send_sem, recv_sem, k, receive_side):
    x, y, c = _mesh_pos()
    me = 4 * x + 2 * y + c
    peer, pidx = _peer(k)
    return pltpu.make_async_remote_copy(
        src_ref=src_ref if gather else src_ref.at[pidx],
        dst_ref=land_ref.at[pidx if receive_side else me],
        send_sem=send_sem, recv_sem=recv_sem,
        device_id=peer, device_id_type=pl.DeviceIdType.MESH)


def _exchange_start(groups, name, gather):
    arrs = [a for g in groups for a in g]
    n, n_groups = len(arrs), len(groups)
    lands = [jax.ShapeDtypeStruct(((N_DEV,) + a.shape) if gather else a.shape, a.dtype)
             for a in arrs]

    def body(*refs):
        srcs, zones = refs[:n], refs[n:2 * n]
        sems = refs[2 * n:2 * n + 2 * n_groups]
        token = refs[-1]
        a = 0
        for gi, g in enumerate(groups):
            send_sems, recv_sems = sems[2 * gi], sems[2 * gi + 1]
            for k in range(1, N_DEV):
                for ai in range(len(g)):
                    slot = ai * (N_DEV - 1) + k - 1
                    _remote_copy(gather, srcs[a + ai], zones[a + ai], send_sems.at[slot],
                                 recv_sems.at[slot], k, False).start()
            a += len(g)
        token[...] = jnp.zeros_like(token)

    hbm = pl.BlockSpec(memory_space=pltpu.HBM)
    sem = pl.BlockSpec(memory_space=pltpu.SEMAPHORE)
    sem_shapes = []
    for g in groups:
        sem_shapes += [pltpu.SemaphoreType.DMA((len(g) * (N_DEV - 1),))] * 2
    outs = pl.pallas_call(
        body, name=name,
        in_specs=[hbm] * (2 * n),
        out_shape=sem_shapes + [pltpu.HBM(a.shape, a.dtype) for a in arrs]
        + [pltpu.HBM(z.shape, z.dtype) for z in lands] + [jax.ShapeDtypeStruct((8, 128), F32)],
        out_specs=[sem] * (2 * n_groups) + [hbm] * (2 * n)
        + [pl.BlockSpec(memory_space=pltpu.VMEM)],
        input_output_aliases={i: 2 * n_groups + i for i in range(2 * n)},
        compiler_params=pltpu.CompilerParams(
            has_side_effects=pltpu.SideEffectType.DATAFLOW_SIDE_EFFECTING),
    )(*[pltpu.with_memory_space_constraint(a, pltpu.HBM) for a in arrs],
      *[pltpu.with_memory_space_constraint(lax.empty(z.shape, z.dtype), pltpu.HBM) for z in lands])
    sems = outs[:2 * n_groups]
    thru = outs[2 * n_groups:2 * n_groups + n]
    zones = outs[2 * n_groups + n:2 * n_groups + 2 * n]
    handles, a = [], 0
    for gi, g in enumerate(groups):
        handles.append((sems[2 * gi], sems[2 * gi + 1], thru[a:a + len(g)], zones[a:a + len(g)]))
        a += len(g)
    return handles, outs[-1]


def _exchange_wait(handle, after, name, gather):
    send_sems, recv_sems, thru, zones = handle
    n = len(thru)

    def body(*refs):
        srcs, lands = refs[:n], refs[n:2 * n]
        ssem, rsem = refs[2 * n], refs[2 * n + 1]
        for k in range(1, N_DEV):
            for ai in range(n):
                slot = ai * (N_DEV - 1) + k - 1
                cp = _remote_copy(gather, srcs[ai], lands[ai], ssem.at[slot], rsem.at[slot], k, True)
                cp.wait_send()
                cp.wait_recv()

    hbm = pl.BlockSpec(memory_space=pltpu.HBM)
    sem = pl.BlockSpec(memory_space=pltpu.SEMAPHORE)
    outs = pl.pallas_call(
        body, name=name,
        in_specs=[hbm] * (2 * n) + [sem, sem, pl.BlockSpec(memory_space=pl.ANY)],
        out_shape=[pltpu.HBM(a.shape, a.dtype) for a in thru]
        + [pltpu.HBM(z.shape, z.dtype) for z in zones],
        out_specs=[hbm] * (2 * n),
        input_output_aliases={i: i for i in range(2 * n)},
        compiler_params=pltpu.CompilerParams(
            has_side_effects=pltpu.SideEffectType.DATAFLOW_SIDE_EFFECTING),
    )(*thru, *zones, send_sems, recv_sems, after)
    return outs[n:]


def _own_block(zone, block):
    x, y, c = _mesh_pos()
    me = 4 * x + 2 * y + c
    return lax.dynamic_update_slice_in_dim(zone, block[None], me, axis=0)


def _proj_fwd(x, g1, wcat, bdiag, gq, gk, bfor, tri, place, pdq, pdk, ones_q, ones_k):
    s_len = x.shape[0]
    tm = TOKEN_TILE
    nt = s_len // tm

    def body(x_ref, g1_ref, w_ref, bd_ref, gq_ref, gk_ref, bf_ref, tri_ref, pl_ref, pdq_ref,
             pdk_ref, oq_ref, ok_ref,
             h_ref, qa_ref, ka_ref, kat_ref, vs_ref, vt_ref, qr_ref, kr_ref, flog_ref, uv_ref,
             gp_ref, carry):
        @pl.when(pl.program_id(0) == 0)
        def _():
            carry[...] = jnp.zeros_like(carry)

        xf = x_ref[...]
        r = lax.rsqrt(jnp.mean(xf * xf, axis=-1, keepdims=True) + EPS)
        h = (xf * r * g1_ref[...]).astype(BF16)
        h_ref[...] = h
        dot = functools.partial(jnp.dot, preferred_element_type=F32)

        def proj(lo, hi):
            return _dot_nt(h, w_ref[lo:hi, :])

        flog = proj(C_F, C_UV) + bf_ref[...]
        flog_ref[...] = flog
        lane = lax.broadcasted_iota(jnp.int32, flog.shape, 1)
        logf = jnp.minimum(flog, 0.0) - jnp.log(1.0 + jnp.exp(-jnp.abs(flog)))
        logf = jnp.where(lane < HEADS, logf, 0.0)
        dcum = _tri_dot(tri_ref[...], logf) + carry[...]
        carry[...] = dcum[tm - 1:tm, :]
        d2 = dcum * LOG2E
        d2a = d2.astype(BF16)
        rem = d2 - d2a.astype(F32)
        d2b = rem.astype(BF16)
        d2c = (rem - d2b.astype(F32)).astype(BF16)

        place_m = pl_ref[...]
        q = proj(C_Q, C_K)
        qr_ref[...] = q.astype(BF16)
        rq = lax.rsqrt(_seg_mean(q * q, bd_ref) + EPS)
        qn = (q * rq * (gq_ref[...] * (HEAD_DIM ** -0.5 * LOG2E))).astype(BF16)
        qa = (dot(qn, place_m) + dot(d2a, pdq_ref[0]) + dot(d2b, pdq_ref[1])
              + dot(d2c, pdq_ref[2]) + oq_ref[...])
        qa_ref[...] = qa.astype(BF16)

        k = proj(C_K, C_V)
        kr_ref[...] = k.astype(BF16)
        rk = lax.rsqrt(_seg_mean(k * k, bd_ref) + EPS)
        kn = (k * rk * gk_ref[...]).astype(BF16)
        ka = (dot(kn, place_m) - dot(d2a, pdk_ref[0]) - dot(d2b, pdk_ref[1])
              - dot(d2c, pdk_ref[2]) + ok_ref[...])
        ka_ref[...] = ka.astype(BF16)
        kat_ref[0] = ka.T.astype(BF16)

        v = proj(C_V, C_F)
        vs_ref[...] = dot(v.astype(BF16), place_m).astype(BF16)
        vt_ref[0] = v.T.astype(BF16)
        uv_ref[...] = proj(C_UV, C_G).astype(BF16)
        gp_ref[...] = proj(C_G, C_END).astype(BF16)

    outs = [((s_len, D_MODEL), BF16, _row_spec(tm, D_MODEL)),
            ((s_len, SLAB_W), BF16, _row_spec(tm, SLAB_W)),
            ((s_len, SLAB_W), BF16, _row_spec(tm, SLAB_W)),
            ((nt, SLAB_W, tm), BF16, _tile_spec(SLAB_W, tm)),
            ((s_len, SLAB_W), BF16, _row_spec(tm, SLAB_W)),
            ((nt, FOX_W, tm), BF16, _tile_spec(FOX_W, tm)),
            ((s_len, FOX_W), BF16, _row_spec(tm, FOX_W)),
            ((s_len, FOX_W), BF16, _row_spec(tm, FOX_W)),
            ((s_len, 128), F32, _row_spec(tm, 128)),
            ((s_len, 2 * SGU_W), BF16, _row_spec(tm, 2 * SGU_W)),
            ((s_len, 2 * D_MODEL), BF16, _row_spec(tm, 2 * D_MODEL))]
    return pl.pallas_call(
        body, name="proj_fwd", grid=(nt,),
        in_specs=[_row_spec(tm, D_MODEL), _const_spec((1, D_MODEL)), _const_spec(wcat.shape),
                  _const_spec(bdiag.shape), _const_spec((1, FOX_W)), _const_spec((1, FOX_W)),
                  _const_spec((1, 128)), _const_spec((tm, tm)), _const_spec(place.shape),
                  _const_spec(pdq.shape), _const_spec(pdk.shape), _const_spec(ones_q.shape),
                  _const_spec(ones_k.shape)],
        out_specs=[o[2] for o in outs],
        out_shape=[jax.ShapeDtypeStruct(o[0], o[1]) for o in outs],
        scratch_shapes=[pltpu.VMEM((1, 128), F32)],
        compiler_params=_params(56, 1),
    )(x, g1, wcat, bdiag, gq, gk, bfor, tri, place, pdq, pdk, ones_q, ones_k)


def _attn_fwd(qa, ka, vt):
    s_len = qa.shape[0]
    t = ATTN_TILE
    nb = s_len // t

    def body(q_ref, k_ref, vt_ref, o_ref, lse_ref, m_sc, l_sc, acc_sc, s_sc, alpha_sc):
        i = pl.program_id(0)
        m_sc[...] = jnp.full_like(m_sc, -jnp.inf)
        l_sc[...] = jnp.zeros_like(l_sc)
        acc_sc[...] = jnp.zeros_like(acc_sc)

        def tile(j, masked):
            krows = pl.ds(pl.multiple_of(j * t, t), t)
            if masked:
                keep = (lax.broadcasted_iota(jnp.int32, (t, t), 0)
                        <= lax.broadcasted_iota(jnp.int32, (t, t), 1))
            for hd in range(HEADS):
                sl = slice(hd * 128, (hd + 1) * 128)
                st = _dot_nt(k_ref[krows, sl], q_ref[:, sl])
                if masked:
                    st = jnp.where(keep, st, -jnp.inf)
                s_sc[hd] = st
                m_prev = m_sc[hd:hd + 1, :]
                m_new = jnp.maximum(m_prev, jnp.max(st, axis=0, keepdims=True))
                alpha_sc[hd:hd + 1, :] = jnp.exp2(m_prev - m_new)
                m_sc[hd:hd + 1, :] = m_new
            for hd in range(HEADS):
                hr = slice(hd * HEAD_DIM, (hd + 1) * HEAD_DIM)
                alpha = alpha_sc[hd:hd + 1, :]
                pt = jnp.exp2(s_sc[hd] - m_sc[hd:hd + 1, :])
                l_sc[hd:hd + 1, :] = alpha * l_sc[hd:hd + 1, :] + jnp.sum(pt, axis=0, keepdims=True)
                acc_sc[hr, :] = alpha * acc_sc[hr, :] + jnp.dot(
                    vt_ref[j, hr, :], pt.astype(BF16), preferred_element_type=F32)

        def off_diagonal(j, carry):
            tile(j, False)
            return carry

        lax.fori_loop(0, i, off_diagonal, 0)
        tile(i, True)

        for hd in range(HEADS):
            hr = slice(hd * HEAD_DIM, (hd + 1) * HEAD_DIM)
            l = l_sc[hd:hd + 1, :]
            acc_sc[hr, :] = acc_sc[hr, :] / l
            lse_ref[0, hd:hd + 1, :] = m_sc[hd:hd + 1, :] + jnp.log2(l)
        o_ref[...] = acc_sc[...].T.astype(BF16)

    return pl.pallas_call(
        body, name="attn_fwd", grid=(nb,),
        in_specs=[_row_spec(t, SLAB_W), _const_spec(ka.shape), _const_spec(vt.shape)],
        out_specs=[_row_spec(t, FOX_W), _tile_spec(HEADS, t)],
        out_shape=[jax.ShapeDtypeStruct((s_len, FOX_W), BF16),
                   jax.ShapeDtypeStruct((nb, HEADS, t), F32)],
        scratch_shapes=[pltpu.VMEM((HEADS, t), F32), pltpu.VMEM((HEADS, t), F32),
                        pltpu.VMEM((FOX_W, t), F32), pltpu.VMEM((HEADS, t, t), F32),
                        pltpu.VMEM((HEADS, t), F32)],
        compiler_params=_params(48, 1),
    )(qa, ka, vt)


def _sgu_mix(vn, ws_ref):
    tm = vn.shape[0]
    lane = lax.broadcasted_iota(jnp.int32, (WINDOW, 128), 1)
    low = lane < HEAD_DIM
    wins = []
    for w in range(tm // WINDOW):
        slabs = []
        for p in range(GROUPS // 2):
            v2 = vn[w * WINDOW:(w + 1) * WINDOW, p * 128:(p + 1) * 128]
            lo = jnp.where(low, v2, 0.0).astype(BF16)
            hi = jnp.where(low, 0.0, v2).astype(BF16)
            slabs.append(jnp.dot(ws_ref[2 * p], lo, preferred_element_type=F32)
                         + jnp.dot(ws_ref[2 * p + 1], hi, preferred_element_type=F32))
        wins.append(jnp.concatenate(slabs, axis=1))
    return jnp.concatenate(wins, axis=0) if len(wins) > 1 else wins[0]


def _layernorm_fwd(vv, g, b):
    mu = jnp.mean(vv, axis=-1, keepdims=True)
    xc = vv - mu
    r = lax.rsqrt(jnp.mean(xc * xc, axis=-1, keepdims=True) + EPS)
    xh = xc * r
    return xh * g + b, xh, r


def _mix_fwd(attn, uvpre, gpre, x, wa, wb, wout, wsm, bsf, gsgu, bsgu, gpost):
    s_len = x.shape[0]
    tm = TOKEN_TILE

    def body(o_ref, uv_ref, gp_ref, x_ref, wa_ref, wb_ref, wo_ref, ws_ref, bs_ref, gs_ref, bsg_ref,
             gpost_ref, sgu_ref, ya_ref, yb_ref, mg_ref, om_ref, x1_ref):
        uvp = uv_ref[...].astype(F32)
        uv, _ = _gelu_and_grad(uvp)
        u, vv = uv[:, :SGU_W], uv[:, SGU_W:]
        vn, _, _ = _layernorm_fwd(vv, gs_ref[...], bsg_ref[...])
        bias = bs_ref[...]
        if tm > WINDOW:
            bias = jnp.concatenate([bias] * (tm // WINDOW), axis=0)
        mixed = _sgu_mix(vn, ws_ref) + bias
        sgu = (u * mixed).astype(BF16)
        sgu_ref[...] = sgu
        ya = jnp.dot(o_ref[...], wa_ref[...], preferred_element_type=F32)
        yb = jnp.dot(sgu, wb_ref[...], preferred_element_type=F32)
        ya_ref[...] = ya.astype(BF16)
        yb_ref[...] = yb.astype(BF16)
        gates = _sigmoid(gp_ref[...].astype(F32))
        merged = (gates[:, :D_MODEL] * ya + gates[:, D_MODEL:] * yb).astype(BF16)
        mg_ref[...] = merged
        om = jnp.dot(merged, wo_ref[...], preferred_element_type=F32)
        om_ref[...] = om
        r = lax.rsqrt(jnp.mean(om * om, axis=-1, keepdims=True) + EPS)
        x1_ref[...] = x_ref[...] + om * r * gpost_ref[...]

    outs = [(SGU_W, BF16), (D_MODEL, BF16), (D_MODEL, BF16), (D_MODEL, BF16), (D_MODEL, F32),
            (D_MODEL, F32)]
    return pl.pallas_call(
        body, name="mix_fwd", grid=(s_len // tm,),
        in_specs=[_row_spec(tm, FOX_W), _row_spec(tm, 2 * SGU_W), _row_spec(tm, 2 * D_MODEL),
                  _row_spec(tm, D_MODEL), _const_spec(wa.shape), _const_spec(wb.shape),
                  _const_spec(wout.shape), _const_spec(wsm.shape), _const_spec(bsf.shape),
                  _const_spec((1, SGU_W)), _const_spec((1, SGU_W)), _const_spec((1, D_MODEL))],
        out_specs=[_row_spec(tm, c) for c, _ in outs],
        out_shape=[jax.ShapeDtypeStruct((s_len, c), dt) for c, dt in outs],
        compiler_params=_params(48, 1),
    )(attn, uvpre, gpre, x, wa, wb, wout, wsm, bsf, gsgu, bsgu, gpost)


def _ffn_fwd_bwd(x1, tgt, wffn, wdown, gpre, gpost):
    s_len = x1.shape[0]
    tm = TOKEN_TILE

    def body(x1_ref, t_ref, wi_ref, wd_ref, gpre_ref, gpost_ref,
             dx1_ref, h2_ref, act_ref, dff_ref, dgu_ref, loss_ref, dgpost_ref, dgpre_ref):
        @pl.when(pl.program_id(0) == 0)
        def _():
            loss_ref[...] = jnp.zeros_like(loss_ref)
            dgpost_ref[...] = jnp.zeros_like(dgpost_ref)
            dgpre_ref[...] = jnp.zeros_like(dgpre_ref)

        x1v = x1_ref[...]
        r2 = lax.rsqrt(jnp.mean(x1v * x1v, axis=-1, keepdims=True) + EPS)
        gpre_v = gpre_ref[...]
        h2 = (x1v * r2 * gpre_v).astype(BF16)
        h2_ref[...] = h2
        gg = _dot_nt(h2, wi_ref[:D_FF, :])
        uu = _dot_nt(h2, wi_ref[D_FF:, :])
        sg = _sigmoid(gg)
        silu = gg * sg
        act = (silu * uu).astype(BF16)
        act_ref[...] = act
        ff = jnp.dot(act, wd_ref[...], preferred_element_type=F32)
        r3 = lax.rsqrt(jnp.mean(ff * ff, axis=-1, keepdims=True) + EPS)
        gpost_v = gpost_ref[...]
        y = x1v + ff * r3 * gpost_v
        err = y - t_ref[...]
        loss_ref[...] += jnp.sum(err * err) * (0.5 / D_MODEL)
        dy = err * (1.0 / D_MODEL)
        dgpost_ref[...] += jnp.sum(dy * ff * r3, axis=0, keepdims=True)
        dff = _rms_bwd(ff, r3, gpost_v, dy).astype(BF16)
        dff_ref[...] = dff
        dact = _dot_nt(dff, wd_ref[...])
        dgg = (dact * uu * (sg * (1.0 + gg * (1.0 - sg)))).astype(BF16)
        duu = (dact * silu).astype(BF16)
        dgu_ref[:, :D_FF] = dgg
        dgu_ref[:, D_FF:] = duu
        dh2 = (jnp.dot(dgg, wi_ref[:D_FF, :], preferred_element_type=F32)
               + jnp.dot(duu, wi_ref[D_FF:, :], preferred_element_type=F32))
        dgpre_ref[...] += jnp.sum(dh2 * x1v * r2, axis=0, keepdims=True)
        dx1_ref[...] = dy + _rms_bwd(x1v, r2, gpre_v, dh2)

    outs = [((s_len, D_MODEL), F32, _row_spec(tm, D_MODEL)),
            ((s_len, D_MODEL), BF16, _row_spec(tm, D_MODEL)),
            ((s_len, D_FF), BF16, _row_spec(tm, D_FF)),
            ((s_len, D_MODEL), BF16, _row_spec(tm, D_MODEL)),
            ((s_len, 2 * D_FF), BF16, _row_spec(tm, 2 * D_FF)),
            ((1, 128), F32, _const_spec((1, 128))),
            ((1, D_MODEL), F32, _const_spec((1, D_MODEL))),
            ((1, D_MODEL), F32, _const_spec((1, D_MODEL)))]
    return pl.pallas_call(
        body, name="ffn_fwd_bwd", grid=(s_len // tm,),
        in_specs=[_row_spec(tm, D_MODEL), _row_spec(tm, D_MODEL), _const_spec(wffn.shape),
                  _const_spec(wdown.shape), _const_spec((1, D_MODEL)), _const_spec((1, D_MODEL))],
        out_specs=[o[2] for o in outs],
        out_shape=[jax.ShapeDtypeStruct(o[0], o[1]) for o in outs],
        compiler_params=_params(60, 1),
    )(x1, tgt, wffn, wdown, gpre, gpost)


def _mix_bwd(dx1, om, ya, yb, gpre, uvpre, attn, wout, wa, wb, wsm, wsmt, bsf, gsgu, bsgu, gpost,
             wmask, egrp):
    s_len = dx1.shape[0]
    tm = TOKEN_TILE
    nw = tm // WINDOW
    nt = s_len // tm

    def body(dx1_ref, om_ref, ya_ref, yb_ref, gp_ref, uv_ref, o_ref, wo_ref, wa_ref, wb_ref, ws_ref,
             wst_ref, bs_ref, gs_ref, bsg_ref, gpost_ref, mask_ref, eg_ref,
             dom_ref, dya_ref, dyb_ref, dgp_ref, dot_ref, delta_ref, duv_ref,
             dws_ref, dbs_ref, dgs_ref, dbsg_ref, dgpost_ref, dbs_acc):
        step = pl.program_id(0)

        @pl.when(step == 0)
        def _():
            dws_ref[...] = jnp.zeros_like(dws_ref)
            dbs_acc[...] = jnp.zeros_like(dbs_acc)
            dgs_ref[...] = jnp.zeros_like(dgs_ref)
            dbsg_ref[...] = jnp.zeros_like(dbsg_ref)
            dgpost_ref[...] = jnp.zeros_like(dgpost_ref)

        om = om_ref[...]
        dx1v = dx1_ref[...]
        r = lax.rsqrt(jnp.mean(om * om, axis=-1, keepdims=True) + EPS)
        gpost_v = gpost_ref[...]
        dgpost_ref[...] += jnp.sum(dx1v * om * r, axis=0, keepdims=True)
        dom = _rms_bwd(om, r, gpost_v, dx1v).astype(BF16)
        dom_ref[...] = dom
        dmg = _dot_nt(dom, wo_ref[...])

        gates = _sigmoid(gp_ref[...].astype(F32))
        ga, gb = gates[:, :D_MODEL], gates[:, D_MODEL:]
        yav, ybv = ya_ref[...].astype(F32), yb_ref[...].astype(F32)
        dya = (dmg * ga).astype(BF16)
        dyb = (dmg * gb).astype(BF16)
        dya_ref[...] = dya
        dyb_ref[...] = dyb
        dgp_ref[:, :D_MODEL] = (dmg * yav * ga * (1.0 - ga)).astype(BF16)
        dgp_ref[:, D_MODEL:] = (dmg * ybv * gb * (1.0 - gb)).astype(BF16)

        dat_t = _dot_nt(dya, wa_ref[...]).T.astype(BF16)
        dot_ref[0] = dat_t
        o_t = o_ref[...].astype(F32).T
        delta_ref[0] = jnp.sum((dat_t.astype(F32) * o_t).reshape(HEADS, HEAD_DIM, tm), axis=1)
        dsgu = _dot_nt(dyb, wb_ref[...])

        uvp = uv_ref[...].astype(F32)
        uv, guv = _gelu_and_grad(uvp)
        u, vv = uv[:, :SGU_W], uv[:, SGU_W:]
        gs_v = gs_ref[...]
        vn, xh, rln = _layernorm_fwd(vv, gs_v, bsg_ref[...])
        bias = bs_ref[...]
        if nw > 1:
            bias = jnp.concatenate([bias] * nw, axis=0)
        mixed = _sgu_mix(vn, ws_ref) + bias
        du = dsgu * mixed
        dmixed = dsgu * u

        lane = lax.broadcasted_iota(jnp.int32, (WINDOW, 128), 1)
        low = lane < HEAD_DIM
        dvn_wins = []
        for w in range(nw):
            rows = slice(w * WINDOW, (w + 1) * WINDOW)
            dbs_acc[...] += dmixed[rows, :]
            slabs = []
            for p in range(GROUPS // 2):
                cols = slice(p * 128, (p + 1) * 128)
                dm2 = dmixed[rows, cols]
                dlo = jnp.where(low, dm2, 0.0).astype(BF16)
                dhi = jnp.where(low, 0.0, dm2).astype(BF16)
                vn2 = vn[rows, cols].astype(BF16)
                dws_ref[2 * p] += _dot_nt(dlo, vn2)
                dws_ref[2 * p + 1] += _dot_nt(dhi, vn2)
                slabs.append(jnp.dot(wst_ref[2 * p], dlo, preferred_element_type=F32)
                             + jnp.dot(wst_ref[2 * p + 1], dhi, preferred_element_type=F32))
            dvn_wins.append(jnp.concatenate(slabs, axis=1))
        dvn = jnp.concatenate(dvn_wins, axis=0) if nw > 1 else dvn_wins[0]

        dgs_ref[...] += jnp.sum(dvn * xh, axis=0, keepdims=True)
        dbsg_ref[...] += jnp.sum(dvn, axis=0, keepdims=True)
        dxh = dvn * gs_v
        dvv = rln * (dxh - jnp.mean(dxh, axis=-1, keepdims=True)
                     - xh * jnp.mean(dxh * xh, axis=-1, keepdims=True))
        duv_ref[:, :SGU_W] = (du * guv[:, :SGU_W]).astype(BF16)
        duv_ref[:, SGU_W:] = (dvv * guv[:, SGU_W:]).astype(BF16)

        @pl.when(step == pl.num_programs(0) - 1)
        def _():
            for g in range(GROUPS):
                dws_ref[g] = dws_ref[g] * mask_ref[...]
            dbs_ref[...] = _split3_dot(dbs_acc[...], eg_ref[...])

    rows_out = [((s_len, D_MODEL), BF16, _row_spec(tm, D_MODEL)),
                ((s_len, D_MODEL), BF16, _row_spec(tm, D_MODEL)),
                ((s_len, D_MODEL), BF16, _row_spec(tm, D_MODEL)),
                ((s_len, 2 * D_MODEL), BF16, _row_spec(tm, 2 * D_MODEL)),
                ((nt, FOX_W, tm), BF16, _tile_spec(FOX_W, tm)),
                ((nt, HEADS, tm), F32, _tile_spec(HEADS, tm)),
                ((s_len, 2 * SGU_W), BF16, _row_spec(tm, 2 * SGU_W))]
    acc_out = [((GROUPS, WINDOW, WINDOW), F32), ((WINDOW, 128), F32), ((1, SGU_W), F32),
               ((1, SGU_W), F32), ((1, D_MODEL), F32)]
    return pl.pallas_call(
        body, name="mix_bwd", grid=(nt,),
        in_specs=[_row_spec(tm, D_MODEL), _row_spec(tm, D_MODEL), _row_spec(tm, D_MODEL),
                  _row_spec(tm, D_MODEL), _row_spec(tm, 2 * D_MODEL), _row_spec(tm, 2 * SGU_W),
                  _row_spec(tm, FOX_W), _const_spec(wout.shape), _const_spec(wa.shape),
                  _const_spec(wb.shape), _const_spec(wsm.shape), _const_spec(wsmt.shape),
                  _const_spec(bsf.shape), _const_spec((1, SGU_W)), _const_spec((1, SGU_W)),
                  _const_spec((1, D_MODEL)), _const_spec(wmask.shape), _const_spec(egrp.shape)],
        out_specs=[o[2] for o in rows_out] + [_const_spec(s) for s, _ in acc_out],
        out_shape=[jax.ShapeDtypeStruct(o[0], o[1]) for o in rows_out]
        + [jax.ShapeDtypeStruct(s, dt) for s, dt in acc_out],
        scratch_shapes=[pltpu.VMEM((WINDOW, SGU_W), F32)],
        compiler_params=_params(48, 1),
    )(dx1, om, ya, yb, gpre, uvpre, attn, wout, wa, wb, wsm, wsmt, bsf, gsgu, bsgu, gpost, wmask,
      egrp)


def _attn_bwd(qa, ka, kat, vs, dot_, lse, delta):
    s_len = qa.shape[0]
    t = ATTN_TILE
    nb = s_len // t

    def body(k_ref, kt_ref, vs_ref, q_ref, do_ref, lse_ref, dl_ref, gk_ref, dvt_ref, gqt_ref,
             p_sc, ds_sc):
        j = pl.program_id(0)

        @pl.when(j == 0)
        def _():
            gqt_ref[...] = jnp.zeros_like(gqt_ref)

        gk_ref[...] = jnp.zeros_like(gk_ref)
        dvt_ref[...] = jnp.zeros_like(dvt_ref)

        def tile(i, masked):
            qrows = pl.ds(pl.multiple_of(i * t, t), t)
            if masked:
                keep = (lax.broadcasted_iota(jnp.int32, (t, t), 0)
                        <= lax.broadcasted_iota(jnp.int32, (t, t), 1))
            for hd in range(HEADS):
                sl = slice(hd * 128, (hd + 1) * 128)
                hr = slice(hd * HEAD_DIM, (hd + 1) * HEAD_DIM)
                st = _dot_nt(k_ref[:, sl], q_ref[qrows, sl])
                if masked:
                    st = jnp.where(keep, st, -jnp.inf)
                pt = jnp.exp2(st - lse_ref[i, hd:hd + 1, :])
                dpt = jnp.dot(vs_ref[:, hd * 128:hd * 128 + HEAD_DIM], do_ref[i, hr, :],
                              preferred_element_type=F32)
                p_sc[hd] = pt.astype(BF16)
                ds_sc[hd] = (pt * (dpt - dl_ref[i, hd:hd + 1, :])).astype(BF16)
            for hd in range(HEADS):
                sl = slice(hd * 128, (hd + 1) * 128)
                hr = slice(hd * HEAD_DIM, (hd + 1) * HEAD_DIM)
                dst = ds_sc[hd]
                dvt_ref[0, hr, :] += _dot_nt(do_ref[i, hr, :], p_sc[hd])
                gk_ref[:, sl] += jnp.dot(dst, q_ref[qrows, sl], preferred_element_type=F32)
                gqt_ref[i, hd * QT_ROWS:(hd + 1) * QT_ROWS, :] += jnp.dot(
                    kt_ref[0, hd * 128:hd * 128 + QT_ROWS, :], dst, preferred_element_type=F32)

        tile(j, True)

        def below_diagonal(i, carry):
            tile(i, False)
            return carry

        lax.fori_loop(j + 1, nb, below_diagonal, 0)

    return pl.pallas_call(
        body, name="attn_bwd", grid=(nb,),
        in_specs=[_row_spec(t, SLAB_W), _tile_spec(SLAB_W, t), _row_spec(t, SLAB_W),
                  _const_spec(qa.shape), _const_spec(dot_.shape), _const_spec(lse.shape),
                  _const_spec(delta.shape)],
        out_specs=[_row_spec(t, SLAB_W), _tile_spec(FOX_W, t),
                   _const_spec((nb, HEADS * QT_ROWS, t))],
        out_shape=[jax.ShapeDtypeStruct((s_len, SLAB_W), F32),
                   jax.ShapeDtypeStruct((nb, FOX_W, t), F32),
                   jax.ShapeDtypeStruct((nb, HEADS * QT_ROWS, t), F32)],
        scratch_shapes=[pltpu.VMEM((HEADS, t, t), BF16), pltpu.VMEM((HEADS, t, t), BF16)],
        compiler_params=_params(60, 1),
    )(ka, kat, vs, qa, dot_, lse, delta)


def _rev_cumsum(gk, gqt, triu, ecol):
    s_len = gk.shape[0]
    tm = TOKEN_TILE
    n = s_len // tm

    def body(gk_ref, gqt_ref, tri_ref, ec_ref, o_ref, carry):
        @pl.when(pl.program_id(0) == 0)
        def _():
            carry[...] = jnp.zeros_like(carry)
        col_sums = _split3_dot(gk_ref[...], ec_ref[...])
        rows = [gqt_ref[0, hd * QT_ROWS + HEAD_DIM:hd * QT_ROWS + HEAD_DIM + 1, :]
                for hd in range(HEADS)]
        row_sums = jnp.concatenate(rows + [jnp.zeros((128 - HEADS, tm), F32)], axis=0).T
        out = _tri_dot(tri_ref[...], row_sums - col_sums) + carry[...]
        o_ref[...] = out
        carry[...] = out[0:1, :]

    return pl.pallas_call(
        body, name="rev_cumsum", grid=(n,),
        in_specs=[pl.BlockSpec((tm, SLAB_W), lambda i: (n - 1 - i, 0)),
                  pl.BlockSpec((1, HEADS * QT_ROWS, tm), lambda i: (n - 1 - i, 0, 0)),
                  _const_spec((tm, tm)), _const_spec(ecol.shape)],
        out_specs=pl.BlockSpec((tm, 128), lambda i: (n - 1 - i, 0)),
        out_shape=jax.ShapeDtypeStruct((s_len, 128), F32),
        scratch_shapes=[pltpu.VMEM((1, 128), F32)],
        compiler_params=_params(32, 1),
    )(gk, gqt, triu, ecol)


def _heads_from_slabs(slabs):
    lane = lax.broadcasted_iota(jnp.int32, slabs[0].shape, 1)
    low = lane < HEAD_DIM
    pairs = [jnp.where(low, slabs[2 * p], pltpu.roll(slabs[2 * p + 1], HEAD_DIM, 1))
             for p in range(HEADS // 2)]
    return jnp.concatenate(pairs, axis=1)


def _proj_bwd(gqt, gk, dvt, dlogf, flog, qraw, kraw, duv, dgp, x, dx1, wcat, bdiag, gq, gk_gain, g1,
              efold):
    s_len = x.shape[0]
    tm = TOKEN_TILE

    def body(gqt_ref, gkk_ref, dvt_ref, dlf_ref, flog_ref, qr_ref, kr_ref, duv_ref, dgp_ref, x_ref,
             dx1_ref, w_ref, bd_ref, gq_ref, gk_ref, g1_ref, ef_ref,
             dx_ref, dproj_ref, dgq_ref, dgk_ref, dbf_ref, dg1_ref, gq_acc, gk_acc):
        step = pl.program_id(0)

        @pl.when(step == 0)
        def _():
            gq_acc[...] = jnp.zeros_like(gq_acc)
            gk_acc[...] = jnp.zeros_like(gk_acc)
            dbf_ref[...] = jnp.zeros_like(dbf_ref)
            dg1_ref[...] = jnp.zeros_like(dg1_ref)

        pad = jnp.zeros((128 - QT_ROWS, tm), F32)
        q_slabs = [jnp.concatenate([gqt_ref[0, hd * QT_ROWS:(hd + 1) * QT_ROWS, :], pad], axis=0).T
                   for hd in range(HEADS)]
        dqn = _heads_from_slabs(q_slabs)
        dkn = _heads_from_slabs([gkk_ref[:, hd * 128:(hd + 1) * 128] for hd in range(HEADS)])

        def head_bwd(raw_ref, dn, g_ref, acc):
            raw = raw_ref[...].astype(F32)
            r = lax.rsqrt(_seg_mean(raw * raw, bd_ref) + EPS)
            xhat = raw * r
            acc[0:1, :] += jnp.sum(dn * xhat, axis=0, keepdims=True)
            dyg = dn * g_ref[...]
            return r * (dyg - xhat * _seg_mean(dyg * xhat, bd_ref))

        dproj_ref[:, C_Q:C_K] = head_bwd(qr_ref, dqn * HEAD_DIM ** -0.5, gq_ref, gq_acc).astype(BF16)
        dproj_ref[:, C_K:C_V] = head_bwd(kr_ref, dkn * LN2, gk_ref, gk_acc).astype(BF16)
        dproj_ref[:, C_V:C_F] = dvt_ref[0].T.astype(BF16)
        dfl = dlf_ref[...] * _sigmoid(-flog_ref[...])
        dbf_ref[...] += jnp.sum(dfl, axis=0, keepdims=True)
        dproj_ref[:, C_F:C_UV] = dfl.astype(BF16)
        dproj_ref[:, C_UV:C_G] = duv_ref[...]
        dproj_ref[:, C_G:C_END] = dgp_ref[...]

        dh = jnp.dot(dproj_ref[...], w_ref[...], preferred_element_type=F32)
        xf = x_ref[...]
        r = lax.rsqrt(jnp.mean(xf * xf, axis=-1, keepdims=True) + EPS)
        dg1_ref[...] += jnp.sum(dh * xf * r, axis=0, keepdims=True)
        dx_ref[...] = dx1_ref[...] + _rms_bwd(xf, r, g1_ref[...], dh)

        @pl.when(step == pl.num_programs(0) - 1)
        def _():
            dgq_ref[...] = _split3_dot(gq_acc[...], ef_ref[...])
            dgk_ref[...] = _split3_dot(gk_acc[...], ef_ref[...])

    outs = [((s_len, D_MODEL), F32, _row_spec(tm, D_MODEL)),
            ((s_len, C_END), BF16, _row_spec(tm, C_END)),
            ((8, 128), F32, _const_spec((8, 128))),
            ((8, 128), F32, _const_spec((8, 128))),
            ((1, 128), F32, _const_spec((1, 128))),
            ((1, D_MODEL), F32, _const_spec((1, D_MODEL)))]
    return pl.pallas_call(
        body, name="proj_bwd", grid=(s_len // tm,),
        in_specs=[_tile_spec(HEADS * QT_ROWS, tm), _row_spec(tm, SLAB_W), _tile_spec(FOX_W, tm),
                  _row_spec(tm, 128), _row_spec(tm, 128), _row_spec(tm, FOX_W),
                  _row_spec(tm, FOX_W), _row_spec(tm, 2 * SGU_W), _row_spec(tm, 2 * D_MODEL),
                  _row_spec(tm, D_MODEL), _row_spec(tm, D_MODEL), _const_spec(wcat.shape),
                  _const_spec(bdiag.shape), _const_spec((1, FOX_W)), _const_spec((1, FOX_W)),
                  _const_spec((1, D_MODEL)), _const_spec(efold.shape)],
        out_specs=[o[2] for o in outs],
        out_shape=[jax.ShapeDtypeStruct(o[0], o[1]) for o in outs],
        scratch_shapes=[pltpu.VMEM((8, FOX_W), F32), pltpu.VMEM((8, FOX_W), F32)],
        compiler_params=_params(56, 1),
    )(gqt, gk, dvt, dlogf, flog, qraw, kraw, duv, dgp, x, dx1, wcat, bdiag, gq, gk_gain, g1, efold)


def _dw_matmul(a, b, tm, name, transpose_out=False):
    s_len, m = a.shape
    n = b.shape[1]
    tk = min(512, s_len)
    nk = s_len // tk

    def body(a_ref, b_ref, o_ref, acc):
        kk = pl.program_id(1)

        @pl.when(kk == 0)
        def _():
            acc[...] = jnp.zeros_like(acc)
        acc[...] += _dot_tn(a_ref[...], b_ref[...])

        @pl.when(kk == nk - 1)
        def _():
            res = acc[...]
            o_ref[...] = (res.T if transpose_out else res).astype(BF16)

    if transpose_out:
        out_spec = pl.BlockSpec((n, tm), lambda i, k: (0, i))
        out_shape = jax.ShapeDtypeStruct((n, m), BF16)
    else:
        out_spec = pl.BlockSpec((tm, n), lambda i, k: (i, 0))
        out_shape = jax.ShapeDtypeStruct((m, n), BF16)
    return pl.pallas_call(
        body, name=name, grid=(m // tm, nk),
        in_specs=[pl.BlockSpec((tk, tm), lambda i, k: (k, i)),
                  pl.BlockSpec((tk, n), lambda i, k: (k, 0))],
        out_specs=out_spec, out_shape=out_shape,
        scratch_shapes=[pltpu.VMEM((tm, n), F32)],
        compiler_params=_params(56, 2),
    )(a, b)


def _adamw(parts, w, m, v, tr, name, col_tile=None):
    n, rows, cols = parts.shape
    bc1 = 1.0 - ADAM_B1 ** ADAM_STEP
    bc2 = 1.0 - ADAM_B2 ** ADAM_STEP

    def body(p_ref, w_ref, m_ref, v_ref, g_ref, d_ref, mo_ref, vo_ref):
        g = p_ref[0].astype(F32)
        for idx in range(1, n):
            g = g + p_ref[idx].astype(F32)
        g_ref[...] = g
        mn = ADAM_B1 * m_ref[...] + (1.0 - ADAM_B1) * g
        vn = ADAM_B2 * v_ref[...] + (1.0 - ADAM_B2) * (g * g)
        mo_ref[...] = mn
        vo_ref[...] = vn
        m_hat = mn / bc1
        v_hat = vn / bc2
        d_ref[...] = -ADAM_LR * (m_hat / (jnp.sqrt(v_hat) + ADAM_EPS) + ADAM_WD * w_ref[...])

    if col_tile is None:
        spec = pl.BlockSpec((tr, cols), lambda i: (i, 0))
        pspec = pl.BlockSpec((n, tr, cols), lambda i: (0, i, 0))
        steps = rows // tr
    else:
        spec = pl.BlockSpec((rows, col_tile), lambda i: (0, i))
        pspec = pl.BlockSpec((n, rows, col_tile), lambda i: (0, 0, i))
        steps = cols // col_tile
    return pl.pallas_call(
        body, name=name, grid=(steps,),
        in_specs=[pspec, spec, spec, spec],
        out_specs=[spec] * 4,
        out_shape=[jax.ShapeDtypeStruct((rows, cols), F32)] * 4,
        compiler_params=_params(48, 1),
    )(parts, w, m, v)


def _sum_parts(parts, name):
    n, rows, cols = parts.shape

    def body(p_ref, o_ref):
        g = p_ref[0]
        for idx in range(1, n):
            g = g + p_ref[idx]
        o_ref[...] = g

    return pl.pallas_call(
        body, name=name, out_shape=jax.ShapeDtypeStruct((rows, cols), F32),
        in_specs=[_const_spec(parts.shape)], out_specs=_const_spec((rows, cols)), grid=(1,),
        compiler_params=_params(16, 1),
    )(parts)


SMALL_NAMES = ("g_pre_mix", "b_forget", "g_q", "g_k", "g_sgu", "b_sgu", "w_spatial", "b_spatial",
               "g_post_mix", "g_pre_ffn", "g_post_ffn")


def _small_rows(size):
    return -(-size // 1024)


def _pack_small(d):
    rows = []
    for k in SMALL_NAMES:
        flat = d[k].reshape(-1).astype(F32)
        nr = _small_rows(flat.shape[0])
        rows.append(jnp.pad(flat, (0, nr * 1024 - flat.shape[0])).reshape(nr, 1024))
    used = sum(r.shape[0] for r in rows)
    rows.append(jnp.zeros((N_DEV * SMALL_ROWS - used, 1024), F32))
    return jnp.concatenate(rows, axis=0)


def _unpack_small(packed, shapes):
    out, off = {}, 0
    for k in SMALL_NAMES:
        size = math.prod(shapes[k])
        nr = _small_rows(size)
        out[k] = packed[off:off + nr].reshape(-1)[:size].reshape(shapes[k])
        off += nr
    return out


def _cols_to_blocks(full, width):
    r = full.shape[0]
    return jnp.transpose(full.reshape(r, N_DEV, width), (1, 0, 2))


def _blocks_to_cols(blocks):
    n, r, width = blocks.shape
    return jnp.transpose(blocks, (1, 0, 2)).reshape(r, n * width)


def kernel(x, g_pre_mix, w_in, b_forget, g_q, g_k, g_sgu, b_sgu, w_spatial, b_spatial, w_branch_a, w_branch_b, w_out, g_post_mix, g_pre_ffn, w_ffn_in, w_ffn_down, g_post_ffn, loss_target, m_g_pre_mix, m_w_in, m_b_forget, m_g_q, m_g_k, m_g_sgu, m_b_sgu, m_w_spatial, m_b_spatial, m_w_branch_a, m_w_branch_b, m_w_out, m_g_post_mix, m_g_pre_ffn, m_w_ffn_in, m_w_ffn_down, m_g_post_ffn, v_g_pre_mix, v_w_in, v_b_forget, v_g_q, v_g_k, v_g_sgu, v_b_sgu, v_w_spatial, v_b_spatial, v_w_branch_a, v_w_branch_b, v_w_out, v_g_post_mix, v_g_pre_ffn, v_w_ffn_in, v_w_ffn_down, v_g_post_ffn):
    big_names = ("w_in", "w_branch_a", "w_branch_b", "w_out", "w_ffn_in", "w_ffn_down")
    weights = dict(g_pre_mix=g_pre_mix, w_in=w_in, b_forget=b_forget, g_q=g_q, g_k=g_k, g_sgu=g_sgu,
                   b_sgu=b_sgu, w_spatial=w_spatial, b_spatial=b_spatial, w_branch_a=w_branch_a,
                   w_branch_b=w_branch_b, w_out=w_out, g_post_mix=g_post_mix, g_pre_ffn=g_pre_ffn,
                   w_ffn_in=w_ffn_in, w_ffn_down=w_ffn_down, g_post_ffn=g_post_ffn)
    mom1 = dict(g_pre_mix=m_g_pre_mix, w_in=m_w_in, b_forget=m_b_forget, g_q=m_g_q, g_k=m_g_k,
                g_sgu=m_g_sgu, b_sgu=m_b_sgu, w_spatial=m_w_spatial, b_spatial=m_b_spatial,
                w_branch_a=m_w_branch_a, w_branch_b=m_w_branch_b, w_out=m_w_out,
                g_post_mix=m_g_post_mix, g_pre_ffn=m_g_pre_ffn, w_ffn_in=m_w_ffn_in,
                w_ffn_down=m_w_ffn_down, g_post_ffn=m_g_post_ffn)
    mom2 = dict(g_pre_mix=v_g_pre_mix, w_in=v_w_in, b_forget=v_b_forget, g_q=v_g_q, g_k=v_g_k,
                g_sgu=v_g_sgu, b_sgu=v_b_sgu, w_spatial=v_w_spatial, b_spatial=v_b_spatial,
                w_branch_a=v_w_branch_a, w_branch_b=v_w_branch_b, w_out=v_w_out,
                g_post_mix=v_g_post_mix, g_pre_ffn=v_g_pre_ffn, w_ffn_in=v_w_ffn_in,
                w_ffn_down=v_w_ffn_down, g_post_ffn=v_g_post_ffn)
    names = list(weights)
    shapes = {k: weights[k].shape for k in names}

    s_len = x.shape[1]
    xs = x.reshape(s_len, D_MODEL)
    tgt = loss_target.reshape(s_len, D_MODEL)

    transposed = ("w_in", "w_ffn_in")

    def local_view(a, k):
        return jnp.transpose(a[0]) if k in transposed else a[0]

    shards = {k: local_view(weights[k], k).astype(BF16) for k in big_names}
    (gat_in, gat_mix, gat_ffn), gat_token = _exchange_start(
        [[shards["w_in"]],
         [shards["w_branch_a"], shards["w_branch_b"], shards["w_out"]],
         [shards["w_ffn_in"], shards["w_ffn_down"]]], "gather_start", gather=True)
    (zone_in,) = _exchange_wait(gat_in, gat_token, "gather_wait_in", gather=True)
    win_t = _own_block(zone_in, shards["w_in"]).reshape(IN_COLS, D_MODEL)
    f_off = 3 * FOX_W
    u_off = f_off + HEADS
    wcat = jnp.concatenate([
        win_t[:f_off], jnp.pad(win_t[f_off:u_off], ((0, 128 - HEADS), (0, 0))), win_t[u_off:]],
        axis=0)

    seg = jnp.arange(FOX_W) // HEAD_DIM
    bdiag = (seg[:, None] == seg[None, :]).astype(BF16)
    tm = TOKEN_TILE
    tril = (jnp.arange(tm)[None, :] <= jnp.arange(tm)[:, None]).astype(BF16)
    triu = tril.T
    egrp = (seg[:, None] == jnp.arange(128)[None, :]).astype(BF16)
    efold = ((jnp.arange(FOX_W) % HEAD_DIM)[:, None] == jnp.arange(128)[None, :]).astype(BF16)
    gq512 = jnp.tile(g_q.reshape(1, HEAD_DIM), (1, HEADS))
    gk512 = jnp.tile(g_k.reshape(1, HEAD_DIM), (1, HEADS))
    bfor = jnp.pad(b_forget.reshape(1, HEADS), ((0, 0), (0, 128 - HEADS)))
    pos = jnp.arange(WINDOW)
    wmask = ((pos[None, :] // CHUNK) <= (pos[:, None] // CHUNK))
    wsm_f = jnp.where(wmask[None], w_spatial[0], 0.0)
    wsm = wsm_f.astype(BF16)
    wsmt = jnp.transpose(wsm_f, (0, 2, 1)).astype(BF16)
    bsf = jnp.repeat(jnp.transpose(b_spatial[0]), HEAD_DIM, axis=1)
    wmask_f = wmask.astype(F32)

    col = jnp.arange(SLAB_W)
    place = ((col[None, :] // 128 == seg[:, None])
             & (col[None, :] % 128 == (jnp.arange(FOX_W) % HEAD_DIM)[:, None])).astype(BF16)
    row128 = jnp.arange(128)

    def d_place(first):
        return jnp.stack([((col[None, :] // 128 == row128[:, None])
                           & (col[None, :] % 128 == first + a)).astype(BF16) for a in range(3)])

    pdq, pdk = d_place(HEAD_DIM), d_place(HEAD_DIM + 3)
    ones_q = ((col % 128 >= HEAD_DIM + 3) & (col % 128 < HEAD_DIM + 6)).astype(F32)[None]
    ones_k = ((col % 128 >= HEAD_DIM) & (col % 128 < HEAD_DIM + 3)).astype(F32)[None]
    ecol = ((col[:, None] // 128 == row128[None, :])
            & (col[:, None] % 128 == HEAD_DIM + 3)).astype(BF16)

    (h, qa, ka, kat, vs, vt, qraw, kraw, flog, uvpre, gpre) = _proj_fwd(
        xs, g_pre_mix, wcat, bdiag, gq512, gk512, bfor, tril, place, pdq, pdk, ones_q, ones_k)
    attn, lse = _attn_fwd(qa, ka, vt)
    zone_a, zone_b, zone_out = _exchange_wait(gat_mix, attn, "gather_wait_mix", gather=True)
    wa = _blocks_to_cols(_own_block(zone_a, shards["w_branch_a"]))
    wb = _blocks_to_cols(_own_block(zone_b, shards["w_branch_b"]))
    wout = _own_block(zone_out, shards["w_out"]).reshape(D_MODEL, D_MODEL)
    sgu, ya, yb, merged, om, x1 = _mix_fwd(attn, uvpre, gpre, xs, wa, wb, wout, wsm, bsf,
                                           g_sgu, b_sgu, g_post_mix)
    zone_ffn, zone_down = _exchange_wait(gat_ffn, x1, "gather_wait_ffn", gather=True)
    wffn = _own_block(zone_ffn, shards["w_ffn_in"]).reshape(2 * D_FF, D_MODEL)
    wdown = _own_block(zone_down, shards["w_ffn_down"]).reshape(D_FF, D_MODEL)
    (dx1, h2, act, dff, dgu, loss_acc, dg_post_ffn, dg_pre_ffn) = _ffn_fwd_bwd(
        x1, tgt, wffn, wdown, g_pre_ffn, g_post_ffn)

    dw_down = _dw_matmul(act, dff, D_FF // 2, "dw_down")
    dw_ffn = _dw_matmul(h2, dgu, 512, "dw_ffn_in", transpose_out=True)
    parts_ffn = [dw_ffn.reshape(N_DEV, 2 * D_FF // N_DEV, D_MODEL),
                 dw_down.reshape(N_DEV, D_FF // N_DEV, D_MODEL)]
    (sct_ffn,), sct_ffn_token = _exchange_start([parts_ffn], "scatter_start_ffn", gather=False)

    (dom, dya, dyb, dgp, dot_, delta, duv, dws, dbs, dg_sgu, db_sgu, dg_post_mix) = _mix_bwd(
        dx1, om, ya, yb, gpre, uvpre, attn, wout, wa, wb, wsm, wsmt, bsf, g_sgu, b_sgu,
        g_post_mix + sct_ffn_token[0:1, 0:1], wmask_f, egrp)
    dw_out = _dw_matmul(merged, dom, 512, "dw_out")
    dw_a = _dw_matmul(attn, dya, 512, "dw_a")
    dw_b = _dw_matmul(sgu, dyb, 512, "dw_b")
    parts_mix = [_cols_to_blocks(dw_a, D_MODEL // N_DEV), _cols_to_blocks(dw_b, D_MODEL // N_DEV),
                 dw_out.reshape(N_DEV, D_MODEL // N_DEV, D_MODEL)]
    (sct_mix,), sct_mix_token = _exchange_start([parts_mix], "scatter_start_mix", gather=False)

    gk_all, dvt, gqt = _attn_bwd(qa, ka, kat, vs, dot_, lse, delta + sct_mix_token[0, 0])
    dlogf = _rev_cumsum(gk_all, gqt, triu, ecol)
    dx, dproj, dgq, dgk, dbf, dg_pre_mix = _proj_bwd(
        gqt, gk_all, dvt, dlogf, flog, qraw, kraw, duv, dgp, xs, dx1, wcat, bdiag, gq512, gk512,
        g_pre_mix, efold)
    dw_cat = _dw_matmul(h, dproj, 512, "dw_in", transpose_out=True)
    dw_in = jnp.concatenate([dw_cat[:C_F + HEADS], dw_cat[C_UV:]], axis=0)

    small_local = dict(
        g_pre_mix=dg_pre_mix, b_forget=dbf[:, :HEADS], g_q=dgq[0:1, :HEAD_DIM],
        g_k=dgk[0:1, :HEAD_DIM], g_sgu=dg_sgu, b_sgu=db_sgu, w_spatial=dws,
        b_spatial=jnp.transpose(dbs[:, :GROUPS]), g_post_mix=dg_post_mix, g_pre_ffn=dg_pre_ffn,
        g_post_ffn=dg_post_ffn)
    small_parts = _pack_small(small_local).reshape(N_DEV, SMALL_ROWS, 1024)

    recv_in, recv_small = _exchange([dw_in.reshape(N_DEV, IN_COLS // N_DEV, D_MODEL), small_parts],
                                    "scatter_in_small", gather=False)
    x_pos, y_pos, c_pos = _mesh_pos()
    me = 4 * x_pos + 2 * y_pos + c_pos

    def with_own(zones, parts):
        return [_own_block(z, lax.dynamic_index_in_dim(p, me, 0, keepdims=False))
                for z, p in zip(zones, parts)]

    recv_ffn, recv_down = with_own(
        _exchange_wait(sct_ffn, recv_in, "scatter_wait_ffn", gather=False), parts_ffn)
    recv_a, recv_b, recv_out = with_own(
        _exchange_wait(sct_mix, recv_ffn, "scatter_wait_mix", gather=False), parts_mix)
    received = [recv_in, recv_a, recv_b, recv_out, recv_ffn, recv_down]

    grads, deltas, new_m, new_v = {}, {}, {}, {}
    row_tiles = {"w_in": None, "w_branch_a": 512, "w_branch_b": 512, "w_out": 128, "w_ffn_in": 176,
                 "w_ffn_down": 352}
    for idx, k in enumerate(big_names):
        outs = _adamw(received[idx], local_view(weights[k], k), local_view(mom1[k], k),
                      local_view(mom2[k], k), row_tiles[k], "adamw_" + k,
                      col_tile=256 if k == "w_in" else None)
        if k in transposed:
            outs = [jnp.transpose(o) for o in outs]
        grads[k], deltas[k], new_m[k], new_v[k] = [o[None] for o in outs]

    small_sum = _sum_parts(recv_small, "sum_small")
    (small_all,) = _exchange([small_sum], "gather_small", gather=True)
    small_all = small_all.reshape(1, N_DEV * SMALL_ROWS, 1024)
    sg, sd, sm, sv = _adamw(small_all, _pack_small(weights), _pack_small(mom1), _pack_small(mom2),
                            N_DEV * SMALL_ROWS, "adamw_small")
    for dst, packed in ((grads, sg), (deltas, sd), (new_m, sm), (new_v, sv)):
        dst.update(_unpack_small(packed, shapes))

    loss = lax.psum(loss_acc[0, 0], ("x", "y", "c"))
    return (loss, dx.reshape(x.shape), *[grads[k] for k in names], *[deltas[k] for k in names],
            *[new_m[k] for k in names], *[new_v[k] for k in names])
```

```python
import functools
import math

import jax
import jax.numpy as jnp
from jax import lax
from jax.experimental import pallas as pl
from jax.experimental.pallas import tpu as pltpu

F32 = jnp.float32
BF16 = jnp.bfloat16

D_MODEL = 1024
FOX_W = 512
HEADS = 8
HEAD_DIM = 64
SGU_W = 512
GROUPS = 8
WINDOW = 128
CHUNK = 64
D_FF = 2816
IN_COLS = 4616
EPS = 1e-6
N_DEV = 8
LOG2E = 1.4426950408889634
LN2 = 0.6931471805599453

C_Q, C_K, C_V, C_F, C_UV, C_G, C_END = 0, 512, 1024, 1536, 1664, 2688, 4736

ADAM_LR, ADAM_B1, ADAM_B2, ADAM_EPS, ADAM_WD, ADAM_STEP = 0.001, 0.9, 0.999, 1e-08, 0.01, 10

MIB = 1024 * 1024
TOKEN_TILE = 256
ATTN_TILE = 256
SLAB_W = HEADS * 128
QT_ROWS = 72

SMALL_ROWS = 18


def _params(vmem_mib, n_axes):
    return pltpu.CompilerParams(
        dimension_semantics=("arbitrary",) * n_axes, vmem_limit_bytes=vmem_mib * MIB)


def _const_spec(shape):
    nd = len(shape)
    return pl.BlockSpec(shape, lambda *_: (0,) * nd)


def _row_spec(tm, cols):
    return pl.BlockSpec((tm, cols), lambda i: (i, 0))


def _tile_spec(rows, tm):
    return pl.BlockSpec((1, rows, tm), lambda i: (i, 0, 0))


def _split3_dot(x, e):
    x1 = x.astype(BF16)
    r1 = x - x1.astype(F32)
    x2 = r1.astype(BF16)
    x3 = (r1 - x2.astype(F32)).astype(BF16)
    dot = functools.partial(jnp.dot, preferred_element_type=F32)
    return dot(x1, e) + dot(x2, e) + dot(x3, e)


def _tri_dot(tri, x):
    x1 = x.astype(BF16)
    r1 = x - x1.astype(F32)
    x2 = r1.astype(BF16)
    x3 = (r1 - x2.astype(F32)).astype(BF16)
    dot = functools.partial(jnp.dot, preferred_element_type=F32)
    return dot(tri, x1) + dot(tri, x2) + dot(tri, x3)


def _seg_mean(sq, bd_ref):
    hi = sq.astype(BF16)
    lo = (sq - hi.astype(F32)).astype(BF16)
    bd = bd_ref[...]
    dot = functools.partial(jnp.dot, preferred_element_type=F32)
    pairs = [dot(hi[:, p * 128:(p + 1) * 128], bd) + dot(lo[:, p * 128:(p + 1) * 128], bd)
             for p in range(HEADS // 2)]
    return jnp.concatenate(pairs, axis=1) * (1.0 / HEAD_DIM)


def _slabs_from_heads(t):
    lane = lax.broadcasted_iota(jnp.int32, (t.shape[0], 128), 1)
    low = lane < HEAD_DIM
    slabs = []
    for p in range(HEADS // 2):
        pair = t[:, p * 128:(p + 1) * 128]
        slabs.append(jnp.where(low, pair, 0.0))
        slabs.append(jnp.where(low, pltpu.roll(pair, HEAD_DIM, 1), 0.0))
    return jnp.concatenate(slabs, axis=1)


def _dot_nt(a, b):
    return lax.dot_general(a, b, (((1,), (1,)), ((), ())), preferred_element_type=F32)


def _dot_tn(a, b):
    return lax.dot_general(a, b, (((0,), (0,)), ((), ())), preferred_element_type=F32)


def _sigmoid(x):
    return 1.0 / (1.0 + jnp.exp(-x))


_GELU_C = math.sqrt(2.0 / math.pi)


def _gelu_and_grad(x):
    inner = _GELU_C * (x + 0.044715 * x * x * x)
    t = jnp.tanh(inner)
    y = 0.5 * x * (1.0 + t)
    dy = 0.5 * (1.0 + t) + 0.5 * x * (1.0 - t * t) * _GELU_C * (1.0 + 3.0 * 0.044715 * x * x)
    return y, dy


def _rms_bwd(xin, r, g, dy):
    dyg = dy * g
    return r * dyg - xin * (r * r * r) * jnp.mean(dyg * xin, axis=-1, keepdims=True)


def _mesh_pos():
    x, y, c = lax.axis_index("x"), lax.axis_index("y"), lax.axis_index("c")
    return x, y, c


def _peer(k):
    x, y, c = _mesh_pos()
    px = (1 - x) if (k >> 2) & 1 else x
    py = (1 - y) if (k >> 1) & 1 else y
    pc = (1 - c) if k & 1 else c
    return (px, py, pc), 4 * px + 2 * py + pc


def _exchange(arrs, name, gather):
    n = len(arrs)
    if gather:
        out_shape = [jax.ShapeDtypeStruct((N_DEV,) + a.shape, a.dtype) for a in arrs]
    else:
        out_shape = [jax.ShapeDtypeStruct(a.shape, a.dtype) for a in arrs]

    def body(*refs):
        ins, outs = refs[:n], refs[n:2 * n]
        send_sems, recv_sems, local_sems = refs[2 * n:]
        x, y, c = _mesh_pos()
        me = 4 * x + 2 * y + c

        def src(a, idx):
            return ins[a] if gather else ins[a].at[idx]

        local = []
        for a in range(n):
            cp = pltpu.make_async_copy(src(a, me), outs[a].at[me], local_sems.at[a])
            cp.start()
            local.append(cp)
        sends = []
        for k in range(1, N_DEV):
            peer, pidx = _peer(k)
            for a in range(n):
                cp = pltpu.make_async_remote_copy(
                    src_ref=src(a, pidx), dst_ref=outs[a].at[me],
                    send_sem=send_sems.at[a, k - 1], recv_sem=recv_sems.at[a, k - 1],
                    device_id=peer, device_id_type=pl.DeviceIdType.MESH)
                cp.start()
                sends.append(cp)
        for k in range(1, N_DEV):
            peer, pidx = _peer(k)
            for a in range(n):
                pltpu.make_async_remote_copy(
                    src_ref=src(a, pidx), dst_ref=outs[a].at[pidx],
                    send_sem=send_sems.at[a, k - 1], recv_sem=recv_sems.at[a, k - 1],
                    device_id=peer, device_id_type=pl.DeviceIdType.MESH).wait_recv()
        for cp in sends:
            cp.wait_send()
        for cp in local:
            cp.wait()

    any_spec = pl.BlockSpec(memory_space=pl.ANY)
    return pl.pallas_call(
        body, name=name, out_shape=out_shape,
        in_specs=[any_spec] * n, out_specs=[any_spec] * n,
        scratch_shapes=[pltpu.SemaphoreType.DMA((n, N_DEV - 1)),
                        pltpu.SemaphoreType.DMA((n, N_DEV - 1)),
                        pltpu.SemaphoreType.DMA((n,))],
    )(*arrs)


def _gather_two_level(shard, name):
    def body(x_ref, out_ref, send_sems, recv_sems, local_sem):
        x, y, c = _mesh_pos()
        me, sibling = (x, y, c), (x, y, 1 - c)
        chips = [(1 - x, y), (x, 1 - y), (1 - x, 1 - y)]

        def slot(px, py, pc):
            return out_ref.at[4 * px + 2 * py + pc]

        def copy(k, block, to, src=None):
            return pltpu.make_async_remote_copy(
                src_ref=slot(*block) if src is None else src, dst_ref=slot(*block),
                send_sem=send_sems.at[k], recv_sem=recv_sems.at[k],
                device_id=to, device_id_type=pl.DeviceIdType.MESH)

        mine = pltpu.make_async_copy(x_ref, slot(*me), local_sem)
        mine.start()
        first = [copy(1 + j, me, (*chip, c), src=x_ref) for j, chip in enumerate(chips)]
        first.append(copy(0, me, sibling, src=x_ref))
        for cp in first:
            cp.start()
        passed = [copy(4 + j, (*chip, c), sibling) for j, chip in enumerate(chips)]
        for j, chip in enumerate(chips):
            copy(1 + j, (*chip, c), me).wait_recv()
            passed[j].start()
        copy(0, sibling, me).wait_recv()
        for j, chip in enumerate(chips):
            copy(4 + j, (*chip, 1 - c), me).wait_recv()
        for cp in first + passed:
            cp.wait_send()
        mine.wait()

    any_spec = pl.BlockSpec(memory_space=pl.ANY)
    return pl.pallas_call(
        body, name=name, out_shape=jax.ShapeDtypeStruct((N_DEV,) + shard.shape, shard.dtype),
        in_specs=[any_spec], out_specs=any_spec,
        scratch_shapes=[pltpu.SemaphoreType.DMA((7,)), pltpu.SemaphoreType.DMA((7,)),
                        pltpu.SemaphoreType.DMA],
    )(shard)


def _reduce_scatter_two_level(parts, name):
    _, rows, cols = parts.shape
    n_chips = N_DEV // 2

    def body(p_ref, out_ref, mine_buf, sib_buf, send_buf, recv_buf, send_sems, recv_sems,
             local_sems):
        x, y, c = _mesh_pos()
        my_chip = 2 * x + y
        sibling = (x, y, 1 - c)
        stage1, local = [], []
        for q in range(n_chips):
            cp = pltpu.make_async_remote_copy(
                src_ref=p_ref.at[2 * q + (1 - c)], dst_ref=sib_buf.at[q],
                send_sem=send_sems.at[q], recv_sem=recv_sems.at[q],
                device_id=sibling, device_id_type=pl.DeviceIdType.MESH)
            cp.start()
            stage1.append(cp)
            lc = pltpu.make_async_copy(p_ref.at[2 * q + c], mine_buf.at[q], local_sems.at[q])
            lc.start()
            local.append(lc)
        for lc in local:
            lc.wait()
        for cp in stage1:
            cp.wait_recv()
        stage2 = []
        for k in range(1, n_chips):
            px = (1 - x) if (k >> 1) & 1 else x
            py = (1 - y) if k & 1 else y
            q = 2 * px + py
            pair = mine_buf[q].astype(F32) + sib_buf[q].astype(F32)
            send_buf[k - 1] = pair.astype(BF16)
            cp = pltpu.make_async_remote_copy(
                src_ref=send_buf.at[k - 1], dst_ref=recv_buf.at[k - 1],
                send_sem=send_sems.at[n_chips + k - 1], recv_sem=recv_sems.at[n_chips + k - 1],
                device_id=(px, py, c), device_id_type=pl.DeviceIdType.MESH)
            cp.start()
            stage2.append(cp)
        total = mine_buf[my_chip].astype(F32) + sib_buf[my_chip].astype(F32)
        for k in range(1, n_chips):
            stage2[k - 1].wait_recv()
            total = total + recv_buf[k - 1].astype(F32)
        out_ref[...] = total
        for cp in stage1 + stage2:
            cp.wait_send()

    return pl.pallas_call(
        body, name=name, out_shape=jax.ShapeDtypeStruct((rows, cols), F32),
        in_specs=[pl.BlockSpec(memory_space=pl.ANY)],
        out_specs=pl.BlockSpec(memory_space=pltpu.VMEM),
        scratch_shapes=[pltpu.VMEM((n_chips, rows, cols), BF16),
                        pltpu.VMEM((n_chips, rows, cols), BF16),
                        pltpu.VMEM((n_chips - 1, rows, cols), BF16),
                        pltpu.VMEM((n_chips - 1, rows, cols), BF16),
                        pltpu.SemaphoreType.DMA((2 * n_chips - 1,)),
                        pltpu.SemaphoreType.DMA((2 * n_chips - 1,)),
                        pltpu.SemaphoreType.DMA((n_chips,))],
        compiler_params=pltpu.CompilerParams(vmem_limit_bytes=40 * MIB),
    )(parts)


def _remote_copy(gather, src_ref, land_ref, send_sem, recv_sem, k, receive_side):
    x, y, c = _mesh_pos()
    me = 4 * x + 2 * y + c
    peer, pidx = _peer(k)
    return pltpu.make_async_remote_copy(
        src_ref=src_ref if gather else src_ref.at[pidx],
        dst_ref=land_ref.at[pidx if receive_side else me],
        send_sem=send_sem, recv_sem=recv_sem,
        device_id=peer, device_id_type=pl.DeviceIdType.MESH)


def _exchange_start(groups, name, gather):
    arrs = [a for g in groups for a in g]
    n, n_groups = len(arrs), len(groups)
    lands = [jax.ShapeDtypeStruct(((N_DEV,) + a.shape) if gather else a.shape, a.dtype)
             for a in arrs]

    def body(*refs):
        srcs, zones = refs[:n], refs[n:2 * n]
        sems = refs[2 * n:2 * n + 2 * n_groups]
        token = refs[-1]
        a = 0
        for gi, g in enumerate(groups):
            send_sems, recv_sems = sems[2 * gi], sems[2 * gi + 1]
            for k in range(1, N_DEV):
                for ai in range(len(g)):
                    slot = ai * (N_DEV - 1) + k - 1
                    _remote_copy(gather, srcs[a + ai], zones[a + ai], send_sems.at[slot],
                                 recv_sems.at[slot], k, False).start()
            a += len(g)
        token[...] = jnp.zeros_like(token)

    hbm = pl.BlockSpec(memory_space=pltpu.HBM)
    sem = pl.BlockSpec(memory_space=pltpu.SEMAPHORE)
    sem_shapes = []
    for g in groups:
        sem_shapes += [pltpu.SemaphoreType.DMA((len(g) * (N_DEV - 1),))] * 2
    outs = pl.pallas_call(
        body, name=name,
        in_specs=[hbm] * (2 * n),
        out_shape=sem_shapes + [pltpu.HBM(a.shape, a.dtype) for a in arrs]
        + [pltpu.HBM(z.shape, z.dtype) for z in lands] + [jax.ShapeDtypeStruct((8, 128), F32)],
        out_specs=[sem] * (2 * n_groups) + [hbm] * (2 * n)
        + [pl.BlockSpec(memory_space=pltpu.VMEM)],
        input_output_aliases={i: 2 * n_groups + i for i in range(2 * n)},
        compiler_params=pltpu.CompilerParams(
            has_side_effects=pltpu.SideEffectType.DATAFLOW_SIDE_EFFECTING),
    )(*[pltpu.with_memory_space_constraint(a, pltpu.HBM) for a in arrs],
      *[pltpu.with_memory_space_constraint(lax.empty(z.shape, z.dtype), pltpu.HBM) for z in lands])
    sems = outs[:2 * n_groups]
    thru = outs[2 * n_groups:2 * n_groups + n]
    zones = outs[2 * n_groups + n:2 * n_groups + 2 * n]
    handles, a = [], 0
    for gi, g in enumerate(groups):
        handles.append((sems[2 * gi], sems[2 * gi + 1], thru[a:a + len(g)], zones[a:a + len(g)]))
        a += len(g)
    return handles, outs[-1]


def _exchange_wait(handle, after, name, gather):
    send_sems, recv_sems, thru, zones = handle
    n = len(thru)

    def body(*refs):
        srcs, lands = refs[:n], refs[n:2 * n]
        ssem, rsem = refs[2 * n], refs[2 * n + 1]
        for k in range(1, N_DEV):
            for ai in range(n):
                slot = ai * (N_DEV - 1) + k - 1
                cp = _remote_copy(gather, srcs[ai], lands[ai], ssem.at[slot], rsem.at[slot], k, True)
                cp.wait_send()
                cp.wait_recv()

    hbm = pl.BlockSpec(memory_space=pltpu.HBM)
    sem = pl.BlockSpec(memory_space=pltpu.SEMAPHORE)
    outs = pl.pallas_call(
        body, name=name,
        in_specs=[hbm] * (2 * n) + [sem, sem, pl.BlockSpec(memory_space=pl.ANY)],
        out_shape=[pltpu.HBM(a.shape, a.dtype) for a in thru]
        + [pltpu.HBM(z.shape, z.dtype) for z in zones],
        out_specs=[hbm] * (2 * n),
        input_output_aliases={i: i for i in range(2 * n)},
        compiler_params=pltpu.CompilerParams(
            has_side_effects=pltpu.SideEffectType.DATAFLOW_SIDE_EFFECTING),
    )(*thru, *zones, send_sems, recv_sems, after)
    return outs[n:]


def _own_block(zone, block):
    x, y, c = _mesh_pos()
    me = 4 * x + 2 * y + c
    return lax.dynamic_update_slice_in_dim(zone, block[None], me, axis=0)


def _proj_fwd(x, g1, wcat, bdiag, gq, gk, bfor, tri, pdq, pdk, ones_q, ones_k):
    s_len = x.shape[0]
    tm = TOKEN_TILE
    nt = s_len // tm

    def body(x_ref, g1_ref, w_ref, bd_ref, gq_ref, gk_ref, bf_ref, tri_ref, pdq_ref,
             pdk_ref, oq_ref, ok_ref,
             h_ref, qa_ref, ka_ref, kat_ref, vs_ref, vt_ref, qr_ref, kr_ref, flog_ref, uv_ref,
             gp_ref, carry):
        @pl.when(pl.program_id(0) == 0)
        def _():
            carry[...] = jnp.zeros_like(carry)

        xf = x_ref[...]
        r = lax.rsqrt(jnp.mean(xf * xf, axis=-1, keepdims=True) + EPS)
        h = (xf * r * g1_ref[...]).astype(BF16)
        h_ref[...] = h
        dot = functools.partial(jnp.dot, preferred_element_type=F32)

        def proj(lo, hi):
            return _dot_nt(h, w_ref[lo:hi, :])

        flog = proj(C_F, C_UV) + bf_ref[...]
        flog_ref[...] = flog
        lane = lax.broadcasted_iota(jnp.int32, flog.shape, 1)
        logf = jnp.minimum(flog, 0.0) - jnp.log(1.0 + jnp.exp(-jnp.abs(flog)))
        logf = jnp.where(lane < HEADS, logf, 0.0)
        dcum = _tri_dot(tri_ref[...], logf) + carry[...]
        carry[...] = dcum[tm - 1:tm, :]
        d2 = dcum * LOG2E
        d2a = d2.astype(BF16)
        rem = d2 - d2a.astype(F32)
        d2b = rem.astype(BF16)
        d2c = (rem - d2b.astype(F32)).astype(BF16)

        q = proj(C_Q, C_K)
        qr_ref[...] = q.astype(BF16)
        rq = lax.rsqrt(_seg_mean(q * q, bd_ref) + EPS)
        qn = q * rq * (gq_ref[...] * (HEAD_DIM ** -0.5 * LOG2E))
        qa = (_slabs_from_heads(qn) + dot(d2a, pdq_ref[0]) + dot(d2b, pdq_ref[1])
              + dot(d2c, pdq_ref[2]) + oq_ref[...])
        qa_ref[...] = qa.astype(BF16)

        k = proj(C_K, C_V)
        kr_ref[...] = k.astype(BF16)
        rk = lax.rsqrt(_seg_mean(k * k, bd_ref) + EPS)
        kn = k * rk * gk_ref[...]
        ka = (_slabs_from_heads(kn) - dot(d2a, pdk_ref[0]) - dot(d2b, pdk_ref[1])
              - dot(d2c, pdk_ref[2]) + ok_ref[...])
        ka_ref[...] = ka.astype(BF16)
        kat_ref[0] = ka.T.astype(BF16)

        v = proj(C_V, C_F)
        vs_ref[...] = _slabs_from_heads(v).astype(BF16)
        vt_ref[0] = v.T.astype(BF16)
        uv_ref[...] = proj(C_UV, C_G).astype(BF16)
        gp_ref[...] = proj(C_G, C_END).astype(BF16)

    outs = [((s_len, D_MODEL), BF16, _row_spec(tm, D_MODEL)),
            ((s_len, SLAB_W), BF16, _row_spec(tm, SLAB_W)),
            ((s_len, SLAB_W), BF16, _row_spec(tm, SLAB_W)),
            ((nt, SLAB_W, tm), BF16, _tile_spec(SLAB_W, tm)),
            ((s_len, SLAB_W), BF16, _row_spec(tm, SLAB_W)),
            ((nt, FOX_W, tm), BF16, _tile_spec(FOX_W, tm)),
            ((s_len, FOX_W), BF16, _row_spec(tm, FOX_W)),
            ((s_len, FOX_W), BF16, _row_spec(tm, FOX_W)),
            ((s_len, 128), F32, _row_spec(tm, 128)),
            ((s_len, 2 * SGU_W), BF16, _row_spec(tm, 2 * SGU_W)),
            ((s_len, 2 * D_MODEL), BF16, _row_spec(tm, 2 * D_MODEL))]
    return pl.pallas_call(
        body, name="proj_fwd", grid=(nt,),
        in_specs=[_row_spec(tm, D_MODEL), _const_spec((1, D_MODEL)), _const_spec(wcat.shape),
                  _const_spec(bdiag.shape), _const_spec((1, FOX_W)), _const_spec((1, FOX_W)),
                  _const_spec((1, 128)), _const_spec((tm, tm)), _const_spec(pdq.shape), _const_spec(pdk.shape), _const_spec(ones_q.shape),
                  _const_spec(ones_k.shape)],
        out_specs=[o[2] for o in outs],
        out_shape=[jax.ShapeDtypeStruct(o[0], o[1]) for o in outs],
        scratch_shapes=[pltpu.VMEM((1, 128), F32)],
        compiler_params=_params(56, 1),
    )(x, g1, wcat, bdiag, gq, gk, bfor, tri, pdq, pdk, ones_q, ones_k)


def _attn_fwd(qa, ka, vt):
    s_len = qa.shape[0]
    t = ATTN_TILE
    nb = s_len // t

    def body(q_ref, k_ref, vt_ref, o_ref, lse_ref, m_sc, l_sc, acc_sc, s_sc, alpha_sc):
        i = pl.program_id(0)
        m_sc[...] = jnp.full_like(m_sc, -jnp.inf)
        l_sc[...] = jnp.zeros_like(l_sc)
        acc_sc[...] = jnp.zeros_like(acc_sc)

        def tile(j, masked):
            krows = pl.ds(pl.multiple_of(j * t, t), t)
            if masked:
                keep = (lax.broadcasted_iota(jnp.int32, (t, t), 0)
                        <= lax.broadcasted_iota(jnp.int32, (t, t), 1))
            for hd in range(HEADS):
                sl = slice(hd * 128, (hd + 1) * 128)
                st = _dot_nt(k_ref[krows, sl], q_ref[:, sl])
                if masked:
                    st = jnp.where(keep, st, -jnp.inf)
                s_sc[hd] = st
                m_prev = m_sc[hd:hd + 1, :]
                m_new = jnp.maximum(m_prev, jnp.max(st, axis=0, keepdims=True))
                alpha_sc[hd:hd + 1, :] = jnp.exp2(m_prev - m_new)
                m_sc[hd:hd + 1, :] = m_new
            for hd in range(HEADS):
                hr = slice(hd * HEAD_DIM, (hd + 1) * HEAD_DIM)
                alpha = alpha_sc[hd:hd + 1, :]
                pt = jnp.exp2(s_sc[hd] - m_sc[hd:hd + 1, :])
                l_sc[hd:hd + 1, :] = alpha * l_sc[hd:hd + 1, :] + jnp.sum(pt, axis=0, keepdims=True)
                acc_sc[hr, :] = alpha * acc_sc[hr, :] + jnp.dot(
                    vt_ref[j, hr, :], pt.astype(BF16), preferred_element_type=F32)

        def off_diagonal(j, carry):
            tile(j, False)
            return carry

        lax.fori_loop(0, i, off_diagonal, 0)
        tile(i, True)

        for hd in range(HEADS):
            hr = slice(hd * HEAD_DIM, (hd + 1) * HEAD_DIM)
            l = l_sc[hd:hd + 1, :]
            acc_sc[hr, :] = acc_sc[hr, :] / l
            lse_ref[0, hd:hd + 1, :] = m_sc[hd:hd + 1, :] + jnp.log2(l)
        o_ref[...] = acc_sc[...].T.astype(BF16)

    return pl.pallas_call(
        body, name="attn_fwd", grid=(nb,),
        in_specs=[_row_spec(t, SLAB_W), _const_spec(ka.shape), _const_spec(vt.shape)],
        out_specs=[_row_spec(t, FOX_W), _tile_spec(HEADS, t)],
        out_shape=[jax.ShapeDtypeStruct((s_len, FOX_W), BF16),
                   jax.ShapeDtypeStruct((nb, HEADS, t), F32)],
        scratch_shapes=[pltpu.VMEM((HEADS, t), F32), pltpu.VMEM((HEADS, t), F32),
                        pltpu.VMEM((FOX_W, t), F32), pltpu.VMEM((HEADS, t, t), F32),
                        pltpu.VMEM((HEADS, t), F32)],
        compiler_params=_params(48, 1),
    )(qa, ka, vt)


def _sgu_mix(vn, ws_ref):
    tm = vn.shape[0]
    lane = lax.broadcasted_iota(jnp.int32, (WINDOW, 128), 1)
    low = lane < HEAD_DIM
    wins = []
    for w in range(tm // WINDOW):
        slabs = []
        for p in range(GROUPS // 2):
            v2 = vn[w * WINDOW:(w + 1) * WINDOW, p * 128:(p + 1) * 128]
            lo = jnp.where(low, v2, 0.0).astype(BF16)
            hi = jnp.where(low, 0.0, v2).astype(BF16)
            slabs.append(jnp.dot(ws_ref[2 * p], lo, preferred_element_type=F32)
                         + jnp.dot(ws_ref[2 * p + 1], hi, preferred_element_type=F32))
        wins.append(jnp.concatenate(slabs, axis=1))
    return jnp.concatenate(wins, axis=0) if len(wins) > 1 else wins[0]


def _layernorm_fwd(vv, g, b):
    mu = jnp.mean(vv, axis=-1, keepdims=True)
    xc = vv - mu
    r = lax.rsqrt(jnp.mean(xc * xc, axis=-1, keepdims=True) + EPS)
    xh = xc * r
    return xh * g + b, xh, r


def _mix_fwd(attn, uvpre, gpre, x, wa, wb, wout, wsm, bsf, gsgu, bsgu, gpost):
    s_len = x.shape[0]
    tm = TOKEN_TILE

    def body(o_ref, uv_ref, gp_ref, x_ref, wa_ref, wb_ref, wo_ref, ws_ref, bs_ref, gs_ref, bsg_ref,
             gpost_ref, sgu_ref, ya_ref, yb_ref, mg_ref, om_ref, x1_ref):
        uvp = uv_ref[...].astype(F32)
        uv, _ = _gelu_and_grad(uvp)
        u, vv = uv[:, :SGU_W], uv[:, SGU_W:]
        vn, _, _ = _layernorm_fwd(vv, gs_ref[...], bsg_ref[...])
        bias = bs_ref[...]
        if tm > WINDOW:
            bias = jnp.concatenate([bias] * (tm // WINDOW), axis=0)
        mixed = _sgu_mix(vn, ws_ref) + bias
        sgu = (u * mixed).astype(BF16)
        sgu_ref[...] = sgu
        ya = jnp.dot(o_ref[...], wa_ref[...], preferred_element_type=F32)
        yb = jnp.dot(sgu, wb_ref[...], preferred_element_type=F32)
        ya_ref[...] = ya.astype(BF16)
        yb_ref[...] = yb.astype(BF16)
        gates = _sigmoid(gp_ref[...].astype(F32))
        merged = (gates[:, :D_MODEL] * ya + gates[:, D_MODEL:] * yb).astype(BF16)
        mg_ref[...] = merged
        om = jnp.dot(merged, wo_ref[...], preferred_element_type=F32)
        om_ref[...] = om
        r = lax.rsqrt(jnp.mean(om * om, axis=-1, keepdims=True) + EPS)
        x1_ref[...] = x_ref[...] + om * r * gpost_ref[...]

    outs = [(SGU_W, BF16), (D_MODEL, BF16), (D_MODEL, BF16), (D_MODEL, BF16), (D_MODEL, F32),
            (D_MODEL, F32)]
    return pl.pallas_call(
        body, name="mix_fwd", grid=(s_len // tm,),
        in_specs=[_row_spec(tm, FOX_W), _row_spec(tm, 2 * SGU_W), _row_spec(tm, 2 * D_MODEL),
                  _row_spec(tm, D_MODEL), _const_spec(wa.shape), _const_spec(wb.shape),
                  _const_spec(wout.shape), _const_spec(wsm.shape), _const_spec(bsf.shape),
                  _const_spec((1, SGU_W)), _const_spec((1, SGU_W)), _const_spec((1, D_MODEL))],
        out_specs=[_row_spec(tm, c) for c, _ in outs],
        out_shape=[jax.ShapeDtypeStruct((s_len, c), dt) for c, dt in outs],
        compiler_params=_params(48, 1),
    )(attn, uvpre, gpre, x, wa, wb, wout, wsm, bsf, gsgu, bsgu, gpost)


def _ffn_fwd_bwd(x1, tgt, wffn, wdown, gpre, gpost):
    s_len = x1.shape[0]
    tm = TOKEN_TILE

    def body(x1_ref, t_ref, wi_ref, wd_ref, gpre_ref, gpost_ref,
             dx1_ref, h2_ref, act_ref, dff_ref, dgu_ref, loss_ref, dgpost_ref, dgpre_ref):
        @pl.when(pl.program_id(0) == 0)
        def _():
            loss_ref[...] = jnp.zeros_like(loss_ref)
            dgpost_ref[...] = jnp.zeros_like(dgpost_ref)
            dgpre_ref[...] = jnp.zeros_like(dgpre_ref)

        x1v = x1_ref[...]
        r2 = lax.rsqrt(jnp.mean(x1v * x1v, axis=-1, keepdims=True) + EPS)
        gpre_v = gpre_ref[...]
        h2 = (x1v * r2 * gpre_v).astype(BF16)
        h2_ref[...] = h2
        gg = _dot_nt(h2, wi_ref[:D_FF, :])
        uu = _dot_nt(h2, wi_ref[D_FF:, :])
        sg = _sigmoid(gg)
        silu = gg * sg
        act = (silu * uu).astype(BF16)
        act_ref[...] = act
        ff = jnp.dot(act, wd_ref[...], preferred_element_type=F32)
        r3 = lax.rsqrt(jnp.mean(ff * ff, axis=-1, keepdims=True) + EPS)
        gpost_v = gpost_ref[...]
        y = x1v + ff * r3 * gpost_v
        err = y - t_ref[...]
        loss_ref[...] += jnp.sum(err * err) * (0.5 / D_MODEL)
        dy = err * (1.0 / D_MODEL)
        dgpost_ref[...] += jnp.sum(dy * ff * r3, axis=0, keepdims=True)
        dff = _rms_bwd(ff, r3, gpost_v, dy).astype(BF16)
        dff_ref[...] = dff
        dact = _dot_nt(dff, wd_ref[...])
        dgg = (dact * uu * (sg * (1.0 + gg * (1.0 - sg)))).astype(BF16)
        duu = (dact * silu).astype(BF16)
        dgu_ref[:, :D_FF] = dgg
        dgu_ref[:, D_FF:] = duu
        dh2 = (jnp.dot(dgg, wi_ref[:D_FF, :], preferred_element_type=F32)
               + jnp.dot(duu, wi_ref[D_FF:, :], preferred_element_type=F32))
        dgpre_ref[...] += jnp.sum(dh2 * x1v * r2, axis=0, keepdims=True)
        dx1_ref[...] = dy + _rms_bwd(x1v, r2, gpre_v, dh2)

    outs = [((s_len, D_MODEL), F32, _row_spec(tm, D_MODEL)),
            ((s_len, D_MODEL), BF16, _row_spec(tm, D_MODEL)),
            ((s_len, D_FF), BF16, _row_spec(tm, D_FF)),
            ((s_len, D_MODEL), BF16, _row_spec(tm, D_MODEL)),
            ((s_len, 2 * D_FF), BF16, _row_spec(tm, 2 * D_FF)),
            ((1, 128), F32, _const_spec((1, 128))),
            ((1, D_MODEL), F32, _const_spec((1, D_MODEL))),
            ((1, D_MODEL), F32, _const_spec((1, D_MODEL)))]
    return pl.pallas_call(
        body, name="ffn_fwd_bwd", grid=(s_len // tm,),
        in_specs=[_row_spec(tm, D_MODEL), _row_spec(tm, D_MODEL), _const_spec(wffn.shape),
                  _const_spec(wdown.shape), _const_spec((1, D_MODEL)), _const_spec((1, D_MODEL))],
        out_specs=[o[2] for o in outs],
        out_shape=[jax.ShapeDtypeStruct(o[0], o[1]) for o in outs],
        compiler_params=_params(60, 1),
    )(x1, tgt, wffn, wdown, gpre, gpost)


def _mix_bwd(dx1, om, ya, yb, gpre, uvpre, attn, wout, wa, wb, wsm, wsmt, bsf, gsgu, bsgu, gpost,
             wmask, egrp):
    s_len = dx1.shape[0]
    tm = TOKEN_TILE
    nw = tm // WINDOW
    nt = s_len // tm

    def body(dx1_ref, om_ref, ya_ref, yb_ref, gp_ref, uv_ref, o_ref, wo_ref, wa_ref, wb_ref, ws_ref,
             wst_ref, bs_ref, gs_ref, bsg_ref, gpost_ref, mask_ref, eg_ref,
             dom_ref, dya_ref, dyb_ref, dgp_ref, dot_ref, delta_ref, duv_ref,
             dws_ref, dbs_ref, dgs_ref, dbsg_ref, dgpost_ref, dbs_acc):
        step = pl.program_id(0)

        @pl.when(step == 0)
        def _():
            dws_ref[...] = jnp.zeros_like(dws_ref)
            dbs_acc[...] = jnp.zeros_like(dbs_acc)
            dgs_ref[...] = jnp.zeros_like(dgs_ref)
            dbsg_ref[...] = jnp.zeros_like(dbsg_ref)
            dgpost_ref[...] = jnp.zeros_like(dgpost_ref)

        om = om_ref[...]
        dx1v = dx1_ref[...]
        r = lax.rsqrt(jnp.mean(om * om, axis=-1, keepdims=True) + EPS)
        gpost_v = gpost_ref[...]
        dgpost_ref[...] += jnp.sum(dx1v * om * r, axis=0, keepdims=True)
        dom = _rms_bwd(om, r, gpost_v, dx1v).astype(BF16)
        dom_ref[...] = dom
        dmg = _dot_nt(dom, wo_ref[...])

        gates = _sigmoid(gp_ref[...].astype(F32))
        ga, gb = gates[:, :D_MODEL], gates[:, D_MODEL:]
        yav, ybv = ya_ref[...].astype(F32), yb_ref[...].astype(F32)
        dya = (dmg * ga).astype(BF16)
        dyb = (dmg * gb).astype(BF16)
        dya_ref[...] = dya
        dyb_ref[...] = dyb
        dgp_ref[:, :D_MODEL] = (dmg * yav * ga * (1.0 - ga)).astype(BF16)
        dgp_ref[:, D_MODEL:] = (dmg * ybv * gb * (1.0 - gb)).astype(BF16)

        dat_t = _dot_nt(dya, wa_ref[...]).T.astype(BF16)
        dot_ref[0] = dat_t
        o_t = o_ref[...].astype(F32).T
        delta_ref[0] = jnp.sum((dat_t.astype(F32) * o_t).reshape(HEADS, HEAD_DIM, tm), axis=1)
        dsgu = _dot_nt(dyb, wb_ref[...])

        uvp = uv_ref[...].astype(F32)
        uv, guv = _gelu_and_grad(uvp)
        u, vv = uv[:, :SGU_W], uv[:, SGU_W:]
        gs_v = gs_ref[...]
        vn, xh, rln = _layernorm_fwd(vv, gs_v, bsg_ref[...])
        bias = bs_ref[...]
        if nw > 1:
            bias = jnp.concatenate([bias] * nw, axis=0)
        mixed = _sgu_mix(vn, ws_ref) + bias
        du = dsgu * mixed
        dmixed = dsgu * u

        lane = lax.broadcasted_iota(jnp.int32, (WINDOW, 128), 1)
        low = lane < HEAD_DIM
        dvn_wins = []
        for w in range(nw):
            rows = slice(w * WINDOW, (w + 1) * WINDOW)
            dbs_acc[...] += dmixed[rows, :]
            slabs = []
            for p in range(GROUPS // 2):
                cols = slice(p * 128, (p + 1) * 128)
                dm2 = dmixed[rows, cols]
                dlo = jnp.where(low, dm2, 0.0).astype(BF16)
                dhi = jnp.where(low, 0.0, dm2).astype(BF16)
                vn2 = vn[rows, cols].astype(BF16)
                dws_ref[2 * p] += _dot_nt(dlo, vn2)
                dws_ref[2 * p + 1] += _dot_nt(dhi, vn2)
                slabs.append(jnp.dot(wst_ref[2 * p], dlo, preferred_element_type=F32)
                             + jnp.dot(wst_ref[2 * p + 1], dhi, preferred_element_type=F32))
            dvn_wins.append(jnp.concatenate(slabs, axis=1))
        dvn = jnp.concatenate(dvn_wins, axis=0) if nw > 1 else dvn_wins[0]

        dgs_ref[...] += jnp.sum(dvn * xh, axis=0, keepdims=True)
        dbsg_ref[...] += jnp.sum(dvn, axis=0, keepdims=True)
        dxh = dvn * gs_v
        dvv = rln * (dxh - jnp.mean(dxh, axis=-1, keepdims=True)
                     - xh * jnp.mean(dxh * xh, axis=-1, keepdims=True))
        duv_ref[:, :SGU_W] = (du * guv[:, :SGU_W]).astype(BF16)
        duv_ref[:, SGU_W:] = (dvv * guv[:, SGU_W:]).astype(BF16)

        @pl.when(step == pl.num_programs(0) - 1)
        def _():
            for g in range(GROUPS):
                dws_ref[g] = dws_ref[g] * mask_ref[...]
            dbs_ref[...] = _split3_dot(dbs_acc[...], eg_ref[...])

    rows_out = [((s_len, D_MODEL), BF16, _row_spec(tm, D_MODEL)),
                ((s_len, D_MODEL), BF16, _row_spec(tm, D_MODEL)),
                ((s_len, D_MODEL), BF16, _row_spec(tm, D_MODEL)),
                ((s_len, 2 * D_MODEL), BF16, _row_spec(tm, 2 * D_MODEL)),
                ((nt, FOX_W, tm), BF16, _tile_spec(FOX_W, tm)),
                ((nt, HEADS, tm), F32, _tile_spec(HEADS, tm)),
                ((s_len, 2 * SGU_W), BF16, _row_spec(tm, 2 * SGU_W))]
    acc_out = [((GROUPS, WINDOW, WINDOW), F32), ((WINDOW, 128), F32), ((1, SGU_W), F32),
               ((1, SGU_W), F32), ((1, D_MODEL), F32)]
    return pl.pallas_call(
        body, name="mix_bwd", grid=(nt,),
        in_specs=[_row_spec(tm, D_MODEL), _row_spec(tm, D_MODEL), _row_spec(tm, D_MODEL),
                  _row_spec(tm, D_MODEL), _row_spec(tm, 2 * D_MODEL), _row_spec(tm, 2 * SGU_W),
                  _row_spec(tm, FOX_W), _const_spec(wout.shape), _const_spec(wa.shape),
                  _const_spec(wb.shape), _const_spec(wsm.shape), _const_spec(wsmt.shape),
                  _const_spec(bsf.shape), _const_spec((1, SGU_W)), _const_spec((1, SGU_W)),
                  _const_spec((1, D_MODEL)), _const_spec(wmask.shape), _const_spec(egrp.shape)],
        out_specs=[o[2] for o in rows_out] + [_const_spec(s) for s, _ in acc_out],
        out_shape=[jax.ShapeDtypeStruct(o[0], o[1]) for o in rows_out]
        + [jax.ShapeDtypeStruct(s, dt) for s, dt in acc_out],
        scratch_shapes=[pltpu.VMEM((WINDOW, SGU_W), F32)],
        compiler_params=_params(48, 1),
    )(dx1, om, ya, yb, gpre, uvpre, attn, wout, wa, wb, wsm, wsmt, bsf, gsgu, bsgu, gpost, wmask,
      egrp)


def _attn_bwd(qa, ka, kat, vs, dot_, lse, delta, ecol):
    s_len = qa.shape[0]
    t = ATTN_TILE
    nb = s_len // t

    def body(k_ref, kt_ref, vs_ref, q_ref, do_ref, lse_ref, dl_ref, ec_ref, gk_ref, dvt_ref,
             gqt_ref, csum_ref, p_sc, ds_sc):
        j = pl.program_id(0)

        @pl.when(j == 0)
        def _():
            gqt_ref[...] = jnp.zeros_like(gqt_ref)

        gk_ref[...] = jnp.zeros_like(gk_ref)
        dvt_ref[...] = jnp.zeros_like(dvt_ref)

        def tile(i, masked):
            qrows = pl.ds(pl.multiple_of(i * t, t), t)
            if masked:
                keep = (lax.broadcasted_iota(jnp.int32, (t, t), 0)
                        <= lax.broadcasted_iota(jnp.int32, (t, t), 1))
            for hd in range(HEADS):
                sl = slice(hd * 128, (hd + 1) * 128)
                hr = slice(hd * HEAD_DIM, (hd + 1) * HEAD_DIM)
                st = _dot_nt(k_ref[:, sl], q_ref[qrows, sl])
                if masked:
                    st = jnp.where(keep, st, -jnp.inf)
                pt = jnp.exp2(st - lse_ref[i, hd:hd + 1, :])
                dpt = jnp.dot(vs_ref[:, hd * 128:hd * 128 + HEAD_DIM], do_ref[i, hr, :],
                              preferred_element_type=F32)
                p_sc[hd] = pt.astype(BF16)
                ds_sc[hd] = (pt * (dpt - dl_ref[i, hd:hd + 1, :])).astype(BF16)
            for hd in range(HEADS):
                sl = slice(hd * 128, (hd + 1) * 128)
                hr = slice(hd * HEAD_DIM, (hd + 1) * HEAD_DIM)
                dst = ds_sc[hd]
                dvt_ref[0, hr, :] += _dot_nt(do_ref[i, hr, :], p_sc[hd])
                gk_ref[:, sl] += jnp.dot(dst, q_ref[qrows, sl], preferred_element_type=F32)
                gqt_ref[i, hd * QT_ROWS:(hd + 1) * QT_ROWS, :] += jnp.dot(
                    kt_ref[0, hd * 128:hd * 128 + QT_ROWS, :], dst, preferred_element_type=F32)

        tile(j, True)

        def below_diagonal(i, carry):
            tile(i, False)
            return carry

        lax.fori_loop(j + 1, nb, below_diagonal, 0)
        csum_ref[...] = _split3_dot(gk_ref[...], ec_ref[...])

    return pl.pallas_call(
        body, name="attn_bwd", grid=(nb,),
        in_specs=[_row_spec(t, SLAB_W), _tile_spec(SLAB_W, t), _row_spec(t, SLAB_W),
                  _const_spec(qa.shape), _const_spec(dot_.shape), _const_spec(lse.shape),
                  _const_spec(delta.shape), _const_spec(ecol.shape)],
        out_specs=[_row_spec(t, SLAB_W), _tile_spec(FOX_W, t),
                   _const_spec((nb, HEADS * QT_ROWS, t)), _row_spec(t, 128)],
        out_shape=[jax.ShapeDtypeStruct((s_len, SLAB_W), F32),
                   jax.ShapeDtypeStruct((nb, FOX_W, t), F32),
                   jax.ShapeDtypeStruct((nb, HEADS * QT_ROWS, t), F32),
                   jax.ShapeDtypeStruct((s_len, 128), F32)],
        scratch_shapes=[pltpu.VMEM((HEADS, t, t), BF16), pltpu.VMEM((HEADS, t, t), BF16)],
        compiler_params=_params(60, 1),
    )(ka, kat, vs, qa, dot_, lse, delta, ecol)


def _rev_cumsum(col_sums, gqt, triu):
    s_len = col_sums.shape[0]
    tm = TOKEN_TILE
    n = s_len // tm

    def body(cs_ref, gqt_ref, tri_ref, o_ref, carry):
        @pl.when(pl.program_id(0) == 0)
        def _():
            carry[...] = jnp.zeros_like(carry)
        rows = [gqt_ref[0, hd * QT_ROWS + HEAD_DIM:hd * QT_ROWS + HEAD_DIM + 1, :]
                for hd in range(HEADS)]
        row_sums = jnp.concatenate(rows + [jnp.zeros((128 - HEADS, tm), F32)], axis=0).T
        out = _tri_dot(tri_ref[...], row_sums - cs_ref[...]) + carry[...]
        o_ref[...] = out
        carry[...] = out[0:1, :]

    return pl.pallas_call(
        body, name="rev_cumsum", grid=(n,),
        in_specs=[pl.BlockSpec((tm, 128), lambda i: (n - 1 - i, 0)),
                  pl.BlockSpec((1, HEADS * QT_ROWS, tm), lambda i: (n - 1 - i, 0, 0)),
                  _const_spec((tm, tm))],
        out_specs=pl.BlockSpec((tm, 128), lambda i: (n - 1 - i, 0)),
        out_shape=jax.ShapeDtypeStruct((s_len, 128), F32),
        scratch_shapes=[pltpu.VMEM((1, 128), F32)],
        compiler_params=_params(32, 1),
    )(col_sums, gqt, triu)


def _heads_from_slabs(slabs):
    lane = lax.broadcasted_iota(jnp.int32, slabs[0].shape, 1)
    low = lane < HEAD_DIM
    pairs = [jnp.where(low, slabs[2 * p], pltpu.roll(slabs[2 * p + 1], HEAD_DIM, 1))
             for p in range(HEADS // 2)]
    return jnp.concatenate(pairs, axis=1)


def _proj_bwd(gqt, gk, dvt, dlogf, flog, qraw, kraw, duv, dgp, x, dx1, wcat, bdiag, gq, gk_gain, g1,
              efold):
    s_len = x.shape[0]
    tm = TOKEN_TILE

    def body(gqt_ref, gkk_ref, dvt_ref, dlf_ref, flog_ref, qr_ref, kr_ref, duv_ref, dgp_ref, x_ref,
             dx1_ref, w_ref, bd_ref, gq_ref, gk_ref, g1_ref, ef_ref,
             dx_ref, dproj_ref, dgq_ref, dgk_ref, dbf_ref, dg1_ref, gq_acc, gk_acc):
        step = pl.program_id(0)

        @pl.when(step == 0)
        def _():
            gq_acc[...] = jnp.zeros_like(gq_acc)
            gk_acc[...] = jnp.zeros_like(gk_acc)
            dbf_ref[...] = jnp.zeros_like(dbf_ref)
            dg1_ref[...] = jnp.zeros_like(dg1_ref)

        pad = jnp.zeros((128 - QT_ROWS, tm), F32)
        q_slabs = [jnp.concatenate([gqt_ref[0, hd * QT_ROWS:(hd + 1) * QT_ROWS, :], pad], axis=0).T
                   for hd in range(HEADS)]
        dqn = _heads_from_slabs(q_slabs)
        dkn = _heads_from_slabs([gkk_ref[:, hd * 128:(hd + 1) * 128] for hd in range(HEADS)])

        def head_bwd(raw_ref, dn, g_ref, acc):
            raw = raw_ref[...].astype(F32)
            r = lax.rsqrt(_seg_mean(raw * raw, bd_ref) + EPS)
            xhat = raw * r
            acc[0:1, :] += jnp.sum(dn * xhat, axis=0, keepdims=True)
            dyg = dn * g_ref[...]
            return r * (dyg - xhat * _seg_mean(dyg * xhat, bd_ref))

        dproj_ref[:, C_Q:C_K] = head_bwd(qr_ref, dqn * HEAD_DIM ** -0.5, gq_ref, gq_acc).astype(BF16)
        dproj_ref[:, C_K:C_V] = head_bwd(kr_ref, dkn * LN2, gk_ref, gk_acc).astype(BF16)
        dproj_ref[:, C_V:C_F] = dvt_ref[0].T.astype(BF16)
        dfl = dlf_ref[...] * _sigmoid(-flog_ref[...])
        dbf_ref[...] += jnp.sum(dfl, axis=0, keepdims=True)
        dproj_ref[:, C_F:C_UV] = dfl.astype(BF16)
        dproj_ref[:, C_UV:C_G] = duv_ref[...]
        dproj_ref[:, C_G:C_END] = dgp_ref[...]

        dh = jnp.dot(dproj_ref[...], w_ref[...], preferred_element_type=F32)
        xf = x_ref[...]
        r = lax.rsqrt(jnp.mean(xf * xf, axis=-1, keepdims=True) + EPS)
        dg1_ref[...] += jnp.sum(dh * xf * r, axis=0, keepdims=True)
        dx_ref[...] = dx1_ref[...] + _rms_bwd(xf, r, g1_ref[...], dh)

        @pl.when(step == pl.num_programs(0) - 1)
        def _():
            dgq_ref[...] = _split3_dot(gq_acc[...], ef_ref[...])
            dgk_ref[...] = _split3_dot(gk_acc[...], ef_ref[...])

    outs = [((s_len, D_MODEL), F32, _row_spec(tm, D_MODEL)),
            ((s_len, C_END), BF16, _row_spec(tm, C_END)),
            ((8, 128), F32, _const_spec((8, 128))),
            ((8, 128), F32, _const_spec((8, 128))),
            ((1, 128), F32, _const_spec((1, 128))),
            ((1, D_MODEL), F32, _const_spec((1, D_MODEL)))]
    return pl.pallas_call(
        body, name="proj_bwd", grid=(s_len // tm,),
        in_specs=[_tile_spec(HEADS * QT_ROWS, tm), _row_spec(tm, SLAB_W), _tile_spec(FOX_W, tm),
                  _row_spec(tm, 128), _row_spec(tm, 128), _row_spec(tm, FOX_W),
                  _row_spec(tm, FOX_W), _row_spec(tm, 2 * SGU_W), _row_spec(tm, 2 * D_MODEL),
                  _row_spec(tm, D_MODEL), _row_spec(tm, D_MODEL), _const_spec(wcat.shape),
                  _const_spec(bdiag.shape), _const_spec((1, FOX_W)), _const_spec((1, FOX_W)),
                  _const_spec((1, D_MODEL)), _const_spec(efold.shape)],
        out_specs=[o[2] for o in outs],
        out_shape=[jax.ShapeDtypeStruct(o[0], o[1]) for o in outs],
        scratch_shapes=[pltpu.VMEM((8, FOX_W), F32), pltpu.VMEM((8, FOX_W), F32)],
        compiler_params=_params(56, 1),
    )(gqt, gk, dvt, dlogf, flog, qraw, kraw, duv, dgp, x, dx1, wcat, bdiag, gq, gk_gain, g1, efold)


def _dw_matmul(a, b, tm, name, transpose_out=False):
    s_len, m = a.shape
    n = b.shape[1]
    tk = min(512, s_len)
    nk = s_len // tk

    def body(a_ref, b_ref, o_ref, acc):
        kk = pl.program_id(1)

        @pl.when(kk == 0)
        def _():
            acc[...] = jnp.zeros_like(acc)
        acc[...] += _dot_tn(a_ref[...], b_ref[...])

        @pl.when(kk == nk - 1)
        def _():
            res = acc[...]
            o_ref[...] = (res.T if transpose_out else res).astype(BF16)

    if transpose_out:
        out_spec = pl.BlockSpec((n, tm), lambda i, k: (0, i))
        out_shape = jax.ShapeDtypeStruct((n, m), BF16)
    else:
        out_spec = pl.BlockSpec((tm, n), lambda i, k: (i, 0))
        out_shape = jax.ShapeDtypeStruct((m, n), BF16)
    return pl.pallas_call(
        body, name=name, grid=(m // tm, nk),
        in_specs=[pl.BlockSpec((tk, tm), lambda i, k: (k, i)),
                  pl.BlockSpec((tk, n), lambda i, k: (k, 0))],
        out_specs=out_spec, out_shape=out_shape,
        scratch_shapes=[pltpu.VMEM((tm, n), F32)],
        compiler_params=_params(56, 2),
    )(a, b)


def _adamw(parts, w, m, v, tr, name, col_tile=None):
    n, rows, cols = parts.shape
    bc1 = 1.0 - ADAM_B1 ** ADAM_STEP
    bc2 = 1.0 - ADAM_B2 ** ADAM_STEP

    def body(p_ref, w_ref, m_ref, v_ref, g_ref, d_ref, mo_ref, vo_ref):
        g = p_ref[0].astype(F32)
        for idx in range(1, n):
            g = g + p_ref[idx].astype(F32)
        g_ref[...] = g
        mn = ADAM_B1 * m_ref[...] + (1.0 - ADAM_B1) * g
        vn = ADAM_B2 * v_ref[...] + (1.0 - ADAM_B2) * (g * g)
        mo_ref[...] = mn
        vo_ref[...] = vn
        m_hat = mn / bc1
        v_hat = vn / bc2
        d_ref[...] = -ADAM_LR * (m_hat / (jnp.sqrt(v_hat) + ADAM_EPS) + ADAM_WD * w_ref[...])

    if col_tile is None:
        spec = pl.BlockSpec((tr, cols), lambda i: (i, 0))
        pspec = pl.BlockSpec((n, tr, cols), lambda i: (0, i, 0))
        steps = rows // tr
    else:
        spec = pl.BlockSpec((rows, col_tile), lambda i: (0, i))
        pspec = pl.BlockSpec((n, rows, col_tile), lambda i: (0, 0, i))
        steps = cols // col_tile
    return pl.pallas_call(
        body, name=name, grid=(steps,),
        in_specs=[pspec, spec, spec, spec],
        out_specs=[spec] * 4,
        out_shape=[jax.ShapeDtypeStruct((rows, cols), F32)] * 4,
        compiler_params=_params(48, 1),
    )(parts, w, m, v)


def _sum_parts(parts, name):
    n, rows, cols = parts.shape

    def body(p_ref, o_ref):
        g = p_ref[0]
        for idx in range(1, n):
            g = g + p_ref[idx]
        o_ref[...] = g

    return pl.pallas_call(
        body, name=name, out_shape=jax.ShapeDtypeStruct((rows, cols), F32),
        in_specs=[_const_spec(parts.shape)], out_specs=_const_spec((rows, cols)), grid=(1,),
        compiler_params=_params(16, 1),
    )(parts)


SMALL_NAMES = ("g_pre_mix", "b_forget", "g_q", "g_k", "g_sgu", "b_sgu", "w_spatial", "b_spatial",
               "g_post_mix", "g_pre_ffn", "g_post_ffn")


def _small_rows(size):
    return -(-size // 1024)


def _pack_small(d):
    rows = []
    for k in SMALL_NAMES:
        flat = d[k].reshape(-1).astype(F32)
        nr = _small_rows(flat.shape[0])
        rows.append(jnp.pad(flat, (0, nr * 1024 - flat.shape[0])).reshape(nr, 1024))
    used = sum(r.shape[0] for r in rows)
    rows.append(jnp.zeros((N_DEV * SMALL_ROWS - used, 1024), F32))
    return jnp.concatenate(rows, axis=0)


def _unpack_small(packed, shapes):
    out, off = {}, 0
    for k in SMALL_NAMES:
        size = math.prod(shapes[k])
        nr = _small_rows(size)
        out[k] = packed[off:off + nr].reshape(-1)[:size].reshape(shapes[k])
        off += nr
    return out


def _cols_to_blocks(full, width):
    r = full.shape[0]
    return jnp.transpose(full.reshape(r, N_DEV, width), (1, 0, 2))


def _blocks_to_cols(blocks):
    n, r, width = blocks.shape
    return jnp.transpose(blocks, (1, 0, 2)).reshape(r, n * width)


def kernel(x, g_pre_mix, w_in, b_forget, g_q, g_k, g_sgu, b_sgu, w_spatial, b_spatial, w_branch_a, w_branch_b, w_out, g_post_mix, g_pre_ffn, w_ffn_in, w_ffn_down, g_post_ffn, loss_target, m_g_pre_mix, m_w_in, m_b_forget, m_g_q, m_g_k, m_g_sgu, m_b_sgu, m_w_spatial, m_b_spatial, m_w_branch_a, m_w_branch_b, m_w_out, m_g_post_mix, m_g_pre_ffn, m_w_ffn_in, m_w_ffn_down, m_g_post_ffn, v_g_pre_mix, v_w_in, v_b_forget, v_g_q, v_g_k, v_g_sgu, v_b_sgu, v_w_spatial, v_b_spatial, v_w_branch_a, v_w_branch_b, v_w_out, v_g_post_mix, v_g_pre_ffn, v_w_ffn_in, v_w_ffn_down, v_g_post_ffn):
    big_names = ("w_in", "w_branch_a", "w_branch_b", "w_out", "w_ffn_in", "w_ffn_down")
    weights = dict(g_pre_mix=g_pre_mix, w_in=w_in, b_forget=b_forget, g_q=g_q, g_k=g_k, g_sgu=g_sgu,
                   b_sgu=b_sgu, w_spatial=w_spatial, b_spatial=b_spatial, w_branch_a=w_branch_a,
                   w_branch_b=w_branch_b, w_out=w_out, g_post_mix=g_post_mix, g_pre_ffn=g_pre_ffn,
                   w_ffn_in=w_ffn_in, w_ffn_down=w_ffn_down, g_post_ffn=g_post_ffn)
    mom1 = dict(g_pre_mix=m_g_pre_mix, w_in=m_w_in, b_forget=m_b_forget, g_q=m_g_q, g_k=m_g_k,
                g_sgu=m_g_sgu, b_sgu=m_b_sgu, w_spatial=m_w_spatial, b_spatial=m_b_spatial,
                w_branch_a=m_w_branch_a, w_branch_b=m_w_branch_b, w_out=m_w_out,
                g_post_mix=m_g_post_mix, g_pre_ffn=m_g_pre_ffn, w_ffn_in=m_w_ffn_in,
                w_ffn_down=m_w_ffn_down, g_post_ffn=m_g_post_ffn)
    mom2 = dict(g_pre_mix=v_g_pre_mix, w_in=v_w_in, b_forget=v_b_forget, g_q=v_g_q, g_k=v_g_k,
                g_sgu=v_g_sgu, b_sgu=v_b_sgu, w_spatial=v_w_spatial, b_spatial=v_b_spatial,
                w_branch_a=v_w_branch_a, w_branch_b=v_w_branch_b, w_out=v_w_out,
                g_post_mix=v_g_post_mix, g_pre_ffn=v_g_pre_ffn, w_ffn_in=v_w_ffn_in,
                w_ffn_down=v_w_ffn_down, g_post_ffn=v_g_post_ffn)
    names = list(weights)
    shapes = {k: weights[k].shape for k in names}

    s_len = x.shape[1]
    xs = x.reshape(s_len, D_MODEL)
    tgt = loss_target.reshape(s_len, D_MODEL)

    transposed = ("w_in", "w_ffn_in")

    def local_view(a, k):
        return jnp.transpose(a[0]) if k in transposed else a[0]

    shards = {k: local_view(weights[k], k).astype(BF16) for k in big_names}
    win_t = _gather_two_level(shards["w_in"], "gather_w_in").reshape(IN_COLS, D_MODEL)
    win_t, later = lax.optimization_barrier(
        (win_t, [shards[k] for k in big_names if k != "w_in"]))
    shards.update(zip([k for k in big_names if k != "w_in"], later))
    (gat_mix, gat_ffn), _ = _exchange_start(
        [[shards["w_branch_a"], shards["w_branch_b"], shards["w_out"]],
         [shards["w_ffn_in"], shards["w_ffn_down"]]], "gather_start", gather=True)
    f_off = 3 * FOX_W
    u_off = f_off + HEADS
    wcat = jnp.concatenate([
        win_t[:f_off], jnp.pad(win_t[f_off:u_off], ((0, 128 - HEADS), (0, 0))), win_t[u_off:]],
        axis=0)

    seg = jnp.arange(FOX_W) // HEAD_DIM
    bdiag = (seg[:128, None] == seg[None, :128]).astype(BF16)
    tm = TOKEN_TILE
    tril = (jnp.arange(tm)[None, :] <= jnp.arange(tm)[:, None]).astype(BF16)
    triu = tril.T
    egrp = (seg[:, None] == jnp.arange(128)[None, :]).astype(BF16)
    efold = ((jnp.arange(FOX_W) % HEAD_DIM)[:, None] == jnp.arange(128)[None, :]).astype(BF16)
    gq512 = jnp.tile(g_q.reshape(1, HEAD_DIM), (1, HEADS))
    gk512 = jnp.tile(g_k.reshape(1, HEAD_DIM), (1, HEADS))
    bfor = jnp.pad(b_forget.reshape(1, HEADS), ((0, 0), (0, 128 - HEADS)))
    pos = jnp.arange(WINDOW)
    wmask = ((pos[None, :] // CHUNK) <= (pos[:, None] // CHUNK))
    wsm_f = jnp.where(wmask[None], w_spatial[0], 0.0)
    wsm = wsm_f.astype(BF16)
    wsmt = jnp.transpose(wsm_f, (0, 2, 1)).astype(BF16)
    bsf = jnp.repeat(jnp.transpose(b_spatial[0]), HEAD_DIM, axis=1)
    wmask_f = wmask.astype(F32)

    col = jnp.arange(SLAB_W)
    row128 = jnp.arange(128)

    def d_place(first):
        return jnp.stack([((col[None, :] // 128 == row128[:, None])
                           & (col[None, :] % 128 == first + a)).astype(BF16) for a in range(3)])

    pdq, pdk = d_place(HEAD_DIM), d_place(HEAD_DIM + 3)
    ones_q = ((col % 128 >= HEAD_DIM + 3) & (col % 128 < HEAD_DIM + 6)).astype(F32)[None]
    ones_k = ((col % 128 >= HEAD_DIM) & (col % 128 < HEAD_DIM + 3)).astype(F32)[None]
    ecol = ((col[:, None] // 128 == row128[None, :])
            & (col[:, None] % 128 == HEAD_DIM + 3)).astype(BF16)

    (h, qa, ka, kat, vs, vt, qraw, kraw, flog, uvpre, gpre) = _proj_fwd(
        xs, g_pre_mix, wcat, bdiag, gq512, gk512, bfor, tril, pdq, pdk, ones_q, ones_k)
    attn, lse = _attn_fwd(qa, ka, vt)
    zone_a, zone_b, zone_out = _exchange_wait(gat_mix, attn, "gather_wait_mix", gather=True)
    wa = _blocks_to_cols(_own_block(zone_a, shards["w_branch_a"]))
    wb = _blocks_to_cols(_own_block(zone_b, shards["w_branch_b"]))
    wout = _own_block(zone_out, shards["w_out"]).reshape(D_MODEL, D_MODEL)
    sgu, ya, yb, merged, om, x1 = _mix_fwd(attn, uvpre, gpre, xs, wa, wb, wout, wsm, bsf,
                                           g_sgu, b_sgu, g_post_mix)
    zone_ffn, zone_down = _exchange_wait(gat_ffn, x1, "gather_wait_ffn", gather=True)
    wffn = _own_block(zone_ffn, shards["w_ffn_in"]).reshape(2 * D_FF, D_MODEL)
    wdown = _own_block(zone_down, shards["w_ffn_down"]).reshape(D_FF, D_MODEL)
    (dx1, h2, act, dff, dgu, loss_acc, dg_post_ffn, dg_pre_ffn) = _ffn_fwd_bwd(
        x1, tgt, wffn, wdown, g_pre_ffn, g_post_ffn)

    dw_down = _dw_matmul(act, dff, D_FF // 2, "dw_down")
    dw_ffn = _dw_matmul(h2, dgu, 512, "dw_ffn_in", transpose_out=True)
    parts_ffn = [dw_ffn.reshape(N_DEV, 2 * D_FF // N_DEV, D_MODEL),
                 dw_down.reshape(N_DEV, D_FF // N_DEV, D_MODEL)]
    (sct_ffn,), sct_ffn_token = _exchange_start([parts_ffn], "scatter_start_ffn", gather=False)

    (dom, dya, dyb, dgp, dot_, delta, duv, dws, dbs, dg_sgu, db_sgu, dg_post_mix) = _mix_bwd(
        dx1, om, ya, yb, gpre, uvpre, attn, wout, wa, wb, wsm, wsmt, bsf, g_sgu, b_sgu,
        g_post_mix + sct_ffn_token[0:1, 0:1], wmask_f, egrp)
    dw_out = _dw_matmul(merged, dom, 512, "dw_out")
    dw_a = _dw_matmul(attn, dya, 512, "dw_a")
    dw_b = _dw_matmul(sgu, dyb, 512, "dw_b")
    parts_mix = [_cols_to_blocks(dw_a, D_MODEL // N_DEV), _cols_to_blocks(dw_b, D_MODEL // N_DEV),
                 dw_out.reshape(N_DEV, D_MODEL // N_DEV, D_MODEL)]
    (sct_mix,), sct_mix_token = _exchange_start([parts_mix], "scatter_start_mix", gather=False)

    gk_all, dvt, gqt, col_sums = _attn_bwd(qa, ka, kat, vs, dot_, lse,
                                           delta + sct_mix_token[0, 0], ecol)
    dlogf = _rev_cumsum(col_sums, gqt, triu)
    dx, dproj, dgq, dgk, dbf, dg_pre_mix = _proj_bwd(
        gqt, gk_all, dvt, dlogf, flog, qraw, kraw, duv, dgp, xs, dx1, wcat, bdiag, gq512, gk512,
        g_pre_mix, efold)
    dw_cat = _dw_matmul(h, dproj, 512, "dw_in", transpose_out=True)
    dw_in = jnp.concatenate([dw_cat[:C_F + HEADS], dw_cat[C_UV:]], axis=0)

    small_local = dict(
        g_pre_mix=dg_pre_mix, b_forget=dbf[:, :HEADS], g_q=dgq[0:1, :HEAD_DIM],
        g_k=dgk[0:1, :HEAD_DIM], g_sgu=dg_sgu, b_sgu=db_sgu, w_spatial=dws,
        b_spatial=jnp.transpose(dbs[:, :GROUPS]), g_post_mix=dg_post_mix, g_pre_ffn=dg_pre_ffn,
        g_post_ffn=dg_post_ffn)
    small_parts = _pack_small(small_local).reshape(N_DEV, SMALL_ROWS, 1024)

    grad_in_t = _reduce_scatter_two_level(dw_in.reshape(N_DEV, IN_COLS // N_DEV, D_MODEL),
                                          "reduce_scatter_in")
    (recv_small,) = _exchange([small_parts], "scatter_small", gather=False)
    x_pos, y_pos, c_pos = _mesh_pos()
    me = 4 * x_pos + 2 * y_pos + c_pos

    def with_own(zones, parts):
        return [_own_block(z, lax.dynamic_index_in_dim(p, me, 0, keepdims=False))
                for z, p in zip(zones, parts)]

    recv_ffn, recv_down = with_own(
        _exchange_wait(sct_ffn, recv_small, "scatter_wait_ffn", gather=False), parts_ffn)
    recv_a, recv_b, recv_out = with_own(
        _exchange_wait(sct_mix, recv_ffn, "scatter_wait_mix", gather=False), parts_mix)
    received = [grad_in_t[None], recv_a, recv_b, recv_out, recv_ffn, recv_down]

    grads, deltas, new_m, new_v = {}, {}, {}, {}
    row_tiles = {"w_in": None, "w_branch_a": 512, "w_branch_b": 512, "w_out": 128, "w_ffn_in": 176,
                 "w_ffn_down": 352}
    for idx, k in enumerate(big_names):
        outs = _adamw(received[idx], local_view(weights[k], k), local_view(mom1[k], k),
                      local_view(mom2[k], k), row_tiles[k], "adamw_" + k,
                      col_tile=256 if k == "w_in" else None)
        if k in transposed:
            outs = [jnp.transpose(o) for o in outs]
        grads[k], deltas[k], new_m[k], new_v[k] = [o[None] for o in outs]

    small_sum = _sum_parts(recv_small, "sum_small")
    (small_all,) = _exchange([small_sum], "gather_small", gather=True)
    small_all = small_all.reshape(1, N_DEV * SMALL_ROWS, 1024)
    sg, sd, sm, sv = _adamw(small_all, _pack_small(weights), _pack_small(mom1), _pack_small(mom2),
                            N_DEV * SMALL_ROWS, "adamw_small")
    for dst, packed in ((grads, sg), (deltas, sd), (new_m, sm), (new_v, sv)):
        dst.update(_unpack_small(packed, shapes))

    loss = lax.psum(loss_acc[0, 0], ("x", "y", "c"))
    return (loss, dx.reshape(x.shape), *[grads[k] for k in names], *[deltas[k] for k in names],
            *[new_m[k] for k in names], *[new_v[k] for k in names])
```

```python
import functools
import math

import jax
import jax.numpy as jnp
from jax import lax
from jax.experimental import pallas as pl
from jax.experimental.pallas import tpu as pltpu

F32 = jnp.float32
BF16 = jnp.bfloat16

D_MODEL = 1024
FOX_W = 512
HEADS = 8
HEAD_DIM = 64
SGU_W = 512
GROUPS = 8
WINDOW = 128
CHUNK = 64
D_FF = 2816
IN_COLS = 4616
EPS = 1e-6
N_DEV = 8
LOG2E = 1.4426950408889634
LN2 = 0.6931471805599453

C_Q, C_K, C_V, C_F, C_UV, C_G, C_END = 0, 512, 1024, 1536, 1664, 2688, 4736

ADAM_LR, ADAM_B1, ADAM_B2, ADAM_EPS, ADAM_WD, ADAM_STEP = 0.001, 0.9, 0.999, 1e-08, 0.01, 10

MIB = 1024 * 1024
TOKEN_TILE = 256
ATTN_TILE = 256
SLAB_W = HEADS * 128
QT_ROWS = 72

SMALL_ROWS = 18


def _params(vmem_mib, n_axes):
    return pltpu.CompilerParams(
        dimension_semantics=("arbitrary",) * n_axes, vmem_limit_bytes=vmem_mib * MIB)


def _const_spec(shape):
    nd = len(shape)
    return pl.BlockSpec(shape, lambda *_: (0,) * nd)


def _row_spec(tm, cols):
    return pl.BlockSpec((tm, cols), lambda i: (i, 0))


def _tile_spec(rows, tm):
    return pl.BlockSpec((1, rows, tm), lambda i: (i, 0, 0))


def _split3_dot(x, e):
    x1 = x.astype(BF16)
    r1 = x - x1.astype(F32)
    x2 = r1.astype(BF16)
    x3 = (r1 - x2.astype(F32)).astype(BF16)
    dot = functools.partial(jnp.dot, preferred_element_type=F32)
    return dot(x1, e) + dot(x2, e) + dot(x3, e)


def _tri_dot(tri, x):
    x1 = x.astype(BF16)
    r1 = x - x1.astype(F32)
    x2 = r1.astype(BF16)
    x3 = (r1 - x2.astype(F32)).astype(BF16)
    dot = functools.partial(jnp.dot, preferred_element_type=F32)
    return dot(tri, x1) + dot(tri, x2) + dot(tri, x3)


def _seg_mean(sq, bd_ref):
    hi = sq.astype(BF16)
    lo = (sq - hi.astype(F32)).astype(BF16)
    bd = bd_ref[...]
    dot = functools.partial(jnp.dot, preferred_element_type=F32)
    pairs = [dot(hi[:, p * 128:(p + 1) * 128], bd) + dot(lo[:, p * 128:(p + 1) * 128], bd)
             for p in range(HEADS // 2)]
    return jnp.concatenate(pairs, axis=1) * (1.0 / HEAD_DIM)


def _slabs_from_heads(t):
    lane = lax.broadcasted_iota(jnp.int32, (t.shape[0], 128), 1)
    low = lane < HEAD_DIM
    slabs = []
    for p in range(HEADS // 2):
        pair = t[:, p * 128:(p + 1) * 128]
        slabs.append(jnp.where(low, pair, 0.0))
        slabs.append(jnp.where(low, pltpu.roll(pair, HEAD_DIM, 1), 0.0))
    return jnp.concatenate(slabs, axis=1)


def _dot_nt(a, b):
    return lax.dot_general(a, b, (((1,), (1,)), ((), ())), preferred_element_type=F32)


def _dot_tn(a, b):
    return lax.dot_general(a, b, (((0,), (0,)), ((), ())), preferred_element_type=F32)


def _sigmoid(x):
    return 1.0 / (1.0 + jnp.exp(-x))


_GELU_C = math.sqrt(2.0 / math.pi)


def _gelu_and_grad(x):
    inner = _GELU_C * (x + 0.044715 * x * x * x)
    t = jnp.tanh(inner)
    y = 0.5 * x * (1.0 + t)
    dy = 0.5 * (1.0 + t) + 0.5 * x * (1.0 - t * t) * _GELU_C * (1.0 + 3.0 * 0.044715 * x * x)
    return y, dy


def _rms_bwd(xin, r, g, dy):
    dyg = dy * g
    return r * dyg - xin * (r * r * r) * jnp.mean(dyg * xin, axis=-1, keepdims=True)


def _mesh_pos():
    x, y, c = lax.axis_index("x"), lax.axis_index("y"), lax.axis_index("c")
    return x, y, c


def _peer(k):
    x, y, c = _mesh_pos()
    px = (1 - x) if (k >> 2) & 1 else x
    py = (1 - y) if (k >> 1) & 1 else y
    pc = (1 - c) if k & 1 else c
    return (px, py, pc), 4 * px + 2 * py + pc


def _exchange(arrs, name, gather):
    n = len(arrs)
    if gather:
        out_shape = [jax.ShapeDtypeStruct((N_DEV,) + a.shape, a.dtype) for a in arrs]
    else:
        out_shape = [jax.ShapeDtypeStruct(a.shape, a.dtype) for a in arrs]

    def body(*refs):
        ins, outs = refs[:n], refs[n:2 * n]
        send_sems, recv_sems, local_sems = refs[2 * n:]
        x, y, c = _mesh_pos()
        me = 4 * x + 2 * y + c

        def src(a, idx):
            return ins[a] if gather else ins[a].at[idx]

        local = []
        for a in range(n):
            cp = pltpu.make_async_copy(src(a, me), outs[a].at[me], local_sems.at[a])
            cp.start()
            local.append(cp)
        sends = []
        for k in range(1, N_DEV):
            peer, pidx = _peer(k)
            for a in range(n):
                cp = pltpu.make_async_remote_copy(
                    src_ref=src(a, pidx), dst_ref=outs[a].at[me],
                    send_sem=send_sems.at[a, k - 1], recv_sem=recv_sems.at[a, k - 1],
                    device_id=peer, device_id_type=pl.DeviceIdType.MESH)
                cp.start()
                sends.append(cp)
        for k in range(1, N_DEV):
            peer, pidx = _peer(k)
            for a in range(n):
                pltpu.make_async_remote_copy(
                    src_ref=src(a, pidx), dst_ref=outs[a].at[pidx],
                    send_sem=send_sems.at[a, k - 1], recv_sem=recv_sems.at[a, k - 1],
                    device_id=peer, device_id_type=pl.DeviceIdType.MESH).wait_recv()
        for cp in sends:
            cp.wait_send()
        for cp in local:
            cp.wait()

    any_spec = pl.BlockSpec(memory_space=pl.ANY)
    return pl.pallas_call(
        body, name=name, out_shape=out_shape,
        in_specs=[any_spec] * n, out_specs=[any_spec] * n,
        scratch_shapes=[pltpu.SemaphoreType.DMA((n, N_DEV - 1)),
                        pltpu.SemaphoreType.DMA((n, N_DEV - 1)),
                        pltpu.SemaphoreType.DMA((n,))],
    )(*arrs)


def _gather_two_level(shard, name):
    def body(x_ref, out_ref, send_sems, recv_sems, local_sem):
        x, y, c = _mesh_pos()
        me, sibling = (x, y, c), (x, y, 1 - c)
        chips = [(1 - x, y), (x, 1 - y), (1 - x, 1 - y)]

        def slot(px, py, pc):
            return out_ref.at[4 * px + 2 * py + pc]

        def copy(k, block, to, src=None):
            return pltpu.make_async_remote_copy(
                src_ref=slot(*block) if src is None else src, dst_ref=slot(*block),
                send_sem=send_sems.at[k], recv_sem=recv_sems.at[k],
                device_id=to, device_id_type=pl.DeviceIdType.MESH)

        mine = pltpu.make_async_copy(x_ref, slot(*me), local_sem)
        mine.start()
        first = [copy(1 + j, me, (*chip, c), src=x_ref) for j, chip in enumerate(chips)]
        first.append(copy(0, me, sibling, src=x_ref))
        for cp in first:
            cp.start()
        passed = [copy(4 + j, (*chip, c), sibling) for j, chip in enumerate(chips)]
        for j, chip in enumerate(chips):
            copy(1 + j, (*chip, c), me).wait_recv()
            passed[j].start()
        copy(0, sibling, me).wait_recv()
        for j, chip in enumerate(chips):
            copy(4 + j, (*chip, 1 - c), me).wait_recv()
        for cp in first + passed:
            cp.wait_send()
        mine.wait()

    any_spec = pl.BlockSpec(memory_space=pl.ANY)
    return pl.pallas_call(
        body, name=name, out_shape=jax.ShapeDtypeStruct((N_DEV,) + shard.shape, shard.dtype),
        in_specs=[any_spec], out_specs=any_spec,
        scratch_shapes=[pltpu.SemaphoreType.DMA((7,)), pltpu.SemaphoreType.DMA((7,)),
                        pltpu.SemaphoreType.DMA],
    )(shard)


def _reduce_scatter_two_level(parts, name):
    _, rows, cols = parts.shape
    n_chips = N_DEV // 2

    def body(p_ref, out_ref, mine_buf, sib_buf, send_buf, recv_buf, send_sems, recv_sems,
             local_sems):
        x, y, c = _mesh_pos()
        my_chip = 2 * x + y
        sibling = (x, y, 1 - c)
        stage1, local = [], []
        for q in range(n_chips):
            cp = pltpu.make_async_remote_copy(
                src_ref=p_ref.at[2 * q + (1 - c)], dst_ref=sib_buf.at[q],
                send_sem=send_sems.at[q], recv_sem=recv_sems.at[q],
                device_id=sibling, device_id_type=pl.DeviceIdType.MESH)
            cp.start()
            stage1.append(cp)
            lc = pltpu.make_async_copy(p_ref.at[2 * q + c], mine_buf.at[q], local_sems.at[q])
            lc.start()
            local.append(lc)
        for lc in local:
            lc.wait()
        for cp in stage1:
            cp.wait_recv()
        stage2 = []
        for k in range(1, n_chips):
            px = (1 - x) if (k >> 1) & 1 else x
            py = (1 - y) if k & 1 else y
            q = 2 * px + py
            pair = mine_buf[q].astype(F32) + sib_buf[q].astype(F32)
            send_buf[k - 1] = pair.astype(BF16)
            cp = pltpu.make_async_remote_copy(
                src_ref=send_buf.at[k - 1], dst_ref=recv_buf.at[k - 1],
                send_sem=send_sems.at[n_chips + k - 1], recv_sem=recv_sems.at[n_chips + k - 1],
                device_id=(px, py, c), device_id_type=pl.DeviceIdType.MESH)
            cp.start()
            stage2.append(cp)
        total = mine_buf[my_chip].astype(F32) + sib_buf[my_chip].astype(F32)
        for k in range(1, n_chips):
            stage2[k - 1].wait_recv()
            total = total + recv_buf[k - 1].astype(F32)
        out_ref[...] = total
        for cp in stage1 + stage2:
            cp.wait_send()

    return pl.pallas_call(
        body, name=name, out_shape=jax.ShapeDtypeStruct((rows, cols), F32),
        in_specs=[pl.BlockSpec(memory_space=pl.ANY)],
        out_specs=pl.BlockSpec(memory_space=pltpu.VMEM),
        scratch_shapes=[pltpu.VMEM((n_chips, rows, cols), BF16),
                        pltpu.VMEM((n_chips, rows, cols), BF16),
                        pltpu.VMEM((n_chips - 1, rows, cols), BF16),
                        pltpu.VMEM((n_chips - 1, rows, cols), BF16),
                        pltpu.SemaphoreType.DMA((2 * n_chips - 1,)),
                        pltpu.SemaphoreType.DMA((2 * n_chips - 1,)),
                        pltpu.SemaphoreType.DMA((n_chips,))],
        compiler_params=pltpu.CompilerParams(vmem_limit_bytes=40 * MIB),
    )(parts)


def _remote_copy(gather, src_ref, land_ref, send_sem, recv_sem, k, receive_side):
    x, y, c = _mesh_pos()
    me = 4 * x + 2 * y + c
    peer, pidx = _peer(k)
    return pltpu.make_async_remote_copy(
        src_ref=src_ref if gather else src_ref.at[pidx],
        dst_ref=land_ref.at[pidx if receive_side else me],
        send_sem=send_sem, recv_sem=recv_sem,
        device_id=peer, device_id_type=pl.DeviceIdType.MESH)


def _exchange_start(groups, name, gather):
    arrs = [a for g in groups for a in g]
    n, n_groups = len(arrs), len(groups)
    lands = [jax.ShapeDtypeStruct(((N_DEV,) + a.shape) if gather else a.shape, a.dtype)
             for a in arrs]

    def body(*refs):
        srcs, zones = refs[:n], refs[n:2 * n]
        sems = refs[2 * n:2 * n + 2 * n_groups]
        token = refs[-1]
        a = 0
        for gi, g in enumerate(groups):
            send_sems, recv_sems = sems[2 * gi], sems[2 * gi + 1]
            for k in range(1, N_DEV):
                for ai in range(len(g)):
                    slot = ai * (N_DEV - 1) + k - 1
                    _remote_copy(gather, srcs[a + ai], zones[a + ai], send_sems.at[slot],
                                 recv_sems.at[slot], k, False).start()
            a += len(g)
        token[...] = jnp.zeros_like(token)

    hbm = pl.BlockSpec(memory_space=pltpu.HBM)
    sem = pl.BlockSpec(memory_space=pltpu.SEMAPHORE)
    sem_shapes = []
    for g in groups:
        sem_shapes += [pltpu.SemaphoreType.DMA((len(g) * (N_DEV - 1),))] * 2
    outs = pl.pallas_call(
        body, name=name,
        in_specs=[hbm] * (2 * n),
        out_shape=sem_shapes + [pltpu.HBM(a.shape, a.dtype) for a in arrs]
        + [pltpu.HBM(z.shape, z.dtype) for z in lands] + [jax.ShapeDtypeStruct((8, 128), F32)],
        out_specs=[sem] * (2 * n_groups) + [hbm] * (2 * n)
        + [pl.BlockSpec(memory_space=pltpu.VMEM)],
        input_output_aliases={i: 2 * n_groups + i for i in range(2 * n)},
        compiler_params=pltpu.CompilerParams(
            has_side_effects=pltpu.SideEffectType.DATAFLOW_SIDE_EFFECTING),
    )(*[pltpu.with_memory_space_constraint(a, pltpu.HBM) for a in arrs],
      *[pltpu.with_memory_space_constraint(lax.empty(z.shape, z.dtype), pltpu.HBM) for z in lands])
    sems = outs[:2 * n_groups]
    thru = outs[2 * n_groups:2 * n_groups + n]
    zones = outs[2 * n_groups + n:2 * n_groups + 2 * n]
    handles, a = [], 0
    for gi, g in enumerate(groups):
        handles.append((sems[2 * gi], sems[2 * gi + 1], thru[a:a + len(g)], zones[a:a + len(g)]))
        a += len(g)
    return handles, outs[-1]


def _exchange_wait(handle, after, name, gather):
    send_sems, recv_sems, thru, zones = handle
    n = len(thru)

    def body(*refs):
        srcs, lands = refs[:n], refs[n:2 * n]
        ssem, rsem = refs[2 * n], refs[2 * n + 1]
        for k in range(1, N_DEV):
            for ai in range(n):
                slot = ai * (N_DEV - 1) + k - 1
                cp = _remote_copy(gather, srcs[ai], lands[ai], ssem.at[slot], rsem.at[slot], k, True)
                cp.wait_send()
                cp.wait_recv()

    hbm = pl.BlockSpec(memory_space=pltpu.HBM)
    sem = pl.BlockSpec(memory_space=pltpu.SEMAPHORE)
    outs = pl.pallas_call(
        body, name=name,
        in_specs=[hbm] * (2 * n) + [sem, sem, pl.BlockSpec(memory_space=pl.ANY)],
        out_shape=[pltpu.HBM(a.shape, a.dtype) for a in thru]
        + [pltpu.HBM(z.shape, z.dtype) for z in zones],
        out_specs=[hbm] * (2 * n),
        input_output_aliases={i: i for i in range(2 * n)},
        compiler_params=pltpu.CompilerParams(
            has_side_effects=pltpu.SideEffectType.DATAFLOW_SIDE_EFFECTING),
    )(*thru, *zones, send_sems, recv_sems, after)
    return outs[n:]


def _own_block(zone, block):
    x, y, c = _mesh_pos()
    me = 4 * x + 2 * y + c
    return lax.dynamic_update_slice_in_dim(zone, block[None], me, axis=0)


def _proj_fwd(x, g1, wcat, bdiag, gq, gk, bfor, tri, pdq, pdk, ones_q, ones_k):
    s_len = x.shape[0]
    tm = TOKEN_TILE
    nt = s_len // tm

    def body(x_ref, g1_ref, w_ref, bd_ref, gq_ref, gk_ref, bf_ref, tri_ref, pdq_ref,
             pdk_ref, oq_ref, ok_ref,
             h_ref, qa_ref, ka_ref, kat_ref, vs_ref, vt_ref, qr_ref, kr_ref, flog_ref, uv_ref,
             gp_ref, carry):
        @pl.when(pl.program_id(0) == 0)
        def _():
            carry[...] = jnp.zeros_like(carry)

        xf = x_ref[...]
        r = lax.rsqrt(jnp.mean(xf * xf, axis=-1, keepdims=True) + EPS)
        h = (xf * r * g1_ref[...]).astype(BF16)
        h_ref[...] = h
        dot = functools.partial(jnp.dot, preferred_element_type=F32)

        def proj(lo, hi):
            return _dot_nt(h, w_ref[lo:hi, :])

        flog = proj(C_F, C_UV) + bf_ref[...]
        flog_ref[...] = flog
        lane = lax.broadcasted_iota(jnp.int32, flog.shape, 1)
        logf = jnp.minimum(flog, 0.0) - jnp.log(1.0 + jnp.exp(-jnp.abs(flog)))
        logf = jnp.where(lane < HEADS, logf, 0.0)
        dcum = _tri_dot(tri_ref[...], logf) + carry[...]
        carry[...] = dcum[tm - 1:tm, :]
        d2 = dcum * LOG2E
        d2a = d2.astype(BF16)
        rem = d2 - d2a.astype(F32)
        d2b = rem.astype(BF16)
        d2c = (rem - d2b.astype(F32)).astype(BF16)

        q = proj(C_Q, C_K)
        qr_ref[...] = q.astype(BF16)
        rq = lax.rsqrt(_seg_mean(q * q, bd_ref) + EPS)
        qn = q * rq * (gq_ref[...] * (HEAD_DIM ** -0.5 * LOG2E))
        qa = (_slabs_from_heads(qn) + dot(d2a, pdq_ref[0]) + dot(d2b, pdq_ref[1])
              + dot(d2c, pdq_ref[2]) + oq_ref[...])
        qa_ref[...] = qa.astype(BF16)

        k = proj(C_K, C_V)
        kr_ref[...] = k.astype(BF16)
        rk = lax.rsqrt(_seg_mean(k * k, bd_ref) + EPS)
        kn = k * rk * gk_ref[...]
        ka = (_slabs_from_heads(kn) - dot(d2a, pdk_ref[0]) - dot(d2b, pdk_ref[1])
              - dot(d2c, pdk_ref[2]) + ok_ref[...])
        ka_ref[...] = ka.astype(BF16)
        kat_ref[0] = ka.T.astype(BF16)

        v = proj(C_V, C_F)
        vs_ref[...] = _slabs_from_heads(v).astype(BF16)
        vt_ref[0] = v.T.astype(BF16)
        uv_ref[...] = proj(C_UV, C_G).astype(BF16)
        gp_ref[...] = proj(C_G, C_END).astype(BF16)

    outs = [((s_len, D_MODEL), BF16, _row_spec(tm, D_MODEL)),
            ((s_len, SLAB_W), BF16, _row_spec(tm, SLAB_W)),
            ((s_len, SLAB_W), BF16, _row_spec(tm, SLAB_W)),
            ((nt, SLAB_W, tm), BF16, _tile_spec(SLAB_W, tm)),
            ((s_len, SLAB_W), BF16, _row_spec(tm, SLAB_W)),
            ((nt, FOX_W, tm), BF16, _tile_spec(FOX_W, tm)),
            ((s_len, FOX_W), BF16, _row_spec(tm, FOX_W)),
            ((s_len, FOX_W), BF16, _row_spec(tm, FOX_W)),
            ((s_len, 128), F32, _row_spec(tm, 128)),
            ((s_len, 2 * SGU_W), BF16, _row_spec(tm, 2 * SGU_W)),
            ((s_len, 2 * D_MODEL), BF16, _row_spec(tm, 2 * D_MODEL))]
    return pl.pallas_call(
        body, name="proj_fwd", grid=(nt,),
        in_specs=[_row_spec(tm, D_MODEL), _const_spec((1, D_MODEL)), _const_spec(wcat.shape),
                  _const_spec(bdiag.shape), _const_spec((1, FOX_W)), _const_spec((1, FOX_W)),
                  _const_spec((1, 128)), _const_spec((tm, tm)), _const_spec(pdq.shape), _const_spec(pdk.shape), _const_spec(ones_q.shape),
                  _const_spec(ones_k.shape)],
        out_specs=[o[2] for o in outs],
        out_shape=[jax.ShapeDtypeStruct(o[0], o[1]) for o in outs],
        scratch_shapes=[pltpu.VMEM((1, 128), F32)],
        compiler_params=_params(56, 1),
    )(x, g1, wcat, bdiag, gq, gk, bfor, tri, pdq, pdk, ones_q, ones_k)


def _attn_fwd(qa, ka, vt):
    s_len = qa.shape[0]
    t = ATTN_TILE
    nb = s_len // t

    def body(q_ref, k_ref, vt_ref, o_ref, lse_ref, m_sc, l_sc, acc_sc, s_sc, alpha_sc):
        i = pl.program_id(0)
        m_sc[...] = jnp.full_like(m_sc, -jnp.inf)
        l_sc[...] = jnp.zeros_like(l_sc)
        acc_sc[...] = jnp.zeros_like(acc_sc)

        def tile(j, masked):
            krows = pl.ds(pl.multiple_of(j * t, t), t)
            if masked:
                keep = (lax.broadcasted_iota(jnp.int32, (t, t), 0)
                        <= lax.broadcasted_iota(jnp.int32, (t, t), 1))
            for hd in range(HEADS):
                sl = slice(hd * 128, (hd + 1) * 128)
                st = _dot_nt(k_ref[krows, sl], q_ref[:, sl])
                if masked:
                    st = jnp.where(keep, st, -jnp.inf)
                s_sc[hd] = st
                m_prev = m_sc[hd:hd + 1, :]
                m_new = jnp.maximum(m_prev, jnp.max(st, axis=0, keepdims=True))
                alpha_sc[hd:hd + 1, :] = jnp.exp2(m_prev - m_new)
                m_sc[hd:hd + 1, :] = m_new
            for hd in range(HEADS):
                hr = slice(hd * HEAD_DIM, (hd + 1) * HEAD_DIM)
                alpha = alpha_sc[hd:hd + 1, :]
                pt = jnp.exp2(s_sc[hd] - m_sc[hd:hd + 1, :])
                l_sc[hd:hd + 1, :] = alpha * l_sc[hd:hd + 1, :] + jnp.sum(pt, axis=0, keepdims=True)
                acc_sc[hr, :] = alpha * acc_sc[hr, :] + jnp.dot(
                    vt_ref[j, hr, :], pt.astype(BF16), preferred_element_type=F32)

        def off_diagonal(j, carry):
            tile(j, False)
            return carry

        lax.fori_loop(0, i, off_diagonal, 0)
        tile(i, True)

        for hd in range(HEADS):
            hr = slice(hd * HEAD_DIM, (hd + 1) * HEAD_DIM)
            l = l_sc[hd:hd + 1, :]
            acc_sc[hr, :] = acc_sc[hr, :] / l
            lse_ref[0, hd:hd + 1, :] = m_sc[hd:hd + 1, :] + jnp.log2(l)
        o_ref[...] = acc_sc[...].T.astype(BF16)

    return pl.pallas_call(
        body, name="attn_fwd", grid=(nb,),
        in_specs=[_row_spec(t, SLAB_W), _const_spec(ka.shape), _const_spec(vt.shape)],
        out_specs=[_row_spec(t, FOX_W), _tile_spec(HEADS, t)],
        out_shape=[jax.ShapeDtypeStruct((s_len, FOX_W), BF16),
                   jax.ShapeDtypeStruct((nb, HEADS, t), F32)],
        scratch_shapes=[pltpu.VMEM((HEADS, t), F32), pltpu.VMEM((HEADS, t), F32),
                        pltpu.VMEM((FOX_W, t), F32), pltpu.VMEM((HEADS, t, t), F32),
                        pltpu.VMEM((HEADS, t), F32)],
        compiler_params=_params(48, 1),
    )(qa, ka, vt)


def _sgu_mix(vn, ws_ref):
    tm = vn.shape[0]
    lane = lax.broadcasted_iota(jnp.int32, (WINDOW, 128), 1)
    low = lane < HEAD_DIM
    wins = []
    for w in range(tm // WINDOW):
        slabs = []
        for p in range(GROUPS // 2):
            v2 = vn[w * WINDOW:(w + 1) * WINDOW, p * 128:(p + 1) * 128]
            lo = jnp.where(low, v2, 0.0).astype(BF16)
            hi = jnp.where(low, 0.0, v2).astype(BF16)
            slabs.append(jnp.dot(ws_ref[2 * p], lo, preferred_element_type=F32)
                         + jnp.dot(ws_ref[2 * p + 1], hi, preferred_element_type=F32))
        wins.append(jnp.concatenate(slabs, axis=1))
    return jnp.concatenate(wins, axis=0) if len(wins) > 1 else wins[0]


def _layernorm_fwd(vv, g, b):
    mu = jnp.mean(vv, axis=-1, keepdims=True)
    xc = vv - mu
    r = lax.rsqrt(jnp.mean(xc * xc, axis=-1, keepdims=True) + EPS)
    xh = xc * r
    return xh * g + b, xh, r


def _mix_fwd(attn, uvpre, gpre, x, wa, wb, wout, wsm, bsf, gsgu, bsgu, gpost):
    s_len = x.shape[0]
    tm = TOKEN_TILE

    def body(o_ref, uv_ref, gp_ref, x_ref, wa_ref, wb_ref, wo_ref, ws_ref, bs_ref, gs_ref, bsg_ref,
             gpost_ref, sgu_ref, ya_ref, yb_ref, mg_ref, om_ref, x1_ref):
        uvp = uv_ref[...].astype(F32)
        uv, _ = _gelu_and_grad(uvp)
        u, vv = uv[:, :SGU_W], uv[:, SGU_W:]
        vn, _, _ = _layernorm_fwd(vv, gs_ref[...], bsg_ref[...])
        bias = bs_ref[...]
        if tm > WINDOW:
            bias = jnp.concatenate([bias] * (tm // WINDOW), axis=0)
        mixed = _sgu_mix(vn, ws_ref) + bias
        sgu = (u * mixed).astype(BF16)
        sgu_ref[...] = sgu
        ya = jnp.dot(o_ref[...], wa_ref[...], preferred_element_type=F32)
        yb = jnp.dot(sgu, wb_ref[...], preferred_element_type=F32)
        ya_ref[...] = ya.astype(BF16)
        yb_ref[...] = yb.astype(BF16)
        gates = _sigmoid(gp_ref[...].astype(F32))
        merged = (gates[:, :D_MODEL] * ya + gates[:, D_MODEL:] * yb).astype(BF16)
        mg_ref[...] = merged
        om = jnp.dot(merged, wo_ref[...], preferred_element_type=F32)
        om_ref[...] = om
        r = lax.rsqrt(jnp.mean(om * om, axis=-1, keepdims=True) + EPS)
        x1_ref[...] = x_ref[...] + om * r * gpost_ref[...]

    outs = [(SGU_W, BF16), (D_MODEL, BF16), (D_MODEL, BF16), (D_MODEL, BF16), (D_MODEL, F32),
            (D_MODEL, F32)]
    return pl.pallas_call(
        body, name="mix_fwd", grid=(s_len // tm,),
        in_specs=[_row_spec(tm, FOX_W), _row_spec(tm, 2 * SGU_W), _row_spec(tm, 2 * D_MODEL),
                  _row_spec(tm, D_MODEL), _const_spec(wa.shape), _const_spec(wb.shape),
                  _const_spec(wout.shape), _const_spec(wsm.shape), _const_spec(bsf.shape),
                  _const_spec((1, SGU_W)), _const_spec((1, SGU_W)), _const_spec((1, D_MODEL))],
        out_specs=[_row_spec(tm, c) for c, _ in outs],
        out_shape=[jax.ShapeDtypeStruct((s_len, c), dt) for c, dt in outs],
        compiler_params=_params(48, 1),
    )(attn, uvpre, gpre, x, wa, wb, wout, wsm, bsf, gsgu, bsgu, gpost)


def _ffn_fwd_bwd(x1, tgt, wffn, wdown, gpre, gpost):
    s_len = x1.shape[0]
    tm = TOKEN_TILE

    def body(x1_ref, t_ref, wi_ref, wd_ref, gpre_ref, gpost_ref,
             dx1_ref, h2_ref, act_ref, dff_ref, dgu_ref, loss_ref, dgpost_ref, dgpre_ref):
        @pl.when(pl.program_id(0) == 0)
        def _():
            loss_ref[...] = jnp.zeros_like(loss_ref)
            dgpost_ref[...] = jnp.zeros_like(dgpost_ref)
            dgpre_ref[...] = jnp.zeros_like(dgpre_ref)

        x1v = x1_ref[...]
        r2 = lax.rsqrt(jnp.mean(x1v * x1v, axis=-1, keepdims=True) + EPS)
        gpre_v = gpre_ref[...]
        h2 = (x1v * r2 * gpre_v).astype(BF16)
        h2_ref[...] = h2
        gg = _dot_nt(h2, wi_ref[:D_FF, :])
        uu = _dot_nt(h2, wi_ref[D_FF:, :])
        sg = _sigmoid(gg)
        silu = gg * sg
        act = (silu * uu).astype(BF16)
        act_ref[...] = act
        ff = jnp.dot(act, wd_ref[...], preferred_element_type=F32)
        r3 = lax.rsqrt(jnp.mean(ff * ff, axis=-1, keepdims=True) + EPS)
        gpost_v = gpost_ref[...]
        y = x1v + ff * r3 * gpost_v
        err = y - t_ref[...]
        loss_ref[...] += jnp.sum(err * err) * (0.5 / D_MODEL)
        dy = err * (1.0 / D_MODEL)
        dgpost_ref[...] += jnp.sum(dy * ff * r3, axis=0, keepdims=True)
        dff = _rms_bwd(ff, r3, gpost_v, dy).astype(BF16)
        dff_ref[...] = dff
        dact = _dot_nt(dff, wd_ref[...])
        dgg = (dact * uu * (sg * (1.0 + gg * (1.0 - sg)))).astype(BF16)
        duu = (dact * silu).astype(BF16)
        dgu_ref[:, :D_FF] = dgg
        dgu_ref[:, D_FF:] = duu
        dh2 = (jnp.dot(dgg, wi_ref[:D_FF, :], preferred_element_type=F32)
               + jnp.dot(duu, wi_ref[D_FF:, :], preferred_element_type=F32))
        dgpre_ref[...] += jnp.sum(dh2 * x1v * r2, axis=0, keepdims=True)
        dx1_ref[...] = dy + _rms_bwd(x1v, r2, gpre_v, dh2)

    outs = [((s_len, D_MODEL), F32, _row_spec(tm, D_MODEL)),
            ((s_len, D_MODEL), BF16, _row_spec(tm, D_MODEL)),
            ((s_len, D_FF), BF16, _row_spec(tm, D_FF)),
            ((s_len, D_MODEL), BF16, _row_spec(tm, D_MODEL)),
            ((s_len, 2 * D_FF), BF16, _row_spec(tm, 2 * D_FF)),
            ((1, 128), F32, _const_spec((1, 128))),
            ((1, D_MODEL), F32, _const_spec((1, D_MODEL))),
            ((1, D_MODEL), F32, _const_spec((1, D_MODEL)))]
    return pl.pallas_call(
        body, name="ffn_fwd_bwd", grid=(s_len // tm,),
        in_specs=[_row_spec(tm, D_MODEL), _row_spec(tm, D_MODEL), _const_spec(wffn.shape),
                  _const_spec(wdown.shape), _const_spec((1, D_MODEL)), _const_spec((1, D_MODEL))],
        out_specs=[o[2] for o in outs],
        out_shape=[jax.ShapeDtypeStruct(o[0], o[1]) for o in outs],
        compiler_params=_params(60, 1),
    )(x1, tgt, wffn, wdown, gpre, gpost)


def _mix_bwd(dx1, om, ya, yb, gpre, uvpre, attn, wout, wa, wb, wsm, wsmt, bsf, gsgu, bsgu, gpost,
             wmask, egrp):
    s_len = dx1.shape[0]
    tm = TOKEN_TILE
    nw = tm // WINDOW
    nt = s_len // tm

    def body(dx1_ref, om_ref, ya_ref, yb_ref, gp_ref, uv_ref, o_ref, wo_ref, wa_ref, wb_ref, ws_ref,
             wst_ref, bs_ref, gs_ref, bsg_ref, gpost_ref, mask_ref, eg_ref,
             dom_ref, dya_ref, dyb_ref, dgp_ref, dot_ref, delta_ref, duv_ref,
             dws_ref, dbs_ref, dgs_ref, dbsg_ref, dgpost_ref, dbs_acc):
        step = pl.program_id(0)

        @pl.when(step == 0)
        def _():
            dws_ref[...] = jnp.zeros_like(dws_ref)
            dbs_acc[...] = jnp.zeros_like(dbs_acc)
            dgs_ref[...] = jnp.zeros_like(dgs_ref)
            dbsg_ref[...] = jnp.zeros_like(dbsg_ref)
            dgpost_ref[...] = jnp.zeros_like(dgpost_ref)

        om = om_ref[...]
        dx1v = dx1_ref[...]
        r = lax.rsqrt(jnp.mean(om * om, axis=-1, keepdims=True) + EPS)
        gpost_v = gpost_ref[...]
        dgpost_ref[...] += jnp.sum(dx1v * om * r, axis=0, keepdims=True)
        dom = _rms_bwd(om, r, gpost_v, dx1v).astype(BF16)
        dom_ref[...] = dom
        dmg = _dot_nt(dom, wo_ref[...])

        gates = _sigmoid(gp_ref[...].astype(F32))
        ga, gb = gates[:, :D_MODEL], gates[:, D_MODEL:]
        yav, ybv = ya_ref[...].astype(F32), yb_ref[...].astype(F32)
        dya = (dmg * ga).astype(BF16)
        dyb = (dmg * gb).astype(BF16)
        dya_ref[...] = dya
        dyb_ref[...] = dyb
        dgp_ref[:, :D_MODEL] = (dmg * yav * ga * (1.0 - ga)).astype(BF16)
        dgp_ref[:, D_MODEL:] = (dmg * ybv * gb * (1.0 - gb)).astype(BF16)

        dat_t = _dot_nt(dya, wa_ref[...]).T.astype(BF16)
        dot_ref[0] = dat_t
        o_t = o_ref[...].astype(F32).T
        delta_ref[0] = jnp.sum((dat_t.astype(F32) * o_t).reshape(HEADS, HEAD_DIM, tm), axis=1)
        dsgu = _dot_nt(dyb, wb_ref[...])

        uvp = uv_ref[...].astype(F32)
        uv, guv = _gelu_and_grad(uvp)
        u, vv = uv[:, :SGU_W], uv[:, SGU_W:]
        gs_v = gs_ref[...]
        vn, xh, rln = _layernorm_fwd(vv, gs_v, bsg_ref[...])
        bias = bs_ref[...]
        if nw > 1:
            bias = jnp.concatenate([bias] * nw, axis=0)
        mixed = _sgu_mix(vn, ws_ref) + bias
        du = dsgu * mixed
        dmixed = dsgu * u

        lane = lax.broadcasted_iota(jnp.int32, (WINDOW, 128), 1)
        low = lane < HEAD_DIM
        dvn_wins = []
        for w in range(nw):
            rows = slice(w * WINDOW, (w + 1) * WINDOW)
            dbs_acc[...] += dmixed[rows, :]
            slabs = []
            for p in range(GROUPS // 2):
                cols = slice(p * 128, (p + 1) * 128)
                dm2 = dmixed[rows, cols]
                dlo = jnp.where(low, dm2, 0.0).astype(BF16)
                dhi = jnp.where(low, 0.0, dm2).astype(BF16)
                vn2 = vn[rows, cols].astype(BF16)
                dws_ref[2 * p] += _dot_nt(dlo, vn2)
                dws_ref[2 * p + 1] += _dot_nt(dhi, vn2)
                slabs.append(jnp.dot(wst_ref[2 * p], dlo, preferred_element_type=F32)
                             + jnp.dot(wst_ref[2 * p + 1], dhi, preferred_element_type=F32))
            dvn_wins.append(jnp.concatenate(slabs, axis=1))
        dvn = jnp.concatenate(dvn_wins, axis=0) if nw > 1 else dvn_wins[0]

        dgs_ref[...] += jnp.sum(dvn * xh, axis=0, keepdims=True)
        dbsg_ref[...] += jnp.sum(dvn, axis=0, keepdims=True)
        dxh = dvn * gs_v
        dvv = rln * (dxh - jnp.mean(dxh, axis=-1, keepdims=True)
                     - xh * jnp.mean(dxh * xh, axis=-1, keepdims=True))
        duv_ref[:, :SGU_W] = (du * guv[:, :SGU_W]).astype(BF16)
        duv_ref[:, SGU_W:] = (dvv * guv[:, SGU_W:]).astype(BF16)

        @pl.when(step == pl.num_programs(0) - 1)
        def _():
            for g in range(GROUPS):
                dws_ref[g] = dws_ref[g] * mask_ref[...]
            dbs_ref[...] = _split3_dot(dbs_acc[...], eg_ref[...])

    rows_out = [((s_len, D_MODEL), BF16, _row_spec(tm, D_MODEL)),
                ((s_len, D_MODEL), BF16, _row_spec(tm, D_MODEL)),
                ((s_len, D_MODEL), BF16, _row_spec(tm, D_MODEL)),
                ((s_len, 2 * D_MODEL), BF16, _row_spec(tm, 2 * D_MODEL)),
                ((nt, FOX_W, tm), BF16, _tile_spec(FOX_W, tm)),
                ((nt, HEADS, tm), F32, _tile_spec(HEADS, tm)),
                ((s_len, 2 * SGU_W), BF16, _row_spec(tm, 2 * SGU_W))]
    acc_out = [((GROUPS, WINDOW, WINDOW), F32), ((WINDOW, 128), F32), ((1, SGU_W), F32),
               ((1, SGU_W), F32), ((1, D_MODEL), F32)]
    return pl.pallas_call(
        body, name="mix_bwd", grid=(nt,),
        in_specs=[_row_spec(tm, D_MODEL), _row_spec(tm, D_MODEL), _row_spec(tm, D_MODEL),
                  _row_spec(tm, D_MODEL), _row_spec(tm, 2 * D_MODEL), _row_spec(tm, 2 * SGU_W),
                  _row_spec(tm, FOX_W), _const_spec(wout.shape), _const_spec(wa.shape),
                  _const_spec(wb.shape), _const_spec(wsm.shape), _const_spec(wsmt.shape),
                  _const_spec(bsf.shape), _const_spec((1, SGU_W)), _const_spec((1, SGU_W)),
                  _const_spec((1, D_MODEL)), _const_spec(wmask.shape), _const_spec(egrp.shape)],
        out_specs=[o[2] for o in rows_out] + [_const_spec(s) for s, _ in acc_out],
        out_shape=[jax.ShapeDtypeStruct(o[0], o[1]) for o in rows_out]
        + [jax.ShapeDtypeStruct(s, dt) for s, dt in acc_out],
        scratch_shapes=[pltpu.VMEM((WINDOW, SGU_W), F32)],
        compiler_params=_params(48, 1),
    )(dx1, om, ya, yb, gpre, uvpre, attn, wout, wa, wb, wsm, wsmt, bsf, gsgu, bsgu, gpost, wmask,
      egrp)


def _attn_bwd(qa, ka, kat, vs, dot_, lse, delta, ecol):
    s_len = qa.shape[0]
    t = ATTN_TILE
    nb = s_len // t

    def body(k_ref, kt_ref, vs_ref, q_ref, do_ref, lse_ref, dl_ref, ec_ref, gk_ref, dvt_ref,
             gqt_ref, csum_ref, p_sc, ds_sc):
        j = pl.program_id(0)

        @pl.when(j == 0)
        def _():
            gqt_ref[...] = jnp.zeros_like(gqt_ref)

        gk_ref[...] = jnp.zeros_like(gk_ref)
        dvt_ref[...] = jnp.zeros_like(dvt_ref)

        def tile(i, masked):
            qrows = pl.ds(pl.multiple_of(i * t, t), t)
            if masked:
                keep = (lax.broadcasted_iota(jnp.int32, (t, t), 0)
                        <= lax.broadcasted_iota(jnp.int32, (t, t), 1))
            for hd in range(HEADS):
                sl = slice(hd * 128, (hd + 1) * 128)
                hr = slice(hd * HEAD_DIM, (hd + 1) * HEAD_DIM)
                st = _dot_nt(k_ref[:, sl], q_ref[qrows, sl])
                if masked:
                    st = jnp.where(keep, st, -jnp.inf)
                pt = jnp.exp2(st - lse_ref[i, hd:hd + 1, :])
                dpt = jnp.dot(vs_ref[:, hd * 128:hd * 128 + HEAD_DIM], do_ref[i, hr, :],
                              preferred_element_type=F32)
                p_sc[hd] = pt.astype(BF16)
                ds_sc[hd] = (pt * (dpt - dl_ref[i, hd:hd + 1, :])).astype(BF16)
            for hd in range(HEADS):
                sl = slice(hd * 128, (hd + 1) * 128)
                hr = slice(hd * HEAD_DIM, (hd + 1) * HEAD_DIM)
                dst = ds_sc[hd]
                dvt_ref[0, hr, :] += _dot_nt(do_ref[i, hr, :], p_sc[hd])
                gk_ref[:, sl] += jnp.dot(dst, q_ref[qrows, sl], preferred_element_type=F32)
                gqt_ref[i, hd * QT_ROWS:(hd + 1) * QT_ROWS, :] += jnp.dot(
                    kt_ref[0, hd * 128:hd * 128 + QT_ROWS, :], dst, preferred_element_type=F32)

        tile(j, True)

        def below_diagonal(i, carry):
            tile(i, False)
            return carry

        lax.fori_loop(j + 1, nb, below_diagonal, 0)
        csum_ref[...] = _split3_dot(gk_ref[...], ec_ref[...])

    return pl.pallas_call(
        body, name="attn_bwd", grid=(nb,),
        in_specs=[_row_spec(t, SLAB_W), _tile_spec(SLAB_W, t), _row_spec(t, SLAB_W),
                  _const_spec(qa.shape), _const_spec(dot_.shape), _const_spec(lse.shape),
                  _const_spec(delta.shape), _const_spec(ecol.shape)],
        out_specs=[_row_spec(t, SLAB_W), _tile_spec(FOX_W, t),
                   _const_spec((nb, HEADS * QT_ROWS, t)), _row_spec(t, 128)],
        out_shape=[jax.ShapeDtypeStruct((s_len, SLAB_W), F32),
                   jax.ShapeDtypeStruct((nb, FOX_W, t), F32),
                   jax.ShapeDtypeStruct((nb, HEADS * QT_ROWS, t), F32),
                   jax.ShapeDtypeStruct((s_len, 128), F32)],
        scratch_shapes=[pltpu.VMEM((HEADS, t, t), BF16), pltpu.VMEM((HEADS, t, t), BF16)],
        compiler_params=_params(60, 1),
    )(ka, kat, vs, qa, dot_, lse, delta, ecol)


def _rev_cumsum(col_sums, gqt, triu):
    s_len = col_sums.shape[0]
    tm = TOKEN_TILE
    n = s_len // tm

    def body(cs_ref, gqt_ref, tri_ref, o_ref, carry):
        @pl.when(pl.program_id(0) == 0)
        def _():
            carry[...] = jnp.zeros_like(carry)
        rows = [gqt_ref[0, hd * QT_ROWS + HEAD_DIM:hd * QT_ROWS + HEAD_DIM + 1, :]
                for hd in range(HEADS)]
        row_sums = jnp.concatenate(rows + [jnp.zeros((128 - HEADS, tm), F32)], axis=0).T
        out = _tri_dot(tri_ref[...], row_sums - cs_ref[...]) + carry[...]
        o_ref[...] = out
        carry[...] = out[0:1, :]

    return pl.pallas_call(
        body, name="rev_cumsum", grid=(n,),
        in_specs=[pl.BlockSpec((tm, 128), lambda i: (n - 1 - i, 0)),
                  pl.BlockSpec((1, HEADS * QT_ROWS, tm), lambda i: (n - 1 - i, 0, 0)),
                  _const_spec((tm, tm))],
        out_specs=pl.BlockSpec((tm, 128), lambda i: (n - 1 - i, 0)),
        out_shape=jax.ShapeDtypeStruct((s_len, 128), F32),
        scratch_shapes=[pltpu.VMEM((1, 128), F32)],
        compiler_params=_params(32, 1),
    )(col_sums, gqt, triu)


def _heads_from_slabs(slabs):
    lane = lax.broadcasted_iota(jnp.int32, slabs[0].shape, 1)
    low = lane < HEAD_DIM
    pairs = [jnp.where(low, slabs[2 * p], pltpu.roll(slabs[2 * p + 1], HEAD_DIM, 1))
             for p in range(HEADS // 2)]
    return jnp.concatenate(pairs, axis=1)


def _proj_bwd(gqt, gk, dvt, dlogf, flog, qraw, kraw, duv, dgp, x, dx1, wcat, bdiag, gq, gk_gain, g1,
              efold):
    s_len = x.shape[0]
    tm = TOKEN_TILE

    def body(gqt_ref, gkk_ref, dvt_ref, dlf_ref, flog_ref, qr_ref, kr_ref, duv_ref, dgp_ref, x_ref,
             dx1_ref, w_ref, bd_ref, gq_ref, gk_ref, g1_ref, ef_ref,
             dx_ref, dproj_ref, dgq_ref, dgk_ref, dbf_ref, dg1_ref, gq_acc, gk_acc):
        step = pl.program_id(0)

        @pl.when(step == 0)
        def _():
            gq_acc[...] = jnp.zeros_like(gq_acc)
            gk_acc[...] = jnp.zeros_like(gk_acc)
            dbf_ref[...] = jnp.zeros_like(dbf_ref)
            dg1_ref[...] = jnp.zeros_like(dg1_ref)

        pad = jnp.zeros((128 - QT_ROWS, tm), F32)
        q_slabs = [jnp.concatenate([gqt_ref[0, hd * QT_ROWS:(hd + 1) * QT_ROWS, :], pad], axis=0).T
                   for hd in range(HEADS)]
        dqn = _heads_from_slabs(q_slabs)
        dkn = _heads_from_slabs([gkk_ref[:, hd * 128:(hd + 1) * 128] for hd in range(HEADS)])

        def head_bwd(raw_ref, dn, g_ref, acc):
            raw = raw_ref[...].astype(F32)
            r = lax.rsqrt(_seg_mean(raw * raw, bd_ref) + EPS)
            xhat = raw * r
            acc[0:1, :] += jnp.sum(dn * xhat, axis=0, keepdims=True)
            dyg = dn * g_ref[...]
            return r * (dyg - xhat * _seg_mean(dyg * xhat, bd_ref))

        dproj_ref[:, C_Q:C_K] = head_bwd(qr_ref, dqn * HEAD_DIM ** -0.5, gq_ref, gq_acc).astype(BF16)
        dproj_ref[:, C_K:C_V] = head_bwd(kr_ref, dkn * LN2, gk_ref, gk_acc).astype(BF16)
        dproj_ref[:, C_V:C_F] = dvt_ref[0].T.astype(BF16)
        dfl = dlf_ref[...] * _sigmoid(-flog_ref[...])
        dbf_ref[...] += jnp.sum(dfl, axis=0, keepdims=True)
        dproj_ref[:, C_F:C_UV] = dfl.astype(BF16)
        dproj_ref[:, C_UV:C_G] = duv_ref[...]
        dproj_ref[:, C_G:C_END] = dgp_ref[...]

        dh = jnp.dot(dproj_ref[...], w_ref[...], preferred_element_type=F32)
        xf = x_ref[...]
        r = lax.rsqrt(jnp.mean(xf * xf, axis=-1, keepdims=True) + EPS)
        dg1_ref[...] += jnp.sum(dh * xf * r, axis=0, keepdims=True)
        dx_ref[...] = dx1_ref[...] + _rms_bwd(xf, r, g1_ref[...], dh)

        @pl.when(step == pl.num_programs(0) - 1)
        def _():
            dgq_ref[...] = _split3_dot(gq_acc[...], ef_ref[...])
            dgk_ref[...] = _split3_dot(gk_acc[...], ef_ref[...])

    outs = [((s_len, D_MODEL), F32, _row_spec(tm, D_MODEL)),
            ((s_len, C_END), BF16, _row_spec(tm, C_END)),
            ((8, 128), F32, _const_spec((8, 128))),
            ((8, 128), F32, _const_spec((8, 128))),
            ((1, 128), F32, _const_spec((1, 128))),
            ((1, D_MODEL), F32, _const_spec((1, D_MODEL)))]
    return pl.pallas_call(
        body, name="proj_bwd", grid=(s_len // tm,),
        in_specs=[_tile_spec(HEADS * QT_ROWS, tm), _row_spec(tm, SLAB_W), _tile_spec(FOX_W, tm),
                  _row_spec(tm, 128), _row_spec(tm, 128), _row_spec(tm, FOX_W),
                  _row_spec(tm, FOX_W), _row_spec(tm, 2 * SGU_W), _row_spec(tm, 2 * D_MODEL),
                  _row_spec(tm, D_MODEL), _row_spec(tm, D_MODEL), _const_spec(wcat.shape),
                  _const_spec(bdiag.shape), _const_spec((1, FOX_W)), _const_spec((1, FOX_W)),
                  _const_spec((1, D_MODEL)), _const_spec(efold.shape)],
        out_specs=[o[2] for o in outs],
        out_shape=[jax.ShapeDtypeStruct(o[0], o[1]) for o in outs],
        scratch_shapes=[pltpu.VMEM((8, FOX_W), F32), pltpu.VMEM((8, FOX_W), F32)],
        compiler_params=_params(56, 1),
    )(gqt, gk, dvt, dlogf, flog, qraw, kraw, duv, dgp, x, dx1, wcat, bdiag, gq, gk_gain, g1, efold)


def _dw_matmul(a, b, tm, name, transpose_out=False):
    s_len, m = a.shape
    n = b.shape[1]
    tk = min(512, s_len)
    nk = s_len // tk

    def body(a_ref, b_ref, o_ref, acc):
        kk = pl.program_id(1)

        @pl.when(kk == 0)
        def _():
            acc[...] = jnp.zeros_like(acc)
        acc[...] += _dot_tn(a_ref[...], b_ref[...])

        @pl.when(kk == nk - 1)
        def _():
            res = acc[...]
            o_ref[...] = (res.T if transpose_out else res).astype(BF16)

    if transpose_out:
        out_spec = pl.BlockSpec((n, tm), lambda i, k: (0, i))
        out_shape = jax.ShapeDtypeStruct((n, m), BF16)
    else:
        out_spec = pl.BlockSpec((tm, n), lambda i, k: (i, 0))
        out_shape = jax.ShapeDtypeStruct((m, n), BF16)
    return pl.pallas_call(
        body, name=name, grid=(m // tm, nk),
        in_specs=[pl.BlockSpec((tk, tm), lambda i, k: (k, i)),
                  pl.BlockSpec((tk, n), lambda i, k: (k, 0))],
        out_specs=out_spec, out_shape=out_shape,
        scratch_shapes=[pltpu.VMEM((tm, n), F32)],
        compiler_params=_params(56, 2),
    )(a, b)


def _adamw(parts, w, m, v, tr, name, col_tile=None):
    n, rows, cols = parts.shape
    bc1 = 1.0 - ADAM_B1 ** ADAM_STEP
    bc2 = 1.0 - ADAM_B2 ** ADAM_STEP

    def body(p_ref, w_ref, m_ref, v_ref, g_ref, d_ref, mo_ref, vo_ref):
        g = p_ref[0].astype(F32)
        for idx in range(1, n):
            g = g + p_ref[idx].astype(F32)
        g_ref[...] = g
        mn = ADAM_B1 * m_ref[...] + (1.0 - ADAM_B1) * g
        vn = ADAM_B2 * v_ref[...] + (1.0 - ADAM_B2) * (g * g)
        mo_ref[...] = mn
        vo_ref[...] = vn
        m_hat = mn / bc1
        v_hat = vn / bc2
        d_ref[...] = -ADAM_LR * (m_hat / (jnp.sqrt(v_hat) + ADAM_EPS) + ADAM_WD * w_ref[...])

    if col_tile is None:
        spec = pl.BlockSpec((tr, cols), lambda i: (i, 0))
        pspec = pl.BlockSpec((n, tr, cols), lambda i: (0, i, 0))
        steps = rows // tr
    else:
        spec = pl.BlockSpec((rows, col_tile), lambda i: (0, i))
        pspec = pl.BlockSpec((n, rows, col_tile), lambda i: (0, 0, i))
        steps = cols // col_tile
    return pl.pallas_call(
        body, name=name, grid=(steps,),
        in_specs=[pspec, spec, spec, spec],
        out_specs=[spec] * 4,
        out_shape=[jax.ShapeDtypeStruct((rows, cols), F32)] * 4,
        compiler_params=_params(48, 1),
    )(parts, w, m, v)


def _sum_parts(parts, name):
    n, rows, cols = parts.shape

    def body(p_ref, o_ref):
        g = p_ref[0]
        for idx in range(1, n):
            g = g + p_ref[idx]
        o_ref[...] = g

    return pl.pallas_call(
        body, name=name, out_shape=jax.ShapeDtypeStruct((rows, cols), F32),
        in_specs=[_const_spec(parts.shape)], out_specs=_const_spec((rows, cols)), grid=(1,),
        compiler_params=_params(16, 1),
    )(parts)


SMALL_NAMES = ("g_pre_mix", "b_forget", "g_q", "g_k", "g_sgu", "b_sgu", "w_spatial", "b_spatial",
               "g_post_mix", "g_pre_ffn", "g_post_ffn")


def _small_rows(size):
    return -(-size // 1024)


def _pack_small(d):
    rows = []
    for k in SMALL_NAMES:
        flat = d[k].reshape(-1).astype(F32)
        nr = _small_rows(flat.shape[0])
        rows.append(jnp.pad(flat, (0, nr * 1024 - flat.shape[0])).reshape(nr, 1024))
    used = sum(r.shape[0] for r in rows)
    rows.append(jnp.zeros((N_DEV * SMALL_ROWS - used, 1024), F32))
    return jnp.concatenate(rows, axis=0)


def _unpack_small(packed, shapes):
    out, off = {}, 0
    for k in SMALL_NAMES:
        size = math.prod(shapes[k])
        nr = _small_rows(size)
        out[k] = packed[off:off + nr].reshape(-1)[:size].reshape(shapes[k])
        off += nr
    return out


def _cols_to_blocks(full, width):
    r = full.shape[0]
    return jnp.transpose(full.reshape(r, N_DEV, width), (1, 0, 2))


def _blocks_to_cols(blocks):
    n, r, width = blocks.shape
    return jnp.transpose(blocks, (1, 0, 2)).reshape(r, n * width)


def kernel(x, g_pre_mix, w_in, b_forget, g_q, g_k, g_sgu, b_sgu, w_spatial, b_spatial, w_branch_a, w_branch_b, w_out, g_post_mix, g_pre_ffn, w_ffn_in, w_ffn_down, g_post_ffn, loss_target, m_g_pre_mix, m_w_in, m_b_forget, m_g_q, m_g_k, m_g_sgu, m_b_sgu, m_w_spatial, m_b_spatial, m_w_branch_a, m_w_branch_b, m_w_out, m_g_post_mix, m_g_pre_ffn, m_w_ffn_in, m_w_ffn_down, m_g_post_ffn, v_g_pre_mix, v_w_in, v_b_forget, v_g_q, v_g_k, v_g_sgu, v_b_sgu, v_w_spatial, v_b_spatial, v_w_branch_a, v_w_branch_b, v_w_out, v_g_post_mix, v_g_pre_ffn, v_w_ffn_in, v_w_ffn_down, v_g_post_ffn):
    big_names = ("w_in", "w_branch_a", "w_branch_b", "w_out", "w_ffn_in", "w_ffn_down")
    weights = dict(g_pre_mix=g_pre_mix, w_in=w_in, b_forget=b_forget, g_q=g_q, g_k=g_k, g_sgu=g_sgu,
                   b_sgu=b_sgu, w_spatial=w_spatial, b_spatial=b_spatial, w_branch_a=w_branch_a,
                   w_branch_b=w_branch_b, w_out=w_out, g_post_mix=g_post_mix, g_pre_ffn=g_pre_ffn,
                   w_ffn_in=w_ffn_in, w_ffn_down=w_ffn_down, g_post_ffn=g_post_ffn)
    mom1 = dict(g_pre_mix=m_g_pre_mix, w_in=m_w_in, b_forget=m_b_forget, g_q=m_g_q, g_k=m_g_k,
                g_sgu=m_g_sgu, b_sgu=m_b_sgu, w_spatial=m_w_spatial, b_spatial=m_b_spatial,
                w_branch_a=m_w_branch_a, w_branch_b=m_w_branch_b, w_out=m_w_out,
                g_post_mix=m_g_post_mix, g_pre_ffn=m_g_pre_ffn, w_ffn_in=m_w_ffn_in,
                w_ffn_down=m_w_ffn_down, g_post_ffn=m_g_post_ffn)
    mom2 = dict(g_pre_mix=v_g_pre_mix, w_in=v_w_in, b_forget=v_b_forget, g_q=v_g_q, g_k=v_g_k,
                g_sgu=v_g_sgu, b_sgu=v_b_sgu, w_spatial=v_w_spatial, b_spatial=v_b_spatial,
                w_branch_a=v_w_branch_a, w_branch_b=v_w_branch_b, w_out=v_w_out,
                g_post_mix=v_g_post_mix, g_pre_ffn=v_g_pre_ffn, w_ffn_in=v_w_ffn_in,
                w_ffn_down=v_w_ffn_down, g_post_ffn=v_g_post_ffn)
    names = list(weights)
    shapes = {k: weights[k].shape for k in names}

    s_len = x.shape[1]
    xs = x.reshape(s_len, D_MODEL)
    tgt = loss_target.reshape(s_len, D_MODEL)

    transposed = ("w_in", "w_ffn_in")

    def local_view(a, k):
        return jnp.transpose(a[0]) if k in transposed else a[0]

    shards = {k: local_view(weights[k], k).astype(BF16) for k in big_names}
    win_t = _gather_two_level(shards["w_in"], "gather_w_in").reshape(IN_COLS, D_MODEL)
    win_t, later = lax.optimization_barrier(
        (win_t, [shards[k] for k in big_names if k != "w_in"]))
    shards.update(zip([k for k in big_names if k != "w_in"], later))
    (gat_mix, gat_ffn), gat_token = _exchange_start(
        [[shards["w_branch_a"], shards["w_branch_b"], shards["w_out"]],
         [shards["w_ffn_in"], shards["w_ffn_down"]]], "gather_start", gather=True)
    f_off = 3 * FOX_W
    u_off = f_off + HEADS
    wcat = jnp.concatenate([
        win_t[:f_off], jnp.pad(win_t[f_off:u_off], ((0, 128 - HEADS), (0, 0))), win_t[u_off:]],
        axis=0)

    seg = jnp.arange(FOX_W) // HEAD_DIM
    bdiag = (seg[:128, None] == seg[None, :128]).astype(BF16)
    tm = TOKEN_TILE
    tril = (jnp.arange(tm)[None, :] <= jnp.arange(tm)[:, None]).astype(BF16)
    triu = tril.T
    egrp = (seg[:, None] == jnp.arange(128)[None, :]).astype(BF16)
    efold = ((jnp.arange(FOX_W) % HEAD_DIM)[:, None] == jnp.arange(128)[None, :]).astype(BF16)
    gq512 = jnp.tile(g_q.reshape(1, HEAD_DIM), (1, HEADS))
    gk512 = jnp.tile(g_k.reshape(1, HEAD_DIM), (1, HEADS))
    bfor = jnp.pad(b_forget.reshape(1, HEADS), ((0, 0), (0, 128 - HEADS)))
    pos = jnp.arange(WINDOW)
    wmask = ((pos[None, :] // CHUNK) <= (pos[:, None] // CHUNK))
    wsm_f = jnp.where(wmask[None], w_spatial[0], 0.0)
    wsm = wsm_f.astype(BF16)
    wsmt = jnp.transpose(wsm_f, (0, 2, 1)).astype(BF16)
    bsf = jnp.repeat(jnp.transpose(b_spatial[0]), HEAD_DIM, axis=1)
    wmask_f = wmask.astype(F32)

    col = jnp.arange(SLAB_W)
    row128 = jnp.arange(128)

    def d_place(first):
        return jnp.stack([((col[None, :] // 128 == row128[:, None])
                           & (col[None, :] % 128 == first + a)).astype(BF16) for a in range(3)])

    pdq, pdk = d_place(HEAD_DIM), d_place(HEAD_DIM + 3)
    ones_q = ((col % 128 >= HEAD_DIM + 3) & (col % 128 < HEAD_DIM + 6)).astype(F32)[None]
    ones_k = ((col % 128 >= HEAD_DIM) & (col % 128 < HEAD_DIM + 3)).astype(F32)[None]
    ecol = ((col[:, None] // 128 == row128[None, :])
            & (col[:, None] % 128 == HEAD_DIM + 3)).astype(BF16)

    (h, qa, ka, kat, vs, vt, qraw, kraw, flog, uvpre, gpre) = _proj_fwd(
        xs, g_pre_mix + gat_token[0:1, 0:1], wcat, bdiag, gq512, gk512, bfor, tril, pdq, pdk,
        ones_q, ones_k)
    attn, lse = _attn_fwd(qa, ka, vt)
    zone_a, zone_b, zone_out = _exchange_wait(gat_mix, attn, "gather_wait_mix", gather=True)
    wa = _blocks_to_cols(_own_block(zone_a, shards["w_branch_a"]))
    wb = _blocks_to_cols(_own_block(zone_b, shards["w_branch_b"]))
    wout = _own_block(zone_out, shards["w_out"]).reshape(D_MODEL, D_MODEL)
    sgu, ya, yb, merged, om, x1 = _mix_fwd(attn, uvpre, gpre, xs, wa, wb, wout, wsm, bsf,
                                           g_sgu, b_sgu, g_post_mix)
    zone_ffn, zone_down = _exchange_wait(gat_ffn, x1, "gather_wait_ffn", gather=True)
    wffn = _own_block(zone_ffn, shards["w_ffn_in"]).reshape(2 * D_FF, D_MODEL)
    wdown = _own_block(zone_down, shards["w_ffn_down"]).reshape(D_FF, D_MODEL)
    (dx1, h2, act, dff, dgu, loss_acc, dg_post_ffn, dg_pre_ffn) = _ffn_fwd_bwd(
        x1, tgt, wffn, wdown, g_pre_ffn, g_post_ffn)

    dw_down = _dw_matmul(act, dff, D_FF // 2, "dw_down")
    dw_ffn = _dw_matmul(h2, dgu, 512, "dw_ffn_in", transpose_out=True)
    parts_ffn = [dw_ffn.reshape(N_DEV, 2 * D_FF // N_DEV, D_MODEL),
                 dw_down.reshape(N_DEV, D_FF // N_DEV, D_MODEL)]
    (sct_ffn,), sct_ffn_token = _exchange_start([parts_ffn], "scatter_start_ffn", gather=False)

    (dom, dya, dyb, dgp, dot_, delta, duv, dws, dbs, dg_sgu, db_sgu, dg_post_mix) = _mix_bwd(
        dx1, om, ya, yb, gpre, uvpre, attn, wout, wa, wb, wsm, wsmt, bsf, g_sgu, b_sgu,
        g_post_mix + sct_ffn_token[0:1, 0:1], wmask_f, egrp)
    dw_out = _dw_matmul(merged, dom, 512, "dw_out")
    dw_a = _dw_matmul(attn, dya, 512, "dw_a")
    dw_b = _dw_matmul(sgu, dyb, 512, "dw_b")
    parts_mix = [_cols_to_blocks(dw_a, D_MODEL // N_DEV), _cols_to_blocks(dw_b, D_MODEL // N_DEV),
                 dw_out.reshape(N_DEV, D_MODEL // N_DEV, D_MODEL)]
    (sct_mix,), sct_mix_token = _exchange_start([parts_mix], "scatter_start_mix", gather=False)

    gk_all, dvt, gqt, col_sums = _attn_bwd(qa, ka, kat, vs, dot_, lse,
                                           delta + sct_mix_token[0, 0], ecol)
    dlogf = _rev_cumsum(col_sums, gqt, triu)
    dx, dproj, dgq, dgk, dbf, dg_pre_mix = _proj_bwd(
        gqt, gk_all, dvt, dlogf, flog, qraw, kraw, duv, dgp, xs, dx1, wcat, bdiag, gq512, gk512,
        g_pre_mix, efold)
    dw_cat = _dw_matmul(h, dproj, 512, "dw_in", transpose_out=True)
    dw_in = jnp.concatenate([dw_cat[:C_F + HEADS], dw_cat[C_UV:]], axis=0)

    small_local = dict(
        g_pre_mix=dg_pre_mix, b_forget=dbf[:, :HEADS], g_q=dgq[0:1, :HEAD_DIM],
        g_k=dgk[0:1, :HEAD_DIM], g_sgu=dg_sgu, b_sgu=db_sgu, w_spatial=dws,
        b_spatial=jnp.transpose(dbs[:, :GROUPS]), g_post_mix=dg_post_mix, g_pre_ffn=dg_pre_ffn,
        g_post_ffn=dg_post_ffn)
    small_parts = _pack_small(small_local).reshape(N_DEV, SMALL_ROWS, 1024)

    grad_in_t = _reduce_scatter_two_level(dw_in.reshape(N_DEV, IN_COLS // N_DEV, D_MODEL),
                                          "reduce_scatter_in")
    (recv_small,) = _exchange([small_parts], "scatter_small", gather=False)
    x_pos, y_pos, c_pos = _mesh_pos()
    me = 4 * x_pos + 2 * y_pos + c_pos

    def with_own(zones, parts):
        return [_own_block(z, lax.dynamic_index_in_dim(p, me, 0, keepdims=False))
                for z, p in zip(zones, parts)]

    recv_ffn, recv_down = with_own(
        _exchange_wait(sct_ffn, recv_small, "scatter_wait_ffn", gather=False), parts_ffn)
    recv_a, recv_b, recv_out = with_own(
        _exchange_wait(sct_mix, recv_ffn, "scatter_wait_mix", gather=False), parts_mix)
    received = [grad_in_t[None], recv_a, recv_b, recv_out, recv_ffn, recv_down]

    grads, deltas, new_m, new_v = {}, {}, {}, {}
    row_tiles = {"w_in": None, "w_branch_a": 512, "w_branch_b": 512, "w_out": 128, "w_ffn_in": 176,
                 "w_ffn_down": 352}
    for idx, k in enumerate(big_names):
        outs = _adamw(received[idx], local_view(weights[k], k), local_view(mom1[k], k),
                      local_view(mom2[k], k), row_tiles[k], "adamw_" + k,
                      col_tile=256 if k == "w_in" else None)
        if k in transposed:
            outs = [jnp.transpose(o) for o in outs]
        grads[k], deltas[k], new_m[k], new_v[k] = [o[None] for o in outs]

    small_sum = _sum_parts(recv_small, "sum_small")
    (small_all,) = _exchange([small_sum], "gather_small", gather=True)
    small_all = small_all.reshape(1, N_DEV * SMALL_ROWS, 1024)
    sg, sd, sm, sv = _adamw(small_all, _pack_small(weights), _pack_small(mom1), _pack_small(mom2),
                            N_DEV * SMALL_ROWS, "adamw_small")
    for dst, packed in ((grads, sg), (deltas, sd), (new_m, sm), (new_v, sv)):
        dst.update(_unpack_small(packed, shapes))

    loss = lax.psum(loss_acc[0, 0], ("x", "y", "c"))
    return (loss, dx.reshape(x.shape), *[grads[k] for k in names], *[deltas[k] for k in names],
            *[new_m[k] for k in names], *[new_v[k] for k in names])
```

```python
import functools
import math

import jax
import jax.numpy as jnp
from jax import lax
from jax.experimental import pallas as pl
from jax.experimental.pallas import tpu as pltpu

F32 = jnp.float32
BF16 = jnp.bfloat16

D_MODEL = 1024
FOX_W = 512
HEADS = 8
HEAD_DIM = 64
SGU_W = 512
GROUPS = 8
WINDOW = 128
CHUNK = 64
D_FF = 2816
IN_COLS = 4616
EPS = 1e-6
N_DEV = 8
LOG2E = 1.4426950408889634
LN2 = 0.6931471805599453

C_Q, C_K, C_V, C_F, C_UV, C_G, C_END = 0, 512, 1024, 1536, 1664, 2688, 4736

ADAM_LR, ADAM_B1, ADAM_B2, ADAM_EPS, ADAM_WD, ADAM_STEP = 0.001, 0.9, 0.999, 1e-08, 0.01, 10

MIB = 1024 * 1024
TOKEN_TILE = 256
ATTN_TILE = 256
SLAB_W = HEADS * 128
QT_ROWS = 72

SMALL_ROWS = 18


def _params(vmem_mib, n_axes):
    return pltpu.CompilerParams(
        dimension_semantics=("arbitrary",) * n_axes, vmem_limit_bytes=vmem_mib * MIB)


def _const_spec(shape):
    nd = len(shape)
    return pl.BlockSpec(shape, lambda *_: (0,) * nd)


def _row_spec(tm, cols):
    return pl.BlockSpec((tm, cols), lambda i: (i, 0))


def _tile_spec(rows, tm):
    return pl.BlockSpec((1, rows, tm), lambda i: (i, 0, 0))


def _split3_dot(x, e):
    x1 = x.astype(BF16)
    r1 = x - x1.astype(F32)
    x2 = r1.astype(BF16)
    x3 = (r1 - x2.astype(F32)).astype(BF16)
    dot = functools.partial(jnp.dot, preferred_element_type=F32)
    return dot(x1, e) + dot(x2, e) + dot(x3, e)


def _tri_dot(tri, x):
    x1 = x.astype(BF16)
    r1 = x - x1.astype(F32)
    x2 = r1.astype(BF16)
    x3 = (r1 - x2.astype(F32)).astype(BF16)
    dot = functools.partial(jnp.dot, preferred_element_type=F32)
    return dot(tri, x1) + dot(tri, x2) + dot(tri, x3)


def _seg_mean(sq, bd_ref):
    hi = sq.astype(BF16)
    lo = (sq - hi.astype(F32)).astype(BF16)
    bd = bd_ref[...]
    dot = functools.partial(jnp.dot, preferred_element_type=F32)
    pairs = [dot(hi[:, p * 128:(p + 1) * 128], bd) + dot(lo[:, p * 128:(p + 1) * 128], bd)
             for p in range(HEADS // 2)]
    return jnp.concatenate(pairs, axis=1) * (1.0 / HEAD_DIM)


def _slabs_from_heads(t):
    lane = lax.broadcasted_iota(jnp.int32, (t.shape[0], 128), 1)
    low = lane < HEAD_DIM
    slabs = []
    for p in range(HEADS // 2):
        pair = t[:, p * 128:(p + 1) * 128]
        slabs.append(jnp.where(low, pair, 0.0))
        slabs.append(jnp.where(low, pltpu.roll(pair, HEAD_DIM, 1), 0.0))
    return jnp.concatenate(slabs, axis=1)


def _dot_nt(a, b):
    return lax.dot_general(a, b, (((1,), (1,)), ((), ())), preferred_element_type=F32)


def _dot_tn(a, b):
    return lax.dot_general(a, b, (((0,), (0,)), ((), ())), preferred_element_type=F32)


def _sigmoid(x):
    return 1.0 / (1.0 + jnp.exp(-x))


_GELU_C = math.sqrt(2.0 / math.pi)


def _gelu_and_grad(x):
    inner = _GELU_C * (x + 0.044715 * x * x * x)
    t = jnp.tanh(inner)
    y = 0.5 * x * (1.0 + t)
    dy = 0.5 * (1.0 + t) + 0.5 * x * (1.0 - t * t) * _GELU_C * (1.0 + 3.0 * 0.044715 * x * x)
    return y, dy


def _rms_bwd(xin, r, g, dy):
    dyg = dy * g
    return r * dyg - xin * (r * r * r) * jnp.mean(dyg * xin, axis=-1, keepdims=True)


def _mesh_pos():
    x, y, c = lax.axis_index("x"), lax.axis_index("y"), lax.axis_index("c")
    return x, y, c


def _peer(k):
    x, y, c = _mesh_pos()
    px = (1 - x) if (k >> 2) & 1 else x
    py = (1 - y) if (k >> 1) & 1 else y
    pc = (1 - c) if k & 1 else c
    return (px, py, pc), 4 * px + 2 * py + pc


def _exchange(arrs, name, gather):
    n = len(arrs)
    if gather:
        out_shape = [jax.ShapeDtypeStruct((N_DEV,) + a.shape, a.dtype) for a in arrs]
    else:
        out_shape = [jax.ShapeDtypeStruct(a.shape, a.dtype) for a in arrs]

    def body(*refs):
        ins, outs = refs[:n], refs[n:2 * n]
        send_sems, recv_sems, local_sems = refs[2 * n:]
        x, y, c = _mesh_pos()
        me = 4 * x + 2 * y + c

        def src(a, idx):
            return ins[a] if gather else ins[a].at[idx]

        local = []
        for a in range(n):
            cp = pltpu.make_async_copy(src(a, me), outs[a].at[me], local_sems.at[a])
            cp.start()
            local.append(cp)
        sends = []
        for k in range(1, N_DEV):
            peer, pidx = _peer(k)
            for a in range(n):
                cp = pltpu.make_async_remote_copy(
                    src_ref=src(a, pidx), dst_ref=outs[a].at[me],
                    send_sem=send_sems.at[a, k - 1], recv_sem=recv_sems.at[a, k - 1],
                    device_id=peer, device_id_type=pl.DeviceIdType.MESH)
                cp.start()
                sends.append(cp)
        for k in range(1, N_DEV):
            peer, pidx = _peer(k)
            for a in range(n):
                pltpu.make_async_remote_copy(
                    src_ref=src(a, pidx), dst_ref=outs[a].at[pidx],
                    send_sem=send_sems.at[a, k - 1], recv_sem=recv_sems.at[a, k - 1],
                    device_id=peer, device_id_type=pl.DeviceIdType.MESH).wait_recv()
        for cp in sends:
            cp.wait_send()
        for cp in local:
            cp.wait()

    any_spec = pl.BlockSpec(memory_space=pl.ANY)
    return pl.pallas_call(
        body, name=name, out_shape=out_shape,
        in_specs=[any_spec] * n, out_specs=[any_spec] * n,
        scratch_shapes=[pltpu.SemaphoreType.DMA((n, N_DEV - 1)),
                        pltpu.SemaphoreType.DMA((n, N_DEV - 1)),
                        pltpu.SemaphoreType.DMA((n,))],
    )(*arrs)


def _gather_two_level(shard, name):
    def body(x_ref, out_ref, send_sems, recv_sems, local_sem):
        x, y, c = _mesh_pos()
        me, sibling = (x, y, c), (x, y, 1 - c)
        chips = [(1 - x, y), (x, 1 - y), (1 - x, 1 - y)]

        def slot(px, py, pc):
            return out_ref.at[4 * px + 2 * py + pc]

        def copy(k, block, to, src=None):
            return pltpu.make_async_remote_copy(
                src_ref=slot(*block) if src is None else src, dst_ref=slot(*block),
                send_sem=send_sems.at[k], recv_sem=recv_sems.at[k],
                device_id=to, device_id_type=pl.DeviceIdType.MESH)

        mine = pltpu.make_async_copy(x_ref, slot(*me), local_sem)
        mine.start()
        first = [copy(1 + j, me, (*chip, c), src=x_ref) for j, chip in enumerate(chips)]
        first.append(copy(0, me, sibling, src=x_ref))
        for cp in first:
            cp.start()
        passed = [copy(4 + j, (*chip, c), sibling) for j, chip in enumerate(chips)]
        for j, chip in enumerate(chips):
            copy(1 + j, (*chip, c), me).wait_recv()
            passed[j].start()
        copy(0, sibling, me).wait_recv()
        for j, chip in enumerate(chips):
            copy(4 + j, (*chip, 1 - c), me).wait_recv()
        for cp in first + passed:
            cp.wait_send()
        mine.wait()

    any_spec = pl.BlockSpec(memory_space=pl.ANY)
    return pl.pallas_call(
        body, name=name, out_shape=jax.ShapeDtypeStruct((N_DEV,) + shard.shape, shard.dtype),
        in_specs=[any_spec], out_specs=any_spec,
        scratch_shapes=[pltpu.SemaphoreType.DMA((7,)), pltpu.SemaphoreType.DMA((7,)),
                        pltpu.SemaphoreType.DMA],
    )(shard)


def _reduce_scatter_two_level(parts, name):
    _, rows, cols = parts.shape
    n_chips = N_DEV // 2

    def body(p_ref, out_ref, mine_buf, sib_buf, send_buf, recv_buf, send_sems, recv_sems,
             local_sems):
        x, y, c = _mesh_pos()
        my_chip = 2 * x + y
        sibling = (x, y, 1 - c)
        stage1, local = [], []
        for q in range(n_chips):
            cp = pltpu.make_async_remote_copy(
                src_ref=p_ref.at[2 * q + (1 - c)], dst_ref=sib_buf.at[q],
                send_sem=send_sems.at[q], recv_sem=recv_sems.at[q],
                device_id=sibling, device_id_type=pl.DeviceIdType.MESH)
            cp.start()
            stage1.append(cp)
            lc = pltpu.make_async_copy(p_ref.at[2 * q + c], mine_buf.at[q], local_sems.at[q])
            lc.start()
            local.append(lc)
        for lc in local:
            lc.wait()
        for cp in stage1:
            cp.wait_recv()
        stage2 = []
        for k in range(1, n_chips):
            px = (1 - x) if (k >> 1) & 1 else x
            py = (1 - y) if k & 1 else y
            q = 2 * px + py
            pair = mine_buf[q].astype(F32) + sib_buf[q].astype(F32)
            send_buf[k - 1] = pair.astype(BF16)
            cp = pltpu.make_async_remote_copy(
                src_ref=send_buf.at[k - 1], dst_ref=recv_buf.at[k - 1],
                send_sem=send_sems.at[n_chips + k - 1], recv_sem=recv_sems.at[n_chips + k - 1],
                device_id=(px, py, c), device_id_type=pl.DeviceIdType.MESH)
            cp.start()
            stage2.append(cp)
        total = mine_buf[my_chip].astype(F32) + sib_buf[my_chip].astype(F32)
        for k in range(1, n_chips):
            stage2[k - 1].wait_recv()
            total = total + recv_buf[k - 1].astype(F32)
        out_ref[...] = total
        for cp in stage1 + stage2:
            cp.wait_send()

    return pl.pallas_call(
        body, name=name, out_shape=jax.ShapeDtypeStruct((rows, cols), F32),
        in_specs=[pl.BlockSpec(memory_space=pl.ANY)],
        out_specs=pl.BlockSpec(memory_space=pltpu.VMEM),
        scratch_shapes=[pltpu.VMEM((n_chips, rows, cols), BF16),
                        pltpu.VMEM((n_chips, rows, cols), BF16),
                        pltpu.VMEM((n_chips - 1, rows, cols), BF16),
                        pltpu.VMEM((n_chips - 1, rows, cols), BF16),
                        pltpu.SemaphoreType.DMA((2 * n_chips - 1,)),
                        pltpu.SemaphoreType.DMA((2 * n_chips - 1,)),
                        pltpu.SemaphoreType.DMA((n_chips,))],
        compiler_params=pltpu.CompilerParams(vmem_limit_bytes=40 * MIB),
    )(parts)


def _remote_copy(gather, src_ref, land_ref, send_sem, recv_sem, k, receive_side):
    x, y, c = _mesh_pos()
    me = 4 * x + 2 * y + c
    peer, pidx = _peer(k)
    return pltpu.make_async_remote_copy(
        src_ref=src_ref if gather else src_ref.at[pidx],
        dst_ref=land_ref.at[pidx if receive_side else me],
        send_sem=send_sem, recv_sem=recv_sem,
        device_id=peer, device_id_type=pl.DeviceIdType.MESH)


def _exchange_start(groups, name, gather):
    arrs = [a for g in groups for a in g]
    n, n_groups = len(arrs), len(groups)
    lands = [jax.ShapeDtypeStruct(((N_DEV,) + a.shape) if gather else a.shape, a.dtype)
             for a in arrs]

    def body(*refs):
        srcs, zones = refs[:n], refs[n:2 * n]
        sems = refs[2 * n:2 * n + 2 * n_groups]
        token = refs[-1]
        a = 0
        for gi, g in enumerate(groups):
            send_sems, recv_sems = sems[2 * gi], sems[2 * gi + 1]
            for k in range(1, N_DEV):
                for ai in range(len(g)):
                    slot = ai * (N_DEV - 1) + k - 1
                    _remote_copy(gather, srcs[a + ai], zones[a + ai], send_sems.at[slot],
                                 recv_sems.at[slot], k, False).start()
            a += len(g)
        token[...] = jnp.zeros_like(token)

    hbm = pl.BlockSpec(memory_space=pltpu.HBM)
    sem = pl.BlockSpec(memory_space=pltpu.SEMAPHORE)
    sem_shapes = []
    for g in groups:
        sem_shapes += [pltpu.SemaphoreType.DMA((len(g) * (N_DEV - 1),))] * 2
    outs = pl.pallas_call(
        body, name=name,
        in_specs=[hbm] * (2 * n),
        out_shape=sem_shapes + [pltpu.HBM(a.shape, a.dtype) for a in arrs]
        + [pltpu.HBM(z.shape, z.dtype) for z in lands] + [jax.ShapeDtypeStruct((8, 128), F32)],
        out_specs=[sem] * (2 * n_groups) + [hbm] * (2 * n)
        + [pl.BlockSpec(memory_space=pltpu.VMEM)],
        input_output_aliases={i: 2 * n_groups + i for i in range(2 * n)},
        compiler_params=pltpu.CompilerParams(
            has_side_effects=pltpu.SideEffectType.DATAFLOW_SIDE_EFFECTING),
    )(*[pltpu.with_memory_space_constraint(a, pltpu.HBM) for a in arrs],
      *[pltpu.with_memory_space_constraint(lax.empty(z.shape, z.dtype), pltpu.HBM) for z in lands])
    sems = outs[:2 * n_groups]
    thru = outs[2 * n_groups:2 * n_groups + n]
    zones = outs[2 * n_groups + n:2 * n_groups + 2 * n]
    handles, a = [], 0
    for gi, g in enumerate(groups):
        handles.append((sems[2 * gi], sems[2 * gi + 1], thru[a:a + len(g)], zones[a:a + len(g)]))
        a += len(g)
    return handles, outs[-1]


def _exchange_wait(handle, after, name, gather):
    send_sems, recv_sems, thru, zones = handle
    n = len(thru)

    def body(*refs):
        srcs, lands = refs[:n], refs[n:2 * n]
        ssem, rsem = refs[2 * n], refs[2 * n + 1]
        for k in range(1, N_DEV):
            for ai in range(n):
                slot = ai * (N_DEV - 1) + k - 1
                cp = _remote_copy(gather, srcs[ai], lands[ai], ssem.at[slot], rsem.at[slot], k, True)
                cp.wait_send()
                cp.wait_recv()

    hbm = pl.BlockSpec(memory_space=pltpu.HBM)
    sem = pl.BlockSpec(memory_space=pltpu.SEMAPHORE)
    outs = pl.pallas_call(
        body, name=name,
        in_specs=[hbm] * (2 * n) + [sem, sem, pl.BlockSpec(memory_space=pl.ANY)],
        out_shape=[pltpu.HBM(a.shape, a.dtype) for a in thru]
        + [pltpu.HBM(z.shape, z.dtype) for z in zones],
        out_specs=[hbm] * (2 * n),
        input_output_aliases={i: i for i in range(2 * n)},
        compiler_params=pltpu.CompilerParams(
            has_side_effects=pltpu.SideEffectType.DATAFLOW_SIDE_EFFECTING),
    )(*thru, *zones, send_sems, recv_sems, after)
    return outs[n:]


def _own_block(zone, block):
    x, y, c = _mesh_pos()
    me = 4 * x + 2 * y + c
    return lax.dynamic_update_slice_in_dim(zone, block[None], me, axis=0)


def _proj_fwd(x, g1, wcat, bdiag, gq, gk, bfor, tri, pdq, pdk, ones_q, ones_k):
    s_len = x.shape[0]
    tm = TOKEN_TILE
    nt = s_len // tm

    def body(x_ref, g1_ref, w_ref, bd_ref, gq_ref, gk_ref, bf_ref, tri_ref, pdq_ref,
             pdk_ref, oq_ref, ok_ref,
             h_ref, qa_ref, ka_ref, kat_ref, vs_ref, vt_ref, qr_ref, kr_ref, flog_ref, uv_ref,
             gp_ref, carry):
        @pl.when(pl.program_id(0) == 0)
        def _():
            carry[...] = jnp.zeros_like(carry)

        xf = x_ref[...]
        r = lax.rsqrt(jnp.mean(xf * xf, axis=-1, keepdims=True) + EPS)
        h = (xf * r * g1_ref[...]).astype(BF16)
        h_ref[...] = h
        dot = functools.partial(jnp.dot, preferred_element_type=F32)

        def proj(lo, hi):
            return _dot_nt(h, w_ref[lo:hi, :])

        flog = proj(C_F, C_UV) + bf_ref[...]
        flog_ref[...] = flog
        lane = lax.broadcasted_iota(jnp.int32, flog.shape, 1)
        logf = jnp.minimum(flog, 0.0) - jnp.log(1.0 + jnp.exp(-jnp.abs(flog)))
        logf = jnp.where(lane < HEADS, logf, 0.0)
        dcum = _tri_dot(tri_ref[...], logf) + carry[...]
        carry[...] = dcum[tm - 1:tm, :]
        d2 = dcum * LOG2E
        d2a = d2.astype(BF16)
        rem = d2 - d2a.astype(F32)
        d2b = rem.astype(BF16)
        d2c = (rem - d2b.astype(F32)).astype(BF16)

        q = proj(C_Q, C_K)
        qr_ref[...] = q.astype(BF16)
        rq = lax.rsqrt(_seg_mean(q * q, bd_ref) + EPS)
        qn = q * rq * (gq_ref[...] * (HEAD_DIM ** -0.5 * LOG2E))
        qa = (_slabs_from_heads(qn) + dot(d2a, pdq_ref[0]) + dot(d2b, pdq_ref[1])
              + dot(d2c, pdq_ref[2]) + oq_ref[...])
        qa_ref[...] = qa.astype(BF16)

        k = proj(C_K, C_V)
        kr_ref[...] = k.astype(BF16)
        rk = lax.rsqrt(_seg_mean(k * k, bd_ref) + EPS)
        kn = k * rk * gk_ref[...]
        ka = (_slabs_from_heads(kn) - dot(d2a, pdk_ref[0]) - dot(d2b, pdk_ref[1])
              - dot(d2c, pdk_ref[2]) + ok_ref[...])
        ka_ref[...] = ka.astype(BF16)
        kat_ref[0] = ka.T.astype(BF16)

        v = proj(C_V, C_F)
        vs_ref[...] = _slabs_from_heads(v).astype(BF16)
        vt_ref[0] = v.T.astype(BF16)
        uv_ref[...] = proj(C_UV, C_G).astype(BF16)
        gp_ref[...] = proj(C_G, C_END).astype(BF16)

    outs = [((s_len, D_MODEL), BF16, _row_spec(tm, D_MODEL)),
            ((s_len, SLAB_W), BF16, _row_spec(tm, SLAB_W)),
            ((s_len, SLAB_W), BF16, _row_spec(tm, SLAB_W)),
            ((nt, SLAB_W, tm), BF16, _tile_spec(SLAB_W, tm)),
            ((s_len, SLAB_W), BF16, _row_spec(tm, SLAB_W)),
            ((nt, FOX_W, tm), BF16, _tile_spec(FOX_W, tm)),
            ((s_len, FOX_W), BF16, _row_spec(tm, FOX_W)),
            ((s_len, FOX_W), BF16, _row_spec(tm, FOX_W)),
            ((s_len, 128), F32, _row_spec(tm, 128)),
            ((s_len, 2 * SGU_W), BF16, _row_spec(tm, 2 * SGU_W)),
            ((s_len, 2 * D_MODEL), BF16, _row_spec(tm, 2 * D_MODEL))]
    return pl.pallas_call(
        body, name="proj_fwd", grid=(nt,),
        in_specs=[_row_spec(tm, D_MODEL), _const_spec((1, D_MODEL)), _const_spec(wcat.shape),
                  _const_spec(bdiag.shape), _const_spec((1, FOX_W)), _const_spec((1, FOX_W)),
                  _const_spec((1, 128)), _const_spec((tm, tm)), _const_spec(pdq.shape), _const_spec(pdk.shape), _const_spec(ones_q.shape),
                  _const_spec(ones_k.shape)],
        out_specs=[o[2] for o in outs],
        out_shape=[jax.ShapeDtypeStruct(o[0], o[1]) for o in outs],
        scratch_shapes=[pltpu.VMEM((1, 128), F32)],
        compiler_params=_params(56, 1),
    )(x, g1, wcat, bdiag, gq, gk, bfor, tri, pdq, pdk, ones_q, ones_k)


def _attn_fwd(qa, ka, vt):
    s_len = qa.shape[0]
    t = ATTN_TILE
    nb = s_len // t

    def body(q_ref, k_ref, vt_ref, o_ref, ot_ref, lse_ref, m_sc, l_sc, acc_sc, s_sc, alpha_sc):
        i = pl.program_id(0)
        m_sc[...] = jnp.full_like(m_sc, -jnp.inf)
        l_sc[...] = jnp.zeros_like(l_sc)
        acc_sc[...] = jnp.zeros_like(acc_sc)

        def tile(j, masked):
            krows = pl.ds(pl.multiple_of(j * t, t), t)
            if masked:
                keep = (lax.broadcasted_iota(jnp.int32, (t, t), 0)
                        <= lax.broadcasted_iota(jnp.int32, (t, t), 1))
            for hd in range(HEADS):
                sl = slice(hd * 128, (hd + 1) * 128)
                st = _dot_nt(k_ref[krows, sl], q_ref[:, sl])
                if masked:
                    st = jnp.where(keep, st, -jnp.inf)
                s_sc[hd] = st
                m_prev = m_sc[hd:hd + 1, :]
                m_new = jnp.maximum(m_prev, jnp.max(st, axis=0, keepdims=True))
                alpha_sc[hd:hd + 1, :] = jnp.exp2(m_prev - m_new)
                m_sc[hd:hd + 1, :] = m_new
            for hd in range(HEADS):
                hr = slice(hd * HEAD_DIM, (hd + 1) * HEAD_DIM)
                alpha = alpha_sc[hd:hd + 1, :]
                pt = jnp.exp2(s_sc[hd] - m_sc[hd:hd + 1, :])
                l_sc[hd:hd + 1, :] = alpha * l_sc[hd:hd + 1, :] + jnp.sum(pt, axis=0, keepdims=True)
                acc_sc[hr, :] = alpha * acc_sc[hr, :] + jnp.dot(
                    vt_ref[j, hr, :], pt.astype(BF16), preferred_element_type=F32)

        def off_diagonal(j, carry):
            tile(j, False)
            return carry

        lax.fori_loop(0, i, off_diagonal, 0)
        tile(i, True)

        for hd in range(HEADS):
            hr = slice(hd * HEAD_DIM, (hd + 1) * HEAD_DIM)
            l = l_sc[hd:hd + 1, :]
            acc_sc[hr, :] = acc_sc[hr, :] / l
            lse_ref[0, hd:hd + 1, :] = m_sc[hd:hd + 1, :] + jnp.log2(l)
        o_ref[...] = acc_sc[...].T.astype(BF16)
        ot_ref[...] = acc_sc[...].astype(BF16)

    return pl.pallas_call(
        body, name="attn_fwd", grid=(nb,),
        in_specs=[_row_spec(t, SLAB_W), _const_spec(ka.shape), _const_spec(vt.shape)],
        out_specs=[_row_spec(t, FOX_W), pl.BlockSpec((FOX_W, t), lambda i: (0, i)),
                   _tile_spec(HEADS, t)],
        out_shape=[jax.ShapeDtypeStruct((s_len, FOX_W), BF16),
                   jax.ShapeDtypeStruct((FOX_W, s_len), BF16),
                   jax.ShapeDtypeStruct((nb, HEADS, t), F32)],
        scratch_shapes=[pltpu.VMEM((HEADS, t), F32), pltpu.VMEM((HEADS, t), F32),
                        pltpu.VMEM((FOX_W, t), F32), pltpu.VMEM((HEADS, t, t), F32),
                        pltpu.VMEM((HEADS, t), F32)],
        compiler_params=_params(48, 1),
    )(qa, ka, vt)


def _sgu_mix(vn, ws_ref):
    tm = vn.shape[0]
    lane = lax.broadcasted_iota(jnp.int32, (WINDOW, 128), 1)
    low = lane < HEAD_DIM
    wins = []
    for w in range(tm // WINDOW):
        slabs = []
        for p in range(GROUPS // 2):
            v2 = vn[w * WINDOW:(w + 1) * WINDOW, p * 128:(p + 1) * 128]
            lo = jnp.where(low, v2, 0.0).astype(BF16)
            hi = jnp.where(low, 0.0, v2).astype(BF16)
            slabs.append(jnp.dot(ws_ref[2 * p], lo, preferred_element_type=F32)
                         + jnp.dot(ws_ref[2 * p + 1], hi, preferred_element_type=F32))
        wins.append(jnp.concatenate(slabs, axis=1))
    return jnp.concatenate(wins, axis=0) if len(wins) > 1 else wins[0]


def _layernorm_fwd(vv, g, b):
    mu = jnp.mean(vv, axis=-1, keepdims=True)
    xc = vv - mu
    r = lax.rsqrt(jnp.mean(xc * xc, axis=-1, keepdims=True) + EPS)
    xh = xc * r
    return xh * g + b, xh, r


def _mix_fwd(attn, uvpre, gpre, x, wa, wb, wout, wsm, bsf, gsgu, bsgu, gpost):
    s_len = x.shape[0]
    tm = TOKEN_TILE

    def body(o_ref, uv_ref, gp_ref, x_ref, wa_ref, wb_ref, wo_ref, ws_ref, bs_ref, gs_ref, bsg_ref,
             gpost_ref, sgut_ref, ya_ref, yb_ref, mgt_ref, om_ref, x1_ref):
        uvp = uv_ref[...].astype(F32)
        uv, _ = _gelu_and_grad(uvp)
        u, vv = uv[:, :SGU_W], uv[:, SGU_W:]
        vn, _, _ = _layernorm_fwd(vv, gs_ref[...], bsg_ref[...])
        bias = bs_ref[...]
        if tm > WINDOW:
            bias = jnp.concatenate([bias] * (tm // WINDOW), axis=0)
        mixed = _sgu_mix(vn, ws_ref) + bias
        sgu_f = u * mixed
        sgu = sgu_f.astype(BF16)
        sgut_ref[...] = sgu_f.T.astype(BF16)
        ya = jnp.dot(o_ref[...], wa_ref[...], preferred_element_type=F32)
        yb = jnp.dot(sgu, wb_ref[...], preferred_element_type=F32)
        ya_ref[...] = ya.astype(BF16)
        yb_ref[...] = yb.astype(BF16)
        gates = _sigmoid(gp_ref[...].astype(F32))
        merged_f = gates[:, :D_MODEL] * ya + gates[:, D_MODEL:] * yb
        merged = merged_f.astype(BF16)
        mgt_ref[...] = merged_f.T.astype(BF16)
        om = jnp.dot(merged, wo_ref[...], preferred_element_type=F32)
        om_ref[...] = om
        r = lax.rsqrt(jnp.mean(om * om, axis=-1, keepdims=True) + EPS)
        x1_ref[...] = x_ref[...] + om * r * gpost_ref[...]

    def t_out(rows):
        return ((rows, s_len), BF16, pl.BlockSpec((rows, tm), lambda i: (0, i)))

    def r_out(cols, dt):
        return ((s_len, cols), dt, _row_spec(tm, cols))

    outs = [t_out(SGU_W), r_out(D_MODEL, BF16), r_out(D_MODEL, BF16), t_out(D_MODEL),
            r_out(D_MODEL, F32), r_out(D_MODEL, F32)]
    return pl.pallas_call(
        body, name="mix_fwd", grid=(s_len // tm,),
        in_specs=[_row_spec(tm, FOX_W), _row_spec(tm, 2 * SGU_W), _row_spec(tm, 2 * D_MODEL),
                  _row_spec(tm, D_MODEL), _const_spec(wa.shape), _const_spec(wb.shape),
                  _const_spec(wout.shape), _const_spec(wsm.shape), _const_spec(bsf.shape),
                  _const_spec((1, SGU_W)), _const_spec((1, SGU_W)), _const_spec((1, D_MODEL))],
        out_specs=[o[2] for o in outs],
        out_shape=[jax.ShapeDtypeStruct(o[0], o[1]) for o in outs],
        compiler_params=_params(48, 1),
    )(attn, uvpre, gpre, x, wa, wb, wout, wsm, bsf, gsgu, bsgu, gpost)


def _ffn_fwd_bwd(x1, tgt, wffn, wdown, gpre, gpost):
    s_len = x1.shape[0]
    tm = TOKEN_TILE

    def body(x1_ref, t_ref, wi_ref, wd_ref, gpre_ref, gpost_ref,
             dx1_ref, h2_ref, actt_ref, dff_ref, dgut_ref, loss_ref, dgpost_ref, dgpre_ref):
        @pl.when(pl.program_id(0) == 0)
        def _():
            loss_ref[...] = jnp.zeros_like(loss_ref)
            dgpost_ref[...] = jnp.zeros_like(dgpost_ref)
            dgpre_ref[...] = jnp.zeros_like(dgpre_ref)

        x1v = x1_ref[...]
        r2 = lax.rsqrt(jnp.mean(x1v * x1v, axis=-1, keepdims=True) + EPS)
        gpre_v = gpre_ref[...]
        h2 = (x1v * r2 * gpre_v).astype(BF16)
        h2_ref[...] = h2
        gg = _dot_nt(h2, wi_ref[:D_FF, :])
        uu = _dot_nt(h2, wi_ref[D_FF:, :])
        sg = _sigmoid(gg)
        silu = gg * sg
        act_f = silu * uu
        act = act_f.astype(BF16)
        actt_ref[...] = act_f.T.astype(BF16)
        ff = jnp.dot(act, wd_ref[...], preferred_element_type=F32)
        r3 = lax.rsqrt(jnp.mean(ff * ff, axis=-1, keepdims=True) + EPS)
        gpost_v = gpost_ref[...]
        y = x1v + ff * r3 * gpost_v
        err = y - t_ref[...]
        loss_ref[...] += jnp.sum(err * err) * (0.5 / D_MODEL)
        dy = err * (1.0 / D_MODEL)
        dgpost_ref[...] += jnp.sum(dy * ff * r3, axis=0, keepdims=True)
        dff = _rms_bwd(ff, r3, gpost_v, dy).astype(BF16)
        dff_ref[...] = dff
        dact = _dot_nt(dff, wd_ref[...])
        dgg_f = dact * uu * (sg * (1.0 + gg * (1.0 - sg)))
        duu_f = dact * silu
        dgg = dgg_f.astype(BF16)
        duu = duu_f.astype(BF16)
        dgut_ref[:D_FF, :] = dgg_f.T.astype(BF16)
        dgut_ref[D_FF:, :] = duu_f.T.astype(BF16)
        dh2 = (jnp.dot(dgg, wi_ref[:D_FF, :], preferred_element_type=F32)
               + jnp.dot(duu, wi_ref[D_FF:, :], preferred_element_type=F32))
        dgpre_ref[...] += jnp.sum(dh2 * x1v * r2, axis=0, keepdims=True)
        dx1_ref[...] = dy + _rms_bwd(x1v, r2, gpre_v, dh2)

    outs = [((s_len, D_MODEL), F32, _row_spec(tm, D_MODEL)),
            ((s_len, D_MODEL), BF16, _row_spec(tm, D_MODEL)),
            ((D_FF, s_len), BF16, pl.BlockSpec((D_FF, tm), lambda i: (0, i))),
            ((s_len, D_MODEL), BF16, _row_spec(tm, D_MODEL)),
            ((2 * D_FF, s_len), BF16, pl.BlockSpec((2 * D_FF, tm), lambda i: (0, i))),
            ((1, 128), F32, _const_spec((1, 128))),
            ((1, D_MODEL), F32, _const_spec((1, D_MODEL))),
            ((1, D_MODEL), F32, _const_spec((1, D_MODEL)))]
    return pl.pallas_call(
        body, name="ffn_fwd_bwd", grid=(s_len // tm,),
        in_specs=[_row_spec(tm, D_MODEL), _row_spec(tm, D_MODEL), _const_spec(wffn.shape),
                  _const_spec(wdown.shape), _const_spec((1, D_MODEL)), _const_spec((1, D_MODEL))],
        out_specs=[o[2] for o in outs],
        out_shape=[jax.ShapeDtypeStruct(o[0], o[1]) for o in outs],
        compiler_params=_params(60, 1),
    )(x1, tgt, wffn, wdown, gpre, gpost)


def _mix_bwd(dx1, om, ya, yb, gpre, uvpre, attn, wout, wa, wb, wsm, wsmt, bsf, gsgu, bsgu, gpost,
             wmask, egrp):
    s_len = dx1.shape[0]
    tm = TOKEN_TILE
    nw = tm // WINDOW
    nt = s_len // tm

    def body(dx1_ref, om_ref, ya_ref, yb_ref, gp_ref, uv_ref, o_ref, wo_ref, wa_ref, wb_ref, ws_ref,
             wst_ref, bs_ref, gs_ref, bsg_ref, gpost_ref, mask_ref, eg_ref,
             dom_ref, dya_ref, dyb_ref, dgp_ref, dot_ref, delta_ref, duv_ref,
             dws_ref, dbs_ref, dgs_ref, dbsg_ref, dgpost_ref, dbs_acc):
        step = pl.program_id(0)

        @pl.when(step == 0)
        def _():
            dws_ref[...] = jnp.zeros_like(dws_ref)
            dbs_acc[...] = jnp.zeros_like(dbs_acc)
            dgs_ref[...] = jnp.zeros_like(dgs_ref)
            dbsg_ref[...] = jnp.zeros_like(dbsg_ref)
            dgpost_ref[...] = jnp.zeros_like(dgpost_ref)

        om = om_ref[...]
        dx1v = dx1_ref[...]
        r = lax.rsqrt(jnp.mean(om * om, axis=-1, keepdims=True) + EPS)
        gpost_v = gpost_ref[...]
        dgpost_ref[...] += jnp.sum(dx1v * om * r, axis=0, keepdims=True)
        dom = _rms_bwd(om, r, gpost_v, dx1v).astype(BF16)
        dom_ref[...] = dom
        dmg = _dot_nt(dom, wo_ref[...])

        gates = _sigmoid(gp_ref[...].astype(F32))
        ga, gb = gates[:, :D_MODEL], gates[:, D_MODEL:]
        yav, ybv = ya_ref[...].astype(F32), yb_ref[...].astype(F32)
        dya = (dmg * ga).astype(BF16)
        dyb = (dmg * gb).astype(BF16)
        dya_ref[...] = dya
        dyb_ref[...] = dyb
        dgp_ref[:, :D_MODEL] = (dmg * yav * ga * (1.0 - ga)).astype(BF16)
        dgp_ref[:, D_MODEL:] = (dmg * ybv * gb * (1.0 - gb)).astype(BF16)

        dat_t = _dot_nt(dya, wa_ref[...]).T.astype(BF16)
        dot_ref[0] = dat_t
        o_t = o_ref[...].astype(F32).T
        delta_ref[0] = jnp.sum((dat_t.astype(F32) * o_t).reshape(HEADS, HEAD_DIM, tm), axis=1)
        dsgu = _dot_nt(dyb, wb_ref[...])

        uvp = uv_ref[...].astype(F32)
        uv, guv = _gelu_and_grad(uvp)
        u, vv = uv[:, :SGU_W], uv[:, SGU_W:]
        gs_v = gs_ref[...]
        vn, xh, rln = _layernorm_fwd(vv, gs_v, bsg_ref[...])
        bias = bs_ref[...]
        if nw > 1:
            bias = jnp.concatenate([bias] * nw, axis=0)
        mixed = _sgu_mix(vn, ws_ref) + bias
        du = dsgu * mixed
        dmixed = dsgu * u

        lane = lax.broadcasted_iota(jnp.int32, (WINDOW, 128), 1)
        low = lane < HEAD_DIM
        dvn_wins = []
        for w in range(nw):
            rows = slice(w * WINDOW, (w + 1) * WINDOW)
            dbs_acc[...] += dmixed[rows, :]
            slabs = []
            for p in range(GROUPS // 2):
                cols = slice(p * 128, (p + 1) * 128)
                dm2 = dmixed[rows, cols]
                dlo = jnp.where(low, dm2, 0.0).astype(BF16)
                dhi = jnp.where(low, 0.0, dm2).astype(BF16)
                vn2 = vn[rows, cols].astype(BF16)
                dws_ref[2 * p] += _dot_nt(dlo, vn2)
                dws_ref[2 * p + 1] += _dot_nt(dhi, vn2)
                slabs.append(jnp.dot(wst_ref[2 * p], dlo, preferred_element_type=F32)
                             + jnp.dot(wst_ref[2 * p + 1], dhi, preferred_element_type=F32))
            dvn_wins.append(jnp.concatenate(slabs, axis=1))
        dvn = jnp.concatenate(dvn_wins, axis=0) if nw > 1 else dvn_wins[0]

        dgs_ref[...] += jnp.sum(dvn * xh, axis=0, keepdims=True)
        dbsg_ref[...] += jnp.sum(dvn, axis=0, keepdims=True)
        dxh = dvn * gs_v
        dvv = rln * (dxh - jnp.mean(dxh, axis=-1, keepdims=True)
                     - xh * jnp.mean(dxh * xh, axis=-1, keepdims=True))
        duv_ref[:, :SGU_W] = (du * guv[:, :SGU_W]).astype(BF16)
        duv_ref[:, SGU_W:] = (dvv * guv[:, SGU_W:]).astype(BF16)

        @pl.when(step == pl.num_programs(0) - 1)
        def _():
            for g in range(GROUPS):
                dws_ref[g] = dws_ref[g] * mask_ref[...]
            dbs_ref[...] = _split3_dot(dbs_acc[...], eg_ref[...])

    rows_out = [((s_len, D_MODEL), BF16, _row_spec(tm, D_MODEL)),
                ((s_len, D_MODEL), BF16, _row_spec(tm, D_MODEL)),
                ((s_len, D_MODEL), BF16, _row_spec(tm, D_MODEL)),
                ((s_len, 2 * D_MODEL), BF16, _row_spec(tm, 2 * D_MODEL)),
                ((nt, FOX_W, tm), BF16, _tile_spec(FOX_W, tm)),
                ((nt, HEADS, tm), F32, _tile_spec(HEADS, tm)),
                ((s_len, 2 * SGU_W), BF16, _row_spec(tm, 2 * SGU_W))]
    acc_out = [((GROUPS, WINDOW, WINDOW), F32), ((WINDOW, 128), F32), ((1, SGU_W), F32),
               ((1, SGU_W), F32), ((1, D_MODEL), F32)]
    return pl.pallas_call(
        body, name="mix_bwd", grid=(nt,),
        in_specs=[_row_spec(tm, D_MODEL), _row_spec(tm, D_MODEL), _row_spec(tm, D_MODEL),
                  _row_spec(tm, D_MODEL), _row_spec(tm, 2 * D_MODEL), _row_spec(tm, 2 * SGU_W),
                  _row_spec(tm, FOX_W), _const_spec(wout.shape), _const_spec(wa.shape),
                  _const_spec(wb.shape), _const_spec(wsm.shape), _const_spec(wsmt.shape),
                  _const_spec(bsf.shape), _const_spec((1, SGU_W)), _const_spec((1, SGU_W)),
                  _const_spec((1, D_MODEL)), _const_spec(wmask.shape), _const_spec(egrp.shape)],
        out_specs=[o[2] for o in rows_out] + [_const_spec(s) for s, _ in acc_out],
        out_shape=[jax.ShapeDtypeStruct(o[0], o[1]) for o in rows_out]
        + [jax.ShapeDtypeStruct(s, dt) for s, dt in acc_out],
        scratch_shapes=[pltpu.VMEM((WINDOW, SGU_W), F32)],
        compiler_params=_params(48, 1),
    )(dx1, om, ya, yb, gpre, uvpre, attn, wout, wa, wb, wsm, wsmt, bsf, gsgu, bsgu, gpost, wmask,
      egrp)


def _attn_bwd(qa, ka, kat, vs, dot_, lse, delta, ecol):
    s_len = qa.shape[0]
    t = ATTN_TILE
    nb = s_len // t

    def body(k_ref, kt_ref, vs_ref, q_ref, do_ref, lse_ref, dl_ref, ec_ref, gk_ref, dvt_ref,
             gqt_ref, csum_ref, p_sc, ds_sc):
        j = pl.program_id(0)

        @pl.when(j == 0)
        def _():
            gqt_ref[...] = jnp.zeros_like(gqt_ref)

        gk_ref[...] = jnp.zeros_like(gk_ref)
        dvt_ref[...] = jnp.zeros_like(dvt_ref)

        def tile(i, masked):
            qrows = pl.ds(pl.multiple_of(i * t, t), t)
            if masked:
                keep = (lax.broadcasted_iota(jnp.int32, (t, t), 0)
                        <= lax.broadcasted_iota(jnp.int32, (t, t), 1))
            for hd in range(HEADS):
                sl = slice(hd * 128, (hd + 1) * 128)
                hr = slice(hd * HEAD_DIM, (hd + 1) * HEAD_DIM)
                st = _dot_nt(k_ref[:, sl], q_ref[qrows, sl])
                if masked:
                    st = jnp.where(keep, st, -jnp.inf)
                pt = jnp.exp2(st - lse_ref[i, hd:hd + 1, :])
                dpt = jnp.dot(vs_ref[:, hd * 128:hd * 128 + HEAD_DIM], do_ref[i, hr, :],
                              preferred_element_type=F32)
                p_sc[hd] = pt.astype(BF16)
                ds_sc[hd] = (pt * (dpt - dl_ref[i, hd:hd + 1, :])).astype(BF16)
            for hd in range(HEADS):
                sl = slice(hd * 128, (hd + 1) * 128)
                hr = slice(hd * HEAD_DIM, (hd + 1) * HEAD_DIM)
                dst = ds_sc[hd]
                dvt_ref[0, hr, :] += _dot_nt(do_ref[i, hr, :], p_sc[hd])
                gk_ref[:, sl] += jnp.dot(dst, q_ref[qrows, sl], preferred_element_type=F32)
                gqt_ref[i, hd * QT_ROWS:(hd + 1) * QT_ROWS, :] += jnp.dot(
                    kt_ref[0, hd * 128:hd * 128 + QT_ROWS, :], dst, preferred_element_type=F32)

        tile(j, True)

        def below_diagonal(i, carry):
            tile(i, False)
            return carry

        lax.fori_loop(j + 1, nb, below_diagonal, 0)
        csum_ref[...] = _split3_dot(gk_ref[...], ec_ref[...])

    return pl.pallas_call(
        body, name="attn_bwd", grid=(nb,),
        in_specs=[_row_spec(t, SLAB_W), _tile_spec(SLAB_W, t), _row_spec(t, SLAB_W),
                  _const_spec(qa.shape), _const_spec(dot_.shape), _const_spec(lse.shape),
                  _const_spec(delta.shape), _const_spec(ecol.shape)],
        out_specs=[_row_spec(t, SLAB_W), _tile_spec(FOX_W, t),
                   _const_spec((nb, HEADS * QT_ROWS, t)), _row_spec(t, 128)],
        out_shape=[jax.ShapeDtypeStruct((s_len, SLAB_W), F32),
                   jax.ShapeDtypeStruct((nb, FOX_W, t), F32),
                   jax.ShapeDtypeStruct((nb, HEADS * QT_ROWS, t), F32),
                   jax.ShapeDtypeStruct((s_len, 128), F32)],
        scratch_shapes=[pltpu.VMEM((HEADS, t, t), BF16), pltpu.VMEM((HEADS, t, t), BF16)],
        compiler_params=_params(60, 1),
    )(ka, kat, vs, qa, dot_, lse, delta, ecol)


def _rev_cumsum(col_sums, gqt, triu):
    s_len = col_sums.shape[0]
    tm = TOKEN_TILE
    n = s_len // tm

    def body(cs_ref, gqt_ref, tri_ref, o_ref, carry):
        @pl.when(pl.program_id(0) == 0)
        def _():
            carry[...] = jnp.zeros_like(carry)
        rows = [gqt_ref[0, hd * QT_ROWS + HEAD_DIM:hd * QT_ROWS + HEAD_DIM + 1, :]
                for hd in range(HEADS)]
        row_sums = jnp.concatenate(rows + [jnp.zeros((128 - HEADS, tm), F32)], axis=0).T
        out = _tri_dot(tri_ref[...], row_sums - cs_ref[...]) + carry[...]
        o_ref[...] = out
        carry[...] = out[0:1, :]

    return pl.pallas_call(
        body, name="rev_cumsum", grid=(n,),
        in_specs=[pl.BlockSpec((tm, 128), lambda i: (n - 1 - i, 0)),
                  pl.BlockSpec((1, HEADS * QT_ROWS, tm), lambda i: (n - 1 - i, 0, 0)),
                  _const_spec((tm, tm))],
        out_specs=pl.BlockSpec((tm, 128), lambda i: (n - 1 - i, 0)),
        out_shape=jax.ShapeDtypeStruct((s_len, 128), F32),
        scratch_shapes=[pltpu.VMEM((1, 128), F32)],
        compiler_params=_params(32, 1),
    )(col_sums, gqt, triu)


def _heads_from_slabs(slabs):
    lane = lax.broadcasted_iota(jnp.int32, slabs[0].shape, 1)
    low = lane < HEAD_DIM
    pairs = [jnp.where(low, slabs[2 * p], pltpu.roll(slabs[2 * p + 1], HEAD_DIM, 1))
             for p in range(HEADS // 2)]
    return jnp.concatenate(pairs, axis=1)


def _proj_bwd(gqt, gk, dvt, dlogf, flog, qraw, kraw, duv, dgp, x, dx1, wcat, bdiag, gq, gk_gain, g1,
              efold):
    s_len = x.shape[0]
    tm = TOKEN_TILE

    def body(gqt_ref, gkk_ref, dvt_ref, dlf_ref, flog_ref, qr_ref, kr_ref, duv_ref, dgp_ref, x_ref,
             dx1_ref, w_ref, bd_ref, gq_ref, gk_ref, g1_ref, ef_ref,
             dx_ref, dprojt_ref, dgq_ref, dgk_ref, dbf_ref, dg1_ref, gq_acc, gk_acc, dproj_ref):
        step = pl.program_id(0)

        @pl.when(step == 0)
        def _():
            gq_acc[...] = jnp.zeros_like(gq_acc)
            gk_acc[...] = jnp.zeros_like(gk_acc)
            dbf_ref[...] = jnp.zeros_like(dbf_ref)
            dg1_ref[...] = jnp.zeros_like(dg1_ref)

        pad = jnp.zeros((128 - QT_ROWS, tm), F32)
        q_slabs = [jnp.concatenate([gqt_ref[0, hd * QT_ROWS:(hd + 1) * QT_ROWS, :], pad], axis=0).T
                   for hd in range(HEADS)]
        dqn = _heads_from_slabs(q_slabs)
        dkn = _heads_from_slabs([gkk_ref[:, hd * 128:(hd + 1) * 128] for hd in range(HEADS)])

        def head_bwd(raw_ref, dn, g_ref, acc):
            raw = raw_ref[...].astype(F32)
            r = lax.rsqrt(_seg_mean(raw * raw, bd_ref) + EPS)
            xhat = raw * r
            acc[0:1, :] += jnp.sum(dn * xhat, axis=0, keepdims=True)
            dyg = dn * g_ref[...]
            return r * (dyg - xhat * _seg_mean(dyg * xhat, bd_ref))

        dproj_ref[:, C_Q:C_K] = head_bwd(qr_ref, dqn * HEAD_DIM ** -0.5, gq_ref, gq_acc).astype(BF16)
        dproj_ref[:, C_K:C_V] = head_bwd(kr_ref, dkn * LN2, gk_ref, gk_acc).astype(BF16)
        dproj_ref[:, C_V:C_F] = dvt_ref[0].T.astype(BF16)
        dfl = dlf_ref[...] * _sigmoid(-flog_ref[...])
        dbf_ref[...] += jnp.sum(dfl, axis=0, keepdims=True)
        dproj_ref[:, C_F:C_UV] = dfl.astype(BF16)
        dproj_ref[:, C_UV:C_G] = duv_ref[...]
        dproj_ref[:, C_G:C_END] = dgp_ref[...]

        dproj = dproj_ref[...]
        dprojt_ref[...] = dproj.astype(F32).T.astype(BF16)
        dh = jnp.dot(dproj, w_ref[...], preferred_element_type=F32)
        xf = x_ref[...]
        r = lax.rsqrt(jnp.mean(xf * xf, axis=-1, keepdims=True) + EPS)
        dg1_ref[...] += jnp.sum(dh * xf * r, axis=0, keepdims=True)
        dx_ref[...] = dx1_ref[...] + _rms_bwd(xf, r, g1_ref[...], dh)

        @pl.when(step == pl.num_programs(0) - 1)
        def _():
            dgq_ref[...] = _split3_dot(gq_acc[...], ef_ref[...])
            dgk_ref[...] = _split3_dot(gk_acc[...], ef_ref[...])

    outs = [((s_len, D_MODEL), F32, _row_spec(tm, D_MODEL)),
            ((C_END, s_len), BF16, pl.BlockSpec((C_END, tm), lambda i: (0, i))),
            ((8, 128), F32, _const_spec((8, 128))),
            ((8, 128), F32, _const_spec((8, 128))),
            ((1, 128), F32, _const_spec((1, 128))),
            ((1, D_MODEL), F32, _const_spec((1, D_MODEL)))]
    return pl.pallas_call(
        body, name="proj_bwd", grid=(s_len // tm,),
        in_specs=[_tile_spec(HEADS * QT_ROWS, tm), _row_spec(tm, SLAB_W), _tile_spec(FOX_W, tm),
                  _row_spec(tm, 128), _row_spec(tm, 128), _row_spec(tm, FOX_W),
                  _row_spec(tm, FOX_W), _row_spec(tm, 2 * SGU_W), _row_spec(tm, 2 * D_MODEL),
                  _row_spec(tm, D_MODEL), _row_spec(tm, D_MODEL), _const_spec(wcat.shape),
                  _const_spec(bdiag.shape), _const_spec((1, FOX_W)), _const_spec((1, FOX_W)),
                  _const_spec((1, D_MODEL)), _const_spec(efold.shape)],
        out_specs=[o[2] for o in outs],
        out_shape=[jax.ShapeDtypeStruct(o[0], o[1]) for o in outs],
        scratch_shapes=[pltpu.VMEM((8, FOX_W), F32), pltpu.VMEM((8, FOX_W), F32),
                        pltpu.VMEM((tm, C_END), BF16)],
        compiler_params=_params(56, 1),
    )(gqt, gk, dvt, dlogf, flog, qraw, kraw, duv, dgp, x, dx1, wcat, bdiag, gq, gk_gain, g1, efold)


def _dw_matmul(at, b, tm, name):
    m, s_len = at.shape
    n = b.shape[1]

    def body(a_ref, b_ref, o_ref):
        o_ref[...] = jnp.dot(a_ref[...], b_ref[...], preferred_element_type=F32).astype(BF16)

    return pl.pallas_call(
        body, name=name, grid=(m // tm,),
        in_specs=[pl.BlockSpec((tm, s_len), lambda i: (i, 0)), _const_spec(b.shape)],
        out_specs=pl.BlockSpec((tm, n), lambda i: (i, 0)),
        out_shape=jax.ShapeDtypeStruct((m, n), BF16),
        compiler_params=_params(48, 1),
    )(at, b)


def _adamw(parts, w, m, v, tr, name, col_tile=None):
    n, rows, cols = parts.shape
    bc1 = 1.0 - ADAM_B1 ** ADAM_STEP
    bc2 = 1.0 - ADAM_B2 ** ADAM_STEP

    def body(p_ref, w_ref, m_ref, v_ref, g_ref, d_ref, mo_ref, vo_ref):
        g = p_ref[0].astype(F32)
        for idx in range(1, n):
            g = g + p_ref[idx].astype(F32)
        g_ref[...] = g
        mn = ADAM_B1 * m_ref[...] + (1.0 - ADAM_B1) * g
        vn = ADAM_B2 * v_ref[...] + (1.0 - ADAM_B2) * (g * g)
        mo_ref[...] = mn
        vo_ref[...] = vn
        m_hat = mn / bc1
        v_hat = vn / bc2
        d_ref[...] = -ADAM_LR * (m_hat / (jnp.sqrt(v_hat) + ADAM_EPS) + ADAM_WD * w_ref[...])

    if col_tile is None:
        spec = pl.BlockSpec((tr, cols), lambda i: (i, 0))
        pspec = pl.BlockSpec((n, tr, cols), lambda i: (0, i, 0))
        steps = rows // tr
    else:
        spec = pl.BlockSpec((rows, col_tile), lambda i: (0, i))
        pspec = pl.BlockSpec((n, rows, col_tile), lambda i: (0, 0, i))
        steps = cols // col_tile
    return pl.pallas_call(
        body, name=name, grid=(steps,),
        in_specs=[pspec, spec, spec, spec],
        out_specs=[spec] * 4,
        out_shape=[jax.ShapeDtypeStruct((rows, cols), F32)] * 4,
        compiler_params=_params(48, 1),
    )(parts, w, m, v)


def _sum_parts(parts, name):
    n, rows, cols = parts.shape

    def body(p_ref, o_ref):
        g = p_ref[0]
        for idx in range(1, n):
            g = g + p_ref[idx]
        o_ref[...] = g

    return pl.pallas_call(
        body, name=name, out_shape=jax.ShapeDtypeStruct((rows, cols), F32),
        in_specs=[_const_spec(parts.shape)], out_specs=_const_spec((rows, cols)), grid=(1,),
        compiler_params=_params(16, 1),
    )(parts)


SMALL_NAMES = ("g_pre_mix", "b_forget", "g_q", "g_k", "g_sgu", "b_sgu", "w_spatial", "b_spatial",
               "g_post_mix", "g_pre_ffn", "g_post_ffn")


def _small_rows(size):
    return -(-size // 1024)


def _pack_small(d):
    rows = []
    for k in SMALL_NAMES:
        flat = d[k].reshape(-1).astype(F32)
        nr = _small_rows(flat.shape[0])
        rows.append(jnp.pad(flat, (0, nr * 1024 - flat.shape[0])).reshape(nr, 1024))
    used = sum(r.shape[0] for r in rows)
    rows.append(jnp.zeros((N_DEV * SMALL_ROWS - used, 1024), F32))
    return jnp.concatenate(rows, axis=0)


def _unpack_small(packed, shapes):
    out, off = {}, 0
    for k in SMALL_NAMES:
        size = math.prod(shapes[k])
        nr = _small_rows(size)
        out[k] = packed[off:off + nr].reshape(-1)[:size].reshape(shapes[k])
        off += nr
    return out


def _cols_to_blocks(full, width):
    r = full.shape[0]
    return jnp.transpose(full.reshape(r, N_DEV, width), (1, 0, 2))


def _blocks_to_cols(blocks):
    n, r, width = blocks.shape
    return jnp.transpose(blocks, (1, 0, 2)).reshape(r, n * width)


def kernel(x, g_pre_mix, w_in, b_forget, g_q, g_k, g_sgu, b_sgu, w_spatial, b_spatial, w_branch_a, w_branch_b, w_out, g_post_mix, g_pre_ffn, w_ffn_in, w_ffn_down, g_post_ffn, loss_target, m_g_pre_mix, m_w_in, m_b_forget, m_g_q, m_g_k, m_g_sgu, m_b_sgu, m_w_spatial, m_b_spatial, m_w_branch_a, m_w_branch_b, m_w_out, m_g_post_mix, m_g_pre_ffn, m_w_ffn_in, m_w_ffn_down, m_g_post_ffn, v_g_pre_mix, v_w_in, v_b_forget, v_g_q, v_g_k, v_g_sgu, v_b_sgu, v_w_spatial, v_b_spatial, v_w_branch_a, v_w_branch_b, v_w_out, v_g_post_mix, v_g_pre_ffn, v_w_ffn_in, v_w_ffn_down, v_g_post_ffn):
    big_names = ("w_in", "w_branch_a", "w_branch_b", "w_out", "w_ffn_in", "w_ffn_down")
    weights = dict(g_pre_mix=g_pre_mix, w_in=w_in, b_forget=b_forget, g_q=g_q, g_k=g_k, g_sgu=g_sgu,
                   b_sgu=b_sgu, w_spatial=w_spatial, b_spatial=b_spatial, w_branch_a=w_branch_a,
                   w_branch_b=w_branch_b, w_out=w_out, g_post_mix=g_post_mix, g_pre_ffn=g_pre_ffn,
                   w_ffn_in=w_ffn_in, w_ffn_down=w_ffn_down, g_post_ffn=g_post_ffn)
    mom1 = dict(g_pre_mix=m_g_pre_mix, w_in=m_w_in, b_forget=m_b_forget, g_q=m_g_q, g_k=m_g_k,
                g_sgu=m_g_sgu, b_sgu=m_b_sgu, w_spatial=m_w_spatial, b_spatial=m_b_spatial,
                w_branch_a=m_w_branch_a, w_branch_b=m_w_branch_b, w_out=m_w_out,
                g_post_mix=m_g_post_mix, g_pre_ffn=m_g_pre_ffn, w_ffn_in=m_w_ffn_in,
                w_ffn_down=m_w_ffn_down, g_post_ffn=m_g_post_ffn)
    mom2 = dict(g_pre_mix=v_g_pre_mix, w_in=v_w_in, b_forget=v_b_forget, g_q=v_g_q, g_k=v_g_k,
                g_sgu=v_g_sgu, b_sgu=v_b_sgu, w_spatial=v_w_spatial, b_spatial=v_b_spatial,
                w_branch_a=v_w_branch_a, w_branch_b=v_w_branch_b, w_out=v_w_out,
                g_post_mix=v_g_post_mix, g_pre_ffn=v_g_pre_ffn, w_ffn_in=v_w_ffn_in,
                w_ffn_down=v_w_ffn_down, g_post_ffn=v_g_post_ffn)
    names = list(weights)
    shapes = {k: weights[k].shape for k in names}

    s_len = x.shape[1]
    xs = x.reshape(s_len, D_MODEL)
    tgt = loss_target.reshape(s_len, D_MODEL)

    transposed = ("w_in", "w_ffn_in")

    def local_view(a, k):
        return jnp.transpose(a[0]) if k in transposed else a[0]

    shards = {k: local_view(weights[k], k).astype(BF16) for k in big_names}
    win_t = _gather_two_level(shards["w_in"], "gather_w_in").reshape(IN_COLS, D_MODEL)
    win_t, later = lax.optimization_barrier(
        (win_t, [shards[k] for k in big_names if k != "w_in"]))
    shards.update(zip([k for k in big_names if k != "w_in"], later))
    (gat_mix, gat_ffn), gat_token = _exchange_start(
        [[shards["w_branch_a"], shards["w_branch_b"], shards["w_out"]],
         [shards["w_ffn_in"], shards["w_ffn_down"]]], "gather_start", gather=True)
    f_off = 3 * FOX_W
    u_off = f_off + HEADS
    wcat = jnp.concatenate([
        win_t[:f_off], jnp.pad(win_t[f_off:u_off], ((0, 128 - HEADS), (0, 0))), win_t[u_off:]],
        axis=0)

    seg = jnp.arange(FOX_W) // HEAD_DIM
    bdiag = (seg[:128, None] == seg[None, :128]).astype(BF16)
    tm = TOKEN_TILE
    tril = (jnp.arange(tm)[None, :] <= jnp.arange(tm)[:, None]).astype(BF16)
    triu = tril.T
    egrp = (seg[:, None] == jnp.arange(128)[None, :]).astype(BF16)
    efold = ((jnp.arange(FOX_W) % HEAD_DIM)[:, None] == jnp.arange(128)[None, :]).astype(BF16)
    gq512 = jnp.tile(g_q.reshape(1, HEAD_DIM), (1, HEADS))
    gk512 = jnp.tile(g_k.reshape(1, HEAD_DIM), (1, HEADS))
    bfor = jnp.pad(b_forget.reshape(1, HEADS), ((0, 0), (0, 128 - HEADS)))
    pos = jnp.arange(WINDOW)
    wmask = ((pos[None, :] // CHUNK) <= (pos[:, None] // CHUNK))
    wsm_f = jnp.where(wmask[None], w_spatial[0], 0.0)
    wsm = wsm_f.astype(BF16)
    wsmt = jnp.transpose(wsm_f, (0, 2, 1)).astype(BF16)
    bsf = jnp.repeat(jnp.transpose(b_spatial[0]), HEAD_DIM, axis=1)
    wmask_f = wmask.astype(F32)

    col = jnp.arange(SLAB_W)
    row128 = jnp.arange(128)

    def d_place(first):
        return jnp.stack([((col[None, :] // 128 == row128[:, None])
                           & (col[None, :] % 128 == first + a)).astype(BF16) for a in range(3)])

    pdq, pdk = d_place(HEAD_DIM), d_place(HEAD_DIM + 3)
    ones_q = ((col % 128 >= HEAD_DIM + 3) & (col % 128 < HEAD_DIM + 6)).astype(F32)[None]
    ones_k = ((col % 128 >= HEAD_DIM) & (col % 128 < HEAD_DIM + 3)).astype(F32)[None]
    ecol = ((col[:, None] // 128 == row128[None, :])
            & (col[:, None] % 128 == HEAD_DIM + 3)).astype(BF16)

    (h, qa, ka, kat, vs, vt, qraw, kraw, flog, uvpre, gpre) = _proj_fwd(
        xs, g_pre_mix + gat_token[0:1, 0:1], wcat, bdiag, gq512, gk512, bfor, tril, pdq, pdk,
        ones_q, ones_k)
    attn, attn_t, lse = _attn_fwd(qa, ka, vt)
    zone_a, zone_b, zone_out = _exchange_wait(gat_mix, attn, "gather_wait_mix", gather=True)
    wa = _blocks_to_cols(_own_block(zone_a, shards["w_branch_a"]))
    wb = _blocks_to_cols(_own_block(zone_b, shards["w_branch_b"]))
    wout = _own_block(zone_out, shards["w_out"]).reshape(D_MODEL, D_MODEL)
    sgu_t, ya, yb, merged_t, om, x1 = _mix_fwd(attn, uvpre, gpre, xs, wa, wb, wout, wsm, bsf,
                                           g_sgu, b_sgu, g_post_mix)
    zone_ffn, zone_down = _exchange_wait(gat_ffn, x1, "gather_wait_ffn", gather=True)
    wffn = _own_block(zone_ffn, shards["w_ffn_in"]).reshape(2 * D_FF, D_MODEL)
    wdown = _own_block(zone_down, shards["w_ffn_down"]).reshape(D_FF, D_MODEL)
    (dx1, h2, act_t, dff, dgu_t, loss_acc, dg_post_ffn, dg_pre_ffn) = _ffn_fwd_bwd(
        x1, tgt, wffn, wdown, g_pre_ffn, g_post_ffn)

    dw_down = _dw_matmul(act_t, dff, D_FF // 4, "dw_down")
    dw_ffn = _dw_matmul(dgu_t, h2, 2 * D_FF // N_DEV, "dw_ffn_in")
    parts_ffn = [dw_ffn.reshape(N_DEV, 2 * D_FF // N_DEV, D_MODEL),
                 dw_down.reshape(N_DEV, D_FF // N_DEV, D_MODEL)]
    (sct_ffn,), sct_ffn_token = _exchange_start([parts_ffn], "scatter_start_ffn", gather=False)

    (dom, dya, dyb, dgp, dot_, delta, duv, dws, dbs, dg_sgu, db_sgu, dg_post_mix) = _mix_bwd(
        dx1, om, ya, yb, gpre, uvpre, attn, wout, wa, wb, wsm, wsmt, bsf, g_sgu, b_sgu,
        g_post_mix + sct_ffn_token[0:1, 0:1], wmask_f, egrp)
    dw_out = _dw_matmul(merged_t, dom, 512, "dw_out")
    dw_a = _dw_matmul(attn_t, dya, 512, "dw_a")
    dw_b = _dw_matmul(sgu_t, dyb, 512, "dw_b")
    parts_mix = [_cols_to_blocks(dw_a, D_MODEL // N_DEV), _cols_to_blocks(dw_b, D_MODEL // N_DEV),
                 dw_out.reshape(N_DEV, D_MODEL // N_DEV, D_MODEL)]
    (sct_mix,), sct_mix_token = _exchange_start([parts_mix], "scatter_start_mix", gather=False)

    gk_all, dvt, gqt, col_sums = _attn_bwd(qa, ka, kat, vs, dot_, lse,
                                           delta + sct_mix_token[0, 0], ecol)
    dlogf = _rev_cumsum(col_sums, gqt, triu)
    dx, dproj_t, dgq, dgk, dbf, dg_pre_mix = _proj_bwd(
        gqt, gk_all, dvt, dlogf, flog, qraw, kraw, duv, dgp, xs, dx1, wcat, bdiag, gq512, gk512,
        g_pre_mix, efold)
    dw_cat = _dw_matmul(dproj_t, h, C_END // N_DEV, "dw_in")
    dw_in = jnp.concatenate([dw_cat[:C_F + HEADS], dw_cat[C_UV:]], axis=0)

    small_local = dict(
        g_pre_mix=dg_pre_mix, b_forget=dbf[:, :HEADS], g_q=dgq[0:1, :HEAD_DIM],
        g_k=dgk[0:1, :HEAD_DIM], g_sgu=dg_sgu, b_sgu=db_sgu, w_spatial=dws,
        b_spatial=jnp.transpose(dbs[:, :GROUPS]), g_post_mix=dg_post_mix, g_pre_ffn=dg_pre_ffn,
        g_post_ffn=dg_post_ffn)
    small_parts = _pack_small(small_local).reshape(N_DEV, SMALL_ROWS, 1024)

    grad_in_t = _reduce_scatter_two_level(dw_in.reshape(N_DEV, IN_COLS // N_DEV, D_MODEL),
                                          "reduce_scatter_in")
    (recv_small,) = _exchange([small_parts], "scatter_small", gather=False)
    x_pos, y_pos, c_pos = _mesh_pos()
    me = 4 * x_pos + 2 * y_pos + c_pos

    def with_own(zones, parts):
        return [_own_block(z, lax.dynamic_index_in_dim(p, me, 0, keepdims=False))
                for z, p in zip(zones, parts)]

    recv_ffn, recv_down = with_own(
        _exchange_wait(sct_ffn, recv_small, "scatter_wait_ffn", gather=False), parts_ffn)
    recv_a, recv_b, recv_out = with_own(
        _exchange_wait(sct_mix, recv_ffn, "scatter_wait_mix", gather=False), parts_mix)
    received = [grad_in_t[None], recv_a, recv_b, recv_out, recv_ffn, recv_down]

    grads, deltas, new_m, new_v = {}, {}, {}, {}
    row_tiles = {"w_in": None, "w_branch_a": 512, "w_branch_b": 512, "w_out": 128, "w_ffn_in": 176,
                 "w_ffn_down": 352}
    for idx, k in enumerate(big_names):
        outs = _adamw(received[idx], local_view(weights[k], k), local_view(mom1[k], k),
                      local_view(mom2[k], k), row_tiles[k], "adamw_" + k,
                      col_tile=256 if k == "w_in" else None)
        if k in transposed:
            outs = [jnp.transpose(o) for o in outs]
        grads[k], deltas[k], new_m[k], new_v[k] = [o[None] for o in outs]

    small_sum = _sum_parts(recv_small, "sum_small")
    (small_all,) = _exchange([small_sum], "gather_small", gather=True)
    small_all = small_all.reshape(1, N_DEV * SMALL_ROWS, 1024)
    sg, sd, sm, sv = _adamw(small_all, _pack_small(weights), _pack_small(mom1), _pack_small(mom2),
                            N_DEV * SMALL_ROWS, "adamw_small")
    for dst, packed in ((grads, sg), (deltas, sd), (new_m, sm), (new_v, sv)):
        dst.update(_unpack_small(packed, shapes))

    loss = lax.psum(loss_acc[0, 0], ("x", "y", "c"))
    return (loss, dx.reshape(x.shape), *[grads[k] for k in names], *[deltas[k] for k in names],
            *[new_m[k] for k in names], *[new_v[k] for k in names])
```

```python
import functools
import math

import jax
import jax.numpy as jnp
import numpy as np
from jax import lax
from jax.experimental import pallas as pl
from jax.experimental.pallas import tpu as pltpu

F32 = jnp.float32
BF16 = jnp.bfloat16

D_MODEL = 1024
FOX_W = 512
HEADS = 8
HEAD_DIM = 64
SGU_W = 512
GROUPS = 8
WINDOW = 128
CHUNK = 64
D_FF = 2816
IN_COLS = 4616
EPS = 1e-6
N_DEV = 8
LOG2E = 1.4426950408889634
LN2 = 0.6931471805599453

C_Q, C_K, C_V, C_F, C_UV, C_G, C_END = 0, 512, 1024, 1536, 1664, 2688, 4736

ADAM_LR, ADAM_B1, ADAM_B2, ADAM_EPS, ADAM_WD, ADAM_STEP = 0.001, 0.9, 0.999, 1e-08, 0.01, 10

MIB = 1024 * 1024
TOKEN_TILE = 256
ATTN_TILE = 256
SLAB_W = HEADS * 128
QT_ROWS = 72

SMALL_ROWS = 18
BLK = IN_COLS // N_DEV


def _params(vmem_mib, n_axes):
    return pltpu.CompilerParams(
        dimension_semantics=("arbitrary",) * n_axes, vmem_limit_bytes=vmem_mib * MIB)


def _const_spec(shape):
    nd = len(shape)
    return pl.BlockSpec(shape, lambda *_: (0,) * nd)


def _row_spec(tm, cols):
    return pl.BlockSpec((tm, cols), lambda i: (i, 0))


def _tile_spec(rows, tm):
    return pl.BlockSpec((1, rows, tm), lambda i: (i, 0, 0))


def _split3_dot(x, e):
    x1 = x.astype(BF16)
    r1 = x - x1.astype(F32)
    x2 = r1.astype(BF16)
    x3 = (r1 - x2.astype(F32)).astype(BF16)
    dot = functools.partial(jnp.dot, preferred_element_type=F32)
    return dot(x1, e) + dot(x2, e) + dot(x3, e)


def _tri_dot(tri, x):
    x1 = x.astype(BF16)
    r1 = x - x1.astype(F32)
    x2 = r1.astype(BF16)
    x3 = (r1 - x2.astype(F32)).astype(BF16)
    dot = functools.partial(jnp.dot, preferred_element_type=F32)
    return dot(tri, x1) + dot(tri, x2) + dot(tri, x3)


def _seg_mean(sq, bd_ref):
    hi = sq.astype(BF16)
    lo = (sq - hi.astype(F32)).astype(BF16)
    bd = bd_ref[...]
    dot = functools.partial(jnp.dot, preferred_element_type=F32)
    pairs = [dot(hi[:, p * 128:(p + 1) * 128], bd) + dot(lo[:, p * 128:(p + 1) * 128], bd)
             for p in range(HEADS // 2)]
    return jnp.concatenate(pairs, axis=1) * (1.0 / HEAD_DIM)


def _slabs_from_heads(t):
    lane = lax.broadcasted_iota(jnp.int32, (t.shape[0], 128), 1)
    low = lane < HEAD_DIM
    slabs = []
    for p in range(HEADS // 2):
        pair = t[:, p * 128:(p + 1) * 128]
        slabs.append(jnp.where(low, pair, 0.0))
        slabs.append(jnp.where(low, pltpu.roll(pair, HEAD_DIM, 1), 0.0))
    return jnp.concatenate(slabs, axis=1)


def _dot_nt(a, b):
    return lax.dot_general(a, b, (((1,), (1,)), ((), ())), preferred_element_type=F32)


def _dot_tn(a, b):
    return lax.dot_general(a, b, (((0,), (0,)), ((), ())), preferred_element_type=F32)


def _sigmoid(x):
    return 1.0 / (1.0 + jnp.exp(-x))


_GELU_C = math.sqrt(2.0 / math.pi)


def _gelu_and_grad(x):
    inner = _GELU_C * (x + 0.044715 * x * x * x)
    t = jnp.tanh(inner)
    y = 0.5 * x * (1.0 + t)
    dy = 0.5 * (1.0 + t) + 0.5 * x * (1.0 - t * t) * _GELU_C * (1.0 + 3.0 * 0.044715 * x * x)
    return y, dy


def _rms_bwd(xin, r, g, dy):
    dyg = dy * g
    return r * dyg - xin * (r * r * r) * jnp.mean(dyg * xin, axis=-1, keepdims=True)


def _mesh_pos():
    x, y, c = lax.axis_index("x"), lax.axis_index("y"), lax.axis_index("c")
    return x, y, c


def _peer(k):
    x, y, c = _mesh_pos()
    px = (1 - x) if (k >> 2) & 1 else x
    py = (1 - y) if (k >> 1) & 1 else y
    pc = (1 - c) if k & 1 else c
    return (px, py, pc), 4 * px + 2 * py + pc


def _exchange(arrs, name, gather):
    n = len(arrs)
    if gather:
        out_shape = [jax.ShapeDtypeStruct((N_DEV,) + a.shape, a.dtype) for a in arrs]
    else:
        out_shape = [jax.ShapeDtypeStruct(a.shape, a.dtype) for a in arrs]

    def body(*refs):
        ins, outs = refs[:n], refs[n:2 * n]
        send_sems, recv_sems, local_sems = refs[2 * n:]
        x, y, c = _mesh_pos()
        me = 4 * x + 2 * y + c

        def src(a, idx):
            return ins[a] if gather else ins[a].at[idx]

        local = []
        for a in range(n):
            cp = pltpu.make_async_copy(src(a, me), outs[a].at[me], local_sems.at[a])
            cp.start()
            local.append(cp)
        sends = []
        for k in range(1, N_DEV):
            peer, pidx = _peer(k)
            for a in range(n):
                cp = pltpu.make_async_remote_copy(
                    src_ref=src(a, pidx), dst_ref=outs[a].at[me],
                    send_sem=send_sems.at[a, k - 1], recv_sem=recv_sems.at[a, k - 1],
                    device_id=peer, device_id_type=pl.DeviceIdType.MESH)
                cp.start()
                sends.append(cp)
        for k in range(1, N_DEV):
            peer, pidx = _peer(k)
            for a in range(n):
                pltpu.make_async_remote_copy(
                    src_ref=src(a, pidx), dst_ref=outs[a].at[pidx],
                    send_sem=send_sems.at[a, k - 1], recv_sem=recv_sems.at[a, k - 1],
                    device_id=peer, device_id_type=pl.DeviceIdType.MESH).wait_recv()
        for cp in sends:
            cp.wait_send()
        for cp in local:
            cp.wait()

    any_spec = pl.BlockSpec(memory_space=pl.ANY)
    return pl.pallas_call(
        body, name=name, out_shape=out_shape,
        in_specs=[any_spec] * n, out_specs=[any_spec] * n,
        scratch_shapes=[pltpu.SemaphoreType.DMA((n, N_DEV - 1)),
                        pltpu.SemaphoreType.DMA((n, N_DEV - 1)),
                        pltpu.SemaphoreType.DMA((n,))],
    )(*arrs)


def _gather_two_level(shard, name):
    def body(x_ref, out_ref, send_sems, recv_sems, local_sem):
        x, y, c = _mesh_pos()
        me, sibling = (x, y, c), (x, y, 1 - c)
        chips = [(1 - x, y), (x, 1 - y), (1 - x, 1 - y)]

        def slot(px, py, pc):
            return out_ref.at[4 * px + 2 * py + pc]

        def copy(k, block, to, src=None):
            return pltpu.make_async_remote_copy(
                src_ref=slot(*block) if src is None else src, dst_ref=slot(*block),
                send_sem=send_sems.at[k], recv_sem=recv_sems.at[k],
                device_id=to, device_id_type=pl.DeviceIdType.MESH)

        mine = pltpu.make_async_copy(x_ref, slot(*me), local_sem)
        mine.start()
        first = [copy(1 + j, me, (*chip, c), src=x_ref) for j, chip in enumerate(chips)]
        first.append(copy(0, me, sibling, src=x_ref))
        for cp in first:
            cp.start()
        passed = [copy(4 + j, (*chip, c), sibling) for j, chip in enumerate(chips)]
        for j, chip in enumerate(chips):
            copy(1 + j, (*chip, c), me).wait_recv()
            passed[j].start()
        copy(0, sibling, me).wait_recv()
        for j, chip in enumerate(chips):
            copy(4 + j, (*chip, 1 - c), me).wait_recv()
        for cp in first + passed:
            cp.wait_send()
        mine.wait()

    any_spec = pl.BlockSpec(memory_space=pl.ANY)
    return pl.pallas_call(
        body, name=name, out_shape=jax.ShapeDtypeStruct((N_DEV,) + shard.shape, shard.dtype),
        in_specs=[any_spec], out_specs=any_spec,
        scratch_shapes=[pltpu.SemaphoreType.DMA((7,)), pltpu.SemaphoreType.DMA((7,)),
                        pltpu.SemaphoreType.DMA],
    )(shard)


def _reduce_scatter_two_level(parts, name):
    _, rows, cols = parts.shape
    n_chips = N_DEV // 2

    def body(p_ref, out_ref, mine_buf, sib_buf, send_buf, recv_buf, send_sems, recv_sems,
             local_sems):
        x, y, c = _mesh_pos()
        my_chip = 2 * x + y
        sibling = (x, y, 1 - c)
        stage1, local = [], []
        for q in range(n_chips):
            cp = pltpu.make_async_remote_copy(
                src_ref=p_ref.at[2 * q + (1 - c)], dst_ref=sib_buf.at[q],
                send_sem=send_sems.at[q], recv_sem=recv_sems.at[q],
                device_id=sibling, device_id_type=pl.DeviceIdType.MESH)
            cp.start()
            stage1.append(cp)
            lc = pltpu.make_async_copy(p_ref.at[2 * q + c], mine_buf.at[q], local_sems.at[q])
            lc.start()
            local.append(lc)
        for lc in local:
            lc.wait()
        for cp in stage1:
            cp.wait_recv()
        stage2 = []
        for k in range(1, n_chips):
            px = (1 - x) if (k >> 1) & 1 else x
            py = (1 - y) if k & 1 else y
            q = 2 * px + py
            pair = mine_buf[q].astype(F32) + sib_buf[q].astype(F32)
            send_buf[k - 1] = pair.astype(BF16)
            cp = pltpu.make_async_remote_copy(
                src_ref=send_buf.at[k - 1], dst_ref=recv_buf.at[k - 1],
                send_sem=send_sems.at[n_chips + k - 1], recv_sem=recv_sems.at[n_chips + k - 1],
                device_id=(px, py, c), device_id_type=pl.DeviceIdType.MESH)
            cp.start()
            stage2.append(cp)
        total = mine_buf[my_chip].astype(F32) + sib_buf[my_chip].astype(F32)
        for k in range(1, n_chips):
            stage2[k - 1].wait_recv()
            total = total + recv_buf[k - 1].astype(F32)
        out_ref[...] = total
        for cp in stage1 + stage2:
            cp.wait_send()

    return pl.pallas_call(
        body, name=name, out_shape=jax.ShapeDtypeStruct((rows, cols), F32),
        in_specs=[pl.BlockSpec(memory_space=pl.ANY)],
        out_specs=pl.BlockSpec(memory_space=pltpu.VMEM),
        scratch_shapes=[pltpu.VMEM((n_chips, rows, cols), BF16),
                        pltpu.VMEM((n_chips, rows, cols), BF16),
                        pltpu.VMEM((n_chips - 1, rows, cols), BF16),
                        pltpu.VMEM((n_chips - 1, rows, cols), BF16),
                        pltpu.SemaphoreType.DMA((2 * n_chips - 1,)),
                        pltpu.SemaphoreType.DMA((2 * n_chips - 1,)),
                        pltpu.SemaphoreType.DMA((n_chips,))],
        compiler_params=pltpu.CompilerParams(vmem_limit_bytes=40 * MIB),
    )(parts)


def _remote_copy(gather, src_ref, land_ref, send_sem, recv_sem, k, receive_side):
    x, y, c = _mesh_pos()
    me = 4 * x + 2 * y + c
    peer, pidx = _peer(k)
    return pltpu.make_async_remote_copy(
        src_ref=src_ref if gather else src_ref.at[pidx],
        dst_ref=land_ref.at[pidx if receive_side else me],
        send_sem=send_sem, recv_sem=recv_sem,
        device_id=peer, device_id_type=pl.DeviceIdType.MESH)


def _exchange_start(groups, name, gather):
    arrs = [a for g in groups for a in g]
    n, n_groups = len(arrs), len(groups)
    lands = [jax.ShapeDtypeStruct(((N_DEV,) + a.shape) if gather else a.shape, a.dtype)
             for a in arrs]

    def body(*refs):
        srcs, zones = refs[:n], refs[n:2 * n]
        sems = refs[2 * n:2 * n + 2 * n_groups]
        token = refs[-1]
        a = 0
        for gi, g in enumerate(groups):
            send_sems, recv_sems = sems[2 * gi], sems[2 * gi + 1]
            for k in range(1, N_DEV):
                for ai in range(len(g)):
                    slot = ai * (N_DEV - 1) + k - 1
                    _remote_copy(gather, srcs[a + ai], zones[a + ai], send_sems.at[slot],
                                 recv_sems.at[slot], k, False).start()
            a += len(g)
        token[...] = jnp.zeros_like(token)

    hbm = pl.BlockSpec(memory_space=pltpu.HBM)
    sem = pl.BlockSpec(memory_space=pltpu.SEMAPHORE)
    sem_shapes = []
    for g in groups:
        sem_shapes += [pltpu.SemaphoreType.DMA((len(g) * (N_DEV - 1),))] * 2
    outs = pl.pallas_call(
        body, name=name,
        in_specs=[hbm] * (2 * n),
        out_shape=sem_shapes + [pltpu.HBM(a.shape, a.dtype) for a in arrs]
        + [pltpu.HBM(z.shape, z.dtype) for z in lands] + [jax.ShapeDtypeStruct((8, 128), F32)],
        out_specs=[sem] * (2 * n_groups) + [hbm] * (2 * n)
        + [pl.BlockSpec(memory_space=pltpu.VMEM)],
        input_output_aliases={i: 2 * n_groups + i for i in range(2 * n)},
        compiler_params=pltpu.CompilerParams(
            has_side_effects=pltpu.SideEffectType.DATAFLOW_SIDE_EFFECTING),
    )(*[pltpu.with_memory_space_constraint(a, pltpu.HBM) for a in arrs],
      *[pltpu.with_memory_space_constraint(lax.empty(z.shape, z.dtype), pltpu.HBM) for z in lands])
    sems = outs[:2 * n_groups]
    thru = outs[2 * n_groups:2 * n_groups + n]
    zones = outs[2 * n_groups + n:2 * n_groups + 2 * n]
    handles, a = [], 0
    for gi, g in enumerate(groups):
        handles.append((sems[2 * gi], sems[2 * gi + 1], thru[a:a + len(g)], zones[a:a + len(g)]))
        a += len(g)
    return handles, outs[-1]


def _exchange_wait(handle, after, name, gather):
    send_sems, recv_sems, thru, zones = handle
    n = len(thru)

    def body(*refs):
        srcs, lands = refs[:n], refs[n:2 * n]
        ssem, rsem = refs[2 * n], refs[2 * n + 1]
        for k in range(1, N_DEV):
            for ai in range(n):
                slot = ai * (N_DEV - 1) + k - 1
                cp = _remote_copy(gather, srcs[ai], lands[ai], ssem.at[slot], rsem.at[slot], k, True)
                cp.wait_send()
                cp.wait_recv()

    hbm = pl.BlockSpec(memory_space=pltpu.HBM)
    sem = pl.BlockSpec(memory_space=pltpu.SEMAPHORE)
    outs = pl.pallas_call(
        body, name=name,
        in_specs=[hbm] * (2 * n) + [sem, sem, pl.BlockSpec(memory_space=pl.ANY)],
        out_shape=[pltpu.HBM(a.shape, a.dtype) for a in thru]
        + [pltpu.HBM(z.shape, z.dtype) for z in zones],
        out_specs=[hbm] * (2 * n),
        input_output_aliases={i: i for i in range(2 * n)},
        compiler_params=pltpu.CompilerParams(
            has_side_effects=pltpu.SideEffectType.DATAFLOW_SIDE_EFFECTING),
    )(*thru, *zones, send_sems, recv_sems, after)
    return outs[:n], outs[n:]


def _own_block(zone, block):
    x, y, c = _mesh_pos()
    me = 4 * x + 2 * y + c
    return lax.dynamic_update_slice_in_dim(zone, block[None], me, axis=0)


def _proj_fwd(x, g1, wcat, bdiag, gq, gk, bfor, tri, pdq, pdk, ones_q, ones_k):
    s_len = x.shape[0]
    tm = TOKEN_TILE
    nt = s_len // tm

    def body(x_ref, g1_ref, w_ref, bd_ref, gq_ref, gk_ref, bf_ref, tri_ref, pdq_ref,
             pdk_ref, oq_ref, ok_ref,
             h_ref, qa_ref, ka_ref, kat_ref, vs_ref, vt_ref, qr_ref, kr_ref, flog_ref, uv_ref,
             gp_ref, carry):
        @pl.when(pl.program_id(0) == 0)
        def _():
            carry[...] = jnp.zeros_like(carry)

        xf = x_ref[...]
        r = lax.rsqrt(jnp.mean(xf * xf, axis=-1, keepdims=True) + EPS)
        h = (xf * r * g1_ref[...]).astype(BF16)
        h_ref[...] = h
        dot = functools.partial(jnp.dot, preferred_element_type=F32)

        def proj(lo, hi):
            return _dot_nt(h, w_ref[lo:hi, :])

        flog = proj(C_F, C_UV) + bf_ref[...]
        flog_ref[...] = flog
        lane = lax.broadcasted_iota(jnp.int32, flog.shape, 1)
        logf = jnp.minimum(flog, 0.0) - jnp.log(1.0 + jnp.exp(-jnp.abs(flog)))
        logf = jnp.where(lane < HEADS, logf, 0.0)
        dcum = _tri_dot(tri_ref[...], logf) + carry[...]
        carry[...] = dcum[tm - 1:tm, :]
        d2 = dcum * LOG2E
        d2a = d2.astype(BF16)
        rem = d2 - d2a.astype(F32)
        d2b = rem.astype(BF16)
        d2c = (rem - d2b.astype(F32)).astype(BF16)

        q = proj(C_Q, C_K)
        qr_ref[...] = q.astype(BF16)
        rq = lax.rsqrt(_seg_mean(q * q, bd_ref) + EPS)
        qn = q * rq * (gq_ref[...] * (HEAD_DIM ** -0.5 * LOG2E))
        qa = (_slabs_from_heads(qn) + dot(d2a, pdq_ref[0]) + dot(d2b, pdq_ref[1])
              + dot(d2c, pdq_ref[2]) + oq_ref[...])
        qa_ref[...] = qa.astype(BF16)

        k = proj(C_K, C_V)
        kr_ref[...] = k.astype(BF16)
        rk = lax.rsqrt(_seg_mean(k * k, bd_ref) + EPS)
        kn = k * rk * gk_ref[...]
        ka = (_slabs_from_heads(kn) - dot(d2a, pdk_ref[0]) - dot(d2b, pdk_ref[1])
              - dot(d2c, pdk_ref[2]) + ok_ref[...])
        ka_ref[...] = ka.astype(BF16)
        kat_ref[0] = ka.T.astype(BF16)

        v = proj(C_V, C_F)
        vs_ref[...] = _slabs_from_heads(v).astype(BF16)
        vt_ref[0] = v.T.astype(BF16)
        uv_ref[...] = proj(C_UV, C_G).astype(BF16)
        gp_ref[...] = proj(C_G, C_END).astype(BF16)

    outs = [((s_len, D_MODEL), BF16, _row_spec(tm, D_MODEL)),
            ((s_len, SLAB_W), BF16, _row_spec(tm, SLAB_W)),
            ((s_len, SLAB_W), BF16, _row_spec(tm, SLAB_W)),
            ((nt, SLAB_W, tm), BF16, _tile_spec(SLAB_W, tm)),
            ((s_len, SLAB_W), BF16, _row_spec(tm, SLAB_W)),
            ((nt, FOX_W, tm), BF16, _tile_spec(FOX_W, tm)),
            ((s_len, FOX_W), BF16, _row_spec(tm, FOX_W)),
            ((s_len, FOX_W), BF16, _row_spec(tm, FOX_W)),
            ((s_len, 128), F32, _row_spec(tm, 128)),
            ((s_len, 2 * SGU_W), BF16, _row_spec(tm, 2 * SGU_W)),
            ((s_len, 2 * D_MODEL), BF16, _row_spec(tm, 2 * D_MODEL))]
    return pl.pallas_call(
        body, name="proj_fwd", grid=(nt,),
        in_specs=[_row_spec(tm, D_MODEL), _const_spec((1, D_MODEL)), _const_spec(wcat.shape),
                  _const_spec(bdiag.shape), _const_spec((1, FOX_W)), _const_spec((1, FOX_W)),
                  _const_spec((1, 128)), _const_spec((tm, tm)), _const_spec(pdq.shape), _const_spec(pdk.shape), _const_spec(ones_q.shape),
                  _const_spec(ones_k.shape)],
        out_specs=[o[2] for o in outs],
        out_shape=[jax.ShapeDtypeStruct(o[0], o[1]) for o in outs],
        scratch_shapes=[pltpu.VMEM((1, 128), F32)],
        compiler_params=_params(56, 1),
    )(x, g1, wcat, bdiag, gq, gk, bfor, tri, pdq, pdk, ones_q, ones_k)


def _attn_fwd(qa, ka, vt):
    s_len = qa.shape[0]
    t = ATTN_TILE
    nb = s_len // t

    def body(q_ref, k_ref, vt_ref, o_ref, ot_ref, lse_ref, m_sc, l_sc, acc_sc, s_sc, alpha_sc):
        i = pl.program_id(0)
        m_sc[...] = jnp.full_like(m_sc, -jnp.inf)
        l_sc[...] = jnp.zeros_like(l_sc)
        acc_sc[...] = jnp.zeros_like(acc_sc)

        def tile(j, masked):
            krows = pl.ds(pl.multiple_of(j * t, t), t)
            if masked:
                keep = (lax.broadcasted_iota(jnp.int32, (t, t), 0)
                        <= lax.broadcasted_iota(jnp.int32, (t, t), 1))
            for hd in range(HEADS):
                sl = slice(hd * 128, (hd + 1) * 128)
                st = _dot_nt(k_ref[krows, sl], q_ref[:, sl])
                if masked:
                    st = jnp.where(keep, st, -jnp.inf)
                s_sc[hd] = st
                m_prev = m_sc[hd:hd + 1, :]
                m_new = jnp.maximum(m_prev, jnp.max(st, axis=0, keepdims=True))
                alpha_sc[hd:hd + 1, :] = jnp.exp2(m_prev - m_new)
                m_sc[hd:hd + 1, :] = m_new
            for hd in range(HEADS):
                hr = slice(hd * HEAD_DIM, (hd + 1) * HEAD_DIM)
                alpha = alpha_sc[hd:hd + 1, :]
                pt = jnp.exp2(s_sc[hd] - m_sc[hd:hd + 1, :])
                l_sc[hd:hd + 1, :] = alpha * l_sc[hd:hd + 1, :] + jnp.sum(pt, axis=0, keepdims=True)
                acc_sc[hr, :] = alpha * acc_sc[hr, :] + jnp.dot(
                    vt_ref[j, hr, :], pt.astype(BF16), preferred_element_type=F32)

        def off_diagonal(j, carry):
            tile(j, False)
            return carry

        lax.fori_loop(0, i, off_diagonal, 0)
        tile(i, True)

        for hd in range(HEADS):
            hr = slice(hd * HEAD_DIM, (hd + 1) * HEAD_DIM)
            l = l_sc[hd:hd + 1, :]
            acc_sc[hr, :] = acc_sc[hr, :] / l
            lse_ref[0, hd:hd + 1, :] = m_sc[hd:hd + 1, :] + jnp.log2(l)
        o_ref[...] = acc_sc[...].T.astype(BF16)
        ot_ref[...] = acc_sc[...].astype(BF16)

    return pl.pallas_call(
        body, name="attn_fwd", grid=(nb,),
        in_specs=[_row_spec(t, SLAB_W), _const_spec(ka.shape), _const_spec(vt.shape)],
        out_specs=[_row_spec(t, FOX_W), pl.BlockSpec((FOX_W, t), lambda i: (0, i)),
                   _tile_spec(HEADS, t)],
        out_shape=[jax.ShapeDtypeStruct((s_len, FOX_W), BF16),
                   jax.ShapeDtypeStruct((FOX_W, s_len), BF16),
                   jax.ShapeDtypeStruct((nb, HEADS, t), F32)],
        scratch_shapes=[pltpu.VMEM((HEADS, t), F32), pltpu.VMEM((HEADS, t), F32),
                        pltpu.VMEM((FOX_W, t), F32), pltpu.VMEM((HEADS, t, t), F32),
                        pltpu.VMEM((HEADS, t), F32)],
        compiler_params=_params(48, 1),
    )(qa, ka, vt)


def _sgu_mix(vn, ws_ref):
    tm = vn.shape[0]
    lane = lax.broadcasted_iota(jnp.int32, (WINDOW, 128), 1)
    low = lane < HEAD_DIM
    wins = []
    for w in range(tm // WINDOW):
        slabs = []
        for p in range(GROUPS // 2):
            v2 = vn[w * WINDOW:(w + 1) * WINDOW, p * 128:(p + 1) * 128]
            lo = jnp.where(low, v2, 0.0).astype(BF16)
            hi = jnp.where(low, 0.0, v2).astype(BF16)
            slabs.append(jnp.dot(ws_ref[2 * p], lo, preferred_element_type=F32)
                         + jnp.dot(ws_ref[2 * p + 1], hi, preferred_element_type=F32))
        wins.append(jnp.concatenate(slabs, axis=1))
    return jnp.concatenate(wins, axis=0) if len(wins) > 1 else wins[0]


def _layernorm_fwd(vv, g, b):
    mu = jnp.mean(vv, axis=-1, keepdims=True)
    xc = vv - mu
    r = lax.rsqrt(jnp.mean(xc * xc, axis=-1, keepdims=True) + EPS)
    xh = xc * r
    return xh * g + b, xh, r


def _mix_fwd(attn, uvpre, gpre, x, wa, wb, wout, wsm, bsf, gsgu, bsgu, gpost):
    s_len = x.shape[0]
    tm = TOKEN_TILE

    def body(o_ref, uv_ref, gp_ref, x_ref, wa_ref, wb_ref, wo_ref, ws_ref, bs_ref, gs_ref, bsg_ref,
             gpost_ref, sgut_ref, ya_ref, yb_ref, mgt_ref, om_ref, x1_ref):
        uvp = uv_ref[...].astype(F32)
        uv, _ = _gelu_and_grad(uvp)
        u, vv = uv[:, :SGU_W], uv[:, SGU_W:]
        vn, _, _ = _layernorm_fwd(vv, gs_ref[...], bsg_ref[...])
        bias = bs_ref[...]
        if tm > WINDOW:
            bias = jnp.concatenate([bias] * (tm // WINDOW), axis=0)
        mixed = _sgu_mix(vn, ws_ref) + bias
        sgu_f = u * mixed
        sgu = sgu_f.astype(BF16)
        sgut_ref[...] = sgu_f.T.astype(BF16)
        ya = jnp.dot(o_ref[...], wa_ref[...], preferred_element_type=F32)
        yb = jnp.dot(sgu, wb_ref[...], preferred_element_type=F32)
        ya_ref[...] = ya.astype(BF16)
        yb_ref[...] = yb.astype(BF16)
        gates = _sigmoid(gp_ref[...].astype(F32))
        merged_f = gates[:, :D_MODEL] * ya + gates[:, D_MODEL:] * yb
        merged = merged_f.astype(BF16)
        mgt_ref[...] = merged_f.T.astype(BF16)
        om = jnp.dot(merged, wo_ref[...], preferred_element_type=F32)
        om_ref[...] = om
        r = lax.rsqrt(jnp.mean(om * om, axis=-1, keepdims=True) + EPS)
        x1_ref[...] = x_ref[...] + om * r * gpost_ref[...]

    def t_out(rows):
        return ((rows, s_len), BF16, pl.BlockSpec((rows, tm), lambda i: (0, i)))

    def r_out(cols, dt):
        return ((s_len, cols), dt, _row_spec(tm, cols))

    outs = [t_out(SGU_W), r_out(D_MODEL, BF16), r_out(D_MODEL, BF16), t_out(D_MODEL),
            r_out(D_MODEL, F32), r_out(D_MODEL, F32)]
    return pl.pallas_call(
        body, name="mix_fwd", grid=(s_len // tm,),
        in_specs=[_row_spec(tm, FOX_W), _row_spec(tm, 2 * SGU_W), _row_spec(tm, 2 * D_MODEL),
                  _row_spec(tm, D_MODEL), _const_spec(wa.shape), _const_spec(wb.shape),
                  _const_spec(wout.shape), _const_spec(wsm.shape), _const_spec(bsf.shape),
                  _const_spec((1, SGU_W)), _const_spec((1, SGU_W)), _const_spec((1, D_MODEL))],
        out_specs=[o[2] for o in outs],
        out_shape=[jax.ShapeDtypeStruct(o[0], o[1]) for o in outs],
        compiler_params=_params(48, 1),
    )(attn, uvpre, gpre, x, wa, wb, wout, wsm, bsf, gsgu, bsgu, gpost)


def _ffn_fwd_bwd(x1, tgt, wffn, wdown, gpre, gpost):
    s_len = x1.shape[0]
    tm = TOKEN_TILE

    def body(x1_ref, t_ref, wi_ref, wd_ref, gpre_ref, gpost_ref,
             dx1_ref, h2_ref, actt_ref, dff_ref, dgut_ref, loss_ref, dgpost_ref, dgpre_ref):
        @pl.when(pl.program_id(0) == 0)
        def _():
            loss_ref[...] = jnp.zeros_like(loss_ref)
            dgpost_ref[...] = jnp.zeros_like(dgpost_ref)
            dgpre_ref[...] = jnp.zeros_like(dgpre_ref)

        x1v = x1_ref[...]
        r2 = lax.rsqrt(jnp.mean(x1v * x1v, axis=-1, keepdims=True) + EPS)
        gpre_v = gpre_ref[...]
        h2 = (x1v * r2 * gpre_v).astype(BF16)
        h2_ref[...] = h2
        gg = _dot_nt(h2, wi_ref[:D_FF, :])
        uu = _dot_nt(h2, wi_ref[D_FF:, :])
        sg = _sigmoid(gg)
        silu = gg * sg
        act_f = silu * uu
        act = act_f.astype(BF16)
        actt_ref[...] = act_f.T.astype(BF16)
        ff = jnp.dot(act, wd_ref[...], preferred_element_type=F32)
        r3 = lax.rsqrt(jnp.mean(ff * ff, axis=-1, keepdims=True) + EPS)
        gpost_v = gpost_ref[...]
        y = x1v + ff * r3 * gpost_v
        err = y - t_ref[...]
        loss_ref[...] += jnp.sum(err * err) * (0.5 / D_MODEL)
        dy = err * (1.0 / D_MODEL)
        dgpost_ref[...] += jnp.sum(dy * ff * r3, axis=0, keepdims=True)
        dff = _rms_bwd(ff, r3, gpost_v, dy).astype(BF16)
        dff_ref[...] = dff
        dact = _dot_nt(dff, wd_ref[...])
        dgg_f = dact * uu * (sg * (1.0 + gg * (1.0 - sg)))
        duu_f = dact * silu
        dgg = dgg_f.astype(BF16)
        duu = duu_f.astype(BF16)
        dgut_ref[:D_FF, :] = dgg_f.T.astype(BF16)
        dgut_ref[D_FF:, :] = duu_f.T.astype(BF16)
        dh2 = (jnp.dot(dgg, wi_ref[:D_FF, :], preferred_element_type=F32)
               + jnp.dot(duu, wi_ref[D_FF:, :], preferred_element_type=F32))
        dgpre_ref[...] += jnp.sum(dh2 * x1v * r2, axis=0, keepdims=True)
        dx1_ref[...] = dy + _rms_bwd(x1v, r2, gpre_v, dh2)

    outs = [((s_len, D_MODEL), F32, _row_spec(tm, D_MODEL)),
            ((s_len, D_MODEL), BF16, _row_spec(tm, D_MODEL)),
            ((D_FF, s_len), BF16, pl.BlockSpec((D_FF, tm), lambda i: (0, i))),
            ((s_len, D_MODEL), BF16, _row_spec(tm, D_MODEL)),
            ((2 * D_FF, s_len), BF16, pl.BlockSpec((2 * D_FF, tm), lambda i: (0, i))),
            ((1, 128), F32, _const_spec((1, 128))),
            ((1, D_MODEL), F32, _const_spec((1, D_MODEL))),
            ((1, D_MODEL), F32, _const_spec((1, D_MODEL)))]
    return pl.pallas_call(
        body, name="ffn_fwd_bwd", grid=(s_len // tm,),
        in_specs=[_row_spec(tm, D_MODEL), _row_spec(tm, D_MODEL), _const_spec(wffn.shape),
                  _const_spec(wdown.shape), _const_spec((1, D_MODEL)), _const_spec((1, D_MODEL))],
        out_specs=[o[2] for o in outs],
        out_shape=[jax.ShapeDtypeStruct(o[0], o[1]) for o in outs],
        compiler_params=_params(60, 1),
    )(x1, tgt, wffn, wdown, gpre, gpost)


def _mix_bwd(dx1, om, ya, yb, gpre, uvpre, attn, wout, wa, wb, wsm, wsmt, bsf, gsgu, bsgu, gpost,
             wmask, egrp):
    s_len = dx1.shape[0]
    tm = TOKEN_TILE
    nw = tm // WINDOW
    nt = s_len // tm

    def body(dx1_ref, om_ref, ya_ref, yb_ref, gp_ref, uv_ref, o_ref, wo_ref, wa_ref, wb_ref, ws_ref,
             wst_ref, bs_ref, gs_ref, bsg_ref, gpost_ref, mask_ref, eg_ref,
             dom_ref, dya_ref, dyb_ref, dgp_ref, dot_ref, delta_ref, duv_ref,
             dws_ref, dbs_ref, dgs_ref, dbsg_ref, dgpost_ref, dbs_acc):
        step = pl.program_id(0)

        @pl.when(step == 0)
        def _():
            dws_ref[...] = jnp.zeros_like(dws_ref)
            dbs_acc[...] = jnp.zeros_like(dbs_acc)
            dgs_ref[...] = jnp.zeros_like(dgs_ref)
            dbsg_ref[...] = jnp.zeros_like(dbsg_ref)
            dgpost_ref[...] = jnp.zeros_like(dgpost_ref)

        om = om_ref[...]
        dx1v = dx1_ref[...]
        r = lax.rsqrt(jnp.mean(om * om, axis=-1, keepdims=True) + EPS)
        gpost_v = gpost_ref[...]
        dgpost_ref[...] += jnp.sum(dx1v * om * r, axis=0, keepdims=True)
        dom = _rms_bwd(om, r, gpost_v, dx1v).astype(BF16)
        dom_ref[...] = dom
        dmg = _dot_nt(dom, wo_ref[...])

        gates = _sigmoid(gp_ref[...].astype(F32))
        ga, gb = gates[:, :D_MODEL], gates[:, D_MODEL:]
        yav, ybv = ya_ref[...].astype(F32), yb_ref[...].astype(F32)
        dya = (dmg * ga).astype(BF16)
        dyb = (dmg * gb).astype(BF16)
        dya_ref[...] = dya
        dyb_ref[...] = dyb
        dgp_ref[:, :D_MODEL] = (dmg * yav * ga * (1.0 - ga)).astype(BF16)
        dgp_ref[:, D_MODEL:] = (dmg * ybv * gb * (1.0 - gb)).astype(BF16)

        dat_t = _dot_nt(dya, wa_ref[...]).T.astype(BF16)
        dot_ref[0] = dat_t
        o_t = o_ref[...].astype(F32).T
        delta_ref[0] = jnp.sum((dat_t.astype(F32) * o_t).reshape(HEADS, HEAD_DIM, tm), axis=1)
        dsgu = _dot_nt(dyb, wb_ref[...])

        uvp = uv_ref[...].astype(F32)
        uv, guv = _gelu_and_grad(uvp)
        u, vv = uv[:, :SGU_W], uv[:, SGU_W:]
        gs_v = gs_ref[...]
        vn, xh, rln = _layernorm_fwd(vv, gs_v, bsg_ref[...])
        bias = bs_ref[...]
        if nw > 1:
            bias = jnp.concatenate([bias] * nw, axis=0)
        mixed = _sgu_mix(vn, ws_ref) + bias
        du = dsgu * mixed
        dmixed = dsgu * u

        lane = lax.broadcasted_iota(jnp.int32, (WINDOW, 128), 1)
        low = lane < HEAD_DIM
        dvn_wins = []
        for w in range(nw):
            rows = slice(w * WINDOW, (w + 1) * WINDOW)
            dbs_acc[...] += dmixed[rows, :]
            slabs = []
            for p in range(GROUPS // 2):
                cols = slice(p * 128, (p + 1) * 128)
                dm2 = dmixed[rows, cols]
                dlo = jnp.where(low, dm2, 0.0).astype(BF16)
                dhi = jnp.where(low, 0.0, dm2).astype(BF16)
                vn2 = vn[rows, cols].astype(BF16)
                dws_ref[2 * p] += _dot_nt(dlo, vn2)
                dws_ref[2 * p + 1] += _dot_nt(dhi, vn2)
                slabs.append(jnp.dot(wst_ref[2 * p], dlo, preferred_element_type=F32)
                             + jnp.dot(wst_ref[2 * p + 1], dhi, preferred_element_type=F32))
            dvn_wins.append(jnp.concatenate(slabs, axis=1))
        dvn = jnp.concatenate(dvn_wins, axis=0) if nw > 1 else dvn_wins[0]

        dgs_ref[...] += jnp.sum(dvn * xh, axis=0, keepdims=True)
        dbsg_ref[...] += jnp.sum(dvn, axis=0, keepdims=True)
        dxh = dvn * gs_v
        dvv = rln * (dxh - jnp.mean(dxh, axis=-1, keepdims=True)
                     - xh * jnp.mean(dxh * xh, axis=-1, keepdims=True))
        duv_ref[:, :SGU_W] = (du * guv[:, :SGU_W]).astype(BF16)
        duv_ref[:, SGU_W:] = (dvv * guv[:, SGU_W:]).astype(BF16)

        @pl.when(step == pl.num_programs(0) - 1)
        def _():
            for g in range(GROUPS):
                dws_ref[g] = dws_ref[g] * mask_ref[...]
            dbs_ref[...] = _split3_dot(dbs_acc[...], eg_ref[...])

    rows_out = [((s_len, D_MODEL), BF16, _row_spec(tm, D_MODEL)),
                ((s_len, D_MODEL), BF16, _row_spec(tm, D_MODEL)),
                ((s_len, D_MODEL), BF16, _row_spec(tm, D_MODEL)),
                ((s_len, 2 * D_MODEL), BF16, _row_spec(tm, 2 * D_MODEL)),
                ((nt, FOX_W, tm), BF16, _tile_spec(FOX_W, tm)),
                ((nt, HEADS, tm), F32, _tile_spec(HEADS, tm)),
                ((s_len, 2 * SGU_W), BF16, _row_spec(tm, 2 * SGU_W))]
    acc_out = [((GROUPS, WINDOW, WINDOW), F32), ((WINDOW, 128), F32), ((1, SGU_W), F32),
               ((1, SGU_W), F32), ((1, D_MODEL), F32)]
    return pl.pallas_call(
        body, name="mix_bwd", grid=(nt,),
        in_specs=[_row_spec(tm, D_MODEL), _row_spec(tm, D_MODEL), _row_spec(tm, D_MODEL),
                  _row_spec(tm, D_MODEL), _row_spec(tm, 2 * D_MODEL), _row_spec(tm, 2 * SGU_W),
                  _row_spec(tm, FOX_W), _const_spec(wout.shape), _const_spec(wa.shape),
                  _const_spec(wb.shape), _const_spec(wsm.shape), _const_spec(wsmt.shape),
                  _const_spec(bsf.shape), _const_spec((1, SGU_W)), _const_spec((1, SGU_W)),
                  _const_spec((1, D_MODEL)), _const_spec(wmask.shape), _const_spec(egrp.shape)],
        out_specs=[o[2] for o in rows_out] + [_const_spec(s) for s, _ in acc_out],
        out_shape=[jax.ShapeDtypeStruct(o[0], o[1]) for o in rows_out]
        + [jax.ShapeDtypeStruct(s, dt) for s, dt in acc_out],
        scratch_shapes=[pltpu.VMEM((WINDOW, SGU_W), F32)],
        compiler_params=_params(48, 1),
    )(dx1, om, ya, yb, gpre, uvpre, attn, wout, wa, wb, wsm, wsmt, bsf, gsgu, bsgu, gpost, wmask,
      egrp)


def _attn_bwd(qa, ka, kat, vs, dot_, lse, delta, ecol):
    s_len = qa.shape[0]
    t = ATTN_TILE
    nb = s_len // t

    def body(k_ref, kt_ref, vs_ref, q_ref, do_ref, lse_ref, dl_ref, ec_ref, gk_ref, dvt_ref,
             gqt_ref, csum_ref, p_sc, ds_sc):
        j = pl.program_id(0)

        @pl.when(j == 0)
        def _():
            gqt_ref[...] = jnp.zeros_like(gqt_ref)

        gk_ref[...] = jnp.zeros_like(gk_ref)
        dvt_ref[...] = jnp.zeros_like(dvt_ref)

        def tile(i, masked):
            qrows = pl.ds(pl.multiple_of(i * t, t), t)
            if masked:
                keep = (lax.broadcasted_iota(jnp.int32, (t, t), 0)
                        <= lax.broadcasted_iota(jnp.int32, (t, t), 1))
            for hd in range(HEADS):
                sl = slice(hd * 128, (hd + 1) * 128)
                hr = slice(hd * HEAD_DIM, (hd + 1) * HEAD_DIM)
                st = _dot_nt(k_ref[:, sl], q_ref[qrows, sl])
                if masked:
                    st = jnp.where(keep, st, -jnp.inf)
                pt = jnp.exp2(st - lse_ref[i, hd:hd + 1, :])
                dpt = jnp.dot(vs_ref[:, hd * 128:hd * 128 + HEAD_DIM], do_ref[i, hr, :],
                              preferred_element_type=F32)
                p_sc[hd] = pt.astype(BF16)
                ds_sc[hd] = (pt * (dpt - dl_ref[i, hd:hd + 1, :])).astype(BF16)
            for hd in range(HEADS):
                sl = slice(hd * 128, (hd + 1) * 128)
                hr = slice(hd * HEAD_DIM, (hd + 1) * HEAD_DIM)
                dst = ds_sc[hd]
                dvt_ref[0, hr, :] += _dot_nt(do_ref[i, hr, :], p_sc[hd])
                gk_ref[:, sl] += jnp.dot(dst, q_ref[qrows, sl], preferred_element_type=F32)
                gqt_ref[i, hd * QT_ROWS:(hd + 1) * QT_ROWS, :] += jnp.dot(
                    kt_ref[0, hd * 128:hd * 128 + QT_ROWS, :], dst, preferred_element_type=F32)

        tile(j, True)

        def below_diagonal(i, carry):
            tile(i, False)
            return carry

        lax.fori_loop(j + 1, nb, below_diagonal, 0)
        csum_ref[...] = _split3_dot(gk_ref[...], ec_ref[...])

    return pl.pallas_call(
        body, name="attn_bwd", grid=(nb,),
        in_specs=[_row_spec(t, SLAB_W), _tile_spec(SLAB_W, t), _row_spec(t, SLAB_W),
                  _const_spec(qa.shape), _const_spec(dot_.shape), _const_spec(lse.shape),
                  _const_spec(delta.shape), _const_spec(ecol.shape)],
        out_specs=[_row_spec(t, SLAB_W), _tile_spec(FOX_W, t),
                   _const_spec((nb, HEADS * QT_ROWS, t)), _row_spec(t, 128)],
        out_shape=[jax.ShapeDtypeStruct((s_len, SLAB_W), F32),
                   jax.ShapeDtypeStruct((nb, FOX_W, t), F32),
                   jax.ShapeDtypeStruct((nb, HEADS * QT_ROWS, t), F32),
                   jax.ShapeDtypeStruct((s_len, 128), F32)],
        scratch_shapes=[pltpu.VMEM((HEADS, t, t), BF16), pltpu.VMEM((HEADS, t, t), BF16)],
        compiler_params=_params(60, 1),
    )(ka, kat, vs, qa, dot_, lse, delta, ecol)


def _rev_cumsum(col_sums, gqt, triu):
    s_len = col_sums.shape[0]
    tm = TOKEN_TILE
    n = s_len // tm

    def body(cs_ref, gqt_ref, tri_ref, o_ref, carry):
        @pl.when(pl.program_id(0) == 0)
        def _():
            carry[...] = jnp.zeros_like(carry)
        rows = [gqt_ref[0, hd * QT_ROWS + HEAD_DIM:hd * QT_ROWS + HEAD_DIM + 1, :]
                for hd in range(HEADS)]
        row_sums = jnp.concatenate(rows + [jnp.zeros((128 - HEADS, tm), F32)], axis=0).T
        out = _tri_dot(tri_ref[...], row_sums - cs_ref[...]) + carry[...]
        o_ref[...] = out
        carry[...] = out[0:1, :]

    return pl.pallas_call(
        body, name="rev_cumsum", grid=(n,),
        in_specs=[pl.BlockSpec((tm, 128), lambda i: (n - 1 - i, 0)),
                  pl.BlockSpec((1, HEADS * QT_ROWS, tm), lambda i: (n - 1 - i, 0, 0)),
                  _const_spec((tm, tm))],
        out_specs=pl.BlockSpec((tm, 128), lambda i: (n - 1 - i, 0)),
        out_shape=jax.ShapeDtypeStruct((s_len, 128), F32),
        scratch_shapes=[pltpu.VMEM((1, 128), F32)],
        compiler_params=_params(32, 1),
    )(col_sums, gqt, triu)


def _heads_from_slabs(slabs):
    lane = lax.broadcasted_iota(jnp.int32, slabs[0].shape, 1)
    low = lane < HEAD_DIM
    pairs = [jnp.where(low, slabs[2 * p], pltpu.roll(slabs[2 * p + 1], HEAD_DIM, 1))
             for p in range(HEADS // 2)]
    return jnp.concatenate(pairs, axis=1)


def _proj_bwd(gqt, gk, dvt, dlogf, flog, qraw, kraw, duv, dgp, x, dx1, wcat, bdiag, gq, gk_gain, g1,
              efold):
    s_len = x.shape[0]
    tm = TOKEN_TILE

    def body(gqt_ref, gkk_ref, dvt_ref, dlf_ref, flog_ref, qr_ref, kr_ref, duv_ref, dgp_ref, x_ref,
             dx1_ref, w_ref, bd_ref, gq_ref, gk_ref, g1_ref, ef_ref,
             dx_ref, dprojt_ref, dgq_ref, dgk_ref, dbf_ref, dg1_ref, gq_acc, gk_acc, dproj_ref):
        step = pl.program_id(0)

        @pl.when(step == 0)
        def _():
            gq_acc[...] = jnp.zeros_like(gq_acc)
            gk_acc[...] = jnp.zeros_like(gk_acc)
            dbf_ref[...] = jnp.zeros_like(dbf_ref)
            dg1_ref[...] = jnp.zeros_like(dg1_ref)

        pad = jnp.zeros((128 - QT_ROWS, tm), F32)
        q_slabs = [jnp.concatenate([gqt_ref[0, hd * QT_ROWS:(hd + 1) * QT_ROWS, :], pad], axis=0).T
                   for hd in range(HEADS)]
        dqn = _heads_from_slabs(q_slabs)
        dkn = _heads_from_slabs([gkk_ref[:, hd * 128:(hd + 1) * 128] for hd in range(HEADS)])

        def head_bwd(raw_ref, dn, g_ref, acc):
            raw = raw_ref[...].astype(F32)
            r = lax.rsqrt(_seg_mean(raw * raw, bd_ref) + EPS)
            xhat = raw * r
            acc[0:1, :] += jnp.sum(dn * xhat, axis=0, keepdims=True)
            dyg = dn * g_ref[...]
            return r * (dyg - xhat * _seg_mean(dyg * xhat, bd_ref))

        dproj_ref[:, C_Q:C_K] = head_bwd(qr_ref, dqn * HEAD_DIM ** -0.5, gq_ref, gq_acc).astype(BF16)
        dproj_ref[:, C_K:C_V] = head_bwd(kr_ref, dkn * LN2, gk_ref, gk_acc).astype(BF16)
        dproj_ref[:, C_V:C_F] = dvt_ref[0].T.astype(BF16)
        dfl = dlf_ref[...] * _sigmoid(-flog_ref[...])
        dbf_ref[...] += jnp.sum(dfl, axis=0, keepdims=True)
        dproj_ref[:, C_F:C_UV] = dfl.astype(BF16)
        dproj_ref[:, C_UV:C_G] = duv_ref[...]
        dproj_ref[:, C_G:C_END] = dgp_ref[...]

        dproj = dproj_ref[...]
        dprojt_ref[...] = dproj.astype(F32).T.astype(BF16)
        dh = jnp.dot(dproj, w_ref[...], preferred_element_type=F32)
        xf = x_ref[...]
        r = lax.rsqrt(jnp.mean(xf * xf, axis=-1, keepdims=True) + EPS)
        dg1_ref[...] += jnp.sum(dh * xf * r, axis=0, keepdims=True)
        dx_ref[...] = dx1_ref[...] + _rms_bwd(xf, r, g1_ref[...], dh)

        @pl.when(step == pl.num_programs(0) - 1)
        def _():
            dgq_ref[...] = _split3_dot(gq_acc[...], ef_ref[...])
            dgk_ref[...] = _split3_dot(gk_acc[...], ef_ref[...])

    outs = [((s_len, D_MODEL), F32, _row_spec(tm, D_MODEL)),
            ((C_END, s_len), BF16, pl.BlockSpec((C_END, tm), lambda i: (0, i))),
            ((8, 128), F32, _const_spec((8, 128))),
            ((8, 128), F32, _const_spec((8, 128))),
            ((1, 128), F32, _const_spec((1, 128))),
            ((1, D_MODEL), F32, _const_spec((1, D_MODEL)))]
    return pl.pallas_call(
        body, name="proj_bwd", grid=(s_len // tm,),
        in_specs=[_tile_spec(HEADS * QT_ROWS, tm), _row_spec(tm, SLAB_W), _tile_spec(FOX_W, tm),
                  _row_spec(tm, 128), _row_spec(tm, 128), _row_spec(tm, FOX_W),
                  _row_spec(tm, FOX_W), _row_spec(tm, 2 * SGU_W), _row_spec(tm, 2 * D_MODEL),
                  _row_spec(tm, D_MODEL), _row_spec(tm, D_MODEL), _const_spec(wcat.shape),
                  _const_spec(bdiag.shape), _const_spec((1, FOX_W)), _const_spec((1, FOX_W)),
                  _const_spec((1, D_MODEL)), _const_spec(efold.shape)],
        out_specs=[o[2] for o in outs],
        out_shape=[jax.ShapeDtypeStruct(o[0], o[1]) for o in outs],
        scratch_shapes=[pltpu.VMEM((8, FOX_W), F32), pltpu.VMEM((8, FOX_W), F32),
                        pltpu.VMEM((tm, C_END), BF16)],
        compiler_params=_params(56, 1),
    )(gqt, gk, dvt, dlogf, flog, qraw, kraw, duv, dgp, x, dx1, wcat, bdiag, gq, gk_gain, g1, efold)


def _dw_matmul(at, b, tm, name):
    m, s_len = at.shape
    n = b.shape[1]

    def body(a_ref, b_ref, o_ref):
        o_ref[...] = jnp.dot(a_ref[...], b_ref[...], preferred_element_type=F32).astype(BF16)

    return pl.pallas_call(
        body, name=name, grid=(m // tm,),
        in_specs=[pl.BlockSpec((tm, s_len), lambda i: (i, 0)), _const_spec(b.shape)],
        out_specs=pl.BlockSpec((tm, n), lambda i: (i, 0)),
        out_shape=jax.ShapeDtypeStruct((m, n), BF16),
        compiler_params=_params(48, 1),
    )(at, b)


def _adamw(parts, w, m, v, tr, name, col_tile=None):
    n, part_rows, cols = parts.shape
    rows = w.shape[0]
    bc1 = 1.0 - ADAM_B1 ** ADAM_STEP
    bc2 = 1.0 - ADAM_B2 ** ADAM_STEP

    def body(p_ref, w_ref, m_ref, v_ref, g_ref, d_ref, mo_ref, vo_ref):
        g = p_ref[0, :rows, :].astype(F32)
        for idx in range(1, n):
            g = g + p_ref[idx, :rows, :].astype(F32)
        g_ref[...] = g
        mn = ADAM_B1 * m_ref[...] + (1.0 - ADAM_B1) * g
        vn = ADAM_B2 * v_ref[...] + (1.0 - ADAM_B2) * (g * g)
        mo_ref[...] = mn
        vo_ref[...] = vn
        m_hat = mn / bc1
        v_hat = vn / bc2
        d_ref[...] = -ADAM_LR * (m_hat / (jnp.sqrt(v_hat) + ADAM_EPS) + ADAM_WD * w_ref[...])

    if col_tile is None:
        spec = pl.BlockSpec((tr, cols), lambda i: (i, 0))
        pspec = pl.BlockSpec((n, tr, cols), lambda i: (0, i, 0))
        steps = rows // tr
    else:
        spec = pl.BlockSpec((rows, col_tile), lambda i: (0, i))
        pspec = pl.BlockSpec((n, part_rows, col_tile), lambda i: (0, 0, i))
        steps = cols // col_tile
    return pl.pallas_call(
        body, name=name, grid=(steps,),
        in_specs=[pspec, spec, spec, spec],
        out_specs=[spec] * 4,
        out_shape=[jax.ShapeDtypeStruct((rows, cols), F32)] * 4,
        compiler_params=_params(48, 1),
    )(parts, w, m, v)


def _sum_parts(parts, name):
    n, rows, cols = parts.shape

    def body(p_ref, o_ref):
        g = p_ref[0]
        for idx in range(1, n):
            g = g + p_ref[idx]
        o_ref[...] = g

    return pl.pallas_call(
        body, name=name, out_shape=jax.ShapeDtypeStruct((rows, cols), F32),
        in_specs=[_const_spec(parts.shape)], out_specs=_const_spec((rows, cols)), grid=(1,),
        compiler_params=_params(16, 1),
    )(parts)


SMALL_NAMES = ("g_pre_mix", "b_forget", "g_q", "g_k", "g_sgu", "b_sgu", "w_spatial", "b_spatial",
               "g_post_mix", "g_pre_ffn", "g_post_ffn")


def _small_rows(size):
    return -(-size // 1024)


def _pack_small(d, extra=None):
    rows = []
    for k in SMALL_NAMES:
        flat = d[k].reshape(-1).astype(F32)
        nr = _small_rows(flat.shape[0])
        rows.append(jnp.pad(flat, (0, nr * 1024 - flat.shape[0])).reshape(nr, 1024))
    if extra is not None:
        rows.append(extra)
    used = sum(r.shape[0] for r in rows)
    rows.append(jnp.zeros((N_DEV * SMALL_ROWS - used, 1024), F32))
    return jnp.concatenate(rows, axis=0)


def _unpack_small(packed, shapes):
    out, off = {}, 0
    for k in SMALL_NAMES:
        size = math.prod(shapes[k])
        nr = _small_rows(size)
        out[k] = packed[off:off + nr].reshape(-1)[:size].reshape(shapes[k])
        off += nr
    return out


def _cols_to_blocks(full, width):
    r = full.shape[0]
    return jnp.transpose(full.reshape(r, N_DEV, width), (1, 0, 2))


def _blocks_to_cols(blocks):
    n, r, width = blocks.shape
    return jnp.transpose(blocks, (1, 0, 2)).reshape(r, n * width)


def kernel(x, g_pre_mix, w_in, b_forget, g_q, g_k, g_sgu, b_sgu, w_spatial, b_spatial, w_branch_a, w_branch_b, w_out, g_post_mix, g_pre_ffn, w_ffn_in, w_ffn_down, g_post_ffn, loss_target, m_g_pre_mix, m_w_in, m_b_forget, m_g_q, m_g_k, m_g_sgu, m_b_sgu, m_w_spatial, m_b_spatial, m_w_branch_a, m_w_branch_b, m_w_out, m_g_post_mix, m_g_pre_ffn, m_w_ffn_in, m_w_ffn_down, m_g_post_ffn, v_g_pre_mix, v_w_in, v_b_forget, v_g_q, v_g_k, v_g_sgu, v_b_sgu, v_w_spatial, v_b_spatial, v_w_branch_a, v_w_branch_b, v_w_out, v_g_post_mix, v_g_pre_ffn, v_w_ffn_in, v_w_ffn_down, v_g_post_ffn):
    big_names = ("w_in", "w_branch_a", "w_branch_b", "w_out", "w_ffn_in", "w_ffn_down")
    weights = dict(g_pre_mix=g_pre_mix, w_in=w_in, b_forget=b_forget, g_q=g_q, g_k=g_k, g_sgu=g_sgu,
                   b_sgu=b_sgu, w_spatial=w_spatial, b_spatial=b_spatial, w_branch_a=w_branch_a,
                   w_branch_b=w_branch_b, w_out=w_out, g_post_mix=g_post_mix, g_pre_ffn=g_pre_ffn,
                   w_ffn_in=w_ffn_in, w_ffn_down=w_ffn_down, g_post_ffn=g_post_ffn)
    mom1 = dict(g_pre_mix=m_g_pre_mix, w_in=m_w_in, b_forget=m_b_forget, g_q=m_g_q, g_k=m_g_k,
                g_sgu=m_g_sgu, b_sgu=m_b_sgu, w_spatial=m_w_spatial, b_spatial=m_b_spatial,
                w_branch_a=m_w_branch_a, w_branch_b=m_w_branch_b, w_out=m_w_out,
                g_post_mix=m_g_post_mix, g_pre_ffn=m_g_pre_ffn, w_ffn_in=m_w_ffn_in,
                w_ffn_down=m_w_ffn_down, g_post_ffn=m_g_post_ffn)
    mom2 = dict(g_pre_mix=v_g_pre_mix, w_in=v_w_in, b_forget=v_b_forget, g_q=v_g_q, g_k=v_g_k,
                g_sgu=v_g_sgu, b_sgu=v_b_sgu, w_spatial=v_w_spatial, b_spatial=v_b_spatial,
                w_branch_a=v_w_branch_a, w_branch_b=v_w_branch_b, w_out=v_w_out,
                g_post_mix=v_g_post_mix, g_pre_ffn=v_g_pre_ffn, w_ffn_in=v_w_ffn_in,
                w_ffn_down=v_w_ffn_down, g_post_ffn=v_g_post_ffn)
    names = list(weights)
    shapes = {k: weights[k].shape for k in names}

    s_len = x.shape[1]
    xs = x.reshape(s_len, D_MODEL)
    tgt = loss_target.reshape(s_len, D_MODEL)

    transposed = ("w_in", "w_ffn_in")

    def local_view(a, k):
        return jnp.transpose(a[0]) if k in transposed else a[0]

    shards = {k: local_view(weights[k], k).astype(BF16) for k in big_names}
    win_t = _gather_two_level(shards["w_in"], "gather_w_in").reshape(IN_COLS, D_MODEL)
    win_t, later = lax.optimization_barrier(
        (win_t, [shards[k] for k in big_names if k != "w_in"]))
    shards.update(zip([k for k in big_names if k != "w_in"], later))
    (gat_mix, gat_ffn), gat_token = _exchange_start(
        [[shards["w_branch_a"], shards["w_branch_b"], shards["w_out"]],
         [shards["w_ffn_in"], shards["w_ffn_down"]]], "gather_start", gather=True)
    f_off = 3 * FOX_W
    u_off = f_off + HEADS
    wcat = jnp.concatenate([
        win_t[:f_off], jnp.pad(win_t[f_off:u_off], ((0, 128 - HEADS), (0, 0))), win_t[u_off:]],
        axis=0)

    seg = np.arange(FOX_W) // HEAD_DIM
    bdiag = jnp.asarray(seg[:128, None] == seg[None, :128], BF16)
    tm = TOKEN_TILE
    lower = np.arange(tm)[None, :] <= np.arange(tm)[:, None]
    tril = jnp.asarray(lower, BF16)
    triu = jnp.asarray(lower.T, BF16)
    egrp = jnp.asarray(seg[:, None] == np.arange(128)[None, :], BF16)
    efold = jnp.asarray((np.arange(FOX_W) % HEAD_DIM)[:, None] == np.arange(128)[None, :], BF16)
    gq512 = jnp.tile(g_q.reshape(1, HEAD_DIM), (1, HEADS))
    gk512 = jnp.tile(g_k.reshape(1, HEAD_DIM), (1, HEADS))
    bfor = jnp.pad(b_forget.reshape(1, HEADS), ((0, 0), (0, 128 - HEADS)))
    pos = np.arange(WINDOW)
    wmask = (pos[None, :] // CHUNK) <= (pos[:, None] // CHUNK)
    wsm_f = jnp.where(jnp.asarray(wmask)[None], w_spatial[0], 0.0)
    wsm = wsm_f.astype(BF16)
    wsmt = jnp.transpose(wsm_f, (0, 2, 1)).astype(BF16)
    bsf = jnp.repeat(jnp.transpose(b_spatial[0]), HEAD_DIM, axis=1)
    wmask_f = jnp.asarray(wmask, F32)

    col = np.arange(SLAB_W)
    row128 = np.arange(128)

    def d_place(first):
        return jnp.asarray(np.stack([(col[None, :] // 128 == row128[:, None])
                                     & (col[None, :] % 128 == first + a) for a in range(3)]), BF16)

    pdq, pdk = d_place(HEAD_DIM), d_place(HEAD_DIM + 3)
    ones_q = jnp.asarray((col % 128 >= HEAD_DIM + 3) & (col % 128 < HEAD_DIM + 6), F32)[None]
    ones_k = jnp.asarray((col % 128 >= HEAD_DIM) & (col % 128 < HEAD_DIM + 3), F32)[None]
    ecol = jnp.asarray((col[:, None] // 128 == row128[None, :])
                       & (col[:, None] % 128 == HEAD_DIM + 3), BF16)

    (h, qa, ka, kat, vs, vt, qraw, kraw, flog, uvpre, gpre) = _proj_fwd(
        xs, g_pre_mix + gat_token[0:1, 0:1], wcat, bdiag, gq512, gk512, bfor, tril, pdq, pdk,
        ones_q, ones_k)
    attn, attn_t, lse = _attn_fwd(qa, ka, vt)
    (own_a, own_b, own_out), (zone_a, zone_b, zone_out) = _exchange_wait(
        gat_mix, attn, "gather_wait_mix", gather=True)
    wa = _blocks_to_cols(_own_block(zone_a, own_a))
    wb = _blocks_to_cols(_own_block(zone_b, own_b))
    wout = _own_block(zone_out, own_out).reshape(D_MODEL, D_MODEL)
    sgu_t, ya, yb, merged_t, om, x1 = _mix_fwd(attn, uvpre, gpre, xs, wa, wb, wout, wsm, bsf,
                                           g_sgu, b_sgu, g_post_mix)
    (own_ffn, own_down), (zone_ffn, zone_down) = _exchange_wait(
        gat_ffn, x1, "gather_wait_ffn", gather=True)
    wffn = _own_block(zone_ffn, own_ffn).reshape(2 * D_FF, D_MODEL)
    wdown = _own_block(zone_down, own_down).reshape(D_FF, D_MODEL)
    (dx1, h2, act_t, dff, dgu_t, loss_acc, dg_post_ffn, dg_pre_ffn) = _ffn_fwd_bwd(
        x1, tgt, wffn, wdown, g_pre_ffn, g_post_ffn)

    dw_down = _dw_matmul(act_t, dff, D_FF // 4, "dw_down")
    dw_ffn = _dw_matmul(dgu_t, h2, 2 * D_FF // N_DEV, "dw_ffn_in")
    x_pos, y_pos, c_pos = _mesh_pos()
    me = 4 * x_pos + 2 * y_pos + c_pos

    def own_of(parts):
        return [lax.dynamic_index_in_dim(p, me, 0, keepdims=False) for p in parts]

    parts_ffn = [dw_ffn.reshape(N_DEV, 2 * D_FF // N_DEV, D_MODEL),
                 dw_down.reshape(N_DEV, D_FF // N_DEV, D_MODEL)]
    mine_ffn = own_of(parts_ffn)
    (sct_ffn,), sct_ffn_token = _exchange_start([parts_ffn], "scatter_start_ffn", gather=False)

    (dom, dya, dyb, dgp, dot_, delta, duv, dws, dbs, dg_sgu, db_sgu, dg_post_mix) = _mix_bwd(
        dx1, om, ya, yb, gpre, uvpre, attn, wout, wa, wb, wsm, wsmt, bsf, g_sgu, b_sgu,
        g_post_mix + sct_ffn_token[0:1, 0:1], wmask_f, egrp)
    dw_out = _dw_matmul(merged_t, dom, 512, "dw_out")
    dw_a = _dw_matmul(attn_t, dya, 512, "dw_a")
    dw_b = _dw_matmul(sgu_t, dyb, 512, "dw_b")
    parts_mix = [_cols_to_blocks(dw_a, D_MODEL // N_DEV), _cols_to_blocks(dw_b, D_MODEL // N_DEV),
                 dw_out.reshape(N_DEV, D_MODEL // N_DEV, D_MODEL)]
    mine_mix = own_of(parts_mix)
    (sct_mix,), sct_mix_token = _exchange_start([parts_mix], "scatter_start_mix", gather=False)

    gk_all, dvt, gqt, col_sums = _attn_bwd(qa, ka, kat, vs, dot_, lse,
                                           delta + sct_mix_token[0, 0], ecol)
    dlogf = _rev_cumsum(col_sums, gqt, triu)
    dx, dproj_t, dgq, dgk, dbf, dg_pre_mix = _proj_bwd(
        gqt, gk_all, dvt, dlogf, flog, qraw, kraw, duv, dgp, xs, dx1, wcat, bdiag, gq512, gk512,
        g_pre_mix, efold)
    dw_cat = _dw_matmul(dproj_t, h, C_END // N_DEV, "dw_in")
    dw_in = jnp.concatenate([dw_cat[:C_F + HEADS], dw_cat[C_UV:]], axis=0)

    small_local = dict(
        g_pre_mix=dg_pre_mix, b_forget=dbf[:, :HEADS], g_q=dgq[0:1, :HEAD_DIM],
        g_k=dgk[0:1, :HEAD_DIM], g_sgu=dg_sgu, b_sgu=db_sgu, w_spatial=dws,
        b_spatial=jnp.transpose(dbs[:, :GROUPS]), g_post_mix=dg_post_mix, g_pre_ffn=dg_pre_ffn,
        g_post_ffn=dg_post_ffn)
    loss_row = jnp.pad(loss_acc[0:1, 0:1], ((0, 0), (0, 1023)))
    small_parts = _pack_small(small_local, loss_row).reshape(N_DEV, SMALL_ROWS, 1024)

    grad_in_t = _reduce_scatter_two_level(dw_in.reshape(N_DEV, BLK, D_MODEL),
                                          "reduce_scatter_in")
    (recv_small,) = _exchange([small_parts], "scatter_small", gather=False)

    def with_own(zones, own_blocks):
        return [_own_block(z, b) for z, b in zip(zones, own_blocks)]

    recv_ffn, recv_down = with_own(
        _exchange_wait(sct_ffn, recv_small, "scatter_wait_ffn", gather=False)[1], mine_ffn)
    recv_a, recv_b, recv_out = with_own(
        _exchange_wait(sct_mix, recv_ffn, "scatter_wait_mix", gather=False)[1], mine_mix)
    received = [grad_in_t[None], recv_a, recv_b, recv_out, recv_ffn, recv_down]

    grads, deltas, new_m, new_v = {}, {}, {}, {}
    row_tiles = {"w_in": None, "w_branch_a": 512, "w_branch_b": 512, "w_out": 128, "w_ffn_in": 176,
                 "w_ffn_down": 352}
    for idx, k in enumerate(big_names):
        outs = _adamw(received[idx], local_view(weights[k], k), local_view(mom1[k], k),
                      local_view(mom2[k], k), row_tiles[k], "adamw_" + k,
                      col_tile=256 if k == "w_in" else None)
        if k in transposed:
            outs = [jnp.transpose(o) for o in outs]
        grads[k], deltas[k], new_m[k], new_v[k] = [o[None] for o in outs]

    small_sum = _sum_parts(recv_small, "sum_small")
    (small_all,) = _exchange([small_sum], "gather_small", gather=True)
    small_all = small_all.reshape(1, N_DEV * SMALL_ROWS, 1024)
    sg, sd, sm, sv = _adamw(small_all, _pack_small(weights), _pack_small(mom1), _pack_small(mom2),
                            N_DEV * SMALL_ROWS, "adamw_small")
    for dst, packed in ((grads, sg), (deltas, sd), (new_m, sm), (new_v, sv)):
        dst.update(_unpack_small(packed, shapes))

    loss = small_all[0, sum(_small_rows(math.prod(shapes[k])) for k in SMALL_NAMES), 0]
    return (loss, dx.reshape(x.shape), *[grads[k] for k in names], *[deltas[k] for k in names],
            *[new_m[k] for k in names], *[new_v[k] for k in names])
```

```python
import functools
import math

import jax
import jax.numpy as jnp
import numpy as np
from jax import lax
from jax.experimental import pallas as pl
from jax.experimental.pallas import tpu as pltpu

F32 = jnp.float32
BF16 = jnp.bfloat16

D_MODEL = 1024
FOX_W = 512
HEADS = 8
HEAD_DIM = 64
SGU_W = 512
GROUPS = 8
WINDOW = 128
CHUNK = 64
D_FF = 2816
IN_COLS = 4616
EPS = 1e-6
N_DEV = 8
LOG2E = 1.4426950408889634
LN2 = 0.6931471805599453

C_Q, C_K, C_V, C_F, C_UV, C_G, C_END = 0, 512, 1024, 1536, 1664, 2688, 4736

ADAM_LR, ADAM_B1, ADAM_B2, ADAM_EPS, ADAM_WD, ADAM_STEP = 0.001, 0.9, 0.999, 1e-08, 0.01, 10

MIB = 1024 * 1024
TOKEN_TILE = 256
ATTN_TILE = 256
SLAB_W = HEADS * 128
QT_ROWS = 72

SMALL_ROWS = 18
BLK = IN_COLS // N_DEV


def _params(vmem_mib, n_axes):
    return pltpu.CompilerParams(
        dimension_semantics=("arbitrary",) * n_axes, vmem_limit_bytes=vmem_mib * MIB)


def _const_spec(shape):
    nd = len(shape)
    return pl.BlockSpec(shape, lambda *_: (0,) * nd)


def _row_spec(tm, cols):
    return pl.BlockSpec((tm, cols), lambda i: (i, 0))


def _tile_spec(rows, tm):
    return pl.BlockSpec((1, rows, tm), lambda i: (i, 0, 0))


def _split3_dot(x, e):
    x1 = x.astype(BF16)
    r1 = x - x1.astype(F32)
    x2 = r1.astype(BF16)
    x3 = (r1 - x2.astype(F32)).astype(BF16)
    dot = functools.partial(jnp.dot, preferred_element_type=F32)
    return dot(x1, e) + dot(x2, e) + dot(x3, e)


def _tri_dot(tri, x):
    x1 = x.astype(BF16)
    r1 = x - x1.astype(F32)
    x2 = r1.astype(BF16)
    x3 = (r1 - x2.astype(F32)).astype(BF16)
    dot = functools.partial(jnp.dot, preferred_element_type=F32)
    return dot(tri, x1) + dot(tri, x2) + dot(tri, x3)


def _seg_mean(sq, bd_ref):
    hi = sq.astype(BF16)
    lo = (sq - hi.astype(F32)).astype(BF16)
    bd = bd_ref[...]
    dot = functools.partial(jnp.dot, preferred_element_type=F32)
    pairs = [dot(hi[:, p * 128:(p + 1) * 128], bd) + dot(lo[:, p * 128:(p + 1) * 128], bd)
             for p in range(HEADS // 2)]
    return jnp.concatenate(pairs, axis=1) * (1.0 / HEAD_DIM)


def _slabs_from_heads(t):
    lane = lax.broadcasted_iota(jnp.int32, (t.shape[0], 128), 1)
    low = lane < HEAD_DIM
    slabs = []
    for p in range(HEADS // 2):
        pair = t[:, p * 128:(p + 1) * 128]
        slabs.append(jnp.where(low, pair, 0.0))
        slabs.append(jnp.where(low, pltpu.roll(pair, HEAD_DIM, 1), 0.0))
    return jnp.concatenate(slabs, axis=1)


def _dot_nt(a, b):
    return lax.dot_general(a, b, (((1,), (1,)), ((), ())), preferred_element_type=F32)


def _dot_tn(a, b):
    return lax.dot_general(a, b, (((0,), (0,)), ((), ())), preferred_element_type=F32)


def _sigmoid(x):
    return 1.0 / (1.0 + jnp.exp(-x))


_GELU_C = math.sqrt(2.0 / math.pi)


def _gelu_and_grad(x):
    inner = _GELU_C * (x + 0.044715 * x * x * x)
    t = jnp.tanh(inner)
    y = 0.5 * x * (1.0 + t)
    dy = 0.5 * (1.0 + t) + 0.5 * x * (1.0 - t * t) * _GELU_C * (1.0 + 3.0 * 0.044715 * x * x)
    return y, dy


def _rms_bwd(xin, r, g, dy):
    dyg = dy * g
    return r * dyg - xin * (r * r * r) * jnp.mean(dyg * xin, axis=-1, keepdims=True)


def _mesh_pos():
    x, y, c = lax.axis_index("x"), lax.axis_index("y"), lax.axis_index("c")
    return x, y, c


def _peer(k):
    x, y, c = _mesh_pos()
    px = (1 - x) if (k >> 2) & 1 else x
    py = (1 - y) if (k >> 1) & 1 else y
    pc = (1 - c) if k & 1 else c
    return (px, py, pc), 4 * px + 2 * py + pc


def _exchange(arrs, name, gather):
    n = len(arrs)
    if gather:
        out_shape = [jax.ShapeDtypeStruct((N_DEV,) + a.shape, a.dtype) for a in arrs]
    else:
        out_shape = [jax.ShapeDtypeStruct(a.shape, a.dtype) for a in arrs]

    def body(*refs):
        ins, outs = refs[:n], refs[n:2 * n]
        send_sems, recv_sems, local_sems = refs[2 * n:]
        x, y, c = _mesh_pos()
        me = 4 * x + 2 * y + c

        def src(a, idx):
            return ins[a] if gather else ins[a].at[idx]

        local = []
        for a in range(n):
            cp = pltpu.make_async_copy(src(a, me), outs[a].at[me], local_sems.at[a])
            cp.start()
            local.append(cp)
        sends = []
        for k in range(1, N_DEV):
            peer, pidx = _peer(k)
            for a in range(n):
                cp = pltpu.make_async_remote_copy(
                    src_ref=src(a, pidx), dst_ref=outs[a].at[me],
                    send_sem=send_sems.at[a, k - 1], recv_sem=recv_sems.at[a, k - 1],
                    device_id=peer, device_id_type=pl.DeviceIdType.MESH)
                cp.start()
                sends.append(cp)
        for k in range(1, N_DEV):
            peer, pidx = _peer(k)
            for a in range(n):
                pltpu.make_async_remote_copy(
                    src_ref=src(a, pidx), dst_ref=outs[a].at[pidx],
                    send_sem=send_sems.at[a, k - 1], recv_sem=recv_sems.at[a, k - 1],
                    device_id=peer, device_id_type=pl.DeviceIdType.MESH).wait_recv()
        for cp in sends:
            cp.wait_send()
        for cp in local:
            cp.wait()

    any_spec = pl.BlockSpec(memory_space=pl.ANY)
    return pl.pallas_call(
        body, name=name, out_shape=out_shape,
        in_specs=[any_spec] * n, out_specs=[any_spec] * n,
        scratch_shapes=[pltpu.SemaphoreType.DMA((n, N_DEV - 1)),
                        pltpu.SemaphoreType.DMA((n, N_DEV - 1)),
                        pltpu.SemaphoreType.DMA((n,))],
    )(*arrs)


def _gather_two_level(shard, name):
    def body(x_ref, out_ref, send_sems, recv_sems, local_sem):
        x, y, c = _mesh_pos()
        me, sibling = (x, y, c), (x, y, 1 - c)
        chips = [(1 - x, y), (x, 1 - y), (1 - x, 1 - y)]

        def slot(px, py, pc):
            return out_ref.at[4 * px + 2 * py + pc]

        def copy(k, block, to, src=None):
            return pltpu.make_async_remote_copy(
                src_ref=slot(*block) if src is None else src, dst_ref=slot(*block),
                send_sem=send_sems.at[k], recv_sem=recv_sems.at[k],
                device_id=to, device_id_type=pl.DeviceIdType.MESH)

        mine = pltpu.make_async_copy(x_ref, slot(*me), local_sem)
        mine.start()
        first = [copy(1 + j, me, (*chip, c), src=x_ref) for j, chip in enumerate(chips)]
        first.append(copy(0, me, sibling, src=x_ref))
        for cp in first:
            cp.start()
        passed = [copy(4 + j, (*chip, c), sibling) for j, chip in enumerate(chips)]
        for j, chip in enumerate(chips):
            copy(1 + j, (*chip, c), me).wait_recv()
            passed[j].start()
        copy(0, sibling, me).wait_recv()
        for j, chip in enumerate(chips):
            copy(4 + j, (*chip, 1 - c), me).wait_recv()
        for cp in first + passed:
            cp.wait_send()
        mine.wait()

    any_spec = pl.BlockSpec(memory_space=pl.ANY)
    return pl.pallas_call(
        body, name=name, out_shape=jax.ShapeDtypeStruct((N_DEV,) + shard.shape, shard.dtype),
        in_specs=[any_spec], out_specs=any_spec,
        scratch_shapes=[pltpu.SemaphoreType.DMA((7,)), pltpu.SemaphoreType.DMA((7,)),
                        pltpu.SemaphoreType.DMA],
    )(shard)


def _chip_peer(k):
    x, y, c = _mesh_pos()
    px = (1 - x) if (k >> 1) & 1 else x
    py = (1 - y) if k & 1 else y
    return (px, py, c), 2 * px + py


def _pair_sums(parts, name):
    _, rows, cols = parts.shape
    n_chips = N_DEV // 2

    def body(p_ref, send_ref, own_ref, mine_buf, sib_buf, send_sems, recv_sems, local_sems):
        x, y, c = _mesh_pos()
        sibling = (x, y, 1 - c)
        copies, local = [], []
        for q in range(n_chips):
            cp = pltpu.make_async_remote_copy(
                src_ref=p_ref.at[2 * q + (1 - c)], dst_ref=sib_buf.at[q],
                send_sem=send_sems.at[q], recv_sem=recv_sems.at[q],
                device_id=sibling, device_id_type=pl.DeviceIdType.MESH)
            cp.start()
            copies.append(cp)
            lc = pltpu.make_async_copy(p_ref.at[2 * q + c], mine_buf.at[q], local_sems.at[q])
            lc.start()
            local.append(lc)
        for lc in local:
            lc.wait()
        for cp in copies:
            cp.wait_recv()
        for k in range(1, n_chips):
            _, q = _chip_peer(k)
            send_ref[k - 1] = (mine_buf[q].astype(F32) + sib_buf[q].astype(F32)).astype(BF16)
        my_chip = 2 * x + y
        own_ref[...] = mine_buf[my_chip].astype(F32) + sib_buf[my_chip].astype(F32)
        for cp in copies:
            cp.wait_send()

    vmem = pl.BlockSpec(memory_space=pltpu.VMEM)
    return pl.pallas_call(
        body, name=name,
        out_shape=[jax.ShapeDtypeStruct((n_chips - 1, rows, cols), BF16),
                   jax.ShapeDtypeStruct((rows, cols), F32)],
        in_specs=[pl.BlockSpec(memory_space=pl.ANY)], out_specs=[vmem, vmem],
        scratch_shapes=[pltpu.VMEM((n_chips, rows, cols), BF16),
                        pltpu.VMEM((n_chips, rows, cols), BF16),
                        pltpu.SemaphoreType.DMA((n_chips,)), pltpu.SemaphoreType.DMA((n_chips,)),
                        pltpu.SemaphoreType.DMA((n_chips,))],
        compiler_params=pltpu.CompilerParams(vmem_limit_bytes=40 * MIB),
    )(parts)


def _chip_copy(src_ref, land_ref, send_sem, recv_sem, k):
    peer, _ = _chip_peer(k)
    return pltpu.make_async_remote_copy(
        src_ref=src_ref.at[k - 1], dst_ref=land_ref.at[k - 1], send_sem=send_sem, recv_sem=recv_sem,
        device_id=peer, device_id_type=pl.DeviceIdType.MESH)


def _chip_exchange_start(blocks, name):
    hbm = pl.BlockSpec(memory_space=pltpu.HBM)
    sem = pl.BlockSpec(memory_space=pltpu.SEMAPHORE)
    n_peers = blocks.shape[0]

    def body(src_ref, zone_ref, send_sems, recv_sems, src_thru, zone_thru, token):
        for k in range(1, n_peers + 1):
            _chip_copy(src_ref, zone_ref, send_sems.at[k - 1], recv_sems.at[k - 1], k).start()
        token[...] = jnp.zeros_like(token)

    outs = pl.pallas_call(
        body, name=name, in_specs=[hbm, hbm],
        out_shape=[pltpu.SemaphoreType.DMA((n_peers,)), pltpu.SemaphoreType.DMA((n_peers,)),
                   pltpu.HBM(blocks.shape, blocks.dtype), pltpu.HBM(blocks.shape, blocks.dtype),
                   jax.ShapeDtypeStruct((8, 128), F32)],
        out_specs=[sem, sem, hbm, hbm, pl.BlockSpec(memory_space=pltpu.VMEM)],
        input_output_aliases={0: 2, 1: 3},
        compiler_params=pltpu.CompilerParams(
            has_side_effects=pltpu.SideEffectType.DATAFLOW_SIDE_EFFECTING),
    )(pltpu.with_memory_space_constraint(blocks, pltpu.HBM),
      pltpu.with_memory_space_constraint(lax.empty(blocks.shape, blocks.dtype), pltpu.HBM))
    return outs[:4], outs[4]


def _chip_exchange_wait(handle, after, name):
    send_sems, recv_sems, src, zone = handle
    hbm = pl.BlockSpec(memory_space=pltpu.HBM)
    sem = pl.BlockSpec(memory_space=pltpu.SEMAPHORE)

    def body(src_ref, zone_ref, ssem, rsem, after_ref, src_out, zone_out):
        for k in range(1, src.shape[0] + 1):
            cp = _chip_copy(src_ref, zone_ref, ssem.at[k - 1], rsem.at[k - 1], k)
            cp.wait_send()
            cp.wait_recv()

    outs = pl.pallas_call(
        body, name=name,
        in_specs=[hbm, hbm, sem, sem, pl.BlockSpec(memory_space=pl.ANY)],
        out_shape=[pltpu.HBM(src.shape, src.dtype), pltpu.HBM(zone.shape, zone.dtype)],
        out_specs=[hbm, hbm], input_output_aliases={0: 0, 1: 1},
        compiler_params=pltpu.CompilerParams(
            has_side_effects=pltpu.SideEffectType.DATAFLOW_SIDE_EFFECTING),
    )(src, zone, send_sems, recv_sems, after)
    return outs[1]


def _remote_copy(gather, src_ref, land_ref, send_sem, recv_sem, k, receive_side):
    x, y, c = _mesh_pos()
    me = 4 * x + 2 * y + c
    peer, pidx = _peer(k)
    return pltpu.make_async_remote_copy(
        src_ref=src_ref if gather else src_ref.at[pidx],
        dst_ref=land_ref.at[pidx if receive_side else me],
        send_sem=send_sem, recv_sem=recv_sem,
        device_id=peer, device_id_type=pl.DeviceIdType.MESH)


def _exchange_start(groups, name, gather):
    arrs = [a for g in groups for a in g]
    n, n_groups = len(arrs), len(groups)
    lands = [jax.ShapeDtypeStruct(((N_DEV,) + a.shape) if gather else a.shape, a.dtype)
             for a in arrs]

    def body(*refs):
        srcs, zones = refs[:n], refs[n:2 * n]
        sems = refs[2 * n:2 * n + 2 * n_groups]
        token = refs[-1]
        a = 0
        for gi, g in enumerate(groups):
            send_sems, recv_sems = sems[2 * gi], sems[2 * gi + 1]
            for k in range(1, N_DEV):
                for ai in range(len(g)):
                    slot = ai * (N_DEV - 1) + k - 1
                    _remote_copy(gather, srcs[a + ai], zones[a + ai], send_sems.at[slot],
                                 recv_sems.at[slot], k, False).start()
            a += len(g)
        token[...] = jnp.zeros_like(token)

    hbm = pl.BlockSpec(memory_space=pltpu.HBM)
    sem = pl.BlockSpec(memory_space=pltpu.SEMAPHORE)
    sem_shapes = []
    for g in groups:
        sem_shapes += [pltpu.SemaphoreType.DMA((len(g) * (N_DEV - 1),))] * 2
    outs = pl.pallas_call(
        body, name=name,
        in_specs=[hbm] * (2 * n),
        out_shape=sem_shapes + [pltpu.HBM(a.shape, a.dtype) for a in arrs]
        + [pltpu.HBM(z.shape, z.dtype) for z in lands] + [jax.ShapeDtypeStruct((8, 128), F32)],
        out_specs=[sem] * (2 * n_groups) + [hbm] * (2 * n)
        + [pl.BlockSpec(memory_space=pltpu.VMEM)],
        input_output_aliases={i: 2 * n_groups + i for i in range(2 * n)},
        compiler_params=pltpu.CompilerParams(
            has_side_effects=pltpu.SideEffectType.DATAFLOW_SIDE_EFFECTING),
    )(*[pltpu.with_memory_space_constraint(a, pltpu.HBM) for a in arrs],
      *[pltpu.with_memory_space_constraint(lax.empty(z.shape, z.dtype), pltpu.HBM) for z in lands])
    sems = outs[:2 * n_groups]
    thru = outs[2 * n_groups:2 * n_groups + n]
    zones = outs[2 * n_groups + n:2 * n_groups + 2 * n]
    handles, a = [], 0
    for gi, g in enumerate(groups):
        handles.append((sems[2 * gi], sems[2 * gi + 1], thru[a:a + len(g)], zones[a:a + len(g)]))
        a += len(g)
    return handles, outs[-1]


def _exchange_wait(handle, after, name, gather):
    send_sems, recv_sems, thru, zones = handle
    n = len(thru)

    def body(*refs):
        srcs, lands = refs[:n], refs[n:2 * n]
        ssem, rsem = refs[2 * n], refs[2 * n + 1]
        for k in range(1, N_DEV):
            for ai in range(n):
                slot = ai * (N_DEV - 1) + k - 1
                cp = _remote_copy(gather, srcs[ai], lands[ai], ssem.at[slot], rsem.at[slot], k, True)
                cp.wait_send()
                cp.wait_recv()

    hbm = pl.BlockSpec(memory_space=pltpu.HBM)
    sem = pl.BlockSpec(memory_space=pltpu.SEMAPHORE)
    outs = pl.pallas_call(
        body, name=name,
        in_specs=[hbm] * (2 * n) + [sem, sem, pl.BlockSpec(memory_space=pl.ANY)],
        out_shape=[pltpu.HBM(a.shape, a.dtype) for a in thru]
        + [pltpu.HBM(z.shape, z.dtype) for z in zones],
        out_specs=[hbm] * (2 * n),
        input_output_aliases={i: i for i in range(2 * n)},
        compiler_params=pltpu.CompilerParams(
            has_side_effects=pltpu.SideEffectType.DATAFLOW_SIDE_EFFECTING),
    )(*thru, *zones, send_sems, recv_sems, after)
    return outs[:n], outs[n:]


def _own_block(zone, block):
    x, y, c = _mesh_pos()
    me = 4 * x + 2 * y + c
    return lax.dynamic_update_slice_in_dim(zone, block[None], me, axis=0)


def _proj_fwd(x, g1, wcat, bdiag, gq, gk, bfor, tri, pdq, pdk, ones_q, ones_k):
    s_len = x.shape[0]
    tm = TOKEN_TILE
    nt = s_len // tm

    def body(x_ref, g1_ref, w_ref, bd_ref, gq_ref, gk_ref, bf_ref, tri_ref, pdq_ref,
             pdk_ref, oq_ref, ok_ref,
             h_ref, qa_ref, ka_ref, kat_ref, vs_ref, vt_ref, qr_ref, kr_ref, flog_ref, uv_ref,
             gp_ref, carry):
        @pl.when(pl.program_id(0) == 0)
        def _():
            carry[...] = jnp.zeros_like(carry)

        xf = x_ref[...]
        r = lax.rsqrt(jnp.mean(xf * xf, axis=-1, keepdims=True) + EPS)
        h = (xf * r * g1_ref[...]).astype(BF16)
        h_ref[...] = h
        dot = functools.partial(jnp.dot, preferred_element_type=F32)

        def proj(lo, hi):
            return _dot_nt(h, w_ref[lo:hi, :])

        flog = proj(C_F, C_UV) + bf_ref[...]
        flog_ref[...] = flog
        lane = lax.broadcasted_iota(jnp.int32, flog.shape, 1)
        logf = jnp.minimum(flog, 0.0) - jnp.log(1.0 + jnp.exp(-jnp.abs(flog)))
        logf = jnp.where(lane < HEADS, logf, 0.0)
        dcum = _tri_dot(tri_ref[...], logf) + carry[...]
        carry[...] = dcum[tm - 1:tm, :]
        d2 = dcum * LOG2E
        d2a = d2.astype(BF16)
        rem = d2 - d2a.astype(F32)
        d2b = rem.astype(BF16)
        d2c = (rem - d2b.astype(F32)).astype(BF16)

        q = proj(C_Q, C_K)
        qr_ref[...] = q.astype(BF16)
        rq = lax.rsqrt(_seg_mean(q * q, bd_ref) + EPS)
        qn = q * rq * (gq_ref[...] * (HEAD_DIM ** -0.5 * LOG2E))
        qa = (_slabs_from_heads(qn) + dot(d2a, pdq_ref[0]) + dot(d2b, pdq_ref[1])
              + dot(d2c, pdq_ref[2]) + oq_ref[...])
        qa_ref[...] = qa.astype(BF16)

        k = proj(C_K, C_V)
        kr_ref[...] = k.astype(BF16)
        rk = lax.rsqrt(_seg_mean(k * k, bd_ref) + EPS)
        kn = k * rk * gk_ref[...]
        ka = (_slabs_from_heads(kn) - dot(d2a, pdk_ref[0]) - dot(d2b, pdk_ref[1])
              - dot(d2c, pdk_ref[2]) + ok_ref[...])
        ka_ref[...] = ka.astype(BF16)
        kat_ref[0] = ka.T.astype(BF16)

        v = proj(C_V, C_F)
        vs_ref[...] = _slabs_from_heads(v).astype(BF16)
        vt_ref[0] = v.T.astype(BF16)
        uv_ref[...] = proj(C_UV, C_G).astype(BF16)
        gp_ref[...] = proj(C_G, C_END).astype(BF16)

    outs = [((s_len, D_MODEL), BF16, _row_spec(tm, D_MODEL)),
            ((s_len, SLAB_W), BF16, _row_spec(tm, SLAB_W)),
            ((s_len, SLAB_W), BF16, _row_spec(tm, SLAB_W)),
            ((nt, SLAB_W, tm), BF16, _tile_spec(SLAB_W, tm)),
            ((s_len, SLAB_W), BF16, _row_spec(tm, SLAB_W)),
            ((nt, FOX_W, tm), BF16, _tile_spec(FOX_W, tm)),
            ((s_len, FOX_W), BF16, _row_spec(tm, FOX_W)),
            ((s_len, FOX_W), BF16, _row_spec(tm, FOX_W)),
            ((s_len, 128), F32, _row_spec(tm, 128)),
            ((s_len, 2 * SGU_W), BF16, _row_spec(tm, 2 * SGU_W)),
            ((s_len, 2 * D_MODEL), BF16, _row_spec(tm, 2 * D_MODEL))]
    return pl.pallas_call(
        body, name="proj_fwd", grid=(nt,),
        in_specs=[_row_spec(tm, D_MODEL), _const_spec((1, D_MODEL)), _const_spec(wcat.shape),
                  _const_spec(bdiag.shape), _const_spec((1, FOX_W)), _const_spec((1, FOX_W)),
                  _const_spec((1, 128)), _const_spec((tm, tm)), _const_spec(pdq.shape), _const_spec(pdk.shape), _const_spec(ones_q.shape),
                  _const_spec(ones_k.shape)],
        out_specs=[o[2] for o in outs],
        out_shape=[jax.ShapeDtypeStruct(o[0], o[1]) for o in outs],
        scratch_shapes=[pltpu.VMEM((1, 128), F32)],
        compiler_params=_params(56, 1),
    )(x, g1, wcat, bdiag, gq, gk, bfor, tri, pdq, pdk, ones_q, ones_k)


def _attn_fwd(qa, ka, vt):
    s_len = qa.shape[0]
    t = ATTN_TILE
    nb = s_len // t

    def body(q_ref, k_ref, vt_ref, o_ref, ot_ref, lse_ref, m_sc, l_sc, acc_sc, s_sc, alpha_sc):
        i = pl.program_id(0)
        m_sc[...] = jnp.full_like(m_sc, -jnp.inf)
        l_sc[...] = jnp.zeros_like(l_sc)
        acc_sc[...] = jnp.zeros_like(acc_sc)

        def tile(j, masked):
            krows = pl.ds(pl.multiple_of(j * t, t), t)
            if masked:
                keep = (lax.broadcasted_iota(jnp.int32, (t, t), 0)
                        <= lax.broadcasted_iota(jnp.int32, (t, t), 1))
            for hd in range(HEADS):
                sl = slice(hd * 128, (hd + 1) * 128)
                st = _dot_nt(k_ref[krows, sl], q_ref[:, sl])
                if masked:
                    st = jnp.where(keep, st, -jnp.inf)
                s_sc[hd] = st
                m_prev = m_sc[hd:hd + 1, :]
                m_new = jnp.maximum(m_prev, jnp.max(st, axis=0, keepdims=True))
                alpha_sc[hd:hd + 1, :] = jnp.exp2(m_prev - m_new)
                m_sc[hd:hd + 1, :] = m_new
            for hd in range(HEADS):
                hr = slice(hd * HEAD_DIM, (hd + 1) * HEAD_DIM)
                alpha = alpha_sc[hd:hd + 1, :]
                pt = jnp.exp2(s_sc[hd] - m_sc[hd:hd + 1, :])
                l_sc[hd:hd + 1, :] = alpha * l_sc[hd:hd + 1, :] + jnp.sum(pt, axis=0, keepdims=True)
                acc_sc[hr, :] = alpha * acc_sc[hr, :] + jnp.dot(
                    vt_ref[j, hr, :], pt.astype(BF16), preferred_element_type=F32)

        def off_diagonal(j, carry):
            tile(j, False)
            return carry

        lax.fori_loop(0, i, off_diagonal, 0)
        tile(i, True)

        for hd in range(HEADS):
            hr = slice(hd * HEAD_DIM, (hd + 1) * HEAD_DIM)
            l = l_sc[hd:hd + 1, :]
            acc_sc[hr, :] = acc_sc[hr, :] / l
            lse_ref[0, hd:hd + 1, :] = m_sc[hd:hd + 1, :] + jnp.log2(l)
        o_ref[...] = acc_sc[...].T.astype(BF16)
        ot_ref[...] = acc_sc[...].astype(BF16)

    return pl.pallas_call(
        body, name="attn_fwd", grid=(nb,),
        in_specs=[_row_spec(t, SLAB_W), _const_spec(ka.shape), _const_spec(vt.shape)],
        out_specs=[_row_spec(t, FOX_W), pl.BlockSpec((FOX_W, t), lambda i: (0, i)),
                   _tile_spec(HEADS, t)],
        out_shape=[jax.ShapeDtypeStruct((s_len, FOX_W), BF16),
                   jax.ShapeDtypeStruct((FOX_W, s_len), BF16),
                   jax.ShapeDtypeStruct((nb, HEADS, t), F32)],
        scratch_shapes=[pltpu.VMEM((HEADS, t), F32), pltpu.VMEM((HEADS, t), F32),
                        pltpu.VMEM((FOX_W, t), F32), pltpu.VMEM((HEADS, t, t), F32),
                        pltpu.VMEM((HEADS, t), F32)],
        compiler_params=_params(48, 1),
    )(qa, ka, vt)


def _sgu_mix(vn, ws_ref):
    tm = vn.shape[0]
    lane = lax.broadcasted_iota(jnp.int32, (WINDOW, 128), 1)
    low = lane < HEAD_DIM
    wins = []
    for w in range(tm // WINDOW):
        slabs = []
        for p in range(GROUPS // 2):
            v2 = vn[w * WINDOW:(w + 1) * WINDOW, p * 128:(p + 1) * 128]
            lo = jnp.where(low, v2, 0.0).astype(BF16)
            hi = jnp.where(low, 0.0, v2).astype(BF16)
            slabs.append(jnp.dot(ws_ref[2 * p], lo, preferred_element_type=F32)
                         + jnp.dot(ws_ref[2 * p + 1], hi, preferred_element_type=F32))
        wins.append(jnp.concatenate(slabs, axis=1))
    return jnp.concatenate(wins, axis=0) if len(wins) > 1 else wins[0]


def _layernorm_fwd(vv, g, b):
    mu = jnp.mean(vv, axis=-1, keepdims=True)
    xc = vv - mu
    r = lax.rsqrt(jnp.mean(xc * xc, axis=-1, keepdims=True) + EPS)
    xh = xc * r
    return xh * g + b, xh, r


def _mix_fwd(attn, uvpre, gpre, x, wa, wb, wout, wsm, bsf, gsgu, bsgu, gpost):
    s_len = x.shape[0]
    tm = TOKEN_TILE

    def body(o_ref, uv_ref, gp_ref, x_ref, wa_ref, wb_ref, wo_ref, ws_ref, bs_ref, gs_ref, bsg_ref,
             gpost_ref, sgut_ref, ya_ref, yb_ref, mgt_ref, om_ref, x1_ref):
        uvp = uv_ref[...].astype(F32)
        uv, _ = _gelu_and_grad(uvp)
        u, vv = uv[:, :SGU_W], uv[:, SGU_W:]
        vn, _, _ = _layernorm_fwd(vv, gs_ref[...], bsg_ref[...])
        bias = bs_ref[...]
        if tm > WINDOW:
            bias = jnp.concatenate([bias] * (tm // WINDOW), axis=0)
        mixed = _sgu_mix(vn, ws_ref) + bias
        sgu_f = u * mixed
        sgu = sgu_f.astype(BF16)
        sgut_ref[...] = sgu_f.T.astype(BF16)
        ya = jnp.dot(o_ref[...], wa_ref[...], preferred_element_type=F32)
        yb = jnp.dot(sgu, wb_ref[...], preferred_element_type=F32)
        ya_ref[...] = ya.astype(BF16)
        yb_ref[...] = yb.astype(BF16)
        gates = _sigmoid(gp_ref[...].astype(F32))
        merged_f = gates[:, :D_MODEL] * ya + gates[:, D_MODEL:] * yb
        merged = merged_f.astype(BF16)
        mgt_ref[...] = merged_f.T.astype(BF16)
        om = jnp.dot(merged, wo_ref[...], preferred_element_type=F32)
        om_ref[...] = om
        r = lax.rsqrt(jnp.mean(om * om, axis=-1, keepdims=True) + EPS)
        x1_ref[...] = x_ref[...] + om * r * gpost_ref[...]

    def t_out(rows):
        return ((rows, s_len), BF16, pl.BlockSpec((rows, tm), lambda i: (0, i)))

    def r_out(cols, dt):
        return ((s_len, cols), dt, _row_spec(tm, cols))

    outs = [t_out(SGU_W), r_out(D_MODEL, BF16), r_out(D_MODEL, BF16), t_out(D_MODEL),
            r_out(D_MODEL, F32), r_out(D_MODEL, F32)]
    return pl.pallas_call(
        body, name="mix_fwd", grid=(s_len // tm,),
        in_specs=[_row_spec(tm, FOX_W), _row_spec(tm, 2 * SGU_W), _row_spec(tm, 2 * D_MODEL),
                  _row_spec(tm, D_MODEL), _const_spec(wa.shape), _const_spec(wb.shape),
                  _const_spec(wout.shape), _const_spec(wsm.shape), _const_spec(bsf.shape),
                  _const_spec((1, SGU_W)), _const_spec((1, SGU_W)), _const_spec((1, D_MODEL))],
        out_specs=[o[2] for o in outs],
        out_shape=[jax.ShapeDtypeStruct(o[0], o[1]) for o in outs],
        compiler_params=_params(48, 1),
    )(attn, uvpre, gpre, x, wa, wb, wout, wsm, bsf, gsgu, bsgu, gpost)


def _ffn_fwd_bwd(x1, tgt, wffn, wdown, gpre, gpost):
    s_len = x1.shape[0]
    tm = TOKEN_TILE

    def body(x1_ref, t_ref, wi_ref, wd_ref, gpre_ref, gpost_ref,
             dx1_ref, h2_ref, actt_ref, dff_ref, dgut_ref, loss_ref, dgpost_ref, dgpre_ref):
        @pl.when(pl.program_id(0) == 0)
        def _():
            loss_ref[...] = jnp.zeros_like(loss_ref)
            dgpost_ref[...] = jnp.zeros_like(dgpost_ref)
            dgpre_ref[...] = jnp.zeros_like(dgpre_ref)

        x1v = x1_ref[...]
        r2 = lax.rsqrt(jnp.mean(x1v * x1v, axis=-1, keepdims=True) + EPS)
        gpre_v = gpre_ref[...]
        h2 = (x1v * r2 * gpre_v).astype(BF16)
        h2_ref[...] = h2
        gg = _dot_nt(h2, wi_ref[:D_FF, :])
        uu = _dot_nt(h2, wi_ref[D_FF:, :])
        sg = _sigmoid(gg)
        silu = gg * sg
        act_f = silu * uu
        act = act_f.astype(BF16)
        actt_ref[...] = act_f.T.astype(BF16)
        ff = jnp.dot(act, wd_ref[...], preferred_element_type=F32)
        r3 = lax.rsqrt(jnp.mean(ff * ff, axis=-1, keepdims=True) + EPS)
        gpost_v = gpost_ref[...]
        y = x1v + ff * r3 * gpost_v
        err = y - t_ref[...]
        loss_ref[...] += jnp.sum(err * err) * (0.5 / D_MODEL)
        dy = err * (1.0 / D_MODEL)
        dgpost_ref[...] += jnp.sum(dy * ff * r3, axis=0, keepdims=True)
        dff = _rms_bwd(ff, r3, gpost_v, dy).astype(BF16)
        dff_ref[...] = dff
        dact = _dot_nt(dff, wd_ref[...])
        dgg_f = dact * uu * (sg * (1.0 + gg * (1.0 - sg)))
        duu_f = dact * silu
        dgg = dgg_f.astype(BF16)
        duu = duu_f.astype(BF16)
        dgut_ref[:D_FF, :] = dgg_f.T.astype(BF16)
        dgut_ref[D_FF:, :] = duu_f.T.astype(BF16)
        dh2 = (jnp.dot(dgg, wi_ref[:D_FF, :], preferred_element_type=F32)
               + jnp.dot(duu, wi_ref[D_FF:, :], preferred_element_type=F32))
        dgpre_ref[...] += jnp.sum(dh2 * x1v * r2, axis=0, keepdims=True)
        dx1_ref[...] = dy + _rms_bwd(x1v, r2, gpre_v, dh2)

    outs = [((s_len, D_MODEL), F32, _row_spec(tm, D_MODEL)),
            ((s_len, D_MODEL), BF16, _row_spec(tm, D_MODEL)),
            ((D_FF, s_len), BF16, pl.BlockSpec((D_FF, tm), lambda i: (0, i))),
            ((s_len, D_MODEL), BF16, _row_spec(tm, D_MODEL)),
            ((2 * D_FF, s_len), BF16, pl.BlockSpec((2 * D_FF, tm), lambda i: (0, i))),
            ((1, 128), F32, _const_spec((1, 128))),
            ((1, D_MODEL), F32, _const_spec((1, D_MODEL))),
            ((1, D_MODEL), F32, _const_spec((1, D_MODEL)))]
    return pl.pallas_call(
        body, name="ffn_fwd_bwd", grid=(s_len // tm,),
        in_specs=[_row_spec(tm, D_MODEL), _row_spec(tm, D_MODEL), _const_spec(wffn.shape),
                  _const_spec(wdown.shape), _const_spec((1, D_MODEL)), _const_spec((1, D_MODEL))],
        out_specs=[o[2] for o in outs],
        out_shape=[jax.ShapeDtypeStruct(o[0], o[1]) for o in outs],
        compiler_params=_params(60, 1),
    )(x1, tgt, wffn, wdown, gpre, gpost)


def _mix_bwd(dx1, om, ya, yb, gpre, uvpre, attn, wout, wa, wb, wsm, wsmt, bsf, gsgu, bsgu, gpost,
             wmask, egrp):
    s_len = dx1.shape[0]
    tm = TOKEN_TILE
    nw = tm // WINDOW
    nt = s_len // tm

    def body(dx1_ref, om_ref, ya_ref, yb_ref, gp_ref, uv_ref, o_ref, wo_ref, wa_ref, wb_ref, ws_ref,
             wst_ref, bs_ref, gs_ref, bsg_ref, gpost_ref, mask_ref, eg_ref,
             dom_ref, dya_ref, dyb_ref, dgp_ref, dot_ref, delta_ref, duv_ref,
             dws_ref, dbs_ref, dgs_ref, dbsg_ref, dgpost_ref, dbs_acc):
        step = pl.program_id(0)

        @pl.when(step == 0)
        def _():
            dws_ref[...] = jnp.zeros_like(dws_ref)
            dbs_acc[...] = jnp.zeros_like(dbs_acc)
            dgs_ref[...] = jnp.zeros_like(dgs_ref)
            dbsg_ref[...] = jnp.zeros_like(dbsg_ref)
            dgpost_ref[...] = jnp.zeros_like(dgpost_ref)

        om = om_ref[...]
        dx1v = dx1_ref[...]
        r = lax.rsqrt(jnp.mean(om * om, axis=-1, keepdims=True) + EPS)
        gpost_v = gpost_ref[...]
        dgpost_ref[...] += jnp.sum(dx1v * om * r, axis=0, keepdims=True)
        dom = _rms_bwd(om, r, gpost_v, dx1v).astype(BF16)
        dom_ref[...] = dom
        dmg = _dot_nt(dom, wo_ref[...])

        gates = _sigmoid(gp_ref[...].astype(F32))
        ga, gb = gates[:, :D_MODEL], gates[:, D_MODEL:]
        yav, ybv = ya_ref[...].astype(F32), yb_ref[...].astype(F32)
        dya = (dmg * ga).astype(BF16)
        dyb = (dmg * gb).astype(BF16)
        dya_ref[...] = dya
        dyb_ref[...] = dyb
        dgp_ref[:, :D_MODEL] = (dmg * yav * ga * (1.0 - ga)).astype(BF16)
        dgp_ref[:, D_MODEL:] = (dmg * ybv * gb * (1.0 - gb)).astype(BF16)

        dat_t = _dot_nt(dya, wa_ref[...]).T.astype(BF16)
        dot_ref[0] = dat_t
        o_t = o_ref[...].astype(F32).T
        delta_ref[0] = jnp.sum((dat_t.astype(F32) * o_t).reshape(HEADS, HEAD_DIM, tm), axis=1)
        dsgu = _dot_nt(dyb, wb_ref[...])

        uvp = uv_ref[...].astype(F32)
        uv, guv = _gelu_and_grad(uvp)
        u, vv = uv[:, :SGU_W], uv[:, SGU_W:]
        gs_v = gs_ref[...]
        vn, xh, rln = _layernorm_fwd(vv, gs_v, bsg_ref[...])
        bias = bs_ref[...]
        if nw > 1:
            bias = jnp.concatenate([bias] * nw, axis=0)
        mixed = _sgu_mix(vn, ws_ref) + bias
        du = dsgu * mixed
        dmixed = dsgu * u

        lane = lax.broadcasted_iota(jnp.int32, (WINDOW, 128), 1)
        low = lane < HEAD_DIM
        dvn_wins = []
        for w in range(nw):
            rows = slice(w * WINDOW, (w + 1) * WINDOW)
            dbs_acc[...] += dmixed[rows, :]
            slabs = []
            for p in range(GROUPS // 2):
                cols = slice(p * 128, (p + 1) * 128)
                dm2 = dmixed[rows, cols]
                dlo = jnp.where(low, dm2, 0.0).astype(BF16)
                dhi = jnp.where(low, 0.0, dm2).astype(BF16)
                vn2 = vn[rows, cols].astype(BF16)
                dws_ref[2 * p] += _dot_nt(dlo, vn2)
                dws_ref[2 * p + 1] += _dot_nt(dhi, vn2)
                slabs.append(jnp.dot(wst_ref[2 * p], dlo, preferred_element_type=F32)
                             + jnp.dot(wst_ref[2 * p + 1], dhi, preferred_element_type=F32))
            dvn_wins.append(jnp.concatenate(slabs, axis=1))
        dvn = jnp.concatenate(dvn_wins, axis=0) if nw > 1 else dvn_wins[0]

        dgs_ref[...] += jnp.sum(dvn * xh, axis=0, keepdims=True)
        dbsg_ref[...] += jnp.sum(dvn, axis=0, keepdims=True)
        dxh = dvn * gs_v
        dvv = rln * (dxh - jnp.mean(dxh, axis=-1, keepdims=True)
                     - xh * jnp.mean(dxh * xh, axis=-1, keepdims=True))
        duv_ref[:, :SGU_W] = (du * guv[:, :SGU_W]).astype(BF16)
        duv_ref[:, SGU_W:] = (dvv * guv[:, SGU_W:]).astype(BF16)

        @pl.when(step == pl.num_programs(0) - 1)
        def _():
            for g in range(GROUPS):
                dws_ref[g] = dws_ref[g] * mask_ref[...]
            dbs_ref[...] = _split3_dot(dbs_acc[...], eg_ref[...])

    rows_out = [((s_len, D_MODEL), BF16, _row_spec(tm, D_MODEL)),
                ((s_len, D_MODEL), BF16, _row_spec(tm, D_MODEL)),
                ((s_len, D_MODEL), BF16, _row_spec(tm, D_MODEL)),
                ((s_len, 2 * D_MODEL), BF16, _row_spec(tm, 2 * D_MODEL)),
                ((nt, FOX_W, tm), BF16, _tile_spec(FOX_W, tm)),
                ((nt, HEADS, tm), F32, _tile_spec(HEADS, tm)),
                ((s_len, 2 * SGU_W), BF16, _row_spec(tm, 2 * SGU_W))]
    acc_out = [((GROUPS, WINDOW, WINDOW), F32), ((WINDOW, 128), F32), ((1, SGU_W), F32),
               ((1, SGU_W), F32), ((1, D_MODEL), F32)]
    return pl.pallas_call(
        body, name="mix_bwd", grid=(nt,),
        in_specs=[_row_spec(tm, D_MODEL), _row_spec(tm, D_MODEL), _row_spec(tm, D_MODEL),
                  _row_spec(tm, D_MODEL), _row_spec(tm, 2 * D_MODEL), _row_spec(tm, 2 * SGU_W),
                  _row_spec(tm, FOX_W), _const_spec(wout.shape), _const_spec(wa.shape),
                  _const_spec(wb.shape), _const_spec(wsm.shape), _const_spec(wsmt.shape),
                  _const_spec(bsf.shape), _const_spec((1, SGU_W)), _const_spec((1, SGU_W)),
                  _const_spec((1, D_MODEL)), _const_spec(wmask.shape), _const_spec(egrp.shape)],
        out_specs=[o[2] for o in rows_out] + [_const_spec(s) for s, _ in acc_out],
        out_shape=[jax.ShapeDtypeStruct(o[0], o[1]) for o in rows_out]
        + [jax.ShapeDtypeStruct(s, dt) for s, dt in acc_out],
        scratch_shapes=[pltpu.VMEM((WINDOW, SGU_W), F32)],
        compiler_params=_params(48, 1),
    )(dx1, om, ya, yb, gpre, uvpre, attn, wout, wa, wb, wsm, wsmt, bsf, gsgu, bsgu, gpost, wmask,
      egrp)


def _attn_bwd(qa, ka, kat, vs, dot_, lse, delta, ecol):
    s_len = qa.shape[0]
    t = ATTN_TILE
    nb = s_len // t

    def body(k_ref, kt_ref, vs_ref, q_ref, do_ref, lse_ref, dl_ref, ec_ref, gk_ref, dvt_ref,
             gqt_ref, csum_ref, p_sc, ds_sc):
        j = pl.program_id(0)

        @pl.when(j == 0)
        def _():
            gqt_ref[...] = jnp.zeros_like(gqt_ref)

        gk_ref[...] = jnp.zeros_like(gk_ref)
        dvt_ref[...] = jnp.zeros_like(dvt_ref)

        def tile(i, masked):
            qrows = pl.ds(pl.multiple_of(i * t, t), t)
            if masked:
                keep = (lax.broadcasted_iota(jnp.int32, (t, t), 0)
                        <= lax.broadcasted_iota(jnp.int32, (t, t), 1))
            for hd in range(HEADS):
                sl = slice(hd * 128, (hd + 1) * 128)
                hr = slice(hd * HEAD_DIM, (hd + 1) * HEAD_DIM)
                st = _dot_nt(k_ref[:, sl], q_ref[qrows, sl])
                if masked:
                    st = jnp.where(keep, st, -jnp.inf)
                pt = jnp.exp2(st - lse_ref[i, hd:hd + 1, :])
                dpt = jnp.dot(vs_ref[:, hd * 128:hd * 128 + HEAD_DIM], do_ref[i, hr, :],
                              preferred_element_type=F32)
                p_sc[hd] = pt.astype(BF16)
                ds_sc[hd] = (pt * (dpt - dl_ref[i, hd:hd + 1, :])).astype(BF16)
            for hd in range(HEADS):
                sl = slice(hd * 128, (hd + 1) * 128)
                hr = slice(hd * HEAD_DIM, (hd + 1) * HEAD_DIM)
                dst = ds_sc[hd]
                dvt_ref[0, hr, :] += _dot_nt(do_ref[i, hr, :], p_sc[hd])
                gk_ref[:, sl] += jnp.dot(dst, q_ref[qrows, sl], preferred_element_type=F32)
                gqt_ref[i, hd * QT_ROWS:(hd + 1) * QT_ROWS, :] += jnp.dot(
                    kt_ref[0, hd * 128:hd * 128 + QT_ROWS, :], dst, preferred_element_type=F32)

        tile(j, True)

        def below_diagonal(i, carry):
            tile(i, False)
            return carry

        lax.fori_loop(j + 1, nb, below_diagonal, 0)
        csum_ref[...] = _split3_dot(gk_ref[...], ec_ref[...])

    return pl.pallas_call(
        body, name="attn_bwd", grid=(nb,),
        in_specs=[_row_spec(t, SLAB_W), _tile_spec(SLAB_W, t), _row_spec(t, SLAB_W),
                  _const_spec(qa.shape), _const_spec(dot_.shape), _const_spec(lse.shape),
                  _const_spec(delta.shape), _const_spec(ecol.shape)],
        out_specs=[_row_spec(t, SLAB_W), _tile_spec(FOX_W, t),
                   _const_spec((nb, HEADS * QT_ROWS, t)), _row_spec(t, 128)],
        out_shape=[jax.ShapeDtypeStruct((s_len, SLAB_W), F32),
                   jax.ShapeDtypeStruct((nb, FOX_W, t), F32),
                   jax.ShapeDtypeStruct((nb, HEADS * QT_ROWS, t), F32),
                   jax.ShapeDtypeStruct((s_len, 128), F32)],
        scratch_shapes=[pltpu.VMEM((HEADS, t, t), BF16), pltpu.VMEM((HEADS, t, t), BF16)],
        compiler_params=_params(60, 1),
    )(ka, kat, vs, qa, dot_, lse, delta, ecol)


def _rev_cumsum(col_sums, gqt, triu):
    s_len = col_sums.shape[0]
    tm = TOKEN_TILE
    n = s_len // tm

    def body(cs_ref, gqt_ref, tri_ref, o_ref, carry):
        @pl.when(pl.program_id(0) == 0)
        def _():
            carry[...] = jnp.zeros_like(carry)
        rows = [gqt_ref[0, hd * QT_ROWS + HEAD_DIM:hd * QT_ROWS + HEAD_DIM + 1, :]
                for hd in range(HEADS)]
        row_sums = jnp.concatenate(rows + [jnp.zeros((128 - HEADS, tm), F32)], axis=0).T
        out = _tri_dot(tri_ref[...], row_sums - cs_ref[...]) + carry[...]
        o_ref[...] = out
        carry[...] = out[0:1, :]

    return pl.pallas_call(
        body, name="rev_cumsum", grid=(n,),
        in_specs=[pl.BlockSpec((tm, 128), lambda i: (n - 1 - i, 0)),
                  pl.BlockSpec((1, HEADS * QT_ROWS, tm), lambda i: (n - 1 - i, 0, 0)),
                  _const_spec((tm, tm))],
        out_specs=pl.BlockSpec((tm, 128), lambda i: (n - 1 - i, 0)),
        out_shape=jax.ShapeDtypeStruct((s_len, 128), F32),
        scratch_shapes=[pltpu.VMEM((1, 128), F32)],
        compiler_params=_params(32, 1),
    )(col_sums, gqt, triu)


def _heads_from_slabs(slabs):
    lane = lax.broadcasted_iota(jnp.int32, slabs[0].shape, 1)
    low = lane < HEAD_DIM
    pairs = [jnp.where(low, slabs[2 * p], pltpu.roll(slabs[2 * p + 1], HEAD_DIM, 1))
             for p in range(HEADS // 2)]
    return jnp.concatenate(pairs, axis=1)


def _proj_bwd(gqt, gk, dvt, dlogf, flog, qraw, kraw, duv, dgp, x, dx1, wcat, bdiag, gq, gk_gain, g1,
              efold):
    s_len = x.shape[0]
    tm = TOKEN_TILE

    def body(gqt_ref, gkk_ref, dvt_ref, dlf_ref, flog_ref, qr_ref, kr_ref, duv_ref, dgp_ref, x_ref,
             dx1_ref, w_ref, bd_ref, gq_ref, gk_ref, g1_ref, ef_ref,
             dx_ref, dprojt_ref, dgq_ref, dgk_ref, dbf_ref, dg1_ref, gq_acc, gk_acc, dproj_ref):
        step = pl.program_id(0)

        @pl.when(step == 0)
        def _():
            gq_acc[...] = jnp.zeros_like(gq_acc)
            gk_acc[...] = jnp.zeros_like(gk_acc)
            dbf_ref[...] = jnp.zeros_like(dbf_ref)
            dg1_ref[...] = jnp.zeros_like(dg1_ref)

        pad = jnp.zeros((128 - QT_ROWS, tm), F32)
        q_slabs = [jnp.concatenate([gqt_ref[0, hd * QT_ROWS:(hd + 1) * QT_ROWS, :], pad], axis=0).T
                   for hd in range(HEADS)]
        dqn = _heads_from_slabs(q_slabs)
        dkn = _heads_from_slabs([gkk_ref[:, hd * 128:(hd + 1) * 128] for hd in range(HEADS)])

        def head_bwd(raw_ref, dn, g_ref, acc):
            raw = raw_ref[...].astype(F32)
            r = lax.rsqrt(_seg_mean(raw * raw, bd_ref) + EPS)
            xhat = raw * r
            acc[0:1, :] += jnp.sum(dn * xhat, axis=0, keepdims=True)
            dyg = dn * g_ref[...]
            return r * (dyg - xhat * _seg_mean(dyg * xhat, bd_ref))

        dproj_ref[:, C_Q:C_K] = head_bwd(qr_ref, dqn * HEAD_DIM ** -0.5, gq_ref, gq_acc).astype(BF16)
        dproj_ref[:, C_K:C_V] = head_bwd(kr_ref, dkn * LN2, gk_ref, gk_acc).astype(BF16)
        dproj_ref[:, C_V:C_F] = dvt_ref[0].T.astype(BF16)
        dfl = dlf_ref[...] * _sigmoid(-flog_ref[...])
        dbf_ref[...] += jnp.sum(dfl, axis=0, keepdims=True)
        dproj_ref[:, C_F:C_UV] = dfl.astype(BF16)
        dproj_ref[:, C_UV:C_G] = duv_ref[...]
        dproj_ref[:, C_G:C_END] = dgp_ref[...]

        dproj = dproj_ref[...]
        dprojt_ref[...] = dproj.astype(F32).T.astype(BF16)
        dh = jnp.dot(dproj, w_ref[...], preferred_element_type=F32)
        xf = x_ref[...]
        r = lax.rsqrt(jnp.mean(xf * xf, axis=-1, keepdims=True) + EPS)
        dg1_ref[...] += jnp.sum(dh * xf * r, axis=0, keepdims=True)
        dx_ref[...] = dx1_ref[...] + _rms_bwd(xf, r, g1_ref[...], dh)

        @pl.when(step == pl.num_programs(0) - 1)
        def _():
            dgq_ref[...] = _split3_dot(gq_acc[...], ef_ref[...])
            dgk_ref[...] = _split3_dot(gk_acc[...], ef_ref[...])

    outs = [((s_len, D_MODEL), F32, _row_spec(tm, D_MODEL)),
            ((C_END, s_len), BF16, pl.BlockSpec((C_END, tm), lambda i: (0, i))),
            ((8, 128), F32, _const_spec((8, 128))),
            ((8, 128), F32, _const_spec((8, 128))),
            ((1, 128), F32, _const_spec((1, 128))),
            ((1, D_MODEL), F32, _const_spec((1, D_MODEL)))]
    return pl.pallas_call(
        body, name="proj_bwd", grid=(s_len // tm,),
        in_specs=[_tile_spec(HEADS * QT_ROWS, tm), _row_spec(tm, SLAB_W), _tile_spec(FOX_W, tm),
                  _row_spec(tm, 128), _row_spec(tm, 128), _row_spec(tm, FOX_W),
                  _row_spec(tm, FOX_W), _row_spec(tm, 2 * SGU_W), _row_spec(tm, 2 * D_MODEL),
                  _row_spec(tm, D_MODEL), _row_spec(tm, D_MODEL), _const_spec(wcat.shape),
                  _const_spec(bdiag.shape), _const_spec((1, FOX_W)), _const_spec((1, FOX_W)),
                  _const_spec((1, D_MODEL)), _const_spec(efold.shape)],
        out_specs=[o[2] for o in outs],
        out_shape=[jax.ShapeDtypeStruct(o[0], o[1]) for o in outs],
        scratch_shapes=[pltpu.VMEM((8, FOX_W), F32), pltpu.VMEM((8, FOX_W), F32),
                        pltpu.VMEM((tm, C_END), BF16)],
        compiler_params=_params(56, 1),
    )(gqt, gk, dvt, dlogf, flog, qraw, kraw, duv, dgp, x, dx1, wcat, bdiag, gq, gk_gain, g1, efold)


def _dw_matmul(at, b, tm, name):
    m, s_len = at.shape
    n = b.shape[1]

    def body(a_ref, b_ref, o_ref):
        o_ref[...] = jnp.dot(a_ref[...], b_ref[...], preferred_element_type=F32).astype(BF16)

    return pl.pallas_call(
        body, name=name, grid=(m // tm,),
        in_specs=[pl.BlockSpec((tm, s_len), lambda i: (i, 0)), _const_spec(b.shape)],
        out_specs=pl.BlockSpec((tm, n), lambda i: (i, 0)),
        out_shape=jax.ShapeDtypeStruct((m, n), BF16),
        compiler_params=_params(48, 1),
    )(at, b)


def _adamw(parts, w, m, v, tr, name, col_tile=None):
    parts = parts if isinstance(parts, (list, tuple)) else [parts]
    rows, cols = w.shape
    bc1 = 1.0 - ADAM_B1 ** ADAM_STEP
    bc2 = 1.0 - ADAM_B2 ** ADAM_STEP

    def body(*refs):
        p_refs = refs[:len(parts)]
        w_ref, m_ref, v_ref, g_ref, d_ref, mo_ref, vo_ref = refs[len(parts):]
        g = None
        for p_ref, p in zip(p_refs, parts):
            for idx in range(p.shape[0]):
                term = p_ref[idx].astype(F32)
                g = term if g is None else g + term
        g_ref[...] = g
        mn = ADAM_B1 * m_ref[...] + (1.0 - ADAM_B1) * g
        vn = ADAM_B2 * v_ref[...] + (1.0 - ADAM_B2) * (g * g)
        mo_ref[...] = mn
        vo_ref[...] = vn
        m_hat = mn / bc1
        v_hat = vn / bc2
        d_ref[...] = -ADAM_LR * (m_hat / (jnp.sqrt(v_hat) + ADAM_EPS) + ADAM_WD * w_ref[...])

    if col_tile is None:
        spec = pl.BlockSpec((tr, cols), lambda i: (i, 0))
        pspecs = [pl.BlockSpec((p.shape[0], tr, cols), lambda i: (0, i, 0)) for p in parts]
        steps = rows // tr
    else:
        spec = pl.BlockSpec((rows, col_tile), lambda i: (0, i))
        pspecs = [pl.BlockSpec((p.shape[0], rows, col_tile), lambda i: (0, 0, i)) for p in parts]
        steps = cols // col_tile
    return pl.pallas_call(
        body, name=name, grid=(steps,),
        in_specs=pspecs + [spec, spec, spec],
        out_specs=[spec] * 4,
        out_shape=[jax.ShapeDtypeStruct((rows, cols), F32)] * 4,
        compiler_params=_params(48, 1),
    )(*parts, w, m, v)


def _sum_parts(parts, name):
    n, rows, cols = parts.shape

    def body(p_ref, o_ref):
        g = p_ref[0]
        for idx in range(1, n):
            g = g + p_ref[idx]
        o_ref[...] = g

    return pl.pallas_call(
        body, name=name, out_shape=jax.ShapeDtypeStruct((rows, cols), F32),
        in_specs=[_const_spec(parts.shape)], out_specs=_const_spec((rows, cols)), grid=(1,),
        compiler_params=_params(16, 1),
    )(parts)


SMALL_NAMES = ("g_pre_mix", "b_forget", "g_q", "g_k", "g_sgu", "b_sgu", "w_spatial", "b_spatial",
               "g_post_mix", "g_pre_ffn", "g_post_ffn")


def _small_rows(size):
    return -(-size // 1024)


def _pack_small(d, extra=None):
    rows = []
    for k in SMALL_NAMES:
        flat = d[k].reshape(-1).astype(F32)
        nr = _small_rows(flat.shape[0])
        rows.append(jnp.pad(flat, (0, nr * 1024 - flat.shape[0])).reshape(nr, 1024))
    if extra is not None:
        rows.append(extra)
    used = sum(r.shape[0] for r in rows)
    rows.append(jnp.zeros((N_DEV * SMALL_ROWS - used, 1024), F32))
    return jnp.concatenate(rows, axis=0)


def _unpack_small(packed, shapes):
    out, off = {}, 0
    for k in SMALL_NAMES:
        size = math.prod(shapes[k])
        nr = _small_rows(size)
        out[k] = packed[off:off + nr].reshape(-1)[:size].reshape(shapes[k])
        off += nr
    return out


def _cols_to_blocks(full, width):
    r = full.shape[0]
    return jnp.transpose(full.reshape(r, N_DEV, width), (1, 0, 2))


def _blocks_to_cols(blocks):
    n, r, width = blocks.shape
    return jnp.transpose(blocks, (1, 0, 2)).reshape(r, n * width)


def kernel(x, g_pre_mix, w_in, b_forget, g_q, g_k, g_sgu, b_sgu, w_spatial, b_spatial, w_branch_a, w_branch_b, w_out, g_post_mix, g_pre_ffn, w_ffn_in, w_ffn_down, g_post_ffn, loss_target, m_g_pre_mix, m_w_in, m_b_forget, m_g_q, m_g_k, m_g_sgu, m_b_sgu, m_w_spatial, m_b_spatial, m_w_branch_a, m_w_branch_b, m_w_out, m_g_post_mix, m_g_pre_ffn, m_w_ffn_in, m_w_ffn_down, m_g_post_ffn, v_g_pre_mix, v_w_in, v_b_forget, v_g_q, v_g_k, v_g_sgu, v_b_sgu, v_w_spatial, v_b_spatial, v_w_branch_a, v_w_branch_b, v_w_out, v_g_post_mix, v_g_pre_ffn, v_w_ffn_in, v_w_ffn_down, v_g_post_ffn):
    big_names = ("w_in", "w_branch_a", "w_branch_b", "w_out", "w_ffn_in", "w_ffn_down")
    weights = dict(g_pre_mix=g_pre_mix, w_in=w_in, b_forget=b_forget, g_q=g_q, g_k=g_k, g_sgu=g_sgu,
                   b_sgu=b_sgu, w_spatial=w_spatial, b_spatial=b_spatial, w_branch_a=w_branch_a,
                   w_branch_b=w_branch_b, w_out=w_out, g_post_mix=g_post_mix, g_pre_ffn=g_pre_ffn,
                   w_ffn_in=w_ffn_in, w_ffn_down=w_ffn_down, g_post_ffn=g_post_ffn)
    mom1 = dict(g_pre_mix=m_g_pre_mix, w_in=m_w_in, b_forget=m_b_forget, g_q=m_g_q, g_k=m_g_k,
                g_sgu=m_g_sgu, b_sgu=m_b_sgu, w_spatial=m_w_spatial, b_spatial=m_b_spatial,
                w_branch_a=m_w_branch_a, w_branch_b=m_w_branch_b, w_out=m_w_out,
                g_post_mix=m_g_post_mix, g_pre_ffn=m_g_pre_ffn, w_ffn_in=m_w_ffn_in,
                w_ffn_down=m_w_ffn_down, g_post_ffn=m_g_post_ffn)
    mom2 = dict(g_pre_mix=v_g_pre_mix, w_in=v_w_in, b_forget=v_b_forget, g_q=v_g_q, g_k=v_g_k,
                g_sgu=v_g_sgu, b_sgu=v_b_sgu, w_spatial=v_w_spatial, b_spatial=v_b_spatial,
                w_branch_a=v_w_branch_a, w_branch_b=v_w_branch_b, w_out=v_w_out,
                g_post_mix=v_g_post_mix, g_pre_ffn=v_g_pre_ffn, w_ffn_in=v_w_ffn_in,
                w_ffn_down=v_w_ffn_down, g_post_ffn=v_g_post_ffn)
    names = list(weights)
    shapes = {k: weights[k].shape for k in names}

    s_len = x.shape[1]
    xs = x.reshape(s_len, D_MODEL)
    tgt = loss_target.reshape(s_len, D_MODEL)

    transposed = ("w_in", "w_ffn_in")

    def local_view(a, k):
        return jnp.transpose(a[0]) if k in transposed else a[0]

    shards = {k: local_view(weights[k], k).astype(BF16) for k in big_names}
    win_t = _gather_two_level(shards["w_in"], "gather_w_in").reshape(IN_COLS, D_MODEL)
    win_t, later = lax.optimization_barrier(
        (win_t, [shards[k] for k in big_names if k != "w_in"]))
    shards.update(zip([k for k in big_names if k != "w_in"], later))
    (gat_mix, gat_ffn), gat_token = _exchange_start(
        [[shards["w_branch_a"], shards["w_branch_b"], shards["w_out"]],
         [shards["w_ffn_in"], shards["w_ffn_down"]]], "gather_start", gather=True)
    f_off = 3 * FOX_W
    u_off = f_off + HEADS
    wcat = jnp.concatenate([
        win_t[:f_off], jnp.pad(win_t[f_off:u_off], ((0, 128 - HEADS), (0, 0))), win_t[u_off:]],
        axis=0)

    seg = np.arange(FOX_W) // HEAD_DIM
    bdiag = jnp.asarray(seg[:128, None] == seg[None, :128], BF16)
    tm = TOKEN_TILE
    lower = np.arange(tm)[None, :] <= np.arange(tm)[:, None]
    tril = jnp.asarray(lower, BF16)
    triu = jnp.asarray(lower.T, BF16)
    egrp = jnp.asarray(seg[:, None] == np.arange(128)[None, :], BF16)
    efold = jnp.asarray((np.arange(FOX_W) % HEAD_DIM)[:, None] == np.arange(128)[None, :], BF16)
    gq512 = jnp.tile(g_q.reshape(1, HEAD_DIM), (1, HEADS))
    gk512 = jnp.tile(g_k.reshape(1, HEAD_DIM), (1, HEADS))
    bfor = jnp.pad(b_forget.reshape(1, HEADS), ((0, 0), (0, 128 - HEADS)))
    pos = np.arange(WINDOW)
    wmask = (pos[None, :] // CHUNK) <= (pos[:, None] // CHUNK)
    wsm_f = jnp.where(jnp.asarray(wmask)[None], w_spatial[0], 0.0)
    wsm = wsm_f.astype(BF16)
    wsmt = jnp.transpose(wsm_f, (0, 2, 1)).astype(BF16)
    bsf = jnp.repeat(jnp.transpose(b_spatial[0]), HEAD_DIM, axis=1)
    wmask_f = jnp.asarray(wmask, F32)

    col = np.arange(SLAB_W)
    row128 = np.arange(128)

    def d_place(first):
        return jnp.asarray(np.stack([(col[None, :] // 128 == row128[:, None])
                                     & (col[None, :] % 128 == first + a) for a in range(3)]), BF16)

    pdq, pdk = d_place(HEAD_DIM), d_place(HEAD_DIM + 3)
    ones_q = jnp.asarray((col % 128 >= HEAD_DIM + 3) & (col % 128 < HEAD_DIM + 6), F32)[None]
    ones_k = jnp.asarray((col % 128 >= HEAD_DIM) & (col % 128 < HEAD_DIM + 3), F32)[None]
    ecol = jnp.asarray((col[:, None] // 128 == row128[None, :])
                       & (col[:, None] % 128 == HEAD_DIM + 3), BF16)

    (h, qa, ka, kat, vs, vt, qraw, kraw, flog, uvpre, gpre) = _proj_fwd(
        xs, g_pre_mix + gat_token[0:1, 0:1], wcat, bdiag, gq512, gk512, bfor, tril, pdq, pdk,
        ones_q, ones_k)
    attn, attn_t, lse = _attn_fwd(qa, ka, vt)
    (own_a, own_b, own_out), (zone_a, zone_b, zone_out) = _exchange_wait(
        gat_mix, attn, "gather_wait_mix", gather=True)
    wa = _blocks_to_cols(_own_block(zone_a, own_a))
    wb = _blocks_to_cols(_own_block(zone_b, own_b))
    wout = _own_block(zone_out, own_out).reshape(D_MODEL, D_MODEL)
    sgu_t, ya, yb, merged_t, om, x1 = _mix_fwd(attn, uvpre, gpre, xs, wa, wb, wout, wsm, bsf,
                                           g_sgu, b_sgu, g_post_mix)
    (own_ffn, own_down), (zone_ffn, zone_down) = _exchange_wait(
        gat_ffn, x1, "gather_wait_ffn", gather=True)
    wffn = _own_block(zone_ffn, own_ffn).reshape(2 * D_FF, D_MODEL)
    wdown = _own_block(zone_down, own_down).reshape(D_FF, D_MODEL)
    (dx1, h2, act_t, dff, dgu_t, loss_acc, dg_post_ffn, dg_pre_ffn) = _ffn_fwd_bwd(
        x1, tgt, wffn, wdown, g_pre_ffn, g_post_ffn)

    dw_down = _dw_matmul(act_t, dff, D_FF // 4, "dw_down")
    dw_ffn = _dw_matmul(dgu_t, h2, 2 * D_FF // N_DEV, "dw_ffn_in")
    x_pos, y_pos, c_pos = _mesh_pos()
    me = 4 * x_pos + 2 * y_pos + c_pos

    def own_of(parts):
        return [lax.dynamic_index_in_dim(p, me, 0, keepdims=False) for p in parts]

    parts_ffn = [dw_ffn.reshape(N_DEV, 2 * D_FF // N_DEV, D_MODEL),
                 dw_down.reshape(N_DEV, D_FF // N_DEV, D_MODEL)]
    mine_ffn = own_of(parts_ffn)
    (sct_ffn,), sct_ffn_token = _exchange_start([parts_ffn], "scatter_start_ffn", gather=False)

    (dom, dya, dyb, dgp, dot_, delta, duv, dws, dbs, dg_sgu, db_sgu, dg_post_mix) = _mix_bwd(
        dx1, om, ya, yb, gpre, uvpre, attn, wout, wa, wb, wsm, wsmt, bsf, g_sgu, b_sgu,
        g_post_mix + sct_ffn_token[0:1, 0:1], wmask_f, egrp)
    dw_out = _dw_matmul(merged_t, dom, 512, "dw_out")
    dw_a = _dw_matmul(attn_t, dya, 512, "dw_a")
    dw_b = _dw_matmul(sgu_t, dyb, 512, "dw_b")
    parts_mix = [_cols_to_blocks(dw_a, D_MODEL // N_DEV), _cols_to_blocks(dw_b, D_MODEL // N_DEV),
                 dw_out.reshape(N_DEV, D_MODEL // N_DEV, D_MODEL)]
    mine_mix = own_of(parts_mix)
    (sct_mix,), sct_mix_token = _exchange_start([parts_mix], "scatter_start_mix", gather=False)

    gk_all, dvt, gqt, col_sums = _attn_bwd(qa, ka, kat, vs, dot_, lse,
                                           delta + sct_mix_token[0, 0], ecol)
    dlogf = _rev_cumsum(col_sums, gqt, triu)
    dx, dproj_t, dgq, dgk, dbf, dg_pre_mix = _proj_bwd(
        gqt, gk_all, dvt, dlogf, flog, qraw, kraw, duv, dgp, xs, dx1, wcat, bdiag, gq512, gk512,
        g_pre_mix, efold)
    dw_cat = _dw_matmul(dproj_t, h, C_END // N_DEV, "dw_in")
    dw_in = jnp.concatenate([dw_cat[:C_F + HEADS], dw_cat[C_UV:]], axis=0)

    small_local = dict(
        g_pre_mix=dg_pre_mix, b_forget=dbf[:, :HEADS], g_q=dgq[0:1, :HEAD_DIM],
        g_k=dgk[0:1, :HEAD_DIM], g_sgu=dg_sgu, b_sgu=db_sgu, w_spatial=dws,
        b_spatial=jnp.transpose(dbs[:, :GROUPS]), g_post_mix=dg_post_mix, g_pre_ffn=dg_pre_ffn,
        g_post_ffn=dg_post_ffn)
    loss_row = jnp.pad(loss_acc[0:1, 0:1], ((0, 0), (0, 1023)))
    small_parts = _pack_small(small_local, loss_row).reshape(N_DEV, SMALL_ROWS, 1024)

    pair_blocks, own_pair = _pair_sums(dw_in.reshape(N_DEV, BLK, D_MODEL), "pair_sums_in")
    rs_in, rs_token = _chip_exchange_start(pair_blocks, "chip_exchange_start_in")
    (recv_small,) = _exchange([small_parts + rs_token[0, 0]], "scatter_small", gather=False)

    def with_own(zones, own_blocks):
        return [_own_block(z, b) for z, b in zip(zones, own_blocks)]

    recv_ffn, recv_down = with_own(
        _exchange_wait(sct_ffn, recv_small, "scatter_wait_ffn", gather=False)[1], mine_ffn)
    recv_a, recv_b, recv_out = with_own(
        _exchange_wait(sct_mix, recv_ffn, "scatter_wait_mix", gather=False)[1], mine_mix)
    received = [None, recv_a, recv_b, recv_out, recv_ffn, recv_down]

    grads, deltas, new_m, new_v = {}, {}, {}, {}
    row_tiles = {"w_in": None, "w_branch_a": 512, "w_branch_b": 512, "w_out": 128, "w_ffn_in": 176,
                 "w_ffn_down": 352}

    def update(k, parts):
        outs = _adamw(parts, local_view(weights[k], k), local_view(mom1[k], k),
                      local_view(mom2[k], k), row_tiles[k], "adamw_" + k,
                      col_tile=256 if k == "w_in" else None)
        if k in transposed:
            outs = [jnp.transpose(o) for o in outs]
        grads[k], deltas[k], new_m[k], new_v[k] = [o[None] for o in outs]
        return outs[0]

    last = None
    for idx, k in enumerate(big_names):
        if k != "w_in":
            last = update(k, received[idx])

    small_sum = _sum_parts(recv_small, "sum_small")
    (small_all,) = _exchange([small_sum], "gather_small", gather=True)
    small_all = small_all.reshape(1, N_DEV * SMALL_ROWS, 1024)
    sg, sd, sm, sv = _adamw(small_all, _pack_small(weights), _pack_small(mom1), _pack_small(mom2),
                            N_DEV * SMALL_ROWS, "adamw_small")
    for dst, packed in ((grads, sg), (deltas, sd), (new_m, sm), (new_v, sv)):
        dst.update(_unpack_small(packed, shapes))
    arrived = _chip_exchange_wait(rs_in, sg[0:1, 0:1] + last[0:1, 0:1], "chip_exchange_wait_in")
    update("w_in", [own_pair[None], arrived])

    loss = small_all[0, sum(_small_rows(math.prod(shapes[k])) for k in SMALL_NAMES), 0]
    return (loss, dx.reshape(x.shape), *[grads[k] for k in names], *[deltas[k] for k in names],
            *[new_m[k] for k in names], *[new_v[k] for k in names])
```

```python
import functools
import math

import jax
import jax.numpy as jnp
import numpy as np
from jax import lax
from jax.experimental import pallas as pl
from jax.experimental.pallas import tpu as pltpu

F32 = jnp.float32
BF16 = jnp.bfloat16

D_MODEL = 1024
FOX_W = 512
HEADS = 8
HEAD_DIM = 64
SGU_W = 512
GROUPS = 8
WINDOW = 128
CHUNK = 64
D_FF = 2816
IN_COLS = 4616
EPS = 1e-6
N_DEV = 8
LOG2E = 1.4426950408889634
LN2 = 0.6931471805599453

C_Q, C_K, C_V, C_F, C_UV, C_G, C_END = 0, 512, 1024, 1536, 1664, 2688, 4736

ADAM_LR, ADAM_B1, ADAM_B2, ADAM_EPS, ADAM_WD, ADAM_STEP = 0.001, 0.9, 0.999, 1e-08, 0.01, 10

MIB = 1024 * 1024
TOKEN_TILE = 256
ATTN_TILE = 256
SLAB_W = HEADS * 128
QT_ROWS = 72

SMALL_ROWS = 18
BLK = IN_COLS // N_DEV


def _params(vmem_mib, n_axes):
    return pltpu.CompilerParams(
        dimension_semantics=("arbitrary",) * n_axes, vmem_limit_bytes=vmem_mib * MIB)


def _const_spec(shape):
    nd = len(shape)
    return pl.BlockSpec(shape, lambda *_: (0,) * nd)


def _row_spec(tm, cols):
    return pl.BlockSpec((tm, cols), lambda i: (i, 0))


def _tile_spec(rows, tm):
    return pl.BlockSpec((1, rows, tm), lambda i: (i, 0, 0))


def _split3_dot(x, e):
    x1 = x.astype(BF16)
    r1 = x - x1.astype(F32)
    x2 = r1.astype(BF16)
    x3 = (r1 - x2.astype(F32)).astype(BF16)
    dot = functools.partial(jnp.dot, preferred_element_type=F32)
    return dot(x1, e) + dot(x2, e) + dot(x3, e)


def _tri_dot(tri, x):
    x1 = x.astype(BF16)
    r1 = x - x1.astype(F32)
    x2 = r1.astype(BF16)
    x3 = (r1 - x2.astype(F32)).astype(BF16)
    dot = functools.partial(jnp.dot, preferred_element_type=F32)
    return dot(tri, x1) + dot(tri, x2) + dot(tri, x3)


def _seg_mean(sq, bd_ref):
    hi = sq.astype(BF16)
    lo = (sq - hi.astype(F32)).astype(BF16)
    bd = bd_ref[...]
    dot = functools.partial(jnp.dot, preferred_element_type=F32)
    pairs = [dot(hi[:, p * 128:(p + 1) * 128], bd) + dot(lo[:, p * 128:(p + 1) * 128], bd)
             for p in range(HEADS // 2)]
    return jnp.concatenate(pairs, axis=1) * (1.0 / HEAD_DIM)


def _slabs_from_heads(t):
    lane = lax.broadcasted_iota(jnp.int32, (t.shape[0], 128), 1)
    low = lane < HEAD_DIM
    slabs = []
    for p in range(HEADS // 2):
        pair = t[:, p * 128:(p + 1) * 128]
        slabs.append(jnp.where(low, pair, 0.0))
        slabs.append(jnp.where(low, pltpu.roll(pair, HEAD_DIM, 1), 0.0))
    return jnp.concatenate(slabs, axis=1)


def _dot_nt(a, b):
    return lax.dot_general(a, b, (((1,), (1,)), ((), ())), preferred_element_type=F32)


def _dot_tn(a, b):
    return lax.dot_general(a, b, (((0,), (0,)), ((), ())), preferred_element_type=F32)


def _sigmoid(x):
    return 1.0 / (1.0 + jnp.exp(-x))


_GELU_C = math.sqrt(2.0 / math.pi)


def _gelu_and_grad(x):
    inner = _GELU_C * (x + 0.044715 * x * x * x)
    t = jnp.tanh(inner)
    y = 0.5 * x * (1.0 + t)
    dy = 0.5 * (1.0 + t) + 0.5 * x * (1.0 - t * t) * _GELU_C * (1.0 + 3.0 * 0.044715 * x * x)
    return y, dy


def _rms_bwd(xin, r, g, dy):
    dyg = dy * g
    return r * dyg - xin * (r * r * r) * jnp.mean(dyg * xin, axis=-1, keepdims=True)


def _mesh_pos():
    x, y, c = lax.axis_index("x"), lax.axis_index("y"), lax.axis_index("c")
    return x, y, c


def _peer(k):
    x, y, c = _mesh_pos()
    px = (1 - x) if (k >> 2) & 1 else x
    py = (1 - y) if (k >> 1) & 1 else y
    pc = (1 - c) if k & 1 else c
    return (px, py, pc), 4 * px + 2 * py + pc


def _exchange(arrs, name, gather):
    n = len(arrs)
    if gather:
        out_shape = [jax.ShapeDtypeStruct((N_DEV,) + a.shape, a.dtype) for a in arrs]
    else:
        out_shape = [jax.ShapeDtypeStruct(a.shape, a.dtype) for a in arrs]

    def body(*refs):
        ins, outs = refs[:n], refs[n:2 * n]
        send_sems, recv_sems, local_sems = refs[2 * n:]
        x, y, c = _mesh_pos()
        me = 4 * x + 2 * y + c

        def src(a, idx):
            return ins[a] if gather else ins[a].at[idx]

        local = []
        for a in range(n):
            cp = pltpu.make_async_copy(src(a, me), outs[a].at[me], local_sems.at[a])
            cp.start()
            local.append(cp)
        sends = []
        for k in range(1, N_DEV):
            peer, pidx = _peer(k)
            for a in range(n):
                cp = pltpu.make_async_remote_copy(
                    src_ref=src(a, pidx), dst_ref=outs[a].at[me],
                    send_sem=send_sems.at[a, k - 1], recv_sem=recv_sems.at[a, k - 1],
                    device_id=peer, device_id_type=pl.DeviceIdType.MESH)
                cp.start()
                sends.append(cp)
        for k in range(1, N_DEV):
            peer, pidx = _peer(k)
            for a in range(n):
                pltpu.make_async_remote_copy(
                    src_ref=src(a, pidx), dst_ref=outs[a].at[pidx],
                    send_sem=send_sems.at[a, k - 1], recv_sem=recv_sems.at[a, k - 1],
                    device_id=peer, device_id_type=pl.DeviceIdType.MESH).wait_recv()
        for cp in sends:
            cp.wait_send()
        for cp in local:
            cp.wait()

    any_spec = pl.BlockSpec(memory_space=pl.ANY)
    return pl.pallas_call(
        body, name=name, out_shape=out_shape,
        in_specs=[any_spec] * n, out_specs=[any_spec] * n,
        scratch_shapes=[pltpu.SemaphoreType.DMA((n, N_DEV - 1)),
                        pltpu.SemaphoreType.DMA((n, N_DEV - 1)),
                        pltpu.SemaphoreType.DMA((n,))],
    )(*arrs)


def _gather_two_level(shard, name):
    def body(x_ref, out_ref, send_sems, recv_sems, local_sem):
        x, y, c = _mesh_pos()
        me, sibling = (x, y, c), (x, y, 1 - c)
        chips = [(1 - x, y), (x, 1 - y), (1 - x, 1 - y)]

        def slot(px, py, pc):
            return out_ref.at[4 * px + 2 * py + pc]

        def copy(k, block, to, src=None):
            return pltpu.make_async_remote_copy(
                src_ref=slot(*block) if src is None else src, dst_ref=slot(*block),
                send_sem=send_sems.at[k], recv_sem=recv_sems.at[k],
                device_id=to, device_id_type=pl.DeviceIdType.MESH)

        mine = pltpu.make_async_copy(x_ref, slot(*me), local_sem)
        mine.start()
        first = [copy(1 + j, me, (*chip, c), src=x_ref) for j, chip in enumerate(chips)]
        first.append(copy(0, me, sibling, src=x_ref))
        for cp in first:
            cp.start()
        passed = [copy(4 + j, (*chip, c), sibling) for j, chip in enumerate(chips)]
        for j, chip in enumerate(chips):
            copy(1 + j, (*chip, c), me).wait_recv()
            passed[j].start()
        copy(0, sibling, me).wait_recv()
        for j, chip in enumerate(chips):
            copy(4 + j, (*chip, 1 - c), me).wait_recv()
        for cp in first + passed:
            cp.wait_send()
        mine.wait()

    any_spec = pl.BlockSpec(memory_space=pl.ANY)
    return pl.pallas_call(
        body, name=name, out_shape=jax.ShapeDtypeStruct((N_DEV,) + shard.shape, shard.dtype),
        in_specs=[any_spec], out_specs=any_spec,
        scratch_shapes=[pltpu.SemaphoreType.DMA((7,)), pltpu.SemaphoreType.DMA((7,)),
                        pltpu.SemaphoreType.DMA],
    )(shard)


def _chip_peer(k):
    x, y, c = _mesh_pos()
    px = (1 - x) if (k >> 1) & 1 else x
    py = (1 - y) if k & 1 else y
    return (px, py, c), 2 * px + py


def _pair_sums(parts, name, after):
    _, rows, cols = parts.shape
    n_chips = N_DEV // 2

    def body(p_ref, after_ref, send_ref, own_ref, mine_buf, sib_buf, send_sems, recv_sems,
             local_sems):
        x, y, c = _mesh_pos()
        sibling = (x, y, 1 - c)
        copies, local = [], []
        for q in range(n_chips):
            cp = pltpu.make_async_remote_copy(
                src_ref=p_ref.at[2 * q + (1 - c)], dst_ref=sib_buf.at[q],
                send_sem=send_sems.at[q], recv_sem=recv_sems.at[q],
                device_id=sibling, device_id_type=pl.DeviceIdType.MESH)
            cp.start()
            copies.append(cp)
            lc = pltpu.make_async_copy(p_ref.at[2 * q + c], mine_buf.at[q], local_sems.at[q])
            lc.start()
            local.append(lc)
        for lc in local:
            lc.wait()
        for cp in copies:
            cp.wait_recv()
        for k in range(1, n_chips):
            _, q = _chip_peer(k)
            send_ref[k - 1] = (mine_buf[q].astype(F32) + sib_buf[q].astype(F32)).astype(BF16)
        my_chip = 2 * x + y
        own_ref[...] = mine_buf[my_chip].astype(F32) + sib_buf[my_chip].astype(F32)
        for cp in copies:
            cp.wait_send()

    vmem = pl.BlockSpec(memory_space=pltpu.VMEM)
    return pl.pallas_call(
        body, name=name,
        out_shape=[jax.ShapeDtypeStruct((n_chips - 1, rows, cols), BF16),
                   jax.ShapeDtypeStruct((rows, cols), F32)],
        in_specs=[pl.BlockSpec(memory_space=pl.ANY)] * 2, out_specs=[vmem, vmem],
        scratch_shapes=[pltpu.VMEM((n_chips, rows, cols), BF16),
                        pltpu.VMEM((n_chips, rows, cols), BF16),
                        pltpu.SemaphoreType.DMA((n_chips,)), pltpu.SemaphoreType.DMA((n_chips,)),
                        pltpu.SemaphoreType.DMA((n_chips,))],
        compiler_params=pltpu.CompilerParams(vmem_limit_bytes=40 * MIB),
    )(parts, after)


def _chip_copy(src_ref, land_ref, send_sem, recv_sem, k):
    peer, _ = _chip_peer(k)
    return pltpu.make_async_remote_copy(
        src_ref=src_ref.at[k - 1], dst_ref=land_ref.at[k - 1], send_sem=send_sem, recv_sem=recv_sem,
        device_id=peer, device_id_type=pl.DeviceIdType.MESH)


def _chip_exchange_start(blocks, name):
    hbm = pl.BlockSpec(memory_space=pltpu.HBM)
    sem = pl.BlockSpec(memory_space=pltpu.SEMAPHORE)
    n_peers = blocks.shape[0]

    def body(src_ref, zone_ref, send_sems, recv_sems, src_thru, zone_thru, token):
        for k in range(1, n_peers + 1):
            _chip_copy(src_ref, zone_ref, send_sems.at[k - 1], recv_sems.at[k - 1], k).start()
        token[...] = jnp.zeros_like(token)

    outs = pl.pallas_call(
        body, name=name, in_specs=[hbm, hbm],
        out_shape=[pltpu.SemaphoreType.DMA((n_peers,)), pltpu.SemaphoreType.DMA((n_peers,)),
                   pltpu.HBM(blocks.shape, blocks.dtype), pltpu.HBM(blocks.shape, blocks.dtype),
                   jax.ShapeDtypeStruct((8, 128), F32)],
        out_specs=[sem, sem, hbm, hbm, pl.BlockSpec(memory_space=pltpu.VMEM)],
        input_output_aliases={0: 2, 1: 3},
        compiler_params=pltpu.CompilerParams(
            has_side_effects=pltpu.SideEffectType.DATAFLOW_SIDE_EFFECTING),
    )(pltpu.with_memory_space_constraint(blocks, pltpu.HBM),
      pltpu.with_memory_space_constraint(lax.empty(blocks.shape, blocks.dtype), pltpu.HBM))
    return outs[:4], outs[4]


def _chip_exchange_wait(handle, after, name):
    send_sems, recv_sems, src, zone = handle
    hbm = pl.BlockSpec(memory_space=pltpu.HBM)
    sem = pl.BlockSpec(memory_space=pltpu.SEMAPHORE)

    def body(src_ref, zone_ref, ssem, rsem, after_ref, src_out, zone_out):
        for k in range(1, src.shape[0] + 1):
            cp = _chip_copy(src_ref, zone_ref, ssem.at[k - 1], rsem.at[k - 1], k)
            cp.wait_send()
            cp.wait_recv()

    outs = pl.pallas_call(
        body, name=name,
        in_specs=[hbm, hbm, sem, sem, pl.BlockSpec(memory_space=pl.ANY)],
        out_shape=[pltpu.HBM(src.shape, src.dtype), pltpu.HBM(zone.shape, zone.dtype)],
        out_specs=[hbm, hbm], input_output_aliases={0: 0, 1: 1},
        compiler_params=pltpu.CompilerParams(
            has_side_effects=pltpu.SideEffectType.DATAFLOW_SIDE_EFFECTING),
    )(src, zone, send_sems, recv_sems, after)
    return outs[1]


def _remote_copy(gather, src_ref, land_ref, send_sem, recv_sem, k, receive_side):
    x, y, c = _mesh_pos()
    me = 4 * x + 2 * y + c
    peer, pidx = _peer(k)
    return pltpu.make_async_remote_copy(
        src_ref=src_ref if gather else src_ref.at[pidx],
        dst_ref=land_ref.at[pidx if receive_side else me],
        send_sem=send_sem, recv_sem=recv_sem,
        device_id=peer, device_id_type=pl.DeviceIdType.MESH)


def _exchange_start(groups, name, gather):
    arrs = [a for g in groups for a in g]
    n, n_groups = len(arrs), len(groups)
    lands = [jax.ShapeDtypeStruct(((N_DEV,) + a.shape) if gather else a.shape, a.dtype)
             for a in arrs]

    def body(*refs):
        srcs, zones = refs[:n], refs[n:2 * n]
        sems = refs[2 * n:2 * n + 2 * n_groups]
        token = refs[-1]
        a = 0
        for gi, g in enumerate(groups):
            send_sems, recv_sems = sems[2 * gi], sems[2 * gi + 1]
            for k in range(1, N_DEV):
                for ai in range(len(g)):
                    slot = ai * (N_DEV - 1) + k - 1
                    _remote_copy(gather, srcs[a + ai], zones[a + ai], send_sems.at[slot],
                                 recv_sems.at[slot], k, False).start()
            a += len(g)
        token[...] = jnp.zeros_like(token)

    hbm = pl.BlockSpec(memory_space=pltpu.HBM)
    sem = pl.BlockSpec(memory_space=pltpu.SEMAPHORE)
    sem_shapes = []
    for g in groups:
        sem_shapes += [pltpu.SemaphoreType.DMA((len(g) * (N_DEV - 1),))] * 2
    outs = pl.pallas_call(
        body, name=name,
        in_specs=[hbm] * (2 * n),
        out_shape=sem_shapes + [pltpu.HBM(a.shape, a.dtype) for a in arrs]
        + [pltpu.HBM(z.shape, z.dtype) for z in lands] + [jax.ShapeDtypeStruct((8, 128), F32)],
        out_specs=[sem] * (2 * n_groups) + [hbm] * (2 * n)
        + [pl.BlockSpec(memory_space=pltpu.VMEM)],
        input_output_aliases={i: 2 * n_groups + i for i in range(2 * n)},
        compiler_params=pltpu.CompilerParams(
            has_side_effects=pltpu.SideEffectType.DATAFLOW_SIDE_EFFECTING),
    )(*[pltpu.with_memory_space_constraint(a, pltpu.HBM) for a in arrs],
      *[pltpu.with_memory_space_constraint(lax.empty(z.shape, z.dtype), pltpu.HBM) for z in lands])
    sems = outs[:2 * n_groups]
    thru = outs[2 * n_groups:2 * n_groups + n]
    zones = outs[2 * n_groups + n:2 * n_groups + 2 * n]
    handles, a = [], 0
    for gi, g in enumerate(groups):
        handles.append((sems[2 * gi], sems[2 * gi + 1], thru[a:a + len(g)], zones[a:a + len(g)]))
        a += len(g)
    return handles, outs[-1]


def _exchange_wait(handle, after, name, gather):
    send_sems, recv_sems, thru, zones = handle
    n = len(thru)

    def body(*refs):
        srcs, lands = refs[:n], refs[n:2 * n]
        ssem, rsem = refs[2 * n], refs[2 * n + 1]
        for k in range(1, N_DEV):
            for ai in range(n):
                slot = ai * (N_DEV - 1) + k - 1
                cp = _remote_copy(gather, srcs[ai], lands[ai], ssem.at[slot], rsem.at[slot], k, True)
                cp.wait_send()
                cp.wait_recv()

    hbm = pl.BlockSpec(memory_space=pltpu.HBM)
    sem = pl.BlockSpec(memory_space=pltpu.SEMAPHORE)
    outs = pl.pallas_call(
        body, name=name,
        in_specs=[hbm] * (2 * n) + [sem, sem, pl.BlockSpec(memory_space=pl.ANY)],
        out_shape=[pltpu.HBM(a.shape, a.dtype) for a in thru]
        + [pltpu.HBM(z.shape, z.dtype) for z in zones],
        out_specs=[hbm] * (2 * n),
        input_output_aliases={i: i for i in range(2 * n)},
        compiler_params=pltpu.CompilerParams(
            has_side_effects=pltpu.SideEffectType.DATAFLOW_SIDE_EFFECTING),
    )(*thru, *zones, send_sems, recv_sems, after)
    return outs[:n], outs[n:]


def _own_block(zone, block):
    x, y, c = _mesh_pos()
    me = 4 * x + 2 * y + c
    return lax.dynamic_update_slice_in_dim(zone, block[None], me, axis=0)


def _proj_fwd(x, g1, wcat, bdiag, gq, gk, bfor, tri, pdq, pdk, ones_q, ones_k):
    s_len = x.shape[0]
    tm = TOKEN_TILE
    nt = s_len // tm

    def body(x_ref, g1_ref, w_ref, bd_ref, gq_ref, gk_ref, bf_ref, tri_ref, pdq_ref,
             pdk_ref, oq_ref, ok_ref,
             h_ref, qa_ref, ka_ref, kat_ref, vs_ref, vt_ref, qr_ref, kr_ref, flog_ref, uv_ref,
             gp_ref, carry):
        @pl.when(pl.program_id(0) == 0)
        def _():
            carry[...] = jnp.zeros_like(carry)

        xf = x_ref[...]
        r = lax.rsqrt(jnp.mean(xf * xf, axis=-1, keepdims=True) + EPS)
        h = (xf * r * g1_ref[...]).astype(BF16)
        h_ref[...] = h
        dot = functools.partial(jnp.dot, preferred_element_type=F32)

        def proj(lo, hi):
            return _dot_nt(h, w_ref[lo:hi, :])

        flog = proj(C_F, C_UV) + bf_ref[...]
        flog_ref[...] = flog
        lane = lax.broadcasted_iota(jnp.int32, flog.shape, 1)
        logf = jnp.minimum(flog, 0.0) - jnp.log(1.0 + jnp.exp(-jnp.abs(flog)))
        logf = jnp.where(lane < HEADS, logf, 0.0)
        dcum = _tri_dot(tri_ref[...], logf) + carry[...]
        carry[...] = dcum[tm - 1:tm, :]
        d2 = dcum * LOG2E
        d2a = d2.astype(BF16)
        rem = d2 - d2a.astype(F32)
        d2b = rem.astype(BF16)
        d2c = (rem - d2b.astype(F32)).astype(BF16)

        q = proj(C_Q, C_K)
        qr_ref[...] = q.astype(BF16)
        rq = lax.rsqrt(_seg_mean(q * q, bd_ref) + EPS)
        qn = q * rq * (gq_ref[...] * (HEAD_DIM ** -0.5 * LOG2E))
        qa = (_slabs_from_heads(qn) + dot(d2a, pdq_ref[0]) + dot(d2b, pdq_ref[1])
              + dot(d2c, pdq_ref[2]) + oq_ref[...])
        qa_ref[...] = qa.astype(BF16)

        k = proj(C_K, C_V)
        kr_ref[...] = k.astype(BF16)
        rk = lax.rsqrt(_seg_mean(k * k, bd_ref) + EPS)
        kn = k * rk * gk_ref[...]
        ka = (_slabs_from_heads(kn) - dot(d2a, pdk_ref[0]) - dot(d2b, pdk_ref[1])
              - dot(d2c, pdk_ref[2]) + ok_ref[...])
        ka_ref[...] = ka.astype(BF16)
        kat_ref[0] = ka.T.astype(BF16)

        v = proj(C_V, C_F)
        vs_ref[...] = _slabs_from_heads(v).astype(BF16)
        vt_ref[0] = v.T.astype(BF16)
        uv_ref[...] = proj(C_UV, C_G).astype(BF16)
        gp_ref[...] = proj(C_G, C_END).astype(BF16)

    outs = [((s_len, D_MODEL), BF16, _row_spec(tm, D_MODEL)),
            ((s_len, SLAB_W), BF16, _row_spec(tm, SLAB_W)),
            ((s_len, SLAB_W), BF16, _row_spec(tm, SLAB_W)),
            ((nt, SLAB_W, tm), BF16, _tile_spec(SLAB_W, tm)),
            ((s_len, SLAB_W), BF16, _row_spec(tm, SLAB_W)),
            ((nt, FOX_W, tm), BF16, _tile_spec(FOX_W, tm)),
            ((s_len, FOX_W), BF16, _row_spec(tm, FOX_W)),
            ((s_len, FOX_W), BF16, _row_spec(tm, FOX_W)),
            ((s_len, 128), F32, _row_spec(tm, 128)),
            ((s_len, 2 * SGU_W), BF16, _row_spec(tm, 2 * SGU_W)),
            ((s_len, 2 * D_MODEL), BF16, _row_spec(tm, 2 * D_MODEL))]
    return pl.pallas_call(
        body, name="proj_fwd", grid=(nt,),
        in_specs=[_row_spec(tm, D_MODEL), _const_spec((1, D_MODEL)), _const_spec(wcat.shape),
                  _const_spec(bdiag.shape), _const_spec((1, FOX_W)), _const_spec((1, FOX_W)),
                  _const_spec((1, 128)), _const_spec((tm, tm)), _const_spec(pdq.shape), _const_spec(pdk.shape), _const_spec(ones_q.shape),
                  _const_spec(ones_k.shape)],
        out_specs=[o[2] for o in outs],
        out_shape=[jax.ShapeDtypeStruct(o[0], o[1]) for o in outs],
        scratch_shapes=[pltpu.VMEM((1, 128), F32)],
        compiler_params=_params(56, 1),
    )(x, g1, wcat, bdiag, gq, gk, bfor, tri, pdq, pdk, ones_q, ones_k)


def _attn_fwd(qa, ka, vt):
    s_len = qa.shape[0]
    t = ATTN_TILE
    nb = s_len // t

    def body(q_ref, k_ref, vt_ref, o_ref, ot_ref, lse_ref, m_sc, l_sc, acc_sc, s_sc, alpha_sc):
        i = pl.program_id(0)
        m_sc[...] = jnp.full_like(m_sc, -jnp.inf)
        l_sc[...] = jnp.zeros_like(l_sc)
        acc_sc[...] = jnp.zeros_like(acc_sc)

        def tile(j, masked):
            krows = pl.ds(pl.multiple_of(j * t, t), t)
            if masked:
                keep = (lax.broadcasted_iota(jnp.int32, (t, t), 0)
                        <= lax.broadcasted_iota(jnp.int32, (t, t), 1))
            for hd in range(HEADS):
                sl = slice(hd * 128, (hd + 1) * 128)
                st = _dot_nt(k_ref[krows, sl], q_ref[:, sl])
                if masked:
                    st = jnp.where(keep, st, -jnp.inf)
                s_sc[hd] = st
                m_prev = m_sc[hd:hd + 1, :]
                m_new = jnp.maximum(m_prev, jnp.max(st, axis=0, keepdims=True))
                alpha_sc[hd:hd + 1, :] = jnp.exp2(m_prev - m_new)
                m_sc[hd:hd + 1, :] = m_new
            for hd in range(HEADS):
                hr = slice(hd * HEAD_DIM, (hd + 1) * HEAD_DIM)
                alpha = alpha_sc[hd:hd + 1, :]
                pt = jnp.exp2(s_sc[hd] - m_sc[hd:hd + 1, :])
                l_sc[hd:hd + 1, :] = alpha * l_sc[hd:hd + 1, :] + jnp.sum(pt, axis=0, keepdims=True)
                acc_sc[hr, :] = alpha * acc_sc[hr, :] + jnp.dot(
                    vt_ref[j, hr, :], pt.astype(BF16), preferred_element_type=F32)

        def off_diagonal(j, carry):
            tile(j, False)
            return carry

        lax.fori_loop(0, i, off_diagonal, 0)
        tile(i, True)

        for hd in range(HEADS):
            hr = slice(hd * HEAD_DIM, (hd + 1) * HEAD_DIM)
            l = l_sc[hd:hd + 1, :]
            acc_sc[hr, :] = acc_sc[hr, :] / l
            lse_ref[0, hd:hd + 1, :] = m_sc[hd:hd + 1, :] + jnp.log2(l)
        o_ref[...] = acc_sc[...].T.astype(BF16)
        ot_ref[...] = acc_sc[...].astype(BF16)

    return pl.pallas_call(
        body, name="attn_fwd", grid=(nb,),
        in_specs=[_row_spec(t, SLAB_W), _const_spec(ka.shape), _const_spec(vt.shape)],
        out_specs=[_row_spec(t, FOX_W), pl.BlockSpec((FOX_W, t), lambda i: (0, i)),
                   _tile_spec(HEADS, t)],
        out_shape=[jax.ShapeDtypeStruct((s_len, FOX_W), BF16),
                   jax.ShapeDtypeStruct((FOX_W, s_len), BF16),
                   jax.ShapeDtypeStruct((nb, HEADS, t), F32)],
        scratch_shapes=[pltpu.VMEM((HEADS, t), F32), pltpu.VMEM((HEADS, t), F32),
                        pltpu.VMEM((FOX_W, t), F32), pltpu.VMEM((HEADS, t, t), F32),
                        pltpu.VMEM((HEADS, t), F32)],
        compiler_params=_params(48, 1),
    )(qa, ka, vt)


def _sgu_mix(vn, ws_ref):
    tm = vn.shape[0]
    lane = lax.broadcasted_iota(jnp.int32, (WINDOW, 128), 1)
    low = lane < HEAD_DIM
    wins = []
    for w in range(tm // WINDOW):
        slabs = []
        for p in range(GROUPS // 2):
            v2 = vn[w * WINDOW:(w + 1) * WINDOW, p * 128:(p + 1) * 128]
            lo = jnp.where(low, v2, 0.0).astype(BF16)
            hi = jnp.where(low, 0.0, v2).astype(BF16)
            slabs.append(jnp.dot(ws_ref[2 * p], lo, preferred_element_type=F32)
                         + jnp.dot(ws_ref[2 * p + 1], hi, preferred_element_type=F32))
        wins.append(jnp.concatenate(slabs, axis=1))
    return jnp.concatenate(wins, axis=0) if len(wins) > 1 else wins[0]


def _layernorm_fwd(vv, g, b):
    mu = jnp.mean(vv, axis=-1, keepdims=True)
    xc = vv - mu
    r = lax.rsqrt(jnp.mean(xc * xc, axis=-1, keepdims=True) + EPS)
    xh = xc * r
    return xh * g + b, xh, r


def _mix_fwd(attn, uvpre, gpre, x, wa, wb, wout, wsm, bsf, gsgu, bsgu, gpost):
    s_len = x.shape[0]
    tm = TOKEN_TILE

    def body(o_ref, uv_ref, gp_ref, x_ref, wa_ref, wb_ref, wo_ref, ws_ref, bs_ref, gs_ref, bsg_ref,
             gpost_ref, sgut_ref, ya_ref, yb_ref, mgt_ref, om_ref, x1_ref):
        uvp = uv_ref[...].astype(F32)
        uv, _ = _gelu_and_grad(uvp)
        u, vv = uv[:, :SGU_W], uv[:, SGU_W:]
        vn, _, _ = _layernorm_fwd(vv, gs_ref[...], bsg_ref[...])
        bias = bs_ref[...]
        if tm > WINDOW:
            bias = jnp.concatenate([bias] * (tm // WINDOW), axis=0)
        mixed = _sgu_mix(vn, ws_ref) + bias
        sgu_f = u * mixed
        sgu = sgu_f.astype(BF16)
        sgut_ref[...] = sgu_f.T.astype(BF16)
        ya = jnp.dot(o_ref[...], wa_ref[...], preferred_element_type=F32)
        yb = jnp.dot(sgu, wb_ref[...], preferred_element_type=F32)
        ya_ref[...] = ya.astype(BF16)
        yb_ref[...] = yb.astype(BF16)
        gates = _sigmoid(gp_ref[...].astype(F32))
        merged_f = gates[:, :D_MODEL] * ya + gates[:, D_MODEL:] * yb
        merged = merged_f.astype(BF16)
        mgt_ref[...] = merged_f.T.astype(BF16)
        om = jnp.dot(merged, wo_ref[...], preferred_element_type=F32)
        om_ref[...] = om
        r = lax.rsqrt(jnp.mean(om * om, axis=-1, keepdims=True) + EPS)
        x1_ref[...] = x_ref[...] + om * r * gpost_ref[...]

    def t_out(rows):
        return ((rows, s_len), BF16, pl.BlockSpec((rows, tm), lambda i: (0, i)))

    def r_out(cols, dt):
        return ((s_len, cols), dt, _row_spec(tm, cols))

    outs = [t_out(SGU_W), r_out(D_MODEL, BF16), r_out(D_MODEL, BF16), t_out(D_MODEL),
            r_out(D_MODEL, F32), r_out(D_MODEL, F32)]
    return pl.pallas_call(
        body, name="mix_fwd", grid=(s_len // tm,),
        in_specs=[_row_spec(tm, FOX_W), _row_spec(tm, 2 * SGU_W), _row_spec(tm, 2 * D_MODEL),
                  _row_spec(tm, D_MODEL), _const_spec(wa.shape), _const_spec(wb.shape),
                  _const_spec(wout.shape), _const_spec(wsm.shape), _const_spec(bsf.shape),
                  _const_spec((1, SGU_W)), _const_spec((1, SGU_W)), _const_spec((1, D_MODEL))],
        out_specs=[o[2] for o in outs],
        out_shape=[jax.ShapeDtypeStruct(o[0], o[1]) for o in outs],
        compiler_params=_params(48, 1),
    )(attn, uvpre, gpre, x, wa, wb, wout, wsm, bsf, gsgu, bsgu, gpost)


def _ffn_fwd_bwd(x1, tgt, wffn, wdown, gpre, gpost):
    s_len = x1.shape[0]
    tm = TOKEN_TILE

    def body(x1_ref, t_ref, wi_ref, wd_ref, gpre_ref, gpost_ref,
             dx1_ref, h2_ref, actt_ref, dff_ref, dgut_ref, loss_ref, dgpost_ref, dgpre_ref):
        @pl.when(pl.program_id(0) == 0)
        def _():
            loss_ref[...] = jnp.zeros_like(loss_ref)
            dgpost_ref[...] = jnp.zeros_like(dgpost_ref)
            dgpre_ref[...] = jnp.zeros_like(dgpre_ref)

        x1v = x1_ref[...]
        r2 = lax.rsqrt(jnp.mean(x1v * x1v, axis=-1, keepdims=True) + EPS)
        gpre_v = gpre_ref[...]
        h2 = (x1v * r2 * gpre_v).astype(BF16)
        h2_ref[...] = h2
        gg = _dot_nt(h2, wi_ref[:D_FF, :])
        uu = _dot_nt(h2, wi_ref[D_FF:, :])
        sg = _sigmoid(gg)
        silu = gg * sg
        act_f = silu * uu
        act = act_f.astype(BF16)
        actt_ref[...] = act_f.T.astype(BF16)
        ff = jnp.dot(act, wd_ref[...], preferred_element_type=F32)
        r3 = lax.rsqrt(jnp.mean(ff * ff, axis=-1, keepdims=True) + EPS)
        gpost_v = gpost_ref[...]
        y = x1v + ff * r3 * gpost_v
        err = y - t_ref[...]
        loss_ref[...] += jnp.sum(err * err) * (0.5 / D_MODEL)
        dy = err * (1.0 / D_MODEL)
        dgpost_ref[...] += jnp.sum(dy * ff * r3, axis=0, keepdims=True)
        dff = _rms_bwd(ff, r3, gpost_v, dy).astype(BF16)
        dff_ref[...] = dff
        dact = _dot_nt(dff, wd_ref[...])
        dgg_f = dact * uu * (sg * (1.0 + gg * (1.0 - sg)))
        duu_f = dact * silu
        dgg = dgg_f.astype(BF16)
        duu = duu_f.astype(BF16)
        dgut_ref[:D_FF, :] = dgg_f.T.astype(BF16)
        dgut_ref[D_FF:, :] = duu_f.T.astype(BF16)
        dh2 = (jnp.dot(dgg, wi_ref[:D_FF, :], preferred_element_type=F32)
               + jnp.dot(duu, wi_ref[D_FF:, :], preferred_element_type=F32))
        dgpre_ref[...] += jnp.sum(dh2 * x1v * r2, axis=0, keepdims=True)
        dx1_ref[...] = dy + _rms_bwd(x1v, r2, gpre_v, dh2)

    outs = [((s_len, D_MODEL), F32, _row_spec(tm, D_MODEL)),
            ((s_len, D_MODEL), BF16, _row_spec(tm, D_MODEL)),
            ((D_FF, s_len), BF16, pl.BlockSpec((D_FF, tm), lambda i: (0, i))),
            ((s_len, D_MODEL), BF16, _row_spec(tm, D_MODEL)),
            ((2 * D_FF, s_len), BF16, pl.BlockSpec((2 * D_FF, tm), lambda i: (0, i))),
            ((1, 128), F32, _const_spec((1, 128))),
            ((1, D_MODEL), F32, _const_spec((1, D_MODEL))),
            ((1, D_MODEL), F32, _const_spec((1, D_MODEL)))]
    return pl.pallas_call(
        body, name="ffn_fwd_bwd", grid=(s_len // tm,),
        in_specs=[_row_spec(tm, D_MODEL), _row_spec(tm, D_MODEL), _const_spec(wffn.shape),
                  _const_spec(wdown.shape), _const_spec((1, D_MODEL)), _const_spec((1, D_MODEL))],
        out_specs=[o[2] for o in outs],
        out_shape=[jax.ShapeDtypeStruct(o[0], o[1]) for o in outs],
        compiler_params=_params(60, 1),
    )(x1, tgt, wffn, wdown, gpre, gpost)


def _mix_bwd(dx1, om, ya, yb, gpre, uvpre, attn, wout, wa, wb, wsm, wsmt, bsf, gsgu, bsgu, gpost,
             wmask, egrp):
    s_len = dx1.shape[0]
    tm = TOKEN_TILE
    nw = tm // WINDOW
    nt = s_len // tm

    def body(dx1_ref, om_ref, ya_ref, yb_ref, gp_ref, uv_ref, o_ref, wo_ref, wa_ref, wb_ref, ws_ref,
             wst_ref, bs_ref, gs_ref, bsg_ref, gpost_ref, mask_ref, eg_ref,
             dom_ref, dya_ref, dyb_ref, dgp_ref, dot_ref, delta_ref, duv_ref,
             dws_ref, dbs_ref, dgs_ref, dbsg_ref, dgpost_ref, dbs_acc):
        step = pl.program_id(0)

        @pl.when(step == 0)
        def _():
            dws_ref[...] = jnp.zeros_like(dws_ref)
            dbs_acc[...] = jnp.zeros_like(dbs_acc)
            dgs_ref[...] = jnp.zeros_like(dgs_ref)
            dbsg_ref[...] = jnp.zeros_like(dbsg_ref)
            dgpost_ref[...] = jnp.zeros_like(dgpost_ref)

        om = om_ref[...]
        dx1v = dx1_ref[...]
        r = lax.rsqrt(jnp.mean(om * om, axis=-1, keepdims=True) + EPS)
        gpost_v = gpost_ref[...]
        dgpost_ref[...] += jnp.sum(dx1v * om * r, axis=0, keepdims=True)
        dom = _rms_bwd(om, r, gpost_v, dx1v).astype(BF16)
        dom_ref[...] = dom
        dmg = _dot_nt(dom, wo_ref[...])

        gates = _sigmoid(gp_ref[...].astype(F32))
        ga, gb = gates[:, :D_MODEL], gates[:, D_MODEL:]
        yav, ybv = ya_ref[...].astype(F32), yb_ref[...].astype(F32)
        dya = (dmg * ga).astype(BF16)
        dyb = (dmg * gb).astype(BF16)
        dya_ref[...] = dya
        dyb_ref[...] = dyb
        dgp_ref[:, :D_MODEL] = (dmg * yav * ga * (1.0 - ga)).astype(BF16)
        dgp_ref[:, D_MODEL:] = (dmg * ybv * gb * (1.0 - gb)).astype(BF16)

        dat_t = _dot_nt(dya, wa_ref[...]).T.astype(BF16)
        dot_ref[0] = dat_t
        o_t = o_ref[...].astype(F32).T
        delta_ref[0] = jnp.sum((dat_t.astype(F32) * o_t).reshape(HEADS, HEAD_DIM, tm), axis=1)
        dsgu = _dot_nt(dyb, wb_ref[...])

        uvp = uv_ref[...].astype(F32)
        uv, guv = _gelu_and_grad(uvp)
        u, vv = uv[:, :SGU_W], uv[:, SGU_W:]
        gs_v = gs_ref[...]
        vn, xh, rln = _layernorm_fwd(vv, gs_v, bsg_ref[...])
        bias = bs_ref[...]
        if nw > 1:
            bias = jnp.concatenate([bias] * nw, axis=0)
        mixed = _sgu_mix(vn, ws_ref) + bias
        du = dsgu * mixed
        dmixed = dsgu * u

        lane = lax.broadcasted_iota(jnp.int32, (WINDOW, 128), 1)
        low = lane < HEAD_DIM
        dvn_wins = []
        for w in range(nw):
            rows = slice(w * WINDOW, (w + 1) * WINDOW)
            dbs_acc[...] += dmixed[rows, :]
            slabs = []
            for p in range(GROUPS // 2):
                cols = slice(p * 128, (p + 1) * 128)
                dm2 = dmixed[rows, cols]
                dlo = jnp.where(low, dm2, 0.0).astype(BF16)
                dhi = jnp.where(low, 0.0, dm2).astype(BF16)
                vn2 = vn[rows, cols].astype(BF16)
                dws_ref[2 * p] += _dot_nt(dlo, vn2)
                dws_ref[2 * p + 1] += _dot_nt(dhi, vn2)
                slabs.append(jnp.dot(wst_ref[2 * p], dlo, preferred_element_type=F32)
                             + jnp.dot(wst_ref[2 * p + 1], dhi, preferred_element_type=F32))
            dvn_wins.append(jnp.concatenate(slabs, axis=1))
        dvn = jnp.concatenate(dvn_wins, axis=0) if nw > 1 else dvn_wins[0]

        dgs_ref[...] += jnp.sum(dvn * xh, axis=0, keepdims=True)
        dbsg_ref[...] += jnp.sum(dvn, axis=0, keepdims=True)
        dxh = dvn * gs_v
        dvv = rln * (dxh - jnp.mean(dxh, axis=-1, keepdims=True)
                     - xh * jnp.mean(dxh * xh, axis=-1, keepdims=True))
        duv_ref[:, :SGU_W] = (du * guv[:, :SGU_W]).astype(BF16)
        duv_ref[:, SGU_W:] = (dvv * guv[:, SGU_W:]).astype(BF16)

        @pl.when(step == pl.num_programs(0) - 1)
        def _():
            for g in range(GROUPS):
                dws_ref[g] = dws_ref[g] * mask_ref[...]
            dbs_ref[...] = _split3_dot(dbs_acc[...], eg_ref[...])

    rows_out = [((s_len, D_MODEL), BF16, _row_spec(tm, D_MODEL)),
                ((s_len, D_MODEL), BF16, _row_spec(tm, D_MODEL)),
                ((s_len, D_MODEL), BF16, _row_spec(tm, D_MODEL)),
                ((s_len, 2 * D_MODEL), BF16, _row_spec(tm, 2 * D_MODEL)),
                ((nt, FOX_W, tm), BF16, _tile_spec(FOX_W, tm)),
                ((nt, HEADS, tm), F32, _tile_spec(HEADS, tm)),
                ((s_len, 2 * SGU_W), BF16, _row_spec(tm, 2 * SGU_W))]
    acc_out = [((GROUPS, WINDOW, WINDOW), F32), ((WINDOW, 128), F32), ((1, SGU_W), F32),
               ((1, SGU_W), F32), ((1, D_MODEL), F32)]
    return pl.pallas_call(
        body, name="mix_bwd", grid=(nt,),
        in_specs=[_row_spec(tm, D_MODEL), _row_spec(tm, D_MODEL), _row_spec(tm, D_MODEL),
                  _row_spec(tm, D_MODEL), _row_spec(tm, 2 * D_MODEL), _row_spec(tm, 2 * SGU_W),
                  _row_spec(tm, FOX_W), _const_spec(wout.shape), _const_spec(wa.shape),
                  _const_spec(wb.shape), _const_spec(wsm.shape), _const_spec(wsmt.shape),
                  _const_spec(bsf.shape), _const_spec((1, SGU_W)), _const_spec((1, SGU_W)),
                  _const_spec((1, D_MODEL)), _const_spec(wmask.shape), _const_spec(egrp.shape)],
        out_specs=[o[2] for o in rows_out] + [_const_spec(s) for s, _ in acc_out],
        out_shape=[jax.ShapeDtypeStruct(o[0], o[1]) for o in rows_out]
        + [jax.ShapeDtypeStruct(s, dt) for s, dt in acc_out],
        scratch_shapes=[pltpu.VMEM((WINDOW, SGU_W), F32)],
        compiler_params=_params(48, 1),
    )(dx1, om, ya, yb, gpre, uvpre, attn, wout, wa, wb, wsm, wsmt, bsf, gsgu, bsgu, gpost, wmask,
      egrp)


def _attn_bwd(qa, ka, kat, vs, dot_, lse, delta, ecol):
    s_len = qa.shape[0]
    t = ATTN_TILE
    nb = s_len // t

    def body(k_ref, kt_ref, vs_ref, q_ref, do_ref, lse_ref, dl_ref, ec_ref, gk_ref, dvt_ref,
             gqt_ref, csum_ref, p_sc, ds_sc):
        j = pl.program_id(0)

        @pl.when(j == 0)
        def _():
            gqt_ref[...] = jnp.zeros_like(gqt_ref)

        gk_ref[...] = jnp.zeros_like(gk_ref)
        dvt_ref[...] = jnp.zeros_like(dvt_ref)

        def tile(i, masked):
            qrows = pl.ds(pl.multiple_of(i * t, t), t)
            if masked:
                keep = (lax.broadcasted_iota(jnp.int32, (t, t), 0)
                        <= lax.broadcasted_iota(jnp.int32, (t, t), 1))
            for hd in range(HEADS):
                sl = slice(hd * 128, (hd + 1) * 128)
                hr = slice(hd * HEAD_DIM, (hd + 1) * HEAD_DIM)
                st = _dot_nt(k_ref[:, sl], q_ref[qrows, sl])
                if masked:
                    st = jnp.where(keep, st, -jnp.inf)
                pt = jnp.exp2(st - lse_ref[i, hd:hd + 1, :])
                dpt = jnp.dot(vs_ref[:, hd * 128:hd * 128 + HEAD_DIM], do_ref[i, hr, :],
                              preferred_element_type=F32)
                p_sc[hd] = pt.astype(BF16)
                ds_sc[hd] = (pt * (dpt - dl_ref[i, hd:hd + 1, :])).astype(BF16)
            for hd in range(HEADS):
                sl = slice(hd * 128, (hd + 1) * 128)
                hr = slice(hd * HEAD_DIM, (hd + 1) * HEAD_DIM)
                dst = ds_sc[hd]
                dvt_ref[0, hr, :] += _dot_nt(do_ref[i, hr, :], p_sc[hd])
                gk_ref[:, sl] += jnp.dot(dst, q_ref[qrows, sl], preferred_element_type=F32)
                gqt_ref[i, hd * QT_ROWS:(hd + 1) * QT_ROWS, :] += jnp.dot(
                    kt_ref[0, hd * 128:hd * 128 + QT_ROWS, :], dst, preferred_element_type=F32)

        tile(j, True)

        def below_diagonal(i, carry):
            tile(i, False)
            return carry

        lax.fori_loop(j + 1, nb, below_diagonal, 0)
        csum_ref[...] = _split3_dot(gk_ref[...], ec_ref[...])

    return pl.pallas_call(
        body, name="attn_bwd", grid=(nb,),
        in_specs=[_row_spec(t, SLAB_W), _tile_spec(SLAB_W, t), _row_spec(t, SLAB_W),
                  _const_spec(qa.shape), _const_spec(dot_.shape), _const_spec(lse.shape),
                  _const_spec(delta.shape), _const_spec(ecol.shape)],
        out_specs=[_row_spec(t, SLAB_W), _tile_spec(FOX_W, t),
                   _const_spec((nb, HEADS * QT_ROWS, t)), _row_spec(t, 128)],
        out_shape=[jax.ShapeDtypeStruct((s_len, SLAB_W), F32),
                   jax.ShapeDtypeStruct((nb, FOX_W, t), F32),
                   jax.ShapeDtypeStruct((nb, HEADS * QT_ROWS, t), F32),
                   jax.ShapeDtypeStruct((s_len, 128), F32)],
        scratch_shapes=[pltpu.VMEM((HEADS, t, t), BF16), pltpu.VMEM((HEADS, t, t), BF16)],
        compiler_params=_params(60, 1),
    )(ka, kat, vs, qa, dot_, lse, delta, ecol)


def _rev_cumsum(col_sums, gqt, triu):
    s_len = col_sums.shape[0]
    tm = TOKEN_TILE
    n = s_len // tm

    def body(cs_ref, gqt_ref, tri_ref, o_ref, carry):
        @pl.when(pl.program_id(0) == 0)
        def _():
            carry[...] = jnp.zeros_like(carry)
        rows = [gqt_ref[0, hd * QT_ROWS + HEAD_DIM:hd * QT_ROWS + HEAD_DIM + 1, :]
                for hd in range(HEADS)]
        row_sums = jnp.concatenate(rows + [jnp.zeros((128 - HEADS, tm), F32)], axis=0).T
        out = _tri_dot(tri_ref[...], row_sums - cs_ref[...]) + carry[...]
        o_ref[...] = out
        carry[...] = out[0:1, :]

    return pl.pallas_call(
        body, name="rev_cumsum", grid=(n,),
        in_specs=[pl.BlockSpec((tm, 128), lambda i: (n - 1 - i, 0)),
                  pl.BlockSpec((1, HEADS * QT_ROWS, tm), lambda i: (n - 1 - i, 0, 0)),
                  _const_spec((tm, tm))],
        out_specs=pl.BlockSpec((tm, 128), lambda i: (n - 1 - i, 0)),
        out_shape=jax.ShapeDtypeStruct((s_len, 128), F32),
        scratch_shapes=[pltpu.VMEM((1, 128), F32)],
        compiler_params=_params(32, 1),
    )(col_sums, gqt, triu)


def _heads_from_slabs(slabs):
    lane = lax.broadcasted_iota(jnp.int32, slabs[0].shape, 1)
    low = lane < HEAD_DIM
    pairs = [jnp.where(low, slabs[2 * p], pltpu.roll(slabs[2 * p + 1], HEAD_DIM, 1))
             for p in range(HEADS // 2)]
    return jnp.concatenate(pairs, axis=1)


def _proj_bwd(gqt, gk, dvt, dlogf, flog, qraw, kraw, duv, dgp, x, dx1, wcat, bdiag, gq, gk_gain, g1,
              efold):
    s_len = x.shape[0]
    tm = TOKEN_TILE

    def body(gqt_ref, gkk_ref, dvt_ref, dlf_ref, flog_ref, qr_ref, kr_ref, duv_ref, dgp_ref, x_ref,
             dx1_ref, w_ref, bd_ref, gq_ref, gk_ref, g1_ref, ef_ref,
             dx_ref, dprojt_ref, dgq_ref, dgk_ref, dbf_ref, dg1_ref, gq_acc, gk_acc, dproj_ref):
        step = pl.program_id(0)

        @pl.when(step == 0)
        def _():
            gq_acc[...] = jnp.zeros_like(gq_acc)
            gk_acc[...] = jnp.zeros_like(gk_acc)
            dbf_ref[...] = jnp.zeros_like(dbf_ref)
            dg1_ref[...] = jnp.zeros_like(dg1_ref)

        pad = jnp.zeros((128 - QT_ROWS, tm), F32)
        q_slabs = [jnp.concatenate([gqt_ref[0, hd * QT_ROWS:(hd + 1) * QT_ROWS, :], pad], axis=0).T
                   for hd in range(HEADS)]
        dqn = _heads_from_slabs(q_slabs)
        dkn = _heads_from_slabs([gkk_ref[:, hd * 128:(hd + 1) * 128] for hd in range(HEADS)])

        def head_bwd(raw_ref, dn, g_ref, acc):
            raw = raw_ref[...].astype(F32)
            r = lax.rsqrt(_seg_mean(raw * raw, bd_ref) + EPS)
            xhat = raw * r
            acc[0:1, :] += jnp.sum(dn * xhat, axis=0, keepdims=True)
            dyg = dn * g_ref[...]
            return r * (dyg - xhat * _seg_mean(dyg * xhat, bd_ref))

        dproj_ref[:, C_Q:C_K] = head_bwd(qr_ref, dqn * HEAD_DIM ** -0.5, gq_ref, gq_acc).astype(BF16)
        dproj_ref[:, C_K:C_V] = head_bwd(kr_ref, dkn * LN2, gk_ref, gk_acc).astype(BF16)
        dproj_ref[:, C_V:C_F] = dvt_ref[0].T.astype(BF16)
        dfl = dlf_ref[...] * _sigmoid(-flog_ref[...])
        dbf_ref[...] += jnp.sum(dfl, axis=0, keepdims=True)
        dproj_ref[:, C_F:C_UV] = dfl.astype(BF16)
        dproj_ref[:, C_UV:C_G] = duv_ref[...]
        dproj_ref[:, C_G:C_END] = dgp_ref[...]

        dproj = dproj_ref[...]
        dprojt_ref[...] = dproj.astype(F32).T.astype(BF16)
        dh = jnp.dot(dproj, w_ref[...], preferred_element_type=F32)
        xf = x_ref[...]
        r = lax.rsqrt(jnp.mean(xf * xf, axis=-1, keepdims=True) + EPS)
        dg1_ref[...] += jnp.sum(dh * xf * r, axis=0, keepdims=True)
        dx_ref[...] = dx1_ref[...] + _rms_bwd(xf, r, g1_ref[...], dh)

        @pl.when(step == pl.num_programs(0) - 1)
        def _():
            dgq_ref[...] = _split3_dot(gq_acc[...], ef_ref[...])
            dgk_ref[...] = _split3_dot(gk_acc[...], ef_ref[...])

    outs = [((s_len, D_MODEL), F32, _row_spec(tm, D_MODEL)),
            ((C_END, s_len), BF16, pl.BlockSpec((C_END, tm), lambda i: (0, i))),
            ((8, 128), F32, _const_spec((8, 128))),
            ((8, 128), F32, _const_spec((8, 128))),
            ((1, 128), F32, _const_spec((1, 128))),
            ((1, D_MODEL), F32, _const_spec((1, D_MODEL)))]
    return pl.pallas_call(
        body, name="proj_bwd", grid=(s_len // tm,),
        in_specs=[_tile_spec(HEADS * QT_ROWS, tm), _row_spec(tm, SLAB_W), _tile_spec(FOX_W, tm),
                  _row_spec(tm, 128), _row_spec(tm, 128), _row_spec(tm, FOX_W),
                  _row_spec(tm, FOX_W), _row_spec(tm, 2 * SGU_W), _row_spec(tm, 2 * D_MODEL),
                  _row_spec(tm, D_MODEL), _row_spec(tm, D_MODEL), _const_spec(wcat.shape),
                  _const_spec(bdiag.shape), _const_spec((1, FOX_W)), _const_spec((1, FOX_W)),
                  _const_spec((1, D_MODEL)), _const_spec(efold.shape)],
        out_specs=[o[2] for o in outs],
        out_shape=[jax.ShapeDtypeStruct(o[0], o[1]) for o in outs],
        scratch_shapes=[pltpu.VMEM((8, FOX_W), F32), pltpu.VMEM((8, FOX_W), F32),
                        pltpu.VMEM((tm, C_END), BF16)],
        compiler_params=_params(56, 1),
    )(gqt, gk, dvt, dlogf, flog, qraw, kraw, duv, dgp, x, dx1, wcat, bdiag, gq, gk_gain, g1, efold)


def _dw_matmul(at, b, tm, name, after=()):
    m, s_len = at.shape
    n = b.shape[1]

    def body(a_ref, b_ref, *rest):
        rest[-1][...] = jnp.dot(a_ref[...], b_ref[...], preferred_element_type=F32).astype(BF16)

    return pl.pallas_call(
        body, name=name, grid=(m // tm,),
        in_specs=[pl.BlockSpec((tm, s_len), lambda i: (i, 0)), _const_spec(b.shape)]
        + [pl.BlockSpec(memory_space=pl.ANY)] * len(after),
        out_specs=pl.BlockSpec((tm, n), lambda i: (i, 0)),
        out_shape=jax.ShapeDtypeStruct((m, n), BF16),
        compiler_params=_params(48, 1),
    )(at, b, *after)


def _adamw(parts, w, m, v, tr, name, col_tile=None):
    parts = parts if isinstance(parts, (list, tuple)) else [parts]
    rows, cols = w.shape
    bc1 = 1.0 - ADAM_B1 ** ADAM_STEP
    bc2 = 1.0 - ADAM_B2 ** ADAM_STEP

    def body(*refs):
        p_refs = refs[:len(parts)]
        w_ref, m_ref, v_ref, g_ref, d_ref, mo_ref, vo_ref = refs[len(parts):]
        g = None
        for p_ref, p in zip(p_refs, parts):
            for idx in range(p.shape[0]):
                term = p_ref[idx].astype(F32)
                g = term if g is None else g + term
        g_ref[...] = g
        mn = ADAM_B1 * m_ref[...] + (1.0 - ADAM_B1) * g
        vn = ADAM_B2 * v_ref[...] + (1.0 - ADAM_B2) * (g * g)
        mo_ref[...] = mn
        vo_ref[...] = vn
        m_hat = mn / bc1
        v_hat = vn / bc2
        d_ref[...] = -ADAM_LR * (m_hat / (jnp.sqrt(v_hat) + ADAM_EPS) + ADAM_WD * w_ref[...])

    if col_tile is None:
        spec = pl.BlockSpec((tr, cols), lambda i: (i, 0))
        pspecs = [pl.BlockSpec((p.shape[0], tr, cols), lambda i: (0, i, 0)) for p in parts]
        steps = rows // tr
    else:
        spec = pl.BlockSpec((rows, col_tile), lambda i: (0, i))
        pspecs = [pl.BlockSpec((p.shape[0], rows, col_tile), lambda i: (0, 0, i)) for p in parts]
        steps = cols // col_tile
    return pl.pallas_call(
        body, name=name, grid=(steps,),
        in_specs=pspecs + [spec, spec, spec],
        out_specs=[spec] * 4,
        out_shape=[jax.ShapeDtypeStruct((rows, cols), F32)] * 4,
        compiler_params=_params(48, 1),
    )(*parts, w, m, v)


def _sum_parts(parts, name):
    n, rows, cols = parts.shape

    def body(p_ref, o_ref):
        g = p_ref[0]
        for idx in range(1, n):
            g = g + p_ref[idx]
        o_ref[...] = g

    return pl.pallas_call(
        body, name=name, out_shape=jax.ShapeDtypeStruct((rows, cols), F32),
        in_specs=[_const_spec(parts.shape)], out_specs=_const_spec((rows, cols)), grid=(1,),
        compiler_params=_params(16, 1),
    )(parts)


SMALL_NAMES = ("g_pre_mix", "b_forget", "g_q", "g_k", "g_sgu", "b_sgu", "w_spatial", "b_spatial",
               "g_post_mix", "g_pre_ffn", "g_post_ffn")


def _small_rows(size):
    return -(-size // 1024)


def _pack_small(d, extra=None):
    rows = []
    for k in SMALL_NAMES:
        flat = d[k].reshape(-1).astype(F32)
        nr = _small_rows(flat.shape[0])
        rows.append(jnp.pad(flat, (0, nr * 1024 - flat.shape[0])).reshape(nr, 1024))
    if extra is not None:
        rows.append(extra)
    used = sum(r.shape[0] for r in rows)
    rows.append(jnp.zeros((N_DEV * SMALL_ROWS - used, 1024), F32))
    return jnp.concatenate(rows, axis=0)


def _unpack_small(packed, shapes):
    out, off = {}, 0
    for k in SMALL_NAMES:
        size = math.prod(shapes[k])
        nr = _small_rows(size)
        out[k] = packed[off:off + nr].reshape(-1)[:size].reshape(shapes[k])
        off += nr
    return out


def _cols_to_blocks(full, width):
    r = full.shape[0]
    return jnp.transpose(full.reshape(r, N_DEV, width), (1, 0, 2))


def _blocks_to_cols(blocks):
    n, r, width = blocks.shape
    return jnp.transpose(blocks, (1, 0, 2)).reshape(r, n * width)


def kernel(x, g_pre_mix, w_in, b_forget, g_q, g_k, g_sgu, b_sgu, w_spatial, b_spatial, w_branch_a, w_branch_b, w_out, g_post_mix, g_pre_ffn, w_ffn_in, w_ffn_down, g_post_ffn, loss_target, m_g_pre_mix, m_w_in, m_b_forget, m_g_q, m_g_k, m_g_sgu, m_b_sgu, m_w_spatial, m_b_spatial, m_w_branch_a, m_w_branch_b, m_w_out, m_g_post_mix, m_g_pre_ffn, m_w_ffn_in, m_w_ffn_down, m_g_post_ffn, v_g_pre_mix, v_w_in, v_b_forget, v_g_q, v_g_k, v_g_sgu, v_b_sgu, v_w_spatial, v_b_spatial, v_w_branch_a, v_w_branch_b, v_w_out, v_g_post_mix, v_g_pre_ffn, v_w_ffn_in, v_w_ffn_down, v_g_post_ffn):
    big_names = ("w_in", "w_branch_a", "w_branch_b", "w_out", "w_ffn_in", "w_ffn_down")
    weights = dict(g_pre_mix=g_pre_mix, w_in=w_in, b_forget=b_forget, g_q=g_q, g_k=g_k, g_sgu=g_sgu,
                   b_sgu=b_sgu, w_spatial=w_spatial, b_spatial=b_spatial, w_branch_a=w_branch_a,
                   w_branch_b=w_branch_b, w_out=w_out, g_post_mix=g_post_mix, g_pre_ffn=g_pre_ffn,
                   w_ffn_in=w_ffn_in, w_ffn_down=w_ffn_down, g_post_ffn=g_post_ffn)
    mom1 = dict(g_pre_mix=m_g_pre_mix, w_in=m_w_in, b_forget=m_b_forget, g_q=m_g_q, g_k=m_g_k,
                g_sgu=m_g_sgu, b_sgu=m_b_sgu, w_spatial=m_w_spatial, b_spatial=m_b_spatial,
                w_branch_a=m_w_branch_a, w_branch_b=m_w_branch_b, w_out=m_w_out,
                g_post_mix=m_g_post_mix, g_pre_ffn=m_g_pre_ffn, w_ffn_in=m_w_ffn_in,
                w_ffn_down=m_w_ffn_down, g_post_ffn=m_g_post_ffn)
    mom2 = dict(g_pre_mix=v_g_pre_mix, w_in=v_w_in, b_forget=v_b_forget, g_q=v_g_q, g_k=v_g_k,
                g_sgu=v_g_sgu, b_sgu=v_b_sgu, w_spatial=v_w_spatial, b_spatial=v_b_spatial,
                w_branch_a=v_w_branch_a, w_branch_b=v_w_branch_b, w_out=v_w_out,
                g_post_mix=v_g_post_mix, g_pre_ffn=v_g_pre_ffn, w_ffn_in=v_w_ffn_in,
                w_ffn_down=v_w_ffn_down, g_post_ffn=v_g_post_ffn)
    names = list(weights)
    shapes = {k: weights[k].shape for k in names}

    s_len = x.shape[1]
    xs = x.reshape(s_len, D_MODEL)
    tgt = loss_target.reshape(s_len, D_MODEL)

    transposed = ("w_in", "w_ffn_in")

    def local_view(a, k):
        return jnp.transpose(a[0]) if k in transposed else a[0]

    shards = {k: local_view(weights[k], k).astype(BF16) for k in big_names}
    win_t = _gather_two_level(shards["w_in"], "gather_w_in").reshape(IN_COLS, D_MODEL)
    win_t, later = lax.optimization_barrier(
        (win_t, [shards[k] for k in big_names if k != "w_in"]))
    shards.update(zip([k for k in big_names if k != "w_in"], later))
    (gat_mix, gat_ffn), gat_token = _exchange_start(
        [[shards["w_branch_a"], shards["w_branch_b"], shards["w_out"]],
         [shards["w_ffn_in"], shards["w_ffn_down"]]], "gather_start", gather=True)
    f_off = 3 * FOX_W
    u_off = f_off + HEADS
    wcat = jnp.concatenate([
        win_t[:f_off], jnp.pad(win_t[f_off:u_off], ((0, 128 - HEADS), (0, 0))), win_t[u_off:]],
        axis=0)

    seg = np.arange(FOX_W) // HEAD_DIM
    bdiag = jnp.asarray(seg[:128, None] == seg[None, :128], BF16)
    tm = TOKEN_TILE
    lower = np.arange(tm)[None, :] <= np.arange(tm)[:, None]
    tril = jnp.asarray(lower, BF16)
    triu = jnp.asarray(lower.T, BF16)
    egrp = jnp.asarray(seg[:, None] == np.arange(128)[None, :], BF16)
    efold = jnp.asarray((np.arange(FOX_W) % HEAD_DIM)[:, None] == np.arange(128)[None, :], BF16)
    gq512 = jnp.tile(g_q.reshape(1, HEAD_DIM), (1, HEADS))
    gk512 = jnp.tile(g_k.reshape(1, HEAD_DIM), (1, HEADS))
    bfor = jnp.pad(b_forget.reshape(1, HEADS), ((0, 0), (0, 128 - HEADS)))
    pos = np.arange(WINDOW)
    wmask = (pos[None, :] // CHUNK) <= (pos[:, None] // CHUNK)
    wsm_f = jnp.where(jnp.asarray(wmask)[None], w_spatial[0], 0.0)
    wsm = wsm_f.astype(BF16)
    wsmt = jnp.transpose(wsm_f, (0, 2, 1)).astype(BF16)
    bsf = jnp.repeat(jnp.transpose(b_spatial[0]), HEAD_DIM, axis=1)
    wmask_f = jnp.asarray(wmask, F32)

    col = np.arange(SLAB_W)
    row128 = np.arange(128)

    def d_place(first):
        return jnp.asarray(np.stack([(col[None, :] // 128 == row128[:, None])
                                     & (col[None, :] % 128 == first + a) for a in range(3)]), BF16)

    pdq, pdk = d_place(HEAD_DIM), d_place(HEAD_DIM + 3)
    ones_q = jnp.asarray((col % 128 >= HEAD_DIM + 3) & (col % 128 < HEAD_DIM + 6), F32)[None]
    ones_k = jnp.asarray((col % 128 >= HEAD_DIM) & (col % 128 < HEAD_DIM + 3), F32)[None]
    ecol = jnp.asarray((col[:, None] // 128 == row128[None, :])
                       & (col[:, None] % 128 == HEAD_DIM + 3), BF16)

    (h, qa, ka, kat, vs, vt, qraw, kraw, flog, uvpre, gpre) = _proj_fwd(
        xs, g_pre_mix + gat_token[0:1, 0:1], wcat, bdiag, gq512, gk512, bfor, tril, pdq, pdk,
        ones_q, ones_k)
    attn, attn_t, lse = _attn_fwd(qa, ka, vt)
    (own_a, own_b, own_out), (zone_a, zone_b, zone_out) = _exchange_wait(
        gat_mix, attn, "gather_wait_mix", gather=True)
    wa = _blocks_to_cols(_own_block(zone_a, own_a))
    wb = _blocks_to_cols(_own_block(zone_b, own_b))
    wout = _own_block(zone_out, own_out).reshape(D_MODEL, D_MODEL)
    sgu_t, ya, yb, merged_t, om, x1 = _mix_fwd(attn, uvpre, gpre, xs, wa, wb, wout, wsm, bsf,
                                           g_sgu, b_sgu, g_post_mix)
    (own_ffn, own_down), (zone_ffn, zone_down) = _exchange_wait(
        gat_ffn, x1, "gather_wait_ffn", gather=True)
    wffn = _own_block(zone_ffn, own_ffn).reshape(2 * D_FF, D_MODEL)
    wdown = _own_block(zone_down, own_down).reshape(D_FF, D_MODEL)
    (dx1, h2, act_t, dff, dgu_t, loss_acc, dg_post_ffn, dg_pre_ffn) = _ffn_fwd_bwd(
        x1, tgt, wffn, wdown, g_pre_ffn, g_post_ffn)

    dw_down = _dw_matmul(act_t, dff, D_FF // 4, "dw_down")
    dw_ffn = _dw_matmul(dgu_t, h2, 2 * D_FF // N_DEV, "dw_ffn_in")
    x_pos, y_pos, c_pos = _mesh_pos()
    me = 4 * x_pos + 2 * y_pos + c_pos

    def own_of(parts):
        return [lax.dynamic_index_in_dim(p, me, 0, keepdims=False) for p in parts]

    parts_ffn = [dw_ffn.reshape(N_DEV, 2 * D_FF // N_DEV, D_MODEL),
                 dw_down.reshape(N_DEV, D_FF // N_DEV, D_MODEL)]
    mine_ffn = own_of(parts_ffn)
    (sct_ffn,), sct_ffn_token = _exchange_start([parts_ffn], "scatter_start_ffn", gather=False)

    (dom, dya, dyb, dgp, dot_, delta, duv, dws, dbs, dg_sgu, db_sgu, dg_post_mix) = _mix_bwd(
        dx1, om, ya, yb, gpre, uvpre, attn, wout, wa, wb, wsm, wsmt, bsf, g_sgu, b_sgu,
        g_post_mix + sct_ffn_token[0:1, 0:1], wmask_f, egrp)
    dw_out = _dw_matmul(merged_t, dom, 512, "dw_out")
    dw_a = _dw_matmul(attn_t, dya, 512, "dw_a")
    dw_b = _dw_matmul(sgu_t, dyb, 512, "dw_b")
    parts_mix = [_cols_to_blocks(dw_a, D_MODEL // N_DEV), _cols_to_blocks(dw_b, D_MODEL // N_DEV),
                 dw_out.reshape(N_DEV, D_MODEL // N_DEV, D_MODEL)]
    mine_mix = own_of(parts_mix)
    (sct_mix,), sct_mix_token = _exchange_start([parts_mix], "scatter_start_mix", gather=False)

    gk_all, dvt, gqt, col_sums = _attn_bwd(qa, ka, kat, vs, dot_, lse,
                                           delta + sct_mix_token[0, 0], ecol)
    dlogf = _rev_cumsum(col_sums, gqt, triu)
    dx, dproj_t, dgq, dgk, dbf, dg_pre_mix = _proj_bwd(
        gqt, gk_all, dvt, dlogf, flog, qraw, kraw, duv, dgp, xs, dx1, wcat, bdiag, gq512, gk512,
        g_pre_mix, efold)

    small_local = dict(
        g_pre_mix=dg_pre_mix, b_forget=dbf[:, :HEADS], g_q=dgq[0:1, :HEAD_DIM],
        g_k=dgk[0:1, :HEAD_DIM], g_sgu=dg_sgu, b_sgu=db_sgu, w_spatial=dws,
        b_spatial=jnp.transpose(dbs[:, :GROUPS]), g_post_mix=dg_post_mix, g_pre_ffn=dg_pre_ffn,
        g_post_ffn=dg_post_ffn)
    loss_row = jnp.pad(loss_acc[0:1, 0:1], ((0, 0), (0, 1023)))
    small_parts = _pack_small(small_local, loss_row).reshape(N_DEV, SMALL_ROWS, 1024)

    def with_own(zones, own_blocks):
        return [_own_block(z, b) for z, b in zip(zones, own_blocks)]

    mine_small = own_of([small_parts])
    (sct_small,), sct_small_token = _exchange_start([[small_parts]], "scatter_start_small",
                                                    gather=False)
    dw_cat = _dw_matmul(dproj_t, h, C_END // N_DEV, "dw_in", after=(sct_small_token,))
    dw_in = jnp.concatenate([dw_cat[:C_F + HEADS], dw_cat[C_UV:]], axis=0)
    (recv_small,) = with_own(
        _exchange_wait(sct_small, dw_in, "scatter_wait_small", gather=False)[1], mine_small)
    small_sum = _sum_parts(recv_small, "sum_small")
    (gat_small,), gat_small_token = _exchange_start([[small_sum]], "gather_start_small",
                                                    gather=True)
    pair_blocks, own_pair = _pair_sums(dw_in.reshape(N_DEV, BLK, D_MODEL), "pair_sums_in",
                                       gat_small_token)
    rs_in, rs_token = _chip_exchange_start(pair_blocks, "chip_exchange_start_in")

    recv_ffn, recv_down = with_own(
        _exchange_wait(sct_ffn, rs_token, "scatter_wait_ffn", gather=False)[1], mine_ffn)
    recv_a, recv_b, recv_out = with_own(
        _exchange_wait(sct_mix, recv_ffn, "scatter_wait_mix", gather=False)[1], mine_mix)
    received = [None, recv_a, recv_b, recv_out, recv_ffn, recv_down]

    grads, deltas, new_m, new_v = {}, {}, {}, {}
    row_tiles = {"w_in": None, "w_branch_a": 512, "w_branch_b": 512, "w_out": 128, "w_ffn_in": 176,
                 "w_ffn_down": 352}

    def update(k, parts):
        outs = _adamw(parts, local_view(weights[k], k), local_view(mom1[k], k),
                      local_view(mom2[k], k), row_tiles[k], "adamw_" + k,
                      col_tile=256 if k == "w_in" else None)
        if k in transposed:
            outs = [jnp.transpose(o) for o in outs]
        grads[k], deltas[k], new_m[k], new_v[k] = [o[None] for o in outs]
        return outs[0]

    last = None
    for idx, k in enumerate(big_names):
        if k != "w_in":
            last = update(k, received[idx])

    (own_small,), (zone_small,) = _exchange_wait(gat_small, last, "gather_wait_small", gather=True)
    small_all = _own_block(zone_small, own_small).reshape(1, N_DEV * SMALL_ROWS, 1024)
    sg, sd, sm, sv = _adamw(small_all, _pack_small(weights), _pack_small(mom1), _pack_small(mom2),
                            N_DEV * SMALL_ROWS, "adamw_small")
    for dst, packed in ((grads, sg), (deltas, sd), (new_m, sm), (new_v, sv)):
        dst.update(_unpack_small(packed, shapes))
    arrived = _chip_exchange_wait(rs_in, sg, "chip_exchange_wait_in")
    update("w_in", [own_pair[None], arrived])

    loss = small_all[0, sum(_small_rows(math.prod(shapes[k])) for k in SMALL_NAMES), 0]
    return (loss, dx.reshape(x.shape), *[grads[k] for k in names], *[deltas[k] for k in names],
            *[new_m[k] for k in names], *[new_v[k] for k in names])
```

```python
import functools
import math

import jax
import jax.numpy as jnp
import numpy as np
from jax import lax
from jax.experimental import pallas as pl
from jax.experimental.pallas import tpu as pltpu

F32 = jnp.float32
BF16 = jnp.bfloat16

D_MODEL = 1024
FOX_W = 512
HEADS = 8
HEAD_DIM = 64
SGU_W = 512
GROUPS = 8
WINDOW = 128
CHUNK = 64
D_FF = 2816
IN_COLS = 4616
EPS = 1e-6
N_DEV = 8
LOG2E = 1.4426950408889634
LN2 = 0.6931471805599453

C_Q, C_K, C_V, C_F, C_UV, C_G, C_END = 0, 512, 1024, 1536, 1664, 2688, 4736

ADAM_LR, ADAM_B1, ADAM_B2, ADAM_EPS, ADAM_WD, ADAM_STEP = 0.001, 0.9, 0.999, 1e-08, 0.01, 10

MIB = 1024 * 1024
TOKEN_TILE = 256
ATTN_TILE = 256
SLAB_W = HEADS * 128
QT_ROWS = 72

SMALL_ROWS = 18
BLK = IN_COLS // N_DEV


def _params(vmem_mib, n_axes):
    return pltpu.CompilerParams(
        dimension_semantics=("arbitrary",) * n_axes, vmem_limit_bytes=vmem_mib * MIB)


def _const_spec(shape):
    nd = len(shape)
    return pl.BlockSpec(shape, lambda *_: (0,) * nd)


def _row_spec(tm, cols):
    return pl.BlockSpec((tm, cols), lambda i: (i, 0))


def _tile_spec(rows, tm):
    return pl.BlockSpec((1, rows, tm), lambda i: (i, 0, 0))


def _split3_dot(x, e):
    x1 = x.astype(BF16)
    r1 = x - x1.astype(F32)
    x2 = r1.astype(BF16)
    x3 = (r1 - x2.astype(F32)).astype(BF16)
    dot = functools.partial(jnp.dot, preferred_element_type=F32)
    return dot(x1, e) + dot(x2, e) + dot(x3, e)


def _tri_dot(tri, x):
    x1 = x.astype(BF16)
    r1 = x - x1.astype(F32)
    x2 = r1.astype(BF16)
    x3 = (r1 - x2.astype(F32)).astype(BF16)
    dot = functools.partial(jnp.dot, preferred_element_type=F32)
    return dot(tri, x1) + dot(tri, x2) + dot(tri, x3)


def _seg_mean(sq, bd_ref):
    hi = sq.astype(BF16)
    lo = (sq - hi.astype(F32)).astype(BF16)
    bd = bd_ref[...]
    dot = functools.partial(jnp.dot, preferred_element_type=F32)
    pairs = [dot(hi[:, p * 128:(p + 1) * 128], bd) + dot(lo[:, p * 128:(p + 1) * 128], bd)
             for p in range(HEADS // 2)]
    return jnp.concatenate(pairs, axis=1) * (1.0 / HEAD_DIM)


def _slabs_from_heads(t):
    lane = lax.broadcasted_iota(jnp.int32, (t.shape[0], 128), 1)
    low = lane < HEAD_DIM
    slabs = []
    for p in range(HEADS // 2):
        pair = t[:, p * 128:(p + 1) * 128]
        slabs.append(jnp.where(low, pair, 0.0))
        slabs.append(jnp.where(low, pltpu.roll(pair, HEAD_DIM, 1), 0.0))
    return jnp.concatenate(slabs, axis=1)


def _dot_nt(a, b):
    return lax.dot_general(a, b, (((1,), (1,)), ((), ())), preferred_element_type=F32)


def _dot_tn(a, b):
    return lax.dot_general(a, b, (((0,), (0,)), ((), ())), preferred_element_type=F32)


def _sigmoid(x):
    return 0.5 * jnp.tanh(0.5 * x) + 0.5


_GELU_C = math.sqrt(2.0 / math.pi)


def _gelu_and_grad(x):
    inner = _GELU_C * (x + 0.044715 * x * x * x)
    t = jnp.tanh(inner)
    y = 0.5 * x * (1.0 + t)
    dy = 0.5 * (1.0 + t) + 0.5 * x * (1.0 - t * t) * _GELU_C * (1.0 + 3.0 * 0.044715 * x * x)
    return y, dy


def _rms_bwd(xin, r, g, dy):
    dyg = dy * g
    return r * dyg - xin * (r * r * r) * jnp.mean(dyg * xin, axis=-1, keepdims=True)


def _mesh_pos():
    x, y, c = lax.axis_index("x"), lax.axis_index("y"), lax.axis_index("c")
    return x, y, c


def _peer(k):
    x, y, c = _mesh_pos()
    px = (1 - x) if (k >> 2) & 1 else x
    py = (1 - y) if (k >> 1) & 1 else y
    pc = (1 - c) if k & 1 else c
    return (px, py, pc), 4 * px + 2 * py + pc


def _gather_two_level(shard, name):
    def body(x_ref, out_ref, send_sems, recv_sems, local_sem):
        x, y, c = _mesh_pos()
        me, sibling = (x, y, c), (x, y, 1 - c)
        chips = [(1 - x, y), (x, 1 - y), (1 - x, 1 - y)]

        def slot(px, py, pc):
            return out_ref.at[4 * px + 2 * py + pc]

        def copy(k, block, to, src=None):
            return pltpu.make_async_remote_copy(
                src_ref=slot(*block) if src is None else src, dst_ref=slot(*block),
                send_sem=send_sems.at[k], recv_sem=recv_sems.at[k],
                device_id=to, device_id_type=pl.DeviceIdType.MESH)

        mine = pltpu.make_async_copy(x_ref, slot(*me), local_sem)
        mine.start()
        first = [copy(1 + j, me, (*chip, c), src=x_ref) for j, chip in enumerate(chips)]
        first.append(copy(0, me, sibling, src=x_ref))
        for cp in first:
            cp.start()
        passed = [copy(4 + j, (*chip, c), sibling) for j, chip in enumerate(chips)]
        for j, chip in enumerate(chips):
            copy(1 + j, (*chip, c), me).wait_recv()
            passed[j].start()
        copy(0, sibling, me).wait_recv()
        for j, chip in enumerate(chips):
            copy(4 + j, (*chip, 1 - c), me).wait_recv()
        for cp in first + passed:
            cp.wait_send()
        mine.wait()

    any_spec = pl.BlockSpec(memory_space=pl.ANY)
    return pl.pallas_call(
        body, name=name, out_shape=jax.ShapeDtypeStruct((N_DEV,) + shard.shape, shard.dtype),
        in_specs=[any_spec], out_specs=any_spec,
        scratch_shapes=[pltpu.SemaphoreType.DMA((7,)), pltpu.SemaphoreType.DMA((7,)),
                        pltpu.SemaphoreType.DMA],
    )(shard)


def _chip_peer(k):
    x, y, c = _mesh_pos()
    px = (1 - x) if (k >> 1) & 1 else x
    py = (1 - y) if k & 1 else y
    return (px, py, c), 2 * px + py


def _pair_sums(parts, name, after):
    _, rows, cols = parts.shape
    n_chips = N_DEV // 2

    def body(p_ref, after_ref, send_ref, own_ref, mine_buf, sib_buf, send_sems, recv_sems,
             local_sems):
        x, y, c = _mesh_pos()
        sibling = (x, y, 1 - c)
        copies, local = [], []
        for q in range(n_chips):
            cp = pltpu.make_async_remote_copy(
                src_ref=p_ref.at[2 * q + (1 - c)], dst_ref=sib_buf.at[q],
                send_sem=send_sems.at[q], recv_sem=recv_sems.at[q],
                device_id=sibling, device_id_type=pl.DeviceIdType.MESH)
            cp.start()
            copies.append(cp)
            lc = pltpu.make_async_copy(p_ref.at[2 * q + c], mine_buf.at[q], local_sems.at[q])
            lc.start()
            local.append(lc)
        for lc in local:
            lc.wait()
        for cp in copies:
            cp.wait_recv()
        for k in range(1, n_chips):
            _, q = _chip_peer(k)
            send_ref[k - 1] = (mine_buf[q].astype(F32) + sib_buf[q].astype(F32)).astype(BF16)
        my_chip = 2 * x + y
        own_ref[...] = mine_buf[my_chip].astype(F32) + sib_buf[my_chip].astype(F32)
        for cp in copies:
            cp.wait_send()

    vmem = pl.BlockSpec(memory_space=pltpu.VMEM)
    return pl.pallas_call(
        body, name=name,
        out_shape=[jax.ShapeDtypeStruct((n_chips - 1, rows, cols), BF16),
                   jax.ShapeDtypeStruct((rows, cols), F32)],
        in_specs=[pl.BlockSpec(memory_space=pl.ANY)] * 2, out_specs=[vmem, vmem],
        scratch_shapes=[pltpu.VMEM((n_chips, rows, cols), BF16),
                        pltpu.VMEM((n_chips, rows, cols), BF16),
                        pltpu.SemaphoreType.DMA((n_chips,)), pltpu.SemaphoreType.DMA((n_chips,)),
                        pltpu.SemaphoreType.DMA((n_chips,))],
        compiler_params=pltpu.CompilerParams(vmem_limit_bytes=40 * MIB),
    )(parts, after)


def _chip_copy(src_ref, land_ref, send_sem, recv_sem, k):
    peer, _ = _chip_peer(k)
    return pltpu.make_async_remote_copy(
        src_ref=src_ref.at[k - 1], dst_ref=land_ref.at[k - 1], send_sem=send_sem, recv_sem=recv_sem,
        device_id=peer, device_id_type=pl.DeviceIdType.MESH)


def _chip_exchange_start(blocks, name):
    hbm = pl.BlockSpec(memory_space=pltpu.HBM)
    sem = pl.BlockSpec(memory_space=pltpu.SEMAPHORE)
    n_peers = blocks.shape[0]

    def body(src_ref, zone_ref, send_sems, recv_sems, src_thru, zone_thru, token):
        for k in range(1, n_peers + 1):
            _chip_copy(src_ref, zone_ref, send_sems.at[k - 1], recv_sems.at[k - 1], k).start()
        token[...] = jnp.zeros_like(token)

    outs = pl.pallas_call(
        body, name=name, in_specs=[hbm, hbm],
        out_shape=[pltpu.SemaphoreType.DMA((n_peers,)), pltpu.SemaphoreType.DMA((n_peers,)),
                   pltpu.HBM(blocks.shape, blocks.dtype), pltpu.HBM(blocks.shape, blocks.dtype),
                   jax.ShapeDtypeStruct((8, 128), F32)],
        out_specs=[sem, sem, hbm, hbm, pl.BlockSpec(memory_space=pltpu.VMEM)],
        input_output_aliases={0: 2, 1: 3},
        compiler_params=pltpu.CompilerParams(
            has_side_effects=pltpu.SideEffectType.DATAFLOW_SIDE_EFFECTING),
    )(pltpu.with_memory_space_constraint(blocks, pltpu.HBM),
      pltpu.with_memory_space_constraint(lax.empty(blocks.shape, blocks.dtype), pltpu.HBM))
    return outs[:4], outs[4]


def _chip_exchange_wait(handle, after, name):
    send_sems, recv_sems, src, zone = handle
    hbm = pl.BlockSpec(memory_space=pltpu.HBM)
    sem = pl.BlockSpec(memory_space=pltpu.SEMAPHORE)

    def body(src_ref, zone_ref, ssem, rsem, after_ref, src_out, zone_out):
        for k in range(1, src.shape[0] + 1):
            cp = _chip_copy(src_ref, zone_ref, ssem.at[k - 1], rsem.at[k - 1], k)
            cp.wait_send()
            cp.wait_recv()

    outs = pl.pallas_call(
        body, name=name,
        in_specs=[hbm, hbm, sem, sem, pl.BlockSpec(memory_space=pl.ANY)],
        out_shape=[pltpu.HBM(src.shape, src.dtype), pltpu.HBM(zone.shape, zone.dtype)],
        out_specs=[hbm, hbm], input_output_aliases={0: 0, 1: 1},
        compiler_params=pltpu.CompilerParams(
            has_side_effects=pltpu.SideEffectType.DATAFLOW_SIDE_EFFECTING),
    )(src, zone, send_sems, recv_sems, after)
    return outs[1]


def _remote_copy(gather, src_ref, land_ref, send_sem, recv_sem, k, receive_side):
    x, y, c = _mesh_pos()
    me = 4 * x + 2 * y + c
    peer, pidx = _peer(k)
    return pltpu.make_async_remote_copy(
        src_ref=src_ref if gather else src_ref.at[pidx],
        dst_ref=land_ref.at[pidx if receive_side else me],
        send_sem=send_sem, recv_sem=recv_sem,
        device_id=peer, device_id_type=pl.DeviceIdType.MESH)


def _exchange_start(groups, name, gather):
    arrs = [a for g in groups for a in g]
    n, n_groups = len(arrs), len(groups)
    lands = [jax.ShapeDtypeStruct(((N_DEV,) + a.shape) if gather else a.shape, a.dtype)
             for a in arrs]

    def body(*refs):
        srcs, zones = refs[:n], refs[n:2 * n]
        sems = refs[2 * n:2 * n + 2 * n_groups]
        token = refs[-1]
        a = 0
        for gi, g in enumerate(groups):
            send_sems, recv_sems = sems[2 * gi], sems[2 * gi + 1]
            for k in range(1, N_DEV):
                for ai in range(len(g)):
                    slot = ai * (N_DEV - 1) + k - 1
                    _remote_copy(gather, srcs[a + ai], zones[a + ai], send_sems.at[slot],
                                 recv_sems.at[slot], k, False).start()
            a += len(g)
        token[...] = jnp.zeros_like(token)

    hbm = pl.BlockSpec(memory_space=pltpu.HBM)
    sem = pl.BlockSpec(memory_space=pltpu.SEMAPHORE)
    sem_shapes = []
    for g in groups:
        sem_shapes += [pltpu.SemaphoreType.DMA((len(g) * (N_DEV - 1),))] * 2
    outs = pl.pallas_call(
        body, name=name,
        in_specs=[hbm] * (2 * n),
        out_shape=sem_shapes + [pltpu.HBM(a.shape, a.dtype) for a in arrs]
        + [pltpu.HBM(z.shape, z.dtype) for z in lands] + [jax.ShapeDtypeStruct((8, 128), F32)],
        out_specs=[sem] * (2 * n_groups) + [hbm] * (2 * n)
        + [pl.BlockSpec(memory_space=pltpu.VMEM)],
        input_output_aliases={i: 2 * n_groups + i for i in range(2 * n)},
        compiler_params=pltpu.CompilerParams(
            has_side_effects=pltpu.SideEffectType.DATAFLOW_SIDE_EFFECTING),
    )(*[pltpu.with_memory_space_constraint(a, pltpu.HBM) for a in arrs],
      *[pltpu.with_memory_space_constraint(lax.empty(z.shape, z.dtype), pltpu.HBM) for z in lands])
    sems = outs[:2 * n_groups]
    thru = outs[2 * n_groups:2 * n_groups + n]
    zones = outs[2 * n_groups + n:2 * n_groups + 2 * n]
    handles, a = [], 0
    for gi, g in enumerate(groups):
        handles.append((sems[2 * gi], sems[2 * gi + 1], thru[a:a + len(g)], zones[a:a + len(g)]))
        a += len(g)
    return handles, outs[-1]


def _exchange_wait(handle, after, name, gather):
    send_sems, recv_sems, thru, zones = handle
    n = len(thru)

    def body(*refs):
        srcs, lands = refs[:n], refs[n:2 * n]
        ssem, rsem = refs[2 * n], refs[2 * n + 1]
        for k in range(1, N_DEV):
            for ai in range(n):
                slot = ai * (N_DEV - 1) + k - 1
                cp = _remote_copy(gather, srcs[ai], lands[ai], ssem.at[slot], rsem.at[slot], k, True)
                cp.wait_send()
                cp.wait_recv()

    hbm = pl.BlockSpec(memory_space=pltpu.HBM)
    sem = pl.BlockSpec(memory_space=pltpu.SEMAPHORE)
    outs = pl.pallas_call(
        body, name=name,
        in_specs=[hbm] * (2 * n) + [sem, sem, pl.BlockSpec(memory_space=pl.ANY)],
        out_shape=[pltpu.HBM(a.shape, a.dtype) for a in thru]
        + [pltpu.HBM(z.shape, z.dtype) for z in zones],
        out_specs=[hbm] * (2 * n),
        input_output_aliases={i: i for i in range(2 * n)},
        compiler_params=pltpu.CompilerParams(
            has_side_effects=pltpu.SideEffectType.DATAFLOW_SIDE_EFFECTING),
    )(*thru, *zones, send_sems, recv_sems, after)
    return outs[:n], outs[n:]


def _own_block(zone, block):
    x, y, c = _mesh_pos()
    me = 4 * x + 2 * y + c
    return lax.dynamic_update_slice_in_dim(zone, block[None], me, axis=0)


def _proj_fwd(x, g1, wcat, bdiag, gq, gk, bfor, tri, pdq, pdk, ones_q, ones_k):
    s_len = x.shape[0]
    tm = TOKEN_TILE
    nt = s_len // tm

    def body(x_ref, g1_ref, w_ref, bd_ref, gq_ref, gk_ref, bf_ref, tri_ref, pdq_ref,
             pdk_ref, oq_ref, ok_ref,
             h_ref, qa_ref, ka_ref, kat_ref, vs_ref, vt_ref, qr_ref, kr_ref, flog_ref, uv_ref,
             gp_ref, carry):
        @pl.when(pl.program_id(0) == 0)
        def _():
            carry[...] = jnp.zeros_like(carry)

        xf = x_ref[...]
        r = lax.rsqrt(jnp.mean(xf * xf, axis=-1, keepdims=True) + EPS)
        h = (xf * r * g1_ref[...]).astype(BF16)
        h_ref[...] = h
        dot = functools.partial(jnp.dot, preferred_element_type=F32)

        def proj(lo, hi):
            return _dot_nt(h, w_ref[lo:hi, :])

        flog = proj(C_F, C_UV) + bf_ref[...]
        flog_ref[...] = flog
        lane = lax.broadcasted_iota(jnp.int32, flog.shape, 1)
        logf = jnp.minimum(flog, 0.0) - jnp.log(1.0 + jnp.exp(-jnp.abs(flog)))
        logf = jnp.where(lane < HEADS, logf, 0.0)
        dcum = _tri_dot(tri_ref[...], logf) + carry[...]
        carry[...] = dcum[tm - 1:tm, :]
        d2 = dcum * LOG2E
        d2a = d2.astype(BF16)
        rem = d2 - d2a.astype(F32)
        d2b = rem.astype(BF16)
        d2c = (rem - d2b.astype(F32)).astype(BF16)

        q = proj(C_Q, C_K)
        qr_ref[...] = q.astype(BF16)
        rq = lax.rsqrt(_seg_mean(q * q, bd_ref) + EPS)
        qn = q * rq * (gq_ref[...] * (HEAD_DIM ** -0.5 * LOG2E))
        qa = (_slabs_from_heads(qn) + dot(d2a, pdq_ref[0]) + dot(d2b, pdq_ref[1])
              + dot(d2c, pdq_ref[2]) + oq_ref[...])
        qa_ref[...] = qa.astype(BF16)

        k = proj(C_K, C_V)
        kr_ref[...] = k.astype(BF16)
        rk = lax.rsqrt(_seg_mean(k * k, bd_ref) + EPS)
        kn = k * rk * gk_ref[...]
        ka = (_slabs_from_heads(kn) - dot(d2a, pdk_ref[0]) - dot(d2b, pdk_ref[1])
              - dot(d2c, pdk_ref[2]) + ok_ref[...])
        ka_ref[...] = ka.astype(BF16)
        kat_ref[0] = ka.T.astype(BF16)

        v = proj(C_V, C_F)
        vs_ref[...] = _slabs_from_heads(v).astype(BF16)
        vt_ref[0] = v.T.astype(BF16)
        uv_ref[...] = proj(C_UV, C_G).astype(BF16)
        gp_ref[...] = proj(C_G, C_END).astype(BF16)

    outs = [((s_len, D_MODEL), BF16, _row_spec(tm, D_MODEL)),
            ((s_len, SLAB_W), BF16, _row_spec(tm, SLAB_W)),
            ((s_len, SLAB_W), BF16, _row_spec(tm, SLAB_W)),
            ((nt, SLAB_W, tm), BF16, _tile_spec(SLAB_W, tm)),
            ((s_len, SLAB_W), BF16, _row_spec(tm, SLAB_W)),
            ((nt, FOX_W, tm), BF16, _tile_spec(FOX_W, tm)),
            ((s_len, FOX_W), BF16, _row_spec(tm, FOX_W)),
            ((s_len, FOX_W), BF16, _row_spec(tm, FOX_W)),
            ((s_len, 128), F32, _row_spec(tm, 128)),
            ((s_len, 2 * SGU_W), BF16, _row_spec(tm, 2 * SGU_W)),
            ((s_len, 2 * D_MODEL), BF16, _row_spec(tm, 2 * D_MODEL))]
    return pl.pallas_call(
        body, name="proj_fwd", grid=(nt,),
        in_specs=[_row_spec(tm, D_MODEL), _const_spec((1, D_MODEL)), _const_spec(wcat.shape),
                  _const_spec(bdiag.shape), _const_spec((1, FOX_W)), _const_spec((1, FOX_W)),
                  _const_spec((1, 128)), _const_spec((tm, tm)), _const_spec(pdq.shape), _const_spec(pdk.shape), _const_spec(ones_q.shape),
                  _const_spec(ones_k.shape)],
        out_specs=[o[2] for o in outs],
        out_shape=[jax.ShapeDtypeStruct(o[0], o[1]) for o in outs],
        scratch_shapes=[pltpu.VMEM((1, 128), F32)],
        compiler_params=_params(56, 1),
    )(x, g1, wcat, bdiag, gq, gk, bfor, tri, pdq, pdk, ones_q, ones_k)


def _attn_fwd(qa, ka, vt):
    s_len = qa.shape[0]
    t = ATTN_TILE
    nb = s_len // t

    def body(q_ref, k_ref, vt_ref, o_ref, ot_ref, lse_ref, m_sc, l_sc, acc_sc, s_sc, mcur_sc,
             alpha_sc):
        i = pl.program_id(0)
        m_sc[...] = jnp.full_like(m_sc, -jnp.inf)
        l_sc[...] = jnp.zeros_like(l_sc)
        acc_sc[...] = jnp.zeros_like(acc_sc)

        def logits(j, slot, masked):
            krows = pl.ds(pl.multiple_of(j * t, t), t)
            if masked:
                keep = (lax.broadcasted_iota(jnp.int32, (t, t), 0)
                        <= lax.broadcasted_iota(jnp.int32, (t, t), 1))
            for hd in range(HEADS):
                sl = slice(hd * 128, (hd + 1) * 128)
                st = _dot_nt(k_ref[krows, sl], q_ref[:, sl])
                if masked:
                    st = jnp.where(keep, st, -jnp.inf)
                s_sc[slot, hd] = st
                m_prev = m_sc[hd:hd + 1, :]
                m_new = jnp.maximum(m_prev, jnp.max(st, axis=0, keepdims=True))
                alpha_sc[slot, hd:hd + 1, :] = jnp.exp2(m_prev - m_new)
                mcur_sc[slot, hd:hd + 1, :] = m_new
                m_sc[hd:hd + 1, :] = m_new

        def accumulate(j, slot):
            for hd in range(HEADS):
                hr = slice(hd * HEAD_DIM, (hd + 1) * HEAD_DIM)
                alpha = alpha_sc[slot, hd:hd + 1, :]
                pt = jnp.exp2(s_sc[slot, hd] - mcur_sc[slot, hd:hd + 1, :])
                l_sc[hd:hd + 1, :] = alpha * l_sc[hd:hd + 1, :] + jnp.sum(pt, axis=0, keepdims=True)
                acc_sc[hr, :] = alpha * acc_sc[hr, :] + jnp.dot(
                    vt_ref[j, hr, :], pt.astype(BF16), preferred_element_type=F32)

        @pl.when(i == 0)
        def _():
            logits(0, 0, True)
            accumulate(0, 0)

        pairs = (i - 1) // 2

        @pl.when(i > 0)
        def _():
            logits(0, 0, False)

            def two_blocks(p, carry):
                logits(2 * p + 1, 1, False)
                accumulate(2 * p, 0)
                logits(2 * p + 2, 0, False)
                accumulate(2 * p + 1, 1)
                return carry

            lax.fori_loop(0, pairs, two_blocks, 0)

        @pl.when((i > 0) & (i - 2 * pairs == 1))
        def _():
            logits(i, 1, True)
            accumulate(i - 1, 0)
            accumulate(i, 1)

        @pl.when((i > 0) & (i - 2 * pairs == 2))
        def _():
            logits(i - 1, 1, False)
            accumulate(i - 2, 0)
            logits(i, 0, True)
            accumulate(i - 1, 1)
            accumulate(i, 0)

        for hd in range(HEADS):
            hr = slice(hd * HEAD_DIM, (hd + 1) * HEAD_DIM)
            l = l_sc[hd:hd + 1, :]
            acc_sc[hr, :] = acc_sc[hr, :] / l
            lse_ref[0, hd:hd + 1, :] = m_sc[hd:hd + 1, :] + jnp.log2(l)
        o_ref[...] = acc_sc[...].T.astype(BF16)
        ot_ref[...] = acc_sc[...].astype(BF16)

    return pl.pallas_call(
        body, name="attn_fwd", grid=(nb,),
        in_specs=[_row_spec(t, SLAB_W), _const_spec(ka.shape), _const_spec(vt.shape)],
        out_specs=[_row_spec(t, FOX_W), pl.BlockSpec((FOX_W, t), lambda i: (0, i)),
                   _tile_spec(HEADS, t)],
        out_shape=[jax.ShapeDtypeStruct((s_len, FOX_W), BF16),
                   jax.ShapeDtypeStruct((FOX_W, s_len), BF16),
                   jax.ShapeDtypeStruct((nb, HEADS, t), F32)],
        scratch_shapes=[pltpu.VMEM((HEADS, t), F32), pltpu.VMEM((HEADS, t), F32),
                        pltpu.VMEM((FOX_W, t), F32), pltpu.VMEM((2, HEADS, t, t), F32),
                        pltpu.VMEM((2, HEADS, t), F32), pltpu.VMEM((2, HEADS, t), F32)],
        compiler_params=_params(48, 1),
    )(qa, ka, vt)


def _sgu_mix(vn, ws_ref):
    tm = vn.shape[0]
    lane = lax.broadcasted_iota(jnp.int32, (WINDOW, 128), 1)
    low = lane < HEAD_DIM
    wins = []
    for w in range(tm // WINDOW):
        slabs = []
        for p in range(GROUPS // 2):
            v2 = vn[w * WINDOW:(w + 1) * WINDOW, p * 128:(p + 1) * 128]
            lo = jnp.where(low, v2, 0.0).astype(BF16)
            hi = jnp.where(low, 0.0, v2).astype(BF16)
            slabs.append(jnp.dot(ws_ref[2 * p], lo, preferred_element_type=F32)
                         + jnp.dot(ws_ref[2 * p + 1], hi, preferred_element_type=F32))
        wins.append(jnp.concatenate(slabs, axis=1))
    return jnp.concatenate(wins, axis=0) if len(wins) > 1 else wins[0]


def _layernorm_fwd(vv, g, b):
    mu = jnp.mean(vv, axis=-1, keepdims=True)
    xc = vv - mu
    r = lax.rsqrt(jnp.mean(xc * xc, axis=-1, keepdims=True) + EPS)
    xh = xc * r
    return xh * g + b, xh, r


def _mix_fwd(attn, uvpre, gpre, x, wa, wb, wout, wsm, bsf, gsgu, bsgu, gpost):
    s_len = x.shape[0]
    tm = TOKEN_TILE

    def body(o_ref, uv_ref, gp_ref, x_ref, wa_ref, wb_ref, wo_ref, ws_ref, bs_ref, gs_ref, bsg_ref,
             gpost_ref, sgut_ref, ya_ref, yb_ref, mgt_ref, om_ref, x1_ref):
        uvp = uv_ref[...].astype(F32)
        uv, _ = _gelu_and_grad(uvp)
        u, vv = uv[:, :SGU_W], uv[:, SGU_W:]
        vn, _, _ = _layernorm_fwd(vv, gs_ref[...], bsg_ref[...])
        bias = bs_ref[...]
        if tm > WINDOW:
            bias = jnp.concatenate([bias] * (tm // WINDOW), axis=0)
        mixed = _sgu_mix(vn, ws_ref) + bias
        sgu_f = u * mixed
        sgu = sgu_f.astype(BF16)
        sgut_ref[...] = sgu_f.T.astype(BF16)
        ya = jnp.dot(o_ref[...], wa_ref[...], preferred_element_type=F32)
        yb = jnp.dot(sgu, wb_ref[...], preferred_element_type=F32)
        ya_ref[...] = ya.astype(BF16)
        yb_ref[...] = yb.astype(BF16)
        gates = _sigmoid(gp_ref[...].astype(F32))
        merged_f = gates[:, :D_MODEL] * ya + gates[:, D_MODEL:] * yb
        merged = merged_f.astype(BF16)
        mgt_ref[...] = merged_f.T.astype(BF16)
        om = jnp.dot(merged, wo_ref[...], preferred_element_type=F32)
        om_ref[...] = om
        r = lax.rsqrt(jnp.mean(om * om, axis=-1, keepdims=True) + EPS)
        x1_ref[...] = x_ref[...] + om * r * gpost_ref[...]

    def t_out(rows):
        return ((rows, s_len), BF16, pl.BlockSpec((rows, tm), lambda i: (0, i)))

    def r_out(cols, dt):
        return ((s_len, cols), dt, _row_spec(tm, cols))

    outs = [t_out(SGU_W), r_out(D_MODEL, BF16), r_out(D_MODEL, BF16), t_out(D_MODEL),
            r_out(D_MODEL, F32), r_out(D_MODEL, F32)]
    return pl.pallas_call(
        body, name="mix_fwd", grid=(s_len // tm,),
        in_specs=[_row_spec(tm, FOX_W), _row_spec(tm, 2 * SGU_W), _row_spec(tm, 2 * D_MODEL),
                  _row_spec(tm, D_MODEL), _const_spec(wa.shape), _const_spec(wb.shape),
                  _const_spec(wout.shape), _const_spec(wsm.shape), _const_spec(bsf.shape),
                  _const_spec((1, SGU_W)), _const_spec((1, SGU_W)), _const_spec((1, D_MODEL))],
        out_specs=[o[2] for o in outs],
        out_shape=[jax.ShapeDtypeStruct(o[0], o[1]) for o in outs],
        compiler_params=_params(48, 1),
    )(attn, uvpre, gpre, x, wa, wb, wout, wsm, bsf, gsgu, bsgu, gpost)


def _ffn_fwd_bwd(x1, tgt, wffn, wdown, gpre, gpost):
    s_len = x1.shape[0]
    tm = TOKEN_TILE

    def body(x1_ref, t_ref, wi_ref, wd_ref, gpre_ref, gpost_ref,
             dx1_ref, h2_ref, actt_ref, dff_ref, dgut_ref, loss_ref, dgpost_ref, dgpre_ref):
        @pl.when(pl.program_id(0) == 0)
        def _():
            loss_ref[...] = jnp.zeros_like(loss_ref)
            dgpost_ref[...] = jnp.zeros_like(dgpost_ref)
            dgpre_ref[...] = jnp.zeros_like(dgpre_ref)

        x1v = x1_ref[...]
        r2 = lax.rsqrt(jnp.mean(x1v * x1v, axis=-1, keepdims=True) + EPS)
        gpre_v = gpre_ref[...]
        h2 = (x1v * r2 * gpre_v).astype(BF16)
        h2_ref[...] = h2
        gg = _dot_nt(h2, wi_ref[:D_FF, :])
        uu = _dot_nt(h2, wi_ref[D_FF:, :])
        sg = _sigmoid(gg)
        silu = gg * sg
        act_f = silu * uu
        act = act_f.astype(BF16)
        actt_ref[...] = act_f.T.astype(BF16)
        ff = jnp.dot(act, wd_ref[...], preferred_element_type=F32)
        r3 = lax.rsqrt(jnp.mean(ff * ff, axis=-1, keepdims=True) + EPS)
        gpost_v = gpost_ref[...]
        y = x1v + ff * r3 * gpost_v
        err = y - t_ref[...]
        loss_ref[...] += jnp.sum(err * err) * (0.5 / D_MODEL)
        dy = err * (1.0 / D_MODEL)
        dgpost_ref[...] += jnp.sum(dy * ff * r3, axis=0, keepdims=True)
        dff = _rms_bwd(ff, r3, gpost_v, dy).astype(BF16)
        dff_ref[...] = dff
        dact = _dot_nt(dff, wd_ref[...])
        dgg_f = dact * uu * (sg * (1.0 + gg * (1.0 - sg)))
        duu_f = dact * silu
        dgg = dgg_f.astype(BF16)
        duu = duu_f.astype(BF16)
        dgut_ref[:D_FF, :] = dgg_f.T.astype(BF16)
        dgut_ref[D_FF:, :] = duu_f.T.astype(BF16)
        dh2 = (jnp.dot(dgg, wi_ref[:D_FF, :], preferred_element_type=F32)
               + jnp.dot(duu, wi_ref[D_FF:, :], preferred_element_type=F32))
        dgpre_ref[...] += jnp.sum(dh2 * x1v * r2, axis=0, keepdims=True)
        dx1_ref[...] = dy + _rms_bwd(x1v, r2, gpre_v, dh2)

    outs = [((s_len, D_MODEL), F32, _row_spec(tm, D_MODEL)),
            ((s_len, D_MODEL), BF16, _row_spec(tm, D_MODEL)),
            ((D_FF, s_len), BF16, pl.BlockSpec((D_FF, tm), lambda i: (0, i))),
            ((s_len, D_MODEL), BF16, _row_spec(tm, D_MODEL)),
            ((2 * D_FF, s_len), BF16, pl.BlockSpec((2 * D_FF, tm), lambda i: (0, i))),
            ((1, 128), F32, _const_spec((1, 128))),
            ((1, D_MODEL), F32, _const_spec((1, D_MODEL))),
            ((1, D_MODEL), F32, _const_spec((1, D_MODEL)))]
    return pl.pallas_call(
        body, name="ffn_fwd_bwd", grid=(s_len // tm,),
        in_specs=[_row_spec(tm, D_MODEL), _row_spec(tm, D_MODEL), _const_spec(wffn.shape),
                  _const_spec(wdown.shape), _const_spec((1, D_MODEL)), _const_spec((1, D_MODEL))],
        out_specs=[o[2] for o in outs],
        out_shape=[jax.ShapeDtypeStruct(o[0], o[1]) for o in outs],
        compiler_params=_params(60, 1),
    )(x1, tgt, wffn, wdown, gpre, gpost)


def _mix_bwd(dx1, om, ya, yb, gpre, uvpre, attn, wout, wa, wb, wsm, wsmt, bsf, gsgu, bsgu, gpost,
             wmask, egrp):
    s_len = dx1.shape[0]
    tm = TOKEN_TILE
    nw = tm // WINDOW
    nt = s_len // tm

    def body(dx1_ref, om_ref, ya_ref, yb_ref, gp_ref, uv_ref, o_ref, wo_ref, wa_ref, wb_ref, ws_ref,
             wst_ref, bs_ref, gs_ref, bsg_ref, gpost_ref, mask_ref, eg_ref,
             dom_ref, dya_ref, dyb_ref, dgp_ref, dot_ref, delta_ref, duv_ref,
             dws_ref, dbs_ref, dgs_ref, dbsg_ref, dgpost_ref, dbs_acc):
        step = pl.program_id(0)

        @pl.when(step == 0)
        def _():
            dws_ref[...] = jnp.zeros_like(dws_ref)
            dbs_acc[...] = jnp.zeros_like(dbs_acc)
            dgs_ref[...] = jnp.zeros_like(dgs_ref)
            dbsg_ref[...] = jnp.zeros_like(dbsg_ref)
            dgpost_ref[...] = jnp.zeros_like(dgpost_ref)

        om = om_ref[...]
        dx1v = dx1_ref[...]
        r = lax.rsqrt(jnp.mean(om * om, axis=-1, keepdims=True) + EPS)
        gpost_v = gpost_ref[...]
        dgpost_ref[...] += jnp.sum(dx1v * om * r, axis=0, keepdims=True)
        dom = _rms_bwd(om, r, gpost_v, dx1v).astype(BF16)
        dom_ref[...] = dom
        dmg = _dot_nt(dom, wo_ref[...])

        gates = _sigmoid(gp_ref[...].astype(F32))
        ga, gb = gates[:, :D_MODEL], gates[:, D_MODEL:]
        yav, ybv = ya_ref[...].astype(F32), yb_ref[...].astype(F32)
        dya = (dmg * ga).astype(BF16)
        dyb = (dmg * gb).astype(BF16)
        dya_ref[...] = dya
        dyb_ref[...] = dyb
        dgp_ref[:, :D_MODEL] = (dmg * yav * ga * (1.0 - ga)).astype(BF16)
        dgp_ref[:, D_MODEL:] = (dmg * ybv * gb * (1.0 - gb)).astype(BF16)

        dat_t = _dot_nt(dya, wa_ref[...]).T.astype(BF16)
        dot_ref[0] = dat_t
        o_t = o_ref[...].astype(F32).T
        delta_ref[0] = jnp.sum((dat_t.astype(F32) * o_t).reshape(HEADS, HEAD_DIM, tm), axis=1)
        dsgu = _dot_nt(dyb, wb_ref[...])

        uvp = uv_ref[...].astype(F32)
        uv, guv = _gelu_and_grad(uvp)
        u, vv = uv[:, :SGU_W], uv[:, SGU_W:]
        gs_v = gs_ref[...]
        vn, xh, rln = _layernorm_fwd(vv, gs_v, bsg_ref[...])
        bias = bs_ref[...]
        if nw > 1:
            bias = jnp.concatenate([bias] * nw, axis=0)
        mixed = _sgu_mix(vn, ws_ref) + bias
        du = dsgu * mixed
        dmixed = dsgu * u

        lane = lax.broadcasted_iota(jnp.int32, (WINDOW, 128), 1)
        low = lane < HEAD_DIM
        dvn_wins = []
        for w in range(nw):
            rows = slice(w * WINDOW, (w + 1) * WINDOW)
            dbs_acc[...] += dmixed[rows, :]
            slabs = []
            for p in range(GROUPS // 2):
                cols = slice(p * 128, (p + 1) * 128)
                dm2 = dmixed[rows, cols]
                dlo = jnp.where(low, dm2, 0.0).astype(BF16)
                dhi = jnp.where(low, 0.0, dm2).astype(BF16)
                vn2 = vn[rows, cols].astype(BF16)
                dws_ref[2 * p] += _dot_nt(dlo, vn2)
                dws_ref[2 * p + 1] += _dot_nt(dhi, vn2)
                slabs.append(jnp.dot(wst_ref[2 * p], dlo, preferred_element_type=F32)
                             + jnp.dot(wst_ref[2 * p + 1], dhi, preferred_element_type=F32))
            dvn_wins.append(jnp.concatenate(slabs, axis=1))
        dvn = jnp.concatenate(dvn_wins, axis=0) if nw > 1 else dvn_wins[0]

        dgs_ref[...] += jnp.sum(dvn * xh, axis=0, keepdims=True)
        dbsg_ref[...] += jnp.sum(dvn, axis=0, keepdims=True)
        dxh = dvn * gs_v
        dvv = rln * (dxh - jnp.mean(dxh, axis=-1, keepdims=True)
                     - xh * jnp.mean(dxh * xh, axis=-1, keepdims=True))
        duv_ref[:, :SGU_W] = (du * guv[:, :SGU_W]).astype(BF16)
        duv_ref[:, SGU_W:] = (dvv * guv[:, SGU_W:]).astype(BF16)

        @pl.when(step == pl.num_programs(0) - 1)
        def _():
            for g in range(GROUPS):
                dws_ref[g] = dws_ref[g] * mask_ref[...]
            dbs_ref[...] = _split3_dot(dbs_acc[...], eg_ref[...])

    rows_out = [((s_len, D_MODEL), BF16, _row_spec(tm, D_MODEL)),
                ((s_len, D_MODEL), BF16, _row_spec(tm, D_MODEL)),
                ((s_len, D_MODEL), BF16, _row_spec(tm, D_MODEL)),
                ((s_len, 2 * D_MODEL), BF16, _row_spec(tm, 2 * D_MODEL)),
                ((nt, FOX_W, tm), BF16, _tile_spec(FOX_W, tm)),
                ((nt, HEADS, tm), F32, _tile_spec(HEADS, tm)),
                ((s_len, 2 * SGU_W), BF16, _row_spec(tm, 2 * SGU_W))]
    acc_out = [((GROUPS, WINDOW, WINDOW), F32), ((WINDOW, 128), F32), ((1, SGU_W), F32),
               ((1, SGU_W), F32), ((1, D_MODEL), F32)]
    return pl.pallas_call(
        body, name="mix_bwd", grid=(nt,),
        in_specs=[_row_spec(tm, D_MODEL), _row_spec(tm, D_MODEL), _row_spec(tm, D_MODEL),
                  _row_spec(tm, D_MODEL), _row_spec(tm, 2 * D_MODEL), _row_spec(tm, 2 * SGU_W),
                  _row_spec(tm, FOX_W), _const_spec(wout.shape), _const_spec(wa.shape),
                  _const_spec(wb.shape), _const_spec(wsm.shape), _const_spec(wsmt.shape),
                  _const_spec(bsf.shape), _const_spec((1, SGU_W)), _const_spec((1, SGU_W)),
                  _const_spec((1, D_MODEL)), _const_spec(wmask.shape), _const_spec(egrp.shape)],
        out_specs=[o[2] for o in rows_out] + [_const_spec(s) for s, _ in acc_out],
        out_shape=[jax.ShapeDtypeStruct(o[0], o[1]) for o in rows_out]
        + [jax.ShapeDtypeStruct(s, dt) for s, dt in acc_out],
        scratch_shapes=[pltpu.VMEM((WINDOW, SGU_W), F32)],
        compiler_params=_params(48, 1),
    )(dx1, om, ya, yb, gpre, uvpre, attn, wout, wa, wb, wsm, wsmt, bsf, gsgu, bsgu, gpost, wmask,
      egrp)


def _attn_bwd(qa, ka, kat, vs, dot_, lse, delta, ecol):
    s_len = qa.shape[0]
    t = ATTN_TILE
    nb = s_len // t

    def body(k_ref, kt_ref, vs_ref, q_ref, do_ref, lse_ref, dl_ref, ec_ref, gk_ref, dvt_ref,
             gqt_ref, csum_ref, p_sc, ds_sc):
        j = pl.program_id(0)

        @pl.when(j == 0)
        def _():
            gqt_ref[...] = jnp.zeros_like(gqt_ref)

        gk_ref[...] = jnp.zeros_like(gk_ref)
        dvt_ref[...] = jnp.zeros_like(dvt_ref)

        def tile(i, masked):
            qrows = pl.ds(pl.multiple_of(i * t, t), t)
            if masked:
                keep = (lax.broadcasted_iota(jnp.int32, (t, t), 0)
                        <= lax.broadcasted_iota(jnp.int32, (t, t), 1))
            for hd in range(HEADS):
                sl = slice(hd * 128, (hd + 1) * 128)
                hr = slice(hd * HEAD_DIM, (hd + 1) * HEAD_DIM)
                st = _dot_nt(k_ref[:, sl], q_ref[qrows, sl])
                if masked:
                    st = jnp.where(keep, st, -jnp.inf)
                pt = jnp.exp2(st - lse_ref[i, hd:hd + 1, :])
                dpt = jnp.dot(vs_ref[:, hd * 128:hd * 128 + HEAD_DIM], do_ref[i, hr, :],
                              preferred_element_type=F32)
                p_sc[hd] = pt.astype(BF16)
                ds_sc[hd] = (pt * (dpt - dl_ref[i, hd:hd + 1, :])).astype(BF16)
            for hd in range(HEADS):
                sl = slice(hd * 128, (hd + 1) * 128)
                hr = slice(hd * HEAD_DIM, (hd + 1) * HEAD_DIM)
                dst = ds_sc[hd]
                dvt_ref[0, hr, :] += _dot_nt(do_ref[i, hr, :], p_sc[hd])
                gk_ref[:, sl] += jnp.dot(dst, q_ref[qrows, sl], preferred_element_type=F32)
                gqt_ref[i, hd * QT_ROWS:(hd + 1) * QT_ROWS, :] += jnp.dot(
                    kt_ref[0, hd * 128:hd * 128 + QT_ROWS, :], dst, preferred_element_type=F32)

        tile(j, True)

        def below_diagonal(i, carry):
            tile(i, False)
            return carry

        lax.fori_loop(j + 1, nb, below_diagonal, 0)
        csum_ref[...] = _split3_dot(gk_ref[...], ec_ref[...])

    return pl.pallas_call(
        body, name="attn_bwd", grid=(nb,),
        in_specs=[_row_spec(t, SLAB_W), _tile_spec(SLAB_W, t), _row_spec(t, SLAB_W),
                  _const_spec(qa.shape), _const_spec(dot_.shape), _const_spec(lse.shape),
                  _const_spec(delta.shape), _const_spec(ecol.shape)],
        out_specs=[_row_spec(t, SLAB_W), _tile_spec(FOX_W, t),
                   _const_spec((nb, HEADS * QT_ROWS, t)), _row_spec(t, 128)],
        out_shape=[jax.ShapeDtypeStruct((s_len, SLAB_W), F32),
                   jax.ShapeDtypeStruct((nb, FOX_W, t), F32),
                   jax.ShapeDtypeStruct((nb, HEADS * QT_ROWS, t), F32),
                   jax.ShapeDtypeStruct((s_len, 128), F32)],
        scratch_shapes=[pltpu.VMEM((HEADS, t, t), BF16), pltpu.VMEM((HEADS, t, t), BF16)],
        compiler_params=_params(60, 1),
    )(ka, kat, vs, qa, dot_, lse, delta, ecol)


def _rev_cumsum(col_sums, gqt, triu):
    s_len = col_sums.shape[0]
    tm = TOKEN_TILE
    n = s_len // tm

    def body(cs_ref, gqt_ref, tri_ref, o_ref, carry):
        @pl.when(pl.program_id(0) == 0)
        def _():
            carry[...] = jnp.zeros_like(carry)
        rows = [gqt_ref[0, hd * QT_ROWS + HEAD_DIM:hd * QT_ROWS + HEAD_DIM + 1, :]
                for hd in range(HEADS)]
        row_sums = jnp.concatenate(rows + [jnp.zeros((128 - HEADS, tm), F32)], axis=0).T
        out = _tri_dot(tri_ref[...], row_sums - cs_ref[...]) + carry[...]
        o_ref[...] = out
        carry[...] = out[0:1, :]

    return pl.pallas_call(
        body, name="rev_cumsum", grid=(n,),
        in_specs=[pl.BlockSpec((tm, 128), lambda i: (n - 1 - i, 0)),
                  pl.BlockSpec((1, HEADS * QT_ROWS, tm), lambda i: (n - 1 - i, 0, 0)),
                  _const_spec((tm, tm))],
        out_specs=pl.BlockSpec((tm, 128), lambda i: (n - 1 - i, 0)),
        out_shape=jax.ShapeDtypeStruct((s_len, 128), F32),
        scratch_shapes=[pltpu.VMEM((1, 128), F32)],
        compiler_params=_params(32, 1),
    )(col_sums, gqt, triu)


def _heads_from_slabs(slabs):
    lane = lax.broadcasted_iota(jnp.int32, slabs[0].shape, 1)
    low = lane < HEAD_DIM
    pairs = [jnp.where(low, slabs[2 * p], pltpu.roll(slabs[2 * p + 1], HEAD_DIM, 1))
             for p in range(HEADS // 2)]
    return jnp.concatenate(pairs, axis=1)


def _proj_bwd(gqt, gk, dvt, dlogf, flog, qraw, kraw, duv, dgp, x, dx1, wcat, bdiag, gq, gk_gain, g1,
              efold):
    s_len = x.shape[0]
    tm = TOKEN_TILE

    def body(gqt_ref, gkk_ref, dvt_ref, dlf_ref, flog_ref, qr_ref, kr_ref, duv_ref, dgp_ref, x_ref,
             dx1_ref, w_ref, bd_ref, gq_ref, gk_ref, g1_ref, ef_ref,
             dx_ref, dprojt_ref, dgq_ref, dgk_ref, dbf_ref, dg1_ref, gq_acc, gk_acc, dproj_ref):
        step = pl.program_id(0)

        @pl.when(step == 0)
        def _():
            gq_acc[...] = jnp.zeros_like(gq_acc)
            gk_acc[...] = jnp.zeros_like(gk_acc)
            dbf_ref[...] = jnp.zeros_like(dbf_ref)
            dg1_ref[...] = jnp.zeros_like(dg1_ref)

        pad = jnp.zeros((128 - QT_ROWS, tm), F32)
        q_slabs = [jnp.concatenate([gqt_ref[0, hd * QT_ROWS:(hd + 1) * QT_ROWS, :], pad], axis=0).T
                   for hd in range(HEADS)]
        dqn = _heads_from_slabs(q_slabs)
        dkn = _heads_from_slabs([gkk_ref[:, hd * 128:(hd + 1) * 128] for hd in range(HEADS)])

        def head_bwd(raw_ref, dn, g_ref, acc):
            raw = raw_ref[...].astype(F32)
            r = lax.rsqrt(_seg_mean(raw * raw, bd_ref) + EPS)
            xhat = raw * r
            acc[0:1, :] += jnp.sum(dn * xhat, axis=0, keepdims=True)
            dyg = dn * g_ref[...]
            return r * (dyg - xhat * _seg_mean(dyg * xhat, bd_ref))

        dproj_ref[:, C_Q:C_K] = head_bwd(qr_ref, dqn * HEAD_DIM ** -0.5, gq_ref, gq_acc).astype(BF16)
        dproj_ref[:, C_K:C_V] = head_bwd(kr_ref, dkn * LN2, gk_ref, gk_acc).astype(BF16)
        dproj_ref[:, C_V:C_F] = dvt_ref[0].T.astype(BF16)
        dfl = dlf_ref[...] * _sigmoid(-flog_ref[...])
        dbf_ref[...] += jnp.sum(dfl, axis=0, keepdims=True)
        dproj_ref[:, C_F:C_UV] = dfl.astype(BF16)
        dproj_ref[:, C_UV:C_G] = duv_ref[...]
        dproj_ref[:, C_G:C_END] = dgp_ref[...]

        dproj = dproj_ref[...]
        dprojt_ref[...] = dproj.astype(F32).T.astype(BF16)
        dh = jnp.dot(dproj, w_ref[...], preferred_element_type=F32)
        xf = x_ref[...]
        r = lax.rsqrt(jnp.mean(xf * xf, axis=-1, keepdims=True) + EPS)
        dg1_ref[...] += jnp.sum(dh * xf * r, axis=0, keepdims=True)
        dx_ref[...] = dx1_ref[...] + _rms_bwd(xf, r, g1_ref[...], dh)

        @pl.when(step == pl.num_programs(0) - 1)
        def _():
            dgq_ref[...] = _split3_dot(gq_acc[...], ef_ref[...])
            dgk_ref[...] = _split3_dot(gk_acc[...], ef_ref[...])

    outs = [((s_len, D_MODEL), F32, _row_spec(tm, D_MODEL)),
            ((C_END, s_len), BF16, pl.BlockSpec((C_END, tm), lambda i: (0, i))),
            ((8, 128), F32, _const_spec((8, 128))),
            ((8, 128), F32, _const_spec((8, 128))),
            ((1, 128), F32, _const_spec((1, 128))),
            ((1, D_MODEL), F32, _const_spec((1, D_MODEL)))]
    return pl.pallas_call(
        body, name="proj_bwd", grid=(s_len // tm,),
        in_specs=[_tile_spec(HEADS * QT_ROWS, tm), _row_spec(tm, SLAB_W), _tile_spec(FOX_W, tm),
                  _row_spec(tm, 128), _row_spec(tm, 128), _row_spec(tm, FOX_W),
                  _row_spec(tm, FOX_W), _row_spec(tm, 2 * SGU_W), _row_spec(tm, 2 * D_MODEL),
                  _row_spec(tm, D_MODEL), _row_spec(tm, D_MODEL), _const_spec(wcat.shape),
                  _const_spec(bdiag.shape), _const_spec((1, FOX_W)), _const_spec((1, FOX_W)),
                  _const_spec((1, D_MODEL)), _const_spec(efold.shape)],
        out_specs=[o[2] for o in outs],
        out_shape=[jax.ShapeDtypeStruct(o[0], o[1]) for o in outs],
        scratch_shapes=[pltpu.VMEM((8, FOX_W), F32), pltpu.VMEM((8, FOX_W), F32),
                        pltpu.VMEM((tm, C_END), BF16)],
        compiler_params=_params(56, 1),
    )(gqt, gk, dvt, dlogf, flog, qraw, kraw, duv, dgp, x, dx1, wcat, bdiag, gq, gk_gain, g1, efold)


def _dw_matmul(at, b, tm, name, after=()):
    m, s_len = at.shape
    n = b.shape[1]

    def body(a_ref, b_ref, *rest):
        rest[-1][...] = jnp.dot(a_ref[...], b_ref[...], preferred_element_type=F32).astype(BF16)

    return pl.pallas_call(
        body, name=name, grid=(m // tm,),
        in_specs=[pl.BlockSpec((tm, s_len), lambda i: (i, 0)), _const_spec(b.shape)]
        + [pl.BlockSpec(memory_space=pl.ANY)] * len(after),
        out_specs=pl.BlockSpec((tm, n), lambda i: (i, 0)),
        out_shape=jax.ShapeDtypeStruct((m, n), BF16),
        compiler_params=_params(48, 1),
    )(at, b, *after)


def _adamw(parts, w, m, v, tr, name, col_tile=None):
    parts = parts if isinstance(parts, (list, tuple)) else [parts]
    rows, cols = w.shape
    bc1 = 1.0 - ADAM_B1 ** ADAM_STEP
    bc2 = 1.0 - ADAM_B2 ** ADAM_STEP

    def body(*refs):
        p_refs = refs[:len(parts)]
        w_ref, m_ref, v_ref, g_ref, d_ref, mo_ref, vo_ref = refs[len(parts):]
        g = None
        for p_ref, p in zip(p_refs, parts):
            for idx in range(p.shape[0]):
                term = p_ref[idx].astype(F32)
                g = term if g is None else g + term
        g_ref[...] = g
        mn = ADAM_B1 * m_ref[...] + (1.0 - ADAM_B1) * g
        vn = ADAM_B2 * v_ref[...] + (1.0 - ADAM_B2) * (g * g)
        mo_ref[...] = mn
        vo_ref[...] = vn
        m_hat = mn / bc1
        v_hat = vn / bc2
        d_ref[...] = -ADAM_LR * (m_hat / (jnp.sqrt(v_hat) + ADAM_EPS) + ADAM_WD * w_ref[...])

    if col_tile is None:
        spec = pl.BlockSpec((tr, cols), lambda i: (i, 0))
        pspecs = [pl.BlockSpec((p.shape[0], tr, cols), lambda i: (0, i, 0)) for p in parts]
        steps = rows // tr
    else:
        spec = pl.BlockSpec((rows, col_tile), lambda i: (0, i))
        pspecs = [pl.BlockSpec((p.shape[0], rows, col_tile), lambda i: (0, 0, i)) for p in parts]
        steps = cols // col_tile
    return pl.pallas_call(
        body, name=name, grid=(steps,),
        in_specs=pspecs + [spec, spec, spec],
        out_specs=[spec] * 4,
        out_shape=[jax.ShapeDtypeStruct((rows, cols), F32)] * 4,
        compiler_params=_params(48, 1),
    )(*parts, w, m, v)


def _sum_parts(parts, name):
    n, rows, cols = parts.shape

    def body(p_ref, o_ref):
        g = p_ref[0]
        for idx in range(1, n):
            g = g + p_ref[idx]
        o_ref[...] = g

    return pl.pallas_call(
        body, name=name, out_shape=jax.ShapeDtypeStruct((rows, cols), F32),
        in_specs=[_const_spec(parts.shape)], out_specs=_const_spec((rows, cols)), grid=(1,),
        compiler_params=_params(16, 1),
    )(parts)


SMALL_NAMES = ("g_pre_mix", "b_forget", "g_q", "g_k", "g_sgu", "b_sgu", "w_spatial", "b_spatial",
               "g_post_mix", "g_pre_ffn", "g_post_ffn")


def _small_rows(size):
    return -(-size // 1024)


def _pack_small(d, extra=None):
    rows = []
    for k in SMALL_NAMES:
        flat = d[k].reshape(-1).astype(F32)
        nr = _small_rows(flat.shape[0])
        rows.append(jnp.pad(flat, (0, nr * 1024 - flat.shape[0])).reshape(nr, 1024))
    if extra is not None:
        rows.append(extra)
    used = sum(r.shape[0] for r in rows)
    rows.append(jnp.zeros((N_DEV * SMALL_ROWS - used, 1024), F32))
    return jnp.concatenate(rows, axis=0)


def _unpack_small(packed, shapes):
    out, off = {}, 0
    for k in SMALL_NAMES:
        size = math.prod(shapes[k])
        nr = _small_rows(size)
        out[k] = packed[off:off + nr].reshape(-1)[:size].reshape(shapes[k])
        off += nr
    return out


def _cols_to_blocks(full, width):
    r = full.shape[0]
    return jnp.transpose(full.reshape(r, N_DEV, width), (1, 0, 2))


def _blocks_to_cols(blocks):
    n, r, width = blocks.shape
    return jnp.transpose(blocks, (1, 0, 2)).reshape(r, n * width)


def kernel(x, g_pre_mix, w_in, b_forget, g_q, g_k, g_sgu, b_sgu, w_spatial, b_spatial, w_branch_a, w_branch_b, w_out, g_post_mix, g_pre_ffn, w_ffn_in, w_ffn_down, g_post_ffn, loss_target, m_g_pre_mix, m_w_in, m_b_forget, m_g_q, m_g_k, m_g_sgu, m_b_sgu, m_w_spatial, m_b_spatial, m_w_branch_a, m_w_branch_b, m_w_out, m_g_post_mix, m_g_pre_ffn, m_w_ffn_in, m_w_ffn_down, m_g_post_ffn, v_g_pre_mix, v_w_in, v_b_forget, v_g_q, v_g_k, v_g_sgu, v_b_sgu, v_w_spatial, v_b_spatial, v_w_branch_a, v_w_branch_b, v_w_out, v_g_post_mix, v_g_pre_ffn, v_w_ffn_in, v_w_ffn_down, v_g_post_ffn):
    big_names = ("w_in", "w_branch_a", "w_branch_b", "w_out", "w_ffn_in", "w_ffn_down")
    weights = dict(g_pre_mix=g_pre_mix, w_in=w_in, b_forget=b_forget, g_q=g_q, g_k=g_k, g_sgu=g_sgu,
                   b_sgu=b_sgu, w_spatial=w_spatial, b_spatial=b_spatial, w_branch_a=w_branch_a,
                   w_branch_b=w_branch_b, w_out=w_out, g_post_mix=g_post_mix, g_pre_ffn=g_pre_ffn,
                   w_ffn_in=w_ffn_in, w_ffn_down=w_ffn_down, g_post_ffn=g_post_ffn)
    mom1 = dict(g_pre_mix=m_g_pre_mix, w_in=m_w_in, b_forget=m_b_forget, g_q=m_g_q, g_k=m_g_k,
                g_sgu=m_g_sgu, b_sgu=m_b_sgu, w_spatial=m_w_spatial, b_spatial=m_b_spatial,
                w_branch_a=m_w_branch_a, w_branch_b=m_w_branch_b, w_out=m_w_out,
                g_post_mix=m_g_post_mix, g_pre_ffn=m_g_pre_ffn, w_ffn_in=m_w_ffn_in,
                w_ffn_down=m_w_ffn_down, g_post_ffn=m_g_post_ffn)
    mom2 = dict(g_pre_mix=v_g_pre_mix, w_in=v_w_in, b_forget=v_b_forget, g_q=v_g_q, g_k=v_g_k,
                g_sgu=v_g_sgu, b_sgu=v_b_sgu, w_spatial=v_w_spatial, b_spatial=v_b_spatial,
                w_branch_a=v_w_branch_a, w_branch_b=v_w_branch_b, w_out=v_w_out,
                g_post_mix=v_g_post_mix, g_pre_ffn=v_g_pre_ffn, w_ffn_in=v_w_ffn_in,
                w_ffn_down=v_w_ffn_down, g_post_ffn=v_g_post_ffn)
    names = list(weights)
    shapes = {k: weights[k].shape for k in names}

    s_len = x.shape[1]
    xs = x.reshape(s_len, D_MODEL)
    tgt = loss_target.reshape(s_len, D_MODEL)

    transposed = ("w_in", "w_ffn_in")

    def local_view(a, k):
        return jnp.transpose(a[0]) if k in transposed else a[0]

    shards = {k: local_view(weights[k], k).astype(BF16) for k in big_names}
    win_t = _gather_two_level(shards["w_in"], "gather_w_in").reshape(IN_COLS, D_MODEL)
    win_t, later = lax.optimization_barrier(
        (win_t, [shards[k] for k in big_names if k != "w_in"]))
    shards.update(zip([k for k in big_names if k != "w_in"], later))
    (gat_mix, gat_ffn), gat_token = _exchange_start(
        [[shards["w_branch_a"], shards["w_branch_b"], shards["w_out"]],
         [shards["w_ffn_in"], shards["w_ffn_down"]]], "gather_start", gather=True)
    f_off = 3 * FOX_W
    u_off = f_off + HEADS
    wcat = jnp.concatenate([
        win_t[:f_off], jnp.pad(win_t[f_off:u_off], ((0, 128 - HEADS), (0, 0))), win_t[u_off:]],
        axis=0)

    seg = np.arange(FOX_W) // HEAD_DIM
    bdiag = jnp.asarray(seg[:128, None] == seg[None, :128], BF16)
    tm = TOKEN_TILE
    lower = np.arange(tm)[None, :] <= np.arange(tm)[:, None]
    tril = jnp.asarray(lower, BF16)
    triu = jnp.asarray(lower.T, BF16)
    egrp = jnp.asarray(seg[:, None] == np.arange(128)[None, :], BF16)
    efold = jnp.asarray((np.arange(FOX_W) % HEAD_DIM)[:, None] == np.arange(128)[None, :], BF16)
    gq512 = jnp.tile(g_q.reshape(1, HEAD_DIM), (1, HEADS))
    gk512 = jnp.tile(g_k.reshape(1, HEAD_DIM), (1, HEADS))
    bfor = jnp.pad(b_forget.reshape(1, HEADS), ((0, 0), (0, 128 - HEADS)))
    pos = np.arange(WINDOW)
    wmask = (pos[None, :] // CHUNK) <= (pos[:, None] // CHUNK)
    wsm_f = jnp.where(jnp.asarray(wmask)[None], w_spatial[0], 0.0)
    wsm = wsm_f.astype(BF16)
    wsmt = jnp.transpose(wsm_f, (0, 2, 1)).astype(BF16)
    bsf = jnp.repeat(jnp.transpose(b_spatial[0]), HEAD_DIM, axis=1)
    wmask_f = jnp.asarray(wmask, F32)

    col = np.arange(SLAB_W)
    row128 = np.arange(128)

    def d_place(first):
        return jnp.asarray(np.stack([(col[None, :] // 128 == row128[:, None])
                                     & (col[None, :] % 128 == first + a) for a in range(3)]), BF16)

    pdq, pdk = d_place(HEAD_DIM), d_place(HEAD_DIM + 3)
    ones_q = jnp.asarray((col % 128 >= HEAD_DIM + 3) & (col % 128 < HEAD_DIM + 6), F32)[None]
    ones_k = jnp.asarray((col % 128 >= HEAD_DIM) & (col % 128 < HEAD_DIM + 3), F32)[None]
    ecol = jnp.asarray((col[:, None] // 128 == row128[None, :])
                       & (col[:, None] % 128 == HEAD_DIM + 3), BF16)

    (h, qa, ka, kat, vs, vt, qraw, kraw, flog, uvpre, gpre) = _proj_fwd(
        xs, g_pre_mix + gat_token[0:1, 0:1], wcat, bdiag, gq512, gk512, bfor, tril, pdq, pdk,
        ones_q, ones_k)
    attn, attn_t, lse = _attn_fwd(qa, ka, vt)
    (own_a, own_b, own_out), (zone_a, zone_b, zone_out) = _exchange_wait(
        gat_mix, attn, "gather_wait_mix", gather=True)
    wa = _blocks_to_cols(_own_block(zone_a, own_a))
    wb = _blocks_to_cols(_own_block(zone_b, own_b))
    wout = _own_block(zone_out, own_out).reshape(D_MODEL, D_MODEL)
    sgu_t, ya, yb, merged_t, om, x1 = _mix_fwd(attn, uvpre, gpre, xs, wa, wb, wout, wsm, bsf,
                                           g_sgu, b_sgu, g_post_mix)
    (own_ffn, own_down), (zone_ffn, zone_down) = _exchange_wait(
        gat_ffn, x1, "gather_wait_ffn", gather=True)
    wffn = _own_block(zone_ffn, own_ffn).reshape(2 * D_FF, D_MODEL)
    wdown = _own_block(zone_down, own_down).reshape(D_FF, D_MODEL)
    (dx1, h2, act_t, dff, dgu_t, loss_acc, dg_post_ffn, dg_pre_ffn) = _ffn_fwd_bwd(
        x1, tgt, wffn, wdown, g_pre_ffn, g_post_ffn)

    dw_down = _dw_matmul(act_t, dff, D_FF // 4, "dw_down")
    dw_ffn = _dw_matmul(dgu_t, h2, 2 * D_FF // N_DEV, "dw_ffn_in")
    x_pos, y_pos, c_pos = _mesh_pos()
    me = 4 * x_pos + 2 * y_pos + c_pos

    def own_of(parts):
        return [lax.dynamic_index_in_dim(p, me, 0, keepdims=False) for p in parts]

    parts_ffn = [dw_ffn.reshape(N_DEV, 2 * D_FF // N_DEV, D_MODEL),
                 dw_down.reshape(N_DEV, D_FF // N_DEV, D_MODEL)]
    mine_ffn = own_of(parts_ffn)
    (sct_ffn,), sct_ffn_token = _exchange_start([parts_ffn], "scatter_start_ffn", gather=False)

    (dom, dya, dyb, dgp, dot_, delta, duv, dws, dbs, dg_sgu, db_sgu, dg_post_mix) = _mix_bwd(
        dx1, om, ya, yb, gpre, uvpre, attn, wout, wa, wb, wsm, wsmt, bsf, g_sgu, b_sgu,
        g_post_mix + sct_ffn_token[0:1, 0:1], wmask_f, egrp)
    dw_out = _dw_matmul(merged_t, dom, 512, "dw_out")
    dw_a = _dw_matmul(attn_t, dya, 512, "dw_a")
    dw_b = _dw_matmul(sgu_t, dyb, 512, "dw_b")
    parts_mix = [_cols_to_blocks(dw_a, D_MODEL // N_DEV), _cols_to_blocks(dw_b, D_MODEL // N_DEV),
                 dw_out.reshape(N_DEV, D_MODEL // N_DEV, D_MODEL)]
    mine_mix = own_of(parts_mix)
    (sct_mix,), sct_mix_token = _exchange_start([parts_mix], "scatter_start_mix", gather=False)

    gk_all, dvt, gqt, col_sums = _attn_bwd(qa, ka, kat, vs, dot_, lse,
                                           delta + sct_mix_token[0, 0], ecol)
    dlogf = _rev_cumsum(col_sums, gqt, triu)
    dx, dproj_t, dgq, dgk, dbf, dg_pre_mix = _proj_bwd(
        gqt, gk_all, dvt, dlogf, flog, qraw, kraw, duv, dgp, xs, dx1, wcat, bdiag, gq512, gk512,
        g_pre_mix, efold)

    small_local = dict(
        g_pre_mix=dg_pre_mix, b_forget=dbf[:, :HEADS], g_q=dgq[0:1, :HEAD_DIM],
        g_k=dgk[0:1, :HEAD_DIM], g_sgu=dg_sgu, b_sgu=db_sgu, w_spatial=dws,
        b_spatial=jnp.transpose(dbs[:, :GROUPS]), g_post_mix=dg_post_mix, g_pre_ffn=dg_pre_ffn,
        g_post_ffn=dg_post_ffn)
    loss_row = jnp.pad(loss_acc[0:1, 0:1], ((0, 0), (0, 1023)))
    small_parts = _pack_small(small_local, loss_row).reshape(N_DEV, SMALL_ROWS, 1024)

    def with_own(zones, own_blocks):
        return [_own_block(z, b) for z, b in zip(zones, own_blocks)]

    mine_small = own_of([small_parts])
    (sct_small,), sct_small_token = _exchange_start([[small_parts]], "scatter_start_small",
                                                    gather=False)
    dw_cat = _dw_matmul(dproj_t, h, C_END // N_DEV, "dw_in", after=(sct_small_token,))
    dw_in = jnp.concatenate([dw_cat[:C_F + HEADS], dw_cat[C_UV:]], axis=0)
    (recv_small,) = with_own(
        _exchange_wait(sct_small, dw_in, "scatter_wait_small", gather=False)[1], mine_small)
    small_sum = _sum_parts(recv_small, "sum_small")
    (gat_small,), gat_small_token = _exchange_start([[small_sum]], "gather_start_small",
                                                    gather=True)
    pair_blocks, own_pair = _pair_sums(dw_in.reshape(N_DEV, BLK, D_MODEL), "pair_sums_in",
                                       gat_small_token)
    rs_in, rs_token = _chip_exchange_start(pair_blocks, "chip_exchange_start_in")

    recv_ffn, recv_down = with_own(
        _exchange_wait(sct_ffn, rs_token, "scatter_wait_ffn", gather=False)[1], mine_ffn)
    recv_a, recv_b, recv_out = with_own(
        _exchange_wait(sct_mix, recv_ffn, "scatter_wait_mix", gather=False)[1], mine_mix)
    received = [None, recv_a, recv_b, recv_out, recv_ffn, recv_down]

    grads, deltas, new_m, new_v = {}, {}, {}, {}
    row_tiles = {"w_in": None, "w_branch_a": 512, "w_branch_b": 512, "w_out": 128, "w_ffn_in": 176,
                 "w_ffn_down": 352}

    def update(k, parts):
        outs = _adamw(parts, local_view(weights[k], k), local_view(mom1[k], k),
                      local_view(mom2[k], k), row_tiles[k], "adamw_" + k,
                      col_tile=256 if k == "w_in" else None)
        if k in transposed:
            outs = [jnp.transpose(o) for o in outs]
        grads[k], deltas[k], new_m[k], new_v[k] = [o[None] for o in outs]
        return outs[0]

    last = None
    for idx, k in enumerate(big_names):
        if k != "w_in":
            last = update(k, received[idx])

    (own_small,), (zone_small,) = _exchange_wait(gat_small, last, "gather_wait_small", gather=True)
    small_all = _own_block(zone_small, own_small).reshape(1, N_DEV * SMALL_ROWS, 1024)
    sg, sd, sm, sv = _adamw(small_all, _pack_small(weights), _pack_small(mom1), _pack_small(mom2),
                            N_DEV * SMALL_ROWS, "adamw_small")
    for dst, packed in ((grads, sg), (deltas, sd), (new_m, sm), (new_v, sv)):
        dst.update(_unpack_small(packed, shapes))
    arrived = _chip_exchange_wait(rs_in, sg, "chip_exchange_wait_in")
    update("w_in", [own_pair[None], arrived])

    loss = small_all[0, sum(_small_rows(math.prod(shapes[k])) for k in SMALL_NAMES), 0]
    return (loss, dx.reshape(x.shape), *[grads[k] for k in names], *[deltas[k] for k in names],
            *[new_m[k] for k in names], *[new_v[k] for k in names])
```

```python
import functools
import math

import jax
import jax.numpy as jnp
import numpy as np
from jax import lax
from jax.experimental import pallas as pl
from jax.experimental.pallas import tpu as pltpu

F32 = jnp.float32
BF16 = jnp.bfloat16

D_MODEL = 1024
FOX_W = 512
HEADS = 8
HEAD_DIM = 64
SGU_W = 512
GROUPS = 8
WINDOW = 128
CHUNK = 64
D_FF = 2816
IN_COLS = 4616
EPS = 1e-6
N_DEV = 8
LOG2E = 1.4426950408889634
LN2 = 0.6931471805599453

C_Q, C_K, C_V, C_F, C_UV, C_G, C_END = 0, 512, 1024, 1536, 1664, 2688, 4736

ADAM_LR, ADAM_B1, ADAM_B2, ADAM_EPS, ADAM_WD, ADAM_STEP = 0.001, 0.9, 0.999, 1e-08, 0.01, 10

MIB = 1024 * 1024
TOKEN_TILE = 256
ATTN_TILE = 256
SLAB_W = HEADS * 128
QT_ROWS = 72

SMALL_ROWS = 18
BLK = IN_COLS // N_DEV


def _params(vmem_mib, n_axes):
    return pltpu.CompilerParams(
        dimension_semantics=("arbitrary",) * n_axes, vmem_limit_bytes=vmem_mib * MIB)


def _const_spec(shape):
    nd = len(shape)
    return pl.BlockSpec(shape, lambda *_: (0,) * nd)


def _row_spec(tm, cols):
    return pl.BlockSpec((tm, cols), lambda i: (i, 0))


def _tile_spec(rows, tm):
    return pl.BlockSpec((1, rows, tm), lambda i: (i, 0, 0))


def _split3_dot(x, e):
    x1 = x.astype(BF16)
    r1 = x - x1.astype(F32)
    x2 = r1.astype(BF16)
    x3 = (r1 - x2.astype(F32)).astype(BF16)
    dot = functools.partial(jnp.dot, preferred_element_type=F32)
    return dot(x1, e) + dot(x2, e) + dot(x3, e)


def _tri_dot(tri, x):
    x1 = x.astype(BF16)
    r1 = x - x1.astype(F32)
    x2 = r1.astype(BF16)
    x3 = (r1 - x2.astype(F32)).astype(BF16)
    dot = functools.partial(jnp.dot, preferred_element_type=F32)
    return dot(tri, x1) + dot(tri, x2) + dot(tri, x3)


def _seg_mean(sq, bd_ref):
    hi = sq.astype(BF16)
    lo = (sq - hi.astype(F32)).astype(BF16)
    bd = bd_ref[...]
    dot = functools.partial(jnp.dot, preferred_element_type=F32)
    pairs = [dot(hi[:, p * 128:(p + 1) * 128], bd) + dot(lo[:, p * 128:(p + 1) * 128], bd)
             for p in range(HEADS // 2)]
    return jnp.concatenate(pairs, axis=1) * (1.0 / HEAD_DIM)


def _slabs_from_heads(t):
    lane = lax.broadcasted_iota(jnp.int32, (t.shape[0], 128), 1)
    low = lane < HEAD_DIM
    slabs = []
    for p in range(HEADS // 2):
        pair = t[:, p * 128:(p + 1) * 128]
        slabs.append(jnp.where(low, pair, 0.0))
        slabs.append(jnp.where(low, pltpu.roll(pair, HEAD_DIM, 1), 0.0))
    return jnp.concatenate(slabs, axis=1)


def _dot_nt(a, b):
    return lax.dot_general(a, b, (((1,), (1,)), ((), ())), preferred_element_type=F32)


def _dot_tn(a, b):
    return lax.dot_general(a, b, (((0,), (0,)), ((), ())), preferred_element_type=F32)


def _sigmoid(x):
    return 0.5 * jnp.tanh(0.5 * x) + 0.5


_GELU_C = math.sqrt(2.0 / math.pi)


def _gelu_and_grad(x):
    inner = _GELU_C * (x + 0.044715 * x * x * x)
    t = jnp.tanh(inner)
    y = 0.5 * x * (1.0 + t)
    dy = 0.5 * (1.0 + t) + 0.5 * x * (1.0 - t * t) * _GELU_C * (1.0 + 3.0 * 0.044715 * x * x)
    return y, dy


def _rms_bwd(xin, r, g, dy):
    dyg = dy * g
    return r * dyg - xin * (r * r * r) * jnp.mean(dyg * xin, axis=-1, keepdims=True)


def _mesh_pos():
    x, y, c = lax.axis_index("x"), lax.axis_index("y"), lax.axis_index("c")
    return x, y, c


def _peer(k):
    x, y, c = _mesh_pos()
    px = (1 - x) if (k >> 2) & 1 else x
    py = (1 - y) if (k >> 1) & 1 else y
    pc = (1 - c) if k & 1 else c
    return (px, py, pc), 4 * px + 2 * py + pc


def _gather_two_level(shard, name):
    def body(x_ref, out_ref, send_sems, recv_sems, local_sem):
        x, y, c = _mesh_pos()
        me, sibling = (x, y, c), (x, y, 1 - c)
        chips = [(1 - x, y), (x, 1 - y), (1 - x, 1 - y)]

        def slot(px, py, pc):
            return out_ref.at[4 * px + 2 * py + pc]

        def copy(k, block, to, src=None):
            return pltpu.make_async_remote_copy(
                src_ref=slot(*block) if src is None else src, dst_ref=slot(*block),
                send_sem=send_sems.at[k], recv_sem=recv_sems.at[k],
                device_id=to, device_id_type=pl.DeviceIdType.MESH)

        mine = pltpu.make_async_copy(x_ref, slot(*me), local_sem)
        mine.start()
        first = [copy(1 + j, me, (*chip, c), src=x_ref) for j, chip in enumerate(chips)]
        first.append(copy(0, me, sibling, src=x_ref))
        for cp in first:
            cp.start()
        passed = [copy(4 + j, (*chip, c), sibling) for j, chip in enumerate(chips)]
        for j, chip in enumerate(chips):
            copy(1 + j, (*chip, c), me).wait_recv()
            passed[j].start()
        copy(0, sibling, me).wait_recv()
        for j, chip in enumerate(chips):
            copy(4 + j, (*chip, 1 - c), me).wait_recv()
        for cp in first + passed:
            cp.wait_send()
        mine.wait()

    any_spec = pl.BlockSpec(memory_space=pl.ANY)
    return pl.pallas_call(
        body, name=name, out_shape=jax.ShapeDtypeStruct((N_DEV,) + shard.shape, shard.dtype),
        in_specs=[any_spec], out_specs=any_spec,
        scratch_shapes=[pltpu.SemaphoreType.DMA((7,)), pltpu.SemaphoreType.DMA((7,)),
                        pltpu.SemaphoreType.DMA],
    )(shard)


def _chip_peer(k):
    x, y, c = _mesh_pos()
    px = (1 - x) if (k >> 1) & 1 else x
    py = (1 - y) if k & 1 else y
    return (px, py, c), 2 * px + py


def _pair_sums(parts, name, after):
    _, rows, cols = parts.shape
    n_chips = N_DEV // 2

    def body(p_ref, after_ref, send_ref, own_ref, mine_buf, sib_buf, send_sems, recv_sems,
             local_sems):
        x, y, c = _mesh_pos()
        sibling = (x, y, 1 - c)
        copies, local = [], []
        for q in range(n_chips):
            cp = pltpu.make_async_remote_copy(
                src_ref=p_ref.at[2 * q + (1 - c)], dst_ref=sib_buf.at[q],
                send_sem=send_sems.at[q], recv_sem=recv_sems.at[q],
                device_id=sibling, device_id_type=pl.DeviceIdType.MESH)
            cp.start()
            copies.append(cp)
            lc = pltpu.make_async_copy(p_ref.at[2 * q + c], mine_buf.at[q], local_sems.at[q])
            lc.start()
            local.append(lc)
        for lc in local:
            lc.wait()
        for cp in copies:
            cp.wait_recv()
        for k in range(1, n_chips):
            _, q = _chip_peer(k)
            send_ref[k - 1] = (mine_buf[q].astype(F32) + sib_buf[q].astype(F32)).astype(BF16)
        my_chip = 2 * x + y
        own_ref[...] = mine_buf[my_chip].astype(F32) + sib_buf[my_chip].astype(F32)
        for cp in copies:
            cp.wait_send()

    vmem = pl.BlockSpec(memory_space=pltpu.VMEM)
    return pl.pallas_call(
        body, name=name,
        out_shape=[jax.ShapeDtypeStruct((n_chips - 1, rows, cols), BF16),
                   jax.ShapeDtypeStruct((rows, cols), F32)],
        in_specs=[pl.BlockSpec(memory_space=pl.ANY)] * 2, out_specs=[vmem, vmem],
        scratch_shapes=[pltpu.VMEM((n_chips, rows, cols), BF16),
                        pltpu.VMEM((n_chips, rows, cols), BF16),
                        pltpu.SemaphoreType.DMA((n_chips,)), pltpu.SemaphoreType.DMA((n_chips,)),
                        pltpu.SemaphoreType.DMA((n_chips,))],
        compiler_params=pltpu.CompilerParams(vmem_limit_bytes=40 * MIB),
    )(parts, after)


def _chip_copy(src_ref, land_ref, send_sem, recv_sem, k):
    peer, _ = _chip_peer(k)
    return pltpu.make_async_remote_copy(
        src_ref=src_ref.at[k - 1], dst_ref=land_ref.at[k - 1], send_sem=send_sem, recv_sem=recv_sem,
        device_id=peer, device_id_type=pl.DeviceIdType.MESH)


def _chip_exchange_start(blocks, name):
    hbm = pl.BlockSpec(memory_space=pltpu.HBM)
    sem = pl.BlockSpec(memory_space=pltpu.SEMAPHORE)
    n_peers = blocks.shape[0]

    def body(src_ref, zone_ref, send_sems, recv_sems, src_thru, zone_thru, token):
        for k in range(1, n_peers + 1):
            _chip_copy(src_ref, zone_ref, send_sems.at[k - 1], recv_sems.at[k - 1], k).start()
        token[...] = jnp.zeros_like(token)

    outs = pl.pallas_call(
        body, name=name, in_specs=[hbm, hbm],
        out_shape=[pltpu.SemaphoreType.DMA((n_peers,)), pltpu.SemaphoreType.DMA((n_peers,)),
                   pltpu.HBM(blocks.shape, blocks.dtype), pltpu.HBM(blocks.shape, blocks.dtype),
                   jax.ShapeDtypeStruct((8, 128), F32)],
        out_specs=[sem, sem, hbm, hbm, pl.BlockSpec(memory_space=pltpu.VMEM)],
        input_output_aliases={0: 2, 1: 3},
        compiler_params=pltpu.CompilerParams(
            has_side_effects=pltpu.SideEffectType.DATAFLOW_SIDE_EFFECTING),
    )(pltpu.with_memory_space_constraint(blocks, pltpu.HBM),
      pltpu.with_memory_space_constraint(lax.empty(blocks.shape, blocks.dtype), pltpu.HBM))
    return outs[:4], outs[4]


def _chip_exchange_wait(handle, after, name):
    send_sems, recv_sems, src, zone = handle
    hbm = pl.BlockSpec(memory_space=pltpu.HBM)
    sem = pl.BlockSpec(memory_space=pltpu.SEMAPHORE)

    def body(src_ref, zone_ref, ssem, rsem, after_ref, src_out, zone_out):
        for k in range(1, src.shape[0] + 1):
            cp = _chip_copy(src_ref, zone_ref, ssem.at[k - 1], rsem.at[k - 1], k)
            cp.wait_send()
            cp.wait_recv()

    outs = pl.pallas_call(
        body, name=name,
        in_specs=[hbm, hbm, sem, sem, pl.BlockSpec(memory_space=pl.ANY)],
        out_shape=[pltpu.HBM(src.shape, src.dtype), pltpu.HBM(zone.shape, zone.dtype)],
        out_specs=[hbm, hbm], input_output_aliases={0: 0, 1: 1},
        compiler_params=pltpu.CompilerParams(
            has_side_effects=pltpu.SideEffectType.DATAFLOW_SIDE_EFFECTING),
    )(src, zone, send_sems, recv_sems, after)
    return outs[1]


def _remote_copy(gather, src_ref, land_ref, send_sem, recv_sem, k, receive_side):
    x, y, c = _mesh_pos()
    me = 4 * x + 2 * y + c
    peer, pidx = _peer(k)
    return pltpu.make_async_remote_copy(
        src_ref=src_ref if gather else src_ref.at[pidx],
        dst_ref=land_ref.at[pidx if receive_side else me],
        send_sem=send_sem, recv_sem=recv_sem,
        device_id=peer, device_id_type=pl.DeviceIdType.MESH)


def _exchange_start(groups, name, gather):
    arrs = [a for g in groups for a in g]
    n, n_groups = len(arrs), len(groups)
    lands = [jax.ShapeDtypeStruct(((N_DEV,) + a.shape) if gather else a.shape, a.dtype)
             for a in arrs]

    def body(*refs):
        srcs, zones = refs[:n], refs[n:2 * n]
        sems = refs[2 * n:2 * n + 2 * n_groups]
        token = refs[-1]
        a = 0
        for gi, g in enumerate(groups):
            send_sems, recv_sems = sems[2 * gi], sems[2 * gi + 1]
            for k in range(1, N_DEV):
                for ai in range(len(g)):
                    slot = ai * (N_DEV - 1) + k - 1
                    _remote_copy(gather, srcs[a + ai], zones[a + ai], send_sems.at[slot],
                                 recv_sems.at[slot], k, False).start()
            a += len(g)
        token[...] = jnp.zeros_like(token)

    hbm = pl.BlockSpec(memory_space=pltpu.HBM)
    sem = pl.BlockSpec(memory_space=pltpu.SEMAPHORE)
    sem_shapes = []
    for g in groups:
        sem_shapes += [pltpu.SemaphoreType.DMA((len(g) * (N_DEV - 1),))] * 2
    outs = pl.pallas_call(
        body, name=name,
        in_specs=[hbm] * (2 * n),
        out_shape=sem_shapes + [pltpu.HBM(a.shape, a.dtype) for a in arrs]
        + [pltpu.HBM(z.shape, z.dtype) for z in lands] + [jax.ShapeDtypeStruct((8, 128), F32)],
        out_specs=[sem] * (2 * n_groups) + [hbm] * (2 * n)
        + [pl.BlockSpec(memory_space=pltpu.VMEM)],
        input_output_aliases={i: 2 * n_groups + i for i in range(2 * n)},
        compiler_params=pltpu.CompilerParams(
            has_side_effects=pltpu.SideEffectType.DATAFLOW_SIDE_EFFECTING),
    )(*[pltpu.with_memory_space_constraint(a, pltpu.HBM) for a in arrs],
      *[pltpu.with_memory_space_constraint(lax.empty(z.shape, z.dtype), pltpu.HBM) for z in lands])
    sems = outs[:2 * n_groups]
    thru = outs[2 * n_groups:2 * n_groups + n]
    zones = outs[2 * n_groups + n:2 * n_groups + 2 * n]
    handles, a = [], 0
    for gi, g in enumerate(groups):
        handles.append((sems[2 * gi], sems[2 * gi + 1], thru[a:a + len(g)], zones[a:a + len(g)]))
        a += len(g)
    return handles, outs[-1]


def _exchange_wait(handle, after, name, gather):
    send_sems, recv_sems, thru, zones = handle
    n = len(thru)

    def body(*refs):
        srcs, lands = refs[:n], refs[n:2 * n]
        ssem, rsem = refs[2 * n], refs[2 * n + 1]
        for k in range(1, N_DEV):
            for ai in range(n):
                slot = ai * (N_DEV - 1) + k - 1
                cp = _remote_copy(gather, srcs[ai], lands[ai], ssem.at[slot], rsem.at[slot], k, True)
                cp.wait_send()
                cp.wait_recv()

    hbm = pl.BlockSpec(memory_space=pltpu.HBM)
    sem = pl.BlockSpec(memory_space=pltpu.SEMAPHORE)
    outs = pl.pallas_call(
        body, name=name,
        in_specs=[hbm] * (2 * n) + [sem, sem, pl.BlockSpec(memory_space=pl.ANY)],
        out_shape=[pltpu.HBM(a.shape, a.dtype) for a in thru]
        + [pltpu.HBM(z.shape, z.dtype) for z in zones],
        out_specs=[hbm] * (2 * n),
        input_output_aliases={i: i for i in range(2 * n)},
        compiler_params=pltpu.CompilerParams(
            has_side_effects=pltpu.SideEffectType.DATAFLOW_SIDE_EFFECTING),
    )(*thru, *zones, send_sems, recv_sems, after)
    return outs[:n], outs[n:]


def _own_block(zone, block):
    x, y, c = _mesh_pos()
    me = 4 * x + 2 * y + c
    return lax.dynamic_update_slice_in_dim(zone, block[None], me, axis=0)


def _proj_fwd(x, g1, wcat, bdiag, gq, gk, bfor, tri, pdq, pdk, ones_q, ones_k):
    s_len = x.shape[0]
    tm = TOKEN_TILE
    nt = s_len // tm

    def body(x_ref, g1_ref, w_ref, bd_ref, gq_ref, gk_ref, bf_ref, tri_ref, pdq_ref,
             pdk_ref, oq_ref, ok_ref,
             h_ref, qa_ref, ka_ref, kat_ref, vs_ref, vt_ref, qr_ref, kr_ref, flog_ref, uv_ref,
             gp_ref, carry):
        @pl.when(pl.program_id(0) == 0)
        def _():
            carry[...] = jnp.zeros_like(carry)

        xf = x_ref[...]
        r = lax.rsqrt(jnp.mean(xf * xf, axis=-1, keepdims=True) + EPS)
        h = (xf * r * g1_ref[...]).astype(BF16)
        h_ref[...] = h
        dot = functools.partial(jnp.dot, preferred_element_type=F32)

        def proj(lo, hi):
            return _dot_nt(h, w_ref[lo:hi, :])

        flog = proj(C_F, C_UV) + bf_ref[...]
        flog_ref[...] = flog
        lane = lax.broadcasted_iota(jnp.int32, flog.shape, 1)
        logf = jnp.minimum(flog, 0.0) - jnp.log(1.0 + jnp.exp(-jnp.abs(flog)))
        logf = jnp.where(lane < HEADS, logf, 0.0)
        dcum = _tri_dot(tri_ref[...], logf) + carry[...]
        carry[...] = dcum[tm - 1:tm, :]
        d2 = dcum * LOG2E
        d2a = d2.astype(BF16)
        rem = d2 - d2a.astype(F32)
        d2b = rem.astype(BF16)
        d2c = (rem - d2b.astype(F32)).astype(BF16)

        q = proj(C_Q, C_K)
        qr_ref[...] = q.astype(BF16)
        rq = lax.rsqrt(_seg_mean(q * q, bd_ref) + EPS)
        qn = q * rq * (gq_ref[...] * (HEAD_DIM ** -0.5 * LOG2E))
        qa = (_slabs_from_heads(qn) + dot(d2a, pdq_ref[0]) + dot(d2b, pdq_ref[1])
              + dot(d2c, pdq_ref[2]) + oq_ref[...])
        qa_ref[...] = qa.astype(BF16)

        k = proj(C_K, C_V)
        kr_ref[...] = k.astype(BF16)
        rk = lax.rsqrt(_seg_mean(k * k, bd_ref) + EPS)
        kn = k * rk * gk_ref[...]
        ka = (_slabs_from_heads(kn) - dot(d2a, pdk_ref[0]) - dot(d2b, pdk_ref[1])
              - dot(d2c, pdk_ref[2]) + ok_ref[...])
        ka_ref[...] = ka.astype(BF16)
        kat_ref[0] = ka.T.astype(BF16)

        v = proj(C_V, C_F)
        vs_ref[...] = _slabs_from_heads(v).astype(BF16)
        vt_ref[0] = v.T.astype(BF16)
        uv_ref[...] = proj(C_UV, C_G).astype(BF16)
        gp_ref[...] = proj(C_G, C_END).astype(BF16)

    outs = [((s_len, D_MODEL), BF16, _row_spec(tm, D_MODEL)),
            ((s_len, SLAB_W), BF16, _row_spec(tm, SLAB_W)),
            ((s_len, SLAB_W), BF16, _row_spec(tm, SLAB_W)),
            ((nt, SLAB_W, tm), BF16, _tile_spec(SLAB_W, tm)),
            ((s_len, SLAB_W), BF16, _row_spec(tm, SLAB_W)),
            ((nt, FOX_W, tm), BF16, _tile_spec(FOX_W, tm)),
            ((s_len, FOX_W), BF16, _row_spec(tm, FOX_W)),
            ((s_len, FOX_W), BF16, _row_spec(tm, FOX_W)),
            ((s_len, 128), F32, _row_spec(tm, 128)),
            ((s_len, 2 * SGU_W), BF16, _row_spec(tm, 2 * SGU_W)),
            ((s_len, 2 * D_MODEL), BF16, _row_spec(tm, 2 * D_MODEL))]
    return pl.pallas_call(
        body, name="proj_fwd", grid=(nt,),
        in_specs=[_row_spec(tm, D_MODEL), _const_spec((1, D_MODEL)), _const_spec(wcat.shape),
                  _const_spec(bdiag.shape), _const_spec((1, FOX_W)), _const_spec((1, FOX_W)),
                  _const_spec((1, 128)), _const_spec((tm, tm)), _const_spec(pdq.shape), _const_spec(pdk.shape), _const_spec(ones_q.shape),
                  _const_spec(ones_k.shape)],
        out_specs=[o[2] for o in outs],
        out_shape=[jax.ShapeDtypeStruct(o[0], o[1]) for o in outs],
        scratch_shapes=[pltpu.VMEM((1, 128), F32)],
        compiler_params=_params(56, 1),
    )(x, g1, wcat, bdiag, gq, gk, bfor, tri, pdq, pdk, ones_q, ones_k)


def _attn_fwd(qa, ka, vt):
    s_len = qa.shape[0]
    t = ATTN_TILE
    nb = s_len // t

    def body(q_ref, k_ref, vt_ref, o_ref, ot_ref, lse_ref, m_sc, l_sc, acc_sc, s_sc, mcur_sc,
             alpha_sc):
        i = pl.program_id(0)
        m_sc[...] = jnp.full_like(m_sc, -jnp.inf)
        l_sc[...] = jnp.zeros_like(l_sc)
        acc_sc[...] = jnp.zeros_like(acc_sc)

        def logits(j, slot, masked):
            krows = pl.ds(pl.multiple_of(j * t, t), t)
            if masked:
                keep = (lax.broadcasted_iota(jnp.int32, (t, t), 0)
                        <= lax.broadcasted_iota(jnp.int32, (t, t), 1))
            for hd in range(HEADS):
                sl = slice(hd * 128, (hd + 1) * 128)
                st = _dot_nt(k_ref[krows, sl], q_ref[:, sl])
                if masked:
                    st = jnp.where(keep, st, -jnp.inf)
                s_sc[slot, hd] = st
                m_prev = m_sc[hd:hd + 1, :]
                m_new = jnp.maximum(m_prev, jnp.max(st, axis=0, keepdims=True))
                alpha_sc[slot, hd:hd + 1, :] = jnp.exp2(m_prev - m_new)
                mcur_sc[slot, hd:hd + 1, :] = m_new
                m_sc[hd:hd + 1, :] = m_new

        def accumulate(j, slot):
            for hd in range(HEADS):
                hr = slice(hd * HEAD_DIM, (hd + 1) * HEAD_DIM)
                alpha = alpha_sc[slot, hd:hd + 1, :]
                pt = jnp.exp2(s_sc[slot, hd] - mcur_sc[slot, hd:hd + 1, :])
                l_sc[hd:hd + 1, :] = alpha * l_sc[hd:hd + 1, :] + jnp.sum(pt, axis=0, keepdims=True)
                acc_sc[hr, :] = alpha * acc_sc[hr, :] + jnp.dot(
                    vt_ref[j, hr, :], pt.astype(BF16), preferred_element_type=F32)

        @pl.when(i == 0)
        def _():
            logits(0, 0, True)
            accumulate(0, 0)

        pairs = (i - 1) // 2

        @pl.when(i > 0)
        def _():
            logits(0, 0, False)

            def two_blocks(p, carry):
                logits(2 * p + 1, 1, False)
                accumulate(2 * p, 0)
                logits(2 * p + 2, 0, False)
                accumulate(2 * p + 1, 1)
                return carry

            lax.fori_loop(0, pairs, two_blocks, 0)

        @pl.when((i > 0) & (i - 2 * pairs == 1))
        def _():
            logits(i, 1, True)
            accumulate(i - 1, 0)
            accumulate(i, 1)

        @pl.when((i > 0) & (i - 2 * pairs == 2))
        def _():
            logits(i - 1, 1, False)
            accumulate(i - 2, 0)
            logits(i, 0, True)
            accumulate(i - 1, 1)
            accumulate(i, 0)

        for hd in range(HEADS):
            hr = slice(hd * HEAD_DIM, (hd + 1) * HEAD_DIM)
            l = l_sc[hd:hd + 1, :]
            acc_sc[hr, :] = acc_sc[hr, :] / l
            lse_ref[0, hd:hd + 1, :] = m_sc[hd:hd + 1, :] + jnp.log2(l)
        o_ref[...] = acc_sc[...].T.astype(BF16)
        ot_ref[...] = acc_sc[...].astype(BF16)

    return pl.pallas_call(
        body, name="attn_fwd", grid=(nb,),
        in_specs=[_row_spec(t, SLAB_W), _const_spec(ka.shape), _const_spec(vt.shape)],
        out_specs=[_row_spec(t, FOX_W), pl.BlockSpec((FOX_W, t), lambda i: (0, i)),
                   _tile_spec(HEADS, t)],
        out_shape=[jax.ShapeDtypeStruct((s_len, FOX_W), BF16),
                   jax.ShapeDtypeStruct((FOX_W, s_len), BF16),
                   jax.ShapeDtypeStruct((nb, HEADS, t), F32)],
        scratch_shapes=[pltpu.VMEM((HEADS, t), F32), pltpu.VMEM((HEADS, t), F32),
                        pltpu.VMEM((FOX_W, t), F32), pltpu.VMEM((2, HEADS, t, t), F32),
                        pltpu.VMEM((2, HEADS, t), F32), pltpu.VMEM((2, HEADS, t), F32)],
        compiler_params=_params(48, 1),
    )(qa, ka, vt)


def _sgu_mix(vn, ws_ref):
    tm = vn.shape[0]
    lane = lax.broadcasted_iota(jnp.int32, (WINDOW, 128), 1)
    low = lane < HEAD_DIM
    wins = []
    for w in range(tm // WINDOW):
        slabs = []
        for p in range(GROUPS // 2):
            v2 = vn[w * WINDOW:(w + 1) * WINDOW, p * 128:(p + 1) * 128]
            lo = jnp.where(low, v2, 0.0).astype(BF16)
            hi = jnp.where(low, 0.0, v2).astype(BF16)
            slabs.append(jnp.dot(ws_ref[2 * p], lo, preferred_element_type=F32)
                         + jnp.dot(ws_ref[2 * p + 1], hi, preferred_element_type=F32))
        wins.append(jnp.concatenate(slabs, axis=1))
    return jnp.concatenate(wins, axis=0) if len(wins) > 1 else wins[0]


def _layernorm_fwd(vv, g, b):
    mu = jnp.mean(vv, axis=-1, keepdims=True)
    xc = vv - mu
    r = lax.rsqrt(jnp.mean(xc * xc, axis=-1, keepdims=True) + EPS)
    xh = xc * r
    return xh * g + b, xh, r


def _mix_fwd(attn, uvpre, gpre, x, wa, wb, wout, wsm, bsf, gsgu, bsgu, gpost):
    s_len = x.shape[0]
    tm = TOKEN_TILE

    def body(o_ref, uv_ref, gp_ref, x_ref, wa_ref, wb_ref, wo_ref, ws_ref, bs_ref, gs_ref, bsg_ref,
             gpost_ref, sgut_ref, ya_ref, yb_ref, mgt_ref, om_ref, x1_ref):
        uvp = uv_ref[...].astype(F32)
        uv, _ = _gelu_and_grad(uvp)
        u, vv = uv[:, :SGU_W], uv[:, SGU_W:]
        vn, _, _ = _layernorm_fwd(vv, gs_ref[...], bsg_ref[...])
        bias = bs_ref[...]
        if tm > WINDOW:
            bias = jnp.concatenate([bias] * (tm // WINDOW), axis=0)
        mixed = _sgu_mix(vn, ws_ref) + bias
        sgu_f = u * mixed
        sgu = sgu_f.astype(BF16)
        sgut_ref[...] = sgu_f.T.astype(BF16)
        ya = jnp.dot(o_ref[...], wa_ref[...], preferred_element_type=F32)
        yb = jnp.dot(sgu, wb_ref[...], preferred_element_type=F32)
        ya_ref[...] = ya.astype(BF16)
        yb_ref[...] = yb.astype(BF16)
        gates = _sigmoid(gp_ref[...].astype(F32))
        merged_f = gates[:, :D_MODEL] * ya + gates[:, D_MODEL:] * yb
        merged = merged_f.astype(BF16)
        mgt_ref[...] = merged_f.T.astype(BF16)
        om = jnp.dot(merged, wo_ref[...], preferred_element_type=F32)
        om_ref[...] = om
        r = lax.rsqrt(jnp.mean(om * om, axis=-1, keepdims=True) + EPS)
        x1_ref[...] = x_ref[...] + om * r * gpost_ref[...]

    def t_out(rows):
        return ((rows, s_len), BF16, pl.BlockSpec((rows, tm), lambda i: (0, i)))

    def r_out(cols, dt):
        return ((s_len, cols), dt, _row_spec(tm, cols))

    outs = [t_out(SGU_W), r_out(D_MODEL, BF16), r_out(D_MODEL, BF16), t_out(D_MODEL),
            r_out(D_MODEL, F32), r_out(D_MODEL, F32)]
    return pl.pallas_call(
        body, name="mix_fwd", grid=(s_len // tm,),
        in_specs=[_row_spec(tm, FOX_W), _row_spec(tm, 2 * SGU_W), _row_spec(tm, 2 * D_MODEL),
                  _row_spec(tm, D_MODEL), _const_spec(wa.shape), _const_spec(wb.shape),
                  _const_spec(wout.shape), _const_spec(wsm.shape), _const_spec(bsf.shape),
                  _const_spec((1, SGU_W)), _const_spec((1, SGU_W)), _const_spec((1, D_MODEL))],
        out_specs=[o[2] for o in outs],
        out_shape=[jax.ShapeDtypeStruct(o[0], o[1]) for o in outs],
        compiler_params=_params(48, 1),
    )(attn, uvpre, gpre, x, wa, wb, wout, wsm, bsf, gsgu, bsgu, gpost)


def _ffn_fwd_bwd(x1, tgt, wffn, wdown, gpre, gpost):
    s_len = x1.shape[0]
    tm = TOKEN_TILE

    def body(x1_ref, t_ref, wi_ref, wd_ref, gpre_ref, gpost_ref,
             dx1_ref, h2_ref, actt_ref, dff_ref, dgut_ref, loss_ref, dgpost_ref, dgpre_ref):
        @pl.when(pl.program_id(0) == 0)
        def _():
            loss_ref[...] = jnp.zeros_like(loss_ref)
            dgpost_ref[...] = jnp.zeros_like(dgpost_ref)
            dgpre_ref[...] = jnp.zeros_like(dgpre_ref)

        x1v = x1_ref[...]
        r2 = lax.rsqrt(jnp.mean(x1v * x1v, axis=-1, keepdims=True) + EPS)
        gpre_v = gpre_ref[...]
        h2 = (x1v * r2 * gpre_v).astype(BF16)
        h2_ref[...] = h2
        gg = _dot_nt(h2, wi_ref[:D_FF, :])
        uu = _dot_nt(h2, wi_ref[D_FF:, :])
        sg = _sigmoid(gg)
        silu = gg * sg
        act_f = silu * uu
        act = act_f.astype(BF16)
        actt_ref[...] = act_f.T.astype(BF16)
        ff = jnp.dot(act, wd_ref[...], preferred_element_type=F32)
        r3 = lax.rsqrt(jnp.mean(ff * ff, axis=-1, keepdims=True) + EPS)
        gpost_v = gpost_ref[...]
        y = x1v + ff * r3 * gpost_v
        err = y - t_ref[...]
        loss_ref[...] += jnp.sum(err * err) * (0.5 / D_MODEL)
        dy = err * (1.0 / D_MODEL)
        dgpost_ref[...] += jnp.sum(dy * ff * r3, axis=0, keepdims=True)
        dff = _rms_bwd(ff, r3, gpost_v, dy).astype(BF16)
        dff_ref[...] = dff
        dact = _dot_nt(dff, wd_ref[...])
        dgg_f = dact * uu * (sg * (1.0 + gg * (1.0 - sg)))
        duu_f = dact * silu
        dgg = dgg_f.astype(BF16)
        duu = duu_f.astype(BF16)
        dgut_ref[:D_FF, :] = dgg_f.T.astype(BF16)
        dgut_ref[D_FF:, :] = duu_f.T.astype(BF16)
        dh2 = (jnp.dot(dgg, wi_ref[:D_FF, :], preferred_element_type=F32)
               + jnp.dot(duu, wi_ref[D_FF:, :], preferred_element_type=F32))
        dgpre_ref[...] += jnp.sum(dh2 * x1v * r2, axis=0, keepdims=True)
        dx1_ref[...] = dy + _rms_bwd(x1v, r2, gpre_v, dh2)

    outs = [((s_len, D_MODEL), F32, _row_spec(tm, D_MODEL)),
            ((s_len, D_MODEL), BF16, _row_spec(tm, D_MODEL)),
            ((D_FF, s_len), BF16, pl.BlockSpec((D_FF, tm), lambda i: (0, i))),
            ((s_len, D_MODEL), BF16, _row_spec(tm, D_MODEL)),
            ((2 * D_FF, s_len), BF16, pl.BlockSpec((2 * D_FF, tm), lambda i: (0, i))),
            ((1, 128), F32, _const_spec((1, 128))),
            ((1, D_MODEL), F32, _const_spec((1, D_MODEL))),
            ((1, D_MODEL), F32, _const_spec((1, D_MODEL)))]
    return pl.pallas_call(
        body, name="ffn_fwd_bwd", grid=(s_len // tm,),
        in_specs=[_row_spec(tm, D_MODEL), _row_spec(tm, D_MODEL), _const_spec(wffn.shape),
                  _const_spec(wdown.shape), _const_spec((1, D_MODEL)), _const_spec((1, D_MODEL))],
        out_specs=[o[2] for o in outs],
        out_shape=[jax.ShapeDtypeStruct(o[0], o[1]) for o in outs],
        compiler_params=_params(60, 1),
    )(x1, tgt, wffn, wdown, gpre, gpost)


def _mix_bwd(dx1, om, ya, yb, gpre, uvpre, attn, wout, wa, wb, wsm, wsmt, bsf, gsgu, bsgu, gpost,
             wmask, egrp):
    s_len = dx1.shape[0]
    tm = TOKEN_TILE
    nw = tm // WINDOW
    nt = s_len // tm

    def body(dx1_ref, om_ref, ya_ref, yb_ref, gp_ref, uv_ref, o_ref, wo_ref, wa_ref, wb_ref, ws_ref,
             wst_ref, bs_ref, gs_ref, bsg_ref, gpost_ref, mask_ref, eg_ref,
             dom_ref, dya_ref, dyb_ref, dgp_ref, dot_ref, delta_ref, duv_ref,
             dws_ref, dbs_ref, dgs_ref, dbsg_ref, dgpost_ref, dbs_acc):
        step = pl.program_id(0)

        @pl.when(step == 0)
        def _():
            dws_ref[...] = jnp.zeros_like(dws_ref)
            dbs_acc[...] = jnp.zeros_like(dbs_acc)
            dgs_ref[...] = jnp.zeros_like(dgs_ref)
            dbsg_ref[...] = jnp.zeros_like(dbsg_ref)
            dgpost_ref[...] = jnp.zeros_like(dgpost_ref)

        om = om_ref[...]
        dx1v = dx1_ref[...]
        r = lax.rsqrt(jnp.mean(om * om, axis=-1, keepdims=True) + EPS)
        gpost_v = gpost_ref[...]
        dgpost_ref[...] += jnp.sum(dx1v * om * r, axis=0, keepdims=True)
        dom = _rms_bwd(om, r, gpost_v, dx1v).astype(BF16)
        dom_ref[...] = dom
        dmg = _dot_nt(dom, wo_ref[...])

        gates = _sigmoid(gp_ref[...].astype(F32))
        ga, gb = gates[:, :D_MODEL], gates[:, D_MODEL:]
        yav, ybv = ya_ref[...].astype(F32), yb_ref[...].astype(F32)
        dya = (dmg * ga).astype(BF16)
        dyb = (dmg * gb).astype(BF16)
        dya_ref[...] = dya
        dyb_ref[...] = dyb
        dgp_ref[:, :D_MODEL] = (dmg * yav * ga * (1.0 - ga)).astype(BF16)
        dgp_ref[:, D_MODEL:] = (dmg * ybv * gb * (1.0 - gb)).astype(BF16)

        dat_t = _dot_nt(dya, wa_ref[...]).T.astype(BF16)
        dot_ref[0] = dat_t
        o_t = o_ref[...].astype(F32).T
        delta_ref[0] = jnp.sum((dat_t.astype(F32) * o_t).reshape(HEADS, HEAD_DIM, tm), axis=1)
        dsgu = _dot_nt(dyb, wb_ref[...])

        uvp = uv_ref[...].astype(F32)
        uv, guv = _gelu_and_grad(uvp)
        u, vv = uv[:, :SGU_W], uv[:, SGU_W:]
        gs_v = gs_ref[...]
        vn, xh, rln = _layernorm_fwd(vv, gs_v, bsg_ref[...])
        bias = bs_ref[...]
        if nw > 1:
            bias = jnp.concatenate([bias] * nw, axis=0)
        mixed = _sgu_mix(vn, ws_ref) + bias
        du = dsgu * mixed
        dmixed = dsgu * u

        lane = lax.broadcasted_iota(jnp.int32, (WINDOW, 128), 1)
        low = lane < HEAD_DIM
        dvn_wins = []
        for w in range(nw):
            rows = slice(w * WINDOW, (w + 1) * WINDOW)
            dbs_acc[...] += dmixed[rows, :]
            slabs = []
            for p in range(GROUPS // 2):
                cols = slice(p * 128, (p + 1) * 128)
                dm2 = dmixed[rows, cols]
                dlo = jnp.where(low, dm2, 0.0).astype(BF16)
                dhi = jnp.where(low, 0.0, dm2).astype(BF16)
                vn2 = vn[rows, cols].astype(BF16)
                dws_ref[2 * p] += _dot_nt(dlo, vn2)
                dws_ref[2 * p + 1] += _dot_nt(dhi, vn2)
                slabs.append(jnp.dot(wst_ref[2 * p], dlo, preferred_element_type=F32)
                             + jnp.dot(wst_ref[2 * p + 1], dhi, preferred_element_type=F32))
            dvn_wins.append(jnp.concatenate(slabs, axis=1))
        dvn = jnp.concatenate(dvn_wins, axis=0) if nw > 1 else dvn_wins[0]

        dgs_ref[...] += jnp.sum(dvn * xh, axis=0, keepdims=True)
        dbsg_ref[...] += jnp.sum(dvn, axis=0, keepdims=True)
        dxh = dvn * gs_v
        dvv = rln * (dxh - jnp.mean(dxh, axis=-1, keepdims=True)
                     - xh * jnp.mean(dxh * xh, axis=-1, keepdims=True))
        duv_ref[:, :SGU_W] = (du * guv[:, :SGU_W]).astype(BF16)
        duv_ref[:, SGU_W:] = (dvv * guv[:, SGU_W:]).astype(BF16)

        @pl.when(step == pl.num_programs(0) - 1)
        def _():
            for g in range(GROUPS):
                dws_ref[g] = dws_ref[g] * mask_ref[...]
            dbs_ref[...] = _split3_dot(dbs_acc[...], eg_ref[...])

    rows_out = [((s_len, D_MODEL), BF16, _row_spec(tm, D_MODEL)),
                ((s_len, D_MODEL), BF16, _row_spec(tm, D_MODEL)),
                ((s_len, D_MODEL), BF16, _row_spec(tm, D_MODEL)),
                ((s_len, 2 * D_MODEL), BF16, _row_spec(tm, 2 * D_MODEL)),
                ((nt, FOX_W, tm), BF16, _tile_spec(FOX_W, tm)),
                ((nt, HEADS, tm), F32, _tile_spec(HEADS, tm)),
                ((s_len, 2 * SGU_W), BF16, _row_spec(tm, 2 * SGU_W))]
    acc_out = [((GROUPS, WINDOW, WINDOW), F32), ((WINDOW, 128), F32), ((1, SGU_W), F32),
               ((1, SGU_W), F32), ((1, D_MODEL), F32)]
    return pl.pallas_call(
        body, name="mix_bwd", grid=(nt,),
        in_specs=[_row_spec(tm, D_MODEL), _row_spec(tm, D_MODEL), _row_spec(tm, D_MODEL),
                  _row_spec(tm, D_MODEL), _row_spec(tm, 2 * D_MODEL), _row_spec(tm, 2 * SGU_W),
                  _row_spec(tm, FOX_W), _const_spec(wout.shape), _const_spec(wa.shape),
                  _const_spec(wb.shape), _const_spec(wsm.shape), _const_spec(wsmt.shape),
                  _const_spec(bsf.shape), _const_spec((1, SGU_W)), _const_spec((1, SGU_W)),
                  _const_spec((1, D_MODEL)), _const_spec(wmask.shape), _const_spec(egrp.shape)],
        out_specs=[o[2] for o in rows_out] + [_const_spec(s) for s, _ in acc_out],
        out_shape=[jax.ShapeDtypeStruct(o[0], o[1]) for o in rows_out]
        + [jax.ShapeDtypeStruct(s, dt) for s, dt in acc_out],
        scratch_shapes=[pltpu.VMEM((WINDOW, SGU_W), F32)],
        compiler_params=_params(48, 1),
    )(dx1, om, ya, yb, gpre, uvpre, attn, wout, wa, wb, wsm, wsmt, bsf, gsgu, bsgu, gpost, wmask,
      egrp)


def _attn_bwd(qa, ka, kat, vs, dot_, lse, delta, ecol):
    s_len = qa.shape[0]
    t = ATTN_TILE
    nb = s_len // t

    def body(k_ref, kt_ref, vs_ref, q_ref, do_ref, lse_ref, dl_ref, ec_ref, gk_ref, dvt_ref,
             gqt_ref, csum_ref, p_sc, ds_sc):
        j = pl.program_id(0)

        @pl.when(j == 0)
        def _():
            gqt_ref[...] = jnp.zeros_like(gqt_ref)

        gk_ref[...] = jnp.zeros_like(gk_ref)
        dvt_ref[...] = jnp.zeros_like(dvt_ref)

        def probs(i, slot, masked):
            qrows = pl.ds(pl.multiple_of(i * t, t), t)
            if masked:
                keep = (lax.broadcasted_iota(jnp.int32, (t, t), 0)
                        <= lax.broadcasted_iota(jnp.int32, (t, t), 1))
            for hd in range(HEADS):
                sl = slice(hd * 128, (hd + 1) * 128)
                hr = slice(hd * HEAD_DIM, (hd + 1) * HEAD_DIM)
                st = _dot_nt(k_ref[:, sl], q_ref[qrows, sl])
                if masked:
                    st = jnp.where(keep, st, -jnp.inf)
                pt = jnp.exp2(st - lse_ref[i, hd:hd + 1, :])
                dpt = jnp.dot(vs_ref[:, hd * 128:hd * 128 + HEAD_DIM], do_ref[i, hr, :],
                              preferred_element_type=F32)
                p_sc[slot, hd] = pt.astype(BF16)
                ds_sc[slot, hd] = (pt * (dpt - dl_ref[i, hd:hd + 1, :])).astype(BF16)

        def grads(i, slot):
            qrows = pl.ds(pl.multiple_of(i * t, t), t)
            for hd in range(HEADS):
                sl = slice(hd * 128, (hd + 1) * 128)
                hr = slice(hd * HEAD_DIM, (hd + 1) * HEAD_DIM)
                dst = ds_sc[slot, hd]
                dvt_ref[0, hr, :] += _dot_nt(do_ref[i, hr, :], p_sc[slot, hd])
                gk_ref[:, sl] += jnp.dot(dst, q_ref[qrows, sl], preferred_element_type=F32)
                gqt_ref[i, hd * QT_ROWS:(hd + 1) * QT_ROWS, :] += jnp.dot(
                    kt_ref[0, hd * 128:hd * 128 + QT_ROWS, :], dst, preferred_element_type=F32)

        probs(j, 0, True)
        pairs = (nb - 1 - j) // 2

        def two_blocks(p, carry):
            i1 = j + 1 + 2 * p
            probs(i1, 1, False)
            grads(i1 - 1, 0)
            probs(i1 + 1, 0, False)
            grads(i1, 1)
            return carry

        lax.fori_loop(0, pairs, two_blocks, 0)

        @pl.when(nb - 1 - j - 2 * pairs == 0)
        def _():
            grads(nb - 1, 0)

        @pl.when(nb - 1 - j - 2 * pairs == 1)
        def _():
            probs(nb - 1, 1, False)
            grads(nb - 2, 0)
            grads(nb - 1, 1)

        csum_ref[...] = _split3_dot(gk_ref[...], ec_ref[...])

    return pl.pallas_call(
        body, name="attn_bwd", grid=(nb,),
        in_specs=[_row_spec(t, SLAB_W), _tile_spec(SLAB_W, t), _row_spec(t, SLAB_W),
                  _const_spec(qa.shape), _const_spec(dot_.shape), _const_spec(lse.shape),
                  _const_spec(delta.shape), _const_spec(ecol.shape)],
        out_specs=[_row_spec(t, SLAB_W), _tile_spec(FOX_W, t),
                   _const_spec((nb, HEADS * QT_ROWS, t)), _row_spec(t, 128)],
        out_shape=[jax.ShapeDtypeStruct((s_len, SLAB_W), F32),
                   jax.ShapeDtypeStruct((nb, FOX_W, t), F32),
                   jax.ShapeDtypeStruct((nb, HEADS * QT_ROWS, t), F32),
                   jax.ShapeDtypeStruct((s_len, 128), F32)],
        scratch_shapes=[pltpu.VMEM((2, HEADS, t, t), BF16), pltpu.VMEM((2, HEADS, t, t), BF16)],
        compiler_params=_params(60, 1),
    )(ka, kat, vs, qa, dot_, lse, delta, ecol)


def _rev_cumsum(col_sums, gqt, triu):
    s_len = col_sums.shape[0]
    tm = TOKEN_TILE
    n = s_len // tm

    def body(cs_ref, gqt_ref, tri_ref, o_ref, carry):
        @pl.when(pl.program_id(0) == 0)
        def _():
            carry[...] = jnp.zeros_like(carry)
        rows = [gqt_ref[0, hd * QT_ROWS + HEAD_DIM:hd * QT_ROWS + HEAD_DIM + 1, :]
                for hd in range(HEADS)]
        row_sums = jnp.concatenate(rows + [jnp.zeros((128 - HEADS, tm), F32)], axis=0).T
        out = _tri_dot(tri_ref[...], row_sums - cs_ref[...]) + carry[...]
        o_ref[...] = out
        carry[...] = out[0:1, :]

    return pl.pallas_call(
        body, name="rev_cumsum", grid=(n,),
        in_specs=[pl.BlockSpec((tm, 128), lambda i: (n - 1 - i, 0)),
                  pl.BlockSpec((1, HEADS * QT_ROWS, tm), lambda i: (n - 1 - i, 0, 0)),
                  _const_spec((tm, tm))],
        out_specs=pl.BlockSpec((tm, 128), lambda i: (n - 1 - i, 0)),
        out_shape=jax.ShapeDtypeStruct((s_len, 128), F32),
        scratch_shapes=[pltpu.VMEM((1, 128), F32)],
        compiler_params=_params(32, 1),
    )(col_sums, gqt, triu)


def _heads_from_slabs(slabs):
    lane = lax.broadcasted_iota(jnp.int32, slabs[0].shape, 1)
    low = lane < HEAD_DIM
    pairs = [jnp.where(low, slabs[2 * p], pltpu.roll(slabs[2 * p + 1], HEAD_DIM, 1))
             for p in range(HEADS // 2)]
    return jnp.concatenate(pairs, axis=1)


def _proj_bwd(gqt, gk, dvt, dlogf, flog, qraw, kraw, duv, dgp, x, dx1, wcat, bdiag, gq, gk_gain, g1,
              efold):
    s_len = x.shape[0]
    tm = TOKEN_TILE

    def body(gqt_ref, gkk_ref, dvt_ref, dlf_ref, flog_ref, qr_ref, kr_ref, duv_ref, dgp_ref, x_ref,
             dx1_ref, w_ref, bd_ref, gq_ref, gk_ref, g1_ref, ef_ref,
             dx_ref, dprojt_ref, dgq_ref, dgk_ref, dbf_ref, dg1_ref, gq_acc, gk_acc, dproj_ref):
        step = pl.program_id(0)

        @pl.when(step == 0)
        def _():
            gq_acc[...] = jnp.zeros_like(gq_acc)
            gk_acc[...] = jnp.zeros_like(gk_acc)
            dbf_ref[...] = jnp.zeros_like(dbf_ref)
            dg1_ref[...] = jnp.zeros_like(dg1_ref)

        pad = jnp.zeros((128 - QT_ROWS, tm), F32)
        q_slabs = [jnp.concatenate([gqt_ref[0, hd * QT_ROWS:(hd + 1) * QT_ROWS, :], pad], axis=0).T
                   for hd in range(HEADS)]
        dqn = _heads_from_slabs(q_slabs)
        dkn = _heads_from_slabs([gkk_ref[:, hd * 128:(hd + 1) * 128] for hd in range(HEADS)])

        def head_bwd(raw_ref, dn, g_ref, acc):
            raw = raw_ref[...].astype(F32)
            r = lax.rsqrt(_seg_mean(raw * raw, bd_ref) + EPS)
            xhat = raw * r
            acc[0:1, :] += jnp.sum(dn * xhat, axis=0, keepdims=True)
            dyg = dn * g_ref[...]
            return r * (dyg - xhat * _seg_mean(dyg * xhat, bd_ref))

        dproj_ref[:, C_Q:C_K] = head_bwd(qr_ref, dqn * HEAD_DIM ** -0.5, gq_ref, gq_acc).astype(BF16)
        dproj_ref[:, C_K:C_V] = head_bwd(kr_ref, dkn * LN2, gk_ref, gk_acc).astype(BF16)
        dproj_ref[:, C_V:C_F] = dvt_ref[0].T.astype(BF16)
        dfl = dlf_ref[...] * _sigmoid(-flog_ref[...])
        dbf_ref[...] += jnp.sum(dfl, axis=0, keepdims=True)
        dproj_ref[:, C_F:C_UV] = dfl.astype(BF16)
        dproj_ref[:, C_UV:C_G] = duv_ref[...]
        dproj_ref[:, C_G:C_END] = dgp_ref[...]

        dproj = dproj_ref[...]
        dprojt_ref[...] = dproj.astype(F32).T.astype(BF16)
        dh = jnp.dot(dproj, w_ref[...], preferred_element_type=F32)
        xf = x_ref[...]
        r = lax.rsqrt(jnp.mean(xf * xf, axis=-1, keepdims=True) + EPS)
        dg1_ref[...] += jnp.sum(dh * xf * r, axis=0, keepdims=True)
        dx_ref[...] = dx1_ref[...] + _rms_bwd(xf, r, g1_ref[...], dh)

        @pl.when(step == pl.num_programs(0) - 1)
        def _():
            dgq_ref[...] = _split3_dot(gq_acc[...], ef_ref[...])
            dgk_ref[...] = _split3_dot(gk_acc[...], ef_ref[...])

    outs = [((s_len, D_MODEL), F32, _row_spec(tm, D_MODEL)),
            ((C_END, s_len), BF16, pl.BlockSpec((C_END, tm), lambda i: (0, i))),
            ((8, 128), F32, _const_spec((8, 128))),
            ((8, 128), F32, _const_spec((8, 128))),
            ((1, 128), F32, _const_spec((1, 128))),
            ((1, D_MODEL), F32, _const_spec((1, D_MODEL)))]
    return pl.pallas_call(
        body, name="proj_bwd", grid=(s_len // tm,),
        in_specs=[_tile_spec(HEADS * QT_ROWS, tm), _row_spec(tm, SLAB_W), _tile_spec(FOX_W, tm),
                  _row_spec(tm, 128), _row_spec(tm, 128), _row_spec(tm, FOX_W),
                  _row_spec(tm, FOX_W), _row_spec(tm, 2 * SGU_W), _row_spec(tm, 2 * D_MODEL),
                  _row_spec(tm, D_MODEL), _row_spec(tm, D_MODEL), _const_spec(wcat.shape),
                  _const_spec(bdiag.shape), _const_spec((1, FOX_W)), _const_spec((1, FOX_W)),
                  _const_spec((1, D_MODEL)), _const_spec(efold.shape)],
        out_specs=[o[2] for o in outs],
        out_shape=[jax.ShapeDtypeStruct(o[0], o[1]) for o in outs],
        scratch_shapes=[pltpu.VMEM((8, FOX_W), F32), pltpu.VMEM((8, FOX_W), F32),
                        pltpu.VMEM((tm, C_END), BF16)],
        compiler_params=_params(56, 1),
    )(gqt, gk, dvt, dlogf, flog, qraw, kraw, duv, dgp, x, dx1, wcat, bdiag, gq, gk_gain, g1, efold)


def _dw_matmul(at, b, tm, name, after=()):
    m, s_len = at.shape
    n = b.shape[1]

    def body(a_ref, b_ref, *rest):
        rest[-1][...] = jnp.dot(a_ref[...], b_ref[...], preferred_element_type=F32).astype(BF16)

    return pl.pallas_call(
        body, name=name, grid=(m // tm,),
        in_specs=[pl.BlockSpec((tm, s_len), lambda i: (i, 0)), _const_spec(b.shape)]
        + [pl.BlockSpec(memory_space=pl.ANY)] * len(after),
        out_specs=pl.BlockSpec((tm, n), lambda i: (i, 0)),
        out_shape=jax.ShapeDtypeStruct((m, n), BF16),
        compiler_params=_params(48, 1),
    )(at, b, *after)


def _adamw(parts, w, m, v, tr, name, col_tile=None):
    parts = parts if isinstance(parts, (list, tuple)) else [parts]
    rows, cols = w.shape
    bc1 = 1.0 - ADAM_B1 ** ADAM_STEP
    bc2 = 1.0 - ADAM_B2 ** ADAM_STEP

    def body(*refs):
        p_refs = refs[:len(parts)]
        w_ref, m_ref, v_ref, g_ref, d_ref, mo_ref, vo_ref = refs[len(parts):]
        g = None
        for p_ref, p in zip(p_refs, parts):
            for idx in range(p.shape[0]):
                term = p_ref[idx].astype(F32)
                g = term if g is None else g + term
        g_ref[...] = g
        mn = ADAM_B1 * m_ref[...] + (1.0 - ADAM_B1) * g
        vn = ADAM_B2 * v_ref[...] + (1.0 - ADAM_B2) * (g * g)
        mo_ref[...] = mn
        vo_ref[...] = vn
        m_hat = mn / bc1
        v_hat = vn / bc2
        d_ref[...] = -ADAM_LR * (m_hat / (jnp.sqrt(v_hat) + ADAM_EPS) + ADAM_WD * w_ref[...])

    if col_tile is None:
        spec = pl.BlockSpec((tr, cols), lambda i: (i, 0))
        pspecs = [pl.BlockSpec((p.shape[0], tr, cols), lambda i: (0, i, 0)) for p in parts]
        steps = rows // tr
    else:
        spec = pl.BlockSpec((rows, col_tile), lambda i: (0, i))
        pspecs = [pl.BlockSpec((p.shape[0], rows, col_tile), lambda i: (0, 0, i)) for p in parts]
        steps = cols // col_tile
    return pl.pallas_call(
        body, name=name, grid=(steps,),
        in_specs=pspecs + [spec, spec, spec],
        out_specs=[spec] * 4,
        out_shape=[jax.ShapeDtypeStruct((rows, cols), F32)] * 4,
        compiler_params=_params(48, 1),
    )(*parts, w, m, v)


def _sum_parts(parts, name):
    n, rows, cols = parts.shape

    def body(p_ref, o_ref):
        g = p_ref[0]
        for idx in range(1, n):
            g = g + p_ref[idx]
        o_ref[...] = g

    return pl.pallas_call(
        body, name=name, out_shape=jax.ShapeDtypeStruct((rows, cols), F32),
        in_specs=[_const_spec(parts.shape)], out_specs=_const_spec((rows, cols)), grid=(1,),
        compiler_params=_params(16, 1),
    )(parts)


SMALL_NAMES = ("g_pre_mix", "b_forget", "g_q", "g_k", "g_sgu", "b_sgu", "w_spatial", "b_spatial",
               "g_post_mix", "g_pre_ffn", "g_post_ffn")


def _small_rows(size):
    return -(-size // 1024)


def _pack_small(d, extra=None):
    rows = []
    for k in SMALL_NAMES:
        flat = d[k].reshape(-1).astype(F32)
        nr = _small_rows(flat.shape[0])
        rows.append(jnp.pad(flat, (0, nr * 1024 - flat.shape[0])).reshape(nr, 1024))
    if extra is not None:
        rows.append(extra)
    used = sum(r.shape[0] for r in rows)
    rows.append(jnp.zeros((N_DEV * SMALL_ROWS - used, 1024), F32))
    return jnp.concatenate(rows, axis=0)


def _unpack_small(packed, shapes):
    out, off = {}, 0
    for k in SMALL_NAMES:
        size = math.prod(shapes[k])
        nr = _small_rows(size)
        out[k] = packed[off:off + nr].reshape(-1)[:size].reshape(shapes[k])
        off += nr
    return out


def _cols_to_blocks(full, width):
    r = full.shape[0]
    return jnp.transpose(full.reshape(r, N_DEV, width), (1, 0, 2))


def _blocks_to_cols(blocks):
    n, r, width = blocks.shape
    return jnp.transpose(blocks, (1, 0, 2)).reshape(r, n * width)


def kernel(x, g_pre_mix, w_in, b_forget, g_q, g_k, g_sgu, b_sgu, w_spatial, b_spatial, w_branch_a, w_branch_b, w_out, g_post_mix, g_pre_ffn, w_ffn_in, w_ffn_down, g_post_ffn, loss_target, m_g_pre_mix, m_w_in, m_b_forget, m_g_q, m_g_k, m_g_sgu, m_b_sgu, m_w_spatial, m_b_spatial, m_w_branch_a, m_w_branch_b, m_w_out, m_g_post_mix, m_g_pre_ffn, m_w_ffn_in, m_w_ffn_down, m_g_post_ffn, v_g_pre_mix, v_w_in, v_b_forget, v_g_q, v_g_k, v_g_sgu, v_b_sgu, v_w_spatial, v_b_spatial, v_w_branch_a, v_w_branch_b, v_w_out, v_g_post_mix, v_g_pre_ffn, v_w_ffn_in, v_w_ffn_down, v_g_post_ffn):
    big_names = ("w_in", "w_branch_a", "w_branch_b", "w_out", "w_ffn_in", "w_ffn_down")
    weights = dict(g_pre_mix=g_pre_mix, w_in=w_in, b_forget=b_forget, g_q=g_q, g_k=g_k, g_sgu=g_sgu,
                   b_sgu=b_sgu, w_spatial=w_spatial, b_spatial=b_spatial, w_branch_a=w_branch_a,
                   w_branch_b=w_branch_b, w_out=w_out, g_post_mix=g_post_mix, g_pre_ffn=g_pre_ffn,
                   w_ffn_in=w_ffn_in, w_ffn_down=w_ffn_down, g_post_ffn=g_post_ffn)
    mom1 = dict(g_pre_mix=m_g_pre_mix, w_in=m_w_in, b_forget=m_b_forget, g_q=m_g_q, g_k=m_g_k,
                g_sgu=m_g_sgu, b_sgu=m_b_sgu, w_spatial=m_w_spatial, b_spatial=m_b_spatial,
                w_branch_a=m_w_branch_a, w_branch_b=m_w_branch_b, w_out=m_w_out,
                g_post_mix=m_g_post_mix, g_pre_ffn=m_g_pre_ffn, w_ffn_in=m_w_ffn_in,
                w_ffn_down=m_w_ffn_down, g_post_ffn=m_g_post_ffn)
    mom2 = dict(g_pre_mix=v_g_pre_mix, w_in=v_w_in, b_forget=v_b_forget, g_q=v_g_q, g_k=v_g_k,
                g_sgu=v_g_sgu, b_sgu=v_b_sgu, w_spatial=v_w_spatial, b_spatial=v_b_spatial,
                w_branch_a=v_w_branch_a, w_branch_b=v_w_branch_b, w_out=v_w_out,
                g_post_mix=v_g_post_mix, g_pre_ffn=v_g_pre_ffn, w_ffn_in=v_w_ffn_in,
                w_ffn_down=v_w_ffn_down, g_post_ffn=v_g_post_ffn)
    names = list(weights)
    shapes = {k: weights[k].shape for k in names}

    s_len = x.shape[1]
    xs = x.reshape(s_len, D_MODEL)
    tgt = loss_target.reshape(s_len, D_MODEL)

    transposed = ("w_in", "w_ffn_in")

    def local_view(a, k):
        return jnp.transpose(a[0]) if k in transposed else a[0]

    shards = {k: local_view(weights[k], k).astype(BF16) for k in big_names}
    win_t = _gather_two_level(shards["w_in"], "gather_w_in").reshape(IN_COLS, D_MODEL)
    win_t, later = lax.optimization_barrier(
        (win_t, [shards[k] for k in big_names if k != "w_in"]))
    shards.update(zip([k for k in big_names if k != "w_in"], later))
    (gat_mix, gat_ffn), gat_token = _exchange_start(
        [[shards["w_branch_a"], shards["w_branch_b"], shards["w_out"]],
         [shards["w_ffn_in"], shards["w_ffn_down"]]], "gather_start", gather=True)
    f_off = 3 * FOX_W
    u_off = f_off + HEADS
    wcat = jnp.concatenate([
        win_t[:f_off], jnp.pad(win_t[f_off:u_off], ((0, 128 - HEADS), (0, 0))), win_t[u_off:]],
        axis=0)

    seg = np.arange(FOX_W) // HEAD_DIM
    bdiag = jnp.asarray(seg[:128, None] == seg[None, :128], BF16)
    tm = TOKEN_TILE
    lower = np.arange(tm)[None, :] <= np.arange(tm)[:, None]
    tril = jnp.asarray(lower, BF16)
    triu = jnp.asarray(lower.T, BF16)
    egrp = jnp.asarray(seg[:, None] == np.arange(128)[None, :], BF16)
    efold = jnp.asarray((np.arange(FOX_W) % HEAD_DIM)[:, None] == np.arange(128)[None, :], BF16)
    gq512 = jnp.tile(g_q.reshape(1, HEAD_DIM), (1, HEADS))
    gk512 = jnp.tile(g_k.reshape(1, HEAD_DIM), (1, HEADS))
    bfor = jnp.pad(b_forget.reshape(1, HEADS), ((0, 0), (0, 128 - HEADS)))
    pos = np.arange(WINDOW)
    wmask = (pos[None, :] // CHUNK) <= (pos[:, None] // CHUNK)
    wsm_f = jnp.where(jnp.asarray(wmask)[None], w_spatial[0], 0.0)
    wsm = wsm_f.astype(BF16)
    wsmt = jnp.transpose(wsm_f, (0, 2, 1)).astype(BF16)
    bsf = jnp.repeat(jnp.transpose(b_spatial[0]), HEAD_DIM, axis=1)
    wmask_f = jnp.asarray(wmask, F32)

    col = np.arange(SLAB_W)
    row128 = np.arange(128)

    def d_place(first):
        return jnp.asarray(np.stack([(col[None, :] // 128 == row128[:, None])
                                     & (col[None, :] % 128 == first + a) for a in range(3)]), BF16)

    pdq, pdk = d_place(HEAD_DIM), d_place(HEAD_DIM + 3)
    ones_q = jnp.asarray((col % 128 >= HEAD_DIM + 3) & (col % 128 < HEAD_DIM + 6), F32)[None]
    ones_k = jnp.asarray((col % 128 >= HEAD_DIM) & (col % 128 < HEAD_DIM + 3), F32)[None]
    ecol = jnp.asarray((col[:, None] // 128 == row128[None, :])
                       & (col[:, None] % 128 == HEAD_DIM + 3), BF16)

    (h, qa, ka, kat, vs, vt, qraw, kraw, flog, uvpre, gpre) = _proj_fwd(
        xs, g_pre_mix + gat_token[0:1, 0:1], wcat, bdiag, gq512, gk512, bfor, tril, pdq, pdk,
        ones_q, ones_k)
    attn, attn_t, lse = _attn_fwd(qa, ka, vt)
    (own_a, own_b, own_out), (zone_a, zone_b, zone_out) = _exchange_wait(
        gat_mix, attn, "gather_wait_mix", gather=True)
    wa = _blocks_to_cols(_own_block(zone_a, own_a))
    wb = _blocks_to_cols(_own_block(zone_b, own_b))
    wout = _own_block(zone_out, own_out).reshape(D_MODEL, D_MODEL)
    sgu_t, ya, yb, merged_t, om, x1 = _mix_fwd(attn, uvpre, gpre, xs, wa, wb, wout, wsm, bsf,
                                           g_sgu, b_sgu, g_post_mix)
    (own_ffn, own_down), (zone_ffn, zone_down) = _exchange_wait(
        gat_ffn, x1, "gather_wait_ffn", gather=True)
    wffn = _own_block(zone_ffn, own_ffn).reshape(2 * D_FF, D_MODEL)
    wdown = _own_block(zone_down, own_down).reshape(D_FF, D_MODEL)
    (dx1, h2, act_t, dff, dgu_t, loss_acc, dg_post_ffn, dg_pre_ffn) = _ffn_fwd_bwd(
        x1, tgt, wffn, wdown, g_pre_ffn, g_post_ffn)

    dw_down = _dw_matmul(act_t, dff, D_FF // 4, "dw_down")
    dw_ffn = _dw_matmul(dgu_t, h2, 2 * D_FF // N_DEV, "dw_ffn_in")
    x_pos, y_pos, c_pos = _mesh_pos()
    me = 4 * x_pos + 2 * y_pos + c_pos

    def own_of(parts):
        return [lax.dynamic_index_in_dim(p, me, 0, keepdims=False) for p in parts]

    parts_ffn = [dw_ffn.reshape(N_DEV, 2 * D_FF // N_DEV, D_MODEL),
                 dw_down.reshape(N_DEV, D_FF // N_DEV, D_MODEL)]
    mine_ffn = own_of(parts_ffn)
    (sct_ffn,), sct_ffn_token = _exchange_start([parts_ffn], "scatter_start_ffn", gather=False)

    (dom, dya, dyb, dgp, dot_, delta, duv, dws, dbs, dg_sgu, db_sgu, dg_post_mix) = _mix_bwd(
        dx1, om, ya, yb, gpre, uvpre, attn, wout, wa, wb, wsm, wsmt, bsf, g_sgu, b_sgu,
        g_post_mix + sct_ffn_token[0:1, 0:1], wmask_f, egrp)
    dw_out = _dw_matmul(merged_t, dom, 512, "dw_out")
    dw_a = _dw_matmul(attn_t, dya, 512, "dw_a")
    dw_b = _dw_matmul(sgu_t, dyb, 512, "dw_b")
    parts_mix = [_cols_to_blocks(dw_a, D_MODEL // N_DEV), _cols_to_blocks(dw_b, D_MODEL // N_DEV),
                 dw_out.reshape(N_DEV, D_MODEL // N_DEV, D_MODEL)]
    mine_mix = own_of(parts_mix)
    (sct_mix,), sct_mix_token = _exchange_start([parts_mix], "scatter_start_mix", gather=False)

    gk_all, dvt, gqt, col_sums = _attn_bwd(qa, ka, kat, vs, dot_, lse,
                                           delta + sct_mix_token[0, 0], ecol)
    dlogf = _rev_cumsum(col_sums, gqt, triu)
    dx, dproj_t, dgq, dgk, dbf, dg_pre_mix = _proj_bwd(
        gqt, gk_all, dvt, dlogf, flog, qraw, kraw, duv, dgp, xs, dx1, wcat, bdiag, gq512, gk512,
        g_pre_mix, efold)

    small_local = dict(
        g_pre_mix=dg_pre_mix, b_forget=dbf[:, :HEADS], g_q=dgq[0:1, :HEAD_DIM],
        g_k=dgk[0:1, :HEAD_DIM], g_sgu=dg_sgu, b_sgu=db_sgu, w_spatial=dws,
        b_spatial=jnp.transpose(dbs[:, :GROUPS]), g_post_mix=dg_post_mix, g_pre_ffn=dg_pre_ffn,
        g_post_ffn=dg_post_ffn)
    loss_row = jnp.pad(loss_acc[0:1, 0:1], ((0, 0), (0, 1023)))
    small_parts = _pack_small(small_local, loss_row).reshape(N_DEV, SMALL_ROWS, 1024)

    def with_own(zones, own_blocks):
        return [_own_block(z, b) for z, b in zip(zones, own_blocks)]

    mine_small = own_of([small_parts])
    (sct_small,), sct_small_token = _exchange_start([[small_parts]], "scatter_start_small",
                                                    gather=False)
    dw_cat = _dw_matmul(dproj_t, h, C_END // N_DEV, "dw_in", after=(sct_small_token,))
    dw_in = jnp.concatenate([dw_cat[:C_F + HEADS], dw_cat[C_UV:]], axis=0)
    (recv_small,) = with_own(
        _exchange_wait(sct_small, dw_in, "scatter_wait_small", gather=False)[1], mine_small)
    small_sum = _sum_parts(recv_small, "sum_small")
    (gat_small,), gat_small_token = _exchange_start([[small_sum]], "gather_start_small",
                                                    gather=True)
    pair_blocks, own_pair = _pair_sums(dw_in.reshape(N_DEV, BLK, D_MODEL), "pair_sums_in",
                                       gat_small_token)
    rs_in, rs_token = _chip_exchange_start(pair_blocks, "chip_exchange_start_in")

    recv_ffn, recv_down = with_own(
        _exchange_wait(sct_ffn, rs_token, "scatter_wait_ffn", gather=False)[1], mine_ffn)
    recv_a, recv_b, recv_out = with_own(
        _exchange_wait(sct_mix, recv_ffn, "scatter_wait_mix", gather=False)[1], mine_mix)
    received = [None, recv_a, recv_b, recv_out, recv_ffn, recv_down]

    grads, deltas, new_m, new_v = {}, {}, {}, {}
    row_tiles = {"w_in": None, "w_branch_a": 512, "w_branch_b": 512, "w_out": 128, "w_ffn_in": 176,
                 "w_ffn_down": 352}

    def update(k, parts):
        outs = _adamw(parts, local_view(weights[k], k), local_view(mom1[k], k),
                      local_view(mom2[k], k), row_tiles[k], "adamw_" + k,
                      col_tile=256 if k == "w_in" else None)
        if k in transposed:
            outs = [jnp.transpose(o) for o in outs]
        grads[k], deltas[k], new_m[k], new_v[k] = [o[None] for o in outs]
        return outs[0]

    last = None
    for idx, k in enumerate(big_names):
        if k != "w_in":
            last = update(k, received[idx])

    (own_small,), (zone_small,) = _exchange_wait(gat_small, last, "gather_wait_small", gather=True)
    small_all = _own_block(zone_small, own_small).reshape(1, N_DEV * SMALL_ROWS, 1024)
    sg, sd, sm, sv = _adamw(small_all, _pack_small(weights), _pack_small(mom1), _pack_small(mom2),
                            N_DEV * SMALL_ROWS, "adamw_small")
    for dst, packed in ((grads, sg), (deltas, sd), (new_m, sm), (new_v, sv)):
        dst.update(_unpack_small(packed, shapes))
    arrived = _chip_exchange_wait(rs_in, sg, "chip_exchange_wait_in")
    update("w_in", [own_pair[None], arrived])

    loss = small_all[0, sum(_small_rows(math.prod(shapes[k])) for k in SMALL_NAMES), 0]
    return (loss, dx.reshape(x.shape), *[grads[k] for k in names], *[deltas[k] for k in names],
            *[new_m[k] for k in names], *[new_v[k] for k in names])
```

```python
import functools
import math

import jax
import jax.numpy as jnp
import numpy as np
from jax import lax
from jax.experimental import pallas as pl
from jax.experimental.pallas import tpu as pltpu

F32 = jnp.float32
BF16 = jnp.bfloat16

D_MODEL = 1024
FOX_W = 512
HEADS = 8
HEAD_DIM = 64
SGU_W = 512
GROUPS = 8
WINDOW = 128
CHUNK = 64
D_FF = 2816
IN_COLS = 4616
EPS = 1e-6
N_DEV = 8
LOG2E = 1.4426950408889634
LN2 = 0.6931471805599453

C_Q, C_K, C_V, C_UV, C_G, C_F, C_END = 0, 512, 1024, 1536, 2560, 4608, 4736

ADAM_LR, ADAM_B1, ADAM_B2, ADAM_EPS, ADAM_WD, ADAM_STEP = 0.001, 0.9, 0.999, 1e-08, 0.01, 10

MIB = 1024 * 1024
TOKEN_TILE = 256
ATTN_TILE = 256
SLAB_W = HEADS * 128
QT_ROWS = 72

SMALL_ROWS = 18
BLK = IN_COLS // N_DEV
F_LO = 3 * FOX_W
F_DEV = F_LO // BLK
F_AT = F_LO - F_DEV * BLK
BLK_AT = [BLK * j - (HEADS if BLK * j > F_LO else 0) for j in range(N_DEV)]
FRAME_START = [a // 16 * 16 for a in BLK_AT]
FRAME = 608
FRAME_ROWS = FRAME + 16


def _params(vmem_mib, n_axes):
    return pltpu.CompilerParams(
        dimension_semantics=("arbitrary",) * n_axes, vmem_limit_bytes=vmem_mib * MIB)


def _const_spec(shape):
    nd = len(shape)
    return pl.BlockSpec(shape, lambda *_: (0,) * nd)


def _row_spec(tm, cols):
    return pl.BlockSpec((tm, cols), lambda i: (i, 0))


def _tile_spec(rows, tm):
    return pl.BlockSpec((1, rows, tm), lambda i: (i, 0, 0))


def _split3_dot(x, e):
    x1 = x.astype(BF16)
    r1 = x - x1.astype(F32)
    x2 = r1.astype(BF16)
    x3 = (r1 - x2.astype(F32)).astype(BF16)
    dot = functools.partial(jnp.dot, preferred_element_type=F32)
    return dot(x1, e) + dot(x2, e) + dot(x3, e)


def _tri_dot(tri, x):
    x1 = x.astype(BF16)
    r1 = x - x1.astype(F32)
    x2 = r1.astype(BF16)
    x3 = (r1 - x2.astype(F32)).astype(BF16)
    dot = functools.partial(jnp.dot, preferred_element_type=F32)
    return dot(tri, x1) + dot(tri, x2) + dot(tri, x3)


def _seg_mean(sq, bd_ref):
    hi = sq.astype(BF16)
    lo = (sq - hi.astype(F32)).astype(BF16)
    bd = bd_ref[...]
    dot = functools.partial(jnp.dot, preferred_element_type=F32)
    pairs = [dot(hi[:, p * 128:(p + 1) * 128], bd) + dot(lo[:, p * 128:(p + 1) * 128], bd)
             for p in range(HEADS // 2)]
    return jnp.concatenate(pairs, axis=1) * (1.0 / HEAD_DIM)


def _slabs_from_heads(t):
    lane = lax.broadcasted_iota(jnp.int32, (t.shape[0], 128), 1)
    low = lane < HEAD_DIM
    slabs = []
    for p in range(HEADS // 2):
        pair = t[:, p * 128:(p + 1) * 128]
        slabs.append(jnp.where(low, pair, 0.0))
        slabs.append(jnp.where(low, pltpu.roll(pair, HEAD_DIM, 1), 0.0))
    return jnp.concatenate(slabs, axis=1)


def _dot_nt(a, b):
    return lax.dot_general(a, b, (((1,), (1,)), ((), ())), preferred_element_type=F32)


def _dot_tn(a, b):
    return lax.dot_general(a, b, (((0,), (0,)), ((), ())), preferred_element_type=F32)


def _sigmoid(x):
    return 0.5 * jnp.tanh(0.5 * x) + 0.5


_GELU_C = math.sqrt(2.0 / math.pi)


def _gelu_and_grad(x):
    inner = _GELU_C * (x + 0.044715 * x * x * x)
    t = jnp.tanh(inner)
    y = 0.5 * x * (1.0 + t)
    dy = 0.5 * (1.0 + t) + 0.5 * x * (1.0 - t * t) * _GELU_C * (1.0 + 3.0 * 0.044715 * x * x)
    return y, dy


def _rms_bwd(xin, r, g, dy):
    dyg = dy * g
    return r * dyg - xin * (r * r * r) * jnp.mean(dyg * xin, axis=-1, keepdims=True)


def _mesh_pos():
    x, y, c = lax.axis_index("x"), lax.axis_index("y"), lax.axis_index("c")
    return x, y, c


def _peer(k):
    x, y, c = _mesh_pos()
    px = (1 - x) if (k >> 2) & 1 else x
    py = (1 - y) if (k >> 1) & 1 else y
    pc = (1 - c) if k & 1 else c
    return (px, py, pc), 4 * px + 2 * py + pc


def _gather_two_level(shard, name):
    def body(x_ref, out_ref, send_sems, recv_sems, local_sem):
        x, y, c = _mesh_pos()
        me, sibling = (x, y, c), (x, y, 1 - c)
        chips = [(1 - x, y), (x, 1 - y), (1 - x, 1 - y)]

        def slot(px, py, pc):
            return out_ref.at[4 * px + 2 * py + pc]

        def copy(k, block, to, src=None):
            return pltpu.make_async_remote_copy(
                src_ref=slot(*block) if src is None else src, dst_ref=slot(*block),
                send_sem=send_sems.at[k], recv_sem=recv_sems.at[k],
                device_id=to, device_id_type=pl.DeviceIdType.MESH)

        mine = pltpu.make_async_copy(x_ref, slot(*me), local_sem)
        mine.start()
        first = [copy(1 + j, me, (*chip, c), src=x_ref) for j, chip in enumerate(chips)]
        first.append(copy(0, me, sibling, src=x_ref))
        for cp in first:
            cp.start()
        passed = [copy(4 + j, (*chip, c), sibling) for j, chip in enumerate(chips)]
        for j, chip in enumerate(chips):
            copy(1 + j, (*chip, c), me).wait_recv()
            passed[j].start()
        copy(0, sibling, me).wait_recv()
        for j, chip in enumerate(chips):
            copy(4 + j, (*chip, 1 - c), me).wait_recv()
        for cp in first + passed:
            cp.wait_send()
        mine.wait()

    any_spec = pl.BlockSpec(memory_space=pl.ANY)
    return pl.pallas_call(
        body, name=name, out_shape=jax.ShapeDtypeStruct((N_DEV,) + shard.shape, shard.dtype),
        in_specs=[any_spec], out_specs=any_spec,
        scratch_shapes=[pltpu.SemaphoreType.DMA((7,)), pltpu.SemaphoreType.DMA((7,)),
                        pltpu.SemaphoreType.DMA],
    )(shard)


def _assemble_w_in(zone):
    def body(z_ref, o_ref):
        o_ref[...] = jnp.zeros_like(o_ref)
        for j in range(N_DEV):
            rows = slice(FRAME_START[j], FRAME_START[j] + FRAME)
            o_ref[rows, :] = (o_ref[rows, :].astype(F32) + z_ref[j, :FRAME, :].astype(F32)).astype(BF16)
        rows = slice(C_F, C_F + FRAME_ROWS - FRAME)
        o_ref[rows, :] = (o_ref[rows, :].astype(F32) + z_ref[F_DEV, FRAME:, :].astype(F32)).astype(BF16)

    return pl.pallas_call(
        body, name="assemble_w_in", grid=(1,),
        in_specs=[_const_spec(zone.shape)], out_specs=_const_spec((C_END, zone.shape[2])),
        out_shape=jax.ShapeDtypeStruct((C_END, zone.shape[2]), zone.dtype),
        compiler_params=_params(56, 1),
    )(zone)


def _chip_peer(k):
    x, y, c = _mesh_pos()
    px = (1 - x) if (k >> 1) & 1 else x
    py = (1 - y) if k & 1 else y
    return (px, py, c), 2 * px + py


def _frame_start(j):
    at = BLK * j - jnp.where(BLK * j > F_LO, HEADS, 0)
    return pl.multiple_of(at // 16 * 16, 16)


def _pair_sums(dw_cat, name, after):
    rows, cols = FRAME_ROWS, dw_cat.shape[1]
    n_chips = N_DEV // 2

    def pieces(p_ref, j):
        return (p_ref.at[pl.ds(_frame_start(j), FRAME)], p_ref.at[pl.ds(C_F, FRAME_ROWS - FRAME)])

    def body(p_ref, after_ref, send_ref, own_ref, mine_buf, sib_buf, send_sems, recv_sems,
             local_sems):
        x, y, c = _mesh_pos()
        sibling = (x, y, 1 - c)
        copies, local = [], []
        for q in range(n_chips):
            for part, (lo, hi) in enumerate(((0, FRAME), (FRAME, FRAME_ROWS))):
                cp = pltpu.make_async_remote_copy(
                    src_ref=pieces(p_ref, 2 * q + (1 - c))[part], dst_ref=sib_buf.at[q, lo:hi],
                    send_sem=send_sems.at[2 * q + part], recv_sem=recv_sems.at[2 * q + part],
                    device_id=sibling, device_id_type=pl.DeviceIdType.MESH)
                cp.start()
                copies.append(cp)
                lc = pltpu.make_async_copy(pieces(p_ref, 2 * q + c)[part], mine_buf.at[q, lo:hi],
                                           local_sems.at[2 * q + part])
                lc.start()
                local.append(lc)
        for lc in local:
            lc.wait()
        for cp in copies:
            cp.wait_recv()
        for k in range(1, n_chips):
            _, q = _chip_peer(k)
            send_ref[k - 1] = (mine_buf[q].astype(F32) + sib_buf[q].astype(F32)).astype(BF16)
        my_chip = 2 * x + y
        own_ref[...] = mine_buf[my_chip].astype(F32) + sib_buf[my_chip].astype(F32)
        for cp in copies:
            cp.wait_send()

    vmem = pl.BlockSpec(memory_space=pltpu.VMEM)
    return pl.pallas_call(
        body, name=name,
        out_shape=[jax.ShapeDtypeStruct((n_chips - 1, rows, cols), BF16),
                   jax.ShapeDtypeStruct((rows, cols), F32)],
        in_specs=[pl.BlockSpec(memory_space=pl.ANY)] * 2, out_specs=[vmem, vmem],
        scratch_shapes=[pltpu.VMEM((n_chips, rows, cols), BF16),
                        pltpu.VMEM((n_chips, rows, cols), BF16),
                        pltpu.SemaphoreType.DMA((2 * n_chips,)),
                        pltpu.SemaphoreType.DMA((2 * n_chips,)),
                        pltpu.SemaphoreType.DMA((2 * n_chips,))],
        compiler_params=pltpu.CompilerParams(vmem_limit_bytes=40 * MIB),
    )(dw_cat, after)


def _chip_copy(src_ref, land_ref, send_sem, recv_sem, k):
    peer, _ = _chip_peer(k)
    return pltpu.make_async_remote_copy(
        src_ref=src_ref.at[k - 1], dst_ref=land_ref.at[k - 1], send_sem=send_sem, recv_sem=recv_sem,
        device_id=peer, device_id_type=pl.DeviceIdType.MESH)


def _chip_exchange_start(blocks, name):
    hbm = pl.BlockSpec(memory_space=pltpu.HBM)
    sem = pl.BlockSpec(memory_space=pltpu.SEMAPHORE)
    n_peers = blocks.shape[0]

    def body(src_ref, zone_ref, send_sems, recv_sems, src_thru, zone_thru, token):
        for k in range(1, n_peers + 1):
            _chip_copy(src_ref, zone_ref, send_sems.at[k - 1], recv_sems.at[k - 1], k).start()
        token[...] = jnp.zeros_like(token)

    outs = pl.pallas_call(
        body, name=name, in_specs=[hbm, hbm],
        out_shape=[pltpu.SemaphoreType.DMA((n_peers,)), pltpu.SemaphoreType.DMA((n_peers,)),
                   pltpu.HBM(blocks.shape, blocks.dtype), pltpu.HBM(blocks.shape, blocks.dtype),
                   jax.ShapeDtypeStruct((8, 128), F32)],
        out_specs=[sem, sem, hbm, hbm, pl.BlockSpec(memory_space=pltpu.VMEM)],
        input_output_aliases={0: 2, 1: 3},
        compiler_params=pltpu.CompilerParams(
            has_side_effects=pltpu.SideEffectType.DATAFLOW_SIDE_EFFECTING),
    )(pltpu.with_memory_space_constraint(blocks, pltpu.HBM),
      pltpu.with_memory_space_constraint(lax.empty(blocks.shape, blocks.dtype), pltpu.HBM))
    return outs[:4], outs[4]


def _chip_exchange_wait(handle, after, name):
    send_sems, recv_sems, src, zone = handle
    hbm = pl.BlockSpec(memory_space=pltpu.HBM)
    sem = pl.BlockSpec(memory_space=pltpu.SEMAPHORE)

    def body(src_ref, zone_ref, ssem, rsem, after_ref, src_out, zone_out):
        for k in range(1, src.shape[0] + 1):
            cp = _chip_copy(src_ref, zone_ref, ssem.at[k - 1], rsem.at[k - 1], k)
            cp.wait_send()
            cp.wait_recv()

    outs = pl.pallas_call(
        body, name=name,
        in_specs=[hbm, hbm, sem, sem, pl.BlockSpec(memory_space=pl.ANY)],
        out_shape=[pltpu.HBM(src.shape, src.dtype), pltpu.HBM(zone.shape, zone.dtype)],
        out_specs=[hbm, hbm], input_output_aliases={0: 0, 1: 1},
        compiler_params=pltpu.CompilerParams(
            has_side_effects=pltpu.SideEffectType.DATAFLOW_SIDE_EFFECTING),
    )(src, zone, send_sems, recv_sems, after)
    return outs[1]


def _remote_copy(gather, src_ref, land_ref, send_sem, recv_sem, k, receive_side):
    x, y, c = _mesh_pos()
    me = 4 * x + 2 * y + c
    peer, pidx = _peer(k)
    return pltpu.make_async_remote_copy(
        src_ref=src_ref if gather else src_ref.at[pidx],
        dst_ref=land_ref.at[pidx if receive_side else me],
        send_sem=send_sem, recv_sem=recv_sem,
        device_id=peer, device_id_type=pl.DeviceIdType.MESH)


def _exchange_start(groups, name, gather):
    arrs = [a for g in groups for a in g]
    n, n_groups = len(arrs), len(groups)
    lands = [jax.ShapeDtypeStruct(((N_DEV,) + a.shape) if gather else a.shape, a.dtype)
             for a in arrs]

    def body(*refs):
        srcs, zones = refs[:n], refs[n:2 * n]
        sems = refs[2 * n:2 * n + 2 * n_groups]
        token = refs[-1]
        a = 0
        for gi, g in enumerate(groups):
            send_sems, recv_sems = sems[2 * gi], sems[2 * gi + 1]
            for k in range(1, N_DEV):
                for ai in range(len(g)):
                    slot = ai * (N_DEV - 1) + k - 1
                    _remote_copy(gather, srcs[a + ai], zones[a + ai], send_sems.at[slot],
                                 recv_sems.at[slot], k, False).start()
            a += len(g)
        token[...] = jnp.zeros_like(token)

    hbm = pl.BlockSpec(memory_space=pltpu.HBM)
    sem = pl.BlockSpec(memory_space=pltpu.SEMAPHORE)
    sem_shapes = []
    for g in groups:
        sem_shapes += [pltpu.SemaphoreType.DMA((len(g) * (N_DEV - 1),))] * 2
    outs = pl.pallas_call(
        body, name=name,
        in_specs=[hbm] * (2 * n),
        out_shape=sem_shapes + [pltpu.HBM(a.shape, a.dtype) for a in arrs]
        + [pltpu.HBM(z.shape, z.dtype) for z in lands] + [jax.ShapeDtypeStruct((8, 128), F32)],
        out_specs=[sem] * (2 * n_groups) + [hbm] * (2 * n)
        + [pl.BlockSpec(memory_space=pltpu.VMEM)],
        input_output_aliases={i: 2 * n_groups + i for i in range(2 * n)},
        compiler_params=pltpu.CompilerParams(
            has_side_effects=pltpu.SideEffectType.DATAFLOW_SIDE_EFFECTING),
    )(*[pltpu.with_memory_space_constraint(a, pltpu.HBM) for a in arrs],
      *[pltpu.with_memory_space_constraint(lax.empty(z.shape, z.dtype), pltpu.HBM) for z in lands])
    sems = outs[:2 * n_groups]
    thru = outs[2 * n_groups:2 * n_groups + n]
    zones = outs[2 * n_groups + n:2 * n_groups + 2 * n]
    handles, a = [], 0
    for gi, g in enumerate(groups):
        handles.append((sems[2 * gi], sems[2 * gi + 1], thru[a:a + len(g)], zones[a:a + len(g)]))
        a += len(g)
    return handles, outs[-1]


def _exchange_wait(handle, after, name, gather):
    send_sems, recv_sems, thru, zones = handle
    n = len(thru)

    def body(*refs):
        srcs, lands = refs[:n], refs[n:2 * n]
        ssem, rsem = refs[2 * n], refs[2 * n + 1]
        for k in range(1, N_DEV):
            for ai in range(n):
                slot = ai * (N_DEV - 1) + k - 1
                cp = _remote_copy(gather, srcs[ai], lands[ai], ssem.at[slot], rsem.at[slot], k, True)
                cp.wait_send()
                cp.wait_recv()

    hbm = pl.BlockSpec(memory_space=pltpu.HBM)
    sem = pl.BlockSpec(memory_space=pltpu.SEMAPHORE)
    outs = pl.pallas_call(
        body, name=name,
        in_specs=[hbm] * (2 * n) + [sem, sem, pl.BlockSpec(memory_space=pl.ANY)],
        out_shape=[pltpu.HBM(a.shape, a.dtype) for a in thru]
        + [pltpu.HBM(z.shape, z.dtype) for z in zones],
        out_specs=[hbm] * (2 * n),
        input_output_aliases={i: i for i in range(2 * n)},
        compiler_params=pltpu.CompilerParams(
            has_side_effects=pltpu.SideEffectType.DATAFLOW_SIDE_EFFECTING),
    )(*thru, *zones, send_sems, recv_sems, after)
    return outs[:n], outs[n:]


def _own_block(zone, block):
    x, y, c = _mesh_pos()
    me = 4 * x + 2 * y + c
    return lax.dynamic_update_slice_in_dim(zone, block[None], me, axis=0)


def _proj_fwd(x, g1, wcat, bdiag, gq, gk, bfor, tri, pdq, pdk, ones_q, ones_k):
    s_len = x.shape[0]
    tm = TOKEN_TILE
    nt = s_len // tm

    def body(x_ref, g1_ref, w_ref, bd_ref, gq_ref, gk_ref, bf_ref, tri_ref, pdq_ref,
             pdk_ref, oq_ref, ok_ref,
             h_ref, qa_ref, ka_ref, kat_ref, vs_ref, vt_ref, qr_ref, kr_ref, flog_ref, uv_ref,
             gp_ref, carry):
        @pl.when(pl.program_id(0) == 0)
        def _():
            carry[...] = jnp.zeros_like(carry)

        xf = x_ref[...]
        r = lax.rsqrt(jnp.mean(xf * xf, axis=-1, keepdims=True) + EPS)
        h = (xf * r * g1_ref[...]).astype(BF16)
        h_ref[...] = h
        dot = functools.partial(jnp.dot, preferred_element_type=F32)

        def proj(lo, hi):
            return _dot_nt(h, w_ref[lo:hi, :])

        flog = proj(C_F, C_END) + bf_ref[...]
        flog_ref[...] = flog
        lane = lax.broadcasted_iota(jnp.int32, flog.shape, 1)
        logf = jnp.minimum(flog, 0.0) - jnp.log(1.0 + jnp.exp(-jnp.abs(flog)))
        logf = jnp.where(lane < HEADS, logf, 0.0)
        dcum = _tri_dot(tri_ref[...], logf) + carry[...]
        carry[...] = dcum[tm - 1:tm, :]
        d2 = dcum * LOG2E
        d2a = d2.astype(BF16)
        rem = d2 - d2a.astype(F32)
        d2b = rem.astype(BF16)
        d2c = (rem - d2b.astype(F32)).astype(BF16)

        q = proj(C_Q, C_K)
        qr_ref[...] = q.astype(BF16)
        rq = lax.rsqrt(_seg_mean(q * q, bd_ref) + EPS)
        qn = q * rq * (gq_ref[...] * (HEAD_DIM ** -0.5 * LOG2E))
        qa = (_slabs_from_heads(qn) + dot(d2a, pdq_ref[0]) + dot(d2b, pdq_ref[1])
              + dot(d2c, pdq_ref[2]) + oq_ref[...])
        qa_ref[...] = qa.astype(BF16)

        k = proj(C_K, C_V)
        kr_ref[...] = k.astype(BF16)
        rk = lax.rsqrt(_seg_mean(k * k, bd_ref) + EPS)
        kn = k * rk * gk_ref[...]
        ka = (_slabs_from_heads(kn) - dot(d2a, pdk_ref[0]) - dot(d2b, pdk_ref[1])
              - dot(d2c, pdk_ref[2]) + ok_ref[...])
        ka_ref[...] = ka.astype(BF16)
        kat_ref[0] = ka.T.astype(BF16)

        v = proj(C_V, C_UV)
        vs_ref[...] = _slabs_from_heads(v).astype(BF16)
        vt_ref[0] = v.T.astype(BF16)
        uv_ref[...] = proj(C_UV, C_G).astype(BF16)
        gp_ref[...] = proj(C_G, C_F).astype(BF16)

    outs = [((s_len, D_MODEL), BF16, _row_spec(tm, D_MODEL)),
            ((s_len, SLAB_W), BF16, _row_spec(tm, SLAB_W)),
            ((s_len, SLAB_W), BF16, _row_spec(tm, SLAB_W)),
            ((nt, SLAB_W, tm), BF16, _tile_spec(SLAB_W, tm)),
            ((s_len, SLAB_W), BF16, _row_spec(tm, SLAB_W)),
            ((nt, FOX_W, tm), BF16, _tile_spec(FOX_W, tm)),
            ((s_len, FOX_W), BF16, _row_spec(tm, FOX_W)),
            ((s_len, FOX_W), BF16, _row_spec(tm, FOX_W)),
            ((s_len, 128), F32, _row_spec(tm, 128)),
            ((s_len, 2 * SGU_W), BF16, _row_spec(tm, 2 * SGU_W)),
            ((s_len, 2 * D_MODEL), BF16, _row_spec(tm, 2 * D_MODEL))]
    return pl.pallas_call(
        body, name="proj_fwd", grid=(nt,),
        in_specs=[_row_spec(tm, D_MODEL), _const_spec((1, D_MODEL)), _const_spec(wcat.shape),
                  _const_spec(bdiag.shape), _const_spec((1, FOX_W)), _const_spec((1, FOX_W)),
                  _const_spec((1, 128)), _const_spec((tm, tm)), _const_spec(pdq.shape), _const_spec(pdk.shape), _const_spec(ones_q.shape),
                  _const_spec(ones_k.shape)],
        out_specs=[o[2] for o in outs],
        out_shape=[jax.ShapeDtypeStruct(o[0], o[1]) for o in outs],
        scratch_shapes=[pltpu.VMEM((1, 128), F32)],
        compiler_params=_params(56, 1),
    )(x, g1, wcat, bdiag, gq, gk, bfor, tri, pdq, pdk, ones_q, ones_k)


def _attn_fwd(qa, ka, vt):
    s_len = qa.shape[0]
    t = ATTN_TILE
    nb = s_len // t

    def body(q_ref, k_ref, vt_ref, o_ref, ot_ref, lse_ref, m_sc, l_sc, acc_sc, s_sc, mcur_sc,
             alpha_sc):
        i = pl.program_id(0)
        m_sc[...] = jnp.full_like(m_sc, -jnp.inf)
        l_sc[...] = jnp.zeros_like(l_sc)
        acc_sc[...] = jnp.zeros_like(acc_sc)

        def logits(j, slot, masked):
            krows = pl.ds(pl.multiple_of(j * t, t), t)
            if masked:
                keep = (lax.broadcasted_iota(jnp.int32, (t, t), 0)
                        <= lax.broadcasted_iota(jnp.int32, (t, t), 1))
            for hd in range(HEADS):
                sl = slice(hd * 128, (hd + 1) * 128)
                st = _dot_nt(k_ref[krows, sl], q_ref[:, sl])
                if masked:
                    st = jnp.where(keep, st, -jnp.inf)
                s_sc[slot, hd] = st
                m_prev = m_sc[hd:hd + 1, :]
                m_new = jnp.maximum(m_prev, jnp.max(st, axis=0, keepdims=True))
                alpha_sc[slot, hd:hd + 1, :] = jnp.exp2(m_prev - m_new)
                mcur_sc[slot, hd:hd + 1, :] = m_new
                m_sc[hd:hd + 1, :] = m_new

        def accumulate(j, slot):
            for hd in range(HEADS):
                hr = slice(hd * HEAD_DIM, (hd + 1) * HEAD_DIM)
                alpha = alpha_sc[slot, hd:hd + 1, :]
                pt = jnp.exp2(s_sc[slot, hd] - mcur_sc[slot, hd:hd + 1, :])
                l_sc[hd:hd + 1, :] = alpha * l_sc[hd:hd + 1, :] + jnp.sum(pt, axis=0, keepdims=True)
                acc_sc[hr, :] = alpha * acc_sc[hr, :] + jnp.dot(
                    vt_ref[j, hr, :], pt.astype(BF16), preferred_element_type=F32)

        @pl.when(i == 0)
        def _():
            logits(0, 0, True)
            accumulate(0, 0)

        pairs = (i - 1) // 2

        @pl.when(i > 0)
        def _():
            logits(0, 0, False)

            def two_blocks(p, carry):
                logits(2 * p + 1, 1, False)
                accumulate(2 * p, 0)
                logits(2 * p + 2, 0, False)
                accumulate(2 * p + 1, 1)
                return carry

            lax.fori_loop(0, pairs, two_blocks, 0)

        @pl.when((i > 0) & (i - 2 * pairs == 1))
        def _():
            logits(i, 1, True)
            accumulate(i - 1, 0)
            accumulate(i, 1)

        @pl.when((i > 0) & (i - 2 * pairs == 2))
        def _():
            logits(i - 1, 1, False)
            accumulate(i - 2, 0)
            logits(i, 0, True)
            accumulate(i - 1, 1)
            accumulate(i, 0)

        for hd in range(HEADS):
            hr = slice(hd * HEAD_DIM, (hd + 1) * HEAD_DIM)
            l = l_sc[hd:hd + 1, :]
            acc_sc[hr, :] = acc_sc[hr, :] / l
            lse_ref[0, hd:hd + 1, :] = m_sc[hd:hd + 1, :] + jnp.log2(l)
        o_ref[...] = acc_sc[...].T.astype(BF16)
        ot_ref[...] = acc_sc[...].astype(BF16)

    return pl.pallas_call(
        body, name="attn_fwd", grid=(nb,),
        in_specs=[_row_spec(t, SLAB_W), _const_spec(ka.shape), _const_spec(vt.shape)],
        out_specs=[_row_spec(t, FOX_W), pl.BlockSpec((FOX_W, t), lambda i: (0, i)),
                   _tile_spec(HEADS, t)],
        out_shape=[jax.ShapeDtypeStruct((s_len, FOX_W), BF16),
                   jax.ShapeDtypeStruct((FOX_W, s_len), BF16),
                   jax.ShapeDtypeStruct((nb, HEADS, t), F32)],
        scratch_shapes=[pltpu.VMEM((HEADS, t), F32), pltpu.VMEM((HEADS, t), F32),
                        pltpu.VMEM((FOX_W, t), F32), pltpu.VMEM((2, HEADS, t, t), F32),
                        pltpu.VMEM((2, HEADS, t), F32), pltpu.VMEM((2, HEADS, t), F32)],
        compiler_params=_params(48, 1),
    )(qa, ka, vt)


def _sgu_mix(vn, ws_ref):
    tm = vn.shape[0]
    lane = lax.broadcasted_iota(jnp.int32, (WINDOW, 128), 1)
    low = lane < HEAD_DIM
    wins = []
    for w in range(tm // WINDOW):
        slabs = []
        for p in range(GROUPS // 2):
            v2 = vn[w * WINDOW:(w + 1) * WINDOW, p * 128:(p + 1) * 128]
            lo = jnp.where(low, v2, 0.0).astype(BF16)
            hi = jnp.where(low, 0.0, v2).astype(BF16)
            slabs.append(jnp.dot(ws_ref[2 * p], lo, preferred_element_type=F32)
                         + jnp.dot(ws_ref[2 * p + 1], hi, preferred_element_type=F32))
        wins.append(jnp.concatenate(slabs, axis=1))
    return jnp.concatenate(wins, axis=0) if len(wins) > 1 else wins[0]


def _layernorm_fwd(vv, g, b):
    mu = jnp.mean(vv, axis=-1, keepdims=True)
    xc = vv - mu
    r = lax.rsqrt(jnp.mean(xc * xc, axis=-1, keepdims=True) + EPS)
    xh = xc * r
    return xh * g + b, xh, r


def _mix_fwd(attn, uvpre, gpre, x, wa, wb, wout, wsm, bsf, gsgu, bsgu, gpost):
    s_len = x.shape[0]
    tm = TOKEN_TILE

    def body(o_ref, uv_ref, gp_ref, x_ref, wa_ref, wb_ref, wo_ref, ws_ref, bs_ref, gs_ref, bsg_ref,
             gpost_ref, sgut_ref, ya_ref, yb_ref, mgt_ref, om_ref, x1_ref):
        uvp = uv_ref[...].astype(F32)
        uv, _ = _gelu_and_grad(uvp)
        u, vv = uv[:, :SGU_W], uv[:, SGU_W:]
        vn, _, _ = _layernorm_fwd(vv, gs_ref[...], bsg_ref[...])
        bias = bs_ref[...]
        if tm > WINDOW:
            bias = jnp.concatenate([bias] * (tm // WINDOW), axis=0)
        mixed = _sgu_mix(vn, ws_ref) + bias
        sgu_f = u * mixed
        sgu = sgu_f.astype(BF16)
        sgut_ref[...] = sgu_f.T.astype(BF16)
        ya = jnp.dot(o_ref[...], wa_ref[...], preferred_element_type=F32)
        yb = jnp.dot(sgu, wb_ref[...], preferred_element_type=F32)
        ya_ref[...] = ya.astype(BF16)
        yb_ref[...] = yb.astype(BF16)
        gates = _sigmoid(gp_ref[...].astype(F32))
        merged_f = gates[:, :D_MODEL] * ya + gates[:, D_MODEL:] * yb
        merged = merged_f.astype(BF16)
        mgt_ref[...] = merged_f.T.astype(BF16)
        om = jnp.dot(merged, wo_ref[...], preferred_element_type=F32)
        om_ref[...] = om
        r = lax.rsqrt(jnp.mean(om * om, axis=-1, keepdims=True) + EPS)
        x1_ref[...] = x_ref[...] + om * r * gpost_ref[...]

    def t_out(rows):
        return ((rows, s_len), BF16, pl.BlockSpec((rows, tm), lambda i: (0, i)))

    def r_out(cols, dt):
        return ((s_len, cols), dt, _row_spec(tm, cols))

    outs = [t_out(SGU_W), r_out(D_MODEL, BF16), r_out(D_MODEL, BF16), t_out(D_MODEL),
            r_out(D_MODEL, F32), r_out(D_MODEL, F32)]
    return pl.pallas_call(
        body, name="mix_fwd", grid=(s_len // tm,),
        in_specs=[_row_spec(tm, FOX_W), _row_spec(tm, 2 * SGU_W), _row_spec(tm, 2 * D_MODEL),
                  _row_spec(tm, D_MODEL), _const_spec(wa.shape), _const_spec(wb.shape),
                  _const_spec(wout.shape), _const_spec(wsm.shape), _const_spec(bsf.shape),
                  _const_spec((1, SGU_W)), _const_spec((1, SGU_W)), _const_spec((1, D_MODEL))],
        out_specs=[o[2] for o in outs],
        out_shape=[jax.ShapeDtypeStruct(o[0], o[1]) for o in outs],
        compiler_params=_params(48, 1),
    )(attn, uvpre, gpre, x, wa, wb, wout, wsm, bsf, gsgu, bsgu, gpost)


def _ffn_fwd_bwd(x1, tgt, wffn, wdown, gpre, gpost):
    s_len = x1.shape[0]
    tm = TOKEN_TILE

    def body(x1_ref, t_ref, wi_ref, wd_ref, gpre_ref, gpost_ref,
             dx1_ref, h2_ref, actt_ref, dff_ref, dgut_ref, loss_ref, dgpost_ref, dgpre_ref):
        @pl.when(pl.program_id(0) == 0)
        def _():
            loss_ref[...] = jnp.zeros_like(loss_ref)
            dgpost_ref[...] = jnp.zeros_like(dgpost_ref)
            dgpre_ref[...] = jnp.zeros_like(dgpre_ref)

        x1v = x1_ref[...]
        r2 = lax.rsqrt(jnp.mean(x1v * x1v, axis=-1, keepdims=True) + EPS)
        gpre_v = gpre_ref[...]
        h2 = (x1v * r2 * gpre_v).astype(BF16)
        h2_ref[...] = h2
        gg = _dot_nt(h2, wi_ref[:D_FF, :])
        uu = _dot_nt(h2, wi_ref[D_FF:, :])
        sg = _sigmoid(gg)
        silu = gg * sg
        act_f = silu * uu
        act = act_f.astype(BF16)
        actt_ref[...] = act_f.T.astype(BF16)
        ff = jnp.dot(act, wd_ref[...], preferred_element_type=F32)
        r3 = lax.rsqrt(jnp.mean(ff * ff, axis=-1, keepdims=True) + EPS)
        gpost_v = gpost_ref[...]
        y = x1v + ff * r3 * gpost_v
        err = y - t_ref[...]
        loss_ref[...] += jnp.sum(err * err) * (0.5 / D_MODEL)
        dy = err * (1.0 / D_MODEL)
        dgpost_ref[...] += jnp.sum(dy * ff * r3, axis=0, keepdims=True)
        dff = _rms_bwd(ff, r3, gpost_v, dy).astype(BF16)
        dff_ref[...] = dff
        dact = _dot_nt(dff, wd_ref[...])
        dgg_f = dact * uu * (sg * (1.0 + gg * (1.0 - sg)))
        duu_f = dact * silu
        dgg = dgg_f.astype(BF16)
        duu = duu_f.astype(BF16)
        dgut_ref[:D_FF, :] = dgg_f.T.astype(BF16)
        dgut_ref[D_FF:, :] = duu_f.T.astype(BF16)
        dh2 = (jnp.dot(dgg, wi_ref[:D_FF, :], preferred_element_type=F32)
               + jnp.dot(duu, wi_ref[D_FF:, :], preferred_element_type=F32))
        dgpre_ref[...] += jnp.sum(dh2 * x1v * r2, axis=0, keepdims=True)
        dx1_ref[...] = dy + _rms_bwd(x1v, r2, gpre_v, dh2)

    outs = [((s_len, D_MODEL), F32, _row_spec(tm, D_MODEL)),
            ((s_len, D_MODEL), BF16, _row_spec(tm, D_MODEL)),
            ((D_FF, s_len), BF16, pl.BlockSpec((D_FF, tm), lambda i: (0, i))),
            ((s_len, D_MODEL), BF16, _row_spec(tm, D_MODEL)),
            ((2 * D_FF, s_len), BF16, pl.BlockSpec((2 * D_FF, tm), lambda i: (0, i))),
            ((1, 128), F32, _const_spec((1, 128))),
            ((1, D_MODEL), F32, _const_spec((1, D_MODEL))),
            ((1, D_MODEL), F32, _const_spec((1, D_MODEL)))]
    return pl.pallas_call(
        body, name="ffn_fwd_bwd", grid=(s_len // tm,),
        in_specs=[_row_spec(tm, D_MODEL), _row_spec(tm, D_MODEL), _const_spec(wffn.shape),
                  _const_spec(wdown.shape), _const_spec((1, D_MODEL)), _const_spec((1, D_MODEL))],
        out_specs=[o[2] for o in outs],
        out_shape=[jax.ShapeDtypeStruct(o[0], o[1]) for o in outs],
        compiler_params=_params(60, 1),
    )(x1, tgt, wffn, wdown, gpre, gpost)


def _mix_bwd(dx1, om, ya, yb, gpre, uvpre, attn, wout, wa, wb, wsm, wsmt, bsf, gsgu, bsgu, gpost,
             wmask, egrp):
    s_len = dx1.shape[0]
    tm = TOKEN_TILE
    nw = tm // WINDOW
    nt = s_len // tm

    def body(dx1_ref, om_ref, ya_ref, yb_ref, gp_ref, uv_ref, o_ref, wo_ref, wa_ref, wb_ref, ws_ref,
             wst_ref, bs_ref, gs_ref, bsg_ref, gpost_ref, mask_ref, eg_ref,
             dom_ref, dya_ref, dyb_ref, dgp_ref, dot_ref, delta_ref, duv_ref,
             dws_ref, dbs_ref, dgs_ref, dbsg_ref, dgpost_ref, dbs_acc):
        step = pl.program_id(0)

        @pl.when(step == 0)
        def _():
            dws_ref[...] = jnp.zeros_like(dws_ref)
            dbs_acc[...] = jnp.zeros_like(dbs_acc)
            dgs_ref[...] = jnp.zeros_like(dgs_ref)
            dbsg_ref[...] = jnp.zeros_like(dbsg_ref)
            dgpost_ref[...] = jnp.zeros_like(dgpost_ref)

        om = om_ref[...]
        dx1v = dx1_ref[...]
        r = lax.rsqrt(jnp.mean(om * om, axis=-1, keepdims=True) + EPS)
        gpost_v = gpost_ref[...]
        dgpost_ref[...] += jnp.sum(dx1v * om * r, axis=0, keepdims=True)
        dom = _rms_bwd(om, r, gpost_v, dx1v).astype(BF16)
        dom_ref[...] = dom
        dmg = _dot_nt(dom, wo_ref[...])

        gates = _sigmoid(gp_ref[...].astype(F32))
        ga, gb = gates[:, :D_MODEL], gates[:, D_MODEL:]
        yav, ybv = ya_ref[...].astype(F32), yb_ref[...].astype(F32)
        dya = (dmg * ga).astype(BF16)
        dyb = (dmg * gb).astype(BF16)
        dya_ref[...] = dya
        dyb_ref[...] = dyb
        dgp_ref[:, :D_MODEL] = (dmg * yav * ga * (1.0 - ga)).astype(BF16)
        dgp_ref[:, D_MODEL:] = (dmg * ybv * gb * (1.0 - gb)).astype(BF16)

        dat_t = _dot_nt(dya, wa_ref[...]).T.astype(BF16)
        dot_ref[0] = dat_t
        o_t = o_ref[...].astype(F32).T
        delta_ref[0] = jnp.sum((dat_t.astype(F32) * o_t).reshape(HEADS, HEAD_DIM, tm), axis=1)
        dsgu = _dot_nt(dyb, wb_ref[...])

        uvp = uv_ref[...].astype(F32)
        uv, guv = _gelu_and_grad(uvp)
        u, vv = uv[:, :SGU_W], uv[:, SGU_W:]
        gs_v = gs_ref[...]
        vn, xh, rln = _layernorm_fwd(vv, gs_v, bsg_ref[...])
        bias = bs_ref[...]
        if nw > 1:
            bias = jnp.concatenate([bias] * nw, axis=0)
        mixed = _sgu_mix(vn, ws_ref) + bias
        du = dsgu * mixed
        dmixed = dsgu * u

        lane = lax.broadcasted_iota(jnp.int32, (WINDOW, 128), 1)
        low = lane < HEAD_DIM
        dvn_wins = []
        for w in range(nw):
            rows = slice(w * WINDOW, (w + 1) * WINDOW)
            dbs_acc[...] += dmixed[rows, :]
            slabs = []
            for p in range(GROUPS // 2):
                cols = slice(p * 128, (p + 1) * 128)
                dm2 = dmixed[rows, cols]
                dlo = jnp.where(low, dm2, 0.0).astype(BF16)
                dhi = jnp.where(low, 0.0, dm2).astype(BF16)
                vn2 = vn[rows, cols].astype(BF16)
                dws_ref[2 * p] += _dot_nt(dlo, vn2)
                dws_ref[2 * p + 1] += _dot_nt(dhi, vn2)
                slabs.append(jnp.dot(wst_ref[2 * p], dlo, preferred_element_type=F32)
                             + jnp.dot(wst_ref[2 * p + 1], dhi, preferred_element_type=F32))
            dvn_wins.append(jnp.concatenate(slabs, axis=1))
        dvn = jnp.concatenate(dvn_wins, axis=0) if nw > 1 else dvn_wins[0]

        dgs_ref[...] += jnp.sum(dvn * xh, axis=0, keepdims=True)
        dbsg_ref[...] += jnp.sum(dvn, axis=0, keepdims=True)
        dxh = dvn * gs_v
        dvv = rln * (dxh - jnp.mean(dxh, axis=-1, keepdims=True)
                     - xh * jnp.mean(dxh * xh, axis=-1, keepdims=True))
        duv_ref[:, :SGU_W] = (du * guv[:, :SGU_W]).astype(BF16)
        duv_ref[:, SGU_W:] = (dvv * guv[:, SGU_W:]).astype(BF16)

        @pl.when(step == pl.num_programs(0) - 1)
        def _():
            for g in range(GROUPS):
                dws_ref[g] = dws_ref[g] * mask_ref[...]
            dbs_ref[...] = _split3_dot(dbs_acc[...], eg_ref[...])

    rows_out = [((s_len, D_MODEL), BF16, _row_spec(tm, D_MODEL)),
                ((s_len, D_MODEL), BF16, _row_spec(tm, D_MODEL)),
                ((s_len, D_MODEL), BF16, _row_spec(tm, D_MODEL)),
                ((s_len, 2 * D_MODEL), BF16, _row_spec(tm, 2 * D_MODEL)),
                ((nt, FOX_W, tm), BF16, _tile_spec(FOX_W, tm)),
                ((nt, HEADS, tm), F32, _tile_spec(HEADS, tm)),
                ((s_len, 2 * SGU_W), BF16, _row_spec(tm, 2 * SGU_W))]
    acc_out = [((GROUPS, WINDOW, WINDOW), F32), ((WINDOW, 128), F32), ((1, SGU_W), F32),
               ((1, SGU_W), F32), ((1, D_MODEL), F32)]
    return pl.pallas_call(
        body, name="mix_bwd", grid=(nt,),
        in_specs=[_row_spec(tm, D_MODEL), _row_spec(tm, D_MODEL), _row_spec(tm, D_MODEL),
                  _row_spec(tm, D_MODEL), _row_spec(tm, 2 * D_MODEL), _row_spec(tm, 2 * SGU_W),
                  _row_spec(tm, FOX_W), _const_spec(wout.shape), _const_spec(wa.shape),
                  _const_spec(wb.shape), _const_spec(wsm.shape), _const_spec(wsmt.shape),
                  _const_spec(bsf.shape), _const_spec((1, SGU_W)), _const_spec((1, SGU_W)),
                  _const_spec((1, D_MODEL)), _const_spec(wmask.shape), _const_spec(egrp.shape)],
        out_specs=[o[2] for o in rows_out] + [_const_spec(s) for s, _ in acc_out],
        out_shape=[jax.ShapeDtypeStruct(o[0], o[1]) for o in rows_out]
        + [jax.ShapeDtypeStruct(s, dt) for s, dt in acc_out],
        scratch_shapes=[pltpu.VMEM((WINDOW, SGU_W), F32)],
        compiler_params=_params(48, 1),
    )(dx1, om, ya, yb, gpre, uvpre, attn, wout, wa, wb, wsm, wsmt, bsf, gsgu, bsgu, gpost, wmask,
      egrp)


def _attn_bwd(qa, ka, kat, vs, dot_, lse, delta, ecol):
    s_len = qa.shape[0]
    t = ATTN_TILE
    nb = s_len // t

    def body(k_ref, kt_ref, vs_ref, q_ref, do_ref, lse_ref, dl_ref, ec_ref, gk_ref, dvt_ref,
             gqt_ref, csum_ref, p_sc, ds_sc):
        j = pl.program_id(0)

        @pl.when(j == 0)
        def _():
            gqt_ref[...] = jnp.zeros_like(gqt_ref)

        gk_ref[...] = jnp.zeros_like(gk_ref)
        dvt_ref[...] = jnp.zeros_like(dvt_ref)

        def probs(i, slot, masked):
            qrows = pl.ds(pl.multiple_of(i * t, t), t)
            if masked:
                keep = (lax.broadcasted_iota(jnp.int32, (t, t), 0)
                        <= lax.broadcasted_iota(jnp.int32, (t, t), 1))
            for hd in range(HEADS):
                sl = slice(hd * 128, (hd + 1) * 128)
                hr = slice(hd * HEAD_DIM, (hd + 1) * HEAD_DIM)
                st = _dot_nt(k_ref[:, sl], q_ref[qrows, sl])
                if masked:
                    st = jnp.where(keep, st, -jnp.inf)
                pt = jnp.exp2(st - lse_ref[i, hd:hd + 1, :])
                dpt = jnp.dot(vs_ref[:, hd * 128:hd * 128 + HEAD_DIM], do_ref[i, hr, :],
                              preferred_element_type=F32)
                p_sc[slot, hd] = pt.astype(BF16)
                ds_sc[slot, hd] = (pt * (dpt - dl_ref[i, hd:hd + 1, :])).astype(BF16)

        def grads(i, slot):
            qrows = pl.ds(pl.multiple_of(i * t, t), t)
            for hd in range(HEADS):
                sl = slice(hd * 128, (hd + 1) * 128)
                hr = slice(hd * HEAD_DIM, (hd + 1) * HEAD_DIM)
                dst = ds_sc[slot, hd]
                dvt_ref[0, hr, :] += _dot_nt(do_ref[i, hr, :], p_sc[slot, hd])
                gk_ref[:, sl] += jnp.dot(dst, q_ref[qrows, sl], preferred_element_type=F32)
                gqt_ref[i, hd * QT_ROWS:(hd + 1) * QT_ROWS, :] += jnp.dot(
                    kt_ref[0, hd * 128:hd * 128 + QT_ROWS, :], dst, preferred_element_type=F32)

        probs(j, 0, True)
        pairs = (nb - 1 - j) // 2

        def two_blocks(p, carry):
            i1 = j + 1 + 2 * p
            probs(i1, 1, False)
            grads(i1 - 1, 0)
            probs(i1 + 1, 0, False)
            grads(i1, 1)
            return carry

        lax.fori_loop(0, pairs, two_blocks, 0)

        @pl.when(nb - 1 - j - 2 * pairs == 0)
        def _():
            grads(nb - 1, 0)

        @pl.when(nb - 1 - j - 2 * pairs == 1)
        def _():
            probs(nb - 1, 1, False)
            grads(nb - 2, 0)
            grads(nb - 1, 1)

        csum_ref[...] = _split3_dot(gk_ref[...], ec_ref[...])

    return pl.pallas_call(
        body, name="attn_bwd", grid=(nb,),
        in_specs=[_row_spec(t, SLAB_W), _tile_spec(SLAB_W, t), _row_spec(t, SLAB_W),
                  _const_spec(qa.shape), _const_spec(dot_.shape), _const_spec(lse.shape),
                  _const_spec(delta.shape), _const_spec(ecol.shape)],
        out_specs=[_row_spec(t, SLAB_W), _tile_spec(FOX_W, t),
                   _const_spec((nb, HEADS * QT_ROWS, t)), _row_spec(t, 128)],
        out_shape=[jax.ShapeDtypeStruct((s_len, SLAB_W), F32),
                   jax.ShapeDtypeStruct((nb, FOX_W, t), F32),
                   jax.ShapeDtypeStruct((nb, HEADS * QT_ROWS, t), F32),
                   jax.ShapeDtypeStruct((s_len, 128), F32)],
        scratch_shapes=[pltpu.VMEM((2, HEADS, t, t), BF16), pltpu.VMEM((2, HEADS, t, t), BF16)],
        compiler_params=_params(60, 1),
    )(ka, kat, vs, qa, dot_, lse, delta, ecol)


def _rev_cumsum(col_sums, gqt, triu):
    s_len = col_sums.shape[0]
    tm = TOKEN_TILE
    n = s_len // tm

    def body(cs_ref, gqt_ref, tri_ref, o_ref, carry):
        @pl.when(pl.program_id(0) == 0)
        def _():
            carry[...] = jnp.zeros_like(carry)
        rows = [gqt_ref[0, hd * QT_ROWS + HEAD_DIM:hd * QT_ROWS + HEAD_DIM + 1, :]
                for hd in range(HEADS)]
        row_sums = jnp.concatenate(rows + [jnp.zeros((128 - HEADS, tm), F32)], axis=0).T
        out = _tri_dot(tri_ref[...], row_sums - cs_ref[...]) + carry[...]
        o_ref[...] = out
        carry[...] = out[0:1, :]

    return pl.pallas_call(
        body, name="rev_cumsum", grid=(n,),
        in_specs=[pl.BlockSpec((tm, 128), lambda i: (n - 1 - i, 0)),
                  pl.BlockSpec((1, HEADS * QT_ROWS, tm), lambda i: (n - 1 - i, 0, 0)),
                  _const_spec((tm, tm))],
        out_specs=pl.BlockSpec((tm, 128), lambda i: (n - 1 - i, 0)),
        out_shape=jax.ShapeDtypeStruct((s_len, 128), F32),
        scratch_shapes=[pltpu.VMEM((1, 128), F32)],
        compiler_params=_params(32, 1),
    )(col_sums, gqt, triu)


def _heads_from_slabs(slabs):
    lane = lax.broadcasted_iota(jnp.int32, slabs[0].shape, 1)
    low = lane < HEAD_DIM
    pairs = [jnp.where(low, slabs[2 * p], pltpu.roll(slabs[2 * p + 1], HEAD_DIM, 1))
             for p in range(HEADS // 2)]
    return jnp.concatenate(pairs, axis=1)


def _proj_bwd(gqt, gk, dvt, dlogf, flog, qraw, kraw, duv, dgp, x, dx1, wcat, bdiag, gq, gk_gain, g1,
              efold):
    s_len = x.shape[0]
    tm = TOKEN_TILE

    def body(gqt_ref, gkk_ref, dvt_ref, dlf_ref, flog_ref, qr_ref, kr_ref, duv_ref, dgp_ref, x_ref,
             dx1_ref, w_ref, bd_ref, gq_ref, gk_ref, g1_ref, ef_ref,
             dx_ref, dprojt_ref, dgq_ref, dgk_ref, dbf_ref, dg1_ref, gq_acc, gk_acc, dproj_ref):
        step = pl.program_id(0)

        @pl.when(step == 0)
        def _():
            gq_acc[...] = jnp.zeros_like(gq_acc)
            gk_acc[...] = jnp.zeros_like(gk_acc)
            dbf_ref[...] = jnp.zeros_like(dbf_ref)
            dg1_ref[...] = jnp.zeros_like(dg1_ref)

        pad = jnp.zeros((128 - QT_ROWS, tm), F32)
        q_slabs = [jnp.concatenate([gqt_ref[0, hd * QT_ROWS:(hd + 1) * QT_ROWS, :], pad], axis=0).T
                   for hd in range(HEADS)]
        dqn = _heads_from_slabs(q_slabs)
        dkn = _heads_from_slabs([gkk_ref[:, hd * 128:(hd + 1) * 128] for hd in range(HEADS)])

        def head_bwd(raw_ref, dn, g_ref, acc):
            raw = raw_ref[...].astype(F32)
            r = lax.rsqrt(_seg_mean(raw * raw, bd_ref) + EPS)
            xhat = raw * r
            acc[0:1, :] += jnp.sum(dn * xhat, axis=0, keepdims=True)
            dyg = dn * g_ref[...]
            return r * (dyg - xhat * _seg_mean(dyg * xhat, bd_ref))

        dproj_ref[:, C_Q:C_K] = head_bwd(qr_ref, dqn * HEAD_DIM ** -0.5, gq_ref, gq_acc).astype(BF16)
        dproj_ref[:, C_K:C_V] = head_bwd(kr_ref, dkn * LN2, gk_ref, gk_acc).astype(BF16)
        dproj_ref[:, C_V:C_UV] = dvt_ref[0].T.astype(BF16)
        dfl = dlf_ref[...] * _sigmoid(-flog_ref[...])
        dbf_ref[...] += jnp.sum(dfl, axis=0, keepdims=True)
        dproj_ref[:, C_F:C_END] = dfl.astype(BF16)
        dproj_ref[:, C_UV:C_G] = duv_ref[...]
        dproj_ref[:, C_G:C_F] = dgp_ref[...]

        dproj = dproj_ref[...]
        dprojt_ref[...] = dproj.astype(F32).T.astype(BF16)
        dh = jnp.dot(dproj, w_ref[...], preferred_element_type=F32)
        xf = x_ref[...]
        r = lax.rsqrt(jnp.mean(xf * xf, axis=-1, keepdims=True) + EPS)
        dg1_ref[...] += jnp.sum(dh * xf * r, axis=0, keepdims=True)
        dx_ref[...] = dx1_ref[...] + _rms_bwd(xf, r, g1_ref[...], dh)

        @pl.when(step == pl.num_programs(0) - 1)
        def _():
            dgq_ref[...] = _split3_dot(gq_acc[...], ef_ref[...])
            dgk_ref[...] = _split3_dot(gk_acc[...], ef_ref[...])

    outs = [((s_len, D_MODEL), F32, _row_spec(tm, D_MODEL)),
            ((C_END, s_len), BF16, pl.BlockSpec((C_END, tm), lambda i: (0, i))),
            ((8, 128), F32, _const_spec((8, 128))),
            ((8, 128), F32, _const_spec((8, 128))),
            ((1, 128), F32, _const_spec((1, 128))),
            ((1, D_MODEL), F32, _const_spec((1, D_MODEL)))]
    return pl.pallas_call(
        body, name="proj_bwd", grid=(s_len // tm,),
        in_specs=[_tile_spec(HEADS * QT_ROWS, tm), _row_spec(tm, SLAB_W), _tile_spec(FOX_W, tm),
                  _row_spec(tm, 128), _row_spec(tm, 128), _row_spec(tm, FOX_W),
                  _row_spec(tm, FOX_W), _row_spec(tm, 2 * SGU_W), _row_spec(tm, 2 * D_MODEL),
                  _row_spec(tm, D_MODEL), _row_spec(tm, D_MODEL), _const_spec(wcat.shape),
                  _const_spec(bdiag.shape), _const_spec((1, FOX_W)), _const_spec((1, FOX_W)),
                  _const_spec((1, D_MODEL)), _const_spec(efold.shape)],
        out_specs=[o[2] for o in outs],
        out_shape=[jax.ShapeDtypeStruct(o[0], o[1]) for o in outs],
        scratch_shapes=[pltpu.VMEM((8, FOX_W), F32), pltpu.VMEM((8, FOX_W), F32),
                        pltpu.VMEM((tm, C_END), BF16)],
        compiler_params=_params(56, 1),
    )(gqt, gk, dvt, dlogf, flog, qraw, kraw, duv, dgp, x, dx1, wcat, bdiag, gq, gk_gain, g1, efold)


def _dw_matmul(at, b, tm, name, after=()):
    m, s_len = at.shape
    n = b.shape[1]

    def body(a_ref, b_ref, *rest):
        rest[-1][...] = jnp.dot(a_ref[...], b_ref[...], preferred_element_type=F32).astype(BF16)

    return pl.pallas_call(
        body, name=name, grid=(m // tm,),
        in_specs=[pl.BlockSpec((tm, s_len), lambda i: (i, 0)), _const_spec(b.shape)]
        + [pl.BlockSpec(memory_space=pl.ANY)] * len(after),
        out_specs=pl.BlockSpec((tm, n), lambda i: (i, 0)),
        out_shape=jax.ShapeDtypeStruct((m, n), BF16),
        compiler_params=_params(48, 1),
    )(at, b, *after)


def _adamw(parts, w, m, v, tr, name, col_tile=None, select=None):
    parts = parts if isinstance(parts, (list, tuple)) else [parts]
    rows, cols = w.shape
    extra = [] if select is None else [select]
    bc1 = 1.0 - ADAM_B1 ** ADAM_STEP
    bc2 = 1.0 - ADAM_B2 ** ADAM_STEP

    def body(*refs):
        p_refs = refs[:len(parts)]
        sel_refs = refs[len(parts):len(parts) + len(extra)]
        w_ref, m_ref, v_ref, g_ref, d_ref, mo_ref, vo_ref = refs[len(parts) + len(extra):]
        g = None
        for p_ref, p in zip(p_refs, parts):
            for idx in range(p.shape[0]):
                term = p_ref[idx].astype(F32)
                g = term if g is None else g + term
        if sel_refs:
            g = _tri_dot(sel_refs[0][...], g)
        g_ref[...] = g
        mn = ADAM_B1 * m_ref[...] + (1.0 - ADAM_B1) * g
        vn = ADAM_B2 * v_ref[...] + (1.0 - ADAM_B2) * (g * g)
        mo_ref[...] = mn
        vo_ref[...] = vn
        m_hat = mn / bc1
        v_hat = vn / bc2
        d_ref[...] = -ADAM_LR * (m_hat / (jnp.sqrt(v_hat) + ADAM_EPS) + ADAM_WD * w_ref[...])

    if col_tile is None:
        spec = pl.BlockSpec((tr, cols), lambda i: (i, 0))
        pspecs = [pl.BlockSpec((p.shape[0], tr, cols), lambda i: (0, i, 0)) for p in parts]
        steps = rows // tr
    else:
        spec = pl.BlockSpec((rows, col_tile), lambda i: (0, i))
        pspecs = [pl.BlockSpec((p.shape[0], p.shape[1], col_tile), lambda i: (0, 0, i))
                  for p in parts]
        steps = cols // col_tile
    return pl.pallas_call(
        body, name=name, grid=(steps,),
        in_specs=pspecs + [_const_spec(e.shape) for e in extra] + [spec, spec, spec],
        out_specs=[spec] * 4,
        out_shape=[jax.ShapeDtypeStruct((rows, cols), F32)] * 4,
        compiler_params=_params(48, 1),
    )(*parts, *extra, w, m, v)


def _sum_parts(parts, name):
    n, rows, cols = parts.shape

    def body(p_ref, o_ref):
        g = p_ref[0]
        for idx in range(1, n):
            g = g + p_ref[idx]
        o_ref[...] = g

    return pl.pallas_call(
        body, name=name, out_shape=jax.ShapeDtypeStruct((rows, cols), F32),
        in_specs=[_const_spec(parts.shape)], out_specs=_const_spec((rows, cols)), grid=(1,),
        compiler_params=_params(16, 1),
    )(parts)


SMALL_NAMES = ("g_pre_mix", "b_forget", "g_q", "g_k", "g_sgu", "b_sgu", "w_spatial", "b_spatial",
               "g_post_mix", "g_pre_ffn", "g_post_ffn")


def _small_rows(size):
    return -(-size // 1024)


def _pack_small(d, extra=None):
    rows = []
    for k in SMALL_NAMES:
        flat = d[k].reshape(-1).astype(F32)
        nr = _small_rows(flat.shape[0])
        rows.append(jnp.pad(flat, (0, nr * 1024 - flat.shape[0])).reshape(nr, 1024))
    if extra is not None:
        rows.append(extra)
    used = sum(r.shape[0] for r in rows)
    rows.append(jnp.zeros((N_DEV * SMALL_ROWS - used, 1024), F32))
    return jnp.concatenate(rows, axis=0)


def _unpack_small(packed, shapes):
    out, off = {}, 0
    for k in SMALL_NAMES:
        size = math.prod(shapes[k])
        nr = _small_rows(size)
        out[k] = packed[off:off + nr].reshape(-1)[:size].reshape(shapes[k])
        off += nr
    return out


def _cols_to_blocks(full, width):
    r = full.shape[0]
    return jnp.transpose(full.reshape(r, N_DEV, width), (1, 0, 2))


def _blocks_to_cols(blocks):
    n, r, width = blocks.shape
    return jnp.transpose(blocks, (1, 0, 2)).reshape(r, n * width)


def kernel(x, g_pre_mix, w_in, b_forget, g_q, g_k, g_sgu, b_sgu, w_spatial, b_spatial, w_branch_a, w_branch_b, w_out, g_post_mix, g_pre_ffn, w_ffn_in, w_ffn_down, g_post_ffn, loss_target, m_g_pre_mix, m_w_in, m_b_forget, m_g_q, m_g_k, m_g_sgu, m_b_sgu, m_w_spatial, m_b_spatial, m_w_branch_a, m_w_branch_b, m_w_out, m_g_post_mix, m_g_pre_ffn, m_w_ffn_in, m_w_ffn_down, m_g_post_ffn, v_g_pre_mix, v_w_in, v_b_forget, v_g_q, v_g_k, v_g_sgu, v_b_sgu, v_w_spatial, v_b_spatial, v_w_branch_a, v_w_branch_b, v_w_out, v_g_post_mix, v_g_pre_ffn, v_w_ffn_in, v_w_ffn_down, v_g_post_ffn):
    big_names = ("w_in", "w_branch_a", "w_branch_b", "w_out", "w_ffn_in", "w_ffn_down")
    weights = dict(g_pre_mix=g_pre_mix, w_in=w_in, b_forget=b_forget, g_q=g_q, g_k=g_k, g_sgu=g_sgu,
                   b_sgu=b_sgu, w_spatial=w_spatial, b_spatial=b_spatial, w_branch_a=w_branch_a,
                   w_branch_b=w_branch_b, w_out=w_out, g_post_mix=g_post_mix, g_pre_ffn=g_pre_ffn,
                   w_ffn_in=w_ffn_in, w_ffn_down=w_ffn_down, g_post_ffn=g_post_ffn)
    mom1 = dict(g_pre_mix=m_g_pre_mix, w_in=m_w_in, b_forget=m_b_forget, g_q=m_g_q, g_k=m_g_k,
                g_sgu=m_g_sgu, b_sgu=m_b_sgu, w_spatial=m_w_spatial, b_spatial=m_b_spatial,
                w_branch_a=m_w_branch_a, w_branch_b=m_w_branch_b, w_out=m_w_out,
                g_post_mix=m_g_post_mix, g_pre_ffn=m_g_pre_ffn, w_ffn_in=m_w_ffn_in,
                w_ffn_down=m_w_ffn_down, g_post_ffn=m_g_post_ffn)
    mom2 = dict(g_pre_mix=v_g_pre_mix, w_in=v_w_in, b_forget=v_b_forget, g_q=v_g_q, g_k=v_g_k,
                g_sgu=v_g_sgu, b_sgu=v_b_sgu, w_spatial=v_w_spatial, b_spatial=v_b_spatial,
                w_branch_a=v_w_branch_a, w_branch_b=v_w_branch_b, w_out=v_w_out,
                g_post_mix=v_g_post_mix, g_pre_ffn=v_g_pre_ffn, w_ffn_in=v_w_ffn_in,
                w_ffn_down=v_w_ffn_down, g_post_ffn=v_g_post_ffn)
    names = list(weights)
    shapes = {k: weights[k].shape for k in names}

    s_len = x.shape[1]
    xs = x.reshape(s_len, D_MODEL)
    tgt = loss_target.reshape(s_len, D_MODEL)

    transposed = ("w_in", "w_ffn_in")

    def local_view(a, k):
        return jnp.transpose(a[0]) if k in transposed else a[0]

    shards = {k: local_view(weights[k], k).astype(BF16) for k in big_names}

    x_pos, y_pos, c_pos = _mesh_pos()
    me = 4 * x_pos + 2 * y_pos + c_pos
    r_idx = jnp.arange(BLK)
    general = (jnp.asarray(BLK_AT, jnp.int32) - jnp.asarray(FRAME_START, jnp.int32))[me] + r_idx
    holder = jnp.where(r_idx < F_AT, BLK_AT[F_DEV] - FRAME_START[F_DEV] + r_idx,
                       jnp.where(r_idx < F_AT + HEADS, FRAME - F_AT + r_idx,
                                 BLK_AT[F_DEV] - FRAME_START[F_DEV] - HEADS + r_idx))
    frame_row = jnp.where(me == F_DEV, holder, general)
    in_frame = (frame_row[:, None] == jnp.arange(FRAME_ROWS)[None, :]).astype(BF16)
    my_frame = jnp.dot(in_frame.T, shards["w_in"], preferred_element_type=F32).astype(BF16)
    wcat = _assemble_w_in(_gather_two_level(my_frame, "gather_w_in"))
    wcat, later = lax.optimization_barrier(
        (wcat, [shards[k] for k in big_names if k != "w_in"]))
    shards.update(zip([k for k in big_names if k != "w_in"], later))
    (gat_mix, gat_ffn), gat_token = _exchange_start(
        [[shards["w_branch_a"], shards["w_branch_b"], shards["w_out"]],
         [shards["w_ffn_in"], shards["w_ffn_down"]]], "gather_start", gather=True)

    seg = np.arange(FOX_W) // HEAD_DIM
    bdiag = jnp.asarray(seg[:128, None] == seg[None, :128], BF16)
    tm = TOKEN_TILE
    lower = np.arange(tm)[None, :] <= np.arange(tm)[:, None]
    tril = jnp.asarray(lower, BF16)
    triu = jnp.asarray(lower.T, BF16)
    egrp = jnp.asarray(seg[:, None] == np.arange(128)[None, :], BF16)
    efold = jnp.asarray((np.arange(FOX_W) % HEAD_DIM)[:, None] == np.arange(128)[None, :], BF16)
    gq512 = jnp.tile(g_q.reshape(1, HEAD_DIM), (1, HEADS))
    gk512 = jnp.tile(g_k.reshape(1, HEAD_DIM), (1, HEADS))
    bfor = jnp.pad(b_forget.reshape(1, HEADS), ((0, 0), (0, 128 - HEADS)))
    pos = np.arange(WINDOW)
    wmask = (pos[None, :] // CHUNK) <= (pos[:, None] // CHUNK)
    wsm_f = jnp.where(jnp.asarray(wmask)[None], w_spatial[0], 0.0)
    wsm = wsm_f.astype(BF16)
    wsmt = jnp.transpose(wsm_f, (0, 2, 1)).astype(BF16)
    bsf = jnp.repeat(jnp.transpose(b_spatial[0]), HEAD_DIM, axis=1)
    wmask_f = jnp.asarray(wmask, F32)

    col = np.arange(SLAB_W)
    row128 = np.arange(128)

    def d_place(first):
        return jnp.asarray(np.stack([(col[None, :] // 128 == row128[:, None])
                                     & (col[None, :] % 128 == first + a) for a in range(3)]), BF16)

    pdq, pdk = d_place(HEAD_DIM), d_place(HEAD_DIM + 3)
    ones_q = jnp.asarray((col % 128 >= HEAD_DIM + 3) & (col % 128 < HEAD_DIM + 6), F32)[None]
    ones_k = jnp.asarray((col % 128 >= HEAD_DIM) & (col % 128 < HEAD_DIM + 3), F32)[None]
    ecol = jnp.asarray((col[:, None] // 128 == row128[None, :])
                       & (col[:, None] % 128 == HEAD_DIM + 3), BF16)

    (h, qa, ka, kat, vs, vt, qraw, kraw, flog, uvpre, gpre) = _proj_fwd(
        xs, g_pre_mix + gat_token[0:1, 0:1], wcat, bdiag, gq512, gk512, bfor, tril, pdq, pdk,
        ones_q, ones_k)
    attn, attn_t, lse = _attn_fwd(qa, ka, vt)
    (own_a, own_b, own_out), (zone_a, zone_b, zone_out) = _exchange_wait(
        gat_mix, attn, "gather_wait_mix", gather=True)
    wa = _blocks_to_cols(_own_block(zone_a, own_a))
    wb = _blocks_to_cols(_own_block(zone_b, own_b))
    wout = _own_block(zone_out, own_out).reshape(D_MODEL, D_MODEL)
    sgu_t, ya, yb, merged_t, om, x1 = _mix_fwd(attn, uvpre, gpre, xs, wa, wb, wout, wsm, bsf,
                                           g_sgu, b_sgu, g_post_mix)
    (own_ffn, own_down), (zone_ffn, zone_down) = _exchange_wait(
        gat_ffn, x1, "gather_wait_ffn", gather=True)
    wffn = _own_block(zone_ffn, own_ffn).reshape(2 * D_FF, D_MODEL)
    wdown = _own_block(zone_down, own_down).reshape(D_FF, D_MODEL)
    (dx1, h2, act_t, dff, dgu_t, loss_acc, dg_post_ffn, dg_pre_ffn) = _ffn_fwd_bwd(
        x1, tgt, wffn, wdown, g_pre_ffn, g_post_ffn)

    dw_down = _dw_matmul(act_t, dff, D_FF // 4, "dw_down")
    dw_ffn = _dw_matmul(dgu_t, h2, 2 * D_FF // N_DEV, "dw_ffn_in")
    def own_of(parts):
        return [lax.dynamic_index_in_dim(p, me, 0, keepdims=False) for p in parts]

    parts_ffn = [dw_ffn.reshape(N_DEV, 2 * D_FF // N_DEV, D_MODEL),
                 dw_down.reshape(N_DEV, D_FF // N_DEV, D_MODEL)]
    mine_ffn = own_of(parts_ffn)
    (sct_ffn,), sct_ffn_token = _exchange_start([parts_ffn], "scatter_start_ffn", gather=False)

    (dom, dya, dyb, dgp, dot_, delta, duv, dws, dbs, dg_sgu, db_sgu, dg_post_mix) = _mix_bwd(
        dx1, om, ya, yb, gpre, uvpre, attn, wout, wa, wb, wsm, wsmt, bsf, g_sgu, b_sgu,
        g_post_mix + sct_ffn_token[0:1, 0:1], wmask_f, egrp)
    dw_out = _dw_matmul(merged_t, dom, 512, "dw_out")
    dw_a = _dw_matmul(attn_t, dya, 512, "dw_a")
    dw_b = _dw_matmul(sgu_t, dyb, 512, "dw_b")
    parts_mix = [_cols_to_blocks(dw_a, D_MODEL // N_DEV), _cols_to_blocks(dw_b, D_MODEL // N_DEV),
                 dw_out.reshape(N_DEV, D_MODEL // N_DEV, D_MODEL)]
    mine_mix = own_of(parts_mix)
    (sct_mix,), sct_mix_token = _exchange_start([parts_mix], "scatter_start_mix", gather=False)

    gk_all, dvt, gqt, col_sums = _attn_bwd(qa, ka, kat, vs, dot_, lse,
                                           delta + sct_mix_token[0, 0], ecol)
    dlogf = _rev_cumsum(col_sums, gqt, triu)
    dx, dproj_t, dgq, dgk, dbf, dg_pre_mix = _proj_bwd(
        gqt, gk_all, dvt, dlogf, flog, qraw, kraw, duv, dgp, xs, dx1, wcat, bdiag, gq512, gk512,
        g_pre_mix, efold)

    small_local = dict(
        g_pre_mix=dg_pre_mix, b_forget=dbf[:, :HEADS], g_q=dgq[0:1, :HEAD_DIM],
        g_k=dgk[0:1, :HEAD_DIM], g_sgu=dg_sgu, b_sgu=db_sgu, w_spatial=dws,
        b_spatial=jnp.transpose(dbs[:, :GROUPS]), g_post_mix=dg_post_mix, g_pre_ffn=dg_pre_ffn,
        g_post_ffn=dg_post_ffn)
    loss_row = jnp.pad(loss_acc[0:1, 0:1], ((0, 0), (0, 1023)))
    small_parts = _pack_small(small_local, loss_row).reshape(N_DEV, SMALL_ROWS, 1024)

    def with_own(zones, own_blocks):
        return [_own_block(z, b) for z, b in zip(zones, own_blocks)]

    mine_small = own_of([small_parts])
    (sct_small,), sct_small_token = _exchange_start([[small_parts]], "scatter_start_small",
                                                    gather=False)
    dw_cat = _dw_matmul(dproj_t, h, C_END // N_DEV, "dw_in", after=(sct_small_token,))
    (recv_small,) = with_own(
        _exchange_wait(sct_small, dw_cat, "scatter_wait_small", gather=False)[1], mine_small)
    small_sum = _sum_parts(recv_small, "sum_small")
    (gat_small,), gat_small_token = _exchange_start([[small_sum]], "gather_start_small",
                                                    gather=True)
    pair_blocks, own_pair = _pair_sums(dw_cat, "pair_sums_in", gat_small_token)
    rs_in, rs_token = _chip_exchange_start(pair_blocks, "chip_exchange_start_in")

    recv_ffn, recv_down = with_own(
        _exchange_wait(sct_ffn, rs_token, "scatter_wait_ffn", gather=False)[1], mine_ffn)
    recv_a, recv_b, recv_out = with_own(
        _exchange_wait(sct_mix, recv_ffn, "scatter_wait_mix", gather=False)[1], mine_mix)
    received = [None, recv_a, recv_b, recv_out, recv_ffn, recv_down]

    grads, deltas, new_m, new_v = {}, {}, {}, {}
    row_tiles = {"w_in": None, "w_branch_a": 512, "w_branch_b": 512, "w_out": 128, "w_ffn_in": 176,
                 "w_ffn_down": 352}

    def update(k, parts):
        outs = _adamw(parts, local_view(weights[k], k), local_view(mom1[k], k),
                      local_view(mom2[k], k), row_tiles[k], "adamw_" + k,
                      col_tile=256 if k == "w_in" else None,
                      select=in_frame if k == "w_in" else None)
        if k in transposed:
            outs = [jnp.transpose(o) for o in outs]
        grads[k], deltas[k], new_m[k], new_v[k] = [o[None] for o in outs]
        return outs[0]

    last = None
    for idx, k in enumerate(big_names):
        if k != "w_in":
            last = update(k, received[idx])

    (own_small,), (zone_small,) = _exchange_wait(gat_small, last, "gather_wait_small", gather=True)
    small_all = _own_block(zone_small, own_small).reshape(1, N_DEV * SMALL_ROWS, 1024)
    sg, sd, sm, sv = _adamw(small_all, _pack_small(weights), _pack_small(mom1), _pack_small(mom2),
                            N_DEV * SMALL_ROWS, "adamw_small")
    for dst, packed in ((grads, sg), (deltas, sd), (new_m, sm), (new_v, sv)):
        dst.update(_unpack_small(packed, shapes))
    arrived = _chip_exchange_wait(rs_in, sg, "chip_exchange_wait_in")
    update("w_in", [own_pair[None], arrived])

    loss = small_all[0, sum(_small_rows(math.prod(shapes[k])) for k in SMALL_NAMES), 0]
    return (loss, dx.reshape(x.shape), *[grads[k] for k in names], *[deltas[k] for k in names],
            *[new_m[k] for k in names], *[new_v[k] for k in names])
```

```python
import functools
import math

import jax
import jax.numpy as jnp
import numpy as np
from jax import lax
from jax.experimental import pallas as pl
from jax.experimental.pallas import tpu as pltpu

F32 = jnp.float32
BF16 = jnp.bfloat16

D_MODEL = 1024
FOX_W = 512
HEADS = 8
HEAD_DIM = 64
SGU_W = 512
GROUPS = 8
WINDOW = 128
CHUNK = 64
D_FF = 2816
IN_COLS = 4616
EPS = 1e-6
N_DEV = 8
LOG2E = 1.4426950408889634
LN2 = 0.6931471805599453

C_Q, C_K, C_V, C_UV, C_G, C_F, C_END = 0, 512, 1024, 1536, 2560, 4608, 4736

ADAM_LR, ADAM_B1, ADAM_B2, ADAM_EPS, ADAM_WD, ADAM_STEP = 0.001, 0.9, 0.999, 1e-08, 0.01, 10

MIB = 1024 * 1024
TOKEN_TILE = 256
ATTN_TILE = 256
SLAB_W = HEADS * 128
QT_ROWS = 72

BLK = IN_COLS // N_DEV
F_LO = 3 * FOX_W
F_DEV = F_LO // BLK
F_AT = F_LO - F_DEV * BLK
BLK_AT = [BLK * j - (HEADS if BLK * j > F_LO else 0) for j in range(N_DEV)]
FRAME_START = [a // 16 * 16 for a in BLK_AT]
FRAME = 608
FRAME_ROWS = FRAME + 16


def _params(vmem_mib, n_axes):
    return pltpu.CompilerParams(
        dimension_semantics=("arbitrary",) * n_axes, vmem_limit_bytes=vmem_mib * MIB)


def _const_spec(shape):
    nd = len(shape)
    return pl.BlockSpec(shape, lambda *_: (0,) * nd)


def _row_spec(tm, cols):
    return pl.BlockSpec((tm, cols), lambda i: (i, 0))


def _tile_spec(rows, tm):
    return pl.BlockSpec((1, rows, tm), lambda i: (i, 0, 0))


def _split3_dot(x, e):
    x1 = x.astype(BF16)
    r1 = x - x1.astype(F32)
    x2 = r1.astype(BF16)
    x3 = (r1 - x2.astype(F32)).astype(BF16)
    dot = functools.partial(jnp.dot, preferred_element_type=F32)
    return dot(x1, e) + dot(x2, e) + dot(x3, e)


def _tri_dot(tri, x):
    x1 = x.astype(BF16)
    r1 = x - x1.astype(F32)
    x2 = r1.astype(BF16)
    x3 = (r1 - x2.astype(F32)).astype(BF16)
    dot = functools.partial(jnp.dot, preferred_element_type=F32)
    return dot(tri, x1) + dot(tri, x2) + dot(tri, x3)


def _seg_mean(sq, bd_ref):
    hi = sq.astype(BF16)
    lo = (sq - hi.astype(F32)).astype(BF16)
    bd = bd_ref[...]
    dot = functools.partial(jnp.dot, preferred_element_type=F32)
    pairs = [dot(hi[:, p * 128:(p + 1) * 128], bd) + dot(lo[:, p * 128:(p + 1) * 128], bd)
             for p in range(HEADS // 2)]
    return jnp.concatenate(pairs, axis=1) * (1.0 / HEAD_DIM)


def _slabs_from_heads(t):
    lane = lax.broadcasted_iota(jnp.int32, (t.shape[0], 128), 1)
    low = lane < HEAD_DIM
    slabs = []
    for p in range(HEADS // 2):
        pair = t[:, p * 128:(p + 1) * 128]
        slabs.append(jnp.where(low, pair, 0.0))
        slabs.append(jnp.where(low, pltpu.roll(pair, HEAD_DIM, 1), 0.0))
    return jnp.concatenate(slabs, axis=1)


def _dot_nt(a, b):
    return lax.dot_general(a, b, (((1,), (1,)), ((), ())), preferred_element_type=F32)


def _dot_tn(a, b):
    return lax.dot_general(a, b, (((0,), (0,)), ((), ())), preferred_element_type=F32)


def _sigmoid(x):
    return 0.5 * jnp.tanh(0.5 * x) + 0.5


_GELU_C = math.sqrt(2.0 / math.pi)


def _gelu_and_grad(x):
    inner = _GELU_C * (x + 0.044715 * x * x * x)
    t = jnp.tanh(inner)
    y = 0.5 * x * (1.0 + t)
    dy = 0.5 * (1.0 + t) + 0.5 * x * (1.0 - t * t) * _GELU_C * (1.0 + 3.0 * 0.044715 * x * x)
    return y, dy


def _rms_bwd(xin, r, g, dy):
    dyg = dy * g
    return r * dyg - xin * (r * r * r) * jnp.mean(dyg * xin, axis=-1, keepdims=True)


def _mesh_pos():
    x, y, c = lax.axis_index("x"), lax.axis_index("y"), lax.axis_index("c")
    return x, y, c


def _peer(k):
    x, y, c = _mesh_pos()
    px = (1 - x) if (k >> 2) & 1 else x
    py = (1 - y) if (k >> 1) & 1 else y
    pc = (1 - c) if k & 1 else c
    return (px, py, pc), 4 * px + 2 * py + pc


def _gather_two_level(shard, name):
    def body(x_ref, out_ref, send_sems, recv_sems, local_sem):
        x, y, c = _mesh_pos()
        me, sibling = (x, y, c), (x, y, 1 - c)
        chips = [(1 - x, y), (x, 1 - y), (1 - x, 1 - y)]

        def slot(px, py, pc):
            return out_ref.at[4 * px + 2 * py + pc]

        def copy(k, block, to, src=None):
            return pltpu.make_async_remote_copy(
                src_ref=slot(*block) if src is None else src, dst_ref=slot(*block),
                send_sem=send_sems.at[k], recv_sem=recv_sems.at[k],
                device_id=to, device_id_type=pl.DeviceIdType.MESH)

        mine = pltpu.make_async_copy(x_ref, slot(*me), local_sem)
        mine.start()
        first = [copy(1 + j, me, (*chip, c), src=x_ref) for j, chip in enumerate(chips)]
        first.append(copy(0, me, sibling, src=x_ref))
        for cp in first:
            cp.start()
        passed = [copy(4 + j, (*chip, c), sibling) for j, chip in enumerate(chips)]
        for j, chip in enumerate(chips):
            copy(1 + j, (*chip, c), me).wait_recv()
            passed[j].start()
        copy(0, sibling, me).wait_recv()
        for j, chip in enumerate(chips):
            copy(4 + j, (*chip, 1 - c), me).wait_recv()
        for cp in first + passed:
            cp.wait_send()
        mine.wait()

    any_spec = pl.BlockSpec(memory_space=pl.ANY)
    return pl.pallas_call(
        body, name=name, out_shape=jax.ShapeDtypeStruct((N_DEV,) + shard.shape, shard.dtype),
        in_specs=[any_spec], out_specs=any_spec,
        scratch_shapes=[pltpu.SemaphoreType.DMA((7,)), pltpu.SemaphoreType.DMA((7,)),
                        pltpu.SemaphoreType.DMA],
    )(shard)


def _assemble_w_in(zone):
    def body(z_ref, o_ref):
        o_ref[...] = jnp.zeros_like(o_ref)
        for j in range(N_DEV):
            rows = slice(FRAME_START[j], FRAME_START[j] + FRAME)
            o_ref[rows, :] = (o_ref[rows, :].astype(F32) + z_ref[j, :FRAME, :].astype(F32)).astype(BF16)
        rows = slice(C_F, C_F + FRAME_ROWS - FRAME)
        o_ref[rows, :] = (o_ref[rows, :].astype(F32) + z_ref[F_DEV, FRAME:, :].astype(F32)).astype(BF16)

    return pl.pallas_call(
        body, name="assemble_w_in", grid=(1,),
        in_specs=[_const_spec(zone.shape)], out_specs=_const_spec((C_END, zone.shape[2])),
        out_shape=jax.ShapeDtypeStruct((C_END, zone.shape[2]), zone.dtype),
        compiler_params=_params(56, 1),
    )(zone)


def _chip_peer(k):
    x, y, c = _mesh_pos()
    px = (1 - x) if (k >> 1) & 1 else x
    py = (1 - y) if k & 1 else y
    return (px, py, c), 2 * px + py


def _frame_start(j):
    at = BLK * j - jnp.where(BLK * j > F_LO, HEADS, 0)
    return pl.multiple_of(at // 16 * 16, 16)


def _pair_sums(dw_cat, name, after):
    rows, cols = FRAME_ROWS, dw_cat.shape[1]
    n_chips = N_DEV // 2

    def pieces(p_ref, j):
        return (p_ref.at[pl.ds(_frame_start(j), FRAME)], p_ref.at[pl.ds(C_F, FRAME_ROWS - FRAME)])

    def body(p_ref, after_ref, send_ref, own_ref, mine_buf, sib_buf, send_sems, recv_sems,
             local_sems):
        x, y, c = _mesh_pos()
        sibling = (x, y, 1 - c)
        copies, local = [], []
        for q in range(n_chips):
            for part, (lo, hi) in enumerate(((0, FRAME), (FRAME, FRAME_ROWS))):
                cp = pltpu.make_async_remote_copy(
                    src_ref=pieces(p_ref, 2 * q + (1 - c))[part], dst_ref=sib_buf.at[q, lo:hi],
                    send_sem=send_sems.at[2 * q + part], recv_sem=recv_sems.at[2 * q + part],
                    device_id=sibling, device_id_type=pl.DeviceIdType.MESH)
                cp.start()
                copies.append(cp)
                lc = pltpu.make_async_copy(pieces(p_ref, 2 * q + c)[part], mine_buf.at[q, lo:hi],
                                           local_sems.at[2 * q + part])
                lc.start()
                local.append(lc)
        for lc in local:
            lc.wait()
        for cp in copies:
            cp.wait_recv()
        for k in range(1, n_chips):
            _, q = _chip_peer(k)
            send_ref[k - 1] = (mine_buf[q].astype(F32) + sib_buf[q].astype(F32)).astype(BF16)
        my_chip = 2 * x + y
        own_ref[...] = mine_buf[my_chip].astype(F32) + sib_buf[my_chip].astype(F32)
        for cp in copies:
            cp.wait_send()

    vmem = pl.BlockSpec(memory_space=pltpu.VMEM)
    return pl.pallas_call(
        body, name=name,
        out_shape=[jax.ShapeDtypeStruct((n_chips - 1, rows, cols), BF16),
                   jax.ShapeDtypeStruct((rows, cols), F32)],
        in_specs=[pl.BlockSpec(memory_space=pl.ANY)] * 2, out_specs=[vmem, vmem],
        scratch_shapes=[pltpu.VMEM((n_chips, rows, cols), BF16),
                        pltpu.VMEM((n_chips, rows, cols), BF16),
                        pltpu.SemaphoreType.DMA((2 * n_chips,)),
                        pltpu.SemaphoreType.DMA((2 * n_chips,)),
                        pltpu.SemaphoreType.DMA((2 * n_chips,))],
        compiler_params=pltpu.CompilerParams(vmem_limit_bytes=40 * MIB),
    )(dw_cat, after)


def _chip_copy(src_ref, land_ref, send_sem, recv_sem, k):
    peer, _ = _chip_peer(k)
    return pltpu.make_async_remote_copy(
        src_ref=src_ref.at[k - 1], dst_ref=land_ref.at[k - 1], send_sem=send_sem, recv_sem=recv_sem,
        device_id=peer, device_id_type=pl.DeviceIdType.MESH)


def _chip_exchange_start(blocks, name):
    hbm = pl.BlockSpec(memory_space=pltpu.HBM)
    sem = pl.BlockSpec(memory_space=pltpu.SEMAPHORE)
    n_peers = blocks.shape[0]

    def body(src_ref, zone_ref, send_sems, recv_sems, src_thru, zone_thru, token):
        for k in range(1, n_peers + 1):
            _chip_copy(src_ref, zone_ref, send_sems.at[k - 1], recv_sems.at[k - 1], k).start()
        token[...] = jnp.zeros_like(token)

    outs = pl.pallas_call(
        body, name=name, in_specs=[hbm, hbm],
        out_shape=[pltpu.SemaphoreType.DMA((n_peers,)), pltpu.SemaphoreType.DMA((n_peers,)),
                   pltpu.HBM(blocks.shape, blocks.dtype), pltpu.HBM(blocks.shape, blocks.dtype),
                   jax.ShapeDtypeStruct((8, 128), F32)],
        out_specs=[sem, sem, hbm, hbm, pl.BlockSpec(memory_space=pltpu.VMEM)],
        input_output_aliases={0: 2, 1: 3},
        compiler_params=pltpu.CompilerParams(
            has_side_effects=pltpu.SideEffectType.DATAFLOW_SIDE_EFFECTING),
    )(pltpu.with_memory_space_constraint(blocks, pltpu.HBM),
      pltpu.with_memory_space_constraint(lax.empty(blocks.shape, blocks.dtype), pltpu.HBM))
    return outs[:4], outs[4]


def _chip_exchange_wait(handle, after, name):
    send_sems, recv_sems, src, zone = handle
    hbm = pl.BlockSpec(memory_space=pltpu.HBM)
    sem = pl.BlockSpec(memory_space=pltpu.SEMAPHORE)

    def body(src_ref, zone_ref, ssem, rsem, after_ref, src_out, zone_out):
        for k in range(1, src.shape[0] + 1):
            cp = _chip_copy(src_ref, zone_ref, ssem.at[k - 1], rsem.at[k - 1], k)
            cp.wait_send()
            cp.wait_recv()

    outs = pl.pallas_call(
        body, name=name,
        in_specs=[hbm, hbm, sem, sem, pl.BlockSpec(memory_space=pl.ANY)],
        out_shape=[pltpu.HBM(src.shape, src.dtype), pltpu.HBM(zone.shape, zone.dtype)],
        out_specs=[hbm, hbm], input_output_aliases={0: 0, 1: 1},
        compiler_params=pltpu.CompilerParams(
            has_side_effects=pltpu.SideEffectType.DATAFLOW_SIDE_EFFECTING),
    )(src, zone, send_sems, recv_sems, after)
    return outs[1]


def _remote_copy(gather, src_ref, land_ref, send_sem, recv_sem, k, receive_side):
    x, y, c = _mesh_pos()
    me = 4 * x + 2 * y + c
    peer, pidx = _peer(k)
    return pltpu.make_async_remote_copy(
        src_ref=src_ref if gather else src_ref.at[pidx],
        dst_ref=land_ref.at[pidx if receive_side else me],
        send_sem=send_sem, recv_sem=recv_sem,
        device_id=peer, device_id_type=pl.DeviceIdType.MESH)


def _exchange_start(groups, name, gather, after=()):
    arrs = [a for g in groups for a in g]
    n, n_groups = len(arrs), len(groups)
    lands = [jax.ShapeDtypeStruct(((N_DEV,) + a.shape) if gather else a.shape, a.dtype)
             for a in arrs]

    def body(*refs):
        srcs, zones = refs[:n], refs[n:2 * n]
        outs_at = 2 * n + len(after)
        sems = refs[outs_at:outs_at + 2 * n_groups]
        token = refs[-1]
        a = 0
        for gi, g in enumerate(groups):
            send_sems, recv_sems = sems[2 * gi], sems[2 * gi + 1]
            for k in range(1, N_DEV):
                for ai in range(len(g)):
                    slot = ai * (N_DEV - 1) + k - 1
                    _remote_copy(gather, srcs[a + ai], zones[a + ai], send_sems.at[slot],
                                 recv_sems.at[slot], k, False).start()
            a += len(g)
        token[...] = jnp.zeros_like(token)

    hbm = pl.BlockSpec(memory_space=pltpu.HBM)
    sem = pl.BlockSpec(memory_space=pltpu.SEMAPHORE)
    sem_shapes = []
    for g in groups:
        sem_shapes += [pltpu.SemaphoreType.DMA((len(g) * (N_DEV - 1),))] * 2
    outs = pl.pallas_call(
        body, name=name,
        in_specs=[hbm] * (2 * n) + [pl.BlockSpec(memory_space=pl.ANY)] * len(after),
        out_shape=sem_shapes + [pltpu.HBM(a.shape, a.dtype) for a in arrs]
        + [pltpu.HBM(z.shape, z.dtype) for z in lands] + [jax.ShapeDtypeStruct((8, 128), F32)],
        out_specs=[sem] * (2 * n_groups) + [hbm] * (2 * n)
        + [pl.BlockSpec(memory_space=pltpu.VMEM)],
        input_output_aliases={i: 2 * n_groups + i for i in range(2 * n)},
        compiler_params=pltpu.CompilerParams(
            has_side_effects=pltpu.SideEffectType.DATAFLOW_SIDE_EFFECTING),
    )(*[pltpu.with_memory_space_constraint(a, pltpu.HBM) for a in arrs],
      *[pltpu.with_memory_space_constraint(lax.empty(z.shape, z.dtype), pltpu.HBM) for z in lands],
      *after)
    sems = outs[:2 * n_groups]
    thru = outs[2 * n_groups:2 * n_groups + n]
    zones = outs[2 * n_groups + n:2 * n_groups + 2 * n]
    handles, a = [], 0
    for gi, g in enumerate(groups):
        handles.append((sems[2 * gi], sems[2 * gi + 1], thru[a:a + len(g)], zones[a:a + len(g)]))
        a += len(g)
    return handles, outs[-1]


def _exchange_wait(handle, after, name, gather):
    send_sems, recv_sems, thru, zones = handle
    n = len(thru)

    def body(*refs):
        srcs, lands = refs[:n], refs[n:2 * n]
        ssem, rsem = refs[2 * n], refs[2 * n + 1]
        for k in range(1, N_DEV):
            for ai in range(n):
                slot = ai * (N_DEV - 1) + k - 1
                cp = _remote_copy(gather, srcs[ai], lands[ai], ssem.at[slot], rsem.at[slot], k, True)
                cp.wait_send()
                cp.wait_recv()

    hbm = pl.BlockSpec(memory_space=pltpu.HBM)
    sem = pl.BlockSpec(memory_space=pltpu.SEMAPHORE)
    outs = pl.pallas_call(
        body, name=name,
        in_specs=[hbm] * (2 * n) + [sem, sem, pl.BlockSpec(memory_space=pl.ANY)],
        out_shape=[pltpu.HBM(a.shape, a.dtype) for a in thru]
        + [pltpu.HBM(z.shape, z.dtype) for z in zones],
        out_specs=[hbm] * (2 * n),
        input_output_aliases={i: i for i in range(2 * n)},
        compiler_params=pltpu.CompilerParams(
            has_side_effects=pltpu.SideEffectType.DATAFLOW_SIDE_EFFECTING),
    )(*thru, *zones, send_sems, recv_sems, after)
    return outs[:n], outs[n:]


def _own_block(zone, block):
    x, y, c = _mesh_pos()
    me = 4 * x + 2 * y + c
    return lax.dynamic_update_slice_in_dim(zone, block[None], me, axis=0)


def _proj_fwd(x, g1, wcat, bdiag, gq, gk, bfor, tri, pdq, pdk, ones_q, ones_k):
    s_len = x.shape[0]
    tm = TOKEN_TILE
    nt = s_len // tm

    def body(x_ref, g1_ref, w_ref, bd_ref, gq_ref, gk_ref, bf_ref, tri_ref, pdq_ref,
             pdk_ref, oq_ref, ok_ref,
             h_ref, qa_ref, ka_ref, kat_ref, vs_ref, vt_ref, qr_ref, kr_ref, flog_ref, uv_ref,
             gp_ref, carry):
        @pl.when(pl.program_id(0) == 0)
        def _():
            carry[...] = jnp.zeros_like(carry)

        xf = x_ref[...]
        r = lax.rsqrt(jnp.mean(xf * xf, axis=-1, keepdims=True) + EPS)
        h = (xf * r * g1_ref[...]).astype(BF16)
        h_ref[...] = h
        dot = functools.partial(jnp.dot, preferred_element_type=F32)

        def proj(lo, hi):
            return _dot_nt(h, w_ref[lo:hi, :])

        flog = proj(C_F, C_END) + bf_ref[...]
        flog_ref[...] = flog
        lane = lax.broadcasted_iota(jnp.int32, flog.shape, 1)
        logf = jnp.minimum(flog, 0.0) - jnp.log(1.0 + jnp.exp(-jnp.abs(flog)))
        logf = jnp.where(lane < HEADS, logf, 0.0)
        dcum = _tri_dot(tri_ref[...], logf) + carry[...]
        carry[...] = dcum[tm - 1:tm, :]
        d2 = dcum * LOG2E
        d2a = d2.astype(BF16)
        rem = d2 - d2a.astype(F32)
        d2b = rem.astype(BF16)
        d2c = (rem - d2b.astype(F32)).astype(BF16)

        q = proj(C_Q, C_K)
        qr_ref[...] = q.astype(BF16)
        rq = lax.rsqrt(_seg_mean(q * q, bd_ref) + EPS)
        qn = q * rq * (gq_ref[...] * (HEAD_DIM ** -0.5 * LOG2E))
        d_parts = jnp.concatenate([d2a, d2b, d2c], axis=1)
        qa = _slabs_from_heads(qn) + dot(d_parts, pdq_ref[...]) + oq_ref[...]
        qa_ref[...] = qa.astype(BF16)

        k = proj(C_K, C_V)
        kr_ref[...] = k.astype(BF16)
        rk = lax.rsqrt(_seg_mean(k * k, bd_ref) + EPS)
        kn = k * rk * gk_ref[...]
        ka = _slabs_from_heads(kn) + dot(d_parts, pdk_ref[...]) + ok_ref[...]
        ka_ref[...] = ka.astype(BF16)
        kat_ref[0] = ka.T.astype(BF16)

        v = proj(C_V, C_UV)
        vs_ref[...] = _slabs_from_heads(v).astype(BF16)
        vt_ref[0] = v.T.astype(BF16)
        uv_ref[...] = proj(C_UV, C_G).astype(BF16)
        gp_ref[...] = proj(C_G, C_F).astype(BF16)

    outs = [((s_len, D_MODEL), BF16, _row_spec(tm, D_MODEL)),
            ((s_len, SLAB_W), BF16, _row_spec(tm, SLAB_W)),
            ((s_len, SLAB_W), BF16, _row_spec(tm, SLAB_W)),
            ((nt, SLAB_W, tm), BF16, _tile_spec(SLAB_W, tm)),
            ((s_len, SLAB_W), BF16, _row_spec(tm, SLAB_W)),
            ((nt, FOX_W, tm), BF16, _tile_spec(FOX_W, tm)),
            ((s_len, FOX_W), BF16, _row_spec(tm, FOX_W)),
            ((s_len, FOX_W), BF16, _row_spec(tm, FOX_W)),
            ((s_len, 128), F32, _row_spec(tm, 128)),
            ((s_len, 2 * SGU_W), BF16, _row_spec(tm, 2 * SGU_W)),
            ((s_len, 2 * D_MODEL), BF16, _row_spec(tm, 2 * D_MODEL))]
    return pl.pallas_call(
        body, name="proj_fwd", grid=(nt,),
        in_specs=[_row_spec(tm, D_MODEL), _const_spec((1, D_MODEL)), _const_spec(wcat.shape),
                  _const_spec(bdiag.shape), _const_spec((1, FOX_W)), _const_spec((1, FOX_W)),
                  _const_spec((1, 128)), _const_spec((tm, tm)), _const_spec(pdq.shape), _const_spec(pdk.shape), _const_spec(ones_q.shape),
                  _const_spec(ones_k.shape)],
        out_specs=[o[2] for o in outs],
        out_shape=[jax.ShapeDtypeStruct(o[0], o[1]) for o in outs],
        scratch_shapes=[pltpu.VMEM((1, 128), F32)],
        compiler_params=_params(56, 1),
    )(x, g1, wcat, bdiag, gq, gk, bfor, tri, pdq, pdk, ones_q, ones_k)


def _attn_fwd(qa, ka, vt):
    s_len = qa.shape[0]
    t = ATTN_TILE
    nb = s_len // t

    def body(q_ref, k_ref, vt_ref, o_ref, ot_ref, lse_ref, m_sc, l_sc, acc_sc, s_sc, mcur_sc,
             alpha_sc):
        i = pl.program_id(0)
        m_sc[...] = jnp.full_like(m_sc, -jnp.inf)
        l_sc[...] = jnp.zeros_like(l_sc)
        acc_sc[...] = jnp.zeros_like(acc_sc)

        def logits(j, slot, masked):
            krows = pl.ds(pl.multiple_of(j * t, t), t)
            if masked:
                keep = (lax.broadcasted_iota(jnp.int32, (t, t), 0)
                        <= lax.broadcasted_iota(jnp.int32, (t, t), 1))
            for hd in range(HEADS):
                sl = slice(hd * 128, (hd + 1) * 128)
                st = _dot_nt(k_ref[krows, sl], q_ref[:, sl])
                if masked:
                    st = jnp.where(keep, st, -jnp.inf)
                s_sc[slot, hd] = st
                m_prev = m_sc[hd:hd + 1, :]
                m_new = jnp.maximum(m_prev, jnp.max(st, axis=0, keepdims=True))
                alpha_sc[slot, hd:hd + 1, :] = jnp.exp2(m_prev - m_new)
                mcur_sc[slot, hd:hd + 1, :] = m_new
                m_sc[hd:hd + 1, :] = m_new

        def accumulate(j, slot):
            for hd in range(HEADS):
                hr = slice(hd * HEAD_DIM, (hd + 1) * HEAD_DIM)
                alpha = alpha_sc[slot, hd:hd + 1, :]
                pt = jnp.exp2(s_sc[slot, hd] - mcur_sc[slot, hd:hd + 1, :])
                l_sc[hd:hd + 1, :] = alpha * l_sc[hd:hd + 1, :] + jnp.sum(pt, axis=0, keepdims=True)
                acc_sc[hr, :] = alpha * acc_sc[hr, :] + jnp.dot(
                    vt_ref[j, hr, :], pt.astype(BF16), preferred_element_type=F32)

        @pl.when(i == 0)
        def _():
            logits(0, 0, True)
            accumulate(0, 0)

        pairs = (i - 1) // 2

        @pl.when(i > 0)
        def _():
            logits(0, 0, False)

            def two_blocks(p, carry):
                logits(2 * p + 1, 1, False)
                accumulate(2 * p, 0)
                logits(2 * p + 2, 0, False)
                accumulate(2 * p + 1, 1)
                return carry

            lax.fori_loop(0, pairs, two_blocks, 0)

        @pl.when((i > 0) & (i - 2 * pairs == 1))
        def _():
            logits(i, 1, True)
            accumulate(i - 1, 0)
            accumulate(i, 1)

        @pl.when((i > 0) & (i - 2 * pairs == 2))
        def _():
            logits(i - 1, 1, False)
            accumulate(i - 2, 0)
            logits(i, 0, True)
            accumulate(i - 1, 1)
            accumulate(i, 0)

        for hd in range(HEADS):
            hr = slice(hd * HEAD_DIM, (hd + 1) * HEAD_DIM)
            l = l_sc[hd:hd + 1, :]
            acc_sc[hr, :] = acc_sc[hr, :] / l
            lse_ref[0, hd:hd + 1, :] = m_sc[hd:hd + 1, :] + jnp.log2(l)
        o_ref[...] = acc_sc[...].T.astype(BF16)
        ot_ref[...] = acc_sc[...].astype(BF16)

    return pl.pallas_call(
        body, name="attn_fwd", grid=(nb,),
        in_specs=[_row_spec(t, SLAB_W), _const_spec(ka.shape), _const_spec(vt.shape)],
        out_specs=[_row_spec(t, FOX_W), pl.BlockSpec((FOX_W, t), lambda i: (0, i)),
                   _tile_spec(HEADS, t)],
        out_shape=[jax.ShapeDtypeStruct((s_len, FOX_W), BF16),
                   jax.ShapeDtypeStruct((FOX_W, s_len), BF16),
                   jax.ShapeDtypeStruct((nb, HEADS, t), F32)],
        scratch_shapes=[pltpu.VMEM((HEADS, t), F32), pltpu.VMEM((HEADS, t), F32),
                        pltpu.VMEM((FOX_W, t), F32), pltpu.VMEM((2, HEADS, t, t), F32),
                        pltpu.VMEM((2, HEADS, t), F32), pltpu.VMEM((2, HEADS, t), F32)],
        compiler_params=_params(48, 1),
    )(qa, ka, vt)


def _sgu_mix(vn, ws_ref):
    tm = vn.shape[0]
    lane = lax.broadcasted_iota(jnp.int32, (WINDOW, 128), 1)
    low = lane < HEAD_DIM
    wins = []
    for w in range(tm // WINDOW):
        slabs = []
        for p in range(GROUPS // 2):
            v2 = vn[w * WINDOW:(w + 1) * WINDOW, p * 128:(p + 1) * 128]
            lo = jnp.where(low, v2, 0.0).astype(BF16)
            hi = jnp.where(low, 0.0, v2).astype(BF16)
            slabs.append(jnp.dot(ws_ref[2 * p], lo, preferred_element_type=F32)
                         + jnp.dot(ws_ref[2 * p + 1], hi, preferred_element_type=F32))
        wins.append(jnp.concatenate(slabs, axis=1))
    return jnp.concatenate(wins, axis=0) if len(wins) > 1 else wins[0]


def _layernorm_fwd(vv, g, b):
    mu = jnp.mean(vv, axis=-1, keepdims=True)
    xc = vv - mu
    r = lax.rsqrt(jnp.mean(xc * xc, axis=-1, keepdims=True) + EPS)
    xh = xc * r
    return xh * g + b, xh, r


def _mix_fwd(attn, uvpre, gpre, x, wa, wb, wout, wsm, bsf, gsgu, bsgu, gpost):
    s_len = x.shape[0]
    tm = TOKEN_TILE

    def body(o_ref, uv_ref, gp_ref, x_ref, wa_ref, wb_ref, wo_ref, ws_ref, bs_ref, gs_ref, bsg_ref,
             gpost_ref, sgut_ref, ya_ref, yb_ref, mgt_ref, om_ref, x1_ref):
        uvp = uv_ref[...].astype(F32)
        uv, _ = _gelu_and_grad(uvp)
        u, vv = uv[:, :SGU_W], uv[:, SGU_W:]
        vn, _, _ = _layernorm_fwd(vv, gs_ref[...], bsg_ref[...])
        bias = bs_ref[...]
        if tm > WINDOW:
            bias = jnp.concatenate([bias] * (tm // WINDOW), axis=0)
        mixed = _sgu_mix(vn, ws_ref) + bias
        sgu_f = u * mixed
        sgu = sgu_f.astype(BF16)
        sgut_ref[...] = sgu_f.T.astype(BF16)
        ya = jnp.dot(o_ref[...], wa_ref[...], preferred_element_type=F32)
        yb = jnp.dot(sgu, wb_ref[...], preferred_element_type=F32)
        ya_ref[...] = ya.astype(BF16)
        yb_ref[...] = yb.astype(BF16)
        gates = _sigmoid(gp_ref[...].astype(F32))
        merged_f = gates[:, :D_MODEL] * ya + gates[:, D_MODEL:] * yb
        merged = merged_f.astype(BF16)
        mgt_ref[...] = merged_f.T.astype(BF16)
        om = jnp.dot(merged, wo_ref[...], preferred_element_type=F32)
        om_ref[...] = om
        r = lax.rsqrt(jnp.mean(om * om, axis=-1, keepdims=True) + EPS)
        x1_ref[...] = x_ref[...] + om * r * gpost_ref[...]

    def t_out(rows):
        return ((rows, s_len), BF16, pl.BlockSpec((rows, tm), lambda i: (0, i)))

    def r_out(cols, dt):
        return ((s_len, cols), dt, _row_spec(tm, cols))

    outs = [t_out(SGU_W), r_out(D_MODEL, BF16), r_out(D_MODEL, BF16), t_out(D_MODEL),
            r_out(D_MODEL, F32), r_out(D_MODEL, F32)]
    return pl.pallas_call(
        body, name="mix_fwd", grid=(s_len // tm,),
        in_specs=[_row_spec(tm, FOX_W), _row_spec(tm, 2 * SGU_W), _row_spec(tm, 2 * D_MODEL),
                  _row_spec(tm, D_MODEL), _const_spec(wa.shape), _const_spec(wb.shape),
                  _const_spec(wout.shape), _const_spec(wsm.shape), _const_spec(bsf.shape),
                  _const_spec((1, SGU_W)), _const_spec((1, SGU_W)), _const_spec((1, D_MODEL))],
        out_specs=[o[2] for o in outs],
        out_shape=[jax.ShapeDtypeStruct(o[0], o[1]) for o in outs],
        compiler_params=_params(48, 1),
    )(attn, uvpre, gpre, x, wa, wb, wout, wsm, bsf, gsgu, bsgu, gpost)


def _ffn_fwd_bwd(x1, tgt, wffn, wdown, gpre, gpost):
    s_len = x1.shape[0]
    tm = TOKEN_TILE

    def body(x1_ref, t_ref, wi_ref, wd_ref, gpre_ref, gpost_ref,
             dx1_ref, h2_ref, actt_ref, dff_ref, dgut_ref, loss_ref, dgpost_ref, dgpre_ref):
        @pl.when(pl.program_id(0) == 0)
        def _():
            loss_ref[...] = jnp.zeros_like(loss_ref)
            dgpost_ref[...] = jnp.zeros_like(dgpost_ref)
            dgpre_ref[...] = jnp.zeros_like(dgpre_ref)

        x1v = x1_ref[...]
        r2 = lax.rsqrt(jnp.mean(x1v * x1v, axis=-1, keepdims=True) + EPS)
        gpre_v = gpre_ref[...]
        h2 = (x1v * r2 * gpre_v).astype(BF16)
        h2_ref[...] = h2
        gg = _dot_nt(h2, wi_ref[:D_FF, :])
        uu = _dot_nt(h2, wi_ref[D_FF:, :])
        sg = _sigmoid(gg)
        silu = gg * sg
        act_f = silu * uu
        act = act_f.astype(BF16)
        actt_ref[...] = act_f.T.astype(BF16)
        ff = jnp.dot(act, wd_ref[...], preferred_element_type=F32)
        r3 = lax.rsqrt(jnp.mean(ff * ff, axis=-1, keepdims=True) + EPS)
        gpost_v = gpost_ref[...]
        y = x1v + ff * r3 * gpost_v
        err = y - t_ref[...]
        loss_ref[...] += jnp.sum(err * err) * (0.5 / D_MODEL)
        dy = err * (1.0 / D_MODEL)
        dgpost_ref[...] += jnp.sum(dy * ff * r3, axis=0, keepdims=True)
        dff = _rms_bwd(ff, r3, gpost_v, dy).astype(BF16)
        dff_ref[...] = dff
        dact = _dot_nt(dff, wd_ref[...])
        dgg_f = dact * uu * (sg * (1.0 + gg * (1.0 - sg)))
        duu_f = dact * silu
        dgg = dgg_f.astype(BF16)
        duu = duu_f.astype(BF16)
        dgut_ref[:D_FF, :] = dgg_f.T.astype(BF16)
        dgut_ref[D_FF:, :] = duu_f.T.astype(BF16)
        dh2 = (jnp.dot(dgg, wi_ref[:D_FF, :], preferred_element_type=F32)
               + jnp.dot(duu, wi_ref[D_FF:, :], preferred_element_type=F32))
        dgpre_ref[...] += jnp.sum(dh2 * x1v * r2, axis=0, keepdims=True)
        dx1_ref[...] = dy + _rms_bwd(x1v, r2, gpre_v, dh2)

    outs = [((s_len, D_MODEL), F32, _row_spec(tm, D_MODEL)),
            ((s_len, D_MODEL), BF16, _row_spec(tm, D_MODEL)),
            ((D_FF, s_len), BF16, pl.BlockSpec((D_FF, tm), lambda i: (0, i))),
            ((s_len, D_MODEL), BF16, _row_spec(tm, D_MODEL)),
            ((2 * D_FF, s_len), BF16, pl.BlockSpec((2 * D_FF, tm), lambda i: (0, i))),
            ((1, 128), F32, _const_spec((1, 128))),
            ((1, D_MODEL), F32, _const_spec((1, D_MODEL))),
            ((1, D_MODEL), F32, _const_spec((1, D_MODEL)))]
    return pl.pallas_call(
        body, name="ffn_fwd_bwd", grid=(s_len // tm,),
        in_specs=[_row_spec(tm, D_MODEL), _row_spec(tm, D_MODEL), _const_spec(wffn.shape),
                  _const_spec(wdown.shape), _const_spec((1, D_MODEL)), _const_spec((1, D_MODEL))],
        out_specs=[o[2] for o in outs],
        out_shape=[jax.ShapeDtypeStruct(o[0], o[1]) for o in outs],
        compiler_params=_params(60, 1),
    )(x1, tgt, wffn, wdown, gpre, gpost)


def _mix_bwd(dx1, om, ya, yb, gpre, uvpre, attn, wout, wa, wb, wsm, wsmt, bsf, gsgu, bsgu, gpost,
             wmask, egrp):
    s_len = dx1.shape[0]
    tm = TOKEN_TILE
    nw = tm // WINDOW
    nt = s_len // tm

    def body(dx1_ref, om_ref, ya_ref, yb_ref, gp_ref, uv_ref, o_ref, wo_ref, wa_ref, wb_ref, ws_ref,
             wst_ref, bs_ref, gs_ref, bsg_ref, gpost_ref, mask_ref, eg_ref,
             dom_ref, dya_ref, dyb_ref, dgp_ref, dot_ref, delta_ref, duv_ref,
             dws_ref, dbs_ref, dgs_ref, dbsg_ref, dgpost_ref, dbs_acc):
        step = pl.program_id(0)

        @pl.when(step == 0)
        def _():
            dws_ref[...] = jnp.zeros_like(dws_ref)
            dbs_acc[...] = jnp.zeros_like(dbs_acc)
            dgs_ref[...] = jnp.zeros_like(dgs_ref)
            dbsg_ref[...] = jnp.zeros_like(dbsg_ref)
            dgpost_ref[...] = jnp.zeros_like(dgpost_ref)

        om = om_ref[...]
        dx1v = dx1_ref[...]
        r = lax.rsqrt(jnp.mean(om * om, axis=-1, keepdims=True) + EPS)
        gpost_v = gpost_ref[...]
        dgpost_ref[...] += jnp.sum(dx1v * om * r, axis=0, keepdims=True)
        dom = _rms_bwd(om, r, gpost_v, dx1v).astype(BF16)
        dom_ref[...] = dom
        dmg = _dot_nt(dom, wo_ref[...])

        gates = _sigmoid(gp_ref[...].astype(F32))
        ga, gb = gates[:, :D_MODEL], gates[:, D_MODEL:]
        yav, ybv = ya_ref[...].astype(F32), yb_ref[...].astype(F32)
        dya = (dmg * ga).astype(BF16)
        dyb = (dmg * gb).astype(BF16)
        dya_ref[...] = dya
        dyb_ref[...] = dyb
        dgp_ref[:, :D_MODEL] = (dmg * yav * ga * (1.0 - ga)).astype(BF16)
        dgp_ref[:, D_MODEL:] = (dmg * ybv * gb * (1.0 - gb)).astype(BF16)

        dat_t = _dot_nt(dya, wa_ref[...]).T.astype(BF16)
        dot_ref[0] = dat_t
        o_t = o_ref[...].astype(F32).T
        delta_ref[0] = jnp.sum((dat_t.astype(F32) * o_t).reshape(HEADS, HEAD_DIM, tm), axis=1)
        dsgu = _dot_nt(dyb, wb_ref[...])

        uvp = uv_ref[...].astype(F32)
        uv, guv = _gelu_and_grad(uvp)
        u, vv = uv[:, :SGU_W], uv[:, SGU_W:]
        gs_v = gs_ref[...]
        vn, xh, rln = _layernorm_fwd(vv, gs_v, bsg_ref[...])
        bias = bs_ref[...]
        if nw > 1:
            bias = jnp.concatenate([bias] * nw, axis=0)
        mixed = _sgu_mix(vn, ws_ref) + bias
        du = dsgu * mixed
        dmixed = dsgu * u

        lane = lax.broadcasted_iota(jnp.int32, (WINDOW, 128), 1)
        low = lane < HEAD_DIM
        dvn_wins = []
        for w in range(nw):
            rows = slice(w * WINDOW, (w + 1) * WINDOW)
            dbs_acc[...] += dmixed[rows, :]
            slabs = []
            for p in range(GROUPS // 2):
                cols = slice(p * 128, (p + 1) * 128)
                dm2 = dmixed[rows, cols]
                dlo = jnp.where(low, dm2, 0.0).astype(BF16)
                dhi = jnp.where(low, 0.0, dm2).astype(BF16)
                vn2 = vn[rows, cols].astype(BF16)
                dws_ref[2 * p] += _dot_nt(dlo, vn2)
                dws_ref[2 * p + 1] += _dot_nt(dhi, vn2)
                slabs.append(jnp.dot(wst_ref[2 * p], dlo, preferred_element_type=F32)
                             + jnp.dot(wst_ref[2 * p + 1], dhi, preferred_element_type=F32))
            dvn_wins.append(jnp.concatenate(slabs, axis=1))
        dvn = jnp.concatenate(dvn_wins, axis=0) if nw > 1 else dvn_wins[0]

        dgs_ref[...] += jnp.sum(dvn * xh, axis=0, keepdims=True)
        dbsg_ref[...] += jnp.sum(dvn, axis=0, keepdims=True)
        dxh = dvn * gs_v
        dvv = rln * (dxh - jnp.mean(dxh, axis=-1, keepdims=True)
                     - xh * jnp.mean(dxh * xh, axis=-1, keepdims=True))
        duv_ref[:, :SGU_W] = (du * guv[:, :SGU_W]).astype(BF16)
        duv_ref[:, SGU_W:] = (dvv * guv[:, SGU_W:]).astype(BF16)

        @pl.when(step == pl.num_programs(0) - 1)
        def _():
            for g in range(GROUPS):
                dws_ref[g] = dws_ref[g] * mask_ref[...]
            dbs_ref[...] = _split3_dot(dbs_acc[...], eg_ref[...])

    rows_out = [((s_len, D_MODEL), BF16, _row_spec(tm, D_MODEL)),
                ((s_len, D_MODEL), BF16, _row_spec(tm, D_MODEL)),
                ((s_len, D_MODEL), BF16, _row_spec(tm, D_MODEL)),
                ((s_len, 2 * D_MODEL), BF16, _row_spec(tm, 2 * D_MODEL)),
                ((nt, FOX_W, tm), BF16, _tile_spec(FOX_W, tm)),
                ((nt, HEADS, tm), F32, _tile_spec(HEADS, tm)),
                ((s_len, 2 * SGU_W), BF16, _row_spec(tm, 2 * SGU_W))]
    acc_out = [((GROUPS, WINDOW, WINDOW), F32), ((WINDOW, 128), F32), ((1, SGU_W), F32),
               ((1, SGU_W), F32), ((1, D_MODEL), F32)]
    return pl.pallas_call(
        body, name="mix_bwd", grid=(nt,),
        in_specs=[_row_spec(tm, D_MODEL), _row_spec(tm, D_MODEL), _row_spec(tm, D_MODEL),
                  _row_spec(tm, D_MODEL), _row_spec(tm, 2 * D_MODEL), _row_spec(tm, 2 * SGU_W),
                  _row_spec(tm, FOX_W), _const_spec(wout.shape), _const_spec(wa.shape),
                  _const_spec(wb.shape), _const_spec(wsm.shape), _const_spec(wsmt.shape),
                  _const_spec(bsf.shape), _const_spec((1, SGU_W)), _const_spec((1, SGU_W)),
                  _const_spec((1, D_MODEL)), _const_spec(wmask.shape), _const_spec(egrp.shape)],
        out_specs=[o[2] for o in rows_out] + [_const_spec(s) for s, _ in acc_out],
        out_shape=[jax.ShapeDtypeStruct(o[0], o[1]) for o in rows_out]
        + [jax.ShapeDtypeStruct(s, dt) for s, dt in acc_out],
        scratch_shapes=[pltpu.VMEM((WINDOW, SGU_W), F32)],
        compiler_params=_params(48, 1),
    )(dx1, om, ya, yb, gpre, uvpre, attn, wout, wa, wb, wsm, wsmt, bsf, gsgu, bsgu, gpost, wmask,
      egrp)


def _attn_bwd(qa, ka, kat, vs, dot_, lse, delta, ecol):
    s_len = qa.shape[0]
    t = ATTN_TILE
    nb = s_len // t

    def body(k_ref, kt_ref, vs_ref, q_ref, do_ref, lse_ref, dl_ref, ec_ref, gk_ref, dvt_ref,
             gqt_ref, csum_ref, p_sc, ds_sc):
        j = pl.program_id(0)

        @pl.when(j == 0)
        def _():
            gqt_ref[...] = jnp.zeros_like(gqt_ref)

        gk_ref[...] = jnp.zeros_like(gk_ref)
        dvt_ref[...] = jnp.zeros_like(dvt_ref)

        def probs(i, slot, masked):
            qrows = pl.ds(pl.multiple_of(i * t, t), t)
            if masked:
                keep = (lax.broadcasted_iota(jnp.int32, (t, t), 0)
                        <= lax.broadcasted_iota(jnp.int32, (t, t), 1))
            for hd in range(HEADS):
                sl = slice(hd * 128, (hd + 1) * 128)
                hr = slice(hd * HEAD_DIM, (hd + 1) * HEAD_DIM)
                st = _dot_nt(k_ref[:, sl], q_ref[qrows, sl])
                if masked:
                    st = jnp.where(keep, st, -jnp.inf)
                pt = jnp.exp2(st - lse_ref[i, hd:hd + 1, :])
                dpt = jnp.dot(vs_ref[:, hd * 128:hd * 128 + HEAD_DIM], do_ref[i, hr, :],
                              preferred_element_type=F32)
                p_sc[slot, hd] = pt.astype(BF16)
                ds_sc[slot, hd] = (pt * (dpt - dl_ref[i, hd:hd + 1, :])).astype(BF16)

        def grads(i, slot):
            qrows = pl.ds(pl.multiple_of(i * t, t), t)
            for hd in range(HEADS):
                sl = slice(hd * 128, (hd + 1) * 128)
                hr = slice(hd * HEAD_DIM, (hd + 1) * HEAD_DIM)
                dst = ds_sc[slot, hd]
                dvt_ref[0, hr, :] += _dot_nt(do_ref[i, hr, :], p_sc[slot, hd])
                gk_ref[:, sl] += jnp.dot(dst, q_ref[qrows, sl], preferred_element_type=F32)
                gqt_ref[i, hd * QT_ROWS:(hd + 1) * QT_ROWS, :] += jnp.dot(
                    kt_ref[0, hd * 128:hd * 128 + QT_ROWS, :], dst, preferred_element_type=F32)

        probs(j, 0, True)
        pairs = (nb - 1 - j) // 2

        def two_blocks(p, carry):
            i1 = j + 1 + 2 * p
            probs(i1, 1, False)
            grads(i1 - 1, 0)
            probs(i1 + 1, 0, False)
            grads(i1, 1)
            return carry

        lax.fori_loop(0, pairs, two_blocks, 0)

        @pl.when(nb - 1 - j - 2 * pairs == 0)
        def _():
            grads(nb - 1, 0)

        @pl.when(nb - 1 - j - 2 * pairs == 1)
        def _():
            probs(nb - 1, 1, False)
            grads(nb - 2, 0)
            grads(nb - 1, 1)

        csum_ref[...] = _split3_dot(gk_ref[...], ec_ref[...])

    return pl.pallas_call(
        body, name="attn_bwd", grid=(nb,),
        in_specs=[_row_spec(t, SLAB_W), _tile_spec(SLAB_W, t), _row_spec(t, SLAB_W),
                  _const_spec(qa.shape), _const_spec(dot_.shape), _const_spec(lse.shape),
                  _const_spec(delta.shape), _const_spec(ecol.shape)],
        out_specs=[_row_spec(t, SLAB_W), _tile_spec(FOX_W, t),
                   _const_spec((nb, HEADS * QT_ROWS, t)), _row_spec(t, 128)],
        out_shape=[jax.ShapeDtypeStruct((s_len, SLAB_W), F32),
                   jax.ShapeDtypeStruct((nb, FOX_W, t), F32),
                   jax.ShapeDtypeStruct((nb, HEADS * QT_ROWS, t), F32),
                   jax.ShapeDtypeStruct((s_len, 128), F32)],
        scratch_shapes=[pltpu.VMEM((2, HEADS, t, t), BF16), pltpu.VMEM((2, HEADS, t, t), BF16)],
        compiler_params=_params(60, 1),
    )(ka, kat, vs, qa, dot_, lse, delta, ecol)


def _rev_cumsum(col_sums, gqt, triu):
    s_len = col_sums.shape[0]
    tm = TOKEN_TILE
    n = s_len // tm

    def body(cs_ref, gqt_ref, tri_ref, o_ref, carry):
        @pl.when(pl.program_id(0) == 0)
        def _():
            carry[...] = jnp.zeros_like(carry)
        rows = [gqt_ref[0, hd * QT_ROWS + HEAD_DIM:hd * QT_ROWS + HEAD_DIM + 1, :]
                for hd in range(HEADS)]
        row_sums = jnp.concatenate(rows + [jnp.zeros((128 - HEADS, tm), F32)], axis=0).T
        out = _tri_dot(tri_ref[...], row_sums - cs_ref[...]) + carry[...]
        o_ref[...] = out
        carry[...] = out[0:1, :]

    return pl.pallas_call(
        body, name="rev_cumsum", grid=(n,),
        in_specs=[pl.BlockSpec((tm, 128), lambda i: (n - 1 - i, 0)),
                  pl.BlockSpec((1, HEADS * QT_ROWS, tm), lambda i: (n - 1 - i, 0, 0)),
                  _const_spec((tm, tm))],
        out_specs=pl.BlockSpec((tm, 128), lambda i: (n - 1 - i, 0)),
        out_shape=jax.ShapeDtypeStruct((s_len, 128), F32),
        scratch_shapes=[pltpu.VMEM((1, 128), F32)],
        compiler_params=_params(32, 1),
    )(col_sums, gqt, triu)


def _heads_from_slabs(slabs):
    lane = lax.broadcasted_iota(jnp.int32, slabs[0].shape, 1)
    low = lane < HEAD_DIM
    pairs = [jnp.where(low, slabs[2 * p], pltpu.roll(slabs[2 * p + 1], HEAD_DIM, 1))
             for p in range(HEADS // 2)]
    return jnp.concatenate(pairs, axis=1)


def _proj_bwd(gqt, gk, dvt, dlogf, flog, qraw, kraw, duv, dgp, x, dx1, wcat, bdiag, gq, gk_gain, g1,
              efold):
    s_len = x.shape[0]
    tm = TOKEN_TILE

    def body(gqt_ref, gkk_ref, dvt_ref, dlf_ref, flog_ref, qr_ref, kr_ref, duv_ref, dgp_ref, x_ref,
             dx1_ref, w_ref, bd_ref, gq_ref, gk_ref, g1_ref, ef_ref,
             dx_ref, dprojt_ref, dgq_ref, dgk_ref, dbf_ref, dg1_ref, gq_acc, gk_acc, dproj_ref):
        step = pl.program_id(0)

        @pl.when(step == 0)
        def _():
            gq_acc[...] = jnp.zeros_like(gq_acc)
            gk_acc[...] = jnp.zeros_like(gk_acc)
            dbf_ref[...] = jnp.zeros_like(dbf_ref)
            dg1_ref[...] = jnp.zeros_like(dg1_ref)

        pad = jnp.zeros((128 - QT_ROWS, tm), F32)
        q_slabs = [jnp.concatenate([gqt_ref[0, hd * QT_ROWS:(hd + 1) * QT_ROWS, :], pad], axis=0).T
                   for hd in range(HEADS)]
        dqn = _heads_from_slabs(q_slabs)
        dkn = _heads_from_slabs([gkk_ref[:, hd * 128:(hd + 1) * 128] for hd in range(HEADS)])

        def head_bwd(raw_ref, dn, g_ref, acc):
            raw = raw_ref[...].astype(F32)
            r = lax.rsqrt(_seg_mean(raw * raw, bd_ref) + EPS)
            xhat = raw * r
            acc[0:1, :] += jnp.sum(dn * xhat, axis=0, keepdims=True)
            dyg = dn * g_ref[...]
            return r * (dyg - xhat * _seg_mean(dyg * xhat, bd_ref))

        dproj_ref[:, C_Q:C_K] = head_bwd(qr_ref, dqn * HEAD_DIM ** -0.5, gq_ref, gq_acc).astype(BF16)
        dproj_ref[:, C_K:C_V] = head_bwd(kr_ref, dkn * LN2, gk_ref, gk_acc).astype(BF16)
        dproj_ref[:, C_V:C_UV] = dvt_ref[0].T.astype(BF16)
        dfl = dlf_ref[...] * _sigmoid(-flog_ref[...])
        dbf_ref[...] += jnp.sum(dfl, axis=0, keepdims=True)
        dproj_ref[:, C_F:C_END] = dfl.astype(BF16)
        dproj_ref[:, C_UV:C_G] = duv_ref[...]
        dproj_ref[:, C_G:C_F] = dgp_ref[...]

        dproj = dproj_ref[...]
        dprojt_ref[...] = dproj.astype(F32).T.astype(BF16)
        dh = jnp.dot(dproj, w_ref[...], preferred_element_type=F32)
        xf = x_ref[...]
        r = lax.rsqrt(jnp.mean(xf * xf, axis=-1, keepdims=True) + EPS)
        dg1_ref[...] += jnp.sum(dh * xf * r, axis=0, keepdims=True)
        dx_ref[...] = dx1_ref[...] + _rms_bwd(xf, r, g1_ref[...], dh)

        @pl.when(step == pl.num_programs(0) - 1)
        def _():
            dgq_ref[...] = _split3_dot(gq_acc[...], ef_ref[...])
            dgk_ref[...] = _split3_dot(gk_acc[...], ef_ref[...])

    outs = [((s_len, D_MODEL), F32, _row_spec(tm, D_MODEL)),
            ((C_END, s_len), BF16, pl.BlockSpec((C_END, tm), lambda i: (0, i))),
            ((8, 128), F32, _const_spec((8, 128))),
            ((8, 128), F32, _const_spec((8, 128))),
            ((1, 128), F32, _const_spec((1, 128))),
            ((1, D_MODEL), F32, _const_spec((1, D_MODEL)))]
    return pl.pallas_call(
        body, name="proj_bwd", grid=(s_len // tm,),
        in_specs=[_tile_spec(HEADS * QT_ROWS, tm), _row_spec(tm, SLAB_W), _tile_spec(FOX_W, tm),
                  _row_spec(tm, 128), _row_spec(tm, 128), _row_spec(tm, FOX_W),
                  _row_spec(tm, FOX_W), _row_spec(tm, 2 * SGU_W), _row_spec(tm, 2 * D_MODEL),
                  _row_spec(tm, D_MODEL), _row_spec(tm, D_MODEL), _const_spec(wcat.shape),
                  _const_spec(bdiag.shape), _const_spec((1, FOX_W)), _const_spec((1, FOX_W)),
                  _const_spec((1, D_MODEL)), _const_spec(efold.shape)],
        out_specs=[o[2] for o in outs],
        out_shape=[jax.ShapeDtypeStruct(o[0], o[1]) for o in outs],
        scratch_shapes=[pltpu.VMEM((8, FOX_W), F32), pltpu.VMEM((8, FOX_W), F32),
                        pltpu.VMEM((tm, C_END), BF16)],
        compiler_params=_params(56, 1),
    )(gqt, gk, dvt, dlogf, flog, qraw, kraw, duv, dgp, x, dx1, wcat, bdiag, gq, gk_gain, g1, efold)


def _dw_matmul(at, b, tm, name, after=()):
    m, s_len = at.shape
    n = b.shape[1]

    def body(a_ref, b_ref, *rest):
        rest[-1][...] = jnp.dot(a_ref[...], b_ref[...], preferred_element_type=F32).astype(BF16)

    return pl.pallas_call(
        body, name=name, grid=(m // tm,),
        in_specs=[pl.BlockSpec((tm, s_len), lambda i: (i, 0)), _const_spec(b.shape)]
        + [pl.BlockSpec(memory_space=pl.ANY)] * len(after),
        out_specs=pl.BlockSpec((tm, n), lambda i: (i, 0)),
        out_shape=jax.ShapeDtypeStruct((m, n), BF16),
        compiler_params=_params(48, 1),
    )(at, b, *after)


def _adamw(parts, w, m, v, tr, name, col_tile=None, select=None):
    parts = parts if isinstance(parts, (list, tuple)) else [parts]
    rows, cols = w.shape
    extra = [] if select is None else [select]
    bc1 = 1.0 - ADAM_B1 ** ADAM_STEP
    bc2 = 1.0 - ADAM_B2 ** ADAM_STEP

    def body(*refs):
        p_refs = refs[:len(parts)]
        sel_refs = refs[len(parts):len(parts) + len(extra)]
        w_ref, m_ref, v_ref, g_ref, d_ref, mo_ref, vo_ref = refs[len(parts) + len(extra):]
        g = None
        for p_ref, p in zip(p_refs, parts):
            for idx in range(p.shape[0]):
                term = p_ref[idx].astype(F32)
                g = term if g is None else g + term
        if sel_refs:
            g = _tri_dot(sel_refs[0][...], g)
        g_ref[...] = g
        mn = ADAM_B1 * m_ref[...] + (1.0 - ADAM_B1) * g
        vn = ADAM_B2 * v_ref[...] + (1.0 - ADAM_B2) * (g * g)
        mo_ref[...] = mn
        vo_ref[...] = vn
        m_hat = mn / bc1
        v_hat = vn / bc2
        d_ref[...] = -ADAM_LR * (m_hat / (jnp.sqrt(v_hat) + ADAM_EPS) + ADAM_WD * w_ref[...])

    if col_tile is None:
        spec = pl.BlockSpec((tr, cols), lambda i: (i, 0))
        pspecs = [pl.BlockSpec((p.shape[0], tr, cols), lambda i: (0, i, 0)) for p in parts]
        steps = rows // tr
    else:
        spec = pl.BlockSpec((rows, col_tile), lambda i: (0, i))
        pspecs = [pl.BlockSpec((p.shape[0], p.shape[1], col_tile), lambda i: (0, 0, i))
                  for p in parts]
        steps = cols // col_tile
    return pl.pallas_call(
        body, name=name, grid=(steps,),
        in_specs=pspecs + [_const_spec(e.shape) for e in extra] + [spec, spec, spec],
        out_specs=[spec] * 4,
        out_shape=[jax.ShapeDtypeStruct((rows, cols), F32)] * 4,
        compiler_params=_params(48, 1),
    )(*parts, *extra, w, m, v)


def _sum_parts(parts, name):
    n, rows, cols = parts.shape

    def body(p_ref, o_ref):
        g = p_ref[0]
        for idx in range(1, n):
            g = g + p_ref[idx]
        o_ref[...] = g

    return pl.pallas_call(
        body, name=name, out_shape=jax.ShapeDtypeStruct((rows, cols), F32),
        in_specs=[_const_spec(parts.shape)], out_specs=_const_spec((rows, cols)), grid=(1,),
        compiler_params=_params(16, 1),
    )(parts)


VEC_NAMES = ("g_pre_mix", "b_forget", "g_q", "g_k", "g_sgu", "b_sgu", "b_spatial", "g_post_mix",
             "g_pre_ffn", "g_post_ffn")
VEC_ROWS = 16
LOSS_ROW = len(VEC_NAMES)


def _pack_vectors(d, loss_row):
    rows = []
    for k in VEC_NAMES:
        flat = d[k].reshape(1, -1).astype(F32)
        rows.append(jnp.pad(flat, ((0, 0), (0, 1024 - flat.shape[1]))))
    rows.append(loss_row)
    rows.append(jnp.zeros((VEC_ROWS - len(rows), 1024), F32))
    return jnp.concatenate(rows, axis=0)


def _adamw_vectors(grad_rows, ws, ms, vs):
    n = len(VEC_NAMES)
    bc1 = 1.0 - ADAM_B1 ** ADAM_STEP
    bc2 = 1.0 - ADAM_B2 ** ADAM_STEP

    def step(g, w, m, v):
        mn = ADAM_B1 * m + (1.0 - ADAM_B1) * g
        vn = ADAM_B2 * v + (1.0 - ADAM_B2) * (g * g)
        delta = -ADAM_LR * ((mn / bc1) / (jnp.sqrt(vn / bc2) + ADAM_EPS) + ADAM_WD * w)
        return g, delta, mn, vn

    def body(*refs):
        g_ref = refs[0]
        ins = [refs[1 + j * n:1 + (j + 1) * n] for j in range(3)]
        outs = [refs[1 + (3 + j) * n:1 + (4 + j) * n] for j in range(4)]
        for i in range(n):
            shape = ws[i].shape
            if len(shape) == 2:
                res = step(g_ref[i:i + 1, :shape[1]], *[r[i][...] for r in ins])
                for o, val in zip(outs, res):
                    o[i][...] = val
            else:
                for r in range(shape[1]):
                    res = step(g_ref[i:i + 1, r * shape[2]:(r + 1) * shape[2]],
                               *[q[i][0, r:r + 1, :] for q in ins])
                    for o, val in zip(outs, res):
                        o[i][0, r:r + 1, :] = val

    vmem = pl.BlockSpec(memory_space=pltpu.VMEM)
    flat = pl.pallas_call(
        body, name="adamw_vectors",
        in_specs=[vmem] * (1 + 3 * n), out_specs=[vmem] * (4 * n),
        out_shape=[jax.ShapeDtypeStruct(w.shape, F32) for _ in range(4) for w in ws],
    )(grad_rows, *ws, *ms, *vs)
    return [flat[j * n:(j + 1) * n] for j in range(4)]


def _cols_to_blocks(full, width):
    r = full.shape[0]
    return jnp.transpose(full.reshape(r, N_DEV, width), (1, 0, 2))


def _blocks_to_cols(blocks):
    n, r, width = blocks.shape
    return jnp.transpose(blocks, (1, 0, 2)).reshape(r, n * width)


def kernel(x, g_pre_mix, w_in, b_forget, g_q, g_k, g_sgu, b_sgu, w_spatial, b_spatial, w_branch_a, w_branch_b, w_out, g_post_mix, g_pre_ffn, w_ffn_in, w_ffn_down, g_post_ffn, loss_target, m_g_pre_mix, m_w_in, m_b_forget, m_g_q, m_g_k, m_g_sgu, m_b_sgu, m_w_spatial, m_b_spatial, m_w_branch_a, m_w_branch_b, m_w_out, m_g_post_mix, m_g_pre_ffn, m_w_ffn_in, m_w_ffn_down, m_g_post_ffn, v_g_pre_mix, v_w_in, v_b_forget, v_g_q, v_g_k, v_g_sgu, v_b_sgu, v_w_spatial, v_b_spatial, v_w_branch_a, v_w_branch_b, v_w_out, v_g_post_mix, v_g_pre_ffn, v_w_ffn_in, v_w_ffn_down, v_g_post_ffn):
    big_names = ("w_in", "w_branch_a", "w_branch_b", "w_out", "w_ffn_in", "w_ffn_down")
    weights = dict(g_pre_mix=g_pre_mix, w_in=w_in, b_forget=b_forget, g_q=g_q, g_k=g_k, g_sgu=g_sgu,
                   b_sgu=b_sgu, w_spatial=w_spatial, b_spatial=b_spatial, w_branch_a=w_branch_a,
                   w_branch_b=w_branch_b, w_out=w_out, g_post_mix=g_post_mix, g_pre_ffn=g_pre_ffn,
                   w_ffn_in=w_ffn_in, w_ffn_down=w_ffn_down, g_post_ffn=g_post_ffn)
    mom1 = dict(g_pre_mix=m_g_pre_mix, w_in=m_w_in, b_forget=m_b_forget, g_q=m_g_q, g_k=m_g_k,
                g_sgu=m_g_sgu, b_sgu=m_b_sgu, w_spatial=m_w_spatial, b_spatial=m_b_spatial,
                w_branch_a=m_w_branch_a, w_branch_b=m_w_branch_b, w_out=m_w_out,
                g_post_mix=m_g_post_mix, g_pre_ffn=m_g_pre_ffn, w_ffn_in=m_w_ffn_in,
                w_ffn_down=m_w_ffn_down, g_post_ffn=m_g_post_ffn)
    mom2 = dict(g_pre_mix=v_g_pre_mix, w_in=v_w_in, b_forget=v_b_forget, g_q=v_g_q, g_k=v_g_k,
                g_sgu=v_g_sgu, b_sgu=v_b_sgu, w_spatial=v_w_spatial, b_spatial=v_b_spatial,
                w_branch_a=v_w_branch_a, w_branch_b=v_w_branch_b, w_out=v_w_out,
                g_post_mix=v_g_post_mix, g_pre_ffn=v_g_pre_ffn, w_ffn_in=v_w_ffn_in,
                w_ffn_down=v_w_ffn_down, g_post_ffn=v_g_post_ffn)
    names = list(weights)
    shapes = {k: weights[k].shape for k in names}

    s_len = x.shape[1]
    xs = x.reshape(s_len, D_MODEL)
    tgt = loss_target.reshape(s_len, D_MODEL)

    transposed = ("w_in", "w_ffn_in")

    def local_view(a, k):
        return jnp.transpose(a[0]) if k in transposed else a[0]

    shards = {k: local_view(weights[k], k).astype(BF16) for k in big_names}

    x_pos, y_pos, c_pos = _mesh_pos()
    me = 4 * x_pos + 2 * y_pos + c_pos
    r_idx = jnp.arange(BLK)
    general = (jnp.asarray(BLK_AT, jnp.int32) - jnp.asarray(FRAME_START, jnp.int32))[me] + r_idx
    holder = jnp.where(r_idx < F_AT, BLK_AT[F_DEV] - FRAME_START[F_DEV] + r_idx,
                       jnp.where(r_idx < F_AT + HEADS, FRAME - F_AT + r_idx,
                                 BLK_AT[F_DEV] - FRAME_START[F_DEV] - HEADS + r_idx))
    frame_row = jnp.where(me == F_DEV, holder, general)
    in_frame = (frame_row[:, None] == jnp.arange(FRAME_ROWS)[None, :]).astype(BF16)
    my_frame = jnp.dot(in_frame.T, shards["w_in"], preferred_element_type=F32).astype(BF16)
    wcat = _assemble_w_in(_gather_two_level(my_frame, "gather_w_in"))
    wcat, later = lax.optimization_barrier(
        (wcat, [shards[k] for k in big_names if k != "w_in"]))
    shards.update(zip([k for k in big_names if k != "w_in"], later))
    (gat_mix, gat_ffn), gat_token = _exchange_start(
        [[shards["w_branch_a"], shards["w_branch_b"], shards["w_out"]],
         [shards["w_ffn_in"], shards["w_ffn_down"]]], "gather_start", gather=True)

    seg = np.arange(FOX_W) // HEAD_DIM
    bdiag = jnp.asarray(seg[:128, None] == seg[None, :128], BF16)
    tm = TOKEN_TILE
    lower = np.arange(tm)[None, :] <= np.arange(tm)[:, None]
    tril = jnp.asarray(lower, BF16)
    triu = jnp.asarray(lower.T, BF16)
    egrp = jnp.asarray(seg[:, None] == np.arange(128)[None, :], BF16)
    efold = jnp.asarray((np.arange(FOX_W) % HEAD_DIM)[:, None] == np.arange(128)[None, :], BF16)
    gq512 = jnp.tile(g_q.reshape(1, HEAD_DIM), (1, HEADS))
    gk512 = jnp.tile(g_k.reshape(1, HEAD_DIM), (1, HEADS))
    bfor = jnp.pad(b_forget.reshape(1, HEADS), ((0, 0), (0, 128 - HEADS)))
    pos = np.arange(WINDOW)
    wmask = (pos[None, :] // CHUNK) <= (pos[:, None] // CHUNK)
    wsm_f = jnp.where(jnp.asarray(wmask)[None], w_spatial[0], 0.0)
    wsm = wsm_f.astype(BF16)
    wsmt = jnp.transpose(wsm_f, (0, 2, 1)).astype(BF16)
    bsf = jnp.repeat(jnp.transpose(b_spatial[0]), HEAD_DIM, axis=1)
    wmask_f = jnp.asarray(wmask, F32)

    col = np.arange(SLAB_W)
    row128 = np.arange(128)

    def d_place(first, sign):
        parts = [(col[None, :] // 128 == row128[:, None]) & (col[None, :] % 128 == first + a)
                 for a in range(3)]
        return jnp.asarray(sign * np.concatenate(parts, axis=0).astype(np.float32), BF16)

    pdq, pdk = d_place(HEAD_DIM, 1.0), d_place(HEAD_DIM + 3, -1.0)
    ones_q = jnp.asarray((col % 128 >= HEAD_DIM + 3) & (col % 128 < HEAD_DIM + 6), F32)[None]
    ones_k = jnp.asarray((col % 128 >= HEAD_DIM) & (col % 128 < HEAD_DIM + 3), F32)[None]
    ecol = jnp.asarray((col[:, None] // 128 == row128[None, :])
                       & (col[:, None] % 128 == HEAD_DIM + 3), BF16)

    (h, qa, ka, kat, vs, vt, qraw, kraw, flog, uvpre, gpre) = _proj_fwd(
        xs, g_pre_mix + gat_token[0:1, 0:1], wcat, bdiag, gq512, gk512, bfor, tril, pdq, pdk,
        ones_q, ones_k)
    attn, attn_t, lse = _attn_fwd(qa, ka, vt)
    (own_a, own_b, own_out), (zone_a, zone_b, zone_out) = _exchange_wait(
        gat_mix, attn, "gather_wait_mix", gather=True)
    wa = _blocks_to_cols(_own_block(zone_a, own_a))
    wb = _blocks_to_cols(_own_block(zone_b, own_b))
    wout = _own_block(zone_out, own_out).reshape(D_MODEL, D_MODEL)
    sgu_t, ya, yb, merged_t, om, x1 = _mix_fwd(attn, uvpre, gpre, xs, wa, wb, wout, wsm, bsf,
                                           g_sgu, b_sgu, g_post_mix)
    (own_ffn, own_down), (zone_ffn, zone_down) = _exchange_wait(
        gat_ffn, x1, "gather_wait_ffn", gather=True)
    wffn = _own_block(zone_ffn, own_ffn).reshape(2 * D_FF, D_MODEL)
    wdown = _own_block(zone_down, own_down).reshape(D_FF, D_MODEL)
    (dx1, h2, act_t, dff, dgu_t, loss_acc, dg_post_ffn, dg_pre_ffn) = _ffn_fwd_bwd(
        x1, tgt, wffn, wdown, g_pre_ffn, g_post_ffn)

    dw_down = _dw_matmul(act_t, dff, D_FF // 4, "dw_down")
    dw_ffn = _dw_matmul(dgu_t, h2, 2 * D_FF // N_DEV, "dw_ffn_in")
    def own_of(parts):
        return [lax.dynamic_index_in_dim(p, me, 0, keepdims=False) for p in parts]

    parts_ffn = [dw_ffn.reshape(N_DEV, 2 * D_FF // N_DEV, D_MODEL),
                 dw_down.reshape(N_DEV, D_FF // N_DEV, D_MODEL)]
    mine_ffn = own_of(parts_ffn)
    (sct_ffn,), sct_ffn_token = _exchange_start([parts_ffn], "scatter_start_ffn", gather=False,
                                                after=mine_ffn)

    (dom, dya, dyb, dgp, dot_, delta, duv, dws, dbs, dg_sgu, db_sgu, dg_post_mix) = _mix_bwd(
        dx1, om, ya, yb, gpre, uvpre, attn, wout, wa, wb, wsm, wsmt, bsf, g_sgu, b_sgu,
        g_post_mix + sct_ffn_token[0:1, 0:1], wmask_f, egrp)
    dw_out = _dw_matmul(merged_t, dom, 512, "dw_out")
    dw_a = _dw_matmul(attn_t, dya, 512, "dw_a")
    dw_b = _dw_matmul(sgu_t, dyb, 512, "dw_b")
    parts_mix = [_cols_to_blocks(dw_a, D_MODEL // N_DEV), _cols_to_blocks(dw_b, D_MODEL // N_DEV),
                 dw_out.reshape(N_DEV, D_MODEL // N_DEV, D_MODEL)]
    mine_mix = own_of(parts_mix)
    (sct_mix,), sct_mix_token = _exchange_start([parts_mix], "scatter_start_mix", gather=False,
                                                after=mine_mix)

    gk_all, dvt, gqt, col_sums = _attn_bwd(qa, ka, kat, vs, dot_, lse,
                                           delta + sct_mix_token[0, 0], ecol)
    dlogf = _rev_cumsum(col_sums, gqt, triu)
    dx, dproj_t, dgq, dgk, dbf, dg_pre_mix = _proj_bwd(
        gqt, gk_all, dvt, dlogf, flog, qraw, kraw, duv, dgp, xs, dx1, wcat, bdiag, gq512, gk512,
        g_pre_mix, efold)

    small_local = dict(
        g_pre_mix=dg_pre_mix, b_forget=dbf[:, :HEADS], g_q=dgq[0:1, :HEAD_DIM],
        g_k=dgk[0:1, :HEAD_DIM], g_sgu=dg_sgu, b_sgu=db_sgu, w_spatial=dws,
        b_spatial=jnp.transpose(dbs[:, :GROUPS]), g_post_mix=dg_post_mix, g_pre_ffn=dg_pre_ffn,
        g_post_ffn=dg_post_ffn)
    loss_row = jnp.pad(loss_acc[0:1, 0:1], ((0, 0), (0, 1023)))
    small_parts = [_pack_vectors(small_local, loss_row).reshape(N_DEV, VEC_ROWS // N_DEV, 1024),
                   dws]

    def with_own(zones, own_blocks):
        return [_own_block(z, b) for z, b in zip(zones, own_blocks)]

    mine_small = own_of(small_parts)
    (sct_small,), sct_small_token = _exchange_start([small_parts], "scatter_start_small",
                                                    gather=False, after=mine_small)
    dw_cat = _dw_matmul(dproj_t, h, C_END // N_DEV, "dw_in", after=(sct_small_token,))
    recv_vec, recv_ws = with_own(
        _exchange_wait(sct_small, dw_cat, "scatter_wait_small", gather=False)[1], mine_small)
    small_sums = [_sum_parts(recv_vec, "sum_vectors"), _sum_parts(recv_ws, "sum_w_spatial")]
    (gat_small,), gat_small_token = _exchange_start([small_sums], "gather_start_small",
                                                    gather=True)
    pair_blocks, own_pair = _pair_sums(dw_cat, "pair_sums_in", gat_small_token)
    rs_in, rs_token = _chip_exchange_start(pair_blocks, "chip_exchange_start_in")

    recv_ffn, recv_down = with_own(
        _exchange_wait(sct_ffn, rs_token, "scatter_wait_ffn", gather=False)[1], mine_ffn)
    recv_a, recv_b, recv_out = with_own(
        _exchange_wait(sct_mix, recv_ffn, "scatter_wait_mix", gather=False)[1], mine_mix)
    received = [None, recv_a, recv_b, recv_out, recv_ffn, recv_down]

    grads, deltas, new_m, new_v = {}, {}, {}, {}
    row_tiles = {"w_in": None, "w_branch_a": 512, "w_branch_b": 512, "w_out": 128, "w_ffn_in": 176,
                 "w_ffn_down": 352}

    def update(k, parts):
        outs = _adamw(parts, local_view(weights[k], k), local_view(mom1[k], k),
                      local_view(mom2[k], k), row_tiles[k], "adamw_" + k,
                      col_tile=256 if k == "w_in" else None,
                      select=in_frame if k == "w_in" else None)
        if k in transposed:
            outs = [jnp.transpose(o) for o in outs]
        grads[k], deltas[k], new_m[k], new_v[k] = [o[None] for o in outs]
        return outs[0]

    last = None
    for idx, k in enumerate(big_names):
        if k != "w_in":
            last = update(k, received[idx])

    (own_vec, own_ws), (zone_vec, zone_ws) = _exchange_wait(gat_small, last, "gather_wait_small",
                                                            gather=True)
    vec_all = _own_block(zone_vec, own_vec).reshape(VEC_ROWS, 1024)
    ws_all = _own_block(zone_ws, own_ws).reshape(1, GROUPS * WINDOW, WINDOW)

    def rows_of(d):
        return d["w_spatial"].reshape(GROUPS * WINDOW, WINDOW)

    outs = _adamw(ws_all, rows_of(weights), rows_of(mom1), rows_of(mom2), GROUPS * WINDOW,
                  "adamw_w_spatial")
    for dst, o in zip((grads, deltas, new_m, new_v), outs):
        dst["w_spatial"] = o.reshape(shapes["w_spatial"])
    sg = outs[0]
    vec_outs = _adamw_vectors(vec_all, *[[d[k] for k in VEC_NAMES] for d in (weights, mom1, mom2)])
    for dst, group in zip((grads, deltas, new_m, new_v), vec_outs):
        dst.update(zip(VEC_NAMES, group))
    arrived = _chip_exchange_wait(rs_in, sg, "chip_exchange_wait_in")
    update("w_in", [own_pair[None], arrived])

    loss = vec_all[LOSS_ROW, 0]
    return (loss, dx.reshape(x.shape), *[grads[k] for k in names], *[deltas[k] for k in names],
            *[new_m[k] for k in names], *[new_v[k] for k in names])
```

```python
import functools
import math

import jax
import jax.numpy as jnp
import numpy as np
from jax import lax
from jax.experimental import pallas as pl
from jax.experimental.pallas import tpu as pltpu

F32 = jnp.float32
BF16 = jnp.bfloat16

D_MODEL = 1024
FOX_W = 512
HEADS = 8
HEAD_DIM = 64
SGU_W = 512
GROUPS = 8
WINDOW = 128
CHUNK = 64
D_FF = 2816
IN_COLS = 4616
EPS = 1e-6
N_DEV = 8
LOG2E = 1.4426950408889634
LN2 = 0.6931471805599453

C_Q, C_K, C_V, C_UV, C_G, C_F, C_END = 0, 512, 1024, 1536, 2560, 4608, 4736

ADAM_LR, ADAM_B1, ADAM_B2, ADAM_EPS, ADAM_WD, ADAM_STEP = 0.001, 0.9, 0.999, 1e-08, 0.01, 10

MIB = 1024 * 1024
TOKEN_TILE = 256
ATTN_TILE = 256
SLAB_W = HEADS * 128
QT_ROWS = 72

BLK = IN_COLS // N_DEV
F_LO = 3 * FOX_W
F_DEV = F_LO // BLK
F_AT = F_LO - F_DEV * BLK
BLK_AT = [BLK * j - (HEADS if BLK * j > F_LO else 0) for j in range(N_DEV)]
FRAME_START = [a // 16 * 16 for a in BLK_AT]
FRAME = 608
FRAME_ROWS = FRAME + 16


def _params(vmem_mib, n_axes):
    return pltpu.CompilerParams(
        dimension_semantics=("arbitrary",) * n_axes, vmem_limit_bytes=vmem_mib * MIB)


def _const_spec(shape):
    nd = len(shape)
    return pl.BlockSpec(shape, lambda *_: (0,) * nd)


def _row_spec(tm, cols):
    return pl.BlockSpec((tm, cols), lambda i: (i, 0))


def _tile_spec(rows, tm):
    return pl.BlockSpec((1, rows, tm), lambda i: (i, 0, 0))


def _split3_dot(x, e):
    x1 = x.astype(BF16)
    r1 = x - x1.astype(F32)
    x2 = r1.astype(BF16)
    x3 = (r1 - x2.astype(F32)).astype(BF16)
    dot = functools.partial(jnp.dot, preferred_element_type=F32)
    return dot(x1, e) + dot(x2, e) + dot(x3, e)


def _tri_dot(tri, x):
    x1 = x.astype(BF16)
    r1 = x - x1.astype(F32)
    x2 = r1.astype(BF16)
    x3 = (r1 - x2.astype(F32)).astype(BF16)
    dot = functools.partial(jnp.dot, preferred_element_type=F32)
    return dot(tri, x1) + dot(tri, x2) + dot(tri, x3)


def _seg_mean(sq, bd_ref):
    hi = sq.astype(BF16)
    lo = (sq - hi.astype(F32)).astype(BF16)
    bd = bd_ref[...]
    dot = functools.partial(jnp.dot, preferred_element_type=F32)
    pairs = [dot(hi[:, p * 128:(p + 1) * 128], bd) + dot(lo[:, p * 128:(p + 1) * 128], bd)
             for p in range(HEADS // 2)]
    return jnp.concatenate(pairs, axis=1) * (1.0 / HEAD_DIM)


def _slabs_from_heads(t):
    lane = lax.broadcasted_iota(jnp.int32, (t.shape[0], 128), 1)
    low = lane < HEAD_DIM
    slabs = []
    for p in range(HEADS // 2):
        pair = t[:, p * 128:(p + 1) * 128]
        slabs.append(jnp.where(low, pair, 0.0))
        slabs.append(jnp.where(low, pltpu.roll(pair, HEAD_DIM, 1), 0.0))
    return jnp.concatenate(slabs, axis=1)


def _dot_nt(a, b):
    return lax.dot_general(a, b, (((1,), (1,)), ((), ())), preferred_element_type=F32)


def _dot_tn(a, b):
    return lax.dot_general(a, b, (((0,), (0,)), ((), ())), preferred_element_type=F32)


def _sigmoid(x):
    return 0.5 * jnp.tanh(0.5 * x) + 0.5


_GELU_C = math.sqrt(2.0 / math.pi)


def _gelu_and_grad(x):
    inner = _GELU_C * (x + 0.044715 * x * x * x)
    t = jnp.tanh(inner)
    y = 0.5 * x * (1.0 + t)
    dy = 0.5 * (1.0 + t) + 0.5 * x * (1.0 - t * t) * _GELU_C * (1.0 + 3.0 * 0.044715 * x * x)
    return y, dy


def _rms_bwd(xin, r, g, dy):
    dyg = dy * g
    return r * dyg - xin * (r * r * r) * jnp.mean(dyg * xin, axis=-1, keepdims=True)


def _mesh_pos():
    x, y, c = lax.axis_index("x"), lax.axis_index("y"), lax.axis_index("c")
    return x, y, c


def _peer(k):
    x, y, c = _mesh_pos()
    px = (1 - x) if (k >> 2) & 1 else x
    py = (1 - y) if (k >> 1) & 1 else y
    pc = (1 - c) if k & 1 else c
    return (px, py, pc), 4 * px + 2 * py + pc


def _frame_start(j):
    at = BLK * j - jnp.where(BLK * j > F_LO, HEADS, 0)
    return pl.multiple_of(at // 16 * 16, 16)


def _gather_w_in(frame):
    def body(x_ref, out_ref, zone, send_sems, recv_sems, local_sem):
        x, y, c = _mesh_pos()
        me, sibling = (x, y, c), (x, y, 1 - c)
        chips = [(1 - x, y), (x, 1 - y), (1 - x, 1 - y)]

        def index(px, py, pc):
            return 4 * px + 2 * py + pc

        def copy(k, block, to, src=None):
            return pltpu.make_async_remote_copy(
                src_ref=zone.at[index(*block)] if src is None else src,
                dst_ref=zone.at[index(*block)],
                send_sem=send_sems.at[k], recv_sem=recv_sems.at[k],
                device_id=to, device_id_type=pl.DeviceIdType.MESH)

        def add(block):
            j = index(*block)
            rows = pl.ds(_frame_start(j), FRAME)
            out_ref[rows, :] = (out_ref[rows, :].astype(F32)
                                + zone[j, :FRAME, :].astype(F32)).astype(BF16)
            tail = slice(C_F, C_F + FRAME_ROWS - FRAME)
            forget = zone[j, FRAME:, :].astype(F32) * (j == F_DEV).astype(F32)
            out_ref[tail, :] = (out_ref[tail, :].astype(F32) + forget).astype(BF16)

        mine = pltpu.make_async_copy(x_ref, zone.at[index(*me)], local_sem)
        mine.start()
        first = [copy(1 + j, me, (*chip, c), src=x_ref) for j, chip in enumerate(chips)]
        first.append(copy(0, me, sibling, src=x_ref))
        for cp in first:
            cp.start()
        out_ref[...] = jnp.zeros_like(out_ref)
        mine.wait()
        add(me)
        passed = [copy(4 + j, (*chip, c), sibling) for j, chip in enumerate(chips)]
        for j, chip in enumerate(chips):
            copy(1 + j, (*chip, c), me).wait_recv()
            passed[j].start()
            add((*chip, c))
        copy(0, sibling, me).wait_recv()
        add(sibling)
        for j, chip in enumerate(chips):
            copy(4 + j, (*chip, 1 - c), me).wait_recv()
            add((*chip, 1 - c))
        for cp in first + passed:
            cp.wait_send()

    return pl.pallas_call(
        body, name="gather_w_in", out_shape=jax.ShapeDtypeStruct((C_END, frame.shape[1]), BF16),
        in_specs=[pl.BlockSpec(memory_space=pl.ANY)],
        out_specs=pl.BlockSpec(memory_space=pltpu.VMEM),
        scratch_shapes=[pltpu.VMEM((N_DEV,) + frame.shape, BF16),
                        pltpu.SemaphoreType.DMA((7,)), pltpu.SemaphoreType.DMA((7,)),
                        pltpu.SemaphoreType.DMA],
        compiler_params=pltpu.CompilerParams(vmem_limit_bytes=40 * MIB),
    )(frame)


def _chip_peer(k):
    x, y, c = _mesh_pos()
    px = (1 - x) if (k >> 1) & 1 else x
    py = (1 - y) if k & 1 else y
    return (px, py, c), 2 * px + py


def _pair_sums(dw_cat, name, after):
    rows, cols = FRAME_ROWS, dw_cat.shape[1]
    n_chips = N_DEV // 2

    def pieces(p_ref, j):
        return (p_ref.at[pl.ds(_frame_start(j), FRAME)], p_ref.at[pl.ds(C_F, FRAME_ROWS - FRAME)])

    def body(p_ref, after_ref, send_ref, own_ref, mine_buf, sib_buf, send_sems, recv_sems,
             local_sems):
        x, y, c = _mesh_pos()
        sibling = (x, y, 1 - c)
        copies, local = [], []
        for q in range(n_chips):
            for part, (lo, hi) in enumerate(((0, FRAME), (FRAME, FRAME_ROWS))):
                cp = pltpu.make_async_remote_copy(
                    src_ref=pieces(p_ref, 2 * q + (1 - c))[part], dst_ref=sib_buf.at[q, lo:hi],
                    send_sem=send_sems.at[2 * q + part], recv_sem=recv_sems.at[2 * q + part],
                    device_id=sibling, device_id_type=pl.DeviceIdType.MESH)
                cp.start()
                copies.append(cp)
                lc = pltpu.make_async_copy(pieces(p_ref, 2 * q + c)[part], mine_buf.at[q, lo:hi],
                                           local_sems.at[2 * q + part])
                lc.start()
                local.append(lc)
        for lc in local:
            lc.wait()
        for cp in copies:
            cp.wait_recv()
        for k in range(1, n_chips):
            _, q = _chip_peer(k)
            send_ref[k - 1] = (mine_buf[q].astype(F32) + sib_buf[q].astype(F32)).astype(BF16)
        my_chip = 2 * x + y
        own_ref[...] = mine_buf[my_chip].astype(F32) + sib_buf[my_chip].astype(F32)
        for cp in copies:
            cp.wait_send()

    vmem = pl.BlockSpec(memory_space=pltpu.VMEM)
    return pl.pallas_call(
        body, name=name,
        out_shape=[jax.ShapeDtypeStruct((n_chips - 1, rows, cols), BF16),
                   jax.ShapeDtypeStruct((rows, cols), F32)],
        in_specs=[pl.BlockSpec(memory_space=pl.ANY)] * 2, out_specs=[vmem, vmem],
        scratch_shapes=[pltpu.VMEM((n_chips, rows, cols), BF16),
                        pltpu.VMEM((n_chips, rows, cols), BF16),
                        pltpu.SemaphoreType.DMA((2 * n_chips,)),
                        pltpu.SemaphoreType.DMA((2 * n_chips,)),
                        pltpu.SemaphoreType.DMA((2 * n_chips,))],
        compiler_params=pltpu.CompilerParams(vmem_limit_bytes=40 * MIB),
    )(dw_cat, after)


def _chip_copy(src_ref, land_ref, send_sem, recv_sem, k):
    peer, _ = _chip_peer(k)
    return pltpu.make_async_remote_copy(
        src_ref=src_ref.at[k - 1], dst_ref=land_ref.at[k - 1], send_sem=send_sem, recv_sem=recv_sem,
        device_id=peer, device_id_type=pl.DeviceIdType.MESH)


def _chip_exchange_start(blocks, name):
    hbm = pl.BlockSpec(memory_space=pltpu.HBM)
    sem = pl.BlockSpec(memory_space=pltpu.SEMAPHORE)
    n_peers = blocks.shape[0]

    def body(src_ref, zone_ref, send_sems, recv_sems, src_thru, zone_thru, token):
        for k in range(1, n_peers + 1):
            _chip_copy(src_ref, zone_ref, send_sems.at[k - 1], recv_sems.at[k - 1], k).start()
        token[...] = jnp.zeros_like(token)

    outs = pl.pallas_call(
        body, name=name, in_specs=[hbm, hbm],
        out_shape=[pltpu.SemaphoreType.DMA((n_peers,)), pltpu.SemaphoreType.DMA((n_peers,)),
                   pltpu.HBM(blocks.shape, blocks.dtype), pltpu.HBM(blocks.shape, blocks.dtype),
                   jax.ShapeDtypeStruct((8, 128), F32)],
        out_specs=[sem, sem, hbm, hbm, pl.BlockSpec(memory_space=pltpu.VMEM)],
        input_output_aliases={0: 2, 1: 3},
        compiler_params=pltpu.CompilerParams(
            has_side_effects=pltpu.SideEffectType.DATAFLOW_SIDE_EFFECTING),
    )(pltpu.with_memory_space_constraint(blocks, pltpu.HBM),
      pltpu.with_memory_space_constraint(lax.empty(blocks.shape, blocks.dtype), pltpu.HBM))
    return outs[:4], outs[4]


def _chip_exchange_wait(handle, after, name):
    send_sems, recv_sems, src, zone = handle
    hbm = pl.BlockSpec(memory_space=pltpu.HBM)
    sem = pl.BlockSpec(memory_space=pltpu.SEMAPHORE)

    def body(src_ref, zone_ref, ssem, rsem, after_ref, src_out, zone_out):
        for k in range(1, src.shape[0] + 1):
            cp = _chip_copy(src_ref, zone_ref, ssem.at[k - 1], rsem.at[k - 1], k)
            cp.wait_send()
            cp.wait_recv()

    outs = pl.pallas_call(
        body, name=name,
        in_specs=[hbm, hbm, sem, sem, pl.BlockSpec(memory_space=pl.ANY)],
        out_shape=[pltpu.HBM(src.shape, src.dtype), pltpu.HBM(zone.shape, zone.dtype)],
        out_specs=[hbm, hbm], input_output_aliases={0: 0, 1: 1},
        compiler_params=pltpu.CompilerParams(
            has_side_effects=pltpu.SideEffectType.DATAFLOW_SIDE_EFFECTING),
    )(src, zone, send_sems, recv_sems, after)
    return outs[1]


def _remote_copy(gather, src_ref, land_ref, send_sem, recv_sem, k, receive_side):
    x, y, c = _mesh_pos()
    me = 4 * x + 2 * y + c
    peer, pidx = _peer(k)
    return pltpu.make_async_remote_copy(
        src_ref=src_ref if gather else src_ref.at[pidx],
        dst_ref=land_ref.at[pidx if receive_side else me],
        send_sem=send_sem, recv_sem=recv_sem,
        device_id=peer, device_id_type=pl.DeviceIdType.MESH)


def _exchange_start(groups, name, gather, after=()):
    arrs = [a for g in groups for a in g]
    n, n_groups = len(arrs), len(groups)
    lands = [jax.ShapeDtypeStruct(((N_DEV,) + a.shape) if gather else a.shape, a.dtype)
             for a in arrs]

    def body(*refs):
        srcs, zones = refs[:n], refs[n:2 * n]
        outs_at = 2 * n + len(after)
        sems = refs[outs_at:outs_at + 2 * n_groups]
        token = refs[-1]
        a = 0
        for gi, g in enumerate(groups):
            send_sems, recv_sems = sems[2 * gi], sems[2 * gi + 1]
            for k in range(1, N_DEV):
                for ai in range(len(g)):
                    slot = ai * (N_DEV - 1) + k - 1
                    _remote_copy(gather, srcs[a + ai], zones[a + ai], send_sems.at[slot],
                                 recv_sems.at[slot], k, False).start()
            a += len(g)
        token[...] = jnp.zeros_like(token)

    hbm = pl.BlockSpec(memory_space=pltpu.HBM)
    sem = pl.BlockSpec(memory_space=pltpu.SEMAPHORE)
    sem_shapes = []
    for g in groups:
        sem_shapes += [pltpu.SemaphoreType.DMA((len(g) * (N_DEV - 1),))] * 2
    outs = pl.pallas_call(
        body, name=name,
        in_specs=[hbm] * (2 * n) + [pl.BlockSpec(memory_space=pl.ANY)] * len(after),
        out_shape=sem_shapes + [pltpu.HBM(a.shape, a.dtype) for a in arrs]
        + [pltpu.HBM(z.shape, z.dtype) for z in lands] + [jax.ShapeDtypeStruct((8, 128), F32)],
        out_specs=[sem] * (2 * n_groups) + [hbm] * (2 * n)
        + [pl.BlockSpec(memory_space=pltpu.VMEM)],
        input_output_aliases={i: 2 * n_groups + i for i in range(2 * n)},
        compiler_params=pltpu.CompilerParams(
            has_side_effects=pltpu.SideEffectType.DATAFLOW_SIDE_EFFECTING),
    )(*[pltpu.with_memory_space_constraint(a, pltpu.HBM) for a in arrs],
      *[pltpu.with_memory_space_constraint(lax.empty(z.shape, z.dtype), pltpu.HBM) for z in lands],
      *after)
    sems = outs[:2 * n_groups]
    thru = outs[2 * n_groups:2 * n_groups + n]
    zones = outs[2 * n_groups + n:2 * n_groups + 2 * n]
    handles, a = [], 0
    for gi, g in enumerate(groups):
        handles.append((sems[2 * gi], sems[2 * gi + 1], thru[a:a + len(g)], zones[a:a + len(g)]))
        a += len(g)
    return handles, outs[-1]


def _exchange_wait(handle, after, name, gather):
    send_sems, recv_sems, thru, zones = handle
    n = len(thru)

    def body(*refs):
        srcs, lands = refs[:n], refs[n:2 * n]
        ssem, rsem = refs[2 * n], refs[2 * n + 1]
        for k in range(1, N_DEV):
            for ai in range(n):
                slot = ai * (N_DEV - 1) + k - 1
                cp = _remote_copy(gather, srcs[ai], lands[ai], ssem.at[slot], rsem.at[slot], k, True)
                cp.wait_send()
                cp.wait_recv()

    hbm = pl.BlockSpec(memory_space=pltpu.HBM)
    sem = pl.BlockSpec(memory_space=pltpu.SEMAPHORE)
    outs = pl.pallas_call(
        body, name=name,
        in_specs=[hbm] * (2 * n) + [sem, sem, pl.BlockSpec(memory_space=pl.ANY)],
        out_shape=[pltpu.HBM(a.shape, a.dtype) for a in thru]
        + [pltpu.HBM(z.shape, z.dtype) for z in zones],
        out_specs=[hbm] * (2 * n),
        input_output_aliases={i: i for i in range(2 * n)},
        compiler_params=pltpu.CompilerParams(
            has_side_effects=pltpu.SideEffectType.DATAFLOW_SIDE_EFFECTING),
    )(*thru, *zones, send_sems, recv_sems, after)
    return outs[:n], outs[n:]


def _own_block(zone, block):
    x, y, c = _mesh_pos()
    me = 4 * x + 2 * y + c
    return lax.dynamic_update_slice_in_dim(zone, block[None], me, axis=0)


def _proj_fwd(x, g1, wcat, bdiag, gq, gk, bfor, tri, pdq, pdk, ones_q, ones_k):
    s_len = x.shape[0]
    tm = TOKEN_TILE
    nt = s_len // tm

    def body(x_ref, g1_ref, w_ref, bd_ref, gq_ref, gk_ref, bf_ref, tri_ref, pdq_ref,
             pdk_ref, oq_ref, ok_ref,
             h_ref, qa_ref, ka_ref, kat_ref, vs_ref, vt_ref, qr_ref, kr_ref, flog_ref, uv_ref,
             gp_ref, carry):
        @pl.when(pl.program_id(0) == 0)
        def _():
            carry[...] = jnp.zeros_like(carry)

        xf = x_ref[...]
        r = lax.rsqrt(jnp.mean(xf * xf, axis=-1, keepdims=True) + EPS)
        h = (xf * r * g1_ref[...]).astype(BF16)
        h_ref[...] = h
        dot = functools.partial(jnp.dot, preferred_element_type=F32)

        def proj(lo, hi):
            return _dot_nt(h, w_ref[lo:hi, :])

        flog = proj(C_F, C_END) + bf_ref[...]
        flog_ref[...] = flog
        lane = lax.broadcasted_iota(jnp.int32, flog.shape, 1)
        logf = jnp.minimum(flog, 0.0) - jnp.log(1.0 + jnp.exp(-jnp.abs(flog)))
        logf = jnp.where(lane < HEADS, logf, 0.0)
        dcum = _tri_dot(tri_ref[...], logf) + carry[...]
        carry[...] = dcum[tm - 1:tm, :]
        d2 = dcum * LOG2E
        d2a = d2.astype(BF16)
        rem = d2 - d2a.astype(F32)
        d2b = rem.astype(BF16)
        d2c = (rem - d2b.astype(F32)).astype(BF16)

        q = proj(C_Q, C_K)
        qr_ref[...] = q.astype(BF16)
        rq = lax.rsqrt(_seg_mean(q * q, bd_ref) + EPS)
        qn = q * rq * (gq_ref[...] * (HEAD_DIM ** -0.5 * LOG2E))
        d_parts = jnp.concatenate([d2a, d2b, d2c], axis=1)
        qa = _slabs_from_heads(qn) + dot(d_parts, pdq_ref[...]) + oq_ref[...]
        qa_ref[...] = qa.astype(BF16)

        k = proj(C_K, C_V)
        kr_ref[...] = k.astype(BF16)
        rk = lax.rsqrt(_seg_mean(k * k, bd_ref) + EPS)
        kn = k * rk * gk_ref[...]
        ka = _slabs_from_heads(kn) + dot(d_parts, pdk_ref[...]) + ok_ref[...]
        ka_ref[...] = ka.astype(BF16)
        kat_ref[0] = ka.T.astype(BF16)

        v = proj(C_V, C_UV)
        vs_ref[...] = _slabs_from_heads(v).astype(BF16)
        vt_ref[0] = v.T.astype(BF16)
        uv_ref[...] = proj(C_UV, C_G).astype(BF16)
        gp_ref[...] = proj(C_G, C_F).astype(BF16)

    outs = [((s_len, D_MODEL), BF16, _row_spec(tm, D_MODEL)),
            ((s_len, SLAB_W), BF16, _row_spec(tm, SLAB_W)),
            ((s_len, SLAB_W), BF16, _row_spec(tm, SLAB_W)),
            ((nt, SLAB_W, tm), BF16, _tile_spec(SLAB_W, tm)),
            ((s_len, SLAB_W), BF16, _row_spec(tm, SLAB_W)),
            ((nt, FOX_W, tm), BF16, _tile_spec(FOX_W, tm)),
            ((s_len, FOX_W), BF16, _row_spec(tm, FOX_W)),
            ((s_len, FOX_W), BF16, _row_spec(tm, FOX_W)),
            ((s_len, 128), F32, _row_spec(tm, 128)),
            ((s_len, 2 * SGU_W), BF16, _row_spec(tm, 2 * SGU_W)),
            ((s_len, 2 * D_MODEL), BF16, _row_spec(tm, 2 * D_MODEL))]
    return pl.pallas_call(
        body, name="proj_fwd", grid=(nt,),
        in_specs=[_row_spec(tm, D_MODEL), _const_spec((1, D_MODEL)), _const_spec(wcat.shape),
                  _const_spec(bdiag.shape), _const_spec((1, FOX_W)), _const_spec((1, FOX_W)),
                  _const_spec((1, 128)), _const_spec((tm, tm)), _const_spec(pdq.shape), _const_spec(pdk.shape), _const_spec(ones_q.shape),
                  _const_spec(ones_k.shape)],
        out_specs=[o[2] for o in outs],
        out_shape=[jax.ShapeDtypeStruct(o[0], o[1]) for o in outs],
        scratch_shapes=[pltpu.VMEM((1, 128), F32)],
        compiler_params=_params(56, 1),
    )(x, g1, wcat, bdiag, gq, gk, bfor, tri, pdq, pdk, ones_q, ones_k)


def _attn_fwd(qa, ka, vt):
    s_len = qa.shape[0]
    t = ATTN_TILE
    nb = s_len // t

    def body(q_ref, k_ref, vt_ref, o_ref, ot_ref, lse_ref, m_sc, l_sc, acc_sc, s_sc, mcur_sc,
             alpha_sc):
        i = pl.program_id(0)
        m_sc[...] = jnp.full_like(m_sc, -jnp.inf)
        l_sc[...] = jnp.zeros_like(l_sc)
        acc_sc[...] = jnp.zeros_like(acc_sc)

        def logits(j, slot, masked):
            krows = pl.ds(pl.multiple_of(j * t, t), t)
            if masked:
                keep = (lax.broadcasted_iota(jnp.int32, (t, t), 0)
                        <= lax.broadcasted_iota(jnp.int32, (t, t), 1))
            for hd in range(HEADS):
                sl = slice(hd * 128, (hd + 1) * 128)
                st = _dot_nt(k_ref[krows, sl], q_ref[:, sl])
                if masked:
                    st = jnp.where(keep, st, -jnp.inf)
                s_sc[slot, hd] = st
                m_prev = m_sc[hd:hd + 1, :]
                m_new = jnp.maximum(m_prev, jnp.max(st, axis=0, keepdims=True))
                alpha_sc[slot, hd:hd + 1, :] = jnp.exp2(m_prev - m_new)
                mcur_sc[slot, hd:hd + 1, :] = m_new
                m_sc[hd:hd + 1, :] = m_new

        def accumulate(j, slot):
            for hd in range(HEADS):
                hr = slice(hd * HEAD_DIM, (hd + 1) * HEAD_DIM)
                alpha = alpha_sc[slot, hd:hd + 1, :]
                pt = jnp.exp2(s_sc[slot, hd] - mcur_sc[slot, hd:hd + 1, :])
                l_sc[hd:hd + 1, :] = alpha * l_sc[hd:hd + 1, :] + jnp.sum(pt, axis=0, keepdims=True)
                acc_sc[hr, :] = alpha * acc_sc[hr, :] + jnp.dot(
                    vt_ref[j, hr, :], pt.astype(BF16), preferred_element_type=F32)

        @pl.when(i == 0)
        def _():
            logits(0, 0, True)
            accumulate(0, 0)

        pairs = (i - 1) // 2

        @pl.when(i > 0)
        def _():
            logits(0, 0, False)

            def two_blocks(p, carry):
                logits(2 * p + 1, 1, False)
                accumulate(2 * p, 0)
                logits(2 * p + 2, 0, False)
                accumulate(2 * p + 1, 1)
                return carry

            lax.fori_loop(0, pairs, two_blocks, 0)

        @pl.when((i > 0) & (i - 2 * pairs == 1))
        def _():
            logits(i, 1, True)
            accumulate(i - 1, 0)
            accumulate(i, 1)

        @pl.when((i > 0) & (i - 2 * pairs == 2))
        def _():
            logits(i - 1, 1, False)
            accumulate(i - 2, 0)
            logits(i, 0, True)
            accumulate(i - 1, 1)
            accumulate(i, 0)

        for hd in range(HEADS):
            hr = slice(hd * HEAD_DIM, (hd + 1) * HEAD_DIM)
            l = l_sc[hd:hd + 1, :]
            acc_sc[hr, :] = acc_sc[hr, :] / l
            lse_ref[0, hd:hd + 1, :] = m_sc[hd:hd + 1, :] + jnp.log2(l)
        o_ref[...] = acc_sc[...].T.astype(BF16)
        ot_ref[...] = acc_sc[...].astype(BF16)

    return pl.pallas_call(
        body, name="attn_fwd", grid=(nb,),
        in_specs=[_row_spec(t, SLAB_W), _const_spec(ka.shape), _const_spec(vt.shape)],
        out_specs=[_row_spec(t, FOX_W), pl.BlockSpec((FOX_W, t), lambda i: (0, i)),
                   _tile_spec(HEADS, t)],
        out_shape=[jax.ShapeDtypeStruct((s_len, FOX_W), BF16),
                   jax.ShapeDtypeStruct((FOX_W, s_len), BF16),
                   jax.ShapeDtypeStruct((nb, HEADS, t), F32)],
        scratch_shapes=[pltpu.VMEM((HEADS, t), F32), pltpu.VMEM((HEADS, t), F32),
                        pltpu.VMEM((FOX_W, t), F32), pltpu.VMEM((2, HEADS, t, t), F32),
                        pltpu.VMEM((2, HEADS, t), F32), pltpu.VMEM((2, HEADS, t), F32)],
        compiler_params=_params(48, 1),
    )(qa, ka, vt)


def _sgu_mix(vn, ws_ref):
    tm = vn.shape[0]
    lane = lax.broadcasted_iota(jnp.int32, (WINDOW, 128), 1)
    low = lane < HEAD_DIM
    wins = []
    for w in range(tm // WINDOW):
        slabs = []
        for p in range(GROUPS // 2):
            v2 = vn[w * WINDOW:(w + 1) * WINDOW, p * 128:(p + 1) * 128]
            lo = jnp.where(low, v2, 0.0).astype(BF16)
            hi = jnp.where(low, 0.0, v2).astype(BF16)
            slabs.append(jnp.dot(ws_ref[2 * p], lo, preferred_element_type=F32)
                         + jnp.dot(ws_ref[2 * p + 1], hi, preferred_element_type=F32))
        wins.append(jnp.concatenate(slabs, axis=1))
    return jnp.concatenate(wins, axis=0) if len(wins) > 1 else wins[0]


def _layernorm_fwd(vv, g, b):
    mu = jnp.mean(vv, axis=-1, keepdims=True)
    xc = vv - mu
    r = lax.rsqrt(jnp.mean(xc * xc, axis=-1, keepdims=True) + EPS)
    xh = xc * r
    return xh * g + b, xh, r


def _mix_fwd(attn, uvpre, gpre, x, wa, wb, wout, wsm, bsf, gsgu, bsgu, gpost):
    s_len = x.shape[0]
    tm = TOKEN_TILE

    def body(o_ref, uv_ref, gp_ref, x_ref, wa_ref, wb_ref, wo_ref, ws_ref, bs_ref, gs_ref, bsg_ref,
             gpost_ref, sgut_ref, ya_ref, yb_ref, mgt_ref, om_ref, x1_ref):
        uvp = uv_ref[...].astype(F32)
        uv, _ = _gelu_and_grad(uvp)
        u, vv = uv[:, :SGU_W], uv[:, SGU_W:]
        vn, _, _ = _layernorm_fwd(vv, gs_ref[...], bsg_ref[...])
        bias = bs_ref[...]
        if tm > WINDOW:
            bias = jnp.concatenate([bias] * (tm // WINDOW), axis=0)
        mixed = _sgu_mix(vn, ws_ref) + bias
        sgu_f = u * mixed
        sgu = sgu_f.astype(BF16)
        sgut_ref[...] = sgu_f.T.astype(BF16)
        ya = jnp.dot(o_ref[...], wa_ref[...], preferred_element_type=F32)
        yb = jnp.dot(sgu, wb_ref[...], preferred_element_type=F32)
        ya_ref[...] = ya.astype(BF16)
        yb_ref[...] = yb.astype(BF16)
        gates = _sigmoid(gp_ref[...].astype(F32))
        merged_f = gates[:, :D_MODEL] * ya + gates[:, D_MODEL:] * yb
        merged = merged_f.astype(BF16)
        mgt_ref[...] = merged_f.T.astype(BF16)
        om = jnp.dot(merged, wo_ref[...], preferred_element_type=F32)
        om_ref[...] = om
        r = lax.rsqrt(jnp.mean(om * om, axis=-1, keepdims=True) + EPS)
        x1_ref[...] = x_ref[...] + om * r * gpost_ref[...]

    def t_out(rows):
        return ((rows, s_len), BF16, pl.BlockSpec((rows, tm), lambda i: (0, i)))

    def r_out(cols, dt):
        return ((s_len, cols), dt, _row_spec(tm, cols))

    outs = [t_out(SGU_W), r_out(D_MODEL, BF16), r_out(D_MODEL, BF16), t_out(D_MODEL),
            r_out(D_MODEL, F32), r_out(D_MODEL, F32)]
    return pl.pallas_call(
        body, name="mix_fwd", grid=(s_len // tm,),
        in_specs=[_row_spec(tm, FOX_W), _row_spec(tm, 2 * SGU_W), _row_spec(tm, 2 * D_MODEL),
                  _row_spec(tm, D_MODEL), _const_spec(wa.shape), _const_spec(wb.shape),
                  _const_spec(wout.shape), _const_spec(wsm.shape), _const_spec(bsf.shape),
                  _const_spec((1, SGU_W)), _const_spec((1, SGU_W)), _const_spec((1, D_MODEL))],
        out_specs=[o[2] for o in outs],
        out_shape=[jax.ShapeDtypeStruct(o[0], o[1]) for o in outs],
        compiler_params=_params(48, 1),
    )(attn, uvpre, gpre, x, wa, wb, wout, wsm, bsf, gsgu, bsgu, gpost)


def _ffn_fwd_bwd(x1, tgt, wffn, wdown, gpre, gpost):
    s_len = x1.shape[0]
    tm = TOKEN_TILE

    def body(x1_ref, t_ref, wi_ref, wd_ref, gpre_ref, gpost_ref,
             dx1_ref, h2_ref, actt_ref, dff_ref, dgut_ref, loss_ref, dgpost_ref, dgpre_ref):
        @pl.when(pl.program_id(0) == 0)
        def _():
            loss_ref[...] = jnp.zeros_like(loss_ref)
            dgpost_ref[...] = jnp.zeros_like(dgpost_ref)
            dgpre_ref[...] = jnp.zeros_like(dgpre_ref)

        x1v = x1_ref[...]
        r2 = lax.rsqrt(jnp.mean(x1v * x1v, axis=-1, keepdims=True) + EPS)
        gpre_v = gpre_ref[...]
        h2 = (x1v * r2 * gpre_v).astype(BF16)
        h2_ref[...] = h2
        gg = _dot_nt(h2, wi_ref[:D_FF, :])
        uu = _dot_nt(h2, wi_ref[D_FF:, :])
        sg = _sigmoid(gg)
        silu = gg * sg
        act_f = silu * uu
        act = act_f.astype(BF16)
        actt_ref[...] = act_f.T.astype(BF16)
        ff = jnp.dot(act, wd_ref[...], preferred_element_type=F32)
        r3 = lax.rsqrt(jnp.mean(ff * ff, axis=-1, keepdims=True) + EPS)
        gpost_v = gpost_ref[...]
        y = x1v + ff * r3 * gpost_v
        err = y - t_ref[...]
        loss_ref[...] += jnp.sum(err * err) * (0.5 / D_MODEL)
        dy = err * (1.0 / D_MODEL)
        dgpost_ref[...] += jnp.sum(dy * ff * r3, axis=0, keepdims=True)
        dff = _rms_bwd(ff, r3, gpost_v, dy).astype(BF16)
        dff_ref[...] = dff
        dact = _dot_nt(dff, wd_ref[...])
        dgg_f = dact * uu * (sg * (1.0 + gg * (1.0 - sg)))
        duu_f = dact * silu
        dgg = dgg_f.astype(BF16)
        duu = duu_f.astype(BF16)
        dgut_ref[:D_FF, :] = dgg_f.T.astype(BF16)
        dgut_ref[D_FF:, :] = duu_f.T.astype(BF16)
        dh2 = (jnp.dot(dgg, wi_ref[:D_FF, :], preferred_element_type=F32)
               + jnp.dot(duu, wi_ref[D_FF:, :], preferred_element_type=F32))
        dgpre_ref[...] += jnp.sum(dh2 * x1v * r2, axis=0, keepdims=True)
        dx1_ref[...] = dy + _rms_bwd(x1v, r2, gpre_v, dh2)

    outs = [((s_len, D_MODEL), F32, _row_spec(tm, D_MODEL)),
            ((s_len, D_MODEL), BF16, _row_spec(tm, D_MODEL)),
            ((D_FF, s_len), BF16, pl.BlockSpec((D_FF, tm), lambda i: (0, i))),
            ((s_len, D_MODEL), BF16, _row_spec(tm, D_MODEL)),
            ((2 * D_FF, s_len), BF16, pl.BlockSpec((2 * D_FF, tm), lambda i: (0, i))),
            ((1, 128), F32, _const_spec((1, 128))),
            ((1, D_MODEL), F32, _const_spec((1, D_MODEL))),
            ((1, D_MODEL), F32, _const_spec((1, D_MODEL)))]
    return pl.pallas_call(
        body, name="ffn_fwd_bwd", grid=(s_len // tm,),
        in_specs=[_row_spec(tm, D_MODEL), _row_spec(tm, D_MODEL), _const_spec(wffn.shape),
                  _const_spec(wdown.shape), _const_spec((1, D_MODEL)), _const_spec((1, D_MODEL))],
        out_specs=[o[2] for o in outs],
        out_shape=[jax.ShapeDtypeStruct(o[0], o[1]) for o in outs],
        compiler_params=_params(60, 1),
    )(x1, tgt, wffn, wdown, gpre, gpost)


def _mix_bwd(dx1, om, ya, yb, gpre, uvpre, attn, wout, wa, wb, wsm, wsmt, bsf, gsgu, bsgu, gpost,
             wmask, egrp):
    s_len = dx1.shape[0]
    tm = TOKEN_TILE
    nw = tm // WINDOW
    nt = s_len // tm

    def body(dx1_ref, om_ref, ya_ref, yb_ref, gp_ref, uv_ref, o_ref, wo_ref, wa_ref, wb_ref, ws_ref,
             wst_ref, bs_ref, gs_ref, bsg_ref, gpost_ref, mask_ref, eg_ref,
             dom_ref, dya_ref, dyb_ref, dgp_ref, dot_ref, delta_ref, duv_ref,
             dws_ref, dbs_ref, dgs_ref, dbsg_ref, dgpost_ref, dbs_acc):
        step = pl.program_id(0)

        @pl.when(step == 0)
        def _():
            dws_ref[...] = jnp.zeros_like(dws_ref)
            dbs_acc[...] = jnp.zeros_like(dbs_acc)
            dgs_ref[...] = jnp.zeros_like(dgs_ref)
            dbsg_ref[...] = jnp.zeros_like(dbsg_ref)
            dgpost_ref[...] = jnp.zeros_like(dgpost_ref)

        om = om_ref[...]
        dx1v = dx1_ref[...]
        r = lax.rsqrt(jnp.mean(om * om, axis=-1, keepdims=True) + EPS)
        gpost_v = gpost_ref[...]
        dgpost_ref[...] += jnp.sum(dx1v * om * r, axis=0, keepdims=True)
        dom = _rms_bwd(om, r, gpost_v, dx1v).astype(BF16)
        dom_ref[...] = dom
        dmg = _dot_nt(dom, wo_ref[...])

        gates = _sigmoid(gp_ref[...].astype(F32))
        ga, gb = gates[:, :D_MODEL], gates[:, D_MODEL:]
        yav, ybv = ya_ref[...].astype(F32), yb_ref[...].astype(F32)
        dya = (dmg * ga).astype(BF16)
        dyb = (dmg * gb).astype(BF16)
        dya_ref[...] = dya
        dyb_ref[...] = dyb
        dgp_ref[:, :D_MODEL] = (dmg * yav * ga * (1.0 - ga)).astype(BF16)
        dgp_ref[:, D_MODEL:] = (dmg * ybv * gb * (1.0 - gb)).astype(BF16)

        dat_t = _dot_nt(dya, wa_ref[...]).T.astype(BF16)
        dot_ref[0] = dat_t
        o_t = o_ref[...].astype(F32).T
        delta_ref[0] = jnp.sum((dat_t.astype(F32) * o_t).reshape(HEADS, HEAD_DIM, tm), axis=1)
        dsgu = _dot_nt(dyb, wb_ref[...])

        uvp = uv_ref[...].astype(F32)
        uv, guv = _gelu_and_grad(uvp)
        u, vv = uv[:, :SGU_W], uv[:, SGU_W:]
        gs_v = gs_ref[...]
        vn, xh, rln = _layernorm_fwd(vv, gs_v, bsg_ref[...])
        bias = bs_ref[...]
        if nw > 1:
            bias = jnp.concatenate([bias] * nw, axis=0)
        mixed = _sgu_mix(vn, ws_ref) + bias
        du = dsgu * mixed
        dmixed = dsgu * u

        lane = lax.broadcasted_iota(jnp.int32, (WINDOW, 128), 1)
        low = lane < HEAD_DIM
        dvn_wins = []
        for w in range(nw):
            rows = slice(w * WINDOW, (w + 1) * WINDOW)
            dbs_acc[...] += dmixed[rows, :]
            slabs = []
            for p in range(GROUPS // 2):
                cols = slice(p * 128, (p + 1) * 128)
                dm2 = dmixed[rows, cols]
                dlo = jnp.where(low, dm2, 0.0).astype(BF16)
                dhi = jnp.where(low, 0.0, dm2).astype(BF16)
                vn2 = vn[rows, cols].astype(BF16)
                dws_ref[2 * p] += _dot_nt(dlo, vn2)
                dws_ref[2 * p + 1] += _dot_nt(dhi, vn2)
                slabs.append(jnp.dot(wst_ref[2 * p], dlo, preferred_element_type=F32)
                             + jnp.dot(wst_ref[2 * p + 1], dhi, preferred_element_type=F32))
            dvn_wins.append(jnp.concatenate(slabs, axis=1))
        dvn = jnp.concatenate(dvn_wins, axis=0) if nw > 1 else dvn_wins[0]

        dgs_ref[...] += jnp.sum(dvn * xh, axis=0, keepdims=True)
        dbsg_ref[...] += jnp.sum(dvn, axis=0, keepdims=True)
        dxh = dvn * gs_v
        dvv = rln * (dxh - jnp.mean(dxh, axis=-1, keepdims=True)
                     - xh * jnp.mean(dxh * xh, axis=-1, keepdims=True))
        duv_ref[:, :SGU_W] = (du * guv[:, :SGU_W]).astype(BF16)
        duv_ref[:, SGU_W:] = (dvv * guv[:, SGU_W:]).astype(BF16)

        @pl.when(step == pl.num_programs(0) - 1)
        def _():
            for g in range(GROUPS):
                dws_ref[g] = dws_ref[g] * mask_ref[...]
            dbs_ref[...] = _split3_dot(dbs_acc[...], eg_ref[...])

    rows_out = [((s_len, D_MODEL), BF16, _row_spec(tm, D_MODEL)),
                ((s_len, D_MODEL), BF16, _row_spec(tm, D_MODEL)),
                ((s_len, D_MODEL), BF16, _row_spec(tm, D_MODEL)),
                ((s_len, 2 * D_MODEL), BF16, _row_spec(tm, 2 * D_MODEL)),
                ((nt, FOX_W, tm), BF16, _tile_spec(FOX_W, tm)),
                ((nt, HEADS, tm), F32, _tile_spec(HEADS, tm)),
                ((s_len, 2 * SGU_W), BF16, _row_spec(tm, 2 * SGU_W))]
    acc_out = [((GROUPS, WINDOW, WINDOW), F32), ((WINDOW, 128), F32), ((1, SGU_W), F32),
               ((1, SGU_W), F32), ((1, D_MODEL), F32)]
    return pl.pallas_call(
        body, name="mix_bwd", grid=(nt,),
        in_specs=[_row_spec(tm, D_MODEL), _row_spec(tm, D_MODEL), _row_spec(tm, D_MODEL),
                  _row_spec(tm, D_MODEL), _row_spec(tm, 2 * D_MODEL), _row_spec(tm, 2 * SGU_W),
                  _row_spec(tm, FOX_W), _const_spec(wout.shape), _const_spec(wa.shape),
                  _const_spec(wb.shape), _const_spec(wsm.shape), _const_spec(wsmt.shape),
                  _const_spec(bsf.shape), _const_spec((1, SGU_W)), _const_spec((1, SGU_W)),
                  _const_spec((1, D_MODEL)), _const_spec(wmask.shape), _const_spec(egrp.shape)],
        out_specs=[o[2] for o in rows_out] + [_const_spec(s) for s, _ in acc_out],
        out_shape=[jax.ShapeDtypeStruct(o[0], o[1]) for o in rows_out]
        + [jax.ShapeDtypeStruct(s, dt) for s, dt in acc_out],
        scratch_shapes=[pltpu.VMEM((WINDOW, SGU_W), F32)],
        compiler_params=_params(48, 1),
    )(dx1, om, ya, yb, gpre, uvpre, attn, wout, wa, wb, wsm, wsmt, bsf, gsgu, bsgu, gpost, wmask,
      egrp)


def _attn_bwd(qa, ka, kat, vs, dot_, lse, delta, ecol):
    s_len = qa.shape[0]
    t = ATTN_TILE
    nb = s_len // t

    def body(k_ref, kt_ref, vs_ref, q_ref, do_ref, lse_ref, dl_ref, ec_ref, gk_ref, dvt_ref,
             gqt_ref, csum_ref, p_sc, ds_sc):
        j = pl.program_id(0)

        @pl.when(j == 0)
        def _():
            gqt_ref[...] = jnp.zeros_like(gqt_ref)

        gk_ref[...] = jnp.zeros_like(gk_ref)
        dvt_ref[...] = jnp.zeros_like(dvt_ref)

        def probs(i, slot, masked):
            qrows = pl.ds(pl.multiple_of(i * t, t), t)
            if masked:
                keep = (lax.broadcasted_iota(jnp.int32, (t, t), 0)
                        <= lax.broadcasted_iota(jnp.int32, (t, t), 1))
            for hd in range(HEADS):
                sl = slice(hd * 128, (hd + 1) * 128)
                hr = slice(hd * HEAD_DIM, (hd + 1) * HEAD_DIM)
                st = _dot_nt(k_ref[:, sl], q_ref[qrows, sl])
                if masked:
                    st = jnp.where(keep, st, -jnp.inf)
                pt = jnp.exp2(st - lse_ref[i, hd:hd + 1, :])
                dpt = jnp.dot(vs_ref[:, hd * 128:hd * 128 + HEAD_DIM], do_ref[i, hr, :],
                              preferred_element_type=F32)
                p_sc[slot, hd] = pt.astype(BF16)
                ds_sc[slot, hd] = (pt * (dpt - dl_ref[i, hd:hd + 1, :])).astype(BF16)

        def grads(i, slot):
            qrows = pl.ds(pl.multiple_of(i * t, t), t)
            for hd in range(HEADS):
                sl = slice(hd * 128, (hd + 1) * 128)
                hr = slice(hd * HEAD_DIM, (hd + 1) * HEAD_DIM)
                dst = ds_sc[slot, hd]
                dvt_ref[0, hr, :] += _dot_nt(do_ref[i, hr, :], p_sc[slot, hd])
                gk_ref[:, sl] += jnp.dot(dst, q_ref[qrows, sl], preferred_element_type=F32)
                gqt_ref[i, hd * QT_ROWS:(hd + 1) * QT_ROWS, :] += jnp.dot(
                    kt_ref[0, hd * 128:hd * 128 + QT_ROWS, :], dst, preferred_element_type=F32)

        probs(j, 0, True)
        pairs = (nb - 1 - j) // 2

        def two_blocks(p, carry):
            i1 = j + 1 + 2 * p
            probs(i1, 1, False)
            grads(i1 - 1, 0)
            probs(i1 + 1, 0, False)
            grads(i1, 1)
            return carry

        lax.fori_loop(0, pairs, two_blocks, 0)

        @pl.when(nb - 1 - j - 2 * pairs == 0)
        def _():
            grads(nb - 1, 0)

        @pl.when(nb - 1 - j - 2 * pairs == 1)
        def _():
            probs(nb - 1, 1, False)
            grads(nb - 2, 0)
            grads(nb - 1, 1)

        csum_ref[...] = _split3_dot(gk_ref[...], ec_ref[...])

    return pl.pallas_call(
        body, name="attn_bwd", grid=(nb,),
        in_specs=[_row_spec(t, SLAB_W), _tile_spec(SLAB_W, t), _row_spec(t, SLAB_W),
                  _const_spec(qa.shape), _const_spec(dot_.shape), _const_spec(lse.shape),
                  _const_spec(delta.shape), _const_spec(ecol.shape)],
        out_specs=[_row_spec(t, SLAB_W), _tile_spec(FOX_W, t),
                   _const_spec((nb, HEADS * QT_ROWS, t)), _row_spec(t, 128)],
        out_shape=[jax.ShapeDtypeStruct((s_len, SLAB_W), F32),
                   jax.ShapeDtypeStruct((nb, FOX_W, t), F32),
                   jax.ShapeDtypeStruct((nb, HEADS * QT_ROWS, t), F32),
                   jax.ShapeDtypeStruct((s_len, 128), F32)],
        scratch_shapes=[pltpu.VMEM((2, HEADS, t, t), BF16), pltpu.VMEM((2, HEADS, t, t), BF16)],
        compiler_params=_params(60, 1),
    )(ka, kat, vs, qa, dot_, lse, delta, ecol)


def _rev_cumsum(col_sums, gqt, triu):
    s_len = col_sums.shape[0]
    tm = TOKEN_TILE
    n = s_len // tm

    def body(cs_ref, gqt_ref, tri_ref, o_ref, carry):
        @pl.when(pl.program_id(0) == 0)
        def _():
            carry[...] = jnp.zeros_like(carry)
        rows = [gqt_ref[0, hd * QT_ROWS + HEAD_DIM:hd * QT_ROWS + HEAD_DIM + 1, :]
                for hd in range(HEADS)]
        row_sums = jnp.concatenate(rows + [jnp.zeros((128 - HEADS, tm), F32)], axis=0).T
        out = _tri_dot(tri_ref[...], row_sums - cs_ref[...]) + carry[...]
        o_ref[...] = out
        carry[...] = out[0:1, :]

    return pl.pallas_call(
        body, name="rev_cumsum", grid=(n,),
        in_specs=[pl.BlockSpec((tm, 128), lambda i: (n - 1 - i, 0)),
                  pl.BlockSpec((1, HEADS * QT_ROWS, tm), lambda i: (n - 1 - i, 0, 0)),
                  _const_spec((tm, tm))],
        out_specs=pl.BlockSpec((tm, 128), lambda i: (n - 1 - i, 0)),
        out_shape=jax.ShapeDtypeStruct((s_len, 128), F32),
        scratch_shapes=[pltpu.VMEM((1, 128), F32)],
        compiler_params=_params(32, 1),
    )(col_sums, gqt, triu)


def _heads_from_slabs(slabs):
    lane = lax.broadcasted_iota(jnp.int32, slabs[0].shape, 1)
    low = lane < HEAD_DIM
    pairs = [jnp.where(low, slabs[2 * p], pltpu.roll(slabs[2 * p + 1], HEAD_DIM, 1))
             for p in range(HEADS // 2)]
    return jnp.concatenate(pairs, axis=1)


def _proj_bwd(gqt, gk, dvt, dlogf, flog, qraw, kraw, duv, dgp, x, dx1, wcat, bdiag, gq, gk_gain, g1,
              efold):
    s_len = x.shape[0]
    tm = TOKEN_TILE

    def body(gqt_ref, gkk_ref, dvt_ref, dlf_ref, flog_ref, qr_ref, kr_ref, duv_ref, dgp_ref, x_ref,
             dx1_ref, w_ref, bd_ref, gq_ref, gk_ref, g1_ref, ef_ref,
             dx_ref, dprojt_ref, dgq_ref, dgk_ref, dbf_ref, dg1_ref, gq_acc, gk_acc):
        step = pl.program_id(0)

        @pl.when(step == 0)
        def _():
            gq_acc[...] = jnp.zeros_like(gq_acc)
            gk_acc[...] = jnp.zeros_like(gk_acc)
            dbf_ref[...] = jnp.zeros_like(dbf_ref)
            dg1_ref[...] = jnp.zeros_like(dg1_ref)

        pad = jnp.zeros((128 - QT_ROWS, tm), F32)
        q_slabs = [jnp.concatenate([gqt_ref[0, hd * QT_ROWS:(hd + 1) * QT_ROWS, :], pad], axis=0).T
                   for hd in range(HEADS)]
        dqn = _heads_from_slabs(q_slabs)
        dkn = _heads_from_slabs([gkk_ref[:, hd * 128:(hd + 1) * 128] for hd in range(HEADS)])

        def head_bwd(raw_ref, dn, g_ref, acc):
            raw = raw_ref[...].astype(F32)
            r = lax.rsqrt(_seg_mean(raw * raw, bd_ref) + EPS)
            xhat = raw * r
            acc[0:1, :] += jnp.sum(dn * xhat, axis=0, keepdims=True)
            dyg = dn * g_ref[...]
            return r * (dyg - xhat * _seg_mean(dyg * xhat, bd_ref))

        dot = functools.partial(jnp.dot, preferred_element_type=F32)
        duv, dgp = duv_ref[...], dgp_ref[...]
        dprojt_ref[C_UV:C_G, :] = duv.astype(F32).T.astype(BF16)
        dprojt_ref[C_G:C_F, :] = dgp.astype(F32).T.astype(BF16)
        dh = dot(duv, w_ref[C_UV:C_G, :]) + dot(dgp, w_ref[C_G:C_F, :])

        dq = head_bwd(qr_ref, dqn * HEAD_DIM ** -0.5, gq_ref, gq_acc)
        dk = head_bwd(kr_ref, dkn * LN2, gk_ref, gk_acc)
        dv_t = dvt_ref[0]
        dfl = dlf_ref[...] * _sigmoid(-flog_ref[...])
        dbf_ref[...] += jnp.sum(dfl, axis=0, keepdims=True)
        dprojt_ref[C_Q:C_K, :] = dq.T.astype(BF16)
        dprojt_ref[C_K:C_V, :] = dk.T.astype(BF16)
        dprojt_ref[C_V:C_UV, :] = dv_t.astype(BF16)
        dprojt_ref[C_F:C_END, :] = dfl.T.astype(BF16)
        dh = (dh + dot(dq.astype(BF16), w_ref[C_Q:C_K, :]) + dot(dk.astype(BF16), w_ref[C_K:C_V, :])
              + dot(dv_t.T.astype(BF16), w_ref[C_V:C_UV, :])
              + dot(dfl.astype(BF16), w_ref[C_F:C_END, :]))
        xf = x_ref[...]
        r = lax.rsqrt(jnp.mean(xf * xf, axis=-1, keepdims=True) + EPS)
        dg1_ref[...] += jnp.sum(dh * xf * r, axis=0, keepdims=True)
        dx_ref[...] = dx1_ref[...] + _rms_bwd(xf, r, g1_ref[...], dh)

        @pl.when(step == pl.num_programs(0) - 1)
        def _():
            dgq_ref[...] = _split3_dot(gq_acc[...], ef_ref[...])
            dgk_ref[...] = _split3_dot(gk_acc[...], ef_ref[...])

    outs = [((s_len, D_MODEL), F32, _row_spec(tm, D_MODEL)),
            ((C_END, s_len), BF16, pl.BlockSpec((C_END, tm), lambda i: (0, i))),
            ((8, 128), F32, _const_spec((8, 128))),
            ((8, 128), F32, _const_spec((8, 128))),
            ((1, 128), F32, _const_spec((1, 128))),
            ((1, D_MODEL), F32, _const_spec((1, D_MODEL)))]
    return pl.pallas_call(
        body, name="proj_bwd", grid=(s_len // tm,),
        in_specs=[_tile_spec(HEADS * QT_ROWS, tm), _row_spec(tm, SLAB_W), _tile_spec(FOX_W, tm),
                  _row_spec(tm, 128), _row_spec(tm, 128), _row_spec(tm, FOX_W),
                  _row_spec(tm, FOX_W), _row_spec(tm, 2 * SGU_W), _row_spec(tm, 2 * D_MODEL),
                  _row_spec(tm, D_MODEL), _row_spec(tm, D_MODEL), _const_spec(wcat.shape),
                  _const_spec(bdiag.shape), _const_spec((1, FOX_W)), _const_spec((1, FOX_W)),
                  _const_spec((1, D_MODEL)), _const_spec(efold.shape)],
        out_specs=[o[2] for o in outs],
        out_shape=[jax.ShapeDtypeStruct(o[0], o[1]) for o in outs],
        scratch_shapes=[pltpu.VMEM((8, FOX_W), F32), pltpu.VMEM((8, FOX_W), F32)],
        compiler_params=_params(56, 1),
    )(gqt, gk, dvt, dlogf, flog, qraw, kraw, duv, dgp, x, dx1, wcat, bdiag, gq, gk_gain, g1, efold)


def _dw_matmul(at, b, tm, name, after=()):
    m, s_len = at.shape
    n = b.shape[1]

    def body(a_ref, b_ref, *rest):
        rest[-1][...] = jnp.dot(a_ref[...], b_ref[...], preferred_element_type=F32).astype(BF16)

    return pl.pallas_call(
        body, name=name, grid=(m // tm,),
        in_specs=[pl.BlockSpec((tm, s_len), lambda i: (i, 0)), _const_spec(b.shape)]
        + [pl.BlockSpec(memory_space=pl.ANY)] * len(after),
        out_specs=pl.BlockSpec((tm, n), lambda i: (i, 0)),
        out_shape=jax.ShapeDtypeStruct((m, n), BF16),
        compiler_params=_params(48, 1),
    )(at, b, *after)


def _adamw(parts, w, m, v, tr, name, col_tile=None, select=None):
    parts = parts if isinstance(parts, (list, tuple)) else [parts]
    rows, cols = w.shape
    extra = [] if select is None else [select]
    bc1 = 1.0 - ADAM_B1 ** ADAM_STEP
    bc2 = 1.0 - ADAM_B2 ** ADAM_STEP

    def body(*refs):
        p_refs = refs[:len(parts)]
        sel_refs = refs[len(parts):len(parts) + len(extra)]
        w_ref, m_ref, v_ref, g_ref, d_ref, mo_ref, vo_ref = refs[len(parts) + len(extra):]
        g = None
        for p_ref, p in zip(p_refs, parts):
            for idx in range(p.shape[0]):
                term = p_ref[idx].astype(F32)
                g = term if g is None else g + term
        if sel_refs:
            g = _tri_dot(sel_refs[0][...], g)
        g_ref[...] = g
        mn = ADAM_B1 * m_ref[...] + (1.0 - ADAM_B1) * g
        vn = ADAM_B2 * v_ref[...] + (1.0 - ADAM_B2) * (g * g)
        mo_ref[...] = mn
        vo_ref[...] = vn
        m_hat = mn / bc1
        v_hat = vn / bc2
        d_ref[...] = -ADAM_LR * (m_hat / (jnp.sqrt(v_hat) + ADAM_EPS) + ADAM_WD * w_ref[...])

    if col_tile is None:
        spec = pl.BlockSpec((tr, cols), lambda i: (i, 0))
        pspecs = [pl.BlockSpec((p.shape[0], tr, cols), lambda i: (0, i, 0)) for p in parts]
        steps = rows // tr
    else:
        spec = pl.BlockSpec((rows, col_tile), lambda i: (0, i))
        pspecs = [pl.BlockSpec((p.shape[0], p.shape[1], col_tile), lambda i: (0, 0, i))
                  for p in parts]
        steps = cols // col_tile
    return pl.pallas_call(
        body, name=name, grid=(steps,),
        in_specs=pspecs + [_const_spec(e.shape) for e in extra] + [spec, spec, spec],
        out_specs=[spec] * 4,
        out_shape=[jax.ShapeDtypeStruct((rows, cols), F32)] * 4,
        compiler_params=_params(48, 1),
    )(*parts, *extra, w, m, v)


def _sum_parts(parts, name):
    n, rows, cols = parts.shape

    def body(p_ref, o_ref):
        g = p_ref[0]
        for idx in range(1, n):
            g = g + p_ref[idx]
        o_ref[...] = g

    return pl.pallas_call(
        body, name=name, out_shape=jax.ShapeDtypeStruct((rows, cols), F32),
        in_specs=[_const_spec(parts.shape)], out_specs=_const_spec((rows, cols)), grid=(1,),
        compiler_params=_params(16, 1),
    )(parts)


VEC_NAMES = ("g_pre_mix", "b_forget", "g_q", "g_k", "g_sgu", "b_sgu", "b_spatial", "g_post_mix",
             "g_pre_ffn", "g_post_ffn")
VEC_ROWS = 16
LOSS_ROW = len(VEC_NAMES)


def _pack_vectors(d, loss_row):
    rows = []
    for k in VEC_NAMES:
        flat = d[k].reshape(1, -1).astype(F32)
        rows.append(jnp.pad(flat, ((0, 0), (0, 1024 - flat.shape[1]))))
    rows.append(loss_row)
    rows.append(jnp.zeros((VEC_ROWS - len(rows), 1024), F32))
    return jnp.concatenate(rows, axis=0)


def _adamw_vectors(grad_rows, ws, ms, vs):
    n = len(VEC_NAMES)
    bc1 = 1.0 - ADAM_B1 ** ADAM_STEP
    bc2 = 1.0 - ADAM_B2 ** ADAM_STEP

    def step(g, w, m, v):
        mn = ADAM_B1 * m + (1.0 - ADAM_B1) * g
        vn = ADAM_B2 * v + (1.0 - ADAM_B2) * (g * g)
        delta = -ADAM_LR * ((mn / bc1) / (jnp.sqrt(vn / bc2) + ADAM_EPS) + ADAM_WD * w)
        return g, delta, mn, vn

    def body(*refs):
        g_ref = refs[0]
        ins = [refs[1 + j * n:1 + (j + 1) * n] for j in range(3)]
        outs = [refs[1 + (3 + j) * n:1 + (4 + j) * n] for j in range(4)]
        for i in range(n):
            shape = ws[i].shape
            if len(shape) == 2:
                res = step(g_ref[i:i + 1, :shape[1]], *[r[i][...] for r in ins])
                for o, val in zip(outs, res):
                    o[i][...] = val
            else:
                for r in range(shape[1]):
                    res = step(g_ref[i:i + 1, r * shape[2]:(r + 1) * shape[2]],
                               *[q[i][0, r:r + 1, :] for q in ins])
                    for o, val in zip(outs, res):
                        o[i][0, r:r + 1, :] = val

    vmem = pl.BlockSpec(memory_space=pltpu.VMEM)
    flat = pl.pallas_call(
        body, name="adamw_vectors",
        in_specs=[vmem] * (1 + 3 * n), out_specs=[vmem] * (4 * n),
        out_shape=[jax.ShapeDtypeStruct(w.shape, F32) for _ in range(4) for w in ws],
    )(grad_rows, *ws, *ms, *vs)
    return [flat[j * n:(j + 1) * n] for j in range(4)]


def _cols_to_blocks(full, width):
    r = full.shape[0]
    return jnp.transpose(full.reshape(r, N_DEV, width), (1, 0, 2))


def _blocks_to_cols(blocks):
    n, r, width = blocks.shape
    return jnp.transpose(blocks, (1, 0, 2)).reshape(r, n * width)


def kernel(x, g_pre_mix, w_in, b_forget, g_q, g_k, g_sgu, b_sgu, w_spatial, b_spatial, w_branch_a, w_branch_b, w_out, g_post_mix, g_pre_ffn, w_ffn_in, w_ffn_down, g_post_ffn, loss_target, m_g_pre_mix, m_w_in, m_b_forget, m_g_q, m_g_k, m_g_sgu, m_b_sgu, m_w_spatial, m_b_spatial, m_w_branch_a, m_w_branch_b, m_w_out, m_g_post_mix, m_g_pre_ffn, m_w_ffn_in, m_w_ffn_down, m_g_post_ffn, v_g_pre_mix, v_w_in, v_b_forget, v_g_q, v_g_k, v_g_sgu, v_b_sgu, v_w_spatial, v_b_spatial, v_w_branch_a, v_w_branch_b, v_w_out, v_g_post_mix, v_g_pre_ffn, v_w_ffn_in, v_w_ffn_down, v_g_post_ffn):
    big_names = ("w_in", "w_branch_a", "w_branch_b", "w_out", "w_ffn_in", "w_ffn_down")
    weights = dict(g_pre_mix=g_pre_mix, w_in=w_in, b_forget=b_forget, g_q=g_q, g_k=g_k, g_sgu=g_sgu,
                   b_sgu=b_sgu, w_spatial=w_spatial, b_spatial=b_spatial, w_branch_a=w_branch_a,
                   w_branch_b=w_branch_b, w_out=w_out, g_post_mix=g_post_mix, g_pre_ffn=g_pre_ffn,
                   w_ffn_in=w_ffn_in, w_ffn_down=w_ffn_down, g_post_ffn=g_post_ffn)
    mom1 = dict(g_pre_mix=m_g_pre_mix, w_in=m_w_in, b_forget=m_b_forget, g_q=m_g_q, g_k=m_g_k,
                g_sgu=m_g_sgu, b_sgu=m_b_sgu, w_spatial=m_w_spatial, b_spatial=m_b_spatial,
                w_branch_a=m_w_branch_a, w_branch_b=m_w_branch_b, w_out=m_w_out,
                g_post_mix=m_g_post_mix, g_pre_ffn=m_g_pre_ffn, w_ffn_in=m_w_ffn_in,
                w_ffn_down=m_w_ffn_down, g_post_ffn=m_g_post_ffn)
    mom2 = dict(g_pre_mix=v_g_pre_mix, w_in=v_w_in, b_forget=v_b_forget, g_q=v_g_q, g_k=v_g_k,
                g_sgu=v_g_sgu, b_sgu=v_b_sgu, w_spatial=v_w_spatial, b_spatial=v_b_spatial,
                w_branch_a=v_w_branch_a, w_branch_b=v_w_branch_b, w_out=v_w_out,
                g_post_mix=v_g_post_mix, g_pre_ffn=v_g_pre_ffn, w_ffn_in=v_w_ffn_in,
                w_ffn_down=v_w_ffn_down, g_post_ffn=v_g_post_ffn)
    names = list(weights)
    shapes = {k: weights[k].shape for k in names}

    s_len = x.shape[1]
    xs = x.reshape(s_len, D_MODEL)
    tgt = loss_target.reshape(s_len, D_MODEL)

    transposed = ("w_in", "w_ffn_in")

    def local_view(a, k):
        return jnp.transpose(a[0]) if k in transposed else a[0]

    shards = {k: local_view(weights[k], k).astype(BF16) for k in big_names}

    x_pos, y_pos, c_pos = _mesh_pos()
    me = 4 * x_pos + 2 * y_pos + c_pos
    r_idx = jnp.arange(BLK)
    general = (jnp.asarray(BLK_AT, jnp.int32) - jnp.asarray(FRAME_START, jnp.int32))[me] + r_idx
    holder = jnp.where(r_idx < F_AT, BLK_AT[F_DEV] - FRAME_START[F_DEV] + r_idx,
                       jnp.where(r_idx < F_AT + HEADS, FRAME - F_AT + r_idx,
                                 BLK_AT[F_DEV] - FRAME_START[F_DEV] - HEADS + r_idx))
    frame_row = jnp.where(me == F_DEV, holder, general)
    in_frame = (frame_row[:, None] == jnp.arange(FRAME_ROWS)[None, :]).astype(BF16)
    my_frame = jnp.dot(in_frame.T, shards["w_in"], preferred_element_type=F32).astype(BF16)
    wcat = _gather_w_in(my_frame)
    wcat, later = lax.optimization_barrier(
        (wcat, [shards[k] for k in big_names if k != "w_in"]))
    shards.update(zip([k for k in big_names if k != "w_in"], later))
    (gat_mix, gat_ffn), gat_token = _exchange_start(
        [[shards["w_branch_a"], shards["w_branch_b"], shards["w_out"]],
         [shards["w_ffn_in"], shards["w_ffn_down"]]], "gather_start", gather=True)

    seg = np.arange(FOX_W) // HEAD_DIM
    bdiag = jnp.asarray(seg[:128, None] == seg[None, :128], BF16)
    tm = TOKEN_TILE
    lower = np.arange(tm)[None, :] <= np.arange(tm)[:, None]
    tril = jnp.asarray(lower, BF16)
    triu = jnp.asarray(lower.T, BF16)
    egrp = jnp.asarray(seg[:, None] == np.arange(128)[None, :], BF16)
    efold = jnp.asarray((np.arange(FOX_W) % HEAD_DIM)[:, None] == np.arange(128)[None, :], BF16)
    gq512 = jnp.tile(g_q.reshape(1, HEAD_DIM), (1, HEADS))
    gk512 = jnp.tile(g_k.reshape(1, HEAD_DIM), (1, HEADS))
    bfor = jnp.pad(b_forget.reshape(1, HEADS), ((0, 0), (0, 128 - HEADS)))
    pos = np.arange(WINDOW)
    wmask = (pos[None, :] // CHUNK) <= (pos[:, None] // CHUNK)
    wsm_f = jnp.where(jnp.asarray(wmask)[None], w_spatial[0], 0.0)
    wsm = wsm_f.astype(BF16)
    wsmt = jnp.transpose(wsm_f, (0, 2, 1)).astype(BF16)
    bsf = jnp.repeat(jnp.transpose(b_spatial[0]), HEAD_DIM, axis=1)
    wmask_f = jnp.asarray(wmask, F32)

    col = np.arange(SLAB_W)
    row128 = np.arange(128)

    def d_place(first, sign):
        parts = [(col[None, :] // 128 == row128[:, None]) & (col[None, :] % 128 == first + a)
                 for a in range(3)]
        return jnp.asarray(sign * np.concatenate(parts, axis=0).astype(np.float32), BF16)

    pdq, pdk = d_place(HEAD_DIM, 1.0), d_place(HEAD_DIM + 3, -1.0)
    ones_q = jnp.asarray((col % 128 >= HEAD_DIM + 3) & (col % 128 < HEAD_DIM + 6), F32)[None]
    ones_k = jnp.asarray((col % 128 >= HEAD_DIM) & (col % 128 < HEAD_DIM + 3), F32)[None]
    ecol = jnp.asarray((col[:, None] // 128 == row128[None, :])
                       & (col[:, None] % 128 == HEAD_DIM + 3), BF16)

    (h, qa, ka, kat, vs, vt, qraw, kraw, flog, uvpre, gpre) = _proj_fwd(
        xs, g_pre_mix + gat_token[0:1, 0:1], wcat, bdiag, gq512, gk512, bfor, tril, pdq, pdk,
        ones_q, ones_k)
    attn, attn_t, lse = _attn_fwd(qa, ka, vt)
    (own_a, own_b, own_out), (zone_a, zone_b, zone_out) = _exchange_wait(
        gat_mix, attn, "gather_wait_mix", gather=True)
    wa = _blocks_to_cols(_own_block(zone_a, own_a))
    wb = _blocks_to_cols(_own_block(zone_b, own_b))
    wout = _own_block(zone_out, own_out).reshape(D_MODEL, D_MODEL)
    sgu_t, ya, yb, merged_t, om, x1 = _mix_fwd(attn, uvpre, gpre, xs, wa, wb, wout, wsm, bsf,
                                           g_sgu, b_sgu, g_post_mix)
    (own_ffn, own_down), (zone_ffn, zone_down) = _exchange_wait(
        gat_ffn, x1, "gather_wait_ffn", gather=True)
    wffn = _own_block(zone_ffn, own_ffn).reshape(2 * D_FF, D_MODEL)
    wdown = _own_block(zone_down, own_down).reshape(D_FF, D_MODEL)
    (dx1, h2, act_t, dff, dgu_t, loss_acc, dg_post_ffn, dg_pre_ffn) = _ffn_fwd_bwd(
        x1, tgt, wffn, wdown, g_pre_ffn, g_post_ffn)

    dw_down = _dw_matmul(act_t, dff, D_FF // 4, "dw_down")
    dw_ffn = _dw_matmul(dgu_t, h2, 2 * D_FF // N_DEV, "dw_ffn_in")
    def own_of(parts):
        return [lax.dynamic_index_in_dim(p, me, 0, keepdims=False) for p in parts]

    parts_ffn = [dw_ffn.reshape(N_DEV, 2 * D_FF // N_DEV, D_MODEL),
                 dw_down.reshape(N_DEV, D_FF // N_DEV, D_MODEL)]
    mine_ffn = own_of(parts_ffn)
    (sct_ffn,), sct_ffn_token = _exchange_start([parts_ffn], "scatter_start_ffn", gather=False,
                                                after=mine_ffn)

    (dom, dya, dyb, dgp, dot_, delta, duv, dws, dbs, dg_sgu, db_sgu, dg_post_mix) = _mix_bwd(
        dx1, om, ya, yb, gpre, uvpre, attn, wout, wa, wb, wsm, wsmt, bsf, g_sgu, b_sgu,
        g_post_mix + sct_ffn_token[0:1, 0:1], wmask_f, egrp)
    dw_out = _dw_matmul(merged_t, dom, 512, "dw_out")
    dw_a = _dw_matmul(attn_t, dya, 512, "dw_a")
    dw_b = _dw_matmul(sgu_t, dyb, 512, "dw_b")
    parts_mix = [_cols_to_blocks(dw_a, D_MODEL // N_DEV), _cols_to_blocks(dw_b, D_MODEL // N_DEV),
                 dw_out.reshape(N_DEV, D_MODEL // N_DEV, D_MODEL)]
    mine_mix = own_of(parts_mix)
    (sct_mix,), sct_mix_token = _exchange_start([parts_mix], "scatter_start_mix", gather=False,
                                                after=mine_mix)

    gk_all, dvt, gqt, col_sums = _attn_bwd(qa, ka, kat, vs, dot_, lse,
                                           delta + sct_mix_token[0, 0], ecol)
    dlogf = _rev_cumsum(col_sums, gqt, triu)
    dx, dproj_t, dgq, dgk, dbf, dg_pre_mix = _proj_bwd(
        gqt, gk_all, dvt, dlogf, flog, qraw, kraw, duv, dgp, xs, dx1, wcat, bdiag, gq512, gk512,
        g_pre_mix, efold)

    small_local = dict(
        g_pre_mix=dg_pre_mix, b_forget=dbf[:, :HEADS], g_q=dgq[0:1, :HEAD_DIM],
        g_k=dgk[0:1, :HEAD_DIM], g_sgu=dg_sgu, b_sgu=db_sgu, w_spatial=dws,
        b_spatial=jnp.transpose(dbs[:, :GROUPS]), g_post_mix=dg_post_mix, g_pre_ffn=dg_pre_ffn,
        g_post_ffn=dg_post_ffn)
    loss_row = jnp.pad(loss_acc[0:1, 0:1], ((0, 0), (0, 1023)))
    small_parts = [_pack_vectors(small_local, loss_row).reshape(N_DEV, VEC_ROWS // N_DEV, 1024),
                   dws]

    def with_own(zones, own_blocks):
        return [_own_block(z, b) for z, b in zip(zones, own_blocks)]

    mine_small = own_of(small_parts)
    (sct_small,), sct_small_token = _exchange_start([small_parts], "scatter_start_small",
                                                    gather=False, after=mine_small)
    dw_cat = _dw_matmul(dproj_t, h, C_END // N_DEV, "dw_in", after=(sct_small_token,))
    recv_vec, recv_ws = with_own(
        _exchange_wait(sct_small, dw_cat, "scatter_wait_small", gather=False)[1], mine_small)
    small_sums = [_sum_parts(recv_vec, "sum_vectors"), _sum_parts(recv_ws, "sum_w_spatial")]
    (gat_small,), gat_small_token = _exchange_start([small_sums], "gather_start_small",
                                                    gather=True)
    pair_blocks, own_pair = _pair_sums(dw_cat, "pair_sums_in", gat_small_token)
    rs_in, rs_token = _chip_exchange_start(pair_blocks, "chip_exchange_start_in")

    recv_ffn, recv_down = with_own(
        _exchange_wait(sct_ffn, rs_token, "scatter_wait_ffn", gather=False)[1], mine_ffn)
    recv_a, recv_b, recv_out = with_own(
        _exchange_wait(sct_mix, recv_ffn, "scatter_wait_mix", gather=False)[1], mine_mix)
    received = [None, recv_a, recv_b, recv_out, recv_ffn, recv_down]

    grads, deltas, new_m, new_v = {}, {}, {}, {}
    row_tiles = {"w_in": None, "w_branch_a": 512, "w_branch_b": 512, "w_out": 128, "w_ffn_in": 176,
                 "w_ffn_down": 352}

    def update(k, parts):
        outs = _adamw(parts, local_view(weights[k], k), local_view(mom1[k], k),
                      local_view(mom2[k], k), row_tiles[k], "adamw_" + k,
                      col_tile=256 if k == "w_in" else None,
                      select=in_frame if k == "w_in" else None)
        if k in transposed:
            outs = [jnp.transpose(o) for o in outs]
        grads[k], deltas[k], new_m[k], new_v[k] = [o[None] for o in outs]
        return outs[0]

    last = None
    for idx, k in enumerate(big_names):
        if k != "w_in":
            last = update(k, received[idx])

    (own_vec, own_ws), (zone_vec, zone_ws) = _exchange_wait(gat_small, last, "gather_wait_small",
                                                            gather=True)
    vec_all = _own_block(zone_vec, own_vec).reshape(VEC_ROWS, 1024)
    ws_all = _own_block(zone_ws, own_ws).reshape(1, GROUPS * WINDOW, WINDOW)

    def rows_of(d):
        return d["w_spatial"].reshape(GROUPS * WINDOW, WINDOW)

    outs = _adamw(ws_all, rows_of(weights), rows_of(mom1), rows_of(mom2), GROUPS * WINDOW,
                  "adamw_w_spatial")
    for dst, o in zip((grads, deltas, new_m, new_v), outs):
        dst["w_spatial"] = o.reshape(shapes["w_spatial"])
    sg = outs[0]
    vec_outs = _adamw_vectors(vec_all, *[[d[k] for k in VEC_NAMES] for d in (weights, mom1, mom2)])
    for dst, group in zip((grads, deltas, new_m, new_v), vec_outs):
        dst.update(zip(VEC_NAMES, group))
    arrived = _chip_exchange_wait(rs_in, sg, "chip_exchange_wait_in")
    update("w_in", [own_pair[None], arrived])

    loss = vec_all[LOSS_ROW, 0]
    return (loss, dx.reshape(x.shape), *[grads[k] for k in names], *[deltas[k] for k in names],
            *[new_m[k] for k in names], *[new_v[k] for k in names])
```

```python
import functools
import math

import jax
import jax.numpy as jnp
import numpy as np
from jax import lax
from jax.experimental import pallas as pl
from jax.experimental.pallas import tpu as pltpu

F32 = jnp.float32
BF16 = jnp.bfloat16

D_MODEL = 1024
FOX_W = 512
HEADS = 8
HEAD_DIM = 64
SGU_W = 512
GROUPS = 8
WINDOW = 128
CHUNK = 64
D_FF = 2816
IN_COLS = 4616
EPS = 1e-6
N_DEV = 8
LOG2E = 1.4426950408889634
LN2 = 0.6931471805599453

C_Q, C_K, C_V, C_UV, C_G, C_F, C_END = 0, 512, 1024, 1536, 2560, 4608, 4736

ADAM_LR, ADAM_B1, ADAM_B2, ADAM_EPS, ADAM_WD, ADAM_STEP = 0.001, 0.9, 0.999, 1e-08, 0.01, 10

MIB = 1024 * 1024
TOKEN_TILE = 256
ATTN_TILE = 256
SLAB_W = HEADS * 128
QT_ROWS = 72

BLK = IN_COLS // N_DEV
F_LO = 3 * FOX_W
F_DEV = F_LO // BLK
F_AT = F_LO - F_DEV * BLK
BLK_AT = [BLK * j - (HEADS if BLK * j > F_LO else 0) for j in range(N_DEV)]
FRAME_START = [a // 16 * 16 for a in BLK_AT]
FRAME = 608
FRAME_ROWS = FRAME + 16


def _params(vmem_mib, n_axes):
    return pltpu.CompilerParams(
        dimension_semantics=("arbitrary",) * n_axes, vmem_limit_bytes=vmem_mib * MIB)


def _const_spec(shape):
    nd = len(shape)
    return pl.BlockSpec(shape, lambda *_: (0,) * nd)


def _row_spec(tm, cols):
    return pl.BlockSpec((tm, cols), lambda i: (i, 0))


def _tile_spec(rows, tm):
    return pl.BlockSpec((1, rows, tm), lambda i: (i, 0, 0))


def _split3_dot(x, e):
    x1 = x.astype(BF16)
    r1 = x - x1.astype(F32)
    x2 = r1.astype(BF16)
    x3 = (r1 - x2.astype(F32)).astype(BF16)
    dot = functools.partial(jnp.dot, preferred_element_type=F32)
    return dot(x1, e) + dot(x2, e) + dot(x3, e)


def _tri_dot(tri, x):
    x1 = x.astype(BF16)
    r1 = x - x1.astype(F32)
    x2 = r1.astype(BF16)
    x3 = (r1 - x2.astype(F32)).astype(BF16)
    dot = functools.partial(jnp.dot, preferred_element_type=F32)
    return dot(tri, x1) + dot(tri, x2) + dot(tri, x3)


def _seg_mean(sq, bd_ref):
    hi = sq.astype(BF16)
    lo = (sq - hi.astype(F32)).astype(BF16)
    bd = bd_ref[...]
    dot = functools.partial(jnp.dot, preferred_element_type=F32)
    pairs = [dot(hi[:, p * 128:(p + 1) * 128], bd) + dot(lo[:, p * 128:(p + 1) * 128], bd)
             for p in range(HEADS // 2)]
    return jnp.concatenate(pairs, axis=1) * (1.0 / HEAD_DIM)


def _slabs_from_heads(t):
    lane = lax.broadcasted_iota(jnp.int32, (t.shape[0], 128), 1)
    low = lane < HEAD_DIM
    slabs = []
    for p in range(HEADS // 2):
        pair = t[:, p * 128:(p + 1) * 128]
        slabs.append(jnp.where(low, pair, 0.0))
        slabs.append(jnp.where(low, pltpu.roll(pair, HEAD_DIM, 1), 0.0))
    return jnp.concatenate(slabs, axis=1)


def _dot_nt(a, b):
    return lax.dot_general(a, b, (((1,), (1,)), ((), ())), preferred_element_type=F32)


def _dot_tn(a, b):
    return lax.dot_general(a, b, (((0,), (0,)), ((), ())), preferred_element_type=F32)


def _sigmoid(x):
    return 0.5 * jnp.tanh(0.5 * x) + 0.5


_GELU_C = math.sqrt(2.0 / math.pi)


def _gelu_and_grad(x):
    inner = _GELU_C * (x + 0.044715 * x * x * x)
    t = jnp.tanh(inner)
    y = 0.5 * x * (1.0 + t)
    dy = 0.5 * (1.0 + t) + 0.5 * x * (1.0 - t * t) * _GELU_C * (1.0 + 3.0 * 0.044715 * x * x)
    return y, dy


def _rms_bwd(xin, r, g, dy):
    dyg = dy * g
    return r * dyg - xin * (r * r * r) * jnp.mean(dyg * xin, axis=-1, keepdims=True)


def _mesh_pos():
    x, y, c = lax.axis_index("x"), lax.axis_index("y"), lax.axis_index("c")
    return x, y, c


def _peer(k):
    x, y, c = _mesh_pos()
    px = (1 - x) if (k >> 2) & 1 else x
    py = (1 - y) if (k >> 1) & 1 else y
    pc = (1 - c) if k & 1 else c
    return (px, py, pc), 4 * px + 2 * py + pc


def _frame_start(j):
    at = BLK * j - jnp.where(BLK * j > F_LO, HEADS, 0)
    return pl.multiple_of(at // 16 * 16, 16)


def _gather_w_in(frame):
    def body(x_ref, out_ref, zone, send_sems, recv_sems, local_sem):
        x, y, c = _mesh_pos()
        me, sibling = (x, y, c), (x, y, 1 - c)
        chips = [(1 - x, y), (x, 1 - y), (1 - x, 1 - y)]

        def index(px, py, pc):
            return 4 * px + 2 * py + pc

        def copy(k, block, to, src=None):
            return pltpu.make_async_remote_copy(
                src_ref=zone.at[index(*block)] if src is None else src,
                dst_ref=zone.at[index(*block)],
                send_sem=send_sems.at[k], recv_sem=recv_sems.at[k],
                device_id=to, device_id_type=pl.DeviceIdType.MESH)

        def add(block):
            j = index(*block)
            rows = pl.ds(_frame_start(j), FRAME)
            out_ref[rows, :] = (out_ref[rows, :].astype(F32)
                                + zone[j, :FRAME, :].astype(F32)).astype(BF16)
            tail = slice(C_F, C_F + FRAME_ROWS - FRAME)
            forget = zone[j, FRAME:, :].astype(F32) * (j == F_DEV).astype(F32)
            out_ref[tail, :] = (out_ref[tail, :].astype(F32) + forget).astype(BF16)

        mine = pltpu.make_async_copy(x_ref, zone.at[index(*me)], local_sem)
        mine.start()
        first = [copy(1 + j, me, (*chip, c), src=x_ref) for j, chip in enumerate(chips)]
        first.append(copy(0, me, sibling, src=x_ref))
        for cp in first:
            cp.start()
        out_ref[...] = jnp.zeros_like(out_ref)
        mine.wait()
        add(me)
        passed = [copy(4 + j, (*chip, c), sibling) for j, chip in enumerate(chips)]
        for j, chip in enumerate(chips):
            copy(1 + j, (*chip, c), me).wait_recv()
            passed[j].start()
            add((*chip, c))
        copy(0, sibling, me).wait_recv()
        add(sibling)
        for j, chip in enumerate(chips):
            copy(4 + j, (*chip, 1 - c), me).wait_recv()
            add((*chip, 1 - c))
        for cp in first + passed:
            cp.wait_send()

    return pl.pallas_call(
        body, name="gather_w_in", out_shape=jax.ShapeDtypeStruct((C_END, frame.shape[1]), BF16),
        in_specs=[pl.BlockSpec(memory_space=pl.ANY)],
        out_specs=pl.BlockSpec(memory_space=pltpu.VMEM),
        scratch_shapes=[pltpu.VMEM((N_DEV,) + frame.shape, BF16),
                        pltpu.SemaphoreType.DMA((7,)), pltpu.SemaphoreType.DMA((7,)),
                        pltpu.SemaphoreType.DMA],
        compiler_params=pltpu.CompilerParams(vmem_limit_bytes=40 * MIB),
    )(frame)


def _chip_peer(k):
    x, y, c = _mesh_pos()
    px = (1 - x) if (k >> 1) & 1 else x
    py = (1 - y) if k & 1 else y
    return (px, py, c), 2 * px + py


def _pair_sums(dw_cat, name, after):
    rows, cols = FRAME_ROWS, dw_cat.shape[1]
    n_chips = N_DEV // 2

    def pieces(p_ref, j):
        return (p_ref.at[pl.ds(_frame_start(j), FRAME)], p_ref.at[pl.ds(C_F, FRAME_ROWS - FRAME)])

    def body(p_ref, after_ref, send_ref, own_ref, mine_buf, sib_buf, send_sems, recv_sems,
             local_sems):
        x, y, c = _mesh_pos()
        sibling = (x, y, 1 - c)
        copies, local = [], []
        for q in range(n_chips):
            for part, (lo, hi) in enumerate(((0, FRAME), (FRAME, FRAME_ROWS))):
                cp = pltpu.make_async_remote_copy(
                    src_ref=pieces(p_ref, 2 * q + (1 - c))[part], dst_ref=sib_buf.at[q, lo:hi],
                    send_sem=send_sems.at[2 * q + part], recv_sem=recv_sems.at[2 * q + part],
                    device_id=sibling, device_id_type=pl.DeviceIdType.MESH)
                cp.start()
                copies.append(cp)
                lc = pltpu.make_async_copy(pieces(p_ref, 2 * q + c)[part], mine_buf.at[q, lo:hi],
                                           local_sems.at[2 * q + part])
                lc.start()
                local.append(lc)
        for lc in local:
            lc.wait()
        for cp in copies:
            cp.wait_recv()
        for k in range(1, n_chips):
            _, q = _chip_peer(k)
            send_ref[k - 1] = (mine_buf[q].astype(F32) + sib_buf[q].astype(F32)).astype(BF16)
        my_chip = 2 * x + y
        own_ref[...] = mine_buf[my_chip].astype(F32) + sib_buf[my_chip].astype(F32)
        for cp in copies:
            cp.wait_send()

    vmem = pl.BlockSpec(memory_space=pltpu.VMEM)
    return pl.pallas_call(
        body, name=name,
        out_shape=[jax.ShapeDtypeStruct((n_chips - 1, rows, cols), BF16),
                   jax.ShapeDtypeStruct((rows, cols), F32)],
        in_specs=[pl.BlockSpec(memory_space=pl.ANY)] * 2, out_specs=[vmem, vmem],
        scratch_shapes=[pltpu.VMEM((n_chips, rows, cols), BF16),
                        pltpu.VMEM((n_chips, rows, cols), BF16),
                        pltpu.SemaphoreType.DMA((2 * n_chips,)),
                        pltpu.SemaphoreType.DMA((2 * n_chips,)),
                        pltpu.SemaphoreType.DMA((2 * n_chips,))],
        compiler_params=pltpu.CompilerParams(vmem_limit_bytes=40 * MIB),
    )(dw_cat, after)


def _chip_copy(src_ref, land_ref, send_sem, recv_sem, k):
    peer, _ = _chip_peer(k)
    return pltpu.make_async_remote_copy(
        src_ref=src_ref.at[k - 1], dst_ref=land_ref.at[k - 1], send_sem=send_sem, recv_sem=recv_sem,
        device_id=peer, device_id_type=pl.DeviceIdType.MESH)


def _chip_exchange_start(blocks, name):
    hbm = pl.BlockSpec(memory_space=pltpu.HBM)
    sem = pl.BlockSpec(memory_space=pltpu.SEMAPHORE)
    n_peers = blocks.shape[0]

    def body(src_ref, zone_ref, send_sems, recv_sems, src_thru, zone_thru, token):
        for k in range(1, n_peers + 1):
            _chip_copy(src_ref, zone_ref, send_sems.at[k - 1], recv_sems.at[k - 1], k).start()
        token[...] = jnp.zeros_like(token)

    outs = pl.pallas_call(
        body, name=name, in_specs=[hbm, hbm],
        out_shape=[pltpu.SemaphoreType.DMA((n_peers,)), pltpu.SemaphoreType.DMA((n_peers,)),
                   pltpu.HBM(blocks.shape, blocks.dtype), pltpu.HBM(blocks.shape, blocks.dtype),
                   jax.ShapeDtypeStruct((8, 128), F32)],
        out_specs=[sem, sem, hbm, hbm, pl.BlockSpec(memory_space=pltpu.VMEM)],
        input_output_aliases={0: 2, 1: 3},
        compiler_params=pltpu.CompilerParams(
            has_side_effects=pltpu.SideEffectType.DATAFLOW_SIDE_EFFECTING),
    )(pltpu.with_memory_space_constraint(blocks, pltpu.HBM),
      pltpu.with_memory_space_constraint(lax.empty(blocks.shape, blocks.dtype), pltpu.HBM))
    return outs[:4], outs[4]


def _chip_exchange_wait(handle, after, name):
    send_sems, recv_sems, src, zone = handle
    hbm = pl.BlockSpec(memory_space=pltpu.HBM)
    sem = pl.BlockSpec(memory_space=pltpu.SEMAPHORE)

    def body(src_ref, zone_ref, ssem, rsem, after_ref, src_out, zone_out):
        for k in range(1, src.shape[0] + 1):
            cp = _chip_copy(src_ref, zone_ref, ssem.at[k - 1], rsem.at[k - 1], k)
            cp.wait_send()
            cp.wait_recv()

    outs = pl.pallas_call(
        body, name=name,
        in_specs=[hbm, hbm, sem, sem, pl.BlockSpec(memory_space=pl.ANY)],
        out_shape=[pltpu.HBM(src.shape, src.dtype), pltpu.HBM(zone.shape, zone.dtype)],
        out_specs=[hbm, hbm], input_output_aliases={0: 0, 1: 1},
        compiler_params=pltpu.CompilerParams(
            has_side_effects=pltpu.SideEffectType.DATAFLOW_SIDE_EFFECTING),
    )(src, zone, send_sems, recv_sems, after)
    return outs[1]


def _remote_copy(gather, src_ref, land_ref, send_sem, recv_sem, k, receive_side):
    x, y, c = _mesh_pos()
    me = 4 * x + 2 * y + c
    peer, pidx = _peer(k)
    return pltpu.make_async_remote_copy(
        src_ref=src_ref if gather else src_ref.at[pidx],
        dst_ref=land_ref.at[pidx if receive_side else me],
        send_sem=send_sem, recv_sem=recv_sem,
        device_id=peer, device_id_type=pl.DeviceIdType.MESH)


def _exchange_start(groups, name, gather, after=()):
    arrs = [a for g in groups for a in g]
    n, n_groups = len(arrs), len(groups)
    lands = [jax.ShapeDtypeStruct(((N_DEV,) + a.shape) if gather else a.shape, a.dtype)
             for a in arrs]

    def body(*refs):
        srcs, zones = refs[:n], refs[n:2 * n]
        outs_at = 2 * n + len(after)
        sems = refs[outs_at:outs_at + 2 * n_groups]
        token = refs[-1]
        a = 0
        for gi, g in enumerate(groups):
            send_sems, recv_sems = sems[2 * gi], sems[2 * gi + 1]
            for k in range(1, N_DEV):
                for ai in range(len(g)):
                    slot = ai * (N_DEV - 1) + k - 1
                    _remote_copy(gather, srcs[a + ai], zones[a + ai], send_sems.at[slot],
                                 recv_sems.at[slot], k, False).start()
            a += len(g)
        token[...] = jnp.zeros_like(token)

    hbm = pl.BlockSpec(memory_space=pltpu.HBM)
    sem = pl.BlockSpec(memory_space=pltpu.SEMAPHORE)
    sem_shapes = []
    for g in groups:
        sem_shapes += [pltpu.SemaphoreType.DMA((len(g) * (N_DEV - 1),))] * 2
    outs = pl.pallas_call(
        body, name=name,
        in_specs=[hbm] * (2 * n) + [pl.BlockSpec(memory_space=pl.ANY)] * len(after),
        out_shape=sem_shapes + [pltpu.HBM(a.shape, a.dtype) for a in arrs]
        + [pltpu.HBM(z.shape, z.dtype) for z in lands] + [jax.ShapeDtypeStruct((8, 128), F32)],
        out_specs=[sem] * (2 * n_groups) + [hbm] * (2 * n)
        + [pl.BlockSpec(memory_space=pltpu.VMEM)],
        input_output_aliases={i: 2 * n_groups + i for i in range(2 * n)},
        compiler_params=pltpu.CompilerParams(
            has_side_effects=pltpu.SideEffectType.DATAFLOW_SIDE_EFFECTING),
    )(*[pltpu.with_memory_space_constraint(a, pltpu.HBM) for a in arrs],
      *[pltpu.with_memory_space_constraint(lax.empty(z.shape, z.dtype), pltpu.HBM) for z in lands],
      *after)
    sems = outs[:2 * n_groups]
    thru = outs[2 * n_groups:2 * n_groups + n]
    zones = outs[2 * n_groups + n:2 * n_groups + 2 * n]
    handles, a = [], 0
    for gi, g in enumerate(groups):
        handles.append((sems[2 * gi], sems[2 * gi + 1], thru[a:a + len(g)], zones[a:a + len(g)]))
        a += len(g)
    return handles, outs[-1]


def _exchange_wait(handle, after, name, gather):
    send_sems, recv_sems, thru, zones = handle
    n = len(thru)

    def body(*refs):
        srcs, lands = refs[:n], refs[n:2 * n]
        ssem, rsem = refs[2 * n], refs[2 * n + 1]
        for k in range(1, N_DEV):
            for ai in range(n):
                slot = ai * (N_DEV - 1) + k - 1
                cp = _remote_copy(gather, srcs[ai], lands[ai], ssem.at[slot], rsem.at[slot], k, True)
                cp.wait_send()
                cp.wait_recv()

    hbm = pl.BlockSpec(memory_space=pltpu.HBM)
    sem = pl.BlockSpec(memory_space=pltpu.SEMAPHORE)
    outs = pl.pallas_call(
        body, name=name,
        in_specs=[hbm] * (2 * n) + [sem, sem, pl.BlockSpec(memory_space=pl.ANY)],
        out_shape=[pltpu.HBM(a.shape, a.dtype) for a in thru]
        + [pltpu.HBM(z.shape, z.dtype) for z in zones],
        out_specs=[hbm] * (2 * n),
        input_output_aliases={i: i for i in range(2 * n)},
        compiler_params=pltpu.CompilerParams(
            has_side_effects=pltpu.SideEffectType.DATAFLOW_SIDE_EFFECTING),
    )(*thru, *zones, send_sems, recv_sems, after)
    return outs[:n], outs[n:]


def _own_block(zone, block):
    x, y, c = _mesh_pos()
    me = 4 * x + 2 * y + c
    return lax.dynamic_update_slice_in_dim(zone, block[None], me, axis=0)


def _proj_fwd(x, g1, wcat, bdiag, gq, gk, bfor, tri, pdq, pdk, ones_q, ones_k):
    s_len = x.shape[0]
    tm = TOKEN_TILE
    nt = s_len // tm

    def body(x_ref, g1_ref, w_ref, bd_ref, gq_ref, gk_ref, bf_ref, tri_ref, pdq_ref,
             pdk_ref, oq_ref, ok_ref,
             h_ref, qa_ref, ka_ref, kat_ref, vs_ref, vt_ref, qr_ref, kr_ref, flog_ref, uv_ref,
             gp_ref, carry):
        @pl.when(pl.program_id(0) == 0)
        def _():
            carry[...] = jnp.zeros_like(carry)

        xf = x_ref[...]
        r = lax.rsqrt(jnp.mean(xf * xf, axis=-1, keepdims=True) + EPS)
        h = (xf * r * g1_ref[...]).astype(BF16)
        h_ref[...] = h
        dot = functools.partial(jnp.dot, preferred_element_type=F32)

        def proj(lo, hi):
            return _dot_nt(h, w_ref[lo:hi, :])

        flog = proj(C_F, C_END) + bf_ref[...]
        flog_ref[...] = flog
        lane = lax.broadcasted_iota(jnp.int32, flog.shape, 1)
        logf = jnp.minimum(flog, 0.0) - jnp.log(1.0 + jnp.exp(-jnp.abs(flog)))
        logf = jnp.where(lane < HEADS, logf, 0.0)
        dcum = _tri_dot(tri_ref[...], logf) + carry[...]
        carry[...] = dcum[tm - 1:tm, :]
        d2 = dcum * LOG2E
        d2a = d2.astype(BF16)
        rem = d2 - d2a.astype(F32)
        d2b = rem.astype(BF16)
        d2c = (rem - d2b.astype(F32)).astype(BF16)

        q = proj(C_Q, C_K)
        qr_ref[...] = q.astype(BF16)
        rq = lax.rsqrt(_seg_mean(q * q, bd_ref) + EPS)
        qn = q * rq * (gq_ref[...] * (HEAD_DIM ** -0.5 * LOG2E))
        d_parts = jnp.concatenate([d2a, d2b, d2c], axis=1)
        qa = _slabs_from_heads(qn) + dot(d_parts, pdq_ref[...]) + oq_ref[...]
        qa_ref[...] = qa.astype(BF16)

        k = proj(C_K, C_V)
        kr_ref[...] = k.astype(BF16)
        rk = lax.rsqrt(_seg_mean(k * k, bd_ref) + EPS)
        kn = k * rk * gk_ref[...]
        ka = _slabs_from_heads(kn) + dot(d_parts, pdk_ref[...]) + ok_ref[...]
        ka_ref[...] = ka.astype(BF16)
        kat_ref[0] = ka.T.astype(BF16)

        v = proj(C_V, C_UV)
        vs_ref[...] = _slabs_from_heads(v).astype(BF16)
        vt_ref[0] = v.T.astype(BF16)
        uv_ref[...] = proj(C_UV, C_G).astype(BF16)
        gp_ref[...] = proj(C_G, C_F).astype(BF16)

    outs = [((s_len, D_MODEL), BF16, _row_spec(tm, D_MODEL)),
            ((s_len, SLAB_W), BF16, _row_spec(tm, SLAB_W)),
            ((s_len, SLAB_W), BF16, _row_spec(tm, SLAB_W)),
            ((nt, SLAB_W, tm), BF16, _tile_spec(SLAB_W, tm)),
            ((s_len, SLAB_W), BF16, _row_spec(tm, SLAB_W)),
            ((nt, FOX_W, tm), BF16, _tile_spec(FOX_W, tm)),
            ((s_len, FOX_W), BF16, _row_spec(tm, FOX_W)),
            ((s_len, FOX_W), BF16, _row_spec(tm, FOX_W)),
            ((s_len, 128), F32, _row_spec(tm, 128)),
            ((s_len, 2 * SGU_W), BF16, _row_spec(tm, 2 * SGU_W)),
            ((s_len, 2 * D_MODEL), BF16, _row_spec(tm, 2 * D_MODEL))]
    return pl.pallas_call(
        body, name="proj_fwd", grid=(nt,),
        in_specs=[_row_spec(tm, D_MODEL), _const_spec((1, D_MODEL)), _const_spec(wcat.shape),
                  _const_spec(bdiag.shape), _const_spec((1, FOX_W)), _const_spec((1, FOX_W)),
                  _const_spec((1, 128)), _const_spec((tm, tm)), _const_spec(pdq.shape), _const_spec(pdk.shape), _const_spec(ones_q.shape),
                  _const_spec(ones_k.shape)],
        out_specs=[o[2] for o in outs],
        out_shape=[jax.ShapeDtypeStruct(o[0], o[1]) for o in outs],
        scratch_shapes=[pltpu.VMEM((1, 128), F32)],
        compiler_params=_params(56, 1),
    )(x, g1, wcat, bdiag, gq, gk, bfor, tri, pdq, pdk, ones_q, ones_k)


def _attn_fwd(qa, ka, vt):
    s_len = qa.shape[0]
    t = ATTN_TILE
    nb = s_len // t

    def body(q_ref, k_hbm, vt_hbm, o_ref, ot_ref, lse_ref, m_sc, l_sc, acc_sc, s_sc, mcur_sc,
             alpha_sc, k_ref, vt_ref, kv_sems):
        i = pl.program_id(0)

        def loads(b):
            rows = pl.ds(pl.multiple_of(b * t, t), t)
            return (pltpu.make_async_copy(k_hbm.at[rows], k_ref.at[rows], kv_sems.at[0, b]),
                    pltpu.make_async_copy(vt_hbm.at[b], vt_ref.at[b], kv_sems.at[1, b]))

        @pl.when(i == 0)
        def _():
            for b in range(nb):
                for cp in loads(b):
                    cp.start()

        for cp in loads(i):
            cp.wait()
        m_sc[...] = jnp.full_like(m_sc, -jnp.inf)
        l_sc[...] = jnp.zeros_like(l_sc)
        acc_sc[...] = jnp.zeros_like(acc_sc)

        def logits(j, slot, masked):
            krows = pl.ds(pl.multiple_of(j * t, t), t)
            if masked:
                keep = (lax.broadcasted_iota(jnp.int32, (t, t), 0)
                        <= lax.broadcasted_iota(jnp.int32, (t, t), 1))
            for hd in range(HEADS):
                sl = slice(hd * 128, (hd + 1) * 128)
                st = _dot_nt(k_ref[krows, sl], q_ref[:, sl])
                if masked:
                    st = jnp.where(keep, st, -jnp.inf)
                s_sc[slot, hd] = st
                m_prev = m_sc[hd:hd + 1, :]
                m_new = jnp.maximum(m_prev, jnp.max(st, axis=0, keepdims=True))
                alpha_sc[slot, hd:hd + 1, :] = jnp.exp2(m_prev - m_new)
                mcur_sc[slot, hd:hd + 1, :] = m_new
                m_sc[hd:hd + 1, :] = m_new

        def accumulate(j, slot):
            for hd in range(HEADS):
                hr = slice(hd * HEAD_DIM, (hd + 1) * HEAD_DIM)
                alpha = alpha_sc[slot, hd:hd + 1, :]
                pt = jnp.exp2(s_sc[slot, hd] - mcur_sc[slot, hd:hd + 1, :])
                l_sc[hd:hd + 1, :] = alpha * l_sc[hd:hd + 1, :] + jnp.sum(pt, axis=0, keepdims=True)
                acc_sc[hr, :] = alpha * acc_sc[hr, :] + jnp.dot(
                    vt_ref[j, hr, :], pt.astype(BF16), preferred_element_type=F32)

        @pl.when(i == 0)
        def _():
            logits(0, 0, True)
            accumulate(0, 0)

        pairs = (i - 1) // 2

        @pl.when(i > 0)
        def _():
            logits(0, 0, False)

            def two_blocks(p, carry):
                logits(2 * p + 1, 1, False)
                accumulate(2 * p, 0)
                logits(2 * p + 2, 0, False)
                accumulate(2 * p + 1, 1)
                return carry

            lax.fori_loop(0, pairs, two_blocks, 0)

        @pl.when((i > 0) & (i - 2 * pairs == 1))
        def _():
            logits(i, 1, True)
            accumulate(i - 1, 0)
            accumulate(i, 1)

        @pl.when((i > 0) & (i - 2 * pairs == 2))
        def _():
            logits(i - 1, 1, False)
            accumulate(i - 2, 0)
            logits(i, 0, True)
            accumulate(i - 1, 1)
            accumulate(i, 0)

        for hd in range(HEADS):
            hr = slice(hd * HEAD_DIM, (hd + 1) * HEAD_DIM)
            l = l_sc[hd:hd + 1, :]
            acc_sc[hr, :] = acc_sc[hr, :] / l
            lse_ref[0, hd:hd + 1, :] = m_sc[hd:hd + 1, :] + jnp.log2(l)
        o_ref[...] = acc_sc[...].T.astype(BF16)
        ot_ref[...] = acc_sc[...].astype(BF16)

    return pl.pallas_call(
        body, name="attn_fwd", grid=(nb,),
        in_specs=[_row_spec(t, SLAB_W), pl.BlockSpec(memory_space=pl.ANY),
                  pl.BlockSpec(memory_space=pl.ANY)],
        out_specs=[_row_spec(t, FOX_W), pl.BlockSpec((FOX_W, t), lambda i: (0, i)),
                   _tile_spec(HEADS, t)],
        out_shape=[jax.ShapeDtypeStruct((s_len, FOX_W), BF16),
                   jax.ShapeDtypeStruct((FOX_W, s_len), BF16),
                   jax.ShapeDtypeStruct((nb, HEADS, t), F32)],
        scratch_shapes=[pltpu.VMEM((HEADS, t), F32), pltpu.VMEM((HEADS, t), F32),
                        pltpu.VMEM((FOX_W, t), F32), pltpu.VMEM((2, HEADS, t, t), F32),
                        pltpu.VMEM((2, HEADS, t), F32), pltpu.VMEM((2, HEADS, t), F32),
                        pltpu.VMEM(ka.shape, BF16), pltpu.VMEM(vt.shape, BF16),
                        pltpu.SemaphoreType.DMA((2, nb))],
        compiler_params=_params(48, 1),
    )(qa, ka, vt)


def _sgu_mix(vn, ws_ref):
    tm = vn.shape[0]
    lane = lax.broadcasted_iota(jnp.int32, (WINDOW, 128), 1)
    low = lane < HEAD_DIM
    wins = []
    for w in range(tm // WINDOW):
        slabs = []
        for p in range(GROUPS // 2):
            v2 = vn[w * WINDOW:(w + 1) * WINDOW, p * 128:(p + 1) * 128]
            lo = jnp.where(low, v2, 0.0).astype(BF16)
            hi = jnp.where(low, 0.0, v2).astype(BF16)
            slabs.append(jnp.dot(ws_ref[2 * p], lo, preferred_element_type=F32)
                         + jnp.dot(ws_ref[2 * p + 1], hi, preferred_element_type=F32))
        wins.append(jnp.concatenate(slabs, axis=1))
    return jnp.concatenate(wins, axis=0) if len(wins) > 1 else wins[0]


def _layernorm_fwd(vv, g, b):
    mu = jnp.mean(vv, axis=-1, keepdims=True)
    xc = vv - mu
    r = lax.rsqrt(jnp.mean(xc * xc, axis=-1, keepdims=True) + EPS)
    xh = xc * r
    return xh * g + b, xh, r


def _mix_fwd(attn, uvpre, gpre, x, wa, wb, wout, wsm, bsf, gsgu, bsgu, gpost):
    s_len = x.shape[0]
    tm = TOKEN_TILE

    def body(o_ref, uv_ref, gp_ref, x_ref, wa_ref, wb_ref, wo_ref, ws_ref, bs_ref, gs_ref, bsg_ref,
             gpost_ref, sgut_ref, ya_ref, yb_ref, mgt_ref, om_ref, x1_ref):
        uvp = uv_ref[...].astype(F32)
        uv, _ = _gelu_and_grad(uvp)
        u, vv = uv[:, :SGU_W], uv[:, SGU_W:]
        vn, _, _ = _layernorm_fwd(vv, gs_ref[...], bsg_ref[...])
        bias = bs_ref[...]
        if tm > WINDOW:
            bias = jnp.concatenate([bias] * (tm // WINDOW), axis=0)
        mixed = _sgu_mix(vn, ws_ref) + bias
        sgu_f = u * mixed
        sgu = sgu_f.astype(BF16)
        sgut_ref[...] = sgu_f.T.astype(BF16)
        ya = jnp.dot(o_ref[...], wa_ref[...], preferred_element_type=F32)
        yb = jnp.dot(sgu, wb_ref[...], preferred_element_type=F32)
        ya_ref[...] = ya.astype(BF16)
        yb_ref[...] = yb.astype(BF16)
        gates = _sigmoid(gp_ref[...].astype(F32))
        merged_f = gates[:, :D_MODEL] * ya + gates[:, D_MODEL:] * yb
        merged = merged_f.astype(BF16)
        mgt_ref[...] = merged_f.T.astype(BF16)
        om = jnp.dot(merged, wo_ref[...], preferred_element_type=F32)
        om_ref[...] = om
        r = lax.rsqrt(jnp.mean(om * om, axis=-1, keepdims=True) + EPS)
        x1_ref[...] = x_ref[...] + om * r * gpost_ref[...]

    def t_out(rows):
        return ((rows, s_len), BF16, pl.BlockSpec((rows, tm), lambda i: (0, i)))

    def r_out(cols, dt):
        return ((s_len, cols), dt, _row_spec(tm, cols))

    outs = [t_out(SGU_W), r_out(D_MODEL, BF16), r_out(D_MODEL, BF16), t_out(D_MODEL),
            r_out(D_MODEL, F32), r_out(D_MODEL, F32)]
    return pl.pallas_call(
        body, name="mix_fwd", grid=(s_len // tm,),
        in_specs=[_row_spec(tm, FOX_W), _row_spec(tm, 2 * SGU_W), _row_spec(tm, 2 * D_MODEL),
                  _row_spec(tm, D_MODEL), _const_spec(wa.shape), _const_spec(wb.shape),
                  _const_spec(wout.shape), _const_spec(wsm.shape), _const_spec(bsf.shape),
                  _const_spec((1, SGU_W)), _const_spec((1, SGU_W)), _const_spec((1, D_MODEL))],
        out_specs=[o[2] for o in outs],
        out_shape=[jax.ShapeDtypeStruct(o[0], o[1]) for o in outs],
        compiler_params=_params(48, 1),
    )(attn, uvpre, gpre, x, wa, wb, wout, wsm, bsf, gsgu, bsgu, gpost)


def _ffn_fwd_bwd(x1, tgt, wffn, wdown, gpre, gpost):
    s_len = x1.shape[0]
    tm = TOKEN_TILE

    def body(x1_ref, t_ref, wi_hbm, wd_hbm, gpre_ref, gpost_ref,
             dx1_ref, h2_ref, actt_ref, dff_ref, dgut_ref, loss_ref, dgpost_ref, dgpre_ref,
             wi_ref, wd_ref, w_sems):
        first = pl.program_id(0) == 0
        loads = [pltpu.make_async_copy(wi_hbm.at[:D_FF], wi_ref.at[:D_FF], w_sems.at[0]),
                 pltpu.make_async_copy(wi_hbm.at[D_FF:], wi_ref.at[D_FF:], w_sems.at[1]),
                 pltpu.make_async_copy(wd_hbm, wd_ref, w_sems.at[2])]

        def arrived(i):
            @pl.when(first)
            def _():
                loads[i].wait()

        @pl.when(first)
        def _():
            for cp in loads:
                cp.start()
            loss_ref[...] = jnp.zeros_like(loss_ref)
            dgpost_ref[...] = jnp.zeros_like(dgpost_ref)
            dgpre_ref[...] = jnp.zeros_like(dgpre_ref)

        x1v = x1_ref[...]
        r2 = lax.rsqrt(jnp.mean(x1v * x1v, axis=-1, keepdims=True) + EPS)
        gpre_v = gpre_ref[...]
        h2 = (x1v * r2 * gpre_v).astype(BF16)
        h2_ref[...] = h2
        arrived(0)
        gg = _dot_nt(h2, wi_ref[:D_FF, :])
        arrived(1)
        uu = _dot_nt(h2, wi_ref[D_FF:, :])
        sg = _sigmoid(gg)
        silu = gg * sg
        act_f = silu * uu
        act = act_f.astype(BF16)
        actt_ref[...] = act_f.T.astype(BF16)
        arrived(2)
        ff = jnp.dot(act, wd_ref[...], preferred_element_type=F32)
        r3 = lax.rsqrt(jnp.mean(ff * ff, axis=-1, keepdims=True) + EPS)
        gpost_v = gpost_ref[...]
        y = x1v + ff * r3 * gpost_v
        err = y - t_ref[...]
        loss_ref[...] += jnp.sum(err * err) * (0.5 / D_MODEL)
        dy = err * (1.0 / D_MODEL)
        dgpost_ref[...] += jnp.sum(dy * ff * r3, axis=0, keepdims=True)
        dff = _rms_bwd(ff, r3, gpost_v, dy).astype(BF16)
        dff_ref[...] = dff
        dact = _dot_nt(dff, wd_ref[...])
        dgg_f = dact * uu * (sg * (1.0 + gg * (1.0 - sg)))
        duu_f = dact * silu
        dgg = dgg_f.astype(BF16)
        duu = duu_f.astype(BF16)
        dgut_ref[:D_FF, :] = dgg_f.T.astype(BF16)
        dgut_ref[D_FF:, :] = duu_f.T.astype(BF16)
        dh2 = (jnp.dot(dgg, wi_ref[:D_FF, :], preferred_element_type=F32)
               + jnp.dot(duu, wi_ref[D_FF:, :], preferred_element_type=F32))
        dgpre_ref[...] += jnp.sum(dh2 * x1v * r2, axis=0, keepdims=True)
        dx1_ref[...] = dy + _rms_bwd(x1v, r2, gpre_v, dh2)

    outs = [((s_len, D_MODEL), F32, _row_spec(tm, D_MODEL)),
            ((s_len, D_MODEL), BF16, _row_spec(tm, D_MODEL)),
            ((D_FF, s_len), BF16, pl.BlockSpec((D_FF, tm), lambda i: (0, i))),
            ((s_len, D_MODEL), BF16, _row_spec(tm, D_MODEL)),
            ((2 * D_FF, s_len), BF16, pl.BlockSpec((2 * D_FF, tm), lambda i: (0, i))),
            ((1, 128), F32, _const_spec((1, 128))),
            ((1, D_MODEL), F32, _const_spec((1, D_MODEL))),
            ((1, D_MODEL), F32, _const_spec((1, D_MODEL)))]
    return pl.pallas_call(
        body, name="ffn_fwd_bwd", grid=(s_len // tm,),
        in_specs=[_row_spec(tm, D_MODEL), _row_spec(tm, D_MODEL),
                  pl.BlockSpec(memory_space=pl.ANY), pl.BlockSpec(memory_space=pl.ANY),
                  _const_spec((1, D_MODEL)), _const_spec((1, D_MODEL))],
        out_specs=[o[2] for o in outs],
        out_shape=[jax.ShapeDtypeStruct(o[0], o[1]) for o in outs],
        scratch_shapes=[pltpu.VMEM(wffn.shape, BF16), pltpu.VMEM(wdown.shape, BF16),
                        pltpu.SemaphoreType.DMA((3,))],
        compiler_params=_params(60, 1),
    )(x1, tgt, wffn, wdown, gpre, gpost)


def _mix_bwd(dx1, om, ya, yb, gpre, uvpre, attn, wout, wa, wb, wsm, wsmt, bsf, gsgu, bsgu, gpost,
             wmask, egrp):
    s_len = dx1.shape[0]
    tm = TOKEN_TILE
    nw = tm // WINDOW
    nt = s_len // tm

    def body(dx1_ref, om_ref, ya_ref, yb_ref, gp_ref, uv_ref, o_ref, wo_ref, wa_ref, wb_ref, ws_ref,
             wst_ref, bs_ref, gs_ref, bsg_ref, gpost_ref, mask_ref, eg_ref,
             dom_ref, dya_ref, dyb_ref, dgp_ref, dot_ref, delta_ref, duv_ref,
             dws_ref, dbs_ref, dgs_ref, dbsg_ref, dgpost_ref, dbs_acc):
        step = pl.program_id(0)

        @pl.when(step == 0)
        def _():
            dws_ref[...] = jnp.zeros_like(dws_ref)
            dbs_acc[...] = jnp.zeros_like(dbs_acc)
            dgs_ref[...] = jnp.zeros_like(dgs_ref)
            dbsg_ref[...] = jnp.zeros_like(dbsg_ref)
            dgpost_ref[...] = jnp.zeros_like(dgpost_ref)

        om = om_ref[...]
        dx1v = dx1_ref[...]
        r = lax.rsqrt(jnp.mean(om * om, axis=-1, keepdims=True) + EPS)
        gpost_v = gpost_ref[...]
        dgpost_ref[...] += jnp.sum(dx1v * om * r, axis=0, keepdims=True)
        dom = _rms_bwd(om, r, gpost_v, dx1v).astype(BF16)
        dom_ref[...] = dom
        dmg = _dot_nt(dom, wo_ref[...])

        gates = _sigmoid(gp_ref[...].astype(F32))
        ga, gb = gates[:, :D_MODEL], gates[:, D_MODEL:]
        yav, ybv = ya_ref[...].astype(F32), yb_ref[...].astype(F32)
        dya = (dmg * ga).astype(BF16)
        dyb = (dmg * gb).astype(BF16)
        dya_ref[...] = dya
        dyb_ref[...] = dyb
        dgp_ref[:, :D_MODEL] = (dmg * yav * ga * (1.0 - ga)).astype(BF16)
        dgp_ref[:, D_MODEL:] = (dmg * ybv * gb * (1.0 - gb)).astype(BF16)

        dat_t = _dot_nt(dya, wa_ref[...]).T.astype(BF16)
        dot_ref[0] = dat_t
        o_t = o_ref[...].astype(F32).T
        delta_ref[0] = jnp.sum((dat_t.astype(F32) * o_t).reshape(HEADS, HEAD_DIM, tm), axis=1)
        dsgu = _dot_nt(dyb, wb_ref[...])

        uvp = uv_ref[...].astype(F32)
        uv, guv = _gelu_and_grad(uvp)
        u, vv = uv[:, :SGU_W], uv[:, SGU_W:]
        gs_v = gs_ref[...]
        vn, xh, rln = _layernorm_fwd(vv, gs_v, bsg_ref[...])
        bias = bs_ref[...]
        if nw > 1:
            bias = jnp.concatenate([bias] * nw, axis=0)
        mixed = _sgu_mix(vn, ws_ref) + bias
        du = dsgu * mixed
        dmixed = dsgu * u

        lane = lax.broadcasted_iota(jnp.int32, (WINDOW, 128), 1)
        low = lane < HEAD_DIM
        dvn_wins = []
        for w in range(nw):
            rows = slice(w * WINDOW, (w + 1) * WINDOW)
            dbs_acc[...] += dmixed[rows, :]
            slabs = []
            for p in range(GROUPS // 2):
                cols = slice(p * 128, (p + 1) * 128)
                dm2 = dmixed[rows, cols]
                dlo = jnp.where(low, dm2, 0.0).astype(BF16)
                dhi = jnp.where(low, 0.0, dm2).astype(BF16)
                vn2 = vn[rows, cols].astype(BF16)
                dws_ref[2 * p] += _dot_nt(dlo, vn2)
                dws_ref[2 * p + 1] += _dot_nt(dhi, vn2)
                slabs.append(jnp.dot(wst_ref[2 * p], dlo, preferred_element_type=F32)
                             + jnp.dot(wst_ref[2 * p + 1], dhi, preferred_element_type=F32))
            dvn_wins.append(jnp.concatenate(slabs, axis=1))
        dvn = jnp.concatenate(dvn_wins, axis=0) if nw > 1 else dvn_wins[0]

        dgs_ref[...] += jnp.sum(dvn * xh, axis=0, keepdims=True)
        dbsg_ref[...] += jnp.sum(dvn, axis=0, keepdims=True)
        dxh = dvn * gs_v
        dvv = rln * (dxh - jnp.mean(dxh, axis=-1, keepdims=True)
                     - xh * jnp.mean(dxh * xh, axis=-1, keepdims=True))
        duv_ref[:, :SGU_W] = (du * guv[:, :SGU_W]).astype(BF16)
        duv_ref[:, SGU_W:] = (dvv * guv[:, SGU_W:]).astype(BF16)

        @pl.when(step == pl.num_programs(0) - 1)
        def _():
            for g in range(GROUPS):
                dws_ref[g] = dws_ref[g] * mask_ref[...]
            dbs_ref[...] = _split3_dot(dbs_acc[...], eg_ref[...])

    rows_out = [((s_len, D_MODEL), BF16, _row_spec(tm, D_MODEL)),
                ((s_len, D_MODEL), BF16, _row_spec(tm, D_MODEL)),
                ((s_len, D_MODEL), BF16, _row_spec(tm, D_MODEL)),
                ((s_len, 2 * D_MODEL), BF16, _row_spec(tm, 2 * D_MODEL)),
                ((nt, FOX_W, tm), BF16, _tile_spec(FOX_W, tm)),
                ((nt, HEADS, tm), F32, _tile_spec(HEADS, tm)),
                ((s_len, 2 * SGU_W), BF16, _row_spec(tm, 2 * SGU_W))]
    acc_out = [((GROUPS, WINDOW, WINDOW), F32), ((WINDOW, 128), F32), ((1, SGU_W), F32),
               ((1, SGU_W), F32), ((1, D_MODEL), F32)]
    return pl.pallas_call(
        body, name="mix_bwd", grid=(nt,),
        in_specs=[_row_spec(tm, D_MODEL), _row_spec(tm, D_MODEL), _row_spec(tm, D_MODEL),
                  _row_spec(tm, D_MODEL), _row_spec(tm, 2 * D_MODEL), _row_spec(tm, 2 * SGU_W),
                  _row_spec(tm, FOX_W), _const_spec(wout.shape), _const_spec(wa.shape),
                  _const_spec(wb.shape), _const_spec(wsm.shape), _const_spec(wsmt.shape),
                  _const_spec(bsf.shape), _const_spec((1, SGU_W)), _const_spec((1, SGU_W)),
                  _const_spec((1, D_MODEL)), _const_spec(wmask.shape), _const_spec(egrp.shape)],
        out_specs=[o[2] for o in rows_out] + [_const_spec(s) for s, _ in acc_out],
        out_shape=[jax.ShapeDtypeStruct(o[0], o[1]) for o in rows_out]
        + [jax.ShapeDtypeStruct(s, dt) for s, dt in acc_out],
        scratch_shapes=[pltpu.VMEM((WINDOW, SGU_W), F32)],
        compiler_params=_params(48, 1),
    )(dx1, om, ya, yb, gpre, uvpre, attn, wout, wa, wb, wsm, wsmt, bsf, gsgu, bsgu, gpost, wmask,
      egrp)


def _attn_bwd(qa, ka, kat, vs, dot_, lse, delta, ecol):
    s_len = qa.shape[0]
    t = ATTN_TILE
    nb = s_len // t

    def body(k_ref, kt_ref, vs_ref, q_ref, do_ref, lse_ref, dl_ref, ec_ref, gk_ref, dvt_ref,
             gqt_ref, csum_ref, p_sc, ds_sc):
        j = pl.program_id(0)

        @pl.when(j == 0)
        def _():
            gqt_ref[...] = jnp.zeros_like(gqt_ref)

        gk_ref[...] = jnp.zeros_like(gk_ref)
        dvt_ref[...] = jnp.zeros_like(dvt_ref)

        def probs(i, slot, masked):
            qrows = pl.ds(pl.multiple_of(i * t, t), t)
            if masked:
                keep = (lax.broadcasted_iota(jnp.int32, (t, t), 0)
                        <= lax.broadcasted_iota(jnp.int32, (t, t), 1))
            for hd in range(HEADS):
                sl = slice(hd * 128, (hd + 1) * 128)
                hr = slice(hd * HEAD_DIM, (hd + 1) * HEAD_DIM)
                st = _dot_nt(k_ref[:, sl], q_ref[qrows, sl])
                if masked:
                    st = jnp.where(keep, st, -jnp.inf)
                pt = jnp.exp2(st - lse_ref[i, hd:hd + 1, :])
                dpt = jnp.dot(vs_ref[:, hd * 128:hd * 128 + HEAD_DIM], do_ref[i, hr, :],
                              preferred_element_type=F32)
                p_sc[slot, hd] = pt.astype(BF16)
                ds_sc[slot, hd] = (pt * (dpt - dl_ref[i, hd:hd + 1, :])).astype(BF16)

        def grads(i, slot):
            qrows = pl.ds(pl.multiple_of(i * t, t), t)
            for hd in range(HEADS):
                sl = slice(hd * 128, (hd + 1) * 128)
                hr = slice(hd * HEAD_DIM, (hd + 1) * HEAD_DIM)
                dst = ds_sc[slot, hd]
                dvt_ref[0, hr, :] += _dot_nt(do_ref[i, hr, :], p_sc[slot, hd])
                gk_ref[:, sl] += jnp.dot(dst, q_ref[qrows, sl], preferred_element_type=F32)
                gqt_ref[i, hd * QT_ROWS:(hd + 1) * QT_ROWS, :] += jnp.dot(
                    kt_ref[0, hd * 128:hd * 128 + QT_ROWS, :], dst, preferred_element_type=F32)

        probs(j, 0, True)
        pairs = (nb - 1 - j) // 2

        def two_blocks(p, carry):
            i1 = j + 1 + 2 * p
            probs(i1, 1, False)
            grads(i1 - 1, 0)
            probs(i1 + 1, 0, False)
            grads(i1, 1)
            return carry

        lax.fori_loop(0, pairs, two_blocks, 0)

        @pl.when(nb - 1 - j - 2 * pairs == 0)
        def _():
            grads(nb - 1, 0)

        @pl.when(nb - 1 - j - 2 * pairs == 1)
        def _():
            probs(nb - 1, 1, False)
            grads(nb - 2, 0)
            grads(nb - 1, 1)

        csum_ref[...] = _split3_dot(gk_ref[...], ec_ref[...])

    return pl.pallas_call(
        body, name="attn_bwd", grid=(nb,),
        in_specs=[_row_spec(t, SLAB_W), _tile_spec(SLAB_W, t), _row_spec(t, SLAB_W),
                  _const_spec(qa.shape), _const_spec(dot_.shape), _const_spec(lse.shape),
                  _const_spec(delta.shape), _const_spec(ecol.shape)],
        out_specs=[_row_spec(t, SLAB_W), _tile_spec(FOX_W, t),
                   _const_spec((nb, HEADS * QT_ROWS, t)), _row_spec(t, 128)],
        out_shape=[jax.ShapeDtypeStruct((s_len, SLAB_W), F32),
                   jax.ShapeDtypeStruct((nb, FOX_W, t), F32),
                   jax.ShapeDtypeStruct((nb, HEADS * QT_ROWS, t), F32),
                   jax.ShapeDtypeStruct((s_len, 128), F32)],
        scratch_shapes=[pltpu.VMEM((2, HEADS, t, t), BF16), pltpu.VMEM((2, HEADS, t, t), BF16)],
        compiler_params=_params(60, 1),
    )(ka, kat, vs, qa, dot_, lse, delta, ecol)


def _rev_cumsum(col_sums, gqt, triu):
    s_len = col_sums.shape[0]
    tm = TOKEN_TILE
    n = s_len // tm

    def body(cs_ref, gqt_ref, tri_ref, o_ref, carry):
        @pl.when(pl.program_id(0) == 0)
        def _():
            carry[...] = jnp.zeros_like(carry)
        rows = [gqt_ref[0, hd * QT_ROWS + HEAD_DIM:hd * QT_ROWS + HEAD_DIM + 1, :]
                for hd in range(HEADS)]
        row_sums = jnp.concatenate(rows + [jnp.zeros((128 - HEADS, tm), F32)], axis=0).T
        out = _tri_dot(tri_ref[...], row_sums - cs_ref[...]) + carry[...]
        o_ref[...] = out
        carry[...] = out[0:1, :]

    return pl.pallas_call(
        body, name="rev_cumsum", grid=(n,),
        in_specs=[pl.BlockSpec((tm, 128), lambda i: (n - 1 - i, 0)),
                  pl.BlockSpec((1, HEADS * QT_ROWS, tm), lambda i: (n - 1 - i, 0, 0)),
                  _const_spec((tm, tm))],
        out_specs=pl.BlockSpec((tm, 128), lambda i: (n - 1 - i, 0)),
        out_shape=jax.ShapeDtypeStruct((s_len, 128), F32),
        scratch_shapes=[pltpu.VMEM((1, 128), F32)],
        compiler_params=_params(32, 1),
    )(col_sums, gqt, triu)


def _heads_from_slabs(slabs):
    lane = lax.broadcasted_iota(jnp.int32, slabs[0].shape, 1)
    low = lane < HEAD_DIM
    pairs = [jnp.where(low, slabs[2 * p], pltpu.roll(slabs[2 * p + 1], HEAD_DIM, 1))
             for p in range(HEADS // 2)]
    return jnp.concatenate(pairs, axis=1)


def _proj_bwd(gqt, gk, dvt, dlogf, flog, qraw, kraw, duv, dgp, x, dx1, wcat, bdiag, gq, gk_gain, g1,
              efold):
    s_len = x.shape[0]
    tm = TOKEN_TILE

    def body(gqt_ref, gkk_ref, dvt_ref, dlf_ref, flog_ref, qr_ref, kr_ref, duv_ref, dgp_ref, x_ref,
             dx1_ref, w_ref, bd_ref, gq_ref, gk_ref, g1_ref, ef_ref,
             dx_ref, dprojt_ref, dgq_ref, dgk_ref, dbf_ref, dg1_ref, gq_acc, gk_acc):
        step = pl.program_id(0)

        @pl.when(step == 0)
        def _():
            gq_acc[...] = jnp.zeros_like(gq_acc)
            gk_acc[...] = jnp.zeros_like(gk_acc)
            dbf_ref[...] = jnp.zeros_like(dbf_ref)
            dg1_ref[...] = jnp.zeros_like(dg1_ref)

        pad = jnp.zeros((128 - QT_ROWS, tm), F32)
        q_slabs = [jnp.concatenate([gqt_ref[0, hd * QT_ROWS:(hd + 1) * QT_ROWS, :], pad], axis=0).T
                   for hd in range(HEADS)]
        dqn = _heads_from_slabs(q_slabs)
        dkn = _heads_from_slabs([gkk_ref[:, hd * 128:(hd + 1) * 128] for hd in range(HEADS)])

        def head_bwd(raw_ref, dn, g_ref, acc):
            raw = raw_ref[...].astype(F32)
            r = lax.rsqrt(_seg_mean(raw * raw, bd_ref) + EPS)
            xhat = raw * r
            acc[0:1, :] += jnp.sum(dn * xhat, axis=0, keepdims=True)
            dyg = dn * g_ref[...]
            return r * (dyg - xhat * _seg_mean(dyg * xhat, bd_ref))

        dot = functools.partial(jnp.dot, preferred_element_type=F32)
        duv, dgp = duv_ref[...], dgp_ref[...]
        dprojt_ref[C_UV:C_G, :] = duv.astype(F32).T.astype(BF16)
        dprojt_ref[C_G:C_F, :] = dgp.astype(F32).T.astype(BF16)
        dh = dot(duv, w_ref[C_UV:C_G, :]) + dot(dgp, w_ref[C_G:C_F, :])

        dq = head_bwd(qr_ref, dqn * HEAD_DIM ** -0.5, gq_ref, gq_acc)
        dk = head_bwd(kr_ref, dkn * LN2, gk_ref, gk_acc)
        dv_t = dvt_ref[0]
        dfl = dlf_ref[...] * _sigmoid(-flog_ref[...])
        dbf_ref[...] += jnp.sum(dfl, axis=0, keepdims=True)
        dprojt_ref[C_Q:C_K, :] = dq.T.astype(BF16)
        dprojt_ref[C_K:C_V, :] = dk.T.astype(BF16)
        dprojt_ref[C_V:C_UV, :] = dv_t.astype(BF16)
        dprojt_ref[C_F:C_END, :] = dfl.T.astype(BF16)
        dh = (dh + dot(dq.astype(BF16), w_ref[C_Q:C_K, :]) + dot(dk.astype(BF16), w_ref[C_K:C_V, :])
              + dot(dv_t.T.astype(BF16), w_ref[C_V:C_UV, :])
              + dot(dfl.astype(BF16), w_ref[C_F:C_END, :]))
        xf = x_ref[...]
        r = lax.rsqrt(jnp.mean(xf * xf, axis=-1, keepdims=True) + EPS)
        dg1_ref[...] += jnp.sum(dh * xf * r, axis=0, keepdims=True)
        dx_ref[...] = dx1_ref[...] + _rms_bwd(xf, r, g1_ref[...], dh)

        @pl.when(step == pl.num_programs(0) - 1)
        def _():
            dgq_ref[...] = _split3_dot(gq_acc[...], ef_ref[...])
            dgk_ref[...] = _split3_dot(gk_acc[...], ef_ref[...])

    outs = [((s_len, D_MODEL), F32, _row_spec(tm, D_MODEL)),
            ((C_END, s_len), BF16, pl.BlockSpec((C_END, tm), lambda i: (0, i))),
            ((8, 128), F32, _const_spec((8, 128))),
            ((8, 128), F32, _const_spec((8, 128))),
            ((1, 128), F32, _const_spec((1, 128))),
            ((1, D_MODEL), F32, _const_spec((1, D_MODEL)))]
    return pl.pallas_call(
        body, name="proj_bwd", grid=(s_len // tm,),
        in_specs=[_tile_spec(HEADS * QT_ROWS, tm), _row_spec(tm, SLAB_W), _tile_spec(FOX_W, tm),
                  _row_spec(tm, 128), _row_spec(tm, 128), _row_spec(tm, FOX_W),
                  _row_spec(tm, FOX_W), _row_spec(tm, 2 * SGU_W), _row_spec(tm, 2 * D_MODEL),
                  _row_spec(tm, D_MODEL), _row_spec(tm, D_MODEL), _const_spec(wcat.shape),
                  _const_spec(bdiag.shape), _const_spec((1, FOX_W)), _const_spec((1, FOX_W)),
                  _const_spec((1, D_MODEL)), _const_spec(efold.shape)],
        out_specs=[o[2] for o in outs],
        out_shape=[jax.ShapeDtypeStruct(o[0], o[1]) for o in outs],
        scratch_shapes=[pltpu.VMEM((8, FOX_W), F32), pltpu.VMEM((8, FOX_W), F32)],
        compiler_params=_params(56, 1),
    )(gqt, gk, dvt, dlogf, flog, qraw, kraw, duv, dgp, x, dx1, wcat, bdiag, gq, gk_gain, g1, efold)


def _dw_matmul(at, b, tm, name, after=()):
    m, s_len = at.shape
    n = b.shape[1]

    n_chunks = 4 if s_len % (4 * 128) == 0 else 1
    ck = s_len // n_chunks

    def body(a_ref, b_hbm, *rest):
        o_ref, b_ref, b_sems = rest[len(after):]
        first = pl.program_id(0) == 0
        loads = [pltpu.make_async_copy(b_hbm.at[c * ck:(c + 1) * ck], b_ref.at[c * ck:(c + 1) * ck],
                                       b_sems.at[c]) for c in range(n_chunks)]

        @pl.when(first)
        def _():
            for cp in loads:
                cp.start()

        acc = None
        for c in range(n_chunks):
            @pl.when(first)
            def _():
                loads[c].wait()
            part = jnp.dot(a_ref[:, c * ck:(c + 1) * ck], b_ref[c * ck:(c + 1) * ck, :],
                           preferred_element_type=F32)
            acc = part if acc is None else acc + part
        o_ref[...] = acc.astype(BF16)

    return pl.pallas_call(
        body, name=name, grid=(m // tm,),
        in_specs=[pl.BlockSpec((tm, s_len), lambda i: (i, 0)), pl.BlockSpec(memory_space=pl.ANY)]
        + [pl.BlockSpec(memory_space=pl.ANY)] * len(after),
        out_specs=pl.BlockSpec((tm, n), lambda i: (i, 0)),
        out_shape=jax.ShapeDtypeStruct((m, n), BF16),
        scratch_shapes=[pltpu.VMEM(b.shape, b.dtype), pltpu.SemaphoreType.DMA((n_chunks,))],
        compiler_params=_params(48, 1),
    )(at, b, *after)


def _adamw(parts, w, m, v, tr, name, col_tile=None, select=None):
    parts = parts if isinstance(parts, (list, tuple)) else [parts]
    rows, cols = w.shape
    extra = [] if select is None else [select]
    bc1 = 1.0 - ADAM_B1 ** ADAM_STEP
    bc2 = 1.0 - ADAM_B2 ** ADAM_STEP

    def body(*refs):
        p_refs = refs[:len(parts)]
        sel_refs = refs[len(parts):len(parts) + len(extra)]
        w_ref, m_ref, v_ref, g_ref, d_ref, mo_ref, vo_ref = refs[len(parts) + len(extra):]
        g = None
        for p_ref, p in zip(p_refs, parts):
            for idx in range(p.shape[0]):
                term = p_ref[idx].astype(F32)
                g = term if g is None else g + term
        if sel_refs:
            g = _tri_dot(sel_refs[0][...], g)
        g_ref[...] = g
        mn = ADAM_B1 * m_ref[...] + (1.0 - ADAM_B1) * g
        vn = ADAM_B2 * v_ref[...] + (1.0 - ADAM_B2) * (g * g)
        mo_ref[...] = mn
        vo_ref[...] = vn
        m_hat = mn / bc1
        v_hat = vn / bc2
        d_ref[...] = -ADAM_LR * (m_hat / (jnp.sqrt(v_hat) + ADAM_EPS) + ADAM_WD * w_ref[...])

    if col_tile is None:
        spec = pl.BlockSpec((tr, cols), lambda i: (i, 0))
        pspecs = [pl.BlockSpec((p.shape[0], tr, cols), lambda i: (0, i, 0)) for p in parts]
        steps = rows // tr
    else:
        spec = pl.BlockSpec((rows, col_tile), lambda i: (0, i))
        pspecs = [pl.BlockSpec((p.shape[0], p.shape[1], col_tile), lambda i: (0, 0, i))
                  for p in parts]
        steps = cols // col_tile
    return pl.pallas_call(
        body, name=name, grid=(steps,),
        in_specs=pspecs + [_const_spec(e.shape) for e in extra] + [spec, spec, spec],
        out_specs=[spec] * 4,
        out_shape=[jax.ShapeDtypeStruct((rows, cols), F32)] * 4,
        compiler_params=_params(48, 1),
    )(*parts, *extra, w, m, v)


def _sum_parts(parts, name):
    n, rows, cols = parts.shape

    def body(p_ref, o_ref):
        g = p_ref[0]
        for idx in range(1, n):
            g = g + p_ref[idx]
        o_ref[...] = g

    return pl.pallas_call(
        body, name=name, out_shape=jax.ShapeDtypeStruct((rows, cols), F32),
        in_specs=[_const_spec(parts.shape)], out_specs=_const_spec((rows, cols)), grid=(1,),
        compiler_params=_params(16, 1),
    )(parts)


VEC_NAMES = ("g_pre_mix", "b_forget", "g_q", "g_k", "g_sgu", "b_sgu", "b_spatial", "g_post_mix",
             "g_pre_ffn", "g_post_ffn")
VEC_ROWS = 16
LOSS_ROW = len(VEC_NAMES)


def _pack_vectors(d, loss_row):
    rows = []
    for k in VEC_NAMES:
        flat = d[k].reshape(1, -1).astype(F32)
        rows.append(jnp.pad(flat, ((0, 0), (0, 1024 - flat.shape[1]))))
    rows.append(loss_row)
    rows.append(jnp.zeros((VEC_ROWS - len(rows), 1024), F32))
    return jnp.concatenate(rows, axis=0)


def _adamw_vectors(grad_rows, ws, ms, vs):
    n = len(VEC_NAMES)
    bc1 = 1.0 - ADAM_B1 ** ADAM_STEP
    bc2 = 1.0 - ADAM_B2 ** ADAM_STEP

    def step(g, w, m, v):
        mn = ADAM_B1 * m + (1.0 - ADAM_B1) * g
        vn = ADAM_B2 * v + (1.0 - ADAM_B2) * (g * g)
        delta = -ADAM_LR * ((mn / bc1) / (jnp.sqrt(vn / bc2) + ADAM_EPS) + ADAM_WD * w)
        return g, delta, mn, vn

    def body(*refs):
        g_ref = refs[0]
        ins = [refs[1 + j * n:1 + (j + 1) * n] for j in range(3)]
        outs = [refs[1 + (3 + j) * n:1 + (4 + j) * n] for j in range(4)]
        for i in range(n):
            shape = ws[i].shape
            if len(shape) == 2:
                res = step(g_ref[i:i + 1, :shape[1]], *[r[i][...] for r in ins])
                for o, val in zip(outs, res):
                    o[i][...] = val
            else:
                for r in range(shape[1]):
                    res = step(g_ref[i:i + 1, r * shape[2]:(r + 1) * shape[2]],
                               *[q[i][0, r:r + 1, :] for q in ins])
                    for o, val in zip(outs, res):
                        o[i][0, r:r + 1, :] = val

    vmem = pl.BlockSpec(memory_space=pltpu.VMEM)
    flat = pl.pallas_call(
        body, name="adamw_vectors",
        in_specs=[vmem] * (1 + 3 * n), out_specs=[vmem] * (4 * n),
        out_shape=[jax.ShapeDtypeStruct(w.shape, F32) for _ in range(4) for w in ws],
    )(grad_rows, *ws, *ms, *vs)
    return [flat[j * n:(j + 1) * n] for j in range(4)]


def _cols_to_blocks(full, width):
    r = full.shape[0]
    return jnp.transpose(full.reshape(r, N_DEV, width), (1, 0, 2))


def _blocks_to_cols(blocks):
    n, r, width = blocks.shape
    return jnp.transpose(blocks, (1, 0, 2)).reshape(r, n * width)


def kernel(x, g_pre_mix, w_in, b_forget, g_q, g_k, g_sgu, b_sgu, w_spatial, b_spatial, w_branch_a, w_branch_b, w_out, g_post_mix, g_pre_ffn, w_ffn_in, w_ffn_down, g_post_ffn, loss_target, m_g_pre_mix, m_w_in, m_b_forget, m_g_q, m_g_k, m_g_sgu, m_b_sgu, m_w_spatial, m_b_spatial, m_w_branch_a, m_w_branch_b, m_w_out, m_g_post_mix, m_g_pre_ffn, m_w_ffn_in, m_w_ffn_down, m_g_post_ffn, v_g_pre_mix, v_w_in, v_b_forget, v_g_q, v_g_k, v_g_sgu, v_b_sgu, v_w_spatial, v_b_spatial, v_w_branch_a, v_w_branch_b, v_w_out, v_g_post_mix, v_g_pre_ffn, v_w_ffn_in, v_w_ffn_down, v_g_post_ffn):
    big_names = ("w_in", "w_branch_a", "w_branch_b", "w_out", "w_ffn_in", "w_ffn_down")
    weights = dict(g_pre_mix=g_pre_mix, w_in=w_in, b_forget=b_forget, g_q=g_q, g_k=g_k, g_sgu=g_sgu,
                   b_sgu=b_sgu, w_spatial=w_spatial, b_spatial=b_spatial, w_branch_a=w_branch_a,
                   w_branch_b=w_branch_b, w_out=w_out, g_post_mix=g_post_mix, g_pre_ffn=g_pre_ffn,
                   w_ffn_in=w_ffn_in, w_ffn_down=w_ffn_down, g_post_ffn=g_post_ffn)
    mom1 = dict(g_pre_mix=m_g_pre_mix, w_in=m_w_in, b_forget=m_b_forget, g_q=m_g_q, g_k=m_g_k,
                g_sgu=m_g_sgu, b_sgu=m_b_sgu, w_spatial=m_w_spatial, b_spatial=m_b_spatial,
                w_branch_a=m_w_branch_a, w_branch_b=m_w_branch_b, w_out=m_w_out,
                g_post_mix=m_g_post_mix, g_pre_ffn=m_g_pre_ffn, w_ffn_in=m_w_ffn_in,
                w_ffn_down=m_w_ffn_down, g_post_ffn=m_g_post_ffn)
    mom2 = dict(g_pre_mix=v_g_pre_mix, w_in=v_w_in, b_forget=v_b_forget, g_q=v_g_q, g_k=v_g_k,
                g_sgu=v_g_sgu, b_sgu=v_b_sgu, w_spatial=v_w_spatial, b_spatial=v_b_spatial,
                w_branch_a=v_w_branch_a, w_branch_b=v_w_branch_b, w_out=v_w_out,
                g_post_mix=v_g_post_mix, g_pre_ffn=v_g_pre_ffn, w_ffn_in=v_w_ffn_in,
                w_ffn_down=v_w_ffn_down, g_post_ffn=v_g_post_ffn)
    names = list(weights)
    shapes = {k: weights[k].shape for k in names}

    s_len = x.shape[1]
    xs = x.reshape(s_len, D_MODEL)
    tgt = loss_target.reshape(s_len, D_MODEL)

    transposed = ("w_in", "w_ffn_in")

    def local_view(a, k):
        return jnp.transpose(a[0]) if k in transposed else a[0]

    shards = {k: local_view(weights[k], k).astype(BF16) for k in big_names}

    x_pos, y_pos, c_pos = _mesh_pos()
    me = 4 * x_pos + 2 * y_pos + c_pos
    r_idx = jnp.arange(BLK)
    general = (jnp.asarray(BLK_AT, jnp.int32) - jnp.asarray(FRAME_START, jnp.int32))[me] + r_idx
    holder = jnp.where(r_idx < F_AT, BLK_AT[F_DEV] - FRAME_START[F_DEV] + r_idx,
                       jnp.where(r_idx < F_AT + HEADS, FRAME - F_AT + r_idx,
                                 BLK_AT[F_DEV] - FRAME_START[F_DEV] - HEADS + r_idx))
    frame_row = jnp.where(me == F_DEV, holder, general)
    in_frame = (frame_row[:, None] == jnp.arange(FRAME_ROWS)[None, :]).astype(BF16)
    my_frame = jnp.dot(in_frame.T, shards["w_in"], preferred_element_type=F32).astype(BF16)
    wcat = _gather_w_in(my_frame)
    wcat, later = lax.optimization_barrier(
        (wcat, [shards[k] for k in big_names if k != "w_in"]))
    shards.update(zip([k for k in big_names if k != "w_in"], later))
    (gat_mix, gat_ffn), gat_token = _exchange_start(
        [[shards["w_branch_a"], shards["w_branch_b"], shards["w_out"]],
         [shards["w_ffn_in"], shards["w_ffn_down"]]], "gather_start", gather=True)

    seg = np.arange(FOX_W) // HEAD_DIM
    bdiag = jnp.asarray(seg[:128, None] == seg[None, :128], BF16)
    tm = TOKEN_TILE
    lower = np.arange(tm)[None, :] <= np.arange(tm)[:, None]
    tril = jnp.asarray(lower, BF16)
    triu = jnp.asarray(lower.T, BF16)
    egrp = jnp.asarray(seg[:, None] == np.arange(128)[None, :], BF16)
    efold = jnp.asarray((np.arange(FOX_W) % HEAD_DIM)[:, None] == np.arange(128)[None, :], BF16)
    gq512 = jnp.tile(g_q.reshape(1, HEAD_DIM), (1, HEADS))
    gk512 = jnp.tile(g_k.reshape(1, HEAD_DIM), (1, HEADS))
    bfor = jnp.pad(b_forget.reshape(1, HEADS), ((0, 0), (0, 128 - HEADS)))
    pos = np.arange(WINDOW)
    wmask = (pos[None, :] // CHUNK) <= (pos[:, None] // CHUNK)
    wsm_f = jnp.where(jnp.asarray(wmask)[None], w_spatial[0], 0.0)
    wsm = wsm_f.astype(BF16)
    wsmt = jnp.transpose(wsm_f, (0, 2, 1)).astype(BF16)
    bsf = jnp.repeat(jnp.transpose(b_spatial[0]), HEAD_DIM, axis=1)
    wmask_f = jnp.asarray(wmask, F32)

    col = np.arange(SLAB_W)
    row128 = np.arange(128)

    def d_place(first, sign):
        parts = [(col[None, :] // 128 == row128[:, None]) & (col[None, :] % 128 == first + a)
                 for a in range(3)]
        return jnp.asarray(sign * np.concatenate(parts, axis=0).astype(np.float32), BF16)

    pdq, pdk = d_place(HEAD_DIM, 1.0), d_place(HEAD_DIM + 3, -1.0)
    ones_q = jnp.asarray((col % 128 >= HEAD_DIM + 3) & (col % 128 < HEAD_DIM + 6), F32)[None]
    ones_k = jnp.asarray((col % 128 >= HEAD_DIM) & (col % 128 < HEAD_DIM + 3), F32)[None]
    ecol = jnp.asarray((col[:, None] // 128 == row128[None, :])
                       & (col[:, None] % 128 == HEAD_DIM + 3), BF16)

    (h, qa, ka, kat, vs, vt, qraw, kraw, flog, uvpre, gpre) = _proj_fwd(
        xs, g_pre_mix + gat_token[0:1, 0:1], wcat, bdiag, gq512, gk512, bfor, tril, pdq, pdk,
        ones_q, ones_k)
    attn, attn_t, lse = _attn_fwd(qa, ka, vt)
    (own_a, own_b, own_out), (zone_a, zone_b, zone_out) = _exchange_wait(
        gat_mix, attn, "gather_wait_mix", gather=True)
    wa = _blocks_to_cols(_own_block(zone_a, own_a))
    wb = _blocks_to_cols(_own_block(zone_b, own_b))
    wout = _own_block(zone_out, own_out).reshape(D_MODEL, D_MODEL)
    sgu_t, ya, yb, merged_t, om, x1 = _mix_fwd(attn, uvpre, gpre, xs, wa, wb, wout, wsm, bsf,
                                           g_sgu, b_sgu, g_post_mix)
    (own_ffn, own_down), (zone_ffn, zone_down) = _exchange_wait(
        gat_ffn, x1, "gather_wait_ffn", gather=True)
    wffn = _own_block(zone_ffn, own_ffn).reshape(2 * D_FF, D_MODEL)
    wdown = _own_block(zone_down, own_down).reshape(D_FF, D_MODEL)
    (dx1, h2, act_t, dff, dgu_t, loss_acc, dg_post_ffn, dg_pre_ffn) = _ffn_fwd_bwd(
        x1, tgt, wffn, wdown, g_pre_ffn, g_post_ffn)

    dw_down = _dw_matmul(act_t, dff, D_FF // 4, "dw_down")
    dw_ffn = _dw_matmul(dgu_t, h2, 2 * D_FF // N_DEV, "dw_ffn_in")
    def own_of(parts):
        return [lax.dynamic_index_in_dim(p, me, 0, keepdims=False) for p in parts]

    parts_ffn = [dw_ffn.reshape(N_DEV, 2 * D_FF // N_DEV, D_MODEL),
                 dw_down.reshape(N_DEV, D_FF // N_DEV, D_MODEL)]
    mine_ffn = own_of(parts_ffn)
    (sct_ffn,), sct_ffn_token = _exchange_start([parts_ffn], "scatter_start_ffn", gather=False,
                                                after=mine_ffn)

    (dom, dya, dyb, dgp, dot_, delta, duv, dws, dbs, dg_sgu, db_sgu, dg_post_mix) = _mix_bwd(
        dx1, om, ya, yb, gpre, uvpre, attn, wout, wa, wb, wsm, wsmt, bsf, g_sgu, b_sgu,
        g_post_mix + sct_ffn_token[0:1, 0:1], wmask_f, egrp)
    dw_out = _dw_matmul(merged_t, dom, 512, "dw_out")
    dw_a = _dw_matmul(attn_t, dya, 512, "dw_a")
    dw_b = _dw_matmul(sgu_t, dyb, 512, "dw_b")
    parts_mix = [_cols_to_blocks(dw_a, D_MODEL // N_DEV), _cols_to_blocks(dw_b, D_MODEL // N_DEV),
                 dw_out.reshape(N_DEV, D_MODEL // N_DEV, D_MODEL)]
    mine_mix = own_of(parts_mix)
    (sct_mix,), sct_mix_token = _exchange_start([parts_mix], "scatter_start_mix", gather=False,
                                                after=mine_mix)

    gk_all, dvt, gqt, col_sums = _attn_bwd(qa, ka, kat, vs, dot_, lse,
                                           delta + sct_mix_token[0, 0], ecol)
    dlogf = _rev_cumsum(col_sums, gqt, triu)
    dx, dproj_t, dgq, dgk, dbf, dg_pre_mix = _proj_bwd(
        gqt, gk_all, dvt, dlogf, flog, qraw, kraw, duv, dgp, xs, dx1, wcat, bdiag, gq512, gk512,
        g_pre_mix, efold)

    small_local = dict(
        g_pre_mix=dg_pre_mix, b_forget=dbf[:, :HEADS], g_q=dgq[0:1, :HEAD_DIM],
        g_k=dgk[0:1, :HEAD_DIM], g_sgu=dg_sgu, b_sgu=db_sgu, w_spatial=dws,
        b_spatial=jnp.transpose(dbs[:, :GROUPS]), g_post_mix=dg_post_mix, g_pre_ffn=dg_pre_ffn,
        g_post_ffn=dg_post_ffn)
    loss_row = jnp.pad(loss_acc[0:1, 0:1], ((0, 0), (0, 1023)))
    small_parts = [_pack_vectors(small_local, loss_row).reshape(N_DEV, VEC_ROWS // N_DEV, 1024),
                   dws]

    def with_own(zones, own_blocks):
        return [_own_block(z, b) for z, b in zip(zones, own_blocks)]

    mine_small = own_of(small_parts)
    (sct_small,), sct_small_token = _exchange_start([small_parts], "scatter_start_small",
                                                    gather=False, after=mine_small)
    dw_cat = _dw_matmul(dproj_t, h, C_END // N_DEV, "dw_in", after=(sct_small_token,))
    recv_vec, recv_ws = with_own(
        _exchange_wait(sct_small, dw_cat, "scatter_wait_small", gather=False)[1], mine_small)
    small_sums = [_sum_parts(recv_vec, "sum_vectors"), _sum_parts(recv_ws, "sum_w_spatial")]
    (gat_small,), gat_small_token = _exchange_start([small_sums], "gather_start_small",
                                                    gather=True)
    pair_blocks, own_pair = _pair_sums(dw_cat, "pair_sums_in", gat_small_token)
    rs_in, rs_token = _chip_exchange_start(pair_blocks, "chip_exchange_start_in")

    recv_ffn, recv_down = with_own(
        _exchange_wait(sct_ffn, rs_token, "scatter_wait_ffn", gather=False)[1], mine_ffn)
    recv_a, recv_b, recv_out = with_own(
        _exchange_wait(sct_mix, recv_ffn, "scatter_wait_mix", gather=False)[1], mine_mix)
    received = [None, recv_a, recv_b, recv_out, recv_ffn, recv_down]

    grads, deltas, new_m, new_v = {}, {}, {}, {}
    row_tiles = {"w_in": None, "w_branch_a": 512, "w_branch_b": 512, "w_out": 128, "w_ffn_in": 176,
                 "w_ffn_down": 352}

    def update(k, parts):
        outs = _adamw(parts, local_view(weights[k], k), local_view(mom1[k], k),
                      local_view(mom2[k], k), row_tiles[k], "adamw_" + k,
                      col_tile=256 if k == "w_in" else None,
                      select=in_frame if k == "w_in" else None)
        if k in transposed:
            outs = [jnp.transpose(o) for o in outs]
        grads[k], deltas[k], new_m[k], new_v[k] = [o[None] for o in outs]
        return outs[0]

    last = None
    for idx, k in enumerate(big_names):
        if k != "w_in":
            last = update(k, received[idx])

    (own_vec, own_ws), (zone_vec, zone_ws) = _exchange_wait(gat_small, last, "gather_wait_small",
                                                            gather=True)
    vec_all = _own_block(zone_vec, own_vec).reshape(VEC_ROWS, 1024)
    ws_all = _own_block(zone_ws, own_ws).reshape(1, GROUPS * WINDOW, WINDOW)

    def rows_of(d):
        return d["w_spatial"].reshape(GROUPS * WINDOW, WINDOW)

    outs = _adamw(ws_all, rows_of(weights), rows_of(mom1), rows_of(mom2), GROUPS * WINDOW,
                  "adamw_w_spatial")
    for dst, o in zip((grads, deltas, new_m, new_v), outs):
        dst["w_spatial"] = o.reshape(shapes["w_spatial"])
    sg = outs[0]
    vec_outs = _adamw_vectors(vec_all, *[[d[k] for k in VEC_NAMES] for d in (weights, mom1, mom2)])
    for dst, group in zip((grads, deltas, new_m, new_v), vec_outs):
        dst.update(zip(VEC_NAMES, group))
    arrived = _chip_exchange_wait(rs_in, sg, "chip_exchange_wait_in")
    update("w_in", [own_pair[None], arrived])

    loss = vec_all[LOSS_ROW, 0]
    return (loss, dx.reshape(x.shape), *[grads[k] for k in names], *[deltas[k] for k in names],
            *[new_m[k] for k in names], *[new_v[k] for k in names])
```

```python
import functools
import math

import jax
import jax.numpy as jnp
import numpy as np
from jax import lax
from jax.experimental import pallas as pl
from jax.experimental.pallas import tpu as pltpu

F32 = jnp.float32
BF16 = jnp.bfloat16

D_MODEL = 1024
FOX_W = 512
HEADS = 8
HEAD_DIM = 64
SGU_W = 512
GROUPS = 8
WINDOW = 128
CHUNK = 64
D_FF = 2816
IN_COLS = 4616
EPS = 1e-6
N_DEV = 8
LOG2E = 1.4426950408889634
LN2 = 0.6931471805599453

C_Q, C_K, C_V, C_UV, C_G, C_F, C_END = 0, 512, 1024, 1536, 2560, 4608, 4736

ADAM_LR, ADAM_B1, ADAM_B2, ADAM_EPS, ADAM_WD, ADAM_STEP = 0.001, 0.9, 0.999, 1e-08, 0.01, 10

MIB = 1024 * 1024
TOKEN_TILE = 256
ATTN_TILE = 256
SLAB_W = HEADS * 128
QT_ROWS = 72

BLK = IN_COLS // N_DEV
F_LO = 3 * FOX_W
F_DEV = F_LO // BLK
F_AT = F_LO - F_DEV * BLK
BLK_AT = [BLK * j - (HEADS if BLK * j > F_LO else 0) for j in range(N_DEV)]
FRAME_START = [a // 16 * 16 for a in BLK_AT]
FRAME = 608
FRAME_ROWS = FRAME + 16


def _params(vmem_mib, n_axes):
    return pltpu.CompilerParams(
        dimension_semantics=("arbitrary",) * n_axes, vmem_limit_bytes=vmem_mib * MIB)


def _const_spec(shape):
    nd = len(shape)
    return pl.BlockSpec(shape, lambda *_: (0,) * nd)


def _row_spec(tm, cols):
    return pl.BlockSpec((tm, cols), lambda i: (i, 0))


def _tile_spec(rows, tm):
    return pl.BlockSpec((1, rows, tm), lambda i: (i, 0, 0))


def _split3_dot(x, e):
    x1 = x.astype(BF16)
    r1 = x - x1.astype(F32)
    x2 = r1.astype(BF16)
    x3 = (r1 - x2.astype(F32)).astype(BF16)
    dot = functools.partial(jnp.dot, preferred_element_type=F32)
    return dot(x1, e) + dot(x2, e) + dot(x3, e)


def _tri_dot(tri, x):
    x1 = x.astype(BF16)
    r1 = x - x1.astype(F32)
    x2 = r1.astype(BF16)
    x3 = (r1 - x2.astype(F32)).astype(BF16)
    dot = functools.partial(jnp.dot, preferred_element_type=F32)
    return dot(tri, x1) + dot(tri, x2) + dot(tri, x3)


def _seg_mean(sq, bd_ref):
    hi = sq.astype(BF16)
    lo = (sq - hi.astype(F32)).astype(BF16)
    bd = bd_ref[...]
    dot = functools.partial(jnp.dot, preferred_element_type=F32)
    pairs = [dot(hi[:, p * 128:(p + 1) * 128], bd) + dot(lo[:, p * 128:(p + 1) * 128], bd)
             for p in range(HEADS // 2)]
    return jnp.concatenate(pairs, axis=1) * (1.0 / HEAD_DIM)


def _slabs_from_heads(t):
    lane = lax.broadcasted_iota(jnp.int32, (t.shape[0], 128), 1)
    low = lane < HEAD_DIM
    slabs = []
    for p in range(HEADS // 2):
        pair = t[:, p * 128:(p + 1) * 128]
        slabs.append(jnp.where(low, pair, 0.0))
        slabs.append(jnp.where(low, pltpu.roll(pair, HEAD_DIM, 1), 0.0))
    return jnp.concatenate(slabs, axis=1)


def _dot_nt(a, b):
    return lax.dot_general(a, b, (((1,), (1,)), ((), ())), preferred_element_type=F32)


def _dot_tn(a, b):
    return lax.dot_general(a, b, (((0,), (0,)), ((), ())), preferred_element_type=F32)


def _sigmoid(x):
    return 0.5 * jnp.tanh(0.5 * x) + 0.5


_GELU_C = math.sqrt(2.0 / math.pi)


def _gelu_and_grad(x):
    inner = _GELU_C * (x + 0.044715 * x * x * x)
    t = jnp.tanh(inner)
    y = 0.5 * x * (1.0 + t)
    dy = 0.5 * (1.0 + t) + 0.5 * x * (1.0 - t * t) * _GELU_C * (1.0 + 3.0 * 0.044715 * x * x)
    return y, dy


def _rms_bwd(xin, r, g, dy):
    dyg = dy * g
    return r * dyg - xin * (r * r * r) * jnp.mean(dyg * xin, axis=-1, keepdims=True)


def _mesh_pos():
    x, y, c = lax.axis_index("x"), lax.axis_index("y"), lax.axis_index("c")
    return x, y, c


def _peer(k):
    x, y, c = _mesh_pos()
    px = (1 - x) if (k >> 2) & 1 else x
    py = (1 - y) if (k >> 1) & 1 else y
    pc = (1 - c) if k & 1 else c
    return (px, py, pc), 4 * px + 2 * py + pc


def _frame_start(j):
    at = BLK * j - jnp.where(BLK * j > F_LO, HEADS, 0)
    return pl.multiple_of(at // 16 * 16, 16)


HALF_A = 320


def _gather_w_in(frame):
    pieces = (slice(0, HALF_A), slice(HALF_A, FRAME_ROWS))

    def body(x_ref, out_ref, zone, send_sems, recv_sems, local_sem):
        x, y, c = _mesh_pos()
        me, sibling = (x, y, c), (x, y, 1 - c)
        nbr_x, nbr_y, across = (1 - x, y, c), (x, 1 - y, c), (1 - x, 1 - y, c)

        def index(pos):
            return 4 * pos[0] + 2 * pos[1] + pos[2]

        def copy(k, block, piece, to, src=None):
            rows = pieces[piece]
            return pltpu.make_async_remote_copy(
                src_ref=(zone.at[index(block), rows] if src is None else src.at[rows]),
                dst_ref=zone.at[index(block), rows],
                send_sem=send_sems.at[k], recv_sem=recv_sems.at[k],
                device_id=to, device_id_type=pl.DeviceIdType.MESH)

        def add(block):
            j = index(block)
            rows = pl.ds(_frame_start(j), FRAME)
            out_ref[rows, :] = (out_ref[rows, :].astype(F32)
                                + zone[j, :FRAME, :].astype(F32)).astype(BF16)
            tail = slice(C_F, C_F + FRAME_ROWS - FRAME)
            forget = zone[j, FRAME:, :].astype(F32) * (j == F_DEV).astype(F32)
            out_ref[tail, :] = (out_ref[tail, :].astype(F32) + forget).astype(BF16)

        mine = pltpu.make_async_copy(x_ref, zone.at[index(me)], local_sem)
        mine.start()
        first = [copy(1, me, 0, nbr_x, src=x_ref), copy(3, me, 1, nbr_y, src=x_ref),
                 copy(2, me, 1, nbr_x, src=x_ref), copy(4, me, 0, nbr_y, src=x_ref)]
        own_to_sibling = pltpu.make_async_remote_copy(
            src_ref=x_ref, dst_ref=zone.at[index(me)], send_sem=send_sems.at[0],
            recv_sem=recv_sems.at[0], device_id=sibling, device_id_type=pl.DeviceIdType.MESH)
        for cp in first:
            cp.start()
        own_to_sibling.start()
        out_ref[...] = jnp.zeros_like(out_ref)
        mine.wait()
        add(me)

        sent = []

        def landed(k, block, piece, forward=None):
            copy(k, block, piece, me).wait_recv()
            if forward is not None:
                cp = copy(*forward)
                cp.start()
                sent.append(cp)
            cp = copy(6 + k, block, piece, sibling)
            cp.start()
            sent.append(cp)

        landed(1, nbr_x, 0, forward=(5, nbr_x, 0, nbr_y))
        landed(3, nbr_y, 1, forward=(6, nbr_y, 1, nbr_x))
        landed(2, nbr_x, 1)
        add(nbr_x)
        landed(4, nbr_y, 0)
        add(nbr_y)
        landed(5, across, 0)
        landed(6, across, 1)
        add(across)
        pltpu.make_async_remote_copy(
            src_ref=x_ref, dst_ref=zone.at[index(sibling)], send_sem=send_sems.at[0],
            recv_sem=recv_sems.at[0], device_id=sibling,
            device_id_type=pl.DeviceIdType.MESH).wait_recv()
        add(sibling)
        for k, block in ((1, nbr_x), (2, nbr_x), (3, nbr_y), (4, nbr_y), (5, across), (6, across)):
            their = (block[0], block[1], 1 - c)
            piece = {1: 0, 2: 1, 3: 1, 4: 0, 5: 0, 6: 1}[k]
            copy(6 + k, their, piece, me).wait_recv()
            if k in (2, 4, 6):
                add(their)
        for cp in first + sent:
            cp.wait_send()
        own_to_sibling.wait_send()

    return pl.pallas_call(
        body, name="gather_w_in", out_shape=jax.ShapeDtypeStruct((C_END, frame.shape[1]), BF16),
        in_specs=[pl.BlockSpec(memory_space=pl.ANY)],
        out_specs=pl.BlockSpec(memory_space=pltpu.VMEM),
        scratch_shapes=[pltpu.VMEM((N_DEV,) + frame.shape, BF16),
                        pltpu.SemaphoreType.DMA((13,)), pltpu.SemaphoreType.DMA((13,)),
                        pltpu.SemaphoreType.DMA],
        compiler_params=pltpu.CompilerParams(vmem_limit_bytes=40 * MIB),
    )(frame)


def _chip_peer(k):
    x, y, c = _mesh_pos()
    px = (1 - x) if (k >> 1) & 1 else x
    py = (1 - y) if k & 1 else y
    return (px, py, c), 2 * px + py


def _pair_sums(dw_cat, name, after):
    rows, cols = FRAME_ROWS, dw_cat.shape[1]
    n_chips = N_DEV // 2

    def pieces(p_ref, j):
        return (p_ref.at[pl.ds(_frame_start(j), FRAME)], p_ref.at[pl.ds(C_F, FRAME_ROWS - FRAME)])

    def body(p_ref, after_ref, send_ref, own_ref, mine_buf, sib_buf, send_sems, recv_sems,
             local_sems):
        x, y, c = _mesh_pos()
        sibling = (x, y, 1 - c)
        copies, local = [], []
        for q in range(n_chips):
            for part, (lo, hi) in enumerate(((0, FRAME), (FRAME, FRAME_ROWS))):
                cp = pltpu.make_async_remote_copy(
                    src_ref=pieces(p_ref, 2 * q + (1 - c))[part], dst_ref=sib_buf.at[q, lo:hi],
                    send_sem=send_sems.at[2 * q + part], recv_sem=recv_sems.at[2 * q + part],
                    device_id=sibling, device_id_type=pl.DeviceIdType.MESH)
                cp.start()
                copies.append(cp)
                lc = pltpu.make_async_copy(pieces(p_ref, 2 * q + c)[part], mine_buf.at[q, lo:hi],
                                           local_sems.at[2 * q + part])
                lc.start()
                local.append(lc)
        for lc in local:
            lc.wait()
        for cp in copies:
            cp.wait_recv()
        for k in range(1, n_chips):
            _, q = _chip_peer(k)
            send_ref[k - 1] = (mine_buf[q].astype(F32) + sib_buf[q].astype(F32)).astype(BF16)
        my_chip = 2 * x + y
        own_ref[...] = mine_buf[my_chip].astype(F32) + sib_buf[my_chip].astype(F32)
        for cp in copies:
            cp.wait_send()

    vmem = pl.BlockSpec(memory_space=pltpu.VMEM)
    return pl.pallas_call(
        body, name=name,
        out_shape=[jax.ShapeDtypeStruct((n_chips - 1, rows, cols), BF16),
                   jax.ShapeDtypeStruct((rows, cols), F32)],
        in_specs=[pl.BlockSpec(memory_space=pl.ANY)] * 2, out_specs=[vmem, vmem],
        scratch_shapes=[pltpu.VMEM((n_chips, rows, cols), BF16),
                        pltpu.VMEM((n_chips, rows, cols), BF16),
                        pltpu.SemaphoreType.DMA((2 * n_chips,)),
                        pltpu.SemaphoreType.DMA((2 * n_chips,)),
                        pltpu.SemaphoreType.DMA((2 * n_chips,))],
        compiler_params=pltpu.CompilerParams(vmem_limit_bytes=40 * MIB),
    )(dw_cat, after)


def _chip_copy(src_ref, land_ref, send_sem, recv_sem, k):
    peer, _ = _chip_peer(k)
    return pltpu.make_async_remote_copy(
        src_ref=src_ref.at[k - 1], dst_ref=land_ref.at[k - 1], send_sem=send_sem, recv_sem=recv_sem,
        device_id=peer, device_id_type=pl.DeviceIdType.MESH)


def _chip_exchange_start(blocks, name):
    hbm = pl.BlockSpec(memory_space=pltpu.HBM)
    sem = pl.BlockSpec(memory_space=pltpu.SEMAPHORE)
    n_peers = blocks.shape[0]

    def body(src_ref, zone_ref, send_sems, recv_sems, src_thru, zone_thru, token):
        for k in range(1, n_peers + 1):
            _chip_copy(src_ref, zone_ref, send_sems.at[k - 1], recv_sems.at[k - 1], k).start()
        token[...] = jnp.zeros_like(token)

    outs = pl.pallas_call(
        body, name=name, in_specs=[hbm, hbm],
        out_shape=[pltpu.SemaphoreType.DMA((n_peers,)), pltpu.SemaphoreType.DMA((n_peers,)),
                   pltpu.HBM(blocks.shape, blocks.dtype), pltpu.HBM(blocks.shape, blocks.dtype),
                   jax.ShapeDtypeStruct((8, 128), F32)],
        out_specs=[sem, sem, hbm, hbm, pl.BlockSpec(memory_space=pltpu.VMEM)],
        input_output_aliases={0: 2, 1: 3},
        compiler_params=pltpu.CompilerParams(
            has_side_effects=pltpu.SideEffectType.DATAFLOW_SIDE_EFFECTING),
    )(pltpu.with_memory_space_constraint(blocks, pltpu.HBM),
      pltpu.with_memory_space_constraint(lax.empty(blocks.shape, blocks.dtype), pltpu.HBM))
    return outs[:4], outs[4]


def _chip_exchange_wait(handle, after, name):
    send_sems, recv_sems, src, zone = handle
    hbm = pl.BlockSpec(memory_space=pltpu.HBM)
    sem = pl.BlockSpec(memory_space=pltpu.SEMAPHORE)

    def body(src_ref, zone_ref, ssem, rsem, after_ref, src_out, zone_out):
        for k in range(1, src.shape[0] + 1):
            cp = _chip_copy(src_ref, zone_ref, ssem.at[k - 1], rsem.at[k - 1], k)
            cp.wait_send()
            cp.wait_recv()

    outs = pl.pallas_call(
        body, name=name,
        in_specs=[hbm, hbm, sem, sem, pl.BlockSpec(memory_space=pl.ANY)],
        out_shape=[pltpu.HBM(src.shape, src.dtype), pltpu.HBM(zone.shape, zone.dtype)],
        out_specs=[hbm, hbm], input_output_aliases={0: 0, 1: 1},
        compiler_params=pltpu.CompilerParams(
            has_side_effects=pltpu.SideEffectType.DATAFLOW_SIDE_EFFECTING),
    )(src, zone, send_sems, recv_sems, after)
    return outs[1]


def _remote_copy(gather, src_ref, land_ref, send_sem, recv_sem, k, receive_side):
    x, y, c = _mesh_pos()
    me = 4 * x + 2 * y + c
    peer, pidx = _peer(k)
    return pltpu.make_async_remote_copy(
        src_ref=src_ref if gather else src_ref.at[pidx],
        dst_ref=land_ref.at[pidx if receive_side else me],
        send_sem=send_sem, recv_sem=recv_sem,
        device_id=peer, device_id_type=pl.DeviceIdType.MESH)


def _exchange_start(groups, name, gather, after=()):
    arrs = [a for g in groups for a in g]
    n, n_groups = len(arrs), len(groups)
    lands = [jax.ShapeDtypeStruct(((N_DEV,) + a.shape) if gather else a.shape, a.dtype)
             for a in arrs]

    def body(*refs):
        srcs, zones = refs[:n], refs[n:2 * n]
        outs_at = 2 * n + len(after)
        sems = refs[outs_at:outs_at + 2 * n_groups]
        token = refs[-1]
        a = 0
        for gi, g in enumerate(groups):
            send_sems, recv_sems = sems[2 * gi], sems[2 * gi + 1]
            for k in range(1, N_DEV):
                for ai in range(len(g)):
                    slot = ai * (N_DEV - 1) + k - 1
                    _remote_copy(gather, srcs[a + ai], zones[a + ai], send_sems.at[slot],
                                 recv_sems.at[slot], k, False).start()
            a += len(g)
        token[...] = jnp.zeros_like(token)

    hbm = pl.BlockSpec(memory_space=pltpu.HBM)
    sem = pl.BlockSpec(memory_space=pltpu.SEMAPHORE)
    sem_shapes = []
    for g in groups:
        sem_shapes += [pltpu.SemaphoreType.DMA((len(g) * (N_DEV - 1),))] * 2
    outs = pl.pallas_call(
        body, name=name,
        in_specs=[hbm] * (2 * n) + [pl.BlockSpec(memory_space=pl.ANY)] * len(after),
        out_shape=sem_shapes + [pltpu.HBM(a.shape, a.dtype) for a in arrs]
        + [pltpu.HBM(z.shape, z.dtype) for z in lands] + [jax.ShapeDtypeStruct((8, 128), F32)],
        out_specs=[sem] * (2 * n_groups) + [hbm] * (2 * n)
        + [pl.BlockSpec(memory_space=pltpu.VMEM)],
        input_output_aliases={i: 2 * n_groups + i for i in range(2 * n)},
        compiler_params=pltpu.CompilerParams(
            has_side_effects=pltpu.SideEffectType.DATAFLOW_SIDE_EFFECTING),
    )(*[pltpu.with_memory_space_constraint(a, pltpu.HBM) for a in arrs],
      *[pltpu.with_memory_space_constraint(lax.empty(z.shape, z.dtype), pltpu.HBM) for z in lands],
      *after)
    sems = outs[:2 * n_groups]
    thru = outs[2 * n_groups:2 * n_groups + n]
    zones = outs[2 * n_groups + n:2 * n_groups + 2 * n]
    handles, a = [], 0
    for gi, g in enumerate(groups):
        handles.append((sems[2 * gi], sems[2 * gi + 1], thru[a:a + len(g)], zones[a:a + len(g)]))
        a += len(g)
    return handles, outs[-1]


def _exchange_wait(handle, after, name, gather):
    send_sems, recv_sems, thru, zones = handle
    n = len(thru)

    def body(*refs):
        srcs, lands = refs[:n], refs[n:2 * n]
        ssem, rsem = refs[2 * n], refs[2 * n + 1]
        for k in range(1, N_DEV):
            for ai in range(n):
                slot = ai * (N_DEV - 1) + k - 1
                cp = _remote_copy(gather, srcs[ai], lands[ai], ssem.at[slot], rsem.at[slot], k, True)
                cp.wait_send()
                cp.wait_recv()

    hbm = pl.BlockSpec(memory_space=pltpu.HBM)
    sem = pl.BlockSpec(memory_space=pltpu.SEMAPHORE)
    outs = pl.pallas_call(
        body, name=name,
        in_specs=[hbm] * (2 * n) + [sem, sem, pl.BlockSpec(memory_space=pl.ANY)],
        out_shape=[pltpu.HBM(a.shape, a.dtype) for a in thru]
        + [pltpu.HBM(z.shape, z.dtype) for z in zones],
        out_specs=[hbm] * (2 * n),
        input_output_aliases={i: i for i in range(2 * n)},
        compiler_params=pltpu.CompilerParams(
            has_side_effects=pltpu.SideEffectType.DATAFLOW_SIDE_EFFECTING),
    )(*thru, *zones, send_sems, recv_sems, after)
    return outs[:n], outs[n:]


def _own_block(zone, block):
    x, y, c = _mesh_pos()
    me = 4 * x + 2 * y + c
    return lax.dynamic_update_slice_in_dim(zone, block[None], me, axis=0)


def _proj_fwd(x, g1, wcat, bdiag, gq, gk, bfor, tri, pdq, pdk, ones_q, ones_k):
    s_len = x.shape[0]
    tm = TOKEN_TILE
    nt = s_len // tm

    def body(x_ref, g1_ref, w_ref, bd_ref, gq_ref, gk_ref, bf_ref, tri_ref, pdq_ref,
             pdk_ref, oq_ref, ok_ref,
             h_ref, qa_ref, ka_ref, kat_ref, vs_ref, vt_ref, qr_ref, kr_ref, flog_ref, uv_ref,
             gp_ref, carry):
        @pl.when(pl.program_id(0) == 0)
        def _():
            carry[...] = jnp.zeros_like(carry)

        xf = x_ref[...]
        r = lax.rsqrt(jnp.mean(xf * xf, axis=-1, keepdims=True) + EPS)
        h = (xf * r * g1_ref[...]).astype(BF16)
        h_ref[...] = h
        dot = functools.partial(jnp.dot, preferred_element_type=F32)

        def proj(lo, hi):
            return _dot_nt(h, w_ref[lo:hi, :])

        flog = proj(C_F, C_END) + bf_ref[...]
        flog_ref[...] = flog
        lane = lax.broadcasted_iota(jnp.int32, flog.shape, 1)
        logf = jnp.minimum(flog, 0.0) - jnp.log(1.0 + jnp.exp(-jnp.abs(flog)))
        logf = jnp.where(lane < HEADS, logf, 0.0)
        dcum = _tri_dot(tri_ref[...], logf) + carry[...]
        carry[...] = dcum[tm - 1:tm, :]
        d2 = dcum * LOG2E
        d2a = d2.astype(BF16)
        rem = d2 - d2a.astype(F32)
        d2b = rem.astype(BF16)
        d2c = (rem - d2b.astype(F32)).astype(BF16)

        q = proj(C_Q, C_K)
        qr_ref[...] = q.astype(BF16)
        rq = lax.rsqrt(_seg_mean(q * q, bd_ref) + EPS)
        qn = q * rq * (gq_ref[...] * (HEAD_DIM ** -0.5 * LOG2E))
        d_parts = jnp.concatenate([d2a, d2b, d2c], axis=1)
        qa = _slabs_from_heads(qn) + dot(d_parts, pdq_ref[...]) + oq_ref[...]
        qa_ref[...] = qa.astype(BF16)

        k = proj(C_K, C_V)
        kr_ref[...] = k.astype(BF16)
        rk = lax.rsqrt(_seg_mean(k * k, bd_ref) + EPS)
        kn = k * rk * gk_ref[...]
        ka = _slabs_from_heads(kn) + dot(d_parts, pdk_ref[...]) + ok_ref[...]
        ka_ref[...] = ka.astype(BF16)
        kat_ref[0] = ka.T.astype(BF16)

        v = proj(C_V, C_UV)
        vs_ref[...] = _slabs_from_heads(v).astype(BF16)
        vt_ref[0] = v.T.astype(BF16)
        uv_ref[...] = proj(C_UV, C_G).astype(BF16)
        gp_ref[...] = proj(C_G, C_F).astype(BF16)

    outs = [((s_len, D_MODEL), BF16, _row_spec(tm, D_MODEL)),
            ((s_len, SLAB_W), BF16, _row_spec(tm, SLAB_W)),
            ((s_len, SLAB_W), BF16, _row_spec(tm, SLAB_W)),
            ((nt, SLAB_W, tm), BF16, _tile_spec(SLAB_W, tm)),
            ((s_len, SLAB_W), BF16, _row_spec(tm, SLAB_W)),
            ((nt, FOX_W, tm), BF16, _tile_spec(FOX_W, tm)),
            ((s_len, FOX_W), BF16, _row_spec(tm, FOX_W)),
            ((s_len, FOX_W), BF16, _row_spec(tm, FOX_W)),
            ((s_len, 128), F32, _row_spec(tm, 128)),
            ((s_len, 2 * SGU_W), BF16, _row_spec(tm, 2 * SGU_W)),
            ((s_len, 2 * D_MODEL), BF16, _row_spec(tm, 2 * D_MODEL))]
    return pl.pallas_call(
        body, name="proj_fwd", grid=(nt,),
        in_specs=[_row_spec(tm, D_MODEL), _const_spec((1, D_MODEL)), _const_spec(wcat.shape),
                  _const_spec(bdiag.shape), _const_spec((1, FOX_W)), _const_spec((1, FOX_W)),
                  _const_spec((1, 128)), _const_spec((tm, tm)), _const_spec(pdq.shape), _const_spec(pdk.shape), _const_spec(ones_q.shape),
                  _const_spec(ones_k.shape)],
        out_specs=[o[2] for o in outs],
        out_shape=[jax.ShapeDtypeStruct(o[0], o[1]) for o in outs],
        scratch_shapes=[pltpu.VMEM((1, 128), F32)],
        compiler_params=_params(56, 1),
    )(x, g1, wcat, bdiag, gq, gk, bfor, tri, pdq, pdk, ones_q, ones_k)


def _attn_fwd(qa, ka, vt):
    s_len = qa.shape[0]
    t = ATTN_TILE
    nb = s_len // t

    def body(q_ref, k_ref, vt_ref, o_ref, ot_ref, lse_ref, m_sc, l_sc, acc_sc, s_sc, mcur_sc,
             alpha_sc):
        i = pl.program_id(0)
        m_sc[...] = jnp.full_like(m_sc, -jnp.inf)
        l_sc[...] = jnp.zeros_like(l_sc)
        acc_sc[...] = jnp.zeros_like(acc_sc)

        def logits(j, slot, masked):
            krows = pl.ds(pl.multiple_of(j * t, t), t)
            if masked:
                keep = (lax.broadcasted_iota(jnp.int32, (t, t), 0)
                        <= lax.broadcasted_iota(jnp.int32, (t, t), 1))
            for hd in range(HEADS):
                sl = slice(hd * 128, (hd + 1) * 128)
                st = _dot_nt(k_ref[krows, sl], q_ref[:, sl])
                if masked:
                    st = jnp.where(keep, st, -jnp.inf)
                s_sc[slot, hd] = st
                m_prev = m_sc[hd:hd + 1, :]
                m_new = jnp.maximum(m_prev, jnp.max(st, axis=0, keepdims=True))
                alpha_sc[slot, hd:hd + 1, :] = jnp.exp2(m_prev - m_new)
                mcur_sc[slot, hd:hd + 1, :] = m_new
                m_sc[hd:hd + 1, :] = m_new

        def accumulate(j, slot):
            for hd in range(HEADS):
                hr = slice(hd * HEAD_DIM, (hd + 1) * HEAD_DIM)
                alpha = alpha_sc[slot, hd:hd + 1, :]
                pt = jnp.exp2(s_sc[slot, hd] - mcur_sc[slot, hd:hd + 1, :])
                l_sc[hd:hd + 1, :] = alpha * l_sc[hd:hd + 1, :] + jnp.sum(pt, axis=0, keepdims=True)
                acc_sc[hr, :] = alpha * acc_sc[hr, :] + jnp.dot(
                    vt_ref[j, hr, :], pt.astype(BF16), preferred_element_type=F32)

        @pl.when(i == 0)
        def _():
            logits(0, 0, True)
            accumulate(0, 0)

        pairs = (i - 1) // 2

        @pl.when(i > 0)
        def _():
            logits(0, 0, False)

            def two_blocks(p, carry):
                logits(2 * p + 1, 1, False)
                accumulate(2 * p, 0)
                logits(2 * p + 2, 0, False)
                accumulate(2 * p + 1, 1)
                return carry

            lax.fori_loop(0, pairs, two_blocks, 0)

        @pl.when((i > 0) & (i - 2 * pairs == 1))
        def _():
            logits(i, 1, True)
            accumulate(i - 1, 0)
            accumulate(i, 1)

        @pl.when((i > 0) & (i - 2 * pairs == 2))
        def _():
            logits(i - 1, 1, False)
            accumulate(i - 2, 0)
            logits(i, 0, True)
            accumulate(i - 1, 1)
            accumulate(i, 0)

        for hd in range(HEADS):
            hr = slice(hd * HEAD_DIM, (hd + 1) * HEAD_DIM)
            l = l_sc[hd:hd + 1, :]
            acc_sc[hr, :] = acc_sc[hr, :] / l
            lse_ref[0, hd:hd + 1, :] = m_sc[hd:hd + 1, :] + jnp.log2(l)
        o_ref[...] = acc_sc[...].T.astype(BF16)
        ot_ref[...] = acc_sc[...].astype(BF16)

    return pl.pallas_call(
        body, name="attn_fwd", grid=(nb,),
        in_specs=[_row_spec(t, SLAB_W), _const_spec(ka.shape), _const_spec(vt.shape)],
        out_specs=[_row_spec(t, FOX_W), pl.BlockSpec((FOX_W, t), lambda i: (0, i)),
                   _tile_spec(HEADS, t)],
        out_shape=[jax.ShapeDtypeStruct((s_len, FOX_W), BF16),
                   jax.ShapeDtypeStruct((FOX_W, s_len), BF16),
                   jax.ShapeDtypeStruct((nb, HEADS, t), F32)],
        scratch_shapes=[pltpu.VMEM((HEADS, t), F32), pltpu.VMEM((HEADS, t), F32),
                        pltpu.VMEM((FOX_W, t), F32), pltpu.VMEM((2, HEADS, t, t), F32),
                        pltpu.VMEM((2, HEADS, t), F32), pltpu.VMEM((2, HEADS, t), F32)],
        compiler_params=_params(48, 1),
    )(qa, ka, vt)


def _sgu_mix(vn, ws_ref):
    tm = vn.shape[0]
    lane = lax.broadcasted_iota(jnp.int32, (WINDOW, 128), 1)
    low = lane < HEAD_DIM
    wins = []
    for w in range(tm // WINDOW):
        slabs = []
        for p in range(GROUPS // 2):
            v2 = vn[w * WINDOW:(w + 1) * WINDOW, p * 128:(p + 1) * 128]
            lo = jnp.where(low, v2, 0.0).astype(BF16)
            hi = jnp.where(low, 0.0, v2).astype(BF16)
            slabs.append(jnp.dot(ws_ref[2 * p], lo, preferred_element_type=F32)
                         + jnp.dot(ws_ref[2 * p + 1], hi, preferred_element_type=F32))
        wins.append(jnp.concatenate(slabs, axis=1))
    return jnp.concatenate(wins, axis=0) if len(wins) > 1 else wins[0]


def _layernorm_fwd(vv, g, b):
    mu = jnp.mean(vv, axis=-1, keepdims=True)
    xc = vv - mu
    r = lax.rsqrt(jnp.mean(xc * xc, axis=-1, keepdims=True) + EPS)
    xh = xc * r
    return xh * g + b, xh, r


def _mix_fwd(attn, uvpre, gpre, x, wa, wb, wout, wsm, bsf, gsgu, bsgu, gpost):
    s_len = x.shape[0]
    tm = TOKEN_TILE

    def body(o_ref, uv_ref, gp_ref, x_ref, wa_ref, wb_ref, wo_ref, ws_ref, bs_ref, gs_ref, bsg_ref,
             gpost_ref, sgut_ref, ya_ref, yb_ref, mgt_ref, om_ref, x1_ref):
        uvp = uv_ref[...].astype(F32)
        uv, _ = _gelu_and_grad(uvp)
        u, vv = uv[:, :SGU_W], uv[:, SGU_W:]
        vn, _, _ = _layernorm_fwd(vv, gs_ref[...], bsg_ref[...])
        bias = bs_ref[...]
        if tm > WINDOW:
            bias = jnp.concatenate([bias] * (tm // WINDOW), axis=0)
        mixed = _sgu_mix(vn, ws_ref) + bias
        sgu_f = u * mixed
        sgu = sgu_f.astype(BF16)
        sgut_ref[...] = sgu_f.T.astype(BF16)
        ya = jnp.dot(o_ref[...], wa_ref[...], preferred_element_type=F32)
        yb = jnp.dot(sgu, wb_ref[...], preferred_element_type=F32)
        ya_ref[...] = ya.astype(BF16)
        yb_ref[...] = yb.astype(BF16)
        gates = _sigmoid(gp_ref[...].astype(F32))
        merged_f = gates[:, :D_MODEL] * ya + gates[:, D_MODEL:] * yb
        merged = merged_f.astype(BF16)
        mgt_ref[...] = merged_f.T.astype(BF16)
        om = jnp.dot(merged, wo_ref[...], preferred_element_type=F32)
        om_ref[...] = om
        r = lax.rsqrt(jnp.mean(om * om, axis=-1, keepdims=True) + EPS)
        x1_ref[...] = x_ref[...] + om * r * gpost_ref[...]

    def t_out(rows):
        return ((rows, s_len), BF16, pl.BlockSpec((rows, tm), lambda i: (0, i)))

    def r_out(cols, dt):
        return ((s_len, cols), dt, _row_spec(tm, cols))

    outs = [t_out(SGU_W), r_out(D_MODEL, BF16), r_out(D_MODEL, BF16), t_out(D_MODEL),
            r_out(D_MODEL, F32), r_out(D_MODEL, F32)]
    return pl.pallas_call(
        body, name="mix_fwd", grid=(s_len // tm,),
        in_specs=[_row_spec(tm, FOX_W), _row_spec(tm, 2 * SGU_W), _row_spec(tm, 2 * D_MODEL),
                  _row_spec(tm, D_MODEL), _const_spec(wa.shape), _const_spec(wb.shape),
                  _const_spec(wout.shape), _const_spec(wsm.shape), _const_spec(bsf.shape),
                  _const_spec((1, SGU_W)), _const_spec((1, SGU_W)), _const_spec((1, D_MODEL))],
        out_specs=[o[2] for o in outs],
        out_shape=[jax.ShapeDtypeStruct(o[0], o[1]) for o in outs],
        compiler_params=_params(48, 1),
    )(attn, uvpre, gpre, x, wa, wb, wout, wsm, bsf, gsgu, bsgu, gpost)


def _ffn_fwd_bwd(x1, tgt, wffn, wdown, gpre, gpost):
    s_len = x1.shape[0]
    tm = TOKEN_TILE

    def body(x1_ref, t_ref, wi_ref, wd_ref, gpre_ref, gpost_ref,
             dx1_ref, h2_ref, actt_ref, dff_ref, dgut_ref, loss_ref, dgpost_ref, dgpre_ref):
        @pl.when(pl.program_id(0) == 0)
        def _():
            loss_ref[...] = jnp.zeros_like(loss_ref)
            dgpost_ref[...] = jnp.zeros_like(dgpost_ref)
            dgpre_ref[...] = jnp.zeros_like(dgpre_ref)

        x1v = x1_ref[...]
        r2 = lax.rsqrt(jnp.mean(x1v * x1v, axis=-1, keepdims=True) + EPS)
        gpre_v = gpre_ref[...]
        h2 = (x1v * r2 * gpre_v).astype(BF16)
        h2_ref[...] = h2
        gg = _dot_nt(h2, wi_ref[:D_FF, :])
        uu = _dot_nt(h2, wi_ref[D_FF:, :])
        sg = _sigmoid(gg)
        silu = gg * sg
        act_f = silu * uu
        act = act_f.astype(BF16)
        actt_ref[...] = act_f.T.astype(BF16)
        ff = jnp.dot(act, wd_ref[...], preferred_element_type=F32)
        r3 = lax.rsqrt(jnp.mean(ff * ff, axis=-1, keepdims=True) + EPS)
        gpost_v = gpost_ref[...]
        y = x1v + ff * r3 * gpost_v
        err = y - t_ref[...]
        loss_ref[...] += jnp.sum(err * err) * (0.5 / D_MODEL)
        dy = err * (1.0 / D_MODEL)
        dgpost_ref[...] += jnp.sum(dy * ff * r3, axis=0, keepdims=True)
        dff = _rms_bwd(ff, r3, gpost_v, dy).astype(BF16)
        dff_ref[...] = dff
        dact = _dot_nt(dff, wd_ref[...])
        dgg_f = dact * uu * (sg * (1.0 + gg * (1.0 - sg)))
        duu_f = dact * silu
        dgg = dgg_f.astype(BF16)
        duu = duu_f.astype(BF16)
        dgut_ref[:D_FF, :] = dgg_f.T.astype(BF16)
        dgut_ref[D_FF:, :] = duu_f.T.astype(BF16)
        dh2 = (jnp.dot(dgg, wi_ref[:D_FF, :], preferred_element_type=F32)
               + jnp.dot(duu, wi_ref[D_FF:, :], preferred_element_type=F32))
        dgpre_ref[...] += jnp.sum(dh2 * x1v * r2, axis=0, keepdims=True)
        dx1_ref[...] = dy + _rms_bwd(x1v, r2, gpre_v, dh2)

    outs = [((s_len, D_MODEL), F32, _row_spec(tm, D_MODEL)),
            ((s_len, D_MODEL), BF16, _row_spec(tm, D_MODEL)),
            ((D_FF, s_len), BF16, pl.BlockSpec((D_FF, tm), lambda i: (0, i))),
            ((s_len, D_MODEL), BF16, _row_spec(tm, D_MODEL)),
            ((2 * D_FF, s_len), BF16, pl.BlockSpec((2 * D_FF, tm), lambda i: (0, i))),
            ((1, 128), F32, _const_spec((1, 128))),
            ((1, D_MODEL), F32, _const_spec((1, D_MODEL))),
            ((1, D_MODEL), F32, _const_spec((1, D_MODEL)))]
    return pl.pallas_call(
        body, name="ffn_fwd_bwd", grid=(s_len // tm,),
        in_specs=[_row_spec(tm, D_MODEL), _row_spec(tm, D_MODEL), _const_spec(wffn.shape),
                  _const_spec(wdown.shape), _const_spec((1, D_MODEL)), _const_spec((1, D_MODEL))],
        out_specs=[o[2] for o in outs],
        out_shape=[jax.ShapeDtypeStruct(o[0], o[1]) for o in outs],
        compiler_params=_params(60, 1),
    )(x1, tgt, wffn, wdown, gpre, gpost)


def _mix_bwd(dx1, om, ya, yb, gpre, uvpre, attn, wout, wa, wb, wsm, wsmt, bsf, gsgu, bsgu, gpost,
             wmask, egrp):
    s_len = dx1.shape[0]
    tm = TOKEN_TILE
    nw = tm // WINDOW
    nt = s_len // tm

    def body(dx1_ref, om_ref, ya_ref, yb_ref, gp_ref, uv_ref, o_ref, wo_ref, wa_ref, wb_ref, ws_ref,
             wst_ref, bs_ref, gs_ref, bsg_ref, gpost_ref, mask_ref, eg_ref,
             dom_ref, dya_ref, dyb_ref, dgp_ref, dot_ref, delta_ref, duv_ref,
             dws_ref, dbs_ref, dgs_ref, dbsg_ref, dgpost_ref, dbs_acc):
        step = pl.program_id(0)

        @pl.when(step == 0)
        def _():
            dws_ref[...] = jnp.zeros_like(dws_ref)
            dbs_acc[...] = jnp.zeros_like(dbs_acc)
            dgs_ref[...] = jnp.zeros_like(dgs_ref)
            dbsg_ref[...] = jnp.zeros_like(dbsg_ref)
            dgpost_ref[...] = jnp.zeros_like(dgpost_ref)

        om = om_ref[...]
        dx1v = dx1_ref[...]
        r = lax.rsqrt(jnp.mean(om * om, axis=-1, keepdims=True) + EPS)
        gpost_v = gpost_ref[...]
        dgpost_ref[...] += jnp.sum(dx1v * om * r, axis=0, keepdims=True)
        dom = _rms_bwd(om, r, gpost_v, dx1v).astype(BF16)
        dom_ref[...] = dom
        dmg = _dot_nt(dom, wo_ref[...])

        gates = _sigmoid(gp_ref[...].astype(F32))
        ga, gb = gates[:, :D_MODEL], gates[:, D_MODEL:]
        yav, ybv = ya_ref[...].astype(F32), yb_ref[...].astype(F32)
        dya = (dmg * ga).astype(BF16)
        dyb = (dmg * gb).astype(BF16)
        dya_ref[...] = dya
        dyb_ref[...] = dyb
        dgp_ref[:, :D_MODEL] = (dmg * yav * ga * (1.0 - ga)).astype(BF16)
        dgp_ref[:, D_MODEL:] = (dmg * ybv * gb * (1.0 - gb)).astype(BF16)

        dat_t = _dot_nt(dya, wa_ref[...]).T.astype(BF16)
        dot_ref[0] = dat_t
        o_t = o_ref[...].astype(F32).T
        delta_ref[0] = jnp.sum((dat_t.astype(F32) * o_t).reshape(HEADS, HEAD_DIM, tm), axis=1)
        dsgu = _dot_nt(dyb, wb_ref[...])

        uvp = uv_ref[...].astype(F32)
        uv, guv = _gelu_and_grad(uvp)
        u, vv = uv[:, :SGU_W], uv[:, SGU_W:]
        gs_v = gs_ref[...]
        vn, xh, rln = _layernorm_fwd(vv, gs_v, bsg_ref[...])
        bias = bs_ref[...]
        if nw > 1:
            bias = jnp.concatenate([bias] * nw, axis=0)
        mixed = _sgu_mix(vn, ws_ref) + bias
        du = dsgu * mixed
        dmixed = dsgu * u

        lane = lax.broadcasted_iota(jnp.int32, (WINDOW, 128), 1)
        low = lane < HEAD_DIM
        dvn_wins = []
        for w in range(nw):
            rows = slice(w * WINDOW, (w + 1) * WINDOW)
            dbs_acc[...] += dmixed[rows, :]
            slabs = []
            for p in range(GROUPS // 2):
                cols = slice(p * 128, (p + 1) * 128)
                dm2 = dmixed[rows, cols]
                dlo = jnp.where(low, dm2, 0.0).astype(BF16)
                dhi = jnp.where(low, 0.0, dm2).astype(BF16)
                vn2 = vn[rows, cols].astype(BF16)
                dws_ref[2 * p] += _dot_nt(dlo, vn2)
                dws_ref[2 * p + 1] += _dot_nt(dhi, vn2)
                slabs.append(jnp.dot(wst_ref[2 * p], dlo, preferred_element_type=F32)
                             + jnp.dot(wst_ref[2 * p + 1], dhi, preferred_element_type=F32))
            dvn_wins.append(jnp.concatenate(slabs, axis=1))
        dvn = jnp.concatenate(dvn_wins, axis=0) if nw > 1 else dvn_wins[0]

        dgs_ref[...] += jnp.sum(dvn * xh, axis=0, keepdims=True)
        dbsg_ref[...] += jnp.sum(dvn, axis=0, keepdims=True)
        dxh = dvn * gs_v
        dvv = rln * (dxh - jnp.mean(dxh, axis=-1, keepdims=True)
                     - xh * jnp.mean(dxh * xh, axis=-1, keepdims=True))
        duv_ref[:, :SGU_W] = (du * guv[:, :SGU_W]).astype(BF16)
        duv_ref[:, SGU_W:] = (dvv * guv[:, SGU_W:]).astype(BF16)

        @pl.when(step == pl.num_programs(0) - 1)
        def _():
            for g in range(GROUPS):
                dws_ref[g] = dws_ref[g] * mask_ref[...]
            dbs_ref[...] = _split3_dot(dbs_acc[...], eg_ref[...])

    rows_out = [((s_len, D_MODEL), BF16, _row_spec(tm, D_MODEL)),
                ((s_len, D_MODEL), BF16, _row_spec(tm, D_MODEL)),
                ((s_len, D_MODEL), BF16, _row_spec(tm, D_MODEL)),
                ((s_len, 2 * D_MODEL), BF16, _row_spec(tm, 2 * D_MODEL)),
                ((nt, FOX_W, tm), BF16, _tile_spec(FOX_W, tm)),
                ((nt, HEADS, tm), F32, _tile_spec(HEADS, tm)),
                ((s_len, 2 * SGU_W), BF16, _row_spec(tm, 2 * SGU_W))]
    acc_out = [((GROUPS, WINDOW, WINDOW), F32), ((WINDOW, 128), F32), ((1, SGU_W), F32),
               ((1, SGU_W), F32), ((1, D_MODEL), F32)]
    return pl.pallas_call(
        body, name="mix_bwd", grid=(nt,),
        in_specs=[_row_spec(tm, D_MODEL), _row_spec(tm, D_MODEL), _row_spec(tm, D_MODEL),
                  _row_spec(tm, D_MODEL), _row_spec(tm, 2 * D_MODEL), _row_spec(tm, 2 * SGU_W),
                  _row_spec(tm, FOX_W), _const_spec(wout.shape), _const_spec(wa.shape),
                  _const_spec(wb.shape), _const_spec(wsm.shape), _const_spec(wsmt.shape),
                  _const_spec(bsf.shape), _const_spec((1, SGU_W)), _const_spec((1, SGU_W)),
                  _const_spec((1, D_MODEL)), _const_spec(wmask.shape), _const_spec(egrp.shape)],
        out_specs=[o[2] for o in rows_out] + [_const_spec(s) for s, _ in acc_out],
        out_shape=[jax.ShapeDtypeStruct(o[0], o[1]) for o in rows_out]
        + [jax.ShapeDtypeStruct(s, dt) for s, dt in acc_out],
        scratch_shapes=[pltpu.VMEM((WINDOW, SGU_W), F32)],
        compiler_params=_params(48, 1),
    )(dx1, om, ya, yb, gpre, uvpre, attn, wout, wa, wb, wsm, wsmt, bsf, gsgu, bsgu, gpost, wmask,
      egrp)


def _attn_bwd(qa, ka, kat, vs, dot_, lse, delta, ecol):
    s_len = qa.shape[0]
    t = ATTN_TILE
    nb = s_len // t

    def body(k_ref, kt_ref, vs_ref, q_ref, do_ref, lse_ref, dl_ref, ec_ref, gk_ref, dvt_ref,
             gqt_ref, csum_ref, p_sc, ds_sc):
        j = pl.program_id(0)

        @pl.when(j == 0)
        def _():
            gqt_ref[...] = jnp.zeros_like(gqt_ref)

        gk_ref[...] = jnp.zeros_like(gk_ref)
        dvt_ref[...] = jnp.zeros_like(dvt_ref)

        def probs(i, slot, masked):
            qrows = pl.ds(pl.multiple_of(i * t, t), t)
            if masked:
                keep = (lax.broadcasted_iota(jnp.int32, (t, t), 0)
                        <= lax.broadcasted_iota(jnp.int32, (t, t), 1))
            for hd in range(HEADS):
                sl = slice(hd * 128, (hd + 1) * 128)
                hr = slice(hd * HEAD_DIM, (hd + 1) * HEAD_DIM)
                st = _dot_nt(k_ref[:, sl], q_ref[qrows, sl])
                if masked:
                    st = jnp.where(keep, st, -jnp.inf)
                pt = jnp.exp2(st - lse_ref[i, hd:hd + 1, :])
                dpt = jnp.dot(vs_ref[:, hd * 128:hd * 128 + HEAD_DIM], do_ref[i, hr, :],
                              preferred_element_type=F32)
                p_sc[slot, hd] = pt.astype(BF16)
                ds_sc[slot, hd] = (pt * (dpt - dl_ref[i, hd:hd + 1, :])).astype(BF16)

        def grads(i, slot):
            qrows = pl.ds(pl.multiple_of(i * t, t), t)
            for hd in range(HEADS):
                sl = slice(hd * 128, (hd + 1) * 128)
                hr = slice(hd * HEAD_DIM, (hd + 1) * HEAD_DIM)
                dst = ds_sc[slot, hd]
                dvt_ref[0, hr, :] += _dot_nt(do_ref[i, hr, :], p_sc[slot, hd])
                gk_ref[:, sl] += jnp.dot(dst, q_ref[qrows, sl], preferred_element_type=F32)
                gqt_ref[i, hd * QT_ROWS:(hd + 1) * QT_ROWS, :] += jnp.dot(
                    kt_ref[0, hd * 128:hd * 128 + QT_ROWS, :], dst, preferred_element_type=F32)

        probs(j, 0, True)
        pairs = (nb - 1 - j) // 2

        def two_blocks(p, carry):
            i1 = j + 1 + 2 * p
            probs(i1, 1, False)
            grads(i1 - 1, 0)
            probs(i1 + 1, 0, False)
            grads(i1, 1)
            return carry

        lax.fori_loop(0, pairs, two_blocks, 0)

        @pl.when(nb - 1 - j - 2 * pairs == 0)
        def _():
            grads(nb - 1, 0)

        @pl.when(nb - 1 - j - 2 * pairs == 1)
        def _():
            probs(nb - 1, 1, False)
            grads(nb - 2, 0)
            grads(nb - 1, 1)

        csum_ref[...] = _split3_dot(gk_ref[...], ec_ref[...])

    return pl.pallas_call(
        body, name="attn_bwd", grid=(nb,),
        in_specs=[_row_spec(t, SLAB_W), _tile_spec(SLAB_W, t), _row_spec(t, SLAB_W),
                  _const_spec(qa.shape), _const_spec(dot_.shape), _const_spec(lse.shape),
                  _const_spec(delta.shape), _const_spec(ecol.shape)],
        out_specs=[_row_spec(t, SLAB_W), _tile_spec(FOX_W, t),
                   _const_spec((nb, HEADS * QT_ROWS, t)), _row_spec(t, 128)],
        out_shape=[jax.ShapeDtypeStruct((s_len, SLAB_W), F32),
                   jax.ShapeDtypeStruct((nb, FOX_W, t), F32),
                   jax.ShapeDtypeStruct((nb, HEADS * QT_ROWS, t), F32),
                   jax.ShapeDtypeStruct((s_len, 128), F32)],
        scratch_shapes=[pltpu.VMEM((2, HEADS, t, t), BF16), pltpu.VMEM((2, HEADS, t, t), BF16)],
        compiler_params=_params(60, 1),
    )(ka, kat, vs, qa, dot_, lse, delta, ecol)


def _rev_cumsum(col_sums, gqt, triu):
    s_len = col_sums.shape[0]
    tm = TOKEN_TILE
    n = s_len // tm

    def body(cs_ref, gqt_ref, tri_ref, o_ref, carry):
        @pl.when(pl.program_id(0) == 0)
        def _():
            carry[...] = jnp.zeros_like(carry)
        rows = [gqt_ref[0, hd * QT_ROWS + HEAD_DIM:hd * QT_ROWS + HEAD_DIM + 1, :]
                for hd in range(HEADS)]
        row_sums = jnp.concatenate(rows + [jnp.zeros((128 - HEADS, tm), F32)], axis=0).T
        out = _tri_dot(tri_ref[...], row_sums - cs_ref[...]) + carry[...]
        o_ref[...] = out
        carry[...] = out[0:1, :]

    return pl.pallas_call(
        body, name="rev_cumsum", grid=(n,),
        in_specs=[pl.BlockSpec((tm, 128), lambda i: (n - 1 - i, 0)),
                  pl.BlockSpec((1, HEADS * QT_ROWS, tm), lambda i: (n - 1 - i, 0, 0)),
                  _const_spec((tm, tm))],
        out_specs=pl.BlockSpec((tm, 128), lambda i: (n - 1 - i, 0)),
        out_shape=jax.ShapeDtypeStruct((s_len, 128), F32),
        scratch_shapes=[pltpu.VMEM((1, 128), F32)],
        compiler_params=_params(32, 1),
    )(col_sums, gqt, triu)


def _heads_from_slabs(slabs):
    lane = lax.broadcasted_iota(jnp.int32, slabs[0].shape, 1)
    low = lane < HEAD_DIM
    pairs = [jnp.where(low, slabs[2 * p], pltpu.roll(slabs[2 * p + 1], HEAD_DIM, 1))
             for p in range(HEADS // 2)]
    return jnp.concatenate(pairs, axis=1)


def _proj_bwd(gqt, gk, dvt, dlogf, flog, qraw, kraw, duv, dgp, x, dx1, wcat, bdiag, gq, gk_gain, g1,
              efold):
    s_len = x.shape[0]
    tm = TOKEN_TILE

    def body(gqt_ref, gkk_ref, dvt_ref, dlf_ref, flog_ref, qr_ref, kr_ref, duv_ref, dgp_ref, x_ref,
             dx1_ref, w_ref, bd_ref, gq_ref, gk_ref, g1_ref, ef_ref,
             dx_ref, dprojt_ref, dgq_ref, dgk_ref, dbf_ref, dg1_ref, gq_acc, gk_acc):
        step = pl.program_id(0)

        @pl.when(step == 0)
        def _():
            gq_acc[...] = jnp.zeros_like(gq_acc)
            gk_acc[...] = jnp.zeros_like(gk_acc)
            dbf_ref[...] = jnp.zeros_like(dbf_ref)
            dg1_ref[...] = jnp.zeros_like(dg1_ref)

        pad = jnp.zeros((128 - QT_ROWS, tm), F32)
        q_slabs = [jnp.concatenate([gqt_ref[0, hd * QT_ROWS:(hd + 1) * QT_ROWS, :], pad], axis=0).T
                   for hd in range(HEADS)]
        dqn = _heads_from_slabs(q_slabs)
        dkn = _heads_from_slabs([gkk_ref[:, hd * 128:(hd + 1) * 128] for hd in range(HEADS)])

        def head_bwd(raw_ref, dn, g_ref, acc):
            raw = raw_ref[...].astype(F32)
            r = lax.rsqrt(_seg_mean(raw * raw, bd_ref) + EPS)
            xhat = raw * r
            acc[0:1, :] += jnp.sum(dn * xhat, axis=0, keepdims=True)
            dyg = dn * g_ref[...]
            return r * (dyg - xhat * _seg_mean(dyg * xhat, bd_ref))

        dot = functools.partial(jnp.dot, preferred_element_type=F32)
        duv, dgp = duv_ref[...], dgp_ref[...]
        dprojt_ref[C_UV:C_G, :] = duv.astype(F32).T.astype(BF16)
        dprojt_ref[C_G:C_F, :] = dgp.astype(F32).T.astype(BF16)
        dh = dot(duv, w_ref[C_UV:C_G, :]) + dot(dgp, w_ref[C_G:C_F, :])

        dq = head_bwd(qr_ref, dqn * HEAD_DIM ** -0.5, gq_ref, gq_acc)
        dk = head_bwd(kr_ref, dkn * LN2, gk_ref, gk_acc)
        dv_t = dvt_ref[0]
        dfl = dlf_ref[...] * _sigmoid(-flog_ref[...])
        dbf_ref[...] += jnp.sum(dfl, axis=0, keepdims=True)
        dprojt_ref[C_Q:C_K, :] = dq.T.astype(BF16)
        dprojt_ref[C_K:C_V, :] = dk.T.astype(BF16)
        dprojt_ref[C_V:C_UV, :] = dv_t.astype(BF16)
        dprojt_ref[C_F:C_END, :] = dfl.T.astype(BF16)
        dh = (dh + dot(dq.astype(BF16), w_ref[C_Q:C_K, :]) + dot(dk.astype(BF16), w_ref[C_K:C_V, :])
              + dot(dv_t.T.astype(BF16), w_ref[C_V:C_UV, :])
              + dot(dfl.astype(BF16), w_ref[C_F:C_END, :]))
        xf = x_ref[...]
        r = lax.rsqrt(jnp.mean(xf * xf, axis=-1, keepdims=True) + EPS)
        dg1_ref[...] += jnp.sum(dh * xf * r, axis=0, keepdims=True)
        dx_ref[...] = dx1_ref[...] + _rms_bwd(xf, r, g1_ref[...], dh)

        @pl.when(step == pl.num_programs(0) - 1)
        def _():
            dgq_ref[...] = _split3_dot(gq_acc[...], ef_ref[...])
            dgk_ref[...] = _split3_dot(gk_acc[...], ef_ref[...])

    outs = [((s_len, D_MODEL), F32, _row_spec(tm, D_MODEL)),
            ((C_END, s_len), BF16, pl.BlockSpec((C_END, tm), lambda i: (0, i))),
            ((8, 128), F32, _const_spec((8, 128))),
            ((8, 128), F32, _const_spec((8, 128))),
            ((1, 128), F32, _const_spec((1, 128))),
            ((1, D_MODEL), F32, _const_spec((1, D_MODEL)))]
    return pl.pallas_call(
        body, name="proj_bwd", grid=(s_len // tm,),
        in_specs=[_tile_spec(HEADS * QT_ROWS, tm), _row_spec(tm, SLAB_W), _tile_spec(FOX_W, tm),
                  _row_spec(tm, 128), _row_spec(tm, 128), _row_spec(tm, FOX_W),
                  _row_spec(tm, FOX_W), _row_spec(tm, 2 * SGU_W), _row_spec(tm, 2 * D_MODEL),
                  _row_spec(tm, D_MODEL), _row_spec(tm, D_MODEL), _const_spec(wcat.shape),
                  _const_spec(bdiag.shape), _const_spec((1, FOX_W)), _const_spec((1, FOX_W)),
                  _const_spec((1, D_MODEL)), _const_spec(efold.shape)],
        out_specs=[o[2] for o in outs],
        out_shape=[jax.ShapeDtypeStruct(o[0], o[1]) for o in outs],
        scratch_shapes=[pltpu.VMEM((8, FOX_W), F32), pltpu.VMEM((8, FOX_W), F32)],
        compiler_params=_params(56, 1),
    )(gqt, gk, dvt, dlogf, flog, qraw, kraw, duv, dgp, x, dx1, wcat, bdiag, gq, gk_gain, g1, efold)


def _dw_matmul(at, b, tm, name, after=()):
    m, s_len = at.shape
    n = b.shape[1]

    def body(a_ref, b_ref, *rest):
        rest[-1][...] = jnp.dot(a_ref[...], b_ref[...], preferred_element_type=F32).astype(BF16)

    return pl.pallas_call(
        body, name=name, grid=(m // tm,),
        in_specs=[pl.BlockSpec((tm, s_len), lambda i: (i, 0)), _const_spec(b.shape)]
        + [pl.BlockSpec(memory_space=pl.ANY)] * len(after),
        out_specs=pl.BlockSpec((tm, n), lambda i: (i, 0)),
        out_shape=jax.ShapeDtypeStruct((m, n), BF16),
        compiler_params=_params(48, 1),
    )(at, b, *after)


def _adamw(parts, w, m, v, tr, name, col_tile=None, select=None):
    parts = parts if isinstance(parts, (list, tuple)) else [parts]
    rows, cols = w.shape
    extra = [] if select is None else [select]
    bc1 = 1.0 - ADAM_B1 ** ADAM_STEP
    bc2 = 1.0 - ADAM_B2 ** ADAM_STEP

    def body(*refs):
        p_refs = refs[:len(parts)]
        sel_refs = refs[len(parts):len(parts) + len(extra)]
        w_ref, m_ref, v_ref, g_ref, d_ref, mo_ref, vo_ref = refs[len(parts) + len(extra):]
        g = None
        for p_ref, p in zip(p_refs, parts):
            for idx in range(p.shape[0]):
                term = p_ref[idx].astype(F32)
                g = term if g is None else g + term
        if sel_refs:
            g = _tri_dot(sel_refs[0][...], g)
        g_ref[...] = g
        mn = ADAM_B1 * m_ref[...] + (1.0 - ADAM_B1) * g
        vn = ADAM_B2 * v_ref[...] + (1.0 - ADAM_B2) * (g * g)
        mo_ref[...] = mn
        vo_ref[...] = vn
        m_hat = mn / bc1
        v_hat = vn / bc2
        d_ref[...] = -ADAM_LR * (m_hat / (jnp.sqrt(v_hat) + ADAM_EPS) + ADAM_WD * w_ref[...])

    if col_tile is None:
        spec = pl.BlockSpec((tr, cols), lambda i: (i, 0))
        pspecs = [pl.BlockSpec((p.shape[0], tr, cols), lambda i: (0, i, 0)) for p in parts]
        steps = rows // tr
    else:
        spec = pl.BlockSpec((rows, col_tile), lambda i: (0, i))
        pspecs = [pl.BlockSpec((p.shape[0], p.shape[1], col_tile), lambda i: (0, 0, i))
                  for p in parts]
        steps = cols // col_tile
    return pl.pallas_call(
        body, name=name, grid=(steps,),
        in_specs=pspecs + [_const_spec(e.shape) for e in extra] + [spec, spec, spec],
        out_specs=[spec] * 4,
        out_shape=[jax.ShapeDtypeStruct((rows, cols), F32)] * 4,
        compiler_params=_params(48, 1),
    )(*parts, *extra, w, m, v)


def _sum_parts(parts, name):
    n, rows, cols = parts.shape

    def body(p_ref, o_ref):
        g = p_ref[0]
        for idx in range(1, n):
            g = g + p_ref[idx]
        o_ref[...] = g

    return pl.pallas_call(
        body, name=name, out_shape=jax.ShapeDtypeStruct((rows, cols), F32),
        in_specs=[_const_spec(parts.shape)], out_specs=_const_spec((rows, cols)), grid=(1,),
        compiler_params=_params(16, 1),
    )(parts)


VEC_NAMES = ("g_pre_mix", "b_forget", "g_q", "g_k", "g_sgu", "b_sgu", "b_spatial", "g_post_mix",
             "g_pre_ffn", "g_post_ffn")
VEC_ROWS = 16
LOSS_ROW = len(VEC_NAMES)


def _pack_vectors(d, loss_row):
    rows = []
    for k in VEC_NAMES:
        flat = d[k].reshape(1, -1).astype(F32)
        rows.append(jnp.pad(flat, ((0, 0), (0, 1024 - flat.shape[1]))))
    rows.append(loss_row)
    rows.append(jnp.zeros((VEC_ROWS - len(rows), 1024), F32))
    return jnp.concatenate(rows, axis=0)


def _adamw_vectors(grad_rows, ws, ms, vs):
    n = len(VEC_NAMES)
    bc1 = 1.0 - ADAM_B1 ** ADAM_STEP
    bc2 = 1.0 - ADAM_B2 ** ADAM_STEP

    def step(g, w, m, v):
        mn = ADAM_B1 * m + (1.0 - ADAM_B1) * g
        vn = ADAM_B2 * v + (1.0 - ADAM_B2) * (g * g)
        delta = -ADAM_LR * ((mn / bc1) / (jnp.sqrt(vn / bc2) + ADAM_EPS) + ADAM_WD * w)
        return g, delta, mn, vn

    def body(*refs):
        g_ref = refs[0]
        ins = [refs[1 + j * n:1 + (j + 1) * n] for j in range(3)]
        outs = [refs[1 + (3 + j) * n:1 + (4 + j) * n] for j in range(4)]
        for i in range(n):
            shape = ws[i].shape
            if len(shape) == 2:
                res = step(g_ref[i:i + 1, :shape[1]], *[r[i][...] for r in ins])
                for o, val in zip(outs, res):
                    o[i][...] = val
            else:
                for r in range(shape[1]):
                    res = step(g_ref[i:i + 1, r * shape[2]:(r + 1) * shape[2]],
                               *[q[i][0, r:r + 1, :] for q in ins])
                    for o, val in zip(outs, res):
                        o[i][0, r:r + 1, :] = val

    vmem = pl.BlockSpec(memory_space=pltpu.VMEM)
    flat = pl.pallas_call(
        body, name="adamw_vectors",
        in_specs=[vmem] * (1 + 3 * n), out_specs=[vmem] * (4 * n),
        out_shape=[jax.ShapeDtypeStruct(w.shape, F32) for _ in range(4) for w in ws],
    )(grad_rows, *ws, *ms, *vs)
    return [flat[j * n:(j + 1) * n] for j in range(4)]


def _cols_to_blocks(full, width):
    r = full.shape[0]
    return jnp.transpose(full.reshape(r, N_DEV, width), (1, 0, 2))


def _blocks_to_cols(blocks):
    n, r, width = blocks.shape
    return jnp.transpose(blocks, (1, 0, 2)).reshape(r, n * width)


def kernel(x, g_pre_mix, w_in, b_forget, g_q, g_k, g_sgu, b_sgu, w_spatial, b_spatial, w_branch_a, w_branch_b, w_out, g_post_mix, g_pre_ffn, w_ffn_in, w_ffn_down, g_post_ffn, loss_target, m_g_pre_mix, m_w_in, m_b_forget, m_g_q, m_g_k, m_g_sgu, m_b_sgu, m_w_spatial, m_b_spatial, m_w_branch_a, m_w_branch_b, m_w_out, m_g_post_mix, m_g_pre_ffn, m_w_ffn_in, m_w_ffn_down, m_g_post_ffn, v_g_pre_mix, v_w_in, v_b_forget, v_g_q, v_g_k, v_g_sgu, v_b_sgu, v_w_spatial, v_b_spatial, v_w_branch_a, v_w_branch_b, v_w_out, v_g_post_mix, v_g_pre_ffn, v_w_ffn_in, v_w_ffn_down, v_g_post_ffn):
    big_names = ("w_in", "w_branch_a", "w_branch_b", "w_out", "w_ffn_in", "w_ffn_down")
    weights = dict(g_pre_mix=g_pre_mix, w_in=w_in, b_forget=b_forget, g_q=g_q, g_k=g_k, g_sgu=g_sgu,
                   b_sgu=b_sgu, w_spatial=w_spatial, b_spatial=b_spatial, w_branch_a=w_branch_a,
                   w_branch_b=w_branch_b, w_out=w_out, g_post_mix=g_post_mix, g_pre_ffn=g_pre_ffn,
                   w_ffn_in=w_ffn_in, w_ffn_down=w_ffn_down, g_post_ffn=g_post_ffn)
    mom1 = dict(g_pre_mix=m_g_pre_mix, w_in=m_w_in, b_forget=m_b_forget, g_q=m_g_q, g_k=m_g_k,
                g_sgu=m_g_sgu, b_sgu=m_b_sgu, w_spatial=m_w_spatial, b_spatial=m_b_spatial,
                w_branch_a=m_w_branch_a, w_branch_b=m_w_branch_b, w_out=m_w_out,
                g_post_mix=m_g_post_mix, g_pre_ffn=m_g_pre_ffn, w_ffn_in=m_w_ffn_in,
                w_ffn_down=m_w_ffn_down, g_post_ffn=m_g_post_ffn)
    mom2 = dict(g_pre_mix=v_g_pre_mix, w_in=v_w_in, b_forget=v_b_forget, g_q=v_g_q, g_k=v_g_k,
                g_sgu=v_g_sgu, b_sgu=v_b_sgu, w_spatial=v_w_spatial, b_spatial=v_b_spatial,
                w_branch_a=v_w_branch_a, w_branch_b=v_w_branch_b, w_out=v_w_out,
                g_post_mix=v_g_post_mix, g_pre_ffn=v_g_pre_ffn, w_ffn_in=v_w_ffn_in,
                w_ffn_down=v_w_ffn_down, g_post_ffn=v_g_post_ffn)
    names = list(weights)
    shapes = {k: weights[k].shape for k in names}

    s_len = x.shape[1]
    xs = x.reshape(s_len, D_MODEL)
    tgt = loss_target.reshape(s_len, D_MODEL)

    transposed = ("w_in", "w_ffn_in")

    def local_view(a, k):
        return jnp.transpose(a[0]) if k in transposed else a[0]

    shards = {k: local_view(weights[k], k).astype(BF16) for k in big_names}

    x_pos, y_pos, c_pos = _mesh_pos()
    me = 4 * x_pos + 2 * y_pos + c_pos
    r_idx = jnp.arange(BLK)
    general = (jnp.asarray(BLK_AT, jnp.int32) - jnp.asarray(FRAME_START, jnp.int32))[me] + r_idx
    holder = jnp.where(r_idx < F_AT, BLK_AT[F_DEV] - FRAME_START[F_DEV] + r_idx,
                       jnp.where(r_idx < F_AT + HEADS, FRAME - F_AT + r_idx,
                                 BLK_AT[F_DEV] - FRAME_START[F_DEV] - HEADS + r_idx))
    frame_row = jnp.where(me == F_DEV, holder, general)
    in_frame = (frame_row[:, None] == jnp.arange(FRAME_ROWS)[None, :]).astype(BF16)
    my_frame = jnp.dot(in_frame.T, shards["w_in"], preferred_element_type=F32).astype(BF16)
    wcat = _gather_w_in(my_frame)
    wcat, later = lax.optimization_barrier(
        (wcat, [shards[k] for k in big_names if k != "w_in"]))
    shards.update(zip([k for k in big_names if k != "w_in"], later))
    (gat_mix, gat_ffn), gat_token = _exchange_start(
        [[shards["w_branch_a"], shards["w_branch_b"], shards["w_out"]],
         [shards["w_ffn_in"], shards["w_ffn_down"]]], "gather_start", gather=True)

    seg = np.arange(FOX_W) // HEAD_DIM
    bdiag = jnp.asarray(seg[:128, None] == seg[None, :128], BF16)
    tm = TOKEN_TILE
    lower = np.arange(tm)[None, :] <= np.arange(tm)[:, None]
    tril = jnp.asarray(lower, BF16)
    triu = jnp.asarray(lower.T, BF16)
    egrp = jnp.asarray(seg[:, None] == np.arange(128)[None, :], BF16)
    efold = jnp.asarray((np.arange(FOX_W) % HEAD_DIM)[:, None] == np.arange(128)[None, :], BF16)
    gq512 = jnp.tile(g_q.reshape(1, HEAD_DIM), (1, HEADS))
    gk512 = jnp.tile(g_k.reshape(1, HEAD_DIM), (1, HEADS))
    bfor = jnp.pad(b_forget.reshape(1, HEADS), ((0, 0), (0, 128 - HEADS)))
    pos = np.arange(WINDOW)
    wmask = (pos[None, :] // CHUNK) <= (pos[:, None] // CHUNK)
    wsm_f = jnp.where(jnp.asarray(wmask)[None], w_spatial[0], 0.0)
    wsm = wsm_f.astype(BF16)
    wsmt = jnp.transpose(wsm_f, (0, 2, 1)).astype(BF16)
    bsf = jnp.repeat(jnp.transpose(b_spatial[0]), HEAD_DIM, axis=1)
    wmask_f = jnp.asarray(wmask, F32)

    col = np.arange(SLAB_W)
    row128 = np.arange(128)

    def d_place(first, sign):
        parts = [(col[None, :] // 128 == row128[:, None]) & (col[None, :] % 128 == first + a)
                 for a in range(3)]
        return jnp.asarray(sign * np.concatenate(parts, axis=0).astype(np.float32), BF16)

    pdq, pdk = d_place(HEAD_DIM, 1.0), d_place(HEAD_DIM + 3, -1.0)
    ones_q = jnp.asarray((col % 128 >= HEAD_DIM + 3) & (col % 128 < HEAD_DIM + 6), F32)[None]
    ones_k = jnp.asarray((col % 128 >= HEAD_DIM) & (col % 128 < HEAD_DIM + 3), F32)[None]
    ecol = jnp.asarray((col[:, None] // 128 == row128[None, :])
                       & (col[:, None] % 128 == HEAD_DIM + 3), BF16)

    (h, qa, ka, kat, vs, vt, qraw, kraw, flog, uvpre, gpre) = _proj_fwd(
        xs, g_pre_mix + gat_token[0:1, 0:1], wcat, bdiag, gq512, gk512, bfor, tril, pdq, pdk,
        ones_q, ones_k)
    attn, attn_t, lse = _attn_fwd(qa, ka, vt)
    (own_a, own_b, own_out), (zone_a, zone_b, zone_out) = _exchange_wait(
        gat_mix, attn, "gather_wait_mix", gather=True)
    wa = _blocks_to_cols(_own_block(zone_a, own_a))
    wb = _blocks_to_cols(_own_block(zone_b, own_b))
    wout = _own_block(zone_out, own_out).reshape(D_MODEL, D_MODEL)
    sgu_t, ya, yb, merged_t, om, x1 = _mix_fwd(attn, uvpre, gpre, xs, wa, wb, wout, wsm, bsf,
                                           g_sgu, b_sgu, g_post_mix)
    (own_ffn, own_down), (zone_ffn, zone_down) = _exchange_wait(
        gat_ffn, x1, "gather_wait_ffn", gather=True)
    wffn = _own_block(zone_ffn, own_ffn).reshape(2 * D_FF, D_MODEL)
    wdown = _own_block(zone_down, own_down).reshape(D_FF, D_MODEL)
    (dx1, h2, act_t, dff, dgu_t, loss_acc, dg_post_ffn, dg_pre_ffn) = _ffn_fwd_bwd(
        x1, tgt, wffn, wdown, g_pre_ffn, g_post_ffn)

    dw_down = _dw_matmul(act_t, dff, D_FF // 4, "dw_down")
    dw_ffn = _dw_matmul(dgu_t, h2, 2 * D_FF // N_DEV, "dw_ffn_in")
    def own_of(parts):
        return [lax.dynamic_index_in_dim(p, me, 0, keepdims=False) for p in parts]

    parts_ffn = [dw_ffn.reshape(N_DEV, 2 * D_FF // N_DEV, D_MODEL),
                 dw_down.reshape(N_DEV, D_FF // N_DEV, D_MODEL)]
    mine_ffn = own_of(parts_ffn)
    (sct_ffn,), sct_ffn_token = _exchange_start([parts_ffn], "scatter_start_ffn", gather=False,
                                                after=mine_ffn)

    (dom, dya, dyb, dgp, dot_, delta, duv, dws, dbs, dg_sgu, db_sgu, dg_post_mix) = _mix_bwd(
        dx1, om, ya, yb, gpre, uvpre, attn, wout, wa, wb, wsm, wsmt, bsf, g_sgu, b_sgu,
        g_post_mix + sct_ffn_token[0:1, 0:1], wmask_f, egrp)
    dw_out = _dw_matmul(merged_t, dom, 512, "dw_out")
    dw_a = _dw_matmul(attn_t, dya, 512, "dw_a")
    dw_b = _dw_matmul(sgu_t, dyb, 512, "dw_b")
    parts_mix = [_cols_to_blocks(dw_a, D_MODEL // N_DEV), _cols_to_blocks(dw_b, D_MODEL // N_DEV),
                 dw_out.reshape(N_DEV, D_MODEL // N_DEV, D_MODEL)]
    mine_mix = own_of(parts_mix)
    (sct_mix,), sct_mix_token = _exchange_start([parts_mix], "scatter_start_mix", gather=False,
                                                after=mine_mix)

    gk_all, dvt, gqt, col_sums = _attn_bwd(qa, ka, kat, vs, dot_, lse,
                                           delta + sct_mix_token[0, 0], ecol)
    dlogf = _rev_cumsum(col_sums, gqt, triu)
    dx, dproj_t, dgq, dgk, dbf, dg_pre_mix = _proj_bwd(
        gqt, gk_all, dvt, dlogf, flog, qraw, kraw, duv, dgp, xs, dx1, wcat, bdiag, gq512, gk512,
        g_pre_mix, efold)

    small_local = dict(
        g_pre_mix=dg_pre_mix, b_forget=dbf[:, :HEADS], g_q=dgq[0:1, :HEAD_DIM],
        g_k=dgk[0:1, :HEAD_DIM], g_sgu=dg_sgu, b_sgu=db_sgu, w_spatial=dws,
        b_spatial=jnp.transpose(dbs[:, :GROUPS]), g_post_mix=dg_post_mix, g_pre_ffn=dg_pre_ffn,
        g_post_ffn=dg_post_ffn)
    loss_row = jnp.pad(loss_acc[0:1, 0:1], ((0, 0), (0, 1023)))
    small_parts = [_pack_vectors(small_local, loss_row).reshape(N_DEV, VEC_ROWS // N_DEV, 1024),
                   dws]

    def with_own(zones, own_blocks):
        return [_own_block(z, b) for z, b in zip(zones, own_blocks)]

    mine_small = own_of(small_parts)
    (sct_small,), sct_small_token = _exchange_start([small_parts], "scatter_start_small",
                                                    gather=False, after=mine_small)
    dw_cat = _dw_matmul(dproj_t, h, C_END // N_DEV, "dw_in", after=(sct_small_token,))
    recv_vec, recv_ws = with_own(
        _exchange_wait(sct_small, dw_cat, "scatter_wait_small", gather=False)[1], mine_small)
    small_sums = [_sum_parts(recv_vec, "sum_vectors"), _sum_parts(recv_ws, "sum_w_spatial")]
    (gat_small,), gat_small_token = _exchange_start([small_sums], "gather_start_small",
                                                    gather=True)
    pair_blocks, own_pair = _pair_sums(dw_cat, "pair_sums_in", gat_small_token)
    rs_in, rs_token = _chip_exchange_start(pair_blocks, "chip_exchange_start_in")

    recv_ffn, recv_down = with_own(
        _exchange_wait(sct_ffn, rs_token, "scatter_wait_ffn", gather=False)[1], mine_ffn)
    recv_a, recv_b, recv_out = with_own(
        _exchange_wait(sct_mix, recv_ffn, "scatter_wait_mix", gather=False)[1], mine_mix)
    received = [None, recv_a, recv_b, recv_out, recv_ffn, recv_down]

    grads, deltas, new_m, new_v = {}, {}, {}, {}
    row_tiles = {"w_in": None, "w_branch_a": 512, "w_branch_b": 512, "w_out": 128, "w_ffn_in": 176,
                 "w_ffn_down": 352}

    def update(k, parts):
        outs = _adamw(parts, local_view(weights[k], k), local_view(mom1[k], k),
                      local_view(mom2[k], k), row_tiles[k], "adamw_" + k,
                      col_tile=256 if k == "w_in" else None,
                      select=in_frame if k == "w_in" else None)
        if k in transposed:
            outs = [jnp.transpose(o) for o in outs]
        grads[k], deltas[k], new_m[k], new_v[k] = [o[None] for o in outs]
        return outs[0]

    last = None
    for idx, k in enumerate(big_names):
        if k != "w_in":
            last = update(k, received[idx])

    (own_vec, own_ws), (zone_vec, zone_ws) = _exchange_wait(gat_small, last, "gather_wait_small",
                                                            gather=True)
    vec_all = _own_block(zone_vec, own_vec).reshape(VEC_ROWS, 1024)
    ws_all = _own_block(zone_ws, own_ws).reshape(1, GROUPS * WINDOW, WINDOW)

    def rows_of(d):
        return d["w_spatial"].reshape(GROUPS * WINDOW, WINDOW)

    outs = _adamw(ws_all, rows_of(weights), rows_of(mom1), rows_of(mom2), GROUPS * WINDOW,
                  "adamw_w_spatial")
    for dst, o in zip((grads, deltas, new_m, new_v), outs):
        dst["w_spatial"] = o.reshape(shapes["w_spatial"])
    sg = outs[0]
    vec_outs = _adamw_vectors(vec_all, *[[d[k] for k in VEC_NAMES] for d in (weights, mom1, mom2)])
    for dst, group in zip((grads, deltas, new_m, new_v), vec_outs):
        dst.update(zip(VEC_NAMES, group))
    arrived = _chip_exchange_wait(rs_in, sg, "chip_exchange_wait_in")
    update("w_in", [own_pair[None], arrived])

    loss = vec_all[LOSS_ROW, 0]
    return (loss, dx.reshape(x.shape), *[grads[k] for k in names], *[deltas[k] for k in names],
            *[new_m[k] for k in names], *[new_v[k] for k in names])
```

```python
import functools
import math

import jax
import jax.numpy as jnp
import numpy as np
from jax import lax
from jax.experimental import pallas as pl
from jax.experimental.pallas import tpu as pltpu

F32 = jnp.float32
BF16 = jnp.bfloat16

D_MODEL = 1024
FOX_W = 512
HEADS = 8
HEAD_DIM = 64
SGU_W = 512
GROUPS = 8
WINDOW = 128
CHUNK = 64
D_FF = 2816
IN_COLS = 4616
EPS = 1e-6
N_DEV = 8
LOG2E = 1.4426950408889634
LN2 = 0.6931471805599453

C_Q, C_K, C_V, C_UV, C_G, C_F, C_END = 0, 512, 1024, 1536, 2560, 4608, 4736

ADAM_LR, ADAM_B1, ADAM_B2, ADAM_EPS, ADAM_WD, ADAM_STEP = 0.001, 0.9, 0.999, 1e-08, 0.01, 10

MIB = 1024 * 1024
TOKEN_TILE = 256
ATTN_TILE = 256
SLAB_W = HEADS * 128
QT_ROWS = 72

BLK = IN_COLS // N_DEV
F_LO = 3 * FOX_W
F_DEV = F_LO // BLK
F_AT = F_LO - F_DEV * BLK
BLK_AT = [BLK * j - (HEADS if BLK * j > F_LO else 0) for j in range(N_DEV)]
FRAME_START = [a // 16 * 16 for a in BLK_AT]
FRAME = 608
FRAME_ROWS = FRAME + 16


def _params(vmem_mib, n_axes):
    return pltpu.CompilerParams(
        dimension_semantics=("arbitrary",) * n_axes, vmem_limit_bytes=vmem_mib * MIB)


def _const_spec(shape):
    nd = len(shape)
    return pl.BlockSpec(shape, lambda *_: (0,) * nd)


def _row_spec(tm, cols):
    return pl.BlockSpec((tm, cols), lambda i: (i, 0))


def _tile_spec(rows, tm):
    return pl.BlockSpec((1, rows, tm), lambda i: (i, 0, 0))


def _split3_dot(x, e):
    x1 = x.astype(BF16)
    r1 = x - x1.astype(F32)
    x2 = r1.astype(BF16)
    x3 = (r1 - x2.astype(F32)).astype(BF16)
    dot = functools.partial(jnp.dot, preferred_element_type=F32)
    return dot(x1, e) + dot(x2, e) + dot(x3, e)


def _tri_dot(tri, x):
    x1 = x.astype(BF16)
    r1 = x - x1.astype(F32)
    x2 = r1.astype(BF16)
    x3 = (r1 - x2.astype(F32)).astype(BF16)
    dot = functools.partial(jnp.dot, preferred_element_type=F32)
    return dot(tri, x1) + dot(tri, x2) + dot(tri, x3)


def _seg_mean(sq, bd_ref):
    hi = sq.astype(BF16)
    bd = bd_ref[...]
    pairs = [jnp.dot(hi[:, p * 128:(p + 1) * 128], bd, preferred_element_type=F32)
             for p in range(HEADS // 2)]
    return jnp.concatenate(pairs, axis=1) * (1.0 / HEAD_DIM)


def _slabs_from_heads(t):
    lane = lax.broadcasted_iota(jnp.int32, (t.shape[0], 128), 1)
    low = lane < HEAD_DIM
    slabs = []
    for p in range(HEADS // 2):
        pair = t[:, p * 128:(p + 1) * 128]
        slabs.append(jnp.where(low, pair, 0.0))
        slabs.append(jnp.where(low, pltpu.roll(pair, HEAD_DIM, 1), 0.0))
    return jnp.concatenate(slabs, axis=1)


def _dot_nt(a, b):
    return lax.dot_general(a, b, (((1,), (1,)), ((), ())), preferred_element_type=F32)


def _dot_tn(a, b):
    return lax.dot_general(a, b, (((0,), (0,)), ((), ())), preferred_element_type=F32)


def _sigmoid(x):
    return 0.5 * jnp.tanh(0.5 * x) + 0.5


_GELU_C = math.sqrt(2.0 / math.pi)


def _gelu_and_grad(x):
    inner = _GELU_C * (x + 0.044715 * x * x * x)
    t = jnp.tanh(inner)
    y = 0.5 * x * (1.0 + t)
    dy = 0.5 * (1.0 + t) + 0.5 * x * (1.0 - t * t) * _GELU_C * (1.0 + 3.0 * 0.044715 * x * x)
    return y, dy


def _rms_bwd(xin, r, g, dy):
    dyg = dy * g
    return r * dyg - xin * (r * r * r) * jnp.mean(dyg * xin, axis=-1, keepdims=True)


def _mesh_pos():
    x, y, c = lax.axis_index("x"), lax.axis_index("y"), lax.axis_index("c")
    return x, y, c


def _peer(k):
    x, y, c = _mesh_pos()
    px = (1 - x) if (k >> 2) & 1 else x
    py = (1 - y) if (k >> 1) & 1 else y
    pc = (1 - c) if k & 1 else c
    return (px, py, pc), 4 * px + 2 * py + pc


def _frame_start(j):
    at = BLK * j - jnp.where(BLK * j > F_LO, HEADS, 0)
    return pl.multiple_of(at // 16 * 16, 16)


HALF_A = 320


def _gather_w_in(frame):
    pieces = (slice(0, HALF_A), slice(HALF_A, FRAME_ROWS))

    def body(x_ref, out_ref, zone, send_sems, recv_sems, local_sem):
        x, y, c = _mesh_pos()
        me, sibling = (x, y, c), (x, y, 1 - c)
        nbr_x, nbr_y, across = (1 - x, y, c), (x, 1 - y, c), (1 - x, 1 - y, c)

        def index(pos):
            return 4 * pos[0] + 2 * pos[1] + pos[2]

        def copy(k, block, piece, to, src=None):
            rows = pieces[piece]
            return pltpu.make_async_remote_copy(
                src_ref=(zone.at[index(block), rows] if src is None else src.at[rows]),
                dst_ref=zone.at[index(block), rows],
                send_sem=send_sems.at[k], recv_sem=recv_sems.at[k],
                device_id=to, device_id_type=pl.DeviceIdType.MESH)

        def add(block):
            j = index(block)
            rows = pl.ds(_frame_start(j), FRAME)
            out_ref[rows, :] = (out_ref[rows, :].astype(F32)
                                + zone[j, :FRAME, :].astype(F32)).astype(BF16)
            tail = slice(C_F, C_F + FRAME_ROWS - FRAME)
            forget = zone[j, FRAME:, :].astype(F32) * (j == F_DEV).astype(F32)
            out_ref[tail, :] = (out_ref[tail, :].astype(F32) + forget).astype(BF16)

        mine = pltpu.make_async_copy(x_ref, zone.at[index(me)], local_sem)
        mine.start()
        first = [copy(1, me, 0, nbr_x, src=x_ref), copy(3, me, 1, nbr_y, src=x_ref),
                 copy(2, me, 1, nbr_x, src=x_ref), copy(4, me, 0, nbr_y, src=x_ref)]
        own_to_sibling = pltpu.make_async_remote_copy(
            src_ref=x_ref, dst_ref=zone.at[index(me)], send_sem=send_sems.at[0],
            recv_sem=recv_sems.at[0], device_id=sibling, device_id_type=pl.DeviceIdType.MESH)
        for cp in first:
            cp.start()
        own_to_sibling.start()
        out_ref[...] = jnp.zeros_like(out_ref)
        mine.wait()
        add(me)

        sent = []

        def landed(k, block, piece, forward=None):
            copy(k, block, piece, me).wait_recv()
            if forward is not None:
                cp = copy(*forward)
                cp.start()
                sent.append(cp)
            cp = copy(6 + k, block, piece, sibling)
            cp.start()
            sent.append(cp)

        landed(1, nbr_x, 0, forward=(5, nbr_x, 0, nbr_y))
        landed(3, nbr_y, 1, forward=(6, nbr_y, 1, nbr_x))
        landed(2, nbr_x, 1)
        add(nbr_x)
        landed(4, nbr_y, 0)
        add(nbr_y)
        landed(5, across, 0)
        landed(6, across, 1)
        add(across)
        pltpu.make_async_remote_copy(
            src_ref=x_ref, dst_ref=zone.at[index(sibling)], send_sem=send_sems.at[0],
            recv_sem=recv_sems.at[0], device_id=sibling,
            device_id_type=pl.DeviceIdType.MESH).wait_recv()
        add(sibling)
        for k, block in ((1, nbr_x), (2, nbr_x), (3, nbr_y), (4, nbr_y), (5, across), (6, across)):
            their = (block[0], block[1], 1 - c)
            piece = {1: 0, 2: 1, 3: 1, 4: 0, 5: 0, 6: 1}[k]
            copy(6 + k, their, piece, me).wait_recv()
            if k in (2, 4, 6):
                add(their)
        for cp in first + sent:
            cp.wait_send()
        own_to_sibling.wait_send()

    return pl.pallas_call(
        body, name="gather_w_in", out_shape=jax.ShapeDtypeStruct((C_END, frame.shape[1]), BF16),
        in_specs=[pl.BlockSpec(memory_space=pl.ANY)],
        out_specs=pl.BlockSpec(memory_space=pltpu.VMEM),
        scratch_shapes=[pltpu.VMEM((N_DEV,) + frame.shape, BF16),
                        pltpu.SemaphoreType.DMA((13,)), pltpu.SemaphoreType.DMA((13,)),
                        pltpu.SemaphoreType.DMA],
        compiler_params=pltpu.CompilerParams(vmem_limit_bytes=40 * MIB),
    )(frame)


def _chip_peer(k):
    x, y, c = _mesh_pos()
    px = (1 - x) if (k >> 1) & 1 else x
    py = (1 - y) if k & 1 else y
    return (px, py, c), 2 * px + py


def _pair_sums(dw_cat, name, after):
    rows, cols = FRAME_ROWS, dw_cat.shape[1]
    n_chips = N_DEV // 2

    def pieces(p_ref, j):
        return (p_ref.at[pl.ds(_frame_start(j), FRAME)], p_ref.at[pl.ds(C_F, FRAME_ROWS - FRAME)])

    def body(p_ref, after_ref, send_ref, own_ref, mine_buf, sib_buf, send_sems, recv_sems,
             local_sems):
        x, y, c = _mesh_pos()
        sibling = (x, y, 1 - c)
        copies, local = [], []
        for q in range(n_chips):
            for part, (lo, hi) in enumerate(((0, FRAME), (FRAME, FRAME_ROWS))):
                cp = pltpu.make_async_remote_copy(
                    src_ref=pieces(p_ref, 2 * q + (1 - c))[part], dst_ref=sib_buf.at[q, lo:hi],
                    send_sem=send_sems.at[2 * q + part], recv_sem=recv_sems.at[2 * q + part],
                    device_id=sibling, device_id_type=pl.DeviceIdType.MESH)
                cp.start()
                copies.append(cp)
                lc = pltpu.make_async_copy(pieces(p_ref, 2 * q + c)[part], mine_buf.at[q, lo:hi],
                                           local_sems.at[2 * q + part])
                lc.start()
                local.append(lc)
        for lc in local:
            lc.wait()
        for cp in copies:
            cp.wait_recv()
        for k in range(1, n_chips):
            _, q = _chip_peer(k)
            send_ref[k - 1] = (mine_buf[q].astype(F32) + sib_buf[q].astype(F32)).astype(BF16)
        my_chip = 2 * x + y
        own_ref[...] = mine_buf[my_chip].astype(F32) + sib_buf[my_chip].astype(F32)
        for cp in copies:
            cp.wait_send()

    vmem = pl.BlockSpec(memory_space=pltpu.VMEM)
    return pl.pallas_call(
        body, name=name,
        out_shape=[jax.ShapeDtypeStruct((n_chips - 1, rows, cols), BF16),
                   jax.ShapeDtypeStruct((rows, cols), F32)],
        in_specs=[pl.BlockSpec(memory_space=pl.ANY)] * 2, out_specs=[vmem, vmem],
        scratch_shapes=[pltpu.VMEM((n_chips, rows, cols), BF16),
                        pltpu.VMEM((n_chips, rows, cols), BF16),
                        pltpu.SemaphoreType.DMA((2 * n_chips,)),
                        pltpu.SemaphoreType.DMA((2 * n_chips,)),
                        pltpu.SemaphoreType.DMA((2 * n_chips,))],
        compiler_params=pltpu.CompilerParams(vmem_limit_bytes=40 * MIB),
    )(dw_cat, after)


def _chip_copy(src_ref, land_ref, send_sem, recv_sem, k):
    peer, _ = _chip_peer(k)
    return pltpu.make_async_remote_copy(
        src_ref=src_ref.at[k - 1], dst_ref=land_ref.at[k - 1], send_sem=send_sem, recv_sem=recv_sem,
        device_id=peer, device_id_type=pl.DeviceIdType.MESH)


def _chip_exchange_start(blocks, name):
    hbm = pl.BlockSpec(memory_space=pltpu.HBM)
    sem = pl.BlockSpec(memory_space=pltpu.SEMAPHORE)
    n_peers = blocks.shape[0]

    def body(src_ref, zone_ref, send_sems, recv_sems, src_thru, zone_thru, token):
        for k in range(1, n_peers + 1):
            _chip_copy(src_ref, zone_ref, send_sems.at[k - 1], recv_sems.at[k - 1], k).start()
        token[...] = jnp.zeros_like(token)

    outs = pl.pallas_call(
        body, name=name, in_specs=[hbm, hbm],
        out_shape=[pltpu.SemaphoreType.DMA((n_peers,)), pltpu.SemaphoreType.DMA((n_peers,)),
                   pltpu.HBM(blocks.shape, blocks.dtype), pltpu.HBM(blocks.shape, blocks.dtype),
                   jax.ShapeDtypeStruct((8, 128), F32)],
        out_specs=[sem, sem, hbm, hbm, pl.BlockSpec(memory_space=pltpu.VMEM)],
        input_output_aliases={0: 2, 1: 3},
        compiler_params=pltpu.CompilerParams(
            has_side_effects=pltpu.SideEffectType.DATAFLOW_SIDE_EFFECTING),
    )(pltpu.with_memory_space_constraint(blocks, pltpu.HBM),
      pltpu.with_memory_space_constraint(lax.empty(blocks.shape, blocks.dtype), pltpu.HBM))
    return outs[:4], outs[4]


def _chip_exchange_wait(handle, after, name):
    send_sems, recv_sems, src, zone = handle
    hbm = pl.BlockSpec(memory_space=pltpu.HBM)
    sem = pl.BlockSpec(memory_space=pltpu.SEMAPHORE)

    def body(src_ref, zone_ref, ssem, rsem, after_ref, src_out, zone_out):
        for k in range(1, src.shape[0] + 1):
            cp = _chip_copy(src_ref, zone_ref, ssem.at[k - 1], rsem.at[k - 1], k)
            cp.wait_send()
            cp.wait_recv()

    outs = pl.pallas_call(
        body, name=name,
        in_specs=[hbm, hbm, sem, sem, pl.BlockSpec(memory_space=pl.ANY)],
        out_shape=[pltpu.HBM(src.shape, src.dtype), pltpu.HBM(zone.shape, zone.dtype)],
        out_specs=[hbm, hbm], input_output_aliases={0: 0, 1: 1},
        compiler_params=pltpu.CompilerParams(
            has_side_effects=pltpu.SideEffectType.DATAFLOW_SIDE_EFFECTING),
    )(src, zone, send_sems, recv_sems, after)
    return outs[1]


def _remote_copy(gather, src_ref, land_ref, send_sem, recv_sem, k, receive_side):
    x, y, c = _mesh_pos()
    me = 4 * x + 2 * y + c
    peer, pidx = _peer(k)
    return pltpu.make_async_remote_copy(
        src_ref=src_ref if gather else src_ref.at[pidx],
        dst_ref=land_ref.at[pidx if receive_side else me],
        send_sem=send_sem, recv_sem=recv_sem,
        device_id=peer, device_id_type=pl.DeviceIdType.MESH)


def _exchange_start(groups, name, gather):
    arrs = [a for g in groups for a in g]
    n, n_groups = len(arrs), len(groups)
    lands = [jax.ShapeDtypeStruct(((N_DEV,) + a.shape) if gather else a.shape, a.dtype)
             for a in arrs]

    def body(*refs):
        srcs, zones = refs[:n], refs[n:2 * n]
        sems = refs[2 * n:2 * n + 2 * n_groups]
        token = refs[-1]
        a = 0
        for gi, g in enumerate(groups):
            send_sems, recv_sems = sems[2 * gi], sems[2 * gi + 1]
            for k in range(1, N_DEV):
                for ai in range(len(g)):
                    slot = ai * (N_DEV - 1) + k - 1
                    _remote_copy(gather, srcs[a + ai], zones[a + ai], send_sems.at[slot],
                                 recv_sems.at[slot], k, False).start()
            a += len(g)
        token[...] = jnp.zeros_like(token)

    hbm = pl.BlockSpec(memory_space=pltpu.HBM)
    sem = pl.BlockSpec(memory_space=pltpu.SEMAPHORE)
    sem_shapes = []
    for g in groups:
        sem_shapes += [pltpu.SemaphoreType.DMA((len(g) * (N_DEV - 1),))] * 2
    outs = pl.pallas_call(
        body, name=name,
        in_specs=[hbm] * (2 * n),
        out_shape=sem_shapes + [pltpu.HBM(a.shape, a.dtype) for a in arrs]
        + [pltpu.HBM(z.shape, z.dtype) for z in lands] + [jax.ShapeDtypeStruct((8, 128), F32)],
        out_specs=[sem] * (2 * n_groups) + [hbm] * (2 * n)
        + [pl.BlockSpec(memory_space=pltpu.VMEM)],
        input_output_aliases={i: 2 * n_groups + i for i in range(2 * n)},
        compiler_params=pltpu.CompilerParams(
            has_side_effects=pltpu.SideEffectType.DATAFLOW_SIDE_EFFECTING),
    )(*[pltpu.with_memory_space_constraint(a, pltpu.HBM) for a in arrs],
      *[pltpu.with_memory_space_constraint(lax.empty(z.shape, z.dtype), pltpu.HBM) for z in lands])
    sems = outs[:2 * n_groups]
    thru = outs[2 * n_groups:2 * n_groups + n]
    zones = outs[2 * n_groups + n:2 * n_groups + 2 * n]
    handles, a = [], 0
    for gi, g in enumerate(groups):
        handles.append((sems[2 * gi], sems[2 * gi + 1], thru[a:a + len(g)], zones[a:a + len(g)]))
        a += len(g)
    return handles, outs[-1]


def _exchange_wait(handle, after, name, gather):
    send_sems, recv_sems, thru, zones = handle
    n = len(thru)

    def body(*refs):
        srcs, lands = refs[:n], refs[n:2 * n]
        ssem, rsem = refs[2 * n], refs[2 * n + 1]
        for k in range(1, N_DEV):
            for ai in range(n):
                slot = ai * (N_DEV - 1) + k - 1
                cp = _remote_copy(gather, srcs[ai], lands[ai], ssem.at[slot], rsem.at[slot], k, True)
                cp.wait_send()
                cp.wait_recv()

    hbm = pl.BlockSpec(memory_space=pltpu.HBM)
    sem = pl.BlockSpec(memory_space=pltpu.SEMAPHORE)
    outs = pl.pallas_call(
        body, name=name,
        in_specs=[hbm] * (2 * n) + [sem, sem, pl.BlockSpec(memory_space=pl.ANY)],
        out_shape=[pltpu.HBM(a.shape, a.dtype) for a in thru]
        + [pltpu.HBM(z.shape, z.dtype) for z in zones],
        out_specs=[hbm] * (2 * n),
        input_output_aliases={i: i for i in range(2 * n)},
        compiler_params=pltpu.CompilerParams(
            has_side_effects=pltpu.SideEffectType.DATAFLOW_SIDE_EFFECTING),
    )(*thru, *zones, send_sems, recv_sems, after)
    return outs[:n], outs[n:]


def _own_block(zone, block):
    x, y, c = _mesh_pos()
    me = 4 * x + 2 * y + c
    return lax.dynamic_update_slice_in_dim(zone, block[None], me, axis=0)


def _proj_fwd(x, g1, wcat, bdiag, gq, gk, bfor, tri, pdq, pdk, ones_q, ones_k):
    s_len = x.shape[0]
    tm = TOKEN_TILE
    nt = s_len // tm

    def body(x_ref, g1_ref, w_ref, bd_ref, gq_ref, gk_ref, bf_ref, tri_ref, pdq_ref,
             pdk_ref, oq_ref, ok_ref,
             h_ref, qa_ref, ka_ref, kat_ref, vs_ref, vt_ref, qr_ref, kr_ref, flog_ref, uv_ref,
             gp_ref, carry):
        @pl.when(pl.program_id(0) == 0)
        def _():
            carry[...] = jnp.zeros_like(carry)

        xf = x_ref[...]
        r = lax.rsqrt(jnp.mean(xf * xf, axis=-1, keepdims=True) + EPS)
        h = (xf * r * g1_ref[...]).astype(BF16)
        h_ref[...] = h
        dot = functools.partial(jnp.dot, preferred_element_type=F32)

        def proj(lo, hi):
            return _dot_nt(h, w_ref[lo:hi, :])

        flog = proj(C_F, C_END) + bf_ref[...]
        flog_ref[...] = flog
        lane = lax.broadcasted_iota(jnp.int32, flog.shape, 1)
        logf = jnp.minimum(flog, 0.0) - jnp.log(1.0 + jnp.exp(-jnp.abs(flog)))
        logf = jnp.where(lane < HEADS, logf, 0.0)
        dcum = _tri_dot(tri_ref[...], logf) + carry[...]
        carry[...] = dcum[tm - 1:tm, :]
        d2 = dcum * LOG2E
        d2a = d2.astype(BF16)
        rem = d2 - d2a.astype(F32)
        d2b = rem.astype(BF16)
        d2c = (rem - d2b.astype(F32)).astype(BF16)

        q = proj(C_Q, C_K)
        qr_ref[...] = q.astype(BF16)
        rq = lax.rsqrt(_seg_mean(q * q, bd_ref) + EPS)
        qn = q * rq * (gq_ref[...] * (HEAD_DIM ** -0.5 * LOG2E))
        d_parts = jnp.concatenate([d2a, d2b, d2c], axis=1)
        qa = _slabs_from_heads(qn) + dot(d_parts, pdq_ref[...]) + oq_ref[...]
        qa_ref[...] = qa.astype(BF16)

        k = proj(C_K, C_V)
        kr_ref[...] = k.astype(BF16)
        rk = lax.rsqrt(_seg_mean(k * k, bd_ref) + EPS)
        kn = k * rk * gk_ref[...]
        ka = _slabs_from_heads(kn) + dot(d_parts, pdk_ref[...]) + ok_ref[...]
        ka_ref[...] = ka.astype(BF16)
        kat_ref[0] = ka.T.astype(BF16)

        v = proj(C_V, C_UV)
        vs_ref[...] = _slabs_from_heads(v).astype(BF16)
        vt_ref[0] = v.T.astype(BF16)
        uv_ref[...] = proj(C_UV, C_G).astype(BF16)
        gp_ref[...] = proj(C_G, C_F).astype(BF16)

    outs = [((s_len, D_MODEL), BF16, _row_spec(tm, D_MODEL)),
            ((s_len, SLAB_W), BF16, _row_spec(tm, SLAB_W)),
            ((s_len, SLAB_W), BF16, _row_spec(tm, SLAB_W)),
            ((nt, SLAB_W, tm), BF16, _tile_spec(SLAB_W, tm)),
            ((s_len, SLAB_W), BF16, _row_spec(tm, SLAB_W)),
            ((nt, FOX_W, tm), BF16, _tile_spec(FOX_W, tm)),
            ((s_len, FOX_W), BF16, _row_spec(tm, FOX_W)),
            ((s_len, FOX_W), BF16, _row_spec(tm, FOX_W)),
            ((s_len, 128), F32, _row_spec(tm, 128)),
            ((s_len, 2 * SGU_W), BF16, _row_spec(tm, 2 * SGU_W)),
            ((s_len, 2 * D_MODEL), BF16, _row_spec(tm, 2 * D_MODEL))]
    return pl.pallas_call(
        body, name="proj_fwd", grid=(nt,),
        in_specs=[_row_spec(tm, D_MODEL), _const_spec((1, D_MODEL)), _const_spec(wcat.shape),
                  _const_spec(bdiag.shape), _const_spec((1, FOX_W)), _const_spec((1, FOX_W)),
                  _const_spec((1, 128)), _const_spec((tm, tm)), _const_spec(pdq.shape), _const_spec(pdk.shape), _const_spec(ones_q.shape),
                  _const_spec(ones_k.shape)],
        out_specs=[o[2] for o in outs],
        out_shape=[jax.ShapeDtypeStruct(o[0], o[1]) for o in outs],
        scratch_shapes=[pltpu.VMEM((1, 128), F32)],
        compiler_params=_params(56, 1),
    )(x, g1, wcat, bdiag, gq, gk, bfor, tri, pdq, pdk, ones_q, ones_k)


def _attn_fwd(qa, ka, vt):
    s_len = qa.shape[0]
    t = ATTN_TILE
    nb = s_len // t

    def body(q_ref, k_ref, vt_ref, o_ref, ot_ref, lse_ref, m_sc, l_sc, acc_sc, s_sc, mcur_sc,
             alpha_sc):
        i = pl.program_id(0)
        m_sc[...] = jnp.full_like(m_sc, -jnp.inf)
        l_sc[...] = jnp.zeros_like(l_sc)
        acc_sc[...] = jnp.zeros_like(acc_sc)

        def logits(j, slot, masked):
            krows = pl.ds(pl.multiple_of(j * t, t), t)
            if masked:
                keep = (lax.broadcasted_iota(jnp.int32, (t, t), 0)
                        <= lax.broadcasted_iota(jnp.int32, (t, t), 1))
            for hd in range(HEADS):
                sl = slice(hd * 128, (hd + 1) * 128)
                st = _dot_nt(k_ref[krows, sl], q_ref[:, sl])
                if masked:
                    st = jnp.where(keep, st, -jnp.inf)
                s_sc[slot, hd] = st
                m_prev = m_sc[hd:hd + 1, :]
                m_new = jnp.maximum(m_prev, jnp.max(st, axis=0, keepdims=True))
                alpha_sc[slot, hd:hd + 1, :] = jnp.exp2(m_prev - m_new)
                mcur_sc[slot, hd:hd + 1, :] = m_new
                m_sc[hd:hd + 1, :] = m_new

        def accumulate(j, slot):
            for hd in range(HEADS):
                hr = slice(hd * HEAD_DIM, (hd + 1) * HEAD_DIM)
                alpha = alpha_sc[slot, hd:hd + 1, :]
                pt = jnp.exp2(s_sc[slot, hd] - mcur_sc[slot, hd:hd + 1, :])
                l_sc[hd:hd + 1, :] = alpha * l_sc[hd:hd + 1, :] + jnp.sum(pt, axis=0, keepdims=True)
                acc_sc[hr, :] = alpha * acc_sc[hr, :] + jnp.dot(
                    vt_ref[j, hr, :], pt.astype(BF16), preferred_element_type=F32)

        @pl.when(i == 0)
        def _():
            logits(0, 0, True)
            accumulate(0, 0)

        pairs = (i - 1) // 2

        @pl.when(i > 0)
        def _():
            logits(0, 0, False)

            def two_blocks(p, carry):
                logits(2 * p + 1, 1, False)
                accumulate(2 * p, 0)
                logits(2 * p + 2, 0, False)
                accumulate(2 * p + 1, 1)
                return carry

            lax.fori_loop(0, pairs, two_blocks, 0)

        @pl.when((i > 0) & (i - 2 * pairs == 1))
        def _():
            logits(i, 1, True)
            accumulate(i - 1, 0)
            accumulate(i, 1)

        @pl.when((i > 0) & (i - 2 * pairs == 2))
        def _():
            logits(i - 1, 1, False)
            accumulate(i - 2, 0)
            logits(i, 0, True)
            accumulate(i - 1, 1)
            accumulate(i, 0)

        for hd in range(HEADS):
            hr = slice(hd * HEAD_DIM, (hd + 1) * HEAD_DIM)
            l = l_sc[hd:hd + 1, :]
            acc_sc[hr, :] = acc_sc[hr, :] / l
            lse_ref[0, hd:hd + 1, :] = m_sc[hd:hd + 1, :] + jnp.log2(l)
        o_ref[...] = acc_sc[...].T.astype(BF16)
        ot_ref[...] = acc_sc[...].astype(BF16)

    return pl.pallas_call(
        body, name="attn_fwd", grid=(nb,),
        in_specs=[_row_spec(t, SLAB_W), _const_spec(ka.shape), _const_spec(vt.shape)],
        out_specs=[_row_spec(t, FOX_W), pl.BlockSpec((FOX_W, t), lambda i: (0, i)),
                   _tile_spec(HEADS, t)],
        out_shape=[jax.ShapeDtypeStruct((s_len, FOX_W), BF16),
                   jax.ShapeDtypeStruct((FOX_W, s_len), BF16),
                   jax.ShapeDtypeStruct((nb, HEADS, t), F32)],
        scratch_shapes=[pltpu.VMEM((HEADS, t), F32), pltpu.VMEM((HEADS, t), F32),
                        pltpu.VMEM((FOX_W, t), F32), pltpu.VMEM((2, HEADS, t, t), F32),
                        pltpu.VMEM((2, HEADS, t), F32), pltpu.VMEM((2, HEADS, t), F32)],
        compiler_params=_params(48, 1),
    )(qa, ka, vt)


def _sgu_mix(vn, ws_ref):
    tm = vn.shape[0]
    lane = lax.broadcasted_iota(jnp.int32, (WINDOW, 128), 1)
    low = lane < HEAD_DIM
    wins = []
    for w in range(tm // WINDOW):
        slabs = []
        for p in range(GROUPS // 2):
            v2 = vn[w * WINDOW:(w + 1) * WINDOW, p * 128:(p + 1) * 128]
            lo = jnp.where(low, v2, 0.0).astype(BF16)
            hi = jnp.where(low, 0.0, v2).astype(BF16)
            slabs.append(jnp.dot(ws_ref[2 * p], lo, preferred_element_type=F32)
                         + jnp.dot(ws_ref[2 * p + 1], hi, preferred_element_type=F32))
        wins.append(jnp.concatenate(slabs, axis=1))
    return jnp.concatenate(wins, axis=0) if len(wins) > 1 else wins[0]


def _layernorm_fwd(vv, g, b):
    mu = jnp.mean(vv, axis=-1, keepdims=True)
    xc = vv - mu
    r = lax.rsqrt(jnp.mean(xc * xc, axis=-1, keepdims=True) + EPS)
    xh = xc * r
    return xh * g + b, xh, r


def _mix_fwd(attn, uvpre, gpre, x, wa, wb, wout, wsm, bsf, gsgu, bsgu, gpost):
    s_len = x.shape[0]
    tm = TOKEN_TILE

    def body(o_ref, uv_ref, gp_ref, x_ref, wa_ref, wb_ref, wo_ref, ws_ref, bs_ref, gs_ref, bsg_ref,
             gpost_ref, sgut_ref, ya_ref, yb_ref, mgt_ref, om_ref, x1_ref):
        uvp = uv_ref[...].astype(F32)
        uv, _ = _gelu_and_grad(uvp)
        u, vv = uv[:, :SGU_W], uv[:, SGU_W:]
        vn, _, _ = _layernorm_fwd(vv, gs_ref[...], bsg_ref[...])
        bias = bs_ref[...]
        if tm > WINDOW:
            bias = jnp.concatenate([bias] * (tm // WINDOW), axis=0)
        mixed = _sgu_mix(vn, ws_ref) + bias
        sgu_f = u * mixed
        sgu = sgu_f.astype(BF16)
        sgut_ref[...] = sgu_f.T.astype(BF16)
        ya = jnp.dot(o_ref[...], wa_ref[...], preferred_element_type=F32)
        yb = jnp.dot(sgu, wb_ref[...], preferred_element_type=F32)
        ya_ref[...] = ya.astype(BF16)
        yb_ref[...] = yb.astype(BF16)
        gates = _sigmoid(gp_ref[...].astype(F32))
        merged_f = gates[:, :D_MODEL] * ya + gates[:, D_MODEL:] * yb
        merged = merged_f.astype(BF16)
        mgt_ref[...] = merged_f.T.astype(BF16)
        om = jnp.dot(merged, wo_ref[...], preferred_element_type=F32)
        om_ref[...] = om
        r = lax.rsqrt(jnp.mean(om * om, axis=-1, keepdims=True) + EPS)
        x1_ref[...] = x_ref[...] + om * r * gpost_ref[...]

    def t_out(rows):
        return ((rows, s_len), BF16, pl.BlockSpec((rows, tm), lambda i: (0, i)))

    def r_out(cols, dt):
        return ((s_len, cols), dt, _row_spec(tm, cols))

    outs = [t_out(SGU_W), r_out(D_MODEL, BF16), r_out(D_MODEL, BF16), t_out(D_MODEL),
            r_out(D_MODEL, F32), r_out(D_MODEL, F32)]
    return pl.pallas_call(
        body, name="mix_fwd", grid=(s_len // tm,),
        in_specs=[_row_spec(tm, FOX_W), _row_spec(tm, 2 * SGU_W), _row_spec(tm, 2 * D_MODEL),
                  _row_spec(tm, D_MODEL), _const_spec(wa.shape), _const_spec(wb.shape),
                  _const_spec(wout.shape), _const_spec(wsm.shape), _const_spec(bsf.shape),
                  _const_spec((1, SGU_W)), _const_spec((1, SGU_W)), _const_spec((1, D_MODEL))],
        out_specs=[o[2] for o in outs],
        out_shape=[jax.ShapeDtypeStruct(o[0], o[1]) for o in outs],
        compiler_params=_params(48, 1),
    )(attn, uvpre, gpre, x, wa, wb, wout, wsm, bsf, gsgu, bsgu, gpost)


def _ffn_fwd_bwd(x1, tgt, wffn, wdown, gpre, gpost):
    s_len = x1.shape[0]
    tm = TOKEN_TILE

    def body(x1_ref, t_ref, wi_ref, wd_ref, gpre_ref, gpost_ref,
             dx1_ref, h2_ref, actt_ref, dff_ref, dgut_ref, loss_ref, dgpost_ref, dgpre_ref):
        @pl.when(pl.program_id(0) == 0)
        def _():
            loss_ref[...] = jnp.zeros_like(loss_ref)
            dgpost_ref[...] = jnp.zeros_like(dgpost_ref)
            dgpre_ref[...] = jnp.zeros_like(dgpre_ref)

        x1v = x1_ref[...]
        r2 = lax.rsqrt(jnp.mean(x1v * x1v, axis=-1, keepdims=True) + EPS)
        gpre_v = gpre_ref[...]
        h2 = (x1v * r2 * gpre_v).astype(BF16)
        h2_ref[...] = h2
        gg = _dot_nt(h2, wi_ref[:D_FF, :])
        uu = _dot_nt(h2, wi_ref[D_FF:, :])
        sg = _sigmoid(gg)
        silu = gg * sg
        act_f = silu * uu
        act = act_f.astype(BF16)
        actt_ref[...] = act_f.T.astype(BF16)
        ff = jnp.dot(act, wd_ref[...], preferred_element_type=F32)
        r3 = lax.rsqrt(jnp.mean(ff * ff, axis=-1, keepdims=True) + EPS)
        gpost_v = gpost_ref[...]
        y = x1v + ff * r3 * gpost_v
        err = y - t_ref[...]
        loss_ref[...] += jnp.sum(err * err) * (0.5 / D_MODEL)
        dy = err * (1.0 / D_MODEL)
        dgpost_ref[...] += jnp.sum(dy * ff * r3, axis=0, keepdims=True)
        dff = _rms_bwd(ff, r3, gpost_v, dy).astype(BF16)
        dff_ref[...] = dff
        dact = _dot_nt(dff, wd_ref[...])
        dgg_f = dact * uu * (sg * (1.0 + gg * (1.0 - sg)))
        duu_f = dact * silu
        dgg = dgg_f.astype(BF16)
        duu = duu_f.astype(BF16)
        dgut_ref[:D_FF, :] = dgg_f.T.astype(BF16)
        dgut_ref[D_FF:, :] = duu_f.T.astype(BF16)
        dh2 = (jnp.dot(dgg, wi_ref[:D_FF, :], preferred_element_type=F32)
               + jnp.dot(duu, wi_ref[D_FF:, :], preferred_element_type=F32))
        dgpre_ref[...] += jnp.sum(dh2 * x1v * r2, axis=0, keepdims=True)
        dx1_ref[...] = dy + _rms_bwd(x1v, r2, gpre_v, dh2)

    outs = [((s_len, D_MODEL), F32, _row_spec(tm, D_MODEL)),
            ((s_len, D_MODEL), BF16, _row_spec(tm, D_MODEL)),
            ((D_FF, s_len), BF16, pl.BlockSpec((D_FF, tm), lambda i: (0, i))),
            ((s_len, D_MODEL), BF16, _row_spec(tm, D_MODEL)),
            ((2 * D_FF, s_len), BF16, pl.BlockSpec((2 * D_FF, tm), lambda i: (0, i))),
            ((1, 128), F32, _const_spec((1, 128))),
            ((1, D_MODEL), F32, _const_spec((1, D_MODEL))),
            ((1, D_MODEL), F32, _const_spec((1, D_MODEL)))]
    return pl.pallas_call(
        body, name="ffn_fwd_bwd", grid=(s_len // tm,),
        in_specs=[_row_spec(tm, D_MODEL), _row_spec(tm, D_MODEL), _const_spec(wffn.shape),
                  _const_spec(wdown.shape), _const_spec((1, D_MODEL)), _const_spec((1, D_MODEL))],
        out_specs=[o[2] for o in outs],
        out_shape=[jax.ShapeDtypeStruct(o[0], o[1]) for o in outs],
        compiler_params=_params(60, 1),
    )(x1, tgt, wffn, wdown, gpre, gpost)


def _mix_bwd(dx1, om, ya, yb, gpre, uvpre, attn, wout, wa, wb, wsm, wsmt, bsf, gsgu, bsgu, gpost,
             wmask, egrp):
    s_len = dx1.shape[0]
    tm = TOKEN_TILE
    nw = tm // WINDOW
    nt = s_len // tm

    def body(dx1_ref, om_ref, ya_ref, yb_ref, gp_ref, uv_ref, o_ref, wo_ref, wa_ref, wb_ref, ws_ref,
             wst_ref, bs_ref, gs_ref, bsg_ref, gpost_ref, mask_ref, eg_ref,
             dom_ref, dya_ref, dyb_ref, dgp_ref, dot_ref, delta_ref, duv_ref,
             dws_ref, dbs_ref, dgs_ref, dbsg_ref, dgpost_ref, dbs_acc):
        step = pl.program_id(0)

        @pl.when(step == 0)
        def _():
            dws_ref[...] = jnp.zeros_like(dws_ref)
            dbs_acc[...] = jnp.zeros_like(dbs_acc)
            dgs_ref[...] = jnp.zeros_like(dgs_ref)
            dbsg_ref[...] = jnp.zeros_like(dbsg_ref)
            dgpost_ref[...] = jnp.zeros_like(dgpost_ref)

        om = om_ref[...]
        dx1v = dx1_ref[...]
        r = lax.rsqrt(jnp.mean(om * om, axis=-1, keepdims=True) + EPS)
        gpost_v = gpost_ref[...]
        dgpost_ref[...] += jnp.sum(dx1v * om * r, axis=0, keepdims=True)
        dom = _rms_bwd(om, r, gpost_v, dx1v).astype(BF16)
        dom_ref[...] = dom
        dmg = _dot_nt(dom, wo_ref[...])

        gates = _sigmoid(gp_ref[...].astype(F32))
        ga, gb = gates[:, :D_MODEL], gates[:, D_MODEL:]
        yav, ybv = ya_ref[...].astype(F32), yb_ref[...].astype(F32)
        dya = (dmg * ga).astype(BF16)
        dyb = (dmg * gb).astype(BF16)
        dya_ref[...] = dya
        dyb_ref[...] = dyb
        dgp_ref[:, :D_MODEL] = (dmg * yav * ga * (1.0 - ga)).astype(BF16)
        dgp_ref[:, D_MODEL:] = (dmg * ybv * gb * (1.0 - gb)).astype(BF16)

        dat_t = _dot_nt(dya, wa_ref[...]).T.astype(BF16)
        dot_ref[0] = dat_t
        o_t = o_ref[...].astype(F32).T
        delta_ref[0] = jnp.sum((dat_t.astype(F32) * o_t).reshape(HEADS, HEAD_DIM, tm), axis=1)
        dsgu = _dot_nt(dyb, wb_ref[...])

        uvp = uv_ref[...].astype(F32)
        uv, guv = _gelu_and_grad(uvp)
        u, vv = uv[:, :SGU_W], uv[:, SGU_W:]
        gs_v = gs_ref[...]
        vn, xh, rln = _layernorm_fwd(vv, gs_v, bsg_ref[...])
        bias = bs_ref[...]
        if nw > 1:
            bias = jnp.concatenate([bias] * nw, axis=0)
        mixed = _sgu_mix(vn, ws_ref) + bias
        du = dsgu * mixed
        dmixed = dsgu * u

        lane = lax.broadcasted_iota(jnp.int32, (WINDOW, 128), 1)
        low = lane < HEAD_DIM
        dvn_wins = []
        for w in range(nw):
            rows = slice(w * WINDOW, (w + 1) * WINDOW)
            dbs_acc[...] += dmixed[rows, :]
            slabs = []
            for p in range(GROUPS // 2):
                cols = slice(p * 128, (p + 1) * 128)
                dm2 = dmixed[rows, cols]
                dlo = jnp.where(low, dm2, 0.0).astype(BF16)
                dhi = jnp.where(low, 0.0, dm2).astype(BF16)
                vn2 = vn[rows, cols].astype(BF16)
                dws_ref[2 * p] += _dot_nt(dlo, vn2)
                dws_ref[2 * p + 1] += _dot_nt(dhi, vn2)
                slabs.append(jnp.dot(wst_ref[2 * p], dlo, preferred_element_type=F32)
                             + jnp.dot(wst_ref[2 * p + 1], dhi, preferred_element_type=F32))
            dvn_wins.append(jnp.concatenate(slabs, axis=1))
        dvn = jnp.concatenate(dvn_wins, axis=0) if nw > 1 else dvn_wins[0]

        dgs_ref[...] += jnp.sum(dvn * xh, axis=0, keepdims=True)
        dbsg_ref[...] += jnp.sum(dvn, axis=0, keepdims=True)
        dxh = dvn * gs_v
        dvv = rln * (dxh - jnp.mean(dxh, axis=-1, keepdims=True)
                     - xh * jnp.mean(dxh * xh, axis=-1, keepdims=True))
        duv_ref[:, :SGU_W] = (du * guv[:, :SGU_W]).astype(BF16)
        duv_ref[:, SGU_W:] = (dvv * guv[:, SGU_W:]).astype(BF16)

        @pl.when(step == pl.num_programs(0) - 1)
        def _():
            for g in range(GROUPS):
                dws_ref[g] = dws_ref[g] * mask_ref[...]
            dbs_ref[...] = _split3_dot(dbs_acc[...], eg_ref[...])

    rows_out = [((s_len, D_MODEL), BF16, _row_spec(tm, D_MODEL)),
                ((s_len, D_MODEL), BF16, _row_spec(tm, D_MODEL)),
                ((s_len, D_MODEL), BF16, _row_spec(tm, D_MODEL)),
                ((s_len, 2 * D_MODEL), BF16, _row_spec(tm, 2 * D_MODEL)),
                ((nt, FOX_W, tm), BF16, _tile_spec(FOX_W, tm)),
                ((nt, HEADS, tm), F32, _tile_spec(HEADS, tm)),
                ((s_len, 2 * SGU_W), BF16, _row_spec(tm, 2 * SGU_W))]
    acc_out = [((GROUPS, WINDOW, WINDOW), F32), ((WINDOW, 128), F32), ((1, SGU_W), F32),
               ((1, SGU_W), F32), ((1, D_MODEL), F32)]
    return pl.pallas_call(
        body, name="mix_bwd", grid=(nt,),
        in_specs=[_row_spec(tm, D_MODEL), _row_spec(tm, D_MODEL), _row_spec(tm, D_MODEL),
                  _row_spec(tm, D_MODEL), _row_spec(tm, 2 * D_MODEL), _row_spec(tm, 2 * SGU_W),
                  _row_spec(tm, FOX_W), _const_spec(wout.shape), _const_spec(wa.shape),
                  _const_spec(wb.shape), _const_spec(wsm.shape), _const_spec(wsmt.shape),
                  _const_spec(bsf.shape), _const_spec((1, SGU_W)), _const_spec((1, SGU_W)),
                  _const_spec((1, D_MODEL)), _const_spec(wmask.shape), _const_spec(egrp.shape)],
        out_specs=[o[2] for o in rows_out] + [_const_spec(s) for s, _ in acc_out],
        out_shape=[jax.ShapeDtypeStruct(o[0], o[1]) for o in rows_out]
        + [jax.ShapeDtypeStruct(s, dt) for s, dt in acc_out],
        scratch_shapes=[pltpu.VMEM((WINDOW, SGU_W), F32)],
        compiler_params=_params(48, 1),
    )(dx1, om, ya, yb, gpre, uvpre, attn, wout, wa, wb, wsm, wsmt, bsf, gsgu, bsgu, gpost, wmask,
      egrp)


def _attn_bwd(qa, ka, kat, vs, dot_, lse, delta, ecol):
    s_len = qa.shape[0]
    t = ATTN_TILE
    nb = s_len // t

    def body(k_ref, kt_ref, vs_ref, q_ref, do_ref, lse_ref, dl_ref, ec_ref, gk_ref, dvt_ref,
             gqt_ref, csum_ref, p_sc, ds_sc):
        j = pl.program_id(0)

        @pl.when(j == 0)
        def _():
            gqt_ref[...] = jnp.zeros_like(gqt_ref)

        gk_ref[...] = jnp.zeros_like(gk_ref)
        dvt_ref[...] = jnp.zeros_like(dvt_ref)

        def probs(i, slot, masked):
            qrows = pl.ds(pl.multiple_of(i * t, t), t)
            if masked:
                keep = (lax.broadcasted_iota(jnp.int32, (t, t), 0)
                        <= lax.broadcasted_iota(jnp.int32, (t, t), 1))
            for hd in range(HEADS):
                sl = slice(hd * 128, (hd + 1) * 128)
                hr = slice(hd * HEAD_DIM, (hd + 1) * HEAD_DIM)
                st = _dot_nt(k_ref[:, sl], q_ref[qrows, sl])
                if masked:
                    st = jnp.where(keep, st, -jnp.inf)
                pt = jnp.exp2(st - lse_ref[i, hd:hd + 1, :])
                dpt = jnp.dot(vs_ref[:, hd * 128:hd * 128 + HEAD_DIM], do_ref[i, hr, :],
                              preferred_element_type=F32)
                p_sc[slot, hd] = pt.astype(BF16)
                ds_sc[slot, hd] = (pt * (dpt - dl_ref[i, hd:hd + 1, :])).astype(BF16)

        def grads(i, slot):
            qrows = pl.ds(pl.multiple_of(i * t, t), t)
            for hd in range(HEADS):
                sl = slice(hd * 128, (hd + 1) * 128)
                hr = slice(hd * HEAD_DIM, (hd + 1) * HEAD_DIM)
                dst = ds_sc[slot, hd]
                dvt_ref[0, hr, :] += _dot_nt(do_ref[i, hr, :], p_sc[slot, hd])
                gk_ref[:, sl] += jnp.dot(dst, q_ref[qrows, sl], preferred_element_type=F32)
                gqt_ref[i, hd * QT_ROWS:(hd + 1) * QT_ROWS, :] += jnp.dot(
                    kt_ref[0, hd * 128:hd * 128 + QT_ROWS, :], dst, preferred_element_type=F32)

        probs(j, 0, True)
        pairs = (nb - 1 - j) // 2

        def two_blocks(p, carry):
            i1 = j + 1 + 2 * p
            probs(i1, 1, False)
            grads(i1 - 1, 0)
            probs(i1 + 1, 0, False)
            grads(i1, 1)
            return carry

        lax.fori_loop(0, pairs, two_blocks, 0)

        @pl.when(nb - 1 - j - 2 * pairs == 0)
        def _():
            grads(nb - 1, 0)

        @pl.when(nb - 1 - j - 2 * pairs == 1)
        def _():
            probs(nb - 1, 1, False)
            grads(nb - 2, 0)
            grads(nb - 1, 1)

        csum_ref[...] = _split3_dot(gk_ref[...], ec_ref[...])

    return pl.pallas_call(
        body, name="attn_bwd", grid=(nb,),
        in_specs=[_row_spec(t, SLAB_W), _tile_spec(SLAB_W, t), _row_spec(t, SLAB_W),
                  _const_spec(qa.shape), _const_spec(dot_.shape), _const_spec(lse.shape),
                  _const_spec(delta.shape), _const_spec(ecol.shape)],
        out_specs=[_row_spec(t, SLAB_W), _tile_spec(FOX_W, t),
                   _const_spec((nb, HEADS * QT_ROWS, t)), _row_spec(t, 128)],
        out_shape=[jax.ShapeDtypeStruct((s_len, SLAB_W), F32),
                   jax.ShapeDtypeStruct((nb, FOX_W, t), F32),
                   jax.ShapeDtypeStruct((nb, HEADS * QT_ROWS, t), F32),
                   jax.ShapeDtypeStruct((s_len, 128), F32)],
        scratch_shapes=[pltpu.VMEM((2, HEADS, t, t), BF16), pltpu.VMEM((2, HEADS, t, t), BF16)],
        compiler_params=_params(60, 1),
    )(ka, kat, vs, qa, dot_, lse, delta, ecol)


def _rev_cumsum(col_sums, gqt, triu):
    s_len = col_sums.shape[0]
    tm = TOKEN_TILE
    n = s_len // tm

    def body(cs_ref, gqt_ref, tri_ref, o_ref, carry):
        @pl.when(pl.program_id(0) == 0)
        def _():
            carry[...] = jnp.zeros_like(carry)
        rows = [gqt_ref[0, hd * QT_ROWS + HEAD_DIM:hd * QT_ROWS + HEAD_DIM + 1, :]
                for hd in range(HEADS)]
        row_sums = jnp.concatenate(rows + [jnp.zeros((128 - HEADS, tm), F32)], axis=0).T
        out = _tri_dot(tri_ref[...], row_sums - cs_ref[...]) + carry[...]
        o_ref[...] = out
        carry[...] = out[0:1, :]

    return pl.pallas_call(
        body, name="rev_cumsum", grid=(n,),
        in_specs=[pl.BlockSpec((tm, 128), lambda i: (n - 1 - i, 0)),
                  pl.BlockSpec((1, HEADS * QT_ROWS, tm), lambda i: (n - 1 - i, 0, 0)),
                  _const_spec((tm, tm))],
        out_specs=pl.BlockSpec((tm, 128), lambda i: (n - 1 - i, 0)),
        out_shape=jax.ShapeDtypeStruct((s_len, 128), F32),
        scratch_shapes=[pltpu.VMEM((1, 128), F32)],
        compiler_params=_params(32, 1),
    )(col_sums, gqt, triu)


def _heads_from_slabs(slabs):
    lane = lax.broadcasted_iota(jnp.int32, slabs[0].shape, 1)
    low = lane < HEAD_DIM
    pairs = [jnp.where(low, slabs[2 * p], pltpu.roll(slabs[2 * p + 1], HEAD_DIM, 1))
             for p in range(HEADS // 2)]
    return jnp.concatenate(pairs, axis=1)


def _proj_bwd(gqt, gk, dvt, dlogf, flog, qraw, kraw, duv, dgp, x, dx1, wcat, bdiag, gq, gk_gain, g1,
              efold):
    s_len = x.shape[0]
    tm = TOKEN_TILE

    def body(gqt_ref, gkk_ref, dvt_ref, dlf_ref, flog_ref, qr_ref, kr_ref, duv_ref, dgp_ref, x_ref,
             dx1_ref, w_ref, bd_ref, gq_ref, gk_ref, g1_ref, ef_ref,
             dx_ref, dprojt_ref, dgq_ref, dgk_ref, dbf_ref, dg1_ref, gq_acc, gk_acc):
        step = pl.program_id(0)

        @pl.when(step == 0)
        def _():
            gq_acc[...] = jnp.zeros_like(gq_acc)
            gk_acc[...] = jnp.zeros_like(gk_acc)
            dbf_ref[...] = jnp.zeros_like(dbf_ref)
            dg1_ref[...] = jnp.zeros_like(dg1_ref)

        pad = jnp.zeros((128 - QT_ROWS, tm), F32)
        q_slabs = [jnp.concatenate([gqt_ref[0, hd * QT_ROWS:(hd + 1) * QT_ROWS, :], pad], axis=0).T
                   for hd in range(HEADS)]
        dqn = _heads_from_slabs(q_slabs)
        dkn = _heads_from_slabs([gkk_ref[:, hd * 128:(hd + 1) * 128] for hd in range(HEADS)])

        def head_bwd(raw_ref, dn, g_ref, acc):
            raw = raw_ref[...].astype(F32)
            r = lax.rsqrt(_seg_mean(raw * raw, bd_ref) + EPS)
            xhat = raw * r
            acc[0:1, :] += jnp.sum(dn * xhat, axis=0, keepdims=True)
            dyg = dn * g_ref[...]
            return r * (dyg - xhat * _seg_mean(dyg * xhat, bd_ref))

        dot = functools.partial(jnp.dot, preferred_element_type=F32)
        duv, dgp = duv_ref[...], dgp_ref[...]
        dprojt_ref[C_UV:C_G, :] = duv.astype(F32).T.astype(BF16)
        dprojt_ref[C_G:C_F, :] = dgp.astype(F32).T.astype(BF16)
        dh = dot(duv, w_ref[C_UV:C_G, :]) + dot(dgp, w_ref[C_G:C_F, :])

        dq = head_bwd(qr_ref, dqn * HEAD_DIM ** -0.5, gq_ref, gq_acc)
        dk = head_bwd(kr_ref, dkn * LN2, gk_ref, gk_acc)
        dv_t = dvt_ref[0]
        dfl = dlf_ref[...] * _sigmoid(-flog_ref[...])
        dbf_ref[...] += jnp.sum(dfl, axis=0, keepdims=True)
        dprojt_ref[C_Q:C_K, :] = dq.T.astype(BF16)
        dprojt_ref[C_K:C_V, :] = dk.T.astype(BF16)
        dprojt_ref[C_V:C_UV, :] = dv_t.astype(BF16)
        dprojt_ref[C_F:C_END, :] = dfl.T.astype(BF16)
        dh = (dh + dot(dq.astype(BF16), w_ref[C_Q:C_K, :]) + dot(dk.astype(BF16), w_ref[C_K:C_V, :])
              + dot(dv_t.T.astype(BF16), w_ref[C_V:C_UV, :])
              + dot(dfl.astype(BF16), w_ref[C_F:C_END, :]))
        xf = x_ref[...]
        r = lax.rsqrt(jnp.mean(xf * xf, axis=-1, keepdims=True) + EPS)
        dg1_ref[...] += jnp.sum(dh * xf * r, axis=0, keepdims=True)
        dx_ref[...] = dx1_ref[...] + _rms_bwd(xf, r, g1_ref[...], dh)

        @pl.when(step == pl.num_programs(0) - 1)
        def _():
            dgq_ref[...] = _split3_dot(gq_acc[...], ef_ref[...])
            dgk_ref[...] = _split3_dot(gk_acc[...], ef_ref[...])

    outs = [((s_len, D_MODEL), F32, _row_spec(tm, D_MODEL)),
            ((C_END, s_len), BF16, pl.BlockSpec((C_END, tm), lambda i: (0, i))),
            ((8, 128), F32, _const_spec((8, 128))),
            ((8, 128), F32, _const_spec((8, 128))),
            ((1, 128), F32, _const_spec((1, 128))),
            ((1, D_MODEL), F32, _const_spec((1, D_MODEL)))]
    return pl.pallas_call(
        body, name="proj_bwd", grid=(s_len // tm,),
        in_specs=[_tile_spec(HEADS * QT_ROWS, tm), _row_spec(tm, SLAB_W), _tile_spec(FOX_W, tm),
                  _row_spec(tm, 128), _row_spec(tm, 128), _row_spec(tm, FOX_W),
                  _row_spec(tm, FOX_W), _row_spec(tm, 2 * SGU_W), _row_spec(tm, 2 * D_MODEL),
                  _row_spec(tm, D_MODEL), _row_spec(tm, D_MODEL), _const_spec(wcat.shape),
                  _const_spec(bdiag.shape), _const_spec((1, FOX_W)), _const_spec((1, FOX_W)),
                  _const_spec((1, D_MODEL)), _const_spec(efold.shape)],
        out_specs=[o[2] for o in outs],
        out_shape=[jax.ShapeDtypeStruct(o[0], o[1]) for o in outs],
        scratch_shapes=[pltpu.VMEM((8, FOX_W), F32), pltpu.VMEM((8, FOX_W), F32)],
        compiler_params=_params(56, 1),
    )(gqt, gk, dvt, dlogf, flog, qraw, kraw, duv, dgp, x, dx1, wcat, bdiag, gq, gk_gain, g1, efold)


def _dw_matmul(at, b, tm, name, after=()):
    m, s_len = at.shape
    n = b.shape[1]

    def body(a_ref, b_ref, *rest):
        rest[-1][...] = jnp.dot(a_ref[...], b_ref[...], preferred_element_type=F32).astype(BF16)

    return pl.pallas_call(
        body, name=name, grid=(m // tm,),
        in_specs=[pl.BlockSpec((tm, s_len), lambda i: (i, 0)), _const_spec(b.shape)]
        + [pl.BlockSpec(memory_space=pl.ANY)] * len(after),
        out_specs=pl.BlockSpec((tm, n), lambda i: (i, 0)),
        out_shape=jax.ShapeDtypeStruct((m, n), BF16),
        compiler_params=_params(48, 1),
    )(at, b, *after)


def _adamw(parts, w, m, v, tr, name, col_tile=None, select=None):
    parts = parts if isinstance(parts, (list, tuple)) else [parts]
    rows, cols = w.shape
    extra = [] if select is None else [select]
    bc1 = 1.0 - ADAM_B1 ** ADAM_STEP
    bc2 = 1.0 - ADAM_B2 ** ADAM_STEP

    def body(*refs):
        p_refs = refs[:len(parts)]
        sel_refs = refs[len(parts):len(parts) + len(extra)]
        w_ref, m_ref, v_ref, g_ref, d_ref, mo_ref, vo_ref = refs[len(parts) + len(extra):]
        g = None
        for p_ref, p in zip(p_refs, parts):
            for idx in range(p.shape[0]):
                term = p_ref[idx].astype(F32)
                g = term if g is None else g + term
        if sel_refs:
            g = _tri_dot(sel_refs[0][...], g)
        g_ref[...] = g
        mn = ADAM_B1 * m_ref[...] + (1.0 - ADAM_B1) * g
        vn = ADAM_B2 * v_ref[...] + (1.0 - ADAM_B2) * (g * g)
        mo_ref[...] = mn
        vo_ref[...] = vn
        m_hat = mn / bc1
        v_hat = vn / bc2
        d_ref[...] = -ADAM_LR * (m_hat / (jnp.sqrt(v_hat) + ADAM_EPS) + ADAM_WD * w_ref[...])

    if col_tile is None:
        spec = pl.BlockSpec((tr, cols), lambda i: (i, 0))
        pspecs = [pl.BlockSpec((p.shape[0], tr, cols), lambda i: (0, i, 0)) for p in parts]
        steps = rows // tr
    else:
        spec = pl.BlockSpec((rows, col_tile), lambda i: (0, i))
        pspecs = [pl.BlockSpec((p.shape[0], p.shape[1], col_tile), lambda i: (0, 0, i))
                  for p in parts]
        steps = cols // col_tile
    return pl.pallas_call(
        body, name=name, grid=(steps,),
        in_specs=pspecs + [_const_spec(e.shape) for e in extra] + [spec, spec, spec],
        out_specs=[spec] * 4,
        out_shape=[jax.ShapeDtypeStruct((rows, cols), F32)] * 4,
        compiler_params=_params(48, 1),
    )(*parts, *extra, w, m, v)


def _sum_parts(parts, name):
    n, rows, cols = parts.shape

    def body(p_ref, o_ref):
        g = p_ref[0]
        for idx in range(1, n):
            g = g + p_ref[idx]
        o_ref[...] = g

    return pl.pallas_call(
        body, name=name, out_shape=jax.ShapeDtypeStruct((rows, cols), F32),
        in_specs=[_const_spec(parts.shape)], out_specs=_const_spec((rows, cols)), grid=(1,),
        compiler_params=_params(16, 1),
    )(parts)


VEC_NAMES = ("g_pre_mix", "b_forget", "g_q", "g_k", "g_sgu", "b_sgu", "b_spatial", "g_post_mix",
             "g_pre_ffn", "g_post_ffn")
VEC_ROWS = 16
LOSS_ROW = len(VEC_NAMES)


def _pack_vectors(d, loss_row):
    rows = []
    for k in VEC_NAMES:
        flat = d[k].reshape(1, -1).astype(F32)
        rows.append(jnp.pad(flat, ((0, 0), (0, 1024 - flat.shape[1]))))
    rows.append(loss_row)
    rows.append(jnp.zeros((VEC_ROWS - len(rows), 1024), F32))
    return jnp.concatenate(rows, axis=0)


def _adamw_vectors(grad_rows, ws, ms, vs):
    n = len(VEC_NAMES)
    bc1 = 1.0 - ADAM_B1 ** ADAM_STEP
    bc2 = 1.0 - ADAM_B2 ** ADAM_STEP

    def step(g, w, m, v):
        mn = ADAM_B1 * m + (1.0 - ADAM_B1) * g
        vn = ADAM_B2 * v + (1.0 - ADAM_B2) * (g * g)
        delta = -ADAM_LR * ((mn / bc1) / (jnp.sqrt(vn / bc2) + ADAM_EPS) + ADAM_WD * w)
        return g, delta, mn, vn

    def body(*refs):
        g_ref = refs[0]
        ins = [refs[1 + j * n:1 + (j + 1) * n] for j in range(3)]
        outs = [refs[1 + (3 + j) * n:1 + (4 + j) * n] for j in range(4)]
        for i in range(n):
            shape = ws[i].shape
            if len(shape) == 2:
                res = step(g_ref[i:i + 1, :shape[1]], *[r[i][...] for r in ins])
                for o, val in zip(outs, res):
                    o[i][...] = val
            else:
                for r in range(shape[1]):
                    res = step(g_ref[i:i + 1, r * shape[2]:(r + 1) * shape[2]],
                               *[q[i][0, r:r + 1, :] for q in ins])
                    for o, val in zip(outs, res):
                        o[i][0, r:r + 1, :] = val

    vmem = pl.BlockSpec(memory_space=pltpu.VMEM)
    flat = pl.pallas_call(
        body, name="adamw_vectors",
        in_specs=[vmem] * (1 + 3 * n), out_specs=[vmem] * (4 * n),
        out_shape=[jax.ShapeDtypeStruct(w.shape, F32) for _ in range(4) for w in ws],
    )(grad_rows, *ws, *ms, *vs)
    return [flat[j * n:(j + 1) * n] for j in range(4)]


def _cols_to_blocks(full, width):
    r = full.shape[0]
    return jnp.transpose(full.reshape(r, N_DEV, width), (1, 0, 2))


def _blocks_to_cols(blocks):
    n, r, width = blocks.shape
    return jnp.transpose(blocks, (1, 0, 2)).reshape(r, n * width)


def kernel(x, g_pre_mix, w_in, b_forget, g_q, g_k, g_sgu, b_sgu, w_spatial, b_spatial, w_branch_a, w_branch_b, w_out, g_post_mix, g_pre_ffn, w_ffn_in, w_ffn_down, g_post_ffn, loss_target, m_g_pre_mix, m_w_in, m_b_forget, m_g_q, m_g_k, m_g_sgu, m_b_sgu, m_w_spatial, m_b_spatial, m_w_branch_a, m_w_branch_b, m_w_out, m_g_post_mix, m_g_pre_ffn, m_w_ffn_in, m_w_ffn_down, m_g_post_ffn, v_g_pre_mix, v_w_in, v_b_forget, v_g_q, v_g_k, v_g_sgu, v_b_sgu, v_w_spatial, v_b_spatial, v_w_branch_a, v_w_branch_b, v_w_out, v_g_post_mix, v_g_pre_ffn, v_w_ffn_in, v_w_ffn_down, v_g_post_ffn):
    big_names = ("w_in", "w_branch_a", "w_branch_b", "w_out", "w_ffn_in", "w_ffn_down")
    weights = dict(g_pre_mix=g_pre_mix, w_in=w_in, b_forget=b_forget, g_q=g_q, g_k=g_k, g_sgu=g_sgu,
                   b_sgu=b_sgu, w_spatial=w_spatial, b_spatial=b_spatial, w_branch_a=w_branch_a,
                   w_branch_b=w_branch_b, w_out=w_out, g_post_mix=g_post_mix, g_pre_ffn=g_pre_ffn,
                   w_ffn_in=w_ffn_in, w_ffn_down=w_ffn_down, g_post_ffn=g_post_ffn)
    mom1 = dict(g_pre_mix=m_g_pre_mix, w_in=m_w_in, b_forget=m_b_forget, g_q=m_g_q, g_k=m_g_k,
                g_sgu=m_g_sgu, b_sgu=m_b_sgu, w_spatial=m_w_spatial, b_spatial=m_b_spatial,
                w_branch_a=m_w_branch_a, w_branch_b=m_w_branch_b, w_out=m_w_out,
                g_post_mix=m_g_post_mix, g_pre_ffn=m_g_pre_ffn, w_ffn_in=m_w_ffn_in,
                w_ffn_down=m_w_ffn_down, g_post_ffn=m_g_post_ffn)
    mom2 = dict(g_pre_mix=v_g_pre_mix, w_in=v_w_in, b_forget=v_b_forget, g_q=v_g_q, g_k=v_g_k,
                g_sgu=v_g_sgu, b_sgu=v_b_sgu, w_spatial=v_w_spatial, b_spatial=v_b_spatial,
                w_branch_a=v_w_branch_a, w_branch_b=v_w_branch_b, w_out=v_w_out,
                g_post_mix=v_g_post_mix, g_pre_ffn=v_g_pre_ffn, w_ffn_in=v_w_ffn_in,
                w_ffn_down=v_w_ffn_down, g_post_ffn=v_g_post_ffn)
    names = list(weights)
    shapes = {k: weights[k].shape for k in names}

    s_len = x.shape[1]
    xs = x.reshape(s_len, D_MODEL)
    tgt = loss_target.reshape(s_len, D_MODEL)

    transposed = ("w_in", "w_ffn_in")

    def local_view(a, k):
        return jnp.transpose(a[0]) if k in transposed else a[0]

    shards = {k: local_view(weights[k], k).astype(BF16) for k in big_names}

    x_pos, y_pos, c_pos = _mesh_pos()
    me = 4 * x_pos + 2 * y_pos + c_pos
    r_idx = jnp.arange(BLK)
    general = (jnp.asarray(BLK_AT, jnp.int32) - jnp.asarray(FRAME_START, jnp.int32))[me] + r_idx
    holder = jnp.where(r_idx < F_AT, BLK_AT[F_DEV] - FRAME_START[F_DEV] + r_idx,
                       jnp.where(r_idx < F_AT + HEADS, FRAME - F_AT + r_idx,
                                 BLK_AT[F_DEV] - FRAME_START[F_DEV] - HEADS + r_idx))
    frame_row = jnp.where(me == F_DEV, holder, general)
    in_frame = (frame_row[:, None] == jnp.arange(FRAME_ROWS)[None, :]).astype(BF16)
    my_frame = jnp.dot(in_frame.T, shards["w_in"], preferred_element_type=F32).astype(BF16)
    wcat = _gather_w_in(my_frame)
    wcat, later = lax.optimization_barrier(
        (wcat, [shards[k] for k in big_names if k != "w_in"]))
    shards.update(zip([k for k in big_names if k != "w_in"], later))
    (gat_mix, gat_ffn), gat_token = _exchange_start(
        [[shards["w_branch_a"], shards["w_branch_b"], shards["w_out"]],
         [shards["w_ffn_in"], shards["w_ffn_down"]]], "gather_start", gather=True)

    seg = np.arange(FOX_W) // HEAD_DIM
    bdiag = jnp.asarray(seg[:128, None] == seg[None, :128], BF16)
    tm = TOKEN_TILE
    lower = np.arange(tm)[None, :] <= np.arange(tm)[:, None]
    tril = jnp.asarray(lower, BF16)
    triu = jnp.asarray(lower.T, BF16)
    egrp = jnp.asarray(seg[:, None] == np.arange(128)[None, :], BF16)
    efold = jnp.asarray((np.arange(FOX_W) % HEAD_DIM)[:, None] == np.arange(128)[None, :], BF16)
    gq512 = jnp.tile(g_q.reshape(1, HEAD_DIM), (1, HEADS))
    gk512 = jnp.tile(g_k.reshape(1, HEAD_DIM), (1, HEADS))
    bfor = jnp.pad(b_forget.reshape(1, HEADS), ((0, 0), (0, 128 - HEADS)))
    pos = np.arange(WINDOW)
    wmask = (pos[None, :] // CHUNK) <= (pos[:, None] // CHUNK)
    wsm_f = jnp.where(jnp.asarray(wmask)[None], w_spatial[0], 0.0)
    wsm = wsm_f.astype(BF16)
    wsmt = jnp.transpose(wsm_f, (0, 2, 1)).astype(BF16)
    bsf = jnp.repeat(jnp.transpose(b_spatial[0]), HEAD_DIM, axis=1)
    wmask_f = jnp.asarray(wmask, F32)

    col = np.arange(SLAB_W)
    row128 = np.arange(128)

    def d_place(first, sign):
        parts = [(col[None, :] // 128 == row128[:, None]) & (col[None, :] % 128 == first + a)
                 for a in range(3)]
        return jnp.asarray(sign * np.concatenate(parts, axis=0).astype(np.float32), BF16)

    pdq, pdk = d_place(HEAD_DIM, 1.0), d_place(HEAD_DIM + 3, -1.0)
    ones_q = jnp.asarray((col % 128 >= HEAD_DIM + 3) & (col % 128 < HEAD_DIM + 6), F32)[None]
    ones_k = jnp.asarray((col % 128 >= HEAD_DIM) & (col % 128 < HEAD_DIM + 3), F32)[None]
    ecol = jnp.asarray((col[:, None] // 128 == row128[None, :])
                       & (col[:, None] % 128 == HEAD_DIM + 3), BF16)

    (h, qa, ka, kat, vs, vt, qraw, kraw, flog, uvpre, gpre) = _proj_fwd(
        xs, g_pre_mix + gat_token[0:1, 0:1], wcat, bdiag, gq512, gk512, bfor, tril, pdq, pdk,
        ones_q, ones_k)
    attn, attn_t, lse = _attn_fwd(qa, ka, vt)
    (own_a, own_b, own_out), (zone_a, zone_b, zone_out) = _exchange_wait(
        gat_mix, attn, "gather_wait_mix", gather=True)
    wa = _blocks_to_cols(_own_block(zone_a, own_a))
    wb = _blocks_to_cols(_own_block(zone_b, own_b))
    wout = _own_block(zone_out, own_out).reshape(D_MODEL, D_MODEL)
    sgu_t, ya, yb, merged_t, om, x1 = _mix_fwd(attn, uvpre, gpre, xs, wa, wb, wout, wsm, bsf,
                                           g_sgu, b_sgu, g_post_mix)
    (own_ffn, own_down), (zone_ffn, zone_down) = _exchange_wait(
        gat_ffn, x1, "gather_wait_ffn", gather=True)
    wffn = _own_block(zone_ffn, own_ffn).reshape(2 * D_FF, D_MODEL)
    wdown = _own_block(zone_down, own_down).reshape(D_FF, D_MODEL)
    (dx1, h2, act_t, dff, dgu_t, loss_acc, dg_post_ffn, dg_pre_ffn) = _ffn_fwd_bwd(
        x1, tgt, wffn, wdown, g_pre_ffn, g_post_ffn)

    dw_down = _dw_matmul(act_t, dff, D_FF // 4, "dw_down")
    dw_ffn = _dw_matmul(dgu_t, h2, 2 * D_FF // N_DEV, "dw_ffn_in")
    def own_of(parts):
        return [lax.dynamic_index_in_dim(p, me, 0, keepdims=False) for p in parts]

    parts_ffn = [dw_ffn.reshape(N_DEV, 2 * D_FF // N_DEV, D_MODEL),
                 dw_down.reshape(N_DEV, D_FF // N_DEV, D_MODEL)]
    mine_ffn = own_of(parts_ffn)
    (sct_ffn,), sct_ffn_token = _exchange_start([parts_ffn], "scatter_start_ffn", gather=False)

    (dom, dya, dyb, dgp, dot_, delta, duv, dws, dbs, dg_sgu, db_sgu, dg_post_mix) = _mix_bwd(
        dx1, om, ya, yb, gpre, uvpre, attn, wout, wa, wb, wsm, wsmt, bsf, g_sgu, b_sgu,
        g_post_mix + sct_ffn_token[0:1, 0:1], wmask_f, egrp)
    dw_out = _dw_matmul(merged_t, dom, 512, "dw_out")
    dw_a = _dw_matmul(attn_t, dya, 512, "dw_a")
    dw_b = _dw_matmul(sgu_t, dyb, 512, "dw_b")
    parts_mix = [_cols_to_blocks(dw_a, D_MODEL // N_DEV), _cols_to_blocks(dw_b, D_MODEL // N_DEV),
                 dw_out.reshape(N_DEV, D_MODEL // N_DEV, D_MODEL)]
    mine_mix = own_of(parts_mix)
    (sct_mix,), sct_mix_token = _exchange_start([parts_mix], "scatter_start_mix", gather=False)

    gk_all, dvt, gqt, col_sums = _attn_bwd(qa, ka, kat, vs, dot_, lse,
                                           delta + sct_mix_token[0, 0], ecol)
    dlogf = _rev_cumsum(col_sums, gqt, triu)
    dx, dproj_t, dgq, dgk, dbf, dg_pre_mix = _proj_bwd(
        gqt, gk_all, dvt, dlogf, flog, qraw, kraw, duv, dgp, xs, dx1, wcat, bdiag, gq512, gk512,
        g_pre_mix, efold)

    small_local = dict(
        g_pre_mix=dg_pre_mix, b_forget=dbf[:, :HEADS], g_q=dgq[0:1, :HEAD_DIM],
        g_k=dgk[0:1, :HEAD_DIM], g_sgu=dg_sgu, b_sgu=db_sgu, w_spatial=dws,
        b_spatial=jnp.transpose(dbs[:, :GROUPS]), g_post_mix=dg_post_mix, g_pre_ffn=dg_pre_ffn,
        g_post_ffn=dg_post_ffn)
    loss_row = jnp.pad(loss_acc[0:1, 0:1], ((0, 0), (0, 1023)))
    small_parts = [_pack_vectors(small_local, loss_row).reshape(N_DEV, VEC_ROWS // N_DEV, 1024),
                   dws]

    def with_own(zones, own_blocks):
        return [_own_block(z, b) for z, b in zip(zones, own_blocks)]

    mine_small = own_of(small_parts)
    (sct_small,), sct_small_token = _exchange_start([small_parts], "scatter_start_small",
                                                    gather=False)
    dw_cat = _dw_matmul(dproj_t, h, C_END // N_DEV, "dw_in", after=(sct_small_token,))
    recv_vec, recv_ws = with_own(
        _exchange_wait(sct_small, dw_cat, "scatter_wait_small", gather=False)[1], mine_small)
    small_sums = [_sum_parts(recv_vec, "sum_vectors"), _sum_parts(recv_ws, "sum_w_spatial")]
    (gat_small,), gat_small_token = _exchange_start([small_sums], "gather_start_small",
                                                    gather=True)
    pair_blocks, own_pair = _pair_sums(dw_cat, "pair_sums_in", gat_small_token)
    rs_in, rs_token = _chip_exchange_start(pair_blocks, "chip_exchange_start_in")

    recv_ffn, recv_down = with_own(
        _exchange_wait(sct_ffn, rs_token, "scatter_wait_ffn", gather=False)[1], mine_ffn)
    recv_a, recv_b, recv_out = with_own(
        _exchange_wait(sct_mix, recv_ffn, "scatter_wait_mix", gather=False)[1], mine_mix)
    received = [None, recv_a, recv_b, recv_out, recv_ffn, recv_down]

    grads, deltas, new_m, new_v = {}, {}, {}, {}
    row_tiles = {"w_in": None, "w_branch_a": 512, "w_branch_b": 512, "w_out": 128, "w_ffn_in": 176,
                 "w_ffn_down": 352}

    def update(k, parts):
        outs = _adamw(parts, local_view(weights[k], k), local_view(mom1[k], k),
                      local_view(mom2[k], k), row_tiles[k], "adamw_" + k,
                      col_tile=256 if k == "w_in" else None,
                      select=in_frame if k == "w_in" else None)
        if k in transposed:
            outs = [jnp.transpose(o) for o in outs]
        grads[k], deltas[k], new_m[k], new_v[k] = [o[None] for o in outs]
        return outs[0]

    last = None
    for idx, k in enumerate(big_names):
        if k != "w_in":
            last = update(k, received[idx])

    (own_vec, own_ws), (zone_vec, zone_ws) = _exchange_wait(gat_small, last, "gather_wait_small",
                                                            gather=True)
    vec_all = _own_block(zone_vec, own_vec).reshape(VEC_ROWS, 1024)
    ws_all = _own_block(zone_ws, own_ws).reshape(1, GROUPS * WINDOW, WINDOW)

    def rows_of(d):
        return d["w_spatial"].reshape(GROUPS * WINDOW, WINDOW)

    outs = _adamw(ws_all, rows_of(weights), rows_of(mom1), rows_of(mom2), GROUPS * WINDOW,
                  "adamw_w_spatial")
    for dst, o in zip((grads, deltas, new_m, new_v), outs):
        dst["w_spatial"] = o.reshape(shapes["w_spatial"])
    sg = outs[0]
    vec_outs = _adamw_vectors(vec_all, *[[d[k] for k in VEC_NAMES] for d in (weights, mom1, mom2)])
    for dst, group in zip((grads, deltas, new_m, new_v), vec_outs):
        dst.update(zip(VEC_NAMES, group))
    arrived = _chip_exchange_wait(rs_in, sg, "chip_exchange_wait_in")
    update("w_in", [own_pair[None], arrived])

    loss = vec_all[LOSS_ROW, 0]
    return (loss, dx.reshape(x.shape), *[grads[k] for k in names], *[deltas[k] for k in names],
            *[new_m[k] for k in names], *[new_v[k] for k in names])
```

```python
import functools
import math

import jax
import jax.numpy as jnp
import numpy as np
from jax import lax
from jax.experimental import pallas as pl
from jax.experimental.pallas import tpu as pltpu

F32 = jnp.float32
BF16 = jnp.bfloat16

D_MODEL = 1024
FOX_W = 512
HEADS = 8
HEAD_DIM = 64
SGU_W = 512
GROUPS = 8
WINDOW = 128
CHUNK = 64
D_FF = 2816
IN_COLS = 4616
EPS = 1e-6
N_DEV = 8
LOG2E = 1.4426950408889634
LN2 = 0.6931471805599453

C_Q, C_K, C_V, C_UV, C_G, C_F, C_END = 0, 512, 1024, 1536, 2560, 4608, 4736

ADAM_LR, ADAM_B1, ADAM_B2, ADAM_EPS, ADAM_WD, ADAM_STEP = 0.001, 0.9, 0.999, 1e-08, 0.01, 10

MIB = 1024 * 1024
TOKEN_TILE = 256
ATTN_TILE = 256
SLAB_W = HEADS * 128
QT_ROWS = 72

BLK = IN_COLS // N_DEV
F_LO = 3 * FOX_W
F_DEV = F_LO // BLK
F_AT = F_LO - F_DEV * BLK
BLK_AT = [BLK * j - (HEADS if BLK * j > F_LO else 0) for j in range(N_DEV)]
FRAME_START = [a // 16 * 16 for a in BLK_AT]
FRAME = 608
FRAME_ROWS = FRAME + 16


def _params(vmem_mib, n_axes):
    return pltpu.CompilerParams(
        dimension_semantics=("arbitrary",) * n_axes, vmem_limit_bytes=vmem_mib * MIB)


def _const_spec(shape):
    nd = len(shape)
    return pl.BlockSpec(shape, lambda *_: (0,) * nd)


def _row_spec(tm, cols):
    return pl.BlockSpec((tm, cols), lambda i: (i, 0))


def _tile_spec(rows, tm):
    return pl.BlockSpec((1, rows, tm), lambda i: (i, 0, 0))


def _split3_dot(x, e):
    x1 = x.astype(BF16)
    r1 = x - x1.astype(F32)
    x2 = r1.astype(BF16)
    x3 = (r1 - x2.astype(F32)).astype(BF16)
    dot = functools.partial(jnp.dot, preferred_element_type=F32)
    return dot(x1, e) + dot(x2, e) + dot(x3, e)


def _tri_dot(tri, x):
    x1 = x.astype(BF16)
    r1 = x - x1.astype(F32)
    x2 = r1.astype(BF16)
    x3 = (r1 - x2.astype(F32)).astype(BF16)
    dot = functools.partial(jnp.dot, preferred_element_type=F32)
    return dot(tri, x1) + dot(tri, x2) + dot(tri, x3)


def _seg_mean(sq, bd_ref):
    hi = sq.astype(BF16)
    bd = bd_ref[...]
    pairs = [jnp.dot(hi[:, p * 128:(p + 1) * 128], bd, preferred_element_type=F32)
             for p in range(HEADS // 2)]
    return jnp.concatenate(pairs, axis=1) * (1.0 / HEAD_DIM)


def _slabs_from_heads(t):
    lane = lax.broadcasted_iota(jnp.int32, (t.shape[0], 128), 1)
    low = lane < HEAD_DIM
    slabs = []
    for p in range(HEADS // 2):
        pair = t[:, p * 128:(p + 1) * 128]
        slabs.append(jnp.where(low, pair, 0.0))
        slabs.append(jnp.where(low, pltpu.roll(pair, HEAD_DIM, 1), 0.0))
    return jnp.concatenate(slabs, axis=1)


def _dot_nt(a, b):
    return lax.dot_general(a, b, (((1,), (1,)), ((), ())), preferred_element_type=F32)


def _sigmoid(x):
    return 0.5 * jnp.tanh(0.5 * x) + 0.5


_GELU_C = math.sqrt(2.0 / math.pi)


def _gelu_and_grad(x):
    inner = _GELU_C * (x + 0.044715 * x * x * x)
    t = jnp.tanh(inner)
    y = 0.5 * x * (1.0 + t)
    dy = 0.5 * (1.0 + t) + 0.5 * x * (1.0 - t * t) * _GELU_C * (1.0 + 3.0 * 0.044715 * x * x)
    return y, dy


def _rms_bwd(xin, r, g, dy):
    dyg = dy * g
    return r * dyg - xin * (r * r * r) * jnp.mean(dyg * xin, axis=-1, keepdims=True)


def _mesh_pos():
    x, y, c = lax.axis_index("x"), lax.axis_index("y"), lax.axis_index("c")
    return x, y, c


def _peer(k):
    x, y, c = _mesh_pos()
    px = (1 - x) if (k >> 2) & 1 else x
    py = (1 - y) if (k >> 1) & 1 else y
    pc = (1 - c) if k & 1 else c
    return (px, py, pc), 4 * px + 2 * py + pc


def _frame_start(j):
    at = BLK * j - jnp.where(BLK * j > F_LO, HEADS, 0)
    return pl.multiple_of(at // 16 * 16, 16)


HALF_A = 320


def _gather_w_in(frame):
    pieces = (slice(0, HALF_A), slice(HALF_A, FRAME_ROWS))

    def body(x_ref, out_ref, zone, send_sems, recv_sems, local_sem):
        x, y, c = _mesh_pos()
        me, sibling = (x, y, c), (x, y, 1 - c)
        nbr_x, nbr_y, across = (1 - x, y, c), (x, 1 - y, c), (1 - x, 1 - y, c)

        def index(pos):
            return 4 * pos[0] + 2 * pos[1] + pos[2]

        def copy(k, block, piece, to, src=None):
            rows = pieces[piece]
            return pltpu.make_async_remote_copy(
                src_ref=(zone.at[index(block), rows] if src is None else src.at[rows]),
                dst_ref=zone.at[index(block), rows],
                send_sem=send_sems.at[k], recv_sem=recv_sems.at[k],
                device_id=to, device_id_type=pl.DeviceIdType.MESH)

        def add(block):
            j = index(block)
            rows = pl.ds(_frame_start(j), FRAME)
            out_ref[rows, :] = (out_ref[rows, :].astype(F32)
                                + zone[j, :FRAME, :].astype(F32)).astype(BF16)
            tail = slice(C_F, C_F + FRAME_ROWS - FRAME)
            forget = zone[j, FRAME:, :].astype(F32) * (j == F_DEV).astype(F32)
            out_ref[tail, :] = (out_ref[tail, :].astype(F32) + forget).astype(BF16)

        mine = pltpu.make_async_copy(x_ref, zone.at[index(me)], local_sem)
        mine.start()
        first = [copy(1, me, 0, nbr_x, src=x_ref), copy(3, me, 1, nbr_y, src=x_ref),
                 copy(2, me, 1, nbr_x, src=x_ref), copy(4, me, 0, nbr_y, src=x_ref)]
        own_to_sibling = pltpu.make_async_remote_copy(
            src_ref=x_ref, dst_ref=zone.at[index(me)], send_sem=send_sems.at[0],
            recv_sem=recv_sems.at[0], device_id=sibling, device_id_type=pl.DeviceIdType.MESH)
        for cp in first:
            cp.start()
        own_to_sibling.start()
        out_ref[...] = jnp.zeros_like(out_ref)
        mine.wait()
        add(me)

        sent = []

        def landed(k, block, piece, forward=None):
            copy(k, block, piece, me).wait_recv()
            if forward is not None:
                cp = copy(*forward)
                cp.start()
                sent.append(cp)
            cp = copy(6 + k, block, piece, sibling)
            cp.start()
            sent.append(cp)

        landed(1, nbr_x, 0, forward=(5, nbr_x, 0, nbr_y))
        landed(3, nbr_y, 1, forward=(6, nbr_y, 1, nbr_x))
        landed(2, nbr_x, 1)
        add(nbr_x)
        landed(4, nbr_y, 0)
        add(nbr_y)
        landed(5, across, 0)
        landed(6, across, 1)
        add(across)
        pltpu.make_async_remote_copy(
            src_ref=x_ref, dst_ref=zone.at[index(sibling)], send_sem=send_sems.at[0],
            recv_sem=recv_sems.at[0], device_id=sibling,
            device_id_type=pl.DeviceIdType.MESH).wait_recv()
        add(sibling)
        for k, block in ((1, nbr_x), (2, nbr_x), (3, nbr_y), (4, nbr_y), (5, across), (6, across)):
            their = (block[0], block[1], 1 - c)
            piece = {1: 0, 2: 1, 3: 1, 4: 0, 5: 0, 6: 1}[k]
            copy(6 + k, their, piece, me).wait_recv()
            if k in (2, 4, 6):
                add(their)
        for cp in first + sent:
            cp.wait_send()
        own_to_sibling.wait_send()

    return pl.pallas_call(
        body, name="gather_w_in", out_shape=jax.ShapeDtypeStruct((C_END, frame.shape[1]), BF16),
        in_specs=[pl.BlockSpec(memory_space=pl.ANY)],
        out_specs=pl.BlockSpec(memory_space=pltpu.VMEM),
        scratch_shapes=[pltpu.VMEM((N_DEV,) + frame.shape, BF16),
                        pltpu.SemaphoreType.DMA((13,)), pltpu.SemaphoreType.DMA((13,)),
                        pltpu.SemaphoreType.DMA],
        compiler_params=pltpu.CompilerParams(vmem_limit_bytes=40 * MIB),
    )(frame)


def _chip_peer(k):
    x, y, c = _mesh_pos()
    px = (1 - x) if (k >> 1) & 1 else x
    py = (1 - y) if k & 1 else y
    return (px, py, c), 2 * px + py


def _pair_sums(dw_cat, name, after):
    rows, cols = FRAME_ROWS, dw_cat.shape[1]
    n_chips = N_DEV // 2

    def pieces(p_ref, j):
        return (p_ref.at[pl.ds(_frame_start(j), FRAME)], p_ref.at[pl.ds(C_F, FRAME_ROWS - FRAME)])

    def body(p_ref, after_ref, send_ref, own_ref, mine_buf, sib_buf, send_sems, recv_sems,
             local_sems):
        x, y, c = _mesh_pos()
        sibling = (x, y, 1 - c)
        copies, local = [], []
        for q in range(n_chips):
            for part, (lo, hi) in enumerate(((0, FRAME), (FRAME, FRAME_ROWS))):
                cp = pltpu.make_async_remote_copy(
                    src_ref=pieces(p_ref, 2 * q + (1 - c))[part], dst_ref=sib_buf.at[q, lo:hi],
                    send_sem=send_sems.at[2 * q + part], recv_sem=recv_sems.at[2 * q + part],
                    device_id=sibling, device_id_type=pl.DeviceIdType.MESH)
                cp.start()
                copies.append(cp)
                lc = pltpu.make_async_copy(pieces(p_ref, 2 * q + c)[part], mine_buf.at[q, lo:hi],
                                           local_sems.at[2 * q + part])
                lc.start()
                local.append(lc)
        for lc in local:
            lc.wait()
        for cp in copies:
            cp.wait_recv()
        for k in range(1, n_chips):
            _, q = _chip_peer(k)
            send_ref[k - 1] = (mine_buf[q].astype(F32) + sib_buf[q].astype(F32)).astype(BF16)
        my_chip = 2 * x + y
        own_ref[...] = mine_buf[my_chip].astype(F32) + sib_buf[my_chip].astype(F32)
        for cp in copies:
            cp.wait_send()

    vmem = pl.BlockSpec(memory_space=pltpu.VMEM)
    return pl.pallas_call(
        body, name=name,
        out_shape=[jax.ShapeDtypeStruct((n_chips - 1, rows, cols), BF16),
                   jax.ShapeDtypeStruct((rows, cols), F32)],
        in_specs=[pl.BlockSpec(memory_space=pl.ANY)] * 2, out_specs=[vmem, vmem],
        scratch_shapes=[pltpu.VMEM((n_chips, rows, cols), BF16),
                        pltpu.VMEM((n_chips, rows, cols), BF16),
                        pltpu.SemaphoreType.DMA((2 * n_chips,)),
                        pltpu.SemaphoreType.DMA((2 * n_chips,)),
                        pltpu.SemaphoreType.DMA((2 * n_chips,))],
        compiler_params=pltpu.CompilerParams(vmem_limit_bytes=40 * MIB),
    )(dw_cat, after)


def _chip_copy(src_ref, land_ref, send_sem, recv_sem, k):
    peer, _ = _chip_peer(k)
    return pltpu.make_async_remote_copy(
        src_ref=src_ref.at[k - 1], dst_ref=land_ref.at[k - 1], send_sem=send_sem, recv_sem=recv_sem,
        device_id=peer, device_id_type=pl.DeviceIdType.MESH)


def _chip_exchange_start(blocks, name):
    hbm = pl.BlockSpec(memory_space=pltpu.HBM)
    sem = pl.BlockSpec(memory_space=pltpu.SEMAPHORE)
    n_peers = blocks.shape[0]

    def body(src_ref, zone_ref, send_sems, recv_sems, src_thru, zone_thru, token):
        for k in range(1, n_peers + 1):
            _chip_copy(src_ref, zone_ref, send_sems.at[k - 1], recv_sems.at[k - 1], k).start()
        token[...] = jnp.zeros_like(token)

    outs = pl.pallas_call(
        body, name=name, in_specs=[hbm, hbm],
        out_shape=[pltpu.SemaphoreType.DMA((n_peers,)), pltpu.SemaphoreType.DMA((n_peers,)),
                   pltpu.HBM(blocks.shape, blocks.dtype), pltpu.HBM(blocks.shape, blocks.dtype),
                   jax.ShapeDtypeStruct((8, 128), F32)],
        out_specs=[sem, sem, hbm, hbm, pl.BlockSpec(memory_space=pltpu.VMEM)],
        input_output_aliases={0: 2, 1: 3},
        compiler_params=pltpu.CompilerParams(
            has_side_effects=pltpu.SideEffectType.DATAFLOW_SIDE_EFFECTING),
    )(pltpu.with_memory_space_constraint(blocks, pltpu.HBM),
      pltpu.with_memory_space_constraint(lax.empty(blocks.shape, blocks.dtype), pltpu.HBM))
    return outs[:4], outs[4]


def _chip_exchange_wait(handle, after, name):
    send_sems, recv_sems, src, zone = handle
    hbm = pl.BlockSpec(memory_space=pltpu.HBM)
    sem = pl.BlockSpec(memory_space=pltpu.SEMAPHORE)

    def body(src_ref, zone_ref, ssem, rsem, *rest):
        for k in range(1, src.shape[0] + 1):
            cp = _chip_copy(src_ref, zone_ref, ssem.at[k - 1], rsem.at[k - 1], k)
            cp.wait_send()
            cp.wait_recv()

    outs = pl.pallas_call(
        body, name=name,
        in_specs=[hbm, hbm, sem, sem] + [pl.BlockSpec(memory_space=pl.ANY)] * len(after),
        out_shape=[pltpu.HBM(src.shape, src.dtype), pltpu.HBM(zone.shape, zone.dtype)],
        out_specs=[hbm, hbm], input_output_aliases={0: 0, 1: 1},
        compiler_params=pltpu.CompilerParams(
            has_side_effects=pltpu.SideEffectType.DATAFLOW_SIDE_EFFECTING),
    )(src, zone, send_sems, recv_sems, *after)
    return outs[1]


def _remote_copy(gather, src_ref, land_ref, send_sem, recv_sem, k, receive_side):
    x, y, c = _mesh_pos()
    me = 4 * x + 2 * y + c
    peer, pidx = _peer(k)
    return pltpu.make_async_remote_copy(
        src_ref=src_ref if gather else src_ref.at[pidx],
        dst_ref=land_ref.at[pidx if receive_side else me],
        send_sem=send_sem, recv_sem=recv_sem,
        device_id=peer, device_id_type=pl.DeviceIdType.MESH)


def _exchange_start(groups, name, gather):
    arrs = [a for g in groups for a in g]
    n, n_groups = len(arrs), len(groups)
    lands = [jax.ShapeDtypeStruct(((N_DEV,) + a.shape) if gather else a.shape, a.dtype)
             for a in arrs]

    def body(*refs):
        srcs, zones = refs[:n], refs[n:2 * n]
        sems = refs[2 * n:2 * n + 2 * n_groups]
        token = refs[-1]
        a = 0
        for gi, g in enumerate(groups):
            send_sems, recv_sems = sems[2 * gi], sems[2 * gi + 1]
            for k in range(1, N_DEV):
                for ai in range(len(g)):
                    slot = ai * (N_DEV - 1) + k - 1
                    _remote_copy(gather, srcs[a + ai], zones[a + ai], send_sems.at[slot],
                                 recv_sems.at[slot], k, False).start()
            a += len(g)
        token[...] = jnp.zeros_like(token)

    hbm = pl.BlockSpec(memory_space=pltpu.HBM)
    sem = pl.BlockSpec(memory_space=pltpu.SEMAPHORE)
    sem_shapes = []
    for g in groups:
        sem_shapes += [pltpu.SemaphoreType.DMA((len(g) * (N_DEV - 1),))] * 2
    outs = pl.pallas_call(
        body, name=name,
        in_specs=[hbm] * (2 * n),
        out_shape=sem_shapes + [pltpu.HBM(a.shape, a.dtype) for a in arrs]
        + [pltpu.HBM(z.shape, z.dtype) for z in lands] + [jax.ShapeDtypeStruct((8, 128), F32)],
        out_specs=[sem] * (2 * n_groups) + [hbm] * (2 * n)
        + [pl.BlockSpec(memory_space=pltpu.VMEM)],
        input_output_aliases={i: 2 * n_groups + i for i in range(2 * n)},
        compiler_params=pltpu.CompilerParams(
            has_side_effects=pltpu.SideEffectType.DATAFLOW_SIDE_EFFECTING),
    )(*[pltpu.with_memory_space_constraint(a, pltpu.HBM) for a in arrs],
      *[pltpu.with_memory_space_constraint(lax.empty(z.shape, z.dtype), pltpu.HBM) for z in lands])
    sems = outs[:2 * n_groups]
    thru = outs[2 * n_groups:2 * n_groups + n]
    zones = outs[2 * n_groups + n:2 * n_groups + 2 * n]
    handles, a = [], 0
    for gi, g in enumerate(groups):
        handles.append((sems[2 * gi], sems[2 * gi + 1], thru[a:a + len(g)], zones[a:a + len(g)]))
        a += len(g)
    return handles, outs[-1]


def _exchange_wait(handle, after, name, gather):
    send_sems, recv_sems, thru, zones = handle
    n = len(thru)

    def body(*refs):
        srcs, lands = refs[:n], refs[n:2 * n]
        ssem, rsem = refs[2 * n], refs[2 * n + 1]
        for k in range(1, N_DEV):
            for ai in range(n):
                slot = ai * (N_DEV - 1) + k - 1
                cp = _remote_copy(gather, srcs[ai], lands[ai], ssem.at[slot], rsem.at[slot], k, True)
                cp.wait_send()
                cp.wait_recv()

    hbm = pl.BlockSpec(memory_space=pltpu.HBM)
    sem = pl.BlockSpec(memory_space=pltpu.SEMAPHORE)
    outs = pl.pallas_call(
        body, name=name,
        in_specs=[hbm] * (2 * n) + [sem, sem, pl.BlockSpec(memory_space=pl.ANY)],
        out_shape=[pltpu.HBM(a.shape, a.dtype) for a in thru]
        + [pltpu.HBM(z.shape, z.dtype) for z in zones],
        out_specs=[hbm] * (2 * n),
        input_output_aliases={i: i for i in range(2 * n)},
        compiler_params=pltpu.CompilerParams(
            has_side_effects=pltpu.SideEffectType.DATAFLOW_SIDE_EFFECTING),
    )(*thru, *zones, send_sems, recv_sems, after)
    return outs[:n], outs[n:]


def _own_block(zone, block):
    x, y, c = _mesh_pos()
    me = 4 * x + 2 * y + c
    return lax.dynamic_update_slice_in_dim(zone, block[None], me, axis=0)


def _proj_fwd(x, g1, wcat, bdiag, gq, gk, bfor, tri, pdq, pdk, ones_q, ones_k):
    s_len = x.shape[0]
    tm = TOKEN_TILE
    nt = s_len // tm

    def body(x_ref, g1_ref, w_ref, bd_ref, gq_ref, gk_ref, bf_ref, tri_ref, pdq_ref,
             pdk_ref, oq_ref, ok_ref,
             h_ref, qa_ref, ka_ref, kat_ref, vs_ref, vt_ref, qr_ref, kr_ref, flog_ref, uv_ref,
             gp_ref, carry):
        @pl.when(pl.program_id(0) == 0)
        def _():
            carry[...] = jnp.zeros_like(carry)

        xf = x_ref[...]
        r = lax.rsqrt(jnp.mean(xf * xf, axis=-1, keepdims=True) + EPS)
        h = (xf * r * g1_ref[...]).astype(BF16)
        h_ref[...] = h
        dot = functools.partial(jnp.dot, preferred_element_type=F32)

        def proj(lo, hi):
            return _dot_nt(h, w_ref[lo:hi, :])

        flog = proj(C_F, C_END) + bf_ref[...]
        flog_ref[...] = flog
        lane = lax.broadcasted_iota(jnp.int32, flog.shape, 1)
        logf = jnp.minimum(flog, 0.0) - jnp.log(1.0 + jnp.exp(-jnp.abs(flog)))
        logf = jnp.where(lane < HEADS, logf, 0.0)
        dcum = _tri_dot(tri_ref[...], logf) + carry[...]
        carry[...] = dcum[tm - 1:tm, :]
        d2 = dcum * LOG2E
        d2a = d2.astype(BF16)
        rem = d2 - d2a.astype(F32)
        d2b = rem.astype(BF16)
        d2c = (rem - d2b.astype(F32)).astype(BF16)

        q = proj(C_Q, C_K)
        qr_ref[...] = q.astype(BF16)
        rq = lax.rsqrt(_seg_mean(q * q, bd_ref) + EPS)
        qn = q * rq * (gq_ref[...] * (HEAD_DIM ** -0.5 * LOG2E))
        d_parts = jnp.concatenate([d2a, d2b, d2c], axis=1)
        qa = _slabs_from_heads(qn) + dot(d_parts, pdq_ref[...]) + oq_ref[...]
        qa_ref[...] = qa.astype(BF16)

        k = proj(C_K, C_V)
        kr_ref[...] = k.astype(BF16)
        rk = lax.rsqrt(_seg_mean(k * k, bd_ref) + EPS)
        kn = k * rk * gk_ref[...]
        ka = _slabs_from_heads(kn) + dot(d_parts, pdk_ref[...]) + ok_ref[...]
        ka_ref[...] = ka.astype(BF16)
        kat_ref[0] = ka.T.astype(BF16)

        v = proj(C_V, C_UV)
        vs_ref[...] = _slabs_from_heads(v).astype(BF16)
        vt_ref[0] = v.T.astype(BF16)
        uv_ref[...] = proj(C_UV, C_G).astype(BF16)
        gp_ref[...] = proj(C_G, C_F).astype(BF16)

    outs = [((s_len, D_MODEL), BF16, _row_spec(tm, D_MODEL)),
            ((s_len, SLAB_W), BF16, _row_spec(tm, SLAB_W)),
            ((s_len, SLAB_W), BF16, _row_spec(tm, SLAB_W)),
            ((nt, SLAB_W, tm), BF16, _tile_spec(SLAB_W, tm)),
            ((s_len, SLAB_W), BF16, _row_spec(tm, SLAB_W)),
            ((nt, FOX_W, tm), BF16, _tile_spec(FOX_W, tm)),
            ((s_len, FOX_W), BF16, _row_spec(tm, FOX_W)),
            ((s_len, FOX_W), BF16, _row_spec(tm, FOX_W)),
            ((s_len, 128), F32, _row_spec(tm, 128)),
            ((s_len, 2 * SGU_W), BF16, _row_spec(tm, 2 * SGU_W)),
            ((s_len, 2 * D_MODEL), BF16, _row_spec(tm, 2 * D_MODEL))]
    return pl.pallas_call(
        body, name="proj_fwd", grid=(nt,),
        in_specs=[_row_spec(tm, D_MODEL), _const_spec((1, D_MODEL)), _const_spec(wcat.shape),
                  _const_spec(bdiag.shape), _const_spec((1, FOX_W)), _const_spec((1, FOX_W)),
                  _const_spec((1, 128)), _const_spec((tm, tm)), _const_spec(pdq.shape), _const_spec(pdk.shape), _const_spec(ones_q.shape),
                  _const_spec(ones_k.shape)],
        out_specs=[o[2] for o in outs],
        out_shape=[jax.ShapeDtypeStruct(o[0], o[1]) for o in outs],
        scratch_shapes=[pltpu.VMEM((1, 128), F32)],
        compiler_params=_params(56, 1),
    )(x, g1, wcat, bdiag, gq, gk, bfor, tri, pdq, pdk, ones_q, ones_k)


def _attn_fwd(qa, ka, vt):
    s_len = qa.shape[0]
    t = ATTN_TILE
    nb = s_len // t

    def body(q_ref, k_ref, vt_ref, o_ref, ot_ref, lse_ref, m_sc, l_sc, acc_sc, s_sc, mcur_sc,
             alpha_sc):
        i = pl.program_id(0)
        m_sc[...] = jnp.full_like(m_sc, -jnp.inf)
        l_sc[...] = jnp.zeros_like(l_sc)
        acc_sc[...] = jnp.zeros_like(acc_sc)

        def logits(j, slot, masked):
            krows = pl.ds(pl.multiple_of(j * t, t), t)
            if masked:
                keep = (lax.broadcasted_iota(jnp.int32, (t, t), 0)
                        <= lax.broadcasted_iota(jnp.int32, (t, t), 1))
            for hd in range(HEADS):
                sl = slice(hd * 128, (hd + 1) * 128)
                st = _dot_nt(k_ref[krows, sl], q_ref[:, sl])
                if masked:
                    st = jnp.where(keep, st, -jnp.inf)
                s_sc[slot, hd] = st
                m_prev = m_sc[hd:hd + 1, :]
                m_new = jnp.maximum(m_prev, jnp.max(st, axis=0, keepdims=True))
                alpha_sc[slot, hd:hd + 1, :] = jnp.exp2(m_prev - m_new)
                mcur_sc[slot, hd:hd + 1, :] = m_new
                m_sc[hd:hd + 1, :] = m_new

        def accumulate(j, slot):
            for hd in range(HEADS):
                hr = slice(hd * HEAD_DIM, (hd + 1) * HEAD_DIM)
                alpha = alpha_sc[slot, hd:hd + 1, :]
                pt = jnp.exp2(s_sc[slot, hd] - mcur_sc[slot, hd:hd + 1, :])
                l_sc[hd:hd + 1, :] = alpha * l_sc[hd:hd + 1, :] + jnp.sum(pt, axis=0, keepdims=True)
                acc_sc[hr, :] = alpha * acc_sc[hr, :] + jnp.dot(
                    vt_ref[j, hr, :], pt.astype(BF16), preferred_element_type=F32)

        @pl.when(i == 0)
        def _():
            logits(0, 0, True)
            accumulate(0, 0)

        pairs = (i - 1) // 2

        @pl.when(i > 0)
        def _():
            logits(0, 0, False)

            def two_blocks(p, carry):
                logits(2 * p + 1, 1, False)
                accumulate(2 * p, 0)
                logits(2 * p + 2, 0, False)
                accumulate(2 * p + 1, 1)
                return carry

            lax.fori_loop(0, pairs, two_blocks, 0)

        @pl.when((i > 0) & (i - 2 * pairs == 1))
        def _():
            logits(i, 1, True)
            accumulate(i - 1, 0)
            accumulate(i, 1)

        @pl.when((i > 0) & (i - 2 * pairs == 2))
        def _():
            logits(i - 1, 1, False)
            accumulate(i - 2, 0)
            logits(i, 0, True)
            accumulate(i - 1, 1)
            accumulate(i, 0)

        for hd in range(HEADS):
            hr = slice(hd * HEAD_DIM, (hd + 1) * HEAD_DIM)
            l = l_sc[hd:hd + 1, :]
            acc_sc[hr, :] = acc_sc[hr, :] / l
            lse_ref[0, hd:hd + 1, :] = m_sc[hd:hd + 1, :] + jnp.log2(l)
        o_ref[...] = acc_sc[...].T.astype(BF16)
        ot_ref[...] = acc_sc[...].astype(BF16)

    return pl.pallas_call(
        body, name="attn_fwd", grid=(nb,),
        in_specs=[_row_spec(t, SLAB_W), _const_spec(ka.shape), _const_spec(vt.shape)],
        out_specs=[_row_spec(t, FOX_W), pl.BlockSpec((FOX_W, t), lambda i: (0, i)),
                   _tile_spec(HEADS, t)],
        out_shape=[jax.ShapeDtypeStruct((s_len, FOX_W), BF16),
                   jax.ShapeDtypeStruct((FOX_W, s_len), BF16),
                   jax.ShapeDtypeStruct((nb, HEADS, t), F32)],
        scratch_shapes=[pltpu.VMEM((HEADS, t), F32), pltpu.VMEM((HEADS, t), F32),
                        pltpu.VMEM((FOX_W, t), F32), pltpu.VMEM((2, HEADS, t, t), F32),
                        pltpu.VMEM((2, HEADS, t), F32), pltpu.VMEM((2, HEADS, t), F32)],
        compiler_params=_params(48, 1),
    )(qa, ka, vt)


def _sgu_mix(vn, ws_ref):
    tm = vn.shape[0]
    lane = lax.broadcasted_iota(jnp.int32, (WINDOW, 128), 1)
    low = lane < HEAD_DIM
    wins = []
    for w in range(tm // WINDOW):
        slabs = []
        for p in range(GROUPS // 2):
            v2 = vn[w * WINDOW:(w + 1) * WINDOW, p * 128:(p + 1) * 128]
            lo = jnp.where(low, v2, 0.0).astype(BF16)
            hi = jnp.where(low, 0.0, v2).astype(BF16)
            slabs.append(jnp.dot(ws_ref[2 * p], lo, preferred_element_type=F32)
                         + jnp.dot(ws_ref[2 * p + 1], hi, preferred_element_type=F32))
        wins.append(jnp.concatenate(slabs, axis=1))
    return jnp.concatenate(wins, axis=0) if len(wins) > 1 else wins[0]


def _layernorm_fwd(vv, g, b):
    mu = jnp.mean(vv, axis=-1, keepdims=True)
    xc = vv - mu
    r = lax.rsqrt(jnp.mean(xc * xc, axis=-1, keepdims=True) + EPS)
    xh = xc * r
    return xh * g + b, xh, r


def _mix_fwd(attn, uvpre, gpre, x, wa, wb, wout, wsm, bsf, gsgu, bsgu, gpost):
    s_len = x.shape[0]
    tm = TOKEN_TILE

    def body(o_ref, uv_ref, gp_ref, x_ref, wa_ref, wb_ref, wo_ref, ws_ref, bs_ref, gs_ref, bsg_ref,
             gpost_ref, sgut_ref, ya_ref, yb_ref, mgt_ref, om_ref, x1_ref):
        uvp = uv_ref[...].astype(F32)
        uv, _ = _gelu_and_grad(uvp)
        u, vv = uv[:, :SGU_W], uv[:, SGU_W:]
        vn, _, _ = _layernorm_fwd(vv, gs_ref[...], bsg_ref[...])
        bias = bs_ref[...]
        if tm > WINDOW:
            bias = jnp.concatenate([bias] * (tm // WINDOW), axis=0)
        mixed = _sgu_mix(vn, ws_ref) + bias
        sgu_f = u * mixed
        sgu = sgu_f.astype(BF16)
        sgut_ref[...] = sgu_f.T.astype(BF16)
        ya = jnp.dot(o_ref[...], wa_ref[...], preferred_element_type=F32)
        yb = jnp.dot(sgu, wb_ref[...], preferred_element_type=F32)
        ya_ref[...] = ya.astype(BF16)
        yb_ref[...] = yb.astype(BF16)
        gates = _sigmoid(gp_ref[...].astype(F32))
        merged_f = gates[:, :D_MODEL] * ya + gates[:, D_MODEL:] * yb
        merged = merged_f.astype(BF16)
        mgt_ref[...] = merged_f.T.astype(BF16)
        om = jnp.dot(merged, wo_ref[...], preferred_element_type=F32)
        om_ref[...] = om
        r = lax.rsqrt(jnp.mean(om * om, axis=-1, keepdims=True) + EPS)
        x1_ref[...] = x_ref[...] + om * r * gpost_ref[...]

    def t_out(rows):
        return ((rows, s_len), BF16, pl.BlockSpec((rows, tm), lambda i: (0, i)))

    def r_out(cols, dt):
        return ((s_len, cols), dt, _row_spec(tm, cols))

    outs = [t_out(SGU_W), r_out(D_MODEL, BF16), r_out(D_MODEL, BF16), t_out(D_MODEL),
            r_out(D_MODEL, F32), r_out(D_MODEL, F32)]
    return pl.pallas_call(
        body, name="mix_fwd", grid=(s_len // tm,),
        in_specs=[_row_spec(tm, FOX_W), _row_spec(tm, 2 * SGU_W), _row_spec(tm, 2 * D_MODEL),
                  _row_spec(tm, D_MODEL), _const_spec(wa.shape), _const_spec(wb.shape),
                  _const_spec(wout.shape), _const_spec(wsm.shape), _const_spec(bsf.shape),
                  _const_spec((1, SGU_W)), _const_spec((1, SGU_W)), _const_spec((1, D_MODEL))],
        out_specs=[o[2] for o in outs],
        out_shape=[jax.ShapeDtypeStruct(o[0], o[1]) for o in outs],
        compiler_params=_params(48, 1),
    )(attn, uvpre, gpre, x, wa, wb, wout, wsm, bsf, gsgu, bsgu, gpost)


def _ffn_fwd_bwd(x1, tgt, wffn, wdown, gpre, gpost):
    s_len = x1.shape[0]
    tm = TOKEN_TILE

    def body(x1_ref, t_ref, wi_ref, wd_ref, gpre_ref, gpost_ref,
             dx1_ref, h2_ref, actt_ref, dff_ref, dgut_ref, loss_ref, dgpost_ref, dgpre_ref):
        @pl.when(pl.program_id(0) == 0)
        def _():
            loss_ref[...] = jnp.zeros_like(loss_ref)
            dgpost_ref[...] = jnp.zeros_like(dgpost_ref)
            dgpre_ref[...] = jnp.zeros_like(dgpre_ref)

        x1v = x1_ref[...]
        r2 = lax.rsqrt(jnp.mean(x1v * x1v, axis=-1, keepdims=True) + EPS)
        gpre_v = gpre_ref[...]
        h2 = (x1v * r2 * gpre_v).astype(BF16)
        h2_ref[...] = h2
        gg = _dot_nt(h2, wi_ref[:D_FF, :])
        uu = _dot_nt(h2, wi_ref[D_FF:, :])
        sg = _sigmoid(gg)
        silu = gg * sg
        act_f = silu * uu
        act = act_f.astype(BF16)
        actt_ref[...] = act_f.T.astype(BF16)
        ff = jnp.dot(act, wd_ref[...], preferred_element_type=F32)
        r3 = lax.rsqrt(jnp.mean(ff * ff, axis=-1, keepdims=True) + EPS)
        gpost_v = gpost_ref[...]
        y = x1v + ff * r3 * gpost_v
        err = y - t_ref[...]
        loss_ref[...] += jnp.sum(err * err) * (0.5 / D_MODEL)
        dy = err * (1.0 / D_MODEL)
        dgpost_ref[...] += jnp.sum(dy * ff * r3, axis=0, keepdims=True)
        dff = _rms_bwd(ff, r3, gpost_v, dy).astype(BF16)
        dff_ref[...] = dff
        dact = _dot_nt(dff, wd_ref[...])
        dgg_f = dact * uu * (sg * (1.0 + gg * (1.0 - sg)))
        duu_f = dact * silu
        dgg = dgg_f.astype(BF16)
        duu = duu_f.astype(BF16)
        dgut_ref[:D_FF, :] = dgg_f.T.astype(BF16)
        dgut_ref[D_FF:, :] = duu_f.T.astype(BF16)
        dh2 = (jnp.dot(dgg, wi_ref[:D_FF, :], preferred_element_type=F32)
               + jnp.dot(duu, wi_ref[D_FF:, :], preferred_element_type=F32))
        dgpre_ref[...] += jnp.sum(dh2 * x1v * r2, axis=0, keepdims=True)
        dx1_ref[...] = dy + _rms_bwd(x1v, r2, gpre_v, dh2)

    outs = [((s_len, D_MODEL), F32, _row_spec(tm, D_MODEL)),
            ((s_len, D_MODEL), BF16, _row_spec(tm, D_MODEL)),
            ((D_FF, s_len), BF16, pl.BlockSpec((D_FF, tm), lambda i: (0, i))),
            ((s_len, D_MODEL), BF16, _row_spec(tm, D_MODEL)),
            ((2 * D_FF, s_len), BF16, pl.BlockSpec((2 * D_FF, tm), lambda i: (0, i))),
            ((1, 128), F32, _const_spec((1, 128))),
            ((1, D_MODEL), F32, _const_spec((1, D_MODEL))),
            ((1, D_MODEL), F32, _const_spec((1, D_MODEL)))]
    return pl.pallas_call(
        body, name="ffn_fwd_bwd", grid=(s_len // tm,),
        in_specs=[_row_spec(tm, D_MODEL), _row_spec(tm, D_MODEL), _const_spec(wffn.shape),
                  _const_spec(wdown.shape), _const_spec((1, D_MODEL)), _const_spec((1, D_MODEL))],
        out_specs=[o[2] for o in outs],
        out_shape=[jax.ShapeDtypeStruct(o[0], o[1]) for o in outs],
        compiler_params=_params(60, 1),
    )(x1, tgt, wffn, wdown, gpre, gpost)


def _mix_bwd(dx1, om, ya, yb, gpre, uvpre, attn, wout, wa, wb, wsm, wsmt, bsf, gsgu, bsgu, gpost,
             wmask, egrp):
    s_len = dx1.shape[0]
    tm = TOKEN_TILE
    nw = tm // WINDOW
    nt = s_len // tm

    def body(dx1_ref, om_ref, ya_ref, yb_ref, gp_ref, uv_ref, o_ref, wo_ref, wa_ref, wb_ref, ws_ref,
             wst_ref, bs_ref, gs_ref, bsg_ref, gpost_ref, mask_ref, eg_ref,
             dom_ref, dya_ref, dyb_ref, dgp_ref, dot_ref, delta_ref, duv_ref,
             dws_ref, dbs_ref, dgs_ref, dbsg_ref, dgpost_ref, dbs_acc):
        step = pl.program_id(0)

        @pl.when(step == 0)
        def _():
            dws_ref[...] = jnp.zeros_like(dws_ref)
            dbs_acc[...] = jnp.zeros_like(dbs_acc)
            dgs_ref[...] = jnp.zeros_like(dgs_ref)
            dbsg_ref[...] = jnp.zeros_like(dbsg_ref)
            dgpost_ref[...] = jnp.zeros_like(dgpost_ref)

        om = om_ref[...]
        dx1v = dx1_ref[...]
        r = lax.rsqrt(jnp.mean(om * om, axis=-1, keepdims=True) + EPS)
        gpost_v = gpost_ref[...]
        dgpost_ref[...] += jnp.sum(dx1v * om * r, axis=0, keepdims=True)
        dom = _rms_bwd(om, r, gpost_v, dx1v).astype(BF16)
        dom_ref[...] = dom
        dmg = _dot_nt(dom, wo_ref[...])

        gates = _sigmoid(gp_ref[...].astype(F32))
        ga, gb = gates[:, :D_MODEL], gates[:, D_MODEL:]
        yav, ybv = ya_ref[...].astype(F32), yb_ref[...].astype(F32)
        dya = (dmg * ga).astype(BF16)
        dyb = (dmg * gb).astype(BF16)
        dya_ref[...] = dya
        dyb_ref[...] = dyb
        dgp_ref[:, :D_MODEL] = (dmg * yav * ga * (1.0 - ga)).astype(BF16)
        dgp_ref[:, D_MODEL:] = (dmg * ybv * gb * (1.0 - gb)).astype(BF16)

        dat_t = _dot_nt(dya, wa_ref[...]).T.astype(BF16)
        dot_ref[0] = dat_t
        o_t = o_ref[...].astype(F32).T
        delta_ref[0] = jnp.sum((dat_t.astype(F32) * o_t).reshape(HEADS, HEAD_DIM, tm), axis=1)
        dsgu = _dot_nt(dyb, wb_ref[...])

        uvp = uv_ref[...].astype(F32)
        uv, guv = _gelu_and_grad(uvp)
        u, vv = uv[:, :SGU_W], uv[:, SGU_W:]
        gs_v = gs_ref[...]
        vn, xh, rln = _layernorm_fwd(vv, gs_v, bsg_ref[...])
        bias = bs_ref[...]
        if nw > 1:
            bias = jnp.concatenate([bias] * nw, axis=0)
        mixed = _sgu_mix(vn, ws_ref) + bias
        du = dsgu * mixed
        dmixed = dsgu * u

        lane = lax.broadcasted_iota(jnp.int32, (WINDOW, 128), 1)
        low = lane < HEAD_DIM
        dvn_wins = []
        for w in range(nw):
            rows = slice(w * WINDOW, (w + 1) * WINDOW)
            dbs_acc[...] += dmixed[rows, :]
            slabs = []
            for p in range(GROUPS // 2):
                cols = slice(p * 128, (p + 1) * 128)
                dm2 = dmixed[rows, cols]
                dlo = jnp.where(low, dm2, 0.0).astype(BF16)
                dhi = jnp.where(low, 0.0, dm2).astype(BF16)
                vn2 = vn[rows, cols].astype(BF16)
                dws_ref[2 * p] += _dot_nt(dlo, vn2)
                dws_ref[2 * p + 1] += _dot_nt(dhi, vn2)
                slabs.append(jnp.dot(wst_ref[2 * p], dlo, preferred_element_type=F32)
                             + jnp.dot(wst_ref[2 * p + 1], dhi, preferred_element_type=F32))
            dvn_wins.append(jnp.concatenate(slabs, axis=1))
        dvn = jnp.concatenate(dvn_wins, axis=0) if nw > 1 else dvn_wins[0]

        dgs_ref[...] += jnp.sum(dvn * xh, axis=0, keepdims=True)
        dbsg_ref[...] += jnp.sum(dvn, axis=0, keepdims=True)
        dxh = dvn * gs_v
        dvv = rln * (dxh - jnp.mean(dxh, axis=-1, keepdims=True)
                     - xh * jnp.mean(dxh * xh, axis=-1, keepdims=True))
        duv_ref[:, :SGU_W] = (du * guv[:, :SGU_W]).astype(BF16)
        duv_ref[:, SGU_W:] = (dvv * guv[:, SGU_W:]).astype(BF16)

        @pl.when(step == pl.num_programs(0) - 1)
        def _():
            for g in range(GROUPS):
                dws_ref[g] = dws_ref[g] * mask_ref[...]
            dbs_ref[...] = _split3_dot(dbs_acc[...], eg_ref[...])

    rows_out = [((s_len, D_MODEL), BF16, _row_spec(tm, D_MODEL)),
                ((s_len, D_MODEL), BF16, _row_spec(tm, D_MODEL)),
                ((s_len, D_MODEL), BF16, _row_spec(tm, D_MODEL)),
                ((s_len, 2 * D_MODEL), BF16, _row_spec(tm, 2 * D_MODEL)),
                ((nt, FOX_W, tm), BF16, _tile_spec(FOX_W, tm)),
                ((nt, HEADS, tm), F32, _tile_spec(HEADS, tm)),
                ((s_len, 2 * SGU_W), BF16, _row_spec(tm, 2 * SGU_W))]
    acc_out = [((GROUPS, WINDOW, WINDOW), F32), ((WINDOW, 128), F32), ((1, SGU_W), F32),
               ((1, SGU_W), F32), ((1, D_MODEL), F32)]
    return pl.pallas_call(
        body, name="mix_bwd", grid=(nt,),
        in_specs=[_row_spec(tm, D_MODEL), _row_spec(tm, D_MODEL), _row_spec(tm, D_MODEL),
                  _row_spec(tm, D_MODEL), _row_spec(tm, 2 * D_MODEL), _row_spec(tm, 2 * SGU_W),
                  _row_spec(tm, FOX_W), _const_spec(wout.shape), _const_spec(wa.shape),
                  _const_spec(wb.shape), _const_spec(wsm.shape), _const_spec(wsmt.shape),
                  _const_spec(bsf.shape), _const_spec((1, SGU_W)), _const_spec((1, SGU_W)),
                  _const_spec((1, D_MODEL)), _const_spec(wmask.shape), _const_spec(egrp.shape)],
        out_specs=[o[2] for o in rows_out] + [_const_spec(s) for s, _ in acc_out],
        out_shape=[jax.ShapeDtypeStruct(o[0], o[1]) for o in rows_out]
        + [jax.ShapeDtypeStruct(s, dt) for s, dt in acc_out],
        scratch_shapes=[pltpu.VMEM((WINDOW, SGU_W), F32)],
        compiler_params=_params(48, 1),
    )(dx1, om, ya, yb, gpre, uvpre, attn, wout, wa, wb, wsm, wsmt, bsf, gsgu, bsgu, gpost, wmask,
      egrp)


def _attn_bwd(qa, ka, kat, vs, dot_, lse, delta, ecol):
    s_len = qa.shape[0]
    t = ATTN_TILE
    nb = s_len // t

    def body(k_ref, kt_ref, vs_ref, q_ref, do_ref, lse_ref, dl_ref, ec_ref, gk_ref, dvt_ref,
             gqt_ref, csum_ref, p_sc, ds_sc):
        j = pl.program_id(0)

        @pl.when(j == 0)
        def _():
            gqt_ref[...] = jnp.zeros_like(gqt_ref)

        gk_ref[...] = jnp.zeros_like(gk_ref)
        dvt_ref[...] = jnp.zeros_like(dvt_ref)

        def probs(i, slot, masked):
            qrows = pl.ds(pl.multiple_of(i * t, t), t)
            if masked:
                keep = (lax.broadcasted_iota(jnp.int32, (t, t), 0)
                        <= lax.broadcasted_iota(jnp.int32, (t, t), 1))
            for hd in range(HEADS):
                sl = slice(hd * 128, (hd + 1) * 128)
                hr = slice(hd * HEAD_DIM, (hd + 1) * HEAD_DIM)
                st = _dot_nt(k_ref[:, sl], q_ref[qrows, sl])
                if masked:
                    st = jnp.where(keep, st, -jnp.inf)
                pt = jnp.exp2(st - lse_ref[i, hd:hd + 1, :])
                dpt = jnp.dot(vs_ref[:, hd * 128:hd * 128 + HEAD_DIM], do_ref[i, hr, :],
                              preferred_element_type=F32)
                p_sc[slot, hd] = pt.astype(BF16)
                ds_sc[slot, hd] = (pt * (dpt - dl_ref[i, hd:hd + 1, :])).astype(BF16)

        def grads(i, slot):
            qrows = pl.ds(pl.multiple_of(i * t, t), t)
            for hd in range(HEADS):
                sl = slice(hd * 128, (hd + 1) * 128)
                hr = slice(hd * HEAD_DIM, (hd + 1) * HEAD_DIM)
                dst = ds_sc[slot, hd]
                dvt_ref[0, hr, :] += _dot_nt(do_ref[i, hr, :], p_sc[slot, hd])
                gk_ref[:, sl] += jnp.dot(dst, q_ref[qrows, sl], preferred_element_type=F32)
                gqt_ref[i, hd * QT_ROWS:(hd + 1) * QT_ROWS, :] += jnp.dot(
                    kt_ref[0, hd * 128:hd * 128 + QT_ROWS, :], dst, preferred_element_type=F32)

        probs(j, 0, True)
        pairs = (nb - 1 - j) // 2

        def two_blocks(p, carry):
            i1 = j + 1 + 2 * p
            probs(i1, 1, False)
            grads(i1 - 1, 0)
            probs(i1 + 1, 0, False)
            grads(i1, 1)
            return carry

        lax.fori_loop(0, pairs, two_blocks, 0)

        @pl.when(nb - 1 - j - 2 * pairs == 0)
        def _():
            grads(nb - 1, 0)

        @pl.when(nb - 1 - j - 2 * pairs == 1)
        def _():
            probs(nb - 1, 1, False)
            grads(nb - 2, 0)
            grads(nb - 1, 1)

        csum_ref[...] = _split3_dot(gk_ref[...], ec_ref[...])

    return pl.pallas_call(
        body, name="attn_bwd", grid=(nb,),
        in_specs=[_row_spec(t, SLAB_W), _tile_spec(SLAB_W, t), _row_spec(t, SLAB_W),
                  _const_spec(qa.shape), _const_spec(dot_.shape), _const_spec(lse.shape),
                  _const_spec(delta.shape), _const_spec(ecol.shape)],
        out_specs=[_row_spec(t, SLAB_W), _tile_spec(FOX_W, t),
                   _const_spec((nb, HEADS * QT_ROWS, t)), _row_spec(t, 128)],
        out_shape=[jax.ShapeDtypeStruct((s_len, SLAB_W), F32),
                   jax.ShapeDtypeStruct((nb, FOX_W, t), F32),
                   jax.ShapeDtypeStruct((nb, HEADS * QT_ROWS, t), F32),
                   jax.ShapeDtypeStruct((s_len, 128), F32)],
        scratch_shapes=[pltpu.VMEM((2, HEADS, t, t), BF16), pltpu.VMEM((2, HEADS, t, t), BF16)],
        compiler_params=_params(60, 1),
    )(ka, kat, vs, qa, dot_, lse, delta, ecol)


def _rev_cumsum(col_sums, gqt, triu):
    s_len = col_sums.shape[0]
    tm = TOKEN_TILE
    n = s_len // tm

    def body(cs_ref, gqt_ref, tri_ref, o_ref, carry):
        @pl.when(pl.program_id(0) == 0)
        def _():
            carry[...] = jnp.zeros_like(carry)
        rows = [gqt_ref[0, hd * QT_ROWS + HEAD_DIM:hd * QT_ROWS + HEAD_DIM + 1, :]
                for hd in range(HEADS)]
        row_sums = jnp.concatenate(rows + [jnp.zeros((128 - HEADS, tm), F32)], axis=0).T
        out = _tri_dot(tri_ref[...], row_sums - cs_ref[...]) + carry[...]
        o_ref[...] = out
        carry[...] = out[0:1, :]

    return pl.pallas_call(
        body, name="rev_cumsum", grid=(n,),
        in_specs=[pl.BlockSpec((tm, 128), lambda i: (n - 1 - i, 0)),
                  pl.BlockSpec((1, HEADS * QT_ROWS, tm), lambda i: (n - 1 - i, 0, 0)),
                  _const_spec((tm, tm))],
        out_specs=pl.BlockSpec((tm, 128), lambda i: (n - 1 - i, 0)),
        out_shape=jax.ShapeDtypeStruct((s_len, 128), F32),
        scratch_shapes=[pltpu.VMEM((1, 128), F32)],
        compiler_params=_params(32, 1),
    )(col_sums, gqt, triu)


def _heads_from_slabs(slabs):
    lane = lax.broadcasted_iota(jnp.int32, slabs[0].shape, 1)
    low = lane < HEAD_DIM
    pairs = [jnp.where(low, slabs[2 * p], pltpu.roll(slabs[2 * p + 1], HEAD_DIM, 1))
             for p in range(HEADS // 2)]
    return jnp.concatenate(pairs, axis=1)


def _proj_bwd(gqt, gk, dvt, dlogf, flog, qraw, kraw, duv, dgp, x, dx1, wcat, bdiag, gq, gk_gain, g1,
              efold):
    s_len = x.shape[0]
    tm = TOKEN_TILE

    def body(gqt_ref, gkk_ref, dvt_ref, dlf_ref, flog_ref, qr_ref, kr_ref, duv_ref, dgp_ref, x_ref,
             dx1_ref, w_ref, bd_ref, gq_ref, gk_ref, g1_ref, ef_ref,
             dx_ref, dprojt_ref, dgq_ref, dgk_ref, dbf_ref, dg1_ref, gq_acc, gk_acc):
        step = pl.program_id(0)

        @pl.when(step == 0)
        def _():
            gq_acc[...] = jnp.zeros_like(gq_acc)
            gk_acc[...] = jnp.zeros_like(gk_acc)
            dbf_ref[...] = jnp.zeros_like(dbf_ref)
            dg1_ref[...] = jnp.zeros_like(dg1_ref)

        pad = jnp.zeros((128 - QT_ROWS, tm), F32)
        q_slabs = [jnp.concatenate([gqt_ref[0, hd * QT_ROWS:(hd + 1) * QT_ROWS, :], pad], axis=0).T
                   for hd in range(HEADS)]
        dqn = _heads_from_slabs(q_slabs)
        dkn = _heads_from_slabs([gkk_ref[:, hd * 128:(hd + 1) * 128] for hd in range(HEADS)])

        def head_bwd(raw_ref, dn, g_ref, acc):
            raw = raw_ref[...].astype(F32)
            r = lax.rsqrt(_seg_mean(raw * raw, bd_ref) + EPS)
            xhat = raw * r
            acc[0:1, :] += jnp.sum(dn * xhat, axis=0, keepdims=True)
            dyg = dn * g_ref[...]
            return r * (dyg - xhat * _seg_mean(dyg * xhat, bd_ref))

        dot = functools.partial(jnp.dot, preferred_element_type=F32)
        duv, dgp = duv_ref[...], dgp_ref[...]
        dprojt_ref[C_UV:C_G, :] = duv.astype(F32).T.astype(BF16)
        dprojt_ref[C_G:C_F, :] = dgp.astype(F32).T.astype(BF16)
        dh = dot(duv, w_ref[C_UV:C_G, :]) + dot(dgp, w_ref[C_G:C_F, :])

        dq = head_bwd(qr_ref, dqn * HEAD_DIM ** -0.5, gq_ref, gq_acc)
        dk = head_bwd(kr_ref, dkn * LN2, gk_ref, gk_acc)
        dv_t = dvt_ref[0]
        dfl = dlf_ref[...] * _sigmoid(-flog_ref[...])
        dbf_ref[...] += jnp.sum(dfl, axis=0, keepdims=True)
        dprojt_ref[C_Q:C_K, :] = dq.T.astype(BF16)
        dprojt_ref[C_K:C_V, :] = dk.T.astype(BF16)
        dprojt_ref[C_V:C_UV, :] = dv_t.astype(BF16)
        dprojt_ref[C_F:C_END, :] = dfl.T.astype(BF16)
        dh = (dh + dot(dq.astype(BF16), w_ref[C_Q:C_K, :]) + dot(dk.astype(BF16), w_ref[C_K:C_V, :])
              + dot(dv_t.T.astype(BF16), w_ref[C_V:C_UV, :])
              + dot(dfl.astype(BF16), w_ref[C_F:C_END, :]))
        xf = x_ref[...]
        r = lax.rsqrt(jnp.mean(xf * xf, axis=-1, keepdims=True) + EPS)
        dg1_ref[...] += jnp.sum(dh * xf * r, axis=0, keepdims=True)
        dx_ref[...] = dx1_ref[...] + _rms_bwd(xf, r, g1_ref[...], dh)

        @pl.when(step == pl.num_programs(0) - 1)
        def _():
            dgq_ref[...] = _split3_dot(gq_acc[...], ef_ref[...])
            dgk_ref[...] = _split3_dot(gk_acc[...], ef_ref[...])

    outs = [((s_len, D_MODEL), F32, _row_spec(tm, D_MODEL)),
            ((C_END, s_len), BF16, pl.BlockSpec((C_END, tm), lambda i: (0, i))),
            ((8, 128), F32, _const_spec((8, 128))),
            ((8, 128), F32, _const_spec((8, 128))),
            ((1, 128), F32, _const_spec((1, 128))),
            ((1, D_MODEL), F32, _const_spec((1, D_MODEL)))]
    return pl.pallas_call(
        body, name="proj_bwd", grid=(s_len // tm,),
        in_specs=[_tile_spec(HEADS * QT_ROWS, tm), _row_spec(tm, SLAB_W), _tile_spec(FOX_W, tm),
                  _row_spec(tm, 128), _row_spec(tm, 128), _row_spec(tm, FOX_W),
                  _row_spec(tm, FOX_W), _row_spec(tm, 2 * SGU_W), _row_spec(tm, 2 * D_MODEL),
                  _row_spec(tm, D_MODEL), _row_spec(tm, D_MODEL), _const_spec(wcat.shape),
                  _const_spec(bdiag.shape), _const_spec((1, FOX_W)), _const_spec((1, FOX_W)),
                  _const_spec((1, D_MODEL)), _const_spec(efold.shape)],
        out_specs=[o[2] for o in outs],
        out_shape=[jax.ShapeDtypeStruct(o[0], o[1]) for o in outs],
        scratch_shapes=[pltpu.VMEM((8, FOX_W), F32), pltpu.VMEM((8, FOX_W), F32)],
        compiler_params=_params(56, 1),
    )(gqt, gk, dvt, dlogf, flog, qraw, kraw, duv, dgp, x, dx1, wcat, bdiag, gq, gk_gain, g1, efold)


def _dw_matmul(at, b, tm, name, after=()):
    m, s_len = at.shape
    n = b.shape[1]

    def body(a_ref, b_ref, *rest):
        rest[-1][...] = jnp.dot(a_ref[...], b_ref[...], preferred_element_type=F32).astype(BF16)

    return pl.pallas_call(
        body, name=name, grid=(m // tm,),
        in_specs=[pl.BlockSpec((tm, s_len), lambda i: (i, 0)), _const_spec(b.shape)]
        + [pl.BlockSpec(memory_space=pl.ANY)] * len(after),
        out_specs=pl.BlockSpec((tm, n), lambda i: (i, 0)),
        out_shape=jax.ShapeDtypeStruct((m, n), BF16),
        compiler_params=_params(48, 1),
    )(at, b, *after)


def _adamw(parts, w, m, v, tr, name, col_tile=None, select=None):
    parts = parts if isinstance(parts, (list, tuple)) else [parts]
    rows, cols = w.shape
    extra = [] if select is None else [select]
    bc1 = 1.0 - ADAM_B1 ** ADAM_STEP
    bc2 = 1.0 - ADAM_B2 ** ADAM_STEP

    def body(*refs):
        p_refs = refs[:len(parts)]
        sel_refs = refs[len(parts):len(parts) + len(extra)]
        w_ref, m_ref, v_ref, g_ref, d_ref, mo_ref, vo_ref = refs[len(parts) + len(extra):]
        g = None
        for p_ref, p in zip(p_refs, parts):
            for idx in range(p.shape[0]):
                term = p_ref[idx].astype(F32)
                g = term if g is None else g + term
        if sel_refs:
            g = _tri_dot(sel_refs[0][...], g)
        g_ref[...] = g
        mn = ADAM_B1 * m_ref[...] + (1.0 - ADAM_B1) * g
        vn = ADAM_B2 * v_ref[...] + (1.0 - ADAM_B2) * (g * g)
        mo_ref[...] = mn
        vo_ref[...] = vn
        m_hat = mn / bc1
        v_hat = vn / bc2
        d_ref[...] = -ADAM_LR * (m_hat / (jnp.sqrt(v_hat) + ADAM_EPS) + ADAM_WD * w_ref[...])

    if col_tile is None:
        spec = pl.BlockSpec((tr, cols), lambda i: (i, 0))
        pspecs = [pl.BlockSpec((p.shape[0], tr, cols), lambda i: (0, i, 0)) for p in parts]
        steps = rows // tr
    else:
        spec = pl.BlockSpec((rows, col_tile), lambda i: (0, i))
        pspecs = [pl.BlockSpec((p.shape[0], p.shape[1], col_tile), lambda i: (0, 0, i))
                  for p in parts]
        steps = cols // col_tile
    return pl.pallas_call(
        body, name=name, grid=(steps,),
        in_specs=pspecs + [_const_spec(e.shape) for e in extra] + [spec, spec, spec],
        out_specs=[spec] * 4,
        out_shape=[jax.ShapeDtypeStruct((rows, cols), F32)] * 4,
        compiler_params=_params(48, 1),
    )(*parts, *extra, w, m, v)


def _sum_parts(parts, name):
    n, rows, cols = parts.shape

    def body(p_ref, o_ref):
        g = p_ref[0]
        for idx in range(1, n):
            g = g + p_ref[idx]
        o_ref[...] = g

    return pl.pallas_call(
        body, name=name, out_shape=jax.ShapeDtypeStruct((rows, cols), F32),
        in_specs=[_const_spec(parts.shape)], out_specs=_const_spec((rows, cols)), grid=(1,),
        compiler_params=_params(16, 1),
    )(parts)


VEC_NAMES = ("g_pre_mix", "b_forget", "g_q", "g_k", "g_sgu", "b_sgu", "b_spatial", "g_post_mix",
             "g_pre_ffn", "g_post_ffn")
VEC_ROWS = 16
LOSS_ROW = len(VEC_NAMES)


def _pack_vectors(d, loss_row):
    rows = []
    for k in VEC_NAMES:
        flat = d[k].reshape(1, -1).astype(F32)
        rows.append(jnp.pad(flat, ((0, 0), (0, 1024 - flat.shape[1]))))
    rows.append(loss_row)
    rows.append(jnp.zeros((VEC_ROWS - len(rows), 1024), F32))
    return jnp.concatenate(rows, axis=0)


def _adamw_vectors(grad_rows, ws, ms, vs):
    n = len(VEC_NAMES)
    bc1 = 1.0 - ADAM_B1 ** ADAM_STEP
    bc2 = 1.0 - ADAM_B2 ** ADAM_STEP

    def step(g, w, m, v):
        mn = ADAM_B1 * m + (1.0 - ADAM_B1) * g
        vn = ADAM_B2 * v + (1.0 - ADAM_B2) * (g * g)
        delta = -ADAM_LR * ((mn / bc1) / (jnp.sqrt(vn / bc2) + ADAM_EPS) + ADAM_WD * w)
        return g, delta, mn, vn

    def body(*refs):
        g_ref = refs[0]
        ins = [refs[1 + j * n:1 + (j + 1) * n] for j in range(3)]
        outs = [refs[1 + (3 + j) * n:1 + (4 + j) * n] for j in range(4)]
        for i in range(n):
            shape = ws[i].shape
            if len(shape) == 2:
                res = step(g_ref[i:i + 1, :shape[1]], *[r[i][...] for r in ins])
                for o, val in zip(outs, res):
                    o[i][...] = val
            else:
                for r in range(shape[1]):
                    res = step(g_ref[i:i + 1, r * shape[2]:(r + 1) * shape[2]],
                               *[q[i][0, r:r + 1, :] for q in ins])
                    for o, val in zip(outs, res):
                        o[i][0, r:r + 1, :] = val

    vmem = pl.BlockSpec(memory_space=pltpu.VMEM)
    flat = pl.pallas_call(
        body, name="adamw_vectors",
        in_specs=[vmem] * (1 + 3 * n), out_specs=[vmem] * (4 * n),
        out_shape=[jax.ShapeDtypeStruct(w.shape, F32) for _ in range(4) for w in ws],
    )(grad_rows, *ws, *ms, *vs)
    return [flat[j * n:(j + 1) * n] for j in range(4)]


def _cols_to_blocks(full, width):
    r = full.shape[0]
    return jnp.transpose(full.reshape(r, N_DEV, width), (1, 0, 2))


def _blocks_to_cols(blocks):
    n, r, width = blocks.shape
    return jnp.transpose(blocks, (1, 0, 2)).reshape(r, n * width)


def kernel(x, g_pre_mix, w_in, b_forget, g_q, g_k, g_sgu, b_sgu, w_spatial, b_spatial, w_branch_a, w_branch_b, w_out, g_post_mix, g_pre_ffn, w_ffn_in, w_ffn_down, g_post_ffn, loss_target, m_g_pre_mix, m_w_in, m_b_forget, m_g_q, m_g_k, m_g_sgu, m_b_sgu, m_w_spatial, m_b_spatial, m_w_branch_a, m_w_branch_b, m_w_out, m_g_post_mix, m_g_pre_ffn, m_w_ffn_in, m_w_ffn_down, m_g_post_ffn, v_g_pre_mix, v_w_in, v_b_forget, v_g_q, v_g_k, v_g_sgu, v_b_sgu, v_w_spatial, v_b_spatial, v_w_branch_a, v_w_branch_b, v_w_out, v_g_post_mix, v_g_pre_ffn, v_w_ffn_in, v_w_ffn_down, v_g_post_ffn):
    big_names = ("w_in", "w_branch_a", "w_branch_b", "w_out", "w_ffn_in", "w_ffn_down")
    weights = dict(g_pre_mix=g_pre_mix, w_in=w_in, b_forget=b_forget, g_q=g_q, g_k=g_k, g_sgu=g_sgu,
                   b_sgu=b_sgu, w_spatial=w_spatial, b_spatial=b_spatial, w_branch_a=w_branch_a,
                   w_branch_b=w_branch_b, w_out=w_out, g_post_mix=g_post_mix, g_pre_ffn=g_pre_ffn,
                   w_ffn_in=w_ffn_in, w_ffn_down=w_ffn_down, g_post_ffn=g_post_ffn)
    mom1 = dict(g_pre_mix=m_g_pre_mix, w_in=m_w_in, b_forget=m_b_forget, g_q=m_g_q, g_k=m_g_k,
                g_sgu=m_g_sgu, b_sgu=m_b_sgu, w_spatial=m_w_spatial, b_spatial=m_b_spatial,
                w_branch_a=m_w_branch_a, w_branch_b=m_w_branch_b, w_out=m_w_out,
                g_post_mix=m_g_post_mix, g_pre_ffn=m_g_pre_ffn, w_ffn_in=m_w_ffn_in,
                w_ffn_down=m_w_ffn_down, g_post_ffn=m_g_post_ffn)
    mom2 = dict(g_pre_mix=v_g_pre_mix, w_in=v_w_in, b_forget=v_b_forget, g_q=v_g_q, g_k=v_g_k,
                g_sgu=v_g_sgu, b_sgu=v_b_sgu, w_spatial=v_w_spatial, b_spatial=v_b_spatial,
                w_branch_a=v_w_branch_a, w_branch_b=v_w_branch_b, w_out=v_w_out,
                g_post_mix=v_g_post_mix, g_pre_ffn=v_g_pre_ffn, w_ffn_in=v_w_ffn_in,
                w_ffn_down=v_w_ffn_down, g_post_ffn=v_g_post_ffn)
    names = list(weights)
    shapes = {k: weights[k].shape for k in names}

    s_len = x.shape[1]
    xs = x.reshape(s_len, D_MODEL)
    tgt = loss_target.reshape(s_len, D_MODEL)

    transposed = ("w_in", "w_ffn_in")

    def local_view(a, k):
        return jnp.transpose(a[0]) if k in transposed else a[0]

    shards = {k: local_view(weights[k], k).astype(BF16) for k in big_names}

    x_pos, y_pos, c_pos = _mesh_pos()
    me = 4 * x_pos + 2 * y_pos + c_pos
    r_idx = jnp.arange(BLK)
    general = (jnp.asarray(BLK_AT, jnp.int32) - jnp.asarray(FRAME_START, jnp.int32))[me] + r_idx
    holder = jnp.where(r_idx < F_AT, BLK_AT[F_DEV] - FRAME_START[F_DEV] + r_idx,
                       jnp.where(r_idx < F_AT + HEADS, FRAME - F_AT + r_idx,
                                 BLK_AT[F_DEV] - FRAME_START[F_DEV] - HEADS + r_idx))
    frame_row = jnp.where(me == F_DEV, holder, general)
    in_frame = (frame_row[:, None] == jnp.arange(FRAME_ROWS)[None, :]).astype(BF16)
    my_frame = jnp.dot(in_frame.T, shards["w_in"], preferred_element_type=F32).astype(BF16)
    wcat = _gather_w_in(my_frame)
    wcat, later = lax.optimization_barrier(
        (wcat, [shards[k] for k in big_names if k != "w_in"]))
    shards.update(zip([k for k in big_names if k != "w_in"], later))
    (gat_mix, gat_ffn), gat_token = _exchange_start(
        [[shards["w_branch_a"], shards["w_branch_b"], shards["w_out"]],
         [shards["w_ffn_in"], shards["w_ffn_down"]]], "gather_start", gather=True)

    seg = np.arange(FOX_W) // HEAD_DIM
    bdiag = jnp.asarray(seg[:128, None] == seg[None, :128], BF16)
    tm = TOKEN_TILE
    lower = np.arange(tm)[None, :] <= np.arange(tm)[:, None]
    tril = jnp.asarray(lower, BF16)
    triu = jnp.asarray(lower.T, BF16)
    egrp = jnp.asarray(seg[:, None] == np.arange(128)[None, :], BF16)
    efold = jnp.asarray((np.arange(FOX_W) % HEAD_DIM)[:, None] == np.arange(128)[None, :], BF16)
    gq512 = jnp.tile(g_q.reshape(1, HEAD_DIM), (1, HEADS))
    gk512 = jnp.tile(g_k.reshape(1, HEAD_DIM), (1, HEADS))
    bfor = jnp.pad(b_forget.reshape(1, HEADS), ((0, 0), (0, 128 - HEADS)))
    pos = np.arange(WINDOW)
    wmask = (pos[None, :] // CHUNK) <= (pos[:, None] // CHUNK)
    wsm_f = jnp.where(jnp.asarray(wmask)[None], w_spatial[0], 0.0)
    wsm = wsm_f.astype(BF16)
    wsmt = jnp.transpose(wsm_f, (0, 2, 1)).astype(BF16)
    bsf = jnp.repeat(jnp.transpose(b_spatial[0]), HEAD_DIM, axis=1)
    wmask_f = jnp.asarray(wmask, F32)

    col = np.arange(SLAB_W)
    row128 = np.arange(128)

    def d_place(first, sign):
        parts = [(col[None, :] // 128 == row128[:, None]) & (col[None, :] % 128 == first + a)
                 for a in range(3)]
        return jnp.asarray(sign * np.concatenate(parts, axis=0).astype(np.float32), BF16)

    pdq, pdk = d_place(HEAD_DIM, 1.0), d_place(HEAD_DIM + 3, -1.0)
    ones_q = jnp.asarray((col % 128 >= HEAD_DIM + 3) & (col % 128 < HEAD_DIM + 6), F32)[None]
    ones_k = jnp.asarray((col % 128 >= HEAD_DIM) & (col % 128 < HEAD_DIM + 3), F32)[None]
    ecol = jnp.asarray((col[:, None] // 128 == row128[None, :])
                       & (col[:, None] % 128 == HEAD_DIM + 3), BF16)

    (h, qa, ka, kat, vs, vt, qraw, kraw, flog, uvpre, gpre) = _proj_fwd(
        xs, g_pre_mix + gat_token[0:1, 0:1], wcat, bdiag, gq512, gk512, bfor, tril, pdq, pdk,
        ones_q, ones_k)
    attn, attn_t, lse = _attn_fwd(qa, ka, vt)
    (own_a, own_b, own_out), (zone_a, zone_b, zone_out) = _exchange_wait(
        gat_mix, attn, "gather_wait_mix", gather=True)
    wa = _blocks_to_cols(_own_block(zone_a, own_a))
    wb = _blocks_to_cols(_own_block(zone_b, own_b))
    wout = _own_block(zone_out, own_out).reshape(D_MODEL, D_MODEL)
    sgu_t, ya, yb, merged_t, om, x1 = _mix_fwd(attn, uvpre, gpre, xs, wa, wb, wout, wsm, bsf,
                                           g_sgu, b_sgu, g_post_mix)
    (own_ffn, own_down), (zone_ffn, zone_down) = _exchange_wait(
        gat_ffn, x1, "gather_wait_ffn", gather=True)
    wffn = _own_block(zone_ffn, own_ffn).reshape(2 * D_FF, D_MODEL)
    wdown = _own_block(zone_down, own_down).reshape(D_FF, D_MODEL)
    (dx1, h2, act_t, dff, dgu_t, loss_acc, dg_post_ffn, dg_pre_ffn) = _ffn_fwd_bwd(
        x1, tgt, wffn, wdown, g_pre_ffn, g_post_ffn)

    dw_down = _dw_matmul(act_t, dff, D_FF // 4, "dw_down")
    dw_ffn = _dw_matmul(dgu_t, h2, 2 * D_FF // N_DEV, "dw_ffn_in")
    def own_of(parts):
        return [lax.dynamic_index_in_dim(p, me, 0, keepdims=False) for p in parts]

    parts_ffn = [dw_ffn.reshape(N_DEV, 2 * D_FF // N_DEV, D_MODEL),
                 dw_down.reshape(N_DEV, D_FF // N_DEV, D_MODEL)]
    mine_ffn = own_of(parts_ffn)
    (sct_ffn,), sct_ffn_token = _exchange_start([parts_ffn], "scatter_start_ffn", gather=False)

    (dom, dya, dyb, dgp, dot_, delta, duv, dws, dbs, dg_sgu, db_sgu, dg_post_mix) = _mix_bwd(
        dx1, om, ya, yb, gpre, uvpre, attn, wout, wa, wb, wsm, wsmt, bsf, g_sgu, b_sgu,
        g_post_mix + sct_ffn_token[0:1, 0:1], wmask_f, egrp)
    dw_out = _dw_matmul(merged_t, dom, 512, "dw_out")
    dw_a = _dw_matmul(attn_t, dya, 512, "dw_a")
    dw_b = _dw_matmul(sgu_t, dyb, 512, "dw_b")
    parts_mix = [_cols_to_blocks(dw_a, D_MODEL // N_DEV), _cols_to_blocks(dw_b, D_MODEL // N_DEV),
                 dw_out.reshape(N_DEV, D_MODEL // N_DEV, D_MODEL)]
    mine_mix = own_of(parts_mix)
    (sct_mix,), sct_mix_token = _exchange_start([parts_mix], "scatter_start_mix", gather=False)

    gk_all, dvt, gqt, col_sums = _attn_bwd(qa, ka, kat, vs, dot_, lse,
                                           delta + sct_mix_token[0, 0], ecol)
    dlogf = _rev_cumsum(col_sums, gqt, triu)
    dx, dproj_t, dgq, dgk, dbf, dg_pre_mix = _proj_bwd(
        gqt, gk_all, dvt, dlogf, flog, qraw, kraw, duv, dgp, xs, dx1, wcat, bdiag, gq512, gk512,
        g_pre_mix, efold)

    small_local = dict(
        g_pre_mix=dg_pre_mix, b_forget=dbf[:, :HEADS], g_q=dgq[0:1, :HEAD_DIM],
        g_k=dgk[0:1, :HEAD_DIM], g_sgu=dg_sgu, b_sgu=db_sgu, w_spatial=dws,
        b_spatial=jnp.transpose(dbs[:, :GROUPS]), g_post_mix=dg_post_mix, g_pre_ffn=dg_pre_ffn,
        g_post_ffn=dg_post_ffn)
    loss_row = jnp.pad(loss_acc[0:1, 0:1], ((0, 0), (0, 1023)))
    small_parts = [_pack_vectors(small_local, loss_row).reshape(N_DEV, VEC_ROWS // N_DEV, 1024),
                   dws]

    def with_own(zones, own_blocks):
        return [_own_block(z, b) for z, b in zip(zones, own_blocks)]

    mine_small = own_of(small_parts)
    (sct_small,), sct_small_token = _exchange_start([small_parts], "scatter_start_small",
                                                    gather=False)
    dw_cat = _dw_matmul(dproj_t, h, C_END // N_DEV, "dw_in", after=(sct_small_token,))
    recv_vec, recv_ws = with_own(
        _exchange_wait(sct_small, dw_cat, "scatter_wait_small", gather=False)[1], mine_small)
    small_sums = [_sum_parts(recv_vec, "sum_vectors"), _sum_parts(recv_ws, "sum_w_spatial")]
    (gat_small,), gat_small_token = _exchange_start([small_sums], "gather_start_small",
                                                    gather=True)
    pair_blocks, own_pair = _pair_sums(dw_cat, "pair_sums_in", gat_small_token)
    rs_in, rs_token = _chip_exchange_start(pair_blocks, "chip_exchange_start_in")

    recv_ffn, recv_down = with_own(
        _exchange_wait(sct_ffn, rs_token, "scatter_wait_ffn", gather=False)[1], mine_ffn)
    recv_a, recv_b, recv_out = with_own(
        _exchange_wait(sct_mix, recv_ffn, "scatter_wait_mix", gather=False)[1], mine_mix)
    received = [None, recv_a, recv_b, recv_out, recv_ffn, recv_down]

    grads, deltas, new_m, new_v = {}, {}, {}, {}
    row_tiles = {"w_in": None, "w_branch_a": 512, "w_branch_b": 512, "w_out": 128, "w_ffn_in": 176,
                 "w_ffn_down": 352}

    def update(k, parts):
        outs = _adamw(parts, local_view(weights[k], k), local_view(mom1[k], k),
                      local_view(mom2[k], k), row_tiles[k], "adamw_" + k,
                      col_tile=256 if k == "w_in" else None,
                      select=in_frame if k == "w_in" else None)
        if k in transposed:
            outs = [jnp.transpose(o) for o in outs]
        grads[k], deltas[k], new_m[k], new_v[k] = [o[None] for o in outs]
        return outs[0]

    last = None
    for idx, k in enumerate(big_names):
        if k != "w_in":
            last = update(k, received[idx])

    (own_vec, own_ws), (zone_vec, zone_ws) = _exchange_wait(gat_small, last, "gather_wait_small",
                                                            gather=True)
    vec_all = _own_block(zone_vec, own_vec).reshape(VEC_ROWS, 1024)
    ws_all = _own_block(zone_ws, own_ws).reshape(1, GROUPS * WINDOW, WINDOW)

    def rows_of(d):
        return d["w_spatial"].reshape(GROUPS * WINDOW, WINDOW)

    outs = _adamw(ws_all, rows_of(weights), rows_of(mom1), rows_of(mom2), GROUPS * WINDOW,
                  "adamw_w_spatial")
    for dst, o in zip((grads, deltas, new_m, new_v), outs):
        dst["w_spatial"] = o.reshape(shapes["w_spatial"])
    sg = outs[0]
    vec_outs = _adamw_vectors(vec_all, *[[d[k] for k in VEC_NAMES] for d in (weights, mom1, mom2)])
    for dst, group in zip((grads, deltas, new_m, new_v), vec_outs):
        dst.update(zip(VEC_NAMES, group))
    done = [d[k] for d in (grads, deltas, new_m, new_v) for k in names if k != "w_in"]
    arrived = _chip_exchange_wait(rs_in, done, "chip_exchange_wait_in")
    update("w_in", [own_pair[None], arrived])

    loss = vec_all[LOSS_ROW, 0]
    return (loss, dx.reshape(x.shape), *[grads[k] for k in names], *[deltas[k] for k in names],
            *[new_m[k] for k in names], *[new_v[k] for k in names])
```

```python
import functools
import math

import jax
import jax.numpy as jnp
import numpy as np
from jax import lax
from jax.experimental import pallas as pl
from jax.experimental.pallas import tpu as pltpu

F32 = jnp.float32
BF16 = jnp.bfloat16

D_MODEL = 1024
FOX_W = 512
HEADS = 8
HEAD_DIM = 64
SGU_W = 512
GROUPS = 8
WINDOW = 128
CHUNK = 64
D_FF = 2816
IN_COLS = 4616
EPS = 1e-6
N_DEV = 8
LOG2E = 1.4426950408889634
LN2 = 0.6931471805599453

C_Q, C_K, C_V, C_UV, C_G, C_F, C_END = 0, 512, 1024, 1536, 2560, 4608, 4736

ADAM_LR, ADAM_B1, ADAM_B2, ADAM_EPS, ADAM_WD, ADAM_STEP = 0.001, 0.9, 0.999, 1e-08, 0.01, 10

MIB = 1024 * 1024
TOKEN_TILE = 256
ATTN_TILE = 256
SLAB_W = HEADS * 128
QT_ROWS = 72

BLK = IN_COLS // N_DEV
F_LO = 3 * FOX_W
F_DEV = F_LO // BLK
F_AT = F_LO - F_DEV * BLK
BLK_AT = [BLK * j - (HEADS if BLK * j > F_LO else 0) for j in range(N_DEV)]
FRAME_START = [a // 16 * 16 for a in BLK_AT]
FRAME = 608
FRAME_ROWS = FRAME + 16


def _params(vmem_mib, n_axes):
    return pltpu.CompilerParams(
        dimension_semantics=("arbitrary",) * n_axes, vmem_limit_bytes=vmem_mib * MIB)


def _const_spec(shape):
    nd = len(shape)
    return pl.BlockSpec(shape, lambda *_: (0,) * nd)


def _row_spec(tm, cols):
    return pl.BlockSpec((tm, cols), lambda i: (i, 0))


def _tile_spec(rows, tm):
    return pl.BlockSpec((1, rows, tm), lambda i: (i, 0, 0))


def _split3_dot(x, e):
    x1 = x.astype(BF16)
    r1 = x - x1.astype(F32)
    x2 = r1.astype(BF16)
    x3 = (r1 - x2.astype(F32)).astype(BF16)
    dot = functools.partial(jnp.dot, preferred_element_type=F32)
    return dot(x1, e) + dot(x2, e) + dot(x3, e)


def _tri_dot(tri, x):
    x1 = x.astype(BF16)
    r1 = x - x1.astype(F32)
    x2 = r1.astype(BF16)
    x3 = (r1 - x2.astype(F32)).astype(BF16)
    dot = functools.partial(jnp.dot, preferred_element_type=F32)
    return dot(tri, x1) + dot(tri, x2) + dot(tri, x3)


def _seg_mean(sq, bd_ref):
    hi = sq.astype(BF16)
    bd = bd_ref[...]
    pairs = [jnp.dot(hi[:, p * 128:(p + 1) * 128], bd, preferred_element_type=F32)
             for p in range(HEADS // 2)]
    return jnp.concatenate(pairs, axis=1) * (1.0 / HEAD_DIM)


def _slabs_from_heads(t):
    lane = lax.broadcasted_iota(jnp.int32, (t.shape[0], 128), 1)
    low = lane < HEAD_DIM
    slabs = []
    for p in range(HEADS // 2):
        pair = t[:, p * 128:(p + 1) * 128]
        slabs.append(jnp.where(low, pair, 0.0))
        slabs.append(jnp.where(low, pltpu.roll(pair, HEAD_DIM, 1), 0.0))
    return jnp.concatenate(slabs, axis=1)


def _dot_nt(a, b):
    return lax.dot_general(a, b, (((1,), (1,)), ((), ())), preferred_element_type=F32)


def _dot_tn(a, b):
    return lax.dot_general(a, b, (((0,), (0,)), ((), ())), preferred_element_type=F32)


def _sigmoid(x):
    return 0.5 * jnp.tanh(0.5 * x) + 0.5


_GELU_C = math.sqrt(2.0 / math.pi)


def _gelu_and_grad(x):
    inner = _GELU_C * (x + 0.044715 * x * x * x)
    t = jnp.tanh(inner)
    y = 0.5 * x * (1.0 + t)
    dy = 0.5 * (1.0 + t) + 0.5 * x * (1.0 - t * t) * _GELU_C * (1.0 + 3.0 * 0.044715 * x * x)
    return y, dy


def _rms_bwd(xin, r, g, dy):
    dyg = dy * g
    return r * dyg - xin * (r * r * r) * jnp.mean(dyg * xin, axis=-1, keepdims=True)


def _mesh_pos():
    x, y, c = lax.axis_index("x"), lax.axis_index("y"), lax.axis_index("c")
    return x, y, c


def _peer(k):
    x, y, c = _mesh_pos()
    px = (1 - x) if (k >> 2) & 1 else x
    py = (1 - y) if (k >> 1) & 1 else y
    pc = (1 - c) if k & 1 else c
    return (px, py, pc), 4 * px + 2 * py + pc


def _frame_start(j):
    at = BLK * j - jnp.where(BLK * j > F_LO, HEADS, 0)
    return pl.multiple_of(at // 16 * 16, 16)


HALF_A = 320


def _gather_w_in(frame):
    pieces = (slice(0, HALF_A), slice(HALF_A, FRAME_ROWS))

    def body(x_ref, out_ref, zone, send_sems, recv_sems, local_sem):
        x, y, c = _mesh_pos()
        me, sibling = (x, y, c), (x, y, 1 - c)
        nbr_x, nbr_y, across = (1 - x, y, c), (x, 1 - y, c), (1 - x, 1 - y, c)

        def index(pos):
            return 4 * pos[0] + 2 * pos[1] + pos[2]

        def copy(k, block, piece, to, src=None):
            rows = pieces[piece]
            return pltpu.make_async_remote_copy(
                src_ref=(zone.at[index(block), rows] if src is None else src.at[rows]),
                dst_ref=zone.at[index(block), rows],
                send_sem=send_sems.at[k], recv_sem=recv_sems.at[k],
                device_id=to, device_id_type=pl.DeviceIdType.MESH)

        def add(block):
            j = index(block)
            rows = pl.ds(_frame_start(j), FRAME)
            out_ref[rows, :] = (out_ref[rows, :].astype(F32)
                                + zone[j, :FRAME, :].astype(F32)).astype(BF16)
            tail = slice(C_F, C_F + FRAME_ROWS - FRAME)
            forget = zone[j, FRAME:, :].astype(F32) * (j == F_DEV).astype(F32)
            out_ref[tail, :] = (out_ref[tail, :].astype(F32) + forget).astype(BF16)

        mine = pltpu.make_async_copy(x_ref, zone.at[index(me)], local_sem)
        mine.start()
        first = [copy(1, me, 0, nbr_x, src=x_ref), copy(3, me, 1, nbr_y, src=x_ref),
                 copy(2, me, 1, nbr_x, src=x_ref), copy(4, me, 0, nbr_y, src=x_ref)]
        own_to_sibling = pltpu.make_async_remote_copy(
            src_ref=x_ref, dst_ref=zone.at[index(me)], send_sem=send_sems.at[0],
            recv_sem=recv_sems.at[0], device_id=sibling, device_id_type=pl.DeviceIdType.MESH)
        for cp in first:
            cp.start()
        own_to_sibling.start()
        out_ref[...] = jnp.zeros_like(out_ref)
        mine.wait()
        add(me)

        sent = []

        def landed(k, block, piece, forward=None):
            copy(k, block, piece, me).wait_recv()
            if forward is not None:
                cp = copy(*forward)
                cp.start()
                sent.append(cp)
            cp = copy(6 + k, block, piece, sibling)
            cp.start()
            sent.append(cp)

        landed(1, nbr_x, 0, forward=(5, nbr_x, 0, nbr_y))
        landed(3, nbr_y, 1, forward=(6, nbr_y, 1, nbr_x))
        landed(2, nbr_x, 1)
        add(nbr_x)
        landed(4, nbr_y, 0)
        add(nbr_y)
        landed(5, across, 0)
        landed(6, across, 1)
        add(across)
        pltpu.make_async_remote_copy(
            src_ref=x_ref, dst_ref=zone.at[index(sibling)], send_sem=send_sems.at[0],
            recv_sem=recv_sems.at[0], device_id=sibling,
            device_id_type=pl.DeviceIdType.MESH).wait_recv()
        add(sibling)
        for k, block in ((1, nbr_x), (2, nbr_x), (3, nbr_y), (4, nbr_y), (5, across), (6, across)):
            their = (block[0], block[1], 1 - c)
            piece = {1: 0, 2: 1, 3: 1, 4: 0, 5: 0, 6: 1}[k]
            copy(6 + k, their, piece, me).wait_recv()
            if k in (2, 4, 6):
                add(their)
        for cp in first + sent:
            cp.wait_send()
        own_to_sibling.wait_send()

    return pl.pallas_call(
        body, name="gather_w_in", out_shape=jax.ShapeDtypeStruct((C_END, frame.shape[1]), BF16),
        in_specs=[pl.BlockSpec(memory_space=pl.ANY)],
        out_specs=pl.BlockSpec(memory_space=pltpu.VMEM),
        scratch_shapes=[pltpu.VMEM((N_DEV,) + frame.shape, BF16),
                        pltpu.SemaphoreType.DMA((13,)), pltpu.SemaphoreType.DMA((13,)),
                        pltpu.SemaphoreType.DMA],
        compiler_params=pltpu.CompilerParams(vmem_limit_bytes=40 * MIB),
    )(frame)


def _chip_peer(k):
    x, y, c = _mesh_pos()
    px = (1 - x) if (k >> 1) & 1 else x
    py = (1 - y) if k & 1 else y
    return (px, py, c), 2 * px + py


def _pair_sums(dw_cat, name, after):
    rows, cols = FRAME_ROWS, dw_cat.shape[1]
    n_chips = N_DEV // 2

    def pieces(p_ref, j):
        return (p_ref.at[pl.ds(_frame_start(j), FRAME)], p_ref.at[pl.ds(C_F, FRAME_ROWS - FRAME)])

    def body(p_ref, after_ref, send_ref, own_ref, mine_buf, sib_buf, send_sems, recv_sems,
             local_sems):
        x, y, c = _mesh_pos()
        sibling = (x, y, 1 - c)
        copies, local = [], []
        for q in range(n_chips):
            for part, (lo, hi) in enumerate(((0, FRAME), (FRAME, FRAME_ROWS))):
                cp = pltpu.make_async_remote_copy(
                    src_ref=pieces(p_ref, 2 * q + (1 - c))[part], dst_ref=sib_buf.at[q, lo:hi],
                    send_sem=send_sems.at[2 * q + part], recv_sem=recv_sems.at[2 * q + part],
                    device_id=sibling, device_id_type=pl.DeviceIdType.MESH)
                cp.start()
                copies.append(cp)
                lc = pltpu.make_async_copy(pieces(p_ref, 2 * q + c)[part], mine_buf.at[q, lo:hi],
                                           local_sems.at[2 * q + part])
                lc.start()
                local.append(lc)
        for lc in local:
            lc.wait()
        for cp in copies:
            cp.wait_recv()
        for k in range(1, n_chips):
            _, q = _chip_peer(k)
            send_ref[k - 1] = (mine_buf[q].astype(F32) + sib_buf[q].astype(F32)).astype(BF16)
        my_chip = 2 * x + y
        own_ref[...] = mine_buf[my_chip].astype(F32) + sib_buf[my_chip].astype(F32)
        for cp in copies:
            cp.wait_send()

    vmem = pl.BlockSpec(memory_space=pltpu.VMEM)
    return pl.pallas_call(
        body, name=name,
        out_shape=[jax.ShapeDtypeStruct((n_chips - 1, rows, cols), BF16),
                   jax.ShapeDtypeStruct((rows, cols), F32)],
        in_specs=[pl.BlockSpec(memory_space=pl.ANY)] * 2, out_specs=[vmem, vmem],
        scratch_shapes=[pltpu.VMEM((n_chips, rows, cols), BF16),
                        pltpu.VMEM((n_chips, rows, cols), BF16),
                        pltpu.SemaphoreType.DMA((2 * n_chips,)),
                        pltpu.SemaphoreType.DMA((2 * n_chips,)),
                        pltpu.SemaphoreType.DMA((2 * n_chips,))],
        compiler_params=pltpu.CompilerParams(vmem_limit_bytes=40 * MIB),
    )(dw_cat, after)


def _chip_copy(src_ref, land_ref, send_sem, recv_sem, k):
    peer, _ = _chip_peer(k)
    return pltpu.make_async_remote_copy(
        src_ref=src_ref.at[k - 1], dst_ref=land_ref.at[k - 1], send_sem=send_sem, recv_sem=recv_sem,
        device_id=peer, device_id_type=pl.DeviceIdType.MESH)


def _chip_exchange_start(blocks, name):
    hbm = pl.BlockSpec(memory_space=pltpu.HBM)
    sem = pl.BlockSpec(memory_space=pltpu.SEMAPHORE)
    n_peers = blocks.shape[0]

    def body(src_ref, zone_ref, send_sems, recv_sems, src_thru, zone_thru, token):
        for k in range(1, n_peers + 1):
            _chip_copy(src_ref, zone_ref, send_sems.at[k - 1], recv_sems.at[k - 1], k).start()
        token[...] = jnp.zeros_like(token)

    outs = pl.pallas_call(
        body, name=name, in_specs=[hbm, hbm],
        out_shape=[pltpu.SemaphoreType.DMA((n_peers,)), pltpu.SemaphoreType.DMA((n_peers,)),
                   pltpu.HBM(blocks.shape, blocks.dtype), pltpu.HBM(blocks.shape, blocks.dtype),
                   jax.ShapeDtypeStruct((8, 128), F32)],
        out_specs=[sem, sem, hbm, hbm, pl.BlockSpec(memory_space=pltpu.VMEM)],
        input_output_aliases={0: 2, 1: 3},
        compiler_params=pltpu.CompilerParams(
            has_side_effects=pltpu.SideEffectType.DATAFLOW_SIDE_EFFECTING),
    )(pltpu.with_memory_space_constraint(blocks, pltpu.HBM),
      pltpu.with_memory_space_constraint(lax.empty(blocks.shape, blocks.dtype), pltpu.HBM))
    return outs[:4], outs[4]


def _chip_exchange_wait(handle, after, name):
    send_sems, recv_sems, src, zone = handle
    hbm = pl.BlockSpec(memory_space=pltpu.HBM)
    sem = pl.BlockSpec(memory_space=pltpu.SEMAPHORE)

    def body(src_ref, zone_ref, ssem, rsem, after_ref, src_out, zone_out):
        for k in range(1, src.shape[0] + 1):
            cp = _chip_copy(src_ref, zone_ref, ssem.at[k - 1], rsem.at[k - 1], k)
            cp.wait_send()
            cp.wait_recv()

    outs = pl.pallas_call(
        body, name=name,
        in_specs=[hbm, hbm, sem, sem, pl.BlockSpec(memory_space=pl.ANY)],
        out_shape=[pltpu.HBM(src.shape, src.dtype), pltpu.HBM(zone.shape, zone.dtype)],
        out_specs=[hbm, hbm], input_output_aliases={0: 0, 1: 1},
        compiler_params=pltpu.CompilerParams(
            has_side_effects=pltpu.SideEffectType.DATAFLOW_SIDE_EFFECTING),
    )(src, zone, send_sems, recv_sems, after)
    return outs[1]


def _remote_copy(gather, src_ref, land_ref, send_sem, recv_sem, k, receive_side):
    x, y, c = _mesh_pos()
    me = 4 * x + 2 * y + c
    peer, pidx = _peer(k)
    return pltpu.make_async_remote_copy(
        src_ref=src_ref if gather else src_ref.at[pidx],
        dst_ref=land_ref.at[pidx if receive_side else me],
        send_sem=send_sem, recv_sem=recv_sem,
        device_id=peer, device_id_type=pl.DeviceIdType.MESH)


def _exchange_start(groups, name, gather):
    arrs = [a for g in groups for a in g]
    n, n_groups = len(arrs), len(groups)
    lands = [jax.ShapeDtypeStruct(((N_DEV,) + a.shape) if gather else a.shape, a.dtype)
             for a in arrs]

    def body(*refs):
        srcs, zones = refs[:n], refs[n:2 * n]
        sems = refs[2 * n:2 * n + 2 * n_groups]
        token = refs[-1]
        a = 0
        for gi, g in enumerate(groups):
            send_sems, recv_sems = sems[2 * gi], sems[2 * gi + 1]
            for k in range(1, N_DEV):
                for ai in range(len(g)):
                    slot = ai * (N_DEV - 1) + k - 1
                    _remote_copy(gather, srcs[a + ai], zones[a + ai], send_sems.at[slot],
                                 recv_sems.at[slot], k, False).start()
            a += len(g)
        token[...] = jnp.zeros_like(token)

    hbm = pl.BlockSpec(memory_space=pltpu.HBM)
    sem = pl.BlockSpec(memory_space=pltpu.SEMAPHORE)
    sem_shapes = []
    for g in groups:
        sem_shapes += [pltpu.SemaphoreType.DMA((len(g) * (N_DEV - 1),))] * 2
    outs = pl.pallas_call(
        body, name=name,
        in_specs=[hbm] * (2 * n),
        out_shape=sem_shapes + [pltpu.HBM(a.shape, a.dtype) for a in arrs]
        + [pltpu.HBM(z.shape, z.dtype) for z in lands] + [jax.ShapeDtypeStruct((8, 128), F32)],
        out_specs=[sem] * (2 * n_groups) + [hbm] * (2 * n)
        + [pl.BlockSpec(memory_space=pltpu.VMEM)],
        input_output_aliases={i: 2 * n_groups + i for i in range(2 * n)},
        compiler_params=pltpu.CompilerParams(
            has_side_effects=pltpu.SideEffectType.DATAFLOW_SIDE_EFFECTING),
    )(*[pltpu.with_memory_space_constraint(a, pltpu.HBM) for a in arrs],
      *[pltpu.with_memory_space_constraint(lax.empty(z.shape, z.dtype), pltpu.HBM) for z in lands])
    sems = outs[:2 * n_groups]
    thru = outs[2 * n_groups:2 * n_groups + n]
    zones = outs[2 * n_groups + n:2 * n_groups + 2 * n]
    handles, a = [], 0
    for gi, g in enumerate(groups):
        handles.append((sems[2 * gi], sems[2 * gi + 1], thru[a:a + len(g)], zones[a:a + len(g)]))
        a += len(g)
    return handles, outs[-1]


def _exchange_wait(handle, after, name, gather):
    send_sems, recv_sems, thru, zones = handle
    n = len(thru)

    def body(*refs):
        srcs, lands = refs[:n], refs[n:2 * n]
        ssem, rsem = refs[2 * n], refs[2 * n + 1]
        for k in range(1, N_DEV):
            for ai in range(n):
                slot = ai * (N_DEV - 1) + k - 1
                cp = _remote_copy(gather, srcs[ai], lands[ai], ssem.at[slot], rsem.at[slot], k, True)
                cp.wait_send()
                cp.wait_recv()

    hbm = pl.BlockSpec(memory_space=pltpu.HBM)
    sem = pl.BlockSpec(memory_space=pltpu.SEMAPHORE)
    outs = pl.pallas_call(
        body, name=name,
        in_specs=[hbm] * (2 * n) + [sem, sem, pl.BlockSpec(memory_space=pl.ANY)],
        out_shape=[pltpu.HBM(a.shape, a.dtype) for a in thru]
        + [pltpu.HBM(z.shape, z.dtype) for z in zones],
        out_specs=[hbm] * (2 * n),
        input_output_aliases={i: i for i in range(2 * n)},
        compiler_params=pltpu.CompilerParams(
            has_side_effects=pltpu.SideEffectType.DATAFLOW_SIDE_EFFECTING),
    )(*thru, *zones, send_sems, recv_sems, after)
    return outs[:n], outs[n:]


def _own_block(zone, block):
    x, y, c = _mesh_pos()
    me = 4 * x + 2 * y + c
    return lax.dynamic_update_slice_in_dim(zone, block[None], me, axis=0)


def _proj_fwd(x, g1, wcat, bdiag, gq, gk, bfor, tri, pdq, pdk, ones_q, ones_k):
    s_len = x.shape[0]
    tm = TOKEN_TILE
    nt = s_len // tm

    def body(x_ref, g1_ref, w_ref, bd_ref, gq_ref, gk_ref, bf_ref, tri_ref, pdq_ref,
             pdk_ref, oq_ref, ok_ref,
             h_ref, qa_ref, ka_ref, kat_ref, vs_ref, vt_ref, qr_ref, kr_ref, flog_ref, uv_ref,
             gp_ref, carry):
        @pl.when(pl.program_id(0) == 0)
        def _():
            carry[...] = jnp.zeros_like(carry)

        xf = x_ref[...]
        r = lax.rsqrt(jnp.mean(xf * xf, axis=-1, keepdims=True) + EPS)
        h = (xf * r * g1_ref[...]).astype(BF16)
        h_ref[...] = h
        dot = functools.partial(jnp.dot, preferred_element_type=F32)

        def proj(lo, hi):
            return _dot_nt(h, w_ref[lo:hi, :])

        flog = proj(C_F, C_END) + bf_ref[...]
        flog_ref[...] = flog
        lane = lax.broadcasted_iota(jnp.int32, flog.shape, 1)
        logf = jnp.minimum(flog, 0.0) - jnp.log(1.0 + jnp.exp(-jnp.abs(flog)))
        logf = jnp.where(lane < HEADS, logf, 0.0)
        dcum = _tri_dot(tri_ref[...], logf) + carry[...]
        carry[...] = dcum[tm - 1:tm, :]
        d2 = dcum * LOG2E
        d2a = d2.astype(BF16)
        rem = d2 - d2a.astype(F32)
        d2b = rem.astype(BF16)
        d2c = (rem - d2b.astype(F32)).astype(BF16)

        q = proj(C_Q, C_K)
        qr_ref[...] = q.astype(BF16)
        rq = lax.rsqrt(_seg_mean(q * q, bd_ref) + EPS)
        qn = q * rq * (gq_ref[...] * (HEAD_DIM ** -0.5 * LOG2E))
        d_parts = (d2a.astype(F32) + pltpu.roll(d2b.astype(F32), HEADS, 1)
                   + pltpu.roll(d2c.astype(F32), 2 * HEADS, 1)).astype(BF16)
        qa = _slabs_from_heads(qn) + dot(d_parts, pdq_ref[...]) + oq_ref[...]
        qa_ref[...] = qa.astype(BF16)

        k = proj(C_K, C_V)
        kr_ref[...] = k.astype(BF16)
        rk = lax.rsqrt(_seg_mean(k * k, bd_ref) + EPS)
        kn = k * rk * gk_ref[...]
        ka = _slabs_from_heads(kn) + dot(d_parts, pdk_ref[...]) + ok_ref[...]
        ka_ref[...] = ka.astype(BF16)
        kat_ref[0] = ka.T.astype(BF16)

        v = proj(C_V, C_UV)
        vs_ref[...] = _slabs_from_heads(v).astype(BF16)
        vt_ref[0] = v.T.astype(BF16)
        uv_ref[...] = proj(C_UV, C_G).astype(BF16)
        gp_ref[...] = proj(C_G, C_F).astype(BF16)

    outs = [((s_len, D_MODEL), BF16, _row_spec(tm, D_MODEL)),
            ((s_len, SLAB_W), BF16, _row_spec(tm, SLAB_W)),
            ((s_len, SLAB_W), BF16, _row_spec(tm, SLAB_W)),
            ((nt, SLAB_W, tm), BF16, _tile_spec(SLAB_W, tm)),
            ((s_len, SLAB_W), BF16, _row_spec(tm, SLAB_W)),
            ((nt, FOX_W, tm), BF16, _tile_spec(FOX_W, tm)),
            ((s_len, FOX_W), BF16, _row_spec(tm, FOX_W)),
            ((s_len, FOX_W), BF16, _row_spec(tm, FOX_W)),
            ((s_len, 128), F32, _row_spec(tm, 128)),
            ((s_len, 2 * SGU_W), BF16, _row_spec(tm, 2 * SGU_W)),
            ((s_len, 2 * D_MODEL), BF16, _row_spec(tm, 2 * D_MODEL))]
    return pl.pallas_call(
        body, name="proj_fwd", grid=(nt,),
        in_specs=[_row_spec(tm, D_MODEL), _const_spec((1, D_MODEL)), _const_spec(wcat.shape),
                  _const_spec(bdiag.shape), _const_spec((1, FOX_W)), _const_spec((1, FOX_W)),
                  _const_spec((1, 128)), _const_spec((tm, tm)), _const_spec(pdq.shape), _const_spec(pdk.shape), _const_spec(ones_q.shape),
                  _const_spec(ones_k.shape)],
        out_specs=[o[2] for o in outs],
        out_shape=[jax.ShapeDtypeStruct(o[0], o[1]) for o in outs],
        scratch_shapes=[pltpu.VMEM((1, 128), F32)],
        compiler_params=_params(56, 1),
    )(x, g1, wcat, bdiag, gq, gk, bfor, tri, pdq, pdk, ones_q, ones_k)


def _attn_fwd(qa, ka, vt):
    s_len = qa.shape[0]
    t = ATTN_TILE
    nb = s_len // t

    def body(q_ref, k_ref, vt_ref, o_ref, ot_ref, lse_ref, m_sc, l_sc, acc_sc, s_sc, mcur_sc,
             alpha_sc):
        i = pl.program_id(0)
        m_sc[...] = jnp.full_like(m_sc, -jnp.inf)
        l_sc[...] = jnp.zeros_like(l_sc)
        acc_sc[...] = jnp.zeros_like(acc_sc)

        def logits(j, slot, masked):
            krows = pl.ds(pl.multiple_of(j * t, t), t)
            if masked:
                keep = (lax.broadcasted_iota(jnp.int32, (t, t), 0)
                        <= lax.broadcasted_iota(jnp.int32, (t, t), 1))
            for hd in range(HEADS):
                sl = slice(hd * 128, (hd + 1) * 128)
                st = _dot_nt(k_ref[krows, sl], q_ref[:, sl])
                if masked:
                    st = jnp.where(keep, st, -jnp.inf)
                s_sc[slot, hd] = st
                m_prev = m_sc[hd:hd + 1, :]
                m_new = jnp.maximum(m_prev, jnp.max(st, axis=0, keepdims=True))
                alpha_sc[slot, hd:hd + 1, :] = jnp.exp2(m_prev - m_new)
                mcur_sc[slot, hd:hd + 1, :] = m_new
                m_sc[hd:hd + 1, :] = m_new

        def accumulate(j, slot):
            for hd in range(HEADS):
                hr = slice(hd * HEAD_DIM, (hd + 1) * HEAD_DIM)
                alpha = alpha_sc[slot, hd:hd + 1, :]
                pt = jnp.exp2(s_sc[slot, hd] - mcur_sc[slot, hd:hd + 1, :])
                l_sc[hd:hd + 1, :] = alpha * l_sc[hd:hd + 1, :] + jnp.sum(pt, axis=0, keepdims=True)
                acc_sc[hr, :] = alpha * acc_sc[hr, :] + jnp.dot(
                    vt_ref[j, hr, :], pt.astype(BF16), preferred_element_type=F32)

        @pl.when(i == 0)
        def _():
            logits(0, 0, True)
            accumulate(0, 0)

        pairs = (i - 1) // 2

        @pl.when(i > 0)
        def _():
            logits(0, 0, False)

            def two_blocks(p, carry):
                logits(2 * p + 1, 1, False)
                accumulate(2 * p, 0)
                logits(2 * p + 2, 0, False)
                accumulate(2 * p + 1, 1)
                return carry

            lax.fori_loop(0, pairs, two_blocks, 0)

        @pl.when((i > 0) & (i - 2 * pairs == 1))
        def _():
            logits(i, 1, True)
            accumulate(i - 1, 0)
            accumulate(i, 1)

        @pl.when((i > 0) & (i - 2 * pairs == 2))
        def _():
            logits(i - 1, 1, False)
            accumulate(i - 2, 0)
            logits(i, 0, True)
            accumulate(i - 1, 1)
            accumulate(i, 0)

        for hd in range(HEADS):
            hr = slice(hd * HEAD_DIM, (hd + 1) * HEAD_DIM)
            l = l_sc[hd:hd + 1, :]
            acc_sc[hr, :] = acc_sc[hr, :] / l
            lse_ref[0, hd:hd + 1, :] = m_sc[hd:hd + 1, :] + jnp.log2(l)
        o_ref[...] = acc_sc[...].T.astype(BF16)
        ot_ref[...] = acc_sc[...].astype(BF16)

    return pl.pallas_call(
        body, name="attn_fwd", grid=(nb,),
        in_specs=[_row_spec(t, SLAB_W), _const_spec(ka.shape), _const_spec(vt.shape)],
        out_specs=[_row_spec(t, FOX_W), pl.BlockSpec((FOX_W, t), lambda i: (0, i)),
                   _tile_spec(HEADS, t)],
        out_shape=[jax.ShapeDtypeStruct((s_len, FOX_W), BF16),
                   jax.ShapeDtypeStruct((FOX_W, s_len), BF16),
                   jax.ShapeDtypeStruct((nb, HEADS, t), F32)],
        scratch_shapes=[pltpu.VMEM((HEADS, t), F32), pltpu.VMEM((HEADS, t), F32),
                        pltpu.VMEM((FOX_W, t), F32), pltpu.VMEM((2, HEADS, t, t), F32),
                        pltpu.VMEM((2, HEADS, t), F32), pltpu.VMEM((2, HEADS, t), F32)],
        compiler_params=_params(48, 1),
    )(qa, ka, vt)


def _sgu_mix(vn, ws_ref):
    tm = vn.shape[0]
    lane = lax.broadcasted_iota(jnp.int32, (WINDOW, 128), 1)
    low = lane < HEAD_DIM
    wins = []
    for w in range(tm // WINDOW):
        slabs = []
        for p in range(GROUPS // 2):
            v2 = vn[w * WINDOW:(w + 1) * WINDOW, p * 128:(p + 1) * 128]
            lo = jnp.where(low, v2, 0.0).astype(BF16)
            hi = jnp.where(low, 0.0, v2).astype(BF16)
            slabs.append(jnp.dot(ws_ref[2 * p], lo, preferred_element_type=F32)
                         + jnp.dot(ws_ref[2 * p + 1], hi, preferred_element_type=F32))
        wins.append(jnp.concatenate(slabs, axis=1))
    return jnp.concatenate(wins, axis=0) if len(wins) > 1 else wins[0]


def _layernorm_fwd(vv, g, b):
    mu = jnp.mean(vv, axis=-1, keepdims=True)
    xc = vv - mu
    r = lax.rsqrt(jnp.mean(xc * xc, axis=-1, keepdims=True) + EPS)
    xh = xc * r
    return xh * g + b, xh, r


def _mix_fwd(attn, uvpre, gpre, x, wa, wb, wout, wsm, bsf, gsgu, bsgu, gpost):
    s_len = x.shape[0]
    tm = TOKEN_TILE

    def body(o_ref, uv_ref, gp_ref, x_ref, wa_ref, wb_ref, wo_ref, ws_ref, bs_ref, gs_ref, bsg_ref,
             gpost_ref, sgut_ref, ya_ref, yb_ref, mgt_ref, om_ref, x1_ref):
        uvp = uv_ref[...].astype(F32)
        uv, _ = _gelu_and_grad(uvp)
        u, vv = uv[:, :SGU_W], uv[:, SGU_W:]
        vn, _, _ = _layernorm_fwd(vv, gs_ref[...], bsg_ref[...])
        bias = bs_ref[...]
        if tm > WINDOW:
            bias = jnp.concatenate([bias] * (tm // WINDOW), axis=0)
        mixed = _sgu_mix(vn, ws_ref) + bias
        sgu_f = u * mixed
        sgu = sgu_f.astype(BF16)
        sgut_ref[...] = sgu_f.T.astype(BF16)
        ya = jnp.dot(o_ref[...], wa_ref[...], preferred_element_type=F32)
        yb = jnp.dot(sgu, wb_ref[...], preferred_element_type=F32)
        ya_ref[...] = ya.astype(BF16)
        yb_ref[...] = yb.astype(BF16)
        gates = _sigmoid(gp_ref[...].astype(F32))
        merged_f = gates[:, :D_MODEL] * ya + gates[:, D_MODEL:] * yb
        merged = merged_f.astype(BF16)
        mgt_ref[...] = merged_f.T.astype(BF16)
        om = jnp.dot(merged, wo_ref[...], preferred_element_type=F32)
        om_ref[...] = om
        r = lax.rsqrt(jnp.mean(om * om, axis=-1, keepdims=True) + EPS)
        x1_ref[...] = x_ref[...] + om * r * gpost_ref[...]

    def t_out(rows):
        return ((rows, s_len), BF16, pl.BlockSpec((rows, tm), lambda i: (0, i)))

    def r_out(cols, dt):
        return ((s_len, cols), dt, _row_spec(tm, cols))

    outs = [t_out(SGU_W), r_out(D_MODEL, BF16), r_out(D_MODEL, BF16), t_out(D_MODEL),
            r_out(D_MODEL, F32), r_out(D_MODEL, F32)]
    return pl.pallas_call(
        body, name="mix_fwd", grid=(s_len // tm,),
        in_specs=[_row_spec(tm, FOX_W), _row_spec(tm, 2 * SGU_W), _row_spec(tm, 2 * D_MODEL),
                  _row_spec(tm, D_MODEL), _const_spec(wa.shape), _const_spec(wb.shape),
                  _const_spec(wout.shape), _const_spec(wsm.shape), _const_spec(bsf.shape),
                  _const_spec((1, SGU_W)), _const_spec((1, SGU_W)), _const_spec((1, D_MODEL))],
        out_specs=[o[2] for o in outs],
        out_shape=[jax.ShapeDtypeStruct(o[0], o[1]) for o in outs],
        compiler_params=_params(48, 1),
    )(attn, uvpre, gpre, x, wa, wb, wout, wsm, bsf, gsgu, bsgu, gpost)


def _ffn_fwd_bwd(x1, tgt, wffn, wdown, gpre, gpost):
    s_len = x1.shape[0]
    tm = TOKEN_TILE

    def body(x1_ref, t_ref, wi_ref, wd_ref, gpre_ref, gpost_ref,
             dx1_ref, h2_ref, actt_ref, dff_ref, dgut_ref, loss_ref, dgpost_ref, dgpre_ref):
        @pl.when(pl.program_id(0) == 0)
        def _():
            loss_ref[...] = jnp.zeros_like(loss_ref)
            dgpost_ref[...] = jnp.zeros_like(dgpost_ref)
            dgpre_ref[...] = jnp.zeros_like(dgpre_ref)

        x1v = x1_ref[...]
        r2 = lax.rsqrt(jnp.mean(x1v * x1v, axis=-1, keepdims=True) + EPS)
        gpre_v = gpre_ref[...]
        h2 = (x1v * r2 * gpre_v).astype(BF16)
        h2_ref[...] = h2
        gg = _dot_nt(h2, wi_ref[:D_FF, :])
        uu = _dot_nt(h2, wi_ref[D_FF:, :])
        sg = _sigmoid(gg)
        silu = gg * sg
        act_f = silu * uu
        act = act_f.astype(BF16)
        actt_ref[...] = act_f.T.astype(BF16)
        ff = jnp.dot(act, wd_ref[...], preferred_element_type=F32)
        r3 = lax.rsqrt(jnp.mean(ff * ff, axis=-1, keepdims=True) + EPS)
        gpost_v = gpost_ref[...]
        y = x1v + ff * r3 * gpost_v
        err = y - t_ref[...]
        loss_ref[...] += jnp.sum(err * err) * (0.5 / D_MODEL)
        dy = err * (1.0 / D_MODEL)
        dgpost_ref[...] += jnp.sum(dy * ff * r3, axis=0, keepdims=True)
        dff = _rms_bwd(ff, r3, gpost_v, dy).astype(BF16)
        dff_ref[...] = dff
        dact = _dot_nt(dff, wd_ref[...])
        dgg_f = dact * uu * (sg * (1.0 + gg * (1.0 - sg)))
        duu_f = dact * silu
        dgg = dgg_f.astype(BF16)
        duu = duu_f.astype(BF16)
        dgut_ref[:D_FF, :] = dgg_f.T.astype(BF16)
        dgut_ref[D_FF:, :] = duu_f.T.astype(BF16)
        dh2 = (jnp.dot(dgg, wi_ref[:D_FF, :], preferred_element_type=F32)
               + jnp.dot(duu, wi_ref[D_FF:, :], preferred_element_type=F32))
        dgpre_ref[...] += jnp.sum(dh2 * x1v * r2, axis=0, keepdims=True)
        dx1_ref[...] = dy + _rms_bwd(x1v, r2, gpre_v, dh2)

    outs = [((s_len, D_MODEL), F32, _row_spec(tm, D_MODEL)),
            ((s_len, D_MODEL), BF16, _row_spec(tm, D_MODEL)),
            ((D_FF, s_len), BF16, pl.BlockSpec((D_FF, tm), lambda i: (0, i))),
            ((s_len, D_MODEL), BF16, _row_spec(tm, D_MODEL)),
            ((2 * D_FF, s_len), BF16, pl.BlockSpec((2 * D_FF, tm), lambda i: (0, i))),
            ((1, 128), F32, _const_spec((1, 128))),
            ((1, D_MODEL), F32, _const_spec((1, D_MODEL))),
            ((1, D_MODEL), F32, _const_spec((1, D_MODEL)))]
    return pl.pallas_call(
        body, name="ffn_fwd_bwd", grid=(s_len // tm,),
        in_specs=[_row_spec(tm, D_MODEL), _row_spec(tm, D_MODEL), _const_spec(wffn.shape),
                  _const_spec(wdown.shape), _const_spec((1, D_MODEL)), _const_spec((1, D_MODEL))],
        out_specs=[o[2] for o in outs],
        out_shape=[jax.ShapeDtypeStruct(o[0], o[1]) for o in outs],
        compiler_params=_params(60, 1),
    )(x1, tgt, wffn, wdown, gpre, gpost)


def _mix_bwd(dx1, om, ya, yb, gpre, uvpre, attn, wout, wa, wb, wsm, wsmt, bsf, gsgu, bsgu, gpost,
             wmask, egrp):
    s_len = dx1.shape[0]
    tm = TOKEN_TILE
    nw = tm // WINDOW
    nt = s_len // tm

    def body(dx1_ref, om_ref, ya_ref, yb_ref, gp_ref, uv_ref, o_ref, wo_ref, wa_ref, wb_ref, ws_ref,
             wst_ref, bs_ref, gs_ref, bsg_ref, gpost_ref, mask_ref, eg_ref,
             dom_ref, dya_ref, dyb_ref, dgp_ref, dot_ref, delta_ref, duv_ref,
             dws_ref, dbs_ref, dgs_ref, dbsg_ref, dgpost_ref, dbs_acc):
        step = pl.program_id(0)

        @pl.when(step == 0)
        def _():
            dws_ref[...] = jnp.zeros_like(dws_ref)
            dbs_acc[...] = jnp.zeros_like(dbs_acc)
            dgs_ref[...] = jnp.zeros_like(dgs_ref)
            dbsg_ref[...] = jnp.zeros_like(dbsg_ref)
            dgpost_ref[...] = jnp.zeros_like(dgpost_ref)

        om = om_ref[...]
        dx1v = dx1_ref[...]
        r = lax.rsqrt(jnp.mean(om * om, axis=-1, keepdims=True) + EPS)
        gpost_v = gpost_ref[...]
        dgpost_ref[...] += jnp.sum(dx1v * om * r, axis=0, keepdims=True)
        dom = _rms_bwd(om, r, gpost_v, dx1v).astype(BF16)
        dom_ref[...] = dom
        dmg = _dot_nt(dom, wo_ref[...])

        gates = _sigmoid(gp_ref[...].astype(F32))
        ga, gb = gates[:, :D_MODEL], gates[:, D_MODEL:]
        yav, ybv = ya_ref[...].astype(F32), yb_ref[...].astype(F32)
        dya = (dmg * ga).astype(BF16)
        dyb = (dmg * gb).astype(BF16)
        dya_ref[...] = dya
        dyb_ref[...] = dyb
        dgp_ref[:, :D_MODEL] = (dmg * yav * ga * (1.0 - ga)).astype(BF16)
        dgp_ref[:, D_MODEL:] = (dmg * ybv * gb * (1.0 - gb)).astype(BF16)

        dat_t = _dot_nt(dya, wa_ref[...]).T.astype(BF16)
        dot_ref[0] = dat_t
        o_t = o_ref[...].astype(F32).T
        delta_ref[0] = jnp.sum((dat_t.astype(F32) * o_t).reshape(HEADS, HEAD_DIM, tm), axis=1)
        dsgu = _dot_nt(dyb, wb_ref[...])

        uvp = uv_ref[...].astype(F32)
        uv, guv = _gelu_and_grad(uvp)
        u, vv = uv[:, :SGU_W], uv[:, SGU_W:]
        gs_v = gs_ref[...]
        vn, xh, rln = _layernorm_fwd(vv, gs_v, bsg_ref[...])
        bias = bs_ref[...]
        if nw > 1:
            bias = jnp.concatenate([bias] * nw, axis=0)
        mixed = _sgu_mix(vn, ws_ref) + bias
        du = dsgu * mixed
        dmixed = dsgu * u

        lane = lax.broadcasted_iota(jnp.int32, (WINDOW, 128), 1)
        low = lane < HEAD_DIM
        dvn_wins = []
        for w in range(nw):
            rows = slice(w * WINDOW, (w + 1) * WINDOW)
            dbs_acc[...] += dmixed[rows, :]
            slabs = []
            for p in range(GROUPS // 2):
                cols = slice(p * 128, (p + 1) * 128)
                dm2 = dmixed[rows, cols]
                dlo = jnp.where(low, dm2, 0.0).astype(BF16)
                dhi = jnp.where(low, 0.0, dm2).astype(BF16)
                vn2 = vn[rows, cols].astype(BF16)
                dws_ref[2 * p] += _dot_nt(dlo, vn2)
                dws_ref[2 * p + 1] += _dot_nt(dhi, vn2)
                slabs.append(jnp.dot(wst_ref[2 * p], dlo, preferred_element_type=F32)
                             + jnp.dot(wst_ref[2 * p + 1], dhi, preferred_element_type=F32))
            dvn_wins.append(jnp.concatenate(slabs, axis=1))
        dvn = jnp.concatenate(dvn_wins, axis=0) if nw > 1 else dvn_wins[0]

        dgs_ref[...] += jnp.sum(dvn * xh, axis=0, keepdims=True)
        dbsg_ref[...] += jnp.sum(dvn, axis=0, keepdims=True)
        dxh = dvn * gs_v
        dvv = rln * (dxh - jnp.mean(dxh, axis=-1, keepdims=True)
                     - xh * jnp.mean(dxh * xh, axis=-1, keepdims=True))
        duv_ref[:, :SGU_W] = (du * guv[:, :SGU_W]).astype(BF16)
        duv_ref[:, SGU_W:] = (dvv * guv[:, SGU_W:]).astype(BF16)

        @pl.when(step == pl.num_programs(0) - 1)
        def _():
            for g in range(GROUPS):
                dws_ref[g] = dws_ref[g] * mask_ref[...]
            dbs_ref[...] = _split3_dot(dbs_acc[...], eg_ref[...])

    rows_out = [((s_len, D_MODEL), BF16, _row_spec(tm, D_MODEL)),
                ((s_len, D_MODEL), BF16, _row_spec(tm, D_MODEL)),
                ((s_len, D_MODEL), BF16, _row_spec(tm, D_MODEL)),
                ((s_len, 2 * D_MODEL), BF16, _row_spec(tm, 2 * D_MODEL)),
                ((nt, FOX_W, tm), BF16, _tile_spec(FOX_W, tm)),
                ((nt, HEADS, tm), F32, _tile_spec(HEADS, tm)),
                ((s_len, 2 * SGU_W), BF16, _row_spec(tm, 2 * SGU_W))]
    acc_out = [((GROUPS, WINDOW, WINDOW), F32), ((WINDOW, 128), F32), ((1, SGU_W), F32),
               ((1, SGU_W), F32), ((1, D_MODEL), F32)]
    return pl.pallas_call(
        body, name="mix_bwd", grid=(nt,),
        in_specs=[_row_spec(tm, D_MODEL), _row_spec(tm, D_MODEL), _row_spec(tm, D_MODEL),
                  _row_spec(tm, D_MODEL), _row_spec(tm, 2 * D_MODEL), _row_spec(tm, 2 * SGU_W),
                  _row_spec(tm, FOX_W), _const_spec(wout.shape), _const_spec(wa.shape),
                  _const_spec(wb.shape), _const_spec(wsm.shape), _const_spec(wsmt.shape),
                  _const_spec(bsf.shape), _const_spec((1, SGU_W)), _const_spec((1, SGU_W)),
                  _const_spec((1, D_MODEL)), _const_spec(wmask.shape), _const_spec(egrp.shape)],
        out_specs=[o[2] for o in rows_out] + [_const_spec(s) for s, _ in acc_out],
        out_shape=[jax.ShapeDtypeStruct(o[0], o[1]) for o in rows_out]
        + [jax.ShapeDtypeStruct(s, dt) for s, dt in acc_out],
        scratch_shapes=[pltpu.VMEM((WINDOW, SGU_W), F32)],
        compiler_params=_params(48, 1),
    )(dx1, om, ya, yb, gpre, uvpre, attn, wout, wa, wb, wsm, wsmt, bsf, gsgu, bsgu, gpost, wmask,
      egrp)


def _attn_bwd(qa, ka, kat, vs, dot_, lse, delta, ecol):
    s_len = qa.shape[0]
    t = ATTN_TILE
    nb = s_len // t

    def body(k_ref, kt_ref, vs_ref, q_ref, do_ref, lse_ref, dl_ref, ec_ref, gk_ref, dvt_ref,
             gqt_ref, csum_ref, p_sc, ds_sc):
        j = pl.program_id(0)

        @pl.when(j == 0)
        def _():
            gqt_ref[...] = jnp.zeros_like(gqt_ref)

        gk_ref[...] = jnp.zeros_like(gk_ref)
        dvt_ref[...] = jnp.zeros_like(dvt_ref)

        def probs(i, slot, masked):
            qrows = pl.ds(pl.multiple_of(i * t, t), t)
            if masked:
                keep = (lax.broadcasted_iota(jnp.int32, (t, t), 0)
                        <= lax.broadcasted_iota(jnp.int32, (t, t), 1))
            for hd in range(HEADS):
                sl = slice(hd * 128, (hd + 1) * 128)
                hr = slice(hd * HEAD_DIM, (hd + 1) * HEAD_DIM)
                st = _dot_nt(k_ref[:, sl], q_ref[qrows, sl])
                if masked:
                    st = jnp.where(keep, st, -jnp.inf)
                pt = jnp.exp2(st - lse_ref[i, hd:hd + 1, :])
                dpt = jnp.dot(vs_ref[:, hd * 128:hd * 128 + HEAD_DIM], do_ref[i, hr, :],
                              preferred_element_type=F32)
                p_sc[slot, hd] = pt.astype(BF16)
                ds_sc[slot, hd] = (pt * (dpt - dl_ref[i, hd:hd + 1, :])).astype(BF16)

        def grads(i, slot):
            qrows = pl.ds(pl.multiple_of(i * t, t), t)
            for hd in range(HEADS):
                sl = slice(hd * 128, (hd + 1) * 128)
                hr = slice(hd * HEAD_DIM, (hd + 1) * HEAD_DIM)
                dst = ds_sc[slot, hd]
                dvt_ref[0, hr, :] += _dot_nt(do_ref[i, hr, :], p_sc[slot, hd])
                gk_ref[:, sl] += jnp.dot(dst, q_ref[qrows, sl], preferred_element_type=F32)
                gqt_ref[i, hd * QT_ROWS:(hd + 1) * QT_ROWS, :] += jnp.dot(
                    kt_ref[0, hd * 128:hd * 128 + QT_ROWS, :], dst, preferred_element_type=F32)

        probs(j, 0, True)
        pairs = (nb - 1 - j) // 2

        def two_blocks(p, carry):
            i1 = j + 1 + 2 * p
            probs(i1, 1, False)
            grads(i1 - 1, 0)
            probs(i1 + 1, 0, False)
            grads(i1, 1)
            return carry

        lax.fori_loop(0, pairs, two_blocks, 0)

        @pl.when(nb - 1 - j - 2 * pairs == 0)
        def _():
            grads(nb - 1, 0)

        @pl.when(nb - 1 - j - 2 * pairs == 1)
        def _():
            probs(nb - 1, 1, False)
            grads(nb - 2, 0)
            grads(nb - 1, 1)

        csum_ref[...] = _split3_dot(gk_ref[...], ec_ref[...])

    return pl.pallas_call(
        body, name="attn_bwd", grid=(nb,),
        in_specs=[_row_spec(t, SLAB_W), _tile_spec(SLAB_W, t), _row_spec(t, SLAB_W),
                  _const_spec(qa.shape), _const_spec(dot_.shape), _const_spec(lse.shape),
                  _const_spec(delta.shape), _const_spec(ecol.shape)],
        out_specs=[_row_spec(t, SLAB_W), _tile_spec(FOX_W, t),
                   _const_spec((nb, HEADS * QT_ROWS, t)), _row_spec(t, 128)],
        out_shape=[jax.ShapeDtypeStruct((s_len, SLAB_W), F32),
                   jax.ShapeDtypeStruct((nb, FOX_W, t), F32),
                   jax.ShapeDtypeStruct((nb, HEADS * QT_ROWS, t), F32),
                   jax.ShapeDtypeStruct((s_len, 128), F32)],
        scratch_shapes=[pltpu.VMEM((2, HEADS, t, t), BF16), pltpu.VMEM((2, HEADS, t, t), BF16)],
        compiler_params=_params(60, 1),
    )(ka, kat, vs, qa, dot_, lse, delta, ecol)


def _rev_cumsum(col_sums, gqt, triu):
    s_len = col_sums.shape[0]
    tm = TOKEN_TILE
    n = s_len // tm

    def body(cs_ref, gqt_ref, tri_ref, o_ref, carry):
        @pl.when(pl.program_id(0) == 0)
        def _():
            carry[...] = jnp.zeros_like(carry)
        rows = [gqt_ref[0, hd * QT_ROWS + HEAD_DIM:hd * QT_ROWS + HEAD_DIM + 1, :]
                for hd in range(HEADS)]
        row_sums = jnp.concatenate(rows + [jnp.zeros((128 - HEADS, tm), F32)], axis=0).T
        out = _tri_dot(tri_ref[...], row_sums - cs_ref[...]) + carry[...]
        o_ref[...] = out
        carry[...] = out[0:1, :]

    return pl.pallas_call(
        body, name="rev_cumsum", grid=(n,),
        in_specs=[pl.BlockSpec((tm, 128), lambda i: (n - 1 - i, 0)),
                  pl.BlockSpec((1, HEADS * QT_ROWS, tm), lambda i: (n - 1 - i, 0, 0)),
                  _const_spec((tm, tm))],
        out_specs=pl.BlockSpec((tm, 128), lambda i: (n - 1 - i, 0)),
        out_shape=jax.ShapeDtypeStruct((s_len, 128), F32),
        scratch_shapes=[pltpu.VMEM((1, 128), F32)],
        compiler_params=_params(32, 1),
    )(col_sums, gqt, triu)


def _heads_from_slabs(slabs):
    lane = lax.broadcasted_iota(jnp.int32, slabs[0].shape, 1)
    low = lane < HEAD_DIM
    pairs = [jnp.where(low, slabs[2 * p], pltpu.roll(slabs[2 * p + 1], HEAD_DIM, 1))
             for p in range(HEADS // 2)]
    return jnp.concatenate(pairs, axis=1)


def _proj_bwd(gqt, gk, dvt, dlogf, flog, qraw, kraw, duv, dgp, x, dx1, wcat, bdiag, gq, gk_gain, g1,
              efold):
    s_len = x.shape[0]
    tm = TOKEN_TILE

    def body(gqt_ref, gkk_ref, dvt_ref, dlf_ref, flog_ref, qr_ref, kr_ref, duv_ref, dgp_ref, x_ref,
             dx1_ref, w_ref, bd_ref, gq_ref, gk_ref, g1_ref, ef_ref,
             dx_ref, dprojt_ref, dgq_ref, dgk_ref, dbf_ref, dg1_ref, gq_acc, gk_acc):
        step = pl.program_id(0)

        @pl.when(step == 0)
        def _():
            gq_acc[...] = jnp.zeros_like(gq_acc)
            gk_acc[...] = jnp.zeros_like(gk_acc)
            dbf_ref[...] = jnp.zeros_like(dbf_ref)
            dg1_ref[...] = jnp.zeros_like(dg1_ref)

        pad = jnp.zeros((128 - QT_ROWS, tm), F32)
        q_slabs = [jnp.concatenate([gqt_ref[0, hd * QT_ROWS:(hd + 1) * QT_ROWS, :], pad], axis=0).T
                   for hd in range(HEADS)]
        dqn = _heads_from_slabs(q_slabs)
        dkn = _heads_from_slabs([gkk_ref[:, hd * 128:(hd + 1) * 128] for hd in range(HEADS)])

        def head_bwd(raw_ref, dn, g_ref, acc):
            raw = raw_ref[...].astype(F32)
            r = lax.rsqrt(_seg_mean(raw * raw, bd_ref) + EPS)
            xhat = raw * r
            acc[0:1, :] += jnp.sum(dn * xhat, axis=0, keepdims=True)
            dyg = dn * g_ref[...]
            return r * (dyg - xhat * _seg_mean(dyg * xhat, bd_ref))

        dot = functools.partial(jnp.dot, preferred_element_type=F32)
        duv, dgp = duv_ref[...], dgp_ref[...]
        dprojt_ref[C_UV:C_G, :] = duv.astype(F32).T.astype(BF16)
        dprojt_ref[C_G:C_F, :] = dgp.astype(F32).T.astype(BF16)
        dh = dot(duv, w_ref[C_UV:C_G, :]) + dot(dgp, w_ref[C_G:C_F, :])

        dq = head_bwd(qr_ref, dqn * HEAD_DIM ** -0.5, gq_ref, gq_acc)
        dk = head_bwd(kr_ref, dkn * LN2, gk_ref, gk_acc)
        dv_t = dvt_ref[0]
        dfl = dlf_ref[...] * _sigmoid(-flog_ref[...])
        dbf_ref[...] += jnp.sum(dfl, axis=0, keepdims=True)
        dprojt_ref[C_Q:C_K, :] = dq.T.astype(BF16)
        dprojt_ref[C_K:C_V, :] = dk.T.astype(BF16)
        dprojt_ref[C_V:C_UV, :] = dv_t.astype(BF16)
        dprojt_ref[C_F:C_END, :] = dfl.T.astype(BF16)
        dh = (dh + dot(dq.astype(BF16), w_ref[C_Q:C_K, :]) + dot(dk.astype(BF16), w_ref[C_K:C_V, :])
              + dot(dv_t.T.astype(BF16), w_ref[C_V:C_UV, :])
              + dot(dfl.astype(BF16), w_ref[C_F:C_END, :]))
        xf = x_ref[...]
        r = lax.rsqrt(jnp.mean(xf * xf, axis=-1, keepdims=True) + EPS)
        dg1_ref[...] += jnp.sum(dh * xf * r, axis=0, keepdims=True)
        dx_ref[...] = dx1_ref[...] + _rms_bwd(xf, r, g1_ref[...], dh)

        @pl.when(step == pl.num_programs(0) - 1)
        def _():
            dgq_ref[...] = _split3_dot(gq_acc[...], ef_ref[...])
            dgk_ref[...] = _split3_dot(gk_acc[...], ef_ref[...])

    outs = [((s_len, D_MODEL), F32, _row_spec(tm, D_MODEL)),
            ((C_END, s_len), BF16, pl.BlockSpec((C_END, tm), lambda i: (0, i))),
            ((8, 128), F32, _const_spec((8, 128))),
            ((8, 128), F32, _const_spec((8, 128))),
            ((1, 128), F32, _const_spec((1, 128))),
            ((1, D_MODEL), F32, _const_spec((1, D_MODEL)))]
    return pl.pallas_call(
        body, name="proj_bwd", grid=(s_len // tm,),
        in_specs=[_tile_spec(HEADS * QT_ROWS, tm), _row_spec(tm, SLAB_W), _tile_spec(FOX_W, tm),
                  _row_spec(tm, 128), _row_spec(tm, 128), _row_spec(tm, FOX_W),
                  _row_spec(tm, FOX_W), _row_spec(tm, 2 * SGU_W), _row_spec(tm, 2 * D_MODEL),
                  _row_spec(tm, D_MODEL), _row_spec(tm, D_MODEL), _const_spec(wcat.shape),
                  _const_spec(bdiag.shape), _const_spec((1, FOX_W)), _const_spec((1, FOX_W)),
                  _const_spec((1, D_MODEL)), _const_spec(efold.shape)],
        out_specs=[o[2] for o in outs],
        out_shape=[jax.ShapeDtypeStruct(o[0], o[1]) for o in outs],
        scratch_shapes=[pltpu.VMEM((8, FOX_W), F32), pltpu.VMEM((8, FOX_W), F32)],
        compiler_params=_params(56, 1),
    )(gqt, gk, dvt, dlogf, flog, qraw, kraw, duv, dgp, x, dx1, wcat, bdiag, gq, gk_gain, g1, efold)


def _dw_matmul(at, b, tm, name, after=()):
    m, s_len = at.shape
    n = b.shape[1]

    def body(a_ref, b_ref, *rest):
        rest[-1][...] = jnp.dot(a_ref[...], b_ref[...], preferred_element_type=F32).astype(BF16)

    return pl.pallas_call(
        body, name=name, grid=(m // tm,),
        in_specs=[pl.BlockSpec((tm, s_len), lambda i: (i, 0)), _const_spec(b.shape)]
        + [pl.BlockSpec(memory_space=pl.ANY)] * len(after),
        out_specs=pl.BlockSpec((tm, n), lambda i: (i, 0)),
        out_shape=jax.ShapeDtypeStruct((m, n), BF16),
        compiler_params=_params(48, 1),
    )(at, b, *after)


def _adamw(parts, w, m, v, tr, name, col_tile=None, select=None):
    parts = parts if isinstance(parts, (list, tuple)) else [parts]
    rows, cols = w.shape
    extra = [] if select is None else [select]
    bc1 = 1.0 - ADAM_B1 ** ADAM_STEP
    bc2 = 1.0 - ADAM_B2 ** ADAM_STEP

    def body(*refs):
        p_refs = refs[:len(parts)]
        sel_refs = refs[len(parts):len(parts) + len(extra)]
        w_ref, m_ref, v_ref, g_ref, d_ref, mo_ref, vo_ref = refs[len(parts) + len(extra):]
        g = None
        for p_ref, p in zip(p_refs, parts):
            for idx in range(p.shape[0]):
                term = p_ref[idx].astype(F32)
                g = term if g is None else g + term
        if sel_refs:
            g = _tri_dot(sel_refs[0][...], g)
        g_ref[...] = g
        mn = ADAM_B1 * m_ref[...] + (1.0 - ADAM_B1) * g
        vn = ADAM_B2 * v_ref[...] + (1.0 - ADAM_B2) * (g * g)
        mo_ref[...] = mn
        vo_ref[...] = vn
        m_hat = mn / bc1
        v_hat = vn / bc2
        d_ref[...] = -ADAM_LR * (m_hat / (jnp.sqrt(v_hat) + ADAM_EPS) + ADAM_WD * w_ref[...])

    if col_tile is None:
        spec = pl.BlockSpec((tr, cols), lambda i: (i, 0))
        pspecs = [pl.BlockSpec((p.shape[0], tr, cols), lambda i: (0, i, 0)) for p in parts]
        steps = rows // tr
    else:
        spec = pl.BlockSpec((rows, col_tile), lambda i: (0, i))
        pspecs = [pl.BlockSpec((p.shape[0], p.shape[1], col_tile), lambda i: (0, 0, i))
                  for p in parts]
        steps = cols // col_tile
    return pl.pallas_call(
        body, name=name, grid=(steps,),
        in_specs=pspecs + [_const_spec(e.shape) for e in extra] + [spec, spec, spec],
        out_specs=[spec] * 4,
        out_shape=[jax.ShapeDtypeStruct((rows, cols), F32)] * 4,
        compiler_params=_params(48, 1),
    )(*parts, *extra, w, m, v)


def _sum_parts(parts, name):
    n, rows, cols = parts.shape

    def body(p_ref, o_ref):
        g = p_ref[0]
        for idx in range(1, n):
            g = g + p_ref[idx]
        o_ref[...] = g

    return pl.pallas_call(
        body, name=name, out_shape=jax.ShapeDtypeStruct((rows, cols), F32),
        in_specs=[_const_spec(parts.shape)], out_specs=_const_spec((rows, cols)), grid=(1,),
        compiler_params=_params(16, 1),
    )(parts)


VEC_NAMES = ("g_pre_mix", "b_forget", "g_q", "g_k", "g_sgu", "b_sgu", "b_spatial", "g_post_mix",
             "g_pre_ffn", "g_post_ffn")
VEC_ROWS = 16
LOSS_ROW = len(VEC_NAMES)


def _pack_vectors(d, loss_row):
    rows = []
    for k in VEC_NAMES:
        flat = d[k].reshape(1, -1).astype(F32)
        rows.append(jnp.pad(flat, ((0, 0), (0, 1024 - flat.shape[1]))))
    rows.append(loss_row)
    rows.append(jnp.zeros((VEC_ROWS - len(rows), 1024), F32))
    return jnp.concatenate(rows, axis=0)


def _adamw_vectors(grad_rows, ws, ms, vs):
    n = len(VEC_NAMES)
    bc1 = 1.0 - ADAM_B1 ** ADAM_STEP
    bc2 = 1.0 - ADAM_B2 ** ADAM_STEP

    def step(g, w, m, v):
        mn = ADAM_B1 * m + (1.0 - ADAM_B1) * g
        vn = ADAM_B2 * v + (1.0 - ADAM_B2) * (g * g)
        delta = -ADAM_LR * ((mn / bc1) / (jnp.sqrt(vn / bc2) + ADAM_EPS) + ADAM_WD * w)
        return g, delta, mn, vn

    def body(*refs):
        g_ref = refs[0]
        ins = [refs[1 + j * n:1 + (j + 1) * n] for j in range(3)]
        outs = [refs[1 + (3 + j) * n:1 + (4 + j) * n] for j in range(4)]
        for i in range(n):
            shape = ws[i].shape
            if len(shape) == 2:
                res = step(g_ref[i:i + 1, :shape[1]], *[r[i][...] for r in ins])
                for o, val in zip(outs, res):
                    o[i][...] = val
            else:
                for r in range(shape[1]):
                    res = step(g_ref[i:i + 1, r * shape[2]:(r + 1) * shape[2]],
                               *[q[i][0, r:r + 1, :] for q in ins])
                    for o, val in zip(outs, res):
                        o[i][0, r:r + 1, :] = val

    vmem = pl.BlockSpec(memory_space=pltpu.VMEM)
    flat = pl.pallas_call(
        body, name="adamw_vectors",
        in_specs=[vmem] * (1 + 3 * n), out_specs=[vmem] * (4 * n),
        out_shape=[jax.ShapeDtypeStruct(w.shape, F32) for _ in range(4) for w in ws],
    )(grad_rows, *ws, *ms, *vs)
    return [flat[j * n:(j + 1) * n] for j in range(4)]


def _cols_to_blocks(full, width):
    r = full.shape[0]
    return jnp.transpose(full.reshape(r, N_DEV, width), (1, 0, 2))


def _blocks_to_cols(blocks):
    n, r, width = blocks.shape
    return jnp.transpose(blocks, (1, 0, 2)).reshape(r, n * width)


def kernel(x, g_pre_mix, w_in, b_forget, g_q, g_k, g_sgu, b_sgu, w_spatial, b_spatial, w_branch_a, w_branch_b, w_out, g_post_mix, g_pre_ffn, w_ffn_in, w_ffn_down, g_post_ffn, loss_target, m_g_pre_mix, m_w_in, m_b_forget, m_g_q, m_g_k, m_g_sgu, m_b_sgu, m_w_spatial, m_b_spatial, m_w_branch_a, m_w_branch_b, m_w_out, m_g_post_mix, m_g_pre_ffn, m_w_ffn_in, m_w_ffn_down, m_g_post_ffn, v_g_pre_mix, v_w_in, v_b_forget, v_g_q, v_g_k, v_g_sgu, v_b_sgu, v_w_spatial, v_b_spatial, v_w_branch_a, v_w_branch_b, v_w_out, v_g_post_mix, v_g_pre_ffn, v_w_ffn_in, v_w_ffn_down, v_g_post_ffn):
    big_names = ("w_in", "w_branch_a", "w_branch_b", "w_out", "w_ffn_in", "w_ffn_down")
    weights = dict(g_pre_mix=g_pre_mix, w_in=w_in, b_forget=b_forget, g_q=g_q, g_k=g_k, g_sgu=g_sgu,
                   b_sgu=b_sgu, w_spatial=w_spatial, b_spatial=b_spatial, w_branch_a=w_branch_a,
                   w_branch_b=w_branch_b, w_out=w_out, g_post_mix=g_post_mix, g_pre_ffn=g_pre_ffn,
                   w_ffn_in=w_ffn_in, w_ffn_down=w_ffn_down, g_post_ffn=g_post_ffn)
    mom1 = dict(g_pre_mix=m_g_pre_mix, w_in=m_w_in, b_forget=m_b_forget, g_q=m_g_q, g_k=m_g_k,
                g_sgu=m_g_sgu, b_sgu=m_b_sgu, w_spatial=m_w_spatial, b_spatial=m_b_spatial,
                w_branch_a=m_w_branch_a, w_branch_b=m_w_branch_b, w_out=m_w_out,
                g_post_mix=m_g_post_mix, g_pre_ffn=m_g_pre_ffn, w_ffn_in=m_w_ffn_in,
                w_ffn_down=m_w_ffn_down, g_post_ffn=m_g_post_ffn)
    mom2 = dict(g_pre_mix=v_g_pre_mix, w_in=v_w_in, b_forget=v_b_forget, g_q=v_g_q, g_k=v_g_k,
                g_sgu=v_g_sgu, b_sgu=v_b_sgu, w_spatial=v_w_spatial, b_spatial=v_b_spatial,
                w_branch_a=v_w_branch_a, w_branch_b=v_w_branch_b, w_out=v_w_out,
                g_post_mix=v_g_post_mix, g_pre_ffn=v_g_pre_ffn, w_ffn_in=v_w_ffn_in,
                w_ffn_down=v_w_ffn_down, g_post_ffn=v_g_post_ffn)
    names = list(weights)
    shapes = {k: weights[k].shape for k in names}

    s_len = x.shape[1]
    xs = x.reshape(s_len, D_MODEL)
    tgt = loss_target.reshape(s_len, D_MODEL)

    transposed = ("w_in", "w_ffn_in")

    def local_view(a, k):
        return jnp.transpose(a[0]) if k in transposed else a[0]

    shards = {k: local_view(weights[k], k).astype(BF16) for k in big_names}

    x_pos, y_pos, c_pos = _mesh_pos()
    me = 4 * x_pos + 2 * y_pos + c_pos
    r_idx = jnp.arange(BLK)
    general = (jnp.asarray(BLK_AT, jnp.int32) - jnp.asarray(FRAME_START, jnp.int32))[me] + r_idx
    holder = jnp.where(r_idx < F_AT, BLK_AT[F_DEV] - FRAME_START[F_DEV] + r_idx,
                       jnp.where(r_idx < F_AT + HEADS, FRAME - F_AT + r_idx,
                                 BLK_AT[F_DEV] - FRAME_START[F_DEV] - HEADS + r_idx))
    frame_row = jnp.where(me == F_DEV, holder, general)
    in_frame = (frame_row[:, None] == jnp.arange(FRAME_ROWS)[None, :]).astype(BF16)
    my_frame = jnp.dot(in_frame.T, shards["w_in"], preferred_element_type=F32).astype(BF16)
    wcat = _gather_w_in(my_frame)
    wcat, later = lax.optimization_barrier(
        (wcat, [shards[k] for k in big_names if k != "w_in"]))
    shards.update(zip([k for k in big_names if k != "w_in"], later))
    (gat_mix, gat_ffn), gat_token = _exchange_start(
        [[shards["w_branch_a"], shards["w_branch_b"], shards["w_out"]],
         [shards["w_ffn_in"], shards["w_ffn_down"]]], "gather_start", gather=True)

    seg = np.arange(FOX_W) // HEAD_DIM
    bdiag = jnp.asarray(seg[:128, None] == seg[None, :128], BF16)
    tm = TOKEN_TILE
    lower = np.arange(tm)[None, :] <= np.arange(tm)[:, None]
    tril = jnp.asarray(lower, BF16)
    triu = jnp.asarray(lower.T, BF16)
    egrp = jnp.asarray(seg[:, None] == np.arange(128)[None, :], BF16)
    efold = jnp.asarray((np.arange(FOX_W) % HEAD_DIM)[:, None] == np.arange(128)[None, :], BF16)
    gq512 = jnp.tile(g_q.reshape(1, HEAD_DIM), (1, HEADS))
    gk512 = jnp.tile(g_k.reshape(1, HEAD_DIM), (1, HEADS))
    bfor = jnp.pad(b_forget.reshape(1, HEADS), ((0, 0), (0, 128 - HEADS)))
    pos = np.arange(WINDOW)
    wmask = (pos[None, :] // CHUNK) <= (pos[:, None] // CHUNK)
    wsm_f = jnp.where(jnp.asarray(wmask)[None], w_spatial[0], 0.0)
    wsm = wsm_f.astype(BF16)
    wsmt = jnp.transpose(wsm_f, (0, 2, 1)).astype(BF16)
    bsf = jnp.repeat(jnp.transpose(b_spatial[0]), HEAD_DIM, axis=1)
    wmask_f = jnp.asarray(wmask, F32)

    col = np.arange(SLAB_W)
    row128 = np.arange(128)

    def d_place(first, sign):
        place = sum(((col[None, :] // 128 == row128[:, None] - HEADS * a)
                     & (col[None, :] % 128 == first + a)).astype(np.float32) for a in range(3))
        return jnp.asarray(sign * place, BF16)

    pdq, pdk = d_place(HEAD_DIM, 1.0), d_place(HEAD_DIM + 3, -1.0)
    ones_q = jnp.asarray((col % 128 >= HEAD_DIM + 3) & (col % 128 < HEAD_DIM + 6), F32)[None]
    ones_k = jnp.asarray((col % 128 >= HEAD_DIM) & (col % 128 < HEAD_DIM + 3), F32)[None]
    ecol = jnp.asarray((col[:, None] // 128 == row128[None, :])
                       & (col[:, None] % 128 == HEAD_DIM + 3), BF16)

    (h, qa, ka, kat, vs, vt, qraw, kraw, flog, uvpre, gpre) = _proj_fwd(
        xs, g_pre_mix + gat_token[0:1, 0:1], wcat, bdiag, gq512, gk512, bfor, tril, pdq, pdk,
        ones_q, ones_k)
    attn, attn_t, lse = _attn_fwd(qa, ka, vt)
    (own_a, own_b, own_out), (zone_a, zone_b, zone_out) = _exchange_wait(
        gat_mix, attn, "gather_wait_mix", gather=True)
    wa = _blocks_to_cols(_own_block(zone_a, own_a))
    wb = _blocks_to_cols(_own_block(zone_b, own_b))
    wout = _own_block(zone_out, own_out).reshape(D_MODEL, D_MODEL)
    sgu_t, ya, yb, merged_t, om, x1 = _mix_fwd(attn, uvpre, gpre, xs, wa, wb, wout, wsm, bsf,
                                           g_sgu, b_sgu, g_post_mix)
    (own_ffn, own_down), (zone_ffn, zone_down) = _exchange_wait(
        gat_ffn, x1, "gather_wait_ffn", gather=True)
    wffn = _own_block(zone_ffn, own_ffn).reshape(2 * D_FF, D_MODEL)
    wdown = _own_block(zone_down, own_down).reshape(D_FF, D_MODEL)
    (dx1, h2, act_t, dff, dgu_t, loss_acc, dg_post_ffn, dg_pre_ffn) = _ffn_fwd_bwd(
        x1, tgt, wffn, wdown, g_pre_ffn, g_post_ffn)

    dw_down = _dw_matmul(act_t, dff, D_FF // 4, "dw_down")
    dw_ffn = _dw_matmul(dgu_t, h2, 2 * D_FF // N_DEV, "dw_ffn_in")
    def own_of(parts):
        return [lax.dynamic_index_in_dim(p, me, 0, keepdims=False) for p in parts]

    parts_ffn = [dw_ffn.reshape(N_DEV, 2 * D_FF // N_DEV, D_MODEL),
                 dw_down.reshape(N_DEV, D_FF // N_DEV, D_MODEL)]
    mine_ffn = own_of(parts_ffn)
    (sct_ffn,), sct_ffn_token = _exchange_start([parts_ffn], "scatter_start_ffn", gather=False)

    (dom, dya, dyb, dgp, dot_, delta, duv, dws, dbs, dg_sgu, db_sgu, dg_post_mix) = _mix_bwd(
        dx1, om, ya, yb, gpre, uvpre, attn, wout, wa, wb, wsm, wsmt, bsf, g_sgu, b_sgu,
        g_post_mix + sct_ffn_token[0:1, 0:1], wmask_f, egrp)
    dw_out = _dw_matmul(merged_t, dom, 512, "dw_out")
    dw_a = _dw_matmul(attn_t, dya, 512, "dw_a")
    dw_b = _dw_matmul(sgu_t, dyb, 512, "dw_b")
    parts_mix = [_cols_to_blocks(dw_a, D_MODEL // N_DEV), _cols_to_blocks(dw_b, D_MODEL // N_DEV),
                 dw_out.reshape(N_DEV, D_MODEL // N_DEV, D_MODEL)]
    mine_mix = own_of(parts_mix)
    (sct_mix,), sct_mix_token = _exchange_start([parts_mix], "scatter_start_mix", gather=False)

    gk_all, dvt, gqt, col_sums = _attn_bwd(qa, ka, kat, vs, dot_, lse,
                                           delta + sct_mix_token[0, 0], ecol)
    dlogf = _rev_cumsum(col_sums, gqt, triu)
    dx, dproj_t, dgq, dgk, dbf, dg_pre_mix = _proj_bwd(
        gqt, gk_all, dvt, dlogf, flog, qraw, kraw, duv, dgp, xs, dx1, wcat, bdiag, gq512, gk512,
        g_pre_mix, efold)

    small_local = dict(
        g_pre_mix=dg_pre_mix, b_forget=dbf[:, :HEADS], g_q=dgq[0:1, :HEAD_DIM],
        g_k=dgk[0:1, :HEAD_DIM], g_sgu=dg_sgu, b_sgu=db_sgu, w_spatial=dws,
        b_spatial=jnp.transpose(dbs[:, :GROUPS]), g_post_mix=dg_post_mix, g_pre_ffn=dg_pre_ffn,
        g_post_ffn=dg_post_ffn)
    loss_row = jnp.pad(loss_acc[0:1, 0:1], ((0, 0), (0, 1023)))
    small_parts = [_pack_vectors(small_local, loss_row).reshape(N_DEV, VEC_ROWS // N_DEV, 1024),
                   dws]

    def with_own(zones, own_blocks):
        return [_own_block(z, b) for z, b in zip(zones, own_blocks)]

    mine_small = own_of(small_parts)
    (sct_small,), sct_small_token = _exchange_start([small_parts], "scatter_start_small",
                                                    gather=False)
    dw_cat = _dw_matmul(dproj_t, h, C_END // N_DEV, "dw_in", after=(sct_small_token,))
    recv_vec, recv_ws = with_own(
        _exchange_wait(sct_small, dw_cat, "scatter_wait_small", gather=False)[1], mine_small)
    small_sums = [_sum_parts(recv_vec, "sum_vectors"), _sum_parts(recv_ws, "sum_w_spatial")]
    (gat_small,), gat_small_token = _exchange_start([small_sums], "gather_start_small",
                                                    gather=True)
    pair_blocks, own_pair = _pair_sums(dw_cat, "pair_sums_in", gat_small_token)
    rs_in, rs_token = _chip_exchange_start(pair_blocks, "chip_exchange_start_in")

    recv_ffn, recv_down = with_own(
        _exchange_wait(sct_ffn, rs_token, "scatter_wait_ffn", gather=False)[1], mine_ffn)
    recv_a, recv_b, recv_out = with_own(
        _exchange_wait(sct_mix, recv_ffn, "scatter_wait_mix", gather=False)[1], mine_mix)
    received = [None, recv_a, recv_b, recv_out, recv_ffn, recv_down]

    grads, deltas, new_m, new_v = {}, {}, {}, {}
    row_tiles = {"w_in": None, "w_branch_a": 512, "w_branch_b": 512, "w_out": 128, "w_ffn_in": 176,
                 "w_ffn_down": 352}

    def update(k, parts):
        outs = _adamw(parts, local_view(weights[k], k), local_view(mom1[k], k),
                      local_view(mom2[k], k), row_tiles[k], "adamw_" + k,
                      col_tile=256 if k == "w_in" else None,
                      select=in_frame if k == "w_in" else None)
        if k in transposed:
            outs = [jnp.transpose(o) for o in outs]
        grads[k], deltas[k], new_m[k], new_v[k] = [o[None] for o in outs]
        return outs[0]

    last = None
    for idx, k in enumerate(big_names):
        if k != "w_in":
            last = update(k, received[idx])

    (own_vec, own_ws), (zone_vec, zone_ws) = _exchange_wait(gat_small, last, "gather_wait_small",
                                                            gather=True)
    vec_all = _own_block(zone_vec, own_vec).reshape(VEC_ROWS, 1024)
    ws_all = _own_block(zone_ws, own_ws).reshape(1, GROUPS * WINDOW, WINDOW)

    def rows_of(d):
        return d["w_spatial"].reshape(GROUPS * WINDOW, WINDOW)

    outs = _adamw(ws_all, rows_of(weights), rows_of(mom1), rows_of(mom2), GROUPS * WINDOW,
                  "adamw_w_spatial")
    for dst, o in zip((grads, deltas, new_m, new_v), outs):
        dst["w_spatial"] = o.reshape(shapes["w_spatial"])
    sg = outs[0]
    vec_outs = _adamw_vectors(vec_all, *[[d[k] for k in VEC_NAMES] for d in (weights, mom1, mom2)])
    for dst, group in zip((grads, deltas, new_m, new_v), vec_outs):
        dst.update(zip(VEC_NAMES, group))
    arrived = _chip_exchange_wait(rs_in, sg, "chip_exchange_wait_in")
    update("w_in", [own_pair[None], arrived])

    loss = vec_all[LOSS_ROW, 0]
    return (loss, dx.reshape(x.shape), *[grads[k] for k in names], *[deltas[k] for k in names],
            *[new_m[k] for k in names], *[new_v[k] for k in names])
```

```python
import functools
import math

import jax
import jax.numpy as jnp
import numpy as np
from jax import lax
from jax.experimental import pallas as pl
from jax.experimental.pallas import tpu as pltpu

F32 = jnp.float32
BF16 = jnp.bfloat16

D_MODEL = 1024
FOX_W = 512
HEADS = 8
HEAD_DIM = 64
SGU_W = 512
GROUPS = 8
WINDOW = 128
CHUNK = 64
D_FF = 2816
IN_COLS = 4616
EPS = 1e-6
N_DEV = 8
LOG2E = 1.4426950408889634
LN2 = 0.6931471805599453

C_Q, C_K, C_V, C_UV, C_G, C_F, C_END = 0, 512, 1024, 1536, 2560, 4608, 4736

ADAM_LR, ADAM_B1, ADAM_B2, ADAM_EPS, ADAM_WD, ADAM_STEP = 0.001, 0.9, 0.999, 1e-08, 0.01, 10

MIB = 1024 * 1024
TOKEN_TILE = 256
ATTN_TILE = 256
SLAB_W = HEADS * 128
QT_ROWS = 72

BLK = IN_COLS // N_DEV
F_LO = 3 * FOX_W
F_DEV = F_LO // BLK
F_AT = F_LO - F_DEV * BLK
BLK_AT = [BLK * j - (HEADS if BLK * j > F_LO else 0) for j in range(N_DEV)]
FRAME_START = [a // 16 * 16 for a in BLK_AT]
FRAME = 608
FRAME_ROWS = FRAME + 16


def _params(vmem_mib, n_axes):
    return pltpu.CompilerParams(
        dimension_semantics=("arbitrary",) * n_axes, vmem_limit_bytes=vmem_mib * MIB)


def _const_spec(shape):
    nd = len(shape)
    return pl.BlockSpec(shape, lambda *_: (0,) * nd)


def _row_spec(tm, cols):
    return pl.BlockSpec((tm, cols), lambda i: (i, 0))


def _tile_spec(rows, tm):
    return pl.BlockSpec((1, rows, tm), lambda i: (i, 0, 0))


def _split3_dot(x, e):
    x1 = x.astype(BF16)
    r1 = x - x1.astype(F32)
    x2 = r1.astype(BF16)
    x3 = (r1 - x2.astype(F32)).astype(BF16)
    dot = functools.partial(jnp.dot, preferred_element_type=F32)
    return dot(x1, e) + dot(x2, e) + dot(x3, e)


def _tri_dot(tri, x):
    x1 = x.astype(BF16)
    r1 = x - x1.astype(F32)
    x2 = r1.astype(BF16)
    x3 = (r1 - x2.astype(F32)).astype(BF16)
    dot = functools.partial(jnp.dot, preferred_element_type=F32)
    return dot(tri, x1) + dot(tri, x2) + dot(tri, x3)


def _seg_mean(sq, bd_ref):
    hi = sq.astype(BF16)
    bd = bd_ref[...]
    pairs = [jnp.dot(hi[:, p * 128:(p + 1) * 128], bd, preferred_element_type=F32)
             for p in range(HEADS // 2)]
    return jnp.concatenate(pairs, axis=1) * (1.0 / HEAD_DIM)


def _slabs_from_heads(t):
    lane = lax.broadcasted_iota(jnp.int32, (t.shape[0], 128), 1)
    low = lane < HEAD_DIM
    slabs = []
    for p in range(HEADS // 2):
        pair = t[:, p * 128:(p + 1) * 128]
        slabs.append(jnp.where(low, pair, 0.0))
        slabs.append(jnp.where(low, pltpu.roll(pair, HEAD_DIM, 1), 0.0))
    return jnp.concatenate(slabs, axis=1)


def _dot_nt(a, b):
    return lax.dot_general(a, b, (((1,), (1,)), ((), ())), preferred_element_type=F32)


def _sigmoid(x):
    return 0.5 * jnp.tanh(0.5 * x) + 0.5


_GELU_C = math.sqrt(2.0 / math.pi)


def _gelu_and_grad(x):
    inner = _GELU_C * (x + 0.044715 * x * x * x)
    t = jnp.tanh(inner)
    y = 0.5 * x * (1.0 + t)
    dy = 0.5 * (1.0 + t) + 0.5 * x * (1.0 - t * t) * _GELU_C * (1.0 + 3.0 * 0.044715 * x * x)
    return y, dy


def _rms_bwd(xin, r, g, dy):
    dyg = dy * g
    return r * dyg - xin * (r * r * r) * jnp.mean(dyg * xin, axis=-1, keepdims=True)


def _mesh_pos():
    x, y, c = lax.axis_index("x"), lax.axis_index("y"), lax.axis_index("c")
    return x, y, c


def _peer(k):
    x, y, c = _mesh_pos()
    px = (1 - x) if (k >> 2) & 1 else x
    py = (1 - y) if (k >> 1) & 1 else y
    pc = (1 - c) if k & 1 else c
    return (px, py, pc), 4 * px + 2 * py + pc


def _frame_start(j):
    at = BLK * j - jnp.where(BLK * j > F_LO, HEADS, 0)
    return pl.multiple_of(at // 16 * 16, 16)


HALF_A = 320


def _gather_w_in(frame):
    pieces = (slice(0, HALF_A), slice(HALF_A, FRAME_ROWS))

    def body(x_ref, out_ref, zone, send_sems, recv_sems, local_sem):
        x, y, c = _mesh_pos()
        me, sibling = (x, y, c), (x, y, 1 - c)
        nbr_x, nbr_y, across = (1 - x, y, c), (x, 1 - y, c), (1 - x, 1 - y, c)

        def index(pos):
            return 4 * pos[0] + 2 * pos[1] + pos[2]

        def copy(k, block, piece, to, src=None):
            rows = pieces[piece]
            return pltpu.make_async_remote_copy(
                src_ref=(zone.at[index(block), rows] if src is None else src.at[rows]),
                dst_ref=zone.at[index(block), rows],
                send_sem=send_sems.at[k], recv_sem=recv_sems.at[k],
                device_id=to, device_id_type=pl.DeviceIdType.MESH)

        def add(block):
            j = index(block)
            rows = pl.ds(_frame_start(j), FRAME)
            out_ref[rows, :] = (out_ref[rows, :].astype(F32)
                                + zone[j, :FRAME, :].astype(F32)).astype(BF16)
            tail = slice(C_F, C_F + FRAME_ROWS - FRAME)
            forget = zone[j, FRAME:, :].astype(F32) * (j == F_DEV).astype(F32)
            out_ref[tail, :] = (out_ref[tail, :].astype(F32) + forget).astype(BF16)

        mine = pltpu.make_async_copy(x_ref, zone.at[index(me)], local_sem)
        mine.start()
        first = [copy(1, me, 0, nbr_x, src=x_ref), copy(3, me, 1, nbr_y, src=x_ref),
                 copy(2, me, 1, nbr_x, src=x_ref), copy(4, me, 0, nbr_y, src=x_ref)]
        own_to_sibling = pltpu.make_async_remote_copy(
            src_ref=x_ref, dst_ref=zone.at[index(me)], send_sem=send_sems.at[0],
            recv_sem=recv_sems.at[0], device_id=sibling, device_id_type=pl.DeviceIdType.MESH)
        for cp in first:
            cp.start()
        own_to_sibling.start()
        out_ref[...] = jnp.zeros_like(out_ref)
        mine.wait()
        add(me)

        sent = []

        def landed(k, block, piece, forward=None):
            copy(k, block, piece, me).wait_recv()
            if forward is not None:
                cp = copy(*forward)
                cp.start()
                sent.append(cp)
            cp = copy(6 + k, block, piece, sibling)
            cp.start()
            sent.append(cp)

        landed(1, nbr_x, 0, forward=(5, nbr_x, 0, nbr_y))
        landed(3, nbr_y, 1, forward=(6, nbr_y, 1, nbr_x))
        landed(2, nbr_x, 1)
        add(nbr_x)
        landed(4, nbr_y, 0)
        add(nbr_y)
        landed(5, across, 0)
        landed(6, across, 1)
        add(across)
        pltpu.make_async_remote_copy(
            src_ref=x_ref, dst_ref=zone.at[index(sibling)], send_sem=send_sems.at[0],
            recv_sem=recv_sems.at[0], device_id=sibling,
            device_id_type=pl.DeviceIdType.MESH).wait_recv()
        add(sibling)
        for k, block in ((1, nbr_x), (2, nbr_x), (3, nbr_y), (4, nbr_y), (5, across), (6, across)):
            their = (block[0], block[1], 1 - c)
            piece = {1: 0, 2: 1, 3: 1, 4: 0, 5: 0, 6: 1}[k]
            copy(6 + k, their, piece, me).wait_recv()
            if k in (2, 4, 6):
                add(their)
        for cp in first + sent:
            cp.wait_send()
        own_to_sibling.wait_send()

    return pl.pallas_call(
        body, name="gather_w_in", out_shape=jax.ShapeDtypeStruct((C_END, frame.shape[1]), BF16),
        in_specs=[pl.BlockSpec(memory_space=pl.ANY)],
        out_specs=pl.BlockSpec(memory_space=pltpu.VMEM),
        scratch_shapes=[pltpu.VMEM((N_DEV,) + frame.shape, BF16),
                        pltpu.SemaphoreType.DMA((13,)), pltpu.SemaphoreType.DMA((13,)),
                        pltpu.SemaphoreType.DMA],
        compiler_params=pltpu.CompilerParams(vmem_limit_bytes=40 * MIB),
    )(frame)


def _chip_peer(k):
    x, y, c = _mesh_pos()
    px = (1 - x) if (k >> 1) & 1 else x
    py = (1 - y) if k & 1 else y
    return (px, py, c), 2 * px + py


def _pair_sums(dw_cat, name, after):
    rows, cols = FRAME_ROWS, dw_cat.shape[1]
    n_chips = N_DEV // 2

    def pieces(p_ref, j):
        return (p_ref.at[pl.ds(_frame_start(j), FRAME)], p_ref.at[pl.ds(C_F, FRAME_ROWS - FRAME)])

    def body(p_ref, after_ref, send_ref, own_ref, mine_buf, sib_buf, send_sems, recv_sems,
             local_sems):
        x, y, c = _mesh_pos()
        sibling = (x, y, 1 - c)
        copies, local = [], []
        for q in range(n_chips):
            for part, (lo, hi) in enumerate(((0, FRAME), (FRAME, FRAME_ROWS))):
                cp = pltpu.make_async_remote_copy(
                    src_ref=pieces(p_ref, 2 * q + (1 - c))[part], dst_ref=sib_buf.at[q, lo:hi],
                    send_sem=send_sems.at[2 * q + part], recv_sem=recv_sems.at[2 * q + part],
                    device_id=sibling, device_id_type=pl.DeviceIdType.MESH)
                cp.start()
                copies.append(cp)
                lc = pltpu.make_async_copy(pieces(p_ref, 2 * q + c)[part], mine_buf.at[q, lo:hi],
                                           local_sems.at[2 * q + part])
                lc.start()
                local.append(lc)
        for lc in local:
            lc.wait()
        for cp in copies:
            cp.wait_recv()
        for k in range(1, n_chips):
            _, q = _chip_peer(k)
            send_ref[k - 1] = (mine_buf[q].astype(F32) + sib_buf[q].astype(F32)).astype(BF16)
        my_chip = 2 * x + y
        own_ref[...] = mine_buf[my_chip].astype(F32) + sib_buf[my_chip].astype(F32)
        for cp in copies:
            cp.wait_send()

    vmem = pl.BlockSpec(memory_space=pltpu.VMEM)
    return pl.pallas_call(
        body, name=name,
        out_shape=[jax.ShapeDtypeStruct((n_chips - 1, rows, cols), BF16),
                   jax.ShapeDtypeStruct((rows, cols), F32)],
        in_specs=[pl.BlockSpec(memory_space=pl.ANY)] * 2, out_specs=[vmem, vmem],
        scratch_shapes=[pltpu.VMEM((n_chips, rows, cols), BF16),
                        pltpu.VMEM((n_chips, rows, cols), BF16),
                        pltpu.SemaphoreType.DMA((2 * n_chips,)),
                        pltpu.SemaphoreType.DMA((2 * n_chips,)),
                        pltpu.SemaphoreType.DMA((2 * n_chips,))],
        compiler_params=pltpu.CompilerParams(vmem_limit_bytes=40 * MIB),
    )(dw_cat, after)


def _chip_copy(src_ref, land_ref, send_sem, recv_sem, k):
    peer, _ = _chip_peer(k)
    return pltpu.make_async_remote_copy(
        src_ref=src_ref.at[k - 1], dst_ref=land_ref.at[k - 1], send_sem=send_sem, recv_sem=recv_sem,
        device_id=peer, device_id_type=pl.DeviceIdType.MESH)


def _chip_exchange_start(blocks, name):
    hbm = pl.BlockSpec(memory_space=pltpu.HBM)
    sem = pl.BlockSpec(memory_space=pltpu.SEMAPHORE)
    n_peers = blocks.shape[0]

    def body(src_ref, zone_ref, send_sems, recv_sems, src_thru, zone_thru, token):
        for k in range(1, n_peers + 1):
            _chip_copy(src_ref, zone_ref, send_sems.at[k - 1], recv_sems.at[k - 1], k).start()
        token[...] = jnp.zeros_like(token)

    outs = pl.pallas_call(
        body, name=name, in_specs=[hbm, hbm],
        out_shape=[pltpu.SemaphoreType.DMA((n_peers,)), pltpu.SemaphoreType.DMA((n_peers,)),
                   pltpu.HBM(blocks.shape, blocks.dtype), pltpu.HBM(blocks.shape, blocks.dtype),
                   jax.ShapeDtypeStruct((8, 128), F32)],
        out_specs=[sem, sem, hbm, hbm, pl.BlockSpec(memory_space=pltpu.VMEM)],
        input_output_aliases={0: 2, 1: 3},
        compiler_params=pltpu.CompilerParams(
            has_side_effects=pltpu.SideEffectType.DATAFLOW_SIDE_EFFECTING),
    )(pltpu.with_memory_space_constraint(blocks, pltpu.HBM),
      pltpu.with_memory_space_constraint(lax.empty(blocks.shape, blocks.dtype), pltpu.HBM))
    return outs[:4], outs[4]


def _chip_exchange_wait(handle, after, name):
    send_sems, recv_sems, src, zone = handle
    hbm = pl.BlockSpec(memory_space=pltpu.HBM)
    sem = pl.BlockSpec(memory_space=pltpu.SEMAPHORE)

    def body(src_ref, zone_ref, ssem, rsem, after_ref, src_out, zone_out):
        for k in range(1, src.shape[0] + 1):
            cp = _chip_copy(src_ref, zone_ref, ssem.at[k - 1], rsem.at[k - 1], k)
            cp.wait_send()
            cp.wait_recv()

    outs = pl.pallas_call(
        body, name=name,
        in_specs=[hbm, hbm, sem, sem, pl.BlockSpec(memory_space=pl.ANY)],
        out_shape=[pltpu.HBM(src.shape, src.dtype), pltpu.HBM(zone.shape, zone.dtype)],
        out_specs=[hbm, hbm], input_output_aliases={0: 0, 1: 1},
        compiler_params=pltpu.CompilerParams(
            has_side_effects=pltpu.SideEffectType.DATAFLOW_SIDE_EFFECTING),
    )(src, zone, send_sems, recv_sems, after)
    return outs[1]


def _remote_copy(gather, src_ref, land_ref, send_sem, recv_sem, k, receive_side):
    x, y, c = _mesh_pos()
    me = 4 * x + 2 * y + c
    peer, pidx = _peer(k)
    return pltpu.make_async_remote_copy(
        src_ref=src_ref if gather else src_ref.at[pidx],
        dst_ref=land_ref.at[pidx if receive_side else me],
        send_sem=send_sem, recv_sem=recv_sem,
        device_id=peer, device_id_type=pl.DeviceIdType.MESH)


def _exchange_start(groups, name, gather):
    arrs = [a for g in groups for a in g]
    n, n_groups = len(arrs), len(groups)
    lands = [jax.ShapeDtypeStruct(((N_DEV,) + a.shape) if gather else a.shape, a.dtype)
             for a in arrs]

    def body(*refs):
        srcs, zones = refs[:n], refs[n:2 * n]
        sems = refs[2 * n:2 * n + 2 * n_groups]
        token = refs[-1]
        a = 0
        for gi, g in enumerate(groups):
            send_sems, recv_sems = sems[2 * gi], sems[2 * gi + 1]
            for k in range(1, N_DEV):
                for ai in range(len(g)):
                    slot = ai * (N_DEV - 1) + k - 1
                    _remote_copy(gather, srcs[a + ai], zones[a + ai], send_sems.at[slot],
                                 recv_sems.at[slot], k, False).start()
            a += len(g)
        token[...] = jnp.zeros_like(token)

    hbm = pl.BlockSpec(memory_space=pltpu.HBM)
    sem = pl.BlockSpec(memory_space=pltpu.SEMAPHORE)
    sem_shapes = []
    for g in groups:
        sem_shapes += [pltpu.SemaphoreType.DMA((len(g) * (N_DEV - 1),))] * 2
    outs = pl.pallas_call(
        body, name=name,
        in_specs=[hbm] * (2 * n),
        out_shape=sem_shapes + [pltpu.HBM(a.shape, a.dtype) for a in arrs]
        + [pltpu.HBM(z.shape, z.dtype) for z in lands] + [jax.ShapeDtypeStruct((8, 128), F32)],
        out_specs=[sem] * (2 * n_groups) + [hbm] * (2 * n)
        + [pl.BlockSpec(memory_space=pltpu.VMEM)],
        input_output_aliases={i: 2 * n_groups + i for i in range(2 * n)},
        compiler_params=pltpu.CompilerParams(
            has_side_effects=pltpu.SideEffectType.DATAFLOW_SIDE_EFFECTING),
    )(*[pltpu.with_memory_space_constraint(a, pltpu.HBM) for a in arrs],
      *[pltpu.with_memory_space_constraint(lax.empty(z.shape, z.dtype), pltpu.HBM) for z in lands])
    sems = outs[:2 * n_groups]
    thru = outs[2 * n_groups:2 * n_groups + n]
    zones = outs[2 * n_groups + n:2 * n_groups + 2 * n]
    handles, a = [], 0
    for gi, g in enumerate(groups):
        handles.append((sems[2 * gi], sems[2 * gi + 1], thru[a:a + len(g)], zones[a:a + len(g)]))
        a += len(g)
    return handles, outs[-1]


def _exchange_wait(handle, after, name, gather):
    send_sems, recv_sems, thru, zones = handle
    n = len(thru)

    def body(*refs):
        srcs, lands = refs[:n], refs[n:2 * n]
        ssem, rsem = refs[2 * n], refs[2 * n + 1]
        for k in range(1, N_DEV):
            for ai in range(n):
                slot = ai * (N_DEV - 1) + k - 1
                cp = _remote_copy(gather, srcs[ai], lands[ai], ssem.at[slot], rsem.at[slot], k, True)
                cp.wait_send()
                cp.wait_recv()

    hbm = pl.BlockSpec(memory_space=pltpu.HBM)
    sem = pl.BlockSpec(memory_space=pltpu.SEMAPHORE)
    outs = pl.pallas_call(
        body, name=name,
        in_specs=[hbm] * (2 * n) + [sem, sem, pl.BlockSpec(memory_space=pl.ANY)],
        out_shape=[pltpu.HBM(a.shape, a.dtype) for a in thru]
        + [pltpu.HBM(z.shape, z.dtype) for z in zones],
        out_specs=[hbm] * (2 * n),
        input_output_aliases={i: i for i in range(2 * n)},
        compiler_params=pltpu.CompilerParams(
            has_side_effects=pltpu.SideEffectType.DATAFLOW_SIDE_EFFECTING),
    )(*thru, *zones, send_sems, recv_sems, after)
    return outs[:n], outs[n:]


def _own_block(zone, block):
    x, y, c = _mesh_pos()
    me = 4 * x + 2 * y + c
    return lax.dynamic_update_slice_in_dim(zone, block[None], me, axis=0)


def _proj_fwd(x, g1, wcat, bdiag, gq, gk, bfor, tri, pdq, pdk, ones_q, ones_k):
    s_len = x.shape[0]
    tm = TOKEN_TILE
    nt = s_len // tm

    def body(x_ref, g1_ref, w_ref, bd_ref, gq_ref, gk_ref, bf_ref, tri_ref, pdq_ref,
             pdk_ref, oq_ref, ok_ref,
             h_ref, qa_ref, ka_ref, kat_ref, vs_ref, vt_ref, qr_ref, kr_ref, flog_ref, uv_ref,
             gp_ref, carry):
        @pl.when(pl.program_id(0) == 0)
        def _():
            carry[...] = jnp.zeros_like(carry)

        xf = x_ref[...]
        r = lax.rsqrt(jnp.mean(xf * xf, axis=-1, keepdims=True) + EPS)
        h = (xf * r * g1_ref[...]).astype(BF16)
        h_ref[...] = h
        dot = functools.partial(jnp.dot, preferred_element_type=F32)

        def proj(lo, hi):
            return _dot_nt(h, w_ref[lo:hi, :])

        flog = proj(C_F, C_END) + bf_ref[...]
        flog_ref[...] = flog
        lane = lax.broadcasted_iota(jnp.int32, flog.shape, 1)
        logf = jnp.minimum(flog, 0.0) - jnp.log(1.0 + jnp.exp(-jnp.abs(flog)))
        logf = jnp.where(lane < HEADS, logf, 0.0)
        dcum = _tri_dot(tri_ref[...], logf) + carry[...]
        carry[...] = dcum[tm - 1:tm, :]
        d2 = dcum * LOG2E
        d2a = d2.astype(BF16)
        rem = d2 - d2a.astype(F32)
        d2b = rem.astype(BF16)
        d2c = (rem - d2b.astype(F32)).astype(BF16)

        q = proj(C_Q, C_K)
        qr_ref[...] = q.astype(BF16)
        rq = lax.rsqrt(_seg_mean(q * q, bd_ref) + EPS)
        qn = q * rq * (gq_ref[...] * (HEAD_DIM ** -0.5 * LOG2E))
        d_parts = (d2a.astype(F32) + pltpu.roll(d2b.astype(F32), HEADS, 1)
                   + pltpu.roll(d2c.astype(F32), 2 * HEADS, 1)).astype(BF16)
        qa = _slabs_from_heads(qn) + dot(d_parts, pdq_ref[...]) + oq_ref[...]
        qa_ref[...] = qa.astype(BF16)

        k = proj(C_K, C_V)
        kr_ref[...] = k.astype(BF16)
        rk = lax.rsqrt(_seg_mean(k * k, bd_ref) + EPS)
        kn = k * rk * gk_ref[...]
        ka = _slabs_from_heads(kn) + dot(d_parts, pdk_ref[...]) + ok_ref[...]
        ka_ref[...] = ka.astype(BF16)
        kat_ref[0] = ka.T.astype(BF16)

        v = proj(C_V, C_UV)
        vs_ref[...] = _slabs_from_heads(v).astype(BF16)
        vt_ref[0] = v.T.astype(BF16)
        uv_ref[...] = proj(C_UV, C_G).astype(BF16)
        gp_ref[...] = proj(C_G, C_F).astype(BF16)

    outs = [((s_len, D_MODEL), BF16, _row_spec(tm, D_MODEL)),
            ((s_len, SLAB_W), BF16, _row_spec(tm, SLAB_W)),
            ((s_len, SLAB_W), BF16, _row_spec(tm, SLAB_W)),
            ((nt, SLAB_W, tm), BF16, _tile_spec(SLAB_W, tm)),
            ((s_len, SLAB_W), BF16, _row_spec(tm, SLAB_W)),
            ((nt, FOX_W, tm), BF16, _tile_spec(FOX_W, tm)),
            ((s_len, FOX_W), BF16, _row_spec(tm, FOX_W)),
            ((s_len, FOX_W), BF16, _row_spec(tm, FOX_W)),
            ((s_len, 128), F32, _row_spec(tm, 128)),
            ((s_len, 2 * SGU_W), BF16, _row_spec(tm, 2 * SGU_W)),
            ((s_len, 2 * D_MODEL), BF16, _row_spec(tm, 2 * D_MODEL))]
    return pl.pallas_call(
        body, name="proj_fwd", grid=(nt,),
        in_specs=[_row_spec(tm, D_MODEL), _const_spec((1, D_MODEL)), _const_spec(wcat.shape),
                  _const_spec(bdiag.shape), _const_spec((1, FOX_W)), _const_spec((1, FOX_W)),
                  _const_spec((1, 128)), _const_spec((tm, tm)), _const_spec(pdq.shape), _const_spec(pdk.shape), _const_spec(ones_q.shape),
                  _const_spec(ones_k.shape)],
        out_specs=[o[2] for o in outs],
        out_shape=[jax.ShapeDtypeStruct(o[0], o[1]) for o in outs],
        scratch_shapes=[pltpu.VMEM((1, 128), F32)],
        compiler_params=_params(56, 1),
    )(x, g1, wcat, bdiag, gq, gk, bfor, tri, pdq, pdk, ones_q, ones_k)


def _attn_fwd(qa, ka, vt):
    s_len = qa.shape[0]
    t = ATTN_TILE
    nb = s_len // t

    def body(q_ref, k_ref, vt_ref, o_ref, ot_ref, lse_ref, m_sc, l_sc, acc_sc, s_sc, mcur_sc,
             alpha_sc):
        i = pl.program_id(0)
        m_sc[...] = jnp.full_like(m_sc, -jnp.inf)
        l_sc[...] = jnp.zeros_like(l_sc)
        acc_sc[...] = jnp.zeros_like(acc_sc)

        def logits(j, slot, masked):
            krows = pl.ds(pl.multiple_of(j * t, t), t)
            if masked:
                keep = (lax.broadcasted_iota(jnp.int32, (t, t), 0)
                        <= lax.broadcasted_iota(jnp.int32, (t, t), 1))
            for hd in range(HEADS):
                sl = slice(hd * 128, (hd + 1) * 128)
                st = _dot_nt(k_ref[krows, sl], q_ref[:, sl])
                if masked:
                    st = jnp.where(keep, st, -jnp.inf)
                s_sc[slot, hd] = st
                m_prev = m_sc[hd:hd + 1, :]
                m_new = jnp.maximum(m_prev, jnp.max(st, axis=0, keepdims=True))
                alpha_sc[slot, hd:hd + 1, :] = jnp.exp2(m_prev - m_new)
                mcur_sc[slot, hd:hd + 1, :] = m_new
                m_sc[hd:hd + 1, :] = m_new

        def accumulate(j, slot):
            for hd in range(HEADS):
                hr = slice(hd * HEAD_DIM, (hd + 1) * HEAD_DIM)
                alpha = alpha_sc[slot, hd:hd + 1, :]
                pt = jnp.exp2(s_sc[slot, hd] - mcur_sc[slot, hd:hd + 1, :])
                l_sc[hd:hd + 1, :] = alpha * l_sc[hd:hd + 1, :] + jnp.sum(pt, axis=0, keepdims=True)
                acc_sc[hr, :] = alpha * acc_sc[hr, :] + jnp.dot(
                    vt_ref[j, hr, :], pt.astype(BF16), preferred_element_type=F32)

        @pl.when(i == 0)
        def _():
            logits(0, 0, True)
            accumulate(0, 0)

        pairs = (i - 1) // 2

        @pl.when(i > 0)
        def _():
            logits(0, 0, False)

            def two_blocks(p, carry):
                logits(2 * p + 1, 1, False)
                accumulate(2 * p, 0)
                logits(2 * p + 2, 0, False)
                accumulate(2 * p + 1, 1)
                return carry

            lax.fori_loop(0, pairs, two_blocks, 0)

        @pl.when((i > 0) & (i - 2 * pairs == 1))
        def _():
            logits(i, 1, True)
            accumulate(i - 1, 0)
            accumulate(i, 1)

        @pl.when((i > 0) & (i - 2 * pairs == 2))
        def _():
            logits(i - 1, 1, False)
            accumulate(i - 2, 0)
            logits(i, 0, True)
            accumulate(i - 1, 1)
            accumulate(i, 0)

        for hd in range(HEADS):
            hr = slice(hd * HEAD_DIM, (hd + 1) * HEAD_DIM)
            l = l_sc[hd:hd + 1, :]
            acc_sc[hr, :] = acc_sc[hr, :] / l
            lse_ref[0, hd:hd + 1, :] = m_sc[hd:hd + 1, :] + jnp.log2(l)
        o_ref[...] = acc_sc[...].T.astype(BF16)
        ot_ref[...] = acc_sc[...].astype(BF16)

    return pl.pallas_call(
        body, name="attn_fwd", grid=(nb,),
        in_specs=[_row_spec(t, SLAB_W), _const_spec(ka.shape), _const_spec(vt.shape)],
        out_specs=[_row_spec(t, FOX_W), pl.BlockSpec((FOX_W, t), lambda i: (0, i)),
                   _tile_spec(HEADS, t)],
        out_shape=[jax.ShapeDtypeStruct((s_len, FOX_W), BF16),
                   jax.ShapeDtypeStruct((FOX_W, s_len), BF16),
                   jax.ShapeDtypeStruct((nb, HEADS, t), F32)],
        scratch_shapes=[pltpu.VMEM((HEADS, t), F32), pltpu.VMEM((HEADS, t), F32),
                        pltpu.VMEM((FOX_W, t), F32), pltpu.VMEM((2, HEADS, t, t), F32),
                        pltpu.VMEM((2, HEADS, t), F32), pltpu.VMEM((2, HEADS, t), F32)],
        compiler_params=_params(48, 1),
    )(qa, ka, vt)


def _sgu_mix(vn, ws_ref):
    tm = vn.shape[0]
    lane = lax.broadcasted_iota(jnp.int32, (WINDOW, 128), 1)
    low = lane < HEAD_DIM
    wins = []
    for w in range(tm // WINDOW):
        slabs = []
        for p in range(GROUPS // 2):
            v2 = vn[w * WINDOW:(w + 1) * WINDOW, p * 128:(p + 1) * 128]
            lo = jnp.where(low, v2, 0.0).astype(BF16)
            hi = jnp.where(low, 0.0, v2).astype(BF16)
            slabs.append(jnp.dot(ws_ref[2 * p], lo, preferred_element_type=F32)
                         + jnp.dot(ws_ref[2 * p + 1], hi, preferred_element_type=F32))
        wins.append(jnp.concatenate(slabs, axis=1))
    return jnp.concatenate(wins, axis=0) if len(wins) > 1 else wins[0]


def _layernorm_fwd(vv, g, b):
    mu = jnp.mean(vv, axis=-1, keepdims=True)
    xc = vv - mu
    r = lax.rsqrt(jnp.mean(xc * xc, axis=-1, keepdims=True) + EPS)
    xh = xc * r
    return xh * g + b, xh, r


def _mix_fwd(attn, uvpre, gpre, x, wa, wb, wout, wsm, bsf, gsgu, bsgu, gpost):
    s_len = x.shape[0]
    tm = TOKEN_TILE

    def body(o_ref, uv_ref, gp_ref, x_ref, wa_ref, wb_ref, wo_ref, ws_ref, bs_ref, gs_ref, bsg_ref,
             gpost_ref, sgut_ref, ya_ref, yb_ref, mgt_ref, om_ref, x1_ref):
        uvp = uv_ref[...].astype(F32)
        uv, _ = _gelu_and_grad(uvp)
        u, vv = uv[:, :SGU_W], uv[:, SGU_W:]
        vn, _, _ = _layernorm_fwd(vv, gs_ref[...], bsg_ref[...])
        bias = bs_ref[...]
        if tm > WINDOW:
            bias = jnp.concatenate([bias] * (tm // WINDOW), axis=0)
        mixed = _sgu_mix(vn, ws_ref) + bias
        sgu_f = u * mixed
        sgu = sgu_f.astype(BF16)
        sgut_ref[...] = sgu_f.T.astype(BF16)
        ya = jnp.dot(o_ref[...], wa_ref[...], preferred_element_type=F32)
        yb = jnp.dot(sgu, wb_ref[...], preferred_element_type=F32)
        ya_ref[...] = ya.astype(BF16)
        yb_ref[...] = yb.astype(BF16)
        gates = _sigmoid(gp_ref[...].astype(F32))
        merged_f = gates[:, :D_MODEL] * ya + gates[:, D_MODEL:] * yb
        merged = merged_f.astype(BF16)
        mgt_ref[...] = merged_f.T.astype(BF16)
        om = jnp.dot(merged, wo_ref[...], preferred_element_type=F32)
        om_ref[...] = om
        r = lax.rsqrt(jnp.mean(om * om, axis=-1, keepdims=True) + EPS)
        x1_ref[...] = x_ref[...] + om * r * gpost_ref[...]

    def t_out(rows):
        return ((rows, s_len), BF16, pl.BlockSpec((rows, tm), lambda i: (0, i)))

    def r_out(cols, dt):
        return ((s_len, cols), dt, _row_spec(tm, cols))

    outs = [t_out(SGU_W), r_out(D_MODEL, BF16), r_out(D_MODEL, BF16), t_out(D_MODEL),
            r_out(D_MODEL, F32), r_out(D_MODEL, F32)]
    return pl.pallas_call(
        body, name="mix_fwd", grid=(s_len // tm,),
        in_specs=[_row_spec(tm, FOX_W), _row_spec(tm, 2 * SGU_W), _row_spec(tm, 2 * D_MODEL),
                  _row_spec(tm, D_MODEL), _const_spec(wa.shape), _const_spec(wb.shape),
                  _const_spec(wout.shape), _const_spec(wsm.shape), _const_spec(bsf.shape),
                  _const_spec((1, SGU_W)), _const_spec((1, SGU_W)), _const_spec((1, D_MODEL))],
        out_specs=[o[2] for o in outs],
        out_shape=[jax.ShapeDtypeStruct(o[0], o[1]) for o in outs],
        compiler_params=_params(48, 1),
    )(attn, uvpre, gpre, x, wa, wb, wout, wsm, bsf, gsgu, bsgu, gpost)


def _ffn_fwd_bwd(x1, tgt, wffn, wdown, gpre, gpost):
    s_len = x1.shape[0]
    tm = TOKEN_TILE

    def body(x1_ref, t_ref, wi_ref, wd_ref, gpre_ref, gpost_ref,
             dx1_ref, h2_ref, actt_ref, dff_ref, dgut_ref, loss_ref, dgpost_ref, dgpre_ref):
        @pl.when(pl.program_id(0) == 0)
        def _():
            loss_ref[...] = jnp.zeros_like(loss_ref)
            dgpost_ref[...] = jnp.zeros_like(dgpost_ref)
            dgpre_ref[...] = jnp.zeros_like(dgpre_ref)

        x1v = x1_ref[...]
        r2 = lax.rsqrt(jnp.mean(x1v * x1v, axis=-1, keepdims=True) + EPS)
        gpre_v = gpre_ref[...]
        h2 = (x1v * r2 * gpre_v).astype(BF16)
        h2_ref[...] = h2
        gg = _dot_nt(h2, wi_ref[:D_FF, :])
        uu = _dot_nt(h2, wi_ref[D_FF:, :])
        sg = _sigmoid(gg)
        silu = gg * sg
        act_f = silu * uu
        act = act_f.astype(BF16)
        actt_ref[...] = act_f.T.astype(BF16)
        ff = jnp.dot(act, wd_ref[...], preferred_element_type=F32)
        r3 = lax.rsqrt(jnp.mean(ff * ff, axis=-1, keepdims=True) + EPS)
        gpost_v = gpost_ref[...]
        y = x1v + ff * r3 * gpost_v
        err = y - t_ref[...]
        loss_ref[...] += jnp.sum(err * err) * (0.5 / D_MODEL)
        dy = err * (1.0 / D_MODEL)
        dgpost_ref[...] += jnp.sum(dy * ff * r3, axis=0, keepdims=True)
        dff = _rms_bwd(ff, r3, gpost_v, dy).astype(BF16)
        dff_ref[...] = dff
        dact = _dot_nt(dff, wd_ref[...])
        dgg_f = dact * uu * (sg * (1.0 + gg * (1.0 - sg)))
        duu_f = dact * silu
        dgg = dgg_f.astype(BF16)
        duu = duu_f.astype(BF16)
        dgut_ref[:D_FF, :] = dgg_f.T.astype(BF16)
        dgut_ref[D_FF:, :] = duu_f.T.astype(BF16)
        dh2 = (jnp.dot(dgg, wi_ref[:D_FF, :], preferred_element_type=F32)
               + jnp.dot(duu, wi_ref[D_FF:, :], preferred_element_type=F32))
        dgpre_ref[...] += jnp.sum(dh2 * x1v * r2, axis=0, keepdims=True)
        dx1_ref[...] = dy + _rms_bwd(x1v, r2, gpre_v, dh2)

    outs = [((s_len, D_MODEL), F32, _row_spec(tm, D_MODEL)),
            ((s_len, D_MODEL), BF16, _row_spec(tm, D_MODEL)),
            ((D_FF, s_len), BF16, pl.BlockSpec((D_FF, tm), lambda i: (0, i))),
            ((s_len, D_MODEL), BF16, _row_spec(tm, D_MODEL)),
            ((2 * D_FF, s_len), BF16, pl.BlockSpec((2 * D_FF, tm), lambda i: (0, i))),
            ((1, 128), F32, _const_spec((1, 128))),
            ((1, D_MODEL), F32, _const_spec((1, D_MODEL))),
            ((1, D_MODEL), F32, _const_spec((1, D_MODEL)))]
    return pl.pallas_call(
        body, name="ffn_fwd_bwd", grid=(s_len // tm,),
        in_specs=[_row_spec(tm, D_MODEL), _row_spec(tm, D_MODEL), _const_spec(wffn.shape),
                  _const_spec(wdown.shape), _const_spec((1, D_MODEL)), _const_spec((1, D_MODEL))],
        out_specs=[o[2] for o in outs],
        out_shape=[jax.ShapeDtypeStruct(o[0], o[1]) for o in outs],
        compiler_params=_params(60, 1),
    )(x1, tgt, wffn, wdown, gpre, gpost)


def _mix_bwd(dx1, om, ya, yb, gpre, uvpre, attn, attn_t, sgu_t, wout, wa, wb, wsm, wsmt, bsf, gsgu,
             bsgu, gpost, wmask, egrp):
    s_len = dx1.shape[0]
    tm = TOKEN_TILE
    nw = tm // WINDOW
    nt = s_len // tm

    def body(dx1_ref, om_ref, ya_ref, yb_ref, gp_ref, uv_ref, o_ref, at_ref, st_ref, wo_ref, wa_ref,
             wb_ref, ws_ref, wst_ref, bs_ref, gs_ref, bsg_ref, gpost_ref, mask_ref, eg_ref,
             dom_ref, dgp_ref, dot_ref, delta_ref, duv_ref,
             dws_ref, dbs_ref, dgs_ref, dbsg_ref, dgpost_ref, dwa_ref, dwb_ref,
             dbs_acc, dwa_acc, dwb_acc):
        step = pl.program_id(0)

        @pl.when(step == 0)
        def _():
            dws_ref[...] = jnp.zeros_like(dws_ref)
            dwa_acc[...] = jnp.zeros_like(dwa_acc)
            dwb_acc[...] = jnp.zeros_like(dwb_acc)
            dbs_acc[...] = jnp.zeros_like(dbs_acc)
            dgs_ref[...] = jnp.zeros_like(dgs_ref)
            dbsg_ref[...] = jnp.zeros_like(dbsg_ref)
            dgpost_ref[...] = jnp.zeros_like(dgpost_ref)

        om = om_ref[...]
        dx1v = dx1_ref[...]
        r = lax.rsqrt(jnp.mean(om * om, axis=-1, keepdims=True) + EPS)
        gpost_v = gpost_ref[...]
        dgpost_ref[...] += jnp.sum(dx1v * om * r, axis=0, keepdims=True)
        dom = _rms_bwd(om, r, gpost_v, dx1v).astype(BF16)
        dom_ref[...] = dom
        dmg = _dot_nt(dom, wo_ref[...])

        gates = _sigmoid(gp_ref[...].astype(F32))
        ga, gb = gates[:, :D_MODEL], gates[:, D_MODEL:]
        yav, ybv = ya_ref[...].astype(F32), yb_ref[...].astype(F32)
        dya = (dmg * ga).astype(BF16)
        dyb = (dmg * gb).astype(BF16)
        dwa_acc[...] += jnp.dot(at_ref[...], dya, preferred_element_type=F32)
        dwb_acc[...] += jnp.dot(st_ref[...], dyb, preferred_element_type=F32)
        dgp_ref[:, :D_MODEL] = (dmg * yav * ga * (1.0 - ga)).astype(BF16)
        dgp_ref[:, D_MODEL:] = (dmg * ybv * gb * (1.0 - gb)).astype(BF16)

        dat_t = _dot_nt(dya, wa_ref[...]).T.astype(BF16)
        dot_ref[0] = dat_t
        o_t = o_ref[...].astype(F32).T
        delta_ref[0] = jnp.sum((dat_t.astype(F32) * o_t).reshape(HEADS, HEAD_DIM, tm), axis=1)
        dsgu = _dot_nt(dyb, wb_ref[...])

        uvp = uv_ref[...].astype(F32)
        uv, guv = _gelu_and_grad(uvp)
        u, vv = uv[:, :SGU_W], uv[:, SGU_W:]
        gs_v = gs_ref[...]
        vn, xh, rln = _layernorm_fwd(vv, gs_v, bsg_ref[...])
        bias = bs_ref[...]
        if nw > 1:
            bias = jnp.concatenate([bias] * nw, axis=0)
        mixed = _sgu_mix(vn, ws_ref) + bias
        du = dsgu * mixed
        dmixed = dsgu * u

        lane = lax.broadcasted_iota(jnp.int32, (WINDOW, 128), 1)
        low = lane < HEAD_DIM
        dvn_wins = []
        for w in range(nw):
            rows = slice(w * WINDOW, (w + 1) * WINDOW)
            dbs_acc[...] += dmixed[rows, :]
            slabs = []
            for p in range(GROUPS // 2):
                cols = slice(p * 128, (p + 1) * 128)
                dm2 = dmixed[rows, cols]
                dlo = jnp.where(low, dm2, 0.0).astype(BF16)
                dhi = jnp.where(low, 0.0, dm2).astype(BF16)
                vn2 = vn[rows, cols].astype(BF16)
                dws_ref[2 * p] += _dot_nt(dlo, vn2)
                dws_ref[2 * p + 1] += _dot_nt(dhi, vn2)
                slabs.append(jnp.dot(wst_ref[2 * p], dlo, preferred_element_type=F32)
                             + jnp.dot(wst_ref[2 * p + 1], dhi, preferred_element_type=F32))
            dvn_wins.append(jnp.concatenate(slabs, axis=1))
        dvn = jnp.concatenate(dvn_wins, axis=0) if nw > 1 else dvn_wins[0]

        dgs_ref[...] += jnp.sum(dvn * xh, axis=0, keepdims=True)
        dbsg_ref[...] += jnp.sum(dvn, axis=0, keepdims=True)
        dxh = dvn * gs_v
        dvv = rln * (dxh - jnp.mean(dxh, axis=-1, keepdims=True)
                     - xh * jnp.mean(dxh * xh, axis=-1, keepdims=True))
        duv_ref[:, :SGU_W] = (du * guv[:, :SGU_W]).astype(BF16)
        duv_ref[:, SGU_W:] = (dvv * guv[:, SGU_W:]).astype(BF16)

        @pl.when(step == pl.num_programs(0) - 1)
        def _():
            for g in range(GROUPS):
                dws_ref[g] = dws_ref[g] * mask_ref[...]
            dbs_ref[...] = _split3_dot(dbs_acc[...], eg_ref[...])
            dwa_ref[...] = dwa_acc[...].astype(BF16)
            dwb_ref[...] = dwb_acc[...].astype(BF16)

    rows_out = [((s_len, D_MODEL), BF16, _row_spec(tm, D_MODEL)),
                ((s_len, 2 * D_MODEL), BF16, _row_spec(tm, 2 * D_MODEL)),
                ((nt, FOX_W, tm), BF16, _tile_spec(FOX_W, tm)),
                ((nt, HEADS, tm), F32, _tile_spec(HEADS, tm)),
                ((s_len, 2 * SGU_W), BF16, _row_spec(tm, 2 * SGU_W))]
    acc_out = [((GROUPS, WINDOW, WINDOW), F32), ((WINDOW, 128), F32), ((1, SGU_W), F32),
               ((1, SGU_W), F32), ((1, D_MODEL), F32), ((FOX_W, D_MODEL), BF16),
               ((SGU_W, D_MODEL), BF16)]
    return pl.pallas_call(
        body, name="mix_bwd", grid=(nt,),
        in_specs=[_row_spec(tm, D_MODEL), _row_spec(tm, D_MODEL), _row_spec(tm, D_MODEL),
                  _row_spec(tm, D_MODEL), _row_spec(tm, 2 * D_MODEL), _row_spec(tm, 2 * SGU_W),
                  _row_spec(tm, FOX_W), pl.BlockSpec((FOX_W, tm), lambda i: (0, i)),
                  pl.BlockSpec((SGU_W, tm), lambda i: (0, i)), _const_spec(wout.shape),
                  _const_spec(wa.shape),
                  _const_spec(wb.shape), _const_spec(wsm.shape), _const_spec(wsmt.shape),
                  _const_spec(bsf.shape), _const_spec((1, SGU_W)), _const_spec((1, SGU_W)),
                  _const_spec((1, D_MODEL)), _const_spec(wmask.shape), _const_spec(egrp.shape)],
        out_specs=[o[2] for o in rows_out] + [_const_spec(s) for s, _ in acc_out],
        out_shape=[jax.ShapeDtypeStruct(o[0], o[1]) for o in rows_out]
        + [jax.ShapeDtypeStruct(s, dt) for s, dt in acc_out],
        scratch_shapes=[pltpu.VMEM((WINDOW, SGU_W), F32), pltpu.VMEM((FOX_W, D_MODEL), F32),
                        pltpu.VMEM((SGU_W, D_MODEL), F32)],
        compiler_params=_params(56, 1),
    )(dx1, om, ya, yb, gpre, uvpre, attn, attn_t, sgu_t, wout, wa, wb, wsm, wsmt, bsf, gsgu, bsgu,
      gpost, wmask, egrp)


def _attn_bwd(qa, ka, kat, vs, dot_, lse, delta, ecol):
    s_len = qa.shape[0]
    t = ATTN_TILE
    nb = s_len // t

    def body(k_ref, kt_ref, vs_ref, q_ref, do_ref, lse_ref, dl_ref, ec_ref, gk_ref, dvt_ref,
             gqt_ref, csum_ref, p_sc, ds_sc):
        j = pl.program_id(0)

        @pl.when(j == 0)
        def _():
            gqt_ref[...] = jnp.zeros_like(gqt_ref)

        gk_ref[...] = jnp.zeros_like(gk_ref)
        dvt_ref[...] = jnp.zeros_like(dvt_ref)

        def probs(i, slot, masked):
            qrows = pl.ds(pl.multiple_of(i * t, t), t)
            if masked:
                keep = (lax.broadcasted_iota(jnp.int32, (t, t), 0)
                        <= lax.broadcasted_iota(jnp.int32, (t, t), 1))
            for hd in range(HEADS):
                sl = slice(hd * 128, (hd + 1) * 128)
                hr = slice(hd * HEAD_DIM, (hd + 1) * HEAD_DIM)
                st = _dot_nt(k_ref[:, sl], q_ref[qrows, sl])
                if masked:
                    st = jnp.where(keep, st, -jnp.inf)
                pt = jnp.exp2(st - lse_ref[i, hd:hd + 1, :])
                dpt = jnp.dot(vs_ref[:, hd * 128:hd * 128 + HEAD_DIM], do_ref[i, hr, :],
                              preferred_element_type=F32)
                p_sc[slot, hd] = pt.astype(BF16)
                ds_sc[slot, hd] = (pt * (dpt - dl_ref[i, hd:hd + 1, :])).astype(BF16)

        def grads(i, slot):
            qrows = pl.ds(pl.multiple_of(i * t, t), t)
            for hd in range(HEADS):
                sl = slice(hd * 128, (hd + 1) * 128)
                hr = slice(hd * HEAD_DIM, (hd + 1) * HEAD_DIM)
                dst = ds_sc[slot, hd]
                dvt_ref[0, hr, :] += _dot_nt(do_ref[i, hr, :], p_sc[slot, hd])
                gk_ref[:, sl] += jnp.dot(dst, q_ref[qrows, sl], preferred_element_type=F32)
                gqt_ref[i, hd * QT_ROWS:(hd + 1) * QT_ROWS, :] += jnp.dot(
                    kt_ref[0, hd * 128:hd * 128 + QT_ROWS, :], dst, preferred_element_type=F32)

        probs(j, 0, True)
        pairs = (nb - 1 - j) // 2

        def two_blocks(p, carry):
            i1 = j + 1 + 2 * p
            probs(i1, 1, False)
            grads(i1 - 1, 0)
            probs(i1 + 1, 0, False)
            grads(i1, 1)
            return carry

        lax.fori_loop(0, pairs, two_blocks, 0)

        @pl.when(nb - 1 - j - 2 * pairs == 0)
        def _():
            grads(nb - 1, 0)

        @pl.when(nb - 1 - j - 2 * pairs == 1)
        def _():
            probs(nb - 1, 1, False)
            grads(nb - 2, 0)
            grads(nb - 1, 1)

        csum_ref[...] = _split3_dot(gk_ref[...], ec_ref[...])

    return pl.pallas_call(
        body, name="attn_bwd", grid=(nb,),
        in_specs=[_row_spec(t, SLAB_W), _tile_spec(SLAB_W, t), _row_spec(t, SLAB_W),
                  _const_spec(qa.shape), _const_spec(dot_.shape), _const_spec(lse.shape),
                  _const_spec(delta.shape), _const_spec(ecol.shape)],
        out_specs=[_row_spec(t, SLAB_W), _tile_spec(FOX_W, t),
                   _const_spec((nb, HEADS * QT_ROWS, t)), _row_spec(t, 128)],
        out_shape=[jax.ShapeDtypeStruct((s_len, SLAB_W), F32),
                   jax.ShapeDtypeStruct((nb, FOX_W, t), F32),
                   jax.ShapeDtypeStruct((nb, HEADS * QT_ROWS, t), F32),
                   jax.ShapeDtypeStruct((s_len, 128), F32)],
        scratch_shapes=[pltpu.VMEM((2, HEADS, t, t), BF16), pltpu.VMEM((2, HEADS, t, t), BF16)],
        compiler_params=_params(60, 1),
    )(ka, kat, vs, qa, dot_, lse, delta, ecol)


def _rev_cumsum(col_sums, gqt, triu):
    s_len = col_sums.shape[0]
    tm = TOKEN_TILE
    n = s_len // tm

    def body(cs_ref, gqt_ref, tri_ref, o_ref, carry):
        @pl.when(pl.program_id(0) == 0)
        def _():
            carry[...] = jnp.zeros_like(carry)
        rows = [gqt_ref[0, hd * QT_ROWS + HEAD_DIM:hd * QT_ROWS + HEAD_DIM + 1, :]
                for hd in range(HEADS)]
        row_sums = jnp.concatenate(rows + [jnp.zeros((128 - HEADS, tm), F32)], axis=0).T
        out = _tri_dot(tri_ref[...], row_sums - cs_ref[...]) + carry[...]
        o_ref[...] = out
        carry[...] = out[0:1, :]

    return pl.pallas_call(
        body, name="rev_cumsum", grid=(n,),
        in_specs=[pl.BlockSpec((tm, 128), lambda i: (n - 1 - i, 0)),
                  pl.BlockSpec((1, HEADS * QT_ROWS, tm), lambda i: (n - 1 - i, 0, 0)),
                  _const_spec((tm, tm))],
        out_specs=pl.BlockSpec((tm, 128), lambda i: (n - 1 - i, 0)),
        out_shape=jax.ShapeDtypeStruct((s_len, 128), F32),
        scratch_shapes=[pltpu.VMEM((1, 128), F32)],
        compiler_params=_params(32, 1),
    )(col_sums, gqt, triu)


def _heads_from_slabs(slabs):
    lane = lax.broadcasted_iota(jnp.int32, slabs[0].shape, 1)
    low = lane < HEAD_DIM
    pairs = [jnp.where(low, slabs[2 * p], pltpu.roll(slabs[2 * p + 1], HEAD_DIM, 1))
             for p in range(HEADS // 2)]
    return jnp.concatenate(pairs, axis=1)


def _proj_bwd(gqt, gk, dvt, dlogf, flog, qraw, kraw, duv, dgp, x, dx1, wcat, bdiag, gq, gk_gain, g1,
              efold):
    s_len = x.shape[0]
    tm = TOKEN_TILE

    def body(gqt_ref, gkk_ref, dvt_ref, dlf_ref, flog_ref, qr_ref, kr_ref, duv_ref, dgp_ref, x_ref,
             dx1_ref, w_ref, bd_ref, gq_ref, gk_ref, g1_ref, ef_ref,
             dx_ref, dprojt_ref, dgq_ref, dgk_ref, dbf_ref, dg1_ref, gq_acc, gk_acc):
        step = pl.program_id(0)

        @pl.when(step == 0)
        def _():
            gq_acc[...] = jnp.zeros_like(gq_acc)
            gk_acc[...] = jnp.zeros_like(gk_acc)
            dbf_ref[...] = jnp.zeros_like(dbf_ref)
            dg1_ref[...] = jnp.zeros_like(dg1_ref)

        pad = jnp.zeros((128 - QT_ROWS, tm), F32)
        q_slabs = [jnp.concatenate([gqt_ref[0, hd * QT_ROWS:(hd + 1) * QT_ROWS, :], pad], axis=0).T
                   for hd in range(HEADS)]
        dqn = _heads_from_slabs(q_slabs)
        dkn = _heads_from_slabs([gkk_ref[:, hd * 128:(hd + 1) * 128] for hd in range(HEADS)])

        def head_bwd(raw_ref, dn, g_ref, acc):
            raw = raw_ref[...].astype(F32)
            r = lax.rsqrt(_seg_mean(raw * raw, bd_ref) + EPS)
            xhat = raw * r
            acc[0:1, :] += jnp.sum(dn * xhat, axis=0, keepdims=True)
            dyg = dn * g_ref[...]
            return r * (dyg - xhat * _seg_mean(dyg * xhat, bd_ref))

        dot = functools.partial(jnp.dot, preferred_element_type=F32)
        duv, dgp = duv_ref[...], dgp_ref[...]
        dprojt_ref[C_UV:C_G, :] = duv.astype(F32).T.astype(BF16)
        dprojt_ref[C_G:C_F, :] = dgp.astype(F32).T.astype(BF16)
        dh = dot(duv, w_ref[C_UV:C_G, :]) + dot(dgp, w_ref[C_G:C_F, :])

        dq = head_bwd(qr_ref, dqn * HEAD_DIM ** -0.5, gq_ref, gq_acc)
        dk = head_bwd(kr_ref, dkn * LN2, gk_ref, gk_acc)
        dv_t = dvt_ref[0]
        dfl = dlf_ref[...] * _sigmoid(-flog_ref[...])
        dbf_ref[...] += jnp.sum(dfl, axis=0, keepdims=True)
        dprojt_ref[C_Q:C_K, :] = dq.T.astype(BF16)
        dprojt_ref[C_K:C_V, :] = dk.T.astype(BF16)
        dprojt_ref[C_V:C_UV, :] = dv_t.astype(BF16)
        dprojt_ref[C_F:C_END, :] = dfl.T.astype(BF16)
        dh = (dh + dot(dq.astype(BF16), w_ref[C_Q:C_K, :]) + dot(dk.astype(BF16), w_ref[C_K:C_V, :])
              + dot(dv_t.T.astype(BF16), w_ref[C_V:C_UV, :])
              + dot(dfl.astype(BF16), w_ref[C_F:C_END, :]))
        xf = x_ref[...]
        r = lax.rsqrt(jnp.mean(xf * xf, axis=-1, keepdims=True) + EPS)
        dg1_ref[...] += jnp.sum(dh * xf * r, axis=0, keepdims=True)
        dx_ref[...] = dx1_ref[...] + _rms_bwd(xf, r, g1_ref[...], dh)

        @pl.when(step == pl.num_programs(0) - 1)
        def _():
            dgq_ref[...] = _split3_dot(gq_acc[...], ef_ref[...])
            dgk_ref[...] = _split3_dot(gk_acc[...], ef_ref[...])

    outs = [((s_len, D_MODEL), F32, _row_spec(tm, D_MODEL)),
            ((C_END, s_len), BF16, pl.BlockSpec((C_END, tm), lambda i: (0, i))),
            ((8, 128), F32, _const_spec((8, 128))),
            ((8, 128), F32, _const_spec((8, 128))),
            ((1, 128), F32, _const_spec((1, 128))),
            ((1, D_MODEL), F32, _const_spec((1, D_MODEL)))]
    return pl.pallas_call(
        body, name="proj_bwd", grid=(s_len // tm,),
        in_specs=[_tile_spec(HEADS * QT_ROWS, tm), _row_spec(tm, SLAB_W), _tile_spec(FOX_W, tm),
                  _row_spec(tm, 128), _row_spec(tm, 128), _row_spec(tm, FOX_W),
                  _row_spec(tm, FOX_W), _row_spec(tm, 2 * SGU_W), _row_spec(tm, 2 * D_MODEL),
                  _row_spec(tm, D_MODEL), _row_spec(tm, D_MODEL), _const_spec(wcat.shape),
                  _const_spec(bdiag.shape), _const_spec((1, FOX_W)), _const_spec((1, FOX_W)),
                  _const_spec((1, D_MODEL)), _const_spec(efold.shape)],
        out_specs=[o[2] for o in outs],
        out_shape=[jax.ShapeDtypeStruct(o[0], o[1]) for o in outs],
        scratch_shapes=[pltpu.VMEM((8, FOX_W), F32), pltpu.VMEM((8, FOX_W), F32)],
        compiler_params=_params(56, 1),
    )(gqt, gk, dvt, dlogf, flog, qraw, kraw, duv, dgp, x, dx1, wcat, bdiag, gq, gk_gain, g1, efold)


def _dw_matmul(at, b, tm, name, after=()):
    m, s_len = at.shape
    n = b.shape[1]

    def body(a_ref, b_ref, *rest):
        rest[-1][...] = jnp.dot(a_ref[...], b_ref[...], preferred_element_type=F32).astype(BF16)

    return pl.pallas_call(
        body, name=name, grid=(m // tm,),
        in_specs=[pl.BlockSpec((tm, s_len), lambda i: (i, 0)), _const_spec(b.shape)]
        + [pl.BlockSpec(memory_space=pl.ANY)] * len(after),
        out_specs=pl.BlockSpec((tm, n), lambda i: (i, 0)),
        out_shape=jax.ShapeDtypeStruct((m, n), BF16),
        compiler_params=_params(48, 1),
    )(at, b, *after)


def _adamw(parts, w, m, v, tr, name, col_tile=None, select=None):
    parts = parts if isinstance(parts, (list, tuple)) else [parts]
    rows, cols = w.shape
    extra = [] if select is None else [select]
    bc1 = 1.0 - ADAM_B1 ** ADAM_STEP
    bc2 = 1.0 - ADAM_B2 ** ADAM_STEP

    def body(*refs):
        p_refs = refs[:len(parts)]
        sel_refs = refs[len(parts):len(parts) + len(extra)]
        w_ref, m_ref, v_ref, g_ref, d_ref, mo_ref, vo_ref = refs[len(parts) + len(extra):]
        g = None
        for p_ref, p in zip(p_refs, parts):
            for idx in range(p.shape[0]):
                term = p_ref[idx].astype(F32)
                g = term if g is None else g + term
        if sel_refs:
            g = _tri_dot(sel_refs[0][...], g)
        g_ref[...] = g
        mn = ADAM_B1 * m_ref[...] + (1.0 - ADAM_B1) * g
        vn = ADAM_B2 * v_ref[...] + (1.0 - ADAM_B2) * (g * g)
        mo_ref[...] = mn
        vo_ref[...] = vn
        m_hat = mn / bc1
        v_hat = vn / bc2
        d_ref[...] = -ADAM_LR * (m_hat / (jnp.sqrt(v_hat) + ADAM_EPS) + ADAM_WD * w_ref[...])

    if col_tile is None:
        spec = pl.BlockSpec((tr, cols), lambda i: (i, 0))
        pspecs = [pl.BlockSpec((p.shape[0], tr, cols), lambda i: (0, i, 0)) for p in parts]
        steps = rows // tr
    else:
        spec = pl.BlockSpec((rows, col_tile), lambda i: (0, i))
        pspecs = [pl.BlockSpec((p.shape[0], p.shape[1], col_tile), lambda i: (0, 0, i))
                  for p in parts]
        steps = cols // col_tile
    return pl.pallas_call(
        body, name=name, grid=(steps,),
        in_specs=pspecs + [_const_spec(e.shape) for e in extra] + [spec, spec, spec],
        out_specs=[spec] * 4,
        out_shape=[jax.ShapeDtypeStruct((rows, cols), F32)] * 4,
        compiler_params=_params(48, 1),
    )(*parts, *extra, w, m, v)


def _sum_parts(parts, name):
    n, rows, cols = parts.shape

    def body(p_ref, o_ref):
        g = p_ref[0]
        for idx in range(1, n):
            g = g + p_ref[idx]
        o_ref[...] = g

    return pl.pallas_call(
        body, name=name, out_shape=jax.ShapeDtypeStruct((rows, cols), F32),
        in_specs=[_const_spec(parts.shape)], out_specs=_const_spec((rows, cols)), grid=(1,),
        compiler_params=_params(16, 1),
    )(parts)


VEC_NAMES = ("g_pre_mix", "b_forget", "g_q", "g_k", "g_sgu", "b_sgu", "b_spatial", "g_post_mix",
             "g_pre_ffn", "g_post_ffn")
VEC_ROWS = 16
LOSS_ROW = len(VEC_NAMES)


def _pack_vectors(d, loss_row):
    rows = []
    for k in VEC_NAMES:
        flat = d[k].reshape(1, -1).astype(F32)
        rows.append(jnp.pad(flat, ((0, 0), (0, 1024 - flat.shape[1]))))
    rows.append(loss_row)
    rows.append(jnp.zeros((VEC_ROWS - len(rows), 1024), F32))
    return jnp.concatenate(rows, axis=0)


def _adamw_vectors(grad_rows, ws, ms, vs):
    n = len(VEC_NAMES)
    bc1 = 1.0 - ADAM_B1 ** ADAM_STEP
    bc2 = 1.0 - ADAM_B2 ** ADAM_STEP

    def step(g, w, m, v):
        mn = ADAM_B1 * m + (1.0 - ADAM_B1) * g
        vn = ADAM_B2 * v + (1.0 - ADAM_B2) * (g * g)
        delta = -ADAM_LR * ((mn / bc1) / (jnp.sqrt(vn / bc2) + ADAM_EPS) + ADAM_WD * w)
        return g, delta, mn, vn

    def body(*refs):
        g_ref = refs[0]
        ins = [refs[1 + j * n:1 + (j + 1) * n] for j in range(3)]
        outs = [refs[1 + (3 + j) * n:1 + (4 + j) * n] for j in range(4)]
        for i in range(n):
            shape = ws[i].shape
            if len(shape) == 2:
                res = step(g_ref[i:i + 1, :shape[1]], *[r[i][...] for r in ins])
                for o, val in zip(outs, res):
                    o[i][...] = val
            else:
                for r in range(shape[1]):
                    res = step(g_ref[i:i + 1, r * shape[2]:(r + 1) * shape[2]],
                               *[q[i][0, r:r + 1, :] for q in ins])
                    for o, val in zip(outs, res):
                        o[i][0, r:r + 1, :] = val

    vmem = pl.BlockSpec(memory_space=pltpu.VMEM)
    flat = pl.pallas_call(
        body, name="adamw_vectors",
        in_specs=[vmem] * (1 + 3 * n), out_specs=[vmem] * (4 * n),
        out_shape=[jax.ShapeDtypeStruct(w.shape, F32) for _ in range(4) for w in ws],
    )(grad_rows, *ws, *ms, *vs)
    return [flat[j * n:(j + 1) * n] for j in range(4)]


def _cols_to_blocks(full, width):
    r = full.shape[0]
    return jnp.transpose(full.reshape(r, N_DEV, width), (1, 0, 2))


def _blocks_to_cols(blocks):
    n, r, width = blocks.shape
    return jnp.transpose(blocks, (1, 0, 2)).reshape(r, n * width)


def kernel(x, g_pre_mix, w_in, b_forget, g_q, g_k, g_sgu, b_sgu, w_spatial, b_spatial, w_branch_a, w_branch_b, w_out, g_post_mix, g_pre_ffn, w_ffn_in, w_ffn_down, g_post_ffn, loss_target, m_g_pre_mix, m_w_in, m_b_forget, m_g_q, m_g_k, m_g_sgu, m_b_sgu, m_w_spatial, m_b_spatial, m_w_branch_a, m_w_branch_b, m_w_out, m_g_post_mix, m_g_pre_ffn, m_w_ffn_in, m_w_ffn_down, m_g_post_ffn, v_g_pre_mix, v_w_in, v_b_forget, v_g_q, v_g_k, v_g_sgu, v_b_sgu, v_w_spatial, v_b_spatial, v_w_branch_a, v_w_branch_b, v_w_out, v_g_post_mix, v_g_pre_ffn, v_w_ffn_in, v_w_ffn_down, v_g_post_ffn):
    big_names = ("w_in", "w_branch_a", "w_branch_b", "w_out", "w_ffn_in", "w_ffn_down")
    weights = dict(g_pre_mix=g_pre_mix, w_in=w_in, b_forget=b_forget, g_q=g_q, g_k=g_k, g_sgu=g_sgu,
                   b_sgu=b_sgu, w_spatial=w_spatial, b_spatial=b_spatial, w_branch_a=w_branch_a,
                   w_branch_b=w_branch_b, w_out=w_out, g_post_mix=g_post_mix, g_pre_ffn=g_pre_ffn,
                   w_ffn_in=w_ffn_in, w_ffn_down=w_ffn_down, g_post_ffn=g_post_ffn)
    mom1 = dict(g_pre_mix=m_g_pre_mix, w_in=m_w_in, b_forget=m_b_forget, g_q=m_g_q, g_k=m_g_k,
                g_sgu=m_g_sgu, b_sgu=m_b_sgu, w_spatial=m_w_spatial, b_spatial=m_b_spatial,
                w_branch_a=m_w_branch_a, w_branch_b=m_w_branch_b, w_out=m_w_out,
                g_post_mix=m_g_post_mix, g_pre_ffn=m_g_pre_ffn, w_ffn_in=m_w_ffn_in,
                w_ffn_down=m_w_ffn_down, g_post_ffn=m_g_post_ffn)
    mom2 = dict(g_pre_mix=v_g_pre_mix, w_in=v_w_in, b_forget=v_b_forget, g_q=v_g_q, g_k=v_g_k,
                g_sgu=v_g_sgu, b_sgu=v_b_sgu, w_spatial=v_w_spatial, b_spatial=v_b_spatial,
                w_branch_a=v_w_branch_a, w_branch_b=v_w_branch_b, w_out=v_w_out,
                g_post_mix=v_g_post_mix, g_pre_ffn=v_g_pre_ffn, w_ffn_in=v_w_ffn_in,
                w_ffn_down=v_w_ffn_down, g_post_ffn=v_g_post_ffn)
    names = list(weights)
    shapes = {k: weights[k].shape for k in names}

    s_len = x.shape[1]
    xs = x.reshape(s_len, D_MODEL)
    tgt = loss_target.reshape(s_len, D_MODEL)

    transposed = ("w_in", "w_ffn_in")

    def local_view(a, k):
        return jnp.transpose(a[0]) if k in transposed else a[0]

    shards = {k: local_view(weights[k], k).astype(BF16) for k in big_names}

    x_pos, y_pos, c_pos = _mesh_pos()
    me = 4 * x_pos + 2 * y_pos + c_pos
    r_idx = jnp.arange(BLK)
    general = (jnp.asarray(BLK_AT, jnp.int32) - jnp.asarray(FRAME_START, jnp.int32))[me] + r_idx
    holder = jnp.where(r_idx < F_AT, BLK_AT[F_DEV] - FRAME_START[F_DEV] + r_idx,
                       jnp.where(r_idx < F_AT + HEADS, FRAME - F_AT + r_idx,
                                 BLK_AT[F_DEV] - FRAME_START[F_DEV] - HEADS + r_idx))
    frame_row = jnp.where(me == F_DEV, holder, general)
    in_frame = (frame_row[:, None] == jnp.arange(FRAME_ROWS)[None, :]).astype(BF16)
    my_frame = jnp.dot(in_frame.T, shards["w_in"], preferred_element_type=F32).astype(BF16)
    wcat = _gather_w_in(my_frame)
    wcat, later = lax.optimization_barrier(
        (wcat, [shards[k] for k in big_names if k != "w_in"]))
    shards.update(zip([k for k in big_names if k != "w_in"], later))
    (gat_mix, gat_ffn), gat_token = _exchange_start(
        [[shards["w_branch_a"], shards["w_branch_b"], shards["w_out"]],
         [shards["w_ffn_in"], shards["w_ffn_down"]]], "gather_start", gather=True)

    seg = np.arange(FOX_W) // HEAD_DIM
    bdiag = jnp.asarray(seg[:128, None] == seg[None, :128], BF16)
    tm = TOKEN_TILE
    lower = np.arange(tm)[None, :] <= np.arange(tm)[:, None]
    tril = jnp.asarray(lower, BF16)
    triu = jnp.asarray(lower.T, BF16)
    egrp = jnp.asarray(seg[:, None] == np.arange(128)[None, :], BF16)
    efold = jnp.asarray((np.arange(FOX_W) % HEAD_DIM)[:, None] == np.arange(128)[None, :], BF16)
    gq512 = jnp.tile(g_q.reshape(1, HEAD_DIM), (1, HEADS))
    gk512 = jnp.tile(g_k.reshape(1, HEAD_DIM), (1, HEADS))
    bfor = jnp.pad(b_forget.reshape(1, HEADS), ((0, 0), (0, 128 - HEADS)))
    pos = np.arange(WINDOW)
    wmask = (pos[None, :] // CHUNK) <= (pos[:, None] // CHUNK)
    wsm_f = jnp.where(jnp.asarray(wmask)[None], w_spatial[0], 0.0)
    wsm = wsm_f.astype(BF16)
    wsmt = jnp.transpose(wsm_f, (0, 2, 1)).astype(BF16)
    bsf = jnp.repeat(jnp.transpose(b_spatial[0]), HEAD_DIM, axis=1)
    wmask_f = jnp.asarray(wmask, F32)

    col = np.arange(SLAB_W)
    row128 = np.arange(128)

    def d_place(first, sign):
        place = sum(((col[None, :] // 128 == row128[:, None] - HEADS * a)
                     & (col[None, :] % 128 == first + a)).astype(np.float32) for a in range(3))
        return jnp.asarray(sign * place, BF16)

    pdq, pdk = d_place(HEAD_DIM, 1.0), d_place(HEAD_DIM + 3, -1.0)
    ones_q = jnp.asarray((col % 128 >= HEAD_DIM + 3) & (col % 128 < HEAD_DIM + 6), F32)[None]
    ones_k = jnp.asarray((col % 128 >= HEAD_DIM) & (col % 128 < HEAD_DIM + 3), F32)[None]
    ecol = jnp.asarray((col[:, None] // 128 == row128[None, :])
                       & (col[:, None] % 128 == HEAD_DIM + 3), BF16)

    (h, qa, ka, kat, vs, vt, qraw, kraw, flog, uvpre, gpre) = _proj_fwd(
        xs, g_pre_mix + gat_token[0:1, 0:1], wcat, bdiag, gq512, gk512, bfor, tril, pdq, pdk,
        ones_q, ones_k)
    attn, attn_t, lse = _attn_fwd(qa, ka, vt)
    (own_a, own_b, own_out), (zone_a, zone_b, zone_out) = _exchange_wait(
        gat_mix, attn, "gather_wait_mix", gather=True)
    wa = _blocks_to_cols(_own_block(zone_a, own_a))
    wb = _blocks_to_cols(_own_block(zone_b, own_b))
    wout = _own_block(zone_out, own_out).reshape(D_MODEL, D_MODEL)
    sgu_t, ya, yb, merged_t, om, x1 = _mix_fwd(attn, uvpre, gpre, xs, wa, wb, wout, wsm, bsf,
                                           g_sgu, b_sgu, g_post_mix)
    (own_ffn, own_down), (zone_ffn, zone_down) = _exchange_wait(
        gat_ffn, x1, "gather_wait_ffn", gather=True)
    wffn = _own_block(zone_ffn, own_ffn).reshape(2 * D_FF, D_MODEL)
    wdown = _own_block(zone_down, own_down).reshape(D_FF, D_MODEL)
    (dx1, h2, act_t, dff, dgu_t, loss_acc, dg_post_ffn, dg_pre_ffn) = _ffn_fwd_bwd(
        x1, tgt, wffn, wdown, g_pre_ffn, g_post_ffn)

    dw_down = _dw_matmul(act_t, dff, D_FF // 4, "dw_down")
    dw_ffn = _dw_matmul(dgu_t, h2, 2 * D_FF // N_DEV, "dw_ffn_in")
    def own_of(parts):
        return [lax.dynamic_index_in_dim(p, me, 0, keepdims=False) for p in parts]

    parts_ffn = [dw_ffn.reshape(N_DEV, 2 * D_FF // N_DEV, D_MODEL),
                 dw_down.reshape(N_DEV, D_FF // N_DEV, D_MODEL)]
    mine_ffn = own_of(parts_ffn)
    (sct_ffn,), sct_ffn_token = _exchange_start([parts_ffn], "scatter_start_ffn", gather=False)

    (dom, dgp, dot_, delta, duv, dws, dbs, dg_sgu, db_sgu, dg_post_mix, dw_a, dw_b) = _mix_bwd(
        dx1, om, ya, yb, gpre, uvpre, attn, attn_t, sgu_t, wout, wa, wb, wsm, wsmt, bsf, g_sgu, b_sgu,
        g_post_mix + sct_ffn_token[0:1, 0:1], wmask_f, egrp)
    dw_out = _dw_matmul(merged_t, dom, 512, "dw_out")
    parts_mix = [_cols_to_blocks(dw_a, D_MODEL // N_DEV), _cols_to_blocks(dw_b, D_MODEL // N_DEV),
                 dw_out.reshape(N_DEV, D_MODEL // N_DEV, D_MODEL)]
    mine_mix = own_of(parts_mix)
    (sct_mix,), sct_mix_token = _exchange_start([parts_mix], "scatter_start_mix", gather=False)

    gk_all, dvt, gqt, col_sums = _attn_bwd(qa, ka, kat, vs, dot_, lse,
                                           delta + sct_mix_token[0, 0], ecol)
    dlogf = _rev_cumsum(col_sums, gqt, triu)
    dx, dproj_t, dgq, dgk, dbf, dg_pre_mix = _proj_bwd(
        gqt, gk_all, dvt, dlogf, flog, qraw, kraw, duv, dgp, xs, dx1, wcat, bdiag, gq512, gk512,
        g_pre_mix, efold)

    small_local = dict(
        g_pre_mix=dg_pre_mix, b_forget=dbf[:, :HEADS], g_q=dgq[0:1, :HEAD_DIM],
        g_k=dgk[0:1, :HEAD_DIM], g_sgu=dg_sgu, b_sgu=db_sgu, w_spatial=dws,
        b_spatial=jnp.transpose(dbs[:, :GROUPS]), g_post_mix=dg_post_mix, g_pre_ffn=dg_pre_ffn,
        g_post_ffn=dg_post_ffn)
    loss_row = jnp.pad(loss_acc[0:1, 0:1], ((0, 0), (0, 1023)))
    small_parts = [_pack_vectors(small_local, loss_row).reshape(N_DEV, VEC_ROWS // N_DEV, 1024),
                   dws]

    def with_own(zones, own_blocks):
        return [_own_block(z, b) for z, b in zip(zones, own_blocks)]

    mine_small = own_of(small_parts)
    (sct_small,), sct_small_token = _exchange_start([small_parts], "scatter_start_small",
                                                    gather=False)
    dw_cat = _dw_matmul(dproj_t, h, C_END // N_DEV, "dw_in", after=(sct_small_token,))
    recv_vec, recv_ws = with_own(
        _exchange_wait(sct_small, dw_cat, "scatter_wait_small", gather=False)[1], mine_small)
    small_sums = [_sum_parts(recv_vec, "sum_vectors"), _sum_parts(recv_ws, "sum_w_spatial")]
    (gat_small,), gat_small_token = _exchange_start([small_sums], "gather_start_small",
                                                    gather=True)
    pair_blocks, own_pair = _pair_sums(dw_cat, "pair_sums_in", gat_small_token)
    rs_in, rs_token = _chip_exchange_start(pair_blocks, "chip_exchange_start_in")

    recv_ffn, recv_down = with_own(
        _exchange_wait(sct_ffn, rs_token, "scatter_wait_ffn", gather=False)[1], mine_ffn)
    recv_a, recv_b, recv_out = with_own(
        _exchange_wait(sct_mix, recv_ffn, "scatter_wait_mix", gather=False)[1], mine_mix)
    received = [None, recv_a, recv_b, recv_out, recv_ffn, recv_down]

    grads, deltas, new_m, new_v = {}, {}, {}, {}
    row_tiles = {"w_in": None, "w_branch_a": 512, "w_branch_b": 512, "w_out": 128, "w_ffn_in": 176,
                 "w_ffn_down": 352}

    def update(k, parts):
        outs = _adamw(parts, local_view(weights[k], k), local_view(mom1[k], k),
                      local_view(mom2[k], k), row_tiles[k], "adamw_" + k,
                      col_tile=256 if k == "w_in" else None,
                      select=in_frame if k == "w_in" else None)
        if k in transposed:
            outs = [jnp.transpose(o) for o in outs]
        grads[k], deltas[k], new_m[k], new_v[k] = [o[None] for o in outs]
        return outs[0]

    last = None
    for idx, k in enumerate(big_names):
        if k != "w_in":
            last = update(k, received[idx])

    (own_vec, own_ws), (zone_vec, zone_ws) = _exchange_wait(gat_small, last, "gather_wait_small",
                                                            gather=True)
    vec_all = _own_block(zone_vec, own_vec).reshape(VEC_ROWS, 1024)
    ws_all = _own_block(zone_ws, own_ws).reshape(1, GROUPS * WINDOW, WINDOW)

    def rows_of(d):
        return d["w_spatial"].reshape(GROUPS * WINDOW, WINDOW)

    outs = _adamw(ws_all, rows_of(weights), rows_of(mom1), rows_of(mom2), GROUPS * WINDOW,
                  "adamw_w_spatial")
    for dst, o in zip((grads, deltas, new_m, new_v), outs):
        dst["w_spatial"] = o.reshape(shapes["w_spatial"])
    sg = outs[0]
    vec_outs = _adamw_vectors(vec_all, *[[d[k] for k in VEC_NAMES] for d in (weights, mom1, mom2)])
    for dst, group in zip((grads, deltas, new_m, new_v), vec_outs):
        dst.update(zip(VEC_NAMES, group))
    arrived = _chip_exchange_wait(rs_in, sg, "chip_exchange_wait_in")
    update("w_in", [own_pair[None], arrived])

    loss = vec_all[LOSS_ROW, 0]
    return (loss, dx.reshape(x.shape), *[grads[k] for k in names], *[deltas[k] for k in names],
            *[new_m[k] for k in names], *[new_v[k] for k in names])
```

```python
import functools
import math

import jax
import jax.numpy as jnp
import numpy as np
from jax import lax
from jax.experimental import pallas as pl
from jax.experimental.pallas import tpu as pltpu

F32 = jnp.float32
BF16 = jnp.bfloat16

D_MODEL = 1024
FOX_W = 512
HEADS = 8
HEAD_DIM = 64
SGU_W = 512
GROUPS = 8
WINDOW = 128
CHUNK = 64
D_FF = 2816
IN_COLS = 4616
EPS = 1e-6
N_DEV = 8
LOG2E = 1.4426950408889634
LN2 = 0.6931471805599453

C_Q, C_K, C_V, C_UV, C_G, C_F, C_END = 0, 512, 1024, 1536, 2560, 4608, 4736

ADAM_LR, ADAM_B1, ADAM_B2, ADAM_EPS, ADAM_WD, ADAM_STEP = 0.001, 0.9, 0.999, 1e-08, 0.01, 10

MIB = 1024 * 1024
TOKEN_TILE = 256
ATTN_TILE = 256
SLAB_W = HEADS * 128
QT_ROWS = 72

BLK = IN_COLS // N_DEV
F_LO = 3 * FOX_W
F_DEV = F_LO // BLK
F_AT = F_LO - F_DEV * BLK
BLK_AT = [BLK * j - (HEADS if BLK * j > F_LO else 0) for j in range(N_DEV)]
FRAME_START = [a // 16 * 16 for a in BLK_AT]
FRAME = 608
FRAME_ROWS = FRAME + 16


def _params(vmem_mib, n_axes):
    return pltpu.CompilerParams(
        dimension_semantics=("arbitrary",) * n_axes, vmem_limit_bytes=vmem_mib * MIB)


def _const_spec(shape):
    nd = len(shape)
    return pl.BlockSpec(shape, lambda *_: (0,) * nd)


def _row_spec(tm, cols):
    return pl.BlockSpec((tm, cols), lambda i: (i, 0))


def _tile_spec(rows, tm):
    return pl.BlockSpec((1, rows, tm), lambda i: (i, 0, 0))


def _split3_dot(x, e):
    x1 = x.astype(BF16)
    r1 = x - x1.astype(F32)
    x2 = r1.astype(BF16)
    x3 = (r1 - x2.astype(F32)).astype(BF16)
    dot = functools.partial(jnp.dot, preferred_element_type=F32)
    return dot(x1, e) + dot(x2, e) + dot(x3, e)


def _tri_dot(tri, x):
    x1 = x.astype(BF16)
    r1 = x - x1.astype(F32)
    x2 = r1.astype(BF16)
    x3 = (r1 - x2.astype(F32)).astype(BF16)
    dot = functools.partial(jnp.dot, preferred_element_type=F32)
    return dot(tri, x1) + dot(tri, x2) + dot(tri, x3)


def _seg_mean(sq, bd_ref):
    hi = sq.astype(BF16)
    bd = bd_ref[...]
    pairs = [jnp.dot(hi[:, p * 128:(p + 1) * 128], bd, preferred_element_type=F32)
             for p in range(HEADS // 2)]
    return jnp.concatenate(pairs, axis=1) * (1.0 / HEAD_DIM)


def _slabs_from_heads(t):
    lane = lax.broadcasted_iota(jnp.int32, (t.shape[0], 128), 1)
    low = lane < HEAD_DIM
    slabs = []
    for p in range(HEADS // 2):
        pair = t[:, p * 128:(p + 1) * 128]
        slabs.append(jnp.where(low, pair, 0.0))
        slabs.append(jnp.where(low, pltpu.roll(pair, HEAD_DIM, 1), 0.0))
    return jnp.concatenate(slabs, axis=1)


def _dot_nt(a, b):
    return lax.dot_general(a, b, (((1,), (1,)), ((), ())), preferred_element_type=F32)


def _sigmoid(x):
    return 0.5 * jnp.tanh(0.5 * x) + 0.5


_GELU_C = math.sqrt(2.0 / math.pi)


def _gelu_and_grad(x):
    inner = _GELU_C * (x + 0.044715 * x * x * x)
    t = jnp.tanh(inner)
    y = 0.5 * x * (1.0 + t)
    dy = 0.5 * (1.0 + t) + 0.5 * x * (1.0 - t * t) * _GELU_C * (1.0 + 3.0 * 0.044715 * x * x)
    return y, dy


def _rms_bwd(xin, r, g, dy):
    dyg = dy * g
    return r * dyg - xin * (r * r * r) * jnp.mean(dyg * xin, axis=-1, keepdims=True)


def _mesh_pos():
    x, y, c = lax.axis_index("x"), lax.axis_index("y"), lax.axis_index("c")
    return x, y, c


def _peer(k):
    x, y, c = _mesh_pos()
    px = (1 - x) if (k >> 2) & 1 else x
    py = (1 - y) if (k >> 1) & 1 else y
    pc = (1 - c) if k & 1 else c
    return (px, py, pc), 4 * px + 2 * py + pc


def _frame_start(j):
    at = BLK * j - jnp.where(BLK * j > F_LO, HEADS, 0)
    return pl.multiple_of(at // 16 * 16, 16)


HALF_A = 320


def _gather_w_in(frame):
    pieces = (slice(0, HALF_A), slice(HALF_A, FRAME_ROWS))

    def body(x_ref, out_ref, zone, send_sems, recv_sems, local_sem):
        x, y, c = _mesh_pos()
        me, sibling = (x, y, c), (x, y, 1 - c)
        nbr_x, nbr_y, across = (1 - x, y, c), (x, 1 - y, c), (1 - x, 1 - y, c)

        def index(pos):
            return 4 * pos[0] + 2 * pos[1] + pos[2]

        def copy(k, block, piece, to, src=None):
            rows = pieces[piece]
            return pltpu.make_async_remote_copy(
                src_ref=(zone.at[index(block), rows] if src is None else src.at[rows]),
                dst_ref=zone.at[index(block), rows],
                send_sem=send_sems.at[k], recv_sem=recv_sems.at[k],
                device_id=to, device_id_type=pl.DeviceIdType.MESH)

        def add(block):
            j = index(block)
            rows = pl.ds(_frame_start(j), FRAME)
            out_ref[rows, :] = (out_ref[rows, :].astype(F32)
                                + zone[j, :FRAME, :].astype(F32)).astype(BF16)
            tail = slice(C_F, C_F + FRAME_ROWS - FRAME)
            forget = zone[j, FRAME:, :].astype(F32) * (j == F_DEV).astype(F32)
            out_ref[tail, :] = (out_ref[tail, :].astype(F32) + forget).astype(BF16)

        mine = pltpu.make_async_copy(x_ref, zone.at[index(me)], local_sem)
        mine.start()
        first = [copy(1, me, 0, nbr_x, src=x_ref), copy(3, me, 1, nbr_y, src=x_ref),
                 copy(2, me, 1, nbr_x, src=x_ref), copy(4, me, 0, nbr_y, src=x_ref)]
        own_to_sibling = pltpu.make_async_remote_copy(
            src_ref=x_ref, dst_ref=zone.at[index(me)], send_sem=send_sems.at[0],
            recv_sem=recv_sems.at[0], device_id=sibling, device_id_type=pl.DeviceIdType.MESH)
        for cp in first:
            cp.start()
        own_to_sibling.start()
        out_ref[...] = jnp.zeros_like(out_ref)
        mine.wait()
        add(me)

        sent = []

        def landed(k, block, piece, forward=None):
            copy(k, block, piece, me).wait_recv()
            if forward is not None:
                cp = copy(*forward)
                cp.start()
                sent.append(cp)
            cp = copy(6 + k, block, piece, sibling)
            cp.start()
            sent.append(cp)

        landed(1, nbr_x, 0, forward=(5, nbr_x, 0, nbr_y))
        landed(3, nbr_y, 1, forward=(6, nbr_y, 1, nbr_x))
        landed(2, nbr_x, 1)
        add(nbr_x)
        landed(4, nbr_y, 0)
        add(nbr_y)
        landed(5, across, 0)
        landed(6, across, 1)
        add(across)
        pltpu.make_async_remote_copy(
            src_ref=x_ref, dst_ref=zone.at[index(sibling)], send_sem=send_sems.at[0],
            recv_sem=recv_sems.at[0], device_id=sibling,
            device_id_type=pl.DeviceIdType.MESH).wait_recv()
        add(sibling)
        for k, block in ((1, nbr_x), (2, nbr_x), (3, nbr_y), (4, nbr_y), (5, across), (6, across)):
            their = (block[0], block[1], 1 - c)
            piece = {1: 0, 2: 1, 3: 1, 4: 0, 5: 0, 6: 1}[k]
            copy(6 + k, their, piece, me).wait_recv()
            if k in (2, 4, 6):
                add(their)
        for cp in first + sent:
            cp.wait_send()
        own_to_sibling.wait_send()

    return pl.pallas_call(
        body, name="gather_w_in", out_shape=jax.ShapeDtypeStruct((C_END, frame.shape[1]), BF16),
        in_specs=[pl.BlockSpec(memory_space=pl.ANY)],
        out_specs=pl.BlockSpec(memory_space=pltpu.VMEM),
        scratch_shapes=[pltpu.VMEM((N_DEV,) + frame.shape, BF16),
                        pltpu.SemaphoreType.DMA((13,)), pltpu.SemaphoreType.DMA((13,)),
                        pltpu.SemaphoreType.DMA],
        compiler_params=pltpu.CompilerParams(vmem_limit_bytes=40 * MIB),
    )(frame)


def _chip_peer(k):
    x, y, c = _mesh_pos()
    px = (1 - x) if (k >> 1) & 1 else x
    py = (1 - y) if k & 1 else y
    return (px, py, c), 2 * px + py


def _pair_sums(dw_cat, name, after):
    rows, cols = FRAME_ROWS, dw_cat.shape[1]
    n_chips = N_DEV // 2

    def pieces(p_ref, j):
        return (p_ref.at[pl.ds(_frame_start(j), FRAME)], p_ref.at[pl.ds(C_F, FRAME_ROWS - FRAME)])

    def body(p_ref, after_ref, send_ref, own_ref, mine_buf, sib_buf, send_sems, recv_sems,
             local_sems):
        x, y, c = _mesh_pos()
        sibling = (x, y, 1 - c)
        copies, local = [], []
        for q in range(n_chips):
            for part, (lo, hi) in enumerate(((0, FRAME), (FRAME, FRAME_ROWS))):
                cp = pltpu.make_async_remote_copy(
                    src_ref=pieces(p_ref, 2 * q + (1 - c))[part], dst_ref=sib_buf.at[q, lo:hi],
                    send_sem=send_sems.at[2 * q + part], recv_sem=recv_sems.at[2 * q + part],
                    device_id=sibling, device_id_type=pl.DeviceIdType.MESH)
                cp.start()
                copies.append(cp)
                lc = pltpu.make_async_copy(pieces(p_ref, 2 * q + c)[part], mine_buf.at[q, lo:hi],
                                           local_sems.at[2 * q + part])
                lc.start()
                local.append(lc)
        for lc in local:
            lc.wait()
        for cp in copies:
            cp.wait_recv()
        for k in range(1, n_chips):
            _, q = _chip_peer(k)
            send_ref[k - 1] = (mine_buf[q].astype(F32) + sib_buf[q].astype(F32)).astype(BF16)
        my_chip = 2 * x + y
        own_ref[...] = mine_buf[my_chip].astype(F32) + sib_buf[my_chip].astype(F32)
        for cp in copies:
            cp.wait_send()

    vmem = pl.BlockSpec(memory_space=pltpu.VMEM)
    return pl.pallas_call(
        body, name=name,
        out_shape=[jax.ShapeDtypeStruct((n_chips - 1, rows, cols), BF16),
                   jax.ShapeDtypeStruct((rows, cols), F32)],
        in_specs=[pl.BlockSpec(memory_space=pl.ANY)] * 2, out_specs=[vmem, vmem],
        scratch_shapes=[pltpu.VMEM((n_chips, rows, cols), BF16),
                        pltpu.VMEM((n_chips, rows, cols), BF16),
                        pltpu.SemaphoreType.DMA((2 * n_chips,)),
                        pltpu.SemaphoreType.DMA((2 * n_chips,)),
                        pltpu.SemaphoreType.DMA((2 * n_chips,))],
        compiler_params=pltpu.CompilerParams(vmem_limit_bytes=40 * MIB),
    )(dw_cat, after)


def _chip_copy(src_ref, land_ref, send_sem, recv_sem, k):
    peer, _ = _chip_peer(k)
    return pltpu.make_async_remote_copy(
        src_ref=src_ref.at[k - 1], dst_ref=land_ref.at[k - 1], send_sem=send_sem, recv_sem=recv_sem,
        device_id=peer, device_id_type=pl.DeviceIdType.MESH)


def _chip_exchange_start(blocks, name):
    hbm = pl.BlockSpec(memory_space=pltpu.HBM)
    sem = pl.BlockSpec(memory_space=pltpu.SEMAPHORE)
    n_peers = blocks.shape[0]

    def body(src_ref, zone_ref, send_sems, recv_sems, src_thru, zone_thru, token):
        for k in range(1, n_peers + 1):
            _chip_copy(src_ref, zone_ref, send_sems.at[k - 1], recv_sems.at[k - 1], k).start()
        token[...] = jnp.zeros_like(token)

    outs = pl.pallas_call(
        body, name=name, in_specs=[hbm, hbm],
        out_shape=[pltpu.SemaphoreType.DMA((n_peers,)), pltpu.SemaphoreType.DMA((n_peers,)),
                   pltpu.HBM(blocks.shape, blocks.dtype), pltpu.HBM(blocks.shape, blocks.dtype),
                   jax.ShapeDtypeStruct((8, 128), F32)],
        out_specs=[sem, sem, hbm, hbm, pl.BlockSpec(memory_space=pltpu.VMEM)],
        input_output_aliases={0: 2, 1: 3},
        compiler_params=pltpu.CompilerParams(
            has_side_effects=pltpu.SideEffectType.DATAFLOW_SIDE_EFFECTING),
    )(pltpu.with_memory_space_constraint(blocks, pltpu.HBM),
      pltpu.with_memory_space_constraint(lax.empty(blocks.shape, blocks.dtype), pltpu.HBM))
    return outs[:4], outs[4]


def _chip_exchange_wait(handle, after, name):
    send_sems, recv_sems, src, zone = handle
    hbm = pl.BlockSpec(memory_space=pltpu.HBM)
    sem = pl.BlockSpec(memory_space=pltpu.SEMAPHORE)

    def body(src_ref, zone_ref, ssem, rsem, after_ref, src_out, zone_out):
        for k in range(1, src.shape[0] + 1):
            cp = _chip_copy(src_ref, zone_ref, ssem.at[k - 1], rsem.at[k - 1], k)
            cp.wait_send()
            cp.wait_recv()

    outs = pl.pallas_call(
        body, name=name,
        in_specs=[hbm, hbm, sem, sem, pl.BlockSpec(memory_space=pl.ANY)],
        out_shape=[pltpu.HBM(src.shape, src.dtype), pltpu.HBM(zone.shape, zone.dtype)],
        out_specs=[hbm, hbm], input_output_aliases={0: 0, 1: 1},
        compiler_params=pltpu.CompilerParams(
            has_side_effects=pltpu.SideEffectType.DATAFLOW_SIDE_EFFECTING),
    )(src, zone, send_sems, recv_sems, after)
    return outs[1]


def _remote_copy(gather, src_ref, land_ref, send_sem, recv_sem, k, receive_side):
    x, y, c = _mesh_pos()
    me = 4 * x + 2 * y + c
    peer, pidx = _peer(k)
    return pltpu.make_async_remote_copy(
        src_ref=src_ref if gather else src_ref.at[pidx],
        dst_ref=land_ref.at[pidx if receive_side else me],
        send_sem=send_sem, recv_sem=recv_sem,
        device_id=peer, device_id_type=pl.DeviceIdType.MESH)


def _exchange_start(groups, name, gather):
    arrs = [a for g in groups for a in g]
    n, n_groups = len(arrs), len(groups)
    lands = [jax.ShapeDtypeStruct(((N_DEV,) + a.shape) if gather else a.shape, a.dtype)
             for a in arrs]

    def body(*refs):
        srcs, zones = refs[:n], refs[n:2 * n]
        sems = refs[2 * n:2 * n + 2 * n_groups]
        token = refs[-1]
        a = 0
        for gi, g in enumerate(groups):
            send_sems, recv_sems = sems[2 * gi], sems[2 * gi + 1]
            for k in range(1, N_DEV):
                for ai in range(len(g)):
                    slot = ai * (N_DEV - 1) + k - 1
                    _remote_copy(gather, srcs[a + ai], zones[a + ai], send_sems.at[slot],
                                 recv_sems.at[slot], k, False).start()
            a += len(g)
        token[...] = jnp.zeros_like(token)

    hbm = pl.BlockSpec(memory_space=pltpu.HBM)
    sem = pl.BlockSpec(memory_space=pltpu.SEMAPHORE)
    sem_shapes = []
    for g in groups:
        sem_shapes += [pltpu.SemaphoreType.DMA((len(g) * (N_DEV - 1),))] * 2
    outs = pl.pallas_call(
        body, name=name,
        in_specs=[hbm] * (2 * n),
        out_shape=sem_shapes + [pltpu.HBM(a.shape, a.dtype) for a in arrs]
        + [pltpu.HBM(z.shape, z.dtype) for z in lands] + [jax.ShapeDtypeStruct((8, 128), F32)],
        out_specs=[sem] * (2 * n_groups) + [hbm] * (2 * n)
        + [pl.BlockSpec(memory_space=pltpu.VMEM)],
        input_output_aliases={i: 2 * n_groups + i for i in range(2 * n)},
        compiler_params=pltpu.CompilerParams(
            has_side_effects=pltpu.SideEffectType.DATAFLOW_SIDE_EFFECTING),
    )(*[pltpu.with_memory_space_constraint(a, pltpu.HBM) for a in arrs],
      *[pltpu.with_memory_space_constraint(lax.empty(z.shape, z.dtype), pltpu.HBM) for z in lands])
    sems = outs[:2 * n_groups]
    thru = outs[2 * n_groups:2 * n_groups + n]
    zones = outs[2 * n_groups + n:2 * n_groups + 2 * n]
    handles, a = [], 0
    for gi, g in enumerate(groups):
        handles.append((sems[2 * gi], sems[2 * gi + 1], thru[a:a + len(g)], zones[a:a + len(g)]))
        a += len(g)
    return handles, outs[-1]


def _exchange_wait(handle, after, name, gather):
    send_sems, recv_sems, thru, zones = handle
    n = len(thru)

    def body(*refs):
        srcs, lands = refs[:n], refs[n:2 * n]
        ssem, rsem = refs[2 * n], refs[2 * n + 1]
        for k in range(1, N_DEV):
            for ai in range(n):
                slot = ai * (N_DEV - 1) + k - 1
                cp = _remote_copy(gather, srcs[ai], lands[ai], ssem.at[slot], rsem.at[slot], k, True)
                cp.wait_send()
                cp.wait_recv()

    hbm = pl.BlockSpec(memory_space=pltpu.HBM)
    sem = pl.BlockSpec(memory_space=pltpu.SEMAPHORE)
    outs = pl.pallas_call(
        body, name=name,
        in_specs=[hbm] * (2 * n) + [sem, sem, pl.BlockSpec(memory_space=pl.ANY)],
        out_shape=[pltpu.HBM(a.shape, a.dtype) for a in thru]
        + [pltpu.HBM(z.shape, z.dtype) for z in zones],
        out_specs=[hbm] * (2 * n),
        input_output_aliases={i: i for i in range(2 * n)},
        compiler_params=pltpu.CompilerParams(
            has_side_effects=pltpu.SideEffectType.DATAFLOW_SIDE_EFFECTING),
    )(*thru, *zones, send_sems, recv_sems, after)
    return outs[:n], outs[n:]


def _own_block(zone, block):
    x, y, c = _mesh_pos()
    me = 4 * x + 2 * y + c
    return lax.dynamic_update_slice_in_dim(zone, block[None], me, axis=0)


def _proj_fwd(x, g1, wcat, bdiag, gq, gk, bfor, tri, pdq, pdk, ones_q, ones_k):
    s_len = x.shape[0]
    tm = TOKEN_TILE
    nt = s_len // tm

    def body(x_ref, g1_ref, w_ref, bd_ref, gq_ref, gk_ref, bf_ref, tri_ref, pdq_ref,
             pdk_ref, oq_ref, ok_ref,
             h_ref, qa_ref, ka_ref, kat_ref, vs_ref, vt_ref, qr_ref, kr_ref, flog_ref, uv_ref,
             gp_ref, carry):
        @pl.when(pl.program_id(0) == 0)
        def _():
            carry[...] = jnp.zeros_like(carry)

        xf = x_ref[...]
        r = lax.rsqrt(jnp.mean(xf * xf, axis=-1, keepdims=True) + EPS)
        h = (xf * r * g1_ref[...]).astype(BF16)
        h_ref[...] = h
        dot = functools.partial(jnp.dot, preferred_element_type=F32)

        def proj(lo, hi):
            return _dot_nt(h, w_ref[lo:hi, :])

        flog = proj(C_F, C_END) + bf_ref[...]
        flog_ref[...] = flog
        lane = lax.broadcasted_iota(jnp.int32, flog.shape, 1)
        logf = jnp.minimum(flog, 0.0) - jnp.log(1.0 + jnp.exp(-jnp.abs(flog)))
        logf = jnp.where(lane < HEADS, logf, 0.0)
        dcum = _tri_dot(tri_ref[...], logf) + carry[...]
        carry[...] = dcum[tm - 1:tm, :]
        d2 = dcum * LOG2E
        d2a = d2.astype(BF16)
        rem = d2 - d2a.astype(F32)
        d2b = rem.astype(BF16)
        d2c = (rem - d2b.astype(F32)).astype(BF16)

        q = proj(C_Q, C_K)
        qr_ref[...] = q.astype(BF16)
        rq = lax.rsqrt(_seg_mean(q * q, bd_ref) + EPS)
        qn = q * rq * (gq_ref[...] * (HEAD_DIM ** -0.5 * LOG2E))
        d_parts = (d2a.astype(F32) + pltpu.roll(d2b.astype(F32), HEADS, 1)
                   + pltpu.roll(d2c.astype(F32), 2 * HEADS, 1)).astype(BF16)
        qa = _slabs_from_heads(qn) + dot(d_parts, pdq_ref[...]) + oq_ref[...]
        qa_ref[...] = qa.astype(BF16)

        k = proj(C_K, C_V)
        kr_ref[...] = k.astype(BF16)
        rk = lax.rsqrt(_seg_mean(k * k, bd_ref) + EPS)
        kn = k * rk * gk_ref[...]
        ka = _slabs_from_heads(kn) + dot(d_parts, pdk_ref[...]) + ok_ref[...]
        ka_ref[...] = ka.astype(BF16)
        kat_ref[0] = ka.T.astype(BF16)

        v = proj(C_V, C_UV)
        vs_ref[...] = _slabs_from_heads(v).astype(BF16)
        vt_ref[0] = v.T.astype(BF16)
        uv_ref[...] = proj(C_UV, C_G).astype(BF16)
        gp_ref[...] = proj(C_G, C_F).astype(BF16)

    outs = [((s_len, D_MODEL), BF16, _row_spec(tm, D_MODEL)),
            ((s_len, SLAB_W), BF16, _row_spec(tm, SLAB_W)),
            ((s_len, SLAB_W), BF16, _row_spec(tm, SLAB_W)),
            ((nt, SLAB_W, tm), BF16, _tile_spec(SLAB_W, tm)),
            ((s_len, SLAB_W), BF16, _row_spec(tm, SLAB_W)),
            ((nt, FOX_W, tm), BF16, _tile_spec(FOX_W, tm)),
            ((s_len, FOX_W), BF16, _row_spec(tm, FOX_W)),
            ((s_len, FOX_W), BF16, _row_spec(tm, FOX_W)),
            ((s_len, 128), F32, _row_spec(tm, 128)),
            ((s_len, 2 * SGU_W), BF16, _row_spec(tm, 2 * SGU_W)),
            ((s_len, 2 * D_MODEL), BF16, _row_spec(tm, 2 * D_MODEL))]
    return pl.pallas_call(
        body, name="proj_fwd", grid=(nt,),
        in_specs=[_row_spec(tm, D_MODEL), _const_spec((1, D_MODEL)), _const_spec(wcat.shape),
                  _const_spec(bdiag.shape), _const_spec((1, FOX_W)), _const_spec((1, FOX_W)),
                  _const_spec((1, 128)), _const_spec((tm, tm)), _const_spec(pdq.shape), _const_spec(pdk.shape), _const_spec(ones_q.shape),
                  _const_spec(ones_k.shape)],
        out_specs=[o[2] for o in outs],
        out_shape=[jax.ShapeDtypeStruct(o[0], o[1]) for o in outs],
        scratch_shapes=[pltpu.VMEM((1, 128), F32)],
        compiler_params=_params(56, 1),
    )(x, g1, wcat, bdiag, gq, gk, bfor, tri, pdq, pdk, ones_q, ones_k)


def _attn_fwd(qa, ka, vt):
    s_len = qa.shape[0]
    t = ATTN_TILE
    nb = s_len // t

    def body(q_ref, k_ref, vt_ref, o_ref, ot_ref, lse_ref, m_sc, l_sc, acc_sc, s_sc, mcur_sc,
             alpha_sc):
        i = pl.program_id(0)
        m_sc[...] = jnp.full_like(m_sc, -jnp.inf)
        l_sc[...] = jnp.zeros_like(l_sc)
        acc_sc[...] = jnp.zeros_like(acc_sc)

        def logits(j, slot, masked):
            krows = pl.ds(pl.multiple_of(j * t, t), t)
            if masked:
                keep = (lax.broadcasted_iota(jnp.int32, (t, t), 0)
                        <= lax.broadcasted_iota(jnp.int32, (t, t), 1))
            for hd in range(HEADS):
                sl = slice(hd * 128, (hd + 1) * 128)
                st = _dot_nt(k_ref[krows, sl], q_ref[:, sl])
                if masked:
                    st = jnp.where(keep, st, -jnp.inf)
                s_sc[slot, hd] = st
                m_prev = m_sc[hd:hd + 1, :]
                m_new = jnp.maximum(m_prev, jnp.max(st, axis=0, keepdims=True))
                alpha_sc[slot, hd:hd + 1, :] = jnp.exp2(m_prev - m_new)
                mcur_sc[slot, hd:hd + 1, :] = m_new
                m_sc[hd:hd + 1, :] = m_new

        def accumulate(j, slot):
            for hd in range(HEADS):
                hr = slice(hd * HEAD_DIM, (hd + 1) * HEAD_DIM)
                alpha = alpha_sc[slot, hd:hd + 1, :]
                pt = jnp.exp2(s_sc[slot, hd] - mcur_sc[slot, hd:hd + 1, :])
                l_sc[hd:hd + 1, :] = alpha * l_sc[hd:hd + 1, :] + jnp.sum(pt, axis=0, keepdims=True)
                acc_sc[hr, :] = alpha * acc_sc[hr, :] + jnp.dot(
                    vt_ref[j, hr, :], pt.astype(BF16), preferred_element_type=F32)

        @pl.when(i == 0)
        def _():
            logits(0, 0, True)
            accumulate(0, 0)

        pairs = (i - 1) // 2

        @pl.when(i > 0)
        def _():
            logits(0, 0, False)

            def two_blocks(p, carry):
                logits(2 * p + 1, 1, False)
                accumulate(2 * p, 0)
                logits(2 * p + 2, 0, False)
                accumulate(2 * p + 1, 1)
                return carry

            lax.fori_loop(0, pairs, two_blocks, 0)

        @pl.when((i > 0) & (i - 2 * pairs == 1))
        def _():
            logits(i, 1, True)
            accumulate(i - 1, 0)
            accumulate(i, 1)

        @pl.when((i > 0) & (i - 2 * pairs == 2))
        def _():
            logits(i - 1, 1, False)
            accumulate(i - 2, 0)
            logits(i, 0, True)
            accumulate(i - 1, 1)
            accumulate(i, 0)

        for hd in range(HEADS):
            hr = slice(hd * HEAD_DIM, (hd + 1) * HEAD_DIM)
            l = l_sc[hd:hd + 1, :]
            acc_sc[hr, :] = acc_sc[hr, :] / l
            lse_ref[0, hd:hd + 1, :] = m_sc[hd:hd + 1, :] + jnp.log2(l)
        o_ref[...] = acc_sc[...].T.astype(BF16)
        ot_ref[...] = acc_sc[...].astype(BF16)

    return pl.pallas_call(
        body, name="attn_fwd", grid=(nb,),
        in_specs=[_row_spec(t, SLAB_W), _const_spec(ka.shape), _const_spec(vt.shape)],
        out_specs=[_row_spec(t, FOX_W), pl.BlockSpec((FOX_W, t), lambda i: (0, i)),
                   _tile_spec(HEADS, t)],
        out_shape=[jax.ShapeDtypeStruct((s_len, FOX_W), BF16),
                   jax.ShapeDtypeStruct((FOX_W, s_len), BF16),
                   jax.ShapeDtypeStruct((nb, HEADS, t), F32)],
        scratch_shapes=[pltpu.VMEM((HEADS, t), F32), pltpu.VMEM((HEADS, t), F32),
                        pltpu.VMEM((FOX_W, t), F32), pltpu.VMEM((2, HEADS, t, t), F32),
                        pltpu.VMEM((2, HEADS, t), F32), pltpu.VMEM((2, HEADS, t), F32)],
        compiler_params=_params(48, 1),
    )(qa, ka, vt)


def _sgu_mix(vn, ws_ref):
    tm = vn.shape[0]
    lane = lax.broadcasted_iota(jnp.int32, (WINDOW, 128), 1)
    low = lane < HEAD_DIM
    wins = []
    for w in range(tm // WINDOW):
        slabs = []
        for p in range(GROUPS // 2):
            v2 = vn[w * WINDOW:(w + 1) * WINDOW, p * 128:(p + 1) * 128]
            lo = jnp.where(low, v2, 0.0).astype(BF16)
            hi = jnp.where(low, 0.0, v2).astype(BF16)
            slabs.append(jnp.dot(ws_ref[2 * p], lo, preferred_element_type=F32)
                         + jnp.dot(ws_ref[2 * p + 1], hi, preferred_element_type=F32))
        wins.append(jnp.concatenate(slabs, axis=1))
    return jnp.concatenate(wins, axis=0) if len(wins) > 1 else wins[0]


def _layernorm_fwd(vv, g, b):
    mu = jnp.mean(vv, axis=-1, keepdims=True)
    xc = vv - mu
    r = lax.rsqrt(jnp.mean(xc * xc, axis=-1, keepdims=True) + EPS)
    xh = xc * r
    return xh * g + b, xh, r


def _mix_fwd(attn, uvpre, gpre, x, wa, wb, wout, wsm, bsf, gsgu, bsgu, gpost):
    s_len = x.shape[0]
    tm = TOKEN_TILE

    def body(o_ref, uv_ref, gp_ref, x_ref, wa_ref, wb_ref, wo_ref, ws_ref, bs_ref, gs_ref, bsg_ref,
             gpost_ref, sgut_ref, ya_ref, yb_ref, mgt_ref, om_ref, x1_ref):
        uvp = uv_ref[...].astype(F32)
        uv, _ = _gelu_and_grad(uvp)
        u, vv = uv[:, :SGU_W], uv[:, SGU_W:]
        vn, _, _ = _layernorm_fwd(vv, gs_ref[...], bsg_ref[...])
        bias = bs_ref[...]
        if tm > WINDOW:
            bias = jnp.concatenate([bias] * (tm // WINDOW), axis=0)
        mixed = _sgu_mix(vn, ws_ref) + bias
        sgu_f = u * mixed
        sgu = sgu_f.astype(BF16)
        sgut_ref[...] = sgu_f.T.astype(BF16)
        ya = jnp.dot(o_ref[...], wa_ref[...], preferred_element_type=F32)
        yb = jnp.dot(sgu, wb_ref[...], preferred_element_type=F32)
        ya_ref[...] = ya.astype(BF16)
        yb_ref[...] = yb.astype(BF16)
        gates = _sigmoid(gp_ref[...].astype(F32))
        merged_f = gates[:, :D_MODEL] * ya + gates[:, D_MODEL:] * yb
        merged = merged_f.astype(BF16)
        mgt_ref[...] = merged_f.T.astype(BF16)
        om = jnp.dot(merged, wo_ref[...], preferred_element_type=F32)
        om_ref[...] = om
        r = lax.rsqrt(jnp.mean(om * om, axis=-1, keepdims=True) + EPS)
        x1_ref[...] = x_ref[...] + om * r * gpost_ref[...]

    def t_out(rows):
        return ((rows, s_len), BF16, pl.BlockSpec((rows, tm), lambda i: (0, i)))

    def r_out(cols, dt):
        return ((s_len, cols), dt, _row_spec(tm, cols))

    outs = [t_out(SGU_W), r_out(D_MODEL, BF16), r_out(D_MODEL, BF16), t_out(D_MODEL),
            r_out(D_MODEL, F32), r_out(D_MODEL, F32)]
    return pl.pallas_call(
        body, name="mix_fwd", grid=(s_len // tm,),
        in_specs=[_row_spec(tm, FOX_W), _row_spec(tm, 2 * SGU_W), _row_spec(tm, 2 * D_MODEL),
                  _row_spec(tm, D_MODEL), _const_spec(wa.shape), _const_spec(wb.shape),
                  _const_spec(wout.shape), _const_spec(wsm.shape), _const_spec(bsf.shape),
                  _const_spec((1, SGU_W)), _const_spec((1, SGU_W)), _const_spec((1, D_MODEL))],
        out_specs=[o[2] for o in outs],
        out_shape=[jax.ShapeDtypeStruct(o[0], o[1]) for o in outs],
        compiler_params=_params(48, 1),
    )(attn, uvpre, gpre, x, wa, wb, wout, wsm, bsf, gsgu, bsgu, gpost)


def _ffn_fwd_bwd(x1, tgt, wffn, wdown, gpre, gpost):
    s_len = x1.shape[0]
    tm = TOKEN_TILE

    def body(x1_ref, t_ref, wi_ref, wd_ref, gpre_ref, gpost_ref,
             dx1_ref, h2_ref, actt_ref, dff_ref, dgut_ref, loss_ref, dgpost_ref, dgpre_ref):
        @pl.when(pl.program_id(0) == 0)
        def _():
            loss_ref[...] = jnp.zeros_like(loss_ref)
            dgpost_ref[...] = jnp.zeros_like(dgpost_ref)
            dgpre_ref[...] = jnp.zeros_like(dgpre_ref)

        x1v = x1_ref[...]
        r2 = lax.rsqrt(jnp.mean(x1v * x1v, axis=-1, keepdims=True) + EPS)
        gpre_v = gpre_ref[...]
        h2 = (x1v * r2 * gpre_v).astype(BF16)
        h2_ref[...] = h2
        gg = _dot_nt(h2, wi_ref[:D_FF, :])
        uu = _dot_nt(h2, wi_ref[D_FF:, :])
        sg = _sigmoid(gg)
        silu = gg * sg
        act_f = silu * uu
        act = act_f.astype(BF16)
        actt_ref[...] = act_f.T.astype(BF16)
        ff = jnp.dot(act, wd_ref[...], preferred_element_type=F32)
        r3 = lax.rsqrt(jnp.mean(ff * ff, axis=-1, keepdims=True) + EPS)
        gpost_v = gpost_ref[...]
        y = x1v + ff * r3 * gpost_v
        err = y - t_ref[...]
        loss_ref[...] += jnp.sum(err * err) * (0.5 / D_MODEL)
        dy = err * (1.0 / D_MODEL)
        dgpost_ref[...] += jnp.sum(dy * ff * r3, axis=0, keepdims=True)
        dff = _rms_bwd(ff, r3, gpost_v, dy).astype(BF16)
        dff_ref[...] = dff
        dact = _dot_nt(dff, wd_ref[...])
        dgg_f = dact * uu * (sg * (1.0 + gg * (1.0 - sg)))
        duu_f = dact * silu
        dgg = dgg_f.astype(BF16)
        duu = duu_f.astype(BF16)
        dgut_ref[:D_FF, :] = dgg_f.T.astype(BF16)
        dgut_ref[D_FF:, :] = duu_f.T.astype(BF16)
        dh2 = (jnp.dot(dgg, wi_ref[:D_FF, :], preferred_element_type=F32)
               + jnp.dot(duu, wi_ref[D_FF:, :], preferred_element_type=F32))
        dgpre_ref[...] += jnp.sum(dh2 * x1v * r2, axis=0, keepdims=True)
        dx1_ref[...] = dy + _rms_bwd(x1v, r2, gpre_v, dh2)

    outs = [((s_len, D_MODEL), F32, _row_spec(tm, D_MODEL)),
            ((s_len, D_MODEL), BF16, _row_spec(tm, D_MODEL)),
            ((D_FF, s_len), BF16, pl.BlockSpec((D_FF, tm), lambda i: (0, i))),
            ((s_len, D_MODEL), BF16, _row_spec(tm, D_MODEL)),
            ((2 * D_FF, s_len), BF16, pl.BlockSpec((2 * D_FF, tm), lambda i: (0, i))),
            ((1, 128), F32, _const_spec((1, 128))),
            ((1, D_MODEL), F32, _const_spec((1, D_MODEL))),
            ((1, D_MODEL), F32, _const_spec((1, D_MODEL)))]
    return pl.pallas_call(
        body, name="ffn_fwd_bwd", grid=(s_len // tm,),
        in_specs=[_row_spec(tm, D_MODEL), _row_spec(tm, D_MODEL), _const_spec(wffn.shape),
                  _const_spec(wdown.shape), _const_spec((1, D_MODEL)), _const_spec((1, D_MODEL))],
        out_specs=[o[2] for o in outs],
        out_shape=[jax.ShapeDtypeStruct(o[0], o[1]) for o in outs],
        compiler_params=_params(60, 1),
    )(x1, tgt, wffn, wdown, gpre, gpost)


def _mix_bwd(dx1, om, ya, yb, gpre, uvpre, attn, attn_t, sgu_t, merged_t, wout, wa, wb, wsm, wsmt, bsf, gsgu,
             bsgu, gpost, wmask, egrp):
    s_len = dx1.shape[0]
    tm = TOKEN_TILE
    nw = tm // WINDOW
    nt = s_len // tm

    def body(dx1_ref, om_ref, ya_ref, yb_ref, gp_ref, uv_ref, o_ref, at_ref, st_ref, mt_ref, wo_ref, wa_ref,
             wb_ref, ws_ref, wst_ref, bs_ref, gs_ref, bsg_ref, gpost_ref, mask_ref, eg_ref,
             dgp_ref, dot_ref, delta_ref, duv_ref,
             dws_ref, dbs_ref, dgs_ref, dbsg_ref, dgpost_ref, dwa_ref, dwb_ref, dwo_ref,
             dbs_acc, dwa_acc, dwb_acc, dwo_acc):
        step = pl.program_id(0)

        @pl.when(step == 0)
        def _():
            dws_ref[...] = jnp.zeros_like(dws_ref)
            dwa_acc[...] = jnp.zeros_like(dwa_acc)
            dwb_acc[...] = jnp.zeros_like(dwb_acc)
            dwo_acc[...] = jnp.zeros_like(dwo_acc)
            dbs_acc[...] = jnp.zeros_like(dbs_acc)
            dgs_ref[...] = jnp.zeros_like(dgs_ref)
            dbsg_ref[...] = jnp.zeros_like(dbsg_ref)
            dgpost_ref[...] = jnp.zeros_like(dgpost_ref)

        om = om_ref[...]
        dx1v = dx1_ref[...]
        r = lax.rsqrt(jnp.mean(om * om, axis=-1, keepdims=True) + EPS)
        gpost_v = gpost_ref[...]
        dgpost_ref[...] += jnp.sum(dx1v * om * r, axis=0, keepdims=True)
        dom = _rms_bwd(om, r, gpost_v, dx1v).astype(BF16)
        dwo_acc[...] += jnp.dot(mt_ref[...], dom, preferred_element_type=F32)
        dmg = _dot_nt(dom, wo_ref[...])

        gates = _sigmoid(gp_ref[...].astype(F32))
        ga, gb = gates[:, :D_MODEL], gates[:, D_MODEL:]
        yav, ybv = ya_ref[...].astype(F32), yb_ref[...].astype(F32)
        dya = (dmg * ga).astype(BF16)
        dyb = (dmg * gb).astype(BF16)
        dwa_acc[...] += jnp.dot(at_ref[...], dya, preferred_element_type=F32)
        dwb_acc[...] += jnp.dot(st_ref[...], dyb, preferred_element_type=F32)
        dgp_ref[:, :D_MODEL] = (dmg * yav * ga * (1.0 - ga)).astype(BF16)
        dgp_ref[:, D_MODEL:] = (dmg * ybv * gb * (1.0 - gb)).astype(BF16)

        dat_t = _dot_nt(dya, wa_ref[...]).T.astype(BF16)
        dot_ref[0] = dat_t
        o_t = o_ref[...].astype(F32).T
        delta_ref[0] = jnp.sum((dat_t.astype(F32) * o_t).reshape(HEADS, HEAD_DIM, tm), axis=1)
        dsgu = _dot_nt(dyb, wb_ref[...])

        uvp = uv_ref[...].astype(F32)
        uv, guv = _gelu_and_grad(uvp)
        u, vv = uv[:, :SGU_W], uv[:, SGU_W:]
        gs_v = gs_ref[...]
        vn, xh, rln = _layernorm_fwd(vv, gs_v, bsg_ref[...])
        bias = bs_ref[...]
        if nw > 1:
            bias = jnp.concatenate([bias] * nw, axis=0)
        mixed = _sgu_mix(vn, ws_ref) + bias
        du = dsgu * mixed
        dmixed = dsgu * u

        lane = lax.broadcasted_iota(jnp.int32, (WINDOW, 128), 1)
        low = lane < HEAD_DIM
        dvn_wins = []
        for w in range(nw):
            rows = slice(w * WINDOW, (w + 1) * WINDOW)
            dbs_acc[...] += dmixed[rows, :]
            slabs = []
            for p in range(GROUPS // 2):
                cols = slice(p * 128, (p + 1) * 128)
                dm2 = dmixed[rows, cols]
                dlo = jnp.where(low, dm2, 0.0).astype(BF16)
                dhi = jnp.where(low, 0.0, dm2).astype(BF16)
                vn2 = vn[rows, cols].astype(BF16)
                dws_ref[2 * p] += _dot_nt(dlo, vn2)
                dws_ref[2 * p + 1] += _dot_nt(dhi, vn2)
                slabs.append(jnp.dot(wst_ref[2 * p], dlo, preferred_element_type=F32)
                             + jnp.dot(wst_ref[2 * p + 1], dhi, preferred_element_type=F32))
            dvn_wins.append(jnp.concatenate(slabs, axis=1))
        dvn = jnp.concatenate(dvn_wins, axis=0) if nw > 1 else dvn_wins[0]

        dgs_ref[...] += jnp.sum(dvn * xh, axis=0, keepdims=True)
        dbsg_ref[...] += jnp.sum(dvn, axis=0, keepdims=True)
        dxh = dvn * gs_v
        dvv = rln * (dxh - jnp.mean(dxh, axis=-1, keepdims=True)
                     - xh * jnp.mean(dxh * xh, axis=-1, keepdims=True))
        duv_ref[:, :SGU_W] = (du * guv[:, :SGU_W]).astype(BF16)
        duv_ref[:, SGU_W:] = (dvv * guv[:, SGU_W:]).astype(BF16)

        @pl.when(step == pl.num_programs(0) - 1)
        def _():
            for g in range(GROUPS):
                dws_ref[g] = dws_ref[g] * mask_ref[...]
            dbs_ref[...] = _split3_dot(dbs_acc[...], eg_ref[...])
            dwa_ref[...] = dwa_acc[...].astype(BF16)
            dwb_ref[...] = dwb_acc[...].astype(BF16)
            dwo_ref[...] = dwo_acc[...].astype(BF16)

    rows_out = [((s_len, 2 * D_MODEL), BF16, _row_spec(tm, 2 * D_MODEL)),
                ((nt, FOX_W, tm), BF16, _tile_spec(FOX_W, tm)),
                ((nt, HEADS, tm), F32, _tile_spec(HEADS, tm)),
                ((s_len, 2 * SGU_W), BF16, _row_spec(tm, 2 * SGU_W))]
    acc_out = [((GROUPS, WINDOW, WINDOW), F32), ((WINDOW, 128), F32), ((1, SGU_W), F32),
               ((1, SGU_W), F32), ((1, D_MODEL), F32), ((FOX_W, D_MODEL), BF16),
               ((SGU_W, D_MODEL), BF16), ((D_MODEL, D_MODEL), BF16)]
    return pl.pallas_call(
        body, name="mix_bwd", grid=(nt,),
        in_specs=[_row_spec(tm, D_MODEL), _row_spec(tm, D_MODEL), _row_spec(tm, D_MODEL),
                  _row_spec(tm, D_MODEL), _row_spec(tm, 2 * D_MODEL), _row_spec(tm, 2 * SGU_W),
                  _row_spec(tm, FOX_W), pl.BlockSpec((FOX_W, tm), lambda i: (0, i)),
                  pl.BlockSpec((SGU_W, tm), lambda i: (0, i)),
                  pl.BlockSpec((D_MODEL, tm), lambda i: (0, i)), _const_spec(wout.shape),
                  _const_spec(wa.shape),
                  _const_spec(wb.shape), _const_spec(wsm.shape), _const_spec(wsmt.shape),
                  _const_spec(bsf.shape), _const_spec((1, SGU_W)), _const_spec((1, SGU_W)),
                  _const_spec((1, D_MODEL)), _const_spec(wmask.shape), _const_spec(egrp.shape)],
        out_specs=[o[2] for o in rows_out] + [_const_spec(s) for s, _ in acc_out],
        out_shape=[jax.ShapeDtypeStruct(o[0], o[1]) for o in rows_out]
        + [jax.ShapeDtypeStruct(s, dt) for s, dt in acc_out],
        scratch_shapes=[pltpu.VMEM((WINDOW, SGU_W), F32), pltpu.VMEM((FOX_W, D_MODEL), F32),
                        pltpu.VMEM((SGU_W, D_MODEL), F32), pltpu.VMEM((D_MODEL, D_MODEL), F32)],
        compiler_params=_params(60, 1),
    )(dx1, om, ya, yb, gpre, uvpre, attn, attn_t, sgu_t, merged_t, wout, wa, wb, wsm, wsmt, bsf, gsgu, bsgu,
      gpost, wmask, egrp)


def _attn_bwd(qa, ka, kat, vs, dot_, lse, delta, ecol):
    s_len = qa.shape[0]
    t = ATTN_TILE
    nb = s_len // t

    def body(k_ref, kt_ref, vs_ref, q_ref, do_ref, lse_ref, dl_ref, ec_ref, gk_ref, dvt_ref,
             gqt_ref, csum_ref, p_sc, ds_sc):
        j = pl.program_id(0)

        @pl.when(j == 0)
        def _():
            gqt_ref[...] = jnp.zeros_like(gqt_ref)

        gk_ref[...] = jnp.zeros_like(gk_ref)
        dvt_ref[...] = jnp.zeros_like(dvt_ref)

        def probs(i, slot, masked):
            qrows = pl.ds(pl.multiple_of(i * t, t), t)
            if masked:
                keep = (lax.broadcasted_iota(jnp.int32, (t, t), 0)
                        <= lax.broadcasted_iota(jnp.int32, (t, t), 1))
            for hd in range(HEADS):
                sl = slice(hd * 128, (hd + 1) * 128)
                hr = slice(hd * HEAD_DIM, (hd + 1) * HEAD_DIM)
                st = _dot_nt(k_ref[:, sl], q_ref[qrows, sl])
                if masked:
                    st = jnp.where(keep, st, -jnp.inf)
                pt = jnp.exp2(st - lse_ref[i, hd:hd + 1, :])
                dpt = jnp.dot(vs_ref[:, hd * 128:hd * 128 + HEAD_DIM], do_ref[i, hr, :],
                              preferred_element_type=F32)
                p_sc[slot, hd] = pt.astype(BF16)
                ds_sc[slot, hd] = (pt * (dpt - dl_ref[i, hd:hd + 1, :])).astype(BF16)

        def grads(i, slot):
            qrows = pl.ds(pl.multiple_of(i * t, t), t)
            for hd in range(HEADS):
                sl = slice(hd * 128, (hd + 1) * 128)
                hr = slice(hd * HEAD_DIM, (hd + 1) * HEAD_DIM)
                dst = ds_sc[slot, hd]
                dvt_ref[0, hr, :] += _dot_nt(do_ref[i, hr, :], p_sc[slot, hd])
                gk_ref[:, sl] += jnp.dot(dst, q_ref[qrows, sl], preferred_element_type=F32)
                gqt_ref[i, hd * QT_ROWS:(hd + 1) * QT_ROWS, :] += jnp.dot(
                    kt_ref[0, hd * 128:hd * 128 + QT_ROWS, :], dst, preferred_element_type=F32)

        probs(j, 0, True)
        pairs = (nb - 1 - j) // 2

        def two_blocks(p, carry):
            i1 = j + 1 + 2 * p
            probs(i1, 1, False)
            grads(i1 - 1, 0)
            probs(i1 + 1, 0, False)
            grads(i1, 1)
            return carry

        lax.fori_loop(0, pairs, two_blocks, 0)

        @pl.when(nb - 1 - j - 2 * pairs == 0)
        def _():
            grads(nb - 1, 0)

        @pl.when(nb - 1 - j - 2 * pairs == 1)
        def _():
            probs(nb - 1, 1, False)
            grads(nb - 2, 0)
            grads(nb - 1, 1)

        csum_ref[...] = _split3_dot(gk_ref[...], ec_ref[...])

    return pl.pallas_call(
        body, name="attn_bwd", grid=(nb,),
        in_specs=[_row_spec(t, SLAB_W), _tile_spec(SLAB_W, t), _row_spec(t, SLAB_W),
                  _const_spec(qa.shape), _const_spec(dot_.shape), _const_spec(lse.shape),
                  _const_spec(delta.shape), _const_spec(ecol.shape)],
        out_specs=[_row_spec(t, SLAB_W), _tile_spec(FOX_W, t),
                   _const_spec((nb, HEADS * QT_ROWS, t)), _row_spec(t, 128)],
        out_shape=[jax.ShapeDtypeStruct((s_len, SLAB_W), F32),
                   jax.ShapeDtypeStruct((nb, FOX_W, t), F32),
                   jax.ShapeDtypeStruct((nb, HEADS * QT_ROWS, t), F32),
                   jax.ShapeDtypeStruct((s_len, 128), F32)],
        scratch_shapes=[pltpu.VMEM((2, HEADS, t, t), BF16), pltpu.VMEM((2, HEADS, t, t), BF16)],
        compiler_params=_params(60, 1),
    )(ka, kat, vs, qa, dot_, lse, delta, ecol)


def _rev_cumsum(col_sums, gqt, triu):
    s_len = col_sums.shape[0]
    tm = TOKEN_TILE
    n = s_len // tm

    def body(cs_ref, gqt_ref, tri_ref, o_ref, carry):
        @pl.when(pl.program_id(0) == 0)
        def _():
            carry[...] = jnp.zeros_like(carry)
        rows = [gqt_ref[0, hd * QT_ROWS + HEAD_DIM:hd * QT_ROWS + HEAD_DIM + 1, :]
                for hd in range(HEADS)]
        row_sums = jnp.concatenate(rows + [jnp.zeros((128 - HEADS, tm), F32)], axis=0).T
        out = _tri_dot(tri_ref[...], row_sums - cs_ref[...]) + carry[...]
        o_ref[...] = out
        carry[...] = out[0:1, :]

    return pl.pallas_call(
        body, name="rev_cumsum", grid=(n,),
        in_specs=[pl.BlockSpec((tm, 128), lambda i: (n - 1 - i, 0)),
                  pl.BlockSpec((1, HEADS * QT_ROWS, tm), lambda i: (n - 1 - i, 0, 0)),
                  _const_spec((tm, tm))],
        out_specs=pl.BlockSpec((tm, 128), lambda i: (n - 1 - i, 0)),
        out_shape=jax.ShapeDtypeStruct((s_len, 128), F32),
        scratch_shapes=[pltpu.VMEM((1, 128), F32)],
        compiler_params=_params(32, 1),
    )(col_sums, gqt, triu)


def _heads_from_slabs(slabs):
    lane = lax.broadcasted_iota(jnp.int32, slabs[0].shape, 1)
    low = lane < HEAD_DIM
    pairs = [jnp.where(low, slabs[2 * p], pltpu.roll(slabs[2 * p + 1], HEAD_DIM, 1))
             for p in range(HEADS // 2)]
    return jnp.concatenate(pairs, axis=1)


def _proj_bwd(gqt, gk, dvt, dlogf, flog, qraw, kraw, duv, dgp, x, dx1, wcat, bdiag, gq, gk_gain, g1,
              efold):
    s_len = x.shape[0]
    tm = TOKEN_TILE

    def body(gqt_ref, gkk_ref, dvt_ref, dlf_ref, flog_ref, qr_ref, kr_ref, duv_ref, dgp_ref, x_ref,
             dx1_ref, w_ref, bd_ref, gq_ref, gk_ref, g1_ref, ef_ref,
             dx_ref, dprojt_ref, dgq_ref, dgk_ref, dbf_ref, dg1_ref, gq_acc, gk_acc):
        step = pl.program_id(0)

        @pl.when(step == 0)
        def _():
            gq_acc[...] = jnp.zeros_like(gq_acc)
            gk_acc[...] = jnp.zeros_like(gk_acc)
            dbf_ref[...] = jnp.zeros_like(dbf_ref)
            dg1_ref[...] = jnp.zeros_like(dg1_ref)

        pad = jnp.zeros((128 - QT_ROWS, tm), F32)
        q_slabs = [jnp.concatenate([gqt_ref[0, hd * QT_ROWS:(hd + 1) * QT_ROWS, :], pad], axis=0).T
                   for hd in range(HEADS)]
        dqn = _heads_from_slabs(q_slabs)
        dkn = _heads_from_slabs([gkk_ref[:, hd * 128:(hd + 1) * 128] for hd in range(HEADS)])

        def head_bwd(raw_ref, dn, g_ref, acc):
            raw = raw_ref[...].astype(F32)
            r = lax.rsqrt(_seg_mean(raw * raw, bd_ref) + EPS)
            xhat = raw * r
            acc[0:1, :] += jnp.sum(dn * xhat, axis=0, keepdims=True)
            dyg = dn * g_ref[...]
            return r * (dyg - xhat * _seg_mean(dyg * xhat, bd_ref))

        dot = functools.partial(jnp.dot, preferred_element_type=F32)
        duv, dgp = duv_ref[...], dgp_ref[...]
        dprojt_ref[C_UV:C_G, :] = duv.astype(F32).T.astype(BF16)
        dprojt_ref[C_G:C_F, :] = dgp.astype(F32).T.astype(BF16)
        dh = dot(duv, w_ref[C_UV:C_G, :]) + dot(dgp, w_ref[C_G:C_F, :])

        dq = head_bwd(qr_ref, dqn * HEAD_DIM ** -0.5, gq_ref, gq_acc)
        dk = head_bwd(kr_ref, dkn * LN2, gk_ref, gk_acc)
        dv_t = dvt_ref[0]
        dfl = dlf_ref[...] * _sigmoid(-flog_ref[...])
        dbf_ref[...] += jnp.sum(dfl, axis=0, keepdims=True)
        dprojt_ref[C_Q:C_K, :] = dq.T.astype(BF16)
        dprojt_ref[C_K:C_V, :] = dk.T.astype(BF16)
        dprojt_ref[C_V:C_UV, :] = dv_t.astype(BF16)
        dprojt_ref[C_F:C_END, :] = dfl.T.astype(BF16)
        dh = (dh + dot(dq.astype(BF16), w_ref[C_Q:C_K, :]) + dot(dk.astype(BF16), w_ref[C_K:C_V, :])
              + dot(dv_t.T.astype(BF16), w_ref[C_V:C_UV, :])
              + dot(dfl.astype(BF16), w_ref[C_F:C_END, :]))
        xf = x_ref[...]
        r = lax.rsqrt(jnp.mean(xf * xf, axis=-1, keepdims=True) + EPS)
        dg1_ref[...] += jnp.sum(dh * xf * r, axis=0, keepdims=True)
        dx_ref[...] = dx1_ref[...] + _rms_bwd(xf, r, g1_ref[...], dh)

        @pl.when(step == pl.num_programs(0) - 1)
        def _():
            dgq_ref[...] = _split3_dot(gq_acc[...], ef_ref[...])
            dgk_ref[...] = _split3_dot(gk_acc[...], ef_ref[...])

    outs = [((s_len, D_MODEL), F32, _row_spec(tm, D_MODEL)),
            ((C_END, s_len), BF16, pl.BlockSpec((C_END, tm), lambda i: (0, i))),
            ((8, 128), F32, _const_spec((8, 128))),
            ((8, 128), F32, _const_spec((8, 128))),
            ((1, 128), F32, _const_spec((1, 128))),
            ((1, D_MODEL), F32, _const_spec((1, D_MODEL)))]
    return pl.pallas_call(
        body, name="proj_bwd", grid=(s_len // tm,),
        in_specs=[_tile_spec(HEADS * QT_ROWS, tm), _row_spec(tm, SLAB_W), _tile_spec(FOX_W, tm),
                  _row_spec(tm, 128), _row_spec(tm, 128), _row_spec(tm, FOX_W),
                  _row_spec(tm, FOX_W), _row_spec(tm, 2 * SGU_W), _row_spec(tm, 2 * D_MODEL),
                  _row_spec(tm, D_MODEL), _row_spec(tm, D_MODEL), _const_spec(wcat.shape),
                  _const_spec(bdiag.shape), _const_spec((1, FOX_W)), _const_spec((1, FOX_W)),
                  _const_spec((1, D_MODEL)), _const_spec(efold.shape)],
        out_specs=[o[2] for o in outs],
        out_shape=[jax.ShapeDtypeStruct(o[0], o[1]) for o in outs],
        scratch_shapes=[pltpu.VMEM((8, FOX_W), F32), pltpu.VMEM((8, FOX_W), F32)],
        compiler_params=_params(56, 1),
    )(gqt, gk, dvt, dlogf, flog, qraw, kraw, duv, dgp, x, dx1, wcat, bdiag, gq, gk_gain, g1, efold)


def _dw_matmul(at, b, tm, name, after=()):
    m, s_len = at.shape
    n = b.shape[1]

    def body(a_ref, b_ref, *rest):
        rest[-1][...] = jnp.dot(a_ref[...], b_ref[...], preferred_element_type=F32).astype(BF16)

    return pl.pallas_call(
        body, name=name, grid=(m // tm,),
        in_specs=[pl.BlockSpec((tm, s_len), lambda i: (i, 0)), _const_spec(b.shape)]
        + [pl.BlockSpec(memory_space=pl.ANY)] * len(after),
        out_specs=pl.BlockSpec((tm, n), lambda i: (i, 0)),
        out_shape=jax.ShapeDtypeStruct((m, n), BF16),
        compiler_params=_params(48, 1),
    )(at, b, *after)


def _adamw(parts, w, m, v, tr, name, col_tile=None, select=None):
    parts = parts if isinstance(parts, (list, tuple)) else [parts]
    rows, cols = w.shape
    extra = [] if select is None else [select]
    bc1 = 1.0 - ADAM_B1 ** ADAM_STEP
    bc2 = 1.0 - ADAM_B2 ** ADAM_STEP

    def body(*refs):
        p_refs = refs[:len(parts)]
        sel_refs = refs[len(parts):len(parts) + len(extra)]
        w_ref, m_ref, v_ref, g_ref, d_ref, mo_ref, vo_ref = refs[len(parts) + len(extra):]
        g = None
        for p_ref, p in zip(p_refs, parts):
            for idx in range(p.shape[0]):
                term = p_ref[idx].astype(F32)
                g = term if g is None else g + term
        if sel_refs:
            g = _tri_dot(sel_refs[0][...], g)
        g_ref[...] = g
        mn = ADAM_B1 * m_ref[...] + (1.0 - ADAM_B1) * g
        vn = ADAM_B2 * v_ref[...] + (1.0 - ADAM_B2) * (g * g)
        mo_ref[...] = mn
        vo_ref[...] = vn
        m_hat = mn / bc1
        v_hat = vn / bc2
        d_ref[...] = -ADAM_LR * (m_hat / (jnp.sqrt(v_hat) + ADAM_EPS) + ADAM_WD * w_ref[...])

    if col_tile is None:
        spec = pl.BlockSpec((tr, cols), lambda i: (i, 0))
        pspecs = [pl.BlockSpec((p.shape[0], tr, cols), lambda i: (0, i, 0)) for p in parts]
        steps = rows // tr
    else:
        spec = pl.BlockSpec((rows, col_tile), lambda i: (0, i))
        pspecs = [pl.BlockSpec((p.shape[0], p.shape[1], col_tile), lambda i: (0, 0, i))
                  for p in parts]
        steps = cols // col_tile
    return pl.pallas_call(
        body, name=name, grid=(steps,),
        in_specs=pspecs + [_const_spec(e.shape) for e in extra] + [spec, spec, spec],
        out_specs=[spec] * 4,
        out_shape=[jax.ShapeDtypeStruct((rows, cols), F32)] * 4,
        compiler_params=_params(48, 1),
    )(*parts, *extra, w, m, v)


def _sum_parts(parts, name):
    n, rows, cols = parts.shape

    def body(p_ref, o_ref):
        g = p_ref[0]
        for idx in range(1, n):
            g = g + p_ref[idx]
        o_ref[...] = g

    return pl.pallas_call(
        body, name=name, out_shape=jax.ShapeDtypeStruct((rows, cols), F32),
        in_specs=[_const_spec(parts.shape)], out_specs=_const_spec((rows, cols)), grid=(1,),
        compiler_params=_params(16, 1),
    )(parts)


VEC_NAMES = ("g_pre_mix", "b_forget", "g_q", "g_k", "g_sgu", "b_sgu", "b_spatial", "g_post_mix",
             "g_pre_ffn", "g_post_ffn")
VEC_ROWS = 16
LOSS_ROW = len(VEC_NAMES)


def _pack_vectors(d, loss_row):
    rows = []
    for k in VEC_NAMES:
        flat = d[k].reshape(1, -1).astype(F32)
        rows.append(jnp.pad(flat, ((0, 0), (0, 1024 - flat.shape[1]))))
    rows.append(loss_row)
    rows.append(jnp.zeros((VEC_ROWS - len(rows), 1024), F32))
    return jnp.concatenate(rows, axis=0)


def _adamw_vectors(grad_rows, ws, ms, vs):
    n = len(VEC_NAMES)
    bc1 = 1.0 - ADAM_B1 ** ADAM_STEP
    bc2 = 1.0 - ADAM_B2 ** ADAM_STEP

    def step(g, w, m, v):
        mn = ADAM_B1 * m + (1.0 - ADAM_B1) * g
        vn = ADAM_B2 * v + (1.0 - ADAM_B2) * (g * g)
        delta = -ADAM_LR * ((mn / bc1) / (jnp.sqrt(vn / bc2) + ADAM_EPS) + ADAM_WD * w)
        return g, delta, mn, vn

    def body(*refs):
        g_ref = refs[0]
        ins = [refs[1 + j * n:1 + (j + 1) * n] for j in range(3)]
        outs = [refs[1 + (3 + j) * n:1 + (4 + j) * n] for j in range(4)]
        for i in range(n):
            shape = ws[i].shape
            if len(shape) == 2:
                res = step(g_ref[i:i + 1, :shape[1]], *[r[i][...] for r in ins])
                for o, val in zip(outs, res):
                    o[i][...] = val
            else:
                for r in range(shape[1]):
                    res = step(g_ref[i:i + 1, r * shape[2]:(r + 1) * shape[2]],
                               *[q[i][0, r:r + 1, :] for q in ins])
                    for o, val in zip(outs, res):
                        o[i][0, r:r + 1, :] = val

    vmem = pl.BlockSpec(memory_space=pltpu.VMEM)
    flat = pl.pallas_call(
        body, name="adamw_vectors",
        in_specs=[vmem] * (1 + 3 * n), out_specs=[vmem] * (4 * n),
        out_shape=[jax.ShapeDtypeStruct(w.shape, F32) for _ in range(4) for w in ws],
    )(grad_rows, *ws, *ms, *vs)
    return [flat[j * n:(j + 1) * n] for j in range(4)]


def _cols_to_blocks(full, width):
    r = full.shape[0]
    return jnp.transpose(full.reshape(r, N_DEV, width), (1, 0, 2))


def _blocks_to_cols(blocks):
    n, r, width = blocks.shape
    return jnp.transpose(blocks, (1, 0, 2)).reshape(r, n * width)


def kernel(x, g_pre_mix, w_in, b_forget, g_q, g_k, g_sgu, b_sgu, w_spatial, b_spatial, w_branch_a, w_branch_b, w_out, g_post_mix, g_pre_ffn, w_ffn_in, w_ffn_down, g_post_ffn, loss_target, m_g_pre_mix, m_w_in, m_b_forget, m_g_q, m_g_k, m_g_sgu, m_b_sgu, m_w_spatial, m_b_spatial, m_w_branch_a, m_w_branch_b, m_w_out, m_g_post_mix, m_g_pre_ffn, m_w_ffn_in, m_w_ffn_down, m_g_post_ffn, v_g_pre_mix, v_w_in, v_b_forget, v_g_q, v_g_k, v_g_sgu, v_b_sgu, v_w_spatial, v_b_spatial, v_w_branch_a, v_w_branch_b, v_w_out, v_g_post_mix, v_g_pre_ffn, v_w_ffn_in, v_w_ffn_down, v_g_post_ffn):
    big_names = ("w_in", "w_branch_a", "w_branch_b", "w_out", "w_ffn_in", "w_ffn_down")
    weights = dict(g_pre_mix=g_pre_mix, w_in=w_in, b_forget=b_forget, g_q=g_q, g_k=g_k, g_sgu=g_sgu,
                   b_sgu=b_sgu, w_spatial=w_spatial, b_spatial=b_spatial, w_branch_a=w_branch_a,
                   w_branch_b=w_branch_b, w_out=w_out, g_post_mix=g_post_mix, g_pre_ffn=g_pre_ffn,
                   w_ffn_in=w_ffn_in, w_ffn_down=w_ffn_down, g_post_ffn=g_post_ffn)
    mom1 = dict(g_pre_mix=m_g_pre_mix, w_in=m_w_in, b_forget=m_b_forget, g_q=m_g_q, g_k=m_g_k,
                g_sgu=m_g_sgu, b_sgu=m_b_sgu, w_spatial=m_w_spatial, b_spatial=m_b_spatial,
                w_branch_a=m_w_branch_a, w_branch_b=m_w_branch_b, w_out=m_w_out,
                g_post_mix=m_g_post_mix, g_pre_ffn=m_g_pre_ffn, w_ffn_in=m_w_ffn_in,
                w_ffn_down=m_w_ffn_down, g_post_ffn=m_g_post_ffn)
    mom2 = dict(g_pre_mix=v_g_pre_mix, w_in=v_w_in, b_forget=v_b_forget, g_q=v_g_q, g_k=v_g_k,
                g_sgu=v_g_sgu, b_sgu=v_b_sgu, w_spatial=v_w_spatial, b_spatial=v_b_spatial,
                w_branch_a=v_w_branch_a, w_branch_b=v_w_branch_b, w_out=v_w_out,
                g_post_mix=v_g_post_mix, g_pre_ffn=v_g_pre_ffn, w_ffn_in=v_w_ffn_in,
                w_ffn_down=v_w_ffn_down, g_post_ffn=v_g_post_ffn)
    names = list(weights)
    shapes = {k: weights[k].shape for k in names}

    s_len = x.shape[1]
    xs = x.reshape(s_len, D_MODEL)
    tgt = loss_target.reshape(s_len, D_MODEL)

    transposed = ("w_in", "w_ffn_in")

    def local_view(a, k):
        return jnp.transpose(a[0]) if k in transposed else a[0]

    shards = {k: local_view(weights[k], k).astype(BF16) for k in big_names}

    x_pos, y_pos, c_pos = _mesh_pos()
    me = 4 * x_pos + 2 * y_pos + c_pos
    r_idx = jnp.arange(BLK)
    general = (jnp.asarray(BLK_AT, jnp.int32) - jnp.asarray(FRAME_START, jnp.int32))[me] + r_idx
    holder = jnp.where(r_idx < F_AT, BLK_AT[F_DEV] - FRAME_START[F_DEV] + r_idx,
                       jnp.where(r_idx < F_AT + HEADS, FRAME - F_AT + r_idx,
                                 BLK_AT[F_DEV] - FRAME_START[F_DEV] - HEADS + r_idx))
    frame_row = jnp.where(me == F_DEV, holder, general)
    in_frame = (frame_row[:, None] == jnp.arange(FRAME_ROWS)[None, :]).astype(BF16)
    my_frame = jnp.dot(in_frame.T, shards["w_in"], preferred_element_type=F32).astype(BF16)
    wcat = _gather_w_in(my_frame)
    wcat, later = lax.optimization_barrier(
        (wcat, [shards[k] for k in big_names if k != "w_in"]))
    shards.update(zip([k for k in big_names if k != "w_in"], later))
    (gat_mix, gat_ffn), gat_token = _exchange_start(
        [[shards["w_branch_a"], shards["w_branch_b"], shards["w_out"]],
         [shards["w_ffn_in"], shards["w_ffn_down"]]], "gather_start", gather=True)

    seg = np.arange(FOX_W) // HEAD_DIM
    bdiag = jnp.asarray(seg[:128, None] == seg[None, :128], BF16)
    tm = TOKEN_TILE
    lower = np.arange(tm)[None, :] <= np.arange(tm)[:, None]
    tril = jnp.asarray(lower, BF16)
    triu = jnp.asarray(lower.T, BF16)
    egrp = jnp.asarray(seg[:, None] == np.arange(128)[None, :], BF16)
    efold = jnp.asarray((np.arange(FOX_W) % HEAD_DIM)[:, None] == np.arange(128)[None, :], BF16)
    gq512 = jnp.tile(g_q.reshape(1, HEAD_DIM), (1, HEADS))
    gk512 = jnp.tile(g_k.reshape(1, HEAD_DIM), (1, HEADS))
    bfor = jnp.pad(b_forget.reshape(1, HEADS), ((0, 0), (0, 128 - HEADS)))
    pos = np.arange(WINDOW)
    wmask = (pos[None, :] // CHUNK) <= (pos[:, None] // CHUNK)
    wsm_f = jnp.where(jnp.asarray(wmask)[None], w_spatial[0], 0.0)
    wsm = wsm_f.astype(BF16)
    wsmt = jnp.transpose(wsm_f, (0, 2, 1)).astype(BF16)
    bsf = jnp.repeat(jnp.transpose(b_spatial[0]), HEAD_DIM, axis=1)
    wmask_f = jnp.asarray(wmask, F32)

    col = np.arange(SLAB_W)
    row128 = np.arange(128)

    def d_place(first, sign):
        place = sum(((col[None, :] // 128 == row128[:, None] - HEADS * a)
                     & (col[None, :] % 128 == first + a)).astype(np.float32) for a in range(3))
        return jnp.asarray(sign * place, BF16)

    pdq, pdk = d_place(HEAD_DIM, 1.0), d_place(HEAD_DIM + 3, -1.0)
    ones_q = jnp.asarray((col % 128 >= HEAD_DIM + 3) & (col % 128 < HEAD_DIM + 6), F32)[None]
    ones_k = jnp.asarray((col % 128 >= HEAD_DIM) & (col % 128 < HEAD_DIM + 3), F32)[None]
    ecol = jnp.asarray((col[:, None] // 128 == row128[None, :])
                       & (col[:, None] % 128 == HEAD_DIM + 3), BF16)

    (h, qa, ka, kat, vs, vt, qraw, kraw, flog, uvpre, gpre) = _proj_fwd(
        xs, g_pre_mix + gat_token[0:1, 0:1], wcat, bdiag, gq512, gk512, bfor, tril, pdq, pdk,
        ones_q, ones_k)
    attn, attn_t, lse = _attn_fwd(qa, ka, vt)
    (own_a, own_b, own_out), (zone_a, zone_b, zone_out) = _exchange_wait(
        gat_mix, attn, "gather_wait_mix", gather=True)
    wa = _blocks_to_cols(_own_block(zone_a, own_a))
    wb = _blocks_to_cols(_own_block(zone_b, own_b))
    wout = _own_block(zone_out, own_out).reshape(D_MODEL, D_MODEL)
    sgu_t, ya, yb, merged_t, om, x1 = _mix_fwd(attn, uvpre, gpre, xs, wa, wb, wout, wsm, bsf,
                                           g_sgu, b_sgu, g_post_mix)
    (own_ffn, own_down), (zone_ffn, zone_down) = _exchange_wait(
        gat_ffn, x1, "gather_wait_ffn", gather=True)
    wffn = _own_block(zone_ffn, own_ffn).reshape(2 * D_FF, D_MODEL)
    wdown = _own_block(zone_down, own_down).reshape(D_FF, D_MODEL)
    (dx1, h2, act_t, dff, dgu_t, loss_acc, dg_post_ffn, dg_pre_ffn) = _ffn_fwd_bwd(
        x1, tgt, wffn, wdown, g_pre_ffn, g_post_ffn)

    dw_down = _dw_matmul(act_t, dff, D_FF // 4, "dw_down")
    dw_ffn = _dw_matmul(dgu_t, h2, 2 * D_FF // N_DEV, "dw_ffn_in")
    def own_of(parts):
        return [lax.dynamic_index_in_dim(p, me, 0, keepdims=False) for p in parts]

    parts_ffn = [dw_ffn.reshape(N_DEV, 2 * D_FF // N_DEV, D_MODEL),
                 dw_down.reshape(N_DEV, D_FF // N_DEV, D_MODEL)]
    mine_ffn = own_of(parts_ffn)
    (sct_ffn,), sct_ffn_token = _exchange_start([parts_ffn], "scatter_start_ffn", gather=False)

    (dgp, dot_, delta, duv, dws, dbs, dg_sgu, db_sgu, dg_post_mix, dw_a, dw_b, dw_out) = _mix_bwd(
        dx1, om, ya, yb, gpre, uvpre, attn, attn_t, sgu_t, merged_t, wout, wa, wb, wsm, wsmt, bsf, g_sgu, b_sgu,
        g_post_mix + sct_ffn_token[0:1, 0:1], wmask_f, egrp)
    parts_mix = [_cols_to_blocks(dw_a, D_MODEL // N_DEV), _cols_to_blocks(dw_b, D_MODEL // N_DEV),
                 dw_out.reshape(N_DEV, D_MODEL // N_DEV, D_MODEL)]
    mine_mix = own_of(parts_mix)
    (sct_mix,), sct_mix_token = _exchange_start([parts_mix], "scatter_start_mix", gather=False)

    gk_all, dvt, gqt, col_sums = _attn_bwd(qa, ka, kat, vs, dot_, lse,
                                           delta + sct_mix_token[0, 0], ecol)
    dlogf = _rev_cumsum(col_sums, gqt, triu)
    dx, dproj_t, dgq, dgk, dbf, dg_pre_mix = _proj_bwd(
        gqt, gk_all, dvt, dlogf, flog, qraw, kraw, duv, dgp, xs, dx1, wcat, bdiag, gq512, gk512,
        g_pre_mix, efold)

    small_local = dict(
        g_pre_mix=dg_pre_mix, b_forget=dbf[:, :HEADS], g_q=dgq[0:1, :HEAD_DIM],
        g_k=dgk[0:1, :HEAD_DIM], g_sgu=dg_sgu, b_sgu=db_sgu, w_spatial=dws,
        b_spatial=jnp.transpose(dbs[:, :GROUPS]), g_post_mix=dg_post_mix, g_pre_ffn=dg_pre_ffn,
        g_post_ffn=dg_post_ffn)
    loss_row = jnp.pad(loss_acc[0:1, 0:1], ((0, 0), (0, 1023)))
    small_parts = [_pack_vectors(small_local, loss_row).reshape(N_DEV, VEC_ROWS // N_DEV, 1024),
                   dws]

    def with_own(zones, own_blocks):
        return [_own_block(z, b) for z, b in zip(zones, own_blocks)]

    mine_small = own_of(small_parts)
    (sct_small,), sct_small_token = _exchange_start([small_parts], "scatter_start_small",
                                                    gather=False)
    dw_cat = _dw_matmul(dproj_t, h, C_END // N_DEV, "dw_in", after=(sct_small_token,))
    recv_vec, recv_ws = with_own(
        _exchange_wait(sct_small, dw_cat, "scatter_wait_small", gather=False)[1], mine_small)
    small_sums = [_sum_parts(recv_vec, "sum_vectors"), _sum_parts(recv_ws, "sum_w_spatial")]
    (gat_small,), gat_small_token = _exchange_start([small_sums], "gather_start_small",
                                                    gather=True)
    pair_blocks, own_pair = _pair_sums(dw_cat, "pair_sums_in", gat_small_token)
    rs_in, rs_token = _chip_exchange_start(pair_blocks, "chip_exchange_start_in")

    recv_ffn, recv_down = with_own(
        _exchange_wait(sct_ffn, rs_token, "scatter_wait_ffn", gather=False)[1], mine_ffn)
    recv_a, recv_b, recv_out = with_own(
        _exchange_wait(sct_mix, recv_ffn, "scatter_wait_mix", gather=False)[1], mine_mix)
    received = [None, recv_a, recv_b, recv_out, recv_ffn, recv_down]

    grads, deltas, new_m, new_v = {}, {}, {}, {}
    row_tiles = {"w_in": None, "w_branch_a": 512, "w_branch_b": 512, "w_out": 128, "w_ffn_in": 176,
                 "w_ffn_down": 352}

    def update(k, parts):
        outs = _adamw(parts, local_view(weights[k], k), local_view(mom1[k], k),
                      local_view(mom2[k], k), row_tiles[k], "adamw_" + k,
                      col_tile=256 if k == "w_in" else None,
                      select=in_frame if k == "w_in" else None)
        if k in transposed:
            outs = [jnp.transpose(o) for o in outs]
        grads[k], deltas[k], new_m[k], new_v[k] = [o[None] for o in outs]
        return outs[0]

    last = None
    for idx, k in enumerate(big_names):
        if k != "w_in":
            last = update(k, received[idx])

    (own_vec, own_ws), (zone_vec, zone_ws) = _exchange_wait(gat_small, last, "gather_wait_small",
                                                            gather=True)
    vec_all = _own_block(zone_vec, own_vec).reshape(VEC_ROWS, 1024)
    ws_all = _own_block(zone_ws, own_ws).reshape(1, GROUPS * WINDOW, WINDOW)

    def rows_of(d):
        return d["w_spatial"].reshape(GROUPS * WINDOW, WINDOW)

    outs = _adamw(ws_all, rows_of(weights), rows_of(mom1), rows_of(mom2), GROUPS * WINDOW,
                  "adamw_w_spatial")
    for dst, o in zip((grads, deltas, new_m, new_v), outs):
        dst["w_spatial"] = o.reshape(shapes["w_spatial"])
    sg = outs[0]
    vec_outs = _adamw_vectors(vec_all, *[[d[k] for k in VEC_NAMES] for d in (weights, mom1, mom2)])
    for dst, group in zip((grads, deltas, new_m, new_v), vec_outs):
        dst.update(zip(VEC_NAMES, group))
    arrived = _chip_exchange_wait(rs_in, sg, "chip_exchange_wait_in")
    update("w_in", [own_pair[None], arrived])

    loss = vec_all[LOSS_ROW, 0]
    return (loss, dx.reshape(x.shape), *[grads[k] for k in names], *[deltas[k] for k in names],
            *[new_m[k] for k in names], *[new_v[k] for k in names])
```
